```python
import jax, jax.numpy as jnp
from jax import lax
import numpy as np

D_MODEL = 1024
BATCH = 8
SEQ = 2048
DEPTH = 1

GRID_W = 64
CTX_LEN = 256
D_RNN = D_MODEL
LRU_BLOCKS = 8
LRU_BLOCK_W = D_RNN // LRU_BLOCKS
LRU_CONV_W = 4
LRU_CONV_LEFT = 2
LRU_C = 8.0
NA_HEADS = 16
HEAD_DIM = 64
D_ATT = NA_HEADS * HEAD_DIM
NA_ROWS = 8
NA_COLS = 16
ROPE_BASE = 10000.0
D_FF = ((8 * D_MODEL // 3 + 127) // 128) * 128
FFN_CONV_W = 3
FFN_CONV_LEFT = 1
N_MOD = 6
EPS = 1e-6
NEG_INF = -1e30
IN_SPLITS = (D_RNN, D_RNN + D_ATT, D_RNN + 2 * D_ATT, 2 * D_RNN + 2 * D_ATT, 2 * D_RNN + 3 * D_ATT, 2 * D_RNN + 3 * D_ATT + D_MODEL)
IN_COLS = 2 * D_RNN + 3 * D_ATT + 2 * D_MODEL
CTX_KV_COLS = D_RNN + 2 * D_ATT

kernel_name = 'hybrid_rglru_natten_dit_block'


def rms_norm(x, g):
    xf = x.astype(jnp.float32)
    y = xf * lax.rsqrt(jnp.mean(xf * xf, axis=-1, keepdims=True) + EPS)
    return (y * g).astype(x.dtype)


def modulate(x, g, shift, scale):
    return rms_norm(x, g) * (1 + scale) + shift


def split_heads(t):
    b, n, _ = t.shape
    return t.reshape(b, n, NA_HEADS, HEAD_DIM)


def head_rms(t, g):
    return rms_norm(split_heads(t), g)


def dwconv(x, w, b, left):
    k_w, ch = w.shape
    y = lax.conv_general_dilated(x, w[:, None, :], window_strides=(1,), padding=[(left, k_w - 1 - left)],
                                 dimension_numbers=('NWC', 'WIO', 'NWC'), feature_group_count=ch)
    return y + b


def _rotate(x, pos):
    n = x.shape[-1] // 2
    freq = ROPE_BASE ** (-jnp.arange(n, dtype=jnp.float32) / n)
    ang = pos.astype(jnp.float32)[:, None] * freq
    cos = jnp.cos(ang)[None, :, None, :]
    sin = jnp.sin(ang)[None, :, None, :]
    x1 = x[..., :n].astype(jnp.float32)
    x2 = x[..., n:].astype(jnp.float32)
    return jnp.concatenate([x1 * cos - x2 * sin, x2 * cos + x1 * sin], axis=-1).astype(x.dtype)


def rope_2d(x, row_pos, col_pos):
    h = x.shape[-1] // 2
    return jnp.concatenate([_rotate(x[..., :h], row_pos), _rotate(x[..., h:], col_pos)], axis=-1)


def rglru_coeffs(xc, wa, ba, wx, bx, lam):
    b_, t_, ch = xc.shape
    xb = xc.reshape(b_, t_, LRU_BLOCKS, LRU_BLOCK_W)
    r = jax.nn.sigmoid((jnp.einsum('btnk,nkj->btnj', xb, wa).reshape(b_, t_, ch) + ba).astype(jnp.float32))
    i = jax.nn.sigmoid((jnp.einsum('btnk,nkj->btnj', xb, wx).reshape(b_, t_, ch) + bx).astype(jnp.float32))
    log_a = -LRU_C * r * jax.nn.softplus(-lam.astype(jnp.float32))
    a = jnp.exp(log_a)
    b = jnp.sqrt(-jnp.expm1(2.0 * log_a)) * i * xc.astype(jnp.float32)
    return a, b


def _lin_combine(left, right):
    a_l, b_l = left
    a_r, b_r = right
    return a_l * a_r, a_r * b_l + b_r


def linear_scan(a, b, h0=None):
    a_cum, h = lax.associative_scan(_lin_combine, (a, b), axis=1)
    if h0 is None:
        return h
    return h + a_cum * h0[:, None, :]


def rglru_bidirectional(x_lat, x_ctx, wa, ba, wx, bx, lam, need_ctx_out):
    lat_seqs, ctx_seqs = [], []
    for d in range(2):
        a_c, b_c = rglru_coeffs(x_ctx, wa[d], ba[d], wx[d], bx[d], lam[d])
        a_x, b_x = rglru_coeffs(x_lat, wa[d], ba[d], wx[d], bx[d], lam[d])
        if d == 1:
            a_c, b_c, a_x, b_x = [jnp.flip(t, axis=1) for t in (a_c, b_c, a_x, b_x)]
        h_c = linear_scan(a_c, b_c)
        h_x = linear_scan(a_x, b_x, h_c[:, -1])
        if d == 1:
            h_x = jnp.flip(h_x, axis=1)
        lat_seqs.append(h_x)
        if need_ctx_out:
            ctx_seqs.append(jnp.flip(h_c, axis=1) if d == 1 else h_c)
    h_lat = (lat_seqs[0] + lat_seqs[1]).astype(x_lat.dtype)
    h_ctx = (ctx_seqs[0] + ctx_seqs[1]).astype(x_ctx.dtype) if need_ctx_out else None
    return h_lat, h_ctx


def neighborhood_attention(q_rot, q_plain, k_rot, v, k_ctx, v_ctx, rpb):
    b_, t_, h_, dh = q_rot.shape
    rows = t_ // GRID_W
    kh = min(NA_ROWS, rows)
    r = jnp.arange(rows)
    row_start = jnp.clip(r - kh // 2, 0, rows - kh)
    row_idx = row_start[:, None] + jnp.arange(kh)[None, :]
    cidx = jnp.arange(GRID_W)
    col_start = jnp.clip(cidx - NA_COLS // 2, 0, GRID_W - NA_COLS)
    in_win = (cidx[None, :] >= col_start[:, None]) & (cidx[None, :] < col_start[:, None] + NA_COLS)
    dr = row_idx - r[:, None] + (NA_ROWS - 1)
    dc = jnp.clip(cidx[None, :] - cidx[:, None], -(NA_COLS - 1), NA_COLS - 1) + (NA_COLS - 1)
    bias = rpb[:, dr[:, None, :, None], dc[None, :, None, :]]
    bias = jnp.where(in_win[None, None, :, None, :], bias, NEG_INF)
    qg = q_rot.reshape(b_, rows, GRID_W, h_, dh)
    kb = k_rot.reshape(b_, rows, GRID_W, h_, dh)[:, row_idx]
    vb = v.reshape(b_, rows, GRID_W, h_, dh)[:, row_idx]
    scale = dh ** -0.5
    s_lat = jnp.einsum('brchd,brikhd->bhrcik', qg, kb).astype(jnp.float32) * scale + bias[None]
    s_ctx = jnp.einsum('brchd,bnhd->bhrcn', q_plain.reshape(b_, rows, GRID_W, h_, dh), k_ctx).astype(jnp.float32) * scale
    n_lat = kh * GRID_W
    s = jnp.concatenate([s_lat.reshape(b_, h_, rows, GRID_W, n_lat), s_ctx], axis=-1)
    p = jax.nn.softmax(s, axis=-1)
    p_lat = p[..., :n_lat].reshape(b_, h_, rows, GRID_W, kh, GRID_W).astype(v.dtype)
    p_ctx = p[..., n_lat:].astype(v.dtype)
    o = jnp.einsum('bhrcik,brikhd->brchd', p_lat, vb) + jnp.einsum('bhrcn,bnhd->brchd', p_ctx, v_ctx)
    return o.reshape(b_, t_, h_ * dh)


def context_attention(q, k, v):
    b_, n_, h_, dh = q.shape
    s = jnp.einsum('bnhd,bmhd->bhnm', q, k).astype(jnp.float32) * (dh ** -0.5)
    p = jax.nn.softmax(s, axis=-1).astype(v.dtype)
    return jnp.einsum('bhnm,bmhd->bnhd', p, v).reshape(b_, n_, h_ * dh)


def merge_branches(y_rnn, y_na, g_rnn, g_na, w_rnn_out, w_na_out, w_out):
    merged = jax.nn.sigmoid(g_rnn) * (y_rnn @ w_rnn_out) + jax.nn.sigmoid(g_na) * (y_na @ w_na_out)
    return merged @ w_out


def conv_ffn(xn, w_up, conv_w, conv_b, w_down):
    h = dwconv(xn @ w_up, conv_w, conv_b, FFN_CONV_LEFT)
    a, g = jnp.split(h, 2, axis=-1)
    return (jax.nn.silu(a) * g) @ w_down


def _fwd_setup_inputs(seed: int = 0) -> dict:
    key = jax.random.key(seed)
    ks = jax.random.split(key, 32)
    L = DEPTH

    def nrm(k, shape, scale):
        return jax.random.normal(k, shape, jnp.float32) * scale

    a0 = jax.random.uniform(ks[15], (L, 2, D_RNN), jnp.float32, 0.9, 0.999)
    return {
        'x': nrm(ks[0], (BATCH, SEQ, D_MODEL), 1.0),
        'c': nrm(ks[1], (BATCH, D_MODEL), 1.0),
        'ctx': nrm(ks[2], (BATCH, CTX_LEN, D_MODEL), 1.0),
        'c_ctx': nrm(ks[3], (D_MODEL,), 1.0),
        'w_mod': nrm(ks[4], (L, D_MODEL, N_MOD * D_MODEL), 0.5 * D_MODEL ** -0.5),
        'b_mod': nrm(ks[5], (L, N_MOD * D_MODEL), 0.02),
        'norm_mix_g': 1.0 + nrm(ks[6], (L, D_MODEL), 0.02),
        'norm_ffn_g': 1.0 + nrm(ks[7], (L, D_MODEL), 0.02),
        'w_in': nrm(ks[8], (L, D_MODEL, IN_COLS), D_MODEL ** -0.5),
        'lru_conv_w': nrm(ks[9], (L, LRU_CONV_W, D_RNN), LRU_CONV_W ** -0.5),
        'lru_conv_b': nrm(ks[10], (L, D_RNN), 0.02),
        'lru_wa': nrm(ks[11], (L, 2, LRU_BLOCKS, LRU_BLOCK_W, LRU_BLOCK_W), LRU_BLOCK_W ** -0.5),
        'lru_ba': nrm(ks[12], (L, 2, D_RNN), 0.02),
        'lru_wx': nrm(ks[13], (L, 2, LRU_BLOCKS, LRU_BLOCK_W, LRU_BLOCK_W), LRU_BLOCK_W ** -0.5),
        'lru_bx': nrm(ks[14], (L, 2, D_RNN), 0.02),
        'lru_lambda': jnp.log(a0) - jnp.log1p(-a0),
        'q_norm_g': 1.0 + nrm(ks[16], (L, HEAD_DIM), 0.02),
        'k_norm_g': 1.0 + nrm(ks[17], (L, HEAD_DIM), 0.02),
        'na_rpb': nrm(ks[18], (L, NA_HEADS, 2 * NA_ROWS - 1, 2 * NA_COLS - 1), 0.1),
        'w_rnn_out': nrm(ks[19], (L, D_RNN, D_MODEL), D_RNN ** -0.5),
        'w_na_out': nrm(ks[20], (L, D_ATT, D_MODEL), D_ATT ** -0.5),
        'w_out': nrm(ks[21], (L, D_MODEL, D_MODEL), D_MODEL ** -0.5),
        'w_up': nrm(ks[22], (L, D_MODEL, 2 * D_FF), D_MODEL ** -0.5),
        'ffn_conv_w': nrm(ks[23], (L, FFN_CONV_W, 2 * D_FF), FFN_CONV_W ** -0.5),
        'ffn_conv_b': nrm(ks[24], (L, 2 * D_FF), 0.02),
        'w_down': nrm(ks[25], (L, D_FF, D_MODEL), D_FF ** -0.5),
    }


def _fwd_reference(x, c, ctx, c_ctx, w_mod, b_mod, norm_mix_g, norm_ffn_g, w_in, lru_conv_w, lru_conv_b,
              lru_wa, lru_ba, lru_wx, lru_bx, lru_lambda, q_norm_g, k_norm_g, na_rpb,
              w_rnn_out, w_na_out, w_out, w_up, ffn_conv_w, ffn_conv_b, w_down):
    seq = x.shape[1]
    t = jnp.arange(seq)
    row_pos = t // GRID_W
    col_pos = t % GRID_W
    s_c = jax.nn.silu(c)
    s_ctx = jax.nn.silu(c_ctx)
    for l in range(DEPTH):
        last = l == DEPTH - 1
        mx = jnp.split((s_c @ w_mod[l] + b_mod[l])[:, None, :], N_MOD, axis=-1)
        mc = jnp.split((s_ctx @ w_mod[l] + b_mod[l])[None, None, :], N_MOD, axis=-1)
        xn = modulate(x, norm_mix_g[l], mx[0], mx[1])
        cn = modulate(ctx, norm_mix_g[l], mc[0], mc[1])
        xr, kx, vx, gx, qx, mrx, mnx = jnp.split(xn @ w_in[l], list(IN_SPLITS), axis=-1)
        if last:
            cr, kc, vc = jnp.split(cn @ w_in[l][:, :CTX_KV_COLS], [D_RNN, D_RNN + D_ATT], axis=-1)
        else:
            cr, kc, vc, gc, qc, mrc, mnc = jnp.split(cn @ w_in[l], list(IN_SPLITS), axis=-1)
        xr = dwconv(xr, lru_conv_w[l], lru_conv_b[l], LRU_CONV_LEFT)
        cr = dwconv(cr, lru_conv_w[l], lru_conv_b[l], LRU_CONV_LEFT)
        h_x, h_c = rglru_bidirectional(xr, cr, lru_wa[l], lru_ba[l], lru_wx[l], lru_bx[l], lru_lambda[l], not last)
        y_rnn_x = h_x * jax.nn.gelu(gx)
        qx = head_rms(qx, q_norm_g[l])
        kx = head_rms(kx, k_norm_g[l])
        kc = head_rms(kc, k_norm_g[l])
        vx = split_heads(vx)
        vc = split_heads(vc)
        y_na_x = neighborhood_attention(rope_2d(qx, row_pos, col_pos), qx, rope_2d(kx, row_pos, col_pos),
                                        vx, kc, vc, na_rpb[l])
        x = x + mx[2] * merge_branches(y_rnn_x, y_na_x, mrx, mnx, w_rnn_out[l], w_na_out[l], w_out[l])
        x = x + mx[5] * conv_ffn(modulate(x, norm_ffn_g[l], mx[3], mx[4]), w_up[l], ffn_conv_w[l], ffn_conv_b[l], w_down[l])
        if not last:
            y_rnn_c = h_c * jax.nn.gelu(gc)
            y_na_c = context_attention(head_rms(qc, q_norm_g[l]), kc, vc)
            ctx = ctx + mc[2] * merge_branches(y_rnn_c, y_na_c, mrc, mnc, w_rnn_out[l], w_na_out[l], w_out[l])
            ctx = ctx + mc[5] * conv_ffn(modulate(ctx, norm_ffn_g[l], mc[3], mc[4]), w_up[l], ffn_conv_w[l], ffn_conv_b[l], w_down[l])
    return x


import jax as _jax
import jax.numpy as _jnp

TWIN_FORMAT = 'train_step'
FWD_PARAMS = ['x', 'c', 'ctx', 'c_ctx', 'w_mod', 'b_mod', 'norm_mix_g', 'norm_ffn_g', 'w_in', 'lru_conv_w', 'lru_conv_b', 'lru_wa', 'lru_ba', 'lru_wx', 'lru_bx', 'lru_lambda', 'q_norm_g', 'k_norm_g', 'na_rpb', 'w_rnn_out', 'w_na_out', 'w_out', 'w_up', 'ffn_conv_w', 'ffn_conv_b', 'w_down']
TWIN_WEIGHTS = ['c_ctx', 'w_mod', 'b_mod', 'norm_mix_g', 'norm_ffn_g', 'w_in', 'lru_conv_w', 'lru_conv_b', 'lru_wa', 'lru_ba', 'lru_wx', 'lru_bx', 'lru_lambda', 'q_norm_g', 'k_norm_g', 'na_rpb', 'w_rnn_out', 'w_na_out', 'w_out', 'w_up', 'ffn_conv_w', 'ffn_conv_b', 'w_down']
TWIN_DIFF_INPUT = 'x'
TWIN_INPUTS = ['x', 'c', 'ctx', 'c_ctx', 'w_mod', 'b_mod', 'norm_mix_g', 'norm_ffn_g', 'w_in', 'lru_conv_w', 'lru_conv_b', 'lru_wa', 'lru_ba', 'lru_wx', 'lru_bx', 'lru_lambda', 'q_norm_g', 'k_norm_g', 'na_rpb', 'w_rnn_out', 'w_na_out', 'w_out', 'w_up', 'ffn_conv_w', 'ffn_conv_b', 'w_down', 'loss_target', 'm_c_ctx', 'm_w_mod', 'm_b_mod', 'm_norm_mix_g', 'm_norm_ffn_g', 'm_w_in', 'm_lru_conv_w', 'm_lru_conv_b', 'm_lru_wa', 'm_lru_ba', 'm_lru_wx', 'm_lru_bx', 'm_lru_lambda', 'm_q_norm_g', 'm_k_norm_g', 'm_na_rpb', 'm_w_rnn_out', 'm_w_na_out', 'm_w_out', 'm_w_up', 'm_ffn_conv_w', 'm_ffn_conv_b', 'm_w_down', 'v_c_ctx', 'v_w_mod', 'v_b_mod', 'v_norm_mix_g', 'v_norm_ffn_g', 'v_w_in', 'v_lru_conv_w', 'v_lru_conv_b', 'v_lru_wa', 'v_lru_ba', 'v_lru_wx', 'v_lru_bx', 'v_lru_lambda', 'v_q_norm_g', 'v_k_norm_g', 'v_na_rpb', 'v_w_rnn_out', 'v_w_na_out', 'v_w_out', 'v_w_up', 'v_ffn_conv_w', 'v_ffn_conv_b', 'v_w_down']
TWIN_OUTPUTS = ['loss', 'grad_x', 'grad_c_ctx', 'grad_w_mod', 'grad_b_mod', 'grad_norm_mix_g', 'grad_norm_ffn_g', 'grad_w_in', 'grad_lru_conv_w', 'grad_lru_conv_b', 'grad_lru_wa', 'grad_lru_ba', 'grad_lru_wx', 'grad_lru_bx', 'grad_lru_lambda', 'grad_q_norm_g', 'grad_k_norm_g', 'grad_na_rpb', 'grad_w_rnn_out', 'grad_w_na_out', 'grad_w_out', 'grad_w_up', 'grad_ffn_conv_w', 'grad_ffn_conv_b', 'grad_w_down', 'delta_c_ctx', 'delta_w_mod', 'delta_b_mod', 'delta_norm_mix_g', 'delta_norm_ffn_g', 'delta_w_in', 'delta_lru_conv_w', 'delta_lru_conv_b', 'delta_lru_wa', 'delta_lru_ba', 'delta_lru_wx', 'delta_lru_bx', 'delta_lru_lambda', 'delta_q_norm_g', 'delta_k_norm_g', 'delta_na_rpb', 'delta_w_rnn_out', 'delta_w_na_out', 'delta_w_out', 'delta_w_up', 'delta_ffn_conv_w', 'delta_ffn_conv_b', 'delta_w_down', 'new_m_c_ctx', 'new_m_w_mod', 'new_m_b_mod', 'new_m_norm_mix_g', 'new_m_norm_ffn_g', 'new_m_w_in', 'new_m_lru_conv_w', 'new_m_lru_conv_b', 'new_m_lru_wa', 'new_m_lru_ba', 'new_m_lru_wx', 'new_m_lru_bx', 'new_m_lru_lambda', 'new_m_q_norm_g', 'new_m_k_norm_g', 'new_m_na_rpb', 'new_m_w_rnn_out', 'new_m_w_na_out', 'new_m_w_out', 'new_m_w_up', 'new_m_ffn_conv_w', 'new_m_ffn_conv_b', 'new_m_w_down', 'new_v_c_ctx', 'new_v_w_mod', 'new_v_b_mod', 'new_v_norm_mix_g', 'new_v_norm_ffn_g', 'new_v_w_in', 'new_v_lru_conv_w', 'new_v_lru_conv_b', 'new_v_lru_wa', 'new_v_lru_ba', 'new_v_lru_wx', 'new_v_lru_bx', 'new_v_lru_lambda', 'new_v_q_norm_g', 'new_v_k_norm_g', 'new_v_na_rpb', 'new_v_w_rnn_out', 'new_v_w_na_out', 'new_v_w_out', 'new_v_w_up', 'new_v_ffn_conv_w', 'new_v_ffn_conv_b', 'new_v_w_down']
TWIN_LEAF_KINDS = {'loss': 'loss', 'grad_x': 'grad_x', 'grad_c_ctx': 'grad_w', 'grad_w_mod': 'grad_w', 'grad_b_mod': 'grad_w', 'grad_norm_mix_g': 'grad_w', 'grad_norm_ffn_g': 'grad_w', 'grad_w_in': 'grad_w', 'grad_lru_conv_w': 'grad_w', 'grad_lru_conv_b': 'grad_w', 'grad_lru_wa': 'grad_w', 'grad_lru_ba': 'grad_w', 'grad_lru_wx': 'grad_w', 'grad_lru_bx': 'grad_w', 'grad_lru_lambda': 'grad_w', 'grad_q_norm_g': 'grad_w', 'grad_k_norm_g': 'grad_w', 'grad_na_rpb': 'grad_w', 'grad_w_rnn_out': 'grad_w', 'grad_w_na_out': 'grad_w', 'grad_w_out': 'grad_w', 'grad_w_up': 'grad_w', 'grad_ffn_conv_w': 'grad_w', 'grad_ffn_conv_b': 'grad_w', 'grad_w_down': 'grad_w', 'delta_c_ctx': 'delta_w', 'delta_w_mod': 'delta_w', 'delta_b_mod': 'delta_w', 'delta_norm_mix_g': 'delta_w', 'delta_norm_ffn_g': 'delta_w', 'delta_w_in': 'delta_w', 'delta_lru_conv_w': 'delta_w', 'delta_lru_conv_b': 'delta_w', 'delta_lru_wa': 'delta_w', 'delta_lru_ba': 'delta_w', 'delta_lru_wx': 'delta_w', 'delta_lru_bx': 'delta_w', 'delta_lru_lambda': 'delta_w', 'delta_q_norm_g': 'delta_w', 'delta_k_norm_g': 'delta_w', 'delta_na_rpb': 'delta_w', 'delta_w_rnn_out': 'delta_w', 'delta_w_na_out': 'delta_w', 'delta_w_out': 'delta_w', 'delta_w_up': 'delta_w', 'delta_ffn_conv_w': 'delta_w', 'delta_ffn_conv_b': 'delta_w', 'delta_w_down': 'delta_w', 'new_m_c_ctx': 'new_m', 'new_m_w_mod': 'new_m', 'new_m_b_mod': 'new_m', 'new_m_norm_mix_g': 'new_m', 'new_m_norm_ffn_g': 'new_m', 'new_m_w_in': 'new_m', 'new_m_lru_conv_w': 'new_m', 'new_m_lru_conv_b': 'new_m', 'new_m_lru_wa': 'new_m', 'new_m_lru_ba': 'new_m', 'new_m_lru_wx': 'new_m', 'new_m_lru_bx': 'new_m', 'new_m_lru_lambda': 'new_m', 'new_m_q_norm_g': 'new_m', 'new_m_k_norm_g': 'new_m', 'new_m_na_rpb': 'new_m', 'new_m_w_rnn_out': 'new_m', 'new_m_w_na_out': 'new_m', 'new_m_w_out': 'new_m', 'new_m_w_up': 'new_m', 'new_m_ffn_conv_w': 'new_m', 'new_m_ffn_conv_b': 'new_m', 'new_m_w_down': 'new_m', 'new_v_c_ctx': 'new_v', 'new_v_w_mod': 'new_v', 'new_v_b_mod': 'new_v', 'new_v_norm_mix_g': 'new_v', 'new_v_norm_ffn_g': 'new_v', 'new_v_w_in': 'new_v', 'new_v_lru_conv_w': 'new_v', 'new_v_lru_conv_b': 'new_v', 'new_v_lru_wa': 'new_v', 'new_v_lru_ba': 'new_v', 'new_v_lru_wx': 'new_v', 'new_v_lru_bx': 'new_v', 'new_v_lru_lambda': 'new_v', 'new_v_q_norm_g': 'new_v', 'new_v_k_norm_g': 'new_v', 'new_v_na_rpb': 'new_v', 'new_v_w_rnn_out': 'new_v', 'new_v_w_na_out': 'new_v', 'new_v_w_out': 'new_v', 'new_v_w_up': 'new_v', 'new_v_ffn_conv_w': 'new_v', 'new_v_ffn_conv_b': 'new_v', 'new_v_w_down': 'new_v'}


def _forward(args):
    return _fwd_reference(*[args[k] for k in FWD_PARAMS])


def _output_shape():
    out = _jax.eval_shape(lambda: _forward(_fwd_setup_inputs(0)))
    return out.shape, out.dtype

N_MICROBATCH = 1
ADAM_LR = 0.001
ADAM_B1 = 0.9
ADAM_B2 = 0.999
ADAM_EPS = 1e-08
ADAM_WD = 0.01
ADAM_STEP = 10
PER_EXAMPLE_BATCH_AXIS = {'x': 0, 'c': 0, 'ctx': 0, 'loss_target': 0}
SHARED_INPUTS = []
_WEIGHT_DTYPES = {'c_ctx': _jnp.float32, 'w_mod': _jnp.float32, 'b_mod': _jnp.float32, 'norm_mix_g': _jnp.float32, 'norm_ffn_g': _jnp.float32, 'w_in': _jnp.float32, 'lru_conv_w': _jnp.float32, 'lru_conv_b': _jnp.float32, 'lru_wa': _jnp.float32, 'lru_ba': _jnp.float32, 'lru_wx': _jnp.float32, 'lru_bx': _jnp.float32, 'lru_lambda': _jnp.float32, 'q_norm_g': _jnp.float32, 'k_norm_g': _jnp.float32, 'na_rpb': _jnp.float32, 'w_rnn_out': _jnp.float32, 'w_na_out': _jnp.float32, 'w_out': _jnp.float32, 'w_up': _jnp.float32, 'ffn_conv_w': _jnp.float32, 'ffn_conv_b': _jnp.float32, 'w_down': _jnp.float32}
MOMENT_SCALE = {'c_ctx': 6.354114e-02, 'w_mod': 8.130678e-01, 'b_mod': 1.614230e+00, 'norm_mix_g': 9.384471e-01, 'norm_ffn_g': 1.874838e+00, 'w_in': 2.613090e-01, 'lru_conv_w': 7.622788e-01, 'lru_conv_b': 1.410411e+00, 'lru_wa': 3.668775e-02, 'lru_ba': 7.774947e-02, 'lru_wx': 9.495440e-02, 'lru_bx': 2.797330e-01, 'lru_lambda': 2.084468e-01, 'q_norm_g': 5.501054e-02, 'k_norm_g': 5.556137e-02, 'na_rpb': 9.186608e-04, 'w_rnn_out': 1.477584e-01, 'w_na_out': 4.986644e-02, 'w_out': 1.303272e-01, 'w_up': 7.160127e-02, 'ffn_conv_w': 2.852284e-01, 'ffn_conv_b': 2.293430e-01, 'w_down': 5.800377e-02}


def _to_microbatches(a, axis):
    t = _jnp.moveaxis(a, axis, 0)
    t = t.reshape((N_MICROBATCH, t.shape[0] // N_MICROBATCH) + t.shape[1:])
    return _jnp.moveaxis(t, 1, axis + 1)


def setup_inputs(seed: int = 0) -> dict:
    inp = _fwd_setup_inputs(seed)
    key = _jax.random.fold_in(_jax.random.key(seed), 7919)
    shape, _ = _output_shape()
    out = dict(inp)
    out["loss_target"] = _jax.random.normal(_jax.random.fold_in(key, 0), shape, _jnp.float32)
    for i, name in enumerate(TWIN_WEIGHTS):
        w = inp[name].astype(_jnp.float32)
        if MOMENT_SCALE is None:
            s = _jnp.sqrt(_jnp.mean(_jnp.square(w)) + 1e-30)
        else:
            s = MOMENT_SCALE[name]
        km, kv = _jax.random.split(_jax.random.fold_in(key, i + 1))
        out[name] = w
        out["m_" + name] = s * _jax.random.normal(km, w.shape, _jnp.float32)
        out["v_" + name] = (s * s) * _jax.random.uniform(kv, w.shape, _jnp.float32, 0.5, 1.5)
    if N_MICROBATCH > 1:
        for name, axis in PER_EXAMPLE_BATCH_AXIS.items():
            out[name] = _to_microbatches(out[name], axis)
    return {'x': out['x'], 'c': out['c'], 'ctx': out['ctx'], 'c_ctx': out['c_ctx'], 'w_mod': out['w_mod'], 'b_mod': out['b_mod'], 'norm_mix_g': out['norm_mix_g'], 'norm_ffn_g': out['norm_ffn_g'], 'w_in': out['w_in'], 'lru_conv_w': out['lru_conv_w'], 'lru_conv_b': out['lru_conv_b'], 'lru_wa': out['lru_wa'], 'lru_ba': out['lru_ba'], 'lru_wx': out['lru_wx'], 'lru_bx': out['lru_bx'], 'lru_lambda': out['lru_lambda'], 'q_norm_g': out['q_norm_g'], 'k_norm_g': out['k_norm_g'], 'na_rpb': out['na_rpb'], 'w_rnn_out': out['w_rnn_out'], 'w_na_out': out['w_na_out'], 'w_out': out['w_out'], 'w_up': out['w_up'], 'ffn_conv_w': out['ffn_conv_w'], 'ffn_conv_b': out['ffn_conv_b'], 'w_down': out['w_down'], 'loss_target': out['loss_target'], 'm_c_ctx': out['m_c_ctx'], 'm_w_mod': out['m_w_mod'], 'm_b_mod': out['m_b_mod'], 'm_norm_mix_g': out['m_norm_mix_g'], 'm_norm_ffn_g': out['m_norm_ffn_g'], 'm_w_in': out['m_w_in'], 'm_lru_conv_w': out['m_lru_conv_w'], 'm_lru_conv_b': out['m_lru_conv_b'], 'm_lru_wa': out['m_lru_wa'], 'm_lru_ba': out['m_lru_ba'], 'm_lru_wx': out['m_lru_wx'], 'm_lru_bx': out['m_lru_bx'], 'm_lru_lambda': out['m_lru_lambda'], 'm_q_norm_g': out['m_q_norm_g'], 'm_k_norm_g': out['m_k_norm_g'], 'm_na_rpb': out['m_na_rpb'], 'm_w_rnn_out': out['m_w_rnn_out'], 'm_w_na_out': out['m_w_na_out'], 'm_w_out': out['m_w_out'], 'm_w_up': out['m_w_up'], 'm_ffn_conv_w': out['m_ffn_conv_w'], 'm_ffn_conv_b': out['m_ffn_conv_b'], 'm_w_down': out['m_w_down'], 'v_c_ctx': out['v_c_ctx'], 'v_w_mod': out['v_w_mod'], 'v_b_mod': out['v_b_mod'], 'v_norm_mix_g': out['v_norm_mix_g'], 'v_norm_ffn_g': out['v_norm_ffn_g'], 'v_w_in': out['v_w_in'], 'v_lru_conv_w': out['v_lru_conv_w'], 'v_lru_conv_b': out['v_lru_conv_b'], 'v_lru_wa': out['v_lru_wa'], 'v_lru_ba': out['v_lru_ba'], 'v_lru_wx': out['v_lru_wx'], 'v_lru_bx': out['v_lru_bx'], 'v_lru_lambda': out['v_lru_lambda'], 'v_q_norm_g': out['v_q_norm_g'], 'v_k_norm_g': out['v_k_norm_g'], 'v_na_rpb': out['v_na_rpb'], 'v_w_rnn_out': out['v_w_rnn_out'], 'v_w_na_out': out['v_w_na_out'], 'v_w_out': out['v_w_out'], 'v_w_up': out['v_w_up'], 'v_ffn_conv_w': out['v_ffn_conv_w'], 'v_ffn_conv_b': out['v_ffn_conv_b'], 'v_w_down': out['v_w_down']}


def _loss(weights, diff, rest, loss_target):
    with _jax.named_scope("forward"):
        args = {**rest, TWIN_DIFF_INPUT: diff, **{k: w.astype(_WEIGHT_DTYPES[k]) for k, w in weights.items()}}
        y = _forward(args)
    with _jax.named_scope("loss_head"):
        err = _jnp.square(y.astype(_jnp.float32) - loss_target)
        return 0.5 * _jnp.sum(_jnp.mean(err, axis=-1)) if err.ndim else 0.5 * err


def _adamw(w, g, m, v):
    m = ADAM_B1 * m + (1.0 - ADAM_B1) * g
    v = ADAM_B2 * v + (1.0 - ADAM_B2) * _jnp.square(g)
    m_hat = m / (1.0 - ADAM_B1 ** ADAM_STEP)
    v_hat = v / (1.0 - ADAM_B2 ** ADAM_STEP)
    delta = -ADAM_LR * (m_hat / (_jnp.sqrt(v_hat) + ADAM_EPS) + ADAM_WD * w)
    return delta, m, v


def reference(x, c, ctx, c_ctx, w_mod, b_mod, norm_mix_g, norm_ffn_g, w_in, lru_conv_w, lru_conv_b, lru_wa, lru_ba, lru_wx, lru_bx, lru_lambda, q_norm_g, k_norm_g, na_rpb, w_rnn_out, w_na_out, w_out, w_up, ffn_conv_w, ffn_conv_b, w_down, loss_target, m_c_ctx, m_w_mod, m_b_mod, m_norm_mix_g, m_norm_ffn_g, m_w_in, m_lru_conv_w, m_lru_conv_b, m_lru_wa, m_lru_ba, m_lru_wx, m_lru_bx, m_lru_lambda, m_q_norm_g, m_k_norm_g, m_na_rpb, m_w_rnn_out, m_w_na_out, m_w_out, m_w_up, m_ffn_conv_w, m_ffn_conv_b, m_w_down, v_c_ctx, v_w_mod, v_b_mod, v_norm_mix_g, v_norm_ffn_g, v_w_in, v_lru_conv_w, v_lru_conv_b, v_lru_wa, v_lru_ba, v_lru_wx, v_lru_bx, v_lru_lambda, v_q_norm_g, v_k_norm_g, v_na_rpb, v_w_rnn_out, v_w_na_out, v_w_out, v_w_up, v_ffn_conv_w, v_ffn_conv_b, v_w_down):
    given = dict(x=x, c=c, ctx=ctx, c_ctx=c_ctx, w_mod=w_mod, b_mod=b_mod, norm_mix_g=norm_mix_g, norm_ffn_g=norm_ffn_g, w_in=w_in, lru_conv_w=lru_conv_w, lru_conv_b=lru_conv_b, lru_wa=lru_wa, lru_ba=lru_ba, lru_wx=lru_wx, lru_bx=lru_bx, lru_lambda=lru_lambda, q_norm_g=q_norm_g, k_norm_g=k_norm_g, na_rpb=na_rpb, w_rnn_out=w_rnn_out, w_na_out=w_na_out, w_out=w_out, w_up=w_up, ffn_conv_w=ffn_conv_w, ffn_conv_b=ffn_conv_b, w_down=w_down, loss_target=loss_target, m_c_ctx=m_c_ctx, m_w_mod=m_w_mod, m_b_mod=m_b_mod, m_norm_mix_g=m_norm_mix_g, m_norm_ffn_g=m_norm_ffn_g, m_w_in=m_w_in, m_lru_conv_w=m_lru_conv_w, m_lru_conv_b=m_lru_conv_b, m_lru_wa=m_lru_wa, m_lru_ba=m_lru_ba, m_lru_wx=m_lru_wx, m_lru_bx=m_lru_bx, m_lru_lambda=m_lru_lambda, m_q_norm_g=m_q_norm_g, m_k_norm_g=m_k_norm_g, m_na_rpb=m_na_rpb, m_w_rnn_out=m_w_rnn_out, m_w_na_out=m_w_na_out, m_w_out=m_w_out, m_w_up=m_w_up, m_ffn_conv_w=m_ffn_conv_w, m_ffn_conv_b=m_ffn_conv_b, m_w_down=m_w_down, v_c_ctx=v_c_ctx, v_w_mod=v_w_mod, v_b_mod=v_b_mod, v_norm_mix_g=v_norm_mix_g, v_norm_ffn_g=v_norm_ffn_g, v_w_in=v_w_in, v_lru_conv_w=v_lru_conv_w, v_lru_conv_b=v_lru_conv_b, v_lru_wa=v_lru_wa, v_lru_ba=v_lru_ba, v_lru_wx=v_lru_wx, v_lru_bx=v_lru_bx, v_lru_lambda=v_lru_lambda, v_q_norm_g=v_q_norm_g, v_k_norm_g=v_k_norm_g, v_na_rpb=v_na_rpb, v_w_rnn_out=v_w_rnn_out, v_w_na_out=v_w_na_out, v_w_out=v_w_out, v_w_up=v_w_up, v_ffn_conv_w=v_ffn_conv_w, v_ffn_conv_b=v_ffn_conv_b, v_w_down=v_w_down)
    weights = {n: given[n] for n in TWIN_WEIGHTS}
    shared = {n: given[n] for n in SHARED_INPUTS}
    per_example = {n: given[n] for n in ['x', 'c', 'ctx']}
    grad_fn = _jax.value_and_grad(_loss, argnums=(0, 1))

    def one_microbatch(ex, loss_target):
        ex = dict(ex)
        diff = ex.pop(TWIN_DIFF_INPUT)
        return grad_fn(weights, diff, {**shared, **ex}, loss_target)

    if N_MICROBATCH == 1:
        loss, (grad_w, grad_x) = one_microbatch(per_example, given["loss_target"])
    else:
        def body(carry, xs):
            loss_sum, grad_sum = carry
            l_k, (gw_k, gx_k) = one_microbatch(xs[0], xs[1])
            with _jax.named_scope("update"):
                return (loss_sum + l_k, _jax.tree.map(_jnp.add, grad_sum, gw_k)), gx_k

        init = (_jnp.zeros((), _jnp.float32), _jax.tree.map(_jnp.zeros_like, weights))
        (loss, grad_w), grad_x = _jax.lax.scan(body, init, (per_example, given["loss_target"]))
    with _jax.named_scope("update"):
        delta_w, new_m, new_v = {}, {}, {}
        for n in TWIN_WEIGHTS:
            delta_w[n], new_m[n], new_v[n] = _adamw(weights[n], grad_w[n], given["m_" + n], given["v_" + n])
    return (loss, grad_x, *[grad_w[n] for n in TWIN_WEIGHTS], *[delta_w[n] for n in TWIN_WEIGHTS],
            *[new_m[n] for n in TWIN_WEIGHTS], *[new_v[n] for n in TWIN_WEIGHTS])
```

```python
import functools

import numpy as np
import jax
import jax.numpy as jnp
from jax import lax
from jax.experimental import pallas as pl
from jax.experimental.pallas import tpu as pltpu

F32 = jnp.float32
BF16 = jnp.bfloat16

D_MODEL = 1024
SEQ = 2048
CTX_LEN = 256
ZLEN = SEQ + CTX_LEN
GRID_W = 64
GRID_ROWS = SEQ // GRID_W
LRU_BLOCK_W = 128
LRU_BLOCKS = 8
LRU_C = 8.0
NA_HEADS = 16
HEAD_DIM = 64
NA_ROWS = 8
NA_COLS = 16
ROPE_BASE = 10000.0
D_FF = 2816
N_MOD = 6
IN_COLS = 7 * D_MODEL
EPS = 1e-6
NEG_INF = -1e30
N_DEV = 8
N_SHARD = 4

ADAM_LR = 0.001
ADAM_B1 = 0.9
ADAM_B2 = 0.999
ADAM_EPS = 1e-08
ADAM_WD = 0.01
ADAM_STEP = 10

ROW_TILE = 256
Q_ROWS = 4
Q_TILE = Q_ROWS * GRID_W
KEY_ROWS = 12
KEY_TILE = KEY_ROWS * GRID_W
BT_PAD = 4
BT_LEN = 22
VMEM_LIMIT_V7X = 56 * 1024 * 1024

MESH_T = pl.DeviceIdType.MESH


def _params(*sem):
    return pltpu.CompilerParams(dimension_semantics=sem if sem else None, vmem_limit_bytes=VMEM_LIMIT_V7X)


def _full(shape):
    nd = len(shape)
    return pl.BlockSpec(shape, lambda *_: (0,) * nd)


def _sigmoid(x):
    return 1.0 / (1.0 + jnp.exp(-x))


def _gelu_parts(x):
    c0 = 0.7978845608028654
    inner = c0 * (x + 0.044715 * x * x * x)
    t = jnp.tanh(inner)
    g = 0.5 * x * (1.0 + t)
    dg = 0.5 * (1.0 + t) + 0.5 * x * (1.0 - t * t) * c0 * (1.0 + 3.0 * 0.044715 * x * x)
    return g, dg


def cast_bf16(x, name):
    r, c = x.shape
    tr = next(t for t in (r if r <= 512 else 512, 352, 256, 128) if r % t == 0)

    def body(x_ref, o_ref):
        o_ref[...] = x_ref[...].astype(BF16)

    return pl.pallas_call(
        body, name=name, grid=(r // tr,),
        in_specs=[pl.BlockSpec((tr, c), lambda i: (i, 0))],
        out_specs=pl.BlockSpec((tr, c), lambda i: (i, 0)),
        out_shape=jax.ShapeDtypeStruct((r, c), BF16),
        compiler_params=_params("parallel"),
    )(x)


def transpose_bf16(x, name, tile=256):
    r, c = x.shape
    assert r % tile == 0 and c % tile == 0

    def body(x_ref, o_ref):
        o_ref[...] = x_ref[...].astype(F32).T.astype(BF16)

    return pl.pallas_call(
        body, name=name, grid=(r // tile, c // tile),
        in_specs=[pl.BlockSpec((tile, tile), lambda i, j: (i, j))],
        out_specs=pl.BlockSpec((tile, tile), lambda i, j: (j, i)),
        out_shape=jax.ShapeDtypeStruct((c, r), BF16),
        compiler_params=_params("parallel", "parallel"),
    )(x)


def matmul_nn(a, b, name, tm, tn, out_dtype=BF16):
    m, k = a.shape
    k2, n = b.shape
    assert k == k2 and m % tm == 0 and n % tn == 0

    def body(a_ref, b_ref, o_ref):
        o_ref[...] = jnp.dot(a_ref[...], b_ref[...], preferred_element_type=F32).astype(out_dtype)

    return pl.pallas_call(
        body, name=name, grid=(m // tm, n // tn),
        in_specs=[pl.BlockSpec((tm, k), lambda i, j: (i, 0)), pl.BlockSpec((k, tn), lambda i, j: (0, j))],
        out_specs=pl.BlockSpec((tm, tn), lambda i, j: (i, j)),
        out_shape=jax.ShapeDtypeStruct((m, n), out_dtype),
        compiler_params=_params("parallel", "parallel"),
    )(a, b)


def _dot_nt(a, b):
    return lax.dot_general(a, b, (((1,), (1,)), ((), ())), preferred_element_type=F32)


def _dot_tn(a, b):
    return lax.dot_general(a, b, (((0,), (0,)), ((), ())), preferred_element_type=F32)


def norm_mod(xin, gain, shift, scale, name):
    r, d = xin.shape
    s_mod = shift.shape[0]
    assert r % ROW_TILE == 0

    def body(x_ref, g_ref, sh_ref, sc_ref, xn_ref):
        x = x_ref[...]
        nrm = x * lax.rsqrt(jnp.mean(x * x, axis=-1, keepdims=True) + EPS)
        xn_ref[...] = ((nrm * g_ref[...]) * (1.0 + sc_ref[0]) + sh_ref[0]).astype(BF16)

    mod_spec = pl.BlockSpec((1, 1, d), lambda i: (jnp.minimum(i, s_mod - 1), 0, 0))
    return pl.pallas_call(
        body, name=name, grid=(r // ROW_TILE,),
        in_specs=[pl.BlockSpec((ROW_TILE, d), lambda i: (i, 0)), _full((1, d)), mod_spec, mod_spec],
        out_specs=pl.BlockSpec((ROW_TILE, d), lambda i: (i, 0)),
        out_shape=jax.ShapeDtypeStruct((r, d), BF16),
        compiler_params=_params("parallel"),
    )(xin, gain, shift, scale)


def matmul_wide(a, b, name, tm, tn):
    m, k = a.shape
    n = b.shape[1]
    assert m % tm == 0 and n % tn == 0

    def body(a_ref, b_ref, o_ref):
        o_ref[...] = jnp.dot(a_ref[...], b_ref[...], preferred_element_type=F32)

    return pl.pallas_call(
        body, name=name, grid=(n // tn, m // tm),
        in_specs=[pl.BlockSpec((tm, k), lambda j, i: (i, 0)), pl.BlockSpec((k, tn), lambda j, i: (0, j))],
        out_specs=pl.BlockSpec((tm, tn), lambda j, i: (i, j)),
        out_shape=jax.ShapeDtypeStruct((m, n), F32),
        compiler_params=_params("parallel", "parallel"),
    )(a, b)


def _row_ids(n, w):
    return lax.broadcasted_iota(jnp.int32, (n, w), 0)


def _lru_conv(xr, cw, cb):
    row = _row_ids(ZLEN, LRU_BLOCK_W)
    segpos = jnp.where(row < CTX_LEN, row, row - CTX_LEN)
    seglen = jnp.where(row < CTX_LEN, CTX_LEN, SEQ)
    acc = xr * cw[2:3, :] + cb
    for k in (0, 1, 3):
        off = k - 2
        sh = pltpu.roll(xr, (-off) % ZLEN, 0)
        ok = (segpos + off >= 0) & (segpos + off < seglen)
        acc = acc + jnp.where(ok, sh, 0.0) * cw[k:k + 1, :]
    return acc


def _lru_conv_t(dxc, cw):
    row = _row_ids(ZLEN, LRU_BLOCK_W)
    segpos = jnp.where(row < CTX_LEN, row, row - CTX_LEN)
    seglen = jnp.where(row < CTX_LEN, CTX_LEN, SEQ)
    acc = dxc * cw[2:3, :]
    for k in (0, 1, 3):
        off = k - 2
        sh = pltpu.roll(dxc, off % ZLEN, 0)
        ok = (segpos - off >= 0) & (segpos - off < seglen)
        acc = acc + jnp.where(ok, sh, 0.0) * cw[k:k + 1, :]
    return acc


def _lru_gates(xc, xcb, wa, ba, wx, bx, lam):
    r = _sigmoid(jnp.dot(xcb, wa, preferred_element_type=F32) + ba)
    i = _sigmoid(jnp.dot(xcb, wx, preferred_element_type=F32) + bx)
    sp = jnp.maximum(-lam, 0.0) + jnp.log1p(jnp.exp(-jnp.abs(lam)))
    la = (-LRU_C) * r * sp
    a = jnp.exp(la)
    sq = jnp.sqrt(-jnp.tanh(la) * (1.0 + a * a))
    b = sq * i * xc
    return r, i, sp, a, sq, b


def _scan8_fwd(a, b, rid):
    for s in (1, 2, 4):
        a_s = pltpu.roll(a, s, 0)
        b_s = pltpu.roll(b, s, 0)
        m = rid >= s
        b = jnp.where(m, a * b_s + b, b)
        a = jnp.where(m, a * a_s, a)
    return a, b


def _scan8_rev(a, b, rid):
    for s in (1, 2, 4):
        a_s = pltpu.roll(a, 8 - s, 0)
        b_s = pltpu.roll(b, 8 - s, 0)
        m = rid < 8 - s
        b = jnp.where(m, a * b_s + b, b)
        a = jnp.where(m, a * a_s, a)
    return a, b


N_CHUNK = ZLEN // 8
CTX_CHUNKS = CTX_LEN // 8


def _scan_up(a_ref, b_ref, h_ref, lo, hi, carry):
    rid = _row_ids(8, LRU_BLOCK_W)

    def step(i, c):
        sl = pl.ds(pl.multiple_of(i * 8, 8), 8)
        a, b = _scan8_fwd(a_ref[sl, :], b_ref[sl, :], rid)
        h = b + a * c
        h_ref[sl, :] = h
        return h[7:8, :]

    return lax.fori_loop(lo, hi, step, carry)


def _scan_down(a_ref, b_ref, h_ref, lo, hi, carry):
    rid = _row_ids(8, LRU_BLOCK_W)

    def step(k, c):
        i = hi - 1 - k
        sl = pl.ds(pl.multiple_of(i * 8, 8), 8)
        a, b = _scan8_rev(a_ref[sl, :], b_ref[sl, :], rid)
        h = b + a * c
        h_ref[sl, :] = h
        return h[0:1, :]

    return lax.fori_loop(0, hi - lo, step, carry)


def _lru_scan_dir(d, a_ref, b_ref, h_ref):
    zero = jnp.zeros((1, LRU_BLOCK_W), F32)
    if d == 0:
        _scan_up(a_ref, b_ref, h_ref, 0, N_CHUNK, zero)
    else:
        c = _scan_down(a_ref, b_ref, h_ref, 0, CTX_CHUNKS, zero)
        _scan_down(a_ref, b_ref, h_ref, CTX_CHUNKS, N_CHUNK, c)


def _lru_in_specs():
    blk = lambda rows: pl.BlockSpec((rows, LRU_BLOCK_W), lambda b: (0, b))
    wspec = pl.BlockSpec((2, 1, LRU_BLOCK_W, LRU_BLOCK_W), lambda b: (0, b, 0, 0))
    return blk, wspec


def lru_fwd(p, conv_w, conv_b, wa, ba, wx, bx, lam):
    blk, wspec = _lru_in_specs()

    def body(xr_ref, gx_ref, cw_ref, cb_ref, wa_ref, ba_ref, wx_ref, bx_ref, lam_ref, y_ref, a_s, b_s, h_s, hsum_s):
        xr = xr_ref[...]
        xc = _lru_conv(xr, cw_ref[...], cb_ref[...])
        xcb = xc.astype(BF16)
        for d in (0, 1):
            _, _, _, a, _, b = _lru_gates(xc, xcb, wa_ref[d, 0].astype(BF16), ba_ref[d:d + 1, :],
                                          wx_ref[d, 0].astype(BF16), bx_ref[d:d + 1, :], lam_ref[d:d + 1, :])
            a_s[...] = a
            b_s[...] = b
            _lru_scan_dir(d, a_s, b_s, h_s)
            if d == 0:
                hsum_s[...] = h_s[...]
            else:
                hsum_s[...] = hsum_s[...] + h_s[...]
        g, _ = _gelu_parts(gx_ref[CTX_LEN:, :])
        y_ref[...] = (hsum_s[CTX_LEN:, :] * g).astype(BF16)

    zs = pltpu.VMEM((ZLEN, LRU_BLOCK_W), F32)
    return pl.pallas_call(
        body, name="lru_fwd", grid=(LRU_BLOCKS,),
        in_specs=[blk(ZLEN), pl.BlockSpec((ZLEN, LRU_BLOCK_W), lambda b: (0, 24 + b)), blk(4), blk(1),
                  wspec, blk(2), wspec, blk(2), blk(2)],
        out_specs=pl.BlockSpec((SEQ, LRU_BLOCK_W), lambda b: (0, b)),
        out_shape=jax.ShapeDtypeStruct((SEQ, D_MODEL), BF16),
        scratch_shapes=[zs, zs, zs, zs],
        compiler_params=_params("arbitrary"),
    )(p, p, conv_w, conv_b, wa, ba, wx, bx, lam)


def _rope_tables():
    t = np.arange(SEQ)
    lane = np.arange(2 * HEAD_DIM)
    in_head = lane % HEAD_DIM
    j = (in_head % 32) % 16
    freq = ROPE_BASE ** (-j.astype(np.float64) / 16.0)
    pos = np.where(in_head[None, :] < 32, (t // GRID_W)[:, None], (t % GRID_W)[:, None]).astype(np.float64)
    ang = (pos.astype(np.float32) * freq.astype(np.float32)[None, :]).astype(np.float32)
    cos = np.cos(ang).astype(np.float32)
    sin = np.sin(ang).astype(np.float32)
    sgn = np.where((in_head % 32) < 16, -1.0, 1.0).astype(np.float32)
    cos = np.concatenate([np.ones((CTX_LEN, 2 * HEAD_DIM), np.float32), cos], 0)
    sin = np.concatenate([np.zeros((CTX_LEN, 2 * HEAD_DIM), np.float32), sin * sgn[None, :]], 0)
    return jnp.asarray(cos), jnp.asarray(sin)


def _head_ones():
    lane = np.arange(2 * HEAD_DIM)
    return jnp.asarray((lane[:, None] // HEAD_DIM == lane[None, :] // HEAD_DIM).astype(np.float32))


def _rope_partner(x):
    lane = lax.broadcasted_iota(jnp.int32, x.shape, 1)
    return jnp.where((lane % 32) < 16, pltpu.roll(x, 128 - 16, 1), pltpu.roll(x, 16, 1))


def _head_rms(x, ones, gain):
    ms = jnp.dot(x * x, ones, preferred_element_type=F32, precision=lax.Precision.HIGHEST) * (1.0 / HEAD_DIM)
    rstd = lax.rsqrt(ms + EPS)
    return x * rstd * gain, rstd


def qkv_prep(p, qg2, kg2, cos, sin, ones):
    scale = HEAD_DIM ** -0.5

    def body(q_ref, k_ref, v_ref, qg_ref, kg_ref, cos_ref, sin_ref, ones_ref, qr_ref, qp_ref, kk_ref, vv_ref):
        ones_m = ones_ref[...]
        c, s = cos_ref[...], sin_ref[...]
        qn, _ = _head_rms(q_ref[...], ones_m, qg_ref[...])
        qn = qn * scale
        qr = qn * c + _rope_partner(qn) * s
        kn, _ = _head_rms(k_ref[...], ones_m, kg_ref[...])
        kr = kn * c + _rope_partner(kn) * s
        v = v_ref[...]
        for hh in range(2):
            sl = slice(hh * HEAD_DIM, (hh + 1) * HEAD_DIM)
            qr_ref[hh] = qr[:, sl].astype(BF16)
            qp_ref[hh] = qn[:, sl].astype(BF16)
            kk_ref[hh] = kr[:, sl].astype(BF16)
            vv_ref[hh] = v[:, sl].astype(BF16)

    col = lambda base: pl.BlockSpec((ROW_TILE, 128), lambda hp, i: (i, base + hp))
    small = pl.BlockSpec((1, 128), lambda hp, i: (0, 0))
    tab = pl.BlockSpec((ROW_TILE, 128), lambda hp, i: (i, 0))
    ospec = pl.BlockSpec((2, ROW_TILE, HEAD_DIM), lambda hp, i: (hp, i, 0))
    oshape = jax.ShapeDtypeStruct((NA_HEADS, ZLEN, HEAD_DIM), BF16)
    return pl.pallas_call(
        body, name="qkv_prep", grid=(NA_HEADS // 2, ZLEN // ROW_TILE),
        in_specs=[col(32), col(8), col(16), small, small, tab, tab, _full((128, 128))],
        out_specs=[ospec] * 4, out_shape=[oshape] * 4,
        compiler_params=_params("parallel", "parallel"),
    )(p, p, p, qg2, kg2, cos, sin, ones)


def _bias_expand():
    qc = np.arange(GRID_W)[:, None]
    kc = np.arange(GRID_W)[None, :]
    col_start = np.clip(qc - NA_COLS // 2, 0, GRID_W - NA_COLS)
    in_win = (kc >= col_start) & (kc < col_start + NA_COLS)
    dc = np.clip(kc - qc, -(NA_COLS - 1), NA_COLS - 1) + (NA_COLS - 1)
    e = np.zeros((2 * NA_COLS - 1, GRID_W, GRID_W), np.float32)
    for d in range(2 * NA_COLS - 1):
        e[d] = ((dc == d) & in_win).astype(np.float32)
    pen = np.where(in_win, 0.0, NEG_INF).astype(np.float32)
    return e, pen


def bias_table(rpb2):
    e, pen = _bias_expand()
    n_dr = 2 * NA_ROWS - 1
    ea = np.zeros((31, GRID_W, 128), np.float32)
    ea[:, :, :GRID_W] = e
    eb = np.zeros((31, GRID_W, 128), np.float32)
    eb[:, :, GRID_W:] = e
    pen2 = np.concatenate([pen, pen], 1)
    ea = jnp.asarray(ea.reshape(31, GRID_W * 128))
    eb = jnp.asarray(eb.reshape(31, GRID_W * 128))
    sel_a = np.zeros((BT_LEN, n_dr), np.float32)
    sel_b = np.zeros((BT_LEN, n_dr), np.float32)
    for r in range(BT_LEN):
        dr = r - BT_PAD
        if 0 <= dr < n_dr:
            sel_a[r, dr] = 1.0
        if 0 <= dr + 1 < n_dr:
            sel_b[r, dr + 1] = 1.0
    sel_a, sel_b = jnp.asarray(sel_a), jnp.asarray(sel_b)
    pen2 = jnp.asarray(pen2.reshape(1, GRID_W * 128))
    hi = lax.Precision.HIGHEST

    def body(rpb_ref, sa_ref, sb_ref, ea_ref, eb_ref, pen_ref, o_ref):
        rp = rpb_ref[0]
        ra = jnp.dot(sa_ref[...], rp, preferred_element_type=F32, precision=hi)
        rb = jnp.dot(sb_ref[...], rp, preferred_element_type=F32, precision=hi)
        o_ref[0] = (jnp.dot(ra, ea_ref[...], preferred_element_type=F32, precision=hi)
                    + jnp.dot(rb, eb_ref[...], preferred_element_type=F32, precision=hi) + pen_ref[...])

    tcol = 2048
    out = pl.pallas_call(
        body, name="bias_table", grid=(NA_HEADS, GRID_W * 128 // tcol),
        in_specs=[pl.BlockSpec((1, n_dr, 31), lambda h, j: (h, 0, 0)), _full((BT_LEN, n_dr)), _full((BT_LEN, n_dr)),
                  pl.BlockSpec((31, tcol), lambda h, j: (0, j)), pl.BlockSpec((31, tcol), lambda h, j: (0, j)),
                  pl.BlockSpec((1, tcol), lambda h, j: (0, j))],
        out_specs=pl.BlockSpec((1, BT_LEN, tcol), lambda h, j: (h, 0, j)),
        out_shape=jax.ShapeDtypeStruct((NA_HEADS, BT_LEN, GRID_W * 128), F32),
        compiler_params=_params("parallel", "parallel"),
    )(rpb2, sel_a, sel_b, ea, eb, pen2)
    return out.reshape(NA_HEADS, BT_LEN, GRID_W, 128)


def _attn_scores(j, q_rot, q_pl, kk_ref, hh, bt_ref, s_ref):
    ws = jnp.clip(Q_ROWS * j - 4, 0, GRID_ROWS - KEY_ROWS)
    start = pl.multiple_of(CTX_LEN + ws * GRID_W, 256)
    kw = kk_ref[hh, pl.ds(start, KEY_TILE), :]
    s_ref[:, :KEY_TILE] = _dot_nt(q_rot, kw)
    s_ref[:, KEY_TILE:] = _dot_nt(q_pl, kk_ref[hh, :CTX_LEN, :])
    lane = lax.broadcasted_iota(jnp.int32, (GRID_W, 128), 1)
    base = ws - Q_ROWS * j + (NA_ROWS - 1) + BT_PAD
    for qi in range(Q_ROWS):
        rs = jnp.clip(Q_ROWS * j + qi - NA_ROWS // 2, 0, GRID_ROWS - NA_ROWS)
        for m in range(KEY_ROWS // 2):
            k0 = ws + 2 * m
            p0 = jnp.where((k0 >= rs) & (k0 < rs + NA_ROWS), 0.0, NEG_INF)
            p1 = jnp.where((k0 + 1 >= rs) & (k0 + 1 < rs + NA_ROWS), 0.0, NEG_INF)
            pen = jnp.where(lane < GRID_W, p0, p1)
            rows = slice(qi * GRID_W, (qi + 1) * GRID_W)
            cols = slice(128 * m, 128 * (m + 1))
            s_ref[rows, cols] = s_ref[rows, cols] + bt_ref[hh, base + 2 * m - qi] + pen
    return start, base


def attn_fwd(q_rot, q_pl, kk, vv, bt):
    def body(qr_ref, qp_ref, kk_ref, vv_ref, bt_ref, o_ref, lse_ref, s_ref):
        j = pl.program_id(1)
        outs = []
        for hh in range(2):
            start, _ = _attn_scores(j, qr_ref[hh], qp_ref[hh], kk_ref, hh, bt_ref, s_ref)
            s = s_ref[...]
            mx = jnp.max(s, axis=-1, keepdims=True)
            pr = jnp.exp(s - mx)
            l = jnp.sum(pr, axis=-1, keepdims=True)
            prb = pr.astype(BF16)
            o = jnp.dot(prb[:, :KEY_TILE], vv_ref[hh, pl.ds(start, KEY_TILE), :], preferred_element_type=F32)
            o = o + jnp.dot(prb[:, KEY_TILE:], vv_ref[hh, :CTX_LEN, :], preferred_element_type=F32)
            outs.append(o / l)
            lse_ref[hh] = mx + jnp.log(l)
        o_ref[...] = jnp.concatenate(outs, axis=1)

    qspec = pl.BlockSpec((2, Q_TILE, HEAD_DIM), lambda hp, j: (hp, j + 1, 0))
    kspec = pl.BlockSpec((2, ZLEN, HEAD_DIM), lambda hp, j: (hp, 0, 0))
    return pl.pallas_call(
        body, name="attn_fwd", grid=(NA_HEADS // 2, SEQ // Q_TILE),
        in_specs=[qspec, qspec, kspec, kspec, pl.BlockSpec((2, BT_LEN, GRID_W, 128), lambda hp, j: (hp, 0, 0, 0))],
        out_specs=[pl.BlockSpec((Q_TILE, 128), lambda hp, j: (j, hp)),
                   pl.BlockSpec((2, Q_TILE, 1), lambda hp, j: (hp, j, 0))],
        out_shape=[jax.ShapeDtypeStruct((SEQ, D_MODEL), F32), jax.ShapeDtypeStruct((NA_HEADS, SEQ, 1), F32)],
        scratch_shapes=[pltpu.VMEM((Q_TILE, KEY_TILE + CTX_LEN), F32)],
        compiler_params=_params("parallel", "arbitrary"),
    )(q_rot, q_pl, kk, vv, bt)


def merge_fwd(y_rnn, y_na, p, z, g2, w_rnn, w_na, w_out):
    def body(yr_ref, yn_ref, mr_ref, mn_ref, x_ref, g2_ref, wr_ref, wn_ref, wo_ref, u_ref, v_ref, mg_ref, out_ref, x1_ref):
        u = jnp.dot(yr_ref[...], wr_ref[...], preferred_element_type=F32)
        v = jnp.dot(yn_ref[...].astype(BF16), wn_ref[...], preferred_element_type=F32)
        merged = (_sigmoid(mr_ref[...]) * u + _sigmoid(mn_ref[...]) * v).astype(BF16)
        out = jnp.dot(merged, wo_ref[...], preferred_element_type=F32)
        u_ref[...] = u
        v_ref[...] = v
        mg_ref[...] = merged
        out_ref[...] = out
        x1_ref[...] = x_ref[...] + g2_ref[...] * out

    row = pl.BlockSpec((ROW_TILE, D_MODEL), lambda i: (i, 0))
    lat = lambda cb: pl.BlockSpec((ROW_TILE, D_MODEL), lambda i: (i + 1, cb))
    wspec = _full((D_MODEL, D_MODEL))
    f32o = jax.ShapeDtypeStruct((SEQ, D_MODEL), F32)
    return pl.pallas_call(
        body, name="merge_fwd", grid=(SEQ // ROW_TILE,),
        in_specs=[row, row, lat(5), lat(6), lat(0), _full((1, D_MODEL)), wspec, wspec, wspec],
        out_specs=[row] * 5,
        out_shape=[f32o, f32o, jax.ShapeDtypeStruct((SEQ, D_MODEL), BF16), f32o, f32o],
        compiler_params=_params("parallel"),
    )(y_rnn, y_na, p, p, z, g2, w_rnn, w_na, w_out)


FF_TILE = 256
FF_TILES = D_FF // FF_TILE


def _ffn_conv(h, cw, cb):
    row = _row_ids(SEQ, FF_TILE)
    prev = jnp.where(row >= 1, pltpu.roll(h, 1, 0), 0.0)
    nxt = jnp.where(row < SEQ - 1, pltpu.roll(h, SEQ - 1, 0), 0.0)
    return prev * cw[0:1, :] + h * cw[1:2, :] + nxt * cw[2:3, :] + cb


def ffn_act(hpre, conv_w, conv_b):
    def body(ha_ref, hg_ref, wa_ref, wg_ref, ba_ref, bg_ref, o_ref):
        a = _ffn_conv(ha_ref[...], wa_ref[...], ba_ref[...])
        g = _ffn_conv(hg_ref[...], wg_ref[...], bg_ref[...])
        o_ref[...] = (a * _sigmoid(a) * g).astype(BF16)

    col = lambda rows, off: pl.BlockSpec((rows, FF_TILE), lambda j: (0, j + off))
    return pl.pallas_call(
        body, name="ffn_act", grid=(FF_TILES,),
        in_specs=[col(SEQ, 0), col(SEQ, FF_TILES), col(3, 0), col(3, FF_TILES), col(1, 0), col(1, FF_TILES)],
        out_specs=col(SEQ, 0),
        out_shape=jax.ShapeDtypeStruct((SEQ, D_FF), BF16),
        compiler_params=_params("parallel"),
    )(hpre, hpre, conv_w, conv_w, conv_b, conv_b)


def ffn_down_loss(act, w_down, x1, g5, target):
    def body(a_ref, w_ref, x1_ref, g5_ref, t_ref, f_ref, dy_ref, df_ref, ls_ref, dg_ref):
        i = pl.program_id(0)
        f = jnp.dot(a_ref[...], w_ref[...], preferred_element_type=F32)
        g5 = g5_ref[...]
        err = x1_ref[...] + g5 * f - t_ref[...]
        dy = err * (1.0 / D_MODEL)
        f_ref[...] = f
        dy_ref[...] = dy
        df_ref[...] = (dy * g5).astype(BF16)

        @pl.when(i == 0)
        def _():
            ls_ref[...] = jnp.zeros_like(ls_ref)
            dg_ref[...] = jnp.zeros_like(dg_ref)

        ls_ref[...] = ls_ref[...] + jnp.sum(err * err)
        dg_ref[...] = dg_ref[...] + jnp.sum(dy * f, axis=0, keepdims=True)

    row = pl.BlockSpec((ROW_TILE, D_MODEL), lambda i: (i, 0))
    f32o = jax.ShapeDtypeStruct((SEQ, D_MODEL), F32)
    return pl.pallas_call(
        body, name="ffn_down_loss", grid=(SEQ // ROW_TILE,),
        in_specs=[pl.BlockSpec((ROW_TILE, D_FF), lambda i: (i, 0)), _full((D_FF, D_MODEL)), row, _full((1, D_MODEL)), row],
        out_specs=[row, row, row, _full((8, 128)), _full((1, D_MODEL))],
        out_shape=[f32o, f32o, jax.ShapeDtypeStruct((SEQ, D_MODEL), BF16), jax.ShapeDtypeStruct((8, 128), F32),
                   jax.ShapeDtypeStruct((1, D_MODEL), F32)],
        compiler_params=_params("arbitrary"),
    )(act, w_down, x1, g5, target)


def ffn_down_bwd(df, w_down):
    def body(df_ref, w_ref, o_ref):
        o_ref[...] = _dot_nt(df_ref[...], w_ref[...])

    return pl.pallas_call(
        body, name="ffn_down_bwd", grid=(SEQ // ROW_TILE,),
        in_specs=[pl.BlockSpec((ROW_TILE, D_MODEL), lambda i: (i, 0)), _full((D_FF, D_MODEL))],
        out_specs=pl.BlockSpec((ROW_TILE, D_FF), lambda i: (i, 0)),
        out_shape=jax.ShapeDtypeStruct((SEQ, D_FF), F32),
        compiler_params=_params("parallel"),
    )(df, w_down)


def ffn_act_bwd(hpre, d_act, conv_w, conv_b):
    def body(ha_ref, hg_ref, da_ref, wa_ref, wg_ref, ba_ref, bg_ref, dh_ref, dw_ref, db_ref):
        is_a = pl.program_id(0) == 0
        ha, hg = ha_ref[...], hg_ref[...]
        a = _ffn_conv(ha, wa_ref[...], ba_ref[...])
        g = _ffn_conv(hg, wg_ref[...], bg_ref[...])
        sig = _sigmoid(a)
        dact = da_ref[...]
        d_a = dact * g * (sig * (1.0 + a * (1.0 - sig)))
        d_g = dact * a * sig
        dc = jnp.where(is_a, d_a, d_g)
        h = jnp.where(is_a, ha, hg)
        w = jnp.where(is_a, wa_ref[...], wg_ref[...])
        row = _row_ids(SEQ, FF_TILE)
        h_prev = jnp.where(row >= 1, pltpu.roll(h, 1, 0), 0.0)
        h_next = jnp.where(row < SEQ - 1, pltpu.roll(h, SEQ - 1, 0), 0.0)
        dw_ref[0:1, :] = jnp.sum(dc * h_prev, axis=0, keepdims=True)
        dw_ref[1:2, :] = jnp.sum(dc * h, axis=0, keepdims=True)
        dw_ref[2:3, :] = jnp.sum(dc * h_next, axis=0, keepdims=True)
        db_ref[...] = jnp.sum(dc, axis=0, keepdims=True)
        dc_next = jnp.where(row < SEQ - 1, pltpu.roll(dc, SEQ - 1, 0), 0.0)
        dc_prev = jnp.where(row >= 1, pltpu.roll(dc, 1, 0), 0.0)
        dh_ref[...] = (dc_next * w[0:1, :] + dc * w[1:2, :] + dc_prev * w[2:3, :]).astype(BF16)

    col = lambda rows, off: pl.BlockSpec((rows, FF_TILE), lambda p, j: (0, j + off))
    ocol = lambda rows: pl.BlockSpec((rows, FF_TILE), lambda p, j: (0, p * FF_TILES + j))
    return pl.pallas_call(
        body, name="ffn_act_bwd", grid=(2, FF_TILES),
        in_specs=[col(SEQ, 0), col(SEQ, FF_TILES), col(SEQ, 0), col(3, 0), col(3, FF_TILES), col(1, 0), col(1, FF_TILES)],
        out_specs=[ocol(SEQ), ocol(3), ocol(1)],
        out_shape=[jax.ShapeDtypeStruct((SEQ, 2 * D_FF), BF16), jax.ShapeDtypeStruct((3, 2 * D_FF), F32),
                   jax.ShapeDtypeStruct((1, 2 * D_FF), F32)],
        compiler_params=_params("parallel", "parallel"),
    )(hpre, hpre, d_act, conv_w, conv_w, conv_b, conv_b)


def _norm_mod_bwd(x, dxn, gain, scale):
    rstd = lax.rsqrt(jnp.mean(x * x, axis=-1, keepdims=True) + EPS)
    nrm = x * rstd
    dsh = jnp.sum(dxn, axis=0, keepdims=True)
    dsc = jnp.sum(dxn * nrm, axis=0, keepdims=True) * gain
    dgn = jnp.sum(dxn * nrm, axis=0, keepdims=True) * (1.0 + scale)
    dn = dxn * (gain * (1.0 + scale))
    dx = rstd * (dn - nrm * jnp.mean(dn * nrm, axis=-1, keepdims=True))
    return dx, dsh, dsc, dgn


def ffn_up_bwd(dhpre, w_up, x1, dy, gain, scale):
    def body(dh_ref, w_ref, x_ref, dy_ref, g_ref, sc_ref, dx_ref, dsh_ref, dsc_ref, dgn_ref):
        i = pl.program_id(0)
        dxn = _dot_nt(dh_ref[...], w_ref[...])
        dx, dsh, dsc, dgn = _norm_mod_bwd(x_ref[...], dxn, g_ref[...], sc_ref[...])
        dx_ref[...] = dy_ref[...] + dx

        @pl.when(i == 0)
        def _():
            dsh_ref[...] = dsh
            dsc_ref[...] = dsc
            dgn_ref[...] = dgn

        @pl.when(i > 0)
        def _():
            dsh_ref[...] = dsh_ref[...] + dsh
            dsc_ref[...] = dsc_ref[...] + dsc
            dgn_ref[...] = dgn_ref[...] + dgn

    row = pl.BlockSpec((ROW_TILE, D_MODEL), lambda i: (i, 0))
    vec = _full((1, D_MODEL))
    vshape = jax.ShapeDtypeStruct((1, D_MODEL), F32)
    return pl.pallas_call(
        body, name="ffn_up_bwd", grid=(SEQ // ROW_TILE,),
        in_specs=[pl.BlockSpec((ROW_TILE, 2 * D_FF), lambda i: (i, 0)), _full((D_MODEL, 2 * D_FF)), row, row, vec, vec],
        out_specs=[row, vec, vec, vec],
        out_shape=[jax.ShapeDtypeStruct((SEQ, D_MODEL), F32), vshape, vshape, vshape],
        compiler_params=_params("arbitrary"),
    )(dhpre, w_up, x1, dy, gain, scale)


def merge_bwd(dx1, out, g2, p, u, v, w_rnn, w_na, w_out):
    def body(dx_ref, out_ref, g2_ref, mr_ref, mn_ref, u_ref, v_ref, wr_ref, wn_ref, wo_ref,
             dout_ref, du_ref, dv_ref, dmr_ref, dmn_ref, dyr_ref, dyn_ref, dg2_ref):
        i = pl.program_id(0)

        @pl.when(i == 0)
        def _():
            dmr_ref[...] = jnp.zeros_like(dmr_ref)
            dmn_ref[...] = jnp.zeros_like(dmn_ref)
            dg2_ref[...] = jnp.zeros_like(dg2_ref)

        @pl.when(i > 0)
        def _():
            dx = dx_ref[...]
            dg2_ref[...] = dg2_ref[...] + jnp.sum(dx * out_ref[...], axis=0, keepdims=True)
            dout = (dx * g2_ref[...]).astype(BF16)
            dout_ref[...] = dout
            dm = _dot_nt(dout, wo_ref[...])
            sr = _sigmoid(mr_ref[...])
            sn = _sigmoid(mn_ref[...])
            du = (dm * sr).astype(BF16)
            dv = (dm * sn).astype(BF16)
            du_ref[...] = du
            dv_ref[...] = dv
            dmr_ref[...] = (dm * u_ref[...] * (sr * (1.0 - sr))).astype(BF16)
            dmn_ref[...] = (dm * v_ref[...] * (sn * (1.0 - sn))).astype(BF16)
            dyr_ref[...] = _dot_nt(du, wr_ref[...])
            dyn_ref[...] = _dot_nt(dv, wn_ref[...])

    lat = pl.BlockSpec((ROW_TILE, D_MODEL), lambda i: (jnp.maximum(i - 1, 0), 0))
    zrow = pl.BlockSpec((ROW_TILE, D_MODEL), lambda i: (i, 0))
    pcol = lambda cb: pl.BlockSpec((ROW_TILE, D_MODEL), lambda i: (i, cb))
    wspec = _full((D_MODEL, D_MODEL))
    tb = jax.ShapeDtypeStruct((SEQ, D_MODEL), BF16)
    zb = jax.ShapeDtypeStruct((ZLEN, D_MODEL), BF16)
    tf = jax.ShapeDtypeStruct((SEQ, D_MODEL), F32)
    return pl.pallas_call(
        body, name="merge_bwd", grid=(ZLEN // ROW_TILE,),
        in_specs=[lat, lat, _full((1, D_MODEL)), pcol(5), pcol(6), lat, lat, wspec, wspec, wspec],
        out_specs=[lat, lat, lat, zrow, zrow, lat, lat, _full((1, D_MODEL))],
        out_shape=[tb, tb, tb, zb, zb, tf, tf, jax.ShapeDtypeStruct((1, D_MODEL), F32)],
        compiler_params=_params("arbitrary"),
    )(dx1, out, g2, p, p, u, v, w_rnn, w_na, w_out)


def attn_bwd(q_rot, q_pl, kk, vv, bt, y_na, d_yna, lse):
    def body(qr_ref, qp_ref, kk_ref, vv_ref, bt_ref, o_ref, do_ref, lse_ref,
             dqr_ref, dqp_ref, dk_ref, dv_ref, dbt_ref, s_ref):
        j = pl.program_id(1)

        @pl.when(j == 0)
        def _():
            dk_ref[...] = jnp.zeros_like(dk_ref)
            dv_ref[...] = jnp.zeros_like(dv_ref)
            dbt_ref[...] = jnp.zeros_like(dbt_ref)

        for hh in range(2):
            q_r, q_p = qr_ref[hh], qp_ref[hh]
            start, base = _attn_scores(j, q_r, q_p, kk_ref, hh, bt_ref, s_ref)
            pr = jnp.exp(s_ref[...] - lse_ref[hh])
            sl = slice(hh * HEAD_DIM, (hh + 1) * HEAD_DIM)
            do = do_ref[:, sl]
            delta = jnp.sum(do * o_ref[:, sl], axis=-1, keepdims=True)
            dob = do.astype(BF16)
            win = pl.ds(start, KEY_TILE)
            ds_lat = pr[:, :KEY_TILE] * (_dot_nt(dob, vv_ref[hh, win, :]) - delta)
            ds_ctx = pr[:, KEY_TILE:] * (_dot_nt(dob, vv_ref[hh, :CTX_LEN, :]) - delta)
            for qi in range(Q_ROWS):
                for m in range(KEY_ROWS // 2):
                    idx = base + 2 * m - qi
                    dbt_ref[hh, idx] = dbt_ref[hh, idx] + ds_lat[qi * GRID_W:(qi + 1) * GRID_W, 128 * m:128 * (m + 1)]
            dsb_lat = ds_lat.astype(BF16)
            dsb_ctx = ds_ctx.astype(BF16)
            prb = pr.astype(BF16)
            dqr_ref[hh] = jnp.dot(dsb_lat, kk_ref[hh, win, :], preferred_element_type=F32)
            dqp_ref[hh] = jnp.dot(dsb_ctx, kk_ref[hh, :CTX_LEN, :], preferred_element_type=F32)
            dk_ref[hh, win, :] = dk_ref[hh, win, :] + _dot_tn(dsb_lat, q_r)
            dk_ref[hh, :CTX_LEN, :] = dk_ref[hh, :CTX_LEN, :] + _dot_tn(dsb_ctx, q_p)
            dv_ref[hh, win, :] = dv_ref[hh, win, :] + _dot_tn(prb[:, :KEY_TILE], dob)
            dv_ref[hh, :CTX_LEN, :] = dv_ref[hh, :CTX_LEN, :] + _dot_tn(prb[:, KEY_TILE:], dob)

    qspec = pl.BlockSpec((2, Q_TILE, HEAD_DIM), lambda hp, j: (hp, j + 1, 0))
    kspec = pl.BlockSpec((2, ZLEN, HEAD_DIM), lambda hp, j: (hp, 0, 0))
    btspec = pl.BlockSpec((2, BT_LEN, GRID_W, 128), lambda hp, j: (hp, 0, 0, 0))
    ospec = pl.BlockSpec((Q_TILE, 128), lambda hp, j: (j, hp))
    dqspec = pl.BlockSpec((2, Q_TILE, HEAD_DIM), lambda hp, j: (hp, j, 0))
    dq_shape = jax.ShapeDtypeStruct((NA_HEADS, SEQ, HEAD_DIM), F32)
    dk_shape = jax.ShapeDtypeStruct((NA_HEADS, ZLEN, HEAD_DIM), F32)
    return pl.pallas_call(
        body, name="attn_bwd", grid=(NA_HEADS // 2, SEQ // Q_TILE),
        in_specs=[qspec, qspec, kspec, kspec, btspec, ospec, ospec,
                  pl.BlockSpec((2, Q_TILE, 1), lambda hp, j: (hp, j, 0))],
        out_specs=[dqspec, dqspec, kspec, kspec, btspec],
        out_shape=[dq_shape, dq_shape, dk_shape, dk_shape,
                   jax.ShapeDtypeStruct((NA_HEADS, BT_LEN, GRID_W, 128), F32)],
        scratch_shapes=[pltpu.VMEM((Q_TILE, KEY_TILE + CTX_LEN), F32)],
        compiler_params=_params("parallel", "arbitrary"),
    )(q_rot, q_pl, kk, vv, bt, y_na, d_yna, lse)


def qk_bwd(is_q, d_rot, d_plain, p, gain2, cos, sin, ones):
    scale = HEAD_DIM ** -0.5
    col_base = 32 if is_q else 8
    n_hp, n_i = NA_HEADS // 2, ZLEN // ROW_TILE

    def body(*refs):
        if is_q:
            dr_ref, dp_ref, x_ref, g_ref, cos_ref, sin_ref, ones_ref, dx_ref, dg_ref, acc_ref = refs
        else:
            dr_ref, x_ref, g_ref, cos_ref, sin_ref, ones_ref, dx_ref, dg_ref, acc_ref = refs
        hp, i = pl.program_id(0), pl.program_id(1)
        first = (hp == 0) & (i == 0)

        @pl.when(first)
        def _():
            acc_ref[...] = jnp.zeros_like(acc_ref)

        def work():
            ones_m = ones_ref[...]
            x = x_ref[...]
            xh, rstd = _head_rms(x, ones_m, 1.0)
            d_r = jnp.concatenate([dr_ref[0], dr_ref[1]], axis=1)
            dn = d_r * cos_ref[...] + _rope_partner(d_r * sin_ref[...])
            if is_q:
                dn = (dn + jnp.concatenate([dp_ref[0], dp_ref[1]], axis=1)) * scale
            acc_ref[...] = acc_ref[...] + jnp.sum(dn * xh, axis=0, keepdims=True)
            dxh = dn * g_ref[...]
            seg = jnp.dot(dxh * xh, ones_m, preferred_element_type=F32, precision=lax.Precision.HIGHEST) * (1.0 / HEAD_DIM)
            dx_ref[...] = (rstd * (dxh - xh * seg)).astype(BF16)

        if is_q:
            @pl.when(i == 0)
            def _():
                dx_ref[...] = jnp.zeros_like(dx_ref)

            pl.when(i > 0)(work)
        else:
            work()

        @pl.when((hp == n_hp - 1) & (i == n_i - 1))
        def _():
            dg_ref[...] = acc_ref[:, :HEAD_DIM] + acc_ref[:, HEAD_DIM:]

    if is_q:
        hspec = pl.BlockSpec((2, ROW_TILE, HEAD_DIM), lambda hp, i: (hp, jnp.maximum(i - 1, 0), 0))
        head_in, head_specs = [d_rot, d_plain], [hspec, hspec]
    else:
        head_in, head_specs = [d_rot], [pl.BlockSpec((2, ROW_TILE, HEAD_DIM), lambda hp, i: (hp, i, 0))]
    tab = pl.BlockSpec((ROW_TILE, 128), lambda hp, i: (i, 0))
    return pl.pallas_call(
        body, name="q_bwd" if is_q else "k_bwd", grid=(n_hp, n_i),
        in_specs=head_specs + [pl.BlockSpec((ROW_TILE, 128), lambda hp, i: (i, col_base + hp)),
                               pl.BlockSpec((1, 128), lambda hp, i: (0, 0)), tab, tab, _full((128, 128))],
        out_specs=[pl.BlockSpec((ROW_TILE, 128), lambda hp, i: (i, hp)), _full((1, HEAD_DIM))],
        out_shape=[jax.ShapeDtypeStruct((ZLEN, D_MODEL), BF16), jax.ShapeDtypeStruct((1, HEAD_DIM), F32)],
        scratch_shapes=[pltpu.VMEM((1, 128), F32)],
        compiler_params=_params("arbitrary", "arbitrary"),
    )(*head_in, p, gain2, cos, sin, ones)


def v_bwd(dv):
    def body(d_ref, o_ref):
        o_ref[...] = jnp.concatenate([d_ref[0], d_ref[1]], axis=1).astype(BF16)

    return pl.pallas_call(
        body, name="v_bwd", grid=(NA_HEADS // 2, ZLEN // ROW_TILE),
        in_specs=[pl.BlockSpec((2, ROW_TILE, HEAD_DIM), lambda hp, i: (hp, i, 0))],
        out_specs=pl.BlockSpec((ROW_TILE, 128), lambda hp, i: (i, hp)),
        out_shape=jax.ShapeDtypeStruct((ZLEN, D_MODEL), BF16),
        compiler_params=_params("parallel", "parallel"),
    )(dv)


def rpb_grad(dbt):
    e, _ = _bias_expand()
    n_dr = 2 * NA_ROWS - 1
    ea = np.zeros((31, GRID_W, 128), np.float32)
    ea[:, :, :GRID_W] = e
    eb = np.zeros((31, GRID_W, 128), np.float32)
    eb[:, :, GRID_W:] = e
    eat = jnp.asarray(ea.reshape(31, GRID_W * 128).T.copy())
    ebt = jnp.asarray(eb.reshape(31, GRID_W * 128).T.copy())
    sel_at = np.zeros((n_dr, BT_LEN), np.float32)
    sel_bt = np.zeros((n_dr, BT_LEN), np.float32)
    for r in range(BT_LEN):
        dr = r - BT_PAD
        if 0 <= dr < n_dr:
            sel_at[dr, r] = 1.0
        if 0 <= dr + 1 < n_dr:
            sel_bt[dr + 1, r] = 1.0
    hi = lax.Precision.HIGHEST

    def body(d_ref, sa_ref, sb_ref, ea_ref, eb_ref, o_ref):
        dm = d_ref[0]
        a = jnp.dot(dm, ea_ref[...], preferred_element_type=F32, precision=hi)
        b = jnp.dot(dm, eb_ref[...], preferred_element_type=F32, precision=hi)
        o_ref[0] = (jnp.dot(sa_ref[...], a, preferred_element_type=F32, precision=hi)
                    + jnp.dot(sb_ref[...], b, preferred_element_type=F32, precision=hi))

    wide = GRID_W * 128
    return pl.pallas_call(
        body, name="rpb_grad", grid=(NA_HEADS,),
        in_specs=[pl.BlockSpec((1, BT_LEN, wide), lambda h: (h, 0, 0)), _full((n_dr, BT_LEN)), _full((n_dr, BT_LEN)),
                  _full((wide, 31)), _full((wide, 31))],
        out_specs=pl.BlockSpec((1, n_dr, 31), lambda h: (h, 0, 0)),
        out_shape=jax.ShapeDtypeStruct((NA_HEADS, n_dr, 31), F32),
        compiler_params=_params("parallel"),
    )(dbt.reshape(NA_HEADS, BT_LEN, wide), jnp.asarray(sel_at), jnp.asarray(sel_bt), eat, ebt)


def lru_bwd(p, d_yrnn, conv_w, conv_b, wa, ba, wx, bx, lam):
    blk, wspec = _lru_in_specs()

    def body(xr_ref, gx_ref, dy_ref, cw_ref, cb_ref, wa_ref, ba_ref, wx_ref, bx_ref, lam_ref,
             dxr_ref, dgx_ref, dcw_ref, dcb_ref, dwa_ref, dba_ref, dwx_ref, dbx_ref, dlam_ref,
             a_s, b_s, h_s, l_s, hsum_s, dxc_s, dh_s):
        xr = xr_ref[...]
        cw = cw_ref[...]
        xc = _lru_conv(xr, cw, cb_ref[...])
        xcb = xc.astype(BF16)
        g, dg = _gelu_parts(gx_ref[CTX_LEN:, :])
        dy = dy_ref[...]
        dh_s[:CTX_LEN, :] = jnp.zeros((CTX_LEN, LRU_BLOCK_W), F32)
        dh_s[CTX_LEN:, :] = dy * g
        row = _row_ids(ZLEN, LRU_BLOCK_W)
        zero = jnp.zeros((1, LRU_BLOCK_W), F32)
        for d in (0, 1):
            wab = wa_ref[d, 0].astype(BF16)
            wxb = wx_ref[d, 0].astype(BF16)
            lam_d = lam_ref[d:d + 1, :]
            r, gi, sp, a, sq, b = _lru_gates(xc, xcb, wab, ba_ref[d:d + 1, :], wxb, bx_ref[d:d + 1, :], lam_d)
            a_s[...] = a
            b_s[...] = b
            _lru_scan_dir(d, a_s, b_s, h_s)
            h = h_s[...]
            if d == 0:
                hsum_s[...] = h
                h_prev = jnp.where(row >= 1, pltpu.roll(h, 1, 0), 0.0)
                a_s[...] = pltpu.roll(a, ZLEN - 1, 0)
                _scan_down(a_s, dh_s, l_s, 0, N_CHUNK, zero)
            else:
                hsum_s[...] = hsum_s[...] + h
                h_prev = jnp.where(row == CTX_LEN - 1, 0.0, pltpu.roll(h, ZLEN - 1, 0))
                a_s[...] = pltpu.roll(a, 1, 0)
                c = _scan_up(a_s, dh_s, l_s, CTX_CHUNKS, N_CHUNK, zero)
                _scan_up(a_s, dh_s, l_s, 0, CTX_CHUNKS, c)
            db = l_s[...]
            da = db * h_prev
            dsq = db * gi * xc
            dgi = db * sq * xc
            dxc_d = db * sq * gi
            dla = da * a - dsq * (a * a) / sq
            dr = dla * ((-LRU_C) * sp)
            dsp = jnp.sum(dla * ((-LRU_C) * r), axis=0, keepdims=True)
            dlam_ref[d:d + 1, :] = -dsp * _sigmoid(-lam_d)
            dzr = dr * r * (1.0 - r)
            dzi = dgi * gi * (1.0 - gi)
            dba_ref[d:d + 1, :] = jnp.sum(dzr, axis=0, keepdims=True)
            dbx_ref[d:d + 1, :] = jnp.sum(dzi, axis=0, keepdims=True)
            dzrb = dzr.astype(BF16)
            dzib = dzi.astype(BF16)
            dwa_ref[d, 0] = _dot_tn(xcb, dzrb)
            dwx_ref[d, 0] = _dot_tn(xcb, dzib)
            dxc_d = dxc_d + _dot_nt(dzrb, wab) + _dot_nt(dzib, wxb)
            if d == 0:
                dxc_s[...] = dxc_d
            else:
                dxc_s[...] = dxc_s[...] + dxc_d
        dxc = dxc_s[...]
        dxr_ref[...] = _lru_conv_t(dxc, cw).astype(BF16)
        dcb_ref[...] = jnp.sum(dxc, axis=0, keepdims=True)
        segpos = jnp.where(row < CTX_LEN, row, row - CTX_LEN)
        seglen = jnp.where(row < CTX_LEN, CTX_LEN, SEQ)
        for k in range(4):
            off = k - 2
            if off == 0:
                sh = xr
            else:
                ok = (segpos + off >= 0) & (segpos + off < seglen)
                sh = jnp.where(ok, pltpu.roll(xr, (-off) % ZLEN, 0), 0.0)
            dcw_ref[k:k + 1, :] = jnp.sum(dxc * sh, axis=0, keepdims=True)
        dgx_ref[:CTX_LEN, :] = jnp.zeros((CTX_LEN, LRU_BLOCK_W), BF16)
        dgx_ref[CTX_LEN:, :] = (dy * hsum_s[CTX_LEN:, :] * dg).astype(BF16)

    zs = pltpu.VMEM((ZLEN, LRU_BLOCK_W), F32)
    zb = jax.ShapeDtypeStruct((ZLEN, D_MODEL), BF16)
    v2 = jax.ShapeDtypeStruct((2, D_MODEL), F32)
    w4 = jax.ShapeDtypeStruct((2, LRU_BLOCKS, LRU_BLOCK_W, LRU_BLOCK_W), F32)
    return pl.pallas_call(
        body, name="lru_bwd", grid=(LRU_BLOCKS,),
        in_specs=[blk(ZLEN), pl.BlockSpec((ZLEN, LRU_BLOCK_W), lambda b: (0, 24 + b)), blk(SEQ), blk(4), blk(1),
                  wspec, blk(2), wspec, blk(2), blk(2)],
        out_specs=[blk(ZLEN), blk(ZLEN), blk(4), blk(1), wspec, blk(2), wspec, blk(2), blk(2)],
        out_shape=[zb, zb, jax.ShapeDtypeStruct((4, D_MODEL), F32), jax.ShapeDtypeStruct((1, D_MODEL), F32),
                   w4, v2, w4, v2, v2],
        scratch_shapes=[zs] * 7,
        compiler_params=_params("arbitrary"),
    )(p, p, d_yrnn, conv_w, conv_b, wa, ba, wx, bx, lam)


def in_proj_bwd(dgs, w_in, z, dx1, gain, scale):
    def body(*refs):
        dg_refs = refs[:7]
        w_ref, z_ref, dx1_ref, g_ref, sc_ref, gx_ref, dsh_ref, dsc_ref, dgn_ref = refs[7:]
        i = pl.program_id(0)
        dxn = _dot_nt(dg_refs[0][...], w_ref[:, 0:D_MODEL])
        for g in range(1, 7):
            dxn = dxn + _dot_nt(dg_refs[g][...], w_ref[:, g * D_MODEL:(g + 1) * D_MODEL])
        dx, dsh, dsc, dgn = _norm_mod_bwd(z_ref[...], dxn, g_ref[...], sc_ref[0])

        @pl.when(i <= 1)
        def _():
            dsh_ref[0] = dsh
            dsc_ref[0] = dsc

        @pl.when(i > 1)
        def _():
            dsh_ref[0] = dsh_ref[0] + dsh
            dsc_ref[0] = dsc_ref[0] + dsc

        @pl.when(i == 0)
        def _():
            dgn_ref[...] = dgn

        @pl.when(i > 0)
        def _():
            dgn_ref[...] = dgn_ref[...] + dgn
            gx_ref[...] = dx1_ref[...] + dx

    zrow = pl.BlockSpec((ROW_TILE, D_MODEL), lambda i: (i, 0))
    lat = pl.BlockSpec((ROW_TILE, D_MODEL), lambda i: (jnp.maximum(i - 1, 0), 0))
    mod = pl.BlockSpec((1, 1, D_MODEL), lambda i: (jnp.minimum(i, 1), 0, 0))
    mshape = jax.ShapeDtypeStruct((2, 1, D_MODEL), F32)
    return pl.pallas_call(
        body, name="in_proj_bwd", grid=(ZLEN // ROW_TILE,),
        in_specs=[zrow] * 7 + [_full((D_MODEL, IN_COLS)), zrow, lat, _full((1, D_MODEL)), mod],
        out_specs=[lat, mod, mod, _full((1, D_MODEL))],
        out_shape=[jax.ShapeDtypeStruct((SEQ, D_MODEL), F32), mshape, mshape, jax.ShapeDtypeStruct((1, D_MODEL), F32)],
        compiler_params=_params("arbitrary"),
    )(*dgs, w_in, z, dx1, gain, scale)


def matmul_nn_cols(a, b, prev, col_block, name, tm, tn):
    m, k = a.shape
    n = b.shape[1]
    assert m % tm == 0 and n % tn == 0
    off = col_block * (n // tn)

    def body(a_ref, b_ref, prev_ref, o_ref):
        o_ref[...] = jnp.dot(a_ref[...], b_ref[...], preferred_element_type=F32).astype(BF16)

    return pl.pallas_call(
        body, name=name, grid=(m // tm, n // tn),
        in_specs=[pl.BlockSpec((tm, k), lambda i, j: (i, 0)), pl.BlockSpec((k, tn), lambda i, j: (0, j)),
                  pl.BlockSpec(memory_space=pl.ANY)],
        out_specs=pl.BlockSpec((tm, tn), lambda i, j: (i, j + off)),
        out_shape=jax.ShapeDtypeStruct(prev.shape, BF16),
        input_output_aliases={2: 0},
        compiler_params=_params("parallel", "parallel"),
    )(a, b, prev)


def local_step(z, target, modx, modc, norm_mix_g, norm_ffn_g, w_in, conv_w, conv_b, wa, ba, wx, bx, lam, qg, kg, rpb,
               w_rnn, w_na, w_out, w_up, fconv_w, fconv_b, w_down):
    d = D_MODEL
    mx = [modx[:, k * d:(k + 1) * d] for k in range(N_MOD)]
    shift = jnp.stack([modc[:, 0:d], mx[0]])
    scale = jnp.stack([modc[:, d:2 * d], mx[1]])
    cos, sin = _rope_tables()
    ones = _head_ones()
    qg2 = jnp.tile(qg, (1, 2))
    kg2 = jnp.tile(kg, (1, 2))

    xn = norm_mod(z, norm_mix_g, shift, scale, "norm_mix")
    p = matmul_wide(xn, w_in, "in_proj", ROW_TILE, 1792)
    y_rnn = lru_fwd(p, conv_w, conv_b, wa, ba, wx, bx, lam)
    q_rot, q_pl, kk, vv = qkv_prep(p, qg2, kg2, cos, sin, ones)
    bt = bias_table(rpb)
    y_na, lse = attn_fwd(q_rot, q_pl, kk, vv, bt)
    u, v, merged, out, x1 = merge_fwd(y_rnn, y_na, p, z, mx[2], w_rnn, w_na, w_out)
    xn2 = norm_mod(x1, norm_ffn_g, mx[3][None], mx[4][None], "norm_ffn")
    hpre = matmul_wide(xn2, w_up, "ffn_up", ROW_TILE, 1408)
    act = ffn_act(hpre, fconv_w, fconv_b)
    f, dy, df, loss_sq, dg5 = ffn_down_loss(act, w_down, x1, mx[5], target)

    d_act = ffn_down_bwd(df, w_down)
    dhpre, d_fcw, d_fcb = ffn_act_bwd(hpre, d_act, fconv_w, fconv_b)
    dx1, d_s3, d_s4, d_gffn = ffn_up_bwd(dhpre, w_up, x1, dy, norm_ffn_g, mx[4])
    g_w_down = matmul_nn(transpose_bf16(act, "act_t"), df, "gw_down", 256, 512)
    g_w_up = matmul_nn(transpose_bf16(xn2, "xn2_t"), dhpre, "gw_up", 512, 1408)
    dout, du, dv, dmr, dmn, dyr, dyn, dg2 = merge_bwd(dx1, out, mx[2], p, u, v, w_rnn, w_na, w_out)
    g_w_out = matmul_nn(transpose_bf16(merged, "merged_t"), dout, "gw_out", 512, 512)
    g_w_rnn = matmul_nn(transpose_bf16(y_rnn, "yrnn_t"), du, "gw_rnn", 512, 512)
    g_w_na = matmul_nn(transpose_bf16(y_na, "yna_t"), dv, "gw_na", 512, 512)
    dqr, dqp, dk, dvh, dbt = attn_bwd(q_rot, q_pl, kk, vv, bt, y_na, dyn, lse)
    dq_cols, d_qg = qk_bwd(True, dqr, dqp, p, qg2, cos, sin, ones)
    dk_cols, d_kg = qk_bwd(False, dk, None, p, kg2, cos, sin, ones)
    dv_cols = v_bwd(dvh)
    d_rpb = rpb_grad(dbt)
    dxr, dgx, d_cw, d_cb, d_wa, d_ba, d_wx, d_bx, d_lam = lru_bwd(p, dyr, conv_w, conv_b, wa, ba, wx, bx, lam)
    dgs = [dxr, dk_cols, dv_cols, dgx, dq_cols, dmr, dmn]
    grad_x, dsh, dsc, d_gmix = in_proj_bwd(dgs, w_in, z, dx1, norm_mix_g, scale)
    xn_t = transpose_bf16(xn, "xn_t")
    g_w_in = jnp.zeros((d, IN_COLS), BF16)
    for g in range(7):
        g_w_in = matmul_nn_cols(xn_t, dgs[g], g_w_in, g, "gw_in_%d" % g, 512, 512)

    d_modx = jnp.concatenate([dsh[1], dsc[1], dg2, d_s3, d_s4, dg5], axis=1)
    d_modc = jnp.concatenate([dsh[0], dsc[0]], axis=1)
    return dict(loss_sq=loss_sq, grad_x=grad_x, d_modx=d_modx, d_modc=d_modc, norm_mix_g=d_gmix, norm_ffn_g=d_gffn,
                w_in=g_w_in, lru_conv_w=d_cw, lru_conv_b=d_cb, lru_wa=d_wa, lru_ba=d_ba, lru_wx=d_wx, lru_bx=d_bx,
                lru_lambda=d_lam, q_norm_g=d_qg, k_norm_g=d_kg, na_rpb=d_rpb, w_rnn_out=g_w_rnn, w_na_out=g_w_na,
                w_out=g_w_out, w_up=g_w_up, ffn_conv_w=d_fcw, ffn_conv_b=d_fcb, w_down=g_w_down)


def _mesh_pos():
    return lax.axis_index("x"), lax.axis_index("y"), lax.axis_index("c")


def _other_chips(x, y):
    return [(1 - x, y), (x, 1 - y), (1 - x, 1 - y)]


def all_gather8(xs, name, with_sum=False):
    m, n = xs.shape
    assert m % 8 == 0

    def body(x_ref, out_ref, *rest):
        if with_sum:
            sum_ref, send_sems, recv_sems, local_sem = rest
        else:
            send_sems, recv_sems, local_sem = rest
        x, y, c = _mesh_pos()
        me, sibling = (x, y, c), (x, y, 1 - c)
        chips = _other_chips(x, y)

        def rows(px, py, pc):
            return out_ref.at[pl.ds((4 * px + 2 * py + pc) * m, m), :]

        def copy(k, block, to, src=None):
            return pltpu.make_async_remote_copy(
                src_ref=rows(*block) if src is None else src, dst_ref=rows(*block),
                send_sem=send_sems.at[k], recv_sem=recv_sems.at[k], device_id=to, device_id_type=MESH_T)

        mine = pltpu.make_async_copy(x_ref, rows(*me), local_sem)
        mine.start()
        first = [copy(0, me, sibling, src=x_ref)]
        first += [copy(1 + j, me, (*chip, c), src=x_ref) for j, chip in enumerate(chips)]
        for cp in first:
            cp.start()
        passed = [copy(4 + j, (*chip, c), sibling) for j, chip in enumerate(chips)]
        for j, chip in enumerate(chips):
            copy(1 + j, (*chip, c), me).wait_recv()
            passed[j].start()
        copy(0, sibling, me).wait_recv()
        for j, chip in enumerate(chips):
            copy(4 + j, (*chip, 1 - c), me).wait_recv()
        for cp in first + passed:
            cp.wait_send()
        mine.wait()
        if with_sum:
            acc = out_ref[0:m, :]
            for k in range(1, N_DEV):
                acc = acc + out_ref[k * m:(k + 1) * m, :]
            sum_ref[...] = acc

    vm = pl.BlockSpec(memory_space=pltpu.VMEM)
    out_shape = [jax.ShapeDtypeStruct((N_DEV * m, n), F32)]
    if with_sum:
        out_shape.append(jax.ShapeDtypeStruct((m, n), F32))
    res = pl.pallas_call(
        body, name=name, in_specs=[vm], out_specs=[vm] * len(out_shape), out_shape=out_shape,
        scratch_shapes=[pltpu.SemaphoreType.DMA((7,)), pltpu.SemaphoreType.DMA((7,)), pltpu.SemaphoreType.DMA],
        compiler_params=pltpu.CompilerParams(vmem_limit_bytes=VMEM_LIMIT_V7X),
    )(xs)
    return res if with_sum else res[0]


BIG = (("w_in", (D_MODEL, IN_COLS), 1), ("w_rnn_out", (D_MODEL, D_MODEL), 0), ("w_na_out", (D_MODEL, D_MODEL), 0),
       ("w_out", (D_MODEL, D_MODEL), 0), ("w_up", (D_MODEL, 2 * D_FF), 1), ("w_down", (D_FF, D_MODEL), 0))


def _shard_shape(full, axis):
    r, c = full
    return (r // N_SHARD, c) if axis == 0 else (r, c // N_SHARD)


def _slot(ref, full, axis, s, h):
    r, c = full
    if axis == 0:
        rs = r // N_SHARD
        return ref.at[pl.ds(s * rs + h * (rs // 2), rs // 2), :]
    cs = c // N_SHARD
    return ref.at[pl.ds(h * (r // 2), r // 2), pl.ds(s * cs, cs)]


def _half(ref, full, axis, h):
    rs = _shard_shape(full, axis)[0]
    return ref.at[pl.ds(h * (rs // 2), rs // 2), :]


def all_gather_weights(shards):
    nw = len(BIG)

    def body(*refs):
        srcs, outs = refs[:nw], refs[nw:2 * nw]
        send1, recv1, send2, recv2, local_sems = refs[2 * nw:]
        x, y, c = _mesh_pos()
        sibling = (x, y, 1 - c)
        chips = _other_chips(x, y)
        s_me = 2 * x + y

        def shard_of(chip):
            return 2 * chip[0] + chip[1]

        locals_ = []
        for w, (_, full, axis) in enumerate(BIG):
            for h in range(2):
                cp = pltpu.make_async_copy(_half(srcs[w], full, axis, h), _slot(outs[w], full, axis, s_me, h),
                                           local_sems.at[2 * w + h])
                cp.start()
                locals_.append(cp)

        def ici(w, j, shard, src=None):
            _, full, axis = BIG[w]
            dst = _slot(outs[w], full, axis, shard, c)
            return pltpu.make_async_remote_copy(
                src_ref=dst if src is None else src, dst_ref=dst, send_sem=send1.at[3 * w + j],
                recv_sem=recv1.at[3 * w + j], device_id=(*chips[j], c), device_id_type=MESH_T)

        def d2d(w, j, shard, half):
            _, full, axis = BIG[w]
            dst = _slot(outs[w], full, axis, shard, half)
            return pltpu.make_async_remote_copy(
                src_ref=dst, dst_ref=dst, send_sem=send2.at[3 * w + j], recv_sem=recv2.at[3 * w + j],
                device_id=sibling, device_id_type=MESH_T)

        first = []
        for w, (_, full, axis) in enumerate(BIG):
            for j in range(3):
                cp = ici(w, j, s_me, src=_half(srcs[w], full, axis, c))
                cp.start()
                first.append(cp)
        passed = []
        for w in range(nw):
            for j in range(3):
                sh = shard_of(chips[j])
                ici(w, j, sh).wait_recv()
                cp = d2d(w, j, sh, c)
                cp.start()
                passed.append(cp)
        for w in range(nw):
            for j in range(3):
                d2d(w, j, shard_of(chips[j]), 1 - c).wait_recv()
        for cp in first + passed:
            cp.wait_send()
        for cp in locals_:
            cp.wait()

    hbm = pl.BlockSpec(memory_space=pl.ANY)
    return pl.pallas_call(
        body, name="all_gather_weights", in_specs=[hbm] * nw, out_specs=[hbm] * nw,
        out_shape=[jax.ShapeDtypeStruct(full, BF16) for _, full, _ in BIG],
        scratch_shapes=[pltpu.SemaphoreType.DMA((3 * nw,))] * 4 + [pltpu.SemaphoreType.DMA((2 * nw,))],
        compiler_params=pltpu.CompilerParams(vmem_limit_bytes=VMEM_LIMIT_V7X),
    )(*shards)


def _grad_view(g, full, axis):
    r, c = full
    if axis == 0:
        return g.reshape(N_SHARD, 2, r // N_SHARD // 2, c)
    return g.reshape(1, 2, r // 2, c)


def exchange_halves(gviews):
    nw = len(BIG)

    def body(*refs):
        srcs, outs = refs[:nw], refs[nw:2 * nw]
        send_sems, recv_sems = refs[2 * nw:]
        x, y, c = _mesh_pos()
        cps = []
        for w in range(nw):
            cp = pltpu.make_async_remote_copy(
                src_ref=srcs[w].at[:, pl.ds(1 - c, 1)], dst_ref=outs[w], send_sem=send_sems.at[w],
                recv_sem=recv_sems.at[w], device_id=(x, y, 1 - c), device_id_type=MESH_T)
            cp.start()
            cps.append(cp)
        for cp in cps:
            cp.wait()

    hbm = pl.BlockSpec(memory_space=pl.ANY)
    return pl.pallas_call(
        body, name="grad_exchange_halves", in_specs=[hbm] * nw, out_specs=[hbm] * nw,
        out_shape=[jax.ShapeDtypeStruct((g.shape[0], 1) + g.shape[2:], BF16) for g in gviews],
        scratch_shapes=[pltpu.SemaphoreType.DMA((nw,)), pltpu.SemaphoreType.DMA((nw,))],
        compiler_params=pltpu.CompilerParams(vmem_limit_bytes=VMEM_LIMIT_V7X),
    )(*gviews)


def _row_tile(rh):
    return 128 if rh % 128 == 0 else rh


def add_halves(gview, recv, c_idx, name):
    a, _, rh, cc = gview.shape
    tr = _row_tile(rh)

    def body(c_ref, g_ref, r_ref, o_ref):
        o_ref[0] = (g_ref[0, 0].astype(F32) + r_ref[0, 0].astype(F32)).astype(BF16)

    return pl.pallas_call(
        body, name=name,
        grid_spec=pltpu.PrefetchScalarGridSpec(
            num_scalar_prefetch=1, grid=(a, rh // tr),
            in_specs=[pl.BlockSpec((1, 1, tr, cc), lambda s, i, c_ref: (s, c_ref[0], i, 0)),
                      pl.BlockSpec((1, 1, tr, cc), lambda s, i, c_ref: (s, 0, i, 0))],
            out_specs=pl.BlockSpec((1, tr, cc), lambda s, i, c_ref: (s, i, 0))),
        out_shape=jax.ShapeDtypeStruct((a, rh, cc), BF16),
        compiler_params=_params("parallel", "parallel"),
    )(c_idx, gview, recv)


def _piece_shape(full, axis):
    rs, cs = _shard_shape(full, axis)
    return (rs // 2, cs)


def scatter_pieces(partials):
    nw = len(BIG)

    def body(*refs):
        srcs, outs = refs[:nw], refs[nw:2 * nw]
        send_sems, recv_sems = refs[2 * nw:]
        x, y, c = _mesh_pos()
        chips = _other_chips(x, y)
        cps = []
        for w, (_, full, axis) in enumerate(BIG):
            cs = full[1] // N_SHARD
            for j, chip in enumerate(chips):
                s_j = 2 * chip[0] + chip[1]
                src = srcs[w].at[s_j] if axis == 0 else srcs[w].at[0, :, pl.ds(s_j * cs, cs)]
                cp = pltpu.make_async_remote_copy(
                    src_ref=src, dst_ref=outs[w].at[j], send_sem=send_sems.at[3 * w + j],
                    recv_sem=recv_sems.at[3 * w + j], device_id=(*chip, c), device_id_type=MESH_T)
                cp.start()
                cps.append(cp)
        for cp in cps:
            cp.wait()

    hbm = pl.BlockSpec(memory_space=pl.ANY)
    return pl.pallas_call(
        body, name="grad_scatter_pieces", in_specs=[hbm] * nw, out_specs=[hbm] * nw,
        out_shape=[jax.ShapeDtypeStruct((3,) + _piece_shape(full, axis), BF16) for _, full, axis in BIG],
        scratch_shapes=[pltpu.SemaphoreType.DMA((3 * nw,)), pltpu.SemaphoreType.DMA((3 * nw,))],
        compiler_params=pltpu.CompilerParams(vmem_limit_bytes=VMEM_LIMIT_V7X),
    )(*partials)


def add_pieces(partial, recv, s_idx, axis, name):
    _, rh, cs = recv.shape
    tr = _row_tile(rh)

    def body(s_ref, p_ref, r_ref, o_ref):
        o_ref[...] = ((p_ref[0].astype(F32) + r_ref[0].astype(F32)) + r_ref[1].astype(F32)) + r_ref[2].astype(F32)

    if axis == 0:
        pspec = pl.BlockSpec((1, tr, cs), lambda i, s_ref: (s_ref[0], i, 0))
    else:
        pspec = pl.BlockSpec((1, tr, cs), lambda i, s_ref: (0, i, s_ref[0]))
    return pl.pallas_call(
        body, name=name,
        grid_spec=pltpu.PrefetchScalarGridSpec(
            num_scalar_prefetch=1, grid=(rh // tr,),
            in_specs=[pspec, pl.BlockSpec((3, tr, cs), lambda i, s_ref: (0, i, 0))],
            out_specs=pl.BlockSpec((tr, cs), lambda i, s_ref: (i, 0))),
        out_shape=jax.ShapeDtypeStruct((rh, cs), F32),
        compiler_params=_params("parallel"),
    )(s_idx, partial, recv)


def join_halves(pieces):
    nw = len(BIG)

    def body(*refs):
        srcs, outs = refs[:nw], refs[nw:2 * nw]
        send_sems, recv_sems, local_sems = refs[2 * nw:]
        x, y, c = _mesh_pos()
        cps, lcs = [], []
        for w in range(nw):
            lc = pltpu.make_async_copy(srcs[w], outs[w].at[c], local_sems.at[w])
            lc.start()
            lcs.append(lc)
            cp = pltpu.make_async_remote_copy(
                src_ref=srcs[w], dst_ref=outs[w].at[c], send_sem=send_sems.at[w], recv_sem=recv_sems.at[w],
                device_id=(x, y, 1 - c), device_id_type=MESH_T)
            cp.start()
            cps.append(cp)
        for cp in cps:
            cp.wait()
        for lc in lcs:
            lc.wait()

    hbm = pl.BlockSpec(memory_space=pl.ANY)
    return pl.pallas_call(
        body, name="grad_join_halves", in_specs=[hbm] * nw, out_specs=[hbm] * nw,
        out_shape=[jax.ShapeDtypeStruct((2,) + p.shape, F32) for p in pieces],
        scratch_shapes=[pltpu.SemaphoreType.DMA((nw,))] * 3,
        compiler_params=pltpu.CompilerParams(vmem_limit_bytes=VMEM_LIMIT_V7X),
    )(*pieces)


MOD_COLS = N_MOD * D_MODEL // N_SHARD
MOD_TILE = 512


def mod_fwd(c16, w_mod):
    def body(c_ref, w_ref, s_ref, o_ref):
        cv = c_ref[...]
        s = cv * _sigmoid(cv)
        s_ref[...] = s
        o_ref[...] = jnp.dot(s.astype(BF16), w_ref[...].astype(BF16), preferred_element_type=F32)

    return pl.pallas_call(
        body, name="mod_fwd", grid=(MOD_COLS // MOD_TILE,),
        in_specs=[_full((16, D_MODEL)), pl.BlockSpec((D_MODEL, MOD_TILE), lambda j: (0, j))],
        out_specs=[_full((16, D_MODEL)), pl.BlockSpec((16, MOD_TILE), lambda j: (0, j))],
        out_shape=[jax.ShapeDtypeStruct((16, D_MODEL), F32), jax.ShapeDtypeStruct((16, MOD_COLS), F32)],
        compiler_params=_params("arbitrary"),
    )(c16, w_mod)


def mod_bwd(s16, dm16, w_mod):
    hi = lax.Precision.HIGHEST

    def body(s_ref, d_ref, w_ref, gw_ref, ds_ref):
        j = pl.program_id(0)
        dm = d_ref[...]
        gw_ref[...] = lax.dot_general(s_ref[...], dm, (((0,), (0,)), ((), ())), preferred_element_type=F32, precision=hi)
        part = lax.dot_general(dm, w_ref[...], (((1,), (1,)), ((), ())), preferred_element_type=F32, precision=hi)

        @pl.when(j == 0)
        def _():
            ds_ref[...] = part

        @pl.when(j > 0)
        def _():
            ds_ref[...] = ds_ref[...] + part

    return pl.pallas_call(
        body, name="mod_bwd", grid=(MOD_COLS // MOD_TILE,),
        in_specs=[_full((16, D_MODEL)), pl.BlockSpec((16, MOD_TILE), lambda j: (0, j)),
                  pl.BlockSpec((D_MODEL, MOD_TILE), lambda j: (0, j))],
        out_specs=[pl.BlockSpec((D_MODEL, MOD_TILE), lambda j: (0, j)), _full((16, D_MODEL))],
        out_shape=[jax.ShapeDtypeStruct((D_MODEL, MOD_COLS), F32), jax.ShapeDtypeStruct((16, D_MODEL), F32)],
        compiler_params=_params("arbitrary"),
    )(s16, dm16, w_mod)


def cctx_grad(parts, c_ctx):
    def body(p_ref, c_ref, o_ref):
        ds = p_ref[0:1, :]
        for s in range(1, N_SHARD):
            ds = ds + p_ref[16 * s:16 * s + 1, :]
        cv = c_ref[...]
        sg = _sigmoid(cv)
        o_ref[...] = ds * (sg * (1.0 + cv * (1.0 - sg)))

    return pl.pallas_call(
        body, name="cctx_grad", in_specs=[_full((N_DEV * 8, D_MODEL)), _full((1, D_MODEL))],
        out_specs=_full((1, D_MODEL)), out_shape=jax.ShapeDtypeStruct((1, D_MODEL), F32),
    )(parts, c_ctx)


def add_rows(a, b, name):
    def body(a_ref, b_ref, o_ref):
        o_ref[...] = a_ref[...] + b_ref[...]

    return pl.pallas_call(body, name=name, in_specs=[_full(a.shape), _full(b.shape)], out_specs=_full(a.shape),
                          out_shape=jax.ShapeDtypeStruct(a.shape, F32))(a, b)


def adamw(w, g, m, v, name):
    r, c = w.shape
    tr = 128 if (r % 128 == 0 and r > 128) else r

    def body(w_ref, g_ref, m_ref, v_ref, d_ref, nm_ref, nv_ref):
        g_ = g_ref[...]
        m_ = ADAM_B1 * m_ref[...] + (1.0 - ADAM_B1) * g_
        v_ = ADAM_B2 * v_ref[...] + (1.0 - ADAM_B2) * (g_ * g_)
        m_hat = m_ / (1.0 - ADAM_B1 ** ADAM_STEP)
        v_hat = v_ / (1.0 - ADAM_B2 ** ADAM_STEP)
        d_ref[...] = -ADAM_LR * (m_hat / (jnp.sqrt(v_hat) + ADAM_EPS) + ADAM_WD * w_ref[...])
        nm_ref[...] = m_
        nv_ref[...] = v_

    spec = pl.BlockSpec((tr, c), lambda i: (i, 0))
    shp = jax.ShapeDtypeStruct((r, c), F32)
    return pl.pallas_call(
        body, name=name, grid=(r // tr,), in_specs=[spec] * 4, out_specs=[spec] * 3, out_shape=[shp] * 3,
        compiler_params=_params("parallel"),
    )(w, g, m, v)


LANES = 1024


def _pack(arrs):
    rows, spans, at = [], [], 0
    for a in arrs:
        n = int(np.prod(a.shape))
        nr = 8 * -(-n // (8 * LANES))
        flat = a.reshape(-1)
        if nr * LANES != n:
            flat = jnp.concatenate([flat, jnp.zeros((nr * LANES - n,), F32)])
        rows.append(flat.reshape(nr, LANES))
        spans.append((at, nr, n, a.shape))
        at += nr
    return jnp.concatenate(rows, axis=0), spans


def _unpack(buf, spans):
    out = []
    for at, nr, n, shape in spans:
        out.append(buf[at:at + nr].reshape(-1)[:n].reshape(shape))
    return out


SMALL_SHARD = ("lru_conv_w", "lru_ba", "lru_bx", "lru_lambda", "ffn_conv_w")


def kernel(x, c, ctx, c_ctx, w_mod, b_mod, norm_mix_g, norm_ffn_g, w_in, lru_conv_w, lru_conv_b, lru_wa, lru_ba, lru_wx, lru_bx, lru_lambda, q_norm_g, k_norm_g, na_rpb, w_rnn_out, w_na_out, w_out, w_up, ffn_conv_w, ffn_conv_b, w_down, loss_target, m_c_ctx, m_w_mod, m_b_mod, m_norm_mix_g, m_norm_ffn_g, m_w_in, m_lru_conv_w, m_lru_conv_b, m_lru_wa, m_lru_ba, m_lru_wx, m_lru_bx, m_lru_lambda, m_q_norm_g, m_k_norm_g, m_na_rpb, m_w_rnn_out, m_w_na_out, m_w_out, m_w_up, m_ffn_conv_w, m_ffn_conv_b, m_w_down, v_c_ctx, v_w_mod, v_b_mod, v_norm_mix_g, v_norm_ffn_g, v_w_in, v_lru_conv_w, v_lru_conv_b, v_lru_wa, v_lru_ba, v_lru_wx, v_lru_bx, v_lru_lambda, v_q_norm_g, v_k_norm_g, v_na_rpb, v_w_rnn_out, v_w_na_out, v_w_out, v_w_up, v_ffn_conv_w, v_ffn_conv_b, v_w_down):
    weights = dict(c_ctx=c_ctx, w_mod=w_mod, b_mod=b_mod, norm_mix_g=norm_mix_g, norm_ffn_g=norm_ffn_g, w_in=w_in,
                   lru_conv_w=lru_conv_w, lru_conv_b=lru_conv_b, lru_wa=lru_wa, lru_ba=lru_ba, lru_wx=lru_wx,
                   lru_bx=lru_bx, lru_lambda=lru_lambda, q_norm_g=q_norm_g, k_norm_g=k_norm_g, na_rpb=na_rpb,
                   w_rnn_out=w_rnn_out, w_na_out=w_na_out, w_out=w_out, w_up=w_up, ffn_conv_w=ffn_conv_w,
                   ffn_conv_b=ffn_conv_b, w_down=w_down)
    mom1 = dict(c_ctx=m_c_ctx, w_mod=m_w_mod, b_mod=m_b_mod, norm_mix_g=m_norm_mix_g, norm_ffn_g=m_norm_ffn_g,
                w_in=m_w_in, lru_conv_w=m_lru_conv_w, lru_conv_b=m_lru_conv_b, lru_wa=m_lru_wa, lru_ba=m_lru_ba,
                lru_wx=m_lru_wx, lru_bx=m_lru_bx, lru_lambda=m_lru_lambda, q_norm_g=m_q_norm_g, k_norm_g=m_k_norm_g,
                na_rpb=m_na_rpb, w_rnn_out=m_w_rnn_out, w_na_out=m_w_na_out, w_out=m_w_out, w_up=m_w_up,
                ffn_conv_w=m_ffn_conv_w, ffn_conv_b=m_ffn_conv_b, w_down=m_w_down)
    mom2 = dict(c_ctx=v_c_ctx, w_mod=v_w_mod, b_mod=v_b_mod, norm_mix_g=v_norm_mix_g, norm_ffn_g=v_norm_ffn_g,
                w_in=v_w_in, lru_conv_w=v_lru_conv_w, lru_conv_b=v_lru_conv_b, lru_wa=v_lru_wa, lru_ba=v_lru_ba,
                lru_wx=v_lru_wx, lru_bx=v_lru_bx, lru_lambda=v_lru_lambda, q_norm_g=v_q_norm_g, k_norm_g=v_k_norm_g,
                na_rpb=v_na_rpb, w_rnn_out=v_w_rnn_out, w_na_out=v_w_na_out, w_out=v_w_out, w_up=v_w_up,
                ffn_conv_w=v_ffn_conv_w, ffn_conv_b=v_ffn_conv_b, w_down=v_w_down)
    order = list(weights)
    d = D_MODEL
    mx_, my_, mc_ = _mesh_pos()
    shard = 2 * mx_ + my_
    dev = 2 * shard + mc_

    local_small, small_spans = _pack([c] + [weights[k][0] for k in SMALL_SHARD])
    gath = all_gather8(local_small, "gather_small").reshape(N_DEV, local_small.shape[0], LANES)
    per_dev = [_unpack(gath[k], small_spans) for k in range(N_DEV)]
    c_all = jnp.concatenate([per_dev[k][0] for k in range(N_DEV)], axis=0)
    full_small = {name: jnp.concatenate([per_dev[2 * s][1 + i] for s in range(N_SHARD)], axis=-1)
                  for i, name in enumerate(SMALL_SHARD)}
    c16 = jnp.concatenate([c_all, c_ctx.reshape(1, d), jnp.zeros((7, d), F32)], axis=0)
    s16, mod_part = mod_fwd(c16, w_mod[0])
    mod_all = all_gather8(mod_part, "gather_mod").reshape(N_DEV, 16, MOD_COLS)
    mod = jnp.concatenate([mod_all[2 * s] for s in range(N_SHARD)], axis=1) + b_mod
    modx = lax.dynamic_slice(mod, (dev, 0), (1, N_MOD * d))
    modc = mod[8:9]

    big_full = all_gather_weights([cast_bf16(weights[name][0], "cast_" + name) for name, _, _ in BIG])
    wfull = {name: big_full[i] for i, (name, _, _) in enumerate(BIG)}

    z = jnp.concatenate([ctx[0], x[0]], axis=0)
    res = local_step(z, loss_target[0], modx, modc, norm_mix_g, norm_ffn_g, wfull["w_in"], full_small["lru_conv_w"],
                     lru_conv_b, lru_wa[0], full_small["lru_ba"], lru_wx[0], full_small["lru_bx"],
                     full_small["lru_lambda"], q_norm_g, k_norm_g, na_rpb[0], wfull["w_rnn_out"], wfull["w_na_out"],
                     wfull["w_out"], wfull["w_up"], full_small["ffn_conv_w"], ffn_conv_b, wfull["w_down"])

    c_idx = jnp.reshape(mc_, (1,)).astype(jnp.int32)
    s_idx = jnp.reshape(shard, (1,)).astype(jnp.int32)
    gviews = [_grad_view(res[name], full, axis) for name, full, axis in BIG]
    recv1 = exchange_halves(gviews)
    partials = [add_halves(gviews[i], recv1[i], c_idx, "add_halves_" + BIG[i][0]) for i in range(len(BIG))]
    recv2 = scatter_pieces(partials)
    pieces = [add_pieces(partials[i], recv2[i], s_idx, BIG[i][2], "add_pieces_" + BIG[i][0]) for i in range(len(BIG))]
    joined = join_halves(pieces)
    grads = {name: joined[i].reshape(_shard_shape(full, axis)) for i, (name, full, axis) in enumerate(BIG)}

    small_names = ["norm_mix_g", "norm_ffn_g", "lru_conv_w", "lru_conv_b", "lru_wa", "lru_ba", "lru_wx", "lru_bx",
                   "lru_lambda", "q_norm_g", "k_norm_g", "na_rpb", "ffn_conv_w", "ffn_conv_b"]
    local_g, g_spans = _pack([res["loss_sq"][0:1, 0:1], res["d_modx"], res["d_modc"]] + [res[k] for k in small_names])
    n_rows = local_g.shape[0]
    g_all, g_tot = all_gather8(local_g, "allreduce_small", with_sum=True)
    tot = _unpack(g_tot, g_spans)
    loss = (0.5 / d) * tot[0][0, 0]
    small_tot = dict(zip(small_names, tot[3:]))
    at_x = g_spans[1][0]
    dmx_rows = g_all.reshape(N_DEV, n_rows, LANES)[:, at_x:at_x + N_MOD, :].reshape(N_DEV, N_MOD * d)
    dmc_row = jnp.concatenate([tot[2], jnp.zeros((1, 4 * d), F32)], axis=1)
    dm16 = jnp.concatenate([dmx_rows, dmc_row, jnp.zeros((7, N_MOD * d), F32)], axis=0)
    grads["b_mod"] = add_rows(tot[1], dmc_row, "b_mod_grad")
    g_w_mod, ds16 = mod_bwd(s16, lax.dynamic_slice(dm16, (0, shard * MOD_COLS), (16, MOD_COLS)), w_mod[0])
    grads["w_mod"] = g_w_mod
    ds_parts = all_gather8(ds16[8:16], "gather_dsctx")
    grads["c_ctx"] = cctx_grad(ds_parts, c_ctx.reshape(1, d))
    for k in small_names:
        g = small_tot[k]
        if k in SMALL_SHARD:
            w_sh = weights[k].shape[-1]
            g = lax.dynamic_slice_in_dim(g, shard * w_sh, w_sh, axis=g.ndim - 1)
        grads[k] = g

    delta, new_m, new_v = {}, {}, {}
    for name, _, _ in BIG + (("w_mod", None, None),):
        delta[name], new_m[name], new_v[name] = adamw(weights[name][0], grads[name], mom1[name][0], mom2[name][0],
                                                      "adamw_" + name)
    rest = [k for k in order if k not in delta]
    pw, spans_w = _pack([weights[k] for k in rest])
    pg, _ = _pack([grads[k].reshape(weights[k].shape) for k in rest])
    pm, _ = _pack([mom1[k] for k in rest])
    pv, _ = _pack([mom2[k] for k in rest])
    pd, pnm, pnv = adamw(pw, pg, pm, pv, "adamw_small")
    for k, a, b_, c_ in zip(rest, _unpack(pd, spans_w), _unpack(pnm, spans_w), _unpack(pnv, spans_w)):
        delta[k], new_m[k], new_v[k] = a, b_, c_

    shaped = lambda t: [t[k].reshape(weights[k].shape) for k in order]
    return (loss, res["grad_x"][None], *shaped(grads), *shaped(delta), *shaped(new_m), *shaped(new_v))
```

```python
import numpy as np
import jax
import jax.numpy as jnp
from jax import lax
from jax.experimental import pallas as pl
from jax.experimental.pallas import tpu as pltpu

F32 = jnp.float32
BF16 = jnp.bfloat16

D_MODEL = 1024
SEQ = 2048
CTX_LEN = 256
ZLEN = SEQ + CTX_LEN
GRID_W = 64
GRID_ROWS = SEQ // GRID_W
LRU_BLOCK_W = 128
LRU_BLOCKS = 8
LRU_C = 8.0
NA_HEADS = 16
HEAD_DIM = 64
NA_ROWS = 8
NA_COLS = 16
ROPE_BASE = 10000.0
D_FF = 2816
N_MOD = 6
IN_COLS = 7 * D_MODEL
EPS = 1e-6
NEG_INF = -1e30
N_DEV = 8
N_SHARD = 4

ADAM_LR = 0.001
ADAM_B1 = 0.9
ADAM_B2 = 0.999
ADAM_EPS = 1e-08
ADAM_WD = 0.01
ADAM_STEP = 10

ROW_TILE = 256
Q_ROWS = 4
Q_TILE = Q_ROWS * GRID_W
KEY_ROWS = 12
KEY_TILE = KEY_ROWS * GRID_W
BT_PAD = 4
BT_LEN = 24
VMEM_LIMIT_V7X = 56 * 1024 * 1024

MESH_T = pl.DeviceIdType.MESH


def _params(*sem):
    return pltpu.CompilerParams(dimension_semantics=sem if sem else None, vmem_limit_bytes=VMEM_LIMIT_V7X)


def _full(shape):
    nd = len(shape)
    return pl.BlockSpec(shape, lambda *_: (0,) * nd)


def _sigmoid(x):
    return 1.0 / (1.0 + jnp.exp(-x))


def _gelu_parts(x):
    c0 = 0.7978845608028654
    inner = c0 * (x + 0.044715 * x * x * x)
    t = jnp.tanh(inner)
    g = 0.5 * x * (1.0 + t)
    dg = 0.5 * (1.0 + t) + 0.5 * x * (1.0 - t * t) * c0 * (1.0 + 3.0 * 0.044715 * x * x)
    return g, dg


def _dot_nt(a, b):
    return lax.dot_general(a, b, (((1,), (1,)), ((), ())), preferred_element_type=F32)


def _dot_tn(a, b):
    return lax.dot_general(a, b, (((0,), (0,)), ((), ())), preferred_element_type=F32)


def norm_mod(xin, gain, shift, scale, name):
    r, d = xin.shape
    s_mod = shift.shape[0]
    assert r % ROW_TILE == 0

    def body(x_ref, g_ref, sh_ref, sc_ref, xn_ref):
        x = x_ref[...]
        nrm = x * lax.rsqrt(jnp.mean(x * x, axis=-1, keepdims=True) + EPS)
        xn_ref[...] = ((nrm * g_ref[...]) * (1.0 + sc_ref[0]) + sh_ref[0]).astype(BF16)

    mod_spec = pl.BlockSpec((1, 1, d), lambda i: (jnp.minimum(i, s_mod - 1), 0, 0))
    return pl.pallas_call(
        body, name=name, grid=(r // ROW_TILE,),
        in_specs=[pl.BlockSpec((ROW_TILE, d), lambda i: (i, 0)), _full((1, d)), mod_spec, mod_spec],
        out_specs=pl.BlockSpec((ROW_TILE, d), lambda i: (i, 0)),
        out_shape=jax.ShapeDtypeStruct((r, d), BF16),
        compiler_params=_params("parallel"),
    )(xin, gain, shift, scale)


def matmul_wide(a, b, name, tm, tn):
    m, k = a.shape
    n = b.shape[1]
    assert m % tm == 0 and n % tn == 0

    def body(a_ref, b_ref, o_ref):
        o_ref[...] = jnp.dot(a_ref[...], b_ref[...], preferred_element_type=F32)

    return pl.pallas_call(
        body, name=name, grid=(n // tn, m // tm),
        in_specs=[pl.BlockSpec((tm, k), lambda j, i: (i, 0)), pl.BlockSpec((k, tn), lambda j, i: (0, j))],
        out_specs=pl.BlockSpec((tm, tn), lambda j, i: (i, j)),
        out_shape=jax.ShapeDtypeStruct((m, n), F32),
        compiler_params=_params("parallel", "parallel"),
    )(a, b)


def _row_ids(n, w):
    return lax.broadcasted_iota(jnp.int32, (n, w), 0)


def _lru_conv(xr, cw, cb):
    row = _row_ids(ZLEN, LRU_BLOCK_W)
    segpos = jnp.where(row < CTX_LEN, row, row - CTX_LEN)
    seglen = jnp.where(row < CTX_LEN, CTX_LEN, SEQ)
    acc = xr * cw[2:3, :] + cb
    for k in (0, 1, 3):
        off = k - 2
        sh = pltpu.roll(xr, (-off) % ZLEN, 0)
        ok = (segpos + off >= 0) & (segpos + off < seglen)
        acc = acc + jnp.where(ok, sh, 0.0) * cw[k:k + 1, :]
    return acc


def _lru_conv_t(dxc, cw):
    row = _row_ids(ZLEN, LRU_BLOCK_W)
    segpos = jnp.where(row < CTX_LEN, row, row - CTX_LEN)
    seglen = jnp.where(row < CTX_LEN, CTX_LEN, SEQ)
    acc = dxc * cw[2:3, :]
    for k in (0, 1, 3):
        off = k - 2
        sh = pltpu.roll(dxc, off % ZLEN, 0)
        ok = (segpos - off >= 0) & (segpos - off < seglen)
        acc = acc + jnp.where(ok, sh, 0.0) * cw[k:k + 1, :]
    return acc


def _lru_gates(xc, xcb, wa, ba, wx, bx, lam):
    r = _sigmoid(jnp.dot(xcb, wa, preferred_element_type=F32) + ba)
    i = _sigmoid(jnp.dot(xcb, wx, preferred_element_type=F32) + bx)
    sp = jnp.maximum(-lam, 0.0) + jnp.log1p(jnp.exp(-jnp.abs(lam)))
    la = (-LRU_C) * r * sp
    a = jnp.exp(la)
    sq = jnp.sqrt(-jnp.tanh(la) * (1.0 + a * a))
    b = sq * i * xc
    return r, i, sp, a, sq, b


def _scan8_fwd(a, b, rid):
    for s in (1, 2, 4):
        a_s = pltpu.roll(a, s, 0)
        b_s = pltpu.roll(b, s, 0)
        m = rid >= s
        b = jnp.where(m, a * b_s + b, b)
        a = jnp.where(m, a * a_s, a)
    return a, b


def _scan8_rev(a, b, rid):
    for s in (1, 2, 4):
        a_s = pltpu.roll(a, 8 - s, 0)
        b_s = pltpu.roll(b, 8 - s, 0)
        m = rid < 8 - s
        b = jnp.where(m, a * b_s + b, b)
        a = jnp.where(m, a * a_s, a)
    return a, b


N_CHUNK = ZLEN // 8
CTX_CHUNKS = CTX_LEN // 8


def _scan_up(a_ref, b_ref, h_ref, lo, hi, carry):
    rid = _row_ids(8, LRU_BLOCK_W)

    def step(i, c):
        sl = pl.ds(pl.multiple_of(i * 8, 8), 8)
        a, b = _scan8_fwd(a_ref[sl, :], b_ref[sl, :], rid)
        h = b + a * c
        h_ref[sl, :] = h
        return h[7:8, :]

    return lax.fori_loop(lo, hi, step, carry)


def _scan_down(a_ref, b_ref, h_ref, lo, hi, carry):
    rid = _row_ids(8, LRU_BLOCK_W)

    def step(k, c):
        i = hi - 1 - k
        sl = pl.ds(pl.multiple_of(i * 8, 8), 8)
        a, b = _scan8_rev(a_ref[sl, :], b_ref[sl, :], rid)
        h = b + a * c
        h_ref[sl, :] = h
        return h[0:1, :]

    return lax.fori_loop(0, hi - lo, step, carry)


def _lru_scan_dir(d, a_ref, b_ref, h_ref):
    zero = jnp.zeros((1, LRU_BLOCK_W), F32)
    if d == 0:
        _scan_up(a_ref, b_ref, h_ref, 0, N_CHUNK, zero)
    else:
        c = _scan_down(a_ref, b_ref, h_ref, 0, CTX_CHUNKS, zero)
        _scan_down(a_ref, b_ref, h_ref, CTX_CHUNKS, N_CHUNK, c)


def _lru_in_specs():
    blk = lambda rows: pl.BlockSpec((rows, LRU_BLOCK_W), lambda b: (0, b))
    wspec = pl.BlockSpec((2, 1, LRU_BLOCK_W, LRU_BLOCK_W), lambda b: (0, b, 0, 0))
    return blk, wspec


def lru_fwd(p, conv_w, conv_b, wa, ba, wx, bx, lam):
    blk, wspec = _lru_in_specs()

    def body(xr_ref, gx_ref, cw_ref, cb_ref, wa_ref, ba_ref, wx_ref, bx_ref, lam_ref, y_ref, a_s, b_s, h_s, hsum_s):
        xr = xr_ref[...]
        xc = _lru_conv(xr, cw_ref[...], cb_ref[...])
        xcb = xc.astype(BF16)
        for d in (0, 1):
            _, _, _, a, _, b = _lru_gates(xc, xcb, wa_ref[d, 0].astype(BF16), ba_ref[d:d + 1, :],
                                          wx_ref[d, 0].astype(BF16), bx_ref[d:d + 1, :], lam_ref[d:d + 1, :])
            a_s[...] = a
            b_s[...] = b
            _lru_scan_dir(d, a_s, b_s, h_s)
            if d == 0:
                hsum_s[...] = h_s[...]
            else:
                hsum_s[...] = hsum_s[...] + h_s[...]
        g, _ = _gelu_parts(gx_ref[CTX_LEN:, :])
        y_ref[...] = (hsum_s[CTX_LEN:, :] * g).astype(BF16)

    zs = pltpu.VMEM((ZLEN, LRU_BLOCK_W), F32)
    return pl.pallas_call(
        body, name="lru_fwd", grid=(LRU_BLOCKS,),
        in_specs=[blk(ZLEN), pl.BlockSpec((ZLEN, LRU_BLOCK_W), lambda b: (0, 24 + b)), blk(4), blk(1),
                  wspec, blk(2), wspec, blk(2), blk(2)],
        out_specs=pl.BlockSpec((SEQ, LRU_BLOCK_W), lambda b: (0, b)),
        out_shape=jax.ShapeDtypeStruct((SEQ, D_MODEL), BF16),
        scratch_shapes=[zs, zs, zs, zs],
        compiler_params=_params("arbitrary"),
    )(p, p, conv_w, conv_b, wa, ba, wx, bx, lam)


def _rope_tables():
    t = np.arange(SEQ)
    lane = np.arange(2 * HEAD_DIM)
    in_head = lane % HEAD_DIM
    j = (in_head % 32) % 16
    freq = ROPE_BASE ** (-j.astype(np.float64) / 16.0)
    pos = np.where(in_head[None, :] < 32, (t // GRID_W)[:, None], (t % GRID_W)[:, None]).astype(np.float64)
    ang = (pos.astype(np.float32) * freq.astype(np.float32)[None, :]).astype(np.float32)
    cos = np.cos(ang).astype(np.float32)
    sin = np.sin(ang).astype(np.float32)
    sgn = np.where((in_head % 32) < 16, -1.0, 1.0).astype(np.float32)
    cos = np.concatenate([np.ones((CTX_LEN, 2 * HEAD_DIM), np.float32), cos], 0)
    sin = np.concatenate([np.zeros((CTX_LEN, 2 * HEAD_DIM), np.float32), sin * sgn[None, :]], 0)
    return jnp.asarray(cos), jnp.asarray(sin)


def _head_ones():
    lane = np.arange(2 * HEAD_DIM)
    return jnp.asarray((lane[:, None] // HEAD_DIM == lane[None, :] // HEAD_DIM).astype(np.float32))


def _rope_partner(x):
    lane = lax.broadcasted_iota(jnp.int32, x.shape, 1)
    return jnp.where((lane % 32) < 16, pltpu.roll(x, 128 - 16, 1), pltpu.roll(x, 16, 1))


def _head_rms(x, ones, gain):
    ms = jnp.dot(x * x, ones, preferred_element_type=F32, precision=lax.Precision.HIGHEST) * (1.0 / HEAD_DIM)
    rstd = lax.rsqrt(ms + EPS)
    return x * rstd * gain, rstd


def qkv_prep(p, qg2, kg2, cos, sin, ones):
    scale = HEAD_DIM ** -0.5

    def body(q_ref, k_ref, v_ref, qg_ref, kg_ref, cos_ref, sin_ref, ones_ref, qr_ref, qp_ref, kk_ref, vv_ref):
        ones_m = ones_ref[...]
        c, s = cos_ref[...], sin_ref[...]
        qn, _ = _head_rms(q_ref[...], ones_m, qg_ref[...])
        qn = qn * scale
        qr = qn * c + _rope_partner(qn) * s
        kn, _ = _head_rms(k_ref[...], ones_m, kg_ref[...])
        kr = kn * c + _rope_partner(kn) * s
        v = v_ref[...]
        for hh in range(2):
            sl = slice(hh * HEAD_DIM, (hh + 1) * HEAD_DIM)
            qr_ref[hh] = qr[:, sl].astype(BF16)
            qp_ref[hh] = qn[:, sl].astype(BF16)
            kk_ref[hh] = kr[:, sl].astype(BF16)
            vv_ref[hh] = v[:, sl].astype(BF16)

    col = lambda base: pl.BlockSpec((ROW_TILE, 128), lambda hp, i: (i, base + hp))
    small = pl.BlockSpec((1, 128), lambda hp, i: (0, 0))
    tab = pl.BlockSpec((ROW_TILE, 128), lambda hp, i: (i, 0))
    ospec = pl.BlockSpec((2, ROW_TILE, HEAD_DIM), lambda hp, i: (hp, i, 0))
    oshape = jax.ShapeDtypeStruct((NA_HEADS, ZLEN, HEAD_DIM), BF16)
    return pl.pallas_call(
        body, name="qkv_prep", grid=(NA_HEADS // 2, ZLEN // ROW_TILE),
        in_specs=[col(32), col(8), col(16), small, small, tab, tab, _full((128, 128))],
        out_specs=[ospec] * 4, out_shape=[oshape] * 4,
        compiler_params=_params("parallel", "parallel"),
    )(p, p, p, qg2, kg2, cos, sin, ones)


def _bias_expand():
    qc = np.arange(GRID_W)[:, None]
    kc = np.arange(GRID_W)[None, :]
    col_start = np.clip(qc - NA_COLS // 2, 0, GRID_W - NA_COLS)
    in_win = (kc >= col_start) & (kc < col_start + NA_COLS)
    dc = np.clip(kc - qc, -(NA_COLS - 1), NA_COLS - 1) + (NA_COLS - 1)
    e = np.zeros((2 * NA_COLS - 1, GRID_W, GRID_W), np.float32)
    for d in range(2 * NA_COLS - 1):
        e[d] = ((dc == d) & in_win).astype(np.float32)
    pen = np.where(in_win, 0.0, NEG_INF).astype(np.float32)
    return e, pen


def bias_table(rpb2):
    e, pen = _bias_expand()
    n_dr = 2 * NA_ROWS - 1
    ea = np.zeros((31, GRID_W, 128), np.float32)
    ea[:, :, :GRID_W] = e
    eb = np.zeros((31, GRID_W, 128), np.float32)
    eb[:, :, GRID_W:] = e
    pen2 = np.concatenate([pen, pen], 1)
    ea = jnp.asarray(ea.reshape(31, GRID_W * 128))
    eb = jnp.asarray(eb.reshape(31, GRID_W * 128))
    sel_a = np.zeros((BT_LEN, n_dr), np.float32)
    sel_b = np.zeros((BT_LEN, n_dr), np.float32)
    for r in range(BT_LEN):
        dr = r - BT_PAD
        if 0 <= dr < n_dr:
            sel_a[r, dr] = 1.0
        if 0 <= dr + 1 < n_dr:
            sel_b[r, dr + 1] = 1.0
    sel_a, sel_b = jnp.asarray(sel_a), jnp.asarray(sel_b)
    pen2 = jnp.asarray(pen2.reshape(1, GRID_W * 128))
    hi = lax.Precision.HIGHEST

    def body(rpb_ref, sa_ref, sb_ref, ea_ref, eb_ref, pen_ref, o_ref, ra_s, rb_s):
        for h in range(NA_HEADS):
            rp = rpb_ref[h]
            ra_s[h * BT_LEN:(h + 1) * BT_LEN, :] = jnp.dot(sa_ref[...], rp, preferred_element_type=F32, precision=hi)
            rb_s[h * BT_LEN:(h + 1) * BT_LEN, :] = jnp.dot(sb_ref[...], rp, preferred_element_type=F32, precision=hi)
        o_ref[...] = (jnp.dot(ra_s[...], ea_ref[...], preferred_element_type=F32, precision=hi)
                      + jnp.dot(rb_s[...], eb_ref[...], preferred_element_type=F32, precision=hi) + pen_ref[...])

    tcol = 2048
    rows = NA_HEADS * BT_LEN
    out = pl.pallas_call(
        body, name="bias_table", grid=(GRID_W * 128 // tcol,),
        in_specs=[_full((NA_HEADS, n_dr, 31)), _full((BT_LEN, n_dr)), _full((BT_LEN, n_dr)),
                  pl.BlockSpec((31, tcol), lambda j: (0, j)), pl.BlockSpec((31, tcol), lambda j: (0, j)),
                  pl.BlockSpec((1, tcol), lambda j: (0, j))],
        out_specs=pl.BlockSpec((rows, tcol), lambda j: (0, j)),
        out_shape=jax.ShapeDtypeStruct((rows, GRID_W * 128), F32),
        scratch_shapes=[pltpu.VMEM((rows, 31), F32), pltpu.VMEM((rows, 31), F32)],
        compiler_params=_params("parallel"),
    )(rpb2, sel_a, sel_b, ea, eb, pen2)
    return out.reshape(NA_HEADS, BT_LEN, GRID_W, 128)


def _attn_scores(j, q_rot, q_pl, kk_ref, hh, bt_ref, s_ref):
    ws = jnp.clip(Q_ROWS * j - 4, 0, GRID_ROWS - KEY_ROWS)
    start = pl.multiple_of(CTX_LEN + ws * GRID_W, 256)
    kw = kk_ref[hh, pl.ds(start, KEY_TILE), :]
    s_ref[:, :KEY_TILE] = _dot_nt(q_rot, kw)
    s_ref[:, KEY_TILE:] = _dot_nt(q_pl, kk_ref[hh, :CTX_LEN, :])
    lane = lax.broadcasted_iota(jnp.int32, (GRID_W, 128), 1)
    base = ws - Q_ROWS * j + (NA_ROWS - 1) + BT_PAD
    for qi in range(Q_ROWS):
        rs = jnp.clip(Q_ROWS * j + qi - NA_ROWS // 2, 0, GRID_ROWS - NA_ROWS)
        for m in range(KEY_ROWS // 2):
            k0 = ws + 2 * m
            p0 = jnp.where((k0 >= rs) & (k0 < rs + NA_ROWS), 0.0, NEG_INF)
            p1 = jnp.where((k0 + 1 >= rs) & (k0 + 1 < rs + NA_ROWS), 0.0, NEG_INF)
            pen = jnp.where(lane < GRID_W, p0, p1)
            rows = slice(qi * GRID_W, (qi + 1) * GRID_W)
            cols = slice(128 * m, 128 * (m + 1))
            s_ref[rows, cols] = s_ref[rows, cols] + bt_ref[hh, base + 2 * m - qi] + pen
    return start, base


def attn_fwd(q_rot, q_pl, kk, vv, bt):
    def body(qr_ref, qp_ref, kk_ref, vv_ref, bt_ref, o_ref, lse_ref, s_ref):
        j = pl.program_id(1)
        outs = []
        for hh in range(2):
            start, _ = _attn_scores(j, qr_ref[hh], qp_ref[hh], kk_ref, hh, bt_ref, s_ref)
            s = s_ref[...]
            mx = jnp.max(s, axis=-1, keepdims=True)
            pr = jnp.exp(s - mx)
            l = jnp.sum(pr, axis=-1, keepdims=True)
            prb = pr.astype(BF16)
            o = jnp.dot(prb[:, :KEY_TILE], vv_ref[hh, pl.ds(start, KEY_TILE), :], preferred_element_type=F32)
            o = o + jnp.dot(prb[:, KEY_TILE:], vv_ref[hh, :CTX_LEN, :], preferred_element_type=F32)
            outs.append(o / l)
            lse_ref[hh] = mx + jnp.log(l)
        o_ref[...] = jnp.concatenate(outs, axis=1)

    qspec = pl.BlockSpec((2, Q_TILE, HEAD_DIM), lambda hp, j: (hp, j + 1, 0))
    kspec = pl.BlockSpec((2, ZLEN, HEAD_DIM), lambda hp, j: (hp, 0, 0))
    return pl.pallas_call(
        body, name="attn_fwd", grid=(NA_HEADS // 2, SEQ // Q_TILE),
        in_specs=[qspec, qspec, kspec, kspec, pl.BlockSpec((2, BT_LEN, GRID_W, 128), lambda hp, j: (hp, 0, 0, 0))],
        out_specs=[pl.BlockSpec((Q_TILE, 128), lambda hp, j: (j, hp)),
                   pl.BlockSpec((2, Q_TILE, 1), lambda hp, j: (hp, j, 0))],
        out_shape=[jax.ShapeDtypeStruct((SEQ, D_MODEL), F32), jax.ShapeDtypeStruct((NA_HEADS, SEQ, 1), F32)],
        scratch_shapes=[pltpu.VMEM((Q_TILE, KEY_TILE + CTX_LEN), F32)],
        compiler_params=_params("parallel", "arbitrary"),
    )(q_rot, q_pl, kk, vv, bt)


def merge_fwd(y_rnn, y_na, p, z, g2, w_rnn, w_na, w_out):
    def body(yr_ref, yn_ref, mr_ref, mn_ref, x_ref, g2_ref, wr_ref, wn_ref, wo_ref, u_ref, v_ref, mg_ref, out_ref, x1_ref):
        u = jnp.dot(yr_ref[...], wr_ref[...], preferred_element_type=F32)
        v = jnp.dot(yn_ref[...].astype(BF16), wn_ref[...], preferred_element_type=F32)
        merged = (_sigmoid(mr_ref[...]) * u + _sigmoid(mn_ref[...]) * v).astype(BF16)
        out = jnp.dot(merged, wo_ref[...], preferred_element_type=F32)
        u_ref[...] = u
        v_ref[...] = v
        mg_ref[...] = merged
        out_ref[...] = out
        x1_ref[...] = x_ref[...] + g2_ref[...] * out

    row = pl.BlockSpec((ROW_TILE, D_MODEL), lambda i: (i, 0))
    lat = lambda cb: pl.BlockSpec((ROW_TILE, D_MODEL), lambda i: (i + 1, cb))
    wspec = _full((D_MODEL, D_MODEL))
    f32o = jax.ShapeDtypeStruct((SEQ, D_MODEL), F32)
    return pl.pallas_call(
        body, name="merge_fwd", grid=(SEQ // ROW_TILE,),
        in_specs=[row, row, lat(5), lat(6), lat(0), _full((1, D_MODEL)), wspec, wspec, wspec],
        out_specs=[row] * 5,
        out_shape=[f32o, f32o, jax.ShapeDtypeStruct((SEQ, D_MODEL), BF16), f32o, f32o],
        compiler_params=_params("parallel"),
    )(y_rnn, y_na, p, p, z, g2, w_rnn, w_na, w_out)


FF_TILE = 256
FF_TILES = D_FF // FF_TILE


def _ffn_conv(h, cw, cb):
    row = _row_ids(SEQ, FF_TILE)
    prev = jnp.where(row >= 1, pltpu.roll(h, 1, 0), 0.0)
    nxt = jnp.where(row < SEQ - 1, pltpu.roll(h, SEQ - 1, 0), 0.0)
    return prev * cw[0:1, :] + h * cw[1:2, :] + nxt * cw[2:3, :] + cb


def ffn_act(hpre, conv_w, conv_b):
    def body(ha_ref, hg_ref, wa_ref, wg_ref, ba_ref, bg_ref, o_ref):
        a = _ffn_conv(ha_ref[...], wa_ref[...], ba_ref[...])
        g = _ffn_conv(hg_ref[...], wg_ref[...], bg_ref[...])
        o_ref[...] = (a * _sigmoid(a) * g).astype(BF16)

    col = lambda rows, off: pl.BlockSpec((rows, FF_TILE), lambda j: (0, j + off))
    return pl.pallas_call(
        body, name="ffn_act", grid=(FF_TILES,),
        in_specs=[col(SEQ, 0), col(SEQ, FF_TILES), col(3, 0), col(3, FF_TILES), col(1, 0), col(1, FF_TILES)],
        out_specs=col(SEQ, 0),
        out_shape=jax.ShapeDtypeStruct((SEQ, D_FF), BF16),
        compiler_params=_params("parallel"),
    )(hpre, hpre, conv_w, conv_w, conv_b, conv_b)


def ffn_down_loss(act, w_down, x1, g5, target):
    def body(a_ref, w_ref, x1_ref, g5_ref, t_ref, f_ref, dy_ref, df_ref, ls_ref, dg_ref):
        i = pl.program_id(0)
        f = jnp.dot(a_ref[...], w_ref[...], preferred_element_type=F32)
        g5 = g5_ref[...]
        err = x1_ref[...] + g5 * f - t_ref[...]
        dy = err * (1.0 / D_MODEL)
        f_ref[...] = f
        dy_ref[...] = dy
        df_ref[...] = (dy * g5).astype(BF16)

        @pl.when(i == 0)
        def _():
            ls_ref[...] = jnp.zeros_like(ls_ref)
            dg_ref[...] = jnp.zeros_like(dg_ref)

        ls_ref[...] = ls_ref[...] + jnp.sum(err * err)
        dg_ref[...] = dg_ref[...] + jnp.sum(dy * f, axis=0, keepdims=True)

    row = pl.BlockSpec((ROW_TILE, D_MODEL), lambda i: (i, 0))
    f32o = jax.ShapeDtypeStruct((SEQ, D_MODEL), F32)
    return pl.pallas_call(
        body, name="ffn_down_loss", grid=(SEQ // ROW_TILE,),
        in_specs=[pl.BlockSpec((ROW_TILE, D_FF), lambda i: (i, 0)), _full((D_FF, D_MODEL)), row, _full((1, D_MODEL)), row],
        out_specs=[row, row, row, _full((8, 128)), _full((1, D_MODEL))],
        out_shape=[f32o, f32o, jax.ShapeDtypeStruct((SEQ, D_MODEL), BF16), jax.ShapeDtypeStruct((8, 128), F32),
                   jax.ShapeDtypeStruct((1, D_MODEL), F32)],
        compiler_params=_params("arbitrary"),
    )(act, w_down, x1, g5, target)


def ffn_down_bwd(df, w_down):
    def body(df_ref, w_ref, o_ref):
        o_ref[...] = _dot_nt(df_ref[...], w_ref[...])

    return pl.pallas_call(
        body, name="ffn_down_bwd", grid=(SEQ // ROW_TILE,),
        in_specs=[pl.BlockSpec((ROW_TILE, D_MODEL), lambda i: (i, 0)), _full((D_FF, D_MODEL))],
        out_specs=pl.BlockSpec((ROW_TILE, D_FF), lambda i: (i, 0)),
        out_shape=jax.ShapeDtypeStruct((SEQ, D_FF), F32),
        compiler_params=_params("parallel"),
    )(df, w_down)


def ffn_act_bwd(hpre, d_act, conv_w, conv_b):
    def half_bwd(dc, h, w, dh_ref, dw_ref, db_ref):
        row = _row_ids(SEQ, FF_TILE)
        h_prev = jnp.where(row >= 1, pltpu.roll(h, 1, 0), 0.0)
        h_next = jnp.where(row < SEQ - 1, pltpu.roll(h, SEQ - 1, 0), 0.0)
        dw_ref[0:1, :] = jnp.sum(dc * h_prev, axis=0, keepdims=True)
        dw_ref[1:2, :] = jnp.sum(dc * h, axis=0, keepdims=True)
        dw_ref[2:3, :] = jnp.sum(dc * h_next, axis=0, keepdims=True)
        db_ref[...] = jnp.sum(dc, axis=0, keepdims=True)
        dc_next = jnp.where(row < SEQ - 1, pltpu.roll(dc, SEQ - 1, 0), 0.0)
        dc_prev = jnp.where(row >= 1, pltpu.roll(dc, 1, 0), 0.0)
        dh_ref[...] = (dc_next * w[0:1, :] + dc * w[1:2, :] + dc_prev * w[2:3, :]).astype(BF16)

    def body(ha_ref, hg_ref, da_ref, wa_ref, wg_ref, ba_ref, bg_ref, dha_ref, dhg_ref, dwa_ref, dwg_ref, dba_ref, dbg_ref):
        ha, hg = ha_ref[...], hg_ref[...]
        a = _ffn_conv(ha, wa_ref[...], ba_ref[...])
        g = _ffn_conv(hg, wg_ref[...], bg_ref[...])
        sig = _sigmoid(a)
        dact = da_ref[...]
        half_bwd(dact * g * (sig * (1.0 + a * (1.0 - sig))), ha, wa_ref[...], dha_ref, dwa_ref, dba_ref)
        half_bwd(dact * a * sig, hg, wg_ref[...], dhg_ref, dwg_ref, dbg_ref)

    col = lambda rows, off: pl.BlockSpec((rows, FF_TILE), lambda j: (0, j + off))
    hshape = jax.ShapeDtypeStruct((SEQ, D_FF), BF16)
    wshape = jax.ShapeDtypeStruct((3, D_FF), F32)
    bshape = jax.ShapeDtypeStruct((1, D_FF), F32)
    return pl.pallas_call(
        body, name="ffn_act_bwd", grid=(FF_TILES,),
        in_specs=[col(SEQ, 0), col(SEQ, FF_TILES), col(SEQ, 0), col(3, 0), col(3, FF_TILES), col(1, 0), col(1, FF_TILES)],
        out_specs=[col(SEQ, 0), col(SEQ, 0), col(3, 0), col(3, 0), col(1, 0), col(1, 0)],
        out_shape=[hshape, hshape, wshape, wshape, bshape, bshape],
        compiler_params=_params("parallel"),
    )(hpre, hpre, d_act, conv_w, conv_w, conv_b, conv_b)


def _norm_mod_bwd(x, dxn, gain, scale):
    rstd = lax.rsqrt(jnp.mean(x * x, axis=-1, keepdims=True) + EPS)
    nrm = x * rstd
    dsh = jnp.sum(dxn, axis=0, keepdims=True)
    dsc = jnp.sum(dxn * nrm, axis=0, keepdims=True) * gain
    dgn = jnp.sum(dxn * nrm, axis=0, keepdims=True) * (1.0 + scale)
    dn = dxn * (gain * (1.0 + scale))
    dx = rstd * (dn - nrm * jnp.mean(dn * nrm, axis=-1, keepdims=True))
    return dx, dsh, dsc, dgn


def ffn_up_bwd(dha, dhg, w_up, x1, dy, gain, scale):
    def body(dha_ref, dhg_ref, w_ref, x_ref, dy_ref, g_ref, sc_ref, dx_ref, dsh_ref, dsc_ref, dgn_ref):
        i = pl.program_id(0)
        dxn = _dot_nt(dha_ref[...], w_ref[:, :D_FF]) + _dot_nt(dhg_ref[...], w_ref[:, D_FF:])
        dx, dsh, dsc, dgn = _norm_mod_bwd(x_ref[...], dxn, g_ref[...], sc_ref[...])
        dx_ref[...] = dy_ref[...] + dx

        @pl.when(i == 0)
        def _():
            dsh_ref[...] = dsh
            dsc_ref[...] = dsc
            dgn_ref[...] = dgn

        @pl.when(i > 0)
        def _():
            dsh_ref[...] = dsh_ref[...] + dsh
            dsc_ref[...] = dsc_ref[...] + dsc
            dgn_ref[...] = dgn_ref[...] + dgn

    row = pl.BlockSpec((ROW_TILE, D_MODEL), lambda i: (i, 0))
    vec = _full((1, D_MODEL))
    vshape = jax.ShapeDtypeStruct((1, D_MODEL), F32)
    return pl.pallas_call(
        body, name="ffn_up_bwd", grid=(SEQ // ROW_TILE,),
        in_specs=[pl.BlockSpec((ROW_TILE, D_FF), lambda i: (i, 0)), pl.BlockSpec((ROW_TILE, D_FF), lambda i: (i, 0)),
                  _full((D_MODEL, 2 * D_FF)), row, row, vec, vec],
        out_specs=[row, vec, vec, vec],
        out_shape=[jax.ShapeDtypeStruct((SEQ, D_MODEL), F32), vshape, vshape, vshape],
        compiler_params=_params("arbitrary"),
    )(dha, dhg, w_up, x1, dy, gain, scale)


def merge_bwd(dx1, out, g2, p, u, v, w_rnn, w_na, w_out):
    def body(dx_ref, out_ref, g2_ref, mr_ref, mn_ref, u_ref, v_ref, wr_ref, wn_ref, wo_ref,
             dout_ref, du_ref, dv_ref, dmr_ref, dmn_ref, dyr_ref, dyn_ref, dg2_ref):
        i = pl.program_id(0)

        @pl.when(i == 0)
        def _():
            dmr_ref[...] = jnp.zeros_like(dmr_ref)
            dmn_ref[...] = jnp.zeros_like(dmn_ref)
            dg2_ref[...] = jnp.zeros_like(dg2_ref)

        @pl.when(i > 0)
        def _():
            dx = dx_ref[...]
            dg2_ref[...] = dg2_ref[...] + jnp.sum(dx * out_ref[...], axis=0, keepdims=True)
            dout = (dx * g2_ref[...]).astype(BF16)
            dout_ref[...] = dout
            dm = _dot_nt(dout, wo_ref[...])
            sr = _sigmoid(mr_ref[...])
            sn = _sigmoid(mn_ref[...])
            du = (dm * sr).astype(BF16)
            dv = (dm * sn).astype(BF16)
            du_ref[...] = du
            dv_ref[...] = dv
            dmr_ref[...] = (dm * u_ref[...] * (sr * (1.0 - sr))).astype(BF16)
            dmn_ref[...] = (dm * v_ref[...] * (sn * (1.0 - sn))).astype(BF16)
            dyr_ref[...] = _dot_nt(du, wr_ref[...])
            dyn_ref[...] = _dot_nt(dv, wn_ref[...])

    lat = pl.BlockSpec((ROW_TILE, D_MODEL), lambda i: (jnp.maximum(i - 1, 0), 0))
    zrow = pl.BlockSpec((ROW_TILE, D_MODEL), lambda i: (i, 0))
    pcol = lambda cb: pl.BlockSpec((ROW_TILE, D_MODEL), lambda i: (i, cb))
    wspec = _full((D_MODEL, D_MODEL))
    tb = jax.ShapeDtypeStruct((SEQ, D_MODEL), BF16)
    zb = jax.ShapeDtypeStruct((ZLEN, D_MODEL), BF16)
    tf = jax.ShapeDtypeStruct((SEQ, D_MODEL), F32)
    return pl.pallas_call(
        body, name="merge_bwd", grid=(ZLEN // ROW_TILE,),
        in_specs=[lat, lat, _full((1, D_MODEL)), pcol(5), pcol(6), lat, lat, wspec, wspec, wspec],
        out_specs=[lat, lat, lat, zrow, zrow, lat, lat, _full((1, D_MODEL))],
        out_shape=[tb, tb, tb, zb, zb, tf, tf, jax.ShapeDtypeStruct((1, D_MODEL), F32)],
        compiler_params=_params("arbitrary"),
    )(dx1, out, g2, p, p, u, v, w_rnn, w_na, w_out)


def attn_bwd(q_rot, q_pl, kk, vv, bt, y_na, d_yna, lse):
    def body(qr_ref, qp_ref, kk_ref, vv_ref, bt_ref, o_ref, do_ref, lse_ref,
             dqr_ref, dqp_ref, dk_ref, dv_ref, dbt_ref, s_ref):
        j = pl.program_id(1)

        @pl.when(j == 0)
        def _():
            dk_ref[...] = jnp.zeros_like(dk_ref)
            dv_ref[...] = jnp.zeros_like(dv_ref)
            dbt_ref[...] = jnp.zeros_like(dbt_ref)

        for hh in range(2):
            q_r, q_p = qr_ref[hh], qp_ref[hh]
            start, base = _attn_scores(j, q_r, q_p, kk_ref, hh, bt_ref, s_ref)
            pr = jnp.exp(s_ref[...] - lse_ref[hh])
            sl = slice(hh * HEAD_DIM, (hh + 1) * HEAD_DIM)
            do = do_ref[:, sl]
            delta = jnp.sum(do * o_ref[:, sl], axis=-1, keepdims=True)
            dob = do.astype(BF16)
            win = pl.ds(start, KEY_TILE)
            ds_lat = pr[:, :KEY_TILE] * (_dot_nt(dob, vv_ref[hh, win, :]) - delta)
            ds_ctx = pr[:, KEY_TILE:] * (_dot_nt(dob, vv_ref[hh, :CTX_LEN, :]) - delta)
            for qi in range(Q_ROWS):
                for m in range(KEY_ROWS // 2):
                    idx = base + 2 * m - qi
                    dbt_ref[hh, idx] = dbt_ref[hh, idx] + ds_lat[qi * GRID_W:(qi + 1) * GRID_W, 128 * m:128 * (m + 1)]
            dsb_lat = ds_lat.astype(BF16)
            dsb_ctx = ds_ctx.astype(BF16)
            prb = pr.astype(BF16)
            dqr_ref[hh] = jnp.dot(dsb_lat, kk_ref[hh, win, :], preferred_element_type=F32)
            dqp_ref[hh] = jnp.dot(dsb_ctx, kk_ref[hh, :CTX_LEN, :], preferred_element_type=F32)
            dk_ref[hh, win, :] = dk_ref[hh, win, :] + _dot_tn(dsb_lat, q_r)
            dk_ref[hh, :CTX_LEN, :] = dk_ref[hh, :CTX_LEN, :] + _dot_tn(dsb_ctx, q_p)
            dv_ref[hh, win, :] = dv_ref[hh, win, :] + _dot_tn(prb[:, :KEY_TILE], dob)
            dv_ref[hh, :CTX_LEN, :] = dv_ref[hh, :CTX_LEN, :] + _dot_tn(prb[:, KEY_TILE:], dob)

    qspec = pl.BlockSpec((2, Q_TILE, HEAD_DIM), lambda hp, j: (hp, j + 1, 0))
    kspec = pl.BlockSpec((2, ZLEN, HEAD_DIM), lambda hp, j: (hp, 0, 0))
    btspec = pl.BlockSpec((2, BT_LEN, GRID_W, 128), lambda hp, j: (hp, 0, 0, 0))
    ospec = pl.BlockSpec((Q_TILE, 128), lambda hp, j: (j, hp))
    dqspec = pl.BlockSpec((2, Q_TILE, HEAD_DIM), lambda hp, j: (hp, j, 0))
    dq_shape = jax.ShapeDtypeStruct((NA_HEADS, SEQ, HEAD_DIM), F32)
    dk_shape = jax.ShapeDtypeStruct((NA_HEADS, ZLEN, HEAD_DIM), F32)
    return pl.pallas_call(
        body, name="attn_bwd", grid=(NA_HEADS // 2, SEQ // Q_TILE),
        in_specs=[qspec, qspec, kspec, kspec, btspec, ospec, ospec,
                  pl.BlockSpec((2, Q_TILE, 1), lambda hp, j: (hp, j, 0))],
        out_specs=[dqspec, dqspec, kspec, kspec, btspec],
        out_shape=[dq_shape, dq_shape, dk_shape, dk_shape,
                   jax.ShapeDtypeStruct((NA_HEADS, BT_LEN, GRID_W, 128), F32)],
        scratch_shapes=[pltpu.VMEM((Q_TILE, KEY_TILE + CTX_LEN), F32)],
        compiler_params=_params("parallel", "arbitrary"),
    )(q_rot, q_pl, kk, vv, bt, y_na, d_yna, lse)


def qk_bwd(is_q, d_rot, d_plain, p, gain2, cos, sin, ones):
    scale = HEAD_DIM ** -0.5
    col_base = 32 if is_q else 8
    n_hp, n_i = NA_HEADS // 2, ZLEN // ROW_TILE

    def body(*refs):
        if is_q:
            dr_ref, dp_ref, x_ref, g_ref, cos_ref, sin_ref, ones_ref, dx_ref, dg_ref, acc_ref = refs
        else:
            dr_ref, x_ref, g_ref, cos_ref, sin_ref, ones_ref, dx_ref, dg_ref, acc_ref = refs
        hp, i = pl.program_id(0), pl.program_id(1)
        first = (hp == 0) & (i == 0)

        @pl.when(first)
        def _():
            acc_ref[...] = jnp.zeros_like(acc_ref)

        def work():
            ones_m = ones_ref[...]
            x = x_ref[...]
            xh, rstd = _head_rms(x, ones_m, 1.0)
            d_r = jnp.concatenate([dr_ref[0], dr_ref[1]], axis=1)
            dn = d_r * cos_ref[...] + _rope_partner(d_r * sin_ref[...])
            if is_q:
                dn = (dn + jnp.concatenate([dp_ref[0], dp_ref[1]], axis=1)) * scale
            acc_ref[...] = acc_ref[...] + jnp.sum(dn * xh, axis=0, keepdims=True)
            dxh = dn * g_ref[...]
            seg = jnp.dot(dxh * xh, ones_m, preferred_element_type=F32, precision=lax.Precision.HIGHEST) * (1.0 / HEAD_DIM)
            dx_ref[...] = (rstd * (dxh - xh * seg)).astype(BF16)

        if is_q:
            @pl.when(i == 0)
            def _():
                dx_ref[...] = jnp.zeros_like(dx_ref)

            pl.when(i > 0)(work)
        else:
            work()

        @pl.when((hp == n_hp - 1) & (i == n_i - 1))
        def _():
            dg_ref[...] = acc_ref[:, :HEAD_DIM] + acc_ref[:, HEAD_DIM:]

    if is_q:
        hspec = pl.BlockSpec((2, ROW_TILE, HEAD_DIM), lambda hp, i: (hp, jnp.maximum(i - 1, 0), 0))
        head_in, head_specs = [d_rot, d_plain], [hspec, hspec]
    else:
        head_in, head_specs = [d_rot], [pl.BlockSpec((2, ROW_TILE, HEAD_DIM), lambda hp, i: (hp, i, 0))]
    tab = pl.BlockSpec((ROW_TILE, 128), lambda hp, i: (i, 0))
    return pl.pallas_call(
        body, name="q_bwd" if is_q else "k_bwd", grid=(n_hp, n_i),
        in_specs=head_specs + [pl.BlockSpec((ROW_TILE, 128), lambda hp, i: (i, col_base + hp)),
                               pl.BlockSpec((1, 128), lambda hp, i: (0, 0)), tab, tab, _full((128, 128))],
        out_specs=[pl.BlockSpec((ROW_TILE, 128), lambda hp, i: (i, hp)), _full((1, HEAD_DIM))],
        out_shape=[jax.ShapeDtypeStruct((ZLEN, D_MODEL), BF16), jax.ShapeDtypeStruct((1, HEAD_DIM), F32)],
        scratch_shapes=[pltpu.VMEM((1, 128), F32)],
        compiler_params=_params("arbitrary", "arbitrary"),
    )(*head_in, p, gain2, cos, sin, ones)


def v_bwd(dv):
    def body(d_ref, o_ref):
        o_ref[...] = jnp.concatenate([d_ref[0], d_ref[1]], axis=1).astype(BF16)

    return pl.pallas_call(
        body, name="v_bwd", grid=(NA_HEADS // 2, ZLEN // ROW_TILE),
        in_specs=[pl.BlockSpec((2, ROW_TILE, HEAD_DIM), lambda hp, i: (hp, i, 0))],
        out_specs=pl.BlockSpec((ROW_TILE, 128), lambda hp, i: (i, hp)),
        out_shape=jax.ShapeDtypeStruct((ZLEN, D_MODEL), BF16),
        compiler_params=_params("parallel", "parallel"),
    )(dv)


def rpb_grad(dbt):
    e, _ = _bias_expand()
    n_dr = 2 * NA_ROWS - 1
    ea = np.zeros((31, GRID_W, 128), np.float32)
    ea[:, :, :GRID_W] = e
    eb = np.zeros((31, GRID_W, 128), np.float32)
    eb[:, :, GRID_W:] = e
    eat = jnp.asarray(ea.reshape(31, GRID_W * 128).T.copy())
    ebt = jnp.asarray(eb.reshape(31, GRID_W * 128).T.copy())
    sel_at = np.zeros((n_dr, BT_LEN), np.float32)
    sel_bt = np.zeros((n_dr, BT_LEN), np.float32)
    for r in range(BT_LEN):
        dr = r - BT_PAD
        if 0 <= dr < n_dr:
            sel_at[dr, r] = 1.0
        if 0 <= dr + 1 < n_dr:
            sel_bt[dr + 1, r] = 1.0
    hi = lax.Precision.HIGHEST

    tk = 2048
    wide = GRID_W * 128
    rows = NA_HEADS * BT_LEN
    n_k = wide // tk

    def body(d_ref, sa_ref, sb_ref, ea_ref, eb_ref, o_ref, a_s, b_s):
        k = pl.program_id(0)
        dm = d_ref[...]
        a = jnp.dot(dm, ea_ref[...], preferred_element_type=F32, precision=hi)
        b = jnp.dot(dm, eb_ref[...], preferred_element_type=F32, precision=hi)

        @pl.when(k == 0)
        def _():
            a_s[...] = a
            b_s[...] = b

        @pl.when(k > 0)
        def _():
            a_s[...] = a_s[...] + a
            b_s[...] = b_s[...] + b

        @pl.when(k == n_k - 1)
        def _():
            for h in range(NA_HEADS):
                sl = slice(h * BT_LEN, (h + 1) * BT_LEN)
                o_ref[h] = (jnp.dot(sa_ref[...], a_s[sl, :], preferred_element_type=F32, precision=hi)
                            + jnp.dot(sb_ref[...], b_s[sl, :], preferred_element_type=F32, precision=hi))

    return pl.pallas_call(
        body, name="rpb_grad", grid=(n_k,),
        in_specs=[pl.BlockSpec((rows, tk), lambda k: (0, k)), _full((n_dr, BT_LEN)), _full((n_dr, BT_LEN)),
                  pl.BlockSpec((tk, 31), lambda k: (k, 0)), pl.BlockSpec((tk, 31), lambda k: (k, 0))],
        out_specs=_full((NA_HEADS, n_dr, 31)),
        out_shape=jax.ShapeDtypeStruct((NA_HEADS, n_dr, 31), F32),
        scratch_shapes=[pltpu.VMEM((rows, 31), F32), pltpu.VMEM((rows, 31), F32)],
        compiler_params=_params("arbitrary"),
    )(dbt.reshape(rows, wide), jnp.asarray(sel_at), jnp.asarray(sel_bt), eat, ebt)


def lru_bwd(p, d_yrnn, conv_w, conv_b, wa, ba, wx, bx, lam):
    blk, wspec = _lru_in_specs()

    def body(xr_ref, gx_ref, dy_ref, cw_ref, cb_ref, wa_ref, ba_ref, wx_ref, bx_ref, lam_ref,
             dxr_ref, dgx_ref, dcw_ref, dcb_ref, dwa_ref, dba_ref, dwx_ref, dbx_ref, dlam_ref,
             a_s, b_s, h_s, l_s, hsum_s, dxc_s, dh_s):
        xr = xr_ref[...]
        cw = cw_ref[...]
        xc = _lru_conv(xr, cw, cb_ref[...])
        xcb = xc.astype(BF16)
        g, dg = _gelu_parts(gx_ref[CTX_LEN:, :])
        dy = dy_ref[...]
        dh_s[:CTX_LEN, :] = jnp.zeros((CTX_LEN, LRU_BLOCK_W), F32)
        dh_s[CTX_LEN:, :] = dy * g
        row = _row_ids(ZLEN, LRU_BLOCK_W)
        zero = jnp.zeros((1, LRU_BLOCK_W), F32)
        for d in (0, 1):
            wab = wa_ref[d, 0].astype(BF16)
            wxb = wx_ref[d, 0].astype(BF16)
            lam_d = lam_ref[d:d + 1, :]
            r, gi, sp, a, sq, b = _lru_gates(xc, xcb, wab, ba_ref[d:d + 1, :], wxb, bx_ref[d:d + 1, :], lam_d)
            a_s[...] = a
            b_s[...] = b
            _lru_scan_dir(d, a_s, b_s, h_s)
            h = h_s[...]
            if d == 0:
                hsum_s[...] = h
                h_prev = jnp.where(row >= 1, pltpu.roll(h, 1, 0), 0.0)
                a_s[...] = pltpu.roll(a, ZLEN - 1, 0)
                _scan_down(a_s, dh_s, l_s, 0, N_CHUNK, zero)
            else:
                hsum_s[...] = hsum_s[...] + h
                h_prev = jnp.where(row == CTX_LEN - 1, 0.0, pltpu.roll(h, ZLEN - 1, 0))
                a_s[...] = pltpu.roll(a, 1, 0)
                c = _scan_up(a_s, dh_s, l_s, CTX_CHUNKS, N_CHUNK, zero)
                _scan_up(a_s, dh_s, l_s, 0, CTX_CHUNKS, c)
            db = l_s[...]
            da = db * h_prev
            dsq = db * gi * xc
            dgi = db * sq * xc
            dxc_d = db * sq * gi
            dla = da * a - dsq * (a * a) / sq
            dr = dla * ((-LRU_C) * sp)
            dsp = jnp.sum(dla * ((-LRU_C) * r), axis=0, keepdims=True)
            dlam_ref[d:d + 1, :] = -dsp * _sigmoid(-lam_d)
            dzr = dr * r * (1.0 - r)
            dzi = dgi * gi * (1.0 - gi)
            dba_ref[d:d + 1, :] = jnp.sum(dzr, axis=0, keepdims=True)
            dbx_ref[d:d + 1, :] = jnp.sum(dzi, axis=0, keepdims=True)
            dzrb = dzr.astype(BF16)
            dzib = dzi.astype(BF16)
            dwa_ref[d, 0] = _dot_tn(xcb, dzrb)
            dwx_ref[d, 0] = _dot_tn(xcb, dzib)
            dxc_d = dxc_d + _dot_nt(dzrb, wab) + _dot_nt(dzib, wxb)
            if d == 0:
                dxc_s[...] = dxc_d
            else:
                dxc_s[...] = dxc_s[...] + dxc_d
        dxc = dxc_s[...]
        dxr_ref[...] = _lru_conv_t(dxc, cw).astype(BF16)
        dcb_ref[...] = jnp.sum(dxc, axis=0, keepdims=True)
        segpos = jnp.where(row < CTX_LEN, row, row - CTX_LEN)
        seglen = jnp.where(row < CTX_LEN, CTX_LEN, SEQ)
        for k in range(4):
            off = k - 2
            if off == 0:
                sh = xr
            else:
                ok = (segpos + off >= 0) & (segpos + off < seglen)
                sh = jnp.where(ok, pltpu.roll(xr, (-off) % ZLEN, 0), 0.0)
            dcw_ref[k:k + 1, :] = jnp.sum(dxc * sh, axis=0, keepdims=True)
        dgx_ref[:CTX_LEN, :] = jnp.zeros((CTX_LEN, LRU_BLOCK_W), BF16)
        dgx_ref[CTX_LEN:, :] = (dy * hsum_s[CTX_LEN:, :] * dg).astype(BF16)

    zs = pltpu.VMEM((ZLEN, LRU_BLOCK_W), F32)
    zb = jax.ShapeDtypeStruct((ZLEN, D_MODEL), BF16)
    v2 = jax.ShapeDtypeStruct((2, D_MODEL), F32)
    w4 = jax.ShapeDtypeStruct((2, LRU_BLOCKS, LRU_BLOCK_W, LRU_BLOCK_W), F32)
    return pl.pallas_call(
        body, name="lru_bwd", grid=(LRU_BLOCKS,),
        in_specs=[blk(ZLEN), pl.BlockSpec((ZLEN, LRU_BLOCK_W), lambda b: (0, 24 + b)), blk(SEQ), blk(4), blk(1),
                  wspec, blk(2), wspec, blk(2), blk(2)],
        out_specs=[blk(ZLEN), blk(ZLEN), blk(4), blk(1), wspec, blk(2), wspec, blk(2), blk(2)],
        out_shape=[zb, zb, jax.ShapeDtypeStruct((4, D_MODEL), F32), jax.ShapeDtypeStruct((1, D_MODEL), F32),
                   w4, v2, w4, v2, v2],
        scratch_shapes=[zs] * 7,
        compiler_params=_params("arbitrary"),
    )(p, p, d_yrnn, conv_w, conv_b, wa, ba, wx, bx, lam)


def in_proj_bwd(dgs, w_in, z, dx1, gain, scale):
    def body(*refs):
        dg_refs = refs[:7]
        w_ref, z_ref, dx1_ref, g_ref, sc_ref, gx_ref, dsh_ref, dsc_ref, dgn_ref = refs[7:]
        i = pl.program_id(0)
        dxn = _dot_nt(dg_refs[0][...], w_ref[:, 0:D_MODEL])
        for g in range(1, 7):
            dxn = dxn + _dot_nt(dg_refs[g][...], w_ref[:, g * D_MODEL:(g + 1) * D_MODEL])
        dx, dsh, dsc, dgn = _norm_mod_bwd(z_ref[...], dxn, g_ref[...], sc_ref[0])

        @pl.when(i <= 1)
        def _():
            dsh_ref[0] = dsh
            dsc_ref[0] = dsc

        @pl.when(i > 1)
        def _():
            dsh_ref[0] = dsh_ref[0] + dsh
            dsc_ref[0] = dsc_ref[0] + dsc

        @pl.when(i == 0)
        def _():
            dgn_ref[...] = dgn

        @pl.when(i > 0)
        def _():
            dgn_ref[...] = dgn_ref[...] + dgn
            gx_ref[...] = dx1_ref[...] + dx

    zrow = pl.BlockSpec((ROW_TILE, D_MODEL), lambda i: (i, 0))
    lat = pl.BlockSpec((ROW_TILE, D_MODEL), lambda i: (jnp.maximum(i - 1, 0), 0))
    mod = pl.BlockSpec((1, 1, D_MODEL), lambda i: (jnp.minimum(i, 1), 0, 0))
    mshape = jax.ShapeDtypeStruct((2, 1, D_MODEL), F32)
    return pl.pallas_call(
        body, name="in_proj_bwd", grid=(ZLEN // ROW_TILE,),
        in_specs=[zrow] * 7 + [_full((D_MODEL, IN_COLS)), zrow, lat, _full((1, D_MODEL)), mod],
        out_specs=[lat, mod, mod, _full((1, D_MODEL))],
        out_shape=[jax.ShapeDtypeStruct((SEQ, D_MODEL), F32), mshape, mshape, jax.ShapeDtypeStruct((1, D_MODEL), F32)],
        compiler_params=_params("arbitrary"),
    )(*dgs, w_in, z, dx1, gain, scale)


def matmul_tn(a, b, name, tm, tn, prev=None, col_block=0, total_cols=None):
    k, m = a.shape
    n = b.shape[1]
    total_cols = n if total_cols is None else total_cols
    assert m % tm == 0 and n % tn == 0
    off = col_block * (n // tn)

    def body(a_ref, b_ref, *rest):
        rest[-1][...] = _dot_tn(a_ref[...].astype(BF16), b_ref[...]).astype(BF16)

    in_specs = [pl.BlockSpec((k, tm), lambda i, j: (0, i)), pl.BlockSpec((k, tn), lambda i, j: (0, j))]
    args = [a, b]
    aliases = {}
    if prev is not None:
        in_specs.append(pl.BlockSpec(memory_space=pl.ANY))
        args.append(prev)
        aliases = {2: 0}
    return pl.pallas_call(
        body, name=name, grid=(m // tm, n // tn), in_specs=in_specs,
        out_specs=pl.BlockSpec((tm, tn), lambda i, j: (i, j + off)),
        out_shape=jax.ShapeDtypeStruct((m, total_cols), BF16),
        input_output_aliases=aliases,
        compiler_params=_params("parallel", "parallel"),
    )(*args)


def local_step(z, target, modx, modc, norm_mix_g, norm_ffn_g, w_in, conv_w, conv_b, wa, ba, wx, bx, lam, qg, kg, rpb,
               w_rnn, w_na, w_out, w_up, fconv_w, fconv_b, w_down):
    d = D_MODEL
    mx = [modx[:, k * d:(k + 1) * d] for k in range(N_MOD)]
    shift = jnp.stack([modc[:, 0:d], mx[0]])
    scale = jnp.stack([modc[:, d:2 * d], mx[1]])
    cos, sin = _rope_tables()
    ones = _head_ones()
    qg2 = jnp.tile(qg, (1, 2))
    kg2 = jnp.tile(kg, (1, 2))

    xn = norm_mod(z, norm_mix_g, shift, scale, "norm_mix")
    p = matmul_wide(xn, w_in, "in_proj", ROW_TILE, 1792)
    y_rnn = lru_fwd(p, conv_w, conv_b, wa, ba, wx, bx, lam)
    q_rot, q_pl, kk, vv = qkv_prep(p, qg2, kg2, cos, sin, ones)
    bt = bias_table(rpb)
    y_na, lse = attn_fwd(q_rot, q_pl, kk, vv, bt)
    u, v, merged, out, x1 = merge_fwd(y_rnn, y_na, p, z, mx[2], w_rnn, w_na, w_out)
    xn2 = norm_mod(x1, norm_ffn_g, mx[3][None], mx[4][None], "norm_ffn")
    hpre = matmul_wide(xn2, w_up, "ffn_up", ROW_TILE, 1408)
    act = ffn_act(hpre, fconv_w, fconv_b)
    f, dy, df, loss_sq, dg5 = ffn_down_loss(act, w_down, x1, mx[5], target)

    d_act = ffn_down_bwd(df, w_down)
    dha, dhg, d_fcw_a, d_fcw_g, d_fcb_a, d_fcb_g = ffn_act_bwd(hpre, d_act, fconv_w, fconv_b)
    d_fcw = jnp.concatenate([d_fcw_a, d_fcw_g], axis=1)
    d_fcb = jnp.concatenate([d_fcb_a, d_fcb_g], axis=1)
    dx1, d_s3, d_s4, d_gffn = ffn_up_bwd(dha, dhg, w_up, x1, dy, norm_ffn_g, mx[4])
    g_w_down = matmul_tn(act, df, "gw_down", 256, D_MODEL)
    g_w_up = matmul_tn(xn2, dha, "gw_up_a", 512, 1408, total_cols=2 * D_FF)
    g_w_up = matmul_tn(xn2, dhg, "gw_up_g", 512, 1408, prev=g_w_up, col_block=1, total_cols=2 * D_FF)
    dout, du, dv, dmr, dmn, dyr, dyn, dg2 = merge_bwd(dx1, out, mx[2], p, u, v, w_rnn, w_na, w_out)
    g_w_out = matmul_tn(merged, dout, "gw_out", 512, 512)
    g_w_rnn = matmul_tn(y_rnn, du, "gw_rnn", 512, 512)
    g_w_na = matmul_tn(y_na, dv, "gw_na", 512, 512)
    dqr, dqp, dk, dvh, dbt = attn_bwd(q_rot, q_pl, kk, vv, bt, y_na, dyn, lse)
    dq_cols, d_qg = qk_bwd(True, dqr, dqp, p, qg2, cos, sin, ones)
    dk_cols, d_kg = qk_bwd(False, dk, None, p, kg2, cos, sin, ones)
    dv_cols = v_bwd(dvh)
    d_rpb = rpb_grad(dbt)
    dxr, dgx, d_cw, d_cb, d_wa, d_ba, d_wx, d_bx, d_lam = lru_bwd(p, dyr, conv_w, conv_b, wa, ba, wx, bx, lam)
    dgs = [dxr, dk_cols, dv_cols, dgx, dq_cols, dmr, dmn]
    grad_x, dsh, dsc, d_gmix = in_proj_bwd(dgs, w_in, z, dx1, norm_mix_g, scale)
    g_w_in = None
    for g in range(7):
        g_w_in = matmul_tn(xn, dgs[g], "gw_in_%d" % g, 512, 512, prev=g_w_in, col_block=g, total_cols=IN_COLS)

    d_modx = jnp.concatenate([dsh[1], dsc[1], dg2, d_s3, d_s4, dg5], axis=1)
    d_modc = jnp.concatenate([dsh[0], dsc[0]], axis=1)
    return dict(loss_sq=loss_sq, grad_x=grad_x, d_modx=d_modx, d_modc=d_modc, norm_mix_g=d_gmix, norm_ffn_g=d_gffn,
                w_in=g_w_in, lru_conv_w=d_cw, lru_conv_b=d_cb, lru_wa=d_wa, lru_ba=d_ba, lru_wx=d_wx, lru_bx=d_bx,
                lru_lambda=d_lam, q_norm_g=d_qg, k_norm_g=d_kg, na_rpb=d_rpb, w_rnn_out=g_w_rnn, w_na_out=g_w_na,
                w_out=g_w_out, w_up=g_w_up, ffn_conv_w=d_fcw, ffn_conv_b=d_fcb, w_down=g_w_down)


def _mesh_pos():
    return lax.axis_index("x"), lax.axis_index("y"), lax.axis_index("c")


def _other_chips(x, y):
    return [(1 - x, y), (x, 1 - y), (1 - x, 1 - y)]


def all_gather8(xs, name, with_sum=False):
    m, n = xs.shape
    assert m % 8 == 0

    def body(x_ref, out_ref, *rest):
        if with_sum:
            sum_ref, send_sems, recv_sems, local_sem = rest
        else:
            send_sems, recv_sems, local_sem = rest
        x, y, c = _mesh_pos()
        me, sibling = (x, y, c), (x, y, 1 - c)
        chips = _other_chips(x, y)

        def rows(px, py, pc):
            return out_ref.at[pl.ds((4 * px + 2 * py + pc) * m, m), :]

        def copy(k, block, to, src=None):
            return pltpu.make_async_remote_copy(
                src_ref=rows(*block) if src is None else src, dst_ref=rows(*block),
                send_sem=send_sems.at[k], recv_sem=recv_sems.at[k], device_id=to, device_id_type=MESH_T)

        mine = pltpu.make_async_copy(x_ref, rows(*me), local_sem)
        mine.start()
        first = [copy(0, me, sibling, src=x_ref)]
        first += [copy(1 + j, me, (*chip, c), src=x_ref) for j, chip in enumerate(chips)]
        for cp in first:
            cp.start()
        passed = [copy(4 + j, (*chip, c), sibling) for j, chip in enumerate(chips)]
        for j, chip in enumerate(chips):
            copy(1 + j, (*chip, c), me).wait_recv()
            passed[j].start()
        copy(0, sibling, me).wait_recv()
        for j, chip in enumerate(chips):
            copy(4 + j, (*chip, 1 - c), me).wait_recv()
        for cp in first + passed:
            cp.wait_send()
        mine.wait()
        if with_sum:
            acc = out_ref[0:m, :]
            for k in range(1, N_DEV):
                acc = acc + out_ref[k * m:(k + 1) * m, :]
            sum_ref[...] = acc

    vm = pl.BlockSpec(memory_space=pltpu.VMEM)
    out_shape = [jax.ShapeDtypeStruct((N_DEV * m, n), F32)]
    if with_sum:
        out_shape.append(jax.ShapeDtypeStruct((m, n), F32))
    res = pl.pallas_call(
        body, name=name, in_specs=[vm], out_specs=[vm] * len(out_shape), out_shape=out_shape,
        scratch_shapes=[pltpu.SemaphoreType.DMA((7,)), pltpu.SemaphoreType.DMA((7,)), pltpu.SemaphoreType.DMA],
        compiler_params=pltpu.CompilerParams(vmem_limit_bytes=VMEM_LIMIT_V7X),
    )(xs)
    return res if with_sum else res[0]


BIG = (("w_in", (D_MODEL, IN_COLS), 1), ("w_rnn_out", (D_MODEL, D_MODEL), 0), ("w_na_out", (D_MODEL, D_MODEL), 0),
       ("w_out", (D_MODEL, D_MODEL), 0), ("w_up", (D_MODEL, 2 * D_FF), 1), ("w_down", (D_FF, D_MODEL), 0))


def _shard_shape(full, axis):
    r, c = full
    return (r // N_SHARD, c) if axis == 0 else (r, c // N_SHARD)


def _slot(ref, full, axis, s, h):
    r, c = full
    if axis == 0:
        rs = r // N_SHARD
        return ref.at[pl.ds(s * rs + h * (rs // 2), rs // 2), :]
    cs = c // N_SHARD
    return ref.at[pl.ds(h * (r // 2), r // 2), pl.ds(s * cs, cs)]


def cast_into_full(x, full, axis, idx, name):
    r, c = x.shape
    tr = next(t for t in (512, 352, 256, 128) if r % t == 0)
    nb = r // tr

    def body(idx_ref, x_ref, o_ref):
        o_ref[...] = x_ref[...].astype(BF16)

    if axis == 0:
        out_spec = pl.BlockSpec((tr, c), lambda i, idx_ref: (idx_ref[0] * nb + i, 0))
    else:
        out_spec = pl.BlockSpec((tr, c), lambda i, idx_ref: (i, idx_ref[0]))
    return pl.pallas_call(
        body, name=name,
        grid_spec=pltpu.PrefetchScalarGridSpec(
            num_scalar_prefetch=1, grid=(nb,), in_specs=[pl.BlockSpec((tr, c), lambda i, idx_ref: (i, 0))],
            out_specs=out_spec),
        out_shape=jax.ShapeDtypeStruct(full, BF16),
        compiler_params=_params("parallel"),
    )(idx, x)


def all_gather_weights(fulls):
    nw = len(BIG)

    def body(*refs):
        outs = refs[nw:2 * nw]
        send1, recv1, send2, recv2 = refs[2 * nw:]
        x, y, c = _mesh_pos()
        sibling = (x, y, 1 - c)
        chips = _other_chips(x, y)
        s_me = 2 * x + y

        def shard_of(chip):
            return 2 * chip[0] + chip[1]

        def ici(w, j, shard):
            _, full, axis = BIG[w]
            dst = _slot(outs[w], full, axis, shard, c)
            return pltpu.make_async_remote_copy(
                src_ref=dst, dst_ref=dst, send_sem=send1.at[3 * w + j],
                recv_sem=recv1.at[3 * w + j], device_id=(*chips[j], c), device_id_type=MESH_T)

        def d2d(w, j, shard, half):
            _, full, axis = BIG[w]
            dst = _slot(outs[w], full, axis, shard, half)
            return pltpu.make_async_remote_copy(
                src_ref=dst, dst_ref=dst, send_sem=send2.at[3 * w + j], recv_sem=recv2.at[3 * w + j],
                device_id=sibling, device_id_type=MESH_T)

        first = []
        for w in range(nw):
            for j in range(3):
                cp = ici(w, j, s_me)
                cp.start()
                first.append(cp)
        passed = []
        for w in range(nw):
            for j in range(3):
                sh = shard_of(chips[j])
                ici(w, j, sh).wait_recv()
                cp = d2d(w, j, sh, c)
                cp.start()
                passed.append(cp)
        for w in range(nw):
            for j in range(3):
                d2d(w, j, shard_of(chips[j]), 1 - c).wait_recv()
        for cp in first + passed:
            cp.wait_send()

    hbm = pl.BlockSpec(memory_space=pl.ANY)
    return pl.pallas_call(
        body, name="all_gather_weights", in_specs=[hbm] * nw, out_specs=[hbm] * nw,
        out_shape=[jax.ShapeDtypeStruct(full, BF16) for _, full, _ in BIG],
        input_output_aliases={i: i for i in range(nw)},
        scratch_shapes=[pltpu.SemaphoreType.DMA((3 * nw,))] * 4,
        compiler_params=pltpu.CompilerParams(vmem_limit_bytes=VMEM_LIMIT_V7X),
    )(*fulls)


def _grad_view(g, full, axis):
    r, c = full
    if axis == 0:
        return g.reshape(N_SHARD, 2, r // N_SHARD // 2, c)
    return g.reshape(1, 2, r // 2, c)


def exchange_halves(gviews):
    nw = len(BIG)

    def body(*refs):
        srcs, outs = refs[:nw], refs[nw:2 * nw]
        send_sems, recv_sems = refs[2 * nw:]
        x, y, c = _mesh_pos()
        cps = []
        for w in range(nw):
            cp = pltpu.make_async_remote_copy(
                src_ref=srcs[w].at[:, pl.ds(1 - c, 1)], dst_ref=outs[w], send_sem=send_sems.at[w],
                recv_sem=recv_sems.at[w], device_id=(x, y, 1 - c), device_id_type=MESH_T)
            cp.start()
            cps.append(cp)
        for cp in cps:
            cp.wait()

    hbm = pl.BlockSpec(memory_space=pl.ANY)
    return pl.pallas_call(
        body, name="grad_exchange_halves", in_specs=[hbm] * nw, out_specs=[hbm] * nw,
        out_shape=[jax.ShapeDtypeStruct((g.shape[0], 1) + g.shape[2:], BF16) for g in gviews],
        scratch_shapes=[pltpu.SemaphoreType.DMA((nw,)), pltpu.SemaphoreType.DMA((nw,))],
        compiler_params=pltpu.CompilerParams(vmem_limit_bytes=VMEM_LIMIT_V7X),
    )(*gviews)


def _row_tile(rh):
    return 128 if rh % 128 == 0 else rh


def add_halves(gview, recv, c_idx, name):
    a, _, rh, cc = gview.shape
    tr = _row_tile(rh)

    def body(c_ref, g_ref, r_ref, o_ref):
        o_ref[0] = (g_ref[0, 0].astype(F32) + r_ref[0, 0].astype(F32)).astype(BF16)

    return pl.pallas_call(
        body, name=name,
        grid_spec=pltpu.PrefetchScalarGridSpec(
            num_scalar_prefetch=1, grid=(a, rh // tr),
            in_specs=[pl.BlockSpec((1, 1, tr, cc), lambda s, i, c_ref: (s, c_ref[0], i, 0)),
                      pl.BlockSpec((1, 1, tr, cc), lambda s, i, c_ref: (s, 0, i, 0))],
            out_specs=pl.BlockSpec((1, tr, cc), lambda s, i, c_ref: (s, i, 0))),
        out_shape=jax.ShapeDtypeStruct((a, rh, cc), BF16),
        compiler_params=_params("parallel", "parallel"),
    )(c_idx, gview, recv)


def _piece_shape(full, axis):
    rs, cs = _shard_shape(full, axis)
    return (rs // 2, cs)


def scatter_pieces(partials):
    nw = len(BIG)

    def body(*refs):
        srcs, outs = refs[:nw], refs[nw:2 * nw]
        send_sems, recv_sems = refs[2 * nw:]
        x, y, c = _mesh_pos()
        chips = _other_chips(x, y)
        cps = []
        for w, (_, full, axis) in enumerate(BIG):
            cs = full[1] // N_SHARD
            for j, chip in enumerate(chips):
                s_j = 2 * chip[0] + chip[1]
                src = srcs[w].at[s_j] if axis == 0 else srcs[w].at[0, :, pl.ds(s_j * cs, cs)]
                cp = pltpu.make_async_remote_copy(
                    src_ref=src, dst_ref=outs[w].at[j], send_sem=send_sems.at[3 * w + j],
                    recv_sem=recv_sems.at[3 * w + j], device_id=(*chip, c), device_id_type=MESH_T)
                cp.start()
                cps.append(cp)
        for cp in cps:
            cp.wait()

    hbm = pl.BlockSpec(memory_space=pl.ANY)
    return pl.pallas_call(
        body, name="grad_scatter_pieces", in_specs=[hbm] * nw, out_specs=[hbm] * nw,
        out_shape=[jax.ShapeDtypeStruct((3,) + _piece_shape(full, axis), BF16) for _, full, axis in BIG],
        scratch_shapes=[pltpu.SemaphoreType.DMA((3 * nw,)), pltpu.SemaphoreType.DMA((3 * nw,))],
        compiler_params=pltpu.CompilerParams(vmem_limit_bytes=VMEM_LIMIT_V7X),
    )(*partials)


def add_pieces(partial, recv, idx, axis, name):
    _, rh, cs = recv.shape
    tr = _row_tile(rh)

    def body(idx_ref, p_ref, r_ref, o_ref):
        o_ref[0] = ((p_ref[0].astype(F32) + r_ref[0].astype(F32)) + r_ref[1].astype(F32)) + r_ref[2].astype(F32)

    if axis == 0:
        pspec = pl.BlockSpec((1, tr, cs), lambda i, idx_ref: (idx_ref[0], i, 0))
    else:
        pspec = pl.BlockSpec((1, tr, cs), lambda i, idx_ref: (0, i, idx_ref[0]))
    return pl.pallas_call(
        body, name=name,
        grid_spec=pltpu.PrefetchScalarGridSpec(
            num_scalar_prefetch=1, grid=(rh // tr,),
            in_specs=[pspec, pl.BlockSpec((3, tr, cs), lambda i, idx_ref: (0, i, 0))],
            out_specs=pl.BlockSpec((1, tr, cs), lambda i, idx_ref: (idx_ref[1], i, 0))),
        out_shape=jax.ShapeDtypeStruct((2, rh, cs), F32),
        compiler_params=_params("parallel"),
    )(idx, partial, recv)


def join_halves(halves):
    nw = len(BIG)

    def body(*refs):
        outs = refs[nw:2 * nw]
        send_sems, recv_sems = refs[2 * nw:]
        x, y, c = _mesh_pos()
        cps = []
        for w in range(nw):
            cp = pltpu.make_async_remote_copy(
                src_ref=outs[w].at[c], dst_ref=outs[w].at[c], send_sem=send_sems.at[w], recv_sem=recv_sems.at[w],
                device_id=(x, y, 1 - c), device_id_type=MESH_T)
            cp.start()
            cps.append(cp)
        for w in range(nw):
            cps[w].wait_send()
            pltpu.make_async_remote_copy(
                src_ref=outs[w].at[1 - c], dst_ref=outs[w].at[1 - c], send_sem=send_sems.at[w],
                recv_sem=recv_sems.at[w], device_id=(x, y, 1 - c), device_id_type=MESH_T).wait_recv()

    hbm = pl.BlockSpec(memory_space=pl.ANY)
    return pl.pallas_call(
        body, name="grad_join_halves", in_specs=[hbm] * nw, out_specs=[hbm] * nw,
        out_shape=[jax.ShapeDtypeStruct(h.shape, F32) for h in halves],
        input_output_aliases={i: i for i in range(nw)},
        scratch_shapes=[pltpu.SemaphoreType.DMA((nw,))] * 2,
        compiler_params=pltpu.CompilerParams(vmem_limit_bytes=VMEM_LIMIT_V7X),
    )(*halves)


MOD_COLS = N_MOD * D_MODEL // N_SHARD
MOD_TILE = 512


def mod_fwd(c16, w_mod):
    def body(c_ref, w_ref, s_ref, o_ref):
        cv = c_ref[...]
        s = cv * _sigmoid(cv)
        s_ref[...] = s
        o_ref[...] = jnp.dot(s.astype(BF16), w_ref[...].astype(BF16), preferred_element_type=F32)

    return pl.pallas_call(
        body, name="mod_fwd", grid=(MOD_COLS // MOD_TILE,),
        in_specs=[_full((16, D_MODEL)), pl.BlockSpec((D_MODEL, MOD_TILE), lambda j: (0, j))],
        out_specs=[_full((16, D_MODEL)), pl.BlockSpec((16, MOD_TILE), lambda j: (0, j))],
        out_shape=[jax.ShapeDtypeStruct((16, D_MODEL), F32), jax.ShapeDtypeStruct((16, MOD_COLS), F32)],
        compiler_params=_params("arbitrary"),
    )(c16, w_mod)


def mod_bwd(s16, dm16, w_mod):
    hi = lax.Precision.HIGHEST

    def body(s_ref, d_ref, w_ref, gw_ref, ds_ref):
        j = pl.program_id(0)
        dm = d_ref[...]
        gw_ref[...] = lax.dot_general(s_ref[...], dm, (((0,), (0,)), ((), ())), preferred_element_type=F32, precision=hi)
        part = lax.dot_general(dm, w_ref[...], (((1,), (1,)), ((), ())), preferred_element_type=F32, precision=hi)

        @pl.when(j == 0)
        def _():
            ds_ref[...] = part

        @pl.when(j > 0)
        def _():
            ds_ref[...] = ds_ref[...] + part

    return pl.pallas_call(
        body, name="mod_bwd", grid=(MOD_COLS // MOD_TILE,),
        in_specs=[_full((16, D_MODEL)), pl.BlockSpec((16, MOD_TILE), lambda j: (0, j)),
                  pl.BlockSpec((D_MODEL, MOD_TILE), lambda j: (0, j))],
        out_specs=[pl.BlockSpec((D_MODEL, MOD_TILE), lambda j: (0, j)), _full((16, D_MODEL))],
        out_shape=[jax.ShapeDtypeStruct((D_MODEL, MOD_COLS), F32), jax.ShapeDtypeStruct((16, D_MODEL), F32)],
        compiler_params=_params("arbitrary"),
    )(s16, dm16, w_mod)


def cctx_grad(parts, c_ctx):
    def body(p_ref, c_ref, o_ref):
        ds = p_ref[0:1, :]
        for s in range(1, N_SHARD):
            ds = ds + p_ref[16 * s:16 * s + 1, :]
        cv = c_ref[...]
        sg = _sigmoid(cv)
        o_ref[...] = ds * (sg * (1.0 + cv * (1.0 - sg)))

    return pl.pallas_call(
        body, name="cctx_grad", in_specs=[_full((N_DEV * 8, D_MODEL)), _full((1, D_MODEL))],
        out_specs=_full((1, D_MODEL)), out_shape=jax.ShapeDtypeStruct((1, D_MODEL), F32),
    )(parts, c_ctx)


def add_rows(a, b, name):
    def body(a_ref, b_ref, o_ref):
        o_ref[...] = a_ref[...] + b_ref[...]

    return pl.pallas_call(body, name=name, in_specs=[_full(a.shape), _full(b.shape)], out_specs=_full(a.shape),
                          out_shape=jax.ShapeDtypeStruct(a.shape, F32))(a, b)


def adamw(w, g, m, v, name):
    r, c = w.shape
    tr = 128 if (r % 128 == 0 and r > 128) else r

    def body(w_ref, g_ref, m_ref, v_ref, d_ref, nm_ref, nv_ref):
        g_ = g_ref[...]
        m_ = ADAM_B1 * m_ref[...] + (1.0 - ADAM_B1) * g_
        v_ = ADAM_B2 * v_ref[...] + (1.0 - ADAM_B2) * (g_ * g_)
        m_hat = m_ / (1.0 - ADAM_B1 ** ADAM_STEP)
        v_hat = v_ / (1.0 - ADAM_B2 ** ADAM_STEP)
        d_ref[...] = -ADAM_LR * (m_hat / (jnp.sqrt(v_hat) + ADAM_EPS) + ADAM_WD * w_ref[...])
        nm_ref[...] = m_
        nv_ref[...] = v_

    spec = pl.BlockSpec((tr, c), lambda i: (i, 0))
    shp = jax.ShapeDtypeStruct((r, c), F32)
    return pl.pallas_call(
        body, name=name, grid=(r // tr,), in_specs=[spec] * 4, out_specs=[spec] * 3, out_shape=[shp] * 3,
        compiler_params=_params("parallel"),
    )(w, g, m, v)


LANES = 1024


def _pack(arrs):
    rows, spans, at = [], [], 0
    for a in arrs:
        n = int(np.prod(a.shape))
        nr = 8 * -(-n // (8 * LANES))
        flat = a.reshape(-1)
        if nr * LANES != n:
            flat = jnp.concatenate([flat, jnp.zeros((nr * LANES - n,), F32)])
        rows.append(flat.reshape(nr, LANES))
        spans.append((at, nr, n, a.shape))
        at += nr
    return jnp.concatenate(rows, axis=0), spans


def _unpack(buf, spans):
    out = []
    for at, nr, n, shape in spans:
        out.append(buf[at:at + nr].reshape(-1)[:n].reshape(shape))
    return out


SMALL_SHARD = ("lru_conv_w", "lru_ba", "lru_bx", "lru_lambda", "ffn_conv_w")


def kernel(x, c, ctx, c_ctx, w_mod, b_mod, norm_mix_g, norm_ffn_g, w_in, lru_conv_w, lru_conv_b, lru_wa, lru_ba, lru_wx, lru_bx, lru_lambda, q_norm_g, k_norm_g, na_rpb, w_rnn_out, w_na_out, w_out, w_up, ffn_conv_w, ffn_conv_b, w_down, loss_target, m_c_ctx, m_w_mod, m_b_mod, m_norm_mix_g, m_norm_ffn_g, m_w_in, m_lru_conv_w, m_lru_conv_b, m_lru_wa, m_lru_ba, m_lru_wx, m_lru_bx, m_lru_lambda, m_q_norm_g, m_k_norm_g, m_na_rpb, m_w_rnn_out, m_w_na_out, m_w_out, m_w_up, m_ffn_conv_w, m_ffn_conv_b, m_w_down, v_c_ctx, v_w_mod, v_b_mod, v_norm_mix_g, v_norm_ffn_g, v_w_in, v_lru_conv_w, v_lru_conv_b, v_lru_wa, v_lru_ba, v_lru_wx, v_lru_bx, v_lru_lambda, v_q_norm_g, v_k_norm_g, v_na_rpb, v_w_rnn_out, v_w_na_out, v_w_out, v_w_up, v_ffn_conv_w, v_ffn_conv_b, v_w_down):
    weights = dict(c_ctx=c_ctx, w_mod=w_mod, b_mod=b_mod, norm_mix_g=norm_mix_g, norm_ffn_g=norm_ffn_g, w_in=w_in,
                   lru_conv_w=lru_conv_w, lru_conv_b=lru_conv_b, lru_wa=lru_wa, lru_ba=lru_ba, lru_wx=lru_wx,
                   lru_bx=lru_bx, lru_lambda=lru_lambda, q_norm_g=q_norm_g, k_norm_g=k_norm_g, na_rpb=na_rpb,
                   w_rnn_out=w_rnn_out, w_na_out=w_na_out, w_out=w_out, w_up=w_up, ffn_conv_w=ffn_conv_w,
                   ffn_conv_b=ffn_conv_b, w_down=w_down)
    mom1 = dict(c_ctx=m_c_ctx, w_mod=m_w_mod, b_mod=m_b_mod, norm_mix_g=m_norm_mix_g, norm_ffn_g=m_norm_ffn_g,
                w_in=m_w_in, lru_conv_w=m_lru_conv_w, lru_conv_b=m_lru_conv_b, lru_wa=m_lru_wa, lru_ba=m_lru_ba,
                lru_wx=m_lru_wx, lru_bx=m_lru_bx, lru_lambda=m_lru_lambda, q_norm_g=m_q_norm_g, k_norm_g=m_k_norm_g,
                na_rpb=m_na_rpb, w_rnn_out=m_w_rnn_out, w_na_out=m_w_na_out, w_out=m_w_out, w_up=m_w_up,
                ffn_conv_w=m_ffn_conv_w, ffn_conv_b=m_ffn_conv_b, w_down=m_w_down)
    mom2 = dict(c_ctx=v_c_ctx, w_mod=v_w_mod, b_mod=v_b_mod, norm_mix_g=v_norm_mix_g, norm_ffn_g=v_norm_ffn_g,
                w_in=v_w_in, lru_conv_w=v_lru_conv_w, lru_conv_b=v_lru_conv_b, lru_wa=v_lru_wa, lru_ba=v_lru_ba,
                lru_wx=v_lru_wx, lru_bx=v_lru_bx, lru_lambda=v_lru_lambda, q_norm_g=v_q_norm_g, k_norm_g=v_k_norm_g,
                na_rpb=v_na_rpb, w_rnn_out=v_w_rnn_out, w_na_out=v_w_na_out, w_out=v_w_out, w_up=v_w_up,
                ffn_conv_w=v_ffn_conv_w, ffn_conv_b=v_ffn_conv_b, w_down=v_w_down)
    order = list(weights)
    d = D_MODEL
    mx_, my_, mc_ = _mesh_pos()
    shard = 2 * mx_ + my_
    dev = 2 * shard + mc_

    local_small, small_spans = _pack([c] + [weights[k][0] for k in SMALL_SHARD])
    gath = all_gather8(local_small, "gather_small").reshape(N_DEV, local_small.shape[0], LANES)
    per_dev = [_unpack(gath[k], small_spans) for k in range(N_DEV)]
    c_all = jnp.concatenate([per_dev[k][0] for k in range(N_DEV)], axis=0)
    full_small = {name: jnp.concatenate([per_dev[2 * s][1 + i] for s in range(N_SHARD)], axis=-1)
                  for i, name in enumerate(SMALL_SHARD)}
    c16 = jnp.concatenate([c_all, c_ctx.reshape(1, d), jnp.zeros((7, d), F32)], axis=0)
    s16, mod_part = mod_fwd(c16, w_mod[0])
    mod_all = all_gather8(mod_part, "gather_mod").reshape(N_DEV, 16, MOD_COLS)
    mod = jnp.concatenate([mod_all[2 * s] for s in range(N_SHARD)], axis=1) + b_mod
    modx = lax.dynamic_slice(mod, (dev, 0), (1, N_MOD * d))
    modc = mod[8:9]

    idx = jnp.stack([shard, mc_]).astype(jnp.int32)
    big_full = all_gather_weights([cast_into_full(weights[name][0], full, axis, idx, "cast_" + name)
                                   for name, full, axis in BIG])
    wfull = {name: big_full[i] for i, (name, _, _) in enumerate(BIG)}

    z = jnp.concatenate([ctx[0], x[0]], axis=0)
    res = local_step(z, loss_target[0], modx, modc, norm_mix_g, norm_ffn_g, wfull["w_in"], full_small["lru_conv_w"],
                     lru_conv_b, lru_wa[0], full_small["lru_ba"], lru_wx[0], full_small["lru_bx"],
                     full_small["lru_lambda"], q_norm_g, k_norm_g, na_rpb[0], wfull["w_rnn_out"], wfull["w_na_out"],
                     wfull["w_out"], wfull["w_up"], full_small["ffn_conv_w"], ffn_conv_b, wfull["w_down"])

    c_idx = jnp.reshape(mc_, (1,)).astype(jnp.int32)
    gviews = [_grad_view(res[name], full, axis) for name, full, axis in BIG]
    recv1 = exchange_halves(gviews)
    partials = [add_halves(gviews[i], recv1[i], c_idx, "add_halves_" + BIG[i][0]) for i in range(len(BIG))]
    recv2 = scatter_pieces(partials)
    halves = [add_pieces(partials[i], recv2[i], idx, BIG[i][2], "add_pieces_" + BIG[i][0]) for i in range(len(BIG))]
    joined = join_halves(halves)
    grads = {name: joined[i].reshape(_shard_shape(full, axis)) for i, (name, full, axis) in enumerate(BIG)}

    small_names = ["norm_mix_g", "norm_ffn_g", "lru_conv_w", "lru_conv_b", "lru_wa", "lru_ba", "lru_wx", "lru_bx",
                   "lru_lambda", "q_norm_g", "k_norm_g", "na_rpb", "ffn_conv_w", "ffn_conv_b"]
    local_g, g_spans = _pack([res["loss_sq"][0:1, 0:1], res["d_modx"], res["d_modc"]] + [res[k] for k in small_names])
    n_rows = local_g.shape[0]
    g_all, g_tot = all_gather8(local_g, "allreduce_small", with_sum=True)
    tot = _unpack(g_tot, g_spans)
    loss = (0.5 / d) * tot[0][0, 0]
    small_tot = dict(zip(small_names, tot[3:]))
    at_x = g_spans[1][0]
    dmx_rows = g_all.reshape(N_DEV, n_rows, LANES)[:, at_x:at_x + N_MOD, :].reshape(N_DEV, N_MOD * d)
    dmc_row = jnp.concatenate([tot[2], jnp.zeros((1, 4 * d), F32)], axis=1)
    dm16 = jnp.concatenate([dmx_rows, dmc_row, jnp.zeros((7, N_MOD * d), F32)], axis=0)
    grads["b_mod"] = add_rows(tot[1], dmc_row, "b_mod_grad")
    g_w_mod, ds16 = mod_bwd(s16, lax.dynamic_slice(dm16, (0, shard * MOD_COLS), (16, MOD_COLS)), w_mod[0])
    grads["w_mod"] = g_w_mod
    ds_parts = all_gather8(ds16[8:16], "gather_dsctx")
    grads["c_ctx"] = cctx_grad(ds_parts, c_ctx.reshape(1, d))
    for k in small_names:
        g = small_tot[k]
        if k in SMALL_SHARD:
            w_sh = weights[k].shape[-1]
            g = lax.dynamic_slice_in_dim(g, shard * w_sh, w_sh, axis=g.ndim - 1)
        grads[k] = g

    delta, new_m, new_v = {}, {}, {}
    for name, _, _ in BIG + (("w_mod", None, None),):
        delta[name], new_m[name], new_v[name] = adamw(weights[name][0], grads[name], mom1[name][0], mom2[name][0],
                                                      "adamw_" + name)
    rest = [k for k in order if k not in delta]
    pw, spans_w = _pack([weights[k] for k in rest])
    pg, _ = _pack([grads[k].reshape(weights[k].shape) for k in rest])
    pm, _ = _pack([mom1[k] for k in rest])
    pv, _ = _pack([mom2[k] for k in rest])
    pd, pnm, pnv = adamw(pw, pg, pm, pv, "adamw_small")
    for k, a, b_, c_ in zip(rest, _unpack(pd, spans_w), _unpack(pnm, spans_w), _unpack(pnv, spans_w)):
        delta[k], new_m[k], new_v[k] = a, b_, c_

    shaped = lambda t: [t[k].reshape(weights[k].shape) for k in order]
    return (loss, res["grad_x"][None], *shaped(grads), *shaped(delta), *shaped(new_m), *shaped(new_v))
```

```python
import numpy as np
import jax
import jax.numpy as jnp
from jax import lax
from jax.experimental import pallas as pl
from jax.experimental.pallas import tpu as pltpu

F32 = jnp.float32
BF16 = jnp.bfloat16

D_MODEL = 1024
SEQ = 2048
CTX_LEN = 256
ZLEN = SEQ + CTX_LEN
GRID_W = 64
GRID_ROWS = SEQ // GRID_W
LRU_BLOCK_W = 128
LRU_BLOCKS = 8
LRU_C = 8.0
NA_HEADS = 16
HEAD_DIM = 64
NA_ROWS = 8
NA_COLS = 16
ROPE_BASE = 10000.0
D_FF = 2816
N_MOD = 6
IN_COLS = 7 * D_MODEL
EPS = 1e-6
NEG_INF = -1e30
N_DEV = 8
N_SHARD = 4

ADAM_LR = 0.001
ADAM_B1 = 0.9
ADAM_B2 = 0.999
ADAM_EPS = 1e-08
ADAM_WD = 0.01
ADAM_STEP = 10

ROW_TILE = 256
Q_ROWS = 4
Q_TILE = Q_ROWS * GRID_W
KEY_ROWS = 12
KEY_TILE = KEY_ROWS * GRID_W
BT_PAD = 4
BT_LEN = 24
VMEM_LIMIT_V7X = 56 * 1024 * 1024

MESH_T = pl.DeviceIdType.MESH


def _params(*sem):
    return pltpu.CompilerParams(dimension_semantics=sem if sem else None, vmem_limit_bytes=VMEM_LIMIT_V7X)


def _full(shape):
    nd = len(shape)
    return pl.BlockSpec(shape, lambda *_: (0,) * nd)


def _sigmoid(x):
    return 1.0 / (1.0 + jnp.exp(-x))


def _gelu_parts(x):
    c0 = 0.7978845608028654
    inner = c0 * (x + 0.044715 * x * x * x)
    t = jnp.tanh(inner)
    g = 0.5 * x * (1.0 + t)
    dg = 0.5 * (1.0 + t) + 0.5 * x * (1.0 - t * t) * c0 * (1.0 + 3.0 * 0.044715 * x * x)
    return g, dg


def _dot_nt(a, b):
    return lax.dot_general(a, b, (((1,), (1,)), ((), ())), preferred_element_type=F32)


def _dot_tn(a, b):
    return lax.dot_general(a, b, (((0,), (0,)), ((), ())), preferred_element_type=F32)


def norm_mod(xin, gain, shift, scale, name):
    r, d = xin.shape
    s_mod = shift.shape[0]
    assert r % ROW_TILE == 0

    def body(x_ref, g_ref, sh_ref, sc_ref, xn_ref):
        x = x_ref[...]
        nrm = x * lax.rsqrt(jnp.mean(x * x, axis=-1, keepdims=True) + EPS)
        xn_ref[...] = ((nrm * g_ref[...]) * (1.0 + sc_ref[0]) + sh_ref[0]).astype(BF16)

    mod_spec = pl.BlockSpec((1, 1, d), lambda i: (jnp.minimum(i, s_mod - 1), 0, 0))
    return pl.pallas_call(
        body, name=name, grid=(r // ROW_TILE,),
        in_specs=[pl.BlockSpec((ROW_TILE, d), lambda i: (i, 0)), _full((1, d)), mod_spec, mod_spec],
        out_specs=pl.BlockSpec((ROW_TILE, d), lambda i: (i, 0)),
        out_shape=jax.ShapeDtypeStruct((r, d), BF16),
        compiler_params=_params("parallel"),
    )(xin, gain, shift, scale)


def matmul_wide(a, b, name, tm, tn):
    m, k = a.shape
    n = b.shape[1]
    assert m % tm == 0 and n % tn == 0

    def body(a_ref, b_ref, o_ref):
        o_ref[...] = jnp.dot(a_ref[...], b_ref[...], preferred_element_type=F32)

    return pl.pallas_call(
        body, name=name, grid=(n // tn, m // tm),
        in_specs=[pl.BlockSpec((tm, k), lambda j, i: (i, 0)), pl.BlockSpec((k, tn), lambda j, i: (0, j))],
        out_specs=pl.BlockSpec((tm, tn), lambda j, i: (i, j)),
        out_shape=jax.ShapeDtypeStruct((m, n), F32),
        compiler_params=_params("parallel", "parallel"),
    )(a, b)


def _row_ids(n, w):
    return lax.broadcasted_iota(jnp.int32, (n, w), 0)


def _lru_conv(xr, cw, cb):
    row = _row_ids(ZLEN, LRU_BLOCK_W)
    segpos = jnp.where(row < CTX_LEN, row, row - CTX_LEN)
    seglen = jnp.where(row < CTX_LEN, CTX_LEN, SEQ)
    acc = xr * cw[2:3, :] + cb
    for k in (0, 1, 3):
        off = k - 2
        sh = pltpu.roll(xr, (-off) % ZLEN, 0)
        ok = (segpos + off >= 0) & (segpos + off < seglen)
        acc = acc + jnp.where(ok, sh, 0.0) * cw[k:k + 1, :]
    return acc


def _lru_conv_t(dxc, cw):
    row = _row_ids(ZLEN, LRU_BLOCK_W)
    segpos = jnp.where(row < CTX_LEN, row, row - CTX_LEN)
    seglen = jnp.where(row < CTX_LEN, CTX_LEN, SEQ)
    acc = dxc * cw[2:3, :]
    for k in (0, 1, 3):
        off = k - 2
        sh = pltpu.roll(dxc, off % ZLEN, 0)
        ok = (segpos - off >= 0) & (segpos - off < seglen)
        acc = acc + jnp.where(ok, sh, 0.0) * cw[k:k + 1, :]
    return acc


def _lru_gates(xc, xcb, wa, ba, wx, bx, lam):
    r = _sigmoid(jnp.dot(xcb, wa, preferred_element_type=F32) + ba)
    i = _sigmoid(jnp.dot(xcb, wx, preferred_element_type=F32) + bx)
    sp = jnp.maximum(-lam, 0.0) + jnp.log1p(jnp.exp(-jnp.abs(lam)))
    la = (-LRU_C) * r * sp
    a = jnp.exp(la)
    sq = jnp.sqrt(-jnp.tanh(la) * (1.0 + a * a))
    b = sq * i * xc
    return r, i, sp, a, sq, b


def _scan8_fwd(a, b, rid):
    for s in (1, 2, 4):
        a_s = pltpu.roll(a, s, 0)
        b_s = pltpu.roll(b, s, 0)
        m = rid >= s
        b = jnp.where(m, a * b_s + b, b)
        a = jnp.where(m, a * a_s, a)
    return a, b


def _scan8_rev(a, b, rid):
    for s in (1, 2, 4):
        a_s = pltpu.roll(a, 8 - s, 0)
        b_s = pltpu.roll(b, 8 - s, 0)
        m = rid < 8 - s
        b = jnp.where(m, a * b_s + b, b)
        a = jnp.where(m, a * a_s, a)
    return a, b


N_CHUNK = ZLEN // 8
CTX_CHUNKS = CTX_LEN // 8
SCAN_UNROLL = 8


def _scan_up(a_ref, b_ref, h_ref, lo, hi, carry):
    rid = _row_ids(8, LRU_BLOCK_W)
    assert (hi - lo) % SCAN_UNROLL == 0

    def step(g, c):
        base = pl.multiple_of((lo + g * SCAN_UNROLL) * 8, 8)
        for u in range(SCAN_UNROLL):
            sl = pl.ds(base + 8 * u, 8)
            a, b = _scan8_fwd(a_ref[sl, :], b_ref[sl, :], rid)
            h = b + a * c
            h_ref[sl, :] = h
            c = h[7:8, :]
        return c

    return lax.fori_loop(0, (hi - lo) // SCAN_UNROLL, step, carry)


def _scan_down(a_ref, b_ref, h_ref, lo, hi, carry):
    rid = _row_ids(8, LRU_BLOCK_W)
    assert (hi - lo) % SCAN_UNROLL == 0

    def step(g, c):
        base = pl.multiple_of((hi - (g + 1) * SCAN_UNROLL) * 8, 8)
        for u in reversed(range(SCAN_UNROLL)):
            sl = pl.ds(base + 8 * u, 8)
            a, b = _scan8_rev(a_ref[sl, :], b_ref[sl, :], rid)
            h = b + a * c
            h_ref[sl, :] = h
            c = h[0:1, :]
        return c

    return lax.fori_loop(0, (hi - lo) // SCAN_UNROLL, step, carry)


def _lru_scan_dir(d, a_ref, b_ref, h_ref):
    zero = jnp.zeros((1, LRU_BLOCK_W), F32)
    if d == 0:
        _scan_up(a_ref, b_ref, h_ref, 0, N_CHUNK, zero)
    else:
        c = _scan_down(a_ref, b_ref, h_ref, 0, CTX_CHUNKS, zero)
        _scan_down(a_ref, b_ref, h_ref, CTX_CHUNKS, N_CHUNK, c)


def _lru_in_specs():
    blk = lambda rows: pl.BlockSpec((rows, LRU_BLOCK_W), lambda b: (0, b))
    wspec = pl.BlockSpec((2, 1, LRU_BLOCK_W, LRU_BLOCK_W), lambda b: (0, b, 0, 0))
    return blk, wspec


def lru_fwd(p, conv_w, conv_b, wa, ba, wx, bx, lam):
    blk, wspec = _lru_in_specs()

    def body(xr_ref, gx_ref, cw_ref, cb_ref, wa_ref, ba_ref, wx_ref, bx_ref, lam_ref, y_ref, a_s, b_s, h_s, hsum_s):
        xr = xr_ref[...]
        xc = _lru_conv(xr, cw_ref[...], cb_ref[...])
        xcb = xc.astype(BF16)
        for d in (0, 1):
            _, _, _, a, _, b = _lru_gates(xc, xcb, wa_ref[d, 0].astype(BF16), ba_ref[d:d + 1, :],
                                          wx_ref[d, 0].astype(BF16), bx_ref[d:d + 1, :], lam_ref[d:d + 1, :])
            a_s[...] = a
            b_s[...] = b
            _lru_scan_dir(d, a_s, b_s, h_s)
            if d == 0:
                hsum_s[...] = h_s[...]
            else:
                hsum_s[...] = hsum_s[...] + h_s[...]
        g, _ = _gelu_parts(gx_ref[CTX_LEN:, :])
        y_ref[...] = (hsum_s[CTX_LEN:, :] * g).astype(BF16)

    zs = pltpu.VMEM((ZLEN, LRU_BLOCK_W), F32)
    return pl.pallas_call(
        body, name="lru_fwd", grid=(LRU_BLOCKS,),
        in_specs=[blk(ZLEN), pl.BlockSpec((ZLEN, LRU_BLOCK_W), lambda b: (0, 24 + b)), blk(4), blk(1),
                  wspec, blk(2), wspec, blk(2), blk(2)],
        out_specs=pl.BlockSpec((SEQ, LRU_BLOCK_W), lambda b: (0, b)),
        out_shape=jax.ShapeDtypeStruct((SEQ, D_MODEL), BF16),
        scratch_shapes=[zs, zs, zs, zs],
        compiler_params=_params("arbitrary"),
    )(p, p, conv_w, conv_b, wa, ba, wx, bx, lam)


def _rope_tables():
    t = np.arange(SEQ)
    lane = np.arange(2 * HEAD_DIM)
    in_head = lane % HEAD_DIM
    j = (in_head % 32) % 16
    freq = ROPE_BASE ** (-j.astype(np.float64) / 16.0)
    pos = np.where(in_head[None, :] < 32, (t // GRID_W)[:, None], (t % GRID_W)[:, None]).astype(np.float64)
    ang = (pos.astype(np.float32) * freq.astype(np.float32)[None, :]).astype(np.float32)
    cos = np.cos(ang).astype(np.float32)
    sin = np.sin(ang).astype(np.float32)
    sgn = np.where((in_head % 32) < 16, -1.0, 1.0).astype(np.float32)
    cos = np.concatenate([np.ones((CTX_LEN, 2 * HEAD_DIM), np.float32), cos], 0)
    sin = np.concatenate([np.zeros((CTX_LEN, 2 * HEAD_DIM), np.float32), sin * sgn[None, :]], 0)
    return jnp.asarray(cos), jnp.asarray(sin)


def _head_ones():
    lane = np.arange(2 * HEAD_DIM)
    return jnp.asarray((lane[:, None] // HEAD_DIM == lane[None, :] // HEAD_DIM).astype(np.float32))


def _rope_partner(x):
    lane = lax.broadcasted_iota(jnp.int32, x.shape, 1)
    return jnp.where((lane % 32) < 16, pltpu.roll(x, 128 - 16, 1), pltpu.roll(x, 16, 1))


def _head_rms(x, ones, gain):
    ms = jnp.dot(x * x, ones, preferred_element_type=F32, precision=lax.Precision.HIGHEST) * (1.0 / HEAD_DIM)
    rstd = lax.rsqrt(ms + EPS)
    return x * rstd * gain, rstd


PREP_TILE = 768


def qkv_prep(p, qg2, kg2, cos, sin, ones):
    scale = HEAD_DIM ** -0.5

    def body(q_ref, k_ref, v_ref, qg_ref, kg_ref, cos_ref, sin_ref, ones_ref, qr_ref, qp_ref, kk_ref, vv_ref):
        ones_m = ones_ref[...]
        c, s = cos_ref[...], sin_ref[...]
        qn, _ = _head_rms(q_ref[...], ones_m, qg_ref[...])
        qn = qn * scale
        qr_ref[...] = (qn * c + _rope_partner(qn) * s).astype(BF16)
        qp_ref[...] = qn.astype(BF16)
        kn, _ = _head_rms(k_ref[...], ones_m, kg_ref[...])
        kk_ref[...] = (kn * c + _rope_partner(kn) * s).astype(BF16)
        vv_ref[...] = v_ref[...].astype(BF16)

    col = lambda base: pl.BlockSpec((PREP_TILE, 128), lambda hp, i: (i, base + hp))
    small = pl.BlockSpec((1, 128), lambda hp, i: (0, 0))
    tab = pl.BlockSpec((PREP_TILE, 128), lambda hp, i: (i, 0))
    oshape = jax.ShapeDtypeStruct((ZLEN, D_MODEL), BF16)
    return pl.pallas_call(
        body, name="qkv_prep", grid=(NA_HEADS // 2, ZLEN // PREP_TILE),
        in_specs=[col(32), col(8), col(16), small, small, tab, tab, _full((128, 128))],
        out_specs=[col(0)] * 4, out_shape=[oshape] * 4,
        compiler_params=_params("parallel", "parallel"),
    )(p, p, p, qg2, kg2, cos, sin, ones)


def _bias_expand():
    qc = np.arange(GRID_W)[:, None]
    kc = np.arange(GRID_W)[None, :]
    col_start = np.clip(qc - NA_COLS // 2, 0, GRID_W - NA_COLS)
    in_win = (kc >= col_start) & (kc < col_start + NA_COLS)
    dc = np.clip(kc - qc, -(NA_COLS - 1), NA_COLS - 1) + (NA_COLS - 1)
    e = np.zeros((2 * NA_COLS - 1, GRID_W, GRID_W), np.float32)
    for d in range(2 * NA_COLS - 1):
        e[d] = ((dc == d) & in_win).astype(np.float32)
    pen = np.where(in_win, 0.0, NEG_INF).astype(np.float32)
    return e, pen


def bias_table(rpb2):
    e, pen = _bias_expand()
    n_dr = 2 * NA_ROWS - 1
    ea = np.zeros((31, GRID_W, 128), np.float32)
    ea[:, :, :GRID_W] = e
    eb = np.zeros((31, GRID_W, 128), np.float32)
    eb[:, :, GRID_W:] = e
    pen2 = np.concatenate([pen, pen], 1)
    ea = jnp.asarray(ea.reshape(31, GRID_W * 128))
    eb = jnp.asarray(eb.reshape(31, GRID_W * 128))
    sel_a = np.zeros((BT_LEN, n_dr), np.float32)
    sel_b = np.zeros((BT_LEN, n_dr), np.float32)
    for r in range(BT_LEN):
        dr = r - BT_PAD
        if 0 <= dr < n_dr:
            sel_a[r, dr] = 1.0
        if 0 <= dr + 1 < n_dr:
            sel_b[r, dr + 1] = 1.0
    sel_a, sel_b = jnp.asarray(sel_a), jnp.asarray(sel_b)
    pen2 = jnp.asarray(pen2.reshape(1, GRID_W * 128))
    hi = lax.Precision.HIGHEST

    def body(rpb_ref, sa_ref, sb_ref, ea_ref, eb_ref, pen_ref, o_ref, ra_s, rb_s):
        for h in range(NA_HEADS):
            rp = rpb_ref[h]
            ra_s[h * BT_LEN:(h + 1) * BT_LEN, :] = jnp.dot(sa_ref[...], rp, preferred_element_type=F32, precision=hi)
            rb_s[h * BT_LEN:(h + 1) * BT_LEN, :] = jnp.dot(sb_ref[...], rp, preferred_element_type=F32, precision=hi)
        o_ref[...] = (jnp.dot(ra_s[...], ea_ref[...], preferred_element_type=F32, precision=hi)
                      + jnp.dot(rb_s[...], eb_ref[...], preferred_element_type=F32, precision=hi) + pen_ref[...])

    tcol = 2048
    rows = NA_HEADS * BT_LEN
    out = pl.pallas_call(
        body, name="bias_table", grid=(GRID_W * 128 // tcol,),
        in_specs=[_full((NA_HEADS, n_dr, 31)), _full((BT_LEN, n_dr)), _full((BT_LEN, n_dr)),
                  pl.BlockSpec((31, tcol), lambda j: (0, j)), pl.BlockSpec((31, tcol), lambda j: (0, j)),
                  pl.BlockSpec((1, tcol), lambda j: (0, j))],
        out_specs=pl.BlockSpec((rows, tcol), lambda j: (0, j)),
        out_shape=jax.ShapeDtypeStruct((rows, GRID_W * 128), F32),
        scratch_shapes=[pltpu.VMEM((rows, 31), F32), pltpu.VMEM((rows, 31), F32)],
        compiler_params=_params("parallel"),
    )(rpb2, sel_a, sel_b, ea, eb, pen2)
    return out.reshape(NA_HEADS, BT_LEN, GRID_W, 128)


def _key_window(j):
    ws = jnp.clip(Q_ROWS * j - 4, 0, GRID_ROWS - KEY_ROWS)
    return ws, pl.multiple_of(CTX_LEN + ws * GRID_W, 256)


def _head_mask(hh):
    lane = lax.broadcasted_iota(jnp.int32, (Q_TILE, 128), 1)
    return (lane < HEAD_DIM) if hh == 0 else (lane >= HEAD_DIM)


def _attn_scores(j, ws, q_rot_h, q_pl_h, kw, kc, hh, bt_ref, s_ref):
    s_ref[:, :KEY_TILE] = _dot_nt(q_rot_h, kw)
    s_ref[:, KEY_TILE:] = _dot_nt(q_pl_h, kc)
    lane = lax.broadcasted_iota(jnp.int32, (GRID_W, 128), 1)
    base = ws - Q_ROWS * j + (NA_ROWS - 1) + BT_PAD
    for qi in range(Q_ROWS):
        rs = jnp.clip(Q_ROWS * j + qi - NA_ROWS // 2, 0, GRID_ROWS - NA_ROWS)
        for m in range(KEY_ROWS // 2):
            k0 = ws + 2 * m
            p0 = jnp.where((k0 >= rs) & (k0 < rs + NA_ROWS), 0.0, NEG_INF)
            p1 = jnp.where((k0 + 1 >= rs) & (k0 + 1 < rs + NA_ROWS), 0.0, NEG_INF)
            pen = jnp.where(lane < GRID_W, p0, p1)
            rows = slice(qi * GRID_W, (qi + 1) * GRID_W)
            cols = slice(128 * m, 128 * (m + 1))
            s_ref[rows, cols] = s_ref[rows, cols] + bt_ref[hh, base + 2 * m - qi] + pen
    return base


def attn_fwd(q_rot, q_pl, kk, vv, bt):
    def body(qr_ref, qp_ref, kk_ref, vv_ref, bt_ref, o_ref, lse_ref, s_ref):
        j = pl.program_id(1)
        ws, start = _key_window(j)
        win = pl.ds(start, KEY_TILE)
        kw, kc = kk_ref[win, :], kk_ref[:CTX_LEN, :]
        vw, vc = vv_ref[win, :], vv_ref[:CTX_LEN, :]
        qr, qp = qr_ref[...], qp_ref[...]
        outs = []
        for hh in range(2):
            msk = _head_mask(hh)
            _attn_scores(j, ws, jnp.where(msk, qr, 0), jnp.where(msk, qp, 0), kw, kc, hh, bt_ref, s_ref)
            s = s_ref[...]
            mx = jnp.max(s, axis=-1, keepdims=True)
            pr = jnp.exp(s - mx)
            l = jnp.sum(pr, axis=-1, keepdims=True)
            prb = pr.astype(BF16)
            o = jnp.dot(prb[:, :KEY_TILE], vw, preferred_element_type=F32)
            o = o + jnp.dot(prb[:, KEY_TILE:], vc, preferred_element_type=F32)
            outs.append(o / l)
            lse_ref[hh] = mx + jnp.log(l)
        o_ref[...] = jnp.where(_head_mask(0), outs[0], outs[1])

    qspec = pl.BlockSpec((Q_TILE, 128), lambda hp, j: (j + 1, hp))
    kspec = pl.BlockSpec((ZLEN, 128), lambda hp, j: (0, hp))
    return pl.pallas_call(
        body, name="attn_fwd", grid=(NA_HEADS // 2, SEQ // Q_TILE),
        in_specs=[qspec, qspec, kspec, kspec, pl.BlockSpec((2, BT_LEN, GRID_W, 128), lambda hp, j: (hp, 0, 0, 0))],
        out_specs=[pl.BlockSpec((Q_TILE, 128), lambda hp, j: (j, hp)),
                   pl.BlockSpec((2, Q_TILE, 1), lambda hp, j: (hp, j, 0))],
        out_shape=[jax.ShapeDtypeStruct((SEQ, D_MODEL), F32), jax.ShapeDtypeStruct((NA_HEADS, SEQ, 1), F32)],
        scratch_shapes=[pltpu.VMEM((Q_TILE, KEY_TILE + CTX_LEN), F32)],
        compiler_params=_params("parallel", "arbitrary"),
    )(q_rot, q_pl, kk, vv, bt)


def merge_fwd(y_rnn, y_na, p, z, g2, w_rnn, w_na, w_out):
    def body(yr_ref, yn_ref, mr_ref, mn_ref, x_ref, g2_ref, wr_ref, wn_ref, wo_ref, u_ref, v_ref, mg_ref, out_ref, x1_ref):
        u = jnp.dot(yr_ref[...], wr_ref[...], preferred_element_type=F32)
        v = jnp.dot(yn_ref[...].astype(BF16), wn_ref[...], preferred_element_type=F32)
        merged = (_sigmoid(mr_ref[...]) * u + _sigmoid(mn_ref[...]) * v).astype(BF16)
        out = jnp.dot(merged, wo_ref[...], preferred_element_type=F32)
        u_ref[...] = u
        v_ref[...] = v
        mg_ref[...] = merged
        out_ref[...] = out
        x1_ref[...] = x_ref[...] + g2_ref[...] * out

    row = pl.BlockSpec((ROW_TILE, D_MODEL), lambda i: (i, 0))
    lat = lambda cb: pl.BlockSpec((ROW_TILE, D_MODEL), lambda i: (i + 1, cb))
    wspec = _full((D_MODEL, D_MODEL))
    f32o = jax.ShapeDtypeStruct((SEQ, D_MODEL), F32)
    return pl.pallas_call(
        body, name="merge_fwd", grid=(SEQ // ROW_TILE,),
        in_specs=[row, row, lat(5), lat(6), lat(0), _full((1, D_MODEL)), wspec, wspec, wspec],
        out_specs=[row] * 5,
        out_shape=[f32o, f32o, jax.ShapeDtypeStruct((SEQ, D_MODEL), BF16), f32o, f32o],
        compiler_params=_params("parallel"),
    )(y_rnn, y_na, p, p, z, g2, w_rnn, w_na, w_out)


FF_TILE = 256
FF_TILES = D_FF // FF_TILE


def _ffn_conv(h, cw, cb):
    row = _row_ids(SEQ, FF_TILE)
    prev = jnp.where(row >= 1, pltpu.roll(h, 1, 0), 0.0)
    nxt = jnp.where(row < SEQ - 1, pltpu.roll(h, SEQ - 1, 0), 0.0)
    return prev * cw[0:1, :] + h * cw[1:2, :] + nxt * cw[2:3, :] + cb


def ffn_act(hpre, conv_w, conv_b):
    def body(ha_ref, hg_ref, wa_ref, wg_ref, ba_ref, bg_ref, o_ref):
        a = _ffn_conv(ha_ref[...], wa_ref[...], ba_ref[...])
        g = _ffn_conv(hg_ref[...], wg_ref[...], bg_ref[...])
        o_ref[...] = (a * _sigmoid(a) * g).astype(BF16)

    col = lambda rows, off: pl.BlockSpec((rows, FF_TILE), lambda j: (0, j + off))
    return pl.pallas_call(
        body, name="ffn_act", grid=(FF_TILES,),
        in_specs=[col(SEQ, 0), col(SEQ, FF_TILES), col(3, 0), col(3, FF_TILES), col(1, 0), col(1, FF_TILES)],
        out_specs=col(SEQ, 0),
        out_shape=jax.ShapeDtypeStruct((SEQ, D_FF), BF16),
        compiler_params=_params("parallel"),
    )(hpre, hpre, conv_w, conv_w, conv_b, conv_b)


def ffn_down_loss(act, w_down, x1, g5, target):
    def body(a_ref, w_ref, x1_ref, g5_ref, t_ref, f_ref, dy_ref, df_ref, ls_ref, dg_ref):
        i = pl.program_id(0)
        f = jnp.dot(a_ref[...], w_ref[...], preferred_element_type=F32)
        g5 = g5_ref[...]
        err = x1_ref[...] + g5 * f - t_ref[...]
        dy = err * (1.0 / D_MODEL)
        f_ref[...] = f
        dy_ref[...] = dy
        df_ref[...] = (dy * g5).astype(BF16)

        @pl.when(i == 0)
        def _():
            ls_ref[...] = jnp.zeros_like(ls_ref)
            dg_ref[...] = jnp.zeros_like(dg_ref)

        ls_ref[...] = ls_ref[...] + jnp.sum(err * err)
        dg_ref[...] = dg_ref[...] + jnp.sum(dy * f, axis=0, keepdims=True)

    row = pl.BlockSpec((ROW_TILE, D_MODEL), lambda i: (i, 0))
    f32o = jax.ShapeDtypeStruct((SEQ, D_MODEL), F32)
    return pl.pallas_call(
        body, name="ffn_down_loss", grid=(SEQ // ROW_TILE,),
        in_specs=[pl.BlockSpec((ROW_TILE, D_FF), lambda i: (i, 0)), _full((D_FF, D_MODEL)), row, _full((1, D_MODEL)), row],
        out_specs=[row, row, row, _full((8, 128)), _full((1, D_MODEL))],
        out_shape=[f32o, f32o, jax.ShapeDtypeStruct((SEQ, D_MODEL), BF16), jax.ShapeDtypeStruct((8, 128), F32),
                   jax.ShapeDtypeStruct((1, D_MODEL), F32)],
        compiler_params=_params("arbitrary"),
    )(act, w_down, x1, g5, target)


def ffn_down_bwd(df, w_down):
    def body(df_ref, w_ref, o_ref):
        o_ref[...] = _dot_nt(df_ref[...], w_ref[...])

    return pl.pallas_call(
        body, name="ffn_down_bwd", grid=(SEQ // ROW_TILE,),
        in_specs=[pl.BlockSpec((ROW_TILE, D_MODEL), lambda i: (i, 0)), _full((D_FF, D_MODEL))],
        out_specs=pl.BlockSpec((ROW_TILE, D_FF), lambda i: (i, 0)),
        out_shape=jax.ShapeDtypeStruct((SEQ, D_FF), F32),
        compiler_params=_params("parallel"),
    )(df, w_down)


def ffn_act_bwd(hpre, d_act, conv_w, conv_b):
    def half_bwd(dc, h, w, dh_ref, dw_ref, db_ref):
        row = _row_ids(SEQ, FF_TILE)
        h_prev = jnp.where(row >= 1, pltpu.roll(h, 1, 0), 0.0)
        h_next = jnp.where(row < SEQ - 1, pltpu.roll(h, SEQ - 1, 0), 0.0)
        dw_ref[0:1, :] = jnp.sum(dc * h_prev, axis=0, keepdims=True)
        dw_ref[1:2, :] = jnp.sum(dc * h, axis=0, keepdims=True)
        dw_ref[2:3, :] = jnp.sum(dc * h_next, axis=0, keepdims=True)
        db_ref[...] = jnp.sum(dc, axis=0, keepdims=True)
        dc_next = jnp.where(row < SEQ - 1, pltpu.roll(dc, SEQ - 1, 0), 0.0)
        dc_prev = jnp.where(row >= 1, pltpu.roll(dc, 1, 0), 0.0)
        dh_ref[...] = (dc_next * w[0:1, :] + dc * w[1:2, :] + dc_prev * w[2:3, :]).astype(BF16)

    def body(ha_ref, hg_ref, da_ref, wa_ref, wg_ref, ba_ref, bg_ref, dha_ref, dhg_ref, dwa_ref, dwg_ref, dba_ref, dbg_ref):
        ha, hg = ha_ref[...], hg_ref[...]
        a = _ffn_conv(ha, wa_ref[...], ba_ref[...])
        g = _ffn_conv(hg, wg_ref[...], bg_ref[...])
        sig = _sigmoid(a)
        dact = da_ref[...]
        half_bwd(dact * g * (sig * (1.0 + a * (1.0 - sig))), ha, wa_ref[...], dha_ref, dwa_ref, dba_ref)
        half_bwd(dact * a * sig, hg, wg_ref[...], dhg_ref, dwg_ref, dbg_ref)

    col = lambda rows, off: pl.BlockSpec((rows, FF_TILE), lambda j: (0, j + off))
    hshape = jax.ShapeDtypeStruct((SEQ, D_FF), BF16)
    wshape = jax.ShapeDtypeStruct((3, D_FF), F32)
    bshape = jax.ShapeDtypeStruct((1, D_FF), F32)
    return pl.pallas_call(
        body, name="ffn_act_bwd", grid=(FF_TILES,),
        in_specs=[col(SEQ, 0), col(SEQ, FF_TILES), col(SEQ, 0), col(3, 0), col(3, FF_TILES), col(1, 0), col(1, FF_TILES)],
        out_specs=[col(SEQ, 0), col(SEQ, 0), col(3, 0), col(3, 0), col(1, 0), col(1, 0)],
        out_shape=[hshape, hshape, wshape, wshape, bshape, bshape],
        compiler_params=_params("parallel"),
    )(hpre, hpre, d_act, conv_w, conv_w, conv_b, conv_b)


def _norm_mod_bwd(x, dxn, gain, scale):
    rstd = lax.rsqrt(jnp.mean(x * x, axis=-1, keepdims=True) + EPS)
    nrm = x * rstd
    dsh = jnp.sum(dxn, axis=0, keepdims=True)
    dsc = jnp.sum(dxn * nrm, axis=0, keepdims=True) * gain
    dgn = jnp.sum(dxn * nrm, axis=0, keepdims=True) * (1.0 + scale)
    dn = dxn * (gain * (1.0 + scale))
    dx = rstd * (dn - nrm * jnp.mean(dn * nrm, axis=-1, keepdims=True))
    return dx, dsh, dsc, dgn


def ffn_up_bwd(dha, dhg, w_up, x1, dy, gain, scale):
    def body(dha_ref, dhg_ref, w_ref, x_ref, dy_ref, g_ref, sc_ref, dx_ref, dsh_ref, dsc_ref, dgn_ref):
        i = pl.program_id(0)
        dxn = _dot_nt(dha_ref[...], w_ref[:, :D_FF]) + _dot_nt(dhg_ref[...], w_ref[:, D_FF:])
        dx, dsh, dsc, dgn = _norm_mod_bwd(x_ref[...], dxn, g_ref[...], sc_ref[...])
        dx_ref[...] = dy_ref[...] + dx

        @pl.when(i == 0)
        def _():
            dsh_ref[...] = dsh
            dsc_ref[...] = dsc
            dgn_ref[...] = dgn

        @pl.when(i > 0)
        def _():
            dsh_ref[...] = dsh_ref[...] + dsh
            dsc_ref[...] = dsc_ref[...] + dsc
            dgn_ref[...] = dgn_ref[...] + dgn

    row = pl.BlockSpec((ROW_TILE, D_MODEL), lambda i: (i, 0))
    vec = _full((1, D_MODEL))
    vshape = jax.ShapeDtypeStruct((1, D_MODEL), F32)
    return pl.pallas_call(
        body, name="ffn_up_bwd", grid=(SEQ // ROW_TILE,),
        in_specs=[pl.BlockSpec((ROW_TILE, D_FF), lambda i: (i, 0)), pl.BlockSpec((ROW_TILE, D_FF), lambda i: (i, 0)),
                  _full((D_MODEL, 2 * D_FF)), row, row, vec, vec],
        out_specs=[row, vec, vec, vec],
        out_shape=[jax.ShapeDtypeStruct((SEQ, D_MODEL), F32), vshape, vshape, vshape],
        compiler_params=_params("arbitrary"),
    )(dha, dhg, w_up, x1, dy, gain, scale)


def merge_bwd(dx1, out, g2, p, u, v, w_rnn, w_na, w_out):
    def body(dx_ref, out_ref, g2_ref, mr_ref, mn_ref, u_ref, v_ref, wr_ref, wn_ref, wo_ref,
             dout_ref, du_ref, dv_ref, dmr_ref, dmn_ref, dyr_ref, dyn_ref, dg2_ref):
        i = pl.program_id(0)

        @pl.when(i == 0)
        def _():
            dmr_ref[...] = jnp.zeros_like(dmr_ref)
            dmn_ref[...] = jnp.zeros_like(dmn_ref)
            dg2_ref[...] = jnp.zeros_like(dg2_ref)

        @pl.when(i > 0)
        def _():
            dx = dx_ref[...]
            dg2_ref[...] = dg2_ref[...] + jnp.sum(dx * out_ref[...], axis=0, keepdims=True)
            dout = (dx * g2_ref[...]).astype(BF16)
            dout_ref[...] = dout
            dm = _dot_nt(dout, wo_ref[...])
            sr = _sigmoid(mr_ref[...])
            sn = _sigmoid(mn_ref[...])
            du = (dm * sr).astype(BF16)
            dv = (dm * sn).astype(BF16)
            du_ref[...] = du
            dv_ref[...] = dv
            dmr_ref[...] = (dm * u_ref[...] * (sr * (1.0 - sr))).astype(BF16)
            dmn_ref[...] = (dm * v_ref[...] * (sn * (1.0 - sn))).astype(BF16)
            dyr_ref[...] = _dot_nt(du, wr_ref[...])
            dyn_ref[...] = _dot_nt(dv, wn_ref[...])

    lat = pl.BlockSpec((ROW_TILE, D_MODEL), lambda i: (jnp.maximum(i - 1, 0), 0))
    zrow = pl.BlockSpec((ROW_TILE, D_MODEL), lambda i: (i, 0))
    pcol = lambda cb: pl.BlockSpec((ROW_TILE, D_MODEL), lambda i: (i, cb))
    wspec = _full((D_MODEL, D_MODEL))
    tb = jax.ShapeDtypeStruct((SEQ, D_MODEL), BF16)
    zb = jax.ShapeDtypeStruct((ZLEN, D_MODEL), BF16)
    tf = jax.ShapeDtypeStruct((SEQ, D_MODEL), F32)
    return pl.pallas_call(
        body, name="merge_bwd", grid=(ZLEN // ROW_TILE,),
        in_specs=[lat, lat, _full((1, D_MODEL)), pcol(5), pcol(6), lat, lat, wspec, wspec, wspec],
        out_specs=[lat, lat, lat, zrow, zrow, lat, lat, _full((1, D_MODEL))],
        out_shape=[tb, tb, tb, zb, zb, tf, tf, jax.ShapeDtypeStruct((1, D_MODEL), F32)],
        compiler_params=_params("arbitrary"),
    )(dx1, out, g2, p, p, u, v, w_rnn, w_na, w_out)


def attn_bwd(q_rot, q_pl, kk, vv, bt, y_na, d_yna, lse):
    def body(qr_ref, qp_ref, kk_ref, vv_ref, bt_ref, o_ref, do_ref, lse_ref,
             dqr_ref, dqp_ref, dk_ref, dv_ref, dbt_ref, s_ref):
        jj = pl.program_id(1)

        @pl.when(jj == 0)
        def _():
            dqr_ref[...] = jnp.zeros_like(dqr_ref)
            dqp_ref[...] = jnp.zeros_like(dqp_ref)
            dk_ref[...] = jnp.zeros_like(dk_ref)
            dv_ref[...] = jnp.zeros_like(dv_ref)
            dbt_ref[...] = jnp.zeros_like(dbt_ref)

        @pl.when(jj > 0)
        def _():
            j = jj - 1
            ws, start = _key_window(j)
            win = pl.ds(start, KEY_TILE)
            kw, kc = kk_ref[win, :], kk_ref[:CTX_LEN, :]
            vw, vc = vv_ref[win, :], vv_ref[:CTX_LEN, :]
            qr, qp = qr_ref[...], qp_ref[...]
            do = do_ref[...]
            do_o = do * o_ref[...]
            dq_r, dq_p = [], []
            for hh in range(2):
                msk = _head_mask(hh)
                q_r, q_p = jnp.where(msk, qr, 0), jnp.where(msk, qp, 0)
                base = _attn_scores(j, ws, q_r, q_p, kw, kc, hh, bt_ref, s_ref)
                pr = jnp.exp(s_ref[...] - lse_ref[hh])
                delta = jnp.sum(jnp.where(msk, do_o, 0.0), axis=-1, keepdims=True)
                dob = jnp.where(msk, do, 0.0).astype(BF16)
                ds_lat = pr[:, :KEY_TILE] * (_dot_nt(dob, vw) - delta)
                ds_ctx = pr[:, KEY_TILE:] * (_dot_nt(dob, vc) - delta)
                for qi in range(Q_ROWS):
                    for m in range(KEY_ROWS // 2):
                        idx = base + 2 * m - qi
                        dbt_ref[hh, idx] = dbt_ref[hh, idx] + ds_lat[qi * GRID_W:(qi + 1) * GRID_W, 128 * m:128 * (m + 1)]
                dsb_lat = ds_lat.astype(BF16)
                dsb_ctx = ds_ctx.astype(BF16)
                prb = pr.astype(BF16)
                dq_r.append(jnp.dot(dsb_lat, kw, preferred_element_type=F32))
                dq_p.append(jnp.dot(dsb_ctx, kc, preferred_element_type=F32))
                dk_ref[win, :] = dk_ref[win, :] + _dot_tn(dsb_lat, q_r)
                dk_ref[:CTX_LEN, :] = dk_ref[:CTX_LEN, :] + _dot_tn(dsb_ctx, q_p)
                dv_ref[win, :] = dv_ref[win, :] + _dot_tn(prb[:, :KEY_TILE], dob)
                dv_ref[:CTX_LEN, :] = dv_ref[:CTX_LEN, :] + _dot_tn(prb[:, KEY_TILE:], dob)
            dqr_ref[...] = jnp.where(_head_mask(0), dq_r[0], dq_r[1])
            dqp_ref[...] = jnp.where(_head_mask(0), dq_p[0], dq_p[1])

    lat = lambda jj: jnp.maximum(jj - 1, 0)
    qspec = pl.BlockSpec((Q_TILE, 128), lambda hp, jj: (lat(jj) + 1, hp))
    kspec = pl.BlockSpec((ZLEN, 128), lambda hp, jj: (0, hp))
    btspec = pl.BlockSpec((2, BT_LEN, GRID_W, 128), lambda hp, jj: (hp, 0, 0, 0))
    ospec = pl.BlockSpec((Q_TILE, 128), lambda hp, jj: (lat(jj), hp))
    dqspec = pl.BlockSpec((Q_TILE, 128), lambda hp, jj: (jj, hp))
    zshape = jax.ShapeDtypeStruct((ZLEN, D_MODEL), F32)
    return pl.pallas_call(
        body, name="attn_bwd", grid=(NA_HEADS // 2, ZLEN // Q_TILE),
        in_specs=[qspec, qspec, kspec, kspec, btspec, ospec, ospec,
                  pl.BlockSpec((2, Q_TILE, 1), lambda hp, jj: (hp, lat(jj), 0))],
        out_specs=[dqspec, dqspec, kspec, kspec, btspec],
        out_shape=[zshape, zshape, zshape, zshape, jax.ShapeDtypeStruct((NA_HEADS, BT_LEN, GRID_W, 128), F32)],
        scratch_shapes=[pltpu.VMEM((Q_TILE, KEY_TILE + CTX_LEN), F32)],
        compiler_params=_params("parallel", "arbitrary"),
    )(q_rot, q_pl, kk, vv, bt, y_na, d_yna, lse)


def qkv_bwd(dq_rot, dq_pl, dk, dv, p, qg2, kg2, cos, sin, ones):
    scale = HEAD_DIM ** -0.5
    n_hp, n_i = NA_HEADS // 2, ZLEN // PREP_TILE

    def norm_rope_bwd(d_rot, d_extra, x, gain, cos_t, sin_t, ones_m, dx_ref, acc_ref):
        xh, rstd = _head_rms(x, ones_m, 1.0)
        dn = d_rot * cos_t + _rope_partner(d_rot * sin_t)
        if d_extra is not None:
            dn = (dn + d_extra) * scale
        acc_ref[...] = acc_ref[...] + jnp.sum(dn * xh, axis=0, keepdims=True)
        dxh = dn * gain
        seg = jnp.dot(dxh * xh, ones_m, preferred_element_type=F32, precision=lax.Precision.HIGHEST) * (1.0 / HEAD_DIM)
        dx_ref[...] = (rstd * (dxh - xh * seg)).astype(BF16)

    def body(dqr_ref, dqp_ref, dk_ref, dv_ref, xq_ref, xk_ref, qg_ref, kg_ref, cos_ref, sin_ref, ones_ref,
             dxq_ref, dxk_ref, dxv_ref, dgq_ref, dgk_ref, accq_ref, acck_ref):
        hp, i = pl.program_id(0), pl.program_id(1)

        @pl.when((hp == 0) & (i == 0))
        def _():
            accq_ref[...] = jnp.zeros_like(accq_ref)
            acck_ref[...] = jnp.zeros_like(acck_ref)

        ones_m = ones_ref[...]
        cos_t, sin_t = cos_ref[...], sin_ref[...]
        norm_rope_bwd(dqr_ref[...], dqp_ref[...], xq_ref[...], qg_ref[...], cos_t, sin_t, ones_m, dxq_ref, accq_ref)
        norm_rope_bwd(dk_ref[...], None, xk_ref[...], kg_ref[...], cos_t, sin_t, ones_m, dxk_ref, acck_ref)
        dxv_ref[...] = dv_ref[...].astype(BF16)

        @pl.when((hp == n_hp - 1) & (i == n_i - 1))
        def _():
            dgq_ref[...] = accq_ref[:, :HEAD_DIM] + accq_ref[:, HEAD_DIM:]
            dgk_ref[...] = acck_ref[:, :HEAD_DIM] + acck_ref[:, HEAD_DIM:]

    col = lambda base: pl.BlockSpec((PREP_TILE, 128), lambda hp, i: (i, base + hp))
    small = pl.BlockSpec((1, 128), lambda hp, i: (0, 0))
    tab = pl.BlockSpec((PREP_TILE, 128), lambda hp, i: (i, 0))
    zb = jax.ShapeDtypeStruct((ZLEN, D_MODEL), BF16)
    gshape = jax.ShapeDtypeStruct((1, HEAD_DIM), F32)
    return pl.pallas_call(
        body, name="qkv_bwd", grid=(n_hp, n_i),
        in_specs=[col(0)] * 4 + [col(32), col(8), small, small, tab, tab, _full((128, 128))],
        out_specs=[col(0)] * 3 + [_full((1, HEAD_DIM))] * 2,
        out_shape=[zb, zb, zb, gshape, gshape],
        scratch_shapes=[pltpu.VMEM((1, 128), F32)] * 2,
        compiler_params=_params("arbitrary", "arbitrary"),
    )(dq_rot, dq_pl, dk, dv, p, p, qg2, kg2, cos, sin, ones)


def rpb_grad(dbt):
    e, _ = _bias_expand()
    n_dr = 2 * NA_ROWS - 1
    ea = np.zeros((31, GRID_W, 128), np.float32)
    ea[:, :, :GRID_W] = e
    eb = np.zeros((31, GRID_W, 128), np.float32)
    eb[:, :, GRID_W:] = e
    eat = jnp.asarray(ea.reshape(31, GRID_W * 128).T.copy())
    ebt = jnp.asarray(eb.reshape(31, GRID_W * 128).T.copy())
    sel_at = np.zeros((n_dr, BT_LEN), np.float32)
    sel_bt = np.zeros((n_dr, BT_LEN), np.float32)
    for r in range(BT_LEN):
        dr = r - BT_PAD
        if 0 <= dr < n_dr:
            sel_at[dr, r] = 1.0
        if 0 <= dr + 1 < n_dr:
            sel_bt[dr + 1, r] = 1.0
    hi = lax.Precision.HIGHEST

    tk = 2048
    wide = GRID_W * 128
    rows = NA_HEADS * BT_LEN
    n_k = wide // tk

    def body(d_ref, sa_ref, sb_ref, ea_ref, eb_ref, o_ref, a_s, b_s):
        k = pl.program_id(0)
        dm = d_ref[...]
        a = jnp.dot(dm, ea_ref[...], preferred_element_type=F32, precision=hi)
        b = jnp.dot(dm, eb_ref[...], preferred_element_type=F32, precision=hi)

        @pl.when(k == 0)
        def _():
            a_s[...] = a
            b_s[...] = b

        @pl.when(k > 0)
        def _():
            a_s[...] = a_s[...] + a
            b_s[...] = b_s[...] + b

        @pl.when(k == n_k - 1)
        def _():
            for h in range(NA_HEADS):
                sl = slice(h * BT_LEN, (h + 1) * BT_LEN)
                o_ref[h] = (jnp.dot(sa_ref[...], a_s[sl, :], preferred_element_type=F32, precision=hi)
                            + jnp.dot(sb_ref[...], b_s[sl, :], preferred_element_type=F32, precision=hi))

    return pl.pallas_call(
        body, name="rpb_grad", grid=(n_k,),
        in_specs=[pl.BlockSpec((rows, tk), lambda k: (0, k)), _full((n_dr, BT_LEN)), _full((n_dr, BT_LEN)),
                  pl.BlockSpec((tk, 31), lambda k: (k, 0)), pl.BlockSpec((tk, 31), lambda k: (k, 0))],
        out_specs=_full((NA_HEADS, n_dr, 31)),
        out_shape=jax.ShapeDtypeStruct((NA_HEADS, n_dr, 31), F32),
        scratch_shapes=[pltpu.VMEM((rows, 31), F32), pltpu.VMEM((rows, 31), F32)],
        compiler_params=_params("arbitrary"),
    )(dbt.reshape(rows, wide), jnp.asarray(sel_at), jnp.asarray(sel_bt), eat, ebt)


def lru_bwd(p, d_yrnn, conv_w, conv_b, wa, ba, wx, bx, lam):
    blk, wspec = _lru_in_specs()

    def body(xr_ref, gx_ref, dy_ref, cw_ref, cb_ref, wa_ref, ba_ref, wx_ref, bx_ref, lam_ref,
             dxr_ref, dgx_ref, dcw_ref, dcb_ref, dwa_ref, dba_ref, dwx_ref, dbx_ref, dlam_ref,
             a_s, b_s, h_s, l_s, hsum_s, dxc_s, dh_s):
        xr = xr_ref[...]
        cw = cw_ref[...]
        xc = _lru_conv(xr, cw, cb_ref[...])
        xcb = xc.astype(BF16)
        g, dg = _gelu_parts(gx_ref[CTX_LEN:, :])
        dy = dy_ref[...]
        dh_s[:CTX_LEN, :] = jnp.zeros((CTX_LEN, LRU_BLOCK_W), F32)
        dh_s[CTX_LEN:, :] = dy * g
        row = _row_ids(ZLEN, LRU_BLOCK_W)
        zero = jnp.zeros((1, LRU_BLOCK_W), F32)
        for d in (0, 1):
            wab = wa_ref[d, 0].astype(BF16)
            wxb = wx_ref[d, 0].astype(BF16)
            lam_d = lam_ref[d:d + 1, :]
            r, gi, sp, a, sq, b = _lru_gates(xc, xcb, wab, ba_ref[d:d + 1, :], wxb, bx_ref[d:d + 1, :], lam_d)
            a_s[...] = a
            b_s[...] = b
            _lru_scan_dir(d, a_s, b_s, h_s)
            h = h_s[...]
            if d == 0:
                hsum_s[...] = h
                h_prev = jnp.where(row >= 1, pltpu.roll(h, 1, 0), 0.0)
                a_s[...] = pltpu.roll(a, ZLEN - 1, 0)
                _scan_down(a_s, dh_s, l_s, 0, N_CHUNK, zero)
            else:
                hsum_s[...] = hsum_s[...] + h
                h_prev = jnp.where(row == CTX_LEN - 1, 0.0, pltpu.roll(h, ZLEN - 1, 0))
                a_s[...] = pltpu.roll(a, 1, 0)
                c = _scan_up(a_s, dh_s, l_s, CTX_CHUNKS, N_CHUNK, zero)
                _scan_up(a_s, dh_s, l_s, 0, CTX_CHUNKS, c)
            db = l_s[...]
            da = db * h_prev
            dsq = db * gi * xc
            dgi = db * sq * xc
            dxc_d = db * sq * gi
            dla = da * a - dsq * (a * a) / sq
            dr = dla * ((-LRU_C) * sp)
            dsp = jnp.sum(dla * ((-LRU_C) * r), axis=0, keepdims=True)
            dlam_ref[d:d + 1, :] = -dsp * _sigmoid(-lam_d)
            dzr = dr * r * (1.0 - r)
            dzi = dgi * gi * (1.0 - gi)
            dba_ref[d:d + 1, :] = jnp.sum(dzr, axis=0, keepdims=True)
            dbx_ref[d:d + 1, :] = jnp.sum(dzi, axis=0, keepdims=True)
            dzrb = dzr.astype(BF16)
            dzib = dzi.astype(BF16)
            dwa_ref[d, 0] = _dot_tn(xcb, dzrb)
            dwx_ref[d, 0] = _dot_tn(xcb, dzib)
            dxc_d = dxc_d + _dot_nt(dzrb, wab) + _dot_nt(dzib, wxb)
            if d == 0:
                dxc_s[...] = dxc_d
            else:
                dxc_s[...] = dxc_s[...] + dxc_d
        dxc = dxc_s[...]
        dxr_ref[...] = _lru_conv_t(dxc, cw).astype(BF16)
        dcb_ref[...] = jnp.sum(dxc, axis=0, keepdims=True)
        segpos = jnp.where(row < CTX_LEN, row, row - CTX_LEN)
        seglen = jnp.where(row < CTX_LEN, CTX_LEN, SEQ)
        for k in range(4):
            off = k - 2
            if off == 0:
                sh = xr
            else:
                ok = (segpos + off >= 0) & (segpos + off < seglen)
                sh = jnp.where(ok, pltpu.roll(xr, (-off) % ZLEN, 0), 0.0)
            dcw_ref[k:k + 1, :] = jnp.sum(dxc * sh, axis=0, keepdims=True)
        dgx_ref[:CTX_LEN, :] = jnp.zeros((CTX_LEN, LRU_BLOCK_W), BF16)
        dgx_ref[CTX_LEN:, :] = (dy * hsum_s[CTX_LEN:, :] * dg).astype(BF16)

    zs = pltpu.VMEM((ZLEN, LRU_BLOCK_W), F32)
    zb = jax.ShapeDtypeStruct((ZLEN, D_MODEL), BF16)
    v2 = jax.ShapeDtypeStruct((2, D_MODEL), F32)
    w4 = jax.ShapeDtypeStruct((2, LRU_BLOCKS, LRU_BLOCK_W, LRU_BLOCK_W), F32)
    return pl.pallas_call(
        body, name="lru_bwd", grid=(LRU_BLOCKS,),
        in_specs=[blk(ZLEN), pl.BlockSpec((ZLEN, LRU_BLOCK_W), lambda b: (0, 24 + b)), blk(SEQ), blk(4), blk(1),
                  wspec, blk(2), wspec, blk(2), blk(2)],
        out_specs=[blk(ZLEN), blk(ZLEN), blk(4), blk(1), wspec, blk(2), wspec, blk(2), blk(2)],
        out_shape=[zb, zb, jax.ShapeDtypeStruct((4, D_MODEL), F32), jax.ShapeDtypeStruct((1, D_MODEL), F32),
                   w4, v2, w4, v2, v2],
        scratch_shapes=[zs] * 7,
        compiler_params=_params("arbitrary"),
    )(p, p, d_yrnn, conv_w, conv_b, wa, ba, wx, bx, lam)


def in_proj_bwd(dgs, w_in, z, dx1, gain, scale):
    def body(*refs):
        dg_refs = refs[:7]
        w_ref, z_ref, dx1_ref, g_ref, sc_ref, gx_ref, dsh_ref, dsc_ref, dgn_ref = refs[7:]
        i = pl.program_id(0)
        dxn = _dot_nt(dg_refs[0][...], w_ref[:, 0:D_MODEL])
        for g in range(1, 7):
            dxn = dxn + _dot_nt(dg_refs[g][...], w_ref[:, g * D_MODEL:(g + 1) * D_MODEL])
        dx, dsh, dsc, dgn = _norm_mod_bwd(z_ref[...], dxn, g_ref[...], sc_ref[0])

        @pl.when(i <= 1)
        def _():
            dsh_ref[0] = dsh
            dsc_ref[0] = dsc

        @pl.when(i > 1)
        def _():
            dsh_ref[0] = dsh_ref[0] + dsh
            dsc_ref[0] = dsc_ref[0] + dsc

        @pl.when(i == 0)
        def _():
            dgn_ref[...] = dgn

        @pl.when(i > 0)
        def _():
            dgn_ref[...] = dgn_ref[...] + dgn
            gx_ref[...] = dx1_ref[...] + dx

    zrow = pl.BlockSpec((ROW_TILE, D_MODEL), lambda i: (i, 0))
    lat = pl.BlockSpec((ROW_TILE, D_MODEL), lambda i: (jnp.maximum(i - 1, 0), 0))
    mod = pl.BlockSpec((1, 1, D_MODEL), lambda i: (jnp.minimum(i, 1), 0, 0))
    mshape = jax.ShapeDtypeStruct((2, 1, D_MODEL), F32)
    return pl.pallas_call(
        body, name="in_proj_bwd", grid=(ZLEN // ROW_TILE,),
        in_specs=[zrow] * 7 + [_full((D_MODEL, IN_COLS)), zrow, lat, _full((1, D_MODEL)), mod],
        out_specs=[lat, mod, mod, _full((1, D_MODEL))],
        out_shape=[jax.ShapeDtypeStruct((SEQ, D_MODEL), F32), mshape, mshape, jax.ShapeDtypeStruct((1, D_MODEL), F32)],
        compiler_params=_params("arbitrary"),
    )(*dgs, w_in, z, dx1, gain, scale)


def matmul_tn(a, b, name, tm, tn, prev=None, col_block=0, total_cols=None):
    k, m = a.shape
    n = b.shape[1]
    total_cols = n if total_cols is None else total_cols
    assert m % tm == 0 and n % tn == 0
    off = col_block * (n // tn)

    def body(a_ref, b_ref, *rest):
        rest[-1][...] = _dot_tn(a_ref[...].astype(BF16), b_ref[...]).astype(BF16)

    in_specs = [pl.BlockSpec((k, tm), lambda i, j: (0, i)), pl.BlockSpec((k, tn), lambda i, j: (0, j))]
    args = [a, b]
    aliases = {}
    if prev is not None:
        in_specs.append(pl.BlockSpec(memory_space=pl.ANY))
        args.append(prev)
        aliases = {2: 0}
    return pl.pallas_call(
        body, name=name, grid=(m // tm, n // tn), in_specs=in_specs,
        out_specs=pl.BlockSpec((tm, tn), lambda i, j: (i, j + off)),
        out_shape=jax.ShapeDtypeStruct((m, total_cols), BF16),
        input_output_aliases=aliases,
        compiler_params=_params("parallel", "parallel"),
    )(*args)


def local_step(z, target, modx, modc, norm_mix_g, norm_ffn_g, w_in, conv_w, conv_b, wa, ba, wx, bx, lam, qg, kg, rpb,
               w_rnn, w_na, w_out, w_up, fconv_w, fconv_b, w_down):
    d = D_MODEL
    mx = [modx[:, k * d:(k + 1) * d] for k in range(N_MOD)]
    shift = jnp.stack([modc[:, 0:d], mx[0]])
    scale = jnp.stack([modc[:, d:2 * d], mx[1]])
    cos, sin = _rope_tables()
    ones = _head_ones()
    qg2 = jnp.tile(qg, (1, 2))
    kg2 = jnp.tile(kg, (1, 2))

    xn = norm_mod(z, norm_mix_g, shift, scale, "norm_mix")
    p = matmul_wide(xn, w_in, "in_proj", ROW_TILE, 1792)
    y_rnn = lru_fwd(p, conv_w, conv_b, wa, ba, wx, bx, lam)
    q_rot, q_pl, kk, vv = qkv_prep(p, qg2, kg2, cos, sin, ones)
    bt = bias_table(rpb)
    y_na, lse = attn_fwd(q_rot, q_pl, kk, vv, bt)
    u, v, merged, out, x1 = merge_fwd(y_rnn, y_na, p, z, mx[2], w_rnn, w_na, w_out)
    xn2 = norm_mod(x1, norm_ffn_g, mx[3][None], mx[4][None], "norm_ffn")
    hpre = matmul_wide(xn2, w_up, "ffn_up", ROW_TILE, 1408)
    act = ffn_act(hpre, fconv_w, fconv_b)
    f, dy, df, loss_sq, dg5 = ffn_down_loss(act, w_down, x1, mx[5], target)

    d_act = ffn_down_bwd(df, w_down)
    dha, dhg, d_fcw_a, d_fcw_g, d_fcb_a, d_fcb_g = ffn_act_bwd(hpre, d_act, fconv_w, fconv_b)
    d_fcw = jnp.concatenate([d_fcw_a, d_fcw_g], axis=1)
    d_fcb = jnp.concatenate([d_fcb_a, d_fcb_g], axis=1)
    dx1, d_s3, d_s4, d_gffn = ffn_up_bwd(dha, dhg, w_up, x1, dy, norm_ffn_g, mx[4])
    g_w_down = matmul_tn(act, df, "gw_down", 256, D_MODEL)
    g_w_up = matmul_tn(xn2, dha, "gw_up_a", 512, 1408, total_cols=2 * D_FF)
    g_w_up = matmul_tn(xn2, dhg, "gw_up_g", 512, 1408, prev=g_w_up, col_block=1, total_cols=2 * D_FF)
    dout, du, dv, dmr, dmn, dyr, dyn, dg2 = merge_bwd(dx1, out, mx[2], p, u, v, w_rnn, w_na, w_out)
    g_w_out = matmul_tn(merged, dout, "gw_out", 512, 512)
    g_w_rnn = matmul_tn(y_rnn, du, "gw_rnn", 512, 512)
    g_w_na = matmul_tn(y_na, dv, "gw_na", 512, 512)
    dqr, dqp, dk, dvh, dbt = attn_bwd(q_rot, q_pl, kk, vv, bt, y_na, dyn, lse)
    dq_cols, dk_cols, dv_cols, d_qg, d_kg = qkv_bwd(dqr, dqp, dk, dvh, p, qg2, kg2, cos, sin, ones)
    d_rpb = rpb_grad(dbt)
    dxr, dgx, d_cw, d_cb, d_wa, d_ba, d_wx, d_bx, d_lam = lru_bwd(p, dyr, conv_w, conv_b, wa, ba, wx, bx, lam)
    dgs = [dxr, dk_cols, dv_cols, dgx, dq_cols, dmr, dmn]
    grad_x, dsh, dsc, d_gmix = in_proj_bwd(dgs, w_in, z, dx1, norm_mix_g, scale)
    g_w_in = None
    for g in range(7):
        g_w_in = matmul_tn(xn, dgs[g], "gw_in_%d" % g, 512, 512, prev=g_w_in, col_block=g, total_cols=IN_COLS)

    d_modx = jnp.concatenate([dsh[1], dsc[1], dg2, d_s3, d_s4, dg5], axis=1)
    d_modc = jnp.concatenate([dsh[0], dsc[0]], axis=1)
    return dict(loss_sq=loss_sq, grad_x=grad_x, d_modx=d_modx, d_modc=d_modc, norm_mix_g=d_gmix, norm_ffn_g=d_gffn,
                w_in=g_w_in, lru_conv_w=d_cw, lru_conv_b=d_cb, lru_wa=d_wa, lru_ba=d_ba, lru_wx=d_wx, lru_bx=d_bx,
                lru_lambda=d_lam, q_norm_g=d_qg, k_norm_g=d_kg, na_rpb=d_rpb, w_rnn_out=g_w_rnn, w_na_out=g_w_na,
                w_out=g_w_out, w_up=g_w_up, ffn_conv_w=d_fcw, ffn_conv_b=d_fcb, w_down=g_w_down)


def _mesh_pos():
    return lax.axis_index("x"), lax.axis_index("y"), lax.axis_index("c")


def _other_chips(x, y):
    return [(1 - x, y), (x, 1 - y), (1 - x, 1 - y)]


def all_gather8(xs, name, with_sum=False):
    m, n = xs.shape
    assert m % 8 == 0

    def body(x_ref, out_ref, *rest):
        if with_sum:
            sum_ref, send_sems, recv_sems, local_sem = rest
        else:
            send_sems, recv_sems, local_sem = rest
        x, y, c = _mesh_pos()
        me, sibling = (x, y, c), (x, y, 1 - c)
        chips = _other_chips(x, y)

        def rows(px, py, pc):
            return out_ref.at[pl.ds((4 * px + 2 * py + pc) * m, m), :]

        def copy(k, block, to, src=None):
            return pltpu.make_async_remote_copy(
                src_ref=rows(*block) if src is None else src, dst_ref=rows(*block),
                send_sem=send_sems.at[k], recv_sem=recv_sems.at[k], device_id=to, device_id_type=MESH_T)

        mine = pltpu.make_async_copy(x_ref, rows(*me), local_sem)
        mine.start()
        first = [copy(0, me, sibling, src=x_ref)]
        first += [copy(1 + j, me, (*chip, c), src=x_ref) for j, chip in enumerate(chips)]
        for cp in first:
            cp.start()
        passed = [copy(4 + j, (*chip, c), sibling) for j, chip in enumerate(chips)]
        for j, chip in enumerate(chips):
            copy(1 + j, (*chip, c), me).wait_recv()
            passed[j].start()
        copy(0, sibling, me).wait_recv()
        for j, chip in enumerate(chips):
            copy(4 + j, (*chip, 1 - c), me).wait_recv()
        for cp in first + passed:
            cp.wait_send()
        mine.wait()
        if with_sum:
            acc = out_ref[0:m, :]
            for k in range(1, N_DEV):
                acc = acc + out_ref[k * m:(k + 1) * m, :]
            sum_ref[...] = acc

    vm = pl.BlockSpec(memory_space=pltpu.VMEM)
    out_shape = [jax.ShapeDtypeStruct((N_DEV * m, n), F32)]
    if with_sum:
        out_shape.append(jax.ShapeDtypeStruct((m, n), F32))
    res = pl.pallas_call(
        body, name=name, in_specs=[vm], out_specs=[vm] * len(out_shape), out_shape=out_shape,
        scratch_shapes=[pltpu.SemaphoreType.DMA((7,)), pltpu.SemaphoreType.DMA((7,)), pltpu.SemaphoreType.DMA],
        compiler_params=pltpu.CompilerParams(vmem_limit_bytes=VMEM_LIMIT_V7X),
    )(xs)
    return res if with_sum else res[0]


BIG = (("w_in", (D_MODEL, IN_COLS), 1), ("w_rnn_out", (D_MODEL, D_MODEL), 0), ("w_na_out", (D_MODEL, D_MODEL), 0),
       ("w_out", (D_MODEL, D_MODEL), 0), ("w_up", (D_MODEL, 2 * D_FF), 1), ("w_down", (D_FF, D_MODEL), 0))


def _shard_shape(full, axis):
    r, c = full
    return (r // N_SHARD, c) if axis == 0 else (r, c // N_SHARD)


def _slot(ref, full, axis, s, h):
    r, c = full
    if axis == 0:
        rs = r // N_SHARD
        return ref.at[pl.ds(s * rs + h * (rs // 2), rs // 2), :]
    cs = c // N_SHARD
    return ref.at[pl.ds(h * (r // 2), r // 2), pl.ds(s * cs, cs)]


def cast_into_full(x, full, axis, idx, name):
    r, c = x.shape
    tr = next(t for t in (512, 352, 256, 128) if r % t == 0)
    nb = r // tr

    def body(idx_ref, x_ref, o_ref):
        o_ref[...] = x_ref[...].astype(BF16)

    if axis == 0:
        out_spec = pl.BlockSpec((tr, c), lambda i, idx_ref: (idx_ref[0] * nb + i, 0))
    else:
        out_spec = pl.BlockSpec((tr, c), lambda i, idx_ref: (i, idx_ref[0]))
    return pl.pallas_call(
        body, name=name,
        grid_spec=pltpu.PrefetchScalarGridSpec(
            num_scalar_prefetch=1, grid=(nb,), in_specs=[pl.BlockSpec((tr, c), lambda i, idx_ref: (i, 0))],
            out_specs=out_spec),
        out_shape=jax.ShapeDtypeStruct(full, BF16),
        compiler_params=_params("parallel"),
    )(idx, x)


def all_gather_weights(fulls):
    nw = len(BIG)

    def body(*refs):
        outs = refs[nw:2 * nw]
        send1, recv1, send2, recv2 = refs[2 * nw:]
        x, y, c = _mesh_pos()
        sibling = (x, y, 1 - c)
        chips = _other_chips(x, y)
        s_me = 2 * x + y

        def shard_of(chip):
            return 2 * chip[0] + chip[1]

        def ici(w, j, shard):
            _, full, axis = BIG[w]
            dst = _slot(outs[w], full, axis, shard, c)
            return pltpu.make_async_remote_copy(
                src_ref=dst, dst_ref=dst, send_sem=send1.at[3 * w + j],
                recv_sem=recv1.at[3 * w + j], device_id=(*chips[j], c), device_id_type=MESH_T)

        def d2d(w, j, shard, half):
            _, full, axis = BIG[w]
            dst = _slot(outs[w], full, axis, shard, half)
            return pltpu.make_async_remote_copy(
                src_ref=dst, dst_ref=dst, send_sem=send2.at[3 * w + j], recv_sem=recv2.at[3 * w + j],
                device_id=sibling, device_id_type=MESH_T)

        first = []
        for w in range(nw):
            for j in range(3):
                cp = ici(w, j, s_me)
                cp.start()
                first.append(cp)
        passed = []
        for w in range(nw):
            for j in range(3):
                sh = shard_of(chips[j])
                ici(w, j, sh).wait_recv()
                cp = d2d(w, j, sh, c)
                cp.start()
                passed.append(cp)
        for w in range(nw):
            for j in range(3):
                d2d(w, j, shard_of(chips[j]), 1 - c).wait_recv()
        for cp in first + passed:
            cp.wait_send()

    hbm = pl.BlockSpec(memory_space=pl.ANY)
    return pl.pallas_call(
        body, name="all_gather_weights", in_specs=[hbm] * nw, out_specs=[hbm] * nw,
        out_shape=[jax.ShapeDtypeStruct(full, BF16) for _, full, _ in BIG],
        input_output_aliases={i: i for i in range(nw)},
        scratch_shapes=[pltpu.SemaphoreType.DMA((3 * nw,))] * 4,
        compiler_params=pltpu.CompilerParams(vmem_limit_bytes=VMEM_LIMIT_V7X),
    )(*fulls)


def _grad_view(g, full, axis):
    r, c = full
    if axis == 0:
        return g.reshape(N_SHARD, 2, r // N_SHARD // 2, c)
    return g.reshape(1, 2, r // 2, c)


def exchange_halves(gviews):
    nw = len(BIG)

    def body(*refs):
        srcs, outs = refs[:nw], refs[nw:2 * nw]
        send_sems, recv_sems = refs[2 * nw:]
        x, y, c = _mesh_pos()
        cps = []
        for w in range(nw):
            cp = pltpu.make_async_remote_copy(
                src_ref=srcs[w].at[:, pl.ds(1 - c, 1)], dst_ref=outs[w], send_sem=send_sems.at[w],
                recv_sem=recv_sems.at[w], device_id=(x, y, 1 - c), device_id_type=MESH_T)
            cp.start()
            cps.append(cp)
        for cp in cps:
            cp.wait()

    hbm = pl.BlockSpec(memory_space=pl.ANY)
    return pl.pallas_call(
        body, name="grad_exchange_halves", in_specs=[hbm] * nw, out_specs=[hbm] * nw,
        out_shape=[jax.ShapeDtypeStruct((g.shape[0], 1) + g.shape[2:], BF16) for g in gviews],
        scratch_shapes=[pltpu.SemaphoreType.DMA((nw,)), pltpu.SemaphoreType.DMA((nw,))],
        compiler_params=pltpu.CompilerParams(vmem_limit_bytes=VMEM_LIMIT_V7X),
    )(*gviews)


def _row_tile(rh):
    return 128 if rh % 128 == 0 else rh


def add_halves(gview, recv, c_idx, name):
    a, _, rh, cc = gview.shape
    tr = _row_tile(rh)

    def body(c_ref, g_ref, r_ref, o_ref):
        o_ref[0] = (g_ref[0, 0].astype(F32) + r_ref[0, 0].astype(F32)).astype(BF16)

    return pl.pallas_call(
        body, name=name,
        grid_spec=pltpu.PrefetchScalarGridSpec(
            num_scalar_prefetch=1, grid=(a, rh // tr),
            in_specs=[pl.BlockSpec((1, 1, tr, cc), lambda s, i, c_ref: (s, c_ref[0], i, 0)),
                      pl.BlockSpec((1, 1, tr, cc), lambda s, i, c_ref: (s, 0, i, 0))],
            out_specs=pl.BlockSpec((1, tr, cc), lambda s, i, c_ref: (s, i, 0))),
        out_shape=jax.ShapeDtypeStruct((a, rh, cc), BF16),
        compiler_params=_params("parallel", "parallel"),
    )(c_idx, gview, recv)


def _piece_shape(full, axis):
    rs, cs = _shard_shape(full, axis)
    return (rs // 2, cs)


def scatter_pieces(partials):
    nw = len(BIG)

    def body(*refs):
        srcs, outs = refs[:nw], refs[nw:2 * nw]
        send_sems, recv_sems = refs[2 * nw:]
        x, y, c = _mesh_pos()
        chips = _other_chips(x, y)
        cps = []
        for w, (_, full, axis) in enumerate(BIG):
            cs = full[1] // N_SHARD
            for j, chip in enumerate(chips):
                s_j = 2 * chip[0] + chip[1]
                src = srcs[w].at[s_j] if axis == 0 else srcs[w].at[0, :, pl.ds(s_j * cs, cs)]
                cp = pltpu.make_async_remote_copy(
                    src_ref=src, dst_ref=outs[w].at[j], send_sem=send_sems.at[3 * w + j],
                    recv_sem=recv_sems.at[3 * w + j], device_id=(*chip, c), device_id_type=MESH_T)
                cp.start()
                cps.append(cp)
        for cp in cps:
            cp.wait()

    hbm = pl.BlockSpec(memory_space=pl.ANY)
    return pl.pallas_call(
        body, name="grad_scatter_pieces", in_specs=[hbm] * nw, out_specs=[hbm] * nw,
        out_shape=[jax.ShapeDtypeStruct((3,) + _piece_shape(full, axis), BF16) for _, full, axis in BIG],
        scratch_shapes=[pltpu.SemaphoreType.DMA((3 * nw,)), pltpu.SemaphoreType.DMA((3 * nw,))],
        compiler_params=pltpu.CompilerParams(vmem_limit_bytes=VMEM_LIMIT_V7X),
    )(*partials)


def add_pieces(partial, recv, idx, axis, name):
    _, rh, cs = recv.shape
    tr = _row_tile(rh)

    def body(idx_ref, p_ref, r_ref, o_ref):
        o_ref[0] = ((p_ref[0].astype(F32) + r_ref[0].astype(F32)) + r_ref[1].astype(F32)) + r_ref[2].astype(F32)

    if axis == 0:
        pspec = pl.BlockSpec((1, tr, cs), lambda i, idx_ref: (idx_ref[0], i, 0))
    else:
        pspec = pl.BlockSpec((1, tr, cs), lambda i, idx_ref: (0, i, idx_ref[0]))
    return pl.pallas_call(
        body, name=name,
        grid_spec=pltpu.PrefetchScalarGridSpec(
            num_scalar_prefetch=1, grid=(rh // tr,),
            in_specs=[pspec, pl.BlockSpec((3, tr, cs), lambda i, idx_ref: (0, i, 0))],
            out_specs=pl.BlockSpec((1, tr, cs), lambda i, idx_ref: (idx_ref[1], i, 0))),
        out_shape=jax.ShapeDtypeStruct((2, rh, cs), F32),
        compiler_params=_params("parallel"),
    )(idx, partial, recv)


def join_halves(halves):
    nw = len(BIG)

    def body(*refs):
        outs = refs[nw:2 * nw]
        send_sems, recv_sems = refs[2 * nw:]
        x, y, c = _mesh_pos()
        cps = []
        for w in range(nw):
            cp = pltpu.make_async_remote_copy(
                src_ref=outs[w].at[c], dst_ref=outs[w].at[c], send_sem=send_sems.at[w], recv_sem=recv_sems.at[w],
                device_id=(x, y, 1 - c), device_id_type=MESH_T)
            cp.start()
            cps.append(cp)
        for w in range(nw):
            cps[w].wait_send()
            pltpu.make_async_remote_copy(
                src_ref=outs[w].at[1 - c], dst_ref=outs[w].at[1 - c], send_sem=send_sems.at[w],
                recv_sem=recv_sems.at[w], device_id=(x, y, 1 - c), device_id_type=MESH_T).wait_recv()

    hbm = pl.BlockSpec(memory_space=pl.ANY)
    return pl.pallas_call(
        body, name="grad_join_halves", in_specs=[hbm] * nw, out_specs=[hbm] * nw,
        out_shape=[jax.ShapeDtypeStruct(h.shape, F32) for h in halves],
        input_output_aliases={i: i for i in range(nw)},
        scratch_shapes=[pltpu.SemaphoreType.DMA((nw,))] * 2,
        compiler_params=pltpu.CompilerParams(vmem_limit_bytes=VMEM_LIMIT_V7X),
    )(*halves)


MOD_COLS = N_MOD * D_MODEL // N_SHARD
MOD_TILE = 512


def mod_fwd(c16, w_mod):
    def body(c_ref, w_ref, s_ref, o_ref):
        cv = c_ref[...]
        s = cv * _sigmoid(cv)
        s_ref[...] = s
        o_ref[...] = jnp.dot(s.astype(BF16), w_ref[...].astype(BF16), preferred_element_type=F32)

    return pl.pallas_call(
        body, name="mod_fwd", grid=(MOD_COLS // MOD_TILE,),
        in_specs=[_full((16, D_MODEL)), pl.BlockSpec((D_MODEL, MOD_TILE), lambda j: (0, j))],
        out_specs=[_full((16, D_MODEL)), pl.BlockSpec((16, MOD_TILE), lambda j: (0, j))],
        out_shape=[jax.ShapeDtypeStruct((16, D_MODEL), F32), jax.ShapeDtypeStruct((16, MOD_COLS), F32)],
        compiler_params=_params("arbitrary"),
    )(c16, w_mod)


def mod_bwd(s16, dm16, w_mod):
    hi = lax.Precision.HIGHEST

    def body(s_ref, d_ref, w_ref, gw_ref, ds_ref):
        j = pl.program_id(0)
        dm = d_ref[...]
        gw_ref[...] = lax.dot_general(s_ref[...], dm, (((0,), (0,)), ((), ())), preferred_element_type=F32, precision=hi)
        part = lax.dot_general(dm, w_ref[...], (((1,), (1,)), ((), ())), preferred_element_type=F32, precision=hi)

        @pl.when(j == 0)
        def _():
            ds_ref[...] = part

        @pl.when(j > 0)
        def _():
            ds_ref[...] = ds_ref[...] + part

    return pl.pallas_call(
        body, name="mod_bwd", grid=(MOD_COLS // MOD_TILE,),
        in_specs=[_full((16, D_MODEL)), pl.BlockSpec((16, MOD_TILE), lambda j: (0, j)),
                  pl.BlockSpec((D_MODEL, MOD_TILE), lambda j: (0, j))],
        out_specs=[pl.BlockSpec((D_MODEL, MOD_TILE), lambda j: (0, j)), _full((16, D_MODEL))],
        out_shape=[jax.ShapeDtypeStruct((D_MODEL, MOD_COLS), F32), jax.ShapeDtypeStruct((16, D_MODEL), F32)],
        compiler_params=_params("arbitrary"),
    )(s16, dm16, w_mod)


def cctx_grad(parts, c_ctx):
    def body(p_ref, c_ref, o_ref):
        ds = p_ref[0:1, :]
        for s in range(1, N_SHARD):
            ds = ds + p_ref[16 * s:16 * s + 1, :]
        cv = c_ref[...]
        sg = _sigmoid(cv)
        o_ref[...] = ds * (sg * (1.0 + cv * (1.0 - sg)))

    return pl.pallas_call(
        body, name="cctx_grad", in_specs=[_full((N_DEV * 8, D_MODEL)), _full((1, D_MODEL))],
        out_specs=_full((1, D_MODEL)), out_shape=jax.ShapeDtypeStruct((1, D_MODEL), F32),
    )(parts, c_ctx)


def add_rows(a, b, name):
    def body(a_ref, b_ref, o_ref):
        o_ref[...] = a_ref[...] + b_ref[...]

    return pl.pallas_call(body, name=name, in_specs=[_full(a.shape), _full(b.shape)], out_specs=_full(a.shape),
                          out_shape=jax.ShapeDtypeStruct(a.shape, F32))(a, b)


def adamw(w, g, m, v, name):
    r, c = w.shape
    tr = 128 if (r % 128 == 0 and r > 128) else r

    def body(w_ref, g_ref, m_ref, v_ref, d_ref, nm_ref, nv_ref):
        g_ = g_ref[...]
        m_ = ADAM_B1 * m_ref[...] + (1.0 - ADAM_B1) * g_
        v_ = ADAM_B2 * v_ref[...] + (1.0 - ADAM_B2) * (g_ * g_)
        m_hat = m_ / (1.0 - ADAM_B1 ** ADAM_STEP)
        v_hat = v_ / (1.0 - ADAM_B2 ** ADAM_STEP)
        d_ref[...] = -ADAM_LR * (m_hat / (jnp.sqrt(v_hat) + ADAM_EPS) + ADAM_WD * w_ref[...])
        nm_ref[...] = m_
        nv_ref[...] = v_

    spec = pl.BlockSpec((tr, c), lambda i: (i, 0))
    shp = jax.ShapeDtypeStruct((r, c), F32)
    return pl.pallas_call(
        body, name=name, grid=(r // tr,), in_specs=[spec] * 4, out_specs=[spec] * 3, out_shape=[shp] * 3,
        compiler_params=_params("parallel"),
    )(w, g, m, v)


LANES = 1024


def _pack(arrs):
    rows, spans, at = [], [], 0
    for a in arrs:
        n = int(np.prod(a.shape))
        nr = 8 * -(-n // (8 * LANES))
        flat = a.reshape(-1)
        if nr * LANES != n:
            flat = jnp.concatenate([flat, jnp.zeros((nr * LANES - n,), F32)])
        rows.append(flat.reshape(nr, LANES))
        spans.append((at, nr, n, a.shape))
        at += nr
    return jnp.concatenate(rows, axis=0), spans


def _unpack(buf, spans):
    out = []
    for at, nr, n, shape in spans:
        out.append(buf[at:at + nr].reshape(-1)[:n].reshape(shape))
    return out


SMALL_SHARD = ("lru_conv_w", "lru_ba", "lru_bx", "lru_lambda", "ffn_conv_w")


def kernel(x, c, ctx, c_ctx, w_mod, b_mod, norm_mix_g, norm_ffn_g, w_in, lru_conv_w, lru_conv_b, lru_wa, lru_ba, lru_wx, lru_bx, lru_lambda, q_norm_g, k_norm_g, na_rpb, w_rnn_out, w_na_out, w_out, w_up, ffn_conv_w, ffn_conv_b, w_down, loss_target, m_c_ctx, m_w_mod, m_b_mod, m_norm_mix_g, m_norm_ffn_g, m_w_in, m_lru_conv_w, m_lru_conv_b, m_lru_wa, m_lru_ba, m_lru_wx, m_lru_bx, m_lru_lambda, m_q_norm_g, m_k_norm_g, m_na_rpb, m_w_rnn_out, m_w_na_out, m_w_out, m_w_up, m_ffn_conv_w, m_ffn_conv_b, m_w_down, v_c_ctx, v_w_mod, v_b_mod, v_norm_mix_g, v_norm_ffn_g, v_w_in, v_lru_conv_w, v_lru_conv_b, v_lru_wa, v_lru_ba, v_lru_wx, v_lru_bx, v_lru_lambda, v_q_norm_g, v_k_norm_g, v_na_rpb, v_w_rnn_out, v_w_na_out, v_w_out, v_w_up, v_ffn_conv_w, v_ffn_conv_b, v_w_down):
    weights = dict(c_ctx=c_ctx, w_mod=w_mod, b_mod=b_mod, norm_mix_g=norm_mix_g, norm_ffn_g=norm_ffn_g, w_in=w_in,
                   lru_conv_w=lru_conv_w, lru_conv_b=lru_conv_b, lru_wa=lru_wa, lru_ba=lru_ba, lru_wx=lru_wx,
                   lru_bx=lru_bx, lru_lambda=lru_lambda, q_norm_g=q_norm_g, k_norm_g=k_norm_g, na_rpb=na_rpb,
                   w_rnn_out=w_rnn_out, w_na_out=w_na_out, w_out=w_out, w_up=w_up, ffn_conv_w=ffn_conv_w,
                   ffn_conv_b=ffn_conv_b, w_down=w_down)
    mom1 = dict(c_ctx=m_c_ctx, w_mod=m_w_mod, b_mod=m_b_mod, norm_mix_g=m_norm_mix_g, norm_ffn_g=m_norm_ffn_g,
                w_in=m_w_in, lru_conv_w=m_lru_conv_w, lru_conv_b=m_lru_conv_b, lru_wa=m_lru_wa, lru_ba=m_lru_ba,
                lru_wx=m_lru_wx, lru_bx=m_lru_bx, lru_lambda=m_lru_lambda, q_norm_g=m_q_norm_g, k_norm_g=m_k_norm_g,
                na_rpb=m_na_rpb, w_rnn_out=m_w_rnn_out, w_na_out=m_w_na_out, w_out=m_w_out, w_up=m_w_up,
                ffn_conv_w=m_ffn_conv_w, ffn_conv_b=m_ffn_conv_b, w_down=m_w_down)
    mom2 = dict(c_ctx=v_c_ctx, w_mod=v_w_mod, b_mod=v_b_mod, norm_mix_g=v_norm_mix_g, norm_ffn_g=v_norm_ffn_g,
                w_in=v_w_in, lru_conv_w=v_lru_conv_w, lru_conv_b=v_lru_conv_b, lru_wa=v_lru_wa, lru_ba=v_lru_ba,
                lru_wx=v_lru_wx, lru_bx=v_lru_bx, lru_lambda=v_lru_lambda, q_norm_g=v_q_norm_g, k_norm_g=v_k_norm_g,
                na_rpb=v_na_rpb, w_rnn_out=v_w_rnn_out, w_na_out=v_w_na_out, w_out=v_w_out, w_up=v_w_up,
                ffn_conv_w=v_ffn_conv_w, ffn_conv_b=v_ffn_conv_b, w_down=v_w_down)
    order = list(weights)
    d = D_MODEL
    mx_, my_, mc_ = _mesh_pos()
    shard = 2 * mx_ + my_
    dev = 2 * shard + mc_

    local_small, small_spans = _pack([c] + [weights[k][0] for k in SMALL_SHARD])
    gath = all_gather8(local_small, "gather_small").reshape(N_DEV, local_small.shape[0], LANES)
    per_dev = [_unpack(gath[k], small_spans) for k in range(N_DEV)]
    c_all = jnp.concatenate([per_dev[k][0] for k in range(N_DEV)], axis=0)
    full_small = {name: jnp.concatenate([per_dev[2 * s][1 + i] for s in range(N_SHARD)], axis=-1)
                  for i, name in enumerate(SMALL_SHARD)}
    c16 = jnp.concatenate([c_all, c_ctx.reshape(1, d), jnp.zeros((7, d), F32)], axis=0)
    s16, mod_part = mod_fwd(c16, w_mod[0])
    mod_all = all_gather8(mod_part, "gather_mod").reshape(N_DEV, 16, MOD_COLS)
    mod = jnp.concatenate([mod_all[2 * s] for s in range(N_SHARD)], axis=1) + b_mod
    modx = lax.dynamic_slice(mod, (dev, 0), (1, N_MOD * d))
    modc = mod[8:9]

    idx = jnp.stack([shard, mc_]).astype(jnp.int32)
    big_full = all_gather_weights([cast_into_full(weights[name][0], full, axis, idx, "cast_" + name)
                                   for name, full, axis in BIG])
    wfull = {name: big_full[i] for i, (name, _, _) in enumerate(BIG)}

    z = jnp.concatenate([ctx[0], x[0]], axis=0)
    res = local_step(z, loss_target[0], modx, modc, norm_mix_g, norm_ffn_g, wfull["w_in"], full_small["lru_conv_w"],
                     lru_conv_b, lru_wa[0], full_small["lru_ba"], lru_wx[0], full_small["lru_bx"],
                     full_small["lru_lambda"], q_norm_g, k_norm_g, na_rpb[0], wfull["w_rnn_out"], wfull["w_na_out"],
                     wfull["w_out"], wfull["w_up"], full_small["ffn_conv_w"], ffn_conv_b, wfull["w_down"])

    c_idx = jnp.reshape(mc_, (1,)).astype(jnp.int32)
    gviews = [_grad_view(res[name], full, axis) for name, full, axis in BIG]
    recv1 = exchange_halves(gviews)
    partials = [add_halves(gviews[i], recv1[i], c_idx, "add_halves_" + BIG[i][0]) for i in range(len(BIG))]
    recv2 = scatter_pieces(partials)
    halves = [add_pieces(partials[i], recv2[i], idx, BIG[i][2], "add_pieces_" + BIG[i][0]) for i in range(len(BIG))]
    joined = join_halves(halves)
    grads = {name: joined[i].reshape(_shard_shape(full, axis)) for i, (name, full, axis) in enumerate(BIG)}

    small_names = ["norm_mix_g", "norm_ffn_g", "lru_conv_w", "lru_conv_b", "lru_wa", "lru_ba", "lru_wx", "lru_bx",
                   "lru_lambda", "q_norm_g", "k_norm_g", "na_rpb", "ffn_conv_w", "ffn_conv_b"]
    local_g, g_spans = _pack([res["loss_sq"][0:1, 0:1], res["d_modx"], res["d_modc"]] + [res[k] for k in small_names])
    n_rows = local_g.shape[0]
    g_all, g_tot = all_gather8(local_g, "allreduce_small", with_sum=True)
    tot = _unpack(g_tot, g_spans)
    loss = (0.5 / d) * tot[0][0, 0]
    small_tot = dict(zip(small_names, tot[3:]))
    at_x = g_spans[1][0]
    dmx_rows = g_all.reshape(N_DEV, n_rows, LANES)[:, at_x:at_x + N_MOD, :].reshape(N_DEV, N_MOD * d)
    dmc_row = jnp.concatenate([tot[2], jnp.zeros((1, 4 * d), F32)], axis=1)
    dm16 = jnp.concatenate([dmx_rows, dmc_row, jnp.zeros((7, N_MOD * d), F32)], axis=0)
    grads["b_mod"] = add_rows(tot[1], dmc_row, "b_mod_grad")
    g_w_mod, ds16 = mod_bwd(s16, lax.dynamic_slice(dm16, (0, shard * MOD_COLS), (16, MOD_COLS)), w_mod[0])
    grads["w_mod"] = g_w_mod
    ds_parts = all_gather8(ds16[8:16], "gather_dsctx")
    grads["c_ctx"] = cctx_grad(ds_parts, c_ctx.reshape(1, d))
    for k in small_names:
        g = small_tot[k]
        if k in SMALL_SHARD:
            w_sh = weights[k].shape[-1]
            g = lax.dynamic_slice_in_dim(g, shard * w_sh, w_sh, axis=g.ndim - 1)
        grads[k] = g

    delta, new_m, new_v = {}, {}, {}
    for name, _, _ in BIG + (("w_mod", None, None),):
        delta[name], new_m[name], new_v[name] = adamw(weights[name][0], grads[name], mom1[name][0], mom2[name][0],
                                                      "adamw_" + name)
    rest = [k for k in order if k not in delta]
    pw, spans_w = _pack([weights[k] for k in rest])
    pg, _ = _pack([grads[k].reshape(weights[k].shape) for k in rest])
    pm, _ = _pack([mom1[k] for k in rest])
    pv, _ = _pack([mom2[k] for k in rest])
    pd, pnm, pnv = adamw(pw, pg, pm, pv, "adamw_small")
    for k, a, b_, c_ in zip(rest, _unpack(pd, spans_w), _unpack(pnm, spans_w), _unpack(pnv, spans_w)):
        delta[k], new_m[k], new_v[k] = a, b_, c_

    shaped = lambda t: [t[k].reshape(weights[k].shape) for k in order]
    return (loss, res["grad_x"][None], *shaped(grads), *shaped(delta), *shaped(new_m), *shaped(new_v))
```

```python
import numpy as np
import jax
import jax.numpy as jnp
from jax import lax
from jax.experimental import pallas as pl
from jax.experimental.pallas import tpu as pltpu

F32 = jnp.float32
BF16 = jnp.bfloat16

D_MODEL = 1024
SEQ = 2048
CTX_LEN = 256
ZLEN = SEQ + CTX_LEN
GRID_W = 64
GRID_ROWS = SEQ // GRID_W
LRU_BLOCK_W = 128
LRU_BLOCKS = 8
LRU_C = 8.0
NA_HEADS = 16
HEAD_DIM = 64
NA_ROWS = 8
NA_COLS = 16
ROPE_BASE = 10000.0
D_FF = 2816
N_MOD = 6
IN_COLS = 7 * D_MODEL
EPS = 1e-6
NEG_INF = -1e30
N_DEV = 8
N_SHARD = 4

ADAM_LR = 0.001
ADAM_B1 = 0.9
ADAM_B2 = 0.999
ADAM_EPS = 1e-08
ADAM_WD = 0.01
ADAM_STEP = 10

ROW_TILE = 256
Q_ROWS = 4
Q_TILE = Q_ROWS * GRID_W
KEY_ROWS = 12
KEY_TILE = KEY_ROWS * GRID_W
BT_PAD = 4
BT_LEN = 24
VMEM_LIMIT_V7X = 56 * 1024 * 1024

MESH_T = pl.DeviceIdType.MESH


def _params(*sem):
    return pltpu.CompilerParams(dimension_semantics=sem if sem else None, vmem_limit_bytes=VMEM_LIMIT_V7X)


def _full(shape):
    nd = len(shape)
    return pl.BlockSpec(shape, lambda *_: (0,) * nd)


class Comm:
    def __init__(self, inputs, out_shapes, aliases, scratch, emit):
        self.inputs, self.out_shapes, self.aliases, self.scratch, self.emit = inputs, out_shapes, aliases, scratch, emit


def _call(body, *, name, grid, in_specs, out_specs, out_shape, args, scratch_shapes=(), sem=(), comm=None):
    n_in, n_out, n_sc = len(in_specs), len(out_specs), len(scratch_shapes)
    if comm is None:
        res = pl.pallas_call(body, name=name, grid=grid, in_specs=list(in_specs), out_specs=list(out_specs),
                             out_shape=list(out_shape), scratch_shapes=list(scratch_shapes),
                             compiler_params=_params(*sem))(*args)
        return list(res), []
    k_in, k_out = len(comm.inputs), len(comm.out_shapes)
    steps = int(np.prod(grid))

    def hosted(*refs):
        ins, cins = refs[:n_in], refs[n_in:n_in + k_in]
        at = n_in + k_in
        outs, couts = refs[at:at + n_out], refs[at + n_out:at + n_out + k_out]
        at += n_out + k_out
        scr, cscr = refs[at:at + n_sc], refs[at + n_sc:]
        start, mid, end = comm.emit(cins, couts, cscr)
        lin = pl.program_id(0)
        for ax in range(1, len(grid)):
            lin = lin * grid[ax] + pl.program_id(ax)
        pl.when(lin == 0)(start)
        body(*ins, *outs, *scr)
        pl.when(lin == steps // 2)(mid)
        pl.when(lin == steps - 1)(end)

    hbm = pl.BlockSpec(memory_space=pl.ANY)
    res = pl.pallas_call(
        hosted, name=name, grid=grid, in_specs=list(in_specs) + [hbm] * k_in, out_specs=list(out_specs) + [hbm] * k_out,
        out_shape=list(out_shape) + list(comm.out_shapes), scratch_shapes=list(scratch_shapes) + list(comm.scratch),
        input_output_aliases={n_in + i: n_out + o for i, o in comm.aliases.items()},
        compiler_params=_params(*(("arbitrary",) * len(grid))))(*args, *comm.inputs)
    return list(res[:n_out]), list(res[n_out:])


def _sigmoid(x):
    return 1.0 / (1.0 + jnp.exp(-x))


def _gelu_parts(x):
    c0 = 0.7978845608028654
    inner = c0 * (x + 0.044715 * x * x * x)
    t = jnp.tanh(inner)
    g = 0.5 * x * (1.0 + t)
    dg = 0.5 * (1.0 + t) + 0.5 * x * (1.0 - t * t) * c0 * (1.0 + 3.0 * 0.044715 * x * x)
    return g, dg


def _dot_nt(a, b):
    return lax.dot_general(a, b, (((1,), (1,)), ((), ())), preferred_element_type=F32)


def _dot_tn(a, b):
    return lax.dot_general(a, b, (((0,), (0,)), ((), ())), preferred_element_type=F32)


def norm_mod(xin, gain, shift, scale, name):
    r, d = xin.shape
    s_mod = shift.shape[0]
    assert r % ROW_TILE == 0

    def body(x_ref, g_ref, sh_ref, sc_ref, xn_ref):
        x = x_ref[...]
        nrm = x * lax.rsqrt(jnp.mean(x * x, axis=-1, keepdims=True) + EPS)
        xn_ref[...] = ((nrm * g_ref[...]) * (1.0 + sc_ref[0]) + sh_ref[0]).astype(BF16)

    mod_spec = pl.BlockSpec((1, 1, d), lambda i: (jnp.minimum(i, s_mod - 1), 0, 0))
    return pl.pallas_call(
        body, name=name, grid=(r // ROW_TILE,),
        in_specs=[pl.BlockSpec((ROW_TILE, d), lambda i: (i, 0)), _full((1, d)), mod_spec, mod_spec],
        out_specs=pl.BlockSpec((ROW_TILE, d), lambda i: (i, 0)),
        out_shape=jax.ShapeDtypeStruct((r, d), BF16),
        compiler_params=_params("parallel"),
    )(xin, gain, shift, scale)


def matmul_wide(a, b, name, tm, tn, comm=None):
    m, k = a.shape
    n = b.shape[1]
    assert m % tm == 0 and n % tn == 0

    def body(a_ref, b_ref, o_ref):
        o_ref[...] = jnp.dot(a_ref[...], b_ref[...], preferred_element_type=F32)

    res, extra = _call(
        body, name=name, grid=(n // tn, m // tm),
        in_specs=[pl.BlockSpec((tm, k), lambda j, i: (i, 0)), pl.BlockSpec((k, tn), lambda j, i: (0, j))],
        out_specs=[pl.BlockSpec((tm, tn), lambda j, i: (i, j))],
        out_shape=[jax.ShapeDtypeStruct((m, n), F32)],
        sem=("parallel", "parallel"), args=(a, b), comm=comm)
    return res[0], extra


def _row_ids(n, w):
    return lax.broadcasted_iota(jnp.int32, (n, w), 0)


def _lru_conv(xr, cw, cb):
    row = _row_ids(ZLEN, LRU_BLOCK_W)
    segpos = jnp.where(row < CTX_LEN, row, row - CTX_LEN)
    seglen = jnp.where(row < CTX_LEN, CTX_LEN, SEQ)
    acc = xr * cw[2:3, :] + cb
    for k in (0, 1, 3):
        off = k - 2
        sh = pltpu.roll(xr, (-off) % ZLEN, 0)
        ok = (segpos + off >= 0) & (segpos + off < seglen)
        acc = acc + jnp.where(ok, sh, 0.0) * cw[k:k + 1, :]
    return acc


def _lru_conv_t(dxc, cw):
    row = _row_ids(ZLEN, LRU_BLOCK_W)
    segpos = jnp.where(row < CTX_LEN, row, row - CTX_LEN)
    seglen = jnp.where(row < CTX_LEN, CTX_LEN, SEQ)
    acc = dxc * cw[2:3, :]
    for k in (0, 1, 3):
        off = k - 2
        sh = pltpu.roll(dxc, off % ZLEN, 0)
        ok = (segpos - off >= 0) & (segpos - off < seglen)
        acc = acc + jnp.where(ok, sh, 0.0) * cw[k:k + 1, :]
    return acc


def _lru_gates(xc, xcb, wa, ba, wx, bx, lam):
    r = _sigmoid(jnp.dot(xcb, wa, preferred_element_type=F32) + ba)
    i = _sigmoid(jnp.dot(xcb, wx, preferred_element_type=F32) + bx)
    sp = jnp.maximum(-lam, 0.0) + jnp.log1p(jnp.exp(-jnp.abs(lam)))
    la = (-LRU_C) * r * sp
    a = jnp.exp(la)
    sq = jnp.sqrt(-jnp.tanh(la) * (1.0 + a * a))
    b = sq * i * xc
    return r, i, sp, a, sq, b


def _scan8_fwd(a, b, rid):
    for s in (1, 2, 4):
        a_s = pltpu.roll(a, s, 0)
        b_s = pltpu.roll(b, s, 0)
        m = rid >= s
        b = jnp.where(m, a * b_s + b, b)
        a = jnp.where(m, a * a_s, a)
    return a, b


def _scan8_rev(a, b, rid):
    for s in (1, 2, 4):
        a_s = pltpu.roll(a, 8 - s, 0)
        b_s = pltpu.roll(b, 8 - s, 0)
        m = rid < 8 - s
        b = jnp.where(m, a * b_s + b, b)
        a = jnp.where(m, a * a_s, a)
    return a, b


N_CHUNK = ZLEN // 8
CTX_CHUNKS = CTX_LEN // 8
SCAN_UNROLL = 8


def _scan_up(a_ref, b_ref, h_ref, lo, hi, carry):
    rid = _row_ids(8, LRU_BLOCK_W)
    assert (hi - lo) % SCAN_UNROLL == 0

    def step(g, c):
        base = pl.multiple_of((lo + g * SCAN_UNROLL) * 8, 8)
        for u in range(SCAN_UNROLL):
            sl = pl.ds(base + 8 * u, 8)
            a, b = _scan8_fwd(a_ref[sl, :], b_ref[sl, :], rid)
            h = b + a * c
            h_ref[sl, :] = h
            c = h[7:8, :]
        return c

    return lax.fori_loop(0, (hi - lo) // SCAN_UNROLL, step, carry)


def _scan_down(a_ref, b_ref, h_ref, lo, hi, carry):
    rid = _row_ids(8, LRU_BLOCK_W)
    assert (hi - lo) % SCAN_UNROLL == 0

    def step(g, c):
        base = pl.multiple_of((hi - (g + 1) * SCAN_UNROLL) * 8, 8)
        for u in reversed(range(SCAN_UNROLL)):
            sl = pl.ds(base + 8 * u, 8)
            a, b = _scan8_rev(a_ref[sl, :], b_ref[sl, :], rid)
            h = b + a * c
            h_ref[sl, :] = h
            c = h[0:1, :]
        return c

    return lax.fori_loop(0, (hi - lo) // SCAN_UNROLL, step, carry)


def _lru_scan_dir(d, a_ref, b_ref, h_ref):
    zero = jnp.zeros((1, LRU_BLOCK_W), F32)
    if d == 0:
        _scan_up(a_ref, b_ref, h_ref, 0, N_CHUNK, zero)
    else:
        c = _scan_down(a_ref, b_ref, h_ref, 0, CTX_CHUNKS, zero)
        _scan_down(a_ref, b_ref, h_ref, CTX_CHUNKS, N_CHUNK, c)


def _lru_in_specs():
    blk = lambda rows: pl.BlockSpec((rows, LRU_BLOCK_W), lambda b: (0, b))
    wspec = pl.BlockSpec((2, 1, LRU_BLOCK_W, LRU_BLOCK_W), lambda b: (0, b, 0, 0))
    return blk, wspec


def lru_fwd(p, conv_w, conv_b, wa, ba, wx, bx, lam, comm=None):
    blk, wspec = _lru_in_specs()

    def body(xr_ref, gx_ref, cw_ref, cb_ref, wa_ref, ba_ref, wx_ref, bx_ref, lam_ref, y_ref, a_s, b_s, h_s, hsum_s):
        xr = xr_ref[...]
        xc = _lru_conv(xr, cw_ref[...], cb_ref[...])
        xcb = xc.astype(BF16)
        for d in (0, 1):
            _, _, _, a, _, b = _lru_gates(xc, xcb, wa_ref[d, 0].astype(BF16), ba_ref[d:d + 1, :],
                                          wx_ref[d, 0].astype(BF16), bx_ref[d:d + 1, :], lam_ref[d:d + 1, :])
            a_s[...] = a
            b_s[...] = b
            _lru_scan_dir(d, a_s, b_s, h_s)
            if d == 0:
                hsum_s[...] = h_s[...]
            else:
                hsum_s[...] = hsum_s[...] + h_s[...]
        g, _ = _gelu_parts(gx_ref[CTX_LEN:, :])
        y_ref[...] = (hsum_s[CTX_LEN:, :] * g).astype(BF16)

    zs = pltpu.VMEM((ZLEN, LRU_BLOCK_W), F32)
    res, extra = _call(
        body, name="lru_fwd", grid=(LRU_BLOCKS,),
        in_specs=[blk(ZLEN), pl.BlockSpec((ZLEN, LRU_BLOCK_W), lambda b: (0, 24 + b)), blk(4), blk(1),
                  wspec, blk(2), wspec, blk(2), blk(2)],
        out_specs=[pl.BlockSpec((SEQ, LRU_BLOCK_W), lambda b: (0, b))],
        out_shape=[jax.ShapeDtypeStruct((SEQ, D_MODEL), BF16)],
        scratch_shapes=[zs, zs, zs, zs], sem=("arbitrary",),
        args=(p, p, conv_w, conv_b, wa, ba, wx, bx, lam), comm=comm)
    return res[0], extra


def _rope_tables():
    t = np.arange(SEQ)
    lane = np.arange(2 * HEAD_DIM)
    in_head = lane % HEAD_DIM
    j = (in_head % 32) % 16
    freq = ROPE_BASE ** (-j.astype(np.float64) / 16.0)
    pos = np.where(in_head[None, :] < 32, (t // GRID_W)[:, None], (t % GRID_W)[:, None]).astype(np.float64)
    ang = (pos.astype(np.float32) * freq.astype(np.float32)[None, :]).astype(np.float32)
    cos = np.cos(ang).astype(np.float32)
    sin = np.sin(ang).astype(np.float32)
    sgn = np.where((in_head % 32) < 16, -1.0, 1.0).astype(np.float32)
    cos = np.concatenate([np.ones((CTX_LEN, 2 * HEAD_DIM), np.float32), cos], 0)
    sin = np.concatenate([np.zeros((CTX_LEN, 2 * HEAD_DIM), np.float32), sin * sgn[None, :]], 0)
    return jnp.asarray(cos), jnp.asarray(sin)


def _head_ones():
    lane = np.arange(2 * HEAD_DIM)
    return jnp.asarray((lane[:, None] // HEAD_DIM == lane[None, :] // HEAD_DIM).astype(np.float32))


def _rope_partner(x):
    lane = lax.broadcasted_iota(jnp.int32, x.shape, 1)
    return jnp.where((lane % 32) < 16, pltpu.roll(x, 128 - 16, 1), pltpu.roll(x, 16, 1))


def _head_rms(x, ones, gain):
    ms = jnp.dot(x * x, ones, preferred_element_type=F32, precision=lax.Precision.HIGHEST) * (1.0 / HEAD_DIM)
    rstd = lax.rsqrt(ms + EPS)
    return x * rstd * gain, rstd


PREP_TILE = 768


def qkv_prep(p, qg2, kg2, cos, sin, ones):
    scale = HEAD_DIM ** -0.5

    def body(q_ref, k_ref, v_ref, qg_ref, kg_ref, cos_ref, sin_ref, ones_ref, qr_ref, qp_ref, kk_ref, vv_ref):
        ones_m = ones_ref[...]
        c, s = cos_ref[...], sin_ref[...]
        qn, _ = _head_rms(q_ref[...], ones_m, qg_ref[...])
        qn = qn * scale
        qr_ref[...] = (qn * c + _rope_partner(qn) * s).astype(BF16)
        qp_ref[...] = qn.astype(BF16)
        kn, _ = _head_rms(k_ref[...], ones_m, kg_ref[...])
        kk_ref[...] = (kn * c + _rope_partner(kn) * s).astype(BF16)
        vv_ref[...] = v_ref[...].astype(BF16)

    col = lambda base: pl.BlockSpec((PREP_TILE, 128), lambda hp, i: (i, base + hp))
    small = pl.BlockSpec((1, 128), lambda hp, i: (0, 0))
    tab = pl.BlockSpec((PREP_TILE, 128), lambda hp, i: (i, 0))
    oshape = jax.ShapeDtypeStruct((ZLEN, D_MODEL), BF16)
    return pl.pallas_call(
        body, name="qkv_prep", grid=(NA_HEADS // 2, ZLEN // PREP_TILE),
        in_specs=[col(32), col(8), col(16), small, small, tab, tab, _full((128, 128))],
        out_specs=[col(0)] * 4, out_shape=[oshape] * 4,
        compiler_params=_params("parallel", "parallel"),
    )(p, p, p, qg2, kg2, cos, sin, ones)


def _bias_expand():
    qc = np.arange(GRID_W)[:, None]
    kc = np.arange(GRID_W)[None, :]
    col_start = np.clip(qc - NA_COLS // 2, 0, GRID_W - NA_COLS)
    in_win = (kc >= col_start) & (kc < col_start + NA_COLS)
    dc = np.clip(kc - qc, -(NA_COLS - 1), NA_COLS - 1) + (NA_COLS - 1)
    e = np.zeros((2 * NA_COLS - 1, GRID_W, GRID_W), np.float32)
    for d in range(2 * NA_COLS - 1):
        e[d] = ((dc == d) & in_win).astype(np.float32)
    pen = np.where(in_win, 0.0, NEG_INF).astype(np.float32)
    return e, pen


def bias_table(rpb2):
    e, pen = _bias_expand()
    n_dr = 2 * NA_ROWS - 1
    ea = np.zeros((31, GRID_W, 128), np.float32)
    ea[:, :, :GRID_W] = e
    eb = np.zeros((31, GRID_W, 128), np.float32)
    eb[:, :, GRID_W:] = e
    pen2 = np.concatenate([pen, pen], 1)
    ea = jnp.asarray(ea.reshape(31, GRID_W * 128))
    eb = jnp.asarray(eb.reshape(31, GRID_W * 128))
    sel_a = np.zeros((BT_LEN, n_dr), np.float32)
    sel_b = np.zeros((BT_LEN, n_dr), np.float32)
    for r in range(BT_LEN):
        dr = r - BT_PAD
        if 0 <= dr < n_dr:
            sel_a[r, dr] = 1.0
        if 0 <= dr + 1 < n_dr:
            sel_b[r, dr + 1] = 1.0
    sel_a, sel_b = jnp.asarray(sel_a), jnp.asarray(sel_b)
    pen2 = jnp.asarray(pen2.reshape(1, GRID_W * 128))
    hi = lax.Precision.HIGHEST

    def body(rpb_ref, sa_ref, sb_ref, ea_ref, eb_ref, pen_ref, o_ref, ra_s, rb_s):
        for h in range(NA_HEADS):
            rp = rpb_ref[h]
            ra_s[h * BT_LEN:(h + 1) * BT_LEN, :] = jnp.dot(sa_ref[...], rp, preferred_element_type=F32, precision=hi)
            rb_s[h * BT_LEN:(h + 1) * BT_LEN, :] = jnp.dot(sb_ref[...], rp, preferred_element_type=F32, precision=hi)
        o_ref[...] = (jnp.dot(ra_s[...], ea_ref[...], preferred_element_type=F32, precision=hi)
                      + jnp.dot(rb_s[...], eb_ref[...], preferred_element_type=F32, precision=hi) + pen_ref[...])

    tcol = 2048
    rows = NA_HEADS * BT_LEN
    out = pl.pallas_call(
        body, name="bias_table", grid=(GRID_W * 128 // tcol,),
        in_specs=[_full((NA_HEADS, n_dr, 31)), _full((BT_LEN, n_dr)), _full((BT_LEN, n_dr)),
                  pl.BlockSpec((31, tcol), lambda j: (0, j)), pl.BlockSpec((31, tcol), lambda j: (0, j)),
                  pl.BlockSpec((1, tcol), lambda j: (0, j))],
        out_specs=pl.BlockSpec((rows, tcol), lambda j: (0, j)),
        out_shape=jax.ShapeDtypeStruct((rows, GRID_W * 128), F32),
        scratch_shapes=[pltpu.VMEM((rows, 31), F32), pltpu.VMEM((rows, 31), F32)],
        compiler_params=_params("parallel"),
    )(rpb2, sel_a, sel_b, ea, eb, pen2)
    return out.reshape(NA_HEADS, BT_LEN, GRID_W, 128)


def _key_window(j):
    ws = jnp.clip(Q_ROWS * j - 4, 0, GRID_ROWS - KEY_ROWS)
    return ws, pl.multiple_of(CTX_LEN + ws * GRID_W, 256)


def _head_mask(hh):
    lane = lax.broadcasted_iota(jnp.int32, (Q_TILE, 128), 1)
    return (lane < HEAD_DIM) if hh == 0 else (lane >= HEAD_DIM)


def _attn_scores(j, ws, q_rot_h, q_pl_h, kw, kc, hh, bt_ref, s_ref):
    s_ref[:, :KEY_TILE] = _dot_nt(q_rot_h, kw)
    s_ref[:, KEY_TILE:] = _dot_nt(q_pl_h, kc)
    lane = lax.broadcasted_iota(jnp.int32, (GRID_W, 128), 1)
    base = ws - Q_ROWS * j + (NA_ROWS - 1) + BT_PAD
    for qi in range(Q_ROWS):
        rs = jnp.clip(Q_ROWS * j + qi - NA_ROWS // 2, 0, GRID_ROWS - NA_ROWS)
        for m in range(KEY_ROWS // 2):
            k0 = ws + 2 * m
            p0 = jnp.where((k0 >= rs) & (k0 < rs + NA_ROWS), 0.0, NEG_INF)
            p1 = jnp.where((k0 + 1 >= rs) & (k0 + 1 < rs + NA_ROWS), 0.0, NEG_INF)
            pen = jnp.where(lane < GRID_W, p0, p1)
            rows = slice(qi * GRID_W, (qi + 1) * GRID_W)
            cols = slice(128 * m, 128 * (m + 1))
            s_ref[rows, cols] = s_ref[rows, cols] + bt_ref[hh, base + 2 * m - qi] + pen
    return base


def attn_fwd(q_rot, q_pl, kk, vv, bt, comm=None):
    def body(qr_ref, qp_ref, kk_ref, vv_ref, bt_ref, o_ref, lse_ref, s_ref):
        j = pl.program_id(1)
        ws, start = _key_window(j)
        win = pl.ds(start, KEY_TILE)
        kw, kc = kk_ref[win, :], kk_ref[:CTX_LEN, :]
        vw, vc = vv_ref[win, :], vv_ref[:CTX_LEN, :]
        qr, qp = qr_ref[...], qp_ref[...]
        outs = []
        for hh in range(2):
            msk = _head_mask(hh)
            _attn_scores(j, ws, jnp.where(msk, qr, 0), jnp.where(msk, qp, 0), kw, kc, hh, bt_ref, s_ref)
            s = s_ref[...]
            mx = jnp.max(s, axis=-1, keepdims=True)
            pr = jnp.exp(s - mx)
            l = jnp.sum(pr, axis=-1, keepdims=True)
            prb = pr.astype(BF16)
            o = jnp.dot(prb[:, :KEY_TILE], vw, preferred_element_type=F32)
            o = o + jnp.dot(prb[:, KEY_TILE:], vc, preferred_element_type=F32)
            outs.append(o / l)
            lse_ref[hh] = mx + jnp.log(l)
        o_ref[...] = jnp.where(_head_mask(0), outs[0], outs[1])

    qspec = pl.BlockSpec((Q_TILE, 128), lambda hp, j: (j + 1, hp))
    kspec = pl.BlockSpec((ZLEN, 128), lambda hp, j: (0, hp))
    res, extra = _call(
        body, name="attn_fwd", grid=(NA_HEADS // 2, SEQ // Q_TILE),
        in_specs=[qspec, qspec, kspec, kspec, pl.BlockSpec((2, BT_LEN, GRID_W, 128), lambda hp, j: (hp, 0, 0, 0))],
        out_specs=[pl.BlockSpec((Q_TILE, 128), lambda hp, j: (j, hp)),
                   pl.BlockSpec((2, Q_TILE, 1), lambda hp, j: (hp, j, 0))],
        out_shape=[jax.ShapeDtypeStruct((SEQ, D_MODEL), F32), jax.ShapeDtypeStruct((NA_HEADS, SEQ, 1), F32)],
        scratch_shapes=[pltpu.VMEM((Q_TILE, KEY_TILE + CTX_LEN), F32)], sem=("parallel", "arbitrary"),
        args=(q_rot, q_pl, kk, vv, bt), comm=comm)
    return res[0], res[1], extra


def merge_fwd(y_rnn, y_na, p, z, g2, w_rnn, w_na, w_out):
    def body(yr_ref, yn_ref, mr_ref, mn_ref, x_ref, g2_ref, wr_ref, wn_ref, wo_ref, u_ref, v_ref, mg_ref, out_ref, x1_ref):
        u = jnp.dot(yr_ref[...], wr_ref[...], preferred_element_type=F32)
        v = jnp.dot(yn_ref[...].astype(BF16), wn_ref[...], preferred_element_type=F32)
        merged = (_sigmoid(mr_ref[...]) * u + _sigmoid(mn_ref[...]) * v).astype(BF16)
        out = jnp.dot(merged, wo_ref[...], preferred_element_type=F32)
        u_ref[...] = u
        v_ref[...] = v
        mg_ref[...] = merged
        out_ref[...] = out
        x1_ref[...] = x_ref[...] + g2_ref[...] * out

    row = pl.BlockSpec((ROW_TILE, D_MODEL), lambda i: (i, 0))
    lat = lambda cb: pl.BlockSpec((ROW_TILE, D_MODEL), lambda i: (i + 1, cb))
    wspec = _full((D_MODEL, D_MODEL))
    f32o = jax.ShapeDtypeStruct((SEQ, D_MODEL), F32)
    return pl.pallas_call(
        body, name="merge_fwd", grid=(SEQ // ROW_TILE,),
        in_specs=[row, row, lat(5), lat(6), lat(0), _full((1, D_MODEL)), wspec, wspec, wspec],
        out_specs=[row] * 5,
        out_shape=[f32o, f32o, jax.ShapeDtypeStruct((SEQ, D_MODEL), BF16), f32o, f32o],
        compiler_params=_params("parallel"),
    )(y_rnn, y_na, p, p, z, g2, w_rnn, w_na, w_out)


FF_TILE = 256
FF_TILES = D_FF // FF_TILE


def _ffn_conv(h, cw, cb):
    row = _row_ids(SEQ, FF_TILE)
    prev = jnp.where(row >= 1, pltpu.roll(h, 1, 0), 0.0)
    nxt = jnp.where(row < SEQ - 1, pltpu.roll(h, SEQ - 1, 0), 0.0)
    return prev * cw[0:1, :] + h * cw[1:2, :] + nxt * cw[2:3, :] + cb


def ffn_act(hpre, conv_w, conv_b):
    def body(ha_ref, hg_ref, wa_ref, wg_ref, ba_ref, bg_ref, o_ref):
        a = _ffn_conv(ha_ref[...], wa_ref[...], ba_ref[...])
        g = _ffn_conv(hg_ref[...], wg_ref[...], bg_ref[...])
        o_ref[...] = (a * _sigmoid(a) * g).astype(BF16)

    col = lambda rows, off: pl.BlockSpec((rows, FF_TILE), lambda j: (0, j + off))
    return pl.pallas_call(
        body, name="ffn_act", grid=(FF_TILES,),
        in_specs=[col(SEQ, 0), col(SEQ, FF_TILES), col(3, 0), col(3, FF_TILES), col(1, 0), col(1, FF_TILES)],
        out_specs=col(SEQ, 0),
        out_shape=jax.ShapeDtypeStruct((SEQ, D_FF), BF16),
        compiler_params=_params("parallel"),
    )(hpre, hpre, conv_w, conv_w, conv_b, conv_b)


def ffn_down_loss(act, w_down, x1, g5, target):
    def body(a_ref, w_ref, x1_ref, g5_ref, t_ref, f_ref, dy_ref, df_ref, ls_ref, dg_ref):
        i = pl.program_id(0)
        f = jnp.dot(a_ref[...], w_ref[...], preferred_element_type=F32)
        g5 = g5_ref[...]
        err = x1_ref[...] + g5 * f - t_ref[...]
        dy = err * (1.0 / D_MODEL)
        f_ref[...] = f
        dy_ref[...] = dy
        df_ref[...] = (dy * g5).astype(BF16)

        @pl.when(i == 0)
        def _():
            ls_ref[...] = jnp.zeros_like(ls_ref)
            dg_ref[...] = jnp.zeros_like(dg_ref)

        ls_ref[...] = ls_ref[...] + jnp.sum(err * err)
        dg_ref[...] = dg_ref[...] + jnp.sum(dy * f, axis=0, keepdims=True)

    row = pl.BlockSpec((ROW_TILE, D_MODEL), lambda i: (i, 0))
    f32o = jax.ShapeDtypeStruct((SEQ, D_MODEL), F32)
    return pl.pallas_call(
        body, name="ffn_down_loss", grid=(SEQ // ROW_TILE,),
        in_specs=[pl.BlockSpec((ROW_TILE, D_FF), lambda i: (i, 0)), _full((D_FF, D_MODEL)), row, _full((1, D_MODEL)), row],
        out_specs=[row, row, row, _full((8, 128)), _full((1, D_MODEL))],
        out_shape=[f32o, f32o, jax.ShapeDtypeStruct((SEQ, D_MODEL), BF16), jax.ShapeDtypeStruct((8, 128), F32),
                   jax.ShapeDtypeStruct((1, D_MODEL), F32)],
        compiler_params=_params("arbitrary"),
    )(act, w_down, x1, g5, target)


def ffn_down_bwd(df, w_down):
    def body(df_ref, w_ref, o_ref):
        o_ref[...] = _dot_nt(df_ref[...], w_ref[...])

    return pl.pallas_call(
        body, name="ffn_down_bwd", grid=(SEQ // ROW_TILE,),
        in_specs=[pl.BlockSpec((ROW_TILE, D_MODEL), lambda i: (i, 0)), _full((D_FF, D_MODEL))],
        out_specs=pl.BlockSpec((ROW_TILE, D_FF), lambda i: (i, 0)),
        out_shape=jax.ShapeDtypeStruct((SEQ, D_FF), F32),
        compiler_params=_params("parallel"),
    )(df, w_down)


def ffn_act_bwd(hpre, d_act, conv_w, conv_b):
    def half_bwd(dc, h, w, dh_ref, dw_ref, db_ref):
        row = _row_ids(SEQ, FF_TILE)
        h_prev = jnp.where(row >= 1, pltpu.roll(h, 1, 0), 0.0)
        h_next = jnp.where(row < SEQ - 1, pltpu.roll(h, SEQ - 1, 0), 0.0)
        dw_ref[0:1, :] = jnp.sum(dc * h_prev, axis=0, keepdims=True)
        dw_ref[1:2, :] = jnp.sum(dc * h, axis=0, keepdims=True)
        dw_ref[2:3, :] = jnp.sum(dc * h_next, axis=0, keepdims=True)
        db_ref[...] = jnp.sum(dc, axis=0, keepdims=True)
        dc_next = jnp.where(row < SEQ - 1, pltpu.roll(dc, SEQ - 1, 0), 0.0)
        dc_prev = jnp.where(row >= 1, pltpu.roll(dc, 1, 0), 0.0)
        dh_ref[...] = (dc_next * w[0:1, :] + dc * w[1:2, :] + dc_prev * w[2:3, :]).astype(BF16)

    def body(ha_ref, hg_ref, da_ref, wa_ref, wg_ref, ba_ref, bg_ref, dha_ref, dhg_ref, dwa_ref, dwg_ref, dba_ref, dbg_ref):
        ha, hg = ha_ref[...], hg_ref[...]
        a = _ffn_conv(ha, wa_ref[...], ba_ref[...])
        g = _ffn_conv(hg, wg_ref[...], bg_ref[...])
        sig = _sigmoid(a)
        dact = da_ref[...]
        half_bwd(dact * g * (sig * (1.0 + a * (1.0 - sig))), ha, wa_ref[...], dha_ref, dwa_ref, dba_ref)
        half_bwd(dact * a * sig, hg, wg_ref[...], dhg_ref, dwg_ref, dbg_ref)

    col = lambda rows, off: pl.BlockSpec((rows, FF_TILE), lambda j: (0, j + off))
    hshape = jax.ShapeDtypeStruct((SEQ, D_FF), BF16)
    wshape = jax.ShapeDtypeStruct((3, D_FF), F32)
    bshape = jax.ShapeDtypeStruct((1, D_FF), F32)
    return pl.pallas_call(
        body, name="ffn_act_bwd", grid=(FF_TILES,),
        in_specs=[col(SEQ, 0), col(SEQ, FF_TILES), col(SEQ, 0), col(3, 0), col(3, FF_TILES), col(1, 0), col(1, FF_TILES)],
        out_specs=[col(SEQ, 0), col(SEQ, 0), col(3, 0), col(3, 0), col(1, 0), col(1, 0)],
        out_shape=[hshape, hshape, wshape, wshape, bshape, bshape],
        compiler_params=_params("parallel"),
    )(hpre, hpre, d_act, conv_w, conv_w, conv_b, conv_b)


def _norm_mod_bwd(x, dxn, gain, scale):
    rstd = lax.rsqrt(jnp.mean(x * x, axis=-1, keepdims=True) + EPS)
    nrm = x * rstd
    dsh = jnp.sum(dxn, axis=0, keepdims=True)
    dsc = jnp.sum(dxn * nrm, axis=0, keepdims=True) * gain
    dgn = jnp.sum(dxn * nrm, axis=0, keepdims=True) * (1.0 + scale)
    dn = dxn * (gain * (1.0 + scale))
    dx = rstd * (dn - nrm * jnp.mean(dn * nrm, axis=-1, keepdims=True))
    return dx, dsh, dsc, dgn


def ffn_up_bwd(dha, dhg, w_up, x1, dy, gain, scale):
    def body(dha_ref, dhg_ref, w_ref, x_ref, dy_ref, g_ref, sc_ref, dx_ref, dsh_ref, dsc_ref, dgn_ref):
        i = pl.program_id(0)
        dxn = _dot_nt(dha_ref[...], w_ref[:, :D_FF]) + _dot_nt(dhg_ref[...], w_ref[:, D_FF:])
        dx, dsh, dsc, dgn = _norm_mod_bwd(x_ref[...], dxn, g_ref[...], sc_ref[...])
        dx_ref[...] = dy_ref[...] + dx

        @pl.when(i == 0)
        def _():
            dsh_ref[...] = dsh
            dsc_ref[...] = dsc
            dgn_ref[...] = dgn

        @pl.when(i > 0)
        def _():
            dsh_ref[...] = dsh_ref[...] + dsh
            dsc_ref[...] = dsc_ref[...] + dsc
            dgn_ref[...] = dgn_ref[...] + dgn

    row = pl.BlockSpec((ROW_TILE, D_MODEL), lambda i: (i, 0))
    vec = _full((1, D_MODEL))
    vshape = jax.ShapeDtypeStruct((1, D_MODEL), F32)
    return pl.pallas_call(
        body, name="ffn_up_bwd", grid=(SEQ // ROW_TILE,),
        in_specs=[pl.BlockSpec((ROW_TILE, D_FF), lambda i: (i, 0)), pl.BlockSpec((ROW_TILE, D_FF), lambda i: (i, 0)),
                  _full((D_MODEL, 2 * D_FF)), row, row, vec, vec],
        out_specs=[row, vec, vec, vec],
        out_shape=[jax.ShapeDtypeStruct((SEQ, D_MODEL), F32), vshape, vshape, vshape],
        compiler_params=_params("arbitrary"),
    )(dha, dhg, w_up, x1, dy, gain, scale)


def merge_bwd(dx1, out, g2, p, u, v, w_rnn, w_na, w_out):
    def body(dx_ref, out_ref, g2_ref, mr_ref, mn_ref, u_ref, v_ref, wr_ref, wn_ref, wo_ref,
             dout_ref, du_ref, dv_ref, dmr_ref, dmn_ref, dyr_ref, dyn_ref, dg2_ref):
        i = pl.program_id(0)

        @pl.when(i == 0)
        def _():
            dmr_ref[...] = jnp.zeros_like(dmr_ref)
            dmn_ref[...] = jnp.zeros_like(dmn_ref)
            dg2_ref[...] = jnp.zeros_like(dg2_ref)

        @pl.when(i > 0)
        def _():
            dx = dx_ref[...]
            dg2_ref[...] = dg2_ref[...] + jnp.sum(dx * out_ref[...], axis=0, keepdims=True)
            dout = (dx * g2_ref[...]).astype(BF16)
            dout_ref[...] = dout
            dm = _dot_nt(dout, wo_ref[...])
            sr = _sigmoid(mr_ref[...])
            sn = _sigmoid(mn_ref[...])
            du = (dm * sr).astype(BF16)
            dv = (dm * sn).astype(BF16)
            du_ref[...] = du
            dv_ref[...] = dv
            dmr_ref[...] = (dm * u_ref[...] * (sr * (1.0 - sr))).astype(BF16)
            dmn_ref[...] = (dm * v_ref[...] * (sn * (1.0 - sn))).astype(BF16)
            dyr_ref[...] = _dot_nt(du, wr_ref[...])
            dyn_ref[...] = _dot_nt(dv, wn_ref[...])

    lat = pl.BlockSpec((ROW_TILE, D_MODEL), lambda i: (jnp.maximum(i - 1, 0), 0))
    zrow = pl.BlockSpec((ROW_TILE, D_MODEL), lambda i: (i, 0))
    pcol = lambda cb: pl.BlockSpec((ROW_TILE, D_MODEL), lambda i: (i, cb))
    wspec = _full((D_MODEL, D_MODEL))
    tb = jax.ShapeDtypeStruct((SEQ, D_MODEL), BF16)
    zb = jax.ShapeDtypeStruct((ZLEN, D_MODEL), BF16)
    tf = jax.ShapeDtypeStruct((SEQ, D_MODEL), F32)
    return pl.pallas_call(
        body, name="merge_bwd", grid=(ZLEN // ROW_TILE,),
        in_specs=[lat, lat, _full((1, D_MODEL)), pcol(5), pcol(6), lat, lat, wspec, wspec, wspec],
        out_specs=[lat, lat, lat, zrow, zrow, lat, lat, _full((1, D_MODEL))],
        out_shape=[tb, tb, tb, zb, zb, tf, tf, jax.ShapeDtypeStruct((1, D_MODEL), F32)],
        compiler_params=_params("arbitrary"),
    )(dx1, out, g2, p, p, u, v, w_rnn, w_na, w_out)


def attn_bwd(q_rot, q_pl, kk, vv, bt, y_na, d_yna, lse, comm=None):
    def body(qr_ref, qp_ref, kk_ref, vv_ref, bt_ref, o_ref, do_ref, lse_ref,
             dqr_ref, dqp_ref, dk_ref, dv_ref, dbt_ref, s_ref):
        jj = pl.program_id(1)

        @pl.when(jj == 0)
        def _():
            dqr_ref[...] = jnp.zeros_like(dqr_ref)
            dqp_ref[...] = jnp.zeros_like(dqp_ref)
            dk_ref[...] = jnp.zeros_like(dk_ref)
            dv_ref[...] = jnp.zeros_like(dv_ref)
            dbt_ref[...] = jnp.zeros_like(dbt_ref)

        @pl.when(jj > 0)
        def _():
            j = jj - 1
            ws, start = _key_window(j)
            win = pl.ds(start, KEY_TILE)
            kw, kc = kk_ref[win, :], kk_ref[:CTX_LEN, :]
            vw, vc = vv_ref[win, :], vv_ref[:CTX_LEN, :]
            qr, qp = qr_ref[...], qp_ref[...]
            do = do_ref[...]
            do_o = do * o_ref[...]
            dq_r, dq_p = [], []
            for hh in range(2):
                msk = _head_mask(hh)
                q_r, q_p = jnp.where(msk, qr, 0), jnp.where(msk, qp, 0)
                base = _attn_scores(j, ws, q_r, q_p, kw, kc, hh, bt_ref, s_ref)
                pr = jnp.exp(s_ref[...] - lse_ref[hh])
                delta = jnp.sum(jnp.where(msk, do_o, 0.0), axis=-1, keepdims=True)
                dob = jnp.where(msk, do, 0.0).astype(BF16)
                ds_lat = pr[:, :KEY_TILE] * (_dot_nt(dob, vw) - delta)
                ds_ctx = pr[:, KEY_TILE:] * (_dot_nt(dob, vc) - delta)
                for qi in range(Q_ROWS):
                    for m in range(KEY_ROWS // 2):
                        idx = base + 2 * m - qi
                        dbt_ref[hh, idx] = dbt_ref[hh, idx] + ds_lat[qi * GRID_W:(qi + 1) * GRID_W, 128 * m:128 * (m + 1)]
                dsb_lat = ds_lat.astype(BF16)
                dsb_ctx = ds_ctx.astype(BF16)
                prb = pr.astype(BF16)
                dq_r.append(jnp.dot(dsb_lat, kw, preferred_element_type=F32))
                dq_p.append(jnp.dot(dsb_ctx, kc, preferred_element_type=F32))
                dk_ref[win, :] = dk_ref[win, :] + _dot_tn(dsb_lat, q_r)
                dk_ref[:CTX_LEN, :] = dk_ref[:CTX_LEN, :] + _dot_tn(dsb_ctx, q_p)
                dv_ref[win, :] = dv_ref[win, :] + _dot_tn(prb[:, :KEY_TILE], dob)
                dv_ref[:CTX_LEN, :] = dv_ref[:CTX_LEN, :] + _dot_tn(prb[:, KEY_TILE:], dob)
            dqr_ref[...] = jnp.where(_head_mask(0), dq_r[0], dq_r[1])
            dqp_ref[...] = jnp.where(_head_mask(0), dq_p[0], dq_p[1])

    lat = lambda jj: jnp.maximum(jj - 1, 0)
    qspec = pl.BlockSpec((Q_TILE, 128), lambda hp, jj: (lat(jj) + 1, hp))
    kspec = pl.BlockSpec((ZLEN, 128), lambda hp, jj: (0, hp))
    btspec = pl.BlockSpec((2, BT_LEN, GRID_W, 128), lambda hp, jj: (hp, 0, 0, 0))
    ospec = pl.BlockSpec((Q_TILE, 128), lambda hp, jj: (lat(jj), hp))
    dqspec = pl.BlockSpec((Q_TILE, 128), lambda hp, jj: (jj, hp))
    zshape = jax.ShapeDtypeStruct((ZLEN, D_MODEL), F32)
    res, extra = _call(
        body, name="attn_bwd", grid=(NA_HEADS // 2, ZLEN // Q_TILE),
        in_specs=[qspec, qspec, kspec, kspec, btspec, ospec, ospec,
                  pl.BlockSpec((2, Q_TILE, 1), lambda hp, jj: (hp, lat(jj), 0))],
        out_specs=[dqspec, dqspec, kspec, kspec, btspec],
        out_shape=[zshape, zshape, zshape, zshape, jax.ShapeDtypeStruct((NA_HEADS, BT_LEN, GRID_W, 128), F32)],
        scratch_shapes=[pltpu.VMEM((Q_TILE, KEY_TILE + CTX_LEN), F32)], sem=("parallel", "arbitrary"),
        args=(q_rot, q_pl, kk, vv, bt, y_na, d_yna, lse), comm=comm)
    return (*res, extra)


def qkv_bwd(dq_rot, dq_pl, dk, dv, p, qg2, kg2, cos, sin, ones):
    scale = HEAD_DIM ** -0.5
    n_hp, n_i = NA_HEADS // 2, ZLEN // PREP_TILE

    def norm_rope_bwd(d_rot, d_extra, x, gain, cos_t, sin_t, ones_m, dx_ref, acc_ref):
        xh, rstd = _head_rms(x, ones_m, 1.0)
        dn = d_rot * cos_t + _rope_partner(d_rot * sin_t)
        if d_extra is not None:
            dn = (dn + d_extra) * scale
        acc_ref[...] = acc_ref[...] + jnp.sum(dn * xh, axis=0, keepdims=True)
        dxh = dn * gain
        seg = jnp.dot(dxh * xh, ones_m, preferred_element_type=F32, precision=lax.Precision.HIGHEST) * (1.0 / HEAD_DIM)
        dx_ref[...] = (rstd * (dxh - xh * seg)).astype(BF16)

    def body(dqr_ref, dqp_ref, dk_ref, dv_ref, xq_ref, xk_ref, qg_ref, kg_ref, cos_ref, sin_ref, ones_ref,
             dxq_ref, dxk_ref, dxv_ref, dgq_ref, dgk_ref, accq_ref, acck_ref):
        hp, i = pl.program_id(0), pl.program_id(1)

        @pl.when((hp == 0) & (i == 0))
        def _():
            accq_ref[...] = jnp.zeros_like(accq_ref)
            acck_ref[...] = jnp.zeros_like(acck_ref)

        ones_m = ones_ref[...]
        cos_t, sin_t = cos_ref[...], sin_ref[...]
        norm_rope_bwd(dqr_ref[...], dqp_ref[...], xq_ref[...], qg_ref[...], cos_t, sin_t, ones_m, dxq_ref, accq_ref)
        norm_rope_bwd(dk_ref[...], None, xk_ref[...], kg_ref[...], cos_t, sin_t, ones_m, dxk_ref, acck_ref)
        dxv_ref[...] = dv_ref[...].astype(BF16)

        @pl.when((hp == n_hp - 1) & (i == n_i - 1))
        def _():
            dgq_ref[...] = accq_ref[:, :HEAD_DIM] + accq_ref[:, HEAD_DIM:]
            dgk_ref[...] = acck_ref[:, :HEAD_DIM] + acck_ref[:, HEAD_DIM:]

    col = lambda base: pl.BlockSpec((PREP_TILE, 128), lambda hp, i: (i, base + hp))
    small = pl.BlockSpec((1, 128), lambda hp, i: (0, 0))
    tab = pl.BlockSpec((PREP_TILE, 128), lambda hp, i: (i, 0))
    zb = jax.ShapeDtypeStruct((ZLEN, D_MODEL), BF16)
    gshape = jax.ShapeDtypeStruct((1, HEAD_DIM), F32)
    return pl.pallas_call(
        body, name="qkv_bwd", grid=(n_hp, n_i),
        in_specs=[col(0)] * 4 + [col(32), col(8), small, small, tab, tab, _full((128, 128))],
        out_specs=[col(0)] * 3 + [_full((1, HEAD_DIM))] * 2,
        out_shape=[zb, zb, zb, gshape, gshape],
        scratch_shapes=[pltpu.VMEM((1, 128), F32)] * 2,
        compiler_params=_params("arbitrary", "arbitrary"),
    )(dq_rot, dq_pl, dk, dv, p, p, qg2, kg2, cos, sin, ones)


def rpb_grad(dbt):
    e, _ = _bias_expand()
    n_dr = 2 * NA_ROWS - 1
    ea = np.zeros((31, GRID_W, 128), np.float32)
    ea[:, :, :GRID_W] = e
    eb = np.zeros((31, GRID_W, 128), np.float32)
    eb[:, :, GRID_W:] = e
    eat = jnp.asarray(ea.reshape(31, GRID_W * 128).T.copy())
    ebt = jnp.asarray(eb.reshape(31, GRID_W * 128).T.copy())
    sel_at = np.zeros((n_dr, BT_LEN), np.float32)
    sel_bt = np.zeros((n_dr, BT_LEN), np.float32)
    for r in range(BT_LEN):
        dr = r - BT_PAD
        if 0 <= dr < n_dr:
            sel_at[dr, r] = 1.0
        if 0 <= dr + 1 < n_dr:
            sel_bt[dr + 1, r] = 1.0
    hi = lax.Precision.HIGHEST

    tk = 2048
    wide = GRID_W * 128
    rows = NA_HEADS * BT_LEN
    n_k = wide // tk

    def body(d_ref, sa_ref, sb_ref, ea_ref, eb_ref, o_ref, a_s, b_s):
        k = pl.program_id(0)
        dm = d_ref[...]
        a = jnp.dot(dm, ea_ref[...], preferred_element_type=F32, precision=hi)
        b = jnp.dot(dm, eb_ref[...], preferred_element_type=F32, precision=hi)

        @pl.when(k == 0)
        def _():
            a_s[...] = a
            b_s[...] = b

        @pl.when(k > 0)
        def _():
            a_s[...] = a_s[...] + a
            b_s[...] = b_s[...] + b

        @pl.when(k == n_k - 1)
        def _():
            for h in range(NA_HEADS):
                sl = slice(h * BT_LEN, (h + 1) * BT_LEN)
                o_ref[h] = (jnp.dot(sa_ref[...], a_s[sl, :], preferred_element_type=F32, precision=hi)
                            + jnp.dot(sb_ref[...], b_s[sl, :], preferred_element_type=F32, precision=hi))

    return pl.pallas_call(
        body, name="rpb_grad", grid=(n_k,),
        in_specs=[pl.BlockSpec((rows, tk), lambda k: (0, k)), _full((n_dr, BT_LEN)), _full((n_dr, BT_LEN)),
                  pl.BlockSpec((tk, 31), lambda k: (k, 0)), pl.BlockSpec((tk, 31), lambda k: (k, 0))],
        out_specs=_full((NA_HEADS, n_dr, 31)),
        out_shape=jax.ShapeDtypeStruct((NA_HEADS, n_dr, 31), F32),
        scratch_shapes=[pltpu.VMEM((rows, 31), F32), pltpu.VMEM((rows, 31), F32)],
        compiler_params=_params("arbitrary"),
    )(dbt.reshape(rows, wide), jnp.asarray(sel_at), jnp.asarray(sel_bt), eat, ebt)


def lru_bwd(p, d_yrnn, conv_w, conv_b, wa, ba, wx, bx, lam, comm=None):
    blk, wspec = _lru_in_specs()

    def body(xr_ref, gx_ref, dy_ref, cw_ref, cb_ref, wa_ref, ba_ref, wx_ref, bx_ref, lam_ref,
             dxr_ref, dgx_ref, dcw_ref, dcb_ref, dwa_ref, dba_ref, dwx_ref, dbx_ref, dlam_ref,
             a_s, b_s, h_s, l_s, hsum_s, dxc_s, dh_s):
        xr = xr_ref[...]
        cw = cw_ref[...]
        xc = _lru_conv(xr, cw, cb_ref[...])
        xcb = xc.astype(BF16)
        g, dg = _gelu_parts(gx_ref[CTX_LEN:, :])
        dy = dy_ref[...]
        dh_s[:CTX_LEN, :] = jnp.zeros((CTX_LEN, LRU_BLOCK_W), F32)
        dh_s[CTX_LEN:, :] = dy * g
        row = _row_ids(ZLEN, LRU_BLOCK_W)
        zero = jnp.zeros((1, LRU_BLOCK_W), F32)
        for d in (0, 1):
            wab = wa_ref[d, 0].astype(BF16)
            wxb = wx_ref[d, 0].astype(BF16)
            lam_d = lam_ref[d:d + 1, :]
            r, gi, sp, a, sq, b = _lru_gates(xc, xcb, wab, ba_ref[d:d + 1, :], wxb, bx_ref[d:d + 1, :], lam_d)
            a_s[...] = a
            b_s[...] = b
            _lru_scan_dir(d, a_s, b_s, h_s)
            h = h_s[...]
            if d == 0:
                hsum_s[...] = h
                h_prev = jnp.where(row >= 1, pltpu.roll(h, 1, 0), 0.0)
                a_s[...] = pltpu.roll(a, ZLEN - 1, 0)
                _scan_down(a_s, dh_s, l_s, 0, N_CHUNK, zero)
            else:
                hsum_s[...] = hsum_s[...] + h
                h_prev = jnp.where(row == CTX_LEN - 1, 0.0, pltpu.roll(h, ZLEN - 1, 0))
                a_s[...] = pltpu.roll(a, 1, 0)
                c = _scan_up(a_s, dh_s, l_s, CTX_CHUNKS, N_CHUNK, zero)
                _scan_up(a_s, dh_s, l_s, 0, CTX_CHUNKS, c)
            db = l_s[...]
            da = db * h_prev
            dsq = db * gi * xc
            dgi = db * sq * xc
            dxc_d = db * sq * gi
            dla = da * a - dsq * (a * a) / sq
            dr = dla * ((-LRU_C) * sp)
            dsp = jnp.sum(dla * ((-LRU_C) * r), axis=0, keepdims=True)
            dlam_ref[d:d + 1, :] = -dsp * _sigmoid(-lam_d)
            dzr = dr * r * (1.0 - r)
            dzi = dgi * gi * (1.0 - gi)
            dba_ref[d:d + 1, :] = jnp.sum(dzr, axis=0, keepdims=True)
            dbx_ref[d:d + 1, :] = jnp.sum(dzi, axis=0, keepdims=True)
            dzrb = dzr.astype(BF16)
            dzib = dzi.astype(BF16)
            dwa_ref[d, 0] = _dot_tn(xcb, dzrb)
            dwx_ref[d, 0] = _dot_tn(xcb, dzib)
            dxc_d = dxc_d + _dot_nt(dzrb, wab) + _dot_nt(dzib, wxb)
            if d == 0:
                dxc_s[...] = dxc_d
            else:
                dxc_s[...] = dxc_s[...] + dxc_d
        dxc = dxc_s[...]
        dxr_ref[...] = _lru_conv_t(dxc, cw).astype(BF16)
        dcb_ref[...] = jnp.sum(dxc, axis=0, keepdims=True)
        segpos = jnp.where(row < CTX_LEN, row, row - CTX_LEN)
        seglen = jnp.where(row < CTX_LEN, CTX_LEN, SEQ)
        for k in range(4):
            off = k - 2
            if off == 0:
                sh = xr
            else:
                ok = (segpos + off >= 0) & (segpos + off < seglen)
                sh = jnp.where(ok, pltpu.roll(xr, (-off) % ZLEN, 0), 0.0)
            dcw_ref[k:k + 1, :] = jnp.sum(dxc * sh, axis=0, keepdims=True)
        dgx_ref[:CTX_LEN, :] = jnp.zeros((CTX_LEN, LRU_BLOCK_W), BF16)
        dgx_ref[CTX_LEN:, :] = (dy * hsum_s[CTX_LEN:, :] * dg).astype(BF16)

    zs = pltpu.VMEM((ZLEN, LRU_BLOCK_W), F32)
    zb = jax.ShapeDtypeStruct((ZLEN, D_MODEL), BF16)
    v2 = jax.ShapeDtypeStruct((2, D_MODEL), F32)
    w4 = jax.ShapeDtypeStruct((2, LRU_BLOCKS, LRU_BLOCK_W, LRU_BLOCK_W), F32)
    res, extra = _call(
        body, name="lru_bwd", grid=(LRU_BLOCKS,),
        in_specs=[blk(ZLEN), pl.BlockSpec((ZLEN, LRU_BLOCK_W), lambda b: (0, 24 + b)), blk(SEQ), blk(4), blk(1),
                  wspec, blk(2), wspec, blk(2), blk(2)],
        out_specs=[blk(ZLEN), blk(ZLEN), blk(4), blk(1), wspec, blk(2), wspec, blk(2), blk(2)],
        out_shape=[zb, zb, jax.ShapeDtypeStruct((4, D_MODEL), F32), jax.ShapeDtypeStruct((1, D_MODEL), F32),
                   w4, v2, w4, v2, v2],
        scratch_shapes=[zs] * 7, sem=("arbitrary",),
        args=(p, p, d_yrnn, conv_w, conv_b, wa, ba, wx, bx, lam), comm=comm)
    return (*res, extra)


def in_proj_bwd(dgs, w_in, z, dx1, gain, scale, comm=None):
    def body(*refs):
        dg_refs = refs[:7]
        w_ref, z_ref, dx1_ref, g_ref, sc_ref, gx_ref, dsh_ref, dsc_ref, dgn_ref = refs[7:]
        i = pl.program_id(0)
        dxn = _dot_nt(dg_refs[0][...], w_ref[:, 0:D_MODEL])
        for g in range(1, 7):
            dxn = dxn + _dot_nt(dg_refs[g][...], w_ref[:, g * D_MODEL:(g + 1) * D_MODEL])
        dx, dsh, dsc, dgn = _norm_mod_bwd(z_ref[...], dxn, g_ref[...], sc_ref[0])

        @pl.when(i <= 1)
        def _():
            dsh_ref[0] = dsh
            dsc_ref[0] = dsc

        @pl.when(i > 1)
        def _():
            dsh_ref[0] = dsh_ref[0] + dsh
            dsc_ref[0] = dsc_ref[0] + dsc

        @pl.when(i == 0)
        def _():
            dgn_ref[...] = dgn

        @pl.when(i > 0)
        def _():
            dgn_ref[...] = dgn_ref[...] + dgn
            gx_ref[...] = dx1_ref[...] + dx

    zrow = pl.BlockSpec((ROW_TILE, D_MODEL), lambda i: (i, 0))
    lat = pl.BlockSpec((ROW_TILE, D_MODEL), lambda i: (jnp.maximum(i - 1, 0), 0))
    mod = pl.BlockSpec((1, 1, D_MODEL), lambda i: (jnp.minimum(i, 1), 0, 0))
    mshape = jax.ShapeDtypeStruct((2, 1, D_MODEL), F32)
    res, extra = _call(
        body, name="in_proj_bwd", grid=(ZLEN // ROW_TILE,),
        in_specs=[zrow] * 7 + [_full((D_MODEL, IN_COLS)), zrow, lat, _full((1, D_MODEL)), mod],
        out_specs=[lat, mod, mod, _full((1, D_MODEL))],
        out_shape=[jax.ShapeDtypeStruct((SEQ, D_MODEL), F32), mshape, mshape, jax.ShapeDtypeStruct((1, D_MODEL), F32)],
        sem=("arbitrary",), args=(*dgs, w_in, z, dx1, gain, scale), comm=comm)
    return (*res, extra)


def matmul_tn(a, b, name, tm, tn, prev=None, col_block=0, total_cols=None):
    k, m = a.shape
    n = b.shape[1]
    total_cols = n if total_cols is None else total_cols
    assert m % tm == 0 and n % tn == 0
    off = col_block * (n // tn)

    def body(a_ref, b_ref, *rest):
        rest[-1][...] = _dot_tn(a_ref[...].astype(BF16), b_ref[...]).astype(BF16)

    in_specs = [pl.BlockSpec((k, tm), lambda i, j: (0, i)), pl.BlockSpec((k, tn), lambda i, j: (0, j))]
    args = [a, b]
    aliases = {}
    if prev is not None:
        in_specs.append(pl.BlockSpec(memory_space=pl.ANY))
        args.append(prev)
        aliases = {2: 0}
    return pl.pallas_call(
        body, name=name, grid=(m // tm, n // tn), in_specs=in_specs,
        out_specs=pl.BlockSpec((tm, tn), lambda i, j: (i, j + off)),
        out_shape=jax.ShapeDtypeStruct((m, total_cols), BF16),
        input_output_aliases=aliases,
        compiler_params=_params("parallel", "parallel"),
    )(*args)


def local_step(z, target, modx, modc, norm_mix_g, norm_ffn_g, w_in, conv_w, conv_b, wa, ba, wx, bx, lam, qg, kg, rpb,
               w_rnn, w_na, w_out, w_up, fconv_w, fconv_b, w_down, c_idx=None):
    dist = c_idx is not None
    d = D_MODEL
    mx = [modx[:, k * d:(k + 1) * d] for k in range(N_MOD)]
    shift = jnp.stack([modc[:, 0:d], mx[0]])
    scale = jnp.stack([modc[:, d:2 * d], mx[1]])
    cos, sin = _rope_tables()
    ones = _head_ones()
    qg2 = jnp.tile(qg, (1, 2))
    kg2 = jnp.tile(kg, (1, 2))

    xn = norm_mod(z, norm_mix_g, shift, scale, "norm_mix")
    p, got = matmul_wide(xn, w_in, "in_proj", ROW_TILE, 1792,
                         comm=gather_weights_comm([w_rnn, w_na, w_out], [1, 2, 3]) if dist else None)
    if dist:
        w_rnn, w_na, w_out = got
    y_rnn, got = lru_fwd(p, conv_w, conv_b, wa, ba, wx, bx, lam,
                         comm=gather_weights_comm([w_down], [5]) if dist else None)
    if dist:
        w_down = got[0]
    q_rot, q_pl, kk, vv = qkv_prep(p, qg2, kg2, cos, sin, ones)
    bt = bias_table(rpb)
    y_na, lse, got = attn_fwd(q_rot, q_pl, kk, vv, bt, comm=gather_weights_comm([w_up], [4]) if dist else None)
    if dist:
        w_up = got[0]
    u, v, merged, out, x1 = merge_fwd(y_rnn, y_na, p, z, mx[2], w_rnn, w_na, w_out)
    xn2 = norm_mod(x1, norm_ffn_g, mx[3][None], mx[4][None], "norm_ffn")
    hpre, _ = matmul_wide(xn2, w_up, "ffn_up", ROW_TILE, 1408)
    act = ffn_act(hpre, fconv_w, fconv_b)
    f, dy, df, loss_sq, dg5 = ffn_down_loss(act, w_down, x1, mx[5], target)

    partials, pieces = {}, {}

    def chip_partials(which, grads, tag):
        views = [_grad_view(g, BIG[w][1], BIG[w][2]) for w, g in zip(which, grads)]
        recv = exchange_halves(views, "grad_exchange_" + tag)
        for w, gv, r in zip(which, views, recv):
            partials[w] = add_halves(gv, r, c_idx, "add_halves_" + BIG[w][0])
        return scatter_pieces_comm([partials[w] for w in which], which)

    d_act = ffn_down_bwd(df, w_down)
    dha, dhg, d_fcw_a, d_fcw_g, d_fcb_a, d_fcb_g = ffn_act_bwd(hpre, d_act, fconv_w, fconv_b)
    d_fcw = jnp.concatenate([d_fcw_a, d_fcw_g], axis=1)
    d_fcb = jnp.concatenate([d_fcb_a, d_fcb_g], axis=1)
    dx1, d_s3, d_s4, d_gffn = ffn_up_bwd(dha, dhg, w_up, x1, dy, norm_ffn_g, mx[4])
    g_w_down = matmul_tn(act, df, "gw_down", 256, D_MODEL)
    g_w_up = matmul_tn(xn2, dha, "gw_up_a", 512, 1408, total_cols=2 * D_FF)
    g_w_up = matmul_tn(xn2, dhg, "gw_up_g", 512, 1408, prev=g_w_up, col_block=1, total_cols=2 * D_FF)
    dout, du, dv, dmr, dmn, dyr, dyn, dg2 = merge_bwd(dx1, out, mx[2], p, u, v, w_rnn, w_na, w_out)
    g_w_out = matmul_tn(merged, dout, "gw_out", 512, 512)
    g_w_rnn = matmul_tn(y_rnn, du, "gw_rnn", 512, 512)
    g_w_na = matmul_tn(y_na, dv, "gw_na", 512, 512)
    dqr, dqp, dk, dvh, dbt, got = attn_bwd(q_rot, q_pl, kk, vv, bt, y_na, dyn, lse,
                                           comm=chip_partials([4, 5], [g_w_up, g_w_down], "ffn") if dist else None)
    if dist:
        pieces[4], pieces[5] = got
    dq_cols, dk_cols, dv_cols, d_qg, d_kg = qkv_bwd(dqr, dqp, dk, dvh, p, qg2, kg2, cos, sin, ones)
    d_rpb = rpb_grad(dbt)
    *lru_grads, got = lru_bwd(p, dyr, conv_w, conv_b, wa, ba, wx, bx, lam,
                              comm=chip_partials([1, 2, 3], [g_w_rnn, g_w_na, g_w_out], "mix") if dist else None)
    dxr, dgx, d_cw, d_cb, d_wa, d_ba, d_wx, d_bx, d_lam = lru_grads
    if dist:
        pieces[1], pieces[2], pieces[3] = got
    dgs = [dxr, dk_cols, dv_cols, dgx, dq_cols, dmr, dmn]
    grad_x, dsh, dsc, d_gmix, _ = in_proj_bwd(dgs, w_in, z, dx1, norm_mix_g, scale)
    g_w_in = None
    for g in range(7):
        g_w_in = matmul_tn(xn, dgs[g], "gw_in_%d" % g, 512, 512, prev=g_w_in, col_block=g, total_cols=IN_COLS)
    if dist:
        pieces[0] = run_comm(chip_partials([0], [g_w_in], "w_in"), "grad_scatter_w_in")[0]

    d_modx = jnp.concatenate([dsh[1], dsc[1], dg2, d_s3, d_s4, dg5], axis=1)
    d_modc = jnp.concatenate([dsh[0], dsc[0]], axis=1)
    return dict(loss_sq=loss_sq, grad_x=grad_x, d_modx=d_modx, d_modc=d_modc, norm_mix_g=d_gmix, norm_ffn_g=d_gffn,
                w_in=g_w_in, lru_conv_w=d_cw, lru_conv_b=d_cb, lru_wa=d_wa, lru_ba=d_ba, lru_wx=d_wx, lru_bx=d_bx,
                lru_lambda=d_lam, q_norm_g=d_qg, k_norm_g=d_kg, na_rpb=d_rpb, w_rnn_out=g_w_rnn, w_na_out=g_w_na,
                w_out=g_w_out, w_up=g_w_up, ffn_conv_w=d_fcw, ffn_conv_b=d_fcb, w_down=g_w_down,
                partials=partials, pieces=pieces)


def _mesh_pos():
    return lax.axis_index("x"), lax.axis_index("y"), lax.axis_index("c")


def _other_chips(x, y):
    return [(1 - x, y), (x, 1 - y), (1 - x, 1 - y)]


def all_gather8(xs, name, with_sum=False):
    m, n = xs.shape
    assert m % 8 == 0

    def body(x_ref, out_ref, *rest):
        if with_sum:
            sum_ref, send_sems, recv_sems, local_sem = rest
        else:
            send_sems, recv_sems, local_sem = rest
        x, y, c = _mesh_pos()
        me, sibling = (x, y, c), (x, y, 1 - c)
        chips = _other_chips(x, y)

        def rows(px, py, pc):
            return out_ref.at[pl.ds((4 * px + 2 * py + pc) * m, m), :]

        def copy(k, block, to, src=None):
            return pltpu.make_async_remote_copy(
                src_ref=rows(*block) if src is None else src, dst_ref=rows(*block),
                send_sem=send_sems.at[k], recv_sem=recv_sems.at[k], device_id=to, device_id_type=MESH_T)

        mine = pltpu.make_async_copy(x_ref, rows(*me), local_sem)
        mine.start()
        first = [copy(0, me, sibling, src=x_ref)]
        first += [copy(1 + j, me, (*chip, c), src=x_ref) for j, chip in enumerate(chips)]
        for cp in first:
            cp.start()
        passed = [copy(4 + j, (*chip, c), sibling) for j, chip in enumerate(chips)]
        for j, chip in enumerate(chips):
            copy(1 + j, (*chip, c), me).wait_recv()
            passed[j].start()
        copy(0, sibling, me).wait_recv()
        for j, chip in enumerate(chips):
            copy(4 + j, (*chip, 1 - c), me).wait_recv()
        for cp in first + passed:
            cp.wait_send()
        mine.wait()
        if with_sum:
            acc = out_ref[0:m, :]
            for k in range(1, N_DEV):
                acc = acc + out_ref[k * m:(k + 1) * m, :]
            sum_ref[...] = acc

    vm = pl.BlockSpec(memory_space=pltpu.VMEM)
    out_shape = [jax.ShapeDtypeStruct((N_DEV * m, n), F32)]
    if with_sum:
        out_shape.append(jax.ShapeDtypeStruct((m, n), F32))
    res = pl.pallas_call(
        body, name=name, in_specs=[vm], out_specs=[vm] * len(out_shape), out_shape=out_shape,
        scratch_shapes=[pltpu.SemaphoreType.DMA((7,)), pltpu.SemaphoreType.DMA((7,)), pltpu.SemaphoreType.DMA],
        compiler_params=pltpu.CompilerParams(vmem_limit_bytes=VMEM_LIMIT_V7X),
    )(xs)
    return res if with_sum else res[0]


BIG = (("w_in", (D_MODEL, IN_COLS), 1), ("w_rnn_out", (D_MODEL, D_MODEL), 0), ("w_na_out", (D_MODEL, D_MODEL), 0),
       ("w_out", (D_MODEL, D_MODEL), 0), ("w_up", (D_MODEL, 2 * D_FF), 1), ("w_down", (D_FF, D_MODEL), 0))


def _shard_shape(full, axis):
    r, c = full
    return (r // N_SHARD, c) if axis == 0 else (r, c // N_SHARD)


def _slot(ref, full, axis, s, h):
    r, c = full
    if axis == 0:
        rs = r // N_SHARD
        return ref.at[pl.ds(s * rs + h * (rs // 2), rs // 2), :]
    cs = c // N_SHARD
    return ref.at[pl.ds(h * (r // 2), r // 2), pl.ds(s * cs, cs)]


def cast_into_full(x, full, axis, idx, name):
    r, c = x.shape
    tr = next(t for t in (512, 352, 256, 128) if r % t == 0)
    nb = r // tr

    def body(idx_ref, x_ref, o_ref):
        o_ref[...] = x_ref[...].astype(BF16)

    if axis == 0:
        out_spec = pl.BlockSpec((tr, c), lambda i, idx_ref: (idx_ref[0] * nb + i, 0))
    else:
        out_spec = pl.BlockSpec((tr, c), lambda i, idx_ref: (i, idx_ref[0]))
    return pl.pallas_call(
        body, name=name,
        grid_spec=pltpu.PrefetchScalarGridSpec(
            num_scalar_prefetch=1, grid=(nb,), in_specs=[pl.BlockSpec((tr, c), lambda i, idx_ref: (i, 0))],
            out_specs=out_spec),
        out_shape=jax.ShapeDtypeStruct(full, BF16),
        compiler_params=_params("parallel"),
    )(idx, x)


def run_comm(comm, name):
    k_in, k_out = len(comm.inputs), len(comm.out_shapes)

    def body(*refs):
        start, mid, end = comm.emit(refs[:k_in], refs[k_in:k_in + k_out], refs[k_in + k_out:])
        start()
        mid()
        end()

    hbm = pl.BlockSpec(memory_space=pl.ANY)
    return pl.pallas_call(
        body, name=name, in_specs=[hbm] * k_in, out_specs=[hbm] * k_out, out_shape=list(comm.out_shapes),
        input_output_aliases=dict(comm.aliases), scratch_shapes=list(comm.scratch),
        compiler_params=pltpu.CompilerParams(vmem_limit_bytes=VMEM_LIMIT_V7X),
    )(*comm.inputs)


def gather_weights_comm(fulls, which):
    nw = len(which)
    specs = [BIG[w] for w in which]

    def emit(_, outs, sems):
        send1, recv1, send2, recv2 = sems
        x, y, c = _mesh_pos()
        sibling = (x, y, 1 - c)
        chips = _other_chips(x, y)
        s_me = 2 * x + y
        shards = [2 * chip[0] + chip[1] for chip in chips]

        def ici(w, j, shard):
            _, full, axis = specs[w]
            dst = _slot(outs[w], full, axis, shard, c)
            return pltpu.make_async_remote_copy(
                src_ref=dst, dst_ref=dst, send_sem=send1.at[3 * w + j],
                recv_sem=recv1.at[3 * w + j], device_id=(*chips[j], c), device_id_type=MESH_T)

        def d2d(w, j, shard, half):
            _, full, axis = specs[w]
            dst = _slot(outs[w], full, axis, shard, half)
            return pltpu.make_async_remote_copy(
                src_ref=dst, dst_ref=dst, send_sem=send2.at[3 * w + j], recv_sem=recv2.at[3 * w + j],
                device_id=sibling, device_id_type=MESH_T)

        pairs = [(w, j) for w in range(nw) for j in range(3)]

        def start():
            for w, j in pairs:
                ici(w, j, s_me).start()

        def mid():
            for w, j in pairs:
                ici(w, j, shards[j]).wait_recv()
                d2d(w, j, shards[j], c).start()

        def end():
            for w, j in pairs:
                d2d(w, j, shards[j], 1 - c).wait_recv()
            for w, j in pairs:
                ici(w, j, s_me).wait_send()
                d2d(w, j, shards[j], c).wait_send()

        return start, mid, end

    return Comm(list(fulls), [jax.ShapeDtypeStruct(full, BF16) for _, full, _ in specs], {i: i for i in range(nw)},
                [pltpu.SemaphoreType.DMA((3 * nw,))] * 4, emit)


def _grad_view(g, full, axis):
    r, c = full
    if axis == 0:
        return g.reshape(N_SHARD, 2, r // N_SHARD // 2, c)
    return g.reshape(1, 2, r // 2, c)


def exchange_halves(gviews, name):
    nw = len(gviews)

    def body(*refs):
        srcs, outs = refs[:nw], refs[nw:2 * nw]
        send_sems, recv_sems = refs[2 * nw:]
        x, y, c = _mesh_pos()
        cps = []
        for w in range(nw):
            cp = pltpu.make_async_remote_copy(
                src_ref=srcs[w].at[:, pl.ds(1 - c, 1)], dst_ref=outs[w], send_sem=send_sems.at[w],
                recv_sem=recv_sems.at[w], device_id=(x, y, 1 - c), device_id_type=MESH_T)
            cp.start()
            cps.append(cp)
        for cp in cps:
            cp.wait()

    hbm = pl.BlockSpec(memory_space=pl.ANY)
    return pl.pallas_call(
        body, name=name, in_specs=[hbm] * nw, out_specs=[hbm] * nw,
        out_shape=[jax.ShapeDtypeStruct((g.shape[0], 1) + g.shape[2:], BF16) for g in gviews],
        scratch_shapes=[pltpu.SemaphoreType.DMA((nw,)), pltpu.SemaphoreType.DMA((nw,))],
        compiler_params=pltpu.CompilerParams(vmem_limit_bytes=VMEM_LIMIT_V7X),
    )(*gviews)


def _row_tile(rh):
    return 128 if rh % 128 == 0 else rh


def add_halves(gview, recv, c_idx, name):
    a, _, rh, cc = gview.shape
    tr = _row_tile(rh)

    def body(c_ref, g_ref, r_ref, o_ref):
        o_ref[0] = (g_ref[0, 0].astype(F32) + r_ref[0, 0].astype(F32)).astype(BF16)

    return pl.pallas_call(
        body, name=name,
        grid_spec=pltpu.PrefetchScalarGridSpec(
            num_scalar_prefetch=1, grid=(a, rh // tr),
            in_specs=[pl.BlockSpec((1, 1, tr, cc), lambda s, i, c_ref: (s, c_ref[0], i, 0)),
                      pl.BlockSpec((1, 1, tr, cc), lambda s, i, c_ref: (s, 0, i, 0))],
            out_specs=pl.BlockSpec((1, tr, cc), lambda s, i, c_ref: (s, i, 0))),
        out_shape=jax.ShapeDtypeStruct((a, rh, cc), BF16),
        compiler_params=_params("parallel", "parallel"),
    )(c_idx, gview, recv)


def _piece_shape(full, axis):
    rs, cs = _shard_shape(full, axis)
    return (rs // 2, cs)


def scatter_pieces_comm(partials, which):
    nw = len(which)
    specs = [BIG[w] for w in which]

    def emit(srcs, outs, sems):
        send_sems, recv_sems = sems
        x, y, c = _mesh_pos()
        chips = _other_chips(x, y)

        def copies():
            cps = []
            for w, (_, full, axis) in enumerate(specs):
                cs = full[1] // N_SHARD
                for j, chip in enumerate(chips):
                    s_j = 2 * chip[0] + chip[1]
                    src = srcs[w].at[s_j] if axis == 0 else srcs[w].at[0, :, pl.ds(s_j * cs, cs)]
                    cps.append(pltpu.make_async_remote_copy(
                        src_ref=src, dst_ref=outs[w].at[j], send_sem=send_sems.at[3 * w + j],
                        recv_sem=recv_sems.at[3 * w + j], device_id=(*chip, c), device_id_type=MESH_T))
            return cps

        def start():
            for cp in copies():
                cp.start()

        def mid():
            pass

        def end():
            for cp in copies():
                cp.wait()

        return start, mid, end

    return Comm(list(partials), [jax.ShapeDtypeStruct((3,) + _piece_shape(full, axis), BF16) for _, full, axis in specs],
                {}, [pltpu.SemaphoreType.DMA((3 * nw,)), pltpu.SemaphoreType.DMA((3 * nw,))], emit)


def add_pieces(partial, recv, idx, axis, name):
    _, rh, cs = recv.shape
    tr = _row_tile(rh)

    def body(idx_ref, p_ref, r_ref, o_ref):
        o_ref[0] = ((p_ref[0].astype(F32) + r_ref[0].astype(F32)) + r_ref[1].astype(F32)) + r_ref[2].astype(F32)

    if axis == 0:
        pspec = pl.BlockSpec((1, tr, cs), lambda i, idx_ref: (idx_ref[0], i, 0))
    else:
        pspec = pl.BlockSpec((1, tr, cs), lambda i, idx_ref: (0, i, idx_ref[0]))
    return pl.pallas_call(
        body, name=name,
        grid_spec=pltpu.PrefetchScalarGridSpec(
            num_scalar_prefetch=1, grid=(rh // tr,),
            in_specs=[pspec, pl.BlockSpec((3, tr, cs), lambda i, idx_ref: (0, i, 0))],
            out_specs=pl.BlockSpec((1, tr, cs), lambda i, idx_ref: (idx_ref[1], i, 0))),
        out_shape=jax.ShapeDtypeStruct((2, rh, cs), F32),
        compiler_params=_params("parallel"),
    )(idx, partial, recv)


def join_halves(halves):
    nw = len(BIG)

    def body(*refs):
        outs = refs[nw:2 * nw]
        send_sems, recv_sems = refs[2 * nw:]
        x, y, c = _mesh_pos()
        cps = []
        for w in range(nw):
            cp = pltpu.make_async_remote_copy(
                src_ref=outs[w].at[c], dst_ref=outs[w].at[c], send_sem=send_sems.at[w], recv_sem=recv_sems.at[w],
                device_id=(x, y, 1 - c), device_id_type=MESH_T)
            cp.start()
            cps.append(cp)
        for w in range(nw):
            cps[w].wait_send()
            pltpu.make_async_remote_copy(
                src_ref=outs[w].at[1 - c], dst_ref=outs[w].at[1 - c], send_sem=send_sems.at[w],
                recv_sem=recv_sems.at[w], device_id=(x, y, 1 - c), device_id_type=MESH_T).wait_recv()

    hbm = pl.BlockSpec(memory_space=pl.ANY)
    return pl.pallas_call(
        body, name="grad_join_halves", in_specs=[hbm] * nw, out_specs=[hbm] * nw,
        out_shape=[jax.ShapeDtypeStruct(h.shape, F32) for h in halves],
        input_output_aliases={i: i for i in range(nw)},
        scratch_shapes=[pltpu.SemaphoreType.DMA((nw,))] * 2,
        compiler_params=pltpu.CompilerParams(vmem_limit_bytes=VMEM_LIMIT_V7X),
    )(*halves)


MOD_COLS = N_MOD * D_MODEL // N_SHARD
MOD_TILE = 512


def mod_fwd(c16, w_mod):
    def body(c_ref, w_ref, s_ref, o_ref):
        cv = c_ref[...]
        s = cv * _sigmoid(cv)
        s_ref[...] = s
        o_ref[...] = jnp.dot(s.astype(BF16), w_ref[...].astype(BF16), preferred_element_type=F32)

    return pl.pallas_call(
        body, name="mod_fwd", grid=(MOD_COLS // MOD_TILE,),
        in_specs=[_full((16, D_MODEL)), pl.BlockSpec((D_MODEL, MOD_TILE), lambda j: (0, j))],
        out_specs=[_full((16, D_MODEL)), pl.BlockSpec((16, MOD_TILE), lambda j: (0, j))],
        out_shape=[jax.ShapeDtypeStruct((16, D_MODEL), F32), jax.ShapeDtypeStruct((16, MOD_COLS), F32)],
        compiler_params=_params("arbitrary"),
    )(c16, w_mod)


def mod_bwd(s16, dm16, w_mod):
    hi = lax.Precision.HIGHEST

    def body(s_ref, d_ref, w_ref, gw_ref, ds_ref):
        j = pl.program_id(0)
        dm = d_ref[...]
        gw_ref[...] = lax.dot_general(s_ref[...], dm, (((0,), (0,)), ((), ())), preferred_element_type=F32, precision=hi)
        part = lax.dot_general(dm, w_ref[...], (((1,), (1,)), ((), ())), preferred_element_type=F32, precision=hi)

        @pl.when(j == 0)
        def _():
            ds_ref[...] = part

        @pl.when(j > 0)
        def _():
            ds_ref[...] = ds_ref[...] + part

    return pl.pallas_call(
        body, name="mod_bwd", grid=(MOD_COLS // MOD_TILE,),
        in_specs=[_full((16, D_MODEL)), pl.BlockSpec((16, MOD_TILE), lambda j: (0, j)),
                  pl.BlockSpec((D_MODEL, MOD_TILE), lambda j: (0, j))],
        out_specs=[pl.BlockSpec((D_MODEL, MOD_TILE), lambda j: (0, j)), _full((16, D_MODEL))],
        out_shape=[jax.ShapeDtypeStruct((D_MODEL, MOD_COLS), F32), jax.ShapeDtypeStruct((16, D_MODEL), F32)],
        compiler_params=_params("arbitrary"),
    )(s16, dm16, w_mod)


def cctx_grad(parts, c_ctx):
    def body(p_ref, c_ref, o_ref):
        ds = p_ref[0:1, :]
        for s in range(1, N_SHARD):
            ds = ds + p_ref[16 * s:16 * s + 1, :]
        cv = c_ref[...]
        sg = _sigmoid(cv)
        o_ref[...] = ds * (sg * (1.0 + cv * (1.0 - sg)))

    return pl.pallas_call(
        body, name="cctx_grad", in_specs=[_full((N_DEV * 8, D_MODEL)), _full((1, D_MODEL))],
        out_specs=_full((1, D_MODEL)), out_shape=jax.ShapeDtypeStruct((1, D_MODEL), F32),
    )(parts, c_ctx)


def add_rows(a, b, name):
    def body(a_ref, b_ref, o_ref):
        o_ref[...] = a_ref[...] + b_ref[...]

    return pl.pallas_call(body, name=name, in_specs=[_full(a.shape), _full(b.shape)], out_specs=_full(a.shape),
                          out_shape=jax.ShapeDtypeStruct(a.shape, F32))(a, b)


def adamw(w, g, m, v, name):
    r, c = w.shape
    tr = 128 if (r % 128 == 0 and r > 128) else r

    def body(w_ref, g_ref, m_ref, v_ref, d_ref, nm_ref, nv_ref):
        g_ = g_ref[...]
        m_ = ADAM_B1 * m_ref[...] + (1.0 - ADAM_B1) * g_
        v_ = ADAM_B2 * v_ref[...] + (1.0 - ADAM_B2) * (g_ * g_)
        m_hat = m_ / (1.0 - ADAM_B1 ** ADAM_STEP)
        v_hat = v_ / (1.0 - ADAM_B2 ** ADAM_STEP)
        d_ref[...] = -ADAM_LR * (m_hat / (jnp.sqrt(v_hat) + ADAM_EPS) + ADAM_WD * w_ref[...])
        nm_ref[...] = m_
        nv_ref[...] = v_

    spec = pl.BlockSpec((tr, c), lambda i: (i, 0))
    shp = jax.ShapeDtypeStruct((r, c), F32)
    return pl.pallas_call(
        body, name=name, grid=(r // tr,), in_specs=[spec] * 4, out_specs=[spec] * 3, out_shape=[shp] * 3,
        compiler_params=_params("parallel"),
    )(w, g, m, v)


LANES = 1024


def _pack(arrs):
    rows, spans, at = [], [], 0
    for a in arrs:
        n = int(np.prod(a.shape))
        nr = 8 * -(-n // (8 * LANES))
        flat = a.reshape(-1)
        if nr * LANES != n:
            flat = jnp.concatenate([flat, jnp.zeros((nr * LANES - n,), F32)])
        rows.append(flat.reshape(nr, LANES))
        spans.append((at, nr, n, a.shape))
        at += nr
    return jnp.concatenate(rows, axis=0), spans


def _unpack(buf, spans):
    out = []
    for at, nr, n, shape in spans:
        out.append(buf[at:at + nr].reshape(-1)[:n].reshape(shape))
    return out


SMALL_SHARD = ("lru_conv_w", "lru_ba", "lru_bx", "lru_lambda", "ffn_conv_w")


def kernel(x, c, ctx, c_ctx, w_mod, b_mod, norm_mix_g, norm_ffn_g, w_in, lru_conv_w, lru_conv_b, lru_wa, lru_ba, lru_wx, lru_bx, lru_lambda, q_norm_g, k_norm_g, na_rpb, w_rnn_out, w_na_out, w_out, w_up, ffn_conv_w, ffn_conv_b, w_down, loss_target, m_c_ctx, m_w_mod, m_b_mod, m_norm_mix_g, m_norm_ffn_g, m_w_in, m_lru_conv_w, m_lru_conv_b, m_lru_wa, m_lru_ba, m_lru_wx, m_lru_bx, m_lru_lambda, m_q_norm_g, m_k_norm_g, m_na_rpb, m_w_rnn_out, m_w_na_out, m_w_out, m_w_up, m_ffn_conv_w, m_ffn_conv_b, m_w_down, v_c_ctx, v_w_mod, v_b_mod, v_norm_mix_g, v_norm_ffn_g, v_w_in, v_lru_conv_w, v_lru_conv_b, v_lru_wa, v_lru_ba, v_lru_wx, v_lru_bx, v_lru_lambda, v_q_norm_g, v_k_norm_g, v_na_rpb, v_w_rnn_out, v_w_na_out, v_w_out, v_w_up, v_ffn_conv_w, v_ffn_conv_b, v_w_down):
    weights = dict(c_ctx=c_ctx, w_mod=w_mod, b_mod=b_mod, norm_mix_g=norm_mix_g, norm_ffn_g=norm_ffn_g, w_in=w_in,
                   lru_conv_w=lru_conv_w, lru_conv_b=lru_conv_b, lru_wa=lru_wa, lru_ba=lru_ba, lru_wx=lru_wx,
                   lru_bx=lru_bx, lru_lambda=lru_lambda, q_norm_g=q_norm_g, k_norm_g=k_norm_g, na_rpb=na_rpb,
                   w_rnn_out=w_rnn_out, w_na_out=w_na_out, w_out=w_out, w_up=w_up, ffn_conv_w=ffn_conv_w,
                   ffn_conv_b=ffn_conv_b, w_down=w_down)
    mom1 = dict(c_ctx=m_c_ctx, w_mod=m_w_mod, b_mod=m_b_mod, norm_mix_g=m_norm_mix_g, norm_ffn_g=m_norm_ffn_g,
                w_in=m_w_in, lru_conv_w=m_lru_conv_w, lru_conv_b=m_lru_conv_b, lru_wa=m_lru_wa, lru_ba=m_lru_ba,
                lru_wx=m_lru_wx, lru_bx=m_lru_bx, lru_lambda=m_lru_lambda, q_norm_g=m_q_norm_g, k_norm_g=m_k_norm_g,
                na_rpb=m_na_rpb, w_rnn_out=m_w_rnn_out, w_na_out=m_w_na_out, w_out=m_w_out, w_up=m_w_up,
                ffn_conv_w=m_ffn_conv_w, ffn_conv_b=m_ffn_conv_b, w_down=m_w_down)
    mom2 = dict(c_ctx=v_c_ctx, w_mod=v_w_mod, b_mod=v_b_mod, norm_mix_g=v_norm_mix_g, norm_ffn_g=v_norm_ffn_g,
                w_in=v_w_in, lru_conv_w=v_lru_conv_w, lru_conv_b=v_lru_conv_b, lru_wa=v_lru_wa, lru_ba=v_lru_ba,
                lru_wx=v_lru_wx, lru_bx=v_lru_bx, lru_lambda=v_lru_lambda, q_norm_g=v_q_norm_g, k_norm_g=v_k_norm_g,
                na_rpb=v_na_rpb, w_rnn_out=v_w_rnn_out, w_na_out=v_w_na_out, w_out=v_w_out, w_up=v_w_up,
                ffn_conv_w=v_ffn_conv_w, ffn_conv_b=v_ffn_conv_b, w_down=v_w_down)
    order = list(weights)
    d = D_MODEL
    mx_, my_, mc_ = _mesh_pos()
    shard = 2 * mx_ + my_
    dev = 2 * shard + mc_

    local_small, small_spans = _pack([c] + [weights[k][0] for k in SMALL_SHARD])
    gath = all_gather8(local_small, "gather_small").reshape(N_DEV, local_small.shape[0], LANES)
    per_dev = [_unpack(gath[k], small_spans) for k in range(N_DEV)]
    c_all = jnp.concatenate([per_dev[k][0] for k in range(N_DEV)], axis=0)
    full_small = {name: jnp.concatenate([per_dev[2 * s][1 + i] for s in range(N_SHARD)], axis=-1)
                  for i, name in enumerate(SMALL_SHARD)}
    c16 = jnp.concatenate([c_all, c_ctx.reshape(1, d), jnp.zeros((7, d), F32)], axis=0)
    s16, mod_part = mod_fwd(c16, w_mod[0])
    mod_all = all_gather8(mod_part, "gather_mod").reshape(N_DEV, 16, MOD_COLS)
    mod = jnp.concatenate([mod_all[2 * s] for s in range(N_SHARD)], axis=1) + b_mod
    modx = lax.dynamic_slice(mod, (dev, 0), (1, N_MOD * d))
    modc = mod[8:9]

    idx = jnp.stack([shard, mc_]).astype(jnp.int32)
    c_idx = jnp.reshape(mc_, (1,)).astype(jnp.int32)
    wsh = {name: cast_into_full(weights[name][0], full, axis, idx, "cast_" + name) for name, full, axis in BIG}
    w_in_full = run_comm(gather_weights_comm([wsh["w_in"]], [0]), "gather_w_in")[0]

    z = jnp.concatenate([ctx[0], x[0]], axis=0)
    res = local_step(z, loss_target[0], modx, modc, norm_mix_g, norm_ffn_g, w_in_full, full_small["lru_conv_w"],
                     lru_conv_b, lru_wa[0], full_small["lru_ba"], lru_wx[0], full_small["lru_bx"],
                     full_small["lru_lambda"], q_norm_g, k_norm_g, na_rpb[0], wsh["w_rnn_out"], wsh["w_na_out"],
                     wsh["w_out"], wsh["w_up"], full_small["ffn_conv_w"], ffn_conv_b, wsh["w_down"], c_idx=c_idx)

    halves = [add_pieces(res["partials"][i], res["pieces"][i], idx, BIG[i][2], "add_pieces_" + BIG[i][0])
              for i in range(len(BIG))]
    joined = join_halves(halves)
    grads = {name: joined[i].reshape(_shard_shape(full, axis)) for i, (name, full, axis) in enumerate(BIG)}

    small_names = ["norm_mix_g", "norm_ffn_g", "lru_conv_w", "lru_conv_b", "lru_wa", "lru_ba", "lru_wx", "lru_bx",
                   "lru_lambda", "q_norm_g", "k_norm_g", "na_rpb", "ffn_conv_w", "ffn_conv_b"]
    local_g, g_spans = _pack([res["loss_sq"][0:1, 0:1], res["d_modx"], res["d_modc"]] + [res[k] for k in small_names])
    n_rows = local_g.shape[0]
    g_all, g_tot = all_gather8(local_g, "allreduce_small", with_sum=True)
    tot = _unpack(g_tot, g_spans)
    loss = (0.5 / d) * tot[0][0, 0]
    small_tot = dict(zip(small_names, tot[3:]))
    at_x = g_spans[1][0]
    dmx_rows = g_all.reshape(N_DEV, n_rows, LANES)[:, at_x:at_x + N_MOD, :].reshape(N_DEV, N_MOD * d)
    dmc_row = jnp.concatenate([tot[2], jnp.zeros((1, 4 * d), F32)], axis=1)
    dm16 = jnp.concatenate([dmx_rows, dmc_row, jnp.zeros((7, N_MOD * d), F32)], axis=0)
    grads["b_mod"] = add_rows(tot[1], dmc_row, "b_mod_grad")
    g_w_mod, ds16 = mod_bwd(s16, lax.dynamic_slice(dm16, (0, shard * MOD_COLS), (16, MOD_COLS)), w_mod[0])
    grads["w_mod"] = g_w_mod
    ds_parts = all_gather8(ds16[8:16], "gather_dsctx")
    grads["c_ctx"] = cctx_grad(ds_parts, c_ctx.reshape(1, d))
    for k in small_names:
        g = small_tot[k]
        if k in SMALL_SHARD:
            w_sh = weights[k].shape[-1]
            g = lax.dynamic_slice_in_dim(g, shard * w_sh, w_sh, axis=g.ndim - 1)
        grads[k] = g

    delta, new_m, new_v = {}, {}, {}
    for name, _, _ in BIG + (("w_mod", None, None),):
        delta[name], new_m[name], new_v[name] = adamw(weights[name][0], grads[name], mom1[name][0], mom2[name][0],
                                                      "adamw_" + name)
    rest = [k for k in order if k not in delta]
    pw, spans_w = _pack([weights[k] for k in rest])
    pg, _ = _pack([grads[k].reshape(weights[k].shape) for k in rest])
    pm, _ = _pack([mom1[k] for k in rest])
    pv, _ = _pack([mom2[k] for k in rest])
    pd, pnm, pnv = adamw(pw, pg, pm, pv, "adamw_small")
    for k, a, b_, c_ in zip(rest, _unpack(pd, spans_w), _unpack(pnm, spans_w), _unpack(pnv, spans_w)):
        delta[k], new_m[k], new_v[k] = a, b_, c_

    shaped = lambda t: [t[k].reshape(weights[k].shape) for k in order]
    return (loss, res["grad_x"][None], *shaped(grads), *shaped(delta), *shaped(new_m), *shaped(new_v))
```

```python
import numpy as np
import jax
import jax.numpy as jnp
from jax import lax
from jax.experimental import pallas as pl
from jax.experimental.pallas import tpu as pltpu

F32 = jnp.float32
BF16 = jnp.bfloat16

D_MODEL = 1024
SEQ = 2048
CTX_LEN = 256
ZLEN = SEQ + CTX_LEN
GRID_W = 64
GRID_ROWS = SEQ // GRID_W
LRU_BLOCK_W = 128
LRU_BLOCKS = 8
LRU_C = 8.0
NA_HEADS = 16
HEAD_DIM = 64
NA_ROWS = 8
NA_COLS = 16
ROPE_BASE = 10000.0
D_FF = 2816
N_MOD = 6
IN_COLS = 7 * D_MODEL
EPS = 1e-6
NEG_INF = -1e30
N_DEV = 8
N_SHARD = 4

ADAM_LR = 0.001
ADAM_B1 = 0.9
ADAM_B2 = 0.999
ADAM_EPS = 1e-08
ADAM_WD = 0.01
ADAM_STEP = 10

ROW_TILE = 256
Q_ROWS = 4
Q_TILE = Q_ROWS * GRID_W
KEY_ROWS = 12
KEY_TILE = KEY_ROWS * GRID_W
BT_PAD = 4
BT_LEN = 24
VMEM_LIMIT_V7X = 56 * 1024 * 1024

MESH_T = pl.DeviceIdType.MESH


def _params(*sem):
    return pltpu.CompilerParams(dimension_semantics=sem if sem else None, vmem_limit_bytes=VMEM_LIMIT_V7X)


def _full(shape):
    nd = len(shape)
    return pl.BlockSpec(shape, lambda *_: (0,) * nd)


class Comm:
    def __init__(self, inputs, out_shapes, aliases, scratch, emit):
        self.inputs, self.out_shapes, self.aliases, self.scratch, self.emit = inputs, out_shapes, aliases, scratch, emit


def _call(body, *, name, grid, in_specs, out_specs, out_shape, args, scratch_shapes=(), sem=(), comm=None):
    n_in, n_out, n_sc = len(in_specs), len(out_specs), len(scratch_shapes)
    if comm is None:
        res = pl.pallas_call(body, name=name, grid=grid, in_specs=list(in_specs), out_specs=list(out_specs),
                             out_shape=list(out_shape), scratch_shapes=list(scratch_shapes),
                             compiler_params=_params(*sem))(*args)
        return list(res), []
    k_in, k_out = len(comm.inputs), len(comm.out_shapes)
    steps = int(np.prod(grid))

    def hosted(*refs):
        ins, cins = refs[:n_in], refs[n_in:n_in + k_in]
        at = n_in + k_in
        outs, couts = refs[at:at + n_out], refs[at + n_out:at + n_out + k_out]
        at += n_out + k_out
        scr, cscr = refs[at:at + n_sc], refs[at + n_sc:]
        start, mid, end = comm.emit(cins, couts, cscr)
        lin = pl.program_id(0)
        for ax in range(1, len(grid)):
            lin = lin * grid[ax] + pl.program_id(ax)
        pl.when(lin == 0)(start)
        body(*ins, *outs, *scr)
        pl.when(lin == steps // 2)(mid)
        pl.when(lin == steps - 1)(end)

    hbm = pl.BlockSpec(memory_space=pl.ANY)
    res = pl.pallas_call(
        hosted, name=name, grid=grid, in_specs=list(in_specs) + [hbm] * k_in, out_specs=list(out_specs) + [hbm] * k_out,
        out_shape=list(out_shape) + list(comm.out_shapes), scratch_shapes=list(scratch_shapes) + list(comm.scratch),
        input_output_aliases={n_in + i: n_out + o for i, o in comm.aliases.items()},
        compiler_params=_params(*(("arbitrary",) * len(grid))))(*args, *comm.inputs)
    return list(res[:n_out]), list(res[n_out:])


def _sigmoid(x):
    return 1.0 / (1.0 + jnp.exp(-x))


def _gelu_parts(x):
    c0 = 0.7978845608028654
    inner = c0 * (x + 0.044715 * x * x * x)
    t = jnp.tanh(inner)
    g = 0.5 * x * (1.0 + t)
    dg = 0.5 * (1.0 + t) + 0.5 * x * (1.0 - t * t) * c0 * (1.0 + 3.0 * 0.044715 * x * x)
    return g, dg


def _dot_nt(a, b):
    return lax.dot_general(a, b, (((1,), (1,)), ((), ())), preferred_element_type=F32)


def _dot_tn(a, b):
    return lax.dot_general(a, b, (((0,), (0,)), ((), ())), preferred_element_type=F32)


def norm_mod(xin, gain, shift, scale, name):
    r, d = xin.shape
    s_mod = shift.shape[0]
    assert r % ROW_TILE == 0

    def body(x_ref, g_ref, sh_ref, sc_ref, xn_ref):
        x = x_ref[...]
        nrm = x * lax.rsqrt(jnp.mean(x * x, axis=-1, keepdims=True) + EPS)
        xn_ref[...] = ((nrm * g_ref[...]) * (1.0 + sc_ref[0]) + sh_ref[0]).astype(BF16)

    mod_spec = pl.BlockSpec((1, 1, d), lambda i: (jnp.minimum(i, s_mod - 1), 0, 0))
    return pl.pallas_call(
        body, name=name, grid=(r // ROW_TILE,),
        in_specs=[pl.BlockSpec((ROW_TILE, d), lambda i: (i, 0)), _full((1, d)), mod_spec, mod_spec],
        out_specs=pl.BlockSpec((ROW_TILE, d), lambda i: (i, 0)),
        out_shape=jax.ShapeDtypeStruct((r, d), BF16),
        compiler_params=_params("parallel"),
    )(xin, gain, shift, scale)


def matmul_wide(a, b, name, tm, tn, comm=None):
    m, k = a.shape
    n = b.shape[1]
    assert m % tm == 0 and n % tn == 0

    def body(a_ref, b_ref, o_ref):
        o_ref[...] = jnp.dot(a_ref[...], b_ref[...], preferred_element_type=F32)

    res, extra = _call(
        body, name=name, grid=(n // tn, m // tm),
        in_specs=[pl.BlockSpec((tm, k), lambda j, i: (i, 0)), pl.BlockSpec((k, tn), lambda j, i: (0, j))],
        out_specs=[pl.BlockSpec((tm, tn), lambda j, i: (i, j))],
        out_shape=[jax.ShapeDtypeStruct((m, n), F32)],
        sem=("parallel", "parallel"), args=(a, b), comm=comm)
    return res[0], extra


def _row_ids(n, w):
    return lax.broadcasted_iota(jnp.int32, (n, w), 0)


def _lru_conv(xr, cw, cb):
    row = _row_ids(ZLEN, LRU_BLOCK_W)
    segpos = jnp.where(row < CTX_LEN, row, row - CTX_LEN)
    seglen = jnp.where(row < CTX_LEN, CTX_LEN, SEQ)
    acc = xr * cw[2:3, :] + cb
    for k in (0, 1, 3):
        off = k - 2
        sh = pltpu.roll(xr, (-off) % ZLEN, 0)
        ok = (segpos + off >= 0) & (segpos + off < seglen)
        acc = acc + jnp.where(ok, sh, 0.0) * cw[k:k + 1, :]
    return acc


def _lru_conv_t(dxc, cw):
    row = _row_ids(ZLEN, LRU_BLOCK_W)
    segpos = jnp.where(row < CTX_LEN, row, row - CTX_LEN)
    seglen = jnp.where(row < CTX_LEN, CTX_LEN, SEQ)
    acc = dxc * cw[2:3, :]
    for k in (0, 1, 3):
        off = k - 2
        sh = pltpu.roll(dxc, off % ZLEN, 0)
        ok = (segpos - off >= 0) & (segpos - off < seglen)
        acc = acc + jnp.where(ok, sh, 0.0) * cw[k:k + 1, :]
    return acc


def _lru_gates(xc, xcb, wa, ba, wx, bx, lam):
    r = _sigmoid(jnp.dot(xcb, wa, preferred_element_type=F32) + ba)
    i = _sigmoid(jnp.dot(xcb, wx, preferred_element_type=F32) + bx)
    sp = jnp.maximum(-lam, 0.0) + jnp.log1p(jnp.exp(-jnp.abs(lam)))
    la = (-LRU_C) * r * sp
    a = jnp.exp(la)
    sq = jnp.sqrt(-jnp.tanh(la) * (1.0 + a * a))
    b = sq * i * xc
    return r, i, sp, a, sq, b


def _scan8_fwd(a, b, rid):
    for s in (1, 2, 4):
        a_s = pltpu.roll(a, s, 0)
        b_s = pltpu.roll(b, s, 0)
        m = rid >= s
        b = jnp.where(m, a * b_s + b, b)
        a = jnp.where(m, a * a_s, a)
    return a, b


def _scan8_rev(a, b, rid):
    for s in (1, 2, 4):
        a_s = pltpu.roll(a, 8 - s, 0)
        b_s = pltpu.roll(b, 8 - s, 0)
        m = rid < 8 - s
        b = jnp.where(m, a * b_s + b, b)
        a = jnp.where(m, a * a_s, a)
    return a, b


N_CHUNK = ZLEN // 8
CTX_CHUNKS = CTX_LEN // 8
SCAN_UNROLL = 8


def _scan_up(a_ref, b_ref, h_ref, lo, hi, carry):
    rid = _row_ids(8, LRU_BLOCK_W)
    assert (hi - lo) % SCAN_UNROLL == 0

    def step(g, c):
        base = pl.multiple_of((lo + g * SCAN_UNROLL) * 8, 8)
        for u in range(SCAN_UNROLL):
            sl = pl.ds(base + 8 * u, 8)
            a, b = _scan8_fwd(a_ref[sl, :], b_ref[sl, :], rid)
            h = b + a * c
            h_ref[sl, :] = h
            c = h[7:8, :]
        return c

    return lax.fori_loop(0, (hi - lo) // SCAN_UNROLL, step, carry)


def _scan_down(a_ref, b_ref, h_ref, lo, hi, carry):
    rid = _row_ids(8, LRU_BLOCK_W)
    assert (hi - lo) % SCAN_UNROLL == 0

    def step(g, c):
        base = pl.multiple_of((hi - (g + 1) * SCAN_UNROLL) * 8, 8)
        for u in reversed(range(SCAN_UNROLL)):
            sl = pl.ds(base + 8 * u, 8)
            a, b = _scan8_rev(a_ref[sl, :], b_ref[sl, :], rid)
            h = b + a * c
            h_ref[sl, :] = h
            c = h[0:1, :]
        return c

    return lax.fori_loop(0, (hi - lo) // SCAN_UNROLL, step, carry)


def _lru_scan_dir(d, a_ref, b_ref, h_ref):
    zero = jnp.zeros((1, LRU_BLOCK_W), F32)
    if d == 0:
        _scan_up(a_ref, b_ref, h_ref, 0, N_CHUNK, zero)
    else:
        c = _scan_down(a_ref, b_ref, h_ref, 0, CTX_CHUNKS, zero)
        _scan_down(a_ref, b_ref, h_ref, CTX_CHUNKS, N_CHUNK, c)


def _lru_in_specs():
    blk = lambda rows: pl.BlockSpec((rows, LRU_BLOCK_W), lambda b: (0, b))
    wspec = pl.BlockSpec((2, 1, LRU_BLOCK_W, LRU_BLOCK_W), lambda b: (0, b, 0, 0))
    return blk, wspec


def lru_fwd(p, conv_w, conv_b, wa, ba, wx, bx, lam, comm=None):
    blk, wspec = _lru_in_specs()

    def body(xr_ref, gx_ref, cw_ref, cb_ref, wa_ref, ba_ref, wx_ref, bx_ref, lam_ref, y_ref, a_s, b_s, h_s, hsum_s):
        xr = xr_ref[...]
        xc = _lru_conv(xr, cw_ref[...], cb_ref[...])
        xcb = xc.astype(BF16)
        for d in (0, 1):
            _, _, _, a, _, b = _lru_gates(xc, xcb, wa_ref[d, 0].astype(BF16), ba_ref[d:d + 1, :],
                                          wx_ref[d, 0].astype(BF16), bx_ref[d:d + 1, :], lam_ref[d:d + 1, :])
            a_s[...] = a
            b_s[...] = b
            _lru_scan_dir(d, a_s, b_s, h_s)
            if d == 0:
                hsum_s[...] = h_s[...]
            else:
                hsum_s[...] = hsum_s[...] + h_s[...]
        g, _ = _gelu_parts(gx_ref[CTX_LEN:, :])
        y_ref[...] = (hsum_s[CTX_LEN:, :] * g).astype(BF16)

    zs = pltpu.VMEM((ZLEN, LRU_BLOCK_W), F32)
    res, extra = _call(
        body, name="lru_fwd", grid=(LRU_BLOCKS,),
        in_specs=[blk(ZLEN), pl.BlockSpec((ZLEN, LRU_BLOCK_W), lambda b: (0, 24 + b)), blk(4), blk(1),
                  wspec, blk(2), wspec, blk(2), blk(2)],
        out_specs=[pl.BlockSpec((SEQ, LRU_BLOCK_W), lambda b: (0, b))],
        out_shape=[jax.ShapeDtypeStruct((SEQ, D_MODEL), BF16)],
        scratch_shapes=[zs, zs, zs, zs], sem=("arbitrary",),
        args=(p, p, conv_w, conv_b, wa, ba, wx, bx, lam), comm=comm)
    return res[0], extra


def _rope_tables():
    t = np.arange(SEQ)
    lane = np.arange(2 * HEAD_DIM)
    in_head = lane % HEAD_DIM
    j = (in_head % 32) % 16
    freq = ROPE_BASE ** (-j.astype(np.float64) / 16.0)
    pos = np.where(in_head[None, :] < 32, (t // GRID_W)[:, None], (t % GRID_W)[:, None]).astype(np.float64)
    ang = (pos.astype(np.float32) * freq.astype(np.float32)[None, :]).astype(np.float32)
    cos = np.cos(ang).astype(np.float32)
    sin = np.sin(ang).astype(np.float32)
    sgn = np.where((in_head % 32) < 16, -1.0, 1.0).astype(np.float32)
    cos = np.concatenate([np.ones((CTX_LEN, 2 * HEAD_DIM), np.float32), cos], 0)
    sin = np.concatenate([np.zeros((CTX_LEN, 2 * HEAD_DIM), np.float32), sin * sgn[None, :]], 0)
    return jnp.asarray(cos), jnp.asarray(sin)


def _head_ones():
    lane = np.arange(2 * HEAD_DIM)
    return jnp.asarray((lane[:, None] // HEAD_DIM == lane[None, :] // HEAD_DIM).astype(np.float32))


def _rope_partner(x):
    lane = lax.broadcasted_iota(jnp.int32, x.shape, 1)
    return jnp.where((lane % 32) < 16, pltpu.roll(x, 128 - 16, 1), pltpu.roll(x, 16, 1))


def _head_rms(x, ones, gain):
    ms = jnp.dot(x * x, ones, preferred_element_type=F32, precision=lax.Precision.HIGHEST) * (1.0 / HEAD_DIM)
    rstd = lax.rsqrt(ms + EPS)
    return x * rstd * gain, rstd


PREP_TILE = 768


def qkv_prep(p, qg2, kg2, cos, sin, ones):
    scale = HEAD_DIM ** -0.5

    def body(q_ref, k_ref, v_ref, qg_ref, kg_ref, cos_ref, sin_ref, ones_ref, qr_ref, qp_ref, kk_ref, vv_ref):
        ones_m = ones_ref[...]
        c, s = cos_ref[...], sin_ref[...]
        qn, _ = _head_rms(q_ref[...], ones_m, qg_ref[...])
        qn = qn * scale
        qr_ref[...] = (qn * c + _rope_partner(qn) * s).astype(BF16)
        qp_ref[...] = qn.astype(BF16)
        kn, _ = _head_rms(k_ref[...], ones_m, kg_ref[...])
        kk_ref[...] = (kn * c + _rope_partner(kn) * s).astype(BF16)
        vv_ref[...] = v_ref[...].astype(BF16)

    col = lambda base: pl.BlockSpec((PREP_TILE, 128), lambda hp, i: (i, base + hp))
    small = pl.BlockSpec((1, 128), lambda hp, i: (0, 0))
    tab = pl.BlockSpec((PREP_TILE, 128), lambda hp, i: (i, 0))
    oshape = jax.ShapeDtypeStruct((ZLEN, D_MODEL), BF16)
    return pl.pallas_call(
        body, name="qkv_prep", grid=(NA_HEADS // 2, ZLEN // PREP_TILE),
        in_specs=[col(32), col(8), col(16), small, small, tab, tab, _full((128, 128))],
        out_specs=[col(0)] * 4, out_shape=[oshape] * 4,
        compiler_params=_params("parallel", "parallel"),
    )(p, p, p, qg2, kg2, cos, sin, ones)


def _bias_expand():
    qc = np.arange(GRID_W)[:, None]
    kc = np.arange(GRID_W)[None, :]
    col_start = np.clip(qc - NA_COLS // 2, 0, GRID_W - NA_COLS)
    in_win = (kc >= col_start) & (kc < col_start + NA_COLS)
    dc = np.clip(kc - qc, -(NA_COLS - 1), NA_COLS - 1) + (NA_COLS - 1)
    e = np.zeros((2 * NA_COLS - 1, GRID_W, GRID_W), np.float32)
    for d in range(2 * NA_COLS - 1):
        e[d] = ((dc == d) & in_win).astype(np.float32)
    pen = np.where(in_win, 0.0, NEG_INF).astype(np.float32)
    return e, pen


def bias_table(rpb2):
    e, pen = _bias_expand()
    n_dr = 2 * NA_ROWS - 1
    ea = np.zeros((31, GRID_W, 128), np.float32)
    ea[:, :, :GRID_W] = e
    eb = np.zeros((31, GRID_W, 128), np.float32)
    eb[:, :, GRID_W:] = e
    pen2 = np.concatenate([pen, pen], 1)
    ea = jnp.asarray(ea.reshape(31, GRID_W * 128))
    eb = jnp.asarray(eb.reshape(31, GRID_W * 128))
    sel_a = np.zeros((BT_LEN, n_dr), np.float32)
    sel_b = np.zeros((BT_LEN, n_dr), np.float32)
    for r in range(BT_LEN):
        dr = r - BT_PAD
        if 0 <= dr < n_dr:
            sel_a[r, dr] = 1.0
        if 0 <= dr + 1 < n_dr:
            sel_b[r, dr + 1] = 1.0
    sel_a, sel_b = jnp.asarray(sel_a), jnp.asarray(sel_b)
    pen2 = jnp.asarray(pen2.reshape(1, GRID_W * 128))
    hi = lax.Precision.HIGHEST

    def body(rpb_ref, sa_ref, sb_ref, ea_ref, eb_ref, pen_ref, o_ref, ra_s, rb_s):
        for h in range(NA_HEADS):
            rp = rpb_ref[h]
            ra_s[h * BT_LEN:(h + 1) * BT_LEN, :] = jnp.dot(sa_ref[...], rp, preferred_element_type=F32, precision=hi)
            rb_s[h * BT_LEN:(h + 1) * BT_LEN, :] = jnp.dot(sb_ref[...], rp, preferred_element_type=F32, precision=hi)
        o_ref[...] = (jnp.dot(ra_s[...], ea_ref[...], preferred_element_type=F32, precision=hi)
                      + jnp.dot(rb_s[...], eb_ref[...], preferred_element_type=F32, precision=hi) + pen_ref[...])

    tcol = 2048
    rows = NA_HEADS * BT_LEN
    out = pl.pallas_call(
        body, name="bias_table", grid=(GRID_W * 128 // tcol,),
        in_specs=[_full((NA_HEADS, n_dr, 31)), _full((BT_LEN, n_dr)), _full((BT_LEN, n_dr)),
                  pl.BlockSpec((31, tcol), lambda j: (0, j)), pl.BlockSpec((31, tcol), lambda j: (0, j)),
                  pl.BlockSpec((1, tcol), lambda j: (0, j))],
        out_specs=pl.BlockSpec((rows, tcol), lambda j: (0, j)),
        out_shape=jax.ShapeDtypeStruct((rows, GRID_W * 128), F32),
        scratch_shapes=[pltpu.VMEM((rows, 31), F32), pltpu.VMEM((rows, 31), F32)],
        compiler_params=_params("parallel"),
    )(rpb2, sel_a, sel_b, ea, eb, pen2)
    return out.reshape(NA_HEADS, BT_LEN, GRID_W, 128)


def _key_window(j):
    ws = jnp.clip(Q_ROWS * j - 4, 0, GRID_ROWS - KEY_ROWS)
    return ws, pl.multiple_of(CTX_LEN + ws * GRID_W, 256)


def _head_mask(hh):
    lane = lax.broadcasted_iota(jnp.int32, (Q_TILE, 128), 1)
    return (lane < HEAD_DIM) if hh == 0 else (lane >= HEAD_DIM)


def _attn_scores(j, ws, q_rot_h, q_pl_h, kw, kc, hh, bt_ref, s_ref):
    s_ref[:, :KEY_TILE] = _dot_nt(q_rot_h, kw)
    s_ref[:, KEY_TILE:] = _dot_nt(q_pl_h, kc)
    lane = lax.broadcasted_iota(jnp.int32, (GRID_W, 128), 1)
    base = ws - Q_ROWS * j + (NA_ROWS - 1) + BT_PAD
    for qi in range(Q_ROWS):
        rs = jnp.clip(Q_ROWS * j + qi - NA_ROWS // 2, 0, GRID_ROWS - NA_ROWS)
        for m in range(KEY_ROWS // 2):
            k0 = ws + 2 * m
            p0 = jnp.where((k0 >= rs) & (k0 < rs + NA_ROWS), 0.0, NEG_INF)
            p1 = jnp.where((k0 + 1 >= rs) & (k0 + 1 < rs + NA_ROWS), 0.0, NEG_INF)
            pen = jnp.where(lane < GRID_W, p0, p1)
            rows = slice(qi * GRID_W, (qi + 1) * GRID_W)
            cols = slice(128 * m, 128 * (m + 1))
            s_ref[rows, cols] = s_ref[rows, cols] + bt_ref[hh, base + 2 * m - qi] + pen
    return base


def attn_fwd(q_rot, q_pl, kk, vv, bt, comm=None):
    def body(qr_ref, qp_ref, kk_ref, vv_ref, bt_ref, o_ref, lse_ref, s_ref):
        j = pl.program_id(1)
        ws, start = _key_window(j)
        win = pl.ds(start, KEY_TILE)
        kw, kc = kk_ref[win, :], kk_ref[:CTX_LEN, :]
        vw, vc = vv_ref[win, :], vv_ref[:CTX_LEN, :]
        qr, qp = qr_ref[...], qp_ref[...]
        outs = []
        for hh in range(2):
            msk = _head_mask(hh)
            _attn_scores(j, ws, jnp.where(msk, qr, 0), jnp.where(msk, qp, 0), kw, kc, hh, bt_ref, s_ref)
            s = s_ref[...]
            mx = jnp.max(s, axis=-1, keepdims=True)
            pr = jnp.exp(s - mx)
            l = jnp.sum(pr, axis=-1, keepdims=True)
            prb = pr.astype(BF16)
            o = jnp.dot(prb[:, :KEY_TILE], vw, preferred_element_type=F32)
            o = o + jnp.dot(prb[:, KEY_TILE:], vc, preferred_element_type=F32)
            outs.append(o / l)
            lse_ref[hh] = mx + jnp.log(l)
        o_ref[...] = jnp.where(_head_mask(0), outs[0], outs[1])

    qspec = pl.BlockSpec((Q_TILE, 128), lambda hp, j: (j + 1, hp))
    kspec = pl.BlockSpec((ZLEN, 128), lambda hp, j: (0, hp))
    res, extra = _call(
        body, name="attn_fwd", grid=(NA_HEADS // 2, SEQ // Q_TILE),
        in_specs=[qspec, qspec, kspec, kspec, pl.BlockSpec((2, BT_LEN, GRID_W, 128), lambda hp, j: (hp, 0, 0, 0))],
        out_specs=[pl.BlockSpec((Q_TILE, 128), lambda hp, j: (j, hp)),
                   pl.BlockSpec((2, Q_TILE, 1), lambda hp, j: (hp, j, 0))],
        out_shape=[jax.ShapeDtypeStruct((SEQ, D_MODEL), F32), jax.ShapeDtypeStruct((NA_HEADS, SEQ, 1), F32)],
        scratch_shapes=[pltpu.VMEM((Q_TILE, KEY_TILE + CTX_LEN), F32)], sem=("parallel", "arbitrary"),
        args=(q_rot, q_pl, kk, vv, bt), comm=comm)
    return res[0], res[1], extra


def merge_fwd(y_rnn, y_na, p, z, g2, w_rnn, w_na, w_out):
    def body(yr_ref, yn_ref, mr_ref, mn_ref, x_ref, g2_ref, wr_ref, wn_ref, wo_ref, u_ref, v_ref, mg_ref, out_ref, x1_ref):
        u = jnp.dot(yr_ref[...], wr_ref[...], preferred_element_type=F32)
        v = jnp.dot(yn_ref[...].astype(BF16), wn_ref[...], preferred_element_type=F32)
        merged = (_sigmoid(mr_ref[...]) * u + _sigmoid(mn_ref[...]) * v).astype(BF16)
        out = jnp.dot(merged, wo_ref[...], preferred_element_type=F32)
        u_ref[...] = u
        v_ref[...] = v
        mg_ref[...] = merged
        out_ref[...] = out
        x1_ref[...] = x_ref[...] + g2_ref[...] * out

    row = pl.BlockSpec((ROW_TILE, D_MODEL), lambda i: (i, 0))
    lat = lambda cb: pl.BlockSpec((ROW_TILE, D_MODEL), lambda i: (i + 1, cb))
    wspec = _full((D_MODEL, D_MODEL))
    f32o = jax.ShapeDtypeStruct((SEQ, D_MODEL), F32)
    return pl.pallas_call(
        body, name="merge_fwd", grid=(SEQ // ROW_TILE,),
        in_specs=[row, row, lat(5), lat(6), lat(0), _full((1, D_MODEL)), wspec, wspec, wspec],
        out_specs=[row] * 5,
        out_shape=[f32o, f32o, jax.ShapeDtypeStruct((SEQ, D_MODEL), BF16), f32o, f32o],
        compiler_params=_params("parallel"),
    )(y_rnn, y_na, p, p, z, g2, w_rnn, w_na, w_out)


FF_TILE = 256
FF_TILES = D_FF // FF_TILE


def _ffn_conv(h, cw, cb):
    row = _row_ids(SEQ, FF_TILE)
    prev = jnp.where(row >= 1, pltpu.roll(h, 1, 0), 0.0)
    nxt = jnp.where(row < SEQ - 1, pltpu.roll(h, SEQ - 1, 0), 0.0)
    return prev * cw[0:1, :] + h * cw[1:2, :] + nxt * cw[2:3, :] + cb


def ffn_act(hpre, conv_w, conv_b):
    def body(ha_ref, hg_ref, wa_ref, wg_ref, ba_ref, bg_ref, o_ref):
        a = _ffn_conv(ha_ref[...], wa_ref[...], ba_ref[...])
        g = _ffn_conv(hg_ref[...], wg_ref[...], bg_ref[...])
        o_ref[...] = (a * _sigmoid(a) * g).astype(BF16)

    col = lambda rows, off: pl.BlockSpec((rows, FF_TILE), lambda j: (0, j + off))
    return pl.pallas_call(
        body, name="ffn_act", grid=(FF_TILES,),
        in_specs=[col(SEQ, 0), col(SEQ, FF_TILES), col(3, 0), col(3, FF_TILES), col(1, 0), col(1, FF_TILES)],
        out_specs=col(SEQ, 0),
        out_shape=jax.ShapeDtypeStruct((SEQ, D_FF), BF16),
        compiler_params=_params("parallel"),
    )(hpre, hpre, conv_w, conv_w, conv_b, conv_b)


def ffn_down_loss(act, w_down, x1, g5, target):
    def body(a_ref, w_ref, x1_ref, g5_ref, t_ref, f_ref, dy_ref, df_ref, ls_ref, dg_ref):
        i = pl.program_id(0)
        f = jnp.dot(a_ref[...], w_ref[...], preferred_element_type=F32)
        g5 = g5_ref[...]
        err = x1_ref[...] + g5 * f - t_ref[...]
        dy = err * (1.0 / D_MODEL)
        f_ref[...] = f
        dy_ref[...] = dy
        df_ref[...] = (dy * g5).astype(BF16)

        @pl.when(i == 0)
        def _():
            ls_ref[...] = jnp.zeros_like(ls_ref)
            dg_ref[...] = jnp.zeros_like(dg_ref)

        ls_ref[...] = ls_ref[...] + jnp.sum(err * err)
        dg_ref[...] = dg_ref[...] + jnp.sum(dy * f, axis=0, keepdims=True)

    row = pl.BlockSpec((ROW_TILE, D_MODEL), lambda i: (i, 0))
    f32o = jax.ShapeDtypeStruct((SEQ, D_MODEL), F32)
    return pl.pallas_call(
        body, name="ffn_down_loss", grid=(SEQ // ROW_TILE,),
        in_specs=[pl.BlockSpec((ROW_TILE, D_FF), lambda i: (i, 0)), _full((D_FF, D_MODEL)), row, _full((1, D_MODEL)), row],
        out_specs=[row, row, row, _full((8, 128)), _full((1, D_MODEL))],
        out_shape=[f32o, f32o, jax.ShapeDtypeStruct((SEQ, D_MODEL), BF16), jax.ShapeDtypeStruct((8, 128), F32),
                   jax.ShapeDtypeStruct((1, D_MODEL), F32)],
        compiler_params=_params("arbitrary"),
    )(act, w_down, x1, g5, target)


def ffn_down_bwd(df, w_down):
    def body(df_ref, w_ref, o_ref):
        o_ref[...] = _dot_nt(df_ref[...], w_ref[...])

    return pl.pallas_call(
        body, name="ffn_down_bwd", grid=(SEQ // ROW_TILE,),
        in_specs=[pl.BlockSpec((ROW_TILE, D_MODEL), lambda i: (i, 0)), _full((D_FF, D_MODEL))],
        out_specs=pl.BlockSpec((ROW_TILE, D_FF), lambda i: (i, 0)),
        out_shape=jax.ShapeDtypeStruct((SEQ, D_FF), F32),
        compiler_params=_params("parallel"),
    )(df, w_down)


def ffn_act_bwd(hpre, d_act, conv_w, conv_b):
    def half_bwd(dc, h, w, dh_ref, dw_ref, db_ref):
        row = _row_ids(SEQ, FF_TILE)
        h_prev = jnp.where(row >= 1, pltpu.roll(h, 1, 0), 0.0)
        h_next = jnp.where(row < SEQ - 1, pltpu.roll(h, SEQ - 1, 0), 0.0)
        dw_ref[0:1, :] = jnp.sum(dc * h_prev, axis=0, keepdims=True)
        dw_ref[1:2, :] = jnp.sum(dc * h, axis=0, keepdims=True)
        dw_ref[2:3, :] = jnp.sum(dc * h_next, axis=0, keepdims=True)
        db_ref[...] = jnp.sum(dc, axis=0, keepdims=True)
        dc_next = jnp.where(row < SEQ - 1, pltpu.roll(dc, SEQ - 1, 0), 0.0)
        dc_prev = jnp.where(row >= 1, pltpu.roll(dc, 1, 0), 0.0)
        dh_ref[...] = (dc_next * w[0:1, :] + dc * w[1:2, :] + dc_prev * w[2:3, :]).astype(BF16)

    def body(ha_ref, hg_ref, da_ref, wa_ref, wg_ref, ba_ref, bg_ref, dha_ref, dhg_ref, dwa_ref, dwg_ref, dba_ref, dbg_ref):
        ha, hg = ha_ref[...], hg_ref[...]
        a = _ffn_conv(ha, wa_ref[...], ba_ref[...])
        g = _ffn_conv(hg, wg_ref[...], bg_ref[...])
        sig = _sigmoid(a)
        dact = da_ref[...]
        half_bwd(dact * g * (sig * (1.0 + a * (1.0 - sig))), ha, wa_ref[...], dha_ref, dwa_ref, dba_ref)
        half_bwd(dact * a * sig, hg, wg_ref[...], dhg_ref, dwg_ref, dbg_ref)

    col = lambda rows, off: pl.BlockSpec((rows, FF_TILE), lambda j: (0, j + off))
    hshape = jax.ShapeDtypeStruct((SEQ, D_FF), BF16)
    wshape = jax.ShapeDtypeStruct((3, D_FF), F32)
    bshape = jax.ShapeDtypeStruct((1, D_FF), F32)
    return pl.pallas_call(
        body, name="ffn_act_bwd", grid=(FF_TILES,),
        in_specs=[col(SEQ, 0), col(SEQ, FF_TILES), col(SEQ, 0), col(3, 0), col(3, FF_TILES), col(1, 0), col(1, FF_TILES)],
        out_specs=[col(SEQ, 0), col(SEQ, 0), col(3, 0), col(3, 0), col(1, 0), col(1, 0)],
        out_shape=[hshape, hshape, wshape, wshape, bshape, bshape],
        compiler_params=_params("parallel"),
    )(hpre, hpre, d_act, conv_w, conv_w, conv_b, conv_b)


def _norm_mod_bwd(x, dxn, gain, scale):
    rstd = lax.rsqrt(jnp.mean(x * x, axis=-1, keepdims=True) + EPS)
    nrm = x * rstd
    dsh = jnp.sum(dxn, axis=0, keepdims=True)
    dsc = jnp.sum(dxn * nrm, axis=0, keepdims=True) * gain
    dgn = jnp.sum(dxn * nrm, axis=0, keepdims=True) * (1.0 + scale)
    dn = dxn * (gain * (1.0 + scale))
    dx = rstd * (dn - nrm * jnp.mean(dn * nrm, axis=-1, keepdims=True))
    return dx, dsh, dsc, dgn


def ffn_up_bwd(dha, dhg, w_up, x1, dy, gain, scale):
    def body(dha_ref, dhg_ref, w_ref, x_ref, dy_ref, g_ref, sc_ref, dx_ref, dsh_ref, dsc_ref, dgn_ref):
        i = pl.program_id(0)
        dxn = _dot_nt(dha_ref[...], w_ref[:, :D_FF]) + _dot_nt(dhg_ref[...], w_ref[:, D_FF:])
        dx, dsh, dsc, dgn = _norm_mod_bwd(x_ref[...], dxn, g_ref[...], sc_ref[...])
        dx_ref[...] = dy_ref[...] + dx

        @pl.when(i == 0)
        def _():
            dsh_ref[...] = dsh
            dsc_ref[...] = dsc
            dgn_ref[...] = dgn

        @pl.when(i > 0)
        def _():
            dsh_ref[...] = dsh_ref[...] + dsh
            dsc_ref[...] = dsc_ref[...] + dsc
            dgn_ref[...] = dgn_ref[...] + dgn

    row = pl.BlockSpec((ROW_TILE, D_MODEL), lambda i: (i, 0))
    vec = _full((1, D_MODEL))
    vshape = jax.ShapeDtypeStruct((1, D_MODEL), F32)
    return pl.pallas_call(
        body, name="ffn_up_bwd", grid=(SEQ // ROW_TILE,),
        in_specs=[pl.BlockSpec((ROW_TILE, D_FF), lambda i: (i, 0)), pl.BlockSpec((ROW_TILE, D_FF), lambda i: (i, 0)),
                  _full((D_MODEL, 2 * D_FF)), row, row, vec, vec],
        out_specs=[row, vec, vec, vec],
        out_shape=[jax.ShapeDtypeStruct((SEQ, D_MODEL), F32), vshape, vshape, vshape],
        compiler_params=_params("arbitrary"),
    )(dha, dhg, w_up, x1, dy, gain, scale)


def merge_bwd(dx1, out, g2, p, u, v, w_rnn, w_na, w_out):
    def body(dx_ref, out_ref, g2_ref, mr_ref, mn_ref, u_ref, v_ref, wr_ref, wn_ref, wo_ref,
             dout_ref, du_ref, dv_ref, dmr_ref, dmn_ref, dyr_ref, dyn_ref, dg2_ref):
        i = pl.program_id(0)

        @pl.when(i == 0)
        def _():
            dmr_ref[...] = jnp.zeros_like(dmr_ref)
            dmn_ref[...] = jnp.zeros_like(dmn_ref)
            dg2_ref[...] = jnp.zeros_like(dg2_ref)

        @pl.when(i > 0)
        def _():
            dx = dx_ref[...]
            dg2_ref[...] = dg2_ref[...] + jnp.sum(dx * out_ref[...], axis=0, keepdims=True)
            dout = (dx * g2_ref[...]).astype(BF16)
            dout_ref[...] = dout
            dm = _dot_nt(dout, wo_ref[...])
            sr = _sigmoid(mr_ref[...])
            sn = _sigmoid(mn_ref[...])
            du = (dm * sr).astype(BF16)
            dv = (dm * sn).astype(BF16)
            du_ref[...] = du
            dv_ref[...] = dv
            dmr_ref[...] = (dm * u_ref[...] * (sr * (1.0 - sr))).astype(BF16)
            dmn_ref[...] = (dm * v_ref[...] * (sn * (1.0 - sn))).astype(BF16)
            dyr_ref[...] = _dot_nt(du, wr_ref[...])
            dyn_ref[...] = _dot_nt(dv, wn_ref[...])

    lat = pl.BlockSpec((ROW_TILE, D_MODEL), lambda i: (jnp.maximum(i - 1, 0), 0))
    zrow = pl.BlockSpec((ROW_TILE, D_MODEL), lambda i: (i, 0))
    pcol = lambda cb: pl.BlockSpec((ROW_TILE, D_MODEL), lambda i: (i, cb))
    wspec = _full((D_MODEL, D_MODEL))
    tb = jax.ShapeDtypeStruct((SEQ, D_MODEL), BF16)
    zb = jax.ShapeDtypeStruct((ZLEN, D_MODEL), BF16)
    tf = jax.ShapeDtypeStruct((SEQ, D_MODEL), F32)
    return pl.pallas_call(
        body, name="merge_bwd", grid=(ZLEN // ROW_TILE,),
        in_specs=[lat, lat, _full((1, D_MODEL)), pcol(5), pcol(6), lat, lat, wspec, wspec, wspec],
        out_specs=[lat, lat, lat, zrow, zrow, lat, lat, _full((1, D_MODEL))],
        out_shape=[tb, tb, tb, zb, zb, tf, tf, jax.ShapeDtypeStruct((1, D_MODEL), F32)],
        compiler_params=_params("arbitrary"),
    )(dx1, out, g2, p, p, u, v, w_rnn, w_na, w_out)


def attn_bwd(q_rot, q_pl, kk, vv, bt, y_na, d_yna, lse, comm=None):
    def body(qr_ref, qp_ref, kk_ref, vv_ref, bt_ref, o_ref, do_ref, lse_ref,
             dqr_ref, dqp_ref, dk_ref, dv_ref, dbt_ref, s_ref):
        jj = pl.program_id(1)

        @pl.when(jj == 0)
        def _():
            dqr_ref[...] = jnp.zeros_like(dqr_ref)
            dqp_ref[...] = jnp.zeros_like(dqp_ref)
            dk_ref[...] = jnp.zeros_like(dk_ref)
            dv_ref[...] = jnp.zeros_like(dv_ref)
            dbt_ref[...] = jnp.zeros_like(dbt_ref)

        @pl.when(jj > 0)
        def _():
            j = jj - 1
            ws, start = _key_window(j)
            win = pl.ds(start, KEY_TILE)
            kw, kc = kk_ref[win, :], kk_ref[:CTX_LEN, :]
            vw, vc = vv_ref[win, :], vv_ref[:CTX_LEN, :]
            qr, qp = qr_ref[...], qp_ref[...]
            do = do_ref[...]
            do_o = do * o_ref[...]
            dq_r, dq_p = [], []
            for hh in range(2):
                msk = _head_mask(hh)
                q_r, q_p = jnp.where(msk, qr, 0), jnp.where(msk, qp, 0)
                base = _attn_scores(j, ws, q_r, q_p, kw, kc, hh, bt_ref, s_ref)
                pr = jnp.exp(s_ref[...] - lse_ref[hh])
                delta = jnp.sum(jnp.where(msk, do_o, 0.0), axis=-1, keepdims=True)
                dob = jnp.where(msk, do, 0.0).astype(BF16)
                ds_lat = pr[:, :KEY_TILE] * (_dot_nt(dob, vw) - delta)
                ds_ctx = pr[:, KEY_TILE:] * (_dot_nt(dob, vc) - delta)
                for qi in range(Q_ROWS):
                    for m in range(KEY_ROWS // 2):
                        idx = base + 2 * m - qi
                        dbt_ref[hh, idx] = dbt_ref[hh, idx] + ds_lat[qi * GRID_W:(qi + 1) * GRID_W, 128 * m:128 * (m + 1)]
                dsb_lat = ds_lat.astype(BF16)
                dsb_ctx = ds_ctx.astype(BF16)
                prb = pr.astype(BF16)
                dq_r.append(jnp.dot(dsb_lat, kw, preferred_element_type=F32))
                dq_p.append(jnp.dot(dsb_ctx, kc, preferred_element_type=F32))
                dk_ref[win, :] = dk_ref[win, :] + _dot_tn(dsb_lat, q_r)
                dk_ref[:CTX_LEN, :] = dk_ref[:CTX_LEN, :] + _dot_tn(dsb_ctx, q_p)
                dv_ref[win, :] = dv_ref[win, :] + _dot_tn(prb[:, :KEY_TILE], dob)
                dv_ref[:CTX_LEN, :] = dv_ref[:CTX_LEN, :] + _dot_tn(prb[:, KEY_TILE:], dob)
            dqr_ref[...] = jnp.where(_head_mask(0), dq_r[0], dq_r[1])
            dqp_ref[...] = jnp.where(_head_mask(0), dq_p[0], dq_p[1])

    lat = lambda jj: jnp.maximum(jj - 1, 0)
    qspec = pl.BlockSpec((Q_TILE, 128), lambda hp, jj: (lat(jj) + 1, hp))
    kspec = pl.BlockSpec((ZLEN, 128), lambda hp, jj: (0, hp))
    btspec = pl.BlockSpec((2, BT_LEN, GRID_W, 128), lambda hp, jj: (hp, 0, 0, 0))
    ospec = pl.BlockSpec((Q_TILE, 128), lambda hp, jj: (lat(jj), hp))
    dqspec = pl.BlockSpec((Q_TILE, 128), lambda hp, jj: (jj, hp))
    zshape = jax.ShapeDtypeStruct((ZLEN, D_MODEL), F32)
    res, extra = _call(
        body, name="attn_bwd", grid=(NA_HEADS // 2, ZLEN // Q_TILE),
        in_specs=[qspec, qspec, kspec, kspec, btspec, ospec, ospec,
                  pl.BlockSpec((2, Q_TILE, 1), lambda hp, jj: (hp, lat(jj), 0))],
        out_specs=[dqspec, dqspec, kspec, kspec, btspec],
        out_shape=[zshape, zshape, zshape, zshape, jax.ShapeDtypeStruct((NA_HEADS, BT_LEN, GRID_W, 128), F32)],
        scratch_shapes=[pltpu.VMEM((Q_TILE, KEY_TILE + CTX_LEN), F32)], sem=("parallel", "arbitrary"),
        args=(q_rot, q_pl, kk, vv, bt, y_na, d_yna, lse), comm=comm)
    return (*res, extra)


def qkv_bwd(dq_rot, dq_pl, dk, dv, p, qg2, kg2, cos, sin, ones):
    scale = HEAD_DIM ** -0.5
    n_hp, n_i = NA_HEADS // 2, ZLEN // PREP_TILE

    def norm_rope_bwd(d_rot, d_extra, x, gain, cos_t, sin_t, ones_m, dx_ref, acc_ref):
        xh, rstd = _head_rms(x, ones_m, 1.0)
        dn = d_rot * cos_t + _rope_partner(d_rot * sin_t)
        if d_extra is not None:
            dn = (dn + d_extra) * scale
        acc_ref[...] = acc_ref[...] + jnp.sum(dn * xh, axis=0, keepdims=True)
        dxh = dn * gain
        seg = jnp.dot(dxh * xh, ones_m, preferred_element_type=F32, precision=lax.Precision.HIGHEST) * (1.0 / HEAD_DIM)
        dx_ref[...] = (rstd * (dxh - xh * seg)).astype(BF16)

    def body(dqr_ref, dqp_ref, dk_ref, dv_ref, xq_ref, xk_ref, qg_ref, kg_ref, cos_ref, sin_ref, ones_ref,
             dxq_ref, dxk_ref, dxv_ref, dgq_ref, dgk_ref, accq_ref, acck_ref):
        hp, i = pl.program_id(0), pl.program_id(1)

        @pl.when((hp == 0) & (i == 0))
        def _():
            accq_ref[...] = jnp.zeros_like(accq_ref)
            acck_ref[...] = jnp.zeros_like(acck_ref)

        ones_m = ones_ref[...]
        cos_t, sin_t = cos_ref[...], sin_ref[...]
        norm_rope_bwd(dqr_ref[...], dqp_ref[...], xq_ref[...], qg_ref[...], cos_t, sin_t, ones_m, dxq_ref, accq_ref)
        norm_rope_bwd(dk_ref[...], None, xk_ref[...], kg_ref[...], cos_t, sin_t, ones_m, dxk_ref, acck_ref)
        dxv_ref[...] = dv_ref[...].astype(BF16)

        @pl.when((hp == n_hp - 1) & (i == n_i - 1))
        def _():
            dgq_ref[...] = accq_ref[:, :HEAD_DIM] + accq_ref[:, HEAD_DIM:]
            dgk_ref[...] = acck_ref[:, :HEAD_DIM] + acck_ref[:, HEAD_DIM:]

    col = lambda base: pl.BlockSpec((PREP_TILE, 128), lambda hp, i: (i, base + hp))
    small = pl.BlockSpec((1, 128), lambda hp, i: (0, 0))
    tab = pl.BlockSpec((PREP_TILE, 128), lambda hp, i: (i, 0))
    zb = jax.ShapeDtypeStruct((ZLEN, D_MODEL), BF16)
    gshape = jax.ShapeDtypeStruct((1, HEAD_DIM), F32)
    return pl.pallas_call(
        body, name="qkv_bwd", grid=(n_hp, n_i),
        in_specs=[col(0)] * 4 + [col(32), col(8), small, small, tab, tab, _full((128, 128))],
        out_specs=[col(0)] * 3 + [_full((1, HEAD_DIM))] * 2,
        out_shape=[zb, zb, zb, gshape, gshape],
        scratch_shapes=[pltpu.VMEM((1, 128), F32)] * 2,
        compiler_params=_params("arbitrary", "arbitrary"),
    )(dq_rot, dq_pl, dk, dv, p, p, qg2, kg2, cos, sin, ones)


def rpb_grad(dbt):
    e, _ = _bias_expand()
    n_dr = 2 * NA_ROWS - 1
    ea = np.zeros((31, GRID_W, 128), np.float32)
    ea[:, :, :GRID_W] = e
    eb = np.zeros((31, GRID_W, 128), np.float32)
    eb[:, :, GRID_W:] = e
    eat = jnp.asarray(ea.reshape(31, GRID_W * 128).T.copy())
    ebt = jnp.asarray(eb.reshape(31, GRID_W * 128).T.copy())
    sel_at = np.zeros((n_dr, BT_LEN), np.float32)
    sel_bt = np.zeros((n_dr, BT_LEN), np.float32)
    for r in range(BT_LEN):
        dr = r - BT_PAD
        if 0 <= dr < n_dr:
            sel_at[dr, r] = 1.0
        if 0 <= dr + 1 < n_dr:
            sel_bt[dr + 1, r] = 1.0
    hi = lax.Precision.HIGHEST

    tk = 2048
    wide = GRID_W * 128
    rows = NA_HEADS * BT_LEN
    n_k = wide // tk

    def body(d_ref, sa_ref, sb_ref, ea_ref, eb_ref, o_ref, a_s, b_s):
        k = pl.program_id(0)
        dm = d_ref[...]
        a = jnp.dot(dm, ea_ref[...], preferred_element_type=F32, precision=hi)
        b = jnp.dot(dm, eb_ref[...], preferred_element_type=F32, precision=hi)

        @pl.when(k == 0)
        def _():
            a_s[...] = a
            b_s[...] = b

        @pl.when(k > 0)
        def _():
            a_s[...] = a_s[...] + a
            b_s[...] = b_s[...] + b

        @pl.when(k == n_k - 1)
        def _():
            for h in range(NA_HEADS):
                sl = slice(h * BT_LEN, (h + 1) * BT_LEN)
                o_ref[h] = (jnp.dot(sa_ref[...], a_s[sl, :], preferred_element_type=F32, precision=hi)
                            + jnp.dot(sb_ref[...], b_s[sl, :], preferred_element_type=F32, precision=hi))

    return pl.pallas_call(
        body, name="rpb_grad", grid=(n_k,),
        in_specs=[pl.BlockSpec((rows, tk), lambda k: (0, k)), _full((n_dr, BT_LEN)), _full((n_dr, BT_LEN)),
                  pl.BlockSpec((tk, 31), lambda k: (k, 0)), pl.BlockSpec((tk, 31), lambda k: (k, 0))],
        out_specs=_full((NA_HEADS, n_dr, 31)),
        out_shape=jax.ShapeDtypeStruct((NA_HEADS, n_dr, 31), F32),
        scratch_shapes=[pltpu.VMEM((rows, 31), F32), pltpu.VMEM((rows, 31), F32)],
        compiler_params=_params("arbitrary"),
    )(dbt.reshape(rows, wide), jnp.asarray(sel_at), jnp.asarray(sel_bt), eat, ebt)


def lru_bwd(p, d_yrnn, conv_w, conv_b, wa, ba, wx, bx, lam, comm=None):
    blk, wspec = _lru_in_specs()

    def body(xr_ref, gx_ref, dy_ref, cw_ref, cb_ref, wa_ref, ba_ref, wx_ref, bx_ref, lam_ref,
             dxr_ref, dgx_ref, dcw_ref, dcb_ref, dwa_ref, dba_ref, dwx_ref, dbx_ref, dlam_ref,
             a_s, b_s, h_s, l_s, hsum_s, dxc_s, dh_s):
        xr = xr_ref[...]
        cw = cw_ref[...]
        xc = _lru_conv(xr, cw, cb_ref[...])
        xcb = xc.astype(BF16)
        g, dg = _gelu_parts(gx_ref[CTX_LEN:, :])
        dy = dy_ref[...]
        dh_s[:CTX_LEN, :] = jnp.zeros((CTX_LEN, LRU_BLOCK_W), F32)
        dh_s[CTX_LEN:, :] = dy * g
        row = _row_ids(ZLEN, LRU_BLOCK_W)
        zero = jnp.zeros((1, LRU_BLOCK_W), F32)
        for d in (0, 1):
            wab = wa_ref[d, 0].astype(BF16)
            wxb = wx_ref[d, 0].astype(BF16)
            lam_d = lam_ref[d:d + 1, :]
            r, gi, sp, a, sq, b = _lru_gates(xc, xcb, wab, ba_ref[d:d + 1, :], wxb, bx_ref[d:d + 1, :], lam_d)
            a_s[...] = a
            b_s[...] = b
            _lru_scan_dir(d, a_s, b_s, h_s)
            h = h_s[...]
            if d == 0:
                hsum_s[...] = h
                h_prev = jnp.where(row >= 1, pltpu.roll(h, 1, 0), 0.0)
                a_s[...] = pltpu.roll(a, ZLEN - 1, 0)
                _scan_down(a_s, dh_s, l_s, 0, N_CHUNK, zero)
            else:
                hsum_s[...] = hsum_s[...] + h
                h_prev = jnp.where(row == CTX_LEN - 1, 0.0, pltpu.roll(h, ZLEN - 1, 0))
                a_s[...] = pltpu.roll(a, 1, 0)
                c = _scan_up(a_s, dh_s, l_s, CTX_CHUNKS, N_CHUNK, zero)
                _scan_up(a_s, dh_s, l_s, 0, CTX_CHUNKS, c)
            db = l_s[...]
            da = db * h_prev
            dsq = db * gi * xc
            dgi = db * sq * xc
            dxc_d = db * sq * gi
            dla = da * a - dsq * (a * a) / sq
            dr = dla * ((-LRU_C) * sp)
            dsp = jnp.sum(dla * ((-LRU_C) * r), axis=0, keepdims=True)
            dlam_ref[d:d + 1, :] = -dsp * _sigmoid(-lam_d)
            dzr = dr * r * (1.0 - r)
            dzi = dgi * gi * (1.0 - gi)
            dba_ref[d:d + 1, :] = jnp.sum(dzr, axis=0, keepdims=True)
            dbx_ref[d:d + 1, :] = jnp.sum(dzi, axis=0, keepdims=True)
            dzrb = dzr.astype(BF16)
            dzib = dzi.astype(BF16)
            dwa_ref[d, 0] = _dot_tn(xcb, dzrb)
            dwx_ref[d, 0] = _dot_tn(xcb, dzib)
            dxc_d = dxc_d + _dot_nt(dzrb, wab) + _dot_nt(dzib, wxb)
            if d == 0:
                dxc_s[...] = dxc_d
            else:
                dxc_s[...] = dxc_s[...] + dxc_d
        dxc = dxc_s[...]
        dxr_ref[...] = _lru_conv_t(dxc, cw).astype(BF16)
        dcb_ref[...] = jnp.sum(dxc, axis=0, keepdims=True)
        segpos = jnp.where(row < CTX_LEN, row, row - CTX_LEN)
        seglen = jnp.where(row < CTX_LEN, CTX_LEN, SEQ)
        for k in range(4):
            off = k - 2
            if off == 0:
                sh = xr
            else:
                ok = (segpos + off >= 0) & (segpos + off < seglen)
                sh = jnp.where(ok, pltpu.roll(xr, (-off) % ZLEN, 0), 0.0)
            dcw_ref[k:k + 1, :] = jnp.sum(dxc * sh, axis=0, keepdims=True)
        dgx_ref[:CTX_LEN, :] = jnp.zeros((CTX_LEN, LRU_BLOCK_W), BF16)
        dgx_ref[CTX_LEN:, :] = (dy * hsum_s[CTX_LEN:, :] * dg).astype(BF16)

    zs = pltpu.VMEM((ZLEN, LRU_BLOCK_W), F32)
    zb = jax.ShapeDtypeStruct((ZLEN, D_MODEL), BF16)
    v2 = jax.ShapeDtypeStruct((2, D_MODEL), F32)
    w4 = jax.ShapeDtypeStruct((2, LRU_BLOCKS, LRU_BLOCK_W, LRU_BLOCK_W), F32)
    res, extra = _call(
        body, name="lru_bwd", grid=(LRU_BLOCKS,),
        in_specs=[blk(ZLEN), pl.BlockSpec((ZLEN, LRU_BLOCK_W), lambda b: (0, 24 + b)), blk(SEQ), blk(4), blk(1),
                  wspec, blk(2), wspec, blk(2), blk(2)],
        out_specs=[blk(ZLEN), blk(ZLEN), blk(4), blk(1), wspec, blk(2), wspec, blk(2), blk(2)],
        out_shape=[zb, zb, jax.ShapeDtypeStruct((4, D_MODEL), F32), jax.ShapeDtypeStruct((1, D_MODEL), F32),
                   w4, v2, w4, v2, v2],
        scratch_shapes=[zs] * 7, sem=("arbitrary",),
        args=(p, p, d_yrnn, conv_w, conv_b, wa, ba, wx, bx, lam), comm=comm)
    return (*res, extra)


def in_proj_bwd(dgs, w_in, z, dx1, gain, scale, comm=None):
    def body(*refs):
        dg_refs = refs[:7]
        w_ref, z_ref, dx1_ref, g_ref, sc_ref, gx_ref, dsh_ref, dsc_ref, dgn_ref = refs[7:]
        i = pl.program_id(0)
        dxn = _dot_nt(dg_refs[0][...], w_ref[:, 0:D_MODEL])
        for g in range(1, 7):
            dxn = dxn + _dot_nt(dg_refs[g][...], w_ref[:, g * D_MODEL:(g + 1) * D_MODEL])
        dx, dsh, dsc, dgn = _norm_mod_bwd(z_ref[...], dxn, g_ref[...], sc_ref[0])

        @pl.when(i <= 1)
        def _():
            dsh_ref[0] = dsh
            dsc_ref[0] = dsc

        @pl.when(i > 1)
        def _():
            dsh_ref[0] = dsh_ref[0] + dsh
            dsc_ref[0] = dsc_ref[0] + dsc

        @pl.when(i == 0)
        def _():
            dgn_ref[...] = dgn

        @pl.when(i > 0)
        def _():
            dgn_ref[...] = dgn_ref[...] + dgn
            gx_ref[...] = dx1_ref[...] + dx

    zrow = pl.BlockSpec((ROW_TILE, D_MODEL), lambda i: (i, 0))
    lat = pl.BlockSpec((ROW_TILE, D_MODEL), lambda i: (jnp.maximum(i - 1, 0), 0))
    mod = pl.BlockSpec((1, 1, D_MODEL), lambda i: (jnp.minimum(i, 1), 0, 0))
    mshape = jax.ShapeDtypeStruct((2, 1, D_MODEL), F32)
    res, extra = _call(
        body, name="in_proj_bwd", grid=(ZLEN // ROW_TILE,),
        in_specs=[zrow] * 7 + [_full((D_MODEL, IN_COLS)), zrow, lat, _full((1, D_MODEL)), mod],
        out_specs=[lat, mod, mod, _full((1, D_MODEL))],
        out_shape=[jax.ShapeDtypeStruct((SEQ, D_MODEL), F32), mshape, mshape, jax.ShapeDtypeStruct((1, D_MODEL), F32)],
        sem=("arbitrary",), args=(*dgs, w_in, z, dx1, gain, scale), comm=comm)
    return (*res, extra)


def matmul_tn(a, b, name, tm, tn, prev=None, col_block=0, total_cols=None):
    k, m = a.shape
    n = b.shape[1]
    total_cols = n if total_cols is None else total_cols
    assert m % tm == 0 and n % tn == 0
    off = col_block * (n // tn)

    def body(a_ref, b_ref, *rest):
        rest[-1][...] = _dot_tn(a_ref[...].astype(BF16), b_ref[...]).astype(BF16)

    in_specs = [pl.BlockSpec((k, tm), lambda i, j: (0, i)), pl.BlockSpec((k, tn), lambda i, j: (0, j))]
    args = [a, b]
    aliases = {}
    if prev is not None:
        in_specs.append(pl.BlockSpec(memory_space=pl.ANY))
        args.append(prev)
        aliases = {2: 0}
    return pl.pallas_call(
        body, name=name, grid=(m // tm, n // tn), in_specs=in_specs,
        out_specs=pl.BlockSpec((tm, tn), lambda i, j: (i, j + off)),
        out_shape=jax.ShapeDtypeStruct((m, total_cols), BF16),
        input_output_aliases=aliases,
        compiler_params=_params("parallel", "parallel"),
    )(*args)


def local_step(z, target, modx, modc, norm_mix_g, norm_ffn_g, w_in, conv_w, conv_b, wa, ba, wx, bx, lam, qg, kg, rpb,
               w_rnn, w_na, w_out, w_up, fconv_w, fconv_b, w_down, c_idx=None):
    dist = c_idx is not None
    d = D_MODEL
    mx = [modx[:, k * d:(k + 1) * d] for k in range(N_MOD)]
    shift = jnp.stack([modc[:, 0:d], mx[0]])
    scale = jnp.stack([modc[:, d:2 * d], mx[1]])
    cos, sin = _rope_tables()
    ones = _head_ones()
    qg2 = jnp.tile(qg, (1, 2))
    kg2 = jnp.tile(kg, (1, 2))

    xn = norm_mod(z, norm_mix_g, shift, scale, "norm_mix")
    p, got = matmul_wide(xn, w_in, "in_proj", ROW_TILE, 1792,
                         comm=gather_weights_comm([w_rnn, w_na, w_out], [1, 2, 3]) if dist else None)
    if dist:
        w_rnn, w_na, w_out = got
    y_rnn, got = lru_fwd(p, conv_w, conv_b, wa, ba, wx, bx, lam,
                         comm=gather_weights_comm([w_down], [5]) if dist else None)
    if dist:
        w_down = got[0]
    q_rot, q_pl, kk, vv = qkv_prep(p, qg2, kg2, cos, sin, ones)
    bt = bias_table(rpb)
    y_na, lse, got = attn_fwd(q_rot, q_pl, kk, vv, bt, comm=gather_weights_comm([w_up], [4]) if dist else None)
    if dist:
        w_up = got[0]
    u, v, merged, out, x1 = merge_fwd(y_rnn, y_na, p, z, mx[2], w_rnn, w_na, w_out)
    xn2 = norm_mod(x1, norm_ffn_g, mx[3][None], mx[4][None], "norm_ffn")
    hpre, _ = matmul_wide(xn2, w_up, "ffn_up", ROW_TILE, 1408)
    act = ffn_act(hpre, fconv_w, fconv_b)
    f, dy, df, loss_sq, dg5 = ffn_down_loss(act, w_down, x1, mx[5], target)

    partials, pieces = {}, {}

    def chip_partials(which, grads, tag):
        views = [_grad_view(g, BIG[w][1], BIG[w][2]) for w, g in zip(which, grads)]
        recv = exchange_halves(views, "grad_exchange_" + tag)
        for w, gv, r in zip(which, views, recv):
            partials[w] = add_halves(gv, r, c_idx, "add_halves_" + BIG[w][0])
        return scatter_pieces_comm([partials[w] for w in which], which)

    d_act = ffn_down_bwd(df, w_down)
    dha, dhg, d_fcw_a, d_fcw_g, d_fcb_a, d_fcb_g = ffn_act_bwd(hpre, d_act, fconv_w, fconv_b)
    d_fcw = jnp.concatenate([d_fcw_a, d_fcw_g], axis=1)
    d_fcb = jnp.concatenate([d_fcb_a, d_fcb_g], axis=1)
    dx1, d_s3, d_s4, d_gffn = ffn_up_bwd(dha, dhg, w_up, x1, dy, norm_ffn_g, mx[4])
    g_w_down = matmul_tn(act, df, "gw_down", 256, D_MODEL)
    g_w_up = matmul_tn(xn2, dha, "gw_up_a", 512, 1408, total_cols=2 * D_FF)
    g_w_up = matmul_tn(xn2, dhg, "gw_up_g", 512, 1408, prev=g_w_up, col_block=1, total_cols=2 * D_FF)
    dout, du, dv, dmr, dmn, dyr, dyn, dg2 = merge_bwd(dx1, out, mx[2], p, u, v, w_rnn, w_na, w_out)
    g_w_out = matmul_tn(merged, dout, "gw_out", 512, 512)
    g_w_rnn = matmul_tn(y_rnn, du, "gw_rnn", 512, 512)
    g_w_na = matmul_tn(y_na, dv, "gw_na", 512, 512)
    *lru_grads, got = lru_bwd(p, dyr, conv_w, conv_b, wa, ba, wx, bx, lam,
                              comm=chip_partials([1, 2, 3], [g_w_rnn, g_w_na, g_w_out], "mix") if dist else None)
    dxr, dgx, d_cw, d_cb, d_wa, d_ba, d_wx, d_bx, d_lam = lru_grads
    if dist:
        pieces[1], pieces[2], pieces[3] = got
    lru_w_all = {}
    dqr, dqp, dk, dvh, dbt, got = attn_bwd(
        q_rot, q_pl, kk, vv, bt, y_na, dyn, lse,
        comm=join_comms(chip_partials([4, 5], [g_w_up, g_w_down], "ffn"),
                        all_gather_comm(d_wa.reshape(-1, LRU_BLOCK_W))) if dist else None)
    if dist:
        pieces[4], pieces[5], lru_w_all["lru_wa"] = got
    dq_cols, dk_cols, dv_cols, d_qg, d_kg = qkv_bwd(dqr, dqp, dk, dvh, p, qg2, kg2, cos, sin, ones)
    d_rpb = rpb_grad(dbt)
    dgs = [dxr, dk_cols, dv_cols, dgx, dq_cols, dmr, dmn]
    grad_x, dsh, dsc, d_gmix, got = in_proj_bwd(dgs, w_in, z, dx1, norm_mix_g, scale,
                                                comm=all_gather_comm(d_wx.reshape(-1, LRU_BLOCK_W)) if dist else None)
    if dist:
        lru_w_all["lru_wx"] = got[0]
    g_w_in = None
    for g in range(7):
        g_w_in = matmul_tn(xn, dgs[g], "gw_in_%d" % g, 512, 512, prev=g_w_in, col_block=g, total_cols=IN_COLS)
    if dist:
        pieces[0] = run_comm(chip_partials([0], [g_w_in], "w_in"), "grad_scatter_w_in")[0]

    d_modx = jnp.concatenate([dsh[1], dsc[1], dg2, d_s3, d_s4, dg5], axis=1)
    d_modc = jnp.concatenate([dsh[0], dsc[0]], axis=1)
    return dict(loss_sq=loss_sq, grad_x=grad_x, d_modx=d_modx, d_modc=d_modc, norm_mix_g=d_gmix, norm_ffn_g=d_gffn,
                w_in=g_w_in, lru_conv_w=d_cw, lru_conv_b=d_cb, lru_wa=d_wa, lru_ba=d_ba, lru_wx=d_wx, lru_bx=d_bx,
                lru_lambda=d_lam, q_norm_g=d_qg, k_norm_g=d_kg, na_rpb=d_rpb, w_rnn_out=g_w_rnn, w_na_out=g_w_na,
                w_out=g_w_out, w_up=g_w_up, ffn_conv_w=d_fcw, ffn_conv_b=d_fcb, w_down=g_w_down,
                partials=partials, pieces=pieces, lru_w_all=lru_w_all)


def _mesh_pos():
    return lax.axis_index("x"), lax.axis_index("y"), lax.axis_index("c")


def _other_chips(x, y):
    return [(1 - x, y), (x, 1 - y), (1 - x, 1 - y)]


def all_gather8(xs, name, with_sum=False):
    m, n = xs.shape
    assert m % 8 == 0

    def body(x_ref, out_ref, *rest):
        if with_sum:
            sum_ref, send_sems, recv_sems, local_sem = rest
        else:
            send_sems, recv_sems, local_sem = rest
        x, y, c = _mesh_pos()
        me, sibling = (x, y, c), (x, y, 1 - c)
        chips = _other_chips(x, y)

        def rows(px, py, pc):
            return out_ref.at[pl.ds((4 * px + 2 * py + pc) * m, m), :]

        def copy(k, block, to, src=None):
            return pltpu.make_async_remote_copy(
                src_ref=rows(*block) if src is None else src, dst_ref=rows(*block),
                send_sem=send_sems.at[k], recv_sem=recv_sems.at[k], device_id=to, device_id_type=MESH_T)

        mine = pltpu.make_async_copy(x_ref, rows(*me), local_sem)
        mine.start()
        first = [copy(0, me, sibling, src=x_ref)]
        first += [copy(1 + j, me, (*chip, c), src=x_ref) for j, chip in enumerate(chips)]
        for cp in first:
            cp.start()
        passed = [copy(4 + j, (*chip, c), sibling) for j, chip in enumerate(chips)]
        for j, chip in enumerate(chips):
            copy(1 + j, (*chip, c), me).wait_recv()
            passed[j].start()
        copy(0, sibling, me).wait_recv()
        for j, chip in enumerate(chips):
            copy(4 + j, (*chip, 1 - c), me).wait_recv()
        for cp in first + passed:
            cp.wait_send()
        mine.wait()
        if with_sum:
            acc = out_ref[0:m, :]
            for k in range(1, N_DEV):
                acc = acc + out_ref[k * m:(k + 1) * m, :]
            sum_ref[...] = acc

    vm = pl.BlockSpec(memory_space=pltpu.VMEM)
    out_shape = [jax.ShapeDtypeStruct((N_DEV * m, n), F32)]
    if with_sum:
        out_shape.append(jax.ShapeDtypeStruct((m, n), F32))
    res = pl.pallas_call(
        body, name=name, in_specs=[vm], out_specs=[vm] * len(out_shape), out_shape=out_shape,
        scratch_shapes=[pltpu.SemaphoreType.DMA((7,)), pltpu.SemaphoreType.DMA((7,)), pltpu.SemaphoreType.DMA],
        compiler_params=pltpu.CompilerParams(vmem_limit_bytes=VMEM_LIMIT_V7X),
    )(xs)
    return res if with_sum else res[0]


BIG = (("w_in", (D_MODEL, IN_COLS), 1), ("w_rnn_out", (D_MODEL, D_MODEL), 0), ("w_na_out", (D_MODEL, D_MODEL), 0),
       ("w_out", (D_MODEL, D_MODEL), 0), ("w_up", (D_MODEL, 2 * D_FF), 1), ("w_down", (D_FF, D_MODEL), 0))


def _shard_shape(full, axis):
    r, c = full
    return (r // N_SHARD, c) if axis == 0 else (r, c // N_SHARD)


def _slot(ref, full, axis, s, h):
    r, c = full
    if axis == 0:
        rs = r // N_SHARD
        return ref.at[pl.ds(s * rs + h * (rs // 2), rs // 2), :]
    cs = c // N_SHARD
    return ref.at[pl.ds(h * (r // 2), r // 2), pl.ds(s * cs, cs)]


def cast_into_full(x, full, axis, idx, name):
    r, c = x.shape
    tr = next(t for t in (512, 352, 256, 128) if r % t == 0)
    nb = r // tr

    def body(idx_ref, x_ref, o_ref):
        o_ref[...] = x_ref[...].astype(BF16)

    if axis == 0:
        out_spec = pl.BlockSpec((tr, c), lambda i, idx_ref: (idx_ref[0] * nb + i, 0))
    else:
        out_spec = pl.BlockSpec((tr, c), lambda i, idx_ref: (i, idx_ref[0]))
    return pl.pallas_call(
        body, name=name,
        grid_spec=pltpu.PrefetchScalarGridSpec(
            num_scalar_prefetch=1, grid=(nb,), in_specs=[pl.BlockSpec((tr, c), lambda i, idx_ref: (i, 0))],
            out_specs=out_spec),
        out_shape=jax.ShapeDtypeStruct(full, BF16),
        compiler_params=_params("parallel"),
    )(idx, x)


def run_comm(comm, name):
    k_in, k_out = len(comm.inputs), len(comm.out_shapes)

    def body(*refs):
        start, mid, end = comm.emit(refs[:k_in], refs[k_in:k_in + k_out], refs[k_in + k_out:])
        start()
        mid()
        end()

    hbm = pl.BlockSpec(memory_space=pl.ANY)
    return pl.pallas_call(
        body, name=name, in_specs=[hbm] * k_in, out_specs=[hbm] * k_out, out_shape=list(comm.out_shapes),
        input_output_aliases=dict(comm.aliases), scratch_shapes=list(comm.scratch),
        compiler_params=pltpu.CompilerParams(vmem_limit_bytes=VMEM_LIMIT_V7X),
    )(*comm.inputs)


def gather_weights_comm(fulls, which):
    nw = len(which)
    specs = [BIG[w] for w in which]

    def emit(_, outs, sems):
        send1, recv1, send2, recv2 = sems
        x, y, c = _mesh_pos()
        sibling = (x, y, 1 - c)
        chips = _other_chips(x, y)
        s_me = 2 * x + y
        shards = [2 * chip[0] + chip[1] for chip in chips]

        def ici(w, j, shard):
            _, full, axis = specs[w]
            dst = _slot(outs[w], full, axis, shard, c)
            return pltpu.make_async_remote_copy(
                src_ref=dst, dst_ref=dst, send_sem=send1.at[3 * w + j],
                recv_sem=recv1.at[3 * w + j], device_id=(*chips[j], c), device_id_type=MESH_T)

        def d2d(w, j, shard, half):
            _, full, axis = specs[w]
            dst = _slot(outs[w], full, axis, shard, half)
            return pltpu.make_async_remote_copy(
                src_ref=dst, dst_ref=dst, send_sem=send2.at[3 * w + j], recv_sem=recv2.at[3 * w + j],
                device_id=sibling, device_id_type=MESH_T)

        pairs = [(w, j) for w in range(nw) for j in range(3)]

        def start():
            for w, j in pairs:
                ici(w, j, s_me).start()

        def mid():
            for w, j in pairs:
                ici(w, j, shards[j]).wait_recv()
                d2d(w, j, shards[j], c).start()

        def end():
            for w, j in pairs:
                d2d(w, j, shards[j], 1 - c).wait_recv()
            for w, j in pairs:
                ici(w, j, s_me).wait_send()
                d2d(w, j, shards[j], c).wait_send()

        return start, mid, end

    return Comm(list(fulls), [jax.ShapeDtypeStruct(full, BF16) for _, full, _ in specs], {i: i for i in range(nw)},
                [pltpu.SemaphoreType.DMA((3 * nw,))] * 4, emit)


def join_comms(a, b):
    ai, ao, asc = len(a.inputs), len(a.out_shapes), len(a.scratch)

    def emit(ins, outs, sems):
        fa = a.emit(ins[:ai], outs[:ao], sems[:asc])
        fb = b.emit(ins[ai:], outs[ao:], sems[asc:])

        def both(k):
            def run():
                fa[k]()
                fb[k]()
            return run

        return both(0), both(1), both(2)

    aliases = dict(a.aliases)
    aliases.update({ai + i: ao + o for i, o in b.aliases.items()})
    return Comm(a.inputs + b.inputs, a.out_shapes + b.out_shapes, aliases, a.scratch + b.scratch, emit)


def all_gather_comm(x):
    def emit(srcs, outs, sems):
        send_sems, recv_sems, local_sem = sems
        x_ref, out_ref = srcs[0], outs[0]
        x, y, c = _mesh_pos()
        me, sibling = (x, y, c), (x, y, 1 - c)
        chips = _other_chips(x, y)

        def blk(px, py, pc):
            return out_ref.at[4 * px + 2 * py + pc]

        def copy(k, block, to, src=None):
            return pltpu.make_async_remote_copy(
                src_ref=blk(*block) if src is None else src, dst_ref=blk(*block),
                send_sem=send_sems.at[k], recv_sem=recv_sems.at[k], device_id=to, device_id_type=MESH_T)

        def mine():
            return pltpu.make_async_copy(x_ref, blk(*me), local_sem)

        def start():
            mine().start()
            copy(0, me, sibling, src=x_ref).start()
            for j, chip in enumerate(chips):
                copy(1 + j, me, (*chip, c), src=x_ref).start()

        def mid():
            for j, chip in enumerate(chips):
                copy(1 + j, (*chip, c), me).wait_recv()
                copy(4 + j, (*chip, c), sibling).start()

        def end():
            copy(0, sibling, me).wait_recv()
            for j, chip in enumerate(chips):
                copy(4 + j, (*chip, 1 - c), me).wait_recv()
            copy(0, me, sibling, src=x_ref).wait_send()
            for j, chip in enumerate(chips):
                copy(1 + j, me, (*chip, c), src=x_ref).wait_send()
                copy(4 + j, (*chip, c), sibling).wait_send()
            mine().wait()

        return start, mid, end

    return Comm([x], [jax.ShapeDtypeStruct((N_DEV,) + x.shape, F32)], {},
                [pltpu.SemaphoreType.DMA((7,)), pltpu.SemaphoreType.DMA((7,)), pltpu.SemaphoreType.DMA], emit)


def sum_blocks(g, name):
    _, r, c = g.shape
    tr = 256

    def body(g_ref, o_ref):
        acc = g_ref[0]
        for k in range(1, N_DEV):
            acc = acc + g_ref[k]
        o_ref[...] = acc

    return pl.pallas_call(
        body, name=name, grid=(r // tr,),
        in_specs=[pl.BlockSpec((N_DEV, tr, c), lambda i: (0, i, 0))],
        out_specs=pl.BlockSpec((tr, c), lambda i: (i, 0)),
        out_shape=jax.ShapeDtypeStruct((r, c), F32),
        compiler_params=_params("parallel"),
    )(g)


def _grad_view(g, full, axis):
    r, c = full
    if axis == 0:
        return g.reshape(N_SHARD, 2, r // N_SHARD // 2, c)
    return g.reshape(1, 2, r // 2, c)


def exchange_halves(gviews, name):
    nw = len(gviews)

    def body(*refs):
        srcs, outs = refs[:nw], refs[nw:2 * nw]
        send_sems, recv_sems = refs[2 * nw:]
        x, y, c = _mesh_pos()
        cps = []
        for w in range(nw):
            cp = pltpu.make_async_remote_copy(
                src_ref=srcs[w].at[:, pl.ds(1 - c, 1)], dst_ref=outs[w], send_sem=send_sems.at[w],
                recv_sem=recv_sems.at[w], device_id=(x, y, 1 - c), device_id_type=MESH_T)
            cp.start()
            cps.append(cp)
        for cp in cps:
            cp.wait()

    hbm = pl.BlockSpec(memory_space=pl.ANY)
    return pl.pallas_call(
        body, name=name, in_specs=[hbm] * nw, out_specs=[hbm] * nw,
        out_shape=[jax.ShapeDtypeStruct((g.shape[0], 1) + g.shape[2:], BF16) for g in gviews],
        scratch_shapes=[pltpu.SemaphoreType.DMA((nw,)), pltpu.SemaphoreType.DMA((nw,))],
        compiler_params=pltpu.CompilerParams(vmem_limit_bytes=VMEM_LIMIT_V7X),
    )(*gviews)


def _row_tile(rh):
    return 128 if rh % 128 == 0 else rh


def add_halves(gview, recv, c_idx, name):
    a, _, rh, cc = gview.shape
    tr = _row_tile(rh)

    def body(c_ref, g_ref, r_ref, o_ref):
        o_ref[0] = (g_ref[0, 0].astype(F32) + r_ref[0, 0].astype(F32)).astype(BF16)

    return pl.pallas_call(
        body, name=name,
        grid_spec=pltpu.PrefetchScalarGridSpec(
            num_scalar_prefetch=1, grid=(a, rh // tr),
            in_specs=[pl.BlockSpec((1, 1, tr, cc), lambda s, i, c_ref: (s, c_ref[0], i, 0)),
                      pl.BlockSpec((1, 1, tr, cc), lambda s, i, c_ref: (s, 0, i, 0))],
            out_specs=pl.BlockSpec((1, tr, cc), lambda s, i, c_ref: (s, i, 0))),
        out_shape=jax.ShapeDtypeStruct((a, rh, cc), BF16),
        compiler_params=_params("parallel", "parallel"),
    )(c_idx, gview, recv)


def _piece_shape(full, axis):
    rs, cs = _shard_shape(full, axis)
    return (rs // 2, cs)


def scatter_pieces_comm(partials, which):
    nw = len(which)
    specs = [BIG[w] for w in which]

    def emit(srcs, outs, sems):
        send_sems, recv_sems = sems
        x, y, c = _mesh_pos()
        chips = _other_chips(x, y)

        def copies():
            cps = []
            for w, (_, full, axis) in enumerate(specs):
                cs = full[1] // N_SHARD
                for j, chip in enumerate(chips):
                    s_j = 2 * chip[0] + chip[1]
                    src = srcs[w].at[s_j] if axis == 0 else srcs[w].at[0, :, pl.ds(s_j * cs, cs)]
                    cps.append(pltpu.make_async_remote_copy(
                        src_ref=src, dst_ref=outs[w].at[j], send_sem=send_sems.at[3 * w + j],
                        recv_sem=recv_sems.at[3 * w + j], device_id=(*chip, c), device_id_type=MESH_T))
            return cps

        def start():
            for cp in copies():
                cp.start()

        def mid():
            pass

        def end():
            for cp in copies():
                cp.wait()

        return start, mid, end

    return Comm(list(partials), [jax.ShapeDtypeStruct((3,) + _piece_shape(full, axis), BF16) for _, full, axis in specs],
                {}, [pltpu.SemaphoreType.DMA((3 * nw,)), pltpu.SemaphoreType.DMA((3 * nw,))], emit)


def add_pieces(partial, recv, idx, axis, name):
    _, rh, cs = recv.shape
    tr = _row_tile(rh)

    def body(idx_ref, p_ref, r_ref, o_ref):
        o_ref[0] = ((p_ref[0].astype(F32) + r_ref[0].astype(F32)) + r_ref[1].astype(F32)) + r_ref[2].astype(F32)

    if axis == 0:
        pspec = pl.BlockSpec((1, tr, cs), lambda i, idx_ref: (idx_ref[0], i, 0))
    else:
        pspec = pl.BlockSpec((1, tr, cs), lambda i, idx_ref: (0, i, idx_ref[0]))
    return pl.pallas_call(
        body, name=name,
        grid_spec=pltpu.PrefetchScalarGridSpec(
            num_scalar_prefetch=1, grid=(rh // tr,),
            in_specs=[pspec, pl.BlockSpec((3, tr, cs), lambda i, idx_ref: (0, i, 0))],
            out_specs=pl.BlockSpec((1, tr, cs), lambda i, idx_ref: (idx_ref[1], i, 0))),
        out_shape=jax.ShapeDtypeStruct((2, rh, cs), F32),
        compiler_params=_params("parallel"),
    )(idx, partial, recv)


def join_halves(halves):
    nw = len(BIG)

    def body(*refs):
        outs = refs[nw:2 * nw]
        send_sems, recv_sems = refs[2 * nw:]
        x, y, c = _mesh_pos()
        cps = []
        for w in range(nw):
            cp = pltpu.make_async_remote_copy(
                src_ref=outs[w].at[c], dst_ref=outs[w].at[c], send_sem=send_sems.at[w], recv_sem=recv_sems.at[w],
                device_id=(x, y, 1 - c), device_id_type=MESH_T)
            cp.start()
            cps.append(cp)
        for w in range(nw):
            cps[w].wait_send()
            pltpu.make_async_remote_copy(
                src_ref=outs[w].at[1 - c], dst_ref=outs[w].at[1 - c], send_sem=send_sems.at[w],
                recv_sem=recv_sems.at[w], device_id=(x, y, 1 - c), device_id_type=MESH_T).wait_recv()

    hbm = pl.BlockSpec(memory_space=pl.ANY)
    return pl.pallas_call(
        body, name="grad_join_halves", in_specs=[hbm] * nw, out_specs=[hbm] * nw,
        out_shape=[jax.ShapeDtypeStruct(h.shape, F32) for h in halves],
        input_output_aliases={i: i for i in range(nw)},
        scratch_shapes=[pltpu.SemaphoreType.DMA((nw,))] * 2,
        compiler_params=pltpu.CompilerParams(vmem_limit_bytes=VMEM_LIMIT_V7X),
    )(*halves)


MOD_COLS = N_MOD * D_MODEL // N_SHARD
MOD_TILE = 512


def mod_fwd(c16, w_mod):
    def body(c_ref, w_ref, s_ref, o_ref):
        cv = c_ref[...]
        s = cv * _sigmoid(cv)
        s_ref[...] = s
        o_ref[...] = jnp.dot(s.astype(BF16), w_ref[...].astype(BF16), preferred_element_type=F32)

    return pl.pallas_call(
        body, name="mod_fwd", grid=(MOD_COLS // MOD_TILE,),
        in_specs=[_full((16, D_MODEL)), pl.BlockSpec((D_MODEL, MOD_TILE), lambda j: (0, j))],
        out_specs=[_full((16, D_MODEL)), pl.BlockSpec((16, MOD_TILE), lambda j: (0, j))],
        out_shape=[jax.ShapeDtypeStruct((16, D_MODEL), F32), jax.ShapeDtypeStruct((16, MOD_COLS), F32)],
        compiler_params=_params("arbitrary"),
    )(c16, w_mod)


def mod_bwd(s16, dm16, w_mod):
    hi = lax.Precision.HIGHEST

    def body(s_ref, d_ref, w_ref, gw_ref, ds_ref):
        j = pl.program_id(0)
        dm = d_ref[...]
        gw_ref[...] = lax.dot_general(s_ref[...], dm, (((0,), (0,)), ((), ())), preferred_element_type=F32, precision=hi)
        part = lax.dot_general(dm, w_ref[...], (((1,), (1,)), ((), ())), preferred_element_type=F32, precision=hi)

        @pl.when(j == 0)
        def _():
            ds_ref[...] = part

        @pl.when(j > 0)
        def _():
            ds_ref[...] = ds_ref[...] + part

    return pl.pallas_call(
        body, name="mod_bwd", grid=(MOD_COLS // MOD_TILE,),
        in_specs=[_full((16, D_MODEL)), pl.BlockSpec((16, MOD_TILE), lambda j: (0, j)),
                  pl.BlockSpec((D_MODEL, MOD_TILE), lambda j: (0, j))],
        out_specs=[pl.BlockSpec((D_MODEL, MOD_TILE), lambda j: (0, j)), _full((16, D_MODEL))],
        out_shape=[jax.ShapeDtypeStruct((D_MODEL, MOD_COLS), F32), jax.ShapeDtypeStruct((16, D_MODEL), F32)],
        compiler_params=_params("arbitrary"),
    )(s16, dm16, w_mod)


def cctx_grad(parts, c_ctx):
    def body(p_ref, c_ref, o_ref):
        ds = p_ref[0:1, :]
        for s in range(1, N_SHARD):
            ds = ds + p_ref[16 * s:16 * s + 1, :]
        cv = c_ref[...]
        sg = _sigmoid(cv)
        o_ref[...] = ds * (sg * (1.0 + cv * (1.0 - sg)))

    return pl.pallas_call(
        body, name="cctx_grad", in_specs=[_full((N_DEV * 8, D_MODEL)), _full((1, D_MODEL))],
        out_specs=_full((1, D_MODEL)), out_shape=jax.ShapeDtypeStruct((1, D_MODEL), F32),
    )(parts, c_ctx)


def add_rows(a, b, name):
    def body(a_ref, b_ref, o_ref):
        o_ref[...] = a_ref[...] + b_ref[...]

    return pl.pallas_call(body, name=name, in_specs=[_full(a.shape), _full(b.shape)], out_specs=_full(a.shape),
                          out_shape=jax.ShapeDtypeStruct(a.shape, F32))(a, b)


def _adamw_update(w_ref, g_ref, m_ref, v_ref, d_ref, nm_ref, nv_ref):
    g_ = g_ref[...]
    m_ = ADAM_B1 * m_ref[...] + (1.0 - ADAM_B1) * g_
    v_ = ADAM_B2 * v_ref[...] + (1.0 - ADAM_B2) * (g_ * g_)
    m_hat = m_ / (1.0 - ADAM_B1 ** ADAM_STEP)
    v_hat = v_ / (1.0 - ADAM_B2 ** ADAM_STEP)
    d_ref[...] = -ADAM_LR * (m_hat / (jnp.sqrt(v_hat) + ADAM_EPS) + ADAM_WD * w_ref[...])
    nm_ref[...] = m_
    nv_ref[...] = v_


def adamw_many(ws, gs, ms, vs):
    n = len(ws)

    def body(*refs):
        for i in range(n):
            _adamw_update(*[refs[k * n + i] for k in range(7)])

    shapes = [jax.ShapeDtypeStruct(w.shape, F32) for w in ws]
    return pl.pallas_call(body, name="adamw_small", out_shape=shapes * 3,
                          compiler_params=pltpu.CompilerParams(vmem_limit_bytes=VMEM_LIMIT_V7X))(*ws, *gs, *ms, *vs)


def adamw(w, g, m, v, name):
    r, c = w.shape
    tr = 128 if (r % 128 == 0 and r > 128) else r

    def body(w_ref, g_ref, m_ref, v_ref, d_ref, nm_ref, nv_ref):
        _adamw_update(w_ref, g_ref, m_ref, v_ref, d_ref, nm_ref, nv_ref)

    spec = pl.BlockSpec((tr, c), lambda i: (i, 0))
    shp = jax.ShapeDtypeStruct((r, c), F32)
    return pl.pallas_call(
        body, name=name, grid=(r // tr,), in_specs=[spec] * 4, out_specs=[spec] * 3, out_shape=[shp] * 3,
        compiler_params=_params("parallel"),
    )(w, g, m, v)


LANES = 1024


def _pack(arrs):
    rows, spans, at = [], [], 0
    for a in arrs:
        n = int(np.prod(a.shape))
        nr = 8 * -(-n // (8 * LANES))
        flat = a.reshape(-1)
        if nr * LANES != n:
            flat = jnp.concatenate([flat, jnp.zeros((nr * LANES - n,), F32)])
        rows.append(flat.reshape(nr, LANES))
        spans.append((at, nr, n, a.shape))
        at += nr
    return jnp.concatenate(rows, axis=0), spans


def _unpack(buf, spans):
    out = []
    for at, nr, n, shape in spans:
        out.append(buf[at:at + nr].reshape(-1)[:n].reshape(shape))
    return out


SMALL_SHARD = ("lru_conv_w", "lru_ba", "lru_bx", "lru_lambda", "ffn_conv_w")


def kernel(x, c, ctx, c_ctx, w_mod, b_mod, norm_mix_g, norm_ffn_g, w_in, lru_conv_w, lru_conv_b, lru_wa, lru_ba, lru_wx, lru_bx, lru_lambda, q_norm_g, k_norm_g, na_rpb, w_rnn_out, w_na_out, w_out, w_up, ffn_conv_w, ffn_conv_b, w_down, loss_target, m_c_ctx, m_w_mod, m_b_mod, m_norm_mix_g, m_norm_ffn_g, m_w_in, m_lru_conv_w, m_lru_conv_b, m_lru_wa, m_lru_ba, m_lru_wx, m_lru_bx, m_lru_lambda, m_q_norm_g, m_k_norm_g, m_na_rpb, m_w_rnn_out, m_w_na_out, m_w_out, m_w_up, m_ffn_conv_w, m_ffn_conv_b, m_w_down, v_c_ctx, v_w_mod, v_b_mod, v_norm_mix_g, v_norm_ffn_g, v_w_in, v_lru_conv_w, v_lru_conv_b, v_lru_wa, v_lru_ba, v_lru_wx, v_lru_bx, v_lru_lambda, v_q_norm_g, v_k_norm_g, v_na_rpb, v_w_rnn_out, v_w_na_out, v_w_out, v_w_up, v_ffn_conv_w, v_ffn_conv_b, v_w_down):
    weights = dict(c_ctx=c_ctx, w_mod=w_mod, b_mod=b_mod, norm_mix_g=norm_mix_g, norm_ffn_g=norm_ffn_g, w_in=w_in,
                   lru_conv_w=lru_conv_w, lru_conv_b=lru_conv_b, lru_wa=lru_wa, lru_ba=lru_ba, lru_wx=lru_wx,
                   lru_bx=lru_bx, lru_lambda=lru_lambda, q_norm_g=q_norm_g, k_norm_g=k_norm_g, na_rpb=na_rpb,
                   w_rnn_out=w_rnn_out, w_na_out=w_na_out, w_out=w_out, w_up=w_up, ffn_conv_w=ffn_conv_w,
                   ffn_conv_b=ffn_conv_b, w_down=w_down)
    mom1 = dict(c_ctx=m_c_ctx, w_mod=m_w_mod, b_mod=m_b_mod, norm_mix_g=m_norm_mix_g, norm_ffn_g=m_norm_ffn_g,
                w_in=m_w_in, lru_conv_w=m_lru_conv_w, lru_conv_b=m_lru_conv_b, lru_wa=m_lru_wa, lru_ba=m_lru_ba,
                lru_wx=m_lru_wx, lru_bx=m_lru_bx, lru_lambda=m_lru_lambda, q_norm_g=m_q_norm_g, k_norm_g=m_k_norm_g,
                na_rpb=m_na_rpb, w_rnn_out=m_w_rnn_out, w_na_out=m_w_na_out, w_out=m_w_out, w_up=m_w_up,
                ffn_conv_w=m_ffn_conv_w, ffn_conv_b=m_ffn_conv_b, w_down=m_w_down)
    mom2 = dict(c_ctx=v_c_ctx, w_mod=v_w_mod, b_mod=v_b_mod, norm_mix_g=v_norm_mix_g, norm_ffn_g=v_norm_ffn_g,
                w_in=v_w_in, lru_conv_w=v_lru_conv_w, lru_conv_b=v_lru_conv_b, lru_wa=v_lru_wa, lru_ba=v_lru_ba,
                lru_wx=v_lru_wx, lru_bx=v_lru_bx, lru_lambda=v_lru_lambda, q_norm_g=v_q_norm_g, k_norm_g=v_k_norm_g,
                na_rpb=v_na_rpb, w_rnn_out=v_w_rnn_out, w_na_out=v_w_na_out, w_out=v_w_out, w_up=v_w_up,
                ffn_conv_w=v_ffn_conv_w, ffn_conv_b=v_ffn_conv_b, w_down=v_w_down)
    order = list(weights)
    d = D_MODEL
    mx_, my_, mc_ = _mesh_pos()
    shard = 2 * mx_ + my_
    dev = 2 * shard + mc_

    local_small, small_spans = _pack([c] + [weights[k][0] for k in SMALL_SHARD])
    gath = all_gather8(local_small, "gather_small").reshape(N_DEV, local_small.shape[0], LANES)
    per_dev = [_unpack(gath[k], small_spans) for k in range(N_DEV)]
    c_all = jnp.concatenate([per_dev[k][0] for k in range(N_DEV)], axis=0)
    full_small = {name: jnp.concatenate([per_dev[2 * s][1 + i] for s in range(N_SHARD)], axis=-1)
                  for i, name in enumerate(SMALL_SHARD)}
    c16 = jnp.concatenate([c_all, c_ctx.reshape(1, d), jnp.zeros((7, d), F32)], axis=0)
    s16, mod_part = mod_fwd(c16, w_mod[0])
    mod_all = all_gather8(mod_part, "gather_mod").reshape(N_DEV, 16, MOD_COLS)
    mod = jnp.concatenate([mod_all[2 * s] for s in range(N_SHARD)], axis=1) + b_mod
    modx = lax.dynamic_slice(mod, (dev, 0), (1, N_MOD * d))
    modc = mod[8:9]

    idx = jnp.stack([shard, mc_]).astype(jnp.int32)
    c_idx = jnp.reshape(mc_, (1,)).astype(jnp.int32)
    wsh = {name: cast_into_full(weights[name][0], full, axis, idx, "cast_" + name) for name, full, axis in BIG}
    w_in_full = run_comm(gather_weights_comm([wsh["w_in"]], [0]), "gather_w_in")[0]

    z = jnp.concatenate([ctx[0], x[0]], axis=0)
    res = local_step(z, loss_target[0], modx, modc, norm_mix_g, norm_ffn_g, w_in_full, full_small["lru_conv_w"],
                     lru_conv_b, lru_wa[0], full_small["lru_ba"], lru_wx[0], full_small["lru_bx"],
                     full_small["lru_lambda"], q_norm_g, k_norm_g, na_rpb[0], wsh["w_rnn_out"], wsh["w_na_out"],
                     wsh["w_out"], wsh["w_up"], full_small["ffn_conv_w"], ffn_conv_b, wsh["w_down"], c_idx=c_idx)

    halves = [add_pieces(res["partials"][i], res["pieces"][i], idx, BIG[i][2], "add_pieces_" + BIG[i][0])
              for i in range(len(BIG))]
    joined = join_halves(halves)
    grads = {name: joined[i].reshape(_shard_shape(full, axis)) for i, (name, full, axis) in enumerate(BIG)}

    for k in ("lru_wa", "lru_wx"):
        grads[k] = sum_blocks(res["lru_w_all"][k], "sum_" + k).reshape(weights[k].shape[1:])
    small_names = ["norm_mix_g", "norm_ffn_g", "lru_conv_w", "lru_conv_b", "lru_ba", "lru_bx",
                   "lru_lambda", "q_norm_g", "k_norm_g", "na_rpb", "ffn_conv_w", "ffn_conv_b"]
    local_g, g_spans = _pack([res["loss_sq"][0:1, 0:1], res["d_modx"], res["d_modc"]] + [res[k] for k in small_names])
    n_rows = local_g.shape[0]
    g_all, g_tot = all_gather8(local_g, "allreduce_small", with_sum=True)
    tot = _unpack(g_tot, g_spans)
    loss = (0.5 / d) * tot[0][0, 0]
    small_tot = dict(zip(small_names, tot[3:]))
    at_x = g_spans[1][0]
    dmx_rows = g_all.reshape(N_DEV, n_rows, LANES)[:, at_x:at_x + N_MOD, :].reshape(N_DEV, N_MOD * d)
    dmc_row = jnp.concatenate([tot[2], jnp.zeros((1, 4 * d), F32)], axis=1)
    dm16 = jnp.concatenate([dmx_rows, dmc_row, jnp.zeros((7, N_MOD * d), F32)], axis=0)
    grads["b_mod"] = add_rows(tot[1], dmc_row, "b_mod_grad")
    g_w_mod, ds16 = mod_bwd(s16, lax.dynamic_slice(dm16, (0, shard * MOD_COLS), (16, MOD_COLS)), w_mod[0])
    grads["w_mod"] = g_w_mod
    ds_parts = all_gather8(ds16[8:16], "gather_dsctx")
    grads["c_ctx"] = cctx_grad(ds_parts, c_ctx.reshape(1, d))
    for k in small_names:
        g = small_tot[k]
        if k in SMALL_SHARD:
            w_sh = weights[k].shape[-1]
            g = lax.dynamic_slice_in_dim(g, shard * w_sh, w_sh, axis=g.ndim - 1)
        grads[k] = g

    delta, new_m, new_v = {}, {}, {}
    for name, _, _ in BIG + (("w_mod", None, None),):
        delta[name], new_m[name], new_v[name] = adamw(weights[name][0], grads[name], mom1[name][0], mom2[name][0],
                                                      "adamw_" + name)
    rest = [k for k in order if k not in delta]
    views = {k: (grads[k].shape if grads[k].ndim <= 3 else (-1, grads[k].shape[-1])) for k in rest}
    small = adamw_many(*[[t[k].reshape(views[k]) for k in rest] for t in (weights, grads, mom1, mom2)])
    n_rest = len(rest)
    for i, k in enumerate(rest):
        delta[k], new_m[k], new_v[k] = small[i], small[n_rest + i], small[2 * n_rest + i]

    shaped = lambda t: [t[k].reshape(weights[k].shape) for k in order]
    return (loss, res["grad_x"][None], *shaped(grads), *shaped(delta), *shaped(new_m), *shaped(new_v))
```

```python
import numpy as np
import jax
import jax.numpy as jnp
from jax import lax
from jax.experimental import pallas as pl
from jax.experimental.pallas import tpu as pltpu

F32 = jnp.float32
BF16 = jnp.bfloat16

D_MODEL = 1024
SEQ = 2048
CTX_LEN = 256
ZLEN = SEQ + CTX_LEN
GRID_W = 64
GRID_ROWS = SEQ // GRID_W
LRU_BLOCK_W = 128
LRU_BLOCKS = 8
LRU_C = 8.0
NA_HEADS = 16
HEAD_DIM = 64
NA_ROWS = 8
NA_COLS = 16
ROPE_BASE = 10000.0
D_FF = 2816
N_MOD = 6
IN_COLS = 7 * D_MODEL
EPS = 1e-6
NEG_INF = -1e30
N_DEV = 8
N_SHARD = 4

ADAM_LR = 0.001
ADAM_B1 = 0.9
ADAM_B2 = 0.999
ADAM_EPS = 1e-08
ADAM_WD = 0.01
ADAM_STEP = 10

ROW_TILE = 256
Q_ROWS = 4
Q_TILE = Q_ROWS * GRID_W
KEY_ROWS = 12
KEY_TILE = KEY_ROWS * GRID_W
BT_PAD = 4
BT_LEN = 24
VMEM_LIMIT_V7X = 56 * 1024 * 1024

MESH_T = pl.DeviceIdType.MESH


def _params(*sem):
    return pltpu.CompilerParams(dimension_semantics=sem if sem else None, vmem_limit_bytes=VMEM_LIMIT_V7X)


def _full(shape):
    nd = len(shape)
    return pl.BlockSpec(shape, lambda *_: (0,) * nd)


class Comm:
    def __init__(self, inputs, out_shapes, aliases, scratch, emit):
        self.inputs, self.out_shapes, self.aliases, self.scratch, self.emit = inputs, out_shapes, aliases, scratch, emit


def _call(body, *, name, grid, in_specs, out_specs, out_shape, args, scratch_shapes=(), sem=(), comm=None):
    n_in, n_out, n_sc = len(in_specs), len(out_specs), len(scratch_shapes)
    if comm is None:
        res = pl.pallas_call(body, name=name, grid=grid, in_specs=list(in_specs), out_specs=list(out_specs),
                             out_shape=list(out_shape), scratch_shapes=list(scratch_shapes),
                             compiler_params=_params(*sem))(*args)
        return list(res), []
    k_in, k_out = len(comm.inputs), len(comm.out_shapes)
    steps = int(np.prod(grid))

    def hosted(*refs):
        ins, cins = refs[:n_in], refs[n_in:n_in + k_in]
        at = n_in + k_in
        outs, couts = refs[at:at + n_out], refs[at + n_out:at + n_out + k_out]
        at += n_out + k_out
        scr, cscr = refs[at:at + n_sc], refs[at + n_sc:]
        start, mid, end = comm.emit(cins, couts, cscr)
        lin = pl.program_id(0)
        for ax in range(1, len(grid)):
            lin = lin * grid[ax] + pl.program_id(ax)
        pl.when(lin == 0)(start)
        body(*ins, *outs, *scr)
        pl.when(lin == steps // 2)(mid)
        pl.when(lin == steps - 1)(end)

    hbm = pl.BlockSpec(memory_space=pl.ANY)
    res = pl.pallas_call(
        hosted, name=name, grid=grid, in_specs=list(in_specs) + [hbm] * k_in, out_specs=list(out_specs) + [hbm] * k_out,
        out_shape=list(out_shape) + list(comm.out_shapes), scratch_shapes=list(scratch_shapes) + list(comm.scratch),
        input_output_aliases={n_in + i: n_out + o for i, o in comm.aliases.items()},
        compiler_params=_params(*(("arbitrary",) * len(grid))))(*args, *comm.inputs)
    return list(res[:n_out]), list(res[n_out:])


def _sigmoid(x):
    return 0.5 * jnp.tanh(0.5 * x) + 0.5


def _gelu_parts(x):
    c0 = 0.7978845608028654
    inner = c0 * (x + 0.044715 * x * x * x)
    t = jnp.tanh(inner)
    g = 0.5 * x * (1.0 + t)
    dg = 0.5 * (1.0 + t) + 0.5 * x * (1.0 - t * t) * c0 * (1.0 + 3.0 * 0.044715 * x * x)
    return g, dg


def _dot_nt(a, b):
    return lax.dot_general(a, b, (((1,), (1,)), ((), ())), preferred_element_type=F32)


def _dot_tn(a, b):
    return lax.dot_general(a, b, (((0,), (0,)), ((), ())), preferred_element_type=F32)


def norm_mod(xin, gain, shift, scale, name):
    r, d = xin.shape
    s_mod = shift.shape[0]
    assert r % ROW_TILE == 0

    def body(x_ref, g_ref, sh_ref, sc_ref, xn_ref):
        x = x_ref[...]
        nrm = x * lax.rsqrt(jnp.mean(x * x, axis=-1, keepdims=True) + EPS)
        xn_ref[...] = ((nrm * g_ref[...]) * (1.0 + sc_ref[0]) + sh_ref[0]).astype(BF16)

    mod_spec = pl.BlockSpec((1, 1, d), lambda i: (jnp.minimum(i, s_mod - 1), 0, 0))
    return pl.pallas_call(
        body, name=name, grid=(r // ROW_TILE,),
        in_specs=[pl.BlockSpec((ROW_TILE, d), lambda i: (i, 0)), _full((1, d)), mod_spec, mod_spec],
        out_specs=pl.BlockSpec((ROW_TILE, d), lambda i: (i, 0)),
        out_shape=jax.ShapeDtypeStruct((r, d), BF16),
        compiler_params=_params("parallel"),
    )(xin, gain, shift, scale)


def matmul_wide(a, b, name, tm, tn, comm=None):
    m, k = a.shape
    n = b.shape[1]
    assert m % tm == 0 and n % tn == 0

    def body(a_ref, b_ref, o_ref):
        o_ref[...] = jnp.dot(a_ref[...], b_ref[...], preferred_element_type=F32)

    res, extra = _call(
        body, name=name, grid=(n // tn, m // tm),
        in_specs=[pl.BlockSpec((tm, k), lambda j, i: (i, 0)), pl.BlockSpec((k, tn), lambda j, i: (0, j))],
        out_specs=[pl.BlockSpec((tm, tn), lambda j, i: (i, j))],
        out_shape=[jax.ShapeDtypeStruct((m, n), F32)],
        sem=("parallel", "parallel"), args=(a, b), comm=comm)
    return res[0], extra


def _row_ids(n, w):
    return lax.broadcasted_iota(jnp.int32, (n, w), 0)


def _lru_conv(xr, cw, cb):
    row = _row_ids(ZLEN, LRU_BLOCK_W)
    segpos = jnp.where(row < CTX_LEN, row, row - CTX_LEN)
    seglen = jnp.where(row < CTX_LEN, CTX_LEN, SEQ)
    acc = xr * cw[2:3, :] + cb
    for k in (0, 1, 3):
        off = k - 2
        sh = pltpu.roll(xr, (-off) % ZLEN, 0)
        ok = (segpos + off >= 0) & (segpos + off < seglen)
        acc = acc + jnp.where(ok, sh, 0.0) * cw[k:k + 1, :]
    return acc


def _lru_conv_t(dxc, cw):
    row = _row_ids(ZLEN, LRU_BLOCK_W)
    segpos = jnp.where(row < CTX_LEN, row, row - CTX_LEN)
    seglen = jnp.where(row < CTX_LEN, CTX_LEN, SEQ)
    acc = dxc * cw[2:3, :]
    for k in (0, 1, 3):
        off = k - 2
        sh = pltpu.roll(dxc, off % ZLEN, 0)
        ok = (segpos - off >= 0) & (segpos - off < seglen)
        acc = acc + jnp.where(ok, sh, 0.0) * cw[k:k + 1, :]
    return acc


def _lru_gates(xc, xcb, wa, ba, wx, bx, lam):
    r = _sigmoid(jnp.dot(xcb, wa, preferred_element_type=F32) + ba)
    i = _sigmoid(jnp.dot(xcb, wx, preferred_element_type=F32) + bx)
    sp = jnp.maximum(-lam, 0.0) + jnp.log1p(jnp.exp(-jnp.abs(lam)))
    la = (-LRU_C) * r * sp
    a = jnp.exp(la)
    sq = jnp.sqrt(-jnp.tanh(la) * (1.0 + a * a))
    b = sq * i * xc
    return r, i, sp, a, sq, b


def _scan8_fwd(a, b, rid):
    for s in (1, 2, 4):
        a_s = pltpu.roll(a, s, 0)
        b_s = pltpu.roll(b, s, 0)
        m = rid >= s
        b = jnp.where(m, a * b_s + b, b)
        a = jnp.where(m, a * a_s, a)
    return a, b


def _scan8_rev(a, b, rid):
    for s in (1, 2, 4):
        a_s = pltpu.roll(a, 8 - s, 0)
        b_s = pltpu.roll(b, 8 - s, 0)
        m = rid < 8 - s
        b = jnp.where(m, a * b_s + b, b)
        a = jnp.where(m, a * a_s, a)
    return a, b


N_CHUNK = ZLEN // 8
CTX_CHUNKS = CTX_LEN // 8
SCAN_UNROLL = 8


def _scan_up(a_ref, b_ref, h_ref, lo, hi, carry):
    rid = _row_ids(8, LRU_BLOCK_W)
    assert (hi - lo) % SCAN_UNROLL == 0

    def step(g, c):
        base = pl.multiple_of((lo + g * SCAN_UNROLL) * 8, 8)
        for u in range(SCAN_UNROLL):
            sl = pl.ds(base + 8 * u, 8)
            a, b = _scan8_fwd(a_ref[sl, :], b_ref[sl, :], rid)
            h = b + a * c
            h_ref[sl, :] = h
            c = h[7:8, :]
        return c

    return lax.fori_loop(0, (hi - lo) // SCAN_UNROLL, step, carry)


def _scan_down(a_ref, b_ref, h_ref, lo, hi, carry):
    rid = _row_ids(8, LRU_BLOCK_W)
    assert (hi - lo) % SCAN_UNROLL == 0

    def step(g, c):
        base = pl.multiple_of((hi - (g + 1) * SCAN_UNROLL) * 8, 8)
        for u in reversed(range(SCAN_UNROLL)):
            sl = pl.ds(base + 8 * u, 8)
            a, b = _scan8_rev(a_ref[sl, :], b_ref[sl, :], rid)
            h = b + a * c
            h_ref[sl, :] = h
            c = h[0:1, :]
        return c

    return lax.fori_loop(0, (hi - lo) // SCAN_UNROLL, step, carry)


def _lru_scan_dir(d, a_ref, b_ref, h_ref):
    zero = jnp.zeros((1, LRU_BLOCK_W), F32)
    if d == 0:
        _scan_up(a_ref, b_ref, h_ref, 0, N_CHUNK, zero)
    else:
        c = _scan_down(a_ref, b_ref, h_ref, 0, CTX_CHUNKS, zero)
        _scan_down(a_ref, b_ref, h_ref, CTX_CHUNKS, N_CHUNK, c)


def _lru_in_specs():
    blk = lambda rows: pl.BlockSpec((rows, LRU_BLOCK_W), lambda b: (0, b))
    wspec = pl.BlockSpec((2, 1, LRU_BLOCK_W, LRU_BLOCK_W), lambda b: (0, b, 0, 0))
    return blk, wspec


def lru_fwd(p, conv_w, conv_b, wa, ba, wx, bx, lam, comm=None):
    blk, wspec = _lru_in_specs()

    def body(xr_ref, gx_ref, cw_ref, cb_ref, wa_ref, ba_ref, wx_ref, bx_ref, lam_ref, y_ref, a_s, b_s, h_s, hsum_s):
        xr = xr_ref[...]
        xc = _lru_conv(xr, cw_ref[...], cb_ref[...])
        xcb = xc.astype(BF16)
        for d in (0, 1):
            _, _, _, a, _, b = _lru_gates(xc, xcb, wa_ref[d, 0].astype(BF16), ba_ref[d:d + 1, :],
                                          wx_ref[d, 0].astype(BF16), bx_ref[d:d + 1, :], lam_ref[d:d + 1, :])
            a_s[...] = a
            b_s[...] = b
            _lru_scan_dir(d, a_s, b_s, h_s)
            if d == 0:
                hsum_s[...] = h_s[...]
            else:
                hsum_s[...] = hsum_s[...] + h_s[...]
        g, _ = _gelu_parts(gx_ref[CTX_LEN:, :])
        y_ref[...] = (hsum_s[CTX_LEN:, :] * g).astype(BF16)

    zs = pltpu.VMEM((ZLEN, LRU_BLOCK_W), F32)
    res, extra = _call(
        body, name="lru_fwd", grid=(LRU_BLOCKS,),
        in_specs=[blk(ZLEN), pl.BlockSpec((ZLEN, LRU_BLOCK_W), lambda b: (0, 24 + b)), blk(4), blk(1),
                  wspec, blk(2), wspec, blk(2), blk(2)],
        out_specs=[pl.BlockSpec((SEQ, LRU_BLOCK_W), lambda b: (0, b))],
        out_shape=[jax.ShapeDtypeStruct((SEQ, D_MODEL), BF16)],
        scratch_shapes=[zs, zs, zs, zs], sem=("arbitrary",),
        args=(p, p, conv_w, conv_b, wa, ba, wx, bx, lam), comm=comm)
    return res[0], extra


def _rope_tables():
    t = np.arange(SEQ)
    lane = np.arange(2 * HEAD_DIM)
    in_head = lane % HEAD_DIM
    j = (in_head % 32) % 16
    freq = ROPE_BASE ** (-j.astype(np.float64) / 16.0)
    pos = np.where(in_head[None, :] < 32, (t // GRID_W)[:, None], (t % GRID_W)[:, None]).astype(np.float64)
    ang = (pos.astype(np.float32) * freq.astype(np.float32)[None, :]).astype(np.float32)
    cos = np.cos(ang).astype(np.float32)
    sin = np.sin(ang).astype(np.float32)
    sgn = np.where((in_head % 32) < 16, -1.0, 1.0).astype(np.float32)
    cos = np.concatenate([np.ones((CTX_LEN, 2 * HEAD_DIM), np.float32), cos], 0)
    sin = np.concatenate([np.zeros((CTX_LEN, 2 * HEAD_DIM), np.float32), sin * sgn[None, :]], 0)
    return jnp.asarray(cos), jnp.asarray(sin)


def _head_ones():
    lane = np.arange(2 * HEAD_DIM)
    return jnp.asarray((lane[:, None] // HEAD_DIM == lane[None, :] // HEAD_DIM).astype(np.float32))


def _rope_partner(x):
    lane = lax.broadcasted_iota(jnp.int32, x.shape, 1)
    return jnp.where((lane % 32) < 16, pltpu.roll(x, 128 - 16, 1), pltpu.roll(x, 16, 1))


def _head_rms(x, ones, gain):
    ms = jnp.dot(x * x, ones, preferred_element_type=F32, precision=lax.Precision.HIGHEST) * (1.0 / HEAD_DIM)
    rstd = lax.rsqrt(ms + EPS)
    return x * rstd * gain, rstd


PREP_TILE = 768


def qkv_prep(p, qg2, kg2, cos, sin, ones):
    scale = HEAD_DIM ** -0.5

    def body(q_ref, k_ref, v_ref, qg_ref, kg_ref, cos_ref, sin_ref, ones_ref, qr_ref, qp_ref, kk_ref, vv_ref):
        ones_m = ones_ref[...]
        c, s = cos_ref[...], sin_ref[...]
        qn, _ = _head_rms(q_ref[...], ones_m, qg_ref[...])
        qn = qn * scale
        qr_ref[...] = (qn * c + _rope_partner(qn) * s).astype(BF16)
        qp_ref[...] = qn.astype(BF16)
        kn, _ = _head_rms(k_ref[...], ones_m, kg_ref[...])
        kk_ref[...] = (kn * c + _rope_partner(kn) * s).astype(BF16)
        vv_ref[...] = v_ref[...].astype(BF16)

    col = lambda base: pl.BlockSpec((PREP_TILE, 128), lambda hp, i: (i, base + hp))
    small = pl.BlockSpec((1, 128), lambda hp, i: (0, 0))
    tab = pl.BlockSpec((PREP_TILE, 128), lambda hp, i: (i, 0))
    oshape = jax.ShapeDtypeStruct((ZLEN, D_MODEL), BF16)
    return pl.pallas_call(
        body, name="qkv_prep", grid=(NA_HEADS // 2, ZLEN // PREP_TILE),
        in_specs=[col(32), col(8), col(16), small, small, tab, tab, _full((128, 128))],
        out_specs=[col(0)] * 4, out_shape=[oshape] * 4,
        compiler_params=_params("parallel", "parallel"),
    )(p, p, p, qg2, kg2, cos, sin, ones)


def _bias_expand():
    qc = np.arange(GRID_W)[:, None]
    kc = np.arange(GRID_W)[None, :]
    col_start = np.clip(qc - NA_COLS // 2, 0, GRID_W - NA_COLS)
    in_win = (kc >= col_start) & (kc < col_start + NA_COLS)
    dc = np.clip(kc - qc, -(NA_COLS - 1), NA_COLS - 1) + (NA_COLS - 1)
    e = np.zeros((2 * NA_COLS - 1, GRID_W, GRID_W), np.float32)
    for d in range(2 * NA_COLS - 1):
        e[d] = ((dc == d) & in_win).astype(np.float32)
    pen = np.where(in_win, 0.0, NEG_INF).astype(np.float32)
    return e, pen


def bias_table(rpb2):
    e, pen = _bias_expand()
    n_dr = 2 * NA_ROWS - 1
    ea = np.zeros((31, GRID_W, 128), np.float32)
    ea[:, :, :GRID_W] = e
    eb = np.zeros((31, GRID_W, 128), np.float32)
    eb[:, :, GRID_W:] = e
    pen2 = np.concatenate([pen, pen], 1)
    ea = jnp.asarray(ea.reshape(31, GRID_W * 128))
    eb = jnp.asarray(eb.reshape(31, GRID_W * 128))
    sel_a = np.zeros((BT_LEN, n_dr), np.float32)
    sel_b = np.zeros((BT_LEN, n_dr), np.float32)
    for r in range(BT_LEN):
        dr = r - BT_PAD
        if 0 <= dr < n_dr:
            sel_a[r, dr] = 1.0
        if 0 <= dr + 1 < n_dr:
            sel_b[r, dr + 1] = 1.0
    sel_a, sel_b = jnp.asarray(sel_a), jnp.asarray(sel_b)
    pen2 = jnp.asarray(pen2.reshape(1, GRID_W * 128))
    hi = lax.Precision.HIGHEST

    def body(rpb_ref, sa_ref, sb_ref, ea_ref, eb_ref, pen_ref, o_ref, ra_s, rb_s):
        for h in range(NA_HEADS):
            rp = rpb_ref[h]
            ra_s[h * BT_LEN:(h + 1) * BT_LEN, :] = jnp.dot(sa_ref[...], rp, preferred_element_type=F32, precision=hi)
            rb_s[h * BT_LEN:(h + 1) * BT_LEN, :] = jnp.dot(sb_ref[...], rp, preferred_element_type=F32, precision=hi)
        o_ref[...] = (jnp.dot(ra_s[...], ea_ref[...], preferred_element_type=F32, precision=hi)
                      + jnp.dot(rb_s[...], eb_ref[...], preferred_element_type=F32, precision=hi) + pen_ref[...])

    tcol = 2048
    rows = NA_HEADS * BT_LEN
    out = pl.pallas_call(
        body, name="bias_table", grid=(GRID_W * 128 // tcol,),
        in_specs=[_full((NA_HEADS, n_dr, 31)), _full((BT_LEN, n_dr)), _full((BT_LEN, n_dr)),
                  pl.BlockSpec((31, tcol), lambda j: (0, j)), pl.BlockSpec((31, tcol), lambda j: (0, j)),
                  pl.BlockSpec((1, tcol), lambda j: (0, j))],
        out_specs=pl.BlockSpec((rows, tcol), lambda j: (0, j)),
        out_shape=jax.ShapeDtypeStruct((rows, GRID_W * 128), F32),
        scratch_shapes=[pltpu.VMEM((rows, 31), F32), pltpu.VMEM((rows, 31), F32)],
        compiler_params=_params("parallel"),
    )(rpb2, sel_a, sel_b, ea, eb, pen2)
    return out.reshape(NA_HEADS, BT_LEN, GRID_W, 128)


def _key_window(j):
    ws = jnp.clip(Q_ROWS * j - 4, 0, GRID_ROWS - KEY_ROWS)
    return ws, pl.multiple_of(CTX_LEN + ws * GRID_W, 256)


def _head_mask(hh):
    lane = lax.broadcasted_iota(jnp.int32, (Q_TILE, 128), 1)
    return (lane < HEAD_DIM) if hh == 0 else (lane >= HEAD_DIM)


def _attn_scores(j, ws, q_rot_h, q_pl_h, kw, kc, hh, bt_ref, s_ref):
    s_ref[:, :KEY_TILE] = _dot_nt(q_rot_h, kw)
    s_ref[:, KEY_TILE:] = _dot_nt(q_pl_h, kc)
    lane = lax.broadcasted_iota(jnp.int32, (GRID_W, 128), 1)
    base = ws - Q_ROWS * j + (NA_ROWS - 1) + BT_PAD
    for qi in range(Q_ROWS):
        rs = jnp.clip(Q_ROWS * j + qi - NA_ROWS // 2, 0, GRID_ROWS - NA_ROWS)
        for m in range(KEY_ROWS // 2):
            k0 = ws + 2 * m
            p0 = jnp.where((k0 >= rs) & (k0 < rs + NA_ROWS), 0.0, NEG_INF)
            p1 = jnp.where((k0 + 1 >= rs) & (k0 + 1 < rs + NA_ROWS), 0.0, NEG_INF)
            pen = jnp.where(lane < GRID_W, p0, p1)
            rows = slice(qi * GRID_W, (qi + 1) * GRID_W)
            cols = slice(128 * m, 128 * (m + 1))
            s_ref[rows, cols] = s_ref[rows, cols] + bt_ref[hh, base + 2 * m - qi] + pen
    return base


def attn_fwd(q_rot, q_pl, kk, vv, bt, comm=None):
    def body(qr_ref, qp_ref, kk_ref, vv_ref, bt_ref, o_ref, lse_ref, s_ref):
        j = pl.program_id(1)
        ws, start = _key_window(j)
        win = pl.ds(start, KEY_TILE)
        kw, kc = kk_ref[win, :], kk_ref[:CTX_LEN, :]
        vw, vc = vv_ref[win, :], vv_ref[:CTX_LEN, :]
        qr, qp = qr_ref[...], qp_ref[...]
        outs = []
        for hh in range(2):
            msk = _head_mask(hh)
            _attn_scores(j, ws, jnp.where(msk, qr, 0), jnp.where(msk, qp, 0), kw, kc, hh, bt_ref, s_ref)
            s = s_ref[...]
            mx = jnp.max(s, axis=-1, keepdims=True)
            pr = jnp.exp(s - mx)
            l = jnp.sum(pr, axis=-1, keepdims=True)
            prb = pr.astype(BF16)
            o = jnp.dot(prb[:, :KEY_TILE], vw, preferred_element_type=F32)
            o = o + jnp.dot(prb[:, KEY_TILE:], vc, preferred_element_type=F32)
            outs.append(o / l)
            lse_ref[hh] = mx + jnp.log(l)
        o_ref[...] = jnp.where(_head_mask(0), outs[0], outs[1])

    qspec = pl.BlockSpec((Q_TILE, 128), lambda hp, j: (j + 1, hp))
    kspec = pl.BlockSpec((ZLEN, 128), lambda hp, j: (0, hp))
    res, extra = _call(
        body, name="attn_fwd", grid=(NA_HEADS // 2, SEQ // Q_TILE),
        in_specs=[qspec, qspec, kspec, kspec, pl.BlockSpec((2, BT_LEN, GRID_W, 128), lambda hp, j: (hp, 0, 0, 0))],
        out_specs=[pl.BlockSpec((Q_TILE, 128), lambda hp, j: (j, hp)),
                   pl.BlockSpec((2, Q_TILE, 1), lambda hp, j: (hp, j, 0))],
        out_shape=[jax.ShapeDtypeStruct((SEQ, D_MODEL), F32), jax.ShapeDtypeStruct((NA_HEADS, SEQ, 1), F32)],
        scratch_shapes=[pltpu.VMEM((Q_TILE, KEY_TILE + CTX_LEN), F32)], sem=("parallel", "arbitrary"),
        args=(q_rot, q_pl, kk, vv, bt), comm=comm)
    return res[0], res[1], extra


def merge_fwd(y_rnn, y_na, p, z, g2, w_rnn, w_na, w_out):
    def body(yr_ref, yn_ref, mr_ref, mn_ref, x_ref, g2_ref, wr_ref, wn_ref, wo_ref, u_ref, v_ref, mg_ref, out_ref, x1_ref):
        u = jnp.dot(yr_ref[...], wr_ref[...], preferred_element_type=F32)
        v = jnp.dot(yn_ref[...].astype(BF16), wn_ref[...], preferred_element_type=F32)
        merged = (_sigmoid(mr_ref[...]) * u + _sigmoid(mn_ref[...]) * v).astype(BF16)
        out = jnp.dot(merged, wo_ref[...], preferred_element_type=F32)
        u_ref[...] = u
        v_ref[...] = v
        mg_ref[...] = merged
        out_ref[...] = out
        x1_ref[...] = x_ref[...] + g2_ref[...] * out

    row = pl.BlockSpec((ROW_TILE, D_MODEL), lambda i: (i, 0))
    lat = lambda cb: pl.BlockSpec((ROW_TILE, D_MODEL), lambda i: (i + 1, cb))
    wspec = _full((D_MODEL, D_MODEL))
    f32o = jax.ShapeDtypeStruct((SEQ, D_MODEL), F32)
    return pl.pallas_call(
        body, name="merge_fwd", grid=(SEQ // ROW_TILE,),
        in_specs=[row, row, lat(5), lat(6), lat(0), _full((1, D_MODEL)), wspec, wspec, wspec],
        out_specs=[row] * 5,
        out_shape=[f32o, f32o, jax.ShapeDtypeStruct((SEQ, D_MODEL), BF16), f32o, f32o],
        compiler_params=_params("parallel"),
    )(y_rnn, y_na, p, p, z, g2, w_rnn, w_na, w_out)


FF_TILE = 256
FF_TILES = D_FF // FF_TILE


def _ffn_conv(h, cw, cb):
    row = _row_ids(SEQ, FF_TILE)
    prev = jnp.where(row >= 1, pltpu.roll(h, 1, 0), 0.0)
    nxt = jnp.where(row < SEQ - 1, pltpu.roll(h, SEQ - 1, 0), 0.0)
    return prev * cw[0:1, :] + h * cw[1:2, :] + nxt * cw[2:3, :] + cb


def ffn_act(hpre, conv_w, conv_b):
    def body(ha_ref, hg_ref, wa_ref, wg_ref, ba_ref, bg_ref, o_ref):
        a = _ffn_conv(ha_ref[...], wa_ref[...], ba_ref[...])
        g = _ffn_conv(hg_ref[...], wg_ref[...], bg_ref[...])
        o_ref[...] = (a * _sigmoid(a) * g).astype(BF16)

    col = lambda rows, off: pl.BlockSpec((rows, FF_TILE), lambda j: (0, j + off))
    return pl.pallas_call(
        body, name="ffn_act", grid=(FF_TILES,),
        in_specs=[col(SEQ, 0), col(SEQ, FF_TILES), col(3, 0), col(3, FF_TILES), col(1, 0), col(1, FF_TILES)],
        out_specs=col(SEQ, 0),
        out_shape=jax.ShapeDtypeStruct((SEQ, D_FF), BF16),
        compiler_params=_params("parallel"),
    )(hpre, hpre, conv_w, conv_w, conv_b, conv_b)


def ffn_down_loss(act, w_down, x1, g5, target):
    def body(a_ref, w_ref, x1_ref, g5_ref, t_ref, f_ref, dy_ref, df_ref, ls_ref, dg_ref):
        i = pl.program_id(0)
        f = jnp.dot(a_ref[...], w_ref[...], preferred_element_type=F32)
        g5 = g5_ref[...]
        err = x1_ref[...] + g5 * f - t_ref[...]
        dy = err * (1.0 / D_MODEL)
        f_ref[...] = f
        dy_ref[...] = dy
        df_ref[...] = (dy * g5).astype(BF16)

        @pl.when(i == 0)
        def _():
            ls_ref[...] = jnp.zeros_like(ls_ref)
            dg_ref[...] = jnp.zeros_like(dg_ref)

        ls_ref[...] = ls_ref[...] + jnp.sum(err * err)
        dg_ref[...] = dg_ref[...] + jnp.sum(dy * f, axis=0, keepdims=True)

    row = pl.BlockSpec((ROW_TILE, D_MODEL), lambda i: (i, 0))
    f32o = jax.ShapeDtypeStruct((SEQ, D_MODEL), F32)
    return pl.pallas_call(
        body, name="ffn_down_loss", grid=(SEQ // ROW_TILE,),
        in_specs=[pl.BlockSpec((ROW_TILE, D_FF), lambda i: (i, 0)), _full((D_FF, D_MODEL)), row, _full((1, D_MODEL)), row],
        out_specs=[row, row, row, _full((8, 128)), _full((1, D_MODEL))],
        out_shape=[f32o, f32o, jax.ShapeDtypeStruct((SEQ, D_MODEL), BF16), jax.ShapeDtypeStruct((8, 128), F32),
                   jax.ShapeDtypeStruct((1, D_MODEL), F32)],
        compiler_params=_params("arbitrary"),
    )(act, w_down, x1, g5, target)


def ffn_down_bwd(df, w_down):
    def body(df_ref, w_ref, o_ref):
        o_ref[...] = _dot_nt(df_ref[...], w_ref[...])

    return pl.pallas_call(
        body, name="ffn_down_bwd", grid=(SEQ // ROW_TILE,),
        in_specs=[pl.BlockSpec((ROW_TILE, D_MODEL), lambda i: (i, 0)), _full((D_FF, D_MODEL))],
        out_specs=pl.BlockSpec((ROW_TILE, D_FF), lambda i: (i, 0)),
        out_shape=jax.ShapeDtypeStruct((SEQ, D_FF), F32),
        compiler_params=_params("parallel"),
    )(df, w_down)


def ffn_act_bwd(hpre, d_act, conv_w, conv_b):
    def half_bwd(dc, h, w, dh_ref, dw_ref, db_ref):
        row = _row_ids(SEQ, FF_TILE)
        h_prev = jnp.where(row >= 1, pltpu.roll(h, 1, 0), 0.0)
        h_next = jnp.where(row < SEQ - 1, pltpu.roll(h, SEQ - 1, 0), 0.0)
        dw_ref[0:1, :] = jnp.sum(dc * h_prev, axis=0, keepdims=True)
        dw_ref[1:2, :] = jnp.sum(dc * h, axis=0, keepdims=True)
        dw_ref[2:3, :] = jnp.sum(dc * h_next, axis=0, keepdims=True)
        db_ref[...] = jnp.sum(dc, axis=0, keepdims=True)
        dc_next = jnp.where(row < SEQ - 1, pltpu.roll(dc, SEQ - 1, 0), 0.0)
        dc_prev = jnp.where(row >= 1, pltpu.roll(dc, 1, 0), 0.0)
        dh_ref[...] = (dc_next * w[0:1, :] + dc * w[1:2, :] + dc_prev * w[2:3, :]).astype(BF16)

    def body(ha_ref, hg_ref, da_ref, wa_ref, wg_ref, ba_ref, bg_ref, dha_ref, dhg_ref, dwa_ref, dwg_ref, dba_ref, dbg_ref):
        ha, hg = ha_ref[...], hg_ref[...]
        a = _ffn_conv(ha, wa_ref[...], ba_ref[...])
        g = _ffn_conv(hg, wg_ref[...], bg_ref[...])
        sig = _sigmoid(a)
        dact = da_ref[...]
        half_bwd(dact * g * (sig * (1.0 + a * (1.0 - sig))), ha, wa_ref[...], dha_ref, dwa_ref, dba_ref)
        half_bwd(dact * a * sig, hg, wg_ref[...], dhg_ref, dwg_ref, dbg_ref)

    col = lambda rows, off: pl.BlockSpec((rows, FF_TILE), lambda j: (0, j + off))
    hshape = jax.ShapeDtypeStruct((SEQ, D_FF), BF16)
    wshape = jax.ShapeDtypeStruct((3, D_FF), F32)
    bshape = jax.ShapeDtypeStruct((1, D_FF), F32)
    return pl.pallas_call(
        body, name="ffn_act_bwd", grid=(FF_TILES,),
        in_specs=[col(SEQ, 0), col(SEQ, FF_TILES), col(SEQ, 0), col(3, 0), col(3, FF_TILES), col(1, 0), col(1, FF_TILES)],
        out_specs=[col(SEQ, 0), col(SEQ, 0), col(3, 0), col(3, 0), col(1, 0), col(1, 0)],
        out_shape=[hshape, hshape, wshape, wshape, bshape, bshape],
        compiler_params=_params("parallel"),
    )(hpre, hpre, d_act, conv_w, conv_w, conv_b, conv_b)


def _norm_mod_bwd(x, dxn, gain, scale):
    rstd = lax.rsqrt(jnp.mean(x * x, axis=-1, keepdims=True) + EPS)
    nrm = x * rstd
    dsh = jnp.sum(dxn, axis=0, keepdims=True)
    dsc = jnp.sum(dxn * nrm, axis=0, keepdims=True) * gain
    dgn = jnp.sum(dxn * nrm, axis=0, keepdims=True) * (1.0 + scale)
    dn = dxn * (gain * (1.0 + scale))
    dx = rstd * (dn - nrm * jnp.mean(dn * nrm, axis=-1, keepdims=True))
    return dx, dsh, dsc, dgn


def ffn_up_bwd(dha, dhg, w_up, x1, dy, gain, scale):
    def body(dha_ref, dhg_ref, w_ref, x_ref, dy_ref, g_ref, sc_ref, dx_ref, dsh_ref, dsc_ref, dgn_ref):
        i = pl.program_id(0)
        dxn = _dot_nt(dha_ref[...], w_ref[:, :D_FF]) + _dot_nt(dhg_ref[...], w_ref[:, D_FF:])
        dx, dsh, dsc, dgn = _norm_mod_bwd(x_ref[...], dxn, g_ref[...], sc_ref[...])
        dx_ref[...] = dy_ref[...] + dx

        @pl.when(i == 0)
        def _():
            dsh_ref[...] = dsh
            dsc_ref[...] = dsc
            dgn_ref[...] = dgn

        @pl.when(i > 0)
        def _():
            dsh_ref[...] = dsh_ref[...] + dsh
            dsc_ref[...] = dsc_ref[...] + dsc
            dgn_ref[...] = dgn_ref[...] + dgn

    row = pl.BlockSpec((ROW_TILE, D_MODEL), lambda i: (i, 0))
    vec = _full((1, D_MODEL))
    vshape = jax.ShapeDtypeStruct((1, D_MODEL), F32)
    return pl.pallas_call(
        body, name="ffn_up_bwd", grid=(SEQ // ROW_TILE,),
        in_specs=[pl.BlockSpec((ROW_TILE, D_FF), lambda i: (i, 0)), pl.BlockSpec((ROW_TILE, D_FF), lambda i: (i, 0)),
                  _full((D_MODEL, 2 * D_FF)), row, row, vec, vec],
        out_specs=[row, vec, vec, vec],
        out_shape=[jax.ShapeDtypeStruct((SEQ, D_MODEL), F32), vshape, vshape, vshape],
        compiler_params=_params("arbitrary"),
    )(dha, dhg, w_up, x1, dy, gain, scale)


def merge_bwd(dx1, out, g2, p, u, v, w_rnn, w_na, w_out):
    def body(dx_ref, out_ref, g2_ref, mr_ref, mn_ref, u_ref, v_ref, wr_ref, wn_ref, wo_ref,
             dout_ref, du_ref, dv_ref, dmr_ref, dmn_ref, dyr_ref, dyn_ref, dg2_ref):
        i = pl.program_id(0)

        @pl.when(i == 0)
        def _():
            dmr_ref[...] = jnp.zeros_like(dmr_ref)
            dmn_ref[...] = jnp.zeros_like(dmn_ref)
            dg2_ref[...] = jnp.zeros_like(dg2_ref)

        @pl.when(i > 0)
        def _():
            dx = dx_ref[...]
            dg2_ref[...] = dg2_ref[...] + jnp.sum(dx * out_ref[...], axis=0, keepdims=True)
            dout = (dx * g2_ref[...]).astype(BF16)
            dout_ref[...] = dout
            dm = _dot_nt(dout, wo_ref[...])
            sr = _sigmoid(mr_ref[...])
            sn = _sigmoid(mn_ref[...])
            du = (dm * sr).astype(BF16)
            dv = (dm * sn).astype(BF16)
            du_ref[...] = du
            dv_ref[...] = dv
            dmr_ref[...] = (dm * u_ref[...] * (sr * (1.0 - sr))).astype(BF16)
            dmn_ref[...] = (dm * v_ref[...] * (sn * (1.0 - sn))).astype(BF16)
            dyr_ref[...] = _dot_nt(du, wr_ref[...])
            dyn_ref[...] = _dot_nt(dv, wn_ref[...])

    lat = pl.BlockSpec((ROW_TILE, D_MODEL), lambda i: (jnp.maximum(i - 1, 0), 0))
    zrow = pl.BlockSpec((ROW_TILE, D_MODEL), lambda i: (i, 0))
    pcol = lambda cb: pl.BlockSpec((ROW_TILE, D_MODEL), lambda i: (i, cb))
    wspec = _full((D_MODEL, D_MODEL))
    tb = jax.ShapeDtypeStruct((SEQ, D_MODEL), BF16)
    zb = jax.ShapeDtypeStruct((ZLEN, D_MODEL), BF16)
    tf = jax.ShapeDtypeStruct((SEQ, D_MODEL), F32)
    return pl.pallas_call(
        body, name="merge_bwd", grid=(ZLEN // ROW_TILE,),
        in_specs=[lat, lat, _full((1, D_MODEL)), pcol(5), pcol(6), lat, lat, wspec, wspec, wspec],
        out_specs=[lat, lat, lat, zrow, zrow, lat, lat, _full((1, D_MODEL))],
        out_shape=[tb, tb, tb, zb, zb, tf, tf, jax.ShapeDtypeStruct((1, D_MODEL), F32)],
        compiler_params=_params("arbitrary"),
    )(dx1, out, g2, p, p, u, v, w_rnn, w_na, w_out)


def attn_bwd(q_rot, q_pl, kk, vv, bt, y_na, d_yna, lse, comm=None):
    def body(qr_ref, qp_ref, kk_ref, vv_ref, bt_ref, o_ref, do_ref, lse_ref,
             dqr_ref, dqp_ref, dk_ref, dv_ref, dbt_ref, s_ref):
        jj = pl.program_id(1)

        @pl.when(jj == 0)
        def _():
            dqr_ref[...] = jnp.zeros_like(dqr_ref)
            dqp_ref[...] = jnp.zeros_like(dqp_ref)
            dk_ref[...] = jnp.zeros_like(dk_ref)
            dv_ref[...] = jnp.zeros_like(dv_ref)
            dbt_ref[...] = jnp.zeros_like(dbt_ref)

        @pl.when(jj > 0)
        def _():
            j = jj - 1
            ws, start = _key_window(j)
            win = pl.ds(start, KEY_TILE)
            kw, kc = kk_ref[win, :], kk_ref[:CTX_LEN, :]
            vw, vc = vv_ref[win, :], vv_ref[:CTX_LEN, :]
            qr, qp = qr_ref[...], qp_ref[...]
            do = do_ref[...]
            do_o = do * o_ref[...]
            dq_r, dq_p = [], []
            for hh in range(2):
                msk = _head_mask(hh)
                q_r, q_p = jnp.where(msk, qr, 0), jnp.where(msk, qp, 0)
                base = _attn_scores(j, ws, q_r, q_p, kw, kc, hh, bt_ref, s_ref)
                pr = jnp.exp(s_ref[...] - lse_ref[hh])
                delta = jnp.sum(jnp.where(msk, do_o, 0.0), axis=-1, keepdims=True)
                dob = jnp.where(msk, do, 0.0).astype(BF16)
                ds_lat = pr[:, :KEY_TILE] * (_dot_nt(dob, vw) - delta)
                ds_ctx = pr[:, KEY_TILE:] * (_dot_nt(dob, vc) - delta)
                for qi in range(Q_ROWS):
                    for m in range(KEY_ROWS // 2):
                        idx = base + 2 * m - qi
                        dbt_ref[hh, idx] = dbt_ref[hh, idx] + ds_lat[qi * GRID_W:(qi + 1) * GRID_W, 128 * m:128 * (m + 1)]
                dsb_lat = ds_lat.astype(BF16)
                dsb_ctx = ds_ctx.astype(BF16)
                prb = pr.astype(BF16)
                dq_r.append(jnp.dot(dsb_lat, kw, preferred_element_type=F32))
                dq_p.append(jnp.dot(dsb_ctx, kc, preferred_element_type=F32))
                dk_ref[win, :] = dk_ref[win, :] + _dot_tn(dsb_lat, q_r)
                dk_ref[:CTX_LEN, :] = dk_ref[:CTX_LEN, :] + _dot_tn(dsb_ctx, q_p)
                dv_ref[win, :] = dv_ref[win, :] + _dot_tn(prb[:, :KEY_TILE], dob)
                dv_ref[:CTX_LEN, :] = dv_ref[:CTX_LEN, :] + _dot_tn(prb[:, KEY_TILE:], dob)
            dqr_ref[...] = jnp.where(_head_mask(0), dq_r[0], dq_r[1])
            dqp_ref[...] = jnp.where(_head_mask(0), dq_p[0], dq_p[1])

    lat = lambda jj: jnp.maximum(jj - 1, 0)
    qspec = pl.BlockSpec((Q_TILE, 128), lambda hp, jj: (lat(jj) + 1, hp))
    kspec = pl.BlockSpec((ZLEN, 128), lambda hp, jj: (0, hp))
    btspec = pl.BlockSpec((2, BT_LEN, GRID_W, 128), lambda hp, jj: (hp, 0, 0, 0))
    ospec = pl.BlockSpec((Q_TILE, 128), lambda hp, jj: (lat(jj), hp))
    dqspec = pl.BlockSpec((Q_TILE, 128), lambda hp, jj: (jj, hp))
    zshape = jax.ShapeDtypeStruct((ZLEN, D_MODEL), F32)
    res, extra = _call(
        body, name="attn_bwd", grid=(NA_HEADS // 2, ZLEN // Q_TILE),
        in_specs=[qspec, qspec, kspec, kspec, btspec, ospec, ospec,
                  pl.BlockSpec((2, Q_TILE, 1), lambda hp, jj: (hp, lat(jj), 0))],
        out_specs=[dqspec, dqspec, kspec, kspec, btspec],
        out_shape=[zshape, zshape, zshape, zshape, jax.ShapeDtypeStruct((NA_HEADS, BT_LEN, GRID_W, 128), F32)],
        scratch_shapes=[pltpu.VMEM((Q_TILE, KEY_TILE + CTX_LEN), F32)], sem=("parallel", "arbitrary"),
        args=(q_rot, q_pl, kk, vv, bt, y_na, d_yna, lse), comm=comm)
    return (*res, extra)


def qkv_bwd(dq_rot, dq_pl, dk, dv, p, qg2, kg2, cos, sin, ones):
    scale = HEAD_DIM ** -0.5
    n_hp, n_i = NA_HEADS // 2, ZLEN // PREP_TILE

    def norm_rope_bwd(d_rot, d_extra, x, gain, cos_t, sin_t, ones_m, dx_ref, acc_ref):
        xh, rstd = _head_rms(x, ones_m, 1.0)
        dn = d_rot * cos_t + _rope_partner(d_rot * sin_t)
        if d_extra is not None:
            dn = (dn + d_extra) * scale
        acc_ref[...] = acc_ref[...] + jnp.sum(dn * xh, axis=0, keepdims=True)
        dxh = dn * gain
        seg = jnp.dot(dxh * xh, ones_m, preferred_element_type=F32, precision=lax.Precision.HIGHEST) * (1.0 / HEAD_DIM)
        dx_ref[...] = (rstd * (dxh - xh * seg)).astype(BF16)

    def body(dqr_ref, dqp_ref, dk_ref, dv_ref, xq_ref, xk_ref, qg_ref, kg_ref, cos_ref, sin_ref, ones_ref,
             dxq_ref, dxk_ref, dxv_ref, dgq_ref, dgk_ref, accq_ref, acck_ref):
        hp, i = pl.program_id(0), pl.program_id(1)

        @pl.when((hp == 0) & (i == 0))
        def _():
            accq_ref[...] = jnp.zeros_like(accq_ref)
            acck_ref[...] = jnp.zeros_like(acck_ref)

        ones_m = ones_ref[...]
        cos_t, sin_t = cos_ref[...], sin_ref[...]
        norm_rope_bwd(dqr_ref[...], dqp_ref[...], xq_ref[...], qg_ref[...], cos_t, sin_t, ones_m, dxq_ref, accq_ref)
        norm_rope_bwd(dk_ref[...], None, xk_ref[...], kg_ref[...], cos_t, sin_t, ones_m, dxk_ref, acck_ref)
        dxv_ref[...] = dv_ref[...].astype(BF16)

        @pl.when((hp == n_hp - 1) & (i == n_i - 1))
        def _():
            dgq_ref[...] = accq_ref[:, :HEAD_DIM] + accq_ref[:, HEAD_DIM:]
            dgk_ref[...] = acck_ref[:, :HEAD_DIM] + acck_ref[:, HEAD_DIM:]

    col = lambda base: pl.BlockSpec((PREP_TILE, 128), lambda hp, i: (i, base + hp))
    small = pl.BlockSpec((1, 128), lambda hp, i: (0, 0))
    tab = pl.BlockSpec((PREP_TILE, 128), lambda hp, i: (i, 0))
    zb = jax.ShapeDtypeStruct((ZLEN, D_MODEL), BF16)
    gshape = jax.ShapeDtypeStruct((1, HEAD_DIM), F32)
    return pl.pallas_call(
        body, name="qkv_bwd", grid=(n_hp, n_i),
        in_specs=[col(0)] * 4 + [col(32), col(8), small, small, tab, tab, _full((128, 128))],
        out_specs=[col(0)] * 3 + [_full((1, HEAD_DIM))] * 2,
        out_shape=[zb, zb, zb, gshape, gshape],
        scratch_shapes=[pltpu.VMEM((1, 128), F32)] * 2,
        compiler_params=_params("arbitrary", "arbitrary"),
    )(dq_rot, dq_pl, dk, dv, p, p, qg2, kg2, cos, sin, ones)


def rpb_grad(dbt):
    e, _ = _bias_expand()
    n_dr = 2 * NA_ROWS - 1
    ea = np.zeros((31, GRID_W, 128), np.float32)
    ea[:, :, :GRID_W] = e
    eb = np.zeros((31, GRID_W, 128), np.float32)
    eb[:, :, GRID_W:] = e
    eat = jnp.asarray(ea.reshape(31, GRID_W * 128).T.copy())
    ebt = jnp.asarray(eb.reshape(31, GRID_W * 128).T.copy())
    sel_at = np.zeros((n_dr, BT_LEN), np.float32)
    sel_bt = np.zeros((n_dr, BT_LEN), np.float32)
    for r in range(BT_LEN):
        dr = r - BT_PAD
        if 0 <= dr < n_dr:
            sel_at[dr, r] = 1.0
        if 0 <= dr + 1 < n_dr:
            sel_bt[dr + 1, r] = 1.0
    hi = lax.Precision.HIGHEST

    tk = 2048
    wide = GRID_W * 128
    rows = NA_HEADS * BT_LEN
    n_k = wide // tk

    def body(d_ref, sa_ref, sb_ref, ea_ref, eb_ref, o_ref, a_s, b_s):
        k = pl.program_id(0)
        dm = d_ref[...]
        a = jnp.dot(dm, ea_ref[...], preferred_element_type=F32, precision=hi)
        b = jnp.dot(dm, eb_ref[...], preferred_element_type=F32, precision=hi)

        @pl.when(k == 0)
        def _():
            a_s[...] = a
            b_s[...] = b

        @pl.when(k > 0)
        def _():
            a_s[...] = a_s[...] + a
            b_s[...] = b_s[...] + b

        @pl.when(k == n_k - 1)
        def _():
            for h in range(NA_HEADS):
                sl = slice(h * BT_LEN, (h + 1) * BT_LEN)
                o_ref[h] = (jnp.dot(sa_ref[...], a_s[sl, :], preferred_element_type=F32, precision=hi)
                            + jnp.dot(sb_ref[...], b_s[sl, :], preferred_element_type=F32, precision=hi))

    return pl.pallas_call(
        body, name="rpb_grad", grid=(n_k,),
        in_specs=[pl.BlockSpec((rows, tk), lambda k: (0, k)), _full((n_dr, BT_LEN)), _full((n_dr, BT_LEN)),
                  pl.BlockSpec((tk, 31), lambda k: (k, 0)), pl.BlockSpec((tk, 31), lambda k: (k, 0))],
        out_specs=_full((NA_HEADS, n_dr, 31)),
        out_shape=jax.ShapeDtypeStruct((NA_HEADS, n_dr, 31), F32),
        scratch_shapes=[pltpu.VMEM((rows, 31), F32), pltpu.VMEM((rows, 31), F32)],
        compiler_params=_params("arbitrary"),
    )(dbt.reshape(rows, wide), jnp.asarray(sel_at), jnp.asarray(sel_bt), eat, ebt)


def lru_bwd(p, d_yrnn, conv_w, conv_b, wa, ba, wx, bx, lam, comm=None):
    blk, wspec = _lru_in_specs()

    def body(xr_ref, gx_ref, dy_ref, cw_ref, cb_ref, wa_ref, ba_ref, wx_ref, bx_ref, lam_ref,
             dxr_ref, dgx_ref, dcw_ref, dcb_ref, dwa_ref, dba_ref, dwx_ref, dbx_ref, dlam_ref,
             a_s, b_s, h_s, l_s, hsum_s, dxc_s, dh_s):
        xr = xr_ref[...]
        cw = cw_ref[...]
        xc = _lru_conv(xr, cw, cb_ref[...])
        xcb = xc.astype(BF16)
        g, dg = _gelu_parts(gx_ref[CTX_LEN:, :])
        dy = dy_ref[...]
        dh_s[:CTX_LEN, :] = jnp.zeros((CTX_LEN, LRU_BLOCK_W), F32)
        dh_s[CTX_LEN:, :] = dy * g
        row = _row_ids(ZLEN, LRU_BLOCK_W)
        zero = jnp.zeros((1, LRU_BLOCK_W), F32)
        for d in (0, 1):
            wab = wa_ref[d, 0].astype(BF16)
            wxb = wx_ref[d, 0].astype(BF16)
            lam_d = lam_ref[d:d + 1, :]
            r, gi, sp, a, sq, b = _lru_gates(xc, xcb, wab, ba_ref[d:d + 1, :], wxb, bx_ref[d:d + 1, :], lam_d)
            a_s[...] = a
            b_s[...] = b
            _lru_scan_dir(d, a_s, b_s, h_s)
            h = h_s[...]
            if d == 0:
                hsum_s[...] = h
                h_prev = jnp.where(row >= 1, pltpu.roll(h, 1, 0), 0.0)
                a_s[...] = pltpu.roll(a, ZLEN - 1, 0)
                _scan_down(a_s, dh_s, l_s, 0, N_CHUNK, zero)
            else:
                hsum_s[...] = hsum_s[...] + h
                h_prev = jnp.where(row == CTX_LEN - 1, 0.0, pltpu.roll(h, ZLEN - 1, 0))
                a_s[...] = pltpu.roll(a, 1, 0)
                c = _scan_up(a_s, dh_s, l_s, CTX_CHUNKS, N_CHUNK, zero)
                _scan_up(a_s, dh_s, l_s, 0, CTX_CHUNKS, c)
            db = l_s[...]
            da = db * h_prev
            dsq = db * gi * xc
            dgi = db * sq * xc
            dxc_d = db * sq * gi
            dla = da * a - dsq * (a * a) / sq
            dr = dla * ((-LRU_C) * sp)
            dsp = jnp.sum(dla * ((-LRU_C) * r), axis=0, keepdims=True)
            dlam_ref[d:d + 1, :] = -dsp * _sigmoid(-lam_d)
            dzr = dr * r * (1.0 - r)
            dzi = dgi * gi * (1.0 - gi)
            dba_ref[d:d + 1, :] = jnp.sum(dzr, axis=0, keepdims=True)
            dbx_ref[d:d + 1, :] = jnp.sum(dzi, axis=0, keepdims=True)
            dzrb = dzr.astype(BF16)
            dzib = dzi.astype(BF16)
            dwa_ref[d, 0] = _dot_tn(xcb, dzrb)
            dwx_ref[d, 0] = _dot_tn(xcb, dzib)
            dxc_d = dxc_d + _dot_nt(dzrb, wab) + _dot_nt(dzib, wxb)
            if d == 0:
                dxc_s[...] = dxc_d
            else:
                dxc_s[...] = dxc_s[...] + dxc_d
        dxc = dxc_s[...]
        dxr_ref[...] = _lru_conv_t(dxc, cw).astype(BF16)
        dcb_ref[...] = jnp.sum(dxc, axis=0, keepdims=True)
        segpos = jnp.where(row < CTX_LEN, row, row - CTX_LEN)
        seglen = jnp.where(row < CTX_LEN, CTX_LEN, SEQ)
        for k in range(4):
            off = k - 2
            if off == 0:
                sh = xr
            else:
                ok = (segpos + off >= 0) & (segpos + off < seglen)
                sh = jnp.where(ok, pltpu.roll(xr, (-off) % ZLEN, 0), 0.0)
            dcw_ref[k:k + 1, :] = jnp.sum(dxc * sh, axis=0, keepdims=True)
        dgx_ref[:CTX_LEN, :] = jnp.zeros((CTX_LEN, LRU_BLOCK_W), BF16)
        dgx_ref[CTX_LEN:, :] = (dy * hsum_s[CTX_LEN:, :] * dg).astype(BF16)

    zs = pltpu.VMEM((ZLEN, LRU_BLOCK_W), F32)
    zb = jax.ShapeDtypeStruct((ZLEN, D_MODEL), BF16)
    v2 = jax.ShapeDtypeStruct((2, D_MODEL), F32)
    w4 = jax.ShapeDtypeStruct((2, LRU_BLOCKS, LRU_BLOCK_W, LRU_BLOCK_W), F32)
    res, extra = _call(
        body, name="lru_bwd", grid=(LRU_BLOCKS,),
        in_specs=[blk(ZLEN), pl.BlockSpec((ZLEN, LRU_BLOCK_W), lambda b: (0, 24 + b)), blk(SEQ), blk(4), blk(1),
                  wspec, blk(2), wspec, blk(2), blk(2)],
        out_specs=[blk(ZLEN), blk(ZLEN), blk(4), blk(1), wspec, blk(2), wspec, blk(2), blk(2)],
        out_shape=[zb, zb, jax.ShapeDtypeStruct((4, D_MODEL), F32), jax.ShapeDtypeStruct((1, D_MODEL), F32),
                   w4, v2, w4, v2, v2],
        scratch_shapes=[zs] * 7, sem=("arbitrary",),
        args=(p, p, d_yrnn, conv_w, conv_b, wa, ba, wx, bx, lam), comm=comm)
    return (*res, extra)


def in_proj_bwd(dgs, w_in, z, dx1, gain, scale, comm=None):
    def body(*refs):
        dg_refs = refs[:7]
        w_ref, z_ref, dx1_ref, g_ref, sc_ref, gx_ref, dsh_ref, dsc_ref, dgn_ref = refs[7:]
        i = pl.program_id(0)
        dxn = _dot_nt(dg_refs[0][...], w_ref[:, 0:D_MODEL])
        for g in range(1, 7):
            dxn = dxn + _dot_nt(dg_refs[g][...], w_ref[:, g * D_MODEL:(g + 1) * D_MODEL])
        dx, dsh, dsc, dgn = _norm_mod_bwd(z_ref[...], dxn, g_ref[...], sc_ref[0])

        @pl.when(i <= 1)
        def _():
            dsh_ref[0] = dsh
            dsc_ref[0] = dsc

        @pl.when(i > 1)
        def _():
            dsh_ref[0] = dsh_ref[0] + dsh
            dsc_ref[0] = dsc_ref[0] + dsc

        @pl.when(i == 0)
        def _():
            dgn_ref[...] = dgn

        @pl.when(i > 0)
        def _():
            dgn_ref[...] = dgn_ref[...] + dgn
            gx_ref[...] = dx1_ref[...] + dx

    zrow = pl.BlockSpec((ROW_TILE, D_MODEL), lambda i: (i, 0))
    lat = pl.BlockSpec((ROW_TILE, D_MODEL), lambda i: (jnp.maximum(i - 1, 0), 0))
    mod = pl.BlockSpec((1, 1, D_MODEL), lambda i: (jnp.minimum(i, 1), 0, 0))
    mshape = jax.ShapeDtypeStruct((2, 1, D_MODEL), F32)
    res, extra = _call(
        body, name="in_proj_bwd", grid=(ZLEN // ROW_TILE,),
        in_specs=[zrow] * 7 + [_full((D_MODEL, IN_COLS)), zrow, lat, _full((1, D_MODEL)), mod],
        out_specs=[lat, mod, mod, _full((1, D_MODEL))],
        out_shape=[jax.ShapeDtypeStruct((SEQ, D_MODEL), F32), mshape, mshape, jax.ShapeDtypeStruct((1, D_MODEL), F32)],
        sem=("arbitrary",), args=(*dgs, w_in, z, dx1, gain, scale), comm=comm)
    return (*res, extra)


def matmul_tn(a, b, name, tm, tn, prev=None, col_block=0, total_cols=None):
    k, m = a.shape
    n = b.shape[1]
    total_cols = n if total_cols is None else total_cols
    assert m % tm == 0 and n % tn == 0
    off = col_block * (n // tn)

    def body(a_ref, b_ref, *rest):
        rest[-1][...] = _dot_tn(a_ref[...].astype(BF16), b_ref[...]).astype(BF16)

    in_specs = [pl.BlockSpec((k, tm), lambda i, j: (0, i)), pl.BlockSpec((k, tn), lambda i, j: (0, j))]
    args = [a, b]
    aliases = {}
    if prev is not None:
        in_specs.append(pl.BlockSpec(memory_space=pl.ANY))
        args.append(prev)
        aliases = {2: 0}
    return pl.pallas_call(
        body, name=name, grid=(m // tm, n // tn), in_specs=in_specs,
        out_specs=pl.BlockSpec((tm, tn), lambda i, j: (i, j + off)),
        out_shape=jax.ShapeDtypeStruct((m, total_cols), BF16),
        input_output_aliases=aliases,
        compiler_params=_params("parallel", "parallel"),
    )(*args)


def local_step(z, target, modx, modc, norm_mix_g, norm_ffn_g, w_in, conv_w, conv_b, wa, ba, wx, bx, lam, qg, kg, rpb,
               w_rnn, w_na, w_out, w_up, fconv_w, fconv_b, w_down, c_idx=None):
    dist = c_idx is not None
    d = D_MODEL
    mx = [modx[:, k * d:(k + 1) * d] for k in range(N_MOD)]
    shift = jnp.stack([modc[:, 0:d], mx[0]])
    scale = jnp.stack([modc[:, d:2 * d], mx[1]])
    cos, sin = _rope_tables()
    ones = _head_ones()
    qg2 = jnp.tile(qg, (1, 2))
    kg2 = jnp.tile(kg, (1, 2))

    xn = norm_mod(z, norm_mix_g, shift, scale, "norm_mix")
    p, got = matmul_wide(xn, w_in, "in_proj", 3 * ROW_TILE, 1792,
                         comm=gather_weights_comm([w_rnn, w_na, w_out], [1, 2, 3]) if dist else None)
    if dist:
        w_rnn, w_na, w_out = got
    y_rnn, got = lru_fwd(p, conv_w, conv_b, wa, ba, wx, bx, lam,
                         comm=gather_weights_comm([w_down], [5]) if dist else None)
    if dist:
        w_down = got[0]
    q_rot, q_pl, kk, vv = qkv_prep(p, qg2, kg2, cos, sin, ones)
    bt = bias_table(rpb)
    y_na, lse, got = attn_fwd(q_rot, q_pl, kk, vv, bt, comm=gather_weights_comm([w_up], [4]) if dist else None)
    if dist:
        w_up = got[0]
    u, v, merged, out, x1 = merge_fwd(y_rnn, y_na, p, z, mx[2], w_rnn, w_na, w_out)
    xn2 = norm_mod(x1, norm_ffn_g, mx[3][None], mx[4][None], "norm_ffn")
    hpre, _ = matmul_wide(xn2, w_up, "ffn_up", 2 * ROW_TILE, 1408)
    act = ffn_act(hpre, fconv_w, fconv_b)
    f, dy, df, loss_sq, dg5 = ffn_down_loss(act, w_down, x1, mx[5], target)

    partials, pieces = {}, {}

    def chip_partials(which, grads, tag):
        views = [_grad_view(g, BIG[w][1], BIG[w][2]) for w, g in zip(which, grads)]
        recv = exchange_halves(views, "grad_exchange_" + tag)
        for w, gv, r in zip(which, views, recv):
            partials[w] = add_halves(gv, r, c_idx, "add_halves_" + BIG[w][0])
        return scatter_pieces_comm([partials[w] for w in which], which)

    d_act = ffn_down_bwd(df, w_down)
    dha, dhg, d_fcw_a, d_fcw_g, d_fcb_a, d_fcb_g = ffn_act_bwd(hpre, d_act, fconv_w, fconv_b)
    d_fcw = jnp.concatenate([d_fcw_a, d_fcw_g], axis=1)
    d_fcb = jnp.concatenate([d_fcb_a, d_fcb_g], axis=1)
    dx1, d_s3, d_s4, d_gffn = ffn_up_bwd(dha, dhg, w_up, x1, dy, norm_ffn_g, mx[4])
    g_w_down = matmul_tn(act, df, "gw_down", 256, D_MODEL)
    g_w_up = matmul_tn(xn2, dha, "gw_up_a", 512, 1408, total_cols=2 * D_FF)
    g_w_up = matmul_tn(xn2, dhg, "gw_up_g", 512, 1408, prev=g_w_up, col_block=1, total_cols=2 * D_FF)
    dout, du, dv, dmr, dmn, dyr, dyn, dg2 = merge_bwd(dx1, out, mx[2], p, u, v, w_rnn, w_na, w_out)
    g_w_out = matmul_tn(merged, dout, "gw_out", 1024, 512)
    g_w_rnn = matmul_tn(y_rnn, du, "gw_rnn", 1024, 512)
    g_w_na = matmul_tn(y_na, dv, "gw_na", 1024, 512)
    *lru_grads, got = lru_bwd(p, dyr, conv_w, conv_b, wa, ba, wx, bx, lam,
                              comm=chip_partials([4, 5], [g_w_up, g_w_down], "ffn") if dist else None)
    dxr, dgx, d_cw, d_cb, d_wa, d_ba, d_wx, d_bx, d_lam = lru_grads
    if dist:
        pieces[4], pieces[5] = got
    lru_w_all = {}
    dqr, dqp, dk, dvh, dbt, got = attn_bwd(
        q_rot, q_pl, kk, vv, bt, y_na, dyn, lse,
        comm=join_comms(chip_partials([1, 2, 3], [g_w_rnn, g_w_na, g_w_out], "mix"),
                        join_comms(all_gather_comm(d_wa.reshape(-1, LRU_BLOCK_W)),
                                   all_gather_comm(d_wx.reshape(-1, LRU_BLOCK_W)))) if dist else None)
    if dist:
        pieces[1], pieces[2], pieces[3], lru_w_all["lru_wa"], lru_w_all["lru_wx"] = got
    dq_cols, dk_cols, dv_cols, d_qg, d_kg = qkv_bwd(dqr, dqp, dk, dvh, p, qg2, kg2, cos, sin, ones)
    d_rpb = rpb_grad(dbt)
    dgs = [dxr, dk_cols, dv_cols, dgx, dq_cols, dmr, dmn]
    grad_x, dsh, dsc, d_gmix, _ = in_proj_bwd(dgs, w_in, z, dx1, norm_mix_g, scale)
    g_w_in = None
    for g in range(7):
        g_w_in = matmul_tn(xn, dgs[g], "gw_in_%d" % g, 1024, 512, prev=g_w_in, col_block=g, total_cols=IN_COLS)
    if dist:
        pieces[0] = run_comm(chip_partials([0], [g_w_in], "w_in"), "grad_scatter_w_in")[0]

    d_modx = jnp.concatenate([dsh[1], dsc[1], dg2, d_s3, d_s4, dg5], axis=1)
    d_modc = jnp.concatenate([dsh[0], dsc[0]], axis=1)
    return dict(loss_sq=loss_sq, grad_x=grad_x, d_modx=d_modx, d_modc=d_modc, norm_mix_g=d_gmix, norm_ffn_g=d_gffn,
                w_in=g_w_in, lru_conv_w=d_cw, lru_conv_b=d_cb, lru_wa=d_wa, lru_ba=d_ba, lru_wx=d_wx, lru_bx=d_bx,
                lru_lambda=d_lam, q_norm_g=d_qg, k_norm_g=d_kg, na_rpb=d_rpb, w_rnn_out=g_w_rnn, w_na_out=g_w_na,
                w_out=g_w_out, w_up=g_w_up, ffn_conv_w=d_fcw, ffn_conv_b=d_fcb, w_down=g_w_down,
                partials=partials, pieces=pieces, lru_w_all=lru_w_all)


def _mesh_pos():
    return lax.axis_index("x"), lax.axis_index("y"), lax.axis_index("c")


def _other_chips(x, y):
    return [(1 - x, y), (x, 1 - y), (1 - x, 1 - y)]


def all_gather8(xs, name, with_sum=False):
    m, n = xs.shape
    assert m % 8 == 0

    def body(x_ref, out_ref, *rest):
        if with_sum:
            sum_ref, send_sems, recv_sems, local_sem = rest
        else:
            send_sems, recv_sems, local_sem = rest
        x, y, c = _mesh_pos()
        me, sibling = (x, y, c), (x, y, 1 - c)
        chips = _other_chips(x, y)

        def rows(px, py, pc):
            return out_ref.at[pl.ds((4 * px + 2 * py + pc) * m, m), :]

        def copy(k, block, to, src=None):
            return pltpu.make_async_remote_copy(
                src_ref=rows(*block) if src is None else src, dst_ref=rows(*block),
                send_sem=send_sems.at[k], recv_sem=recv_sems.at[k], device_id=to, device_id_type=MESH_T)

        mine = pltpu.make_async_copy(x_ref, rows(*me), local_sem)
        mine.start()
        first = [copy(0, me, sibling, src=x_ref)]
        first += [copy(1 + j, me, (*chip, c), src=x_ref) for j, chip in enumerate(chips)]
        for cp in first:
            cp.start()
        passed = [copy(4 + j, (*chip, c), sibling) for j, chip in enumerate(chips)]
        for j, chip in enumerate(chips):
            copy(1 + j, (*chip, c), me).wait_recv()
            passed[j].start()
        copy(0, sibling, me).wait_recv()
        for j, chip in enumerate(chips):
            copy(4 + j, (*chip, 1 - c), me).wait_recv()
        for cp in first + passed:
            cp.wait_send()
        mine.wait()
        if with_sum:
            acc = out_ref[0:m, :]
            for k in range(1, N_DEV):
                acc = acc + out_ref[k * m:(k + 1) * m, :]
            sum_ref[...] = acc

    vm = pl.BlockSpec(memory_space=pltpu.VMEM)
    out_shape = [jax.ShapeDtypeStruct((N_DEV * m, n), F32)]
    if with_sum:
        out_shape.append(jax.ShapeDtypeStruct((m, n), F32))
    res = pl.pallas_call(
        body, name=name, in_specs=[vm], out_specs=[vm] * len(out_shape), out_shape=out_shape,
        scratch_shapes=[pltpu.SemaphoreType.DMA((7,)), pltpu.SemaphoreType.DMA((7,)), pltpu.SemaphoreType.DMA],
        compiler_params=pltpu.CompilerParams(vmem_limit_bytes=VMEM_LIMIT_V7X),
    )(xs)
    return res if with_sum else res[0]


BIG = (("w_in", (D_MODEL, IN_COLS), 1), ("w_rnn_out", (D_MODEL, D_MODEL), 0), ("w_na_out", (D_MODEL, D_MODEL), 0),
       ("w_out", (D_MODEL, D_MODEL), 0), ("w_up", (D_MODEL, 2 * D_FF), 1), ("w_down", (D_FF, D_MODEL), 0))


def _shard_shape(full, axis):
    r, c = full
    return (r // N_SHARD, c) if axis == 0 else (r, c // N_SHARD)


def _slot(ref, full, axis, s, h):
    r, c = full
    if axis == 0:
        rs = r // N_SHARD
        return ref.at[pl.ds(s * rs + h * (rs // 2), rs // 2), :]
    cs = c // N_SHARD
    return ref.at[pl.ds(h * (r // 2), r // 2), pl.ds(s * cs, cs)]


def cast_into_full(x, full, axis, idx, name):
    r, c = x.shape
    tr = next(t for t in (512, 352, 256, 128) if r % t == 0)
    nb = r // tr

    def body(idx_ref, x_ref, o_ref):
        o_ref[...] = x_ref[...].astype(BF16)

    if axis == 0:
        out_spec = pl.BlockSpec((tr, c), lambda i, idx_ref: (idx_ref[0] * nb + i, 0))
    else:
        out_spec = pl.BlockSpec((tr, c), lambda i, idx_ref: (i, idx_ref[0]))
    return pl.pallas_call(
        body, name=name,
        grid_spec=pltpu.PrefetchScalarGridSpec(
            num_scalar_prefetch=1, grid=(nb,), in_specs=[pl.BlockSpec((tr, c), lambda i, idx_ref: (i, 0))],
            out_specs=out_spec),
        out_shape=jax.ShapeDtypeStruct(full, BF16),
        compiler_params=_params("parallel"),
    )(idx, x)


def run_comm(comm, name):
    k_in, k_out = len(comm.inputs), len(comm.out_shapes)

    def body(*refs):
        start, mid, end = comm.emit(refs[:k_in], refs[k_in:k_in + k_out], refs[k_in + k_out:])
        start()
        mid()
        end()

    hbm = pl.BlockSpec(memory_space=pl.ANY)
    return pl.pallas_call(
        body, name=name, in_specs=[hbm] * k_in, out_specs=[hbm] * k_out, out_shape=list(comm.out_shapes),
        input_output_aliases=dict(comm.aliases), scratch_shapes=list(comm.scratch),
        compiler_params=pltpu.CompilerParams(vmem_limit_bytes=VMEM_LIMIT_V7X),
    )(*comm.inputs)


def gather_weights_comm(fulls, which):
    nw = len(which)
    specs = [BIG[w] for w in which]

    def emit(_, outs, sems):
        send1, recv1, send2, recv2 = sems
        x, y, c = _mesh_pos()
        sibling = (x, y, 1 - c)
        chips = _other_chips(x, y)
        s_me = 2 * x + y
        shards = [2 * chip[0] + chip[1] for chip in chips]

        def ici(w, j, shard):
            _, full, axis = specs[w]
            dst = _slot(outs[w], full, axis, shard, c)
            return pltpu.make_async_remote_copy(
                src_ref=dst, dst_ref=dst, send_sem=send1.at[3 * w + j],
                recv_sem=recv1.at[3 * w + j], device_id=(*chips[j], c), device_id_type=MESH_T)

        def d2d(w, j, shard, half):
            _, full, axis = specs[w]
            dst = _slot(outs[w], full, axis, shard, half)
            return pltpu.make_async_remote_copy(
                src_ref=dst, dst_ref=dst, send_sem=send2.at[3 * w + j], recv_sem=recv2.at[3 * w + j],
                device_id=sibling, device_id_type=MESH_T)

        pairs = [(w, j) for w in range(nw) for j in range(3)]

        def start():
            for w, j in pairs:
                ici(w, j, s_me).start()

        def mid():
            for w, j in pairs:
                ici(w, j, shards[j]).wait_recv()
                d2d(w, j, shards[j], c).start()

        def end():
            for w, j in pairs:
                d2d(w, j, shards[j], 1 - c).wait_recv()
            for w, j in pairs:
                ici(w, j, s_me).wait_send()
                d2d(w, j, shards[j], c).wait_send()

        return start, mid, end

    return Comm(list(fulls), [jax.ShapeDtypeStruct(full, BF16) for _, full, _ in specs], {i: i for i in range(nw)},
                [pltpu.SemaphoreType.DMA((3 * nw,))] * 4, emit)


def join_comms(a, b):
    ai, ao, asc = len(a.inputs), len(a.out_shapes), len(a.scratch)

    def emit(ins, outs, sems):
        fa = a.emit(ins[:ai], outs[:ao], sems[:asc])
        fb = b.emit(ins[ai:], outs[ao:], sems[asc:])

        def both(k):
            def run():
                fa[k]()
                fb[k]()
            return run

        return both(0), both(1), both(2)

    aliases = dict(a.aliases)
    aliases.update({ai + i: ao + o for i, o in b.aliases.items()})
    return Comm(a.inputs + b.inputs, a.out_shapes + b.out_shapes, aliases, a.scratch + b.scratch, emit)


def all_gather_comm(x):
    def emit(srcs, outs, sems):
        send_sems, recv_sems, local_sem = sems
        x_ref, out_ref = srcs[0], outs[0]
        x, y, c = _mesh_pos()
        me, sibling = (x, y, c), (x, y, 1 - c)
        chips = _other_chips(x, y)

        def blk(px, py, pc):
            return out_ref.at[4 * px + 2 * py + pc]

        def copy(k, block, to, src=None):
            return pltpu.make_async_remote_copy(
                src_ref=blk(*block) if src is None else src, dst_ref=blk(*block),
                send_sem=send_sems.at[k], recv_sem=recv_sems.at[k], device_id=to, device_id_type=MESH_T)

        def mine():
            return pltpu.make_async_copy(x_ref, blk(*me), local_sem)

        def start():
            mine().start()
            copy(0, me, sibling, src=x_ref).start()
            for j, chip in enumerate(chips):
                copy(1 + j, me, (*chip, c), src=x_ref).start()

        def mid():
            for j, chip in enumerate(chips):
                copy(1 + j, (*chip, c), me).wait_recv()
                copy(4 + j, (*chip, c), sibling).start()

        def end():
            copy(0, sibling, me).wait_recv()
            for j, chip in enumerate(chips):
                copy(4 + j, (*chip, 1 - c), me).wait_recv()
            copy(0, me, sibling, src=x_ref).wait_send()
            for j, chip in enumerate(chips):
                copy(1 + j, me, (*chip, c), src=x_ref).wait_send()
                copy(4 + j, (*chip, c), sibling).wait_send()
            mine().wait()

        return start, mid, end

    return Comm([x], [jax.ShapeDtypeStruct((N_DEV,) + x.shape, F32)], {},
                [pltpu.SemaphoreType.DMA((7,)), pltpu.SemaphoreType.DMA((7,)), pltpu.SemaphoreType.DMA], emit)


def sum_blocks(g, name):
    _, r, c = g.shape
    tr = 256

    def body(g_ref, o_ref):
        acc = g_ref[0]
        for k in range(1, N_DEV):
            acc = acc + g_ref[k]
        o_ref[...] = acc

    return pl.pallas_call(
        body, name=name, grid=(r // tr,),
        in_specs=[pl.BlockSpec((N_DEV, tr, c), lambda i: (0, i, 0))],
        out_specs=pl.BlockSpec((tr, c), lambda i: (i, 0)),
        out_shape=jax.ShapeDtypeStruct((r, c), F32),
        compiler_params=_params("parallel"),
    )(g)


def _grad_view(g, full, axis):
    r, c = full
    if axis == 0:
        return g.reshape(N_SHARD, 2, r // N_SHARD // 2, c)
    return g.reshape(1, 2, r // 2, c)


def exchange_halves(gviews, name):
    nw = len(gviews)

    def body(*refs):
        srcs, outs = refs[:nw], refs[nw:2 * nw]
        send_sems, recv_sems = refs[2 * nw:]
        x, y, c = _mesh_pos()
        cps = []
        for w in range(nw):
            cp = pltpu.make_async_remote_copy(
                src_ref=srcs[w].at[:, pl.ds(1 - c, 1)], dst_ref=outs[w], send_sem=send_sems.at[w],
                recv_sem=recv_sems.at[w], device_id=(x, y, 1 - c), device_id_type=MESH_T)
            cp.start()
            cps.append(cp)
        for cp in cps:
            cp.wait()

    hbm = pl.BlockSpec(memory_space=pl.ANY)
    return pl.pallas_call(
        body, name=name, in_specs=[hbm] * nw, out_specs=[hbm] * nw,
        out_shape=[jax.ShapeDtypeStruct((g.shape[0], 1) + g.shape[2:], BF16) for g in gviews],
        scratch_shapes=[pltpu.SemaphoreType.DMA((nw,)), pltpu.SemaphoreType.DMA((nw,))],
        compiler_params=pltpu.CompilerParams(vmem_limit_bytes=VMEM_LIMIT_V7X),
    )(*gviews)


def _row_tile(rh):
    return 128 if rh % 128 == 0 else rh


def add_halves(gview, recv, c_idx, name):
    a, _, rh, cc = gview.shape
    tr = _row_tile(rh)

    def body(c_ref, g_ref, r_ref, o_ref):
        o_ref[0] = (g_ref[0, 0].astype(F32) + r_ref[0, 0].astype(F32)).astype(BF16)

    return pl.pallas_call(
        body, name=name,
        grid_spec=pltpu.PrefetchScalarGridSpec(
            num_scalar_prefetch=1, grid=(a, rh // tr),
            in_specs=[pl.BlockSpec((1, 1, tr, cc), lambda s, i, c_ref: (s, c_ref[0], i, 0)),
                      pl.BlockSpec((1, 1, tr, cc), lambda s, i, c_ref: (s, 0, i, 0))],
            out_specs=pl.BlockSpec((1, tr, cc), lambda s, i, c_ref: (s, i, 0))),
        out_shape=jax.ShapeDtypeStruct((a, rh, cc), BF16),
        compiler_params=_params("parallel", "parallel"),
    )(c_idx, gview, recv)


def _piece_shape(full, axis):
    rs, cs = _shard_shape(full, axis)
    return (rs // 2, cs)


def scatter_pieces_comm(partials, which):
    nw = len(which)
    specs = [BIG[w] for w in which]

    def emit(srcs, outs, sems):
        send_sems, recv_sems = sems
        x, y, c = _mesh_pos()
        chips = _other_chips(x, y)

        def copies():
            cps = []
            for w, (_, full, axis) in enumerate(specs):
                cs = full[1] // N_SHARD
                for j, chip in enumerate(chips):
                    s_j = 2 * chip[0] + chip[1]
                    src = srcs[w].at[s_j] if axis == 0 else srcs[w].at[0, :, pl.ds(s_j * cs, cs)]
                    cps.append(pltpu.make_async_remote_copy(
                        src_ref=src, dst_ref=outs[w].at[j], send_sem=send_sems.at[3 * w + j],
                        recv_sem=recv_sems.at[3 * w + j], device_id=(*chip, c), device_id_type=MESH_T))
            return cps

        def start():
            for cp in copies():
                cp.start()

        def mid():
            pass

        def end():
            for cp in copies():
                cp.wait()

        return start, mid, end

    return Comm(list(partials), [jax.ShapeDtypeStruct((3,) + _piece_shape(full, axis), BF16) for _, full, axis in specs],
                {}, [pltpu.SemaphoreType.DMA((3 * nw,)), pltpu.SemaphoreType.DMA((3 * nw,))], emit)


def add_pieces(partial, recv, idx, axis, name):
    _, rh, cs = recv.shape
    tr = _row_tile(rh)

    def body(idx_ref, p_ref, r_ref, o_ref):
        o_ref[0] = ((p_ref[0].astype(F32) + r_ref[0].astype(F32)) + r_ref[1].astype(F32)) + r_ref[2].astype(F32)

    if axis == 0:
        pspec = pl.BlockSpec((1, tr, cs), lambda i, idx_ref: (idx_ref[0], i, 0))
    else:
        pspec = pl.BlockSpec((1, tr, cs), lambda i, idx_ref: (0, i, idx_ref[0]))
    return pl.pallas_call(
        body, name=name,
        grid_spec=pltpu.PrefetchScalarGridSpec(
            num_scalar_prefetch=1, grid=(rh // tr,),
            in_specs=[pspec, pl.BlockSpec((3, tr, cs), lambda i, idx_ref: (0, i, 0))],
            out_specs=pl.BlockSpec((1, tr, cs), lambda i, idx_ref: (idx_ref[1], i, 0))),
        out_shape=jax.ShapeDtypeStruct((2, rh, cs), F32),
        compiler_params=_params("parallel"),
    )(idx, partial, recv)


def join_halves(halves):
    nw = len(BIG)

    def body(*refs):
        outs = refs[nw:2 * nw]
        send_sems, recv_sems = refs[2 * nw:]
        x, y, c = _mesh_pos()
        cps = []
        for w in range(nw):
            cp = pltpu.make_async_remote_copy(
                src_ref=outs[w].at[c], dst_ref=outs[w].at[c], send_sem=send_sems.at[w], recv_sem=recv_sems.at[w],
                device_id=(x, y, 1 - c), device_id_type=MESH_T)
            cp.start()
            cps.append(cp)
        for w in range(nw):
            cps[w].wait_send()
            pltpu.make_async_remote_copy(
                src_ref=outs[w].at[1 - c], dst_ref=outs[w].at[1 - c], send_sem=send_sems.at[w],
                recv_sem=recv_sems.at[w], device_id=(x, y, 1 - c), device_id_type=MESH_T).wait_recv()

    hbm = pl.BlockSpec(memory_space=pl.ANY)
    return pl.pallas_call(
        body, name="grad_join_halves", in_specs=[hbm] * nw, out_specs=[hbm] * nw,
        out_shape=[jax.ShapeDtypeStruct(h.shape, F32) for h in halves],
        input_output_aliases={i: i for i in range(nw)},
        scratch_shapes=[pltpu.SemaphoreType.DMA((nw,))] * 2,
        compiler_params=pltpu.CompilerParams(vmem_limit_bytes=VMEM_LIMIT_V7X),
    )(*halves)


MOD_COLS = N_MOD * D_MODEL // N_SHARD
MOD_TILE = 512


def mod_fwd(c16, w_mod):
    def body(c_ref, w_ref, s_ref, o_ref):
        cv = c_ref[...]
        s = cv * _sigmoid(cv)
        s_ref[...] = s
        o_ref[...] = jnp.dot(s.astype(BF16), w_ref[...].astype(BF16), preferred_element_type=F32)

    return pl.pallas_call(
        body, name="mod_fwd", grid=(MOD_COLS // MOD_TILE,),
        in_specs=[_full((16, D_MODEL)), pl.BlockSpec((D_MODEL, MOD_TILE), lambda j: (0, j))],
        out_specs=[_full((16, D_MODEL)), pl.BlockSpec((16, MOD_TILE), lambda j: (0, j))],
        out_shape=[jax.ShapeDtypeStruct((16, D_MODEL), F32), jax.ShapeDtypeStruct((16, MOD_COLS), F32)],
        compiler_params=_params("arbitrary"),
    )(c16, w_mod)


def mod_bwd(s16, dm16, w_mod):
    hi = lax.Precision.HIGHEST

    def body(s_ref, d_ref, w_ref, gw_ref, ds_ref):
        j = pl.program_id(0)
        dm = d_ref[...]
        gw_ref[...] = lax.dot_general(s_ref[...], dm, (((0,), (0,)), ((), ())), preferred_element_type=F32, precision=hi)
        part = lax.dot_general(dm, w_ref[...], (((1,), (1,)), ((), ())), preferred_element_type=F32, precision=hi)

        @pl.when(j == 0)
        def _():
            ds_ref[...] = part

        @pl.when(j > 0)
        def _():
            ds_ref[...] = ds_ref[...] + part

    return pl.pallas_call(
        body, name="mod_bwd", grid=(MOD_COLS // MOD_TILE,),
        in_specs=[_full((16, D_MODEL)), pl.BlockSpec((16, MOD_TILE), lambda j: (0, j)),
                  pl.BlockSpec((D_MODEL, MOD_TILE), lambda j: (0, j))],
        out_specs=[pl.BlockSpec((D_MODEL, MOD_TILE), lambda j: (0, j)), _full((16, D_MODEL))],
        out_shape=[jax.ShapeDtypeStruct((D_MODEL, MOD_COLS), F32), jax.ShapeDtypeStruct((16, D_MODEL), F32)],
        compiler_params=_params("arbitrary"),
    )(s16, dm16, w_mod)


def cctx_grad(parts, c_ctx):
    def body(p_ref, c_ref, o_ref):
        ds = p_ref[0:1, :]
        for s in range(1, N_SHARD):
            ds = ds + p_ref[16 * s:16 * s + 1, :]
        cv = c_ref[...]
        sg = _sigmoid(cv)
        o_ref[...] = ds * (sg * (1.0 + cv * (1.0 - sg)))

    return pl.pallas_call(
        body, name="cctx_grad", in_specs=[_full((N_DEV * 8, D_MODEL)), _full((1, D_MODEL))],
        out_specs=_full((1, D_MODEL)), out_shape=jax.ShapeDtypeStruct((1, D_MODEL), F32),
    )(parts, c_ctx)


def add_rows(a, b, name):
    def body(a_ref, b_ref, o_ref):
        o_ref[...] = a_ref[...] + b_ref[...]

    return pl.pallas_call(body, name=name, in_specs=[_full(a.shape), _full(b.shape)], out_specs=_full(a.shape),
                          out_shape=jax.ShapeDtypeStruct(a.shape, F32))(a, b)


def _adamw_update(w_ref, g_ref, m_ref, v_ref, d_ref, nm_ref, nv_ref):
    g_ = g_ref[...]
    m_ = ADAM_B1 * m_ref[...] + (1.0 - ADAM_B1) * g_
    v_ = ADAM_B2 * v_ref[...] + (1.0 - ADAM_B2) * (g_ * g_)
    m_hat = m_ / (1.0 - ADAM_B1 ** ADAM_STEP)
    v_hat = v_ / (1.0 - ADAM_B2 ** ADAM_STEP)
    d_ref[...] = -ADAM_LR * (m_hat / (jnp.sqrt(v_hat) + ADAM_EPS) + ADAM_WD * w_ref[...])
    nm_ref[...] = m_
    nv_ref[...] = v_


def adamw_many(ws, gs, ms, vs):
    n = len(ws)

    def body(*refs):
        for i in range(n):
            _adamw_update(*[refs[k * n + i] for k in range(7)])

    shapes = [jax.ShapeDtypeStruct(w.shape, F32) for w in ws]
    return pl.pallas_call(body, name="adamw_small", out_shape=shapes * 3,
                          compiler_params=pltpu.CompilerParams(vmem_limit_bytes=VMEM_LIMIT_V7X))(*ws, *gs, *ms, *vs)


def adamw(w, g, m, v, name):
    r, c = w.shape
    tr = 128 if (r % 128 == 0 and r > 128) else r

    def body(w_ref, g_ref, m_ref, v_ref, d_ref, nm_ref, nv_ref):
        _adamw_update(w_ref, g_ref, m_ref, v_ref, d_ref, nm_ref, nv_ref)

    spec = pl.BlockSpec((tr, c), lambda i: (i, 0))
    shp = jax.ShapeDtypeStruct((r, c), F32)
    return pl.pallas_call(
        body, name=name, grid=(r // tr,), in_specs=[spec] * 4, out_specs=[spec] * 3, out_shape=[shp] * 3,
        compiler_params=_params("parallel"),
    )(w, g, m, v)


LANES = 1024


def _pack(arrs):
    rows, spans, at = [], [], 0
    for a in arrs:
        n = int(np.prod(a.shape))
        nr = 8 * -(-n // (8 * LANES))
        flat = a.reshape(-1)
        if nr * LANES != n:
            flat = jnp.concatenate([flat, jnp.zeros((nr * LANES - n,), F32)])
        rows.append(flat.reshape(nr, LANES))
        spans.append((at, nr, n, a.shape))
        at += nr
    return jnp.concatenate(rows, axis=0), spans


def _unpack(buf, spans):
    out = []
    for at, nr, n, shape in spans:
        out.append(buf[at:at + nr].reshape(-1)[:n].reshape(shape))
    return out


SMALL_SHARD = ("lru_conv_w", "lru_ba", "lru_bx", "lru_lambda", "ffn_conv_w")


def kernel(x, c, ctx, c_ctx, w_mod, b_mod, norm_mix_g, norm_ffn_g, w_in, lru_conv_w, lru_conv_b, lru_wa, lru_ba, lru_wx, lru_bx, lru_lambda, q_norm_g, k_norm_g, na_rpb, w_rnn_out, w_na_out, w_out, w_up, ffn_conv_w, ffn_conv_b, w_down, loss_target, m_c_ctx, m_w_mod, m_b_mod, m_norm_mix_g, m_norm_ffn_g, m_w_in, m_lru_conv_w, m_lru_conv_b, m_lru_wa, m_lru_ba, m_lru_wx, m_lru_bx, m_lru_lambda, m_q_norm_g, m_k_norm_g, m_na_rpb, m_w_rnn_out, m_w_na_out, m_w_out, m_w_up, m_ffn_conv_w, m_ffn_conv_b, m_w_down, v_c_ctx, v_w_mod, v_b_mod, v_norm_mix_g, v_norm_ffn_g, v_w_in, v_lru_conv_w, v_lru_conv_b, v_lru_wa, v_lru_ba, v_lru_wx, v_lru_bx, v_lru_lambda, v_q_norm_g, v_k_norm_g, v_na_rpb, v_w_rnn_out, v_w_na_out, v_w_out, v_w_up, v_ffn_conv_w, v_ffn_conv_b, v_w_down):
    weights = dict(c_ctx=c_ctx, w_mod=w_mod, b_mod=b_mod, norm_mix_g=norm_mix_g, norm_ffn_g=norm_ffn_g, w_in=w_in,
                   lru_conv_w=lru_conv_w, lru_conv_b=lru_conv_b, lru_wa=lru_wa, lru_ba=lru_ba, lru_wx=lru_wx,
                   lru_bx=lru_bx, lru_lambda=lru_lambda, q_norm_g=q_norm_g, k_norm_g=k_norm_g, na_rpb=na_rpb,
                   w_rnn_out=w_rnn_out, w_na_out=w_na_out, w_out=w_out, w_up=w_up, ffn_conv_w=ffn_conv_w,
                   ffn_conv_b=ffn_conv_b, w_down=w_down)
    mom1 = dict(c_ctx=m_c_ctx, w_mod=m_w_mod, b_mod=m_b_mod, norm_mix_g=m_norm_mix_g, norm_ffn_g=m_norm_ffn_g,
                w_in=m_w_in, lru_conv_w=m_lru_conv_w, lru_conv_b=m_lru_conv_b, lru_wa=m_lru_wa, lru_ba=m_lru_ba,
                lru_wx=m_lru_wx, lru_bx=m_lru_bx, lru_lambda=m_lru_lambda, q_norm_g=m_q_norm_g, k_norm_g=m_k_norm_g,
                na_rpb=m_na_rpb, w_rnn_out=m_w_rnn_out, w_na_out=m_w_na_out, w_out=m_w_out, w_up=m_w_up,
                ffn_conv_w=m_ffn_conv_w, ffn_conv_b=m_ffn_conv_b, w_down=m_w_down)
    mom2 = dict(c_ctx=v_c_ctx, w_mod=v_w_mod, b_mod=v_b_mod, norm_mix_g=v_norm_mix_g, norm_ffn_g=v_norm_ffn_g,
                w_in=v_w_in, lru_conv_w=v_lru_conv_w, lru_conv_b=v_lru_conv_b, lru_wa=v_lru_wa, lru_ba=v_lru_ba,
                lru_wx=v_lru_wx, lru_bx=v_lru_bx, lru_lambda=v_lru_lambda, q_norm_g=v_q_norm_g, k_norm_g=v_k_norm_g,
                na_rpb=v_na_rpb, w_rnn_out=v_w_rnn_out, w_na_out=v_w_na_out, w_out=v_w_out, w_up=v_w_up,
                ffn_conv_w=v_ffn_conv_w, ffn_conv_b=v_ffn_conv_b, w_down=v_w_down)
    order = list(weights)
    d = D_MODEL
    mx_, my_, mc_ = _mesh_pos()
    shard = 2 * mx_ + my_
    dev = 2 * shard + mc_

    local_small, small_spans = _pack([c] + [weights[k][0] for k in SMALL_SHARD])
    gath = all_gather8(local_small, "gather_small").reshape(N_DEV, local_small.shape[0], LANES)
    per_dev = [_unpack(gath[k], small_spans) for k in range(N_DEV)]
    c_all = jnp.concatenate([per_dev[k][0] for k in range(N_DEV)], axis=0)
    full_small = {name: jnp.concatenate([per_dev[2 * s][1 + i] for s in range(N_SHARD)], axis=-1)
                  for i, name in enumerate(SMALL_SHARD)}
    c16 = jnp.concatenate([c_all, c_ctx.reshape(1, d), jnp.zeros((7, d), F32)], axis=0)
    s16, mod_part = mod_fwd(c16, w_mod[0])
    mod_all = all_gather8(mod_part, "gather_mod").reshape(N_DEV, 16, MOD_COLS)
    mod = jnp.concatenate([mod_all[2 * s] for s in range(N_SHARD)], axis=1) + b_mod
    modx = lax.dynamic_slice(mod, (dev, 0), (1, N_MOD * d))
    modc = mod[8:9]

    idx = jnp.stack([shard, mc_]).astype(jnp.int32)
    c_idx = jnp.reshape(mc_, (1,)).astype(jnp.int32)
    wsh = {name: cast_into_full(weights[name][0], full, axis, idx, "cast_" + name) for name, full, axis in BIG}
    w_in_full = run_comm(gather_weights_comm([wsh["w_in"]], [0]), "gather_w_in")[0]

    z = jnp.concatenate([ctx[0], x[0]], axis=0)
    res = local_step(z, loss_target[0], modx, modc, norm_mix_g, norm_ffn_g, w_in_full, full_small["lru_conv_w"],
                     lru_conv_b, lru_wa[0], full_small["lru_ba"], lru_wx[0], full_small["lru_bx"],
                     full_small["lru_lambda"], q_norm_g, k_norm_g, na_rpb[0], wsh["w_rnn_out"], wsh["w_na_out"],
                     wsh["w_out"], wsh["w_up"], full_small["ffn_conv_w"], ffn_conv_b, wsh["w_down"], c_idx=c_idx)

    halves = [add_pieces(res["partials"][i], res["pieces"][i], idx, BIG[i][2], "add_pieces_" + BIG[i][0])
              for i in range(len(BIG))]
    joined = join_halves(halves)
    grads = {name: joined[i].reshape(_shard_shape(full, axis)) for i, (name, full, axis) in enumerate(BIG)}

    for k in ("lru_wa", "lru_wx"):
        grads[k] = sum_blocks(res["lru_w_all"][k], "sum_" + k).reshape(weights[k].shape[1:])
    small_names = ["norm_mix_g", "norm_ffn_g", "lru_conv_w", "lru_conv_b", "lru_ba", "lru_bx",
                   "lru_lambda", "q_norm_g", "k_norm_g", "na_rpb", "ffn_conv_w", "ffn_conv_b"]
    local_g, g_spans = _pack([res["loss_sq"][0:1, 0:1], res["d_modx"], res["d_modc"]] + [res[k] for k in small_names])
    n_rows = local_g.shape[0]
    g_all, g_tot = all_gather8(local_g, "allreduce_small", with_sum=True)
    tot = _unpack(g_tot, g_spans)
    loss = (0.5 / d) * tot[0][0, 0]
    small_tot = dict(zip(small_names, tot[3:]))
    at_x = g_spans[1][0]
    dmx_rows = g_all.reshape(N_DEV, n_rows, LANES)[:, at_x:at_x + N_MOD, :].reshape(N_DEV, N_MOD * d)
    dmc_row = jnp.concatenate([tot[2], jnp.zeros((1, 4 * d), F32)], axis=1)
    dm16 = jnp.concatenate([dmx_rows, dmc_row, jnp.zeros((7, N_MOD * d), F32)], axis=0)
    grads["b_mod"] = add_rows(tot[1], dmc_row, "b_mod_grad")
    g_w_mod, ds16 = mod_bwd(s16, lax.dynamic_slice(dm16, (0, shard * MOD_COLS), (16, MOD_COLS)), w_mod[0])
    grads["w_mod"] = g_w_mod
    ds_parts = all_gather8(ds16[8:16], "gather_dsctx")
    grads["c_ctx"] = cctx_grad(ds_parts, c_ctx.reshape(1, d))
    for k in small_names:
        g = small_tot[k]
        if k in SMALL_SHARD:
            w_sh = weights[k].shape[-1]
            g = lax.dynamic_slice_in_dim(g, shard * w_sh, w_sh, axis=g.ndim - 1)
        grads[k] = g

    delta, new_m, new_v = {}, {}, {}
    for name, _, _ in BIG + (("w_mod", None, None),):
        delta[name], new_m[name], new_v[name] = adamw(weights[name][0], grads[name], mom1[name][0], mom2[name][0],
                                                      "adamw_" + name)
    rest = [k for k in order if k not in delta]
    views = {k: (grads[k].shape if grads[k].ndim <= 3 else (-1, grads[k].shape[-1])) for k in rest}
    small = adamw_many(*[[t[k].reshape(views[k]) for k in rest] for t in (weights, grads, mom1, mom2)])
    n_rest = len(rest)
    for i, k in enumerate(rest):
        delta[k], new_m[k], new_v[k] = small[i], small[n_rest + i], small[2 * n_rest + i]

    shaped = lambda t: [t[k].reshape(weights[k].shape) for k in order]
    return (loss, res["grad_x"][None], *shaped(grads), *shaped(delta), *shaped(new_m), *shaped(new_v))
```

```python
import numpy as np
import jax
import jax.numpy as jnp
from jax import lax
from jax.experimental import pallas as pl
from jax.experimental.pallas import tpu as pltpu

F32 = jnp.float32
BF16 = jnp.bfloat16

D_MODEL = 1024
SEQ = 2048
CTX_LEN = 256
ZLEN = SEQ + CTX_LEN
GRID_W = 64
GRID_ROWS = SEQ // GRID_W
LRU_BLOCK_W = 128
LRU_BLOCKS = 8
LRU_C = 8.0
NA_HEADS = 16
HEAD_DIM = 64
NA_ROWS = 8
NA_COLS = 16
ROPE_BASE = 10000.0
D_FF = 2816
N_MOD = 6
IN_COLS = 7 * D_MODEL
EPS = 1e-6
NEG_INF = -1e30
N_DEV = 8
N_SHARD = 4

ADAM_LR = 0.001
ADAM_B1 = 0.9
ADAM_B2 = 0.999
ADAM_EPS = 1e-08
ADAM_WD = 0.01
ADAM_STEP = 10

ROW_TILE = 256
Q_ROWS = 4
Q_TILE = Q_ROWS * GRID_W
KEY_ROWS = 12
KEY_TILE = KEY_ROWS * GRID_W
BT_PAD = 4
BT_LEN = 24
VMEM_LIMIT_V7X = 56 * 1024 * 1024

MESH_T = pl.DeviceIdType.MESH


def _params(*sem):
    return pltpu.CompilerParams(dimension_semantics=sem if sem else None, vmem_limit_bytes=VMEM_LIMIT_V7X)


def _full(shape):
    nd = len(shape)
    return pl.BlockSpec(shape, lambda *_: (0,) * nd)


class Comm:
    def __init__(self, inputs, out_shapes, aliases, scratch, emit):
        self.inputs, self.out_shapes, self.aliases, self.scratch, self.emit = inputs, out_shapes, aliases, scratch, emit


def _call(body, *, name, grid, in_specs, out_specs, out_shape, args, scratch_shapes=(), sem=(), comm=None):
    n_in, n_out, n_sc = len(in_specs), len(out_specs), len(scratch_shapes)
    if comm is None:
        res = pl.pallas_call(body, name=name, grid=grid, in_specs=list(in_specs), out_specs=list(out_specs),
                             out_shape=list(out_shape), scratch_shapes=list(scratch_shapes),
                             compiler_params=_params(*sem))(*args)
        return list(res), []
    k_in, k_out = len(comm.inputs), len(comm.out_shapes)
    steps = int(np.prod(grid))

    def hosted(*refs):
        ins, cins = refs[:n_in], refs[n_in:n_in + k_in]
        at = n_in + k_in
        outs, couts = refs[at:at + n_out], refs[at + n_out:at + n_out + k_out]
        at += n_out + k_out
        scr, cscr = refs[at:at + n_sc], refs[at + n_sc:]
        start, mid, end = comm.emit(cins, couts, cscr)
        lin = pl.program_id(0)
        for ax in range(1, len(grid)):
            lin = lin * grid[ax] + pl.program_id(ax)
        pl.when(lin == 0)(start)
        body(*ins, *outs, *scr)
        pl.when(lin == steps - 1 - steps // 7)(mid)
        pl.when(lin == steps - 1)(end)

    hbm = pl.BlockSpec(memory_space=pl.ANY)
    res = pl.pallas_call(
        hosted, name=name, grid=grid, in_specs=list(in_specs) + [hbm] * k_in, out_specs=list(out_specs) + [hbm] * k_out,
        out_shape=list(out_shape) + list(comm.out_shapes), scratch_shapes=list(scratch_shapes) + list(comm.scratch),
        input_output_aliases={n_in + i: n_out + o for i, o in comm.aliases.items()},
        compiler_params=_params(*(("arbitrary",) * len(grid))))(*args, *comm.inputs)
    return list(res[:n_out]), list(res[n_out:])


def _sigmoid(x):
    return 0.5 * jnp.tanh(0.5 * x) + 0.5


def _gelu_parts(x):
    c0 = 0.7978845608028654
    inner = c0 * (x + 0.044715 * x * x * x)
    t = jnp.tanh(inner)
    g = 0.5 * x * (1.0 + t)
    dg = 0.5 * (1.0 + t) + 0.5 * x * (1.0 - t * t) * c0 * (1.0 + 3.0 * 0.044715 * x * x)
    return g, dg


def _dot_nt(a, b):
    return lax.dot_general(a, b, (((1,), (1,)), ((), ())), preferred_element_type=F32)


def _dot_tn(a, b):
    return lax.dot_general(a, b, (((0,), (0,)), ((), ())), preferred_element_type=F32)


def norm_mod(xin, gain, shift, scale, name):
    r, d = xin.shape
    s_mod = shift.shape[0]
    assert r % ROW_TILE == 0

    def body(x_ref, g_ref, sh_ref, sc_ref, xn_ref):
        x = x_ref[...]
        nrm = x * lax.rsqrt(jnp.mean(x * x, axis=-1, keepdims=True) + EPS)
        xn_ref[...] = ((nrm * g_ref[...]) * (1.0 + sc_ref[0]) + sh_ref[0]).astype(BF16)

    mod_spec = pl.BlockSpec((1, 1, d), lambda i: (jnp.minimum(i, s_mod - 1), 0, 0))
    return pl.pallas_call(
        body, name=name, grid=(r // ROW_TILE,),
        in_specs=[pl.BlockSpec((ROW_TILE, d), lambda i: (i, 0)), _full((1, d)), mod_spec, mod_spec],
        out_specs=pl.BlockSpec((ROW_TILE, d), lambda i: (i, 0)),
        out_shape=jax.ShapeDtypeStruct((r, d), BF16),
        compiler_params=_params("parallel"),
    )(xin, gain, shift, scale)


def matmul_wide(a, b, name, tm, tn, comm=None):
    m, k = a.shape
    n = b.shape[1]
    assert m % tm == 0 and n % tn == 0

    def body(a_ref, b_ref, o_ref):
        o_ref[...] = jnp.dot(a_ref[...], b_ref[...], preferred_element_type=F32)

    res, extra = _call(
        body, name=name, grid=(n // tn, m // tm),
        in_specs=[pl.BlockSpec((tm, k), lambda j, i: (i, 0)), pl.BlockSpec((k, tn), lambda j, i: (0, j))],
        out_specs=[pl.BlockSpec((tm, tn), lambda j, i: (i, j))],
        out_shape=[jax.ShapeDtypeStruct((m, n), F32)],
        sem=("parallel", "parallel"), args=(a, b), comm=comm)
    return res[0], extra


def _row_ids(n, w):
    return lax.broadcasted_iota(jnp.int32, (n, w), 0)


def _lru_conv(xr, cw, cb):
    row = _row_ids(ZLEN, LRU_BLOCK_W)
    segpos = jnp.where(row < CTX_LEN, row, row - CTX_LEN)
    seglen = jnp.where(row < CTX_LEN, CTX_LEN, SEQ)
    acc = xr * cw[2:3, :] + cb
    for k in (0, 1, 3):
        off = k - 2
        sh = pltpu.roll(xr, (-off) % ZLEN, 0)
        ok = (segpos + off >= 0) & (segpos + off < seglen)
        acc = acc + jnp.where(ok, sh, 0.0) * cw[k:k + 1, :]
    return acc


def _lru_conv_t(dxc, cw):
    row = _row_ids(ZLEN, LRU_BLOCK_W)
    segpos = jnp.where(row < CTX_LEN, row, row - CTX_LEN)
    seglen = jnp.where(row < CTX_LEN, CTX_LEN, SEQ)
    acc = dxc * cw[2:3, :]
    for k in (0, 1, 3):
        off = k - 2
        sh = pltpu.roll(dxc, off % ZLEN, 0)
        ok = (segpos - off >= 0) & (segpos - off < seglen)
        acc = acc + jnp.where(ok, sh, 0.0) * cw[k:k + 1, :]
    return acc


def _lru_gates(xc, xcb, wa, ba, wx, bx, lam):
    r = _sigmoid(jnp.dot(xcb, wa, preferred_element_type=F32) + ba)
    i = _sigmoid(jnp.dot(xcb, wx, preferred_element_type=F32) + bx)
    sp = jnp.maximum(-lam, 0.0) + jnp.log1p(jnp.exp(-jnp.abs(lam)))
    la = (-LRU_C) * r * sp
    a = jnp.exp(la)
    sq = jnp.sqrt(-jnp.tanh(la) * (1.0 + a * a))
    b = sq * i * xc
    return r, i, sp, a, sq, b


def _scan8_fwd(a, b, rid):
    for s in (1, 2, 4):
        a_s = pltpu.roll(a, s, 0)
        b_s = pltpu.roll(b, s, 0)
        m = rid >= s
        b = jnp.where(m, a * b_s + b, b)
        a = jnp.where(m, a * a_s, a)
    return a, b


def _scan8_rev(a, b, rid):
    for s in (1, 2, 4):
        a_s = pltpu.roll(a, 8 - s, 0)
        b_s = pltpu.roll(b, 8 - s, 0)
        m = rid < 8 - s
        b = jnp.where(m, a * b_s + b, b)
        a = jnp.where(m, a * a_s, a)
    return a, b


N_CHUNK = ZLEN // 8
CTX_CHUNKS = CTX_LEN // 8
SCAN_UNROLL = 8


def _scan_up(a_ref, b_ref, h_ref, lo, hi, carry):
    rid = _row_ids(8, LRU_BLOCK_W)
    assert (hi - lo) % SCAN_UNROLL == 0

    def step(g, c):
        base = pl.multiple_of((lo + g * SCAN_UNROLL) * 8, 8)
        for u in range(SCAN_UNROLL):
            sl = pl.ds(base + 8 * u, 8)
            a, b = _scan8_fwd(a_ref[sl, :], b_ref[sl, :], rid)
            h = b + a * c
            h_ref[sl, :] = h
            c = h[7:8, :]
        return c

    return lax.fori_loop(0, (hi - lo) // SCAN_UNROLL, step, carry)


def _scan_down(a_ref, b_ref, h_ref, lo, hi, carry):
    rid = _row_ids(8, LRU_BLOCK_W)
    assert (hi - lo) % SCAN_UNROLL == 0

    def step(g, c):
        base = pl.multiple_of((hi - (g + 1) * SCAN_UNROLL) * 8, 8)
        for u in reversed(range(SCAN_UNROLL)):
            sl = pl.ds(base + 8 * u, 8)
            a, b = _scan8_rev(a_ref[sl, :], b_ref[sl, :], rid)
            h = b + a * c
            h_ref[sl, :] = h
            c = h[0:1, :]
        return c

    return lax.fori_loop(0, (hi - lo) // SCAN_UNROLL, step, carry)


def _lru_scan_dir(d, a_ref, b_ref, h_ref):
    zero = jnp.zeros((1, LRU_BLOCK_W), F32)
    if d == 0:
        _scan_up(a_ref, b_ref, h_ref, 0, N_CHUNK, zero)
    else:
        c = _scan_down(a_ref, b_ref, h_ref, 0, CTX_CHUNKS, zero)
        _scan_down(a_ref, b_ref, h_ref, CTX_CHUNKS, N_CHUNK, c)


def _lru_in_specs():
    blk = lambda rows: pl.BlockSpec((rows, LRU_BLOCK_W), lambda b: (0, b))
    wspec = pl.BlockSpec((2, 1, LRU_BLOCK_W, LRU_BLOCK_W), lambda b: (0, b, 0, 0))
    return blk, wspec


def lru_fwd(p, conv_w, conv_b, wa, ba, wx, bx, lam, comm=None):
    blk, wspec = _lru_in_specs()

    def body(xr_ref, gx_ref, cw_ref, cb_ref, wa_ref, ba_ref, wx_ref, bx_ref, lam_ref, y_ref, a_s, b_s, h_s, hsum_s):
        xr = xr_ref[...]
        xc = _lru_conv(xr, cw_ref[...], cb_ref[...])
        xcb = xc.astype(BF16)
        for d in (0, 1):
            _, _, _, a, _, b = _lru_gates(xc, xcb, wa_ref[d, 0].astype(BF16), ba_ref[d:d + 1, :],
                                          wx_ref[d, 0].astype(BF16), bx_ref[d:d + 1, :], lam_ref[d:d + 1, :])
            a_s[...] = a
            b_s[...] = b
            _lru_scan_dir(d, a_s, b_s, h_s)
            if d == 0:
                hsum_s[...] = h_s[...]
            else:
                hsum_s[...] = hsum_s[...] + h_s[...]
        g, _ = _gelu_parts(gx_ref[CTX_LEN:, :])
        y_ref[...] = (hsum_s[CTX_LEN:, :] * g).astype(BF16)

    zs = pltpu.VMEM((ZLEN, LRU_BLOCK_W), F32)
    res, extra = _call(
        body, name="lru_fwd", grid=(LRU_BLOCKS,),
        in_specs=[blk(ZLEN), pl.BlockSpec((ZLEN, LRU_BLOCK_W), lambda b: (0, 24 + b)), blk(4), blk(1),
                  wspec, blk(2), wspec, blk(2), blk(2)],
        out_specs=[pl.BlockSpec((SEQ, LRU_BLOCK_W), lambda b: (0, b))],
        out_shape=[jax.ShapeDtypeStruct((SEQ, D_MODEL), BF16)],
        scratch_shapes=[zs, zs, zs, zs], sem=("arbitrary",),
        args=(p, p, conv_w, conv_b, wa, ba, wx, bx, lam), comm=comm)
    return res[0], extra


def _rope_tables():
    t = np.arange(SEQ)
    lane = np.arange(2 * HEAD_DIM)
    in_head = lane % HEAD_DIM
    j = (in_head % 32) % 16
    freq = ROPE_BASE ** (-j.astype(np.float64) / 16.0)
    pos = np.where(in_head[None, :] < 32, (t // GRID_W)[:, None], (t % GRID_W)[:, None]).astype(np.float64)
    ang = (pos.astype(np.float32) * freq.astype(np.float32)[None, :]).astype(np.float32)
    cos = np.cos(ang).astype(np.float32)
    sin = np.sin(ang).astype(np.float32)
    sgn = np.where((in_head % 32) < 16, -1.0, 1.0).astype(np.float32)
    cos = np.concatenate([np.ones((CTX_LEN, 2 * HEAD_DIM), np.float32), cos], 0)
    sin = np.concatenate([np.zeros((CTX_LEN, 2 * HEAD_DIM), np.float32), sin * sgn[None, :]], 0)
    return jnp.asarray(cos), jnp.asarray(sin)


def _head_ones():
    lane = np.arange(2 * HEAD_DIM)
    return jnp.asarray((lane[:, None] // HEAD_DIM == lane[None, :] // HEAD_DIM).astype(np.float32))


def _rope_partner(x):
    lane = lax.broadcasted_iota(jnp.int32, x.shape, 1)
    return jnp.where((lane % 32) < 16, pltpu.roll(x, 128 - 16, 1), pltpu.roll(x, 16, 1))


def _head_rms(x, ones, gain):
    ms = jnp.dot(x * x, ones, preferred_element_type=F32, precision=lax.Precision.HIGHEST) * (1.0 / HEAD_DIM)
    rstd = lax.rsqrt(ms + EPS)
    return x * rstd * gain, rstd


PREP_TILE = 768


def qkv_prep(p, qg2, kg2, cos, sin, ones):
    scale = HEAD_DIM ** -0.5

    def body(q_ref, k_ref, v_ref, qg_ref, kg_ref, cos_ref, sin_ref, ones_ref, qr_ref, qp_ref, kk_ref, vv_ref):
        ones_m = ones_ref[...]
        c, s = cos_ref[...], sin_ref[...]
        qn, _ = _head_rms(q_ref[...], ones_m, qg_ref[...])
        qn = qn * scale
        qr_ref[...] = (qn * c + _rope_partner(qn) * s).astype(BF16)
        qp_ref[...] = qn.astype(BF16)
        kn, _ = _head_rms(k_ref[...], ones_m, kg_ref[...])
        kk_ref[...] = (kn * c + _rope_partner(kn) * s).astype(BF16)
        vv_ref[...] = v_ref[...].astype(BF16)

    col = lambda base: pl.BlockSpec((PREP_TILE, 128), lambda hp, i: (i, base + hp))
    small = pl.BlockSpec((1, 128), lambda hp, i: (0, 0))
    tab = pl.BlockSpec((PREP_TILE, 128), lambda hp, i: (i, 0))
    oshape = jax.ShapeDtypeStruct((ZLEN, D_MODEL), BF16)
    return pl.pallas_call(
        body, name="qkv_prep", grid=(NA_HEADS // 2, ZLEN // PREP_TILE),
        in_specs=[col(32), col(8), col(16), small, small, tab, tab, _full((128, 128))],
        out_specs=[col(0)] * 4, out_shape=[oshape] * 4,
        compiler_params=_params("parallel", "parallel"),
    )(p, p, p, qg2, kg2, cos, sin, ones)


def _bias_expand():
    qc = np.arange(GRID_W)[:, None]
    kc = np.arange(GRID_W)[None, :]
    col_start = np.clip(qc - NA_COLS // 2, 0, GRID_W - NA_COLS)
    in_win = (kc >= col_start) & (kc < col_start + NA_COLS)
    dc = np.clip(kc - qc, -(NA_COLS - 1), NA_COLS - 1) + (NA_COLS - 1)
    e = np.zeros((2 * NA_COLS - 1, GRID_W, GRID_W), np.float32)
    for d in range(2 * NA_COLS - 1):
        e[d] = ((dc == d) & in_win).astype(np.float32)
    pen = np.where(in_win, 0.0, NEG_INF).astype(np.float32)
    return e, pen


def bias_table(rpb2):
    e, pen = _bias_expand()
    n_dr = 2 * NA_ROWS - 1
    ea = np.zeros((31, GRID_W, 128), np.float32)
    ea[:, :, :GRID_W] = e
    eb = np.zeros((31, GRID_W, 128), np.float32)
    eb[:, :, GRID_W:] = e
    pen2 = np.concatenate([pen, pen], 1)
    ea = jnp.asarray(ea.reshape(31, GRID_W * 128))
    eb = jnp.asarray(eb.reshape(31, GRID_W * 128))
    sel_a = np.zeros((BT_LEN, n_dr), np.float32)
    sel_b = np.zeros((BT_LEN, n_dr), np.float32)
    for r in range(BT_LEN):
        dr = r - BT_PAD
        if 0 <= dr < n_dr:
            sel_a[r, dr] = 1.0
        if 0 <= dr + 1 < n_dr:
            sel_b[r, dr + 1] = 1.0
    sel_a, sel_b = jnp.asarray(sel_a), jnp.asarray(sel_b)
    pen2 = jnp.asarray(pen2.reshape(1, GRID_W * 128))
    hi = lax.Precision.HIGHEST

    def body(rpb_ref, sa_ref, sb_ref, ea_ref, eb_ref, pen_ref, o_ref, ra_s, rb_s):
        for h in range(NA_HEADS):
            rp = rpb_ref[h]
            ra_s[h * BT_LEN:(h + 1) * BT_LEN, :] = jnp.dot(sa_ref[...], rp, preferred_element_type=F32, precision=hi)
            rb_s[h * BT_LEN:(h + 1) * BT_LEN, :] = jnp.dot(sb_ref[...], rp, preferred_element_type=F32, precision=hi)
        o_ref[...] = (jnp.dot(ra_s[...], ea_ref[...], preferred_element_type=F32, precision=hi)
                      + jnp.dot(rb_s[...], eb_ref[...], preferred_element_type=F32, precision=hi) + pen_ref[...])

    tcol = 2048
    rows = NA_HEADS * BT_LEN
    out = pl.pallas_call(
        body, name="bias_table", grid=(GRID_W * 128 // tcol,),
        in_specs=[_full((NA_HEADS, n_dr, 31)), _full((BT_LEN, n_dr)), _full((BT_LEN, n_dr)),
                  pl.BlockSpec((31, tcol), lambda j: (0, j)), pl.BlockSpec((31, tcol), lambda j: (0, j)),
                  pl.BlockSpec((1, tcol), lambda j: (0, j))],
        out_specs=pl.BlockSpec((rows, tcol), lambda j: (0, j)),
        out_shape=jax.ShapeDtypeStruct((rows, GRID_W * 128), F32),
        scratch_shapes=[pltpu.VMEM((rows, 31), F32), pltpu.VMEM((rows, 31), F32)],
        compiler_params=_params("parallel"),
    )(rpb2, sel_a, sel_b, ea, eb, pen2)
    return out.reshape(NA_HEADS, BT_LEN, GRID_W, 128)


def _key_window(j):
    ws = jnp.clip(Q_ROWS * j - 4, 0, GRID_ROWS - KEY_ROWS)
    return ws, pl.multiple_of(CTX_LEN + ws * GRID_W, 256)


def _head_mask(hh):
    lane = lax.broadcasted_iota(jnp.int32, (Q_TILE, 128), 1)
    return (lane < HEAD_DIM) if hh == 0 else (lane >= HEAD_DIM)


def _attn_scores(j, ws, q_rot_h, q_pl_h, kw, kc, hh, bt_ref, s_ref):
    s_ref[:, :KEY_TILE] = _dot_nt(q_rot_h, kw)
    s_ref[:, KEY_TILE:] = _dot_nt(q_pl_h, kc)
    lane = lax.broadcasted_iota(jnp.int32, (GRID_W, 128), 1)
    base = ws - Q_ROWS * j + (NA_ROWS - 1) + BT_PAD
    for qi in range(Q_ROWS):
        rs = jnp.clip(Q_ROWS * j + qi - NA_ROWS // 2, 0, GRID_ROWS - NA_ROWS)
        for m in range(KEY_ROWS // 2):
            k0 = ws + 2 * m
            p0 = jnp.where((k0 >= rs) & (k0 < rs + NA_ROWS), 0.0, NEG_INF)
            p1 = jnp.where((k0 + 1 >= rs) & (k0 + 1 < rs + NA_ROWS), 0.0, NEG_INF)
            pen = jnp.where(lane < GRID_W, p0, p1)
            rows = slice(qi * GRID_W, (qi + 1) * GRID_W)
            cols = slice(128 * m, 128 * (m + 1))
            s_ref[rows, cols] = s_ref[rows, cols] + bt_ref[hh, base + 2 * m - qi] + pen
    return base


def attn_fwd(q_rot, q_pl, kk, vv, bt, comm=None):
    def body(qr_ref, qp_ref, kk_ref, vv_ref, bt_ref, o_ref, lse_ref, s_ref):
        j = pl.program_id(1)
        ws, start = _key_window(j)
        win = pl.ds(start, KEY_TILE)
        kw, kc = kk_ref[win, :], kk_ref[:CTX_LEN, :]
        vw, vc = vv_ref[win, :], vv_ref[:CTX_LEN, :]
        qr, qp = qr_ref[...], qp_ref[...]
        outs = []
        for hh in range(2):
            msk = _head_mask(hh)
            _attn_scores(j, ws, jnp.where(msk, qr, 0), jnp.where(msk, qp, 0), kw, kc, hh, bt_ref, s_ref)
            s = s_ref[...]
            mx = jnp.max(s, axis=-1, keepdims=True)
            pr = jnp.exp(s - mx)
            l = jnp.sum(pr, axis=-1, keepdims=True)
            prb = pr.astype(BF16)
            o = jnp.dot(prb[:, :KEY_TILE], vw, preferred_element_type=F32)
            o = o + jnp.dot(prb[:, KEY_TILE:], vc, preferred_element_type=F32)
            outs.append(o / l)
            lse_ref[hh] = mx + jnp.log(l)
        o_ref[...] = jnp.where(_head_mask(0), outs[0], outs[1])

    qspec = pl.BlockSpec((Q_TILE, 128), lambda hp, j: (j + 1, hp))
    kspec = pl.BlockSpec((ZLEN, 128), lambda hp, j: (0, hp))
    res, extra = _call(
        body, name="attn_fwd", grid=(NA_HEADS // 2, SEQ // Q_TILE),
        in_specs=[qspec, qspec, kspec, kspec, pl.BlockSpec((2, BT_LEN, GRID_W, 128), lambda hp, j: (hp, 0, 0, 0))],
        out_specs=[pl.BlockSpec((Q_TILE, 128), lambda hp, j: (j, hp)),
                   pl.BlockSpec((2, Q_TILE, 1), lambda hp, j: (hp, j, 0))],
        out_shape=[jax.ShapeDtypeStruct((SEQ, D_MODEL), F32), jax.ShapeDtypeStruct((NA_HEADS, SEQ, 1), F32)],
        scratch_shapes=[pltpu.VMEM((Q_TILE, KEY_TILE + CTX_LEN), F32)], sem=("parallel", "arbitrary"),
        args=(q_rot, q_pl, kk, vv, bt), comm=comm)
    return res[0], res[1], extra


def merge_fwd(y_rnn, y_na, p, z, g2, w_rnn, w_na, w_out):
    def body(yr_ref, yn_ref, mr_ref, mn_ref, x_ref, g2_ref, wr_ref, wn_ref, wo_ref, u_ref, v_ref, mg_ref, out_ref, x1_ref):
        u = jnp.dot(yr_ref[...], wr_ref[...], preferred_element_type=F32)
        v = jnp.dot(yn_ref[...].astype(BF16), wn_ref[...], preferred_element_type=F32)
        merged = (_sigmoid(mr_ref[...]) * u + _sigmoid(mn_ref[...]) * v).astype(BF16)
        out = jnp.dot(merged, wo_ref[...], preferred_element_type=F32)
        u_ref[...] = u
        v_ref[...] = v
        mg_ref[...] = merged
        out_ref[...] = out
        x1_ref[...] = x_ref[...] + g2_ref[...] * out

    row = pl.BlockSpec((ROW_TILE, D_MODEL), lambda i: (i, 0))
    lat = lambda cb: pl.BlockSpec((ROW_TILE, D_MODEL), lambda i: (i + 1, cb))
    wspec = _full((D_MODEL, D_MODEL))
    f32o = jax.ShapeDtypeStruct((SEQ, D_MODEL), F32)
    return pl.pallas_call(
        body, name="merge_fwd", grid=(SEQ // ROW_TILE,),
        in_specs=[row, row, lat(5), lat(6), lat(0), _full((1, D_MODEL)), wspec, wspec, wspec],
        out_specs=[row] * 5,
        out_shape=[f32o, f32o, jax.ShapeDtypeStruct((SEQ, D_MODEL), BF16), f32o, f32o],
        compiler_params=_params("parallel"),
    )(y_rnn, y_na, p, p, z, g2, w_rnn, w_na, w_out)


FF_TILE = 256
FF_TILES = D_FF // FF_TILE


def _ffn_conv(h, cw, cb):
    row = _row_ids(SEQ, FF_TILE)
    prev = jnp.where(row >= 1, pltpu.roll(h, 1, 0), 0.0)
    nxt = jnp.where(row < SEQ - 1, pltpu.roll(h, SEQ - 1, 0), 0.0)
    return prev * cw[0:1, :] + h * cw[1:2, :] + nxt * cw[2:3, :] + cb


def ffn_act(hpre, conv_w, conv_b):
    def body(ha_ref, hg_ref, wa_ref, wg_ref, ba_ref, bg_ref, o_ref):
        a = _ffn_conv(ha_ref[...], wa_ref[...], ba_ref[...])
        g = _ffn_conv(hg_ref[...], wg_ref[...], bg_ref[...])
        o_ref[...] = (a * _sigmoid(a) * g).astype(BF16)

    col = lambda rows, off: pl.BlockSpec((rows, FF_TILE), lambda j: (0, j + off))
    return pl.pallas_call(
        body, name="ffn_act", grid=(FF_TILES,),
        in_specs=[col(SEQ, 0), col(SEQ, FF_TILES), col(3, 0), col(3, FF_TILES), col(1, 0), col(1, FF_TILES)],
        out_specs=col(SEQ, 0),
        out_shape=jax.ShapeDtypeStruct((SEQ, D_FF), BF16),
        compiler_params=_params("parallel"),
    )(hpre, hpre, conv_w, conv_w, conv_b, conv_b)


def ffn_down_loss(act, w_down, x1, g5, target):
    def body(a_ref, w_ref, x1_ref, g5_ref, t_ref, f_ref, dy_ref, df_ref, ls_ref, dg_ref):
        i = pl.program_id(0)
        f = jnp.dot(a_ref[...], w_ref[...], preferred_element_type=F32)
        g5 = g5_ref[...]
        err = x1_ref[...] + g5 * f - t_ref[...]
        dy = err * (1.0 / D_MODEL)
        f_ref[...] = f
        dy_ref[...] = dy
        df_ref[...] = (dy * g5).astype(BF16)

        @pl.when(i == 0)
        def _():
            ls_ref[...] = jnp.zeros_like(ls_ref)
            dg_ref[...] = jnp.zeros_like(dg_ref)

        ls_ref[...] = ls_ref[...] + jnp.sum(err * err)
        dg_ref[...] = dg_ref[...] + jnp.sum(dy * f, axis=0, keepdims=True)

    row = pl.BlockSpec((ROW_TILE, D_MODEL), lambda i: (i, 0))
    f32o = jax.ShapeDtypeStruct((SEQ, D_MODEL), F32)
    return pl.pallas_call(
        body, name="ffn_down_loss", grid=(SEQ // ROW_TILE,),
        in_specs=[pl.BlockSpec((ROW_TILE, D_FF), lambda i: (i, 0)), _full((D_FF, D_MODEL)), row, _full((1, D_MODEL)), row],
        out_specs=[row, row, row, _full((8, 128)), _full((1, D_MODEL))],
        out_shape=[f32o, f32o, jax.ShapeDtypeStruct((SEQ, D_MODEL), BF16), jax.ShapeDtypeStruct((8, 128), F32),
                   jax.ShapeDtypeStruct((1, D_MODEL), F32)],
        compiler_params=_params("arbitrary"),
    )(act, w_down, x1, g5, target)


def ffn_down_bwd(df, w_down):
    def body(df_ref, w_ref, o_ref):
        o_ref[...] = _dot_nt(df_ref[...], w_ref[...])

    return pl.pallas_call(
        body, name="ffn_down_bwd", grid=(SEQ // ROW_TILE,),
        in_specs=[pl.BlockSpec((ROW_TILE, D_MODEL), lambda i: (i, 0)), _full((D_FF, D_MODEL))],
        out_specs=pl.BlockSpec((ROW_TILE, D_FF), lambda i: (i, 0)),
        out_shape=jax.ShapeDtypeStruct((SEQ, D_FF), F32),
        compiler_params=_params("parallel"),
    )(df, w_down)


def ffn_act_bwd(hpre, d_act, conv_w, conv_b):
    def half_bwd(dc, h, w, dh_ref, dw_ref, db_ref):
        row = _row_ids(SEQ, FF_TILE)
        h_prev = jnp.where(row >= 1, pltpu.roll(h, 1, 0), 0.0)
        h_next = jnp.where(row < SEQ - 1, pltpu.roll(h, SEQ - 1, 0), 0.0)
        dw_ref[0:1, :] = jnp.sum(dc * h_prev, axis=0, keepdims=True)
        dw_ref[1:2, :] = jnp.sum(dc * h, axis=0, keepdims=True)
        dw_ref[2:3, :] = jnp.sum(dc * h_next, axis=0, keepdims=True)
        db_ref[...] = jnp.sum(dc, axis=0, keepdims=True)
        dc_next = jnp.where(row < SEQ - 1, pltpu.roll(dc, SEQ - 1, 0), 0.0)
        dc_prev = jnp.where(row >= 1, pltpu.roll(dc, 1, 0), 0.0)
        dh_ref[...] = (dc_next * w[0:1, :] + dc * w[1:2, :] + dc_prev * w[2:3, :]).astype(BF16)

    def body(ha_ref, hg_ref, da_ref, wa_ref, wg_ref, ba_ref, bg_ref, dha_ref, dhg_ref, dwa_ref, dwg_ref, dba_ref, dbg_ref):
        ha, hg = ha_ref[...], hg_ref[...]
        a = _ffn_conv(ha, wa_ref[...], ba_ref[...])
        g = _ffn_conv(hg, wg_ref[...], bg_ref[...])
        sig = _sigmoid(a)
        dact = da_ref[...]
        half_bwd(dact * g * (sig * (1.0 + a * (1.0 - sig))), ha, wa_ref[...], dha_ref, dwa_ref, dba_ref)
        half_bwd(dact * a * sig, hg, wg_ref[...], dhg_ref, dwg_ref, dbg_ref)

    col = lambda rows, off: pl.BlockSpec((rows, FF_TILE), lambda j: (0, j + off))
    hshape = jax.ShapeDtypeStruct((SEQ, D_FF), BF16)
    wshape = jax.ShapeDtypeStruct((3, D_FF), F32)
    bshape = jax.ShapeDtypeStruct((1, D_FF), F32)
    return pl.pallas_call(
        body, name="ffn_act_bwd", grid=(FF_TILES,),
        in_specs=[col(SEQ, 0), col(SEQ, FF_TILES), col(SEQ, 0), col(3, 0), col(3, FF_TILES), col(1, 0), col(1, FF_TILES)],
        out_specs=[col(SEQ, 0), col(SEQ, 0), col(3, 0), col(3, 0), col(1, 0), col(1, 0)],
        out_shape=[hshape, hshape, wshape, wshape, bshape, bshape],
        compiler_params=_params("parallel"),
    )(hpre, hpre, d_act, conv_w, conv_w, conv_b, conv_b)


def _norm_mod_bwd(x, dxn, gain, scale):
    rstd = lax.rsqrt(jnp.mean(x * x, axis=-1, keepdims=True) + EPS)
    nrm = x * rstd
    dsh = jnp.sum(dxn, axis=0, keepdims=True)
    dsc = jnp.sum(dxn * nrm, axis=0, keepdims=True) * gain
    dgn = jnp.sum(dxn * nrm, axis=0, keepdims=True) * (1.0 + scale)
    dn = dxn * (gain * (1.0 + scale))
    dx = rstd * (dn - nrm * jnp.mean(dn * nrm, axis=-1, keepdims=True))
    return dx, dsh, dsc, dgn


def ffn_up_bwd(dha, dhg, w_up, x1, dy, gain, scale):
    def body(dha_ref, dhg_ref, w_ref, x_ref, dy_ref, g_ref, sc_ref, dx_ref, dsh_ref, dsc_ref, dgn_ref):
        i = pl.program_id(0)
        dxn = _dot_nt(dha_ref[...], w_ref[:, :D_FF]) + _dot_nt(dhg_ref[...], w_ref[:, D_FF:])
        dx, dsh, dsc, dgn = _norm_mod_bwd(x_ref[...], dxn, g_ref[...], sc_ref[...])
        dx_ref[...] = dy_ref[...] + dx

        @pl.when(i == 0)
        def _():
            dsh_ref[...] = dsh
            dsc_ref[...] = dsc
            dgn_ref[...] = dgn

        @pl.when(i > 0)
        def _():
            dsh_ref[...] = dsh_ref[...] + dsh
            dsc_ref[...] = dsc_ref[...] + dsc
            dgn_ref[...] = dgn_ref[...] + dgn

    row = pl.BlockSpec((ROW_TILE, D_MODEL), lambda i: (i, 0))
    vec = _full((1, D_MODEL))
    vshape = jax.ShapeDtypeStruct((1, D_MODEL), F32)
    return pl.pallas_call(
        body, name="ffn_up_bwd", grid=(SEQ // ROW_TILE,),
        in_specs=[pl.BlockSpec((ROW_TILE, D_FF), lambda i: (i, 0)), pl.BlockSpec((ROW_TILE, D_FF), lambda i: (i, 0)),
                  _full((D_MODEL, 2 * D_FF)), row, row, vec, vec],
        out_specs=[row, vec, vec, vec],
        out_shape=[jax.ShapeDtypeStruct((SEQ, D_MODEL), F32), vshape, vshape, vshape],
        compiler_params=_params("arbitrary"),
    )(dha, dhg, w_up, x1, dy, gain, scale)


def merge_bwd(dx1, out, g2, p, u, v, w_rnn, w_na, w_out):
    def body(dx_ref, out_ref, g2_ref, mr_ref, mn_ref, u_ref, v_ref, wr_ref, wn_ref, wo_ref,
             dout_ref, du_ref, dv_ref, dmr_ref, dmn_ref, dyr_ref, dyn_ref, dg2_ref):
        i = pl.program_id(0)

        @pl.when(i == 0)
        def _():
            dmr_ref[...] = jnp.zeros_like(dmr_ref)
            dmn_ref[...] = jnp.zeros_like(dmn_ref)
            dg2_ref[...] = jnp.zeros_like(dg2_ref)

        @pl.when(i > 0)
        def _():
            dx = dx_ref[...]
            dg2_ref[...] = dg2_ref[...] + jnp.sum(dx * out_ref[...], axis=0, keepdims=True)
            dout = (dx * g2_ref[...]).astype(BF16)
            dout_ref[...] = dout
            dm = _dot_nt(dout, wo_ref[...])
            sr = _sigmoid(mr_ref[...])
            sn = _sigmoid(mn_ref[...])
            du = (dm * sr).astype(BF16)
            dv = (dm * sn).astype(BF16)
            du_ref[...] = du
            dv_ref[...] = dv
            dmr_ref[...] = (dm * u_ref[...] * (sr * (1.0 - sr))).astype(BF16)
            dmn_ref[...] = (dm * v_ref[...] * (sn * (1.0 - sn))).astype(BF16)
            dyr_ref[...] = _dot_nt(du, wr_ref[...])
            dyn_ref[...] = _dot_nt(dv, wn_ref[...])

    lat = pl.BlockSpec((ROW_TILE, D_MODEL), lambda i: (jnp.maximum(i - 1, 0), 0))
    zrow = pl.BlockSpec((ROW_TILE, D_MODEL), lambda i: (i, 0))
    pcol = lambda cb: pl.BlockSpec((ROW_TILE, D_MODEL), lambda i: (i, cb))
    wspec = _full((D_MODEL, D_MODEL))
    tb = jax.ShapeDtypeStruct((SEQ, D_MODEL), BF16)
    zb = jax.ShapeDtypeStruct((ZLEN, D_MODEL), BF16)
    tf = jax.ShapeDtypeStruct((SEQ, D_MODEL), F32)
    return pl.pallas_call(
        body, name="merge_bwd", grid=(ZLEN // ROW_TILE,),
        in_specs=[lat, lat, _full((1, D_MODEL)), pcol(5), pcol(6), lat, lat, wspec, wspec, wspec],
        out_specs=[lat, lat, lat, zrow, zrow, lat, lat, _full((1, D_MODEL))],
        out_shape=[tb, tb, tb, zb, zb, tf, tf, jax.ShapeDtypeStruct((1, D_MODEL), F32)],
        compiler_params=_params("arbitrary"),
    )(dx1, out, g2, p, p, u, v, w_rnn, w_na, w_out)


def attn_bwd(q_rot, q_pl, kk, vv, bt, y_na, d_yna, lse, comm=None):
    def body(qr_ref, qp_ref, kk_ref, vv_ref, bt_ref, o_ref, do_ref, lse_ref,
             dqr_ref, dqp_ref, dk_ref, dv_ref, dbt_ref, s_ref):
        jj = pl.program_id(1)

        @pl.when(jj == 0)
        def _():
            dqr_ref[...] = jnp.zeros_like(dqr_ref)
            dqp_ref[...] = jnp.zeros_like(dqp_ref)
            dk_ref[...] = jnp.zeros_like(dk_ref)
            dv_ref[...] = jnp.zeros_like(dv_ref)
            dbt_ref[...] = jnp.zeros_like(dbt_ref)

        @pl.when(jj > 0)
        def _():
            j = jj - 1
            ws, start = _key_window(j)
            win = pl.ds(start, KEY_TILE)
            kw, kc = kk_ref[win, :], kk_ref[:CTX_LEN, :]
            vw, vc = vv_ref[win, :], vv_ref[:CTX_LEN, :]
            qr, qp = qr_ref[...], qp_ref[...]
            do = do_ref[...]
            do_o = do * o_ref[...]
            dq_r, dq_p = [], []
            for hh in range(2):
                msk = _head_mask(hh)
                q_r, q_p = jnp.where(msk, qr, 0), jnp.where(msk, qp, 0)
                base = _attn_scores(j, ws, q_r, q_p, kw, kc, hh, bt_ref, s_ref)
                pr = jnp.exp(s_ref[...] - lse_ref[hh])
                delta = jnp.sum(jnp.where(msk, do_o, 0.0), axis=-1, keepdims=True)
                dob = jnp.where(msk, do, 0.0).astype(BF16)
                ds_lat = pr[:, :KEY_TILE] * (_dot_nt(dob, vw) - delta)
                ds_ctx = pr[:, KEY_TILE:] * (_dot_nt(dob, vc) - delta)
                for qi in range(Q_ROWS):
                    for m in range(KEY_ROWS // 2):
                        idx = base + 2 * m - qi
                        dbt_ref[hh, idx] = dbt_ref[hh, idx] + ds_lat[qi * GRID_W:(qi + 1) * GRID_W, 128 * m:128 * (m + 1)]
                dsb_lat = ds_lat.astype(BF16)
                dsb_ctx = ds_ctx.astype(BF16)
                prb = pr.astype(BF16)
                dq_r.append(jnp.dot(dsb_lat, kw, preferred_element_type=F32))
                dq_p.append(jnp.dot(dsb_ctx, kc, preferred_element_type=F32))
                dk_ref[win, :] = dk_ref[win, :] + _dot_tn(dsb_lat, q_r)
                dk_ref[:CTX_LEN, :] = dk_ref[:CTX_LEN, :] + _dot_tn(dsb_ctx, q_p)
                dv_ref[win, :] = dv_ref[win, :] + _dot_tn(prb[:, :KEY_TILE], dob)
                dv_ref[:CTX_LEN, :] = dv_ref[:CTX_LEN, :] + _dot_tn(prb[:, KEY_TILE:], dob)
            dqr_ref[...] = jnp.where(_head_mask(0), dq_r[0], dq_r[1])
            dqp_ref[...] = jnp.where(_head_mask(0), dq_p[0], dq_p[1])

    lat = lambda jj: jnp.maximum(jj - 1, 0)
    qspec = pl.BlockSpec((Q_TILE, 128), lambda hp, jj: (lat(jj) + 1, hp))
    kspec = pl.BlockSpec((ZLEN, 128), lambda hp, jj: (0, hp))
    btspec = pl.BlockSpec((2, BT_LEN, GRID_W, 128), lambda hp, jj: (hp, 0, 0, 0))
    ospec = pl.BlockSpec((Q_TILE, 128), lambda hp, jj: (lat(jj), hp))
    dqspec = pl.BlockSpec((Q_TILE, 128), lambda hp, jj: (jj, hp))
    zshape = jax.ShapeDtypeStruct((ZLEN, D_MODEL), F32)
    res, extra = _call(
        body, name="attn_bwd", grid=(NA_HEADS // 2, ZLEN // Q_TILE),
        in_specs=[qspec, qspec, kspec, kspec, btspec, ospec, ospec,
                  pl.BlockSpec((2, Q_TILE, 1), lambda hp, jj: (hp, lat(jj), 0))],
        out_specs=[dqspec, dqspec, kspec, kspec, btspec],
        out_shape=[zshape, zshape, zshape, zshape, jax.ShapeDtypeStruct((NA_HEADS, BT_LEN, GRID_W, 128), F32)],
        scratch_shapes=[pltpu.VMEM((Q_TILE, KEY_TILE + CTX_LEN), F32)], sem=("parallel", "arbitrary"),
        args=(q_rot, q_pl, kk, vv, bt, y_na, d_yna, lse), comm=comm)
    return (*res, extra)


def qkv_bwd(dq_rot, dq_pl, dk, dv, p, qg2, kg2, cos, sin, ones, comm=None):
    scale = HEAD_DIM ** -0.5
    n_hp, n_i = NA_HEADS // 2, ZLEN // PREP_TILE

    def norm_rope_bwd(d_rot, d_extra, x, gain, cos_t, sin_t, ones_m, dx_ref, acc_ref):
        xh, rstd = _head_rms(x, ones_m, 1.0)
        dn = d_rot * cos_t + _rope_partner(d_rot * sin_t)
        if d_extra is not None:
            dn = (dn + d_extra) * scale
        acc_ref[...] = acc_ref[...] + jnp.sum(dn * xh, axis=0, keepdims=True)
        dxh = dn * gain
        seg = jnp.dot(dxh * xh, ones_m, preferred_element_type=F32, precision=lax.Precision.HIGHEST) * (1.0 / HEAD_DIM)
        dx_ref[...] = (rstd * (dxh - xh * seg)).astype(BF16)

    def body(dqr_ref, dqp_ref, dk_ref, dv_ref, xq_ref, xk_ref, qg_ref, kg_ref, cos_ref, sin_ref, ones_ref,
             dxq_ref, dxk_ref, dxv_ref, dgq_ref, dgk_ref, accq_ref, acck_ref):
        hp, i = pl.program_id(0), pl.program_id(1)

        @pl.when((hp == 0) & (i == 0))
        def _():
            accq_ref[...] = jnp.zeros_like(accq_ref)
            acck_ref[...] = jnp.zeros_like(acck_ref)

        ones_m = ones_ref[...]
        cos_t, sin_t = cos_ref[...], sin_ref[...]
        norm_rope_bwd(dqr_ref[...], dqp_ref[...], xq_ref[...], qg_ref[...], cos_t, sin_t, ones_m, dxq_ref, accq_ref)
        norm_rope_bwd(dk_ref[...], None, xk_ref[...], kg_ref[...], cos_t, sin_t, ones_m, dxk_ref, acck_ref)
        dxv_ref[...] = dv_ref[...].astype(BF16)

        @pl.when((hp == n_hp - 1) & (i == n_i - 1))
        def _():
            dgq_ref[...] = accq_ref[:, :HEAD_DIM] + accq_ref[:, HEAD_DIM:]
            dgk_ref[...] = acck_ref[:, :HEAD_DIM] + acck_ref[:, HEAD_DIM:]

    col = lambda base: pl.BlockSpec((PREP_TILE, 128), lambda hp, i: (i, base + hp))
    small = pl.BlockSpec((1, 128), lambda hp, i: (0, 0))
    tab = pl.BlockSpec((PREP_TILE, 128), lambda hp, i: (i, 0))
    zb = jax.ShapeDtypeStruct((ZLEN, D_MODEL), BF16)
    gshape = jax.ShapeDtypeStruct((1, HEAD_DIM), F32)
    res, extra = _call(
        body, name="qkv_bwd", grid=(n_hp, n_i),
        in_specs=[col(0)] * 4 + [col(32), col(8), small, small, tab, tab, _full((128, 128))],
        out_specs=[col(0)] * 3 + [_full((1, HEAD_DIM))] * 2,
        out_shape=[zb, zb, zb, gshape, gshape],
        scratch_shapes=[pltpu.VMEM((1, 128), F32)] * 2, sem=("arbitrary", "arbitrary"),
        args=(dq_rot, dq_pl, dk, dv, p, p, qg2, kg2, cos, sin, ones), comm=comm)
    return (*res, extra)


def rpb_grad(dbt):
    e, _ = _bias_expand()
    n_dr = 2 * NA_ROWS - 1
    ea = np.zeros((31, GRID_W, 128), np.float32)
    ea[:, :, :GRID_W] = e
    eb = np.zeros((31, GRID_W, 128), np.float32)
    eb[:, :, GRID_W:] = e
    eat = jnp.asarray(ea.reshape(31, GRID_W * 128).T.copy())
    ebt = jnp.asarray(eb.reshape(31, GRID_W * 128).T.copy())
    sel_at = np.zeros((n_dr, BT_LEN), np.float32)
    sel_bt = np.zeros((n_dr, BT_LEN), np.float32)
    for r in range(BT_LEN):
        dr = r - BT_PAD
        if 0 <= dr < n_dr:
            sel_at[dr, r] = 1.0
        if 0 <= dr + 1 < n_dr:
            sel_bt[dr + 1, r] = 1.0
    hi = lax.Precision.HIGHEST

    tk = 2048
    wide = GRID_W * 128
    rows = NA_HEADS * BT_LEN
    n_k = wide // tk

    def body(d_ref, sa_ref, sb_ref, ea_ref, eb_ref, o_ref, a_s, b_s):
        k = pl.program_id(0)
        dm = d_ref[...]
        a = jnp.dot(dm, ea_ref[...], preferred_element_type=F32, precision=hi)
        b = jnp.dot(dm, eb_ref[...], preferred_element_type=F32, precision=hi)

        @pl.when(k == 0)
        def _():
            a_s[...] = a
            b_s[...] = b

        @pl.when(k > 0)
        def _():
            a_s[...] = a_s[...] + a
            b_s[...] = b_s[...] + b

        @pl.when(k == n_k - 1)
        def _():
            for h in range(NA_HEADS):
                sl = slice(h * BT_LEN, (h + 1) * BT_LEN)
                o_ref[h] = (jnp.dot(sa_ref[...], a_s[sl, :], preferred_element_type=F32, precision=hi)
                            + jnp.dot(sb_ref[...], b_s[sl, :], preferred_element_type=F32, precision=hi))

    return pl.pallas_call(
        body, name="rpb_grad", grid=(n_k,),
        in_specs=[pl.BlockSpec((rows, tk), lambda k: (0, k)), _full((n_dr, BT_LEN)), _full((n_dr, BT_LEN)),
                  pl.BlockSpec((tk, 31), lambda k: (k, 0)), pl.BlockSpec((tk, 31), lambda k: (k, 0))],
        out_specs=_full((NA_HEADS, n_dr, 31)),
        out_shape=jax.ShapeDtypeStruct((NA_HEADS, n_dr, 31), F32),
        scratch_shapes=[pltpu.VMEM((rows, 31), F32), pltpu.VMEM((rows, 31), F32)],
        compiler_params=_params("arbitrary"),
    )(dbt.reshape(rows, wide), jnp.asarray(sel_at), jnp.asarray(sel_bt), eat, ebt)


def lru_bwd(p, d_yrnn, conv_w, conv_b, wa, ba, wx, bx, lam, comm=None):
    blk, wspec = _lru_in_specs()

    def body(xr_ref, gx_ref, dy_ref, cw_ref, cb_ref, wa_ref, ba_ref, wx_ref, bx_ref, lam_ref,
             dxr_ref, dgx_ref, dcw_ref, dcb_ref, dwa_ref, dba_ref, dwx_ref, dbx_ref, dlam_ref,
             a_s, b_s, h_s, l_s, hsum_s, dxc_s, dh_s):
        xr = xr_ref[...]
        cw = cw_ref[...]
        xc = _lru_conv(xr, cw, cb_ref[...])
        xcb = xc.astype(BF16)
        g, dg = _gelu_parts(gx_ref[CTX_LEN:, :])
        dy = dy_ref[...]
        dh_s[:CTX_LEN, :] = jnp.zeros((CTX_LEN, LRU_BLOCK_W), F32)
        dh_s[CTX_LEN:, :] = dy * g
        row = _row_ids(ZLEN, LRU_BLOCK_W)
        zero = jnp.zeros((1, LRU_BLOCK_W), F32)
        for d in (0, 1):
            wab = wa_ref[d, 0].astype(BF16)
            wxb = wx_ref[d, 0].astype(BF16)
            lam_d = lam_ref[d:d + 1, :]
            r, gi, sp, a, sq, b = _lru_gates(xc, xcb, wab, ba_ref[d:d + 1, :], wxb, bx_ref[d:d + 1, :], lam_d)
            a_s[...] = a
            b_s[...] = b
            _lru_scan_dir(d, a_s, b_s, h_s)
            h = h_s[...]
            if d == 0:
                hsum_s[...] = h
                h_prev = jnp.where(row >= 1, pltpu.roll(h, 1, 0), 0.0)
                a_s[...] = pltpu.roll(a, ZLEN - 1, 0)
                _scan_down(a_s, dh_s, l_s, 0, N_CHUNK, zero)
            else:
                hsum_s[...] = hsum_s[...] + h
                h_prev = jnp.where(row == CTX_LEN - 1, 0.0, pltpu.roll(h, ZLEN - 1, 0))
                a_s[...] = pltpu.roll(a, 1, 0)
                c = _scan_up(a_s, dh_s, l_s, CTX_CHUNKS, N_CHUNK, zero)
                _scan_up(a_s, dh_s, l_s, 0, CTX_CHUNKS, c)
            db = l_s[...]
            da = db * h_prev
            dsq = db * gi * xc
            dgi = db * sq * xc
            dxc_d = db * sq * gi
            dla = da * a - dsq * (a * a) / sq
            dr = dla * ((-LRU_C) * sp)
            dsp = jnp.sum(dla * ((-LRU_C) * r), axis=0, keepdims=True)
            dlam_ref[d:d + 1, :] = -dsp * _sigmoid(-lam_d)
            dzr = dr * r * (1.0 - r)
            dzi = dgi * gi * (1.0 - gi)
            dba_ref[d:d + 1, :] = jnp.sum(dzr, axis=0, keepdims=True)
            dbx_ref[d:d + 1, :] = jnp.sum(dzi, axis=0, keepdims=True)
            dzrb = dzr.astype(BF16)
            dzib = dzi.astype(BF16)
            dwa_ref[d, 0] = _dot_tn(xcb, dzrb)
            dwx_ref[d, 0] = _dot_tn(xcb, dzib)
            dxc_d = dxc_d + _dot_nt(dzrb, wab) + _dot_nt(dzib, wxb)
            if d == 0:
                dxc_s[...] = dxc_d
            else:
                dxc_s[...] = dxc_s[...] + dxc_d
        dxc = dxc_s[...]
        dxr_ref[...] = _lru_conv_t(dxc, cw).astype(BF16)
        dcb_ref[...] = jnp.sum(dxc, axis=0, keepdims=True)
        segpos = jnp.where(row < CTX_LEN, row, row - CTX_LEN)
        seglen = jnp.where(row < CTX_LEN, CTX_LEN, SEQ)
        for k in range(4):
            off = k - 2
            if off == 0:
                sh = xr
            else:
                ok = (segpos + off >= 0) & (segpos + off < seglen)
                sh = jnp.where(ok, pltpu.roll(xr, (-off) % ZLEN, 0), 0.0)
            dcw_ref[k:k + 1, :] = jnp.sum(dxc * sh, axis=0, keepdims=True)
        dgx_ref[:CTX_LEN, :] = jnp.zeros((CTX_LEN, LRU_BLOCK_W), BF16)
        dgx_ref[CTX_LEN:, :] = (dy * hsum_s[CTX_LEN:, :] * dg).astype(BF16)

    zs = pltpu.VMEM((ZLEN, LRU_BLOCK_W), F32)
    zb = jax.ShapeDtypeStruct((ZLEN, D_MODEL), BF16)
    v2 = jax.ShapeDtypeStruct((2, D_MODEL), F32)
    w4 = jax.ShapeDtypeStruct((2, LRU_BLOCKS, LRU_BLOCK_W, LRU_BLOCK_W), F32)
    res, extra = _call(
        body, name="lru_bwd", grid=(LRU_BLOCKS,),
        in_specs=[blk(ZLEN), pl.BlockSpec((ZLEN, LRU_BLOCK_W), lambda b: (0, 24 + b)), blk(SEQ), blk(4), blk(1),
                  wspec, blk(2), wspec, blk(2), blk(2)],
        out_specs=[blk(ZLEN), blk(ZLEN), blk(4), blk(1), wspec, blk(2), wspec, blk(2), blk(2)],
        out_shape=[zb, zb, jax.ShapeDtypeStruct((4, D_MODEL), F32), jax.ShapeDtypeStruct((1, D_MODEL), F32),
                   w4, v2, w4, v2, v2],
        scratch_shapes=[zs] * 7, sem=("arbitrary",),
        args=(p, p, d_yrnn, conv_w, conv_b, wa, ba, wx, bx, lam), comm=comm)
    return (*res, extra)


def in_proj_bwd(dgs, w_in, z, dx1, gain, scale, comm=None):
    def body(*refs):
        dg_refs = refs[:7]
        w_ref, z_ref, dx1_ref, g_ref, sc_ref, gx_ref, dsh_ref, dsc_ref, dgn_ref = refs[7:]
        i = pl.program_id(0)
        dxn = _dot_nt(dg_refs[0][...], w_ref[:, 0:D_MODEL])
        for g in range(1, 7):
            dxn = dxn + _dot_nt(dg_refs[g][...], w_ref[:, g * D_MODEL:(g + 1) * D_MODEL])
        dx, dsh, dsc, dgn = _norm_mod_bwd(z_ref[...], dxn, g_ref[...], sc_ref[0])

        @pl.when(i <= 1)
        def _():
            dsh_ref[0] = dsh
            dsc_ref[0] = dsc

        @pl.when(i > 1)
        def _():
            dsh_ref[0] = dsh_ref[0] + dsh
            dsc_ref[0] = dsc_ref[0] + dsc

        @pl.when(i == 0)
        def _():
            dgn_ref[...] = dgn

        @pl.when(i > 0)
        def _():
            dgn_ref[...] = dgn_ref[...] + dgn
            gx_ref[...] = dx1_ref[...] + dx

    zrow = pl.BlockSpec((ROW_TILE, D_MODEL), lambda i: (i, 0))
    lat = pl.BlockSpec((ROW_TILE, D_MODEL), lambda i: (jnp.maximum(i - 1, 0), 0))
    mod = pl.BlockSpec((1, 1, D_MODEL), lambda i: (jnp.minimum(i, 1), 0, 0))
    mshape = jax.ShapeDtypeStruct((2, 1, D_MODEL), F32)
    res, extra = _call(
        body, name="in_proj_bwd", grid=(ZLEN // ROW_TILE,),
        in_specs=[zrow] * 7 + [_full((D_MODEL, IN_COLS)), zrow, lat, _full((1, D_MODEL)), mod],
        out_specs=[lat, mod, mod, _full((1, D_MODEL))],
        out_shape=[jax.ShapeDtypeStruct((SEQ, D_MODEL), F32), mshape, mshape, jax.ShapeDtypeStruct((1, D_MODEL), F32)],
        sem=("arbitrary",), args=(*dgs, w_in, z, dx1, gain, scale), comm=comm)
    return (*res, extra)


def matmul_tn(a, b, name, tm, tn, prev=None, col_block=0, total_cols=None):
    k, m = a.shape
    n = b.shape[1]
    total_cols = n if total_cols is None else total_cols
    assert m % tm == 0 and n % tn == 0
    off = col_block * (n // tn)

    def body(a_ref, b_ref, *rest):
        rest[-1][...] = _dot_tn(a_ref[...].astype(BF16), b_ref[...]).astype(BF16)

    in_specs = [pl.BlockSpec((k, tm), lambda i, j: (0, i)), pl.BlockSpec((k, tn), lambda i, j: (0, j))]
    args = [a, b]
    aliases = {}
    if prev is not None:
        in_specs.append(pl.BlockSpec(memory_space=pl.ANY))
        args.append(prev)
        aliases = {2: 0}
    return pl.pallas_call(
        body, name=name, grid=(m // tm, n // tn), in_specs=in_specs,
        out_specs=pl.BlockSpec((tm, tn), lambda i, j: (i, j + off)),
        out_shape=jax.ShapeDtypeStruct((m, total_cols), BF16),
        input_output_aliases=aliases,
        compiler_params=_params("parallel", "parallel"),
    )(*args)


def local_step(z, target, modx, modc, norm_mix_g, norm_ffn_g, w_in, conv_w, conv_b, wa, ba, wx, bx, lam, qg, kg, rpb,
               w_rnn, w_na, w_out, w_up, fconv_w, fconv_b, w_down, c_idx=None):
    dist = c_idx is not None
    d = D_MODEL
    mx = [modx[:, k * d:(k + 1) * d] for k in range(N_MOD)]
    shift = jnp.stack([modc[:, 0:d], mx[0]])
    scale = jnp.stack([modc[:, d:2 * d], mx[1]])
    cos, sin = _rope_tables()
    ones = _head_ones()
    qg2 = jnp.tile(qg, (1, 2))
    kg2 = jnp.tile(kg, (1, 2))

    xn = norm_mod(z, norm_mix_g, shift, scale, "norm_mix")
    p, got = matmul_wide(xn, w_in, "in_proj", 3 * ROW_TILE, 1792,
                         comm=gather_weights_comm([w_rnn, w_na, w_out], [1, 2, 3]) if dist else None)
    if dist:
        w_rnn, w_na, w_out = got
    y_rnn, got = lru_fwd(p, conv_w, conv_b, wa, ba, wx, bx, lam,
                         comm=gather_weights_comm([w_down], [5]) if dist else None)
    if dist:
        w_down = got[0]
    q_rot, q_pl, kk, vv = qkv_prep(p, qg2, kg2, cos, sin, ones)
    bt = bias_table(rpb)
    y_na, lse, got = attn_fwd(q_rot, q_pl, kk, vv, bt, comm=gather_weights_comm([w_up], [4]) if dist else None)
    if dist:
        w_up = got[0]
    u, v, merged, out, x1 = merge_fwd(y_rnn, y_na, p, z, mx[2], w_rnn, w_na, w_out)
    xn2 = norm_mod(x1, norm_ffn_g, mx[3][None], mx[4][None], "norm_ffn")
    hpre, _ = matmul_wide(xn2, w_up, "ffn_up", 2 * ROW_TILE, 1408)
    act = ffn_act(hpre, fconv_w, fconv_b)
    f, dy, df, loss_sq, dg5 = ffn_down_loss(act, w_down, x1, mx[5], target)

    partials, pieces = {}, {}

    def chip_partials(which, grads, tag):
        views = [_grad_view(g, BIG[w][1], BIG[w][2]) for w, g in zip(which, grads)]
        recv = exchange_halves(views, "grad_exchange_" + tag)
        for w, gv, r in zip(which, views, recv):
            partials[w] = add_halves(gv, r, c_idx, "add_halves_" + BIG[w][0])
        return scatter_pieces_comm([partials[w] for w in which], which)

    d_act = ffn_down_bwd(df, w_down)
    dha, dhg, d_fcw_a, d_fcw_g, d_fcb_a, d_fcb_g = ffn_act_bwd(hpre, d_act, fconv_w, fconv_b)
    d_fcw = jnp.concatenate([d_fcw_a, d_fcw_g], axis=1)
    d_fcb = jnp.concatenate([d_fcb_a, d_fcb_g], axis=1)
    dx1, d_s3, d_s4, d_gffn = ffn_up_bwd(dha, dhg, w_up, x1, dy, norm_ffn_g, mx[4])
    g_w_down = matmul_tn(act, df, "gw_down", 256, D_MODEL)
    g_w_up = matmul_tn(xn2, dha, "gw_up_a", 512, 1408, total_cols=2 * D_FF)
    g_w_up = matmul_tn(xn2, dhg, "gw_up_g", 512, 1408, prev=g_w_up, col_block=1, total_cols=2 * D_FF)
    dout, du, dv, dmr, dmn, dyr, dyn, dg2 = merge_bwd(dx1, out, mx[2], p, u, v, w_rnn, w_na, w_out)
    g_w_out = matmul_tn(merged, dout, "gw_out", 1024, 512)
    g_w_rnn = matmul_tn(y_rnn, du, "gw_rnn", 1024, 512)
    g_w_na = matmul_tn(y_na, dv, "gw_na", 1024, 512)
    *lru_grads, got = lru_bwd(p, dyr, conv_w, conv_b, wa, ba, wx, bx, lam,
                              comm=chip_partials([4, 5], [g_w_up, g_w_down], "ffn") if dist else None)
    dxr, dgx, d_cw, d_cb, d_wa, d_ba, d_wx, d_bx, d_lam = lru_grads
    if dist:
        pieces[4], pieces[5] = got
    lru_w_all = {}
    dqr, dqp, dk, dvh, dbt, got = attn_bwd(
        q_rot, q_pl, kk, vv, bt, y_na, dyn, lse,
        comm=join_comms(chip_partials([1, 2, 3], [g_w_rnn, g_w_na, g_w_out], "mix"),
                        all_gather_comm(d_wa.reshape(-1, LRU_BLOCK_W))) if dist else None)
    if dist:
        pieces[1], pieces[2], pieces[3], lru_w_all["lru_wa"] = got
    dq_cols, dk_cols, dv_cols, d_qg, d_kg, got = qkv_bwd(
        dqr, dqp, dk, dvh, p, qg2, kg2, cos, sin, ones,
        comm=all_gather_comm(d_wx.reshape(-1, LRU_BLOCK_W)) if dist else None)
    if dist:
        lru_w_all["lru_wx"] = got[0]
    d_rpb = rpb_grad(dbt)
    dgs = [dxr, dk_cols, dv_cols, dgx, dq_cols, dmr, dmn]
    grad_x, dsh, dsc, d_gmix, _ = in_proj_bwd(dgs, w_in, z, dx1, norm_mix_g, scale)
    g_w_in = None
    for g in range(7):
        g_w_in = matmul_tn(xn, dgs[g], "gw_in_%d" % g, 1024, 512, prev=g_w_in, col_block=g, total_cols=IN_COLS)
    if dist:
        pieces[0] = run_comm(chip_partials([0], [g_w_in], "w_in"), "grad_scatter_w_in")[0]

    d_modx = jnp.concatenate([dsh[1], dsc[1], dg2, d_s3, d_s4, dg5], axis=1)
    d_modc = jnp.concatenate([dsh[0], dsc[0]], axis=1)
    return dict(loss_sq=loss_sq, grad_x=grad_x, d_modx=d_modx, d_modc=d_modc, norm_mix_g=d_gmix, norm_ffn_g=d_gffn,
                w_in=g_w_in, lru_conv_w=d_cw, lru_conv_b=d_cb, lru_wa=d_wa, lru_ba=d_ba, lru_wx=d_wx, lru_bx=d_bx,
                lru_lambda=d_lam, q_norm_g=d_qg, k_norm_g=d_kg, na_rpb=d_rpb, w_rnn_out=g_w_rnn, w_na_out=g_w_na,
                w_out=g_w_out, w_up=g_w_up, ffn_conv_w=d_fcw, ffn_conv_b=d_fcb, w_down=g_w_down,
                partials=partials, pieces=pieces, lru_w_all=lru_w_all)


def _mesh_pos():
    return lax.axis_index("x"), lax.axis_index("y"), lax.axis_index("c")


def _other_chips(x, y):
    return [(1 - x, y), (x, 1 - y), (1 - x, 1 - y)]


def all_gather8(xs, name, with_sum=False):
    m, n = xs.shape
    assert m % 8 == 0

    def body(x_ref, out_ref, *rest):
        if with_sum:
            sum_ref, send_sems, recv_sems, local_sem = rest
        else:
            send_sems, recv_sems, local_sem = rest
        x, y, c = _mesh_pos()
        me, sibling = (x, y, c), (x, y, 1 - c)
        chips = _other_chips(x, y)

        def rows(px, py, pc):
            return out_ref.at[pl.ds((4 * px + 2 * py + pc) * m, m), :]

        def copy(k, block, to, src=None):
            return pltpu.make_async_remote_copy(
                src_ref=rows(*block) if src is None else src, dst_ref=rows(*block),
                send_sem=send_sems.at[k], recv_sem=recv_sems.at[k], device_id=to, device_id_type=MESH_T)

        mine = pltpu.make_async_copy(x_ref, rows(*me), local_sem)
        mine.start()
        first = [copy(0, me, sibling, src=x_ref)]
        first += [copy(1 + j, me, (*chip, c), src=x_ref) for j, chip in enumerate(chips)]
        for cp in first:
            cp.start()
        passed = [copy(4 + j, (*chip, c), sibling) for j, chip in enumerate(chips)]
        for j, chip in enumerate(chips):
            copy(1 + j, (*chip, c), me).wait_recv()
            passed[j].start()
        copy(0, sibling, me).wait_recv()
        for j, chip in enumerate(chips):
            copy(4 + j, (*chip, 1 - c), me).wait_recv()
        for cp in first + passed:
            cp.wait_send()
        mine.wait()
        if with_sum:
            acc = out_ref[0:m, :]
            for k in range(1, N_DEV):
                acc = acc + out_ref[k * m:(k + 1) * m, :]
            sum_ref[...] = acc

    vm = pl.BlockSpec(memory_space=pltpu.VMEM)
    out_shape = [jax.ShapeDtypeStruct((N_DEV * m, n), F32)]
    if with_sum:
        out_shape.append(jax.ShapeDtypeStruct((m, n), F32))
    res = pl.pallas_call(
        body, name=name, in_specs=[vm], out_specs=[vm] * len(out_shape), out_shape=out_shape,
        scratch_shapes=[pltpu.SemaphoreType.DMA((7,)), pltpu.SemaphoreType.DMA((7,)), pltpu.SemaphoreType.DMA],
        compiler_params=pltpu.CompilerParams(vmem_limit_bytes=VMEM_LIMIT_V7X),
    )(xs)
    return res if with_sum else res[0]


BIG = (("w_in", (D_MODEL, IN_COLS), 1), ("w_rnn_out", (D_MODEL, D_MODEL), 0), ("w_na_out", (D_MODEL, D_MODEL), 0),
       ("w_out", (D_MODEL, D_MODEL), 0), ("w_up", (D_MODEL, 2 * D_FF), 1), ("w_down", (D_FF, D_MODEL), 0))


def _shard_shape(full, axis):
    r, c = full
    return (r // N_SHARD, c) if axis == 0 else (r, c // N_SHARD)


def _slot(ref, full, axis, s, h):
    r, c = full
    if axis == 0:
        rs = r // N_SHARD
        return ref.at[pl.ds(s * rs + h * (rs // 2), rs // 2), :]
    cs = c // N_SHARD
    return ref.at[pl.ds(h * (r // 2), r // 2), pl.ds(s * cs, cs)]


def cast_into_full(x, full, axis, idx, name):
    r, c = x.shape
    tr = next(t for t in (512, 352, 256, 128) if r % t == 0)
    nb = r // tr

    def body(idx_ref, x_ref, o_ref):
        o_ref[...] = x_ref[...].astype(BF16)

    if axis == 0:
        out_spec = pl.BlockSpec((tr, c), lambda i, idx_ref: (idx_ref[0] * nb + i, 0))
    else:
        out_spec = pl.BlockSpec((tr, c), lambda i, idx_ref: (i, idx_ref[0]))
    return pl.pallas_call(
        body, name=name,
        grid_spec=pltpu.PrefetchScalarGridSpec(
            num_scalar_prefetch=1, grid=(nb,), in_specs=[pl.BlockSpec((tr, c), lambda i, idx_ref: (i, 0))],
            out_specs=out_spec),
        out_shape=jax.ShapeDtypeStruct(full, BF16),
        compiler_params=_params("parallel"),
    )(idx, x)


def run_comm(comm, name):
    k_in, k_out = len(comm.inputs), len(comm.out_shapes)

    def body(*refs):
        start, mid, end = comm.emit(refs[:k_in], refs[k_in:k_in + k_out], refs[k_in + k_out:])
        start()
        mid()
        end()

    hbm = pl.BlockSpec(memory_space=pl.ANY)
    return pl.pallas_call(
        body, name=name, in_specs=[hbm] * k_in, out_specs=[hbm] * k_out, out_shape=list(comm.out_shapes),
        input_output_aliases=dict(comm.aliases), scratch_shapes=list(comm.scratch),
        compiler_params=pltpu.CompilerParams(vmem_limit_bytes=VMEM_LIMIT_V7X),
    )(*comm.inputs)


def gather_weights_comm(fulls, which):
    nw = len(which)
    specs = [BIG[w] for w in which]

    def emit(_, outs, sems):
        send1, recv1, send2, recv2 = sems
        x, y, c = _mesh_pos()
        sibling = (x, y, 1 - c)
        chips = _other_chips(x, y)
        s_me = 2 * x + y
        shards = [2 * chip[0] + chip[1] for chip in chips]

        def ici(w, j, shard):
            _, full, axis = specs[w]
            dst = _slot(outs[w], full, axis, shard, c)
            return pltpu.make_async_remote_copy(
                src_ref=dst, dst_ref=dst, send_sem=send1.at[3 * w + j],
                recv_sem=recv1.at[3 * w + j], device_id=(*chips[j], c), device_id_type=MESH_T)

        def d2d(w, j, shard, half):
            _, full, axis = specs[w]
            dst = _slot(outs[w], full, axis, shard, half)
            return pltpu.make_async_remote_copy(
                src_ref=dst, dst_ref=dst, send_sem=send2.at[3 * w + j], recv_sem=recv2.at[3 * w + j],
                device_id=sibling, device_id_type=MESH_T)

        pairs = [(w, j) for w in range(nw) for j in range(3)]

        def start():
            for w, j in pairs:
                ici(w, j, s_me).start()

        def mid():
            for w, j in pairs:
                ici(w, j, shards[j]).wait_recv()
                d2d(w, j, shards[j], c).start()

        def end():
            for w, j in pairs:
                d2d(w, j, shards[j], 1 - c).wait_recv()
            for w, j in pairs:
                ici(w, j, s_me).wait_send()
                d2d(w, j, shards[j], c).wait_send()

        return start, mid, end

    return Comm(list(fulls), [jax.ShapeDtypeStruct(full, BF16) for _, full, _ in specs], {i: i for i in range(nw)},
                [pltpu.SemaphoreType.DMA((3 * nw,))] * 4, emit)


def join_comms(a, b):
    ai, ao, asc = len(a.inputs), len(a.out_shapes), len(a.scratch)

    def emit(ins, outs, sems):
        fa = a.emit(ins[:ai], outs[:ao], sems[:asc])
        fb = b.emit(ins[ai:], outs[ao:], sems[asc:])

        def both(k):
            def run():
                fa[k]()
                fb[k]()
            return run

        return both(0), both(1), both(2)

    aliases = dict(a.aliases)
    aliases.update({ai + i: ao + o for i, o in b.aliases.items()})
    return Comm(a.inputs + b.inputs, a.out_shapes + b.out_shapes, aliases, a.scratch + b.scratch, emit)


def all_gather_comm(x):
    def emit(srcs, outs, sems):
        send_sems, recv_sems, local_sem = sems
        x_ref, out_ref = srcs[0], outs[0]
        x, y, c = _mesh_pos()
        me, sibling = (x, y, c), (x, y, 1 - c)
        chips = _other_chips(x, y)

        def blk(px, py, pc):
            return out_ref.at[4 * px + 2 * py + pc]

        def copy(k, block, to, src=None):
            return pltpu.make_async_remote_copy(
                src_ref=blk(*block) if src is None else src, dst_ref=blk(*block),
                send_sem=send_sems.at[k], recv_sem=recv_sems.at[k], device_id=to, device_id_type=MESH_T)

        def mine():
            return pltpu.make_async_copy(x_ref, blk(*me), local_sem)

        def start():
            mine().start()
            copy(0, me, sibling, src=x_ref).start()
            for j, chip in enumerate(chips):
                copy(1 + j, me, (*chip, c), src=x_ref).start()

        def mid():
            for j, chip in enumerate(chips):
                copy(1 + j, (*chip, c), me).wait_recv()
                copy(4 + j, (*chip, c), sibling).start()

        def end():
            copy(0, sibling, me).wait_recv()
            for j, chip in enumerate(chips):
                copy(4 + j, (*chip, 1 - c), me).wait_recv()
            copy(0, me, sibling, src=x_ref).wait_send()
            for j, chip in enumerate(chips):
                copy(1 + j, me, (*chip, c), src=x_ref).wait_send()
                copy(4 + j, (*chip, c), sibling).wait_send()
            mine().wait()

        return start, mid, end

    return Comm([x], [jax.ShapeDtypeStruct((N_DEV,) + x.shape, F32)], {},
                [pltpu.SemaphoreType.DMA((7,)), pltpu.SemaphoreType.DMA((7,)), pltpu.SemaphoreType.DMA], emit)


def sum_blocks(g, name):
    _, r, c = g.shape
    tr = 256

    def body(g_ref, o_ref):
        acc = g_ref[0]
        for k in range(1, N_DEV):
            acc = acc + g_ref[k]
        o_ref[...] = acc

    return pl.pallas_call(
        body, name=name, grid=(r // tr,),
        in_specs=[pl.BlockSpec((N_DEV, tr, c), lambda i: (0, i, 0))],
        out_specs=pl.BlockSpec((tr, c), lambda i: (i, 0)),
        out_shape=jax.ShapeDtypeStruct((r, c), F32),
        compiler_params=_params("parallel"),
    )(g)


def _grad_view(g, full, axis):
    r, c = full
    if axis == 0:
        return g.reshape(N_SHARD, 2, r // N_SHARD // 2, c)
    return g.reshape(1, 2, r // 2, c)


def exchange_halves(gviews, name):
    nw = len(gviews)

    def body(*refs):
        srcs, outs = refs[:nw], refs[nw:2 * nw]
        send_sems, recv_sems = refs[2 * nw:]
        x, y, c = _mesh_pos()
        cps = []
        for w in range(nw):
            cp = pltpu.make_async_remote_copy(
                src_ref=srcs[w].at[:, pl.ds(1 - c, 1)], dst_ref=outs[w], send_sem=send_sems.at[w],
                recv_sem=recv_sems.at[w], device_id=(x, y, 1 - c), device_id_type=MESH_T)
            cp.start()
            cps.append(cp)
        for cp in cps:
            cp.wait()

    hbm = pl.BlockSpec(memory_space=pl.ANY)
    return pl.pallas_call(
        body, name=name, in_specs=[hbm] * nw, out_specs=[hbm] * nw,
        out_shape=[jax.ShapeDtypeStruct((g.shape[0], 1) + g.shape[2:], BF16) for g in gviews],
        scratch_shapes=[pltpu.SemaphoreType.DMA((nw,)), pltpu.SemaphoreType.DMA((nw,))],
        compiler_params=pltpu.CompilerParams(vmem_limit_bytes=VMEM_LIMIT_V7X),
    )(*gviews)


def _row_tile(rh):
    return 128 if rh % 128 == 0 else rh


def add_halves(gview, recv, c_idx, name):
    a, _, rh, cc = gview.shape
    tr = _row_tile(rh)

    def body(c_ref, g_ref, r_ref, o_ref):
        o_ref[0] = (g_ref[0, 0].astype(F32) + r_ref[0, 0].astype(F32)).astype(BF16)

    return pl.pallas_call(
        body, name=name,
        grid_spec=pltpu.PrefetchScalarGridSpec(
            num_scalar_prefetch=1, grid=(a, rh // tr),
            in_specs=[pl.BlockSpec((1, 1, tr, cc), lambda s, i, c_ref: (s, c_ref[0], i, 0)),
                      pl.BlockSpec((1, 1, tr, cc), lambda s, i, c_ref: (s, 0, i, 0))],
            out_specs=pl.BlockSpec((1, tr, cc), lambda s, i, c_ref: (s, i, 0))),
        out_shape=jax.ShapeDtypeStruct((a, rh, cc), BF16),
        compiler_params=_params("parallel", "parallel"),
    )(c_idx, gview, recv)


def _piece_shape(full, axis):
    rs, cs = _shard_shape(full, axis)
    return (rs // 2, cs)


def scatter_pieces_comm(partials, which):
    nw = len(which)
    specs = [BIG[w] for w in which]

    def emit(srcs, outs, sems):
        send_sems, recv_sems = sems
        x, y, c = _mesh_pos()
        chips = _other_chips(x, y)

        def copies():
            cps = []
            for w, (_, full, axis) in enumerate(specs):
                cs = full[1] // N_SHARD
                for j, chip in enumerate(chips):
                    s_j = 2 * chip[0] + chip[1]
                    src = srcs[w].at[s_j] if axis == 0 else srcs[w].at[0, :, pl.ds(s_j * cs, cs)]
                    cps.append(pltpu.make_async_remote_copy(
                        src_ref=src, dst_ref=outs[w].at[j], send_sem=send_sems.at[3 * w + j],
                        recv_sem=recv_sems.at[3 * w + j], device_id=(*chip, c), device_id_type=MESH_T))
            return cps

        def start():
            for cp in copies():
                cp.start()

        def mid():
            pass

        def end():
            for cp in copies():
                cp.wait()

        return start, mid, end

    return Comm(list(partials), [jax.ShapeDtypeStruct((3,) + _piece_shape(full, axis), BF16) for _, full, axis in specs],
                {}, [pltpu.SemaphoreType.DMA((3 * nw,)), pltpu.SemaphoreType.DMA((3 * nw,))], emit)


def add_pieces(partial, recv, idx, axis, name):
    _, rh, cs = recv.shape
    tr = _row_tile(rh)

    def body(idx_ref, p_ref, r_ref, o_ref):
        o_ref[0] = ((p_ref[0].astype(F32) + r_ref[0].astype(F32)) + r_ref[1].astype(F32)) + r_ref[2].astype(F32)

    if axis == 0:
        pspec = pl.BlockSpec((1, tr, cs), lambda i, idx_ref: (idx_ref[0], i, 0))
    else:
        pspec = pl.BlockSpec((1, tr, cs), lambda i, idx_ref: (0, i, idx_ref[0]))
    return pl.pallas_call(
        body, name=name,
        grid_spec=pltpu.PrefetchScalarGridSpec(
            num_scalar_prefetch=1, grid=(rh // tr,),
            in_specs=[pspec, pl.BlockSpec((3, tr, cs), lambda i, idx_ref: (0, i, 0))],
            out_specs=pl.BlockSpec((1, tr, cs), lambda i, idx_ref: (idx_ref[1], i, 0))),
        out_shape=jax.ShapeDtypeStruct((2, rh, cs), F32),
        compiler_params=_params("parallel"),
    )(idx, partial, recv)


def join_halves(halves):
    nw = len(BIG)

    def body(*refs):
        outs = refs[nw:2 * nw]
        send_sems, recv_sems = refs[2 * nw:]
        x, y, c = _mesh_pos()
        cps = []
        for w in range(nw):
            cp = pltpu.make_async_remote_copy(
                src_ref=outs[w].at[c], dst_ref=outs[w].at[c], send_sem=send_sems.at[w], recv_sem=recv_sems.at[w],
                device_id=(x, y, 1 - c), device_id_type=MESH_T)
            cp.start()
            cps.append(cp)
        for w in range(nw):
            cps[w].wait_send()
            pltpu.make_async_remote_copy(
                src_ref=outs[w].at[1 - c], dst_ref=outs[w].at[1 - c], send_sem=send_sems.at[w],
                recv_sem=recv_sems.at[w], device_id=(x, y, 1 - c), device_id_type=MESH_T).wait_recv()

    hbm = pl.BlockSpec(memory_space=pl.ANY)
    return pl.pallas_call(
        body, name="grad_join_halves", in_specs=[hbm] * nw, out_specs=[hbm] * nw,
        out_shape=[jax.ShapeDtypeStruct(h.shape, F32) for h in halves],
        input_output_aliases={i: i for i in range(nw)},
        scratch_shapes=[pltpu.SemaphoreType.DMA((nw,))] * 2,
        compiler_params=pltpu.CompilerParams(vmem_limit_bytes=VMEM_LIMIT_V7X),
    )(*halves)


MOD_COLS = N_MOD * D_MODEL // N_SHARD
MOD_TILE = 512


def mod_fwd(c16, w_mod):
    def body(c_ref, w_ref, s_ref, o_ref):
        cv = c_ref[...]
        s = cv * _sigmoid(cv)
        s_ref[...] = s
        o_ref[...] = jnp.dot(s.astype(BF16), w_ref[...].astype(BF16), preferred_element_type=F32)

    return pl.pallas_call(
        body, name="mod_fwd", grid=(MOD_COLS // MOD_TILE,),
        in_specs=[_full((16, D_MODEL)), pl.BlockSpec((D_MODEL, MOD_TILE), lambda j: (0, j))],
        out_specs=[_full((16, D_MODEL)), pl.BlockSpec((16, MOD_TILE), lambda j: (0, j))],
        out_shape=[jax.ShapeDtypeStruct((16, D_MODEL), F32), jax.ShapeDtypeStruct((16, MOD_COLS), F32)],
        compiler_params=_params("arbitrary"),
    )(c16, w_mod)


def mod_bwd(s16, dm16, w_mod):
    hi = lax.Precision.HIGHEST

    def body(s_ref, d_ref, w_ref, gw_ref, ds_ref):
        j = pl.program_id(0)
        dm = d_ref[...]
        gw_ref[...] = lax.dot_general(s_ref[...], dm, (((0,), (0,)), ((), ())), preferred_element_type=F32, precision=hi)
        part = lax.dot_general(dm, w_ref[...], (((1,), (1,)), ((), ())), preferred_element_type=F32, precision=hi)

        @pl.when(j == 0)
        def _():
            ds_ref[...] = part

        @pl.when(j > 0)
        def _():
            ds_ref[...] = ds_ref[...] + part

    return pl.pallas_call(
        body, name="mod_bwd", grid=(MOD_COLS // MOD_TILE,),
        in_specs=[_full((16, D_MODEL)), pl.BlockSpec((16, MOD_TILE), lambda j: (0, j)),
                  pl.BlockSpec((D_MODEL, MOD_TILE), lambda j: (0, j))],
        out_specs=[pl.BlockSpec((D_MODEL, MOD_TILE), lambda j: (0, j)), _full((16, D_MODEL))],
        out_shape=[jax.ShapeDtypeStruct((D_MODEL, MOD_COLS), F32), jax.ShapeDtypeStruct((16, D_MODEL), F32)],
        compiler_params=_params("arbitrary"),
    )(s16, dm16, w_mod)


def cctx_grad(parts, c_ctx):
    def body(p_ref, c_ref, o_ref):
        ds = p_ref[0:1, :]
        for s in range(1, N_SHARD):
            ds = ds + p_ref[16 * s:16 * s + 1, :]
        cv = c_ref[...]
        sg = _sigmoid(cv)
        o_ref[...] = ds * (sg * (1.0 + cv * (1.0 - sg)))

    return pl.pallas_call(
        body, name="cctx_grad", in_specs=[_full((N_DEV * 8, D_MODEL)), _full((1, D_MODEL))],
        out_specs=_full((1, D_MODEL)), out_shape=jax.ShapeDtypeStruct((1, D_MODEL), F32),
    )(parts, c_ctx)


def add_rows(a, b, name):
    def body(a_ref, b_ref, o_ref):
        o_ref[...] = a_ref[...] + b_ref[...]

    return pl.pallas_call(body, name=name, in_specs=[_full(a.shape), _full(b.shape)], out_specs=_full(a.shape),
                          out_shape=jax.ShapeDtypeStruct(a.shape, F32))(a, b)


def _adamw_update(w_ref, g_ref, m_ref, v_ref, d_ref, nm_ref, nv_ref):
    g_ = g_ref[...]
    m_ = ADAM_B1 * m_ref[...] + (1.0 - ADAM_B1) * g_
    v_ = ADAM_B2 * v_ref[...] + (1.0 - ADAM_B2) * (g_ * g_)
    m_hat = m_ / (1.0 - ADAM_B1 ** ADAM_STEP)
    v_hat = v_ / (1.0 - ADAM_B2 ** ADAM_STEP)
    d_ref[...] = -ADAM_LR * (m_hat / (jnp.sqrt(v_hat) + ADAM_EPS) + ADAM_WD * w_ref[...])
    nm_ref[...] = m_
    nv_ref[...] = v_


def adamw_many(ws, gs, ms, vs):
    n = len(ws)

    def body(*refs):
        for i in range(n):
            _adamw_update(*[refs[k * n + i] for k in range(7)])

    shapes = [jax.ShapeDtypeStruct(w.shape, F32) for w in ws]
    return pl.pallas_call(body, name="adamw_small", out_shape=shapes * 3,
                          compiler_params=pltpu.CompilerParams(vmem_limit_bytes=VMEM_LIMIT_V7X))(*ws, *gs, *ms, *vs)


def adamw(w, g, m, v, name):
    r, c = w.shape
    tr = 128 if (r % 128 == 0 and r > 128) else r

    def body(w_ref, g_ref, m_ref, v_ref, d_ref, nm_ref, nv_ref):
        _adamw_update(w_ref, g_ref, m_ref, v_ref, d_ref, nm_ref, nv_ref)

    spec = pl.BlockSpec((tr, c), lambda i: (i, 0))
    shp = jax.ShapeDtypeStruct((r, c), F32)
    return pl.pallas_call(
        body, name=name, grid=(r // tr,), in_specs=[spec] * 4, out_specs=[spec] * 3, out_shape=[shp] * 3,
        compiler_params=_params("parallel"),
    )(w, g, m, v)


LANES = 1024


def _pack(arrs):
    rows, spans, at = [], [], 0
    for a in arrs:
        n = int(np.prod(a.shape))
        nr = 8 * -(-n // (8 * LANES))
        flat = a.reshape(-1)
        if nr * LANES != n:
            flat = jnp.concatenate([flat, jnp.zeros((nr * LANES - n,), F32)])
        rows.append(flat.reshape(nr, LANES))
        spans.append((at, nr, n, a.shape))
        at += nr
    return jnp.concatenate(rows, axis=0), spans


def _unpack(buf, spans):
    out = []
    for at, nr, n, shape in spans:
        out.append(buf[at:at + nr].reshape(-1)[:n].reshape(shape))
    return out


SMALL_SHARD = ("lru_conv_w", "lru_ba", "lru_bx", "lru_lambda", "ffn_conv_w")


def kernel(x, c, ctx, c_ctx, w_mod, b_mod, norm_mix_g, norm_ffn_g, w_in, lru_conv_w, lru_conv_b, lru_wa, lru_ba, lru_wx, lru_bx, lru_lambda, q_norm_g, k_norm_g, na_rpb, w_rnn_out, w_na_out, w_out, w_up, ffn_conv_w, ffn_conv_b, w_down, loss_target, m_c_ctx, m_w_mod, m_b_mod, m_norm_mix_g, m_norm_ffn_g, m_w_in, m_lru_conv_w, m_lru_conv_b, m_lru_wa, m_lru_ba, m_lru_wx, m_lru_bx, m_lru_lambda, m_q_norm_g, m_k_norm_g, m_na_rpb, m_w_rnn_out, m_w_na_out, m_w_out, m_w_up, m_ffn_conv_w, m_ffn_conv_b, m_w_down, v_c_ctx, v_w_mod, v_b_mod, v_norm_mix_g, v_norm_ffn_g, v_w_in, v_lru_conv_w, v_lru_conv_b, v_lru_wa, v_lru_ba, v_lru_wx, v_lru_bx, v_lru_lambda, v_q_norm_g, v_k_norm_g, v_na_rpb, v_w_rnn_out, v_w_na_out, v_w_out, v_w_up, v_ffn_conv_w, v_ffn_conv_b, v_w_down):
    weights = dict(c_ctx=c_ctx, w_mod=w_mod, b_mod=b_mod, norm_mix_g=norm_mix_g, norm_ffn_g=norm_ffn_g, w_in=w_in,
                   lru_conv_w=lru_conv_w, lru_conv_b=lru_conv_b, lru_wa=lru_wa, lru_ba=lru_ba, lru_wx=lru_wx,
                   lru_bx=lru_bx, lru_lambda=lru_lambda, q_norm_g=q_norm_g, k_norm_g=k_norm_g, na_rpb=na_rpb,
                   w_rnn_out=w_rnn_out, w_na_out=w_na_out, w_out=w_out, w_up=w_up, ffn_conv_w=ffn_conv_w,
                   ffn_conv_b=ffn_conv_b, w_down=w_down)
    mom1 = dict(c_ctx=m_c_ctx, w_mod=m_w_mod, b_mod=m_b_mod, norm_mix_g=m_norm_mix_g, norm_ffn_g=m_norm_ffn_g,
                w_in=m_w_in, lru_conv_w=m_lru_conv_w, lru_conv_b=m_lru_conv_b, lru_wa=m_lru_wa, lru_ba=m_lru_ba,
                lru_wx=m_lru_wx, lru_bx=m_lru_bx, lru_lambda=m_lru_lambda, q_norm_g=m_q_norm_g, k_norm_g=m_k_norm_g,
                na_rpb=m_na_rpb, w_rnn_out=m_w_rnn_out, w_na_out=m_w_na_out, w_out=m_w_out, w_up=m_w_up,
                ffn_conv_w=m_ffn_conv_w, ffn_conv_b=m_ffn_conv_b, w_down=m_w_down)
    mom2 = dict(c_ctx=v_c_ctx, w_mod=v_w_mod, b_mod=v_b_mod, norm_mix_g=v_norm_mix_g, norm_ffn_g=v_norm_ffn_g,
                w_in=v_w_in, lru_conv_w=v_lru_conv_w, lru_conv_b=v_lru_conv_b, lru_wa=v_lru_wa, lru_ba=v_lru_ba,
                lru_wx=v_lru_wx, lru_bx=v_lru_bx, lru_lambda=v_lru_lambda, q_norm_g=v_q_norm_g, k_norm_g=v_k_norm_g,
                na_rpb=v_na_rpb, w_rnn_out=v_w_rnn_out, w_na_out=v_w_na_out, w_out=v_w_out, w_up=v_w_up,
                ffn_conv_w=v_ffn_conv_w, ffn_conv_b=v_ffn_conv_b, w_down=v_w_down)
    order = list(weights)
    d = D_MODEL
    mx_, my_, mc_ = _mesh_pos()
    shard = 2 * mx_ + my_
    dev = 2 * shard + mc_

    local_small, small_spans = _pack([c] + [weights[k][0] for k in SMALL_SHARD])
    gath = all_gather8(local_small, "gather_small").reshape(N_DEV, local_small.shape[0], LANES)
    per_dev = [_unpack(gath[k], small_spans) for k in range(N_DEV)]
    c_all = jnp.concatenate([per_dev[k][0] for k in range(N_DEV)], axis=0)
    full_small = {name: jnp.concatenate([per_dev[2 * s][1 + i] for s in range(N_SHARD)], axis=-1)
                  for i, name in enumerate(SMALL_SHARD)}
    c16 = jnp.concatenate([c_all, c_ctx.reshape(1, d), jnp.zeros((7, d), F32)], axis=0)
    s16, mod_part = mod_fwd(c16, w_mod[0])
    mod_all = all_gather8(mod_part, "gather_mod").reshape(N_DEV, 16, MOD_COLS)
    mod = jnp.concatenate([mod_all[2 * s] for s in range(N_SHARD)], axis=1) + b_mod
    modx = lax.dynamic_slice(mod, (dev, 0), (1, N_MOD * d))
    modc = mod[8:9]

    idx = jnp.stack([shard, mc_]).astype(jnp.int32)
    c_idx = jnp.reshape(mc_, (1,)).astype(jnp.int32)
    wsh = {name: cast_into_full(weights[name][0], full, axis, idx, "cast_" + name) for name, full, axis in BIG}
    w_in_full = run_comm(gather_weights_comm([wsh["w_in"]], [0]), "gather_w_in")[0]

    z = jnp.concatenate([ctx[0], x[0]], axis=0)
    res = local_step(z, loss_target[0], modx, modc, norm_mix_g, norm_ffn_g, w_in_full, full_small["lru_conv_w"],
                     lru_conv_b, lru_wa[0], full_small["lru_ba"], lru_wx[0], full_small["lru_bx"],
                     full_small["lru_lambda"], q_norm_g, k_norm_g, na_rpb[0], wsh["w_rnn_out"], wsh["w_na_out"],
                     wsh["w_out"], wsh["w_up"], full_small["ffn_conv_w"], ffn_conv_b, wsh["w_down"], c_idx=c_idx)

    halves = [add_pieces(res["partials"][i], res["pieces"][i], idx, BIG[i][2], "add_pieces_" + BIG[i][0])
              for i in range(len(BIG))]
    joined = join_halves(halves)
    grads = {name: joined[i].reshape(_shard_shape(full, axis)) for i, (name, full, axis) in enumerate(BIG)}

    for k in ("lru_wa", "lru_wx"):
        grads[k] = sum_blocks(res["lru_w_all"][k], "sum_" + k).reshape(weights[k].shape[1:])
    small_names = ["norm_mix_g", "norm_ffn_g", "lru_conv_w", "lru_conv_b", "lru_ba", "lru_bx",
                   "lru_lambda", "q_norm_g", "k_norm_g", "na_rpb", "ffn_conv_w", "ffn_conv_b"]
    local_g, g_spans = _pack([res["loss_sq"][0:1, 0:1], res["d_modx"], res["d_modc"]] + [res[k] for k in small_names])
    n_rows = local_g.shape[0]
    g_all, g_tot = all_gather8(local_g, "allreduce_small", with_sum=True)
    tot = _unpack(g_tot, g_spans)
    loss = (0.5 / d) * tot[0][0, 0]
    small_tot = dict(zip(small_names, tot[3:]))
    at_x = g_spans[1][0]
    dmx_rows = g_all.reshape(N_DEV, n_rows, LANES)[:, at_x:at_x + N_MOD, :].reshape(N_DEV, N_MOD * d)
    dmc_row = jnp.concatenate([tot[2], jnp.zeros((1, 4 * d), F32)], axis=1)
    dm16 = jnp.concatenate([dmx_rows, dmc_row, jnp.zeros((7, N_MOD * d), F32)], axis=0)
    grads["b_mod"] = add_rows(tot[1], dmc_row, "b_mod_grad")
    g_w_mod, ds16 = mod_bwd(s16, lax.dynamic_slice(dm16, (0, shard * MOD_COLS), (16, MOD_COLS)), w_mod[0])
    grads["w_mod"] = g_w_mod
    ds_parts = all_gather8(ds16[8:16], "gather_dsctx")
    grads["c_ctx"] = cctx_grad(ds_parts, c_ctx.reshape(1, d))
    for k in small_names:
        g = small_tot[k]
        if k in SMALL_SHARD:
            w_sh = weights[k].shape[-1]
            g = lax.dynamic_slice_in_dim(g, shard * w_sh, w_sh, axis=g.ndim - 1)
        grads[k] = g

    delta, new_m, new_v = {}, {}, {}
    for name, _, _ in BIG + (("w_mod", None, None),):
        delta[name], new_m[name], new_v[name] = adamw(weights[name][0], grads[name], mom1[name][0], mom2[name][0],
                                                      "adamw_" + name)
    rest = [k for k in order if k not in delta]
    views = {k: (grads[k].shape if grads[k].ndim <= 3 else (-1, grads[k].shape[-1])) for k in rest}
    small = adamw_many(*[[t[k].reshape(views[k]) for k in rest] for t in (weights, grads, mom1, mom2)])
    n_rest = len(rest)
    for i, k in enumerate(rest):
        delta[k], new_m[k], new_v[k] = small[i], small[n_rest + i], small[2 * n_rest + i]

    shaped = lambda t: [t[k].reshape(weights[k].shape) for k in order]
    return (loss, res["grad_x"][None], *shaped(grads), *shaped(delta), *shaped(new_m), *shaped(new_v))
```

```python
import numpy as np
import jax
import jax.numpy as jnp
from jax import lax
from jax.experimental import pallas as pl
from jax.experimental.pallas import tpu as pltpu

F32 = jnp.float32
BF16 = jnp.bfloat16

D_MODEL = 1024
SEQ = 2048
CTX_LEN = 256
ZLEN = SEQ + CTX_LEN
GRID_W = 64
GRID_ROWS = SEQ // GRID_W
LRU_BLOCK_W = 128
LRU_BLOCKS = 8
LRU_C = 8.0
NA_HEADS = 16
HEAD_DIM = 64
NA_ROWS = 8
NA_COLS = 16
ROPE_BASE = 10000.0
D_FF = 2816
N_MOD = 6
IN_COLS = 7 * D_MODEL
EPS = 1e-6
NEG_INF = -1e30
N_DEV = 8
N_SHARD = 4

ADAM_LR = 0.001
ADAM_B1 = 0.9
ADAM_B2 = 0.999
ADAM_EPS = 1e-08
ADAM_WD = 0.01
ADAM_STEP = 10

ROW_TILE = 256
Q_ROWS = 4
Q_TILE = Q_ROWS * GRID_W
KEY_ROWS = 12
KEY_TILE = KEY_ROWS * GRID_W
BT_PAD = 4
BT_LEN = 24
VMEM_LIMIT_V7X = 56 * 1024 * 1024

MESH_T = pl.DeviceIdType.MESH


def _params(*sem):
    return pltpu.CompilerParams(dimension_semantics=sem if sem else None, vmem_limit_bytes=VMEM_LIMIT_V7X)


def _full(shape):
    nd = len(shape)
    return pl.BlockSpec(shape, lambda *_: (0,) * nd)


class Comm:
    def __init__(self, inputs, out_shapes, aliases, scratch, emit):
        self.inputs, self.out_shapes, self.aliases, self.scratch, self.emit = inputs, out_shapes, aliases, scratch, emit


def _call(body, *, name, grid, in_specs, out_specs, out_shape, args, scratch_shapes=(), sem=(), comm=None):
    n_in, n_out, n_sc = len(in_specs), len(out_specs), len(scratch_shapes)
    if comm is None:
        res = pl.pallas_call(body, name=name, grid=grid, in_specs=list(in_specs), out_specs=list(out_specs),
                             out_shape=list(out_shape), scratch_shapes=list(scratch_shapes),
                             compiler_params=_params(*sem))(*args)
        return list(res), []
    k_in, k_out = len(comm.inputs), len(comm.out_shapes)
    steps = int(np.prod(grid))

    def hosted(*refs):
        ins, cins = refs[:n_in], refs[n_in:n_in + k_in]
        at = n_in + k_in
        outs, couts = refs[at:at + n_out], refs[at + n_out:at + n_out + k_out]
        at += n_out + k_out
        scr, cscr = refs[at:at + n_sc], refs[at + n_sc:]
        start, mid, end = comm.emit(cins, couts, cscr)
        lin = pl.program_id(0)
        for ax in range(1, len(grid)):
            lin = lin * grid[ax] + pl.program_id(ax)
        pl.when(lin == 0)(start)
        body(*ins, *outs, *scr)
        pl.when(lin == steps - 1 - steps // 7)(mid)
        pl.when(lin == steps - 1)(end)

    hbm = pl.BlockSpec(memory_space=pl.ANY)
    res = pl.pallas_call(
        hosted, name=name, grid=grid, in_specs=list(in_specs) + [hbm] * k_in, out_specs=list(out_specs) + [hbm] * k_out,
        out_shape=list(out_shape) + list(comm.out_shapes), scratch_shapes=list(scratch_shapes) + list(comm.scratch),
        input_output_aliases={n_in + i: n_out + o for i, o in comm.aliases.items()},
        compiler_params=_params(*(("arbitrary",) * len(grid))))(*args, *comm.inputs)
    return list(res[:n_out]), list(res[n_out:])


def _sigmoid(x):
    return 0.5 * jnp.tanh(0.5 * x) + 0.5


def _gelu_parts(x):
    c0 = 0.7978845608028654
    inner = c0 * (x + 0.044715 * x * x * x)
    t = jnp.tanh(inner)
    g = 0.5 * x * (1.0 + t)
    dg = 0.5 * (1.0 + t) + 0.5 * x * (1.0 - t * t) * c0 * (1.0 + 3.0 * 0.044715 * x * x)
    return g, dg


def _dot_nt(a, b):
    return lax.dot_general(a, b, (((1,), (1,)), ((), ())), preferred_element_type=F32)


def _dot_tn(a, b):
    return lax.dot_general(a, b, (((0,), (0,)), ((), ())), preferred_element_type=F32)


def norm_mod(xin, gain, shift, scale, name):
    r, d = xin.shape
    s_mod = shift.shape[0]
    assert r % ROW_TILE == 0

    def body(x_ref, g_ref, sh_ref, sc_ref, xn_ref):
        x = x_ref[...]
        nrm = x * lax.rsqrt(jnp.mean(x * x, axis=-1, keepdims=True) + EPS)
        xn_ref[...] = ((nrm * g_ref[...]) * (1.0 + sc_ref[0]) + sh_ref[0]).astype(BF16)

    mod_spec = pl.BlockSpec((1, 1, d), lambda i: (jnp.minimum(i, s_mod - 1), 0, 0))
    return pl.pallas_call(
        body, name=name, grid=(r // ROW_TILE,),
        in_specs=[pl.BlockSpec((ROW_TILE, d), lambda i: (i, 0)), _full((1, d)), mod_spec, mod_spec],
        out_specs=pl.BlockSpec((ROW_TILE, d), lambda i: (i, 0)),
        out_shape=jax.ShapeDtypeStruct((r, d), BF16),
        compiler_params=_params("parallel"),
    )(xin, gain, shift, scale)


def matmul_wide(a, b, name, tm, tn, comm=None):
    m, k = a.shape
    n = b.shape[1]
    assert m % tm == 0 and n % tn == 0

    def body(a_ref, b_ref, o_ref):
        o_ref[...] = jnp.dot(a_ref[...], b_ref[...], preferred_element_type=F32)

    res, extra = _call(
        body, name=name, grid=(n // tn, m // tm),
        in_specs=[pl.BlockSpec((tm, k), lambda j, i: (i, 0)), pl.BlockSpec((k, tn), lambda j, i: (0, j))],
        out_specs=[pl.BlockSpec((tm, tn), lambda j, i: (i, j))],
        out_shape=[jax.ShapeDtypeStruct((m, n), F32)],
        sem=("parallel", "parallel"), args=(a, b), comm=comm)
    return res[0], extra


def _row_ids(n, w):
    return lax.broadcasted_iota(jnp.int32, (n, w), 0)


def _lru_conv(xr, cw, cb):
    row = _row_ids(ZLEN, LRU_BLOCK_W)
    segpos = jnp.where(row < CTX_LEN, row, row - CTX_LEN)
    seglen = jnp.where(row < CTX_LEN, CTX_LEN, SEQ)
    acc = xr * cw[2:3, :] + cb
    for k in (0, 1, 3):
        off = k - 2
        sh = pltpu.roll(xr, (-off) % ZLEN, 0)
        ok = (segpos + off >= 0) & (segpos + off < seglen)
        acc = acc + jnp.where(ok, sh, 0.0) * cw[k:k + 1, :]
    return acc


def _lru_conv_t(dxc, cw):
    row = _row_ids(ZLEN, LRU_BLOCK_W)
    segpos = jnp.where(row < CTX_LEN, row, row - CTX_LEN)
    seglen = jnp.where(row < CTX_LEN, CTX_LEN, SEQ)
    acc = dxc * cw[2:3, :]
    for k in (0, 1, 3):
        off = k - 2
        sh = pltpu.roll(dxc, off % ZLEN, 0)
        ok = (segpos - off >= 0) & (segpos - off < seglen)
        acc = acc + jnp.where(ok, sh, 0.0) * cw[k:k + 1, :]
    return acc


def _lru_gates(xc, xcb, wa, ba, wx, bx, lam):
    r = _sigmoid(jnp.dot(xcb, wa, preferred_element_type=F32) + ba)
    i = _sigmoid(jnp.dot(xcb, wx, preferred_element_type=F32) + bx)
    sp = jnp.maximum(-lam, 0.0) + jnp.log1p(jnp.exp(-jnp.abs(lam)))
    la = (-LRU_C) * r * sp
    a = jnp.exp(la)
    sq = jnp.sqrt(-jnp.tanh(la) * (1.0 + a * a))
    b = sq * i * xc
    return r, i, sp, a, sq, b


def _scan8_fwd(a, b, rid):
    for s in (1, 2, 4):
        a_s = pltpu.roll(a, s, 0)
        b_s = pltpu.roll(b, s, 0)
        m = rid >= s
        b = jnp.where(m, a * b_s + b, b)
        a = jnp.where(m, a * a_s, a)
    return a, b


def _scan8_rev(a, b, rid):
    for s in (1, 2, 4):
        a_s = pltpu.roll(a, 8 - s, 0)
        b_s = pltpu.roll(b, 8 - s, 0)
        m = rid < 8 - s
        b = jnp.where(m, a * b_s + b, b)
        a = jnp.where(m, a * a_s, a)
    return a, b


N_CHUNK = ZLEN // 8
CTX_CHUNKS = CTX_LEN // 8
SCAN_UNROLL = 8


def _scan_up(a_ref, b_ref, h_ref, lo, hi, carry):
    rid = _row_ids(8, LRU_BLOCK_W)
    assert (hi - lo) % SCAN_UNROLL == 0

    def step(g, c):
        base = pl.multiple_of((lo + g * SCAN_UNROLL) * 8, 8)
        for u in range(SCAN_UNROLL):
            sl = pl.ds(base + 8 * u, 8)
            a, b = _scan8_fwd(a_ref[sl, :], b_ref[sl, :], rid)
            h = b + a * c
            h_ref[sl, :] = h
            c = h[7:8, :]
        return c

    return lax.fori_loop(0, (hi - lo) // SCAN_UNROLL, step, carry)


def _scan_down(a_ref, b_ref, h_ref, lo, hi, carry):
    rid = _row_ids(8, LRU_BLOCK_W)
    assert (hi - lo) % SCAN_UNROLL == 0

    def step(g, c):
        base = pl.multiple_of((hi - (g + 1) * SCAN_UNROLL) * 8, 8)
        for u in reversed(range(SCAN_UNROLL)):
            sl = pl.ds(base + 8 * u, 8)
            a, b = _scan8_rev(a_ref[sl, :], b_ref[sl, :], rid)
            h = b + a * c
            h_ref[sl, :] = h
            c = h[0:1, :]
        return c

    return lax.fori_loop(0, (hi - lo) // SCAN_UNROLL, step, carry)


def _lru_scan_dir(d, a_ref, b_ref, h_ref):
    zero = jnp.zeros((1, LRU_BLOCK_W), F32)
    if d == 0:
        _scan_up(a_ref, b_ref, h_ref, 0, N_CHUNK, zero)
    else:
        c = _scan_down(a_ref, b_ref, h_ref, 0, CTX_CHUNKS, zero)
        _scan_down(a_ref, b_ref, h_ref, CTX_CHUNKS, N_CHUNK, c)


def _lru_in_specs():
    blk = lambda rows: pl.BlockSpec((rows, LRU_BLOCK_W), lambda b: (0, b))
    wspec = pl.BlockSpec((2, 1, LRU_BLOCK_W, LRU_BLOCK_W), lambda b: (0, b, 0, 0))
    return blk, wspec


def lru_fwd(p, conv_w, conv_b, wa, ba, wx, bx, lam, comm=None):
    blk, wspec = _lru_in_specs()

    def body(xr_ref, gx_ref, cw_ref, cb_ref, wa_ref, ba_ref, wx_ref, bx_ref, lam_ref, y_ref, a_s, b_s, h_s, hsum_s):
        xr = xr_ref[...]
        xc = _lru_conv(xr, cw_ref[...], cb_ref[...])
        xcb = xc.astype(BF16)
        for d in (0, 1):
            _, _, _, a, _, b = _lru_gates(xc, xcb, wa_ref[d, 0].astype(BF16), ba_ref[d:d + 1, :],
                                          wx_ref[d, 0].astype(BF16), bx_ref[d:d + 1, :], lam_ref[d:d + 1, :])
            a_s[...] = a
            b_s[...] = b
            _lru_scan_dir(d, a_s, b_s, h_s)
            if d == 0:
                hsum_s[...] = h_s[...]
            else:
                hsum_s[...] = hsum_s[...] + h_s[...]
        g, _ = _gelu_parts(gx_ref[CTX_LEN:, :])
        y_ref[...] = (hsum_s[CTX_LEN:, :] * g).astype(BF16)

    zs = pltpu.VMEM((ZLEN, LRU_BLOCK_W), F32)
    res, extra = _call(
        body, name="lru_fwd", grid=(LRU_BLOCKS,),
        in_specs=[blk(ZLEN), pl.BlockSpec((ZLEN, LRU_BLOCK_W), lambda b: (0, 24 + b)), blk(4), blk(1),
                  wspec, blk(2), wspec, blk(2), blk(2)],
        out_specs=[pl.BlockSpec((SEQ, LRU_BLOCK_W), lambda b: (0, b))],
        out_shape=[jax.ShapeDtypeStruct((SEQ, D_MODEL), BF16)],
        scratch_shapes=[zs, zs, zs, zs], sem=("arbitrary",),
        args=(p, p, conv_w, conv_b, wa, ba, wx, bx, lam), comm=comm)
    return res[0], extra


def _rope_tables():
    t = np.arange(SEQ)
    lane = np.arange(2 * HEAD_DIM)
    in_head = lane % HEAD_DIM
    j = (in_head % 32) % 16
    freq = ROPE_BASE ** (-j.astype(np.float64) / 16.0)
    pos = np.where(in_head[None, :] < 32, (t // GRID_W)[:, None], (t % GRID_W)[:, None]).astype(np.float64)
    ang = (pos.astype(np.float32) * freq.astype(np.float32)[None, :]).astype(np.float32)
    cos = np.cos(ang).astype(np.float32)
    sin = np.sin(ang).astype(np.float32)
    sgn = np.where((in_head % 32) < 16, -1.0, 1.0).astype(np.float32)
    cos = np.concatenate([np.ones((CTX_LEN, 2 * HEAD_DIM), np.float32), cos], 0)
    sin = np.concatenate([np.zeros((CTX_LEN, 2 * HEAD_DIM), np.float32), sin * sgn[None, :]], 0)
    return jnp.asarray(cos), jnp.asarray(sin)


def _head_ones():
    lane = np.arange(2 * HEAD_DIM)
    return jnp.asarray((lane[:, None] // HEAD_DIM == lane[None, :] // HEAD_DIM).astype(np.float32))


def _rope_partner(x):
    lane = lax.broadcasted_iota(jnp.int32, x.shape, 1)
    return jnp.where((lane % 32) < 16, pltpu.roll(x, 128 - 16, 1), pltpu.roll(x, 16, 1))


def _head_rms(x, ones, gain):
    ms = jnp.dot(x * x, ones, preferred_element_type=F32, precision=lax.Precision.HIGHEST) * (1.0 / HEAD_DIM)
    rstd = lax.rsqrt(ms + EPS)
    return x * rstd * gain, rstd


PREP_TILE = 768


def qkv_prep(p, qg2, kg2, cos, sin, ones, comm=None):
    scale = HEAD_DIM ** -0.5

    def body(q_ref, k_ref, v_ref, qg_ref, kg_ref, cos_ref, sin_ref, ones_ref, qr_ref, qp_ref, kk_ref, vv_ref):
        ones_m = ones_ref[...]
        c, s = cos_ref[...], sin_ref[...]
        qn, _ = _head_rms(q_ref[...], ones_m, qg_ref[...])
        qn = qn * scale
        qr_ref[...] = (qn * c + _rope_partner(qn) * s).astype(BF16)
        qp_ref[...] = qn.astype(BF16)
        kn, _ = _head_rms(k_ref[...], ones_m, kg_ref[...])
        kk_ref[...] = (kn * c + _rope_partner(kn) * s).astype(BF16)
        vv_ref[...] = v_ref[...].astype(BF16)

    col = lambda base: pl.BlockSpec((PREP_TILE, 128), lambda hp, i: (i, base + hp))
    small = pl.BlockSpec((1, 128), lambda hp, i: (0, 0))
    tab = pl.BlockSpec((PREP_TILE, 128), lambda hp, i: (i, 0))
    oshape = jax.ShapeDtypeStruct((ZLEN, D_MODEL), BF16)
    res, extra = _call(
        body, name="qkv_prep", grid=(NA_HEADS // 2, ZLEN // PREP_TILE),
        in_specs=[col(32), col(8), col(16), small, small, tab, tab, _full((128, 128))],
        out_specs=[col(0)] * 4, out_shape=[oshape] * 4, sem=("parallel", "parallel"),
        args=(p, p, p, qg2, kg2, cos, sin, ones), comm=comm)
    return (*res, extra)


def _bias_expand():
    qc = np.arange(GRID_W)[:, None]
    kc = np.arange(GRID_W)[None, :]
    col_start = np.clip(qc - NA_COLS // 2, 0, GRID_W - NA_COLS)
    in_win = (kc >= col_start) & (kc < col_start + NA_COLS)
    dc = np.clip(kc - qc, -(NA_COLS - 1), NA_COLS - 1) + (NA_COLS - 1)
    e = np.zeros((2 * NA_COLS - 1, GRID_W, GRID_W), np.float32)
    for d in range(2 * NA_COLS - 1):
        e[d] = ((dc == d) & in_win).astype(np.float32)
    pen = np.where(in_win, 0.0, NEG_INF).astype(np.float32)
    return e, pen


def bias_table(rpb2, comm=None):
    e, pen = _bias_expand()
    n_dr = 2 * NA_ROWS - 1
    ea = np.zeros((31, GRID_W, 128), np.float32)
    ea[:, :, :GRID_W] = e
    eb = np.zeros((31, GRID_W, 128), np.float32)
    eb[:, :, GRID_W:] = e
    pen2 = np.concatenate([pen, pen], 1)
    ea = jnp.asarray(ea.reshape(31, GRID_W * 128))
    eb = jnp.asarray(eb.reshape(31, GRID_W * 128))
    sel_a = np.zeros((BT_LEN, n_dr), np.float32)
    sel_b = np.zeros((BT_LEN, n_dr), np.float32)
    for r in range(BT_LEN):
        dr = r - BT_PAD
        if 0 <= dr < n_dr:
            sel_a[r, dr] = 1.0
        if 0 <= dr + 1 < n_dr:
            sel_b[r, dr + 1] = 1.0
    sel_a, sel_b = jnp.asarray(sel_a), jnp.asarray(sel_b)
    pen2 = jnp.asarray(pen2.reshape(1, GRID_W * 128))
    hi = lax.Precision.HIGHEST

    def body(rpb_ref, sa_ref, sb_ref, ea_ref, eb_ref, pen_ref, o_ref, ra_s, rb_s):
        for h in range(NA_HEADS):
            rp = rpb_ref[h]
            ra_s[h * BT_LEN:(h + 1) * BT_LEN, :] = jnp.dot(sa_ref[...], rp, preferred_element_type=F32, precision=hi)
            rb_s[h * BT_LEN:(h + 1) * BT_LEN, :] = jnp.dot(sb_ref[...], rp, preferred_element_type=F32, precision=hi)
        o_ref[...] = (jnp.dot(ra_s[...], ea_ref[...], preferred_element_type=F32, precision=hi)
                      + jnp.dot(rb_s[...], eb_ref[...], preferred_element_type=F32, precision=hi) + pen_ref[...])

    tcol = 2048
    rows = NA_HEADS * BT_LEN
    res, extra = _call(
        body, name="bias_table", grid=(GRID_W * 128 // tcol,),
        in_specs=[_full((NA_HEADS, n_dr, 31)), _full((BT_LEN, n_dr)), _full((BT_LEN, n_dr)),
                  pl.BlockSpec((31, tcol), lambda j: (0, j)), pl.BlockSpec((31, tcol), lambda j: (0, j)),
                  pl.BlockSpec((1, tcol), lambda j: (0, j))],
        out_specs=[pl.BlockSpec((rows, tcol), lambda j: (0, j))],
        out_shape=[jax.ShapeDtypeStruct((rows, GRID_W * 128), F32)],
        scratch_shapes=[pltpu.VMEM((rows, 31), F32), pltpu.VMEM((rows, 31), F32)], sem=("parallel",),
        args=(rpb2, sel_a, sel_b, ea, eb, pen2), comm=comm)
    return res[0].reshape(NA_HEADS, BT_LEN, GRID_W, 128), extra


def _key_window(j):
    ws = jnp.clip(Q_ROWS * j - 4, 0, GRID_ROWS - KEY_ROWS)
    return ws, pl.multiple_of(CTX_LEN + ws * GRID_W, 256)


def _head_mask(hh):
    lane = lax.broadcasted_iota(jnp.int32, (Q_TILE, 128), 1)
    return (lane < HEAD_DIM) if hh == 0 else (lane >= HEAD_DIM)


def _attn_scores(j, ws, q_rot_h, q_pl_h, kw, kc, hh, bt_ref, s_ref):
    s_ref[:, :KEY_TILE] = _dot_nt(q_rot_h, kw)
    s_ref[:, KEY_TILE:] = _dot_nt(q_pl_h, kc)
    lane = lax.broadcasted_iota(jnp.int32, (GRID_W, 128), 1)
    base = ws - Q_ROWS * j + (NA_ROWS - 1) + BT_PAD
    for qi in range(Q_ROWS):
        rs = jnp.clip(Q_ROWS * j + qi - NA_ROWS // 2, 0, GRID_ROWS - NA_ROWS)
        for m in range(KEY_ROWS // 2):
            k0 = ws + 2 * m
            p0 = jnp.where((k0 >= rs) & (k0 < rs + NA_ROWS), 0.0, NEG_INF)
            p1 = jnp.where((k0 + 1 >= rs) & (k0 + 1 < rs + NA_ROWS), 0.0, NEG_INF)
            pen = jnp.where(lane < GRID_W, p0, p1)
            rows = slice(qi * GRID_W, (qi + 1) * GRID_W)
            cols = slice(128 * m, 128 * (m + 1))
            s_ref[rows, cols] = s_ref[rows, cols] + bt_ref[hh, base + 2 * m - qi] + pen
    return base


def attn_fwd(q_rot, q_pl, kk, vv, bt, comm=None):
    def body(qr_ref, qp_ref, kk_ref, vv_ref, bt_ref, o_ref, lse_ref, s_ref):
        j = pl.program_id(1)
        ws, start = _key_window(j)
        win = pl.ds(start, KEY_TILE)
        kw, kc = kk_ref[win, :], kk_ref[:CTX_LEN, :]
        vw, vc = vv_ref[win, :], vv_ref[:CTX_LEN, :]
        qr, qp = qr_ref[...], qp_ref[...]
        outs = []
        for hh in range(2):
            msk = _head_mask(hh)
            _attn_scores(j, ws, jnp.where(msk, qr, 0), jnp.where(msk, qp, 0), kw, kc, hh, bt_ref, s_ref)
            s = s_ref[...]
            mx = jnp.max(s, axis=-1, keepdims=True)
            pr = jnp.exp(s - mx)
            l = jnp.sum(pr, axis=-1, keepdims=True)
            prb = pr.astype(BF16)
            o = jnp.dot(prb[:, :KEY_TILE], vw, preferred_element_type=F32)
            o = o + jnp.dot(prb[:, KEY_TILE:], vc, preferred_element_type=F32)
            outs.append(o / l)
            lse_ref[hh] = mx + jnp.log(l)
        o_ref[...] = jnp.where(_head_mask(0), outs[0], outs[1])

    qspec = pl.BlockSpec((Q_TILE, 128), lambda hp, j: (j + 1, hp))
    kspec = pl.BlockSpec((ZLEN, 128), lambda hp, j: (0, hp))
    res, extra = _call(
        body, name="attn_fwd", grid=(NA_HEADS // 2, SEQ // Q_TILE),
        in_specs=[qspec, qspec, kspec, kspec, pl.BlockSpec((2, BT_LEN, GRID_W, 128), lambda hp, j: (hp, 0, 0, 0))],
        out_specs=[pl.BlockSpec((Q_TILE, 128), lambda hp, j: (j, hp)),
                   pl.BlockSpec((2, Q_TILE, 1), lambda hp, j: (hp, j, 0))],
        out_shape=[jax.ShapeDtypeStruct((SEQ, D_MODEL), F32), jax.ShapeDtypeStruct((NA_HEADS, SEQ, 1), F32)],
        scratch_shapes=[pltpu.VMEM((Q_TILE, KEY_TILE + CTX_LEN), F32)], sem=("parallel", "arbitrary"),
        args=(q_rot, q_pl, kk, vv, bt), comm=comm)
    return res[0], res[1], extra


def merge_fwd(y_rnn, y_na, p, z, g2, w_rnn, w_na, w_out):
    def body(yr_ref, yn_ref, mr_ref, mn_ref, x_ref, g2_ref, wr_ref, wn_ref, wo_ref, u_ref, v_ref, mg_ref, out_ref, x1_ref):
        u = jnp.dot(yr_ref[...], wr_ref[...], preferred_element_type=F32)
        v = jnp.dot(yn_ref[...].astype(BF16), wn_ref[...], preferred_element_type=F32)
        merged = (_sigmoid(mr_ref[...]) * u + _sigmoid(mn_ref[...]) * v).astype(BF16)
        out = jnp.dot(merged, wo_ref[...], preferred_element_type=F32)
        u_ref[...] = u
        v_ref[...] = v
        mg_ref[...] = merged
        out_ref[...] = out
        x1_ref[...] = x_ref[...] + g2_ref[...] * out

    row = pl.BlockSpec((ROW_TILE, D_MODEL), lambda i: (i, 0))
    lat = lambda cb: pl.BlockSpec((ROW_TILE, D_MODEL), lambda i: (i + 1, cb))
    wspec = _full((D_MODEL, D_MODEL))
    f32o = jax.ShapeDtypeStruct((SEQ, D_MODEL), F32)
    return pl.pallas_call(
        body, name="merge_fwd", grid=(SEQ // ROW_TILE,),
        in_specs=[row, row, lat(5), lat(6), lat(0), _full((1, D_MODEL)), wspec, wspec, wspec],
        out_specs=[row] * 5,
        out_shape=[f32o, f32o, jax.ShapeDtypeStruct((SEQ, D_MODEL), BF16), f32o, f32o],
        compiler_params=_params("parallel"),
    )(y_rnn, y_na, p, p, z, g2, w_rnn, w_na, w_out)


FF_TILE = 256
FF_TILES = D_FF // FF_TILE


def _ffn_conv(h, cw, cb):
    row = _row_ids(SEQ, FF_TILE)
    prev = jnp.where(row >= 1, pltpu.roll(h, 1, 0), 0.0)
    nxt = jnp.where(row < SEQ - 1, pltpu.roll(h, SEQ - 1, 0), 0.0)
    return prev * cw[0:1, :] + h * cw[1:2, :] + nxt * cw[2:3, :] + cb


def ffn_act(hpre, conv_w, conv_b):
    def body(ha_ref, hg_ref, wa_ref, wg_ref, ba_ref, bg_ref, o_ref):
        a = _ffn_conv(ha_ref[...], wa_ref[...], ba_ref[...])
        g = _ffn_conv(hg_ref[...], wg_ref[...], bg_ref[...])
        o_ref[...] = (a * _sigmoid(a) * g).astype(BF16)

    col = lambda rows, off: pl.BlockSpec((rows, FF_TILE), lambda j: (0, j + off))
    return pl.pallas_call(
        body, name="ffn_act", grid=(FF_TILES,),
        in_specs=[col(SEQ, 0), col(SEQ, FF_TILES), col(3, 0), col(3, FF_TILES), col(1, 0), col(1, FF_TILES)],
        out_specs=col(SEQ, 0),
        out_shape=jax.ShapeDtypeStruct((SEQ, D_FF), BF16),
        compiler_params=_params("parallel"),
    )(hpre, hpre, conv_w, conv_w, conv_b, conv_b)


def ffn_down_loss(act, w_down, x1, g5, target):
    def body(a_ref, w_ref, x1_ref, g5_ref, t_ref, f_ref, dy_ref, df_ref, ls_ref, dg_ref):
        i = pl.program_id(0)
        f = jnp.dot(a_ref[...], w_ref[...], preferred_element_type=F32)
        g5 = g5_ref[...]
        err = x1_ref[...] + g5 * f - t_ref[...]
        dy = err * (1.0 / D_MODEL)
        f_ref[...] = f
        dy_ref[...] = dy
        df_ref[...] = (dy * g5).astype(BF16)

        @pl.when(i == 0)
        def _():
            ls_ref[...] = jnp.zeros_like(ls_ref)
            dg_ref[...] = jnp.zeros_like(dg_ref)

        ls_ref[...] = ls_ref[...] + jnp.sum(err * err)
        dg_ref[...] = dg_ref[...] + jnp.sum(dy * f, axis=0, keepdims=True)

    row = pl.BlockSpec((ROW_TILE, D_MODEL), lambda i: (i, 0))
    f32o = jax.ShapeDtypeStruct((SEQ, D_MODEL), F32)
    return pl.pallas_call(
        body, name="ffn_down_loss", grid=(SEQ // ROW_TILE,),
        in_specs=[pl.BlockSpec((ROW_TILE, D_FF), lambda i: (i, 0)), _full((D_FF, D_MODEL)), row, _full((1, D_MODEL)), row],
        out_specs=[row, row, row, _full((8, 128)), _full((1, D_MODEL))],
        out_shape=[f32o, f32o, jax.ShapeDtypeStruct((SEQ, D_MODEL), BF16), jax.ShapeDtypeStruct((8, 128), F32),
                   jax.ShapeDtypeStruct((1, D_MODEL), F32)],
        compiler_params=_params("arbitrary"),
    )(act, w_down, x1, g5, target)


def ffn_down_bwd(df, w_down):
    def body(df_ref, w_ref, o_ref):
        o_ref[...] = _dot_nt(df_ref[...], w_ref[...])

    return pl.pallas_call(
        body, name="ffn_down_bwd", grid=(SEQ // ROW_TILE,),
        in_specs=[pl.BlockSpec((ROW_TILE, D_MODEL), lambda i: (i, 0)), _full((D_FF, D_MODEL))],
        out_specs=pl.BlockSpec((ROW_TILE, D_FF), lambda i: (i, 0)),
        out_shape=jax.ShapeDtypeStruct((SEQ, D_FF), F32),
        compiler_params=_params("parallel"),
    )(df, w_down)


def ffn_act_bwd(hpre, d_act, conv_w, conv_b):
    def half_bwd(dc, h, w, dh_ref, dw_ref, db_ref):
        row = _row_ids(SEQ, FF_TILE)
        h_prev = jnp.where(row >= 1, pltpu.roll(h, 1, 0), 0.0)
        h_next = jnp.where(row < SEQ - 1, pltpu.roll(h, SEQ - 1, 0), 0.0)
        dw_ref[0:1, :] = jnp.sum(dc * h_prev, axis=0, keepdims=True)
        dw_ref[1:2, :] = jnp.sum(dc * h, axis=0, keepdims=True)
        dw_ref[2:3, :] = jnp.sum(dc * h_next, axis=0, keepdims=True)
        db_ref[...] = jnp.sum(dc, axis=0, keepdims=True)
        dc_next = jnp.where(row < SEQ - 1, pltpu.roll(dc, SEQ - 1, 0), 0.0)
        dc_prev = jnp.where(row >= 1, pltpu.roll(dc, 1, 0), 0.0)
        dh_ref[...] = (dc_next * w[0:1, :] + dc * w[1:2, :] + dc_prev * w[2:3, :]).astype(BF16)

    def body(ha_ref, hg_ref, da_ref, wa_ref, wg_ref, ba_ref, bg_ref, dha_ref, dhg_ref, dwa_ref, dwg_ref, dba_ref, dbg_ref):
        ha, hg = ha_ref[...], hg_ref[...]
        a = _ffn_conv(ha, wa_ref[...], ba_ref[...])
        g = _ffn_conv(hg, wg_ref[...], bg_ref[...])
        sig = _sigmoid(a)
        dact = da_ref[...]
        half_bwd(dact * g * (sig * (1.0 + a * (1.0 - sig))), ha, wa_ref[...], dha_ref, dwa_ref, dba_ref)
        half_bwd(dact * a * sig, hg, wg_ref[...], dhg_ref, dwg_ref, dbg_ref)

    col = lambda rows, off: pl.BlockSpec((rows, FF_TILE), lambda j: (0, j + off))
    hshape = jax.ShapeDtypeStruct((SEQ, D_FF), BF16)
    wshape = jax.ShapeDtypeStruct((3, D_FF), F32)
    bshape = jax.ShapeDtypeStruct((1, D_FF), F32)
    return pl.pallas_call(
        body, name="ffn_act_bwd", grid=(FF_TILES,),
        in_specs=[col(SEQ, 0), col(SEQ, FF_TILES), col(SEQ, 0), col(3, 0), col(3, FF_TILES), col(1, 0), col(1, FF_TILES)],
        out_specs=[col(SEQ, 0), col(SEQ, 0), col(3, 0), col(3, 0), col(1, 0), col(1, 0)],
        out_shape=[hshape, hshape, wshape, wshape, bshape, bshape],
        compiler_params=_params("parallel"),
    )(hpre, hpre, d_act, conv_w, conv_w, conv_b, conv_b)


def _norm_mod_bwd(x, dxn, gain, scale):
    rstd = lax.rsqrt(jnp.mean(x * x, axis=-1, keepdims=True) + EPS)
    nrm = x * rstd
    dsh = jnp.sum(dxn, axis=0, keepdims=True)
    dsc = jnp.sum(dxn * nrm, axis=0, keepdims=True) * gain
    dgn = jnp.sum(dxn * nrm, axis=0, keepdims=True) * (1.0 + scale)
    dn = dxn * (gain * (1.0 + scale))
    dx = rstd * (dn - nrm * jnp.mean(dn * nrm, axis=-1, keepdims=True))
    return dx, dsh, dsc, dgn


def ffn_up_bwd(dha, dhg, w_up, x1, dy, gain, scale):
    def body(dha_ref, dhg_ref, w_ref, x_ref, dy_ref, g_ref, sc_ref, dx_ref, dsh_ref, dsc_ref, dgn_ref):
        i = pl.program_id(0)
        dxn = _dot_nt(dha_ref[...], w_ref[:, :D_FF]) + _dot_nt(dhg_ref[...], w_ref[:, D_FF:])
        dx, dsh, dsc, dgn = _norm_mod_bwd(x_ref[...], dxn, g_ref[...], sc_ref[...])
        dx_ref[...] = dy_ref[...] + dx

        @pl.when(i == 0)
        def _():
            dsh_ref[...] = dsh
            dsc_ref[...] = dsc
            dgn_ref[...] = dgn

        @pl.when(i > 0)
        def _():
            dsh_ref[...] = dsh_ref[...] + dsh
            dsc_ref[...] = dsc_ref[...] + dsc
            dgn_ref[...] = dgn_ref[...] + dgn

    row = pl.BlockSpec((ROW_TILE, D_MODEL), lambda i: (i, 0))
    vec = _full((1, D_MODEL))
    vshape = jax.ShapeDtypeStruct((1, D_MODEL), F32)
    return pl.pallas_call(
        body, name="ffn_up_bwd", grid=(SEQ // ROW_TILE,),
        in_specs=[pl.BlockSpec((ROW_TILE, D_FF), lambda i: (i, 0)), pl.BlockSpec((ROW_TILE, D_FF), lambda i: (i, 0)),
                  _full((D_MODEL, 2 * D_FF)), row, row, vec, vec],
        out_specs=[row, vec, vec, vec],
        out_shape=[jax.ShapeDtypeStruct((SEQ, D_MODEL), F32), vshape, vshape, vshape],
        compiler_params=_params("arbitrary"),
    )(dha, dhg, w_up, x1, dy, gain, scale)


def merge_bwd(dx1, out, g2, p, u, v, w_rnn, w_na, w_out):
    def body(dx_ref, out_ref, g2_ref, mr_ref, mn_ref, u_ref, v_ref, wr_ref, wn_ref, wo_ref,
             dout_ref, du_ref, dv_ref, dmr_ref, dmn_ref, dyr_ref, dyn_ref, dg2_ref):
        i = pl.program_id(0)

        @pl.when(i == 0)
        def _():
            dmr_ref[...] = jnp.zeros_like(dmr_ref)
            dmn_ref[...] = jnp.zeros_like(dmn_ref)
            dg2_ref[...] = jnp.zeros_like(dg2_ref)

        @pl.when(i > 0)
        def _():
            dx = dx_ref[...]
            dg2_ref[...] = dg2_ref[...] + jnp.sum(dx * out_ref[...], axis=0, keepdims=True)
            dout = (dx * g2_ref[...]).astype(BF16)
            dout_ref[...] = dout
            dm = _dot_nt(dout, wo_ref[...])
            sr = _sigmoid(mr_ref[...])
            sn = _sigmoid(mn_ref[...])
            du = (dm * sr).astype(BF16)
            dv = (dm * sn).astype(BF16)
            du_ref[...] = du
            dv_ref[...] = dv
            dmr_ref[...] = (dm * u_ref[...] * (sr * (1.0 - sr))).astype(BF16)
            dmn_ref[...] = (dm * v_ref[...] * (sn * (1.0 - sn))).astype(BF16)
            dyr_ref[...] = _dot_nt(du, wr_ref[...])
            dyn_ref[...] = _dot_nt(dv, wn_ref[...])

    lat = pl.BlockSpec((ROW_TILE, D_MODEL), lambda i: (jnp.maximum(i - 1, 0), 0))
    zrow = pl.BlockSpec((ROW_TILE, D_MODEL), lambda i: (i, 0))
    pcol = lambda cb: pl.BlockSpec((ROW_TILE, D_MODEL), lambda i: (i, cb))
    wspec = _full((D_MODEL, D_MODEL))
    tb = jax.ShapeDtypeStruct((SEQ, D_MODEL), BF16)
    zb = jax.ShapeDtypeStruct((ZLEN, D_MODEL), BF16)
    tf = jax.ShapeDtypeStruct((SEQ, D_MODEL), F32)
    return pl.pallas_call(
        body, name="merge_bwd", grid=(ZLEN // ROW_TILE,),
        in_specs=[lat, lat, _full((1, D_MODEL)), pcol(5), pcol(6), lat, lat, wspec, wspec, wspec],
        out_specs=[lat, lat, lat, zrow, zrow, lat, lat, _full((1, D_MODEL))],
        out_shape=[tb, tb, tb, zb, zb, tf, tf, jax.ShapeDtypeStruct((1, D_MODEL), F32)],
        compiler_params=_params("arbitrary"),
    )(dx1, out, g2, p, p, u, v, w_rnn, w_na, w_out)


def attn_bwd(q_rot, q_pl, kk, vv, bt, y_na, d_yna, lse, comm=None):
    def body(qr_ref, qp_ref, kk_ref, vv_ref, bt_ref, o_ref, do_ref, lse_ref,
             dqr_ref, dqp_ref, dk_ref, dv_ref, dbt_ref, s_ref):
        jj = pl.program_id(1)

        @pl.when(jj == 0)
        def _():
            dqr_ref[...] = jnp.zeros_like(dqr_ref)
            dqp_ref[...] = jnp.zeros_like(dqp_ref)
            dk_ref[...] = jnp.zeros_like(dk_ref)
            dv_ref[...] = jnp.zeros_like(dv_ref)
            dbt_ref[...] = jnp.zeros_like(dbt_ref)

        @pl.when(jj > 0)
        def _():
            j = jj - 1
            ws, start = _key_window(j)
            win = pl.ds(start, KEY_TILE)
            kw, kc = kk_ref[win, :], kk_ref[:CTX_LEN, :]
            vw, vc = vv_ref[win, :], vv_ref[:CTX_LEN, :]
            qr, qp = qr_ref[...], qp_ref[...]
            do = do_ref[...]
            do_o = do * o_ref[...]
            dq_r, dq_p = [], []
            for hh in range(2):
                msk = _head_mask(hh)
                q_r, q_p = jnp.where(msk, qr, 0), jnp.where(msk, qp, 0)
                base = _attn_scores(j, ws, q_r, q_p, kw, kc, hh, bt_ref, s_ref)
                pr = jnp.exp(s_ref[...] - lse_ref[hh])
                delta = jnp.sum(jnp.where(msk, do_o, 0.0), axis=-1, keepdims=True)
                dob = jnp.where(msk, do, 0.0).astype(BF16)
                ds_lat = pr[:, :KEY_TILE] * (_dot_nt(dob, vw) - delta)
                ds_ctx = pr[:, KEY_TILE:] * (_dot_nt(dob, vc) - delta)
                for qi in range(Q_ROWS):
                    for m in range(KEY_ROWS // 2):
                        idx = base + 2 * m - qi
                        dbt_ref[hh, idx] = dbt_ref[hh, idx] + ds_lat[qi * GRID_W:(qi + 1) * GRID_W, 128 * m:128 * (m + 1)]
                dsb_lat = ds_lat.astype(BF16)
                dsb_ctx = ds_ctx.astype(BF16)
                prb = pr.astype(BF16)
                dq_r.append(jnp.dot(dsb_lat, kw, preferred_element_type=F32))
                dq_p.append(jnp.dot(dsb_ctx, kc, preferred_element_type=F32))
                dk_ref[win, :] = dk_ref[win, :] + _dot_tn(dsb_lat, q_r)
                dk_ref[:CTX_LEN, :] = dk_ref[:CTX_LEN, :] + _dot_tn(dsb_ctx, q_p)
                dv_ref[win, :] = dv_ref[win, :] + _dot_tn(prb[:, :KEY_TILE], dob)
                dv_ref[:CTX_LEN, :] = dv_ref[:CTX_LEN, :] + _dot_tn(prb[:, KEY_TILE:], dob)
            dqr_ref[...] = jnp.where(_head_mask(0), dq_r[0], dq_r[1])
            dqp_ref[...] = jnp.where(_head_mask(0), dq_p[0], dq_p[1])

    lat = lambda jj: jnp.maximum(jj - 1, 0)
    qspec = pl.BlockSpec((Q_TILE, 128), lambda hp, jj: (lat(jj) + 1, hp))
    kspec = pl.BlockSpec((ZLEN, 128), lambda hp, jj: (0, hp))
    btspec = pl.BlockSpec((2, BT_LEN, GRID_W, 128), lambda hp, jj: (hp, 0, 0, 0))
    ospec = pl.BlockSpec((Q_TILE, 128), lambda hp, jj: (lat(jj), hp))
    dqspec = pl.BlockSpec((Q_TILE, 128), lambda hp, jj: (jj, hp))
    zshape = jax.ShapeDtypeStruct((ZLEN, D_MODEL), F32)
    res, extra = _call(
        body, name="attn_bwd", grid=(NA_HEADS // 2, ZLEN // Q_TILE),
        in_specs=[qspec, qspec, kspec, kspec, btspec, ospec, ospec,
                  pl.BlockSpec((2, Q_TILE, 1), lambda hp, jj: (hp, lat(jj), 0))],
        out_specs=[dqspec, dqspec, kspec, kspec, btspec],
        out_shape=[zshape, zshape, zshape, zshape, jax.ShapeDtypeStruct((NA_HEADS, BT_LEN, GRID_W, 128), F32)],
        scratch_shapes=[pltpu.VMEM((Q_TILE, KEY_TILE + CTX_LEN), F32)], sem=("parallel", "arbitrary"),
        args=(q_rot, q_pl, kk, vv, bt, y_na, d_yna, lse), comm=comm)
    return (*res, extra)


def qkv_bwd(dq_rot, dq_pl, dk, dv, p, qg2, kg2, cos, sin, ones, comm=None):
    scale = HEAD_DIM ** -0.5
    n_hp, n_i = NA_HEADS // 2, ZLEN // PREP_TILE

    def norm_rope_bwd(d_rot, d_extra, x, gain, cos_t, sin_t, ones_m, dx_ref, acc_ref):
        xh, rstd = _head_rms(x, ones_m, 1.0)
        dn = d_rot * cos_t + _rope_partner(d_rot * sin_t)
        if d_extra is not None:
            dn = (dn + d_extra) * scale
        acc_ref[...] = acc_ref[...] + jnp.sum(dn * xh, axis=0, keepdims=True)
        dxh = dn * gain
        seg = jnp.dot(dxh * xh, ones_m, preferred_element_type=F32, precision=lax.Precision.HIGHEST) * (1.0 / HEAD_DIM)
        dx_ref[...] = (rstd * (dxh - xh * seg)).astype(BF16)

    def body(dqr_ref, dqp_ref, dk_ref, dv_ref, xq_ref, xk_ref, qg_ref, kg_ref, cos_ref, sin_ref, ones_ref,
             dxq_ref, dxk_ref, dxv_ref, dgq_ref, dgk_ref, accq_ref, acck_ref):
        hp, i = pl.program_id(0), pl.program_id(1)

        @pl.when((hp == 0) & (i == 0))
        def _():
            accq_ref[...] = jnp.zeros_like(accq_ref)
            acck_ref[...] = jnp.zeros_like(acck_ref)

        ones_m = ones_ref[...]
        cos_t, sin_t = cos_ref[...], sin_ref[...]
        norm_rope_bwd(dqr_ref[...], dqp_ref[...], xq_ref[...], qg_ref[...], cos_t, sin_t, ones_m, dxq_ref, accq_ref)
        norm_rope_bwd(dk_ref[...], None, xk_ref[...], kg_ref[...], cos_t, sin_t, ones_m, dxk_ref, acck_ref)
        dxv_ref[...] = dv_ref[...].astype(BF16)

        @pl.when((hp == n_hp - 1) & (i == n_i - 1))
        def _():
            dgq_ref[...] = accq_ref[:, :HEAD_DIM] + accq_ref[:, HEAD_DIM:]
            dgk_ref[...] = acck_ref[:, :HEAD_DIM] + acck_ref[:, HEAD_DIM:]

    col = lambda base: pl.BlockSpec((PREP_TILE, 128), lambda hp, i: (i, base + hp))
    small = pl.BlockSpec((1, 128), lambda hp, i: (0, 0))
    tab = pl.BlockSpec((PREP_TILE, 128), lambda hp, i: (i, 0))
    zb = jax.ShapeDtypeStruct((ZLEN, D_MODEL), BF16)
    gshape = jax.ShapeDtypeStruct((1, HEAD_DIM), F32)
    res, extra = _call(
        body, name="qkv_bwd", grid=(n_hp, n_i),
        in_specs=[col(0)] * 4 + [col(32), col(8), small, small, tab, tab, _full((128, 128))],
        out_specs=[col(0)] * 3 + [_full((1, HEAD_DIM))] * 2,
        out_shape=[zb, zb, zb, gshape, gshape],
        scratch_shapes=[pltpu.VMEM((1, 128), F32)] * 2, sem=("arbitrary", "arbitrary"),
        args=(dq_rot, dq_pl, dk, dv, p, p, qg2, kg2, cos, sin, ones), comm=comm)
    return (*res, extra)


def rpb_grad(dbt):
    e, _ = _bias_expand()
    n_dr = 2 * NA_ROWS - 1
    ea = np.zeros((31, GRID_W, 128), np.float32)
    ea[:, :, :GRID_W] = e
    eb = np.zeros((31, GRID_W, 128), np.float32)
    eb[:, :, GRID_W:] = e
    eat = jnp.asarray(ea.reshape(31, GRID_W * 128).T.copy())
    ebt = jnp.asarray(eb.reshape(31, GRID_W * 128).T.copy())
    sel_at = np.zeros((n_dr, BT_LEN), np.float32)
    sel_bt = np.zeros((n_dr, BT_LEN), np.float32)
    for r in range(BT_LEN):
        dr = r - BT_PAD
        if 0 <= dr < n_dr:
            sel_at[dr, r] = 1.0
        if 0 <= dr + 1 < n_dr:
            sel_bt[dr + 1, r] = 1.0
    hi = lax.Precision.HIGHEST

    tk = 2048
    wide = GRID_W * 128
    rows = NA_HEADS * BT_LEN
    n_k = wide // tk

    def body(d_ref, sa_ref, sb_ref, ea_ref, eb_ref, o_ref, a_s, b_s):
        k = pl.program_id(0)
        dm = d_ref[...]
        a = jnp.dot(dm, ea_ref[...], preferred_element_type=F32, precision=hi)
        b = jnp.dot(dm, eb_ref[...], preferred_element_type=F32, precision=hi)

        @pl.when(k == 0)
        def _():
            a_s[...] = a
            b_s[...] = b

        @pl.when(k > 0)
        def _():
            a_s[...] = a_s[...] + a
            b_s[...] = b_s[...] + b

        @pl.when(k == n_k - 1)
        def _():
            for h in range(NA_HEADS):
                sl = slice(h * BT_LEN, (h + 1) * BT_LEN)
                o_ref[h] = (jnp.dot(sa_ref[...], a_s[sl, :], preferred_element_type=F32, precision=hi)
                            + jnp.dot(sb_ref[...], b_s[sl, :], preferred_element_type=F32, precision=hi))

    return pl.pallas_call(
        body, name="rpb_grad", grid=(n_k,),
        in_specs=[pl.BlockSpec((rows, tk), lambda k: (0, k)), _full((n_dr, BT_LEN)), _full((n_dr, BT_LEN)),
                  pl.BlockSpec((tk, 31), lambda k: (k, 0)), pl.BlockSpec((tk, 31), lambda k: (k, 0))],
        out_specs=_full((NA_HEADS, n_dr, 31)),
        out_shape=jax.ShapeDtypeStruct((NA_HEADS, n_dr, 31), F32),
        scratch_shapes=[pltpu.VMEM((rows, 31), F32), pltpu.VMEM((rows, 31), F32)],
        compiler_params=_params("arbitrary"),
    )(dbt.reshape(rows, wide), jnp.asarray(sel_at), jnp.asarray(sel_bt), eat, ebt)


def lru_bwd(p, d_yrnn, conv_w, conv_b, wa, ba, wx, bx, lam, comm=None):
    blk, wspec = _lru_in_specs()

    def body(xr_ref, gx_ref, dy_ref, cw_ref, cb_ref, wa_ref, ba_ref, wx_ref, bx_ref, lam_ref,
             dxr_ref, dgx_ref, dcw_ref, dcb_ref, dwa_ref, dba_ref, dwx_ref, dbx_ref, dlam_ref,
             a_s, b_s, h_s, l_s, hsum_s, dxc_s, dh_s):
        xr = xr_ref[...]
        cw = cw_ref[...]
        xc = _lru_conv(xr, cw, cb_ref[...])
        xcb = xc.astype(BF16)
        g, dg = _gelu_parts(gx_ref[CTX_LEN:, :])
        dy = dy_ref[...]
        dh_s[:CTX_LEN, :] = jnp.zeros((CTX_LEN, LRU_BLOCK_W), F32)
        dh_s[CTX_LEN:, :] = dy * g
        row = _row_ids(ZLEN, LRU_BLOCK_W)
        zero = jnp.zeros((1, LRU_BLOCK_W), F32)
        for d in (0, 1):
            wab = wa_ref[d, 0].astype(BF16)
            wxb = wx_ref[d, 0].astype(BF16)
            lam_d = lam_ref[d:d + 1, :]
            r, gi, sp, a, sq, b = _lru_gates(xc, xcb, wab, ba_ref[d:d + 1, :], wxb, bx_ref[d:d + 1, :], lam_d)
            a_s[...] = a
            b_s[...] = b
            _lru_scan_dir(d, a_s, b_s, h_s)
            h = h_s[...]
            if d == 0:
                hsum_s[...] = h
                h_prev = jnp.where(row >= 1, pltpu.roll(h, 1, 0), 0.0)
                a_s[...] = pltpu.roll(a, ZLEN - 1, 0)
                _scan_down(a_s, dh_s, l_s, 0, N_CHUNK, zero)
            else:
                hsum_s[...] = hsum_s[...] + h
                h_prev = jnp.where(row == CTX_LEN - 1, 0.0, pltpu.roll(h, ZLEN - 1, 0))
                a_s[...] = pltpu.roll(a, 1, 0)
                c = _scan_up(a_s, dh_s, l_s, CTX_CHUNKS, N_CHUNK, zero)
                _scan_up(a_s, dh_s, l_s, 0, CTX_CHUNKS, c)
            db = l_s[...]
            da = db * h_prev
            dsq = db * gi * xc
            dgi = db * sq * xc
            dxc_d = db * sq * gi
            dla = da * a - dsq * (a * a) / sq
            dr = dla * ((-LRU_C) * sp)
            dsp = jnp.sum(dla * ((-LRU_C) * r), axis=0, keepdims=True)
            dlam_ref[d:d + 1, :] = -dsp * _sigmoid(-lam_d)
            dzr = dr * r * (1.0 - r)
            dzi = dgi * gi * (1.0 - gi)
            dba_ref[d:d + 1, :] = jnp.sum(dzr, axis=0, keepdims=True)
            dbx_ref[d:d + 1, :] = jnp.sum(dzi, axis=0, keepdims=True)
            dzrb = dzr.astype(BF16)
            dzib = dzi.astype(BF16)
            dwa_ref[d, 0] = _dot_tn(xcb, dzrb)
            dwx_ref[d, 0] = _dot_tn(xcb, dzib)
            dxc_d = dxc_d + _dot_nt(dzrb, wab) + _dot_nt(dzib, wxb)
            if d == 0:
                dxc_s[...] = dxc_d
            else:
                dxc_s[...] = dxc_s[...] + dxc_d
        dxc = dxc_s[...]
        dxr_ref[...] = _lru_conv_t(dxc, cw).astype(BF16)
        dcb_ref[...] = jnp.sum(dxc, axis=0, keepdims=True)
        segpos = jnp.where(row < CTX_LEN, row, row - CTX_LEN)
        seglen = jnp.where(row < CTX_LEN, CTX_LEN, SEQ)
        for k in range(4):
            off = k - 2
            if off == 0:
                sh = xr
            else:
                ok = (segpos + off >= 0) & (segpos + off < seglen)
                sh = jnp.where(ok, pltpu.roll(xr, (-off) % ZLEN, 0), 0.0)
            dcw_ref[k:k + 1, :] = jnp.sum(dxc * sh, axis=0, keepdims=True)
        dgx_ref[:CTX_LEN, :] = jnp.zeros((CTX_LEN, LRU_BLOCK_W), BF16)
        dgx_ref[CTX_LEN:, :] = (dy * hsum_s[CTX_LEN:, :] * dg).astype(BF16)

    zs = pltpu.VMEM((ZLEN, LRU_BLOCK_W), F32)
    zb = jax.ShapeDtypeStruct((ZLEN, D_MODEL), BF16)
    v2 = jax.ShapeDtypeStruct((2, D_MODEL), F32)
    w4 = jax.ShapeDtypeStruct((2, LRU_BLOCKS, LRU_BLOCK_W, LRU_BLOCK_W), F32)
    res, extra = _call(
        body, name="lru_bwd", grid=(LRU_BLOCKS,),
        in_specs=[blk(ZLEN), pl.BlockSpec((ZLEN, LRU_BLOCK_W), lambda b: (0, 24 + b)), blk(SEQ), blk(4), blk(1),
                  wspec, blk(2), wspec, blk(2), blk(2)],
        out_specs=[blk(ZLEN), blk(ZLEN), blk(4), blk(1), wspec, blk(2), wspec, blk(2), blk(2)],
        out_shape=[zb, zb, jax.ShapeDtypeStruct((4, D_MODEL), F32), jax.ShapeDtypeStruct((1, D_MODEL), F32),
                   w4, v2, w4, v2, v2],
        scratch_shapes=[zs] * 7, sem=("arbitrary",),
        args=(p, p, d_yrnn, conv_w, conv_b, wa, ba, wx, bx, lam), comm=comm)
    return (*res, extra)


def in_proj_bwd(dgs, w_in, z, dx1, gain, scale, comm=None):
    def body(*refs):
        dg_refs = refs[:7]
        w_ref, z_ref, dx1_ref, g_ref, sc_ref, gx_ref, dsh_ref, dsc_ref, dgn_ref = refs[7:]
        i = pl.program_id(0)
        dxn = _dot_nt(dg_refs[0][...], w_ref[:, 0:D_MODEL])
        for g in range(1, 7):
            dxn = dxn + _dot_nt(dg_refs[g][...], w_ref[:, g * D_MODEL:(g + 1) * D_MODEL])
        dx, dsh, dsc, dgn = _norm_mod_bwd(z_ref[...], dxn, g_ref[...], sc_ref[0])

        @pl.when(i <= 1)
        def _():
            dsh_ref[0] = dsh
            dsc_ref[0] = dsc

        @pl.when(i > 1)
        def _():
            dsh_ref[0] = dsh_ref[0] + dsh
            dsc_ref[0] = dsc_ref[0] + dsc

        @pl.when(i == 0)
        def _():
            dgn_ref[...] = dgn

        @pl.when(i > 0)
        def _():
            dgn_ref[...] = dgn_ref[...] + dgn
            gx_ref[...] = dx1_ref[...] + dx

    zrow = pl.BlockSpec((ROW_TILE, D_MODEL), lambda i: (i, 0))
    lat = pl.BlockSpec((ROW_TILE, D_MODEL), lambda i: (jnp.maximum(i - 1, 0), 0))
    mod = pl.BlockSpec((1, 1, D_MODEL), lambda i: (jnp.minimum(i, 1), 0, 0))
    mshape = jax.ShapeDtypeStruct((2, 1, D_MODEL), F32)
    res, extra = _call(
        body, name="in_proj_bwd", grid=(ZLEN // ROW_TILE,),
        in_specs=[zrow] * 7 + [_full((D_MODEL, IN_COLS)), zrow, lat, _full((1, D_MODEL)), mod],
        out_specs=[lat, mod, mod, _full((1, D_MODEL))],
        out_shape=[jax.ShapeDtypeStruct((SEQ, D_MODEL), F32), mshape, mshape, jax.ShapeDtypeStruct((1, D_MODEL), F32)],
        sem=("arbitrary",), args=(*dgs, w_in, z, dx1, gain, scale), comm=comm)
    return (*res, extra)


def matmul_tn(a, b, name, tm, tn, prev=None, col_block=0, total_cols=None):
    k, m = a.shape
    n = b.shape[1]
    total_cols = n if total_cols is None else total_cols
    assert m % tm == 0 and n % tn == 0
    off = col_block * (n // tn)

    def body(a_ref, b_ref, *rest):
        rest[-1][...] = _dot_tn(a_ref[...].astype(BF16), b_ref[...]).astype(BF16)

    in_specs = [pl.BlockSpec((k, tm), lambda i, j: (0, i)), pl.BlockSpec((k, tn), lambda i, j: (0, j))]
    args = [a, b]
    aliases = {}
    if prev is not None:
        in_specs.append(pl.BlockSpec(memory_space=pl.ANY))
        args.append(prev)
        aliases = {2: 0}
    return pl.pallas_call(
        body, name=name, grid=(m // tm, n // tn), in_specs=in_specs,
        out_specs=pl.BlockSpec((tm, tn), lambda i, j: (i, j + off)),
        out_shape=jax.ShapeDtypeStruct((m, total_cols), BF16),
        input_output_aliases=aliases,
        compiler_params=_params("parallel", "parallel"),
    )(*args)


def local_step(z, target, modx, modc, norm_mix_g, norm_ffn_g, w_in, conv_w, conv_b, wa, ba, wx, bx, lam, qg, kg, rpb,
               w_rnn, w_na, w_out, w_up, fconv_w, fconv_b, w_down, idx=None):
    dist = idx is not None
    c_idx = idx[1:2] if dist else None
    d = D_MODEL
    mx = [modx[:, k * d:(k + 1) * d] for k in range(N_MOD)]
    shift = jnp.stack([modc[:, 0:d], mx[0]])
    scale = jnp.stack([modc[:, d:2 * d], mx[1]])
    cos, sin = _rope_tables()
    ones = _head_ones()
    qg2 = jnp.tile(qg, (1, 2))
    kg2 = jnp.tile(kg, (1, 2))

    xn = norm_mod(z, norm_mix_g, shift, scale, "norm_mix")
    if dist:
        p, w_in, _ = in_proj_gather(xn, w_in, idx, Comm([], [], {}, [], lambda *_: (lambda: None,) * 3))
    else:
        p, _ = matmul_wide(xn, w_in, "in_proj", 3 * ROW_TILE, 1792)
    y_rnn, got = lru_fwd(p, conv_w, conv_b, wa, ba, wx, bx, lam,
                         comm=gather_weights_comm([w_down], [5]) if dist else None)
    if dist:
        w_down = got[0]
    q_rot, q_pl, kk, vv, got = qkv_prep(p, qg2, kg2, cos, sin, ones,
                                        comm=gather_weights_comm([w_rnn, w_na, w_out], [1, 2, 3]) if dist else None)
    if dist:
        w_rnn, w_na, w_out = got
    bt, _ = bias_table(rpb)
    y_na, lse, got = attn_fwd(q_rot, q_pl, kk, vv, bt, comm=gather_weights_comm([w_up], [4]) if dist else None)
    if dist:
        w_up = got[0]
    u, v, merged, out, x1 = merge_fwd(y_rnn, y_na, p, z, mx[2], w_rnn, w_na, w_out)
    xn2 = norm_mod(x1, norm_ffn_g, mx[3][None], mx[4][None], "norm_ffn")
    hpre, _ = matmul_wide(xn2, w_up, "ffn_up", 2 * ROW_TILE, 1408)
    act = ffn_act(hpre, fconv_w, fconv_b)
    f, dy, df, loss_sq, dg5 = ffn_down_loss(act, w_down, x1, mx[5], target)

    partials, pieces = {}, {}

    def chip_partials(which, grads, tag):
        views = [_grad_view(g, BIG[w][1], BIG[w][2]) for w, g in zip(which, grads)]
        recv = exchange_halves(views, "grad_exchange_" + tag)
        for w, gv, r in zip(which, views, recv):
            partials[w] = add_halves(gv, r, c_idx, "add_halves_" + BIG[w][0])
        return scatter_pieces_comm([partials[w] for w in which], which)

    d_act = ffn_down_bwd(df, w_down)
    dha, dhg, d_fcw_a, d_fcw_g, d_fcb_a, d_fcb_g = ffn_act_bwd(hpre, d_act, fconv_w, fconv_b)
    d_fcw = jnp.concatenate([d_fcw_a, d_fcw_g], axis=1)
    d_fcb = jnp.concatenate([d_fcb_a, d_fcb_g], axis=1)
    dx1, d_s3, d_s4, d_gffn = ffn_up_bwd(dha, dhg, w_up, x1, dy, norm_ffn_g, mx[4])
    g_w_down = matmul_tn(act, df, "gw_down", 256, D_MODEL)
    g_w_up = matmul_tn(xn2, dha, "gw_up_a", 512, 1408, total_cols=2 * D_FF)
    g_w_up = matmul_tn(xn2, dhg, "gw_up_g", 512, 1408, prev=g_w_up, col_block=1, total_cols=2 * D_FF)
    dout, du, dv, dmr, dmn, dyr, dyn, dg2 = merge_bwd(dx1, out, mx[2], p, u, v, w_rnn, w_na, w_out)
    g_w_out = matmul_tn(merged, dout, "gw_out", 1024, 512)
    g_w_rnn = matmul_tn(y_rnn, du, "gw_rnn", 1024, 512)
    g_w_na = matmul_tn(y_na, dv, "gw_na", 1024, 512)
    *lru_grads, got = lru_bwd(p, dyr, conv_w, conv_b, wa, ba, wx, bx, lam,
                              comm=chip_partials([4, 5], [g_w_up, g_w_down], "ffn") if dist else None)
    dxr, dgx, d_cw, d_cb, d_wa, d_ba, d_wx, d_bx, d_lam = lru_grads
    if dist:
        pieces[4], pieces[5] = got
    lru_w_all = {}
    dqr, dqp, dk, dvh, dbt, got = attn_bwd(
        q_rot, q_pl, kk, vv, bt, y_na, dyn, lse,
        comm=join_comms(chip_partials([1, 2, 3], [g_w_rnn, g_w_na, g_w_out], "mix"),
                        all_gather_comm(d_wa.reshape(-1, LRU_BLOCK_W))) if dist else None)
    if dist:
        pieces[1], pieces[2], pieces[3], lru_w_all["lru_wa"] = got
    dq_cols, dk_cols, dv_cols, d_qg, d_kg, got = qkv_bwd(
        dqr, dqp, dk, dvh, p, qg2, kg2, cos, sin, ones,
        comm=all_gather_comm(d_wx.reshape(-1, LRU_BLOCK_W)) if dist else None)
    if dist:
        lru_w_all["lru_wx"] = got[0]
    d_rpb = rpb_grad(dbt)
    dgs = [dxr, dk_cols, dv_cols, dgx, dq_cols, dmr, dmn]
    grad_x, dsh, dsc, d_gmix, _ = in_proj_bwd(dgs, w_in, z, dx1, norm_mix_g, scale)
    g_w_in = None
    for g in range(7):
        g_w_in = matmul_tn(xn, dgs[g], "gw_in_%d" % g, 1024, 512, prev=g_w_in, col_block=g, total_cols=IN_COLS)
    if dist:
        pieces[0] = run_comm(chip_partials([0], [g_w_in], "w_in"), "grad_scatter_w_in")[0]

    d_modx = jnp.concatenate([dsh[1], dsc[1], dg2, d_s3, d_s4, dg5], axis=1)
    d_modc = jnp.concatenate([dsh[0], dsc[0]], axis=1)
    return dict(loss_sq=loss_sq, grad_x=grad_x, d_modx=d_modx, d_modc=d_modc, norm_mix_g=d_gmix, norm_ffn_g=d_gffn,
                w_in=g_w_in, lru_conv_w=d_cw, lru_conv_b=d_cb, lru_wa=d_wa, lru_ba=d_ba, lru_wx=d_wx, lru_bx=d_bx,
                lru_lambda=d_lam, q_norm_g=d_qg, k_norm_g=d_kg, na_rpb=d_rpb, w_rnn_out=g_w_rnn, w_na_out=g_w_na,
                w_out=g_w_out, w_up=g_w_up, ffn_conv_w=d_fcw, ffn_conv_b=d_fcb, w_down=g_w_down,
                partials=partials, pieces=pieces, lru_w_all=lru_w_all)


def _mesh_pos():
    return lax.axis_index("x"), lax.axis_index("y"), lax.axis_index("c")


def _other_chips(x, y):
    return [(1 - x, y), (x, 1 - y), (1 - x, 1 - y)]


def all_gather8(xs, name, with_sum=False):
    m, n = xs.shape
    assert m % 8 == 0

    def body(x_ref, out_ref, *rest):
        if with_sum:
            sum_ref, send_sems, recv_sems, local_sem = rest
        else:
            send_sems, recv_sems, local_sem = rest
        x, y, c = _mesh_pos()
        me, sibling = (x, y, c), (x, y, 1 - c)
        chips = _other_chips(x, y)

        def rows(px, py, pc):
            return out_ref.at[pl.ds((4 * px + 2 * py + pc) * m, m), :]

        def copy(k, block, to, src=None):
            return pltpu.make_async_remote_copy(
                src_ref=rows(*block) if src is None else src, dst_ref=rows(*block),
                send_sem=send_sems.at[k], recv_sem=recv_sems.at[k], device_id=to, device_id_type=MESH_T)

        mine = pltpu.make_async_copy(x_ref, rows(*me), local_sem)
        mine.start()
        first = [copy(0, me, sibling, src=x_ref)]
        first += [copy(1 + j, me, (*chip, c), src=x_ref) for j, chip in enumerate(chips)]
        for cp in first:
            cp.start()
        passed = [copy(4 + j, (*chip, c), sibling) for j, chip in enumerate(chips)]
        for j, chip in enumerate(chips):
            copy(1 + j, (*chip, c), me).wait_recv()
            passed[j].start()
        copy(0, sibling, me).wait_recv()
        for j, chip in enumerate(chips):
            copy(4 + j, (*chip, 1 - c), me).wait_recv()
        for cp in first + passed:
            cp.wait_send()
        mine.wait()
        if with_sum:
            acc = out_ref[0:m, :]
            for k in range(1, N_DEV):
                acc = acc + out_ref[k * m:(k + 1) * m, :]
            sum_ref[...] = acc

    vm = pl.BlockSpec(memory_space=pltpu.VMEM)
    out_shape = [jax.ShapeDtypeStruct((N_DEV * m, n), F32)]
    if with_sum:
        out_shape.append(jax.ShapeDtypeStruct((m, n), F32))
    res = pl.pallas_call(
        body, name=name, in_specs=[vm], out_specs=[vm] * len(out_shape), out_shape=out_shape,
        scratch_shapes=[pltpu.SemaphoreType.DMA((7,)), pltpu.SemaphoreType.DMA((7,)), pltpu.SemaphoreType.DMA],
        compiler_params=pltpu.CompilerParams(vmem_limit_bytes=VMEM_LIMIT_V7X),
    )(xs)
    return res if with_sum else res[0]


BIG = (("w_in", (D_MODEL, IN_COLS), 1), ("w_rnn_out", (D_MODEL, D_MODEL), 0), ("w_na_out", (D_MODEL, D_MODEL), 0),
       ("w_out", (D_MODEL, D_MODEL), 0), ("w_up", (D_MODEL, 2 * D_FF), 1), ("w_down", (D_FF, D_MODEL), 0))


def _shard_shape(full, axis):
    r, c = full
    return (r // N_SHARD, c) if axis == 0 else (r, c // N_SHARD)


def _slot(ref, full, axis, s, h):
    r, c = full
    if axis == 0:
        rs = r // N_SHARD
        return ref.at[pl.ds(s * rs + h * (rs // 2), rs // 2), :]
    cs = c // N_SHARD
    return ref.at[pl.ds(h * (r // 2), r // 2), pl.ds(s * cs, cs)]


def cast_into_full(x, full, axis, idx, name):
    r, c = x.shape
    tr = next(t for t in (512, 352, 256, 128) if r % t == 0)
    nb = r // tr

    def body(idx_ref, x_ref, o_ref):
        o_ref[...] = x_ref[...].astype(BF16)

    if axis == 0:
        out_spec = pl.BlockSpec((tr, c), lambda i, idx_ref: (idx_ref[0] * nb + i, 0))
    else:
        out_spec = pl.BlockSpec((tr, c), lambda i, idx_ref: (i, idx_ref[0]))
    return pl.pallas_call(
        body, name=name,
        grid_spec=pltpu.PrefetchScalarGridSpec(
            num_scalar_prefetch=1, grid=(nb,), in_specs=[pl.BlockSpec((tr, c), lambda i, idx_ref: (i, 0))],
            out_specs=out_spec),
        out_shape=jax.ShapeDtypeStruct(full, BF16),
        compiler_params=_params("parallel"),
    )(idx, x)


def run_comm(comm, name):
    k_in, k_out = len(comm.inputs), len(comm.out_shapes)

    def body(*refs):
        start, mid, end = comm.emit(refs[:k_in], refs[k_in:k_in + k_out], refs[k_in + k_out:])
        start()
        mid()
        end()

    hbm = pl.BlockSpec(memory_space=pl.ANY)
    return pl.pallas_call(
        body, name=name, in_specs=[hbm] * k_in, out_specs=[hbm] * k_out, out_shape=list(comm.out_shapes),
        input_output_aliases=dict(comm.aliases), scratch_shapes=list(comm.scratch),
        compiler_params=pltpu.CompilerParams(vmem_limit_bytes=VMEM_LIMIT_V7X),
    )(*comm.inputs)


def gather_weights_comm(fulls, which):
    nw = len(which)
    specs = [BIG[w] for w in which]

    def emit(_, outs, sems):
        send1, recv1, send2, recv2 = sems
        x, y, c = _mesh_pos()
        sibling = (x, y, 1 - c)
        chips = _other_chips(x, y)
        s_me = 2 * x + y
        shards = [2 * chip[0] + chip[1] for chip in chips]

        def ici(w, j, shard):
            _, full, axis = specs[w]
            dst = _slot(outs[w], full, axis, shard, c)
            return pltpu.make_async_remote_copy(
                src_ref=dst, dst_ref=dst, send_sem=send1.at[3 * w + j],
                recv_sem=recv1.at[3 * w + j], device_id=(*chips[j], c), device_id_type=MESH_T)

        def d2d(w, j, shard, half):
            _, full, axis = specs[w]
            dst = _slot(outs[w], full, axis, shard, half)
            return pltpu.make_async_remote_copy(
                src_ref=dst, dst_ref=dst, send_sem=send2.at[3 * w + j], recv_sem=recv2.at[3 * w + j],
                device_id=sibling, device_id_type=MESH_T)

        pairs = [(w, j) for w in range(nw) for j in range(3)]

        def start():
            for w, j in pairs:
                ici(w, j, s_me).start()

        def mid():
            for w, j in pairs:
                ici(w, j, shards[j]).wait_recv()
                d2d(w, j, shards[j], c).start()

        def end():
            for w, j in pairs:
                d2d(w, j, shards[j], 1 - c).wait_recv()
            for w, j in pairs:
                ici(w, j, s_me).wait_send()
                d2d(w, j, shards[j], c).wait_send()

        return start, mid, end

    return Comm(list(fulls), [jax.ShapeDtypeStruct(full, BF16) for _, full, _ in specs], {i: i for i in range(nw)},
                [pltpu.SemaphoreType.DMA((3 * nw,))] * 4, emit)


def in_proj_gather(xn, w_own, idx, extra):
    m, kdim = xn.shape
    _, full, axis = BIG[0]
    tn = full[1] // N_SHARD
    tm = 3 * ROW_TILE
    n_i = m // tm
    steps = N_SHARD * n_i
    rel = (1, 0, 2)
    k_in, k_out = len(extra.inputs), len(extra.out_shapes)

    def body(idx_ref, a_ref, w_in_ref, *rest):
        cins = rest[:k_in]
        o_ref, w_ref = rest[k_in], rest[k_in + 1]
        couts = rest[k_in + 2:k_in + 2 + k_out]
        wbuf, wsem, send1, recv1, send2, recv2 = rest[k_in + 2 + k_out:k_in + 8 + k_out]
        cscr = rest[k_in + 8 + k_out:]
        kk, i = pl.program_id(0), pl.program_id(1)
        lin = kk * n_i + i
        x, y, c = _mesh_pos()
        sibling = (x, y, 1 - c)
        chips = _other_chips(x, y)
        s_me = 2 * x + y
        shards = [2 * chip[0] + chip[1] for chip in chips]
        start_e, mid_e, end_e = extra.emit(cins, couts, cscr)

        def ici(j, shard):
            dst = _slot(w_ref, full, axis, shard, c)
            return pltpu.make_async_remote_copy(src_ref=dst, dst_ref=dst, send_sem=send1.at[j], recv_sem=recv1.at[j],
                                                device_id=(*chips[j], c), device_id_type=MESH_T)

        def d2d(j, shard, half):
            dst = _slot(w_ref, full, axis, shard, half)
            return pltpu.make_async_remote_copy(src_ref=dst, dst_ref=dst, send_sem=send2.at[j], recv_sem=recv2.at[j],
                                                device_id=sibling, device_id_type=MESH_T)

        @pl.when(lin == 0)
        def _():
            for j in range(3):
                ici(j, s_me).start()
            start_e()

        for step in (1, 2, 3):
            j = rel[step - 1]

            @pl.when((kk == step) & (i == 0))
            def _():
                ici(j, shards[j]).wait_recv()
                d2d(j, shards[j], c).start()
                d2d(j, shards[j], 1 - c).wait_recv()

        @pl.when(i == 0)
        def _():
            col = pl.multiple_of(jnp.bitwise_xor(s_me, kk) * tn, 128)
            cp = pltpu.make_async_copy(w_ref.at[:, pl.ds(col, tn)], wbuf, wsem)
            cp.start()
            cp.wait()

        o_ref[...] = jnp.dot(a_ref[...], wbuf[...], preferred_element_type=F32)
        pl.when(lin == steps - 1 - steps // 7)(mid_e)

        @pl.when(lin == steps - 1)
        def _():
            for j in range(3):
                ici(j, s_me).wait_send()
                d2d(j, shards[j], c).wait_send()
            end_e()

    hbm = pl.BlockSpec(memory_space=pl.ANY)
    dma = pltpu.SemaphoreType.DMA
    res = pl.pallas_call(
        body, name="in_proj",
        grid_spec=pltpu.PrefetchScalarGridSpec(
            num_scalar_prefetch=1, grid=(N_SHARD, n_i),
            in_specs=[pl.BlockSpec((tm, kdim), lambda kk, i, idx_ref: (i, 0)), hbm] + [hbm] * k_in,
            out_specs=[pl.BlockSpec((tm, tn), lambda kk, i, idx_ref: (i, jnp.bitwise_xor(idx_ref[0], kk))), hbm]
            + [hbm] * k_out,
            scratch_shapes=[pltpu.VMEM((kdim, tn), BF16), dma, dma((3,)), dma((3,)), dma((3,)), dma((3,))]
            + list(extra.scratch)),
        out_shape=[jax.ShapeDtypeStruct((m, full[1]), F32), jax.ShapeDtypeStruct(full, BF16)] + list(extra.out_shapes),
        input_output_aliases={2: 1, **{3 + i_: 2 + o_ for i_, o_ in extra.aliases.items()}},
        compiler_params=_params("arbitrary", "arbitrary"),
    )(idx, xn, w_own, *extra.inputs)
    return res[0], res[1], list(res[2:])


def join_comms(a, b):
    ai, ao, asc = len(a.inputs), len(a.out_shapes), len(a.scratch)

    def emit(ins, outs, sems):
        fa = a.emit(ins[:ai], outs[:ao], sems[:asc])
        fb = b.emit(ins[ai:], outs[ao:], sems[asc:])

        def both(k):
            def run():
                fa[k]()
                fb[k]()
            return run

        return both(0), both(1), both(2)

    aliases = dict(a.aliases)
    aliases.update({ai + i: ao + o for i, o in b.aliases.items()})
    return Comm(a.inputs + b.inputs, a.out_shapes + b.out_shapes, aliases, a.scratch + b.scratch, emit)


def all_gather_comm(x):
    def emit(srcs, outs, sems):
        send_sems, recv_sems, local_sem = sems
        x_ref, out_ref = srcs[0], outs[0]
        x, y, c = _mesh_pos()
        me, sibling = (x, y, c), (x, y, 1 - c)
        chips = _other_chips(x, y)

        def blk(px, py, pc):
            return out_ref.at[4 * px + 2 * py + pc]

        def copy(k, block, to, src=None):
            return pltpu.make_async_remote_copy(
                src_ref=blk(*block) if src is None else src, dst_ref=blk(*block),
                send_sem=send_sems.at[k], recv_sem=recv_sems.at[k], device_id=to, device_id_type=MESH_T)

        def mine():
            return pltpu.make_async_copy(x_ref, blk(*me), local_sem)

        def start():
            mine().start()
            copy(0, me, sibling, src=x_ref).start()
            for j, chip in enumerate(chips):
                copy(1 + j, me, (*chip, c), src=x_ref).start()

        def mid():
            for j, chip in enumerate(chips):
                copy(1 + j, (*chip, c), me).wait_recv()
                copy(4 + j, (*chip, c), sibling).start()

        def end():
            copy(0, sibling, me).wait_recv()
            for j, chip in enumerate(chips):
                copy(4 + j, (*chip, 1 - c), me).wait_recv()
            copy(0, me, sibling, src=x_ref).wait_send()
            for j, chip in enumerate(chips):
                copy(1 + j, me, (*chip, c), src=x_ref).wait_send()
                copy(4 + j, (*chip, c), sibling).wait_send()
            mine().wait()

        return start, mid, end

    return Comm([x], [jax.ShapeDtypeStruct((N_DEV,) + x.shape, F32)], {},
                [pltpu.SemaphoreType.DMA((7,)), pltpu.SemaphoreType.DMA((7,)), pltpu.SemaphoreType.DMA], emit)


def sum_blocks(g, name):
    _, r, c = g.shape
    tr = 256

    def body(g_ref, o_ref):
        acc = g_ref[0]
        for k in range(1, N_DEV):
            acc = acc + g_ref[k]
        o_ref[...] = acc

    return pl.pallas_call(
        body, name=name, grid=(r // tr,),
        in_specs=[pl.BlockSpec((N_DEV, tr, c), lambda i: (0, i, 0))],
        out_specs=pl.BlockSpec((tr, c), lambda i: (i, 0)),
        out_shape=jax.ShapeDtypeStruct((r, c), F32),
        compiler_params=_params("parallel"),
    )(g)


def _grad_view(g, full, axis):
    r, c = full
    if axis == 0:
        return g.reshape(N_SHARD, 2, r // N_SHARD // 2, c)
    return g.reshape(1, 2, r // 2, c)


def exchange_halves(gviews, name):
    nw = len(gviews)

    def body(*refs):
        srcs, outs = refs[:nw], refs[nw:2 * nw]
        send_sems, recv_sems = refs[2 * nw:]
        x, y, c = _mesh_pos()
        cps = []
        for w in range(nw):
            cp = pltpu.make_async_remote_copy(
                src_ref=srcs[w].at[:, pl.ds(1 - c, 1)], dst_ref=outs[w], send_sem=send_sems.at[w],
                recv_sem=recv_sems.at[w], device_id=(x, y, 1 - c), device_id_type=MESH_T)
            cp.start()
            cps.append(cp)
        for cp in cps:
            cp.wait()

    hbm = pl.BlockSpec(memory_space=pl.ANY)
    return pl.pallas_call(
        body, name=name, in_specs=[hbm] * nw, out_specs=[hbm] * nw,
        out_shape=[jax.ShapeDtypeStruct((g.shape[0], 1) + g.shape[2:], BF16) for g in gviews],
        scratch_shapes=[pltpu.SemaphoreType.DMA((nw,)), pltpu.SemaphoreType.DMA((nw,))],
        compiler_params=pltpu.CompilerParams(vmem_limit_bytes=VMEM_LIMIT_V7X),
    )(*gviews)


def _row_tile(rh):
    return 128 if rh % 128 == 0 else rh


def add_halves(gview, recv, c_idx, name):
    a, _, rh, cc = gview.shape
    tr = _row_tile(rh)

    def body(c_ref, g_ref, r_ref, o_ref):
        o_ref[0] = (g_ref[0, 0].astype(F32) + r_ref[0, 0].astype(F32)).astype(BF16)

    return pl.pallas_call(
        body, name=name,
        grid_spec=pltpu.PrefetchScalarGridSpec(
            num_scalar_prefetch=1, grid=(a, rh // tr),
            in_specs=[pl.BlockSpec((1, 1, tr, cc), lambda s, i, c_ref: (s, c_ref[0], i, 0)),
                      pl.BlockSpec((1, 1, tr, cc), lambda s, i, c_ref: (s, 0, i, 0))],
            out_specs=pl.BlockSpec((1, tr, cc), lambda s, i, c_ref: (s, i, 0))),
        out_shape=jax.ShapeDtypeStruct((a, rh, cc), BF16),
        compiler_params=_params("parallel", "parallel"),
    )(c_idx, gview, recv)


def _piece_shape(full, axis):
    rs, cs = _shard_shape(full, axis)
    return (rs // 2, cs)


def scatter_pieces_comm(partials, which):
    nw = len(which)
    specs = [BIG[w] for w in which]

    def emit(srcs, outs, sems):
        send_sems, recv_sems = sems
        x, y, c = _mesh_pos()
        chips = _other_chips(x, y)

        def copies():
            cps = []
            for w, (_, full, axis) in enumerate(specs):
                cs = full[1] // N_SHARD
                for j, chip in enumerate(chips):
                    s_j = 2 * chip[0] + chip[1]
                    src = srcs[w].at[s_j] if axis == 0 else srcs[w].at[0, :, pl.ds(s_j * cs, cs)]
                    cps.append(pltpu.make_async_remote_copy(
                        src_ref=src, dst_ref=outs[w].at[j], send_sem=send_sems.at[3 * w + j],
                        recv_sem=recv_sems.at[3 * w + j], device_id=(*chip, c), device_id_type=MESH_T))
            return cps

        def start():
            for cp in copies():
                cp.start()

        def mid():
            pass

        def end():
            for cp in copies():
                cp.wait()

        return start, mid, end

    return Comm(list(partials), [jax.ShapeDtypeStruct((3,) + _piece_shape(full, axis), BF16) for _, full, axis in specs],
                {}, [pltpu.SemaphoreType.DMA((3 * nw,)), pltpu.SemaphoreType.DMA((3 * nw,))], emit)


def add_pieces(partial, recv, idx, axis, name):
    _, rh, cs = recv.shape
    tr = _row_tile(rh)

    def body(idx_ref, p_ref, r_ref, o_ref):
        o_ref[0] = ((p_ref[0].astype(F32) + r_ref[0].astype(F32)) + r_ref[1].astype(F32)) + r_ref[2].astype(F32)

    if axis == 0:
        pspec = pl.BlockSpec((1, tr, cs), lambda i, idx_ref: (idx_ref[0], i, 0))
    else:
        pspec = pl.BlockSpec((1, tr, cs), lambda i, idx_ref: (0, i, idx_ref[0]))
    return pl.pallas_call(
        body, name=name,
        grid_spec=pltpu.PrefetchScalarGridSpec(
            num_scalar_prefetch=1, grid=(rh // tr,),
            in_specs=[pspec, pl.BlockSpec((3, tr, cs), lambda i, idx_ref: (0, i, 0))],
            out_specs=pl.BlockSpec((1, tr, cs), lambda i, idx_ref: (idx_ref[1], i, 0))),
        out_shape=jax.ShapeDtypeStruct((2, rh, cs), F32),
        compiler_params=_params("parallel"),
    )(idx, partial, recv)


def join_halves(halves):
    nw = len(BIG)

    def body(*refs):
        outs = refs[nw:2 * nw]
        send_sems, recv_sems = refs[2 * nw:]
        x, y, c = _mesh_pos()
        cps = []
        for w in range(nw):
            cp = pltpu.make_async_remote_copy(
                src_ref=outs[w].at[c], dst_ref=outs[w].at[c], send_sem=send_sems.at[w], recv_sem=recv_sems.at[w],
                device_id=(x, y, 1 - c), device_id_type=MESH_T)
            cp.start()
            cps.append(cp)
        for w in range(nw):
            cps[w].wait_send()
            pltpu.make_async_remote_copy(
                src_ref=outs[w].at[1 - c], dst_ref=outs[w].at[1 - c], send_sem=send_sems.at[w],
                recv_sem=recv_sems.at[w], device_id=(x, y, 1 - c), device_id_type=MESH_T).wait_recv()

    hbm = pl.BlockSpec(memory_space=pl.ANY)
    return pl.pallas_call(
        body, name="grad_join_halves", in_specs=[hbm] * nw, out_specs=[hbm] * nw,
        out_shape=[jax.ShapeDtypeStruct(h.shape, F32) for h in halves],
        input_output_aliases={i: i for i in range(nw)},
        scratch_shapes=[pltpu.SemaphoreType.DMA((nw,))] * 2,
        compiler_params=pltpu.CompilerParams(vmem_limit_bytes=VMEM_LIMIT_V7X),
    )(*halves)


MOD_COLS = N_MOD * D_MODEL // N_SHARD
MOD_TILE = 512


def mod_fwd(c16, w_mod):
    def body(c_ref, w_ref, s_ref, o_ref):
        cv = c_ref[...]
        s = cv * _sigmoid(cv)
        s_ref[...] = s
        o_ref[...] = jnp.dot(s.astype(BF16), w_ref[...].astype(BF16), preferred_element_type=F32)

    return pl.pallas_call(
        body, name="mod_fwd", grid=(MOD_COLS // MOD_TILE,),
        in_specs=[_full((16, D_MODEL)), pl.BlockSpec((D_MODEL, MOD_TILE), lambda j: (0, j))],
        out_specs=[_full((16, D_MODEL)), pl.BlockSpec((16, MOD_TILE), lambda j: (0, j))],
        out_shape=[jax.ShapeDtypeStruct((16, D_MODEL), F32), jax.ShapeDtypeStruct((16, MOD_COLS), F32)],
        compiler_params=_params("arbitrary"),
    )(c16, w_mod)


def mod_bwd(s16, dm16, w_mod):
    hi = lax.Precision.HIGHEST

    def body(s_ref, d_ref, w_ref, gw_ref, ds_ref):
        j = pl.program_id(0)
        dm = d_ref[...]
        gw_ref[...] = lax.dot_general(s_ref[...], dm, (((0,), (0,)), ((), ())), preferred_element_type=F32, precision=hi)
        part = lax.dot_general(dm, w_ref[...], (((1,), (1,)), ((), ())), preferred_element_type=F32, precision=hi)

        @pl.when(j == 0)
        def _():
            ds_ref[...] = part

        @pl.when(j > 0)
        def _():
            ds_ref[...] = ds_ref[...] + part

    return pl.pallas_call(
        body, name="mod_bwd", grid=(MOD_COLS // MOD_TILE,),
        in_specs=[_full((16, D_MODEL)), pl.BlockSpec((16, MOD_TILE), lambda j: (0, j)),
                  pl.BlockSpec((D_MODEL, MOD_TILE), lambda j: (0, j))],
        out_specs=[pl.BlockSpec((D_MODEL, MOD_TILE), lambda j: (0, j)), _full((16, D_MODEL))],
        out_shape=[jax.ShapeDtypeStruct((D_MODEL, MOD_COLS), F32), jax.ShapeDtypeStruct((16, D_MODEL), F32)],
        compiler_params=_params("arbitrary"),
    )(s16, dm16, w_mod)


def cctx_grad(parts, c_ctx):
    def body(p_ref, c_ref, o_ref):
        ds = p_ref[0:1, :]
        for s in range(1, N_SHARD):
            ds = ds + p_ref[16 * s:16 * s + 1, :]
        cv = c_ref[...]
        sg = _sigmoid(cv)
        o_ref[...] = ds * (sg * (1.0 + cv * (1.0 - sg)))

    return pl.pallas_call(
        body, name="cctx_grad", in_specs=[_full((N_DEV * 8, D_MODEL)), _full((1, D_MODEL))],
        out_specs=_full((1, D_MODEL)), out_shape=jax.ShapeDtypeStruct((1, D_MODEL), F32),
    )(parts, c_ctx)


def add_rows(a, b, name):
    def body(a_ref, b_ref, o_ref):
        o_ref[...] = a_ref[...] + b_ref[...]

    return pl.pallas_call(body, name=name, in_specs=[_full(a.shape), _full(b.shape)], out_specs=_full(a.shape),
                          out_shape=jax.ShapeDtypeStruct(a.shape, F32))(a, b)


def _adamw_update(w_ref, g_ref, m_ref, v_ref, d_ref, nm_ref, nv_ref):
    g_ = g_ref[...]
    m_ = ADAM_B1 * m_ref[...] + (1.0 - ADAM_B1) * g_
    v_ = ADAM_B2 * v_ref[...] + (1.0 - ADAM_B2) * (g_ * g_)
    m_hat = m_ / (1.0 - ADAM_B1 ** ADAM_STEP)
    v_hat = v_ / (1.0 - ADAM_B2 ** ADAM_STEP)
    d_ref[...] = -ADAM_LR * (m_hat / (jnp.sqrt(v_hat) + ADAM_EPS) + ADAM_WD * w_ref[...])
    nm_ref[...] = m_
    nv_ref[...] = v_


def adamw_many(ws, gs, ms, vs):
    n = len(ws)

    def body(*refs):
        for i in range(n):
            _adamw_update(*[refs[k * n + i] for k in range(7)])

    shapes = [jax.ShapeDtypeStruct(w.shape, F32) for w in ws]
    return pl.pallas_call(body, name="adamw_small", out_shape=shapes * 3,
                          compiler_params=pltpu.CompilerParams(vmem_limit_bytes=VMEM_LIMIT_V7X))(*ws, *gs, *ms, *vs)


def adamw(w, g, m, v, name):
    r, c = w.shape
    tr = 128 if (r % 128 == 0 and r > 128) else r

    def body(w_ref, g_ref, m_ref, v_ref, d_ref, nm_ref, nv_ref):
        _adamw_update(w_ref, g_ref, m_ref, v_ref, d_ref, nm_ref, nv_ref)

    spec = pl.BlockSpec((tr, c), lambda i: (i, 0))
    shp = jax.ShapeDtypeStruct((r, c), F32)
    return pl.pallas_call(
        body, name=name, grid=(r // tr,), in_specs=[spec] * 4, out_specs=[spec] * 3, out_shape=[shp] * 3,
        compiler_params=_params("parallel"),
    )(w, g, m, v)


LANES = 1024


def _pack(arrs):
    rows, spans, at = [], [], 0
    for a in arrs:
        n = int(np.prod(a.shape))
        nr = 8 * -(-n // (8 * LANES))
        flat = a.reshape(-1)
        if nr * LANES != n:
            flat = jnp.concatenate([flat, jnp.zeros((nr * LANES - n,), F32)])
        rows.append(flat.reshape(nr, LANES))
        spans.append((at, nr, n, a.shape))
        at += nr
    return jnp.concatenate(rows, axis=0), spans


def _unpack(buf, spans):
    out = []
    for at, nr, n, shape in spans:
        out.append(buf[at:at + nr].reshape(-1)[:n].reshape(shape))
    return out


SMALL_SHARD = ("lru_conv_w", "lru_ba", "lru_bx", "lru_lambda", "ffn_conv_w")


def kernel(x, c, ctx, c_ctx, w_mod, b_mod, norm_mix_g, norm_ffn_g, w_in, lru_conv_w, lru_conv_b, lru_wa, lru_ba, lru_wx, lru_bx, lru_lambda, q_norm_g, k_norm_g, na_rpb, w_rnn_out, w_na_out, w_out, w_up, ffn_conv_w, ffn_conv_b, w_down, loss_target, m_c_ctx, m_w_mod, m_b_mod, m_norm_mix_g, m_norm_ffn_g, m_w_in, m_lru_conv_w, m_lru_conv_b, m_lru_wa, m_lru_ba, m_lru_wx, m_lru_bx, m_lru_lambda, m_q_norm_g, m_k_norm_g, m_na_rpb, m_w_rnn_out, m_w_na_out, m_w_out, m_w_up, m_ffn_conv_w, m_ffn_conv_b, m_w_down, v_c_ctx, v_w_mod, v_b_mod, v_norm_mix_g, v_norm_ffn_g, v_w_in, v_lru_conv_w, v_lru_conv_b, v_lru_wa, v_lru_ba, v_lru_wx, v_lru_bx, v_lru_lambda, v_q_norm_g, v_k_norm_g, v_na_rpb, v_w_rnn_out, v_w_na_out, v_w_out, v_w_up, v_ffn_conv_w, v_ffn_conv_b, v_w_down):
    weights = dict(c_ctx=c_ctx, w_mod=w_mod, b_mod=b_mod, norm_mix_g=norm_mix_g, norm_ffn_g=norm_ffn_g, w_in=w_in,
                   lru_conv_w=lru_conv_w, lru_conv_b=lru_conv_b, lru_wa=lru_wa, lru_ba=lru_ba, lru_wx=lru_wx,
                   lru_bx=lru_bx, lru_lambda=lru_lambda, q_norm_g=q_norm_g, k_norm_g=k_norm_g, na_rpb=na_rpb,
                   w_rnn_out=w_rnn_out, w_na_out=w_na_out, w_out=w_out, w_up=w_up, ffn_conv_w=ffn_conv_w,
                   ffn_conv_b=ffn_conv_b, w_down=w_down)
    mom1 = dict(c_ctx=m_c_ctx, w_mod=m_w_mod, b_mod=m_b_mod, norm_mix_g=m_norm_mix_g, norm_ffn_g=m_norm_ffn_g,
                w_in=m_w_in, lru_conv_w=m_lru_conv_w, lru_conv_b=m_lru_conv_b, lru_wa=m_lru_wa, lru_ba=m_lru_ba,
                lru_wx=m_lru_wx, lru_bx=m_lru_bx, lru_lambda=m_lru_lambda, q_norm_g=m_q_norm_g, k_norm_g=m_k_norm_g,
                na_rpb=m_na_rpb, w_rnn_out=m_w_rnn_out, w_na_out=m_w_na_out, w_out=m_w_out, w_up=m_w_up,
                ffn_conv_w=m_ffn_conv_w, ffn_conv_b=m_ffn_conv_b, w_down=m_w_down)
    mom2 = dict(c_ctx=v_c_ctx, w_mod=v_w_mod, b_mod=v_b_mod, norm_mix_g=v_norm_mix_g, norm_ffn_g=v_norm_ffn_g,
                w_in=v_w_in, lru_conv_w=v_lru_conv_w, lru_conv_b=v_lru_conv_b, lru_wa=v_lru_wa, lru_ba=v_lru_ba,
                lru_wx=v_lru_wx, lru_bx=v_lru_bx, lru_lambda=v_lru_lambda, q_norm_g=v_q_norm_g, k_norm_g=v_k_norm_g,
                na_rpb=v_na_rpb, w_rnn_out=v_w_rnn_out, w_na_out=v_w_na_out, w_out=v_w_out, w_up=v_w_up,
                ffn_conv_w=v_ffn_conv_w, ffn_conv_b=v_ffn_conv_b, w_down=v_w_down)
    order = list(weights)
    d = D_MODEL
    mx_, my_, mc_ = _mesh_pos()
    shard = 2 * mx_ + my_
    dev = 2 * shard + mc_

    local_small, small_spans = _pack([c] + [weights[k][0] for k in SMALL_SHARD])
    gath = all_gather8(local_small, "gather_small").reshape(N_DEV, local_small.shape[0], LANES)
    per_dev = [_unpack(gath[k], small_spans) for k in range(N_DEV)]
    c_all = jnp.concatenate([per_dev[k][0] for k in range(N_DEV)], axis=0)
    full_small = {name: jnp.concatenate([per_dev[2 * s][1 + i] for s in range(N_SHARD)], axis=-1)
                  for i, name in enumerate(SMALL_SHARD)}
    c16 = jnp.concatenate([c_all, c_ctx.reshape(1, d), jnp.zeros((7, d), F32)], axis=0)
    s16, mod_part = mod_fwd(c16, w_mod[0])
    mod_all = all_gather8(mod_part, "gather_mod").reshape(N_DEV, 16, MOD_COLS)
    mod = jnp.concatenate([mod_all[2 * s] for s in range(N_SHARD)], axis=1) + b_mod
    modx = lax.dynamic_slice(mod, (dev, 0), (1, N_MOD * d))
    modc = mod[8:9]

    idx = jnp.stack([shard, mc_]).astype(jnp.int32)
    wsh = {name: cast_into_full(weights[name][0], full, axis, idx, "cast_" + name) for name, full, axis in BIG}

    z = jnp.concatenate([ctx[0], x[0]], axis=0)
    res = local_step(z, loss_target[0], modx, modc, norm_mix_g, norm_ffn_g, wsh["w_in"], full_small["lru_conv_w"],
                     lru_conv_b, lru_wa[0], full_small["lru_ba"], lru_wx[0], full_small["lru_bx"],
                     full_small["lru_lambda"], q_norm_g, k_norm_g, na_rpb[0], wsh["w_rnn_out"], wsh["w_na_out"],
                     wsh["w_out"], wsh["w_up"], full_small["ffn_conv_w"], ffn_conv_b, wsh["w_down"], idx=idx)

    halves = [add_pieces(res["partials"][i], res["pieces"][i], idx, BIG[i][2], "add_pieces_" + BIG[i][0])
              for i in range(len(BIG))]
    joined = join_halves(halves)
    grads = {name: joined[i].reshape(_shard_shape(full, axis)) for i, (name, full, axis) in enumerate(BIG)}

    for k in ("lru_wa", "lru_wx"):
        grads[k] = sum_blocks(res["lru_w_all"][k], "sum_" + k).reshape(weights[k].shape[1:])
    small_names = ["norm_mix_g", "norm_ffn_g", "lru_conv_w", "lru_conv_b", "lru_ba", "lru_bx",
                   "lru_lambda", "q_norm_g", "k_norm_g", "na_rpb", "ffn_conv_w", "ffn_conv_b"]
    local_g, g_spans = _pack([res["loss_sq"][0:1, 0:1], res["d_modx"], res["d_modc"]] + [res[k] for k in small_names])
    n_rows = local_g.shape[0]
    g_all, g_tot = all_gather8(local_g, "allreduce_small", with_sum=True)
    tot = _unpack(g_tot, g_spans)
    loss = (0.5 / d) * tot[0][0, 0]
    small_tot = dict(zip(small_names, tot[3:]))
    at_x = g_spans[1][0]
    dmx_rows = g_all.reshape(N_DEV, n_rows, LANES)[:, at_x:at_x + N_MOD, :].reshape(N_DEV, N_MOD * d)
    dmc_row = jnp.concatenate([tot[2], jnp.zeros((1, 4 * d), F32)], axis=1)
    dm16 = jnp.concatenate([dmx_rows, dmc_row, jnp.zeros((7, N_MOD * d), F32)], axis=0)
    grads["b_mod"] = add_rows(tot[1], dmc_row, "b_mod_grad")
    g_w_mod, ds16 = mod_bwd(s16, lax.dynamic_slice(dm16, (0, shard * MOD_COLS), (16, MOD_COLS)), w_mod[0])
    grads["w_mod"] = g_w_mod
    ds_parts = all_gather8(ds16[8:16], "gather_dsctx")
    grads["c_ctx"] = cctx_grad(ds_parts, c_ctx.reshape(1, d))
    for k in small_names:
        g = small_tot[k]
        if k in SMALL_SHARD:
            w_sh = weights[k].shape[-1]
            g = lax.dynamic_slice_in_dim(g, shard * w_sh, w_sh, axis=g.ndim - 1)
        grads[k] = g

    delta, new_m, new_v = {}, {}, {}
    for name, _, _ in BIG + (("w_mod", None, None),):
        delta[name], new_m[name], new_v[name] = adamw(weights[name][0], grads[name], mom1[name][0], mom2[name][0],
                                                      "adamw_" + name)
    rest = [k for k in order if k not in delta]
    views = {k: (grads[k].shape if grads[k].ndim <= 3 else (-1, grads[k].shape[-1])) for k in rest}
    small = adamw_many(*[[t[k].reshape(views[k]) for k in rest] for t in (weights, grads, mom1, mom2)])
    n_rest = len(rest)
    for i, k in enumerate(rest):
        delta[k], new_m[k], new_v[k] = small[i], small[n_rest + i], small[2 * n_rest + i]

    shaped = lambda t: [t[k].reshape(weights[k].shape) for k in order]
    return (loss, res["grad_x"][None], *shaped(grads), *shaped(delta), *shaped(new_m), *shaped(new_v))
```

```python
import numpy as np
import jax
import jax.numpy as jnp
from jax import lax
from jax.experimental import pallas as pl
from jax.experimental.pallas import tpu as pltpu

F32 = jnp.float32
BF16 = jnp.bfloat16

D_MODEL = 1024
SEQ = 2048
CTX_LEN = 256
ZLEN = SEQ + CTX_LEN
GRID_W = 64
GRID_ROWS = SEQ // GRID_W
LRU_BLOCK_W = 128
LRU_BLOCKS = 8
LRU_C = 8.0
NA_HEADS = 16
HEAD_DIM = 64
NA_ROWS = 8
NA_COLS = 16
ROPE_BASE = 10000.0
D_FF = 2816
N_MOD = 6
IN_COLS = 7 * D_MODEL
EPS = 1e-6
NEG_INF = -1e30
N_DEV = 8
N_SHARD = 4

ADAM_LR = 0.001
ADAM_B1 = 0.9
ADAM_B2 = 0.999
ADAM_EPS = 1e-08
ADAM_WD = 0.01
ADAM_STEP = 10

ROW_TILE = 256
Q_ROWS = 4
Q_TILE = Q_ROWS * GRID_W
KEY_ROWS = 12
KEY_TILE = KEY_ROWS * GRID_W
BT_PAD = 4
BT_LEN = 24
VMEM_LIMIT_V7X = 56 * 1024 * 1024

MESH_T = pl.DeviceIdType.MESH


def _params(*sem):
    return pltpu.CompilerParams(dimension_semantics=sem if sem else None, vmem_limit_bytes=VMEM_LIMIT_V7X)


def _full(shape):
    nd = len(shape)
    return pl.BlockSpec(shape, lambda *_: (0,) * nd)


class Comm:
    def __init__(self, inputs, out_shapes, aliases, scratch, emit):
        self.inputs, self.out_shapes, self.aliases, self.scratch, self.emit = inputs, out_shapes, aliases, scratch, emit


def _call(body, *, name, grid, in_specs, out_specs, out_shape, args, scratch_shapes=(), sem=(), comm=None):
    n_in, n_out, n_sc = len(in_specs), len(out_specs), len(scratch_shapes)
    if comm is None:
        res = pl.pallas_call(body, name=name, grid=grid, in_specs=list(in_specs), out_specs=list(out_specs),
                             out_shape=list(out_shape), scratch_shapes=list(scratch_shapes),
                             compiler_params=_params(*sem))(*args)
        return list(res), []
    k_in, k_out = len(comm.inputs), len(comm.out_shapes)
    steps = int(np.prod(grid))

    def hosted(*refs):
        ins, cins = refs[:n_in], refs[n_in:n_in + k_in]
        at = n_in + k_in
        outs, couts = refs[at:at + n_out], refs[at + n_out:at + n_out + k_out]
        at += n_out + k_out
        scr, cscr = refs[at:at + n_sc], refs[at + n_sc:]
        start, mid, end = comm.emit(cins, couts, cscr)
        lin = pl.program_id(0)
        for ax in range(1, len(grid)):
            lin = lin * grid[ax] + pl.program_id(ax)
        pl.when(lin == 0)(start)
        body(*ins, *outs, *scr)
        pl.when(lin == steps - 1 - steps // 7)(mid)
        pl.when(lin == steps - 1)(end)

    hbm = pl.BlockSpec(memory_space=pl.ANY)
    res = pl.pallas_call(
        hosted, name=name, grid=grid, in_specs=list(in_specs) + [hbm] * k_in, out_specs=list(out_specs) + [hbm] * k_out,
        out_shape=list(out_shape) + list(comm.out_shapes), scratch_shapes=list(scratch_shapes) + list(comm.scratch),
        input_output_aliases={n_in + i: n_out + o for i, o in comm.aliases.items()},
        compiler_params=_params(*(("arbitrary",) * len(grid))))(*args, *comm.inputs)
    return list(res[:n_out]), list(res[n_out:])


def _sigmoid(x):
    return 0.5 * jnp.tanh(0.5 * x) + 0.5


def _gelu_parts(x):
    c0 = 0.7978845608028654
    inner = c0 * (x + 0.044715 * x * x * x)
    t = jnp.tanh(inner)
    g = 0.5 * x * (1.0 + t)
    dg = 0.5 * (1.0 + t) + 0.5 * x * (1.0 - t * t) * c0 * (1.0 + 3.0 * 0.044715 * x * x)
    return g, dg


def _dot_nt(a, b):
    return lax.dot_general(a, b, (((1,), (1,)), ((), ())), preferred_element_type=F32)


def _dot_tn(a, b):
    return lax.dot_general(a, b, (((0,), (0,)), ((), ())), preferred_element_type=F32)


def norm_mod(xin, gain, shift, scale, name):
    r, d = xin.shape
    s_mod = shift.shape[0]
    assert r % ROW_TILE == 0

    def body(x_ref, g_ref, sh_ref, sc_ref, xn_ref):
        x = x_ref[...]
        nrm = x * lax.rsqrt(jnp.mean(x * x, axis=-1, keepdims=True) + EPS)
        xn_ref[...] = ((nrm * g_ref[...]) * (1.0 + sc_ref[0]) + sh_ref[0]).astype(BF16)

    mod_spec = pl.BlockSpec((1, 1, d), lambda i: (jnp.minimum(i, s_mod - 1), 0, 0))
    return pl.pallas_call(
        body, name=name, grid=(r // ROW_TILE,),
        in_specs=[pl.BlockSpec((ROW_TILE, d), lambda i: (i, 0)), _full((1, d)), mod_spec, mod_spec],
        out_specs=pl.BlockSpec((ROW_TILE, d), lambda i: (i, 0)),
        out_shape=jax.ShapeDtypeStruct((r, d), BF16),
        compiler_params=_params("parallel"),
    )(xin, gain, shift, scale)


def matmul_wide(a, b, name, tm, tn, comm=None):
    m, k = a.shape
    n = b.shape[1]
    assert m % tm == 0 and n % tn == 0

    def body(a_ref, b_ref, o_ref):
        o_ref[...] = jnp.dot(a_ref[...], b_ref[...], preferred_element_type=F32)

    res, extra = _call(
        body, name=name, grid=(n // tn, m // tm),
        in_specs=[pl.BlockSpec((tm, k), lambda j, i: (i, 0)), pl.BlockSpec((k, tn), lambda j, i: (0, j))],
        out_specs=[pl.BlockSpec((tm, tn), lambda j, i: (i, j))],
        out_shape=[jax.ShapeDtypeStruct((m, n), F32)],
        sem=("parallel", "parallel"), args=(a, b), comm=comm)
    return res[0], extra


def _row_ids(n, w):
    return lax.broadcasted_iota(jnp.int32, (n, w), 0)


def _lru_conv(xr, cw, cb):
    row = _row_ids(ZLEN, LRU_BLOCK_W)
    segpos = jnp.where(row < CTX_LEN, row, row - CTX_LEN)
    seglen = jnp.where(row < CTX_LEN, CTX_LEN, SEQ)
    acc = xr * cw[2:3, :] + cb
    for k in (0, 1, 3):
        off = k - 2
        sh = pltpu.roll(xr, (-off) % ZLEN, 0)
        ok = (segpos + off >= 0) & (segpos + off < seglen)
        acc = acc + jnp.where(ok, sh, 0.0) * cw[k:k + 1, :]
    return acc


def _lru_conv_t(dxc, cw):
    row = _row_ids(ZLEN, LRU_BLOCK_W)
    segpos = jnp.where(row < CTX_LEN, row, row - CTX_LEN)
    seglen = jnp.where(row < CTX_LEN, CTX_LEN, SEQ)
    acc = dxc * cw[2:3, :]
    for k in (0, 1, 3):
        off = k - 2
        sh = pltpu.roll(dxc, off % ZLEN, 0)
        ok = (segpos - off >= 0) & (segpos - off < seglen)
        acc = acc + jnp.where(ok, sh, 0.0) * cw[k:k + 1, :]
    return acc


def _lru_gates(xc, xcb, wa, ba, wx, bx, lam):
    r = _sigmoid(jnp.dot(xcb, wa, preferred_element_type=F32) + ba)
    i = _sigmoid(jnp.dot(xcb, wx, preferred_element_type=F32) + bx)
    sp = jnp.maximum(-lam, 0.0) + jnp.log1p(jnp.exp(-jnp.abs(lam)))
    la = (-LRU_C) * r * sp
    a = jnp.exp(la)
    sq = jnp.sqrt(-jnp.tanh(la) * (1.0 + a * a))
    b = sq * i * xc
    return r, i, sp, a, sq, b


def _scan8_fwd(a, b, rid):
    for s in (1, 2, 4):
        a_s = pltpu.roll(a, s, 0)
        b_s = pltpu.roll(b, s, 0)
        m = rid >= s
        b = jnp.where(m, a * b_s + b, b)
        a = jnp.where(m, a * a_s, a)
    return a, b


def _scan8_rev(a, b, rid):
    for s in (1, 2, 4):
        a_s = pltpu.roll(a, 8 - s, 0)
        b_s = pltpu.roll(b, 8 - s, 0)
        m = rid < 8 - s
        b = jnp.where(m, a * b_s + b, b)
        a = jnp.where(m, a * a_s, a)
    return a, b


N_CHUNK = ZLEN // 8
CTX_CHUNKS = CTX_LEN // 8
SCAN_UNROLL = 8


def _scan_up(a_ref, b_ref, h_ref, lo, hi, carry):
    rid = _row_ids(8, LRU_BLOCK_W)
    assert (hi - lo) % SCAN_UNROLL == 0

    def step(g, c):
        base = pl.multiple_of((lo + g * SCAN_UNROLL) * 8, 8)
        for u in range(SCAN_UNROLL):
            sl = pl.ds(base + 8 * u, 8)
            a, b = _scan8_fwd(a_ref[sl, :], b_ref[sl, :], rid)
            h = b + a * c
            h_ref[sl, :] = h
            c = h[7:8, :]
        return c

    return lax.fori_loop(0, (hi - lo) // SCAN_UNROLL, step, carry)


def _scan_down(a_ref, b_ref, h_ref, lo, hi, carry):
    rid = _row_ids(8, LRU_BLOCK_W)
    assert (hi - lo) % SCAN_UNROLL == 0

    def step(g, c):
        base = pl.multiple_of((hi - (g + 1) * SCAN_UNROLL) * 8, 8)
        for u in reversed(range(SCAN_UNROLL)):
            sl = pl.ds(base + 8 * u, 8)
            a, b = _scan8_rev(a_ref[sl, :], b_ref[sl, :], rid)
            h = b + a * c
            h_ref[sl, :] = h
            c = h[0:1, :]
        return c

    return lax.fori_loop(0, (hi - lo) // SCAN_UNROLL, step, carry)


def _lru_scan_dir(d, a_ref, b_ref, h_ref):
    zero = jnp.zeros((1, LRU_BLOCK_W), F32)
    if d == 0:
        _scan_up(a_ref, b_ref, h_ref, 0, N_CHUNK, zero)
    else:
        c = _scan_down(a_ref, b_ref, h_ref, 0, CTX_CHUNKS, zero)
        _scan_down(a_ref, b_ref, h_ref, CTX_CHUNKS, N_CHUNK, c)


def _lru_in_specs():
    blk = lambda rows: pl.BlockSpec((rows, LRU_BLOCK_W), lambda b: (0, b))
    wspec = pl.BlockSpec((2, 1, LRU_BLOCK_W, LRU_BLOCK_W), lambda b: (0, b, 0, 0))
    return blk, wspec


def lru_fwd(p, conv_w, conv_b, wa, ba, wx, bx, lam, comm=None):
    blk, wspec = _lru_in_specs()

    def body(xr_ref, gx_ref, cw_ref, cb_ref, wa_ref, ba_ref, wx_ref, bx_ref, lam_ref, y_ref, a_s, b_s, h_s, hsum_s):
        xr = xr_ref[...]
        xc = _lru_conv(xr, cw_ref[...], cb_ref[...])
        xcb = xc.astype(BF16)
        for d in (0, 1):
            _, _, _, a, _, b = _lru_gates(xc, xcb, wa_ref[d, 0].astype(BF16), ba_ref[d:d + 1, :],
                                          wx_ref[d, 0].astype(BF16), bx_ref[d:d + 1, :], lam_ref[d:d + 1, :])
            a_s[...] = a
            b_s[...] = b
            _lru_scan_dir(d, a_s, b_s, h_s)
            if d == 0:
                hsum_s[...] = h_s[...]
            else:
                hsum_s[...] = hsum_s[...] + h_s[...]
        g, _ = _gelu_parts(gx_ref[CTX_LEN:, :])
        y_ref[...] = (hsum_s[CTX_LEN:, :] * g).astype(BF16)

    zs = pltpu.VMEM((ZLEN, LRU_BLOCK_W), F32)
    res, extra = _call(
        body, name="lru_fwd", grid=(LRU_BLOCKS,),
        in_specs=[blk(ZLEN), pl.BlockSpec((ZLEN, LRU_BLOCK_W), lambda b: (0, 24 + b)), blk(4), blk(1),
                  wspec, blk(2), wspec, blk(2), blk(2)],
        out_specs=[pl.BlockSpec((SEQ, LRU_BLOCK_W), lambda b: (0, b))],
        out_shape=[jax.ShapeDtypeStruct((SEQ, D_MODEL), BF16)],
        scratch_shapes=[zs, zs, zs, zs], sem=("arbitrary",),
        args=(p, p, conv_w, conv_b, wa, ba, wx, bx, lam), comm=comm)
    return res[0], extra


def _rope_tables():
    t = np.arange(SEQ)
    lane = np.arange(2 * HEAD_DIM)
    in_head = lane % HEAD_DIM
    j = (in_head % 32) % 16
    freq = ROPE_BASE ** (-j.astype(np.float64) / 16.0)
    pos = np.where(in_head[None, :] < 32, (t // GRID_W)[:, None], (t % GRID_W)[:, None]).astype(np.float64)
    ang = (pos.astype(np.float32) * freq.astype(np.float32)[None, :]).astype(np.float32)
    cos = np.cos(ang).astype(np.float32)
    sin = np.sin(ang).astype(np.float32)
    sgn = np.where((in_head % 32) < 16, -1.0, 1.0).astype(np.float32)
    cos = np.concatenate([np.ones((CTX_LEN, 2 * HEAD_DIM), np.float32), cos], 0)
    sin = np.concatenate([np.zeros((CTX_LEN, 2 * HEAD_DIM), np.float32), sin * sgn[None, :]], 0)
    return jnp.asarray(cos), jnp.asarray(sin)


def _head_ones():
    lane = np.arange(2 * HEAD_DIM)
    return jnp.asarray((lane[:, None] // HEAD_DIM == lane[None, :] // HEAD_DIM).astype(np.float32))


def _rope_partner(x):
    lane = lax.broadcasted_iota(jnp.int32, x.shape, 1)
    return jnp.where((lane % 32) < 16, pltpu.roll(x, 128 - 16, 1), pltpu.roll(x, 16, 1))


def _head_rms(x, ones, gain):
    ms = jnp.dot(x * x, ones, preferred_element_type=F32, precision=lax.Precision.HIGHEST) * (1.0 / HEAD_DIM)
    rstd = lax.rsqrt(ms + EPS)
    return x * rstd * gain, rstd


PREP_TILE = 768


def qkv_prep(p, qg2, kg2, cos, sin, ones, comm=None):
    scale = HEAD_DIM ** -0.5

    def body(q_ref, k_ref, v_ref, qg_ref, kg_ref, cos_ref, sin_ref, ones_ref, qr_ref, qp_ref, kk_ref, vv_ref):
        ones_m = ones_ref[...]
        c, s = cos_ref[...], sin_ref[...]
        qn, _ = _head_rms(q_ref[...], ones_m, qg_ref[...])
        qn = qn * scale
        qr_ref[...] = (qn * c + _rope_partner(qn) * s).astype(BF16)
        qp_ref[...] = qn.astype(BF16)
        kn, _ = _head_rms(k_ref[...], ones_m, kg_ref[...])
        kk_ref[...] = (kn * c + _rope_partner(kn) * s).astype(BF16)
        vv_ref[...] = v_ref[...].astype(BF16)

    col = lambda base: pl.BlockSpec((PREP_TILE, 128), lambda hp, i: (i, base + hp))
    small = pl.BlockSpec((1, 128), lambda hp, i: (0, 0))
    tab = pl.BlockSpec((PREP_TILE, 128), lambda hp, i: (i, 0))
    oshape = jax.ShapeDtypeStruct((ZLEN, D_MODEL), BF16)
    res, extra = _call(
        body, name="qkv_prep", grid=(NA_HEADS // 2, ZLEN // PREP_TILE),
        in_specs=[col(32), col(8), col(16), small, small, tab, tab, _full((128, 128))],
        out_specs=[col(0)] * 4, out_shape=[oshape] * 4, sem=("parallel", "parallel"),
        args=(p, p, p, qg2, kg2, cos, sin, ones), comm=comm)
    return (*res, extra)


def _bias_expand():
    qc = np.arange(GRID_W)[:, None]
    kc = np.arange(GRID_W)[None, :]
    col_start = np.clip(qc - NA_COLS // 2, 0, GRID_W - NA_COLS)
    in_win = (kc >= col_start) & (kc < col_start + NA_COLS)
    dc = np.clip(kc - qc, -(NA_COLS - 1), NA_COLS - 1) + (NA_COLS - 1)
    e = np.zeros((2 * NA_COLS - 1, GRID_W, GRID_W), np.float32)
    for d in range(2 * NA_COLS - 1):
        e[d] = ((dc == d) & in_win).astype(np.float32)
    pen = np.where(in_win, 0.0, NEG_INF).astype(np.float32)
    return e, pen


def bias_table(rpb2, comm=None):
    e, pen = _bias_expand()
    n_dr = 2 * NA_ROWS - 1
    ea = np.zeros((31, GRID_W, 128), np.float32)
    ea[:, :, :GRID_W] = e
    eb = np.zeros((31, GRID_W, 128), np.float32)
    eb[:, :, GRID_W:] = e
    pen2 = np.concatenate([pen, pen], 1)
    ea = jnp.asarray(ea.reshape(31, GRID_W * 128))
    eb = jnp.asarray(eb.reshape(31, GRID_W * 128))
    sel_a = np.zeros((BT_LEN, n_dr), np.float32)
    sel_b = np.zeros((BT_LEN, n_dr), np.float32)
    for r in range(BT_LEN):
        dr = r - BT_PAD
        if 0 <= dr < n_dr:
            sel_a[r, dr] = 1.0
        if 0 <= dr + 1 < n_dr:
            sel_b[r, dr + 1] = 1.0
    sel_a, sel_b = jnp.asarray(sel_a), jnp.asarray(sel_b)
    pen2 = jnp.asarray(pen2.reshape(1, GRID_W * 128))
    hi = lax.Precision.HIGHEST

    def body(rpb_ref, sa_ref, sb_ref, ea_ref, eb_ref, pen_ref, o_ref, ra_s, rb_s):
        for h in range(NA_HEADS):
            rp = rpb_ref[h]
            ra_s[h * BT_LEN:(h + 1) * BT_LEN, :] = jnp.dot(sa_ref[...], rp, preferred_element_type=F32, precision=hi)
            rb_s[h * BT_LEN:(h + 1) * BT_LEN, :] = jnp.dot(sb_ref[...], rp, preferred_element_type=F32, precision=hi)
        o_ref[...] = (jnp.dot(ra_s[...], ea_ref[...], preferred_element_type=F32, precision=hi)
                      + jnp.dot(rb_s[...], eb_ref[...], preferred_element_type=F32, precision=hi) + pen_ref[...])

    tcol = 2048
    rows = NA_HEADS * BT_LEN
    res, extra = _call(
        body, name="bias_table", grid=(GRID_W * 128 // tcol,),
        in_specs=[_full((NA_HEADS, n_dr, 31)), _full((BT_LEN, n_dr)), _full((BT_LEN, n_dr)),
                  pl.BlockSpec((31, tcol), lambda j: (0, j)), pl.BlockSpec((31, tcol), lambda j: (0, j)),
                  pl.BlockSpec((1, tcol), lambda j: (0, j))],
        out_specs=[pl.BlockSpec((rows, tcol), lambda j: (0, j))],
        out_shape=[jax.ShapeDtypeStruct((rows, GRID_W * 128), F32)],
        scratch_shapes=[pltpu.VMEM((rows, 31), F32), pltpu.VMEM((rows, 31), F32)], sem=("parallel",),
        args=(rpb2, sel_a, sel_b, ea, eb, pen2), comm=comm)
    return res[0].reshape(NA_HEADS, BT_LEN, GRID_W, 128), extra


def _key_window(j):
    ws = jnp.clip(Q_ROWS * j - 4, 0, GRID_ROWS - KEY_ROWS)
    return ws, pl.multiple_of(CTX_LEN + ws * GRID_W, 256)


def _head_mask(hh):
    lane = lax.broadcasted_iota(jnp.int32, (Q_TILE, 128), 1)
    return (lane < HEAD_DIM) if hh == 0 else (lane >= HEAD_DIM)


def _attn_scores(j, ws, q_rot_h, q_pl_h, kw, kc, hh, bt_ref, s_ref):
    s_ref[:, :KEY_TILE] = _dot_nt(q_rot_h, kw)
    s_ref[:, KEY_TILE:] = _dot_nt(q_pl_h, kc)
    lane = lax.broadcasted_iota(jnp.int32, (GRID_W, 128), 1)
    base = ws - Q_ROWS * j + (NA_ROWS - 1) + BT_PAD
    for qi in range(Q_ROWS):
        rs = jnp.clip(Q_ROWS * j + qi - NA_ROWS // 2, 0, GRID_ROWS - NA_ROWS)
        for m in range(KEY_ROWS // 2):
            k0 = ws + 2 * m
            p0 = jnp.where((k0 >= rs) & (k0 < rs + NA_ROWS), 0.0, NEG_INF)
            p1 = jnp.where((k0 + 1 >= rs) & (k0 + 1 < rs + NA_ROWS), 0.0, NEG_INF)
            pen = jnp.where(lane < GRID_W, p0, p1)
            rows = slice(qi * GRID_W, (qi + 1) * GRID_W)
            cols = slice(128 * m, 128 * (m + 1))
            s_ref[rows, cols] = s_ref[rows, cols] + bt_ref[hh, base + 2 * m - qi] + pen
    return base


def attn_fwd(q_rot, q_pl, kk, vv, bt, comm=None):
    def body(qr_ref, qp_ref, kk_ref, vv_ref, bt_ref, o_ref, lse_ref, s_ref):
        j = pl.program_id(1)
        ws, start = _key_window(j)
        win = pl.ds(start, KEY_TILE)
        kw, kc = kk_ref[win, :], kk_ref[:CTX_LEN, :]
        vw, vc = vv_ref[win, :], vv_ref[:CTX_LEN, :]
        qr, qp = qr_ref[...], qp_ref[...]
        outs = []
        for hh in range(2):
            msk = _head_mask(hh)
            _attn_scores(j, ws, jnp.where(msk, qr, 0), jnp.where(msk, qp, 0), kw, kc, hh, bt_ref, s_ref)
            s = s_ref[...]
            mx = jnp.max(s, axis=-1, keepdims=True)
            pr = jnp.exp(s - mx)
            l = jnp.sum(pr, axis=-1, keepdims=True)
            prb = pr.astype(BF16)
            o = jnp.dot(prb[:, :KEY_TILE], vw, preferred_element_type=F32)
            o = o + jnp.dot(prb[:, KEY_TILE:], vc, preferred_element_type=F32)
            outs.append(o / l)
            lse_ref[hh] = mx + jnp.log(l)
        o_ref[...] = jnp.where(_head_mask(0), outs[0], outs[1])

    qspec = pl.BlockSpec((Q_TILE, 128), lambda hp, j: (j + 1, hp))
    kspec = pl.BlockSpec((ZLEN, 128), lambda hp, j: (0, hp))
    res, extra = _call(
        body, name="attn_fwd", grid=(NA_HEADS // 2, SEQ // Q_TILE),
        in_specs=[qspec, qspec, kspec, kspec, pl.BlockSpec((2, BT_LEN, GRID_W, 128), lambda hp, j: (hp, 0, 0, 0))],
        out_specs=[pl.BlockSpec((Q_TILE, 128), lambda hp, j: (j, hp)),
                   pl.BlockSpec((2, Q_TILE, 1), lambda hp, j: (hp, j, 0))],
        out_shape=[jax.ShapeDtypeStruct((SEQ, D_MODEL), F32), jax.ShapeDtypeStruct((NA_HEADS, SEQ, 1), F32)],
        scratch_shapes=[pltpu.VMEM((Q_TILE, KEY_TILE + CTX_LEN), F32)], sem=("parallel", "arbitrary"),
        args=(q_rot, q_pl, kk, vv, bt), comm=comm)
    return res[0], res[1], extra


def merge_fwd(y_rnn, y_na, p, z, g2, w_rnn, w_na, w_out):
    def body(yr_ref, yn_ref, mr_ref, mn_ref, x_ref, g2_ref, wr_ref, wn_ref, wo_ref, u_ref, v_ref, mg_ref, out_ref, x1_ref):
        u = jnp.dot(yr_ref[...], wr_ref[...], preferred_element_type=F32)
        v = jnp.dot(yn_ref[...].astype(BF16), wn_ref[...], preferred_element_type=F32)
        merged = (_sigmoid(mr_ref[...]) * u + _sigmoid(mn_ref[...]) * v).astype(BF16)
        out = jnp.dot(merged, wo_ref[...], preferred_element_type=F32)
        u_ref[...] = u
        v_ref[...] = v
        mg_ref[...] = merged
        out_ref[...] = out
        x1_ref[...] = x_ref[...] + g2_ref[...] * out

    row = pl.BlockSpec((ROW_TILE, D_MODEL), lambda i: (i, 0))
    lat = lambda cb: pl.BlockSpec((ROW_TILE, D_MODEL), lambda i: (i + 1, cb))
    wspec = _full((D_MODEL, D_MODEL))
    f32o = jax.ShapeDtypeStruct((SEQ, D_MODEL), F32)
    return pl.pallas_call(
        body, name="merge_fwd", grid=(SEQ // ROW_TILE,),
        in_specs=[row, row, lat(5), lat(6), lat(0), _full((1, D_MODEL)), wspec, wspec, wspec],
        out_specs=[row] * 5,
        out_shape=[f32o, f32o, jax.ShapeDtypeStruct((SEQ, D_MODEL), BF16), f32o, f32o],
        compiler_params=_params("parallel"),
    )(y_rnn, y_na, p, p, z, g2, w_rnn, w_na, w_out)


FF_TILE = 256
FF_TILES = D_FF // FF_TILE


def _ffn_conv(h, cw, cb):
    row = _row_ids(SEQ, FF_TILE)
    prev = jnp.where(row >= 1, pltpu.roll(h, 1, 0), 0.0)
    nxt = jnp.where(row < SEQ - 1, pltpu.roll(h, SEQ - 1, 0), 0.0)
    return prev * cw[0:1, :] + h * cw[1:2, :] + nxt * cw[2:3, :] + cb


def ffn_act(hpre, conv_w, conv_b):
    def body(ha_ref, hg_ref, wa_ref, wg_ref, ba_ref, bg_ref, o_ref):
        a = _ffn_conv(ha_ref[...], wa_ref[...], ba_ref[...])
        g = _ffn_conv(hg_ref[...], wg_ref[...], bg_ref[...])
        o_ref[...] = (a * _sigmoid(a) * g).astype(BF16)

    col = lambda rows, off: pl.BlockSpec((rows, FF_TILE), lambda j: (0, j + off))
    return pl.pallas_call(
        body, name="ffn_act", grid=(FF_TILES,),
        in_specs=[col(SEQ, 0), col(SEQ, FF_TILES), col(3, 0), col(3, FF_TILES), col(1, 0), col(1, FF_TILES)],
        out_specs=col(SEQ, 0),
        out_shape=jax.ShapeDtypeStruct((SEQ, D_FF), BF16),
        compiler_params=_params("parallel"),
    )(hpre, hpre, conv_w, conv_w, conv_b, conv_b)


def ffn_down_loss(act, w_down, x1, g5, target):
    def body(a_ref, w_ref, x1_ref, g5_ref, t_ref, f_ref, dy_ref, df_ref, ls_ref, dg_ref):
        i = pl.program_id(0)
        f = jnp.dot(a_ref[...], w_ref[...], preferred_element_type=F32)
        g5 = g5_ref[...]
        err = x1_ref[...] + g5 * f - t_ref[...]
        dy = err * (1.0 / D_MODEL)
        f_ref[...] = f
        dy_ref[...] = dy
        df_ref[...] = (dy * g5).astype(BF16)

        @pl.when(i == 0)
        def _():
            ls_ref[...] = jnp.zeros_like(ls_ref)
            dg_ref[...] = jnp.zeros_like(dg_ref)

        ls_ref[...] = ls_ref[...] + jnp.sum(err * err)
        dg_ref[...] = dg_ref[...] + jnp.sum(dy * f, axis=0, keepdims=True)

    row = pl.BlockSpec((ROW_TILE, D_MODEL), lambda i: (i, 0))
    f32o = jax.ShapeDtypeStruct((SEQ, D_MODEL), F32)
    return pl.pallas_call(
        body, name="ffn_down_loss", grid=(SEQ // ROW_TILE,),
        in_specs=[pl.BlockSpec((ROW_TILE, D_FF), lambda i: (i, 0)), _full((D_FF, D_MODEL)), row, _full((1, D_MODEL)), row],
        out_specs=[row, row, row, _full((8, 128)), _full((1, D_MODEL))],
        out_shape=[f32o, f32o, jax.ShapeDtypeStruct((SEQ, D_MODEL), BF16), jax.ShapeDtypeStruct((8, 128), F32),
                   jax.ShapeDtypeStruct((1, D_MODEL), F32)],
        compiler_params=_params("arbitrary"),
    )(act, w_down, x1, g5, target)


def ffn_down_bwd(df, w_down):
    def body(df_ref, w_ref, o_ref):
        o_ref[...] = _dot_nt(df_ref[...], w_ref[...])

    return pl.pallas_call(
        body, name="ffn_down_bwd", grid=(SEQ // ROW_TILE,),
        in_specs=[pl.BlockSpec((ROW_TILE, D_MODEL), lambda i: (i, 0)), _full((D_FF, D_MODEL))],
        out_specs=pl.BlockSpec((ROW_TILE, D_FF), lambda i: (i, 0)),
        out_shape=jax.ShapeDtypeStruct((SEQ, D_FF), F32),
        compiler_params=_params("parallel"),
    )(df, w_down)


def ffn_act_bwd(hpre, d_act, conv_w, conv_b):
    def half_bwd(dc, h, w, dh_ref, dw_ref, db_ref):
        row = _row_ids(SEQ, FF_TILE)
        h_prev = jnp.where(row >= 1, pltpu.roll(h, 1, 0), 0.0)
        h_next = jnp.where(row < SEQ - 1, pltpu.roll(h, SEQ - 1, 0), 0.0)
        dw_ref[0:1, :] = jnp.sum(dc * h_prev, axis=0, keepdims=True)
        dw_ref[1:2, :] = jnp.sum(dc * h, axis=0, keepdims=True)
        dw_ref[2:3, :] = jnp.sum(dc * h_next, axis=0, keepdims=True)
        db_ref[...] = jnp.sum(dc, axis=0, keepdims=True)
        dc_next = jnp.where(row < SEQ - 1, pltpu.roll(dc, SEQ - 1, 0), 0.0)
        dc_prev = jnp.where(row >= 1, pltpu.roll(dc, 1, 0), 0.0)
        dh_ref[...] = (dc_next * w[0:1, :] + dc * w[1:2, :] + dc_prev * w[2:3, :]).astype(BF16)

    def body(ha_ref, hg_ref, da_ref, wa_ref, wg_ref, ba_ref, bg_ref, dha_ref, dhg_ref, dwa_ref, dwg_ref, dba_ref, dbg_ref):
        ha, hg = ha_ref[...], hg_ref[...]
        a = _ffn_conv(ha, wa_ref[...], ba_ref[...])
        g = _ffn_conv(hg, wg_ref[...], bg_ref[...])
        sig = _sigmoid(a)
        dact = da_ref[...]
        half_bwd(dact * g * (sig * (1.0 + a * (1.0 - sig))), ha, wa_ref[...], dha_ref, dwa_ref, dba_ref)
        half_bwd(dact * a * sig, hg, wg_ref[...], dhg_ref, dwg_ref, dbg_ref)

    col = lambda rows, off: pl.BlockSpec((rows, FF_TILE), lambda j: (0, j + off))
    hshape = jax.ShapeDtypeStruct((SEQ, D_FF), BF16)
    wshape = jax.ShapeDtypeStruct((3, D_FF), F32)
    bshape = jax.ShapeDtypeStruct((1, D_FF), F32)
    return pl.pallas_call(
        body, name="ffn_act_bwd", grid=(FF_TILES,),
        in_specs=[col(SEQ, 0), col(SEQ, FF_TILES), col(SEQ, 0), col(3, 0), col(3, FF_TILES), col(1, 0), col(1, FF_TILES)],
        out_specs=[col(SEQ, 0), col(SEQ, 0), col(3, 0), col(3, 0), col(1, 0), col(1, 0)],
        out_shape=[hshape, hshape, wshape, wshape, bshape, bshape],
        compiler_params=_params("parallel"),
    )(hpre, hpre, d_act, conv_w, conv_w, conv_b, conv_b)


def _norm_mod_bwd(x, dxn, gain, scale):
    rstd = lax.rsqrt(jnp.mean(x * x, axis=-1, keepdims=True) + EPS)
    nrm = x * rstd
    dsh = jnp.sum(dxn, axis=0, keepdims=True)
    dsc = jnp.sum(dxn * nrm, axis=0, keepdims=True) * gain
    dgn = jnp.sum(dxn * nrm, axis=0, keepdims=True) * (1.0 + scale)
    dn = dxn * (gain * (1.0 + scale))
    dx = rstd * (dn - nrm * jnp.mean(dn * nrm, axis=-1, keepdims=True))
    return dx, dsh, dsc, dgn


def ffn_up_bwd(dha, dhg, w_up, x1, dy, gain, scale):
    def body(dha_ref, dhg_ref, w_ref, x_ref, dy_ref, g_ref, sc_ref, dx_ref, dsh_ref, dsc_ref, dgn_ref):
        i = pl.program_id(0)
        dxn = _dot_nt(dha_ref[...], w_ref[:, :D_FF]) + _dot_nt(dhg_ref[...], w_ref[:, D_FF:])
        dx, dsh, dsc, dgn = _norm_mod_bwd(x_ref[...], dxn, g_ref[...], sc_ref[...])
        dx_ref[...] = dy_ref[...] + dx

        @pl.when(i == 0)
        def _():
            dsh_ref[...] = dsh
            dsc_ref[...] = dsc
            dgn_ref[...] = dgn

        @pl.when(i > 0)
        def _():
            dsh_ref[...] = dsh_ref[...] + dsh
            dsc_ref[...] = dsc_ref[...] + dsc
            dgn_ref[...] = dgn_ref[...] + dgn

    row = pl.BlockSpec((ROW_TILE, D_MODEL), lambda i: (i, 0))
    vec = _full((1, D_MODEL))
    vshape = jax.ShapeDtypeStruct((1, D_MODEL), F32)
    return pl.pallas_call(
        body, name="ffn_up_bwd", grid=(SEQ // ROW_TILE,),
        in_specs=[pl.BlockSpec((ROW_TILE, D_FF), lambda i: (i, 0)), pl.BlockSpec((ROW_TILE, D_FF), lambda i: (i, 0)),
                  _full((D_MODEL, 2 * D_FF)), row, row, vec, vec],
        out_specs=[row, vec, vec, vec],
        out_shape=[jax.ShapeDtypeStruct((SEQ, D_MODEL), F32), vshape, vshape, vshape],
        compiler_params=_params("arbitrary"),
    )(dha, dhg, w_up, x1, dy, gain, scale)


def merge_bwd(dx1, out, g2, p, u, v, w_rnn, w_na, w_out):
    def body(dx_ref, out_ref, g2_ref, mr_ref, mn_ref, u_ref, v_ref, wr_ref, wn_ref, wo_ref,
             dout_ref, du_ref, dv_ref, dmr_ref, dmn_ref, dyr_ref, dyn_ref, dg2_ref):
        i = pl.program_id(0)

        @pl.when(i == 0)
        def _():
            dmr_ref[...] = jnp.zeros_like(dmr_ref)
            dmn_ref[...] = jnp.zeros_like(dmn_ref)
            dg2_ref[...] = jnp.zeros_like(dg2_ref)

        @pl.when(i > 0)
        def _():
            dx = dx_ref[...]
            dg2_ref[...] = dg2_ref[...] + jnp.sum(dx * out_ref[...], axis=0, keepdims=True)
            dout = (dx * g2_ref[...]).astype(BF16)
            dout_ref[...] = dout
            dm = _dot_nt(dout, wo_ref[...])
            sr = _sigmoid(mr_ref[...])
            sn = _sigmoid(mn_ref[...])
            du = (dm * sr).astype(BF16)
            dv = (dm * sn).astype(BF16)
            du_ref[...] = du
            dv_ref[...] = dv
            dmr_ref[...] = (dm * u_ref[...] * (sr * (1.0 - sr))).astype(BF16)
            dmn_ref[...] = (dm * v_ref[...] * (sn * (1.0 - sn))).astype(BF16)
            dyr_ref[...] = _dot_nt(du, wr_ref[...])
            dyn_ref[...] = _dot_nt(dv, wn_ref[...])

    lat = pl.BlockSpec((ROW_TILE, D_MODEL), lambda i: (jnp.maximum(i - 1, 0), 0))
    zrow = pl.BlockSpec((ROW_TILE, D_MODEL), lambda i: (i, 0))
    pcol = lambda cb: pl.BlockSpec((ROW_TILE, D_MODEL), lambda i: (i, cb))
    wspec = _full((D_MODEL, D_MODEL))
    tb = jax.ShapeDtypeStruct((SEQ, D_MODEL), BF16)
    zb = jax.ShapeDtypeStruct((ZLEN, D_MODEL), BF16)
    tf = jax.ShapeDtypeStruct((SEQ, D_MODEL), F32)
    return pl.pallas_call(
        body, name="merge_bwd", grid=(ZLEN // ROW_TILE,),
        in_specs=[lat, lat, _full((1, D_MODEL)), pcol(5), pcol(6), lat, lat, wspec, wspec, wspec],
        out_specs=[lat, lat, lat, zrow, zrow, lat, lat, _full((1, D_MODEL))],
        out_shape=[tb, tb, tb, zb, zb, tf, tf, jax.ShapeDtypeStruct((1, D_MODEL), F32)],
        compiler_params=_params("arbitrary"),
    )(dx1, out, g2, p, p, u, v, w_rnn, w_na, w_out)


def attn_bwd(q_rot, q_pl, kk, vv, bt, y_na, d_yna, lse, comm=None):
    def body(qr_ref, qp_ref, kk_ref, vv_ref, bt_ref, o_ref, do_ref, lse_ref,
             dqr_ref, dqp_ref, dk_ref, dv_ref, dbt_ref, s_ref):
        jj = pl.program_id(1)

        @pl.when(jj == 0)
        def _():
            dqr_ref[...] = jnp.zeros_like(dqr_ref)
            dqp_ref[...] = jnp.zeros_like(dqp_ref)
            dk_ref[...] = jnp.zeros_like(dk_ref)
            dv_ref[...] = jnp.zeros_like(dv_ref)
            dbt_ref[...] = jnp.zeros_like(dbt_ref)

        @pl.when(jj > 0)
        def _():
            j = jj - 1
            ws, start = _key_window(j)
            win = pl.ds(start, KEY_TILE)
            kw, kc = kk_ref[win, :], kk_ref[:CTX_LEN, :]
            vw, vc = vv_ref[win, :], vv_ref[:CTX_LEN, :]
            qr, qp = qr_ref[...], qp_ref[...]
            do = do_ref[...]
            do_o = do * o_ref[...]
            dq_r, dq_p = [], []
            for hh in range(2):
                msk = _head_mask(hh)
                q_r, q_p = jnp.where(msk, qr, 0), jnp.where(msk, qp, 0)
                base = _attn_scores(j, ws, q_r, q_p, kw, kc, hh, bt_ref, s_ref)
                pr = jnp.exp(s_ref[...] - lse_ref[hh])
                delta = jnp.sum(jnp.where(msk, do_o, 0.0), axis=-1, keepdims=True)
                dob = jnp.where(msk, do, 0.0).astype(BF16)
                ds_lat = pr[:, :KEY_TILE] * (_dot_nt(dob, vw) - delta)
                ds_ctx = pr[:, KEY_TILE:] * (_dot_nt(dob, vc) - delta)
                for qi in range(Q_ROWS):
                    for m in range(KEY_ROWS // 2):
                        idx = base + 2 * m - qi
                        dbt_ref[hh, idx] = dbt_ref[hh, idx] + ds_lat[qi * GRID_W:(qi + 1) * GRID_W, 128 * m:128 * (m + 1)]
                dsb_lat = ds_lat.astype(BF16)
                dsb_ctx = ds_ctx.astype(BF16)
                prb = pr.astype(BF16)
                dq_r.append(jnp.dot(dsb_lat, kw, preferred_element_type=F32))
                dq_p.append(jnp.dot(dsb_ctx, kc, preferred_element_type=F32))
                dk_ref[win, :] = dk_ref[win, :] + _dot_tn(dsb_lat, q_r)
                dk_ref[:CTX_LEN, :] = dk_ref[:CTX_LEN, :] + _dot_tn(dsb_ctx, q_p)
                dv_ref[win, :] = dv_ref[win, :] + _dot_tn(prb[:, :KEY_TILE], dob)
                dv_ref[:CTX_LEN, :] = dv_ref[:CTX_LEN, :] + _dot_tn(prb[:, KEY_TILE:], dob)
            dqr_ref[...] = jnp.where(_head_mask(0), dq_r[0], dq_r[1])
            dqp_ref[...] = jnp.where(_head_mask(0), dq_p[0], dq_p[1])

    lat = lambda jj: jnp.maximum(jj - 1, 0)
    qspec = pl.BlockSpec((Q_TILE, 128), lambda hp, jj: (lat(jj) + 1, hp))
    kspec = pl.BlockSpec((ZLEN, 128), lambda hp, jj: (0, hp))
    btspec = pl.BlockSpec((2, BT_LEN, GRID_W, 128), lambda hp, jj: (hp, 0, 0, 0))
    ospec = pl.BlockSpec((Q_TILE, 128), lambda hp, jj: (lat(jj), hp))
    dqspec = pl.BlockSpec((Q_TILE, 128), lambda hp, jj: (jj, hp))
    zshape = jax.ShapeDtypeStruct((ZLEN, D_MODEL), F32)
    res, extra = _call(
        body, name="attn_bwd", grid=(NA_HEADS // 2, ZLEN // Q_TILE),
        in_specs=[qspec, qspec, kspec, kspec, btspec, ospec, ospec,
                  pl.BlockSpec((2, Q_TILE, 1), lambda hp, jj: (hp, lat(jj), 0))],
        out_specs=[dqspec, dqspec, kspec, kspec, btspec],
        out_shape=[zshape, zshape, zshape, zshape, jax.ShapeDtypeStruct((NA_HEADS, BT_LEN, GRID_W, 128), F32)],
        scratch_shapes=[pltpu.VMEM((Q_TILE, KEY_TILE + CTX_LEN), F32)], sem=("parallel", "arbitrary"),
        args=(q_rot, q_pl, kk, vv, bt, y_na, d_yna, lse), comm=comm)
    return (*res, extra)


def qkv_bwd(dq_rot, dq_pl, dk, dv, p, qg2, kg2, cos, sin, ones, comm=None):
    scale = HEAD_DIM ** -0.5
    n_hp, n_i = NA_HEADS // 2, ZLEN // PREP_TILE

    def norm_rope_bwd(d_rot, d_extra, x, gain, cos_t, sin_t, ones_m, dx_ref, acc_ref):
        xh, rstd = _head_rms(x, ones_m, 1.0)
        dn = d_rot * cos_t + _rope_partner(d_rot * sin_t)
        if d_extra is not None:
            dn = (dn + d_extra) * scale
        acc_ref[...] = acc_ref[...] + jnp.sum(dn * xh, axis=0, keepdims=True)
        dxh = dn * gain
        seg = jnp.dot(dxh * xh, ones_m, preferred_element_type=F32, precision=lax.Precision.HIGHEST) * (1.0 / HEAD_DIM)
        dx_ref[...] = (rstd * (dxh - xh * seg)).astype(BF16)

    def body(dqr_ref, dqp_ref, dk_ref, dv_ref, xq_ref, xk_ref, qg_ref, kg_ref, cos_ref, sin_ref, ones_ref,
             dxq_ref, dxk_ref, dxv_ref, dgq_ref, dgk_ref, accq_ref, acck_ref):
        hp, i = pl.program_id(0), pl.program_id(1)

        @pl.when((hp == 0) & (i == 0))
        def _():
            accq_ref[...] = jnp.zeros_like(accq_ref)
            acck_ref[...] = jnp.zeros_like(acck_ref)

        ones_m = ones_ref[...]
        cos_t, sin_t = cos_ref[...], sin_ref[...]
        norm_rope_bwd(dqr_ref[...], dqp_ref[...], xq_ref[...], qg_ref[...], cos_t, sin_t, ones_m, dxq_ref, accq_ref)
        norm_rope_bwd(dk_ref[...], None, xk_ref[...], kg_ref[...], cos_t, sin_t, ones_m, dxk_ref, acck_ref)
        dxv_ref[...] = dv_ref[...].astype(BF16)

        @pl.when((hp == n_hp - 1) & (i == n_i - 1))
        def _():
            dgq_ref[...] = accq_ref[:, :HEAD_DIM] + accq_ref[:, HEAD_DIM:]
            dgk_ref[...] = acck_ref[:, :HEAD_DIM] + acck_ref[:, HEAD_DIM:]

    col = lambda base: pl.BlockSpec((PREP_TILE, 128), lambda hp, i: (i, base + hp))
    small = pl.BlockSpec((1, 128), lambda hp, i: (0, 0))
    tab = pl.BlockSpec((PREP_TILE, 128), lambda hp, i: (i, 0))
    zb = jax.ShapeDtypeStruct((ZLEN, D_MODEL), BF16)
    gshape = jax.ShapeDtypeStruct((1, HEAD_DIM), F32)
    res, extra = _call(
        body, name="qkv_bwd", grid=(n_hp, n_i),
        in_specs=[col(0)] * 4 + [col(32), col(8), small, small, tab, tab, _full((128, 128))],
        out_specs=[col(0)] * 3 + [_full((1, HEAD_DIM))] * 2,
        out_shape=[zb, zb, zb, gshape, gshape],
        scratch_shapes=[pltpu.VMEM((1, 128), F32)] * 2, sem=("arbitrary", "arbitrary"),
        args=(dq_rot, dq_pl, dk, dv, p, p, qg2, kg2, cos, sin, ones), comm=comm)
    return (*res, extra)


def rpb_grad(dbt):
    e, _ = _bias_expand()
    n_dr = 2 * NA_ROWS - 1
    ea = np.zeros((31, GRID_W, 128), np.float32)
    ea[:, :, :GRID_W] = e
    eb = np.zeros((31, GRID_W, 128), np.float32)
    eb[:, :, GRID_W:] = e
    eat = jnp.asarray(ea.reshape(31, GRID_W * 128).T.copy())
    ebt = jnp.asarray(eb.reshape(31, GRID_W * 128).T.copy())
    sel_at = np.zeros((n_dr, BT_LEN), np.float32)
    sel_bt = np.zeros((n_dr, BT_LEN), np.float32)
    for r in range(BT_LEN):
        dr = r - BT_PAD
        if 0 <= dr < n_dr:
            sel_at[dr, r] = 1.0
        if 0 <= dr + 1 < n_dr:
            sel_bt[dr + 1, r] = 1.0
    hi = lax.Precision.HIGHEST

    tk = 2048
    wide = GRID_W * 128
    rows = NA_HEADS * BT_LEN
    n_k = wide // tk

    def body(d_ref, sa_ref, sb_ref, ea_ref, eb_ref, o_ref, a_s, b_s):
        k = pl.program_id(0)
        dm = d_ref[...]
        a = jnp.dot(dm, ea_ref[...], preferred_element_type=F32, precision=hi)
        b = jnp.dot(dm, eb_ref[...], preferred_element_type=F32, precision=hi)

        @pl.when(k == 0)
        def _():
            a_s[...] = a
            b_s[...] = b

        @pl.when(k > 0)
        def _():
            a_s[...] = a_s[...] + a
            b_s[...] = b_s[...] + b

        @pl.when(k == n_k - 1)
        def _():
            for h in range(NA_HEADS):
                sl = slice(h * BT_LEN, (h + 1) * BT_LEN)
                o_ref[h] = (jnp.dot(sa_ref[...], a_s[sl, :], preferred_element_type=F32, precision=hi)
                            + jnp.dot(sb_ref[...], b_s[sl, :], preferred_element_type=F32, precision=hi))

    return pl.pallas_call(
        body, name="rpb_grad", grid=(n_k,),
        in_specs=[pl.BlockSpec((rows, tk), lambda k: (0, k)), _full((n_dr, BT_LEN)), _full((n_dr, BT_LEN)),
                  pl.BlockSpec((tk, 31), lambda k: (k, 0)), pl.BlockSpec((tk, 31), lambda k: (k, 0))],
        out_specs=_full((NA_HEADS, n_dr, 31)),
        out_shape=jax.ShapeDtypeStruct((NA_HEADS, n_dr, 31), F32),
        scratch_shapes=[pltpu.VMEM((rows, 31), F32), pltpu.VMEM((rows, 31), F32)],
        compiler_params=_params("arbitrary"),
    )(dbt.reshape(rows, wide), jnp.asarray(sel_at), jnp.asarray(sel_bt), eat, ebt)


def lru_bwd(p, d_yrnn, conv_w, conv_b, wa, ba, wx, bx, lam, comm=None):
    blk, wspec = _lru_in_specs()

    def body(xr_ref, gx_ref, dy_ref, cw_ref, cb_ref, wa_ref, ba_ref, wx_ref, bx_ref, lam_ref,
             dxr_ref, dgx_ref, dcw_ref, dcb_ref, dwa_ref, dba_ref, dwx_ref, dbx_ref, dlam_ref,
             a_s, b_s, h_s, l_s, hsum_s, dxc_s, dh_s):
        xr = xr_ref[...]
        cw = cw_ref[...]
        xc = _lru_conv(xr, cw, cb_ref[...])
        xcb = xc.astype(BF16)
        g, dg = _gelu_parts(gx_ref[CTX_LEN:, :])
        dy = dy_ref[...]
        dh_s[:CTX_LEN, :] = jnp.zeros((CTX_LEN, LRU_BLOCK_W), F32)
        dh_s[CTX_LEN:, :] = dy * g
        row = _row_ids(ZLEN, LRU_BLOCK_W)
        zero = jnp.zeros((1, LRU_BLOCK_W), F32)
        for d in (0, 1):
            wab = wa_ref[d, 0].astype(BF16)
            wxb = wx_ref[d, 0].astype(BF16)
            lam_d = lam_ref[d:d + 1, :]
            r, gi, sp, a, sq, b = _lru_gates(xc, xcb, wab, ba_ref[d:d + 1, :], wxb, bx_ref[d:d + 1, :], lam_d)
            a_s[...] = a
            b_s[...] = b
            _lru_scan_dir(d, a_s, b_s, h_s)
            h = h_s[...]
            if d == 0:
                hsum_s[...] = h
                h_prev = jnp.where(row >= 1, pltpu.roll(h, 1, 0), 0.0)
                a_s[...] = pltpu.roll(a, ZLEN - 1, 0)
                _scan_down(a_s, dh_s, l_s, 0, N_CHUNK, zero)
            else:
                hsum_s[...] = hsum_s[...] + h
                h_prev = jnp.where(row == CTX_LEN - 1, 0.0, pltpu.roll(h, ZLEN - 1, 0))
                a_s[...] = pltpu.roll(a, 1, 0)
                c = _scan_up(a_s, dh_s, l_s, CTX_CHUNKS, N_CHUNK, zero)
                _scan_up(a_s, dh_s, l_s, 0, CTX_CHUNKS, c)
            db = l_s[...]
            da = db * h_prev
            dsq = db * gi * xc
            dgi = db * sq * xc
            dxc_d = db * sq * gi
            dla = da * a - dsq * (a * a) / sq
            dr = dla * ((-LRU_C) * sp)
            dsp = jnp.sum(dla * ((-LRU_C) * r), axis=0, keepdims=True)
            dlam_ref[d:d + 1, :] = -dsp * _sigmoid(-lam_d)
            dzr = dr * r * (1.0 - r)
            dzi = dgi * gi * (1.0 - gi)
            dba_ref[d:d + 1, :] = jnp.sum(dzr, axis=0, keepdims=True)
            dbx_ref[d:d + 1, :] = jnp.sum(dzi, axis=0, keepdims=True)
            dzrb = dzr.astype(BF16)
            dzib = dzi.astype(BF16)
            dwa_ref[d, 0] = _dot_tn(xcb, dzrb)
            dwx_ref[d, 0] = _dot_tn(xcb, dzib)
            dxc_d = dxc_d + _dot_nt(dzrb, wab) + _dot_nt(dzib, wxb)
            if d == 0:
                dxc_s[...] = dxc_d
            else:
                dxc_s[...] = dxc_s[...] + dxc_d
        dxc = dxc_s[...]
        dxr_ref[...] = _lru_conv_t(dxc, cw).astype(BF16)
        dcb_ref[...] = jnp.sum(dxc, axis=0, keepdims=True)
        segpos = jnp.where(row < CTX_LEN, row, row - CTX_LEN)
        seglen = jnp.where(row < CTX_LEN, CTX_LEN, SEQ)
        for k in range(4):
            off = k - 2
            if off == 0:
                sh = xr
            else:
                ok = (segpos + off >= 0) & (segpos + off < seglen)
                sh = jnp.where(ok, pltpu.roll(xr, (-off) % ZLEN, 0), 0.0)
            dcw_ref[k:k + 1, :] = jnp.sum(dxc * sh, axis=0, keepdims=True)
        dgx_ref[:CTX_LEN, :] = jnp.zeros((CTX_LEN, LRU_BLOCK_W), BF16)
        dgx_ref[CTX_LEN:, :] = (dy * hsum_s[CTX_LEN:, :] * dg).astype(BF16)

    zs = pltpu.VMEM((ZLEN, LRU_BLOCK_W), F32)
    zb = jax.ShapeDtypeStruct((ZLEN, D_MODEL), BF16)
    v2 = jax.ShapeDtypeStruct((2, D_MODEL), F32)
    w4 = jax.ShapeDtypeStruct((2, LRU_BLOCKS, LRU_BLOCK_W, LRU_BLOCK_W), F32)
    res, extra = _call(
        body, name="lru_bwd", grid=(LRU_BLOCKS,),
        in_specs=[blk(ZLEN), pl.BlockSpec((ZLEN, LRU_BLOCK_W), lambda b: (0, 24 + b)), blk(SEQ), blk(4), blk(1),
                  wspec, blk(2), wspec, blk(2), blk(2)],
        out_specs=[blk(ZLEN), blk(ZLEN), blk(4), blk(1), wspec, blk(2), wspec, blk(2), blk(2)],
        out_shape=[zb, zb, jax.ShapeDtypeStruct((4, D_MODEL), F32), jax.ShapeDtypeStruct((1, D_MODEL), F32),
                   w4, v2, w4, v2, v2],
        scratch_shapes=[zs] * 7, sem=("arbitrary",),
        args=(p, p, d_yrnn, conv_w, conv_b, wa, ba, wx, bx, lam), comm=comm)
    return (*res, extra)


def in_proj_bwd(dgs, w_in, z, dx1, gain, scale, comm=None):
    def body(*refs):
        dg_refs = refs[:7]
        w_ref, z_ref, dx1_ref, g_ref, sc_ref, gx_ref, dsh_ref, dsc_ref, dgn_ref = refs[7:]
        i = pl.program_id(0)
        dxn = _dot_nt(dg_refs[0][...], w_ref[:, 0:D_MODEL])
        for g in range(1, 7):
            dxn = dxn + _dot_nt(dg_refs[g][...], w_ref[:, g * D_MODEL:(g + 1) * D_MODEL])
        dx, dsh, dsc, dgn = _norm_mod_bwd(z_ref[...], dxn, g_ref[...], sc_ref[0])

        @pl.when(i <= 1)
        def _():
            dsh_ref[0] = dsh
            dsc_ref[0] = dsc

        @pl.when(i > 1)
        def _():
            dsh_ref[0] = dsh_ref[0] + dsh
            dsc_ref[0] = dsc_ref[0] + dsc

        @pl.when(i == 0)
        def _():
            dgn_ref[...] = dgn

        @pl.when(i > 0)
        def _():
            dgn_ref[...] = dgn_ref[...] + dgn
            gx_ref[...] = dx1_ref[...] + dx

    zrow = pl.BlockSpec((ROW_TILE, D_MODEL), lambda i: (i, 0))
    lat = pl.BlockSpec((ROW_TILE, D_MODEL), lambda i: (jnp.maximum(i - 1, 0), 0))
    mod = pl.BlockSpec((1, 1, D_MODEL), lambda i: (jnp.minimum(i, 1), 0, 0))
    mshape = jax.ShapeDtypeStruct((2, 1, D_MODEL), F32)
    res, extra = _call(
        body, name="in_proj_bwd", grid=(ZLEN // ROW_TILE,),
        in_specs=[zrow] * 7 + [_full((D_MODEL, IN_COLS)), zrow, lat, _full((1, D_MODEL)), mod],
        out_specs=[lat, mod, mod, _full((1, D_MODEL))],
        out_shape=[jax.ShapeDtypeStruct((SEQ, D_MODEL), F32), mshape, mshape, jax.ShapeDtypeStruct((1, D_MODEL), F32)],
        sem=("arbitrary",), args=(*dgs, w_in, z, dx1, gain, scale), comm=comm)
    return (*res, extra)


def matmul_tn(a, b, name, tm, tn, prev=None, col_block=0, total_cols=None):
    k, m = a.shape
    n = b.shape[1]
    total_cols = n if total_cols is None else total_cols
    assert m % tm == 0 and n % tn == 0
    off = col_block * (n // tn)

    def body(a_ref, b_ref, *rest):
        rest[-1][...] = _dot_tn(a_ref[...].astype(BF16), b_ref[...]).astype(BF16)

    in_specs = [pl.BlockSpec((k, tm), lambda i, j: (0, i)), pl.BlockSpec((k, tn), lambda i, j: (0, j))]
    args = [a, b]
    aliases = {}
    if prev is not None:
        in_specs.append(pl.BlockSpec(memory_space=pl.ANY))
        args.append(prev)
        aliases = {2: 0}
    return pl.pallas_call(
        body, name=name, grid=(m // tm, n // tn), in_specs=in_specs,
        out_specs=pl.BlockSpec((tm, tn), lambda i, j: (i, j + off)),
        out_shape=jax.ShapeDtypeStruct((m, total_cols), BF16),
        input_output_aliases=aliases,
        compiler_params=_params("parallel", "parallel"),
    )(*args)


def local_step(z, target, modx, modc, norm_mix_g, norm_ffn_g, w_in, conv_w, conv_b, wa, ba, wx, bx, lam, qg, kg, rpb,
               w_rnn, w_na, w_out, w_up, fconv_w, fconv_b, w_down, idx=None):
    dist = idx is not None
    c_idx = idx[1:2] if dist else None
    d = D_MODEL
    mx = [modx[:, k * d:(k + 1) * d] for k in range(N_MOD)]
    shift = jnp.stack([modc[:, 0:d], mx[0]])
    scale = jnp.stack([modc[:, d:2 * d], mx[1]])
    cos, sin = _rope_tables()
    ones = _head_ones()
    qg2 = jnp.tile(qg, (1, 2))
    kg2 = jnp.tile(kg, (1, 2))

    xn = norm_mod(z, norm_mix_g, shift, scale, "norm_mix")
    bt, got = bias_table(rpb, comm=gather_weights_comm([w_in], [0]) if dist else None)
    if dist:
        w_in = got[0]
    p, _ = matmul_wide(xn, w_in, "in_proj", 3 * ROW_TILE, 1792)
    y_rnn, got = lru_fwd(p, conv_w, conv_b, wa, ba, wx, bx, lam,
                         comm=gather_weights_comm([w_down], [5]) if dist else None)
    if dist:
        w_down = got[0]
    q_rot, q_pl, kk, vv, got = qkv_prep(p, qg2, kg2, cos, sin, ones,
                                        comm=gather_weights_comm([w_rnn, w_na, w_out], [1, 2, 3]) if dist else None)
    if dist:
        w_rnn, w_na, w_out = got
    y_na, lse, got = attn_fwd(q_rot, q_pl, kk, vv, bt, comm=gather_weights_comm([w_up], [4]) if dist else None)
    if dist:
        w_up = got[0]
    u, v, merged, out, x1 = merge_fwd(y_rnn, y_na, p, z, mx[2], w_rnn, w_na, w_out)
    xn2 = norm_mod(x1, norm_ffn_g, mx[3][None], mx[4][None], "norm_ffn")
    hpre, _ = matmul_wide(xn2, w_up, "ffn_up", 2 * ROW_TILE, 1408)
    act = ffn_act(hpre, fconv_w, fconv_b)
    f, dy, df, loss_sq, dg5 = ffn_down_loss(act, w_down, x1, mx[5], target)

    partials, pieces = {}, {}

    def chip_partials(which, grads, tag):
        views = [_grad_view(g, BIG[w][1], BIG[w][2]) for w, g in zip(which, grads)]
        recv = exchange_halves(views, "grad_exchange_" + tag)
        for w, gv, r in zip(which, views, recv):
            partials[w] = add_halves(gv, r, c_idx, "add_halves_" + BIG[w][0])
        return scatter_pieces_comm([partials[w] for w in which], which)

    d_act = ffn_down_bwd(df, w_down)
    dha, dhg, d_fcw_a, d_fcw_g, d_fcb_a, d_fcb_g = ffn_act_bwd(hpre, d_act, fconv_w, fconv_b)
    d_fcw = jnp.concatenate([d_fcw_a, d_fcw_g], axis=1)
    d_fcb = jnp.concatenate([d_fcb_a, d_fcb_g], axis=1)
    dx1, d_s3, d_s4, d_gffn = ffn_up_bwd(dha, dhg, w_up, x1, dy, norm_ffn_g, mx[4])
    g_w_down = matmul_tn(act, df, "gw_down", 256, D_MODEL)
    g_w_up = matmul_tn(xn2, dha, "gw_up_a", 512, 1408, total_cols=2 * D_FF)
    g_w_up = matmul_tn(xn2, dhg, "gw_up_g", 512, 1408, prev=g_w_up, col_block=1, total_cols=2 * D_FF)
    dout, du, dv, dmr, dmn, dyr, dyn, dg2 = merge_bwd(dx1, out, mx[2], p, u, v, w_rnn, w_na, w_out)
    g_w_out = matmul_tn(merged, dout, "gw_out", 1024, 512)
    g_w_rnn = matmul_tn(y_rnn, du, "gw_rnn", 1024, 512)
    g_w_na = matmul_tn(y_na, dv, "gw_na", 1024, 512)
    *lru_grads, got = lru_bwd(p, dyr, conv_w, conv_b, wa, ba, wx, bx, lam,
                              comm=chip_partials([4, 5], [g_w_up, g_w_down], "ffn") if dist else None)
    dxr, dgx, d_cw, d_cb, d_wa, d_ba, d_wx, d_bx, d_lam = lru_grads
    if dist:
        pieces[4], pieces[5] = got
    lru_w_all = {}
    dqr, dqp, dk, dvh, dbt, got = attn_bwd(
        q_rot, q_pl, kk, vv, bt, y_na, dyn, lse,
        comm=join_comms(chip_partials([1, 2, 3], [g_w_rnn, g_w_na, g_w_out], "mix"),
                        all_gather_comm(d_wa.reshape(-1, LRU_BLOCK_W))) if dist else None)
    if dist:
        pieces[1], pieces[2], pieces[3], lru_w_all["lru_wa"] = got
    dq_cols, dk_cols, dv_cols, d_qg, d_kg, got = qkv_bwd(
        dqr, dqp, dk, dvh, p, qg2, kg2, cos, sin, ones,
        comm=all_gather_comm(d_wx.reshape(-1, LRU_BLOCK_W)) if dist else None)
    if dist:
        lru_w_all["lru_wx"] = got[0]
    d_rpb = rpb_grad(dbt)
    dgs = [dxr, dk_cols, dv_cols, dgx, dq_cols, dmr, dmn]
    g_w_in = None
    for g in range(7):
        g_w_in = matmul_tn(xn, dgs[g], "gw_in_%d" % g, 1024, 512, prev=g_w_in, col_block=g, total_cols=IN_COLS)
    grad_x, dsh, dsc, d_gmix, got = in_proj_bwd(dgs, w_in, z, dx1, norm_mix_g, scale,
                                                comm=chip_partials([0], [g_w_in], "w_in") if dist else None)
    if dist:
        pieces[0] = got[0]

    d_modx = jnp.concatenate([dsh[1], dsc[1], dg2, d_s3, d_s4, dg5], axis=1)
    d_modc = jnp.concatenate([dsh[0], dsc[0]], axis=1)
    return dict(loss_sq=loss_sq, grad_x=grad_x, d_modx=d_modx, d_modc=d_modc, norm_mix_g=d_gmix, norm_ffn_g=d_gffn,
                w_in=g_w_in, lru_conv_w=d_cw, lru_conv_b=d_cb, lru_wa=d_wa, lru_ba=d_ba, lru_wx=d_wx, lru_bx=d_bx,
                lru_lambda=d_lam, q_norm_g=d_qg, k_norm_g=d_kg, na_rpb=d_rpb, w_rnn_out=g_w_rnn, w_na_out=g_w_na,
                w_out=g_w_out, w_up=g_w_up, ffn_conv_w=d_fcw, ffn_conv_b=d_fcb, w_down=g_w_down,
                partials=partials, pieces=pieces, lru_w_all=lru_w_all)


def _mesh_pos():
    return lax.axis_index("x"), lax.axis_index("y"), lax.axis_index("c")


def _other_chips(x, y):
    return [(1 - x, y), (x, 1 - y), (1 - x, 1 - y)]


def all_gather8(xs, name, with_sum=False):
    m, n = xs.shape
    assert m % 8 == 0

    def body(x_ref, out_ref, *rest):
        if with_sum:
            sum_ref, send_sems, recv_sems, local_sem = rest
        else:
            send_sems, recv_sems, local_sem = rest
        x, y, c = _mesh_pos()
        me, sibling = (x, y, c), (x, y, 1 - c)
        chips = _other_chips(x, y)

        def rows(px, py, pc):
            return out_ref.at[pl.ds((4 * px + 2 * py + pc) * m, m), :]

        def copy(k, block, to, src=None):
            return pltpu.make_async_remote_copy(
                src_ref=rows(*block) if src is None else src, dst_ref=rows(*block),
                send_sem=send_sems.at[k], recv_sem=recv_sems.at[k], device_id=to, device_id_type=MESH_T)

        mine = pltpu.make_async_copy(x_ref, rows(*me), local_sem)
        mine.start()
        first = [copy(0, me, sibling, src=x_ref)]
        first += [copy(1 + j, me, (*chip, c), src=x_ref) for j, chip in enumerate(chips)]
        for cp in first:
            cp.start()
        passed = [copy(4 + j, (*chip, c), sibling) for j, chip in enumerate(chips)]
        for j, chip in enumerate(chips):
            copy(1 + j, (*chip, c), me).wait_recv()
            passed[j].start()
        copy(0, sibling, me).wait_recv()
        for j, chip in enumerate(chips):
            copy(4 + j, (*chip, 1 - c), me).wait_recv()
        for cp in first + passed:
            cp.wait_send()
        mine.wait()
        if with_sum:
            acc = out_ref[0:m, :]
            for k in range(1, N_DEV):
                acc = acc + out_ref[k * m:(k + 1) * m, :]
            sum_ref[...] = acc

    vm = pl.BlockSpec(memory_space=pltpu.VMEM)
    out_shape = [jax.ShapeDtypeStruct((N_DEV * m, n), F32)]
    if with_sum:
        out_shape.append(jax.ShapeDtypeStruct((m, n), F32))
    res = pl.pallas_call(
        body, name=name, in_specs=[vm], out_specs=[vm] * len(out_shape), out_shape=out_shape,
        scratch_shapes=[pltpu.SemaphoreType.DMA((7,)), pltpu.SemaphoreType.DMA((7,)), pltpu.SemaphoreType.DMA],
        compiler_params=pltpu.CompilerParams(vmem_limit_bytes=VMEM_LIMIT_V7X),
    )(xs)
    return res if with_sum else res[0]


BIG = (("w_in", (D_MODEL, IN_COLS), 1), ("w_rnn_out", (D_MODEL, D_MODEL), 0), ("w_na_out", (D_MODEL, D_MODEL), 0),
       ("w_out", (D_MODEL, D_MODEL), 0), ("w_up", (D_MODEL, 2 * D_FF), 1), ("w_down", (D_FF, D_MODEL), 0))


def _shard_shape(full, axis):
    r, c = full
    return (r // N_SHARD, c) if axis == 0 else (r, c // N_SHARD)


def _slot(ref, full, axis, s, h):
    r, c = full
    if axis == 0:
        rs = r // N_SHARD
        return ref.at[pl.ds(s * rs + h * (rs // 2), rs // 2), :]
    cs = c // N_SHARD
    return ref.at[pl.ds(h * (r // 2), r // 2), pl.ds(s * cs, cs)]


def cast_into_full(x, full, axis, idx, name):
    r, c = x.shape
    tr = next(t for t in (512, 352, 256, 128) if r % t == 0)
    nb = r // tr

    def body(idx_ref, x_ref, o_ref):
        o_ref[...] = x_ref[...].astype(BF16)

    if axis == 0:
        out_spec = pl.BlockSpec((tr, c), lambda i, idx_ref: (idx_ref[0] * nb + i, 0))
    else:
        out_spec = pl.BlockSpec((tr, c), lambda i, idx_ref: (i, idx_ref[0]))
    return pl.pallas_call(
        body, name=name,
        grid_spec=pltpu.PrefetchScalarGridSpec(
            num_scalar_prefetch=1, grid=(nb,), in_specs=[pl.BlockSpec((tr, c), lambda i, idx_ref: (i, 0))],
            out_specs=out_spec),
        out_shape=jax.ShapeDtypeStruct(full, BF16),
        compiler_params=_params("parallel"),
    )(idx, x)


def run_comm(comm, name):
    k_in, k_out = len(comm.inputs), len(comm.out_shapes)

    def body(*refs):
        start, mid, end = comm.emit(refs[:k_in], refs[k_in:k_in + k_out], refs[k_in + k_out:])
        start()
        mid()
        end()

    hbm = pl.BlockSpec(memory_space=pl.ANY)
    return pl.pallas_call(
        body, name=name, in_specs=[hbm] * k_in, out_specs=[hbm] * k_out, out_shape=list(comm.out_shapes),
        input_output_aliases=dict(comm.aliases), scratch_shapes=list(comm.scratch),
        compiler_params=pltpu.CompilerParams(vmem_limit_bytes=VMEM_LIMIT_V7X),
    )(*comm.inputs)


def gather_weights_comm(fulls, which):
    nw = len(which)
    specs = [BIG[w] for w in which]

    def emit(_, outs, sems):
        send1, recv1, send2, recv2 = sems
        x, y, c = _mesh_pos()
        sibling = (x, y, 1 - c)
        chips = _other_chips(x, y)
        s_me = 2 * x + y
        shards = [2 * chip[0] + chip[1] for chip in chips]

        def ici(w, j, shard):
            _, full, axis = specs[w]
            dst = _slot(outs[w], full, axis, shard, c)
            return pltpu.make_async_remote_copy(
                src_ref=dst, dst_ref=dst, send_sem=send1.at[3 * w + j],
                recv_sem=recv1.at[3 * w + j], device_id=(*chips[j], c), device_id_type=MESH_T)

        def d2d(w, j, shard, half):
            _, full, axis = specs[w]
            dst = _slot(outs[w], full, axis, shard, half)
            return pltpu.make_async_remote_copy(
                src_ref=dst, dst_ref=dst, send_sem=send2.at[3 * w + j], recv_sem=recv2.at[3 * w + j],
                device_id=sibling, device_id_type=MESH_T)

        pairs = [(w, j) for w in range(nw) for j in range(3)]

        def start():
            for w, j in pairs:
                ici(w, j, s_me).start()

        def mid():
            for w, j in pairs:
                ici(w, j, shards[j]).wait_recv()
                d2d(w, j, shards[j], c).start()

        def end():
            for w, j in pairs:
                d2d(w, j, shards[j], 1 - c).wait_recv()
            for w, j in pairs:
                ici(w, j, s_me).wait_send()
                d2d(w, j, shards[j], c).wait_send()

        return start, mid, end

    return Comm(list(fulls), [jax.ShapeDtypeStruct(full, BF16) for _, full, _ in specs], {i: i for i in range(nw)},
                [pltpu.SemaphoreType.DMA((3 * nw,))] * 4, emit)


def join_comms(a, b):
    ai, ao, asc = len(a.inputs), len(a.out_shapes), len(a.scratch)

    def emit(ins, outs, sems):
        fa = a.emit(ins[:ai], outs[:ao], sems[:asc])
        fb = b.emit(ins[ai:], outs[ao:], sems[asc:])

        def both(k):
            def run():
                fa[k]()
                fb[k]()
            return run

        return both(0), both(1), both(2)

    aliases = dict(a.aliases)
    aliases.update({ai + i: ao + o for i, o in b.aliases.items()})
    return Comm(a.inputs + b.inputs, a.out_shapes + b.out_shapes, aliases, a.scratch + b.scratch, emit)


def all_gather_comm(x):
    def emit(srcs, outs, sems):
        send_sems, recv_sems, local_sem = sems
        x_ref, out_ref = srcs[0], outs[0]
        x, y, c = _mesh_pos()
        me, sibling = (x, y, c), (x, y, 1 - c)
        chips = _other_chips(x, y)

        def blk(px, py, pc):
            return out_ref.at[4 * px + 2 * py + pc]

        def copy(k, block, to, src=None):
            return pltpu.make_async_remote_copy(
                src_ref=blk(*block) if src is None else src, dst_ref=blk(*block),
                send_sem=send_sems.at[k], recv_sem=recv_sems.at[k], device_id=to, device_id_type=MESH_T)

        def mine():
            return pltpu.make_async_copy(x_ref, blk(*me), local_sem)

        def start():
            mine().start()
            copy(0, me, sibling, src=x_ref).start()
            for j, chip in enumerate(chips):
                copy(1 + j, me, (*chip, c), src=x_ref).start()

        def mid():
            for j, chip in enumerate(chips):
                copy(1 + j, (*chip, c), me).wait_recv()
                copy(4 + j, (*chip, c), sibling).start()

        def end():
            copy(0, sibling, me).wait_recv()
            for j, chip in enumerate(chips):
                copy(4 + j, (*chip, 1 - c), me).wait_recv()
            copy(0, me, sibling, src=x_ref).wait_send()
            for j, chip in enumerate(chips):
                copy(1 + j, me, (*chip, c), src=x_ref).wait_send()
                copy(4 + j, (*chip, c), sibling).wait_send()
            mine().wait()

        return start, mid, end

    return Comm([x], [jax.ShapeDtypeStruct((N_DEV,) + x.shape, F32)], {},
                [pltpu.SemaphoreType.DMA((7,)), pltpu.SemaphoreType.DMA((7,)), pltpu.SemaphoreType.DMA], emit)


def sum_blocks(g, name):
    _, r, c = g.shape
    tr = 256

    def body(g_ref, o_ref):
        acc = g_ref[0]
        for k in range(1, N_DEV):
            acc = acc + g_ref[k]
        o_ref[...] = acc

    return pl.pallas_call(
        body, name=name, grid=(r // tr,),
        in_specs=[pl.BlockSpec((N_DEV, tr, c), lambda i: (0, i, 0))],
        out_specs=pl.BlockSpec((tr, c), lambda i: (i, 0)),
        out_shape=jax.ShapeDtypeStruct((r, c), F32),
        compiler_params=_params("parallel"),
    )(g)


def _grad_view(g, full, axis):
    r, c = full
    if axis == 0:
        return g.reshape(N_SHARD, 2, r // N_SHARD // 2, c)
    return g.reshape(1, 2, r // 2, c)


def exchange_halves(gviews, name):
    nw = len(gviews)

    def body(*refs):
        srcs, outs = refs[:nw], refs[nw:2 * nw]
        send_sems, recv_sems = refs[2 * nw:]
        x, y, c = _mesh_pos()
        cps = []
        for w in range(nw):
            cp = pltpu.make_async_remote_copy(
                src_ref=srcs[w].at[:, pl.ds(1 - c, 1)], dst_ref=outs[w], send_sem=send_sems.at[w],
                recv_sem=recv_sems.at[w], device_id=(x, y, 1 - c), device_id_type=MESH_T)
            cp.start()
            cps.append(cp)
        for cp in cps:
            cp.wait()

    hbm = pl.BlockSpec(memory_space=pl.ANY)
    return pl.pallas_call(
        body, name=name, in_specs=[hbm] * nw, out_specs=[hbm] * nw,
        out_shape=[jax.ShapeDtypeStruct((g.shape[0], 1) + g.shape[2:], BF16) for g in gviews],
        scratch_shapes=[pltpu.SemaphoreType.DMA((nw,)), pltpu.SemaphoreType.DMA((nw,))],
        compiler_params=pltpu.CompilerParams(vmem_limit_bytes=VMEM_LIMIT_V7X),
    )(*gviews)


def _row_tile(rh):
    return 128 if rh % 128 == 0 else rh


def add_halves(gview, recv, c_idx, name):
    a, _, rh, cc = gview.shape
    tr = _row_tile(rh)

    def body(c_ref, g_ref, r_ref, o_ref):
        o_ref[0] = (g_ref[0, 0].astype(F32) + r_ref[0, 0].astype(F32)).astype(BF16)

    return pl.pallas_call(
        body, name=name,
        grid_spec=pltpu.PrefetchScalarGridSpec(
            num_scalar_prefetch=1, grid=(a, rh // tr),
            in_specs=[pl.BlockSpec((1, 1, tr, cc), lambda s, i, c_ref: (s, c_ref[0], i, 0)),
                      pl.BlockSpec((1, 1, tr, cc), lambda s, i, c_ref: (s, 0, i, 0))],
            out_specs=pl.BlockSpec((1, tr, cc), lambda s, i, c_ref: (s, i, 0))),
        out_shape=jax.ShapeDtypeStruct((a, rh, cc), BF16),
        compiler_params=_params("parallel", "parallel"),
    )(c_idx, gview, recv)


def _piece_shape(full, axis):
    rs, cs = _shard_shape(full, axis)
    return (rs // 2, cs)


def scatter_pieces_comm(partials, which):
    nw = len(which)
    specs = [BIG[w] for w in which]

    def emit(srcs, outs, sems):
        send_sems, recv_sems = sems
        x, y, c = _mesh_pos()
        chips = _other_chips(x, y)

        def copies():
            cps = []
            for w, (_, full, axis) in enumerate(specs):
                cs = full[1] // N_SHARD
                for j, chip in enumerate(chips):
                    s_j = 2 * chip[0] + chip[1]
                    src = srcs[w].at[s_j] if axis == 0 else srcs[w].at[0, :, pl.ds(s_j * cs, cs)]
                    cps.append(pltpu.make_async_remote_copy(
                        src_ref=src, dst_ref=outs[w].at[j], send_sem=send_sems.at[3 * w + j],
                        recv_sem=recv_sems.at[3 * w + j], device_id=(*chip, c), device_id_type=MESH_T))
            return cps

        def start():
            for cp in copies():
                cp.start()

        def mid():
            pass

        def end():
            for cp in copies():
                cp.wait()

        return start, mid, end

    return Comm(list(partials), [jax.ShapeDtypeStruct((3,) + _piece_shape(full, axis), BF16) for _, full, axis in specs],
                {}, [pltpu.SemaphoreType.DMA((3 * nw,)), pltpu.SemaphoreType.DMA((3 * nw,))], emit)


def add_pieces(partial, recv, idx, axis, name):
    _, rh, cs = recv.shape
    tr = _row_tile(rh)

    def body(idx_ref, p_ref, r_ref, o_ref):
        o_ref[0] = ((p_ref[0].astype(F32) + r_ref[0].astype(F32)) + r_ref[1].astype(F32)) + r_ref[2].astype(F32)

    if axis == 0:
        pspec = pl.BlockSpec((1, tr, cs), lambda i, idx_ref: (idx_ref[0], i, 0))
    else:
        pspec = pl.BlockSpec((1, tr, cs), lambda i, idx_ref: (0, i, idx_ref[0]))
    return pl.pallas_call(
        body, name=name,
        grid_spec=pltpu.PrefetchScalarGridSpec(
            num_scalar_prefetch=1, grid=(rh // tr,),
            in_specs=[pspec, pl.BlockSpec((3, tr, cs), lambda i, idx_ref: (0, i, 0))],
            out_specs=pl.BlockSpec((1, tr, cs), lambda i, idx_ref: (idx_ref[1], i, 0))),
        out_shape=jax.ShapeDtypeStruct((2, rh, cs), F32),
        compiler_params=_params("parallel"),
    )(idx, partial, recv)


def join_halves(halves):
    nw = len(BIG)

    def body(*refs):
        outs = refs[nw:2 * nw]
        send_sems, recv_sems = refs[2 * nw:]
        x, y, c = _mesh_pos()
        cps = []
        for w in range(nw):
            cp = pltpu.make_async_remote_copy(
                src_ref=outs[w].at[c], dst_ref=outs[w].at[c], send_sem=send_sems.at[w], recv_sem=recv_sems.at[w],
                device_id=(x, y, 1 - c), device_id_type=MESH_T)
            cp.start()
            cps.append(cp)
        for w in range(nw):
            cps[w].wait_send()
            pltpu.make_async_remote_copy(
                src_ref=outs[w].at[1 - c], dst_ref=outs[w].at[1 - c], send_sem=send_sems.at[w],
                recv_sem=recv_sems.at[w], device_id=(x, y, 1 - c), device_id_type=MESH_T).wait_recv()

    hbm = pl.BlockSpec(memory_space=pl.ANY)
    return pl.pallas_call(
        body, name="grad_join_halves", in_specs=[hbm] * nw, out_specs=[hbm] * nw,
        out_shape=[jax.ShapeDtypeStruct(h.shape, F32) for h in halves],
        input_output_aliases={i: i for i in range(nw)},
        scratch_shapes=[pltpu.SemaphoreType.DMA((nw,))] * 2,
        compiler_params=pltpu.CompilerParams(vmem_limit_bytes=VMEM_LIMIT_V7X),
    )(*halves)


MOD_COLS = N_MOD * D_MODEL // N_SHARD
MOD_TILE = 512


def mod_fwd(c16, w_mod):
    def body(c_ref, w_ref, s_ref, o_ref):
        cv = c_ref[...]
        s = cv * _sigmoid(cv)
        s_ref[...] = s
        o_ref[...] = jnp.dot(s.astype(BF16), w_ref[...].astype(BF16), preferred_element_type=F32)

    return pl.pallas_call(
        body, name="mod_fwd", grid=(MOD_COLS // MOD_TILE,),
        in_specs=[_full((16, D_MODEL)), pl.BlockSpec((D_MODEL, MOD_TILE), lambda j: (0, j))],
        out_specs=[_full((16, D_MODEL)), pl.BlockSpec((16, MOD_TILE), lambda j: (0, j))],
        out_shape=[jax.ShapeDtypeStruct((16, D_MODEL), F32), jax.ShapeDtypeStruct((16, MOD_COLS), F32)],
        compiler_params=_params("arbitrary"),
    )(c16, w_mod)


def mod_bwd(s16, dm16, w_mod):
    hi = lax.Precision.HIGHEST

    def body(s_ref, d_ref, w_ref, gw_ref, ds_ref):
        j = pl.program_id(0)
        dm = d_ref[...]
        gw_ref[...] = lax.dot_general(s_ref[...], dm, (((0,), (0,)), ((), ())), preferred_element_type=F32, precision=hi)
        part = lax.dot_general(dm, w_ref[...], (((1,), (1,)), ((), ())), preferred_element_type=F32, precision=hi)

        @pl.when(j == 0)
        def _():
            ds_ref[...] = part

        @pl.when(j > 0)
        def _():
            ds_ref[...] = ds_ref[...] + part

    return pl.pallas_call(
        body, name="mod_bwd", grid=(MOD_COLS // MOD_TILE,),
        in_specs=[_full((16, D_MODEL)), pl.BlockSpec((16, MOD_TILE), lambda j: (0, j)),
                  pl.BlockSpec((D_MODEL, MOD_TILE), lambda j: (0, j))],
        out_specs=[pl.BlockSpec((D_MODEL, MOD_TILE), lambda j: (0, j)), _full((16, D_MODEL))],
        out_shape=[jax.ShapeDtypeStruct((D_MODEL, MOD_COLS), F32), jax.ShapeDtypeStruct((16, D_MODEL), F32)],
        compiler_params=_params("arbitrary"),
    )(s16, dm16, w_mod)


def cctx_grad(parts, c_ctx):
    def body(p_ref, c_ref, o_ref):
        ds = p_ref[0:1, :]
        for s in range(1, N_SHARD):
            ds = ds + p_ref[16 * s:16 * s + 1, :]
        cv = c_ref[...]
        sg = _sigmoid(cv)
        o_ref[...] = ds * (sg * (1.0 + cv * (1.0 - sg)))

    return pl.pallas_call(
        body, name="cctx_grad", in_specs=[_full((N_DEV * 8, D_MODEL)), _full((1, D_MODEL))],
        out_specs=_full((1, D_MODEL)), out_shape=jax.ShapeDtypeStruct((1, D_MODEL), F32),
    )(parts, c_ctx)


def add_rows(a, b, name):
    def body(a_ref, b_ref, o_ref):
        o_ref[...] = a_ref[...] + b_ref[...]

    return pl.pallas_call(body, name=name, in_specs=[_full(a.shape), _full(b.shape)], out_specs=_full(a.shape),
                          out_shape=jax.ShapeDtypeStruct(a.shape, F32))(a, b)


def _adamw_update(w_ref, g_ref, m_ref, v_ref, d_ref, nm_ref, nv_ref):
    g_ = g_ref[...]
    m_ = ADAM_B1 * m_ref[...] + (1.0 - ADAM_B1) * g_
    v_ = ADAM_B2 * v_ref[...] + (1.0 - ADAM_B2) * (g_ * g_)
    m_hat = m_ / (1.0 - ADAM_B1 ** ADAM_STEP)
    v_hat = v_ / (1.0 - ADAM_B2 ** ADAM_STEP)
    d_ref[...] = -ADAM_LR * (m_hat / (jnp.sqrt(v_hat) + ADAM_EPS) + ADAM_WD * w_ref[...])
    nm_ref[...] = m_
    nv_ref[...] = v_


def adamw_many(ws, gs, ms, vs):
    n = len(ws)

    def body(*refs):
        for i in range(n):
            _adamw_update(*[refs[k * n + i] for k in range(7)])

    shapes = [jax.ShapeDtypeStruct(w.shape, F32) for w in ws]
    return pl.pallas_call(body, name="adamw_small", out_shape=shapes * 3,
                          compiler_params=pltpu.CompilerParams(vmem_limit_bytes=VMEM_LIMIT_V7X))(*ws, *gs, *ms, *vs)


def adamw(w, g, m, v, name):
    r, c = w.shape
    tr = 128 if (r % 128 == 0 and r > 128) else r

    def body(w_ref, g_ref, m_ref, v_ref, d_ref, nm_ref, nv_ref):
        _adamw_update(w_ref, g_ref, m_ref, v_ref, d_ref, nm_ref, nv_ref)

    spec = pl.BlockSpec((tr, c), lambda i: (i, 0))
    shp = jax.ShapeDtypeStruct((r, c), F32)
    return pl.pallas_call(
        body, name=name, grid=(r // tr,), in_specs=[spec] * 4, out_specs=[spec] * 3, out_shape=[shp] * 3,
        compiler_params=_params("parallel"),
    )(w, g, m, v)


LANES = 1024


def _pack(arrs):
    rows, spans, at = [], [], 0
    for a in arrs:
        n = int(np.prod(a.shape))
        nr = 8 * -(-n // (8 * LANES))
        flat = a.reshape(-1)
        if nr * LANES != n:
            flat = jnp.concatenate([flat, jnp.zeros((nr * LANES - n,), F32)])
        rows.append(flat.reshape(nr, LANES))
        spans.append((at, nr, n, a.shape))
        at += nr
    return jnp.concatenate(rows, axis=0), spans


def _unpack(buf, spans):
    out = []
    for at, nr, n, shape in spans:
        out.append(buf[at:at + nr].reshape(-1)[:n].reshape(shape))
    return out


SMALL_SHARD = ("lru_conv_w", "lru_ba", "lru_bx", "lru_lambda", "ffn_conv_w")


def kernel(x, c, ctx, c_ctx, w_mod, b_mod, norm_mix_g, norm_ffn_g, w_in, lru_conv_w, lru_conv_b, lru_wa, lru_ba, lru_wx, lru_bx, lru_lambda, q_norm_g, k_norm_g, na_rpb, w_rnn_out, w_na_out, w_out, w_up, ffn_conv_w, ffn_conv_b, w_down, loss_target, m_c_ctx, m_w_mod, m_b_mod, m_norm_mix_g, m_norm_ffn_g, m_w_in, m_lru_conv_w, m_lru_conv_b, m_lru_wa, m_lru_ba, m_lru_wx, m_lru_bx, m_lru_lambda, m_q_norm_g, m_k_norm_g, m_na_rpb, m_w_rnn_out, m_w_na_out, m_w_out, m_w_up, m_ffn_conv_w, m_ffn_conv_b, m_w_down, v_c_ctx, v_w_mod, v_b_mod, v_norm_mix_g, v_norm_ffn_g, v_w_in, v_lru_conv_w, v_lru_conv_b, v_lru_wa, v_lru_ba, v_lru_wx, v_lru_bx, v_lru_lambda, v_q_norm_g, v_k_norm_g, v_na_rpb, v_w_rnn_out, v_w_na_out, v_w_out, v_w_up, v_ffn_conv_w, v_ffn_conv_b, v_w_down):
    weights = dict(c_ctx=c_ctx, w_mod=w_mod, b_mod=b_mod, norm_mix_g=norm_mix_g, norm_ffn_g=norm_ffn_g, w_in=w_in,
                   lru_conv_w=lru_conv_w, lru_conv_b=lru_conv_b, lru_wa=lru_wa, lru_ba=lru_ba, lru_wx=lru_wx,
                   lru_bx=lru_bx, lru_lambda=lru_lambda, q_norm_g=q_norm_g, k_norm_g=k_norm_g, na_rpb=na_rpb,
                   w_rnn_out=w_rnn_out, w_na_out=w_na_out, w_out=w_out, w_up=w_up, ffn_conv_w=ffn_conv_w,
                   ffn_conv_b=ffn_conv_b, w_down=w_down)
    mom1 = dict(c_ctx=m_c_ctx, w_mod=m_w_mod, b_mod=m_b_mod, norm_mix_g=m_norm_mix_g, norm_ffn_g=m_norm_ffn_g,
                w_in=m_w_in, lru_conv_w=m_lru_conv_w, lru_conv_b=m_lru_conv_b, lru_wa=m_lru_wa, lru_ba=m_lru_ba,
                lru_wx=m_lru_wx, lru_bx=m_lru_bx, lru_lambda=m_lru_lambda, q_norm_g=m_q_norm_g, k_norm_g=m_k_norm_g,
                na_rpb=m_na_rpb, w_rnn_out=m_w_rnn_out, w_na_out=m_w_na_out, w_out=m_w_out, w_up=m_w_up,
                ffn_conv_w=m_ffn_conv_w, ffn_conv_b=m_ffn_conv_b, w_down=m_w_down)
    mom2 = dict(c_ctx=v_c_ctx, w_mod=v_w_mod, b_mod=v_b_mod, norm_mix_g=v_norm_mix_g, norm_ffn_g=v_norm_ffn_g,
                w_in=v_w_in, lru_conv_w=v_lru_conv_w, lru_conv_b=v_lru_conv_b, lru_wa=v_lru_wa, lru_ba=v_lru_ba,
                lru_wx=v_lru_wx, lru_bx=v_lru_bx, lru_lambda=v_lru_lambda, q_norm_g=v_q_norm_g, k_norm_g=v_k_norm_g,
                na_rpb=v_na_rpb, w_rnn_out=v_w_rnn_out, w_na_out=v_w_na_out, w_out=v_w_out, w_up=v_w_up,
                ffn_conv_w=v_ffn_conv_w, ffn_conv_b=v_ffn_conv_b, w_down=v_w_down)
    order = list(weights)
    d = D_MODEL
    mx_, my_, mc_ = _mesh_pos()
    shard = 2 * mx_ + my_
    dev = 2 * shard + mc_

    local_small, small_spans = _pack([c] + [weights[k][0] for k in SMALL_SHARD])
    gath = all_gather8(local_small, "gather_small").reshape(N_DEV, local_small.shape[0], LANES)
    per_dev = [_unpack(gath[k], small_spans) for k in range(N_DEV)]
    c_all = jnp.concatenate([per_dev[k][0] for k in range(N_DEV)], axis=0)
    full_small = {name: jnp.concatenate([per_dev[2 * s][1 + i] for s in range(N_SHARD)], axis=-1)
                  for i, name in enumerate(SMALL_SHARD)}
    c16 = jnp.concatenate([c_all, c_ctx.reshape(1, d), jnp.zeros((7, d), F32)], axis=0)
    s16, mod_part = mod_fwd(c16, w_mod[0])
    mod_all = all_gather8(mod_part, "gather_mod").reshape(N_DEV, 16, MOD_COLS)
    mod = jnp.concatenate([mod_all[2 * s] for s in range(N_SHARD)], axis=1) + b_mod
    modx = lax.dynamic_slice(mod, (dev, 0), (1, N_MOD * d))
    modc = mod[8:9]

    idx = jnp.stack([shard, mc_]).astype(jnp.int32)
    wsh = {name: cast_into_full(weights[name][0], full, axis, idx, "cast_" + name) for name, full, axis in BIG}

    z = jnp.concatenate([ctx[0], x[0]], axis=0)
    res = local_step(z, loss_target[0], modx, modc, norm_mix_g, norm_ffn_g, wsh["w_in"], full_small["lru_conv_w"],
                     lru_conv_b, lru_wa[0], full_small["lru_ba"], lru_wx[0], full_small["lru_bx"],
                     full_small["lru_lambda"], q_norm_g, k_norm_g, na_rpb[0], wsh["w_rnn_out"], wsh["w_na_out"],
                     wsh["w_out"], wsh["w_up"], full_small["ffn_conv_w"], ffn_conv_b, wsh["w_down"], idx=idx)

    halves = [add_pieces(res["partials"][i], res["pieces"][i], idx, BIG[i][2], "add_pieces_" + BIG[i][0])
              for i in range(len(BIG))]
    joined = join_halves(halves)
    grads = {name: joined[i].reshape(_shard_shape(full, axis)) for i, (name, full, axis) in enumerate(BIG)}

    for k in ("lru_wa", "lru_wx"):
        grads[k] = sum_blocks(res["lru_w_all"][k], "sum_" + k).reshape(weights[k].shape[1:])
    small_names = ["norm_mix_g", "norm_ffn_g", "lru_conv_w", "lru_conv_b", "lru_ba", "lru_bx",
                   "lru_lambda", "q_norm_g", "k_norm_g", "na_rpb", "ffn_conv_w", "ffn_conv_b"]
    local_g, g_spans = _pack([res["loss_sq"][0:1, 0:1], res["d_modx"], res["d_modc"]] + [res[k] for k in small_names])
    n_rows = local_g.shape[0]
    g_all, g_tot = all_gather8(local_g, "allreduce_small", with_sum=True)
    tot = _unpack(g_tot, g_spans)
    loss = (0.5 / d) * tot[0][0, 0]
    small_tot = dict(zip(small_names, tot[3:]))
    at_x = g_spans[1][0]
    dmx_rows = g_all.reshape(N_DEV, n_rows, LANES)[:, at_x:at_x + N_MOD, :].reshape(N_DEV, N_MOD * d)
    dmc_row = jnp.concatenate([tot[2], jnp.zeros((1, 4 * d), F32)], axis=1)
    dm16 = jnp.concatenate([dmx_rows, dmc_row, jnp.zeros((7, N_MOD * d), F32)], axis=0)
    grads["b_mod"] = add_rows(tot[1], dmc_row, "b_mod_grad")
    g_w_mod, ds16 = mod_bwd(s16, lax.dynamic_slice(dm16, (0, shard * MOD_COLS), (16, MOD_COLS)), w_mod[0])
    grads["w_mod"] = g_w_mod
    ds_parts = all_gather8(ds16[8:16], "gather_dsctx")
    grads["c_ctx"] = cctx_grad(ds_parts, c_ctx.reshape(1, d))
    for k in small_names:
        g = small_tot[k]
        if k in SMALL_SHARD:
            w_sh = weights[k].shape[-1]
            g = lax.dynamic_slice_in_dim(g, shard * w_sh, w_sh, axis=g.ndim - 1)
        grads[k] = g

    delta, new_m, new_v = {}, {}, {}
    for name, _, _ in BIG + (("w_mod", None, None),):
        delta[name], new_m[name], new_v[name] = adamw(weights[name][0], grads[name], mom1[name][0], mom2[name][0],
                                                      "adamw_" + name)
    rest = [k for k in order if k not in delta]
    views = {k: (grads[k].shape if grads[k].ndim <= 3 else (-1, grads[k].shape[-1])) for k in rest}
    small = adamw_many(*[[t[k].reshape(views[k]) for k in rest] for t in (weights, grads, mom1, mom2)])
    n_rest = len(rest)
    for i, k in enumerate(rest):
        delta[k], new_m[k], new_v[k] = small[i], small[n_rest + i], small[2 * n_rest + i]

    shaped = lambda t: [t[k].reshape(weights[k].shape) for k in order]
    return (loss, res["grad_x"][None], *shaped(grads), *shaped(delta), *shaped(new_m), *shaped(new_v))
```

```python
import numpy as np
import jax
import jax.numpy as jnp
from jax import lax
from jax.experimental import pallas as pl
from jax.experimental.pallas import tpu as pltpu

F32 = jnp.float32
BF16 = jnp.bfloat16

D_MODEL = 1024
SEQ = 2048
CTX_LEN = 256
ZLEN = SEQ + CTX_LEN
GRID_W = 64
GRID_ROWS = SEQ // GRID_W
LRU_BLOCK_W = 128
LRU_BLOCKS = 8
LRU_C = 8.0
NA_HEADS = 16
HEAD_DIM = 64
NA_ROWS = 8
NA_COLS = 16
ROPE_BASE = 10000.0
D_FF = 2816
N_MOD = 6
IN_COLS = 7 * D_MODEL
EPS = 1e-6
NEG_INF = -1e30
N_DEV = 8
N_SHARD = 4

ADAM_LR = 0.001
ADAM_B1 = 0.9
ADAM_B2 = 0.999
ADAM_EPS = 1e-08
ADAM_WD = 0.01
ADAM_STEP = 10

ROW_TILE = 256
Q_ROWS = 4
Q_TILE = Q_ROWS * GRID_W
KEY_ROWS = 12
KEY_TILE = KEY_ROWS * GRID_W
BT_PAD = 4
BT_LEN = 24
VMEM_LIMIT_V7X = 56 * 1024 * 1024

MESH_T = pl.DeviceIdType.MESH


def _params(*sem):
    return pltpu.CompilerParams(dimension_semantics=sem if sem else None, vmem_limit_bytes=VMEM_LIMIT_V7X)


def _full(shape):
    nd = len(shape)
    return pl.BlockSpec(shape, lambda *_: (0,) * nd)


class Comm:
    def __init__(self, inputs, out_shapes, aliases, scratch, emit):
        self.inputs, self.out_shapes, self.aliases, self.scratch, self.emit = inputs, out_shapes, aliases, scratch, emit


def _call(body, *, name, grid, in_specs, out_specs, out_shape, args, scratch_shapes=(), sem=(), comm=None):
    n_in, n_out, n_sc = len(in_specs), len(out_specs), len(scratch_shapes)
    if comm is None:
        res = pl.pallas_call(body, name=name, grid=grid, in_specs=list(in_specs), out_specs=list(out_specs),
                             out_shape=list(out_shape), scratch_shapes=list(scratch_shapes),
                             compiler_params=_params(*sem))(*args)
        return list(res), []
    k_in, k_out = len(comm.inputs), len(comm.out_shapes)
    steps = int(np.prod(grid))

    def hosted(*refs):
        ins, cins = refs[:n_in], refs[n_in:n_in + k_in]
        at = n_in + k_in
        outs, couts = refs[at:at + n_out], refs[at + n_out:at + n_out + k_out]
        at += n_out + k_out
        scr, cscr = refs[at:at + n_sc], refs[at + n_sc:]
        start, mid, end = comm.emit(cins, couts, cscr)
        lin = pl.program_id(0)
        for ax in range(1, len(grid)):
            lin = lin * grid[ax] + pl.program_id(ax)
        pl.when(lin == 0)(start)
        body(*ins, *outs, *scr)
        pl.when(lin == steps - 1 - steps // 7)(mid)
        pl.when(lin == steps - 1)(end)

    hbm = pl.BlockSpec(memory_space=pl.ANY)
    res = pl.pallas_call(
        hosted, name=name, grid=grid, in_specs=list(in_specs) + [hbm] * k_in, out_specs=list(out_specs) + [hbm] * k_out,
        out_shape=list(out_shape) + list(comm.out_shapes), scratch_shapes=list(scratch_shapes) + list(comm.scratch),
        input_output_aliases={n_in + i: n_out + o for i, o in comm.aliases.items()},
        compiler_params=_params(*(("arbitrary",) * len(grid))))(*args, *comm.inputs)
    return list(res[:n_out]), list(res[n_out:])


def _sigmoid(x):
    return 0.5 * jnp.tanh(0.5 * x) + 0.5


def _gelu_parts(x):
    c0 = 0.7978845608028654
    inner = c0 * (x + 0.044715 * x * x * x)
    t = jnp.tanh(inner)
    g = 0.5 * x * (1.0 + t)
    dg = 0.5 * (1.0 + t) + 0.5 * x * (1.0 - t * t) * c0 * (1.0 + 3.0 * 0.044715 * x * x)
    return g, dg


def _dot_nt(a, b):
    return lax.dot_general(a, b, (((1,), (1,)), ((), ())), preferred_element_type=F32)


def _dot_tn(a, b):
    return lax.dot_general(a, b, (((0,), (0,)), ((), ())), preferred_element_type=F32)


def norm_mod(xin, gain, shift, scale, name):
    r, d = xin.shape
    s_mod = shift.shape[0]
    assert r % ROW_TILE == 0

    def body(x_ref, g_ref, sh_ref, sc_ref, xn_ref):
        x = x_ref[...]
        nrm = x * lax.rsqrt(jnp.mean(x * x, axis=-1, keepdims=True) + EPS)
        xn_ref[...] = ((nrm * g_ref[...]) * (1.0 + sc_ref[0]) + sh_ref[0]).astype(BF16)

    mod_spec = pl.BlockSpec((1, 1, d), lambda i: (jnp.minimum(i, s_mod - 1), 0, 0))
    return pl.pallas_call(
        body, name=name, grid=(r // ROW_TILE,),
        in_specs=[pl.BlockSpec((ROW_TILE, d), lambda i: (i, 0)), _full((1, d)), mod_spec, mod_spec],
        out_specs=pl.BlockSpec((ROW_TILE, d), lambda i: (i, 0)),
        out_shape=jax.ShapeDtypeStruct((r, d), BF16),
        compiler_params=_params("parallel"),
    )(xin, gain, shift, scale)


def matmul_wide(a, b, name, tm, tn, comm=None):
    m, k = a.shape
    n = b.shape[1]
    assert m % tm == 0 and n % tn == 0

    def body(a_ref, b_ref, o_ref):
        o_ref[...] = jnp.dot(a_ref[...], b_ref[...], preferred_element_type=F32)

    res, extra = _call(
        body, name=name, grid=(n // tn, m // tm),
        in_specs=[pl.BlockSpec((tm, k), lambda j, i: (i, 0)), pl.BlockSpec((k, tn), lambda j, i: (0, j))],
        out_specs=[pl.BlockSpec((tm, tn), lambda j, i: (i, j))],
        out_shape=[jax.ShapeDtypeStruct((m, n), F32)],
        sem=("parallel", "parallel"), args=(a, b), comm=comm)
    return res[0], extra


def _row_ids(n, w):
    return lax.broadcasted_iota(jnp.int32, (n, w), 0)


def _lru_conv(xr, cw, cb):
    row = _row_ids(ZLEN, LRU_BLOCK_W)
    segpos = jnp.where(row < CTX_LEN, row, row - CTX_LEN)
    seglen = jnp.where(row < CTX_LEN, CTX_LEN, SEQ)
    acc = xr * cw[2:3, :] + cb
    for k in (0, 1, 3):
        off = k - 2
        sh = pltpu.roll(xr, (-off) % ZLEN, 0)
        ok = (segpos + off >= 0) & (segpos + off < seglen)
        acc = acc + jnp.where(ok, sh, 0.0) * cw[k:k + 1, :]
    return acc


def _lru_conv_t(dxc, cw):
    row = _row_ids(ZLEN, LRU_BLOCK_W)
    segpos = jnp.where(row < CTX_LEN, row, row - CTX_LEN)
    seglen = jnp.where(row < CTX_LEN, CTX_LEN, SEQ)
    acc = dxc * cw[2:3, :]
    for k in (0, 1, 3):
        off = k - 2
        sh = pltpu.roll(dxc, off % ZLEN, 0)
        ok = (segpos - off >= 0) & (segpos - off < seglen)
        acc = acc + jnp.where(ok, sh, 0.0) * cw[k:k + 1, :]
    return acc


def _lru_gates(xc, xcb, wa, ba, wx, bx, lam):
    r = _sigmoid(jnp.dot(xcb, wa, preferred_element_type=F32) + ba)
    i = _sigmoid(jnp.dot(xcb, wx, preferred_element_type=F32) + bx)
    sp = jnp.maximum(-lam, 0.0) + jnp.log1p(jnp.exp(-jnp.abs(lam)))
    la = (-LRU_C) * r * sp
    a = jnp.exp(la)
    sq = jnp.sqrt(-jnp.tanh(la) * (1.0 + a * a))
    b = sq * i * xc
    return r, i, sp, a, sq, b


def _scan8_fwd(a, b, rid):
    for s in (1, 2, 4):
        a_s = pltpu.roll(a, s, 0)
        b_s = pltpu.roll(b, s, 0)
        m = rid >= s
        b = jnp.where(m, a * b_s + b, b)
        a = jnp.where(m, a * a_s, a)
    return a, b


def _scan8_rev(a, b, rid):
    for s in (1, 2, 4):
        a_s = pltpu.roll(a, 8 - s, 0)
        b_s = pltpu.roll(b, 8 - s, 0)
        m = rid < 8 - s
        b = jnp.where(m, a * b_s + b, b)
        a = jnp.where(m, a * a_s, a)
    return a, b


N_CHUNK = ZLEN // 8
CTX_CHUNKS = CTX_LEN // 8
SCAN_UNROLL = 8


def _scan_up(a_ref, b_ref, h_ref, lo, hi, carry):
    rid = _row_ids(8, LRU_BLOCK_W)
    assert (hi - lo) % SCAN_UNROLL == 0

    def step(g, c):
        base = pl.multiple_of((lo + g * SCAN_UNROLL) * 8, 8)
        for u in range(SCAN_UNROLL):
            sl = pl.ds(base + 8 * u, 8)
            a, b = _scan8_fwd(a_ref[sl, :], b_ref[sl, :], rid)
            h = b + a * c
            h_ref[sl, :] = h
            c = h[7:8, :]
        return c

    return lax.fori_loop(0, (hi - lo) // SCAN_UNROLL, step, carry)


def _scan_down(a_ref, b_ref, h_ref, lo, hi, carry):
    rid = _row_ids(8, LRU_BLOCK_W)
    assert (hi - lo) % SCAN_UNROLL == 0

    def step(g, c):
        base = pl.multiple_of((hi - (g + 1) * SCAN_UNROLL) * 8, 8)
        for u in reversed(range(SCAN_UNROLL)):
            sl = pl.ds(base + 8 * u, 8)
            a, b = _scan8_rev(a_ref[sl, :], b_ref[sl, :], rid)
            h = b + a * c
            h_ref[sl, :] = h
            c = h[0:1, :]
        return c

    return lax.fori_loop(0, (hi - lo) // SCAN_UNROLL, step, carry)


def _lru_scan_dir(d, a_ref, b_ref, h_ref):
    zero = jnp.zeros((1, LRU_BLOCK_W), F32)
    if d == 0:
        _scan_up(a_ref, b_ref, h_ref, 0, N_CHUNK, zero)
    else:
        c = _scan_down(a_ref, b_ref, h_ref, 0, CTX_CHUNKS, zero)
        _scan_down(a_ref, b_ref, h_ref, CTX_CHUNKS, N_CHUNK, c)


def _lru_in_specs():
    blk = lambda rows: pl.BlockSpec((rows, LRU_BLOCK_W), lambda b: (0, b))
    wspec = pl.BlockSpec((2, 1, LRU_BLOCK_W, LRU_BLOCK_W), lambda b: (0, b, 0, 0))
    return blk, wspec


def lru_fwd(p, conv_w, conv_b, wa, ba, wx, bx, lam, comm=None):
    blk, wspec = _lru_in_specs()

    def body(xr_ref, gx_ref, cw_ref, cb_ref, wa_ref, ba_ref, wx_ref, bx_ref, lam_ref, y_ref, a_s, b_s, h_s, hsum_s):
        xr = xr_ref[...]
        xc = _lru_conv(xr, cw_ref[...], cb_ref[...])
        xcb = xc.astype(BF16)
        for d in (0, 1):
            _, _, _, a, _, b = _lru_gates(xc, xcb, wa_ref[d, 0].astype(BF16), ba_ref[d:d + 1, :],
                                          wx_ref[d, 0].astype(BF16), bx_ref[d:d + 1, :], lam_ref[d:d + 1, :])
            a_s[...] = a
            b_s[...] = b
            _lru_scan_dir(d, a_s, b_s, h_s)
            if d == 0:
                hsum_s[...] = h_s[...]
            else:
                hsum_s[...] = hsum_s[...] + h_s[...]
        g, _ = _gelu_parts(gx_ref[CTX_LEN:, :])
        y_ref[...] = (hsum_s[CTX_LEN:, :] * g).astype(BF16)

    zs = pltpu.VMEM((ZLEN, LRU_BLOCK_W), F32)
    res, extra = _call(
        body, name="lru_fwd", grid=(LRU_BLOCKS,),
        in_specs=[blk(ZLEN), pl.BlockSpec((ZLEN, LRU_BLOCK_W), lambda b: (0, 24 + b)), blk(4), blk(1),
                  wspec, blk(2), wspec, blk(2), blk(2)],
        out_specs=[pl.BlockSpec((SEQ, LRU_BLOCK_W), lambda b: (0, b))],
        out_shape=[jax.ShapeDtypeStruct((SEQ, D_MODEL), BF16)],
        scratch_shapes=[zs, zs, zs, zs], sem=("arbitrary",),
        args=(p, p, conv_w, conv_b, wa, ba, wx, bx, lam), comm=comm)
    return res[0], extra


def _rope_tables():
    t = np.arange(SEQ)
    lane = np.arange(2 * HEAD_DIM)
    in_head = lane % HEAD_DIM
    j = (in_head % 32) % 16
    freq = ROPE_BASE ** (-j.astype(np.float64) / 16.0)
    pos = np.where(in_head[None, :] < 32, (t // GRID_W)[:, None], (t % GRID_W)[:, None]).astype(np.float64)
    ang = (pos.astype(np.float32) * freq.astype(np.float32)[None, :]).astype(np.float32)
    cos = np.cos(ang).astype(np.float32)
    sin = np.sin(ang).astype(np.float32)
    sgn = np.where((in_head % 32) < 16, -1.0, 1.0).astype(np.float32)
    cos = np.concatenate([np.ones((CTX_LEN, 2 * HEAD_DIM), np.float32), cos], 0)
    sin = np.concatenate([np.zeros((CTX_LEN, 2 * HEAD_DIM), np.float32), sin * sgn[None, :]], 0)
    return jnp.asarray(cos), jnp.asarray(sin)


def _head_ones():
    lane = np.arange(2 * HEAD_DIM)
    return jnp.asarray((lane[:, None] // HEAD_DIM == lane[None, :] // HEAD_DIM).astype(np.float32))


def _rope_partner(x):
    lane = lax.broadcasted_iota(jnp.int32, x.shape, 1)
    return jnp.where((lane % 32) < 16, pltpu.roll(x, 128 - 16, 1), pltpu.roll(x, 16, 1))


def _head_rms(x, ones, gain):
    ms = jnp.dot(x * x, ones, preferred_element_type=F32, precision=lax.Precision.HIGHEST) * (1.0 / HEAD_DIM)
    rstd = lax.rsqrt(ms + EPS)
    return x * rstd * gain, rstd


PREP_TILE = 768


def qkv_prep(p, qg2, kg2, cos, sin, ones, comm=None):
    scale = HEAD_DIM ** -0.5

    def body(q_ref, k_ref, v_ref, qg_ref, kg_ref, cos_ref, sin_ref, ones_ref, qr_ref, qp_ref, kk_ref, vv_ref):
        ones_m = ones_ref[...]
        c, s = cos_ref[...], sin_ref[...]
        qn, _ = _head_rms(q_ref[...], ones_m, qg_ref[...])
        qn = qn * scale
        qr_ref[...] = (qn * c + _rope_partner(qn) * s).astype(BF16)
        qp_ref[...] = qn.astype(BF16)
        kn, _ = _head_rms(k_ref[...], ones_m, kg_ref[...])
        kk_ref[...] = (kn * c + _rope_partner(kn) * s).astype(BF16)
        vv_ref[...] = v_ref[...].astype(BF16)

    col = lambda base: pl.BlockSpec((PREP_TILE, 128), lambda hp, i: (i, base + hp))
    small = pl.BlockSpec((1, 128), lambda hp, i: (0, 0))
    tab = pl.BlockSpec((PREP_TILE, 128), lambda hp, i: (i, 0))
    oshape = jax.ShapeDtypeStruct((ZLEN, D_MODEL), BF16)
    res, extra = _call(
        body, name="qkv_prep", grid=(NA_HEADS // 2, ZLEN // PREP_TILE),
        in_specs=[col(32), col(8), col(16), small, small, tab, tab, _full((128, 128))],
        out_specs=[col(0)] * 4, out_shape=[oshape] * 4, sem=("parallel", "parallel"),
        args=(p, p, p, qg2, kg2, cos, sin, ones), comm=comm)
    return (*res, extra)


def _bias_expand():
    qc = np.arange(GRID_W)[:, None]
    kc = np.arange(GRID_W)[None, :]
    col_start = np.clip(qc - NA_COLS // 2, 0, GRID_W - NA_COLS)
    in_win = (kc >= col_start) & (kc < col_start + NA_COLS)
    dc = np.clip(kc - qc, -(NA_COLS - 1), NA_COLS - 1) + (NA_COLS - 1)
    e = np.zeros((2 * NA_COLS - 1, GRID_W, GRID_W), np.float32)
    for d in range(2 * NA_COLS - 1):
        e[d] = ((dc == d) & in_win).astype(np.float32)
    pen = np.where(in_win, 0.0, NEG_INF).astype(np.float32)
    return e, pen


def bias_table(rpb2, comm=None):
    e, pen = _bias_expand()
    n_dr = 2 * NA_ROWS - 1
    ea = np.zeros((31, GRID_W, 128), np.float32)
    ea[:, :, :GRID_W] = e
    eb = np.zeros((31, GRID_W, 128), np.float32)
    eb[:, :, GRID_W:] = e
    pen2 = np.concatenate([pen, pen], 1)
    ea = jnp.asarray(ea.reshape(31, GRID_W * 128))
    eb = jnp.asarray(eb.reshape(31, GRID_W * 128))
    sel_a = np.zeros((BT_LEN, n_dr), np.float32)
    sel_b = np.zeros((BT_LEN, n_dr), np.float32)
    for r in range(BT_LEN):
        dr = r - BT_PAD
        if 0 <= dr < n_dr:
            sel_a[r, dr] = 1.0
        if 0 <= dr + 1 < n_dr:
            sel_b[r, dr + 1] = 1.0
    sel_a, sel_b = jnp.asarray(sel_a), jnp.asarray(sel_b)
    pen2 = jnp.asarray(pen2.reshape(1, GRID_W * 128))
    hi = lax.Precision.HIGHEST

    def body(rpb_ref, sa_ref, sb_ref, ea_ref, eb_ref, pen_ref, o_ref, ra_s, rb_s):
        for h in range(NA_HEADS):
            rp = rpb_ref[h]
            ra_s[h * BT_LEN:(h + 1) * BT_LEN, :] = jnp.dot(sa_ref[...], rp, preferred_element_type=F32, precision=hi)
            rb_s[h * BT_LEN:(h + 1) * BT_LEN, :] = jnp.dot(sb_ref[...], rp, preferred_element_type=F32, precision=hi)
        o_ref[...] = (jnp.dot(ra_s[...], ea_ref[...], preferred_element_type=F32, precision=hi)
                      + jnp.dot(rb_s[...], eb_ref[...], preferred_element_type=F32, precision=hi) + pen_ref[...])

    tcol = 2048
    rows = NA_HEADS * BT_LEN
    res, extra = _call(
        body, name="bias_table", grid=(GRID_W * 128 // tcol,),
        in_specs=[_full((NA_HEADS, n_dr, 31)), _full((BT_LEN, n_dr)), _full((BT_LEN, n_dr)),
                  pl.BlockSpec((31, tcol), lambda j: (0, j)), pl.BlockSpec((31, tcol), lambda j: (0, j)),
                  pl.BlockSpec((1, tcol), lambda j: (0, j))],
        out_specs=[pl.BlockSpec((rows, tcol), lambda j: (0, j))],
        out_shape=[jax.ShapeDtypeStruct((rows, GRID_W * 128), F32)],
        scratch_shapes=[pltpu.VMEM((rows, 31), F32), pltpu.VMEM((rows, 31), F32)], sem=("parallel",),
        args=(rpb2, sel_a, sel_b, ea, eb, pen2), comm=comm)
    return res[0].reshape(NA_HEADS, BT_LEN, GRID_W, 128), extra


def _key_window(j):
    ws = jnp.clip(Q_ROWS * j - 4, 0, GRID_ROWS - KEY_ROWS)
    return ws, pl.multiple_of(CTX_LEN + ws * GRID_W, 256)


def _head_mask(hh):
    lane = lax.broadcasted_iota(jnp.int32, (Q_TILE, 128), 1)
    return (lane < HEAD_DIM) if hh == 0 else (lane >= HEAD_DIM)


def _attn_scores(j, ws, q_rot_h, q_pl_h, kw, kc, hh, bt_ref, s_ref):
    s_ref[:, :KEY_TILE] = _dot_nt(q_rot_h, kw)
    s_ref[:, KEY_TILE:] = _dot_nt(q_pl_h, kc)
    lane = lax.broadcasted_iota(jnp.int32, (GRID_W, 128), 1)
    base = ws - Q_ROWS * j + (NA_ROWS - 1) + BT_PAD
    for qi in range(Q_ROWS):
        rs = jnp.clip(Q_ROWS * j + qi - NA_ROWS // 2, 0, GRID_ROWS - NA_ROWS)
        for m in range(KEY_ROWS // 2):
            k0 = ws + 2 * m
            p0 = jnp.where((k0 >= rs) & (k0 < rs + NA_ROWS), 0.0, NEG_INF)
            p1 = jnp.where((k0 + 1 >= rs) & (k0 + 1 < rs + NA_ROWS), 0.0, NEG_INF)
            pen = jnp.where(lane < GRID_W, p0, p1)
            rows = slice(qi * GRID_W, (qi + 1) * GRID_W)
            cols = slice(128 * m, 128 * (m + 1))
            s_ref[rows, cols] = s_ref[rows, cols] + bt_ref[hh, base + 2 * m - qi] + pen
    return base


def attn_fwd(q_rot, q_pl, kk, vv, bt, comm=None):
    def body(qr_ref, qp_ref, kk_ref, vv_ref, bt_ref, o_ref, lse_ref, s_ref):
        j = pl.program_id(1)
        ws, start = _key_window(j)
        win = pl.ds(start, KEY_TILE)
        kw, kc = kk_ref[win, :], kk_ref[:CTX_LEN, :]
        vw, vc = vv_ref[win, :], vv_ref[:CTX_LEN, :]
        qr, qp = qr_ref[...], qp_ref[...]
        outs = []
        for hh in range(2):
            msk = _head_mask(hh)
            _attn_scores(j, ws, jnp.where(msk, qr, 0), jnp.where(msk, qp, 0), kw, kc, hh, bt_ref, s_ref)
            s = s_ref[...]
            mx = jnp.max(s, axis=-1, keepdims=True)
            pr = jnp.exp(s - mx)
            l = jnp.sum(pr, axis=-1, keepdims=True)
            prb = pr.astype(BF16)
            o = jnp.dot(prb[:, :KEY_TILE], vw, preferred_element_type=F32)
            o = o + jnp.dot(prb[:, KEY_TILE:], vc, preferred_element_type=F32)
            outs.append(o / l)
            lse_ref[hh] = mx + jnp.log(l)
        o_ref[...] = jnp.where(_head_mask(0), outs[0], outs[1])

    qspec = pl.BlockSpec((Q_TILE, 128), lambda hp, j: (j + 1, hp))
    kspec = pl.BlockSpec((ZLEN, 128), lambda hp, j: (0, hp))
    res, extra = _call(
        body, name="attn_fwd", grid=(NA_HEADS // 2, SEQ // Q_TILE),
        in_specs=[qspec, qspec, kspec, kspec, pl.BlockSpec((2, BT_LEN, GRID_W, 128), lambda hp, j: (hp, 0, 0, 0))],
        out_specs=[pl.BlockSpec((Q_TILE, 128), lambda hp, j: (j, hp)),
                   pl.BlockSpec((2, Q_TILE, 1), lambda hp, j: (hp, j, 0))],
        out_shape=[jax.ShapeDtypeStruct((SEQ, D_MODEL), F32), jax.ShapeDtypeStruct((NA_HEADS, SEQ, 1), F32)],
        scratch_shapes=[pltpu.VMEM((Q_TILE, KEY_TILE + CTX_LEN), F32)], sem=("parallel", "arbitrary"),
        args=(q_rot, q_pl, kk, vv, bt), comm=comm)
    return res[0], res[1], extra


def merge_fwd(y_rnn, y_na, p, z, g2, w_rnn, w_na, w_out):
    def body(yr_ref, yn_ref, mr_ref, mn_ref, x_ref, g2_ref, wr_ref, wn_ref, wo_ref, u_ref, v_ref, mg_ref, out_ref, x1_ref):
        u = jnp.dot(yr_ref[...], wr_ref[...], preferred_element_type=F32)
        v = jnp.dot(yn_ref[...].astype(BF16), wn_ref[...], preferred_element_type=F32)
        merged = (_sigmoid(mr_ref[...]) * u + _sigmoid(mn_ref[...]) * v).astype(BF16)
        out = jnp.dot(merged, wo_ref[...], preferred_element_type=F32)
        u_ref[...] = u
        v_ref[...] = v
        mg_ref[...] = merged
        out_ref[...] = out
        x1_ref[...] = x_ref[...] + g2_ref[...] * out

    row = pl.BlockSpec((ROW_TILE, D_MODEL), lambda i: (i, 0))
    lat = lambda cb: pl.BlockSpec((ROW_TILE, D_MODEL), lambda i: (i + 1, cb))
    wspec = _full((D_MODEL, D_MODEL))
    f32o = jax.ShapeDtypeStruct((SEQ, D_MODEL), F32)
    return pl.pallas_call(
        body, name="merge_fwd", grid=(SEQ // ROW_TILE,),
        in_specs=[row, row, lat(5), lat(6), lat(0), _full((1, D_MODEL)), wspec, wspec, wspec],
        out_specs=[row] * 5,
        out_shape=[f32o, f32o, jax.ShapeDtypeStruct((SEQ, D_MODEL), BF16), f32o, f32o],
        compiler_params=_params("parallel"),
    )(y_rnn, y_na, p, p, z, g2, w_rnn, w_na, w_out)


FF_TILE = 256
FF_TILES = D_FF // FF_TILE


def _ffn_conv(h, cw, cb):
    row = _row_ids(SEQ, FF_TILE)
    prev = jnp.where(row >= 1, pltpu.roll(h, 1, 0), 0.0)
    nxt = jnp.where(row < SEQ - 1, pltpu.roll(h, SEQ - 1, 0), 0.0)
    return prev * cw[0:1, :] + h * cw[1:2, :] + nxt * cw[2:3, :] + cb


def ffn_act(hpre, conv_w, conv_b):
    def body(ha_ref, hg_ref, wa_ref, wg_ref, ba_ref, bg_ref, o_ref):
        a = _ffn_conv(ha_ref[...], wa_ref[...], ba_ref[...])
        g = _ffn_conv(hg_ref[...], wg_ref[...], bg_ref[...])
        o_ref[...] = (a * _sigmoid(a) * g).astype(BF16)

    col = lambda rows, off: pl.BlockSpec((rows, FF_TILE), lambda j: (0, j + off))
    return pl.pallas_call(
        body, name="ffn_act", grid=(FF_TILES,),
        in_specs=[col(SEQ, 0), col(SEQ, FF_TILES), col(3, 0), col(3, FF_TILES), col(1, 0), col(1, FF_TILES)],
        out_specs=col(SEQ, 0),
        out_shape=jax.ShapeDtypeStruct((SEQ, D_FF), BF16),
        compiler_params=_params("parallel"),
    )(hpre, hpre, conv_w, conv_w, conv_b, conv_b)


def ffn_down_loss(act, w_down, x1, g5, target):
    def body(a_ref, w_ref, x1_ref, g5_ref, t_ref, f_ref, dy_ref, df_ref, ls_ref, dg_ref):
        i = pl.program_id(0)
        f = jnp.dot(a_ref[...], w_ref[...], preferred_element_type=F32)
        g5 = g5_ref[...]
        err = x1_ref[...] + g5 * f - t_ref[...]
        dy = err * (1.0 / D_MODEL)
        f_ref[...] = f
        dy_ref[...] = dy
        df_ref[...] = (dy * g5).astype(BF16)

        @pl.when(i == 0)
        def _():
            ls_ref[...] = jnp.zeros_like(ls_ref)
            dg_ref[...] = jnp.zeros_like(dg_ref)

        ls_ref[...] = ls_ref[...] + jnp.sum(err * err)
        dg_ref[...] = dg_ref[...] + jnp.sum(dy * f, axis=0, keepdims=True)

    row = pl.BlockSpec((ROW_TILE, D_MODEL), lambda i: (i, 0))
    f32o = jax.ShapeDtypeStruct((SEQ, D_MODEL), F32)
    return pl.pallas_call(
        body, name="ffn_down_loss", grid=(SEQ // ROW_TILE,),
        in_specs=[pl.BlockSpec((ROW_TILE, D_FF), lambda i: (i, 0)), _full((D_FF, D_MODEL)), row, _full((1, D_MODEL)), row],
        out_specs=[row, row, row, _full((8, 128)), _full((1, D_MODEL))],
        out_shape=[f32o, f32o, jax.ShapeDtypeStruct((SEQ, D_MODEL), BF16), jax.ShapeDtypeStruct((8, 128), F32),
                   jax.ShapeDtypeStruct((1, D_MODEL), F32)],
        compiler_params=_params("arbitrary"),
    )(act, w_down, x1, g5, target)


def ffn_down_bwd(df, w_down):
    def body(df_ref, w_ref, o_ref):
        o_ref[...] = _dot_nt(df_ref[...], w_ref[...])

    return pl.pallas_call(
        body, name="ffn_down_bwd", grid=(SEQ // ROW_TILE,),
        in_specs=[pl.BlockSpec((ROW_TILE, D_MODEL), lambda i: (i, 0)), _full((D_FF, D_MODEL))],
        out_specs=pl.BlockSpec((ROW_TILE, D_FF), lambda i: (i, 0)),
        out_shape=jax.ShapeDtypeStruct((SEQ, D_FF), F32),
        compiler_params=_params("parallel"),
    )(df, w_down)


def ffn_act_bwd(hpre, d_act, conv_w, conv_b):
    def half_bwd(dc, h, w, dh_ref, dw_ref, db_ref):
        row = _row_ids(SEQ, FF_TILE)
        h_prev = jnp.where(row >= 1, pltpu.roll(h, 1, 0), 0.0)
        h_next = jnp.where(row < SEQ - 1, pltpu.roll(h, SEQ - 1, 0), 0.0)
        dw_ref[0:1, :] = jnp.sum(dc * h_prev, axis=0, keepdims=True)
        dw_ref[1:2, :] = jnp.sum(dc * h, axis=0, keepdims=True)
        dw_ref[2:3, :] = jnp.sum(dc * h_next, axis=0, keepdims=True)
        db_ref[...] = jnp.sum(dc, axis=0, keepdims=True)
        dc_next = jnp.where(row < SEQ - 1, pltpu.roll(dc, SEQ - 1, 0), 0.0)
        dc_prev = jnp.where(row >= 1, pltpu.roll(dc, 1, 0), 0.0)
        dh_ref[...] = (dc_next * w[0:1, :] + dc * w[1:2, :] + dc_prev * w[2:3, :]).astype(BF16)

    def body(ha_ref, hg_ref, da_ref, wa_ref, wg_ref, ba_ref, bg_ref, dha_ref, dhg_ref, dwa_ref, dwg_ref, dba_ref, dbg_ref):
        ha, hg = ha_ref[...], hg_ref[...]
        a = _ffn_conv(ha, wa_ref[...], ba_ref[...])
        g = _ffn_conv(hg, wg_ref[...], bg_ref[...])
        sig = _sigmoid(a)
        dact = da_ref[...]
        half_bwd(dact * g * (sig * (1.0 + a * (1.0 - sig))), ha, wa_ref[...], dha_ref, dwa_ref, dba_ref)
        half_bwd(dact * a * sig, hg, wg_ref[...], dhg_ref, dwg_ref, dbg_ref)

    col = lambda rows, off: pl.BlockSpec((rows, FF_TILE), lambda j: (0, j + off))
    hshape = jax.ShapeDtypeStruct((SEQ, D_FF), BF16)
    wshape = jax.ShapeDtypeStruct((3, D_FF), F32)
    bshape = jax.ShapeDtypeStruct((1, D_FF), F32)
    return pl.pallas_call(
        body, name="ffn_act_bwd", grid=(FF_TILES,),
        in_specs=[col(SEQ, 0), col(SEQ, FF_TILES), col(SEQ, 0), col(3, 0), col(3, FF_TILES), col(1, 0), col(1, FF_TILES)],
        out_specs=[col(SEQ, 0), col(SEQ, 0), col(3, 0), col(3, 0), col(1, 0), col(1, 0)],
        out_shape=[hshape, hshape, wshape, wshape, bshape, bshape],
        compiler_params=_params("parallel"),
    )(hpre, hpre, d_act, conv_w, conv_w, conv_b, conv_b)


def _norm_mod_bwd(x, dxn, gain, scale):
    rstd = lax.rsqrt(jnp.mean(x * x, axis=-1, keepdims=True) + EPS)
    nrm = x * rstd
    dsh = jnp.sum(dxn, axis=0, keepdims=True)
    dsc = jnp.sum(dxn * nrm, axis=0, keepdims=True) * gain
    dgn = jnp.sum(dxn * nrm, axis=0, keepdims=True) * (1.0 + scale)
    dn = dxn * (gain * (1.0 + scale))
    dx = rstd * (dn - nrm * jnp.mean(dn * nrm, axis=-1, keepdims=True))
    return dx, dsh, dsc, dgn


def ffn_up_bwd(dha, dhg, w_up, x1, dy, gain, scale):
    def body(dha_ref, dhg_ref, w_ref, x_ref, dy_ref, g_ref, sc_ref, dx_ref, dsh_ref, dsc_ref, dgn_ref):
        i = pl.program_id(0)
        dxn = _dot_nt(dha_ref[...], w_ref[:, :D_FF]) + _dot_nt(dhg_ref[...], w_ref[:, D_FF:])
        dx, dsh, dsc, dgn = _norm_mod_bwd(x_ref[...], dxn, g_ref[...], sc_ref[...])
        dx_ref[...] = dy_ref[...] + dx

        @pl.when(i == 0)
        def _():
            dsh_ref[...] = dsh
            dsc_ref[...] = dsc
            dgn_ref[...] = dgn

        @pl.when(i > 0)
        def _():
            dsh_ref[...] = dsh_ref[...] + dsh
            dsc_ref[...] = dsc_ref[...] + dsc
            dgn_ref[...] = dgn_ref[...] + dgn

    row = pl.BlockSpec((ROW_TILE, D_MODEL), lambda i: (i, 0))
    vec = _full((1, D_MODEL))
    vshape = jax.ShapeDtypeStruct((1, D_MODEL), F32)
    return pl.pallas_call(
        body, name="ffn_up_bwd", grid=(SEQ // ROW_TILE,),
        in_specs=[pl.BlockSpec((ROW_TILE, D_FF), lambda i: (i, 0)), pl.BlockSpec((ROW_TILE, D_FF), lambda i: (i, 0)),
                  _full((D_MODEL, 2 * D_FF)), row, row, vec, vec],
        out_specs=[row, vec, vec, vec],
        out_shape=[jax.ShapeDtypeStruct((SEQ, D_MODEL), F32), vshape, vshape, vshape],
        compiler_params=_params("arbitrary"),
    )(dha, dhg, w_up, x1, dy, gain, scale)


def merge_bwd(dx1, out, g2, p, u, v, w_rnn, w_na, w_out, comm=None):
    def body(dx_ref, out_ref, g2_ref, mr_ref, mn_ref, u_ref, v_ref, wr_ref, wn_ref, wo_ref,
             dout_ref, du_ref, dv_ref, dmr_ref, dmn_ref, dyr_ref, dyn_ref, dg2_ref):
        i = pl.program_id(0)

        @pl.when(i == 0)
        def _():
            dmr_ref[...] = jnp.zeros_like(dmr_ref)
            dmn_ref[...] = jnp.zeros_like(dmn_ref)
            dg2_ref[...] = jnp.zeros_like(dg2_ref)

        @pl.when(i > 0)
        def _():
            dx = dx_ref[...]
            dg2_ref[...] = dg2_ref[...] + jnp.sum(dx * out_ref[...], axis=0, keepdims=True)
            dout = (dx * g2_ref[...]).astype(BF16)
            dout_ref[...] = dout
            dm = _dot_nt(dout, wo_ref[...])
            sr = _sigmoid(mr_ref[...])
            sn = _sigmoid(mn_ref[...])
            du = (dm * sr).astype(BF16)
            dv = (dm * sn).astype(BF16)
            du_ref[...] = du
            dv_ref[...] = dv
            dmr_ref[...] = (dm * u_ref[...] * (sr * (1.0 - sr))).astype(BF16)
            dmn_ref[...] = (dm * v_ref[...] * (sn * (1.0 - sn))).astype(BF16)
            dyr_ref[...] = _dot_nt(du, wr_ref[...])
            dyn_ref[...] = _dot_nt(dv, wn_ref[...])

    lat = pl.BlockSpec((ROW_TILE, D_MODEL), lambda i: (jnp.maximum(i - 1, 0), 0))
    zrow = pl.BlockSpec((ROW_TILE, D_MODEL), lambda i: (i, 0))
    pcol = lambda cb: pl.BlockSpec((ROW_TILE, D_MODEL), lambda i: (i, cb))
    wspec = _full((D_MODEL, D_MODEL))
    tb = jax.ShapeDtypeStruct((SEQ, D_MODEL), BF16)
    zb = jax.ShapeDtypeStruct((ZLEN, D_MODEL), BF16)
    tf = jax.ShapeDtypeStruct((SEQ, D_MODEL), F32)
    res, extra = _call(
        body, name="merge_bwd", grid=(ZLEN // ROW_TILE,),
        in_specs=[lat, lat, _full((1, D_MODEL)), pcol(5), pcol(6), lat, lat, wspec, wspec, wspec],
        out_specs=[lat, lat, lat, zrow, zrow, lat, lat, _full((1, D_MODEL))],
        out_shape=[tb, tb, tb, zb, zb, tf, tf, jax.ShapeDtypeStruct((1, D_MODEL), F32)],
        sem=("arbitrary",), args=(dx1, out, g2, p, p, u, v, w_rnn, w_na, w_out), comm=comm)
    return (*res, extra)


def attn_bwd(q_rot, q_pl, kk, vv, bt, y_na, d_yna, lse, comm=None):
    def body(qr_ref, qp_ref, kk_ref, vv_ref, bt_ref, o_ref, do_ref, lse_ref,
             dqr_ref, dqp_ref, dk_ref, dv_ref, dbt_ref, s_ref):
        jj = pl.program_id(1)

        @pl.when(jj == 0)
        def _():
            dqr_ref[...] = jnp.zeros_like(dqr_ref)
            dqp_ref[...] = jnp.zeros_like(dqp_ref)
            dk_ref[...] = jnp.zeros_like(dk_ref)
            dv_ref[...] = jnp.zeros_like(dv_ref)
            dbt_ref[...] = jnp.zeros_like(dbt_ref)

        @pl.when(jj > 0)
        def _():
            j = jj - 1
            ws, start = _key_window(j)
            win = pl.ds(start, KEY_TILE)
            kw, kc = kk_ref[win, :], kk_ref[:CTX_LEN, :]
            vw, vc = vv_ref[win, :], vv_ref[:CTX_LEN, :]
            qr, qp = qr_ref[...], qp_ref[...]
            do = do_ref[...]
            do_o = do * o_ref[...]
            dq_r, dq_p = [], []
            for hh in range(2):
                msk = _head_mask(hh)
                q_r, q_p = jnp.where(msk, qr, 0), jnp.where(msk, qp, 0)
                base = _attn_scores(j, ws, q_r, q_p, kw, kc, hh, bt_ref, s_ref)
                pr = jnp.exp(s_ref[...] - lse_ref[hh])
                delta = jnp.sum(jnp.where(msk, do_o, 0.0), axis=-1, keepdims=True)
                dob = jnp.where(msk, do, 0.0).astype(BF16)
                ds_lat = pr[:, :KEY_TILE] * (_dot_nt(dob, vw) - delta)
                ds_ctx = pr[:, KEY_TILE:] * (_dot_nt(dob, vc) - delta)
                for qi in range(Q_ROWS):
                    for m in range(KEY_ROWS // 2):
                        idx = base + 2 * m - qi
                        dbt_ref[hh, idx] = dbt_ref[hh, idx] + ds_lat[qi * GRID_W:(qi + 1) * GRID_W, 128 * m:128 * (m + 1)]
                dsb_lat = ds_lat.astype(BF16)
                dsb_ctx = ds_ctx.astype(BF16)
                prb = pr.astype(BF16)
                dq_r.append(jnp.dot(dsb_lat, kw, preferred_element_type=F32))
                dq_p.append(jnp.dot(dsb_ctx, kc, preferred_element_type=F32))
                dk_ref[win, :] = dk_ref[win, :] + _dot_tn(dsb_lat, q_r)
                dk_ref[:CTX_LEN, :] = dk_ref[:CTX_LEN, :] + _dot_tn(dsb_ctx, q_p)
                dv_ref[win, :] = dv_ref[win, :] + _dot_tn(prb[:, :KEY_TILE], dob)
                dv_ref[:CTX_LEN, :] = dv_ref[:CTX_LEN, :] + _dot_tn(prb[:, KEY_TILE:], dob)
            dqr_ref[...] = jnp.where(_head_mask(0), dq_r[0], dq_r[1])
            dqp_ref[...] = jnp.where(_head_mask(0), dq_p[0], dq_p[1])

    lat = lambda jj: jnp.maximum(jj - 1, 0)
    qspec = pl.BlockSpec((Q_TILE, 128), lambda hp, jj: (lat(jj) + 1, hp))
    kspec = pl.BlockSpec((ZLEN, 128), lambda hp, jj: (0, hp))
    btspec = pl.BlockSpec((2, BT_LEN, GRID_W, 128), lambda hp, jj: (hp, 0, 0, 0))
    ospec = pl.BlockSpec((Q_TILE, 128), lambda hp, jj: (lat(jj), hp))
    dqspec = pl.BlockSpec((Q_TILE, 128), lambda hp, jj: (jj, hp))
    zshape = jax.ShapeDtypeStruct((ZLEN, D_MODEL), F32)
    res, extra = _call(
        body, name="attn_bwd", grid=(NA_HEADS // 2, ZLEN // Q_TILE),
        in_specs=[qspec, qspec, kspec, kspec, btspec, ospec, ospec,
                  pl.BlockSpec((2, Q_TILE, 1), lambda hp, jj: (hp, lat(jj), 0))],
        out_specs=[dqspec, dqspec, kspec, kspec, btspec],
        out_shape=[zshape, zshape, zshape, zshape, jax.ShapeDtypeStruct((NA_HEADS, BT_LEN, GRID_W, 128), F32)],
        scratch_shapes=[pltpu.VMEM((Q_TILE, KEY_TILE + CTX_LEN), F32)], sem=("parallel", "arbitrary"),
        args=(q_rot, q_pl, kk, vv, bt, y_na, d_yna, lse), comm=comm)
    return (*res, extra)


def qkv_bwd(dq_rot, dq_pl, dk, dv, p, qg2, kg2, cos, sin, ones, comm=None):
    scale = HEAD_DIM ** -0.5
    n_hp, n_i = NA_HEADS // 2, ZLEN // PREP_TILE

    def norm_rope_bwd(d_rot, d_extra, x, gain, cos_t, sin_t, ones_m, dx_ref, acc_ref):
        xh, rstd = _head_rms(x, ones_m, 1.0)
        dn = d_rot * cos_t + _rope_partner(d_rot * sin_t)
        if d_extra is not None:
            dn = (dn + d_extra) * scale
        acc_ref[...] = acc_ref[...] + jnp.sum(dn * xh, axis=0, keepdims=True)
        dxh = dn * gain
        seg = jnp.dot(dxh * xh, ones_m, preferred_element_type=F32, precision=lax.Precision.HIGHEST) * (1.0 / HEAD_DIM)
        dx_ref[...] = (rstd * (dxh - xh * seg)).astype(BF16)

    def body(dqr_ref, dqp_ref, dk_ref, dv_ref, xq_ref, xk_ref, qg_ref, kg_ref, cos_ref, sin_ref, ones_ref,
             dxq_ref, dxk_ref, dxv_ref, dgq_ref, dgk_ref, accq_ref, acck_ref):
        hp, i = pl.program_id(0), pl.program_id(1)

        @pl.when((hp == 0) & (i == 0))
        def _():
            accq_ref[...] = jnp.zeros_like(accq_ref)
            acck_ref[...] = jnp.zeros_like(acck_ref)

        ones_m = ones_ref[...]
        cos_t, sin_t = cos_ref[...], sin_ref[...]
        norm_rope_bwd(dqr_ref[...], dqp_ref[...], xq_ref[...], qg_ref[...], cos_t, sin_t, ones_m, dxq_ref, accq_ref)
        norm_rope_bwd(dk_ref[...], None, xk_ref[...], kg_ref[...], cos_t, sin_t, ones_m, dxk_ref, acck_ref)
        dxv_ref[...] = dv_ref[...].astype(BF16)

        @pl.when((hp == n_hp - 1) & (i == n_i - 1))
        def _():
            dgq_ref[...] = accq_ref[:, :HEAD_DIM] + accq_ref[:, HEAD_DIM:]
            dgk_ref[...] = acck_ref[:, :HEAD_DIM] + acck_ref[:, HEAD_DIM:]

    col = lambda base: pl.BlockSpec((PREP_TILE, 128), lambda hp, i: (i, base + hp))
    small = pl.BlockSpec((1, 128), lambda hp, i: (0, 0))
    tab = pl.BlockSpec((PREP_TILE, 128), lambda hp, i: (i, 0))
    zb = jax.ShapeDtypeStruct((ZLEN, D_MODEL), BF16)
    gshape = jax.ShapeDtypeStruct((1, HEAD_DIM), F32)
    res, extra = _call(
        body, name="qkv_bwd", grid=(n_hp, n_i),
        in_specs=[col(0)] * 4 + [col(32), col(8), small, small, tab, tab, _full((128, 128))],
        out_specs=[col(0)] * 3 + [_full((1, HEAD_DIM))] * 2,
        out_shape=[zb, zb, zb, gshape, gshape],
        scratch_shapes=[pltpu.VMEM((1, 128), F32)] * 2, sem=("arbitrary", "arbitrary"),
        args=(dq_rot, dq_pl, dk, dv, p, p, qg2, kg2, cos, sin, ones), comm=comm)
    return (*res, extra)


def rpb_grad(dbt):
    e, _ = _bias_expand()
    n_dr = 2 * NA_ROWS - 1
    ea = np.zeros((31, GRID_W, 128), np.float32)
    ea[:, :, :GRID_W] = e
    eb = np.zeros((31, GRID_W, 128), np.float32)
    eb[:, :, GRID_W:] = e
    eat = jnp.asarray(ea.reshape(31, GRID_W * 128).T.copy())
    ebt = jnp.asarray(eb.reshape(31, GRID_W * 128).T.copy())
    sel_at = np.zeros((n_dr, BT_LEN), np.float32)
    sel_bt = np.zeros((n_dr, BT_LEN), np.float32)
    for r in range(BT_LEN):
        dr = r - BT_PAD
        if 0 <= dr < n_dr:
            sel_at[dr, r] = 1.0
        if 0 <= dr + 1 < n_dr:
            sel_bt[dr + 1, r] = 1.0
    hi = lax.Precision.HIGHEST

    tk = 2048
    wide = GRID_W * 128
    rows = NA_HEADS * BT_LEN
    n_k = wide // tk

    def body(d_ref, sa_ref, sb_ref, ea_ref, eb_ref, o_ref, a_s, b_s):
        k = pl.program_id(0)
        dm = d_ref[...]
        a = jnp.dot(dm, ea_ref[...], preferred_element_type=F32, precision=hi)
        b = jnp.dot(dm, eb_ref[...], preferred_element_type=F32, precision=hi)

        @pl.when(k == 0)
        def _():
            a_s[...] = a
            b_s[...] = b

        @pl.when(k > 0)
        def _():
            a_s[...] = a_s[...] + a
            b_s[...] = b_s[...] + b

        @pl.when(k == n_k - 1)
        def _():
            for h in range(NA_HEADS):
                sl = slice(h * BT_LEN, (h + 1) * BT_LEN)
                o_ref[h] = (jnp.dot(sa_ref[...], a_s[sl, :], preferred_element_type=F32, precision=hi)
                            + jnp.dot(sb_ref[...], b_s[sl, :], preferred_element_type=F32, precision=hi))

    return pl.pallas_call(
        body, name="rpb_grad", grid=(n_k,),
        in_specs=[pl.BlockSpec((rows, tk), lambda k: (0, k)), _full((n_dr, BT_LEN)), _full((n_dr, BT_LEN)),
                  pl.BlockSpec((tk, 31), lambda k: (k, 0)), pl.BlockSpec((tk, 31), lambda k: (k, 0))],
        out_specs=_full((NA_HEADS, n_dr, 31)),
        out_shape=jax.ShapeDtypeStruct((NA_HEADS, n_dr, 31), F32),
        scratch_shapes=[pltpu.VMEM((rows, 31), F32), pltpu.VMEM((rows, 31), F32)],
        compiler_params=_params("arbitrary"),
    )(dbt.reshape(rows, wide), jnp.asarray(sel_at), jnp.asarray(sel_bt), eat, ebt)


def lru_bwd(p, d_yrnn, conv_w, conv_b, wa, ba, wx, bx, lam, comm=None):
    blk, wspec = _lru_in_specs()

    def body(xr_ref, gx_ref, dy_ref, cw_ref, cb_ref, wa_ref, ba_ref, wx_ref, bx_ref, lam_ref,
             dxr_ref, dgx_ref, dcw_ref, dcb_ref, dwa_ref, dba_ref, dwx_ref, dbx_ref, dlam_ref,
             a_s, b_s, h_s, l_s, hsum_s, dxc_s, dh_s):
        xr = xr_ref[...]
        cw = cw_ref[...]
        xc = _lru_conv(xr, cw, cb_ref[...])
        xcb = xc.astype(BF16)
        g, dg = _gelu_parts(gx_ref[CTX_LEN:, :])
        dy = dy_ref[...]
        dh_s[:CTX_LEN, :] = jnp.zeros((CTX_LEN, LRU_BLOCK_W), F32)
        dh_s[CTX_LEN:, :] = dy * g
        row = _row_ids(ZLEN, LRU_BLOCK_W)
        zero = jnp.zeros((1, LRU_BLOCK_W), F32)
        for d in (0, 1):
            wab = wa_ref[d, 0].astype(BF16)
            wxb = wx_ref[d, 0].astype(BF16)
            lam_d = lam_ref[d:d + 1, :]
            r, gi, sp, a, sq, b = _lru_gates(xc, xcb, wab, ba_ref[d:d + 1, :], wxb, bx_ref[d:d + 1, :], lam_d)
            a_s[...] = a
            b_s[...] = b
            _lru_scan_dir(d, a_s, b_s, h_s)
            h = h_s[...]
            if d == 0:
                hsum_s[...] = h
                h_prev = jnp.where(row >= 1, pltpu.roll(h, 1, 0), 0.0)
                a_s[...] = pltpu.roll(a, ZLEN - 1, 0)
                _scan_down(a_s, dh_s, l_s, 0, N_CHUNK, zero)
            else:
                hsum_s[...] = hsum_s[...] + h
                h_prev = jnp.where(row == CTX_LEN - 1, 0.0, pltpu.roll(h, ZLEN - 1, 0))
                a_s[...] = pltpu.roll(a, 1, 0)
                c = _scan_up(a_s, dh_s, l_s, CTX_CHUNKS, N_CHUNK, zero)
                _scan_up(a_s, dh_s, l_s, 0, CTX_CHUNKS, c)
            db = l_s[...]
            da = db * h_prev
            dsq = db * gi * xc
            dgi = db * sq * xc
            dxc_d = db * sq * gi
            dla = da * a - dsq * (a * a) / sq
            dr = dla * ((-LRU_C) * sp)
            dsp = jnp.sum(dla * ((-LRU_C) * r), axis=0, keepdims=True)
            dlam_ref[d:d + 1, :] = -dsp * _sigmoid(-lam_d)
            dzr = dr * r * (1.0 - r)
            dzi = dgi * gi * (1.0 - gi)
            dba_ref[d:d + 1, :] = jnp.sum(dzr, axis=0, keepdims=True)
            dbx_ref[d:d + 1, :] = jnp.sum(dzi, axis=0, keepdims=True)
            dzrb = dzr.astype(BF16)
            dzib = dzi.astype(BF16)
            dwa_ref[d, 0] = _dot_tn(xcb, dzrb)
            dwx_ref[d, 0] = _dot_tn(xcb, dzib)
            dxc_d = dxc_d + _dot_nt(dzrb, wab) + _dot_nt(dzib, wxb)
            if d == 0:
                dxc_s[...] = dxc_d
            else:
                dxc_s[...] = dxc_s[...] + dxc_d
        dxc = dxc_s[...]
        dxr_ref[...] = _lru_conv_t(dxc, cw).astype(BF16)
        dcb_ref[...] = jnp.sum(dxc, axis=0, keepdims=True)
        segpos = jnp.where(row < CTX_LEN, row, row - CTX_LEN)
        seglen = jnp.where(row < CTX_LEN, CTX_LEN, SEQ)
        for k in range(4):
            off = k - 2
            if off == 0:
                sh = xr
            else:
                ok = (segpos + off >= 0) & (segpos + off < seglen)
                sh = jnp.where(ok, pltpu.roll(xr, (-off) % ZLEN, 0), 0.0)
            dcw_ref[k:k + 1, :] = jnp.sum(dxc * sh, axis=0, keepdims=True)
        dgx_ref[:CTX_LEN, :] = jnp.zeros((CTX_LEN, LRU_BLOCK_W), BF16)
        dgx_ref[CTX_LEN:, :] = (dy * hsum_s[CTX_LEN:, :] * dg).astype(BF16)

    zs = pltpu.VMEM((ZLEN, LRU_BLOCK_W), F32)
    zb = jax.ShapeDtypeStruct((ZLEN, D_MODEL), BF16)
    v2 = jax.ShapeDtypeStruct((2, D_MODEL), F32)
    w4 = jax.ShapeDtypeStruct((2, LRU_BLOCKS, LRU_BLOCK_W, LRU_BLOCK_W), F32)
    res, extra = _call(
        body, name="lru_bwd", grid=(LRU_BLOCKS,),
        in_specs=[blk(ZLEN), pl.BlockSpec((ZLEN, LRU_BLOCK_W), lambda b: (0, 24 + b)), blk(SEQ), blk(4), blk(1),
                  wspec, blk(2), wspec, blk(2), blk(2)],
        out_specs=[blk(ZLEN), blk(ZLEN), blk(4), blk(1), wspec, blk(2), wspec, blk(2), blk(2)],
        out_shape=[zb, zb, jax.ShapeDtypeStruct((4, D_MODEL), F32), jax.ShapeDtypeStruct((1, D_MODEL), F32),
                   w4, v2, w4, v2, v2],
        scratch_shapes=[zs] * 7, sem=("arbitrary",),
        args=(p, p, d_yrnn, conv_w, conv_b, wa, ba, wx, bx, lam), comm=comm)
    return (*res, extra)


def in_proj_bwd(dgs, w_in, z, dx1, gain, scale, comm=None):
    def body(*refs):
        dg_refs = refs[:7]
        w_ref, z_ref, dx1_ref, g_ref, sc_ref, gx_ref, dsh_ref, dsc_ref, dgn_ref = refs[7:]
        i = pl.program_id(0)
        dxn = _dot_nt(dg_refs[0][...], w_ref[:, 0:D_MODEL])
        for g in range(1, 7):
            dxn = dxn + _dot_nt(dg_refs[g][...], w_ref[:, g * D_MODEL:(g + 1) * D_MODEL])
        dx, dsh, dsc, dgn = _norm_mod_bwd(z_ref[...], dxn, g_ref[...], sc_ref[0])

        @pl.when(i <= 1)
        def _():
            dsh_ref[0] = dsh
            dsc_ref[0] = dsc

        @pl.when(i > 1)
        def _():
            dsh_ref[0] = dsh_ref[0] + dsh
            dsc_ref[0] = dsc_ref[0] + dsc

        @pl.when(i == 0)
        def _():
            dgn_ref[...] = dgn

        @pl.when(i > 0)
        def _():
            dgn_ref[...] = dgn_ref[...] + dgn
            gx_ref[...] = dx1_ref[...] + dx

    zrow = pl.BlockSpec((ROW_TILE, D_MODEL), lambda i: (i, 0))
    lat = pl.BlockSpec((ROW_TILE, D_MODEL), lambda i: (jnp.maximum(i - 1, 0), 0))
    mod = pl.BlockSpec((1, 1, D_MODEL), lambda i: (jnp.minimum(i, 1), 0, 0))
    mshape = jax.ShapeDtypeStruct((2, 1, D_MODEL), F32)
    res, extra = _call(
        body, name="in_proj_bwd", grid=(ZLEN // ROW_TILE,),
        in_specs=[zrow] * 7 + [_full((D_MODEL, IN_COLS)), zrow, lat, _full((1, D_MODEL)), mod],
        out_specs=[lat, mod, mod, _full((1, D_MODEL))],
        out_shape=[jax.ShapeDtypeStruct((SEQ, D_MODEL), F32), mshape, mshape, jax.ShapeDtypeStruct((1, D_MODEL), F32)],
        sem=("arbitrary",), args=(*dgs, w_in, z, dx1, gain, scale), comm=comm)
    return (*res, extra)


def matmul_tn(a, b, name, tm, tn, prev=None, col_block=0, total_cols=None):
    k, m = a.shape
    n = b.shape[1]
    total_cols = n if total_cols is None else total_cols
    assert m % tm == 0 and n % tn == 0
    off = col_block * (n // tn)

    def body(a_ref, b_ref, *rest):
        rest[-1][...] = _dot_tn(a_ref[...].astype(BF16), b_ref[...]).astype(BF16)

    in_specs = [pl.BlockSpec((k, tm), lambda i, j: (0, i)), pl.BlockSpec((k, tn), lambda i, j: (0, j))]
    args = [a, b]
    aliases = {}
    if prev is not None:
        in_specs.append(pl.BlockSpec(memory_space=pl.ANY))
        args.append(prev)
        aliases = {2: 0}
    return pl.pallas_call(
        body, name=name, grid=(m // tm, n // tn), in_specs=in_specs,
        out_specs=pl.BlockSpec((tm, tn), lambda i, j: (i, j + off)),
        out_shape=jax.ShapeDtypeStruct((m, total_cols), BF16),
        input_output_aliases=aliases,
        compiler_params=_params("parallel", "parallel"),
    )(*args)


def local_step(z, target, modx, modc, norm_mix_g, norm_ffn_g, w_in, conv_w, conv_b, wa, ba, wx, bx, lam, qg, kg, rpb,
               w_rnn, w_na, w_out, w_up, fconv_w, fconv_b, w_down, idx=None, bt=None):
    dist = idx is not None
    c_idx = idx[1:2] if dist else None
    d = D_MODEL
    mx = [modx[:, k * d:(k + 1) * d] for k in range(N_MOD)]
    shift = jnp.stack([modc[:, 0:d], mx[0]])
    scale = jnp.stack([modc[:, d:2 * d], mx[1]])
    cos, sin = _rope_tables()
    ones = _head_ones()
    qg2 = jnp.tile(qg, (1, 2))
    kg2 = jnp.tile(kg, (1, 2))

    xn = norm_mod(z, norm_mix_g, shift, scale, "norm_mix")
    if bt is None:
        bt, _ = bias_table(rpb)
    p, _ = matmul_wide(xn, w_in, "in_proj", 3 * ROW_TILE, 1792)
    y_rnn, got = lru_fwd(p, conv_w, conv_b, wa, ba, wx, bx, lam,
                         comm=gather_weights_comm([w_down], [5]) if dist else None)
    if dist:
        w_down = got[0]
    q_rot, q_pl, kk, vv, got = qkv_prep(p, qg2, kg2, cos, sin, ones,
                                        comm=gather_weights_comm([w_rnn, w_na, w_out], [1, 2, 3]) if dist else None)
    if dist:
        w_rnn, w_na, w_out = got
    y_na, lse, got = attn_fwd(q_rot, q_pl, kk, vv, bt, comm=gather_weights_comm([w_up], [4]) if dist else None)
    if dist:
        w_up = got[0]
    u, v, merged, out, x1 = merge_fwd(y_rnn, y_na, p, z, mx[2], w_rnn, w_na, w_out)
    xn2 = norm_mod(x1, norm_ffn_g, mx[3][None], mx[4][None], "norm_ffn")
    hpre, _ = matmul_wide(xn2, w_up, "ffn_up", 2 * ROW_TILE, 1408)
    act = ffn_act(hpre, fconv_w, fconv_b)
    f, dy, df, loss_sq, dg5 = ffn_down_loss(act, w_down, x1, mx[5], target)

    partials, pieces = {}, {}

    def views_of(which, grads):
        return [_grad_view(g, BIG[w][1], BIG[w][2]) for w, g in zip(which, grads)]

    def chip_partials(which, views, recv):
        for w, gv, r in zip(which, views, recv):
            partials[w] = add_halves(gv, r, c_idx, "add_halves_" + BIG[w][0])
        return scatter_pieces_comm([partials[w] for w in which], which)

    d_act = ffn_down_bwd(df, w_down)
    dha, dhg, d_fcw_a, d_fcw_g, d_fcb_a, d_fcb_g = ffn_act_bwd(hpre, d_act, fconv_w, fconv_b)
    d_fcw = jnp.concatenate([d_fcw_a, d_fcw_g], axis=1)
    d_fcb = jnp.concatenate([d_fcb_a, d_fcb_g], axis=1)
    dx1, d_s3, d_s4, d_gffn = ffn_up_bwd(dha, dhg, w_up, x1, dy, norm_ffn_g, mx[4])
    g_w_down = matmul_tn(act, df, "gw_down", 256, D_MODEL)
    g_w_up = matmul_tn(xn2, dha, "gw_up_a", 512, 1408, total_cols=2 * D_FF)
    g_w_up = matmul_tn(xn2, dhg, "gw_up_g", 512, 1408, prev=g_w_up, col_block=1, total_cols=2 * D_FF)
    v_ffn = views_of([4, 5], [g_w_up, g_w_down]) if dist else None
    *mb, got = merge_bwd(dx1, out, mx[2], p, u, v, w_rnn, w_na, w_out,
                         comm=exchange_halves_comm(v_ffn) if dist else None)
    dout, du, dv, dmr, dmn, dyr, dyn, dg2 = mb
    recv_ffn = got
    g_w_out = matmul_tn(merged, dout, "gw_out", 1024, 512)
    g_w_rnn = matmul_tn(y_rnn, du, "gw_rnn", 1024, 512)
    g_w_na = matmul_tn(y_na, dv, "gw_na", 1024, 512)
    v_mix = views_of([1, 2, 3], [g_w_rnn, g_w_na, g_w_out]) if dist else None
    *lru_grads, got = lru_bwd(p, dyr, conv_w, conv_b, wa, ba, wx, bx, lam,
                              comm=join_comms(chip_partials([4, 5], v_ffn, recv_ffn),
                                              exchange_halves_comm(v_mix)) if dist else None)
    dxr, dgx, d_cw, d_cb, d_wa, d_ba, d_wx, d_bx, d_lam = lru_grads
    if dist:
        pieces[4], pieces[5] = got[:2]
    lru_w_all = {}
    dqr, dqp, dk, dvh, dbt, got = attn_bwd(
        q_rot, q_pl, kk, vv, bt, y_na, dyn, lse,
        comm=join_comms(chip_partials([1, 2, 3], v_mix, got[2:]),
                        all_gather_comm(d_wa.reshape(-1, LRU_BLOCK_W))) if dist else None)
    if dist:
        pieces[1], pieces[2], pieces[3], lru_w_all["lru_wa"] = got
    dq_cols, dk_cols, dv_cols, d_qg, d_kg, got = qkv_bwd(
        dqr, dqp, dk, dvh, p, qg2, kg2, cos, sin, ones,
        comm=all_gather_comm(d_wx.reshape(-1, LRU_BLOCK_W)) if dist else None)
    if dist:
        lru_w_all["lru_wx"] = got[0]
    d_rpb = rpb_grad(dbt)
    dgs = [dxr, dk_cols, dv_cols, dgx, dq_cols, dmr, dmn]
    g_w_in = None
    for g in range(7):
        g_w_in = matmul_tn(xn, dgs[g], "gw_in_%d" % g, 1024, 512, prev=g_w_in, col_block=g, total_cols=IN_COLS)
    if dist:
        v_in = views_of([0], [g_w_in])
        recv_in = run_comm(exchange_halves_comm(v_in), "grad_exchange_w_in")
    grad_x, dsh, dsc, d_gmix, got = in_proj_bwd(dgs, w_in, z, dx1, norm_mix_g, scale,
                                                comm=chip_partials([0], v_in, recv_in) if dist else None)
    if dist:
        pieces[0] = got[0]

    d_modx = jnp.concatenate([dsh[1], dsc[1], dg2, d_s3, d_s4, dg5], axis=1)
    d_modc = jnp.concatenate([dsh[0], dsc[0]], axis=1)
    return dict(loss_sq=loss_sq, grad_x=grad_x, d_modx=d_modx, d_modc=d_modc, norm_mix_g=d_gmix, norm_ffn_g=d_gffn,
                w_in=g_w_in, lru_conv_w=d_cw, lru_conv_b=d_cb, lru_wa=d_wa, lru_ba=d_ba, lru_wx=d_wx, lru_bx=d_bx,
                lru_lambda=d_lam, q_norm_g=d_qg, k_norm_g=d_kg, na_rpb=d_rpb, w_rnn_out=g_w_rnn, w_na_out=g_w_na,
                w_out=g_w_out, w_up=g_w_up, ffn_conv_w=d_fcw, ffn_conv_b=d_fcb, w_down=g_w_down,
                partials=partials, pieces=pieces, lru_w_all=lru_w_all)


def _mesh_pos():
    return lax.axis_index("x"), lax.axis_index("y"), lax.axis_index("c")


def _other_chips(x, y):
    return [(1 - x, y), (x, 1 - y), (1 - x, 1 - y)]


def all_gather8(xs, name, with_sum=False):
    m, n = xs.shape
    assert m % 8 == 0

    def body(x_ref, out_ref, *rest):
        if with_sum:
            sum_ref, send_sems, recv_sems, local_sem = rest
        else:
            send_sems, recv_sems, local_sem = rest
        x, y, c = _mesh_pos()
        me, sibling = (x, y, c), (x, y, 1 - c)
        chips = _other_chips(x, y)

        def rows(px, py, pc):
            return out_ref.at[pl.ds((4 * px + 2 * py + pc) * m, m), :]

        def copy(k, block, to, src=None):
            return pltpu.make_async_remote_copy(
                src_ref=rows(*block) if src is None else src, dst_ref=rows(*block),
                send_sem=send_sems.at[k], recv_sem=recv_sems.at[k], device_id=to, device_id_type=MESH_T)

        mine = pltpu.make_async_copy(x_ref, rows(*me), local_sem)
        mine.start()
        first = [copy(0, me, sibling, src=x_ref)]
        first += [copy(1 + j, me, (*chip, c), src=x_ref) for j, chip in enumerate(chips)]
        for cp in first:
            cp.start()
        passed = [copy(4 + j, (*chip, c), sibling) for j, chip in enumerate(chips)]
        for j, chip in enumerate(chips):
            copy(1 + j, (*chip, c), me).wait_recv()
            passed[j].start()
        copy(0, sibling, me).wait_recv()
        for j, chip in enumerate(chips):
            copy(4 + j, (*chip, 1 - c), me).wait_recv()
        for cp in first + passed:
            cp.wait_send()
        mine.wait()
        if with_sum:
            acc = out_ref[0:m, :]
            for k in range(1, N_DEV):
                acc = acc + out_ref[k * m:(k + 1) * m, :]
            sum_ref[...] = acc

    vm = pl.BlockSpec(memory_space=pltpu.VMEM)
    out_shape = [jax.ShapeDtypeStruct((N_DEV * m, n), F32)]
    if with_sum:
        out_shape.append(jax.ShapeDtypeStruct((m, n), F32))
    res = pl.pallas_call(
        body, name=name, in_specs=[vm], out_specs=[vm] * len(out_shape), out_shape=out_shape,
        scratch_shapes=[pltpu.SemaphoreType.DMA((7,)), pltpu.SemaphoreType.DMA((7,)), pltpu.SemaphoreType.DMA],
        compiler_params=pltpu.CompilerParams(vmem_limit_bytes=VMEM_LIMIT_V7X),
    )(xs)
    return res if with_sum else res[0]


BIG = (("w_in", (D_MODEL, IN_COLS), 1), ("w_rnn_out", (D_MODEL, D_MODEL), 0), ("w_na_out", (D_MODEL, D_MODEL), 0),
       ("w_out", (D_MODEL, D_MODEL), 0), ("w_up", (D_MODEL, 2 * D_FF), 1), ("w_down", (D_FF, D_MODEL), 0))


def _shard_shape(full, axis):
    r, c = full
    return (r // N_SHARD, c) if axis == 0 else (r, c // N_SHARD)


def _slot(ref, full, axis, s, h):
    r, c = full
    if axis == 0:
        rs = r // N_SHARD
        return ref.at[pl.ds(s * rs + h * (rs // 2), rs // 2), :]
    cs = c // N_SHARD
    return ref.at[pl.ds(h * (r // 2), r // 2), pl.ds(s * cs, cs)]


def cast_into_full(x, full, axis, idx, name):
    r, c = x.shape
    tr = next(t for t in (512, 352, 256, 128) if r % t == 0)
    nb = r // tr

    def body(idx_ref, x_ref, o_ref):
        o_ref[...] = x_ref[...].astype(BF16)

    if axis == 0:
        out_spec = pl.BlockSpec((tr, c), lambda i, idx_ref: (idx_ref[0] * nb + i, 0))
    else:
        out_spec = pl.BlockSpec((tr, c), lambda i, idx_ref: (i, idx_ref[0]))
    return pl.pallas_call(
        body, name=name,
        grid_spec=pltpu.PrefetchScalarGridSpec(
            num_scalar_prefetch=1, grid=(nb,), in_specs=[pl.BlockSpec((tr, c), lambda i, idx_ref: (i, 0))],
            out_specs=out_spec),
        out_shape=jax.ShapeDtypeStruct(full, BF16),
        compiler_params=_params("parallel"),
    )(idx, x)


def run_comm(comm, name):
    k_in, k_out = len(comm.inputs), len(comm.out_shapes)

    def body(*refs):
        start, mid, end = comm.emit(refs[:k_in], refs[k_in:k_in + k_out], refs[k_in + k_out:])
        start()
        mid()
        end()

    hbm = pl.BlockSpec(memory_space=pl.ANY)
    return pl.pallas_call(
        body, name=name, in_specs=[hbm] * k_in, out_specs=[hbm] * k_out, out_shape=list(comm.out_shapes),
        input_output_aliases=dict(comm.aliases), scratch_shapes=list(comm.scratch),
        compiler_params=pltpu.CompilerParams(vmem_limit_bytes=VMEM_LIMIT_V7X),
    )(*comm.inputs)


def gather_weights_comm(fulls, which):
    nw = len(which)
    specs = [BIG[w] for w in which]

    def emit(_, outs, sems):
        send1, recv1, send2, recv2 = sems
        x, y, c = _mesh_pos()
        sibling = (x, y, 1 - c)
        chips = _other_chips(x, y)
        s_me = 2 * x + y
        shards = [2 * chip[0] + chip[1] for chip in chips]

        def ici(w, j, shard):
            _, full, axis = specs[w]
            dst = _slot(outs[w], full, axis, shard, c)
            return pltpu.make_async_remote_copy(
                src_ref=dst, dst_ref=dst, send_sem=send1.at[3 * w + j],
                recv_sem=recv1.at[3 * w + j], device_id=(*chips[j], c), device_id_type=MESH_T)

        def d2d(w, j, shard, half):
            _, full, axis = specs[w]
            dst = _slot(outs[w], full, axis, shard, half)
            return pltpu.make_async_remote_copy(
                src_ref=dst, dst_ref=dst, send_sem=send2.at[3 * w + j], recv_sem=recv2.at[3 * w + j],
                device_id=sibling, device_id_type=MESH_T)

        pairs = [(w, j) for w in range(nw) for j in range(3)]

        def start():
            for w, j in pairs:
                ici(w, j, s_me).start()

        def mid():
            for w, j in pairs:
                ici(w, j, shards[j]).wait_recv()
                d2d(w, j, shards[j], c).start()

        def end():
            for w, j in pairs:
                d2d(w, j, shards[j], 1 - c).wait_recv()
            for w, j in pairs:
                ici(w, j, s_me).wait_send()
                d2d(w, j, shards[j], c).wait_send()

        return start, mid, end

    return Comm(list(fulls), [jax.ShapeDtypeStruct(full, BF16) for _, full, _ in specs], {i: i for i in range(nw)},
                [pltpu.SemaphoreType.DMA((3 * nw,))] * 4, emit)


def join_comms(a, b):
    ai, ao, asc = len(a.inputs), len(a.out_shapes), len(a.scratch)

    def emit(ins, outs, sems):
        fa = a.emit(ins[:ai], outs[:ao], sems[:asc])
        fb = b.emit(ins[ai:], outs[ao:], sems[asc:])

        def both(k):
            def run():
                fa[k]()
                fb[k]()
            return run

        return both(0), both(1), both(2)

    aliases = dict(a.aliases)
    aliases.update({ai + i: ao + o for i, o in b.aliases.items()})
    return Comm(a.inputs + b.inputs, a.out_shapes + b.out_shapes, aliases, a.scratch + b.scratch, emit)


def all_gather_comm(x):
    def emit(srcs, outs, sems):
        send_sems, recv_sems, local_sem = sems
        x_ref, out_ref = srcs[0], outs[0]
        x, y, c = _mesh_pos()
        me, sibling = (x, y, c), (x, y, 1 - c)
        chips = _other_chips(x, y)

        def blk(px, py, pc):
            return out_ref.at[4 * px + 2 * py + pc]

        def copy(k, block, to, src=None):
            return pltpu.make_async_remote_copy(
                src_ref=blk(*block) if src is None else src, dst_ref=blk(*block),
                send_sem=send_sems.at[k], recv_sem=recv_sems.at[k], device_id=to, device_id_type=MESH_T)

        def mine():
            return pltpu.make_async_copy(x_ref, blk(*me), local_sem)

        def start():
            mine().start()
            copy(0, me, sibling, src=x_ref).start()
            for j, chip in enumerate(chips):
                copy(1 + j, me, (*chip, c), src=x_ref).start()

        def mid():
            for j, chip in enumerate(chips):
                copy(1 + j, (*chip, c), me).wait_recv()
                copy(4 + j, (*chip, c), sibling).start()

        def end():
            copy(0, sibling, me).wait_recv()
            for j, chip in enumerate(chips):
                copy(4 + j, (*chip, 1 - c), me).wait_recv()
            copy(0, me, sibling, src=x_ref).wait_send()
            for j, chip in enumerate(chips):
                copy(1 + j, me, (*chip, c), src=x_ref).wait_send()
                copy(4 + j, (*chip, c), sibling).wait_send()
            mine().wait()

        return start, mid, end

    return Comm([x], [jax.ShapeDtypeStruct((N_DEV,) + x.shape, F32)], {},
                [pltpu.SemaphoreType.DMA((7,)), pltpu.SemaphoreType.DMA((7,)), pltpu.SemaphoreType.DMA], emit)


def sum_blocks(g, name):
    _, r, c = g.shape
    tr = 256 if r % 256 == 0 else r

    def body(g_ref, o_ref):
        acc = g_ref[0]
        for k in range(1, N_DEV):
            acc = acc + g_ref[k]
        o_ref[...] = acc

    return pl.pallas_call(
        body, name=name, grid=(r // tr,),
        in_specs=[pl.BlockSpec((N_DEV, tr, c), lambda i: (0, i, 0))],
        out_specs=pl.BlockSpec((tr, c), lambda i: (i, 0)),
        out_shape=jax.ShapeDtypeStruct((r, c), F32),
        compiler_params=_params("parallel"),
    )(g)


def _grad_view(g, full, axis):
    r, c = full
    if axis == 0:
        return g.reshape(N_SHARD, 2, r // N_SHARD // 2, c)
    return g.reshape(1, 2, r // 2, c)


def exchange_halves_comm(gviews):
    nw = len(gviews)

    def emit(srcs, outs, sems):
        send_sems, recv_sems = sems
        x, y, c = _mesh_pos()

        def copies():
            return [pltpu.make_async_remote_copy(
                src_ref=srcs[w].at[:, pl.ds(1 - c, 1)], dst_ref=outs[w], send_sem=send_sems.at[w],
                recv_sem=recv_sems.at[w], device_id=(x, y, 1 - c), device_id_type=MESH_T) for w in range(nw)]

        def start():
            for cp in copies():
                cp.start()

        def end():
            for cp in copies():
                cp.wait()

        return start, lambda: None, end

    return Comm(list(gviews), [jax.ShapeDtypeStruct((g.shape[0], 1) + g.shape[2:], BF16) for g in gviews], {},
                [pltpu.SemaphoreType.DMA((nw,)), pltpu.SemaphoreType.DMA((nw,))], emit)


def _row_tile(rh):
    return 128 if rh % 128 == 0 else rh


def add_halves(gview, recv, c_idx, name):
    a, _, rh, cc = gview.shape
    tr = _row_tile(rh)

    def body(c_ref, g_ref, r_ref, o_ref):
        o_ref[0] = (g_ref[0, 0].astype(F32) + r_ref[0, 0].astype(F32)).astype(BF16)

    return pl.pallas_call(
        body, name=name,
        grid_spec=pltpu.PrefetchScalarGridSpec(
            num_scalar_prefetch=1, grid=(a, rh // tr),
            in_specs=[pl.BlockSpec((1, 1, tr, cc), lambda s, i, c_ref: (s, c_ref[0], i, 0)),
                      pl.BlockSpec((1, 1, tr, cc), lambda s, i, c_ref: (s, 0, i, 0))],
            out_specs=pl.BlockSpec((1, tr, cc), lambda s, i, c_ref: (s, i, 0))),
        out_shape=jax.ShapeDtypeStruct((a, rh, cc), BF16),
        compiler_params=_params("parallel", "parallel"),
    )(c_idx, gview, recv)


def _piece_shape(full, axis):
    rs, cs = _shard_shape(full, axis)
    return (rs // 2, cs)


def scatter_pieces_comm(partials, which):
    nw = len(which)
    specs = [BIG[w] for w in which]

    def emit(srcs, outs, sems):
        send_sems, recv_sems = sems
        x, y, c = _mesh_pos()
        chips = _other_chips(x, y)

        def copies():
            cps = []
            for w, (_, full, axis) in enumerate(specs):
                cs = full[1] // N_SHARD
                for j, chip in enumerate(chips):
                    s_j = 2 * chip[0] + chip[1]
                    src = srcs[w].at[s_j] if axis == 0 else srcs[w].at[0, :, pl.ds(s_j * cs, cs)]
                    cps.append(pltpu.make_async_remote_copy(
                        src_ref=src, dst_ref=outs[w].at[j], send_sem=send_sems.at[3 * w + j],
                        recv_sem=recv_sems.at[3 * w + j], device_id=(*chip, c), device_id_type=MESH_T))
            return cps

        def start():
            for cp in copies():
                cp.start()

        def mid():
            pass

        def end():
            for cp in copies():
                cp.wait()

        return start, mid, end

    return Comm(list(partials), [jax.ShapeDtypeStruct((3,) + _piece_shape(full, axis), BF16) for _, full, axis in specs],
                {}, [pltpu.SemaphoreType.DMA((3 * nw,)), pltpu.SemaphoreType.DMA((3 * nw,))], emit)


def add_pieces(partial, recv, idx, axis, name):
    _, rh, cs = recv.shape
    tr = _row_tile(rh)

    def body(idx_ref, p_ref, r_ref, o_ref):
        o_ref[0] = ((p_ref[0].astype(F32) + r_ref[0].astype(F32)) + r_ref[1].astype(F32)) + r_ref[2].astype(F32)

    if axis == 0:
        pspec = pl.BlockSpec((1, tr, cs), lambda i, idx_ref: (idx_ref[0], i, 0))
    else:
        pspec = pl.BlockSpec((1, tr, cs), lambda i, idx_ref: (0, i, idx_ref[0]))
    return pl.pallas_call(
        body, name=name,
        grid_spec=pltpu.PrefetchScalarGridSpec(
            num_scalar_prefetch=1, grid=(rh // tr,),
            in_specs=[pspec, pl.BlockSpec((3, tr, cs), lambda i, idx_ref: (0, i, 0))],
            out_specs=pl.BlockSpec((1, tr, cs), lambda i, idx_ref: (idx_ref[1], i, 0))),
        out_shape=jax.ShapeDtypeStruct((2, rh, cs), F32),
        compiler_params=_params("parallel"),
    )(idx, partial, recv)


def join_halves_comm(halves):
    nw = len(halves)

    def emit(_, outs, sems):
        send_sems, recv_sems = sems
        x, y, c = _mesh_pos()

        def copy(w, half):
            return pltpu.make_async_remote_copy(
                src_ref=outs[w].at[half], dst_ref=outs[w].at[half], send_sem=send_sems.at[w], recv_sem=recv_sems.at[w],
                device_id=(x, y, 1 - c), device_id_type=MESH_T)

        def start():
            for w in range(nw):
                copy(w, c).start()

        def end():
            for w in range(nw):
                copy(w, c).wait_send()
                copy(w, 1 - c).wait_recv()

        return start, lambda: None, end

    return Comm(list(halves), [jax.ShapeDtypeStruct(h.shape, F32) for h in halves], {i: i for i in range(nw)},
                [pltpu.SemaphoreType.DMA((nw,))] * 2, emit)


MOD_COLS = N_MOD * D_MODEL // N_SHARD
MOD_TILE = 512


def mod_fwd(c16, w_mod):
    def body(c_ref, w_ref, s_ref, o_ref):
        cv = c_ref[...]
        s = cv * _sigmoid(cv)
        s_ref[...] = s
        o_ref[...] = jnp.dot(s.astype(BF16), w_ref[...].astype(BF16), preferred_element_type=F32)

    return pl.pallas_call(
        body, name="mod_fwd", grid=(MOD_COLS // MOD_TILE,),
        in_specs=[_full((16, D_MODEL)), pl.BlockSpec((D_MODEL, MOD_TILE), lambda j: (0, j))],
        out_specs=[_full((16, D_MODEL)), pl.BlockSpec((16, MOD_TILE), lambda j: (0, j))],
        out_shape=[jax.ShapeDtypeStruct((16, D_MODEL), F32), jax.ShapeDtypeStruct((16, MOD_COLS), F32)],
        compiler_params=_params("arbitrary"),
    )(c16, w_mod)


def mod_bwd(s16, dm16, w_mod):
    hi = lax.Precision.HIGHEST

    def body(s_ref, d_ref, w_ref, gw_ref, ds_ref):
        j = pl.program_id(0)
        dm = d_ref[...]
        gw_ref[...] = lax.dot_general(s_ref[...], dm, (((0,), (0,)), ((), ())), preferred_element_type=F32, precision=hi)
        part = lax.dot_general(dm, w_ref[...], (((1,), (1,)), ((), ())), preferred_element_type=F32, precision=hi)

        @pl.when(j == 0)
        def _():
            ds_ref[...] = part

        @pl.when(j > 0)
        def _():
            ds_ref[...] = ds_ref[...] + part

    return pl.pallas_call(
        body, name="mod_bwd", grid=(MOD_COLS // MOD_TILE,),
        in_specs=[_full((16, D_MODEL)), pl.BlockSpec((16, MOD_TILE), lambda j: (0, j)),
                  pl.BlockSpec((D_MODEL, MOD_TILE), lambda j: (0, j))],
        out_specs=[pl.BlockSpec((D_MODEL, MOD_TILE), lambda j: (0, j)), _full((16, D_MODEL))],
        out_shape=[jax.ShapeDtypeStruct((D_MODEL, MOD_COLS), F32), jax.ShapeDtypeStruct((16, D_MODEL), F32)],
        compiler_params=_params("arbitrary"),
    )(s16, dm16, w_mod)


def cctx_grad(parts, c_ctx):
    def body(p_ref, c_ref, o_ref):
        ds = p_ref[0:1, :]
        for s in range(1, N_SHARD):
            ds = ds + p_ref[16 * s:16 * s + 1, :]
        cv = c_ref[...]
        sg = _sigmoid(cv)
        o_ref[...] = ds * (sg * (1.0 + cv * (1.0 - sg)))

    return pl.pallas_call(
        body, name="cctx_grad", in_specs=[_full((N_DEV * 8, D_MODEL)), _full((1, D_MODEL))],
        out_specs=_full((1, D_MODEL)), out_shape=jax.ShapeDtypeStruct((1, D_MODEL), F32),
    )(parts, c_ctx)


def add_rows(a, b, name):
    def body(a_ref, b_ref, o_ref):
        o_ref[...] = a_ref[...] + b_ref[...]

    return pl.pallas_call(body, name=name, in_specs=[_full(a.shape), _full(b.shape)], out_specs=_full(a.shape),
                          out_shape=jax.ShapeDtypeStruct(a.shape, F32))(a, b)


def _adamw_update(w_ref, g_ref, m_ref, v_ref, d_ref, nm_ref, nv_ref):
    g_ = g_ref[...]
    m_ = ADAM_B1 * m_ref[...] + (1.0 - ADAM_B1) * g_
    v_ = ADAM_B2 * v_ref[...] + (1.0 - ADAM_B2) * (g_ * g_)
    m_hat = m_ / (1.0 - ADAM_B1 ** ADAM_STEP)
    v_hat = v_ / (1.0 - ADAM_B2 ** ADAM_STEP)
    d_ref[...] = -ADAM_LR * (m_hat / (jnp.sqrt(v_hat) + ADAM_EPS) + ADAM_WD * w_ref[...])
    nm_ref[...] = m_
    nv_ref[...] = v_


def adamw_many(ws, gs, ms, vs):
    n = len(ws)

    def body(*refs):
        for i in range(n):
            _adamw_update(*[refs[k * n + i] for k in range(7)])

    shapes = [jax.ShapeDtypeStruct(w.shape, F32) for w in ws]
    return pl.pallas_call(body, name="adamw_small", out_shape=shapes * 3,
                          compiler_params=pltpu.CompilerParams(vmem_limit_bytes=VMEM_LIMIT_V7X))(*ws, *gs, *ms, *vs)


def adamw(w, g, m, v, name):
    r, c = w.shape
    tr = 128 if (r % 128 == 0 and r > 128) else r

    def body(w_ref, g_ref, m_ref, v_ref, d_ref, nm_ref, nv_ref):
        _adamw_update(w_ref, g_ref, m_ref, v_ref, d_ref, nm_ref, nv_ref)

    spec = pl.BlockSpec((tr, c), lambda i: (i, 0))
    shp = jax.ShapeDtypeStruct((r, c), F32)
    return pl.pallas_call(
        body, name=name, grid=(r // tr,), in_specs=[spec] * 4, out_specs=[spec] * 3, out_shape=[shp] * 3,
        compiler_params=_params("parallel"),
    )(w, g, m, v)


LANES = 1024


def _pack(arrs):
    rows, spans, at = [], [], 0
    for a in arrs:
        n = int(np.prod(a.shape))
        nr = 8 * -(-n // (8 * LANES))
        flat = a.reshape(-1)
        if nr * LANES != n:
            flat = jnp.concatenate([flat, jnp.zeros((nr * LANES - n,), F32)])
        rows.append(flat.reshape(nr, LANES))
        spans.append((at, nr, n, a.shape))
        at += nr
    return jnp.concatenate(rows, axis=0), spans


def _unpack(buf, spans):
    out = []
    for at, nr, n, shape in spans:
        out.append(buf[at:at + nr].reshape(-1)[:n].reshape(shape))
    return out


SMALL_SHARD = ("lru_conv_w", "lru_ba", "lru_bx", "lru_lambda", "ffn_conv_w")


def kernel(x, c, ctx, c_ctx, w_mod, b_mod, norm_mix_g, norm_ffn_g, w_in, lru_conv_w, lru_conv_b, lru_wa, lru_ba, lru_wx, lru_bx, lru_lambda, q_norm_g, k_norm_g, na_rpb, w_rnn_out, w_na_out, w_out, w_up, ffn_conv_w, ffn_conv_b, w_down, loss_target, m_c_ctx, m_w_mod, m_b_mod, m_norm_mix_g, m_norm_ffn_g, m_w_in, m_lru_conv_w, m_lru_conv_b, m_lru_wa, m_lru_ba, m_lru_wx, m_lru_bx, m_lru_lambda, m_q_norm_g, m_k_norm_g, m_na_rpb, m_w_rnn_out, m_w_na_out, m_w_out, m_w_up, m_ffn_conv_w, m_ffn_conv_b, m_w_down, v_c_ctx, v_w_mod, v_b_mod, v_norm_mix_g, v_norm_ffn_g, v_w_in, v_lru_conv_w, v_lru_conv_b, v_lru_wa, v_lru_ba, v_lru_wx, v_lru_bx, v_lru_lambda, v_q_norm_g, v_k_norm_g, v_na_rpb, v_w_rnn_out, v_w_na_out, v_w_out, v_w_up, v_ffn_conv_w, v_ffn_conv_b, v_w_down):
    weights = dict(c_ctx=c_ctx, w_mod=w_mod, b_mod=b_mod, norm_mix_g=norm_mix_g, norm_ffn_g=norm_ffn_g, w_in=w_in,
                   lru_conv_w=lru_conv_w, lru_conv_b=lru_conv_b, lru_wa=lru_wa, lru_ba=lru_ba, lru_wx=lru_wx,
                   lru_bx=lru_bx, lru_lambda=lru_lambda, q_norm_g=q_norm_g, k_norm_g=k_norm_g, na_rpb=na_rpb,
                   w_rnn_out=w_rnn_out, w_na_out=w_na_out, w_out=w_out, w_up=w_up, ffn_conv_w=ffn_conv_w,
                   ffn_conv_b=ffn_conv_b, w_down=w_down)
    mom1 = dict(c_ctx=m_c_ctx, w_mod=m_w_mod, b_mod=m_b_mod, norm_mix_g=m_norm_mix_g, norm_ffn_g=m_norm_ffn_g,
                w_in=m_w_in, lru_conv_w=m_lru_conv_w, lru_conv_b=m_lru_conv_b, lru_wa=m_lru_wa, lru_ba=m_lru_ba,
                lru_wx=m_lru_wx, lru_bx=m_lru_bx, lru_lambda=m_lru_lambda, q_norm_g=m_q_norm_g, k_norm_g=m_k_norm_g,
                na_rpb=m_na_rpb, w_rnn_out=m_w_rnn_out, w_na_out=m_w_na_out, w_out=m_w_out, w_up=m_w_up,
                ffn_conv_w=m_ffn_conv_w, ffn_conv_b=m_ffn_conv_b, w_down=m_w_down)
    mom2 = dict(c_ctx=v_c_ctx, w_mod=v_w_mod, b_mod=v_b_mod, norm_mix_g=v_norm_mix_g, norm_ffn_g=v_norm_ffn_g,
                w_in=v_w_in, lru_conv_w=v_lru_conv_w, lru_conv_b=v_lru_conv_b, lru_wa=v_lru_wa, lru_ba=v_lru_ba,
                lru_wx=v_lru_wx, lru_bx=v_lru_bx, lru_lambda=v_lru_lambda, q_norm_g=v_q_norm_g, k_norm_g=v_k_norm_g,
                na_rpb=v_na_rpb, w_rnn_out=v_w_rnn_out, w_na_out=v_w_na_out, w_out=v_w_out, w_up=v_w_up,
                ffn_conv_w=v_ffn_conv_w, ffn_conv_b=v_ffn_conv_b, w_down=v_w_down)
    order = list(weights)
    d = D_MODEL
    mx_, my_, mc_ = _mesh_pos()
    shard = 2 * mx_ + my_
    dev = 2 * shard + mc_

    idx = jnp.stack([shard, mc_]).astype(jnp.int32)
    wsh = {name: cast_into_full(weights[name][0], full, axis, idx, "cast_" + name) for name, full, axis in BIG}
    local_small, small_spans = _pack([c] + [weights[k][0] for k in SMALL_SHARD])
    bt, (w_in_full, gath) = bias_table(na_rpb[0], comm=join_comms(gather_weights_comm([wsh["w_in"]], [0]),
                                                                  all_gather_comm(local_small)))
    per_dev = [_unpack(gath[k], small_spans) for k in range(N_DEV)]
    c_all = jnp.concatenate([per_dev[k][0] for k in range(N_DEV)], axis=0)
    full_small = {name: jnp.concatenate([per_dev[2 * s][1 + i] for s in range(N_SHARD)], axis=-1)
                  for i, name in enumerate(SMALL_SHARD)}
    c16 = jnp.concatenate([c_all, c_ctx.reshape(1, d), jnp.zeros((7, d), F32)], axis=0)
    s16, mod_part = mod_fwd(c16, w_mod[0])
    mod_all = all_gather8(mod_part, "gather_mod").reshape(N_DEV, 16, MOD_COLS)
    mod = jnp.concatenate([mod_all[2 * s] for s in range(N_SHARD)], axis=1) + b_mod
    modx = lax.dynamic_slice(mod, (dev, 0), (1, N_MOD * d))
    modc = mod[8:9]

    z = jnp.concatenate([ctx[0], x[0]], axis=0)
    res = local_step(z, loss_target[0], modx, modc, norm_mix_g, norm_ffn_g, w_in_full, full_small["lru_conv_w"],
                     lru_conv_b, lru_wa[0], full_small["lru_ba"], lru_wx[0], full_small["lru_bx"],
                     full_small["lru_lambda"], q_norm_g, k_norm_g, na_rpb[0], wsh["w_rnn_out"], wsh["w_na_out"],
                     wsh["w_out"], wsh["w_up"], full_small["ffn_conv_w"], ffn_conv_b, wsh["w_down"], idx=idx, bt=bt)

    halves = [add_pieces(res["partials"][i], res["pieces"][i], idx, BIG[i][2], "add_pieces_" + BIG[i][0])
              for i in range(len(BIG))]
    lru_tot = {k: sum_blocks(res["lru_w_all"][k], "sum_" + k).reshape(weights[k].shape[1:])
               for k in ("lru_wa", "lru_wx")}
    small_names = ["norm_mix_g", "norm_ffn_g", "lru_conv_w", "lru_conv_b", "lru_ba", "lru_bx",
                   "lru_lambda", "q_norm_g", "k_norm_g", "na_rpb", "ffn_conv_w", "ffn_conv_b"]
    local_g, g_spans = _pack([res["loss_sq"][0:1, 0:1], res["d_modx"], res["d_modc"]] + [res[k] for k in small_names])
    n_rows = local_g.shape[0]
    *joined, g_all = run_comm(join_comms(join_halves_comm(halves), all_gather_comm(local_g)), "tail_exchange")
    grads = {name: joined[i].reshape(_shard_shape(full, axis)) for i, (name, full, axis) in enumerate(BIG)}
    grads.update(lru_tot)
    g_tot = sum_blocks(g_all, "sum_small")
    tot = _unpack(g_tot, g_spans)
    loss = (0.5 / d) * tot[0][0, 0]
    small_tot = dict(zip(small_names, tot[3:]))
    at_x = g_spans[1][0]
    dmx_rows = g_all.reshape(N_DEV, n_rows, LANES)[:, at_x:at_x + N_MOD, :].reshape(N_DEV, N_MOD * d)
    dmc_row = jnp.concatenate([tot[2], jnp.zeros((1, 4 * d), F32)], axis=1)
    dm16 = jnp.concatenate([dmx_rows, dmc_row, jnp.zeros((7, N_MOD * d), F32)], axis=0)
    grads["b_mod"] = add_rows(tot[1], dmc_row, "b_mod_grad")
    g_w_mod, ds16 = mod_bwd(s16, lax.dynamic_slice(dm16, (0, shard * MOD_COLS), (16, MOD_COLS)), w_mod[0])
    grads["w_mod"] = g_w_mod
    ds_parts = all_gather8(ds16[8:16], "gather_dsctx")
    grads["c_ctx"] = cctx_grad(ds_parts, c_ctx.reshape(1, d))
    for k in small_names:
        g = small_tot[k]
        if k in SMALL_SHARD:
            w_sh = weights[k].shape[-1]
            g = lax.dynamic_slice_in_dim(g, shard * w_sh, w_sh, axis=g.ndim - 1)
        grads[k] = g

    delta, new_m, new_v = {}, {}, {}
    for name, _, _ in BIG + (("w_mod", None, None),):
        delta[name], new_m[name], new_v[name] = adamw(weights[name][0], grads[name], mom1[name][0], mom2[name][0],
                                                      "adamw_" + name)
    rest = [k for k in order if k not in delta]
    views = {k: (grads[k].shape if grads[k].ndim <= 3 else (-1, grads[k].shape[-1])) for k in rest}
    small = adamw_many(*[[t[k].reshape(views[k]) for k in rest] for t in (weights, grads, mom1, mom2)])
    n_rest = len(rest)
    for i, k in enumerate(rest):
        delta[k], new_m[k], new_v[k] = small[i], small[n_rest + i], small[2 * n_rest + i]

    shaped = lambda t: [t[k].reshape(weights[k].shape) for k in order]
    return (loss, res["grad_x"][None], *shaped(grads), *shaped(delta), *shaped(new_m), *shaped(new_v))
```

```python
import numpy as np
import jax
import jax.numpy as jnp
from jax import lax
from jax.experimental import pallas as pl
from jax.experimental.pallas import tpu as pltpu

F32 = jnp.float32
BF16 = jnp.bfloat16

D_MODEL = 1024
SEQ = 2048
CTX_LEN = 256
ZLEN = SEQ + CTX_LEN
GRID_W = 64
GRID_ROWS = SEQ // GRID_W
LRU_BLOCK_W = 128
LRU_BLOCKS = 8
LRU_C = 8.0
NA_HEADS = 16
HEAD_DIM = 64
NA_ROWS = 8
NA_COLS = 16
ROPE_BASE = 10000.0
D_FF = 2816
N_MOD = 6
IN_COLS = 7 * D_MODEL
EPS = 1e-6
NEG_INF = -1e30
N_DEV = 8
N_SHARD = 4

ADAM_LR = 0.001
ADAM_B1 = 0.9
ADAM_B2 = 0.999
ADAM_EPS = 1e-08
ADAM_WD = 0.01
ADAM_STEP = 10

ROW_TILE = 256
Q_ROWS = 4
Q_TILE = Q_ROWS * GRID_W
KEY_ROWS = 12
KEY_TILE = KEY_ROWS * GRID_W
BT_PAD = 4
BT_LEN = 24
VMEM_LIMIT_V7X = 56 * 1024 * 1024

MESH_T = pl.DeviceIdType.MESH


def _params(*sem):
    return pltpu.CompilerParams(dimension_semantics=sem if sem else None, vmem_limit_bytes=VMEM_LIMIT_V7X)


def _full(shape):
    nd = len(shape)
    return pl.BlockSpec(shape, lambda *_: (0,) * nd)


class Comm:
    def __init__(self, inputs, out_shapes, aliases, scratch, emit):
        self.inputs, self.out_shapes, self.aliases, self.scratch, self.emit = inputs, out_shapes, aliases, scratch, emit


def _call(body, *, name, grid, in_specs, out_specs, out_shape, args, scratch_shapes=(), sem=(), comm=None):
    n_in, n_out, n_sc = len(in_specs), len(out_specs), len(scratch_shapes)
    if comm is None:
        res = pl.pallas_call(body, name=name, grid=grid, in_specs=list(in_specs), out_specs=list(out_specs),
                             out_shape=list(out_shape), scratch_shapes=list(scratch_shapes),
                             compiler_params=_params(*sem))(*args)
        return list(res), []
    k_in, k_out = len(comm.inputs), len(comm.out_shapes)
    steps = int(np.prod(grid))

    def hosted(*refs):
        ins, cins = refs[:n_in], refs[n_in:n_in + k_in]
        at = n_in + k_in
        outs, couts = refs[at:at + n_out], refs[at + n_out:at + n_out + k_out]
        at += n_out + k_out
        scr, cscr = refs[at:at + n_sc], refs[at + n_sc:]
        start, mid, end = comm.emit(cins, couts, cscr)
        lin = pl.program_id(0)
        for ax in range(1, len(grid)):
            lin = lin * grid[ax] + pl.program_id(ax)
        pl.when(lin == 0)(start)
        body(*ins, *outs, *scr)
        pl.when(lin == steps - 1 - steps // 7)(mid)
        pl.when(lin == steps - 1)(end)

    hbm = pl.BlockSpec(memory_space=pl.ANY)
    res = pl.pallas_call(
        hosted, name=name, grid=grid, in_specs=list(in_specs) + [hbm] * k_in, out_specs=list(out_specs) + [hbm] * k_out,
        out_shape=list(out_shape) + list(comm.out_shapes), scratch_shapes=list(scratch_shapes) + list(comm.scratch),
        input_output_aliases={n_in + i: n_out + o for i, o in comm.aliases.items()},
        compiler_params=_params(*(("arbitrary",) * len(grid))))(*args, *comm.inputs)
    return list(res[:n_out]), list(res[n_out:])


def _sigmoid(x):
    return 0.5 * jnp.tanh(0.5 * x) + 0.5


def _gelu_parts(x):
    c0 = 0.7978845608028654
    inner = c0 * (x + 0.044715 * x * x * x)
    t = jnp.tanh(inner)
    g = 0.5 * x * (1.0 + t)
    dg = 0.5 * (1.0 + t) + 0.5 * x * (1.0 - t * t) * c0 * (1.0 + 3.0 * 0.044715 * x * x)
    return g, dg


def _dot_nt(a, b):
    return lax.dot_general(a, b, (((1,), (1,)), ((), ())), preferred_element_type=F32)


def _dot_tn(a, b):
    return lax.dot_general(a, b, (((0,), (0,)), ((), ())), preferred_element_type=F32)


def norm_mod(xin, gain, shift, scale, name):
    r, d = xin.shape
    s_mod = shift.shape[0]
    assert r % ROW_TILE == 0

    def body(x_ref, g_ref, sh_ref, sc_ref, xn_ref):
        x = x_ref[...]
        nrm = x * lax.rsqrt(jnp.mean(x * x, axis=-1, keepdims=True) + EPS)
        xn_ref[...] = ((nrm * g_ref[...]) * (1.0 + sc_ref[0]) + sh_ref[0]).astype(BF16)

    mod_spec = pl.BlockSpec((1, 1, d), lambda i: (jnp.minimum(i, s_mod - 1), 0, 0))
    return pl.pallas_call(
        body, name=name, grid=(r // ROW_TILE,),
        in_specs=[pl.BlockSpec((ROW_TILE, d), lambda i: (i, 0)), _full((1, d)), mod_spec, mod_spec],
        out_specs=pl.BlockSpec((ROW_TILE, d), lambda i: (i, 0)),
        out_shape=jax.ShapeDtypeStruct((r, d), BF16),
        compiler_params=_params("parallel"),
    )(xin, gain, shift, scale)


def matmul_wide(a, b, name, tm, tn, comm=None):
    m, k = a.shape
    n = b.shape[1]
    assert m % tm == 0 and n % tn == 0

    def body(a_ref, b_ref, o_ref):
        o_ref[...] = jnp.dot(a_ref[...], b_ref[...], preferred_element_type=F32)

    res, extra = _call(
        body, name=name, grid=(n // tn, m // tm),
        in_specs=[pl.BlockSpec((tm, k), lambda j, i: (i, 0)), pl.BlockSpec((k, tn), lambda j, i: (0, j))],
        out_specs=[pl.BlockSpec((tm, tn), lambda j, i: (i, j))],
        out_shape=[jax.ShapeDtypeStruct((m, n), F32)],
        sem=("parallel", "parallel"), args=(a, b), comm=comm)
    return res[0], extra


def _row_ids(n, w):
    return lax.broadcasted_iota(jnp.int32, (n, w), 0)


def _lru_conv(xr, cw, cb):
    row = _row_ids(ZLEN, LRU_BLOCK_W)
    segpos = jnp.where(row < CTX_LEN, row, row - CTX_LEN)
    seglen = jnp.where(row < CTX_LEN, CTX_LEN, SEQ)
    acc = xr * cw[2:3, :] + cb
    for k in (0, 1, 3):
        off = k - 2
        sh = pltpu.roll(xr, (-off) % ZLEN, 0)
        ok = (segpos + off >= 0) & (segpos + off < seglen)
        acc = acc + jnp.where(ok, sh, 0.0) * cw[k:k + 1, :]
    return acc


def _lru_conv_t(dxc, cw):
    row = _row_ids(ZLEN, LRU_BLOCK_W)
    segpos = jnp.where(row < CTX_LEN, row, row - CTX_LEN)
    seglen = jnp.where(row < CTX_LEN, CTX_LEN, SEQ)
    acc = dxc * cw[2:3, :]
    for k in (0, 1, 3):
        off = k - 2
        sh = pltpu.roll(dxc, off % ZLEN, 0)
        ok = (segpos - off >= 0) & (segpos - off < seglen)
        acc = acc + jnp.where(ok, sh, 0.0) * cw[k:k + 1, :]
    return acc


def _lru_gates(xc, xcb, wa, ba, wx, bx, lam):
    r = _sigmoid(jnp.dot(xcb, wa, preferred_element_type=F32) + ba)
    i = _sigmoid(jnp.dot(xcb, wx, preferred_element_type=F32) + bx)
    sp = jnp.maximum(-lam, 0.0) + jnp.log1p(jnp.exp(-jnp.abs(lam)))
    la = (-LRU_C) * r * sp
    a = jnp.exp(la)
    sq = jnp.sqrt(-jnp.tanh(la) * (1.0 + a * a))
    b = sq * i * xc
    return r, i, sp, a, sq, b


def _scan8_fwd(a, b, rid):
    for s in (1, 2, 4):
        a_s = pltpu.roll(a, s, 0)
        b_s = pltpu.roll(b, s, 0)
        m = rid >= s
        b = jnp.where(m, a * b_s + b, b)
        a = jnp.where(m, a * a_s, a)
    return a, b


def _scan8_rev(a, b, rid):
    for s in (1, 2, 4):
        a_s = pltpu.roll(a, 8 - s, 0)
        b_s = pltpu.roll(b, 8 - s, 0)
        m = rid < 8 - s
        b = jnp.where(m, a * b_s + b, b)
        a = jnp.where(m, a * a_s, a)
    return a, b


N_CHUNK = ZLEN // 8
CTX_CHUNKS = CTX_LEN // 8
SCAN_UNROLL = 8


def _scan_up(a_ref, b_ref, h_ref, lo, hi, carry):
    rid = _row_ids(8, LRU_BLOCK_W)
    assert (hi - lo) % SCAN_UNROLL == 0

    def step(g, c):
        base = pl.multiple_of((lo + g * SCAN_UNROLL) * 8, 8)
        for u in range(SCAN_UNROLL):
            sl = pl.ds(base + 8 * u, 8)
            a, b = _scan8_fwd(a_ref[sl, :], b_ref[sl, :], rid)
            h_ref[sl, :] = b + a * c
            c = b[7:8, :] + a[7:8, :] * c
        return c

    return lax.fori_loop(0, (hi - lo) // SCAN_UNROLL, step, carry)


def _scan_down(a_ref, b_ref, h_ref, lo, hi, carry):
    rid = _row_ids(8, LRU_BLOCK_W)
    assert (hi - lo) % SCAN_UNROLL == 0

    def step(g, c):
        base = pl.multiple_of((hi - (g + 1) * SCAN_UNROLL) * 8, 8)
        for u in reversed(range(SCAN_UNROLL)):
            sl = pl.ds(base + 8 * u, 8)
            a, b = _scan8_rev(a_ref[sl, :], b_ref[sl, :], rid)
            h_ref[sl, :] = b + a * c
            c = b[0:1, :] + a[0:1, :] * c
        return c

    return lax.fori_loop(0, (hi - lo) // SCAN_UNROLL, step, carry)


def _lru_scan_dir(d, a_ref, b_ref, h_ref):
    zero = jnp.zeros((1, LRU_BLOCK_W), F32)
    if d == 0:
        _scan_up(a_ref, b_ref, h_ref, 0, N_CHUNK, zero)
    else:
        c = _scan_down(a_ref, b_ref, h_ref, 0, CTX_CHUNKS, zero)
        _scan_down(a_ref, b_ref, h_ref, CTX_CHUNKS, N_CHUNK, c)


def _lru_in_specs():
    blk = lambda rows: pl.BlockSpec((rows, LRU_BLOCK_W), lambda b: (0, b))
    wspec = pl.BlockSpec((2, 1, LRU_BLOCK_W, LRU_BLOCK_W), lambda b: (0, b, 0, 0))
    return blk, wspec


def lru_fwd(p, conv_w, conv_b, wa, ba, wx, bx, lam, comm=None):
    blk, wspec = _lru_in_specs()

    def body(xr_ref, gx_ref, cw_ref, cb_ref, wa_ref, ba_ref, wx_ref, bx_ref, lam_ref, y_ref, a_s, b_s, h_s, hsum_s):
        xr = xr_ref[...]
        xc = _lru_conv(xr, cw_ref[...], cb_ref[...])
        xcb = xc.astype(BF16)
        for d in (0, 1):
            _, _, _, a, _, b = _lru_gates(xc, xcb, wa_ref[d, 0].astype(BF16), ba_ref[d:d + 1, :],
                                          wx_ref[d, 0].astype(BF16), bx_ref[d:d + 1, :], lam_ref[d:d + 1, :])
            a_s[...] = a
            b_s[...] = b
            _lru_scan_dir(d, a_s, b_s, h_s)
            if d == 0:
                hsum_s[...] = h_s[...]
            else:
                hsum_s[...] = hsum_s[...] + h_s[...]
        g, _ = _gelu_parts(gx_ref[CTX_LEN:, :])
        y_ref[...] = (hsum_s[CTX_LEN:, :] * g).astype(BF16)

    zs = pltpu.VMEM((ZLEN, LRU_BLOCK_W), F32)
    res, extra = _call(
        body, name="lru_fwd", grid=(LRU_BLOCKS,),
        in_specs=[blk(ZLEN), pl.BlockSpec((ZLEN, LRU_BLOCK_W), lambda b: (0, 24 + b)), blk(4), blk(1),
                  wspec, blk(2), wspec, blk(2), blk(2)],
        out_specs=[pl.BlockSpec((SEQ, LRU_BLOCK_W), lambda b: (0, b))],
        out_shape=[jax.ShapeDtypeStruct((SEQ, D_MODEL), BF16)],
        scratch_shapes=[zs, zs, zs, zs], sem=("arbitrary",),
        args=(p, p, conv_w, conv_b, wa, ba, wx, bx, lam), comm=comm)
    return res[0], extra


def _rope_tables():
    t = np.arange(SEQ)
    lane = np.arange(2 * HEAD_DIM)
    in_head = lane % HEAD_DIM
    j = (in_head % 32) % 16
    freq = ROPE_BASE ** (-j.astype(np.float64) / 16.0)
    pos = np.where(in_head[None, :] < 32, (t // GRID_W)[:, None], (t % GRID_W)[:, None]).astype(np.float64)
    ang = (pos.astype(np.float32) * freq.astype(np.float32)[None, :]).astype(np.float32)
    cos = np.cos(ang).astype(np.float32)
    sin = np.sin(ang).astype(np.float32)
    sgn = np.where((in_head % 32) < 16, -1.0, 1.0).astype(np.float32)
    cos = np.concatenate([np.ones((CTX_LEN, 2 * HEAD_DIM), np.float32), cos], 0)
    sin = np.concatenate([np.zeros((CTX_LEN, 2 * HEAD_DIM), np.float32), sin * sgn[None, :]], 0)
    return jnp.asarray(cos), jnp.asarray(sin)


def _head_ones():
    lane = np.arange(2 * HEAD_DIM)
    return jnp.asarray((lane[:, None] // HEAD_DIM == lane[None, :] // HEAD_DIM).astype(np.float32))


def _rope_partner(x):
    lane = lax.broadcasted_iota(jnp.int32, x.shape, 1)
    return jnp.where((lane % 32) < 16, pltpu.roll(x, 128 - 16, 1), pltpu.roll(x, 16, 1))


def _head_rms(x, ones, gain):
    ms = jnp.dot(x * x, ones, preferred_element_type=F32, precision=lax.Precision.HIGHEST) * (1.0 / HEAD_DIM)
    rstd = lax.rsqrt(ms + EPS)
    return x * rstd * gain, rstd


PREP_TILE = 768


def qkv_prep(p, qg2, kg2, cos, sin, ones, comm=None):
    scale = HEAD_DIM ** -0.5

    def body(q_ref, k_ref, v_ref, qg_ref, kg_ref, cos_ref, sin_ref, ones_ref, qr_ref, qp_ref, kk_ref, vv_ref):
        ones_m = ones_ref[...]
        c, s = cos_ref[...], sin_ref[...]
        qn, _ = _head_rms(q_ref[...], ones_m, qg_ref[...])
        qn = qn * scale
        qr_ref[...] = (qn * c + _rope_partner(qn) * s).astype(BF16)
        qp_ref[...] = qn.astype(BF16)
        kn, _ = _head_rms(k_ref[...], ones_m, kg_ref[...])
        kk_ref[...] = (kn * c + _rope_partner(kn) * s).astype(BF16)
        vv_ref[...] = v_ref[...].astype(BF16)

    col = lambda base: pl.BlockSpec((PREP_TILE, 128), lambda hp, i: (i, base + hp))
    small = pl.BlockSpec((1, 128), lambda hp, i: (0, 0))
    tab = pl.BlockSpec((PREP_TILE, 128), lambda hp, i: (i, 0))
    oshape = jax.ShapeDtypeStruct((ZLEN, D_MODEL), BF16)
    res, extra = _call(
        body, name="qkv_prep", grid=(NA_HEADS // 2, ZLEN // PREP_TILE),
        in_specs=[col(32), col(8), col(16), small, small, tab, tab, _full((128, 128))],
        out_specs=[col(0)] * 4, out_shape=[oshape] * 4, sem=("parallel", "parallel"),
        args=(p, p, p, qg2, kg2, cos, sin, ones), comm=comm)
    return (*res, extra)


def _bias_expand():
    qc = np.arange(GRID_W)[:, None]
    kc = np.arange(GRID_W)[None, :]
    col_start = np.clip(qc - NA_COLS // 2, 0, GRID_W - NA_COLS)
    in_win = (kc >= col_start) & (kc < col_start + NA_COLS)
    dc = np.clip(kc - qc, -(NA_COLS - 1), NA_COLS - 1) + (NA_COLS - 1)
    e = np.zeros((2 * NA_COLS - 1, GRID_W, GRID_W), np.float32)
    for d in range(2 * NA_COLS - 1):
        e[d] = ((dc == d) & in_win).astype(np.float32)
    pen = np.where(in_win, 0.0, NEG_INF).astype(np.float32)
    return e, pen


def bias_table(rpb2, comm=None):
    e, pen = _bias_expand()
    n_dr = 2 * NA_ROWS - 1
    ea = np.zeros((31, GRID_W, 128), np.float32)
    ea[:, :, :GRID_W] = e
    eb = np.zeros((31, GRID_W, 128), np.float32)
    eb[:, :, GRID_W:] = e
    pen2 = np.concatenate([pen, pen], 1)
    ea = jnp.asarray(ea.reshape(31, GRID_W * 128))
    eb = jnp.asarray(eb.reshape(31, GRID_W * 128))
    sel_a = np.zeros((BT_LEN, n_dr), np.float32)
    sel_b = np.zeros((BT_LEN, n_dr), np.float32)
    for r in range(BT_LEN):
        dr = r - BT_PAD
        if 0 <= dr < n_dr:
            sel_a[r, dr] = 1.0
        if 0 <= dr + 1 < n_dr:
            sel_b[r, dr + 1] = 1.0
    sel_a, sel_b = jnp.asarray(sel_a), jnp.asarray(sel_b)
    pen2 = jnp.asarray(pen2.reshape(1, GRID_W * 128))
    hi = lax.Precision.HIGHEST

    def body(rpb_ref, sa_ref, sb_ref, ea_ref, eb_ref, pen_ref, o_ref, ra_s, rb_s):
        for h in range(NA_HEADS):
            rp = rpb_ref[h]
            ra_s[h * BT_LEN:(h + 1) * BT_LEN, :] = jnp.dot(sa_ref[...], rp, preferred_element_type=F32, precision=hi)
            rb_s[h * BT_LEN:(h + 1) * BT_LEN, :] = jnp.dot(sb_ref[...], rp, preferred_element_type=F32, precision=hi)
        o_ref[...] = (jnp.dot(ra_s[...], ea_ref[...], preferred_element_type=F32, precision=hi)
                      + jnp.dot(rb_s[...], eb_ref[...], preferred_element_type=F32, precision=hi) + pen_ref[...])

    tcol = 2048
    rows = NA_HEADS * BT_LEN
    res, extra = _call(
        body, name="bias_table", grid=(GRID_W * 128 // tcol,),
        in_specs=[_full((NA_HEADS, n_dr, 31)), _full((BT_LEN, n_dr)), _full((BT_LEN, n_dr)),
                  pl.BlockSpec((31, tcol), lambda j: (0, j)), pl.BlockSpec((31, tcol), lambda j: (0, j)),
                  pl.BlockSpec((1, tcol), lambda j: (0, j))],
        out_specs=[pl.BlockSpec((rows, tcol), lambda j: (0, j))],
        out_shape=[jax.ShapeDtypeStruct((rows, GRID_W * 128), F32)],
        scratch_shapes=[pltpu.VMEM((rows, 31), F32), pltpu.VMEM((rows, 31), F32)], sem=("parallel",),
        args=(rpb2, sel_a, sel_b, ea, eb, pen2), comm=comm)
    return res[0].reshape(NA_HEADS, BT_LEN, GRID_W, 128), extra


def _key_window(j):
    ws = jnp.clip(Q_ROWS * j - 4, 0, GRID_ROWS - KEY_ROWS)
    return ws, pl.multiple_of(CTX_LEN + ws * GRID_W, 256)


def _head_mask(hh):
    lane = lax.broadcasted_iota(jnp.int32, (Q_TILE, 128), 1)
    return (lane < HEAD_DIM) if hh == 0 else (lane >= HEAD_DIM)


def _attn_scores(j, ws, q_rot_h, q_pl_h, kw, kc, hh, bt_ref, s_ref):
    s_ref[:, :KEY_TILE] = _dot_nt(q_rot_h, kw)
    s_ref[:, KEY_TILE:] = _dot_nt(q_pl_h, kc)
    lane = lax.broadcasted_iota(jnp.int32, (GRID_W, 128), 1)
    base = ws - Q_ROWS * j + (NA_ROWS - 1) + BT_PAD
    for qi in range(Q_ROWS):
        rs = jnp.clip(Q_ROWS * j + qi - NA_ROWS // 2, 0, GRID_ROWS - NA_ROWS)
        for m in range(KEY_ROWS // 2):
            k0 = ws + 2 * m
            p0 = jnp.where((k0 >= rs) & (k0 < rs + NA_ROWS), 0.0, NEG_INF)
            p1 = jnp.where((k0 + 1 >= rs) & (k0 + 1 < rs + NA_ROWS), 0.0, NEG_INF)
            pen = jnp.where(lane < GRID_W, p0, p1)
            rows = slice(qi * GRID_W, (qi + 1) * GRID_W)
            cols = slice(128 * m, 128 * (m + 1))
            s_ref[rows, cols] = s_ref[rows, cols] + bt_ref[hh, base + 2 * m - qi] + pen
    return base


def attn_fwd(q_rot, q_pl, kk, vv, bt, comm=None):
    def body(qr_ref, qp_ref, kk_ref, vv_ref, bt_ref, o_ref, lse_ref, s_ref):
        j = pl.program_id(1)
        ws, start = _key_window(j)
        win = pl.ds(start, KEY_TILE)
        kw, kc = kk_ref[win, :], kk_ref[:CTX_LEN, :]
        vw, vc = vv_ref[win, :], vv_ref[:CTX_LEN, :]
        qr, qp = qr_ref[...], qp_ref[...]
        outs = []
        for hh in range(2):
            msk = _head_mask(hh)
            _attn_scores(j, ws, jnp.where(msk, qr, 0), jnp.where(msk, qp, 0), kw, kc, hh, bt_ref, s_ref)
            s = s_ref[...]
            mx = jnp.max(s, axis=-1, keepdims=True)
            pr = jnp.exp(s - mx)
            l = jnp.sum(pr, axis=-1, keepdims=True)
            prb = pr.astype(BF16)
            o = jnp.dot(prb[:, :KEY_TILE], vw, preferred_element_type=F32)
            o = o + jnp.dot(prb[:, KEY_TILE:], vc, preferred_element_type=F32)
            outs.append(o / l)
            lse_ref[hh] = mx + jnp.log(l)
        o_ref[...] = jnp.where(_head_mask(0), outs[0], outs[1])

    qspec = pl.BlockSpec((Q_TILE, 128), lambda hp, j: (j + 1, hp))
    kspec = pl.BlockSpec((ZLEN, 128), lambda hp, j: (0, hp))
    res, extra = _call(
        body, name="attn_fwd", grid=(NA_HEADS // 2, SEQ // Q_TILE),
        in_specs=[qspec, qspec, kspec, kspec, pl.BlockSpec((2, BT_LEN, GRID_W, 128), lambda hp, j: (hp, 0, 0, 0))],
        out_specs=[pl.BlockSpec((Q_TILE, 128), lambda hp, j: (j, hp)),
                   pl.BlockSpec((2, Q_TILE, 1), lambda hp, j: (hp, j, 0))],
        out_shape=[jax.ShapeDtypeStruct((SEQ, D_MODEL), F32), jax.ShapeDtypeStruct((NA_HEADS, SEQ, 1), F32)],
        scratch_shapes=[pltpu.VMEM((Q_TILE, KEY_TILE + CTX_LEN), F32)], sem=("parallel", "arbitrary"),
        args=(q_rot, q_pl, kk, vv, bt), comm=comm)
    return res[0], res[1], extra


def merge_fwd(y_rnn, y_na, p, z, g2, w_rnn, w_na, w_out):
    def body(yr_ref, yn_ref, mr_ref, mn_ref, x_ref, g2_ref, wr_ref, wn_ref, wo_ref, u_ref, v_ref, mg_ref, out_ref, x1_ref):
        u = jnp.dot(yr_ref[...], wr_ref[...], preferred_element_type=F32)
        v = jnp.dot(yn_ref[...].astype(BF16), wn_ref[...], preferred_element_type=F32)
        merged = (_sigmoid(mr_ref[...]) * u + _sigmoid(mn_ref[...]) * v).astype(BF16)
        out = jnp.dot(merged, wo_ref[...], preferred_element_type=F32)
        u_ref[...] = u
        v_ref[...] = v
        mg_ref[...] = merged
        out_ref[...] = out
        x1_ref[...] = x_ref[...] + g2_ref[...] * out

    row = pl.BlockSpec((ROW_TILE, D_MODEL), lambda i: (i, 0))
    lat = lambda cb: pl.BlockSpec((ROW_TILE, D_MODEL), lambda i: (i + 1, cb))
    wspec = _full((D_MODEL, D_MODEL))
    f32o = jax.ShapeDtypeStruct((SEQ, D_MODEL), F32)
    return pl.pallas_call(
        body, name="merge_fwd", grid=(SEQ // ROW_TILE,),
        in_specs=[row, row, lat(5), lat(6), lat(0), _full((1, D_MODEL)), wspec, wspec, wspec],
        out_specs=[row] * 5,
        out_shape=[f32o, f32o, jax.ShapeDtypeStruct((SEQ, D_MODEL), BF16), f32o, f32o],
        compiler_params=_params("parallel"),
    )(y_rnn, y_na, p, p, z, g2, w_rnn, w_na, w_out)


FF_TILE = 256
FF_TILES = D_FF // FF_TILE


def _ffn_conv(h, cw, cb):
    row = _row_ids(SEQ, FF_TILE)
    prev = jnp.where(row >= 1, pltpu.roll(h, 1, 0), 0.0)
    nxt = jnp.where(row < SEQ - 1, pltpu.roll(h, SEQ - 1, 0), 0.0)
    return prev * cw[0:1, :] + h * cw[1:2, :] + nxt * cw[2:3, :] + cb


def ffn_act(hpre, conv_w, conv_b):
    def body(ha_ref, hg_ref, wa_ref, wg_ref, ba_ref, bg_ref, o_ref):
        a = _ffn_conv(ha_ref[...], wa_ref[...], ba_ref[...])
        g = _ffn_conv(hg_ref[...], wg_ref[...], bg_ref[...])
        o_ref[...] = (a * _sigmoid(a) * g).astype(BF16)

    col = lambda rows, off: pl.BlockSpec((rows, FF_TILE), lambda j: (0, j + off))
    return pl.pallas_call(
        body, name="ffn_act", grid=(FF_TILES,),
        in_specs=[col(SEQ, 0), col(SEQ, FF_TILES), col(3, 0), col(3, FF_TILES), col(1, 0), col(1, FF_TILES)],
        out_specs=col(SEQ, 0),
        out_shape=jax.ShapeDtypeStruct((SEQ, D_FF), BF16),
        compiler_params=_params("parallel"),
    )(hpre, hpre, conv_w, conv_w, conv_b, conv_b)


def ffn_down_loss(act, w_down, x1, g5, target):
    def body(a_ref, w_ref, x1_ref, g5_ref, t_ref, f_ref, dy_ref, df_ref, ls_ref, dg_ref):
        i = pl.program_id(0)
        f = jnp.dot(a_ref[...], w_ref[...], preferred_element_type=F32)
        g5 = g5_ref[...]
        err = x1_ref[...] + g5 * f - t_ref[...]
        dy = err * (1.0 / D_MODEL)
        f_ref[...] = f
        dy_ref[...] = dy
        df_ref[...] = (dy * g5).astype(BF16)

        @pl.when(i == 0)
        def _():
            ls_ref[...] = jnp.zeros_like(ls_ref)
            dg_ref[...] = jnp.zeros_like(dg_ref)

        ls_ref[...] = ls_ref[...] + jnp.sum(err * err)
        dg_ref[...] = dg_ref[...] + jnp.sum(dy * f, axis=0, keepdims=True)

    row = pl.BlockSpec((ROW_TILE, D_MODEL), lambda i: (i, 0))
    f32o = jax.ShapeDtypeStruct((SEQ, D_MODEL), F32)
    return pl.pallas_call(
        body, name="ffn_down_loss", grid=(SEQ // ROW_TILE,),
        in_specs=[pl.BlockSpec((ROW_TILE, D_FF), lambda i: (i, 0)), _full((D_FF, D_MODEL)), row, _full((1, D_MODEL)), row],
        out_specs=[row, row, row, _full((8, 128)), _full((1, D_MODEL))],
        out_shape=[f32o, f32o, jax.ShapeDtypeStruct((SEQ, D_MODEL), BF16), jax.ShapeDtypeStruct((8, 128), F32),
                   jax.ShapeDtypeStruct((1, D_MODEL), F32)],
        compiler_params=_params("arbitrary"),
    )(act, w_down, x1, g5, target)


def ffn_down_bwd(df, w_down):
    def body(df_ref, w_ref, o_ref):
        o_ref[...] = _dot_nt(df_ref[...], w_ref[...])

    return pl.pallas_call(
        body, name="ffn_down_bwd", grid=(SEQ // ROW_TILE,),
        in_specs=[pl.BlockSpec((ROW_TILE, D_MODEL), lambda i: (i, 0)), _full((D_FF, D_MODEL))],
        out_specs=pl.BlockSpec((ROW_TILE, D_FF), lambda i: (i, 0)),
        out_shape=jax.ShapeDtypeStruct((SEQ, D_FF), F32),
        compiler_params=_params("parallel"),
    )(df, w_down)


def ffn_act_bwd(hpre, d_act, conv_w, conv_b):
    def half_bwd(dc, h, w, dh_ref, dw_ref, db_ref):
        row = _row_ids(SEQ, FF_TILE)
        h_prev = jnp.where(row >= 1, pltpu.roll(h, 1, 0), 0.0)
        h_next = jnp.where(row < SEQ - 1, pltpu.roll(h, SEQ - 1, 0), 0.0)
        dw_ref[0:1, :] = jnp.sum(dc * h_prev, axis=0, keepdims=True)
        dw_ref[1:2, :] = jnp.sum(dc * h, axis=0, keepdims=True)
        dw_ref[2:3, :] = jnp.sum(dc * h_next, axis=0, keepdims=True)
        db_ref[...] = jnp.sum(dc, axis=0, keepdims=True)
        dc_next = jnp.where(row < SEQ - 1, pltpu.roll(dc, SEQ - 1, 0), 0.0)
        dc_prev = jnp.where(row >= 1, pltpu.roll(dc, 1, 0), 0.0)
        dh_ref[...] = (dc_next * w[0:1, :] + dc * w[1:2, :] + dc_prev * w[2:3, :]).astype(BF16)

    def body(ha_ref, hg_ref, da_ref, wa_ref, wg_ref, ba_ref, bg_ref, dha_ref, dhg_ref, dwa_ref, dwg_ref, dba_ref, dbg_ref):
        ha, hg = ha_ref[...], hg_ref[...]
        a = _ffn_conv(ha, wa_ref[...], ba_ref[...])
        g = _ffn_conv(hg, wg_ref[...], bg_ref[...])
        sig = _sigmoid(a)
        dact = da_ref[...]
        half_bwd(dact * g * (sig * (1.0 + a * (1.0 - sig))), ha, wa_ref[...], dha_ref, dwa_ref, dba_ref)
        half_bwd(dact * a * sig, hg, wg_ref[...], dhg_ref, dwg_ref, dbg_ref)

    col = lambda rows, off: pl.BlockSpec((rows, FF_TILE), lambda j: (0, j + off))
    hshape = jax.ShapeDtypeStruct((SEQ, D_FF), BF16)
    wshape = jax.ShapeDtypeStruct((3, D_FF), F32)
    bshape = jax.ShapeDtypeStruct((1, D_FF), F32)
    return pl.pallas_call(
        body, name="ffn_act_bwd", grid=(FF_TILES,),
        in_specs=[col(SEQ, 0), col(SEQ, FF_TILES), col(SEQ, 0), col(3, 0), col(3, FF_TILES), col(1, 0), col(1, FF_TILES)],
        out_specs=[col(SEQ, 0), col(SEQ, 0), col(3, 0), col(3, 0), col(1, 0), col(1, 0)],
        out_shape=[hshape, hshape, wshape, wshape, bshape, bshape],
        compiler_params=_params("parallel"),
    )(hpre, hpre, d_act, conv_w, conv_w, conv_b, conv_b)


def _norm_mod_bwd(x, dxn, gain, scale):
    rstd = lax.rsqrt(jnp.mean(x * x, axis=-1, keepdims=True) + EPS)
    nrm = x * rstd
    dsh = jnp.sum(dxn, axis=0, keepdims=True)
    dsc = jnp.sum(dxn * nrm, axis=0, keepdims=True) * gain
    dgn = jnp.sum(dxn * nrm, axis=0, keepdims=True) * (1.0 + scale)
    dn = dxn * (gain * (1.0 + scale))
    dx = rstd * (dn - nrm * jnp.mean(dn * nrm, axis=-1, keepdims=True))
    return dx, dsh, dsc, dgn


def ffn_up_bwd(dha, dhg, w_up, x1, dy, gain, scale):
    def body(dha_ref, dhg_ref, w_ref, x_ref, dy_ref, g_ref, sc_ref, dx_ref, dsh_ref, dsc_ref, dgn_ref):
        i = pl.program_id(0)
        dxn = _dot_nt(dha_ref[...], w_ref[:, :D_FF]) + _dot_nt(dhg_ref[...], w_ref[:, D_FF:])
        dx, dsh, dsc, dgn = _norm_mod_bwd(x_ref[...], dxn, g_ref[...], sc_ref[...])
        dx_ref[...] = dy_ref[...] + dx

        @pl.when(i == 0)
        def _():
            dsh_ref[...] = dsh
            dsc_ref[...] = dsc
            dgn_ref[...] = dgn

        @pl.when(i > 0)
        def _():
            dsh_ref[...] = dsh_ref[...] + dsh
            dsc_ref[...] = dsc_ref[...] + dsc
            dgn_ref[...] = dgn_ref[...] + dgn

    row = pl.BlockSpec((ROW_TILE, D_MODEL), lambda i: (i, 0))
    vec = _full((1, D_MODEL))
    vshape = jax.ShapeDtypeStruct((1, D_MODEL), F32)
    return pl.pallas_call(
        body, name="ffn_up_bwd", grid=(SEQ // ROW_TILE,),
        in_specs=[pl.BlockSpec((ROW_TILE, D_FF), lambda i: (i, 0)), pl.BlockSpec((ROW_TILE, D_FF), lambda i: (i, 0)),
                  _full((D_MODEL, 2 * D_FF)), row, row, vec, vec],
        out_specs=[row, vec, vec, vec],
        out_shape=[jax.ShapeDtypeStruct((SEQ, D_MODEL), F32), vshape, vshape, vshape],
        compiler_params=_params("arbitrary"),
    )(dha, dhg, w_up, x1, dy, gain, scale)


def merge_bwd(dx1, out, g2, p, u, v, w_rnn, w_na, w_out, comm=None):
    def body(dx_ref, out_ref, g2_ref, mr_ref, mn_ref, u_ref, v_ref, wr_ref, wn_ref, wo_ref,
             dout_ref, du_ref, dv_ref, dmr_ref, dmn_ref, dyr_ref, dyn_ref, dg2_ref):
        i = pl.program_id(0)

        @pl.when(i == 0)
        def _():
            dmr_ref[...] = jnp.zeros_like(dmr_ref)
            dmn_ref[...] = jnp.zeros_like(dmn_ref)
            dg2_ref[...] = jnp.zeros_like(dg2_ref)

        @pl.when(i > 0)
        def _():
            dx = dx_ref[...]
            dg2_ref[...] = dg2_ref[...] + jnp.sum(dx * out_ref[...], axis=0, keepdims=True)
            dout = (dx * g2_ref[...]).astype(BF16)
            dout_ref[...] = dout
            dm = _dot_nt(dout, wo_ref[...])
            sr = _sigmoid(mr_ref[...])
            sn = _sigmoid(mn_ref[...])
            du = (dm * sr).astype(BF16)
            dv = (dm * sn).astype(BF16)
            du_ref[...] = du
            dv_ref[...] = dv
            dmr_ref[...] = (dm * u_ref[...] * (sr * (1.0 - sr))).astype(BF16)
            dmn_ref[...] = (dm * v_ref[...] * (sn * (1.0 - sn))).astype(BF16)
            dyr_ref[...] = _dot_nt(du, wr_ref[...])
            dyn_ref[...] = _dot_nt(dv, wn_ref[...])

    lat = pl.BlockSpec((ROW_TILE, D_MODEL), lambda i: (jnp.maximum(i - 1, 0), 0))
    zrow = pl.BlockSpec((ROW_TILE, D_MODEL), lambda i: (i, 0))
    pcol = lambda cb: pl.BlockSpec((ROW_TILE, D_MODEL), lambda i: (i, cb))
    wspec = _full((D_MODEL, D_MODEL))
    tb = jax.ShapeDtypeStruct((SEQ, D_MODEL), BF16)
    zb = jax.ShapeDtypeStruct((ZLEN, D_MODEL), BF16)
    tf = jax.ShapeDtypeStruct((SEQ, D_MODEL), F32)
    res, extra = _call(
        body, name="merge_bwd", grid=(ZLEN // ROW_TILE,),
        in_specs=[lat, lat, _full((1, D_MODEL)), pcol(5), pcol(6), lat, lat, wspec, wspec, wspec],
        out_specs=[lat, lat, lat, zrow, zrow, lat, lat, _full((1, D_MODEL))],
        out_shape=[tb, tb, tb, zb, zb, tf, tf, jax.ShapeDtypeStruct((1, D_MODEL), F32)],
        sem=("arbitrary",), args=(dx1, out, g2, p, p, u, v, w_rnn, w_na, w_out), comm=comm)
    return (*res, extra)


def attn_bwd(q_rot, q_pl, kk, vv, bt, y_na, d_yna, lse, comm=None):
    def body(qr_ref, qp_ref, kk_ref, vv_ref, bt_ref, o_ref, do_ref, lse_ref,
             dqr_ref, dqp_ref, dk_ref, dv_ref, dbt_ref, s_ref):
        jj = pl.program_id(1)

        @pl.when(jj == 0)
        def _():
            dqr_ref[...] = jnp.zeros_like(dqr_ref)
            dqp_ref[...] = jnp.zeros_like(dqp_ref)
            dk_ref[...] = jnp.zeros_like(dk_ref)
            dv_ref[...] = jnp.zeros_like(dv_ref)
            dbt_ref[...] = jnp.zeros_like(dbt_ref)

        @pl.when(jj > 0)
        def _():
            j = jj - 1
            ws, start = _key_window(j)
            win = pl.ds(start, KEY_TILE)
            kw, kc = kk_ref[win, :], kk_ref[:CTX_LEN, :]
            vw, vc = vv_ref[win, :], vv_ref[:CTX_LEN, :]
            qr, qp = qr_ref[...], qp_ref[...]
            do = do_ref[...]
            do_o = do * o_ref[...]
            dq_r, dq_p = [], []
            for hh in range(2):
                msk = _head_mask(hh)
                q_r, q_p = jnp.where(msk, qr, 0), jnp.where(msk, qp, 0)
                base = _attn_scores(j, ws, q_r, q_p, kw, kc, hh, bt_ref, s_ref)
                pr = jnp.exp(s_ref[...] - lse_ref[hh])
                delta = jnp.sum(jnp.where(msk, do_o, 0.0), axis=-1, keepdims=True)
                dob = jnp.where(msk, do, 0.0).astype(BF16)
                ds_lat = pr[:, :KEY_TILE] * (_dot_nt(dob, vw) - delta)
                ds_ctx = pr[:, KEY_TILE:] * (_dot_nt(dob, vc) - delta)
                for qi in range(Q_ROWS):
                    for m in range(KEY_ROWS // 2):
                        idx = base + 2 * m - qi
                        dbt_ref[hh, idx] = dbt_ref[hh, idx] + ds_lat[qi * GRID_W:(qi + 1) * GRID_W, 128 * m:128 * (m + 1)]
                dsb_lat = ds_lat.astype(BF16)
                dsb_ctx = ds_ctx.astype(BF16)
                prb = pr.astype(BF16)
                dq_r.append(jnp.dot(dsb_lat, kw, preferred_element_type=F32))
                dq_p.append(jnp.dot(dsb_ctx, kc, preferred_element_type=F32))
                dk_ref[win, :] = dk_ref[win, :] + _dot_tn(dsb_lat, q_r)
                dk_ref[:CTX_LEN, :] = dk_ref[:CTX_LEN, :] + _dot_tn(dsb_ctx, q_p)
                dv_ref[win, :] = dv_ref[win, :] + _dot_tn(prb[:, :KEY_TILE], dob)
                dv_ref[:CTX_LEN, :] = dv_ref[:CTX_LEN, :] + _dot_tn(prb[:, KEY_TILE:], dob)
            dqr_ref[...] = jnp.where(_head_mask(0), dq_r[0], dq_r[1])
            dqp_ref[...] = jnp.where(_head_mask(0), dq_p[0], dq_p[1])

    lat = lambda jj: jnp.maximum(jj - 1, 0)
    qspec = pl.BlockSpec((Q_TILE, 128), lambda hp, jj: (lat(jj) + 1, hp))
    kspec = pl.BlockSpec((ZLEN, 128), lambda hp, jj: (0, hp))
    btspec = pl.BlockSpec((2, BT_LEN, GRID_W, 128), lambda hp, jj: (hp, 0, 0, 0))
    ospec = pl.BlockSpec((Q_TILE, 128), lambda hp, jj: (lat(jj), hp))
    dqspec = pl.BlockSpec((Q_TILE, 128), lambda hp, jj: (jj, hp))
    zshape = jax.ShapeDtypeStruct((ZLEN, D_MODEL), F32)
    res, extra = _call(
        body, name="attn_bwd", grid=(NA_HEADS // 2, ZLEN // Q_TILE),
        in_specs=[qspec, qspec, kspec, kspec, btspec, ospec, ospec,
                  pl.BlockSpec((2, Q_TILE, 1), lambda hp, jj: (hp, lat(jj), 0))],
        out_specs=[dqspec, dqspec, kspec, kspec, btspec],
        out_shape=[zshape, zshape, zshape, zshape, jax.ShapeDtypeStruct((NA_HEADS, BT_LEN, GRID_W, 128), F32)],
        scratch_shapes=[pltpu.VMEM((Q_TILE, KEY_TILE + CTX_LEN), F32)], sem=("parallel", "arbitrary"),
        args=(q_rot, q_pl, kk, vv, bt, y_na, d_yna, lse), comm=comm)
    return (*res, extra)


def qkv_bwd(dq_rot, dq_pl, dk, dv, p, qg2, kg2, cos, sin, ones, comm=None):
    scale = HEAD_DIM ** -0.5
    n_hp, n_i = NA_HEADS // 2, ZLEN // PREP_TILE

    def norm_rope_bwd(d_rot, d_extra, x, gain, cos_t, sin_t, ones_m, dx_ref, acc_ref):
        xh, rstd = _head_rms(x, ones_m, 1.0)
        dn = d_rot * cos_t + _rope_partner(d_rot * sin_t)
        if d_extra is not None:
            dn = (dn + d_extra) * scale
        acc_ref[...] = acc_ref[...] + jnp.sum(dn * xh, axis=0, keepdims=True)
        dxh = dn * gain
        seg = jnp.dot(dxh * xh, ones_m, preferred_element_type=F32, precision=lax.Precision.HIGHEST) * (1.0 / HEAD_DIM)
        dx_ref[...] = (rstd * (dxh - xh * seg)).astype(BF16)

    def body(dqr_ref, dqp_ref, dk_ref, dv_ref, xq_ref, xk_ref, qg_ref, kg_ref, cos_ref, sin_ref, ones_ref,
             dxq_ref, dxk_ref, dxv_ref, dgq_ref, dgk_ref, accq_ref, acck_ref):
        hp, i = pl.program_id(0), pl.program_id(1)

        @pl.when((hp == 0) & (i == 0))
        def _():
            accq_ref[...] = jnp.zeros_like(accq_ref)
            acck_ref[...] = jnp.zeros_like(acck_ref)

        ones_m = ones_ref[...]
        cos_t, sin_t = cos_ref[...], sin_ref[...]
        norm_rope_bwd(dqr_ref[...], dqp_ref[...], xq_ref[...], qg_ref[...], cos_t, sin_t, ones_m, dxq_ref, accq_ref)
        norm_rope_bwd(dk_ref[...], None, xk_ref[...], kg_ref[...], cos_t, sin_t, ones_m, dxk_ref, acck_ref)
        dxv_ref[...] = dv_ref[...].astype(BF16)

        @pl.when((hp == n_hp - 1) & (i == n_i - 1))
        def _():
            dgq_ref[...] = accq_ref[:, :HEAD_DIM] + accq_ref[:, HEAD_DIM:]
            dgk_ref[...] = acck_ref[:, :HEAD_DIM] + acck_ref[:, HEAD_DIM:]

    col = lambda base: pl.BlockSpec((PREP_TILE, 128), lambda hp, i: (i, base + hp))
    small = pl.BlockSpec((1, 128), lambda hp, i: (0, 0))
    tab = pl.BlockSpec((PREP_TILE, 128), lambda hp, i: (i, 0))
    zb = jax.ShapeDtypeStruct((ZLEN, D_MODEL), BF16)
    gshape = jax.ShapeDtypeStruct((1, HEAD_DIM), F32)
    res, extra = _call(
        body, name="qkv_bwd", grid=(n_hp, n_i),
        in_specs=[col(0)] * 4 + [col(32), col(8), small, small, tab, tab, _full((128, 128))],
        out_specs=[col(0)] * 3 + [_full((1, HEAD_DIM))] * 2,
        out_shape=[zb, zb, zb, gshape, gshape],
        scratch_shapes=[pltpu.VMEM((1, 128), F32)] * 2, sem=("arbitrary", "arbitrary"),
        args=(dq_rot, dq_pl, dk, dv, p, p, qg2, kg2, cos, sin, ones), comm=comm)
    return (*res, extra)


def rpb_grad(dbt):
    e, _ = _bias_expand()
    n_dr = 2 * NA_ROWS - 1
    ea = np.zeros((31, GRID_W, 128), np.float32)
    ea[:, :, :GRID_W] = e
    eb = np.zeros((31, GRID_W, 128), np.float32)
    eb[:, :, GRID_W:] = e
    eat = jnp.asarray(ea.reshape(31, GRID_W * 128).T.copy())
    ebt = jnp.asarray(eb.reshape(31, GRID_W * 128).T.copy())
    sel_at = np.zeros((n_dr, BT_LEN), np.float32)
    sel_bt = np.zeros((n_dr, BT_LEN), np.float32)
    for r in range(BT_LEN):
        dr = r - BT_PAD
        if 0 <= dr < n_dr:
            sel_at[dr, r] = 1.0
        if 0 <= dr + 1 < n_dr:
            sel_bt[dr + 1, r] = 1.0
    hi = lax.Precision.HIGHEST

    tk = 2048
    wide = GRID_W * 128
    rows = NA_HEADS * BT_LEN
    n_k = wide // tk

    def body(d_ref, sa_ref, sb_ref, ea_ref, eb_ref, o_ref, a_s, b_s):
        k = pl.program_id(0)
        dm = d_ref[...]
        a = jnp.dot(dm, ea_ref[...], preferred_element_type=F32, precision=hi)
        b = jnp.dot(dm, eb_ref[...], preferred_element_type=F32, precision=hi)

        @pl.when(k == 0)
        def _():
            a_s[...] = a
            b_s[...] = b

        @pl.when(k > 0)
        def _():
            a_s[...] = a_s[...] + a
            b_s[...] = b_s[...] + b

        @pl.when(k == n_k - 1)
        def _():
            for h in range(NA_HEADS):
                sl = slice(h * BT_LEN, (h + 1) * BT_LEN)
                o_ref[h] = (jnp.dot(sa_ref[...], a_s[sl, :], preferred_element_type=F32, precision=hi)
                            + jnp.dot(sb_ref[...], b_s[sl, :], preferred_element_type=F32, precision=hi))

    return pl.pallas_call(
        body, name="rpb_grad", grid=(n_k,),
        in_specs=[pl.BlockSpec((rows, tk), lambda k: (0, k)), _full((n_dr, BT_LEN)), _full((n_dr, BT_LEN)),
                  pl.BlockSpec((tk, 31), lambda k: (k, 0)), pl.BlockSpec((tk, 31), lambda k: (k, 0))],
        out_specs=_full((NA_HEADS, n_dr, 31)),
        out_shape=jax.ShapeDtypeStruct((NA_HEADS, n_dr, 31), F32),
        scratch_shapes=[pltpu.VMEM((rows, 31), F32), pltpu.VMEM((rows, 31), F32)],
        compiler_params=_params("arbitrary"),
    )(dbt.reshape(rows, wide), jnp.asarray(sel_at), jnp.asarray(sel_bt), eat, ebt)


def lru_bwd(p, d_yrnn, conv_w, conv_b, wa, ba, wx, bx, lam, comm=None):
    blk, wspec = _lru_in_specs()

    def body(xr_ref, gx_ref, dy_ref, cw_ref, cb_ref, wa_ref, ba_ref, wx_ref, bx_ref, lam_ref,
             dxr_ref, dgx_ref, dcw_ref, dcb_ref, dwa_ref, dba_ref, dwx_ref, dbx_ref, dlam_ref,
             a_s, b_s, h_s, l_s, hsum_s, dxc_s, dh_s):
        xr = xr_ref[...]
        cw = cw_ref[...]
        xc = _lru_conv(xr, cw, cb_ref[...])
        xcb = xc.astype(BF16)
        g, dg = _gelu_parts(gx_ref[CTX_LEN:, :])
        dy = dy_ref[...]
        dh_s[:CTX_LEN, :] = jnp.zeros((CTX_LEN, LRU_BLOCK_W), F32)
        dh_s[CTX_LEN:, :] = dy * g
        row = _row_ids(ZLEN, LRU_BLOCK_W)
        zero = jnp.zeros((1, LRU_BLOCK_W), F32)
        for d in (0, 1):
            wab = wa_ref[d, 0].astype(BF16)
            wxb = wx_ref[d, 0].astype(BF16)
            lam_d = lam_ref[d:d + 1, :]
            r, gi, sp, a, sq, b = _lru_gates(xc, xcb, wab, ba_ref[d:d + 1, :], wxb, bx_ref[d:d + 1, :], lam_d)
            a_s[...] = a
            b_s[...] = b
            _lru_scan_dir(d, a_s, b_s, h_s)
            h = h_s[...]
            if d == 0:
                hsum_s[...] = h
                h_prev = jnp.where(row >= 1, pltpu.roll(h, 1, 0), 0.0)
                a_s[...] = pltpu.roll(a, ZLEN - 1, 0)
                _scan_down(a_s, dh_s, l_s, 0, N_CHUNK, zero)
            else:
                hsum_s[...] = hsum_s[...] + h
                h_prev = jnp.where(row == CTX_LEN - 1, 0.0, pltpu.roll(h, ZLEN - 1, 0))
                a_s[...] = pltpu.roll(a, 1, 0)
                c = _scan_up(a_s, dh_s, l_s, CTX_CHUNKS, N_CHUNK, zero)
                _scan_up(a_s, dh_s, l_s, 0, CTX_CHUNKS, c)
            db = l_s[...]
            da = db * h_prev
            dsq = db * gi * xc
            dgi = db * sq * xc
            dxc_d = db * sq * gi
            dla = da * a - dsq * (a * a) / sq
            dr = dla * ((-LRU_C) * sp)
            dsp = jnp.sum(dla * ((-LRU_C) * r), axis=0, keepdims=True)
            dlam_ref[d:d + 1, :] = -dsp * _sigmoid(-lam_d)
            dzr = dr * r * (1.0 - r)
            dzi = dgi * gi * (1.0 - gi)
            dba_ref[d:d + 1, :] = jnp.sum(dzr, axis=0, keepdims=True)
            dbx_ref[d:d + 1, :] = jnp.sum(dzi, axis=0, keepdims=True)
            dzrb = dzr.astype(BF16)
            dzib = dzi.astype(BF16)
            dwa_ref[d, 0] = _dot_tn(xcb, dzrb)
            dwx_ref[d, 0] = _dot_tn(xcb, dzib)
            dxc_d = dxc_d + _dot_nt(dzrb, wab) + _dot_nt(dzib, wxb)
            if d == 0:
                dxc_s[...] = dxc_d
            else:
                dxc_s[...] = dxc_s[...] + dxc_d
        dxc = dxc_s[...]
        dxr_ref[...] = _lru_conv_t(dxc, cw).astype(BF16)
        dcb_ref[...] = jnp.sum(dxc, axis=0, keepdims=True)
        segpos = jnp.where(row < CTX_LEN, row, row - CTX_LEN)
        seglen = jnp.where(row < CTX_LEN, CTX_LEN, SEQ)
        for k in range(4):
            off = k - 2
            if off == 0:
                sh = xr
            else:
                ok = (segpos + off >= 0) & (segpos + off < seglen)
                sh = jnp.where(ok, pltpu.roll(xr, (-off) % ZLEN, 0), 0.0)
            dcw_ref[k:k + 1, :] = jnp.sum(dxc * sh, axis=0, keepdims=True)
        dgx_ref[:CTX_LEN, :] = jnp.zeros((CTX_LEN, LRU_BLOCK_W), BF16)
        dgx_ref[CTX_LEN:, :] = (dy * hsum_s[CTX_LEN:, :] * dg).astype(BF16)

    zs = pltpu.VMEM((ZLEN, LRU_BLOCK_W), F32)
    zb = jax.ShapeDtypeStruct((ZLEN, D_MODEL), BF16)
    v2 = jax.ShapeDtypeStruct((2, D_MODEL), F32)
    w4 = jax.ShapeDtypeStruct((2, LRU_BLOCKS, LRU_BLOCK_W, LRU_BLOCK_W), F32)
    res, extra = _call(
        body, name="lru_bwd", grid=(LRU_BLOCKS,),
        in_specs=[blk(ZLEN), pl.BlockSpec((ZLEN, LRU_BLOCK_W), lambda b: (0, 24 + b)), blk(SEQ), blk(4), blk(1),
                  wspec, blk(2), wspec, blk(2), blk(2)],
        out_specs=[blk(ZLEN), blk(ZLEN), blk(4), blk(1), wspec, blk(2), wspec, blk(2), blk(2)],
        out_shape=[zb, zb, jax.ShapeDtypeStruct((4, D_MODEL), F32), jax.ShapeDtypeStruct((1, D_MODEL), F32),
                   w4, v2, w4, v2, v2],
        scratch_shapes=[zs] * 7, sem=("arbitrary",),
        args=(p, p, d_yrnn, conv_w, conv_b, wa, ba, wx, bx, lam), comm=comm)
    return (*res, extra)


def in_proj_bwd(dgs, w_in, z, dx1, gain, scale, comm=None):
    def body(*refs):
        dg_refs = refs[:7]
        w_ref, z_ref, dx1_ref, g_ref, sc_ref, gx_ref, dsh_ref, dsc_ref, dgn_ref = refs[7:]
        i = pl.program_id(0)
        dxn = _dot_nt(dg_refs[0][...], w_ref[:, 0:D_MODEL])
        for g in range(1, 7):
            dxn = dxn + _dot_nt(dg_refs[g][...], w_ref[:, g * D_MODEL:(g + 1) * D_MODEL])
        dx, dsh, dsc, dgn = _norm_mod_bwd(z_ref[...], dxn, g_ref[...], sc_ref[0])

        @pl.when(i <= 1)
        def _():
            dsh_ref[0] = dsh
            dsc_ref[0] = dsc

        @pl.when(i > 1)
        def _():
            dsh_ref[0] = dsh_ref[0] + dsh
            dsc_ref[0] = dsc_ref[0] + dsc

        @pl.when(i == 0)
        def _():
            dgn_ref[...] = dgn

        @pl.when(i > 0)
        def _():
            dgn_ref[...] = dgn_ref[...] + dgn
            gx_ref[...] = dx1_ref[...] + dx

    zrow = pl.BlockSpec((ROW_TILE, D_MODEL), lambda i: (i, 0))
    lat = pl.BlockSpec((ROW_TILE, D_MODEL), lambda i: (jnp.maximum(i - 1, 0), 0))
    mod = pl.BlockSpec((1, 1, D_MODEL), lambda i: (jnp.minimum(i, 1), 0, 0))
    mshape = jax.ShapeDtypeStruct((2, 1, D_MODEL), F32)
    res, extra = _call(
        body, name="in_proj_bwd", grid=(ZLEN // ROW_TILE,),
        in_specs=[zrow] * 7 + [_full((D_MODEL, IN_COLS)), zrow, lat, _full((1, D_MODEL)), mod],
        out_specs=[lat, mod, mod, _full((1, D_MODEL))],
        out_shape=[jax.ShapeDtypeStruct((SEQ, D_MODEL), F32), mshape, mshape, jax.ShapeDtypeStruct((1, D_MODEL), F32)],
        sem=("arbitrary",), args=(*dgs, w_in, z, dx1, gain, scale), comm=comm)
    return (*res, extra)


def matmul_tn(a, b, name, tm, tn, prev=None, col_block=0, total_cols=None):
    k, m = a.shape
    n = b.shape[1]
    total_cols = n if total_cols is None else total_cols
    assert m % tm == 0 and n % tn == 0
    off = col_block * (n // tn)

    def body(a_ref, b_ref, *rest):
        rest[-1][...] = _dot_tn(a_ref[...].astype(BF16), b_ref[...]).astype(BF16)

    in_specs = [pl.BlockSpec((k, tm), lambda i, j: (0, i)), pl.BlockSpec((k, tn), lambda i, j: (0, j))]
    args = [a, b]
    aliases = {}
    if prev is not None:
        in_specs.append(pl.BlockSpec(memory_space=pl.ANY))
        args.append(prev)
        aliases = {2: 0}
    return pl.pallas_call(
        body, name=name, grid=(m // tm, n // tn), in_specs=in_specs,
        out_specs=pl.BlockSpec((tm, tn), lambda i, j: (i, j + off)),
        out_shape=jax.ShapeDtypeStruct((m, total_cols), BF16),
        input_output_aliases=aliases,
        compiler_params=_params("parallel", "parallel"),
    )(*args)


def local_step(z, target, modx, modc, norm_mix_g, norm_ffn_g, w_in, conv_w, conv_b, wa, ba, wx, bx, lam, qg, kg, rpb,
               w_rnn, w_na, w_out, w_up, fconv_w, fconv_b, w_down, idx=None, bt=None):
    dist = idx is not None
    c_idx = idx[1:2] if dist else None
    d = D_MODEL
    mx = [modx[:, k * d:(k + 1) * d] for k in range(N_MOD)]
    shift = jnp.stack([modc[:, 0:d], mx[0]])
    scale = jnp.stack([modc[:, d:2 * d], mx[1]])
    cos, sin = _rope_tables()
    ones = _head_ones()
    qg2 = jnp.tile(qg, (1, 2))
    kg2 = jnp.tile(kg, (1, 2))

    xn = norm_mod(z, norm_mix_g, shift, scale, "norm_mix")
    if bt is None:
        bt, _ = bias_table(rpb)
    p, _ = matmul_wide(xn, w_in, "in_proj", 3 * ROW_TILE, 1792)
    y_rnn, got = lru_fwd(p, conv_w, conv_b, wa, ba, wx, bx, lam,
                         comm=gather_weights_comm([w_down], [5]) if dist else None)
    if dist:
        w_down = got[0]
    q_rot, q_pl, kk, vv, got = qkv_prep(p, qg2, kg2, cos, sin, ones,
                                        comm=gather_weights_comm([w_rnn, w_na, w_out], [1, 2, 3]) if dist else None)
    if dist:
        w_rnn, w_na, w_out = got
    y_na, lse, got = attn_fwd(q_rot, q_pl, kk, vv, bt, comm=gather_weights_comm([w_up], [4]) if dist else None)
    if dist:
        w_up = got[0]
    u, v, merged, out, x1 = merge_fwd(y_rnn, y_na, p, z, mx[2], w_rnn, w_na, w_out)
    xn2 = norm_mod(x1, norm_ffn_g, mx[3][None], mx[4][None], "norm_ffn")
    hpre, _ = matmul_wide(xn2, w_up, "ffn_up", 2 * ROW_TILE, 1408)
    act = ffn_act(hpre, fconv_w, fconv_b)
    f, dy, df, loss_sq, dg5 = ffn_down_loss(act, w_down, x1, mx[5], target)

    partials, pieces = {}, {}

    def views_of(which, grads):
        return [_grad_view(g, BIG[w][1], BIG[w][2]) for w, g in zip(which, grads)]

    def chip_partials(which, views, recv):
        for w, gv, r in zip(which, views, recv):
            partials[w] = add_halves(gv, r, c_idx, "add_halves_" + BIG[w][0])
        return scatter_pieces_comm([partials[w] for w in which], which)

    d_act = ffn_down_bwd(df, w_down)
    dha, dhg, d_fcw_a, d_fcw_g, d_fcb_a, d_fcb_g = ffn_act_bwd(hpre, d_act, fconv_w, fconv_b)
    d_fcw = jnp.concatenate([d_fcw_a, d_fcw_g], axis=1)
    d_fcb = jnp.concatenate([d_fcb_a, d_fcb_g], axis=1)
    dx1, d_s3, d_s4, d_gffn = ffn_up_bwd(dha, dhg, w_up, x1, dy, norm_ffn_g, mx[4])
    g_w_down = matmul_tn(act, df, "gw_down", 256, D_MODEL)
    g_w_up = matmul_tn(xn2, dha, "gw_up_a", 512, 1408, total_cols=2 * D_FF)
    g_w_up = matmul_tn(xn2, dhg, "gw_up_g", 512, 1408, prev=g_w_up, col_block=1, total_cols=2 * D_FF)
    v_ffn = views_of([4, 5], [g_w_up, g_w_down]) if dist else None
    *mb, got = merge_bwd(dx1, out, mx[2], p, u, v, w_rnn, w_na, w_out,
                         comm=exchange_halves_comm(v_ffn) if dist else None)
    dout, du, dv, dmr, dmn, dyr, dyn, dg2 = mb
    recv_ffn = got
    g_w_out = matmul_tn(merged, dout, "gw_out", 1024, 512)
    g_w_rnn = matmul_tn(y_rnn, du, "gw_rnn", 1024, 512)
    g_w_na = matmul_tn(y_na, dv, "gw_na", 1024, 512)
    v_mix = views_of([1, 2, 3], [g_w_rnn, g_w_na, g_w_out]) if dist else None
    *lru_grads, got = lru_bwd(p, dyr, conv_w, conv_b, wa, ba, wx, bx, lam,
                              comm=join_comms(chip_partials([4, 5], v_ffn, recv_ffn),
                                              exchange_halves_comm(v_mix)) if dist else None)
    dxr, dgx, d_cw, d_cb, d_wa, d_ba, d_wx, d_bx, d_lam = lru_grads
    if dist:
        pieces[4], pieces[5] = got[:2]
    lru_w_all = {}
    dqr, dqp, dk, dvh, dbt, got = attn_bwd(
        q_rot, q_pl, kk, vv, bt, y_na, dyn, lse,
        comm=join_comms(chip_partials([1, 2, 3], v_mix, got[2:]),
                        all_gather_comm(d_wa.reshape(-1, LRU_BLOCK_W))) if dist else None)
    if dist:
        pieces[1], pieces[2], pieces[3], lru_w_all["lru_wa"] = got
    dq_cols, dk_cols, dv_cols, d_qg, d_kg, got = qkv_bwd(
        dqr, dqp, dk, dvh, p, qg2, kg2, cos, sin, ones,
        comm=all_gather_comm(d_wx.reshape(-1, LRU_BLOCK_W)) if dist else None)
    if dist:
        lru_w_all["lru_wx"] = got[0]
    d_rpb = rpb_grad(dbt)
    dgs = [dxr, dk_cols, dv_cols, dgx, dq_cols, dmr, dmn]
    g_w_in = None
    for g in range(7):
        g_w_in = matmul_tn(xn, dgs[g], "gw_in_%d" % g, 1024, 512, prev=g_w_in, col_block=g, total_cols=IN_COLS)
    if dist:
        v_in = views_of([0], [g_w_in])
        recv_in = run_comm(exchange_halves_comm(v_in), "grad_exchange_w_in")
    grad_x, dsh, dsc, d_gmix, got = in_proj_bwd(dgs, w_in, z, dx1, norm_mix_g, scale,
                                                comm=chip_partials([0], v_in, recv_in) if dist else None)
    if dist:
        pieces[0] = got[0]

    d_modx = jnp.concatenate([dsh[1], dsc[1], dg2, d_s3, d_s4, dg5], axis=1)
    d_modc = jnp.concatenate([dsh[0], dsc[0]], axis=1)
    return dict(loss_sq=loss_sq, grad_x=grad_x, d_modx=d_modx, d_modc=d_modc, norm_mix_g=d_gmix, norm_ffn_g=d_gffn,
                w_in=g_w_in, lru_conv_w=d_cw, lru_conv_b=d_cb, lru_wa=d_wa, lru_ba=d_ba, lru_wx=d_wx, lru_bx=d_bx,
                lru_lambda=d_lam, q_norm_g=d_qg, k_norm_g=d_kg, na_rpb=d_rpb, w_rnn_out=g_w_rnn, w_na_out=g_w_na,
                w_out=g_w_out, w_up=g_w_up, ffn_conv_w=d_fcw, ffn_conv_b=d_fcb, w_down=g_w_down,
                partials=partials, pieces=pieces, lru_w_all=lru_w_all)


def _mesh_pos():
    return lax.axis_index("x"), lax.axis_index("y"), lax.axis_index("c")


def _other_chips(x, y):
    return [(1 - x, y), (x, 1 - y), (1 - x, 1 - y)]


def all_gather8(xs, name, with_sum=False):
    m, n = xs.shape
    assert m % 8 == 0

    def body(x_ref, out_ref, *rest):
        if with_sum:
            sum_ref, send_sems, recv_sems, local_sem = rest
        else:
            send_sems, recv_sems, local_sem = rest
        x, y, c = _mesh_pos()
        me, sibling = (x, y, c), (x, y, 1 - c)
        chips = _other_chips(x, y)

        def rows(px, py, pc):
            return out_ref.at[pl.ds((4 * px + 2 * py + pc) * m, m), :]

        def copy(k, block, to, src=None):
            return pltpu.make_async_remote_copy(
                src_ref=rows(*block) if src is None else src, dst_ref=rows(*block),
                send_sem=send_sems.at[k], recv_sem=recv_sems.at[k], device_id=to, device_id_type=MESH_T)

        mine = pltpu.make_async_copy(x_ref, rows(*me), local_sem)
        mine.start()
        first = [copy(0, me, sibling, src=x_ref)]
        first += [copy(1 + j, me, (*chip, c), src=x_ref) for j, chip in enumerate(chips)]
        for cp in first:
            cp.start()
        passed = [copy(4 + j, (*chip, c), sibling) for j, chip in enumerate(chips)]
        for j, chip in enumerate(chips):
            copy(1 + j, (*chip, c), me).wait_recv()
            passed[j].start()
        copy(0, sibling, me).wait_recv()
        for j, chip in enumerate(chips):
            copy(4 + j, (*chip, 1 - c), me).wait_recv()
        for cp in first + passed:
            cp.wait_send()
        mine.wait()
        if with_sum:
            acc = out_ref[0:m, :]
            for k in range(1, N_DEV):
                acc = acc + out_ref[k * m:(k + 1) * m, :]
            sum_ref[...] = acc

    vm = pl.BlockSpec(memory_space=pltpu.VMEM)
    out_shape = [jax.ShapeDtypeStruct((N_DEV * m, n), F32)]
    if with_sum:
        out_shape.append(jax.ShapeDtypeStruct((m, n), F32))
    res = pl.pallas_call(
        body, name=name, in_specs=[vm], out_specs=[vm] * len(out_shape), out_shape=out_shape,
        scratch_shapes=[pltpu.SemaphoreType.DMA((7,)), pltpu.SemaphoreType.DMA((7,)), pltpu.SemaphoreType.DMA],
        compiler_params=pltpu.CompilerParams(vmem_limit_bytes=VMEM_LIMIT_V7X),
    )(xs)
    return res if with_sum else res[0]


BIG = (("w_in", (D_MODEL, IN_COLS), 1), ("w_rnn_out", (D_MODEL, D_MODEL), 0), ("w_na_out", (D_MODEL, D_MODEL), 0),
       ("w_out", (D_MODEL, D_MODEL), 0), ("w_up", (D_MODEL, 2 * D_FF), 1), ("w_down", (D_FF, D_MODEL), 0))


def _shard_shape(full, axis):
    r, c = full
    return (r // N_SHARD, c) if axis == 0 else (r, c // N_SHARD)


def _slot(ref, full, axis, s, h):
    r, c = full
    if axis == 0:
        rs = r // N_SHARD
        return ref.at[pl.ds(s * rs + h * (rs // 2), rs // 2), :]
    cs = c // N_SHARD
    return ref.at[pl.ds(h * (r // 2), r // 2), pl.ds(s * cs, cs)]


def cast_into_full(x, full, axis, idx, name):
    r, c = x.shape
    tr = next(t for t in (512, 352, 256, 128) if r % t == 0)
    nb = r // tr

    def body(idx_ref, x_ref, o_ref):
        o_ref[...] = x_ref[...].astype(BF16)

    if axis == 0:
        out_spec = pl.BlockSpec((tr, c), lambda i, idx_ref: (idx_ref[0] * nb + i, 0))
    else:
        out_spec = pl.BlockSpec((tr, c), lambda i, idx_ref: (i, idx_ref[0]))
    return pl.pallas_call(
        body, name=name,
        grid_spec=pltpu.PrefetchScalarGridSpec(
            num_scalar_prefetch=1, grid=(nb,), in_specs=[pl.BlockSpec((tr, c), lambda i, idx_ref: (i, 0))],
            out_specs=out_spec),
        out_shape=jax.ShapeDtypeStruct(full, BF16),
        compiler_params=_params("parallel"),
    )(idx, x)


def run_comm(comm, name):
    k_in, k_out = len(comm.inputs), len(comm.out_shapes)

    def body(*refs):
        start, mid, end = comm.emit(refs[:k_in], refs[k_in:k_in + k_out], refs[k_in + k_out:])
        start()
        mid()
        end()

    hbm = pl.BlockSpec(memory_space=pl.ANY)
    return pl.pallas_call(
        body, name=name, in_specs=[hbm] * k_in, out_specs=[hbm] * k_out, out_shape=list(comm.out_shapes),
        input_output_aliases=dict(comm.aliases), scratch_shapes=list(comm.scratch),
        compiler_params=pltpu.CompilerParams(vmem_limit_bytes=VMEM_LIMIT_V7X),
    )(*comm.inputs)


def gather_weights_comm(fulls, which):
    nw = len(which)
    specs = [BIG[w] for w in which]

    def emit(_, outs, sems):
        send1, recv1, send2, recv2 = sems
        x, y, c = _mesh_pos()
        sibling = (x, y, 1 - c)
        chips = _other_chips(x, y)
        s_me = 2 * x + y
        shards = [2 * chip[0] + chip[1] for chip in chips]

        def ici(w, j, shard):
            _, full, axis = specs[w]
            dst = _slot(outs[w], full, axis, shard, c)
            return pltpu.make_async_remote_copy(
                src_ref=dst, dst_ref=dst, send_sem=send1.at[3 * w + j],
                recv_sem=recv1.at[3 * w + j], device_id=(*chips[j], c), device_id_type=MESH_T)

        def d2d(w, j, shard, half):
            _, full, axis = specs[w]
            dst = _slot(outs[w], full, axis, shard, half)
            return pltpu.make_async_remote_copy(
                src_ref=dst, dst_ref=dst, send_sem=send2.at[3 * w + j], recv_sem=recv2.at[3 * w + j],
                device_id=sibling, device_id_type=MESH_T)

        pairs = [(w, j) for w in range(nw) for j in range(3)]

        def start():
            for w, j in pairs:
                ici(w, j, s_me).start()

        def mid():
            for w, j in pairs:
                ici(w, j, shards[j]).wait_recv()
                d2d(w, j, shards[j], c).start()

        def end():
            for w, j in pairs:
                d2d(w, j, shards[j], 1 - c).wait_recv()
            for w, j in pairs:
                ici(w, j, s_me).wait_send()
                d2d(w, j, shards[j], c).wait_send()

        return start, mid, end

    return Comm(list(fulls), [jax.ShapeDtypeStruct(full, BF16) for _, full, _ in specs], {i: i for i in range(nw)},
                [pltpu.SemaphoreType.DMA((3 * nw,))] * 4, emit)


def join_comms(a, b):
    ai, ao, asc = len(a.inputs), len(a.out_shapes), len(a.scratch)

    def emit(ins, outs, sems):
        fa = a.emit(ins[:ai], outs[:ao], sems[:asc])
        fb = b.emit(ins[ai:], outs[ao:], sems[asc:])

        def both(k):
            def run():
                fa[k]()
                fb[k]()
            return run

        return both(0), both(1), both(2)

    aliases = dict(a.aliases)
    aliases.update({ai + i: ao + o for i, o in b.aliases.items()})
    return Comm(a.inputs + b.inputs, a.out_shapes + b.out_shapes, aliases, a.scratch + b.scratch, emit)


def all_gather_comm(x):
    def emit(srcs, outs, sems):
        send_sems, recv_sems, local_sem = sems
        x_ref, out_ref = srcs[0], outs[0]
        x, y, c = _mesh_pos()
        me, sibling = (x, y, c), (x, y, 1 - c)
        chips = _other_chips(x, y)

        def blk(px, py, pc):
            return out_ref.at[4 * px + 2 * py + pc]

        def copy(k, block, to, src=None):
            return pltpu.make_async_remote_copy(
                src_ref=blk(*block) if src is None else src, dst_ref=blk(*block),
                send_sem=send_sems.at[k], recv_sem=recv_sems.at[k], device_id=to, device_id_type=MESH_T)

        def mine():
            return pltpu.make_async_copy(x_ref, blk(*me), local_sem)

        def start():
            mine().start()
            copy(0, me, sibling, src=x_ref).start()
            for j, chip in enumerate(chips):
                copy(1 + j, me, (*chip, c), src=x_ref).start()

        def mid():
            for j, chip in enumerate(chips):
                copy(1 + j, (*chip, c), me).wait_recv()
                copy(4 + j, (*chip, c), sibling).start()

        def end():
            copy(0, sibling, me).wait_recv()
            for j, chip in enumerate(chips):
                copy(4 + j, (*chip, 1 - c), me).wait_recv()
            copy(0, me, sibling, src=x_ref).wait_send()
            for j, chip in enumerate(chips):
                copy(1 + j, me, (*chip, c), src=x_ref).wait_send()
                copy(4 + j, (*chip, c), sibling).wait_send()
            mine().wait()

        return start, mid, end

    return Comm([x], [jax.ShapeDtypeStruct((N_DEV,) + x.shape, F32)], {},
                [pltpu.SemaphoreType.DMA((7,)), pltpu.SemaphoreType.DMA((7,)), pltpu.SemaphoreType.DMA], emit)


def sum_blocks(g, name):
    _, r, c = g.shape
    tr = 256 if r % 256 == 0 else r

    def body(g_ref, o_ref):
        acc = g_ref[0]
        for k in range(1, N_DEV):
            acc = acc + g_ref[k]
        o_ref[...] = acc

    return pl.pallas_call(
        body, name=name, grid=(r // tr,),
        in_specs=[pl.BlockSpec((N_DEV, tr, c), lambda i: (0, i, 0))],
        out_specs=pl.BlockSpec((tr, c), lambda i: (i, 0)),
        out_shape=jax.ShapeDtypeStruct((r, c), F32),
        compiler_params=_params("parallel"),
    )(g)


def _grad_view(g, full, axis):
    r, c = full
    if axis == 0:
        return g.reshape(N_SHARD, 2, r // N_SHARD // 2, c)
    return g.reshape(1, 2, r // 2, c)


def exchange_halves_comm(gviews):
    nw = len(gviews)

    def emit(srcs, outs, sems):
        send_sems, recv_sems = sems
        x, y, c = _mesh_pos()

        def copies():
            return [pltpu.make_async_remote_copy(
                src_ref=srcs[w].at[:, pl.ds(1 - c, 1)], dst_ref=outs[w], send_sem=send_sems.at[w],
                recv_sem=recv_sems.at[w], device_id=(x, y, 1 - c), device_id_type=MESH_T) for w in range(nw)]

        def start():
            for cp in copies():
                cp.start()

        def end():
            for cp in copies():
                cp.wait()

        return start, lambda: None, end

    return Comm(list(gviews), [jax.ShapeDtypeStruct((g.shape[0], 1) + g.shape[2:], BF16) for g in gviews], {},
                [pltpu.SemaphoreType.DMA((nw,)), pltpu.SemaphoreType.DMA((nw,))], emit)


def _row_tile(rh):
    return 128 if rh % 128 == 0 else rh


def add_halves(gview, recv, c_idx, name):
    a, _, rh, cc = gview.shape
    tr = _row_tile(rh)

    def body(c_ref, g_ref, r_ref, o_ref):
        o_ref[0] = (g_ref[0, 0].astype(F32) + r_ref[0, 0].astype(F32)).astype(BF16)

    return pl.pallas_call(
        body, name=name,
        grid_spec=pltpu.PrefetchScalarGridSpec(
            num_scalar_prefetch=1, grid=(a, rh // tr),
            in_specs=[pl.BlockSpec((1, 1, tr, cc), lambda s, i, c_ref: (s, c_ref[0], i, 0)),
                      pl.BlockSpec((1, 1, tr, cc), lambda s, i, c_ref: (s, 0, i, 0))],
            out_specs=pl.BlockSpec((1, tr, cc), lambda s, i, c_ref: (s, i, 0))),
        out_shape=jax.ShapeDtypeStruct((a, rh, cc), BF16),
        compiler_params=_params("parallel", "parallel"),
    )(c_idx, gview, recv)


def _piece_shape(full, axis):
    rs, cs = _shard_shape(full, axis)
    return (rs // 2, cs)


def scatter_pieces_comm(partials, which):
    nw = len(which)
    specs = [BIG[w] for w in which]

    def emit(srcs, outs, sems):
        send_sems, recv_sems = sems
        x, y, c = _mesh_pos()
        chips = _other_chips(x, y)

        def copies():
            cps = []
            for w, (_, full, axis) in enumerate(specs):
                cs = full[1] // N_SHARD
                for j, chip in enumerate(chips):
                    s_j = 2 * chip[0] + chip[1]
                    src = srcs[w].at[s_j] if axis == 0 else srcs[w].at[0, :, pl.ds(s_j * cs, cs)]
                    cps.append(pltpu.make_async_remote_copy(
                        src_ref=src, dst_ref=outs[w].at[j], send_sem=send_sems.at[3 * w + j],
                        recv_sem=recv_sems.at[3 * w + j], device_id=(*chip, c), device_id_type=MESH_T))
            return cps

        def start():
            for cp in copies():
                cp.start()

        def mid():
            pass

        def end():
            for cp in copies():
                cp.wait()

        return start, mid, end

    return Comm(list(partials), [jax.ShapeDtypeStruct((3,) + _piece_shape(full, axis), BF16) for _, full, axis in specs],
                {}, [pltpu.SemaphoreType.DMA((3 * nw,)), pltpu.SemaphoreType.DMA((3 * nw,))], emit)


def add_pieces(partial, recv, idx, axis, name):
    _, rh, cs = recv.shape
    tr = _row_tile(rh)

    def body(idx_ref, p_ref, r_ref, o_ref):
        o_ref[0] = ((p_ref[0].astype(F32) + r_ref[0].astype(F32)) + r_ref[1].astype(F32)) + r_ref[2].astype(F32)

    if axis == 0:
        pspec = pl.BlockSpec((1, tr, cs), lambda i, idx_ref: (idx_ref[0], i, 0))
    else:
        pspec = pl.BlockSpec((1, tr, cs), lambda i, idx_ref: (0, i, idx_ref[0]))
    return pl.pallas_call(
        body, name=name,
        grid_spec=pltpu.PrefetchScalarGridSpec(
            num_scalar_prefetch=1, grid=(rh // tr,),
            in_specs=[pspec, pl.BlockSpec((3, tr, cs), lambda i, idx_ref: (0, i, 0))],
            out_specs=pl.BlockSpec((1, tr, cs), lambda i, idx_ref: (idx_ref[1], i, 0))),
        out_shape=jax.ShapeDtypeStruct((2, rh, cs), F32),
        compiler_params=_params("parallel"),
    )(idx, partial, recv)


def join_halves_comm(halves):
    nw = len(halves)

    def emit(_, outs, sems):
        send_sems, recv_sems = sems
        x, y, c = _mesh_pos()

        def copy(w, half):
            return pltpu.make_async_remote_copy(
                src_ref=outs[w].at[half], dst_ref=outs[w].at[half], send_sem=send_sems.at[w], recv_sem=recv_sems.at[w],
                device_id=(x, y, 1 - c), device_id_type=MESH_T)

        def start():
            for w in range(nw):
                copy(w, c).start()

        def end():
            for w in range(nw):
                copy(w, c).wait_send()
                copy(w, 1 - c).wait_recv()

        return start, lambda: None, end

    return Comm(list(halves), [jax.ShapeDtypeStruct(h.shape, F32) for h in halves], {i: i for i in range(nw)},
                [pltpu.SemaphoreType.DMA((nw,))] * 2, emit)


MOD_COLS = N_MOD * D_MODEL // N_SHARD
MOD_TILE = 512


def mod_fwd(c16, w_mod):
    def body(c_ref, w_ref, s_ref, o_ref):
        cv = c_ref[...]
        s = cv * _sigmoid(cv)
        s_ref[...] = s
        o_ref[...] = jnp.dot(s.astype(BF16), w_ref[...].astype(BF16), preferred_element_type=F32)

    return pl.pallas_call(
        body, name="mod_fwd", grid=(MOD_COLS // MOD_TILE,),
        in_specs=[_full((16, D_MODEL)), pl.BlockSpec((D_MODEL, MOD_TILE), lambda j: (0, j))],
        out_specs=[_full((16, D_MODEL)), pl.BlockSpec((16, MOD_TILE), lambda j: (0, j))],
        out_shape=[jax.ShapeDtypeStruct((16, D_MODEL), F32), jax.ShapeDtypeStruct((16, MOD_COLS), F32)],
        compiler_params=_params("arbitrary"),
    )(c16, w_mod)


def mod_bwd(s16, dm16, w_mod):
    hi = lax.Precision.HIGHEST

    def body(s_ref, d_ref, w_ref, gw_ref, ds_ref):
        j = pl.program_id(0)
        dm = d_ref[...]
        gw_ref[...] = lax.dot_general(s_ref[...], dm, (((0,), (0,)), ((), ())), preferred_element_type=F32, precision=hi)
        part = lax.dot_general(dm, w_ref[...], (((1,), (1,)), ((), ())), preferred_element_type=F32, precision=hi)

        @pl.when(j == 0)
        def _():
            ds_ref[...] = part

        @pl.when(j > 0)
        def _():
            ds_ref[...] = ds_ref[...] + part

    return pl.pallas_call(
        body, name="mod_bwd", grid=(MOD_COLS // MOD_TILE,),
        in_specs=[_full((16, D_MODEL)), pl.BlockSpec((16, MOD_TILE), lambda j: (0, j)),
                  pl.BlockSpec((D_MODEL, MOD_TILE), lambda j: (0, j))],
        out_specs=[pl.BlockSpec((D_MODEL, MOD_TILE), lambda j: (0, j)), _full((16, D_MODEL))],
        out_shape=[jax.ShapeDtypeStruct((D_MODEL, MOD_COLS), F32), jax.ShapeDtypeStruct((16, D_MODEL), F32)],
        compiler_params=_params("arbitrary"),
    )(s16, dm16, w_mod)


def cctx_grad(parts, c_ctx):
    def body(p_ref, c_ref, o_ref):
        ds = p_ref[0:1, :]
        for s in range(1, N_SHARD):
            ds = ds + p_ref[16 * s:16 * s + 1, :]
        cv = c_ref[...]
        sg = _sigmoid(cv)
        o_ref[...] = ds * (sg * (1.0 + cv * (1.0 - sg)))

    return pl.pallas_call(
        body, name="cctx_grad", in_specs=[_full((N_DEV * 8, D_MODEL)), _full((1, D_MODEL))],
        out_specs=_full((1, D_MODEL)), out_shape=jax.ShapeDtypeStruct((1, D_MODEL), F32),
    )(parts, c_ctx)


def add_rows(a, b, name):
    def body(a_ref, b_ref, o_ref):
        o_ref[...] = a_ref[...] + b_ref[...]

    return pl.pallas_call(body, name=name, in_specs=[_full(a.shape), _full(b.shape)], out_specs=_full(a.shape),
                          out_shape=jax.ShapeDtypeStruct(a.shape, F32))(a, b)


def _adamw_update(w_ref, g_ref, m_ref, v_ref, d_ref, nm_ref, nv_ref):
    g_ = g_ref[...]
    m_ = ADAM_B1 * m_ref[...] + (1.0 - ADAM_B1) * g_
    v_ = ADAM_B2 * v_ref[...] + (1.0 - ADAM_B2) * (g_ * g_)
    m_hat = m_ / (1.0 - ADAM_B1 ** ADAM_STEP)
    v_hat = v_ / (1.0 - ADAM_B2 ** ADAM_STEP)
    d_ref[...] = -ADAM_LR * (m_hat / (jnp.sqrt(v_hat) + ADAM_EPS) + ADAM_WD * w_ref[...])
    nm_ref[...] = m_
    nv_ref[...] = v_


def adamw_many(ws, gs, ms, vs):
    n = len(ws)

    def body(*refs):
        for i in range(n):
            _adamw_update(*[refs[k * n + i] for k in range(7)])

    shapes = [jax.ShapeDtypeStruct(w.shape, F32) for w in ws]
    return pl.pallas_call(body, name="adamw_small", out_shape=shapes * 3,
                          compiler_params=pltpu.CompilerParams(vmem_limit_bytes=VMEM_LIMIT_V7X))(*ws, *gs, *ms, *vs)


def adamw(w, g, m, v, name, comm=None):
    r, c = w.shape
    tr = 128 if (r % 128 == 0 and r > 128) else r

    def body(w_ref, g_ref, m_ref, v_ref, d_ref, nm_ref, nv_ref):
        _adamw_update(w_ref, g_ref, m_ref, v_ref, d_ref, nm_ref, nv_ref)

    spec = pl.BlockSpec((tr, c), lambda i: (i, 0))
    shp = jax.ShapeDtypeStruct((r, c), F32)
    res, extra = _call(body, name=name, grid=(r // tr,), in_specs=[spec] * 4, out_specs=[spec] * 3,
                       out_shape=[shp] * 3, sem=("parallel",), args=(w, g, m, v), comm=comm)
    return (*res, extra)


LANES = 1024


def _pack(arrs):
    rows, spans, at = [], [], 0
    for a in arrs:
        n = int(np.prod(a.shape))
        nr = 8 * -(-n // (8 * LANES))
        flat = a.reshape(-1)
        if nr * LANES != n:
            flat = jnp.concatenate([flat, jnp.zeros((nr * LANES - n,), F32)])
        rows.append(flat.reshape(nr, LANES))
        spans.append((at, nr, n, a.shape))
        at += nr
    return jnp.concatenate(rows, axis=0), spans


def _unpack(buf, spans):
    out = []
    for at, nr, n, shape in spans:
        out.append(buf[at:at + nr].reshape(-1)[:n].reshape(shape))
    return out


SMALL_SHARD = ("lru_conv_w", "lru_ba", "lru_bx", "lru_lambda", "ffn_conv_w")


def kernel(x, c, ctx, c_ctx, w_mod, b_mod, norm_mix_g, norm_ffn_g, w_in, lru_conv_w, lru_conv_b, lru_wa, lru_ba, lru_wx, lru_bx, lru_lambda, q_norm_g, k_norm_g, na_rpb, w_rnn_out, w_na_out, w_out, w_up, ffn_conv_w, ffn_conv_b, w_down, loss_target, m_c_ctx, m_w_mod, m_b_mod, m_norm_mix_g, m_norm_ffn_g, m_w_in, m_lru_conv_w, m_lru_conv_b, m_lru_wa, m_lru_ba, m_lru_wx, m_lru_bx, m_lru_lambda, m_q_norm_g, m_k_norm_g, m_na_rpb, m_w_rnn_out, m_w_na_out, m_w_out, m_w_up, m_ffn_conv_w, m_ffn_conv_b, m_w_down, v_c_ctx, v_w_mod, v_b_mod, v_norm_mix_g, v_norm_ffn_g, v_w_in, v_lru_conv_w, v_lru_conv_b, v_lru_wa, v_lru_ba, v_lru_wx, v_lru_bx, v_lru_lambda, v_q_norm_g, v_k_norm_g, v_na_rpb, v_w_rnn_out, v_w_na_out, v_w_out, v_w_up, v_ffn_conv_w, v_ffn_conv_b, v_w_down):
    weights = dict(c_ctx=c_ctx, w_mod=w_mod, b_mod=b_mod, norm_mix_g=norm_mix_g, norm_ffn_g=norm_ffn_g, w_in=w_in,
                   lru_conv_w=lru_conv_w, lru_conv_b=lru_conv_b, lru_wa=lru_wa, lru_ba=lru_ba, lru_wx=lru_wx,
                   lru_bx=lru_bx, lru_lambda=lru_lambda, q_norm_g=q_norm_g, k_norm_g=k_norm_g, na_rpb=na_rpb,
                   w_rnn_out=w_rnn_out, w_na_out=w_na_out, w_out=w_out, w_up=w_up, ffn_conv_w=ffn_conv_w,
                   ffn_conv_b=ffn_conv_b, w_down=w_down)
    mom1 = dict(c_ctx=m_c_ctx, w_mod=m_w_mod, b_mod=m_b_mod, norm_mix_g=m_norm_mix_g, norm_ffn_g=m_norm_ffn_g,
                w_in=m_w_in, lru_conv_w=m_lru_conv_w, lru_conv_b=m_lru_conv_b, lru_wa=m_lru_wa, lru_ba=m_lru_ba,
                lru_wx=m_lru_wx, lru_bx=m_lru_bx, lru_lambda=m_lru_lambda, q_norm_g=m_q_norm_g, k_norm_g=m_k_norm_g,
                na_rpb=m_na_rpb, w_rnn_out=m_w_rnn_out, w_na_out=m_w_na_out, w_out=m_w_out, w_up=m_w_up,
                ffn_conv_w=m_ffn_conv_w, ffn_conv_b=m_ffn_conv_b, w_down=m_w_down)
    mom2 = dict(c_ctx=v_c_ctx, w_mod=v_w_mod, b_mod=v_b_mod, norm_mix_g=v_norm_mix_g, norm_ffn_g=v_norm_ffn_g,
                w_in=v_w_in, lru_conv_w=v_lru_conv_w, lru_conv_b=v_lru_conv_b, lru_wa=v_lru_wa, lru_ba=v_lru_ba,
                lru_wx=v_lru_wx, lru_bx=v_lru_bx, lru_lambda=v_lru_lambda, q_norm_g=v_q_norm_g, k_norm_g=v_k_norm_g,
                na_rpb=v_na_rpb, w_rnn_out=v_w_rnn_out, w_na_out=v_w_na_out, w_out=v_w_out, w_up=v_w_up,
                ffn_conv_w=v_ffn_conv_w, ffn_conv_b=v_ffn_conv_b, w_down=v_w_down)
    order = list(weights)
    d = D_MODEL
    mx_, my_, mc_ = _mesh_pos()
    shard = 2 * mx_ + my_
    dev = 2 * shard + mc_

    idx = jnp.stack([shard, mc_]).astype(jnp.int32)
    wsh = {name: cast_into_full(weights[name][0], full, axis, idx, "cast_" + name) for name, full, axis in BIG}
    local_small, small_spans = _pack([c] + [weights[k][0] for k in SMALL_SHARD])
    bt, (w_in_full, gath) = bias_table(na_rpb[0], comm=join_comms(gather_weights_comm([wsh["w_in"]], [0]),
                                                                  all_gather_comm(local_small)))
    per_dev = [_unpack(gath[k], small_spans) for k in range(N_DEV)]
    c_all = jnp.concatenate([per_dev[k][0] for k in range(N_DEV)], axis=0)
    full_small = {name: jnp.concatenate([per_dev[2 * s][1 + i] for s in range(N_SHARD)], axis=-1)
                  for i, name in enumerate(SMALL_SHARD)}
    c16 = jnp.concatenate([c_all, c_ctx.reshape(1, d), jnp.zeros((7, d), F32)], axis=0)
    s16, mod_part = mod_fwd(c16, w_mod[0])
    mod_all = all_gather8(mod_part, "gather_mod").reshape(N_DEV, 16, MOD_COLS)
    mod = jnp.concatenate([mod_all[2 * s] for s in range(N_SHARD)], axis=1) + b_mod
    modx = lax.dynamic_slice(mod, (dev, 0), (1, N_MOD * d))
    modc = mod[8:9]

    z = jnp.concatenate([ctx[0], x[0]], axis=0)
    res = local_step(z, loss_target[0], modx, modc, norm_mix_g, norm_ffn_g, w_in_full, full_small["lru_conv_w"],
                     lru_conv_b, lru_wa[0], full_small["lru_ba"], lru_wx[0], full_small["lru_bx"],
                     full_small["lru_lambda"], q_norm_g, k_norm_g, na_rpb[0], wsh["w_rnn_out"], wsh["w_na_out"],
                     wsh["w_out"], wsh["w_up"], full_small["ffn_conv_w"], ffn_conv_b, wsh["w_down"], idx=idx, bt=bt)

    halves = [add_pieces(res["partials"][i], res["pieces"][i], idx, BIG[i][2], "add_pieces_" + BIG[i][0])
              for i in range(len(BIG))]
    lru_tot = {k: sum_blocks(res["lru_w_all"][k], "sum_" + k).reshape(weights[k].shape[1:])
               for k in ("lru_wa", "lru_wx")}
    small_names = ["norm_mix_g", "norm_ffn_g", "lru_conv_w", "lru_conv_b", "lru_ba", "lru_bx",
                   "lru_lambda", "q_norm_g", "k_norm_g", "na_rpb", "ffn_conv_w", "ffn_conv_b"]
    local_g, g_spans = _pack([res["loss_sq"][0:1, 0:1], res["d_modx"], res["d_modc"]] + [res[k] for k in small_names])
    n_rows = local_g.shape[0]
    *joined, g_all = run_comm(join_comms(join_halves_comm(halves), all_gather_comm(local_g)), "tail_exchange")
    grads = {name: joined[i].reshape(_shard_shape(full, axis)) for i, (name, full, axis) in enumerate(BIG)}
    grads.update(lru_tot)
    g_tot = sum_blocks(g_all, "sum_small")
    tot = _unpack(g_tot, g_spans)
    loss = (0.5 / d) * tot[0][0, 0]
    small_tot = dict(zip(small_names, tot[3:]))
    at_x = g_spans[1][0]
    dmx_rows = g_all.reshape(N_DEV, n_rows, LANES)[:, at_x:at_x + N_MOD, :].reshape(N_DEV, N_MOD * d)
    dmc_row = jnp.concatenate([tot[2], jnp.zeros((1, 4 * d), F32)], axis=1)
    dm16 = jnp.concatenate([dmx_rows, dmc_row, jnp.zeros((7, N_MOD * d), F32)], axis=0)
    grads["b_mod"] = add_rows(tot[1], dmc_row, "b_mod_grad")
    g_w_mod, ds16 = mod_bwd(s16, lax.dynamic_slice(dm16, (0, shard * MOD_COLS), (16, MOD_COLS)), w_mod[0])
    grads["w_mod"] = g_w_mod
    for k in small_names:
        g = small_tot[k]
        if k in SMALL_SHARD:
            w_sh = weights[k].shape[-1]
            g = lax.dynamic_slice_in_dim(g, shard * w_sh, w_sh, axis=g.ndim - 1)
        grads[k] = g

    delta, new_m, new_v = {}, {}, {}
    for name, _, _ in BIG + (("w_mod", None, None),):
        *upd, got = adamw(weights[name][0], grads[name], mom1[name][0], mom2[name][0], "adamw_" + name,
                          comm=all_gather_comm(ds16[8:16]) if name == "w_in" else None)
        delta[name], new_m[name], new_v[name] = upd
        if name == "w_in":
            grads["c_ctx"] = cctx_grad(got[0].reshape(N_DEV * 8, d), c_ctx.reshape(1, d))
    rest = [k for k in order if k not in delta]
    views = {k: (grads[k].shape if grads[k].ndim <= 3 else (-1, grads[k].shape[-1])) for k in rest}
    small = adamw_many(*[[t[k].reshape(views[k]) for k in rest] for t in (weights, grads, mom1, mom2)])
    n_rest = len(rest)
    for i, k in enumerate(rest):
        delta[k], new_m[k], new_v[k] = small[i], small[n_rest + i], small[2 * n_rest + i]

    shaped = lambda t: [t[k].reshape(weights[k].shape) for k in order]
    return (loss, res["grad_x"][None], *shaped(grads), *shaped(delta), *shaped(new_m), *shaped(new_v))
```

```python
import numpy as np
import jax
import jax.numpy as jnp
from jax import lax
from jax.experimental import pallas as pl
from jax.experimental.pallas import tpu as pltpu

F32 = jnp.float32
BF16 = jnp.bfloat16

D_MODEL = 1024
SEQ = 2048
CTX_LEN = 256
ZLEN = SEQ + CTX_LEN
GRID_W = 64
GRID_ROWS = SEQ // GRID_W
LRU_BLOCK_W = 128
LRU_BLOCKS = 8
LRU_C = 8.0
NA_HEADS = 16
HEAD_DIM = 64
NA_ROWS = 8
NA_COLS = 16
ROPE_BASE = 10000.0
D_FF = 2816
N_MOD = 6
IN_COLS = 7 * D_MODEL
EPS = 1e-6
NEG_INF = -1e30
N_DEV = 8
N_SHARD = 4

ADAM_LR = 0.001
ADAM_B1 = 0.9
ADAM_B2 = 0.999
ADAM_EPS = 1e-08
ADAM_WD = 0.01
ADAM_STEP = 10

ROW_TILE = 256
Q_ROWS = 4
Q_TILE = Q_ROWS * GRID_W
KEY_ROWS = 12
KEY_TILE = KEY_ROWS * GRID_W
BT_PAD = 4
BT_LEN = 24
VMEM_LIMIT_V7X = 56 * 1024 * 1024

MESH_T = pl.DeviceIdType.MESH


def _params(*sem):
    return pltpu.CompilerParams(dimension_semantics=sem if sem else None, vmem_limit_bytes=VMEM_LIMIT_V7X)


def _full(shape):
    nd = len(shape)
    return pl.BlockSpec(shape, lambda *_: (0,) * nd)


class Comm:
    def __init__(self, inputs, out_shapes, aliases, scratch, emit):
        self.inputs, self.out_shapes, self.aliases, self.scratch, self.emit = inputs, out_shapes, aliases, scratch, emit


def _call(body, *, name, grid, in_specs, out_specs, out_shape, args, scratch_shapes=(), sem=(), comm=None):
    n_in, n_out, n_sc = len(in_specs), len(out_specs), len(scratch_shapes)
    if comm is None:
        res = pl.pallas_call(body, name=name, grid=grid, in_specs=list(in_specs), out_specs=list(out_specs),
                             out_shape=list(out_shape), scratch_shapes=list(scratch_shapes),
                             compiler_params=_params(*sem))(*args)
        return list(res), []
    k_in, k_out = len(comm.inputs), len(comm.out_shapes)
    steps = int(np.prod(grid))

    def hosted(*refs):
        ins, cins = refs[:n_in], refs[n_in:n_in + k_in]
        at = n_in + k_in
        outs, couts = refs[at:at + n_out], refs[at + n_out:at + n_out + k_out]
        at += n_out + k_out
        scr, cscr = refs[at:at + n_sc], refs[at + n_sc:]
        start, mid, end = comm.emit(cins, couts, cscr)
        lin = pl.program_id(0)
        for ax in range(1, len(grid)):
            lin = lin * grid[ax] + pl.program_id(ax)
        pl.when(lin == 0)(start)
        body(*ins, *outs, *scr)
        pl.when(lin == steps - 1 - steps // 7)(mid)
        pl.when(lin == steps - 1)(end)

    hbm = pl.BlockSpec(memory_space=pl.ANY)
    res = pl.pallas_call(
        hosted, name=name, grid=grid, in_specs=list(in_specs) + [hbm] * k_in, out_specs=list(out_specs) + [hbm] * k_out,
        out_shape=list(out_shape) + list(comm.out_shapes), scratch_shapes=list(scratch_shapes) + list(comm.scratch),
        input_output_aliases={n_in + i: n_out + o for i, o in comm.aliases.items()},
        compiler_params=_params(*(("arbitrary",) * len(grid))))(*args, *comm.inputs)
    return list(res[:n_out]), list(res[n_out:])


def _sigmoid(x):
    return 0.5 * jnp.tanh(0.5 * x) + 0.5


def _gelu_parts(x):
    c0 = 0.7978845608028654
    inner = c0 * (x + 0.044715 * x * x * x)
    t = jnp.tanh(inner)
    g = 0.5 * x * (1.0 + t)
    dg = 0.5 * (1.0 + t) + 0.5 * x * (1.0 - t * t) * c0 * (1.0 + 3.0 * 0.044715 * x * x)
    return g, dg


def _dot_nt(a, b):
    return lax.dot_general(a, b, (((1,), (1,)), ((), ())), preferred_element_type=F32)


def _dot_tn(a, b):
    return lax.dot_general(a, b, (((0,), (0,)), ((), ())), preferred_element_type=F32)


def norm_mod(xin, gain, shift, scale, name):
    r, d = xin.shape
    s_mod = shift.shape[0]
    assert r % ROW_TILE == 0

    def body(x_ref, g_ref, sh_ref, sc_ref, xn_ref):
        x = x_ref[...]
        nrm = x * lax.rsqrt(jnp.mean(x * x, axis=-1, keepdims=True) + EPS)
        xn_ref[...] = ((nrm * g_ref[...]) * (1.0 + sc_ref[0]) + sh_ref[0]).astype(BF16)

    mod_spec = pl.BlockSpec((1, 1, d), lambda i: (jnp.minimum(i, s_mod - 1), 0, 0))
    return pl.pallas_call(
        body, name=name, grid=(r // ROW_TILE,),
        in_specs=[pl.BlockSpec((ROW_TILE, d), lambda i: (i, 0)), _full((1, d)), mod_spec, mod_spec],
        out_specs=pl.BlockSpec((ROW_TILE, d), lambda i: (i, 0)),
        out_shape=jax.ShapeDtypeStruct((r, d), BF16),
        compiler_params=_params("parallel"),
    )(xin, gain, shift, scale)


def matmul_wide(a, b, name, tm, tn, comm=None):
    m, k = a.shape
    n = b.shape[1]
    assert m % tm == 0 and n % tn == 0

    def body(a_ref, b_ref, o_ref):
        o_ref[...] = jnp.dot(a_ref[...], b_ref[...], preferred_element_type=F32)

    res, extra = _call(
        body, name=name, grid=(n // tn, m // tm),
        in_specs=[pl.BlockSpec((tm, k), lambda j, i: (i, 0)), pl.BlockSpec((k, tn), lambda j, i: (0, j))],
        out_specs=[pl.BlockSpec((tm, tn), lambda j, i: (i, j))],
        out_shape=[jax.ShapeDtypeStruct((m, n), F32)],
        sem=("parallel", "parallel"), args=(a, b), comm=comm)
    return res[0], extra


def _row_ids(n, w):
    return lax.broadcasted_iota(jnp.int32, (n, w), 0)


def _lru_conv(xr, cw, cb):
    row = _row_ids(ZLEN, LRU_BLOCK_W)
    segpos = jnp.where(row < CTX_LEN, row, row - CTX_LEN)
    seglen = jnp.where(row < CTX_LEN, CTX_LEN, SEQ)
    acc = xr * cw[2:3, :] + cb
    for k in (0, 1, 3):
        off = k - 2
        sh = pltpu.roll(xr, (-off) % ZLEN, 0)
        ok = (segpos + off >= 0) & (segpos + off < seglen)
        acc = acc + jnp.where(ok, sh, 0.0) * cw[k:k + 1, :]
    return acc


def _lru_conv_t(dxc, cw):
    row = _row_ids(ZLEN, LRU_BLOCK_W)
    segpos = jnp.where(row < CTX_LEN, row, row - CTX_LEN)
    seglen = jnp.where(row < CTX_LEN, CTX_LEN, SEQ)
    acc = dxc * cw[2:3, :]
    for k in (0, 1, 3):
        off = k - 2
        sh = pltpu.roll(dxc, off % ZLEN, 0)
        ok = (segpos - off >= 0) & (segpos - off < seglen)
        acc = acc + jnp.where(ok, sh, 0.0) * cw[k:k + 1, :]
    return acc


def _lru_gates(xc, xcb, wa, ba, wx, bx, lam):
    r = _sigmoid(jnp.dot(xcb, wa, preferred_element_type=F32) + ba)
    i = _sigmoid(jnp.dot(xcb, wx, preferred_element_type=F32) + bx)
    sp = jnp.maximum(-lam, 0.0) + jnp.log1p(jnp.exp(-jnp.abs(lam)))
    la = (-LRU_C) * r * sp
    a = jnp.exp(la)
    sq = jnp.sqrt(-jnp.tanh(la) * (1.0 + a * a))
    b = sq * i * xc
    return r, i, sp, a, sq, b


def _scan8_fwd(a, b, rid):
    for s in (1, 2, 4):
        a_s = pltpu.roll(a, s, 0)
        b_s = pltpu.roll(b, s, 0)
        m = rid >= s
        b = jnp.where(m, a * b_s + b, b)
        a = jnp.where(m, a * a_s, a)
    return a, b


def _scan8_rev(a, b, rid):
    for s in (1, 2, 4):
        a_s = pltpu.roll(a, 8 - s, 0)
        b_s = pltpu.roll(b, 8 - s, 0)
        m = rid < 8 - s
        b = jnp.where(m, a * b_s + b, b)
        a = jnp.where(m, a * a_s, a)
    return a, b


N_CHUNK = ZLEN // 8
CTX_CHUNKS = CTX_LEN // 8
SCAN_UNROLL = 8


def _scan_up(a_ref, b_ref, h_ref, lo, hi, carry):
    rid = _row_ids(8, LRU_BLOCK_W)
    assert (hi - lo) % SCAN_UNROLL == 0

    def step(g, c):
        base = pl.multiple_of((lo + g * SCAN_UNROLL) * 8, 8)
        for u in range(SCAN_UNROLL):
            sl = pl.ds(base + 8 * u, 8)
            a, b = _scan8_fwd(a_ref[sl, :], b_ref[sl, :], rid)
            h_ref[sl, :] = b + a * c
            c = b[7:8, :] + a[7:8, :] * c
        return c

    return lax.fori_loop(0, (hi - lo) // SCAN_UNROLL, step, carry)


def _scan_down(a_ref, b_ref, h_ref, lo, hi, carry):
    rid = _row_ids(8, LRU_BLOCK_W)
    assert (hi - lo) % SCAN_UNROLL == 0

    def step(g, c):
        base = pl.multiple_of((hi - (g + 1) * SCAN_UNROLL) * 8, 8)
        for u in reversed(range(SCAN_UNROLL)):
            sl = pl.ds(base + 8 * u, 8)
            a, b = _scan8_rev(a_ref[sl, :], b_ref[sl, :], rid)
            h_ref[sl, :] = b + a * c
            c = b[0:1, :] + a[0:1, :] * c
        return c

    return lax.fori_loop(0, (hi - lo) // SCAN_UNROLL, step, carry)


def _lru_scan_dir(d, a_ref, b_ref, h_ref):
    zero = jnp.zeros((1, LRU_BLOCK_W), F32)
    if d == 0:
        _scan_up(a_ref, b_ref, h_ref, 0, N_CHUNK, zero)
    else:
        c = _scan_down(a_ref, b_ref, h_ref, 0, CTX_CHUNKS, zero)
        _scan_down(a_ref, b_ref, h_ref, CTX_CHUNKS, N_CHUNK, c)


def _lru_in_specs():
    blk = lambda rows: pl.BlockSpec((rows, LRU_BLOCK_W), lambda b: (0, b))
    wspec = pl.BlockSpec((2, 1, LRU_BLOCK_W, LRU_BLOCK_W), lambda b: (0, b, 0, 0))
    return blk, wspec


def lru_fwd(p, conv_w, conv_b, wa, ba, wx, bx, lam, comm=None):
    blk, wspec = _lru_in_specs()

    def body(xr_ref, gx_ref, cw_ref, cb_ref, wa_ref, ba_ref, wx_ref, bx_ref, lam_ref, y_ref, a_s, b_s, h_s, hsum_s):
        xr = xr_ref[...]
        xc = _lru_conv(xr, cw_ref[...], cb_ref[...])
        xcb = xc.astype(BF16)
        for d in (0, 1):
            _, _, _, a, _, b = _lru_gates(xc, xcb, wa_ref[d, 0].astype(BF16), ba_ref[d:d + 1, :],
                                          wx_ref[d, 0].astype(BF16), bx_ref[d:d + 1, :], lam_ref[d:d + 1, :])
            a_s[...] = a
            b_s[...] = b
            _lru_scan_dir(d, a_s, b_s, h_s)
            if d == 0:
                hsum_s[...] = h_s[...]
            else:
                hsum_s[...] = hsum_s[...] + h_s[...]
        g, _ = _gelu_parts(gx_ref[CTX_LEN:, :])
        y_ref[...] = (hsum_s[CTX_LEN:, :] * g).astype(BF16)

    zs = pltpu.VMEM((ZLEN, LRU_BLOCK_W), F32)
    res, extra = _call(
        body, name="lru_fwd", grid=(LRU_BLOCKS,),
        in_specs=[blk(ZLEN), pl.BlockSpec((ZLEN, LRU_BLOCK_W), lambda b: (0, 24 + b)), blk(4), blk(1),
                  wspec, blk(2), wspec, blk(2), blk(2)],
        out_specs=[pl.BlockSpec((SEQ, LRU_BLOCK_W), lambda b: (0, b))],
        out_shape=[jax.ShapeDtypeStruct((SEQ, D_MODEL), BF16)],
        scratch_shapes=[zs, zs, zs, zs], sem=("arbitrary",),
        args=(p, p, conv_w, conv_b, wa, ba, wx, bx, lam), comm=comm)
    return res[0], extra


def _rope_tables():
    t = np.arange(SEQ)
    lane = np.arange(2 * HEAD_DIM)
    in_head = lane % HEAD_DIM
    j = (in_head % 32) % 16
    freq = ROPE_BASE ** (-j.astype(np.float64) / 16.0)
    pos = np.where(in_head[None, :] < 32, (t // GRID_W)[:, None], (t % GRID_W)[:, None]).astype(np.float64)
    ang = (pos.astype(np.float32) * freq.astype(np.float32)[None, :]).astype(np.float32)
    cos = np.cos(ang).astype(np.float32)
    sin = np.sin(ang).astype(np.float32)
    sgn = np.where((in_head % 32) < 16, -1.0, 1.0).astype(np.float32)
    cos = np.concatenate([np.ones((CTX_LEN, 2 * HEAD_DIM), np.float32), cos], 0)
    sin = np.concatenate([np.zeros((CTX_LEN, 2 * HEAD_DIM), np.float32), sin * sgn[None, :]], 0)
    return jnp.asarray(cos), jnp.asarray(sin)


def _head_ones():
    lane = np.arange(2 * HEAD_DIM)
    return jnp.asarray((lane[:, None] // HEAD_DIM == lane[None, :] // HEAD_DIM).astype(np.float32))


def _rope_partner(x):
    lane = lax.broadcasted_iota(jnp.int32, x.shape, 1)
    return jnp.where((lane % 32) < 16, pltpu.roll(x, 128 - 16, 1), pltpu.roll(x, 16, 1))


def _head_rms(x, ones, gain):
    ms = jnp.dot(x * x, ones, preferred_element_type=F32, precision=lax.Precision.HIGHEST) * (1.0 / HEAD_DIM)
    rstd = lax.rsqrt(ms + EPS)
    return x * rstd * gain, rstd


PREP_TILE = 768


def qkv_prep(p, qg2, kg2, cos, sin, ones, comm=None):
    scale = HEAD_DIM ** -0.5

    def body(q_ref, k_ref, v_ref, qg_ref, kg_ref, cos_ref, sin_ref, ones_ref, qr_ref, qp_ref, kk_ref, vv_ref):
        ones_m = ones_ref[...]
        c, s = cos_ref[...], sin_ref[...]
        qn, _ = _head_rms(q_ref[...], ones_m, qg_ref[...])
        qn = qn * scale
        qr_ref[...] = (qn * c + _rope_partner(qn) * s).astype(BF16)
        qp_ref[...] = qn.astype(BF16)
        kn, _ = _head_rms(k_ref[...], ones_m, kg_ref[...])
        kk_ref[...] = (kn * c + _rope_partner(kn) * s).astype(BF16)
        vv_ref[...] = v_ref[...].astype(BF16)

    col = lambda base: pl.BlockSpec((PREP_TILE, 128), lambda hp, i: (i, base + hp))
    small = pl.BlockSpec((1, 128), lambda hp, i: (0, 0))
    tab = pl.BlockSpec((PREP_TILE, 128), lambda hp, i: (i, 0))
    oshape = jax.ShapeDtypeStruct((ZLEN, D_MODEL), BF16)
    res, extra = _call(
        body, name="qkv_prep", grid=(NA_HEADS // 2, ZLEN // PREP_TILE),
        in_specs=[col(32), col(8), col(16), small, small, tab, tab, _full((128, 128))],
        out_specs=[col(0)] * 4, out_shape=[oshape] * 4, sem=("parallel", "parallel"),
        args=(p, p, p, qg2, kg2, cos, sin, ones), comm=comm)
    return (*res, extra)


def _bias_expand():
    qc = np.arange(GRID_W)[:, None]
    kc = np.arange(GRID_W)[None, :]
    col_start = np.clip(qc - NA_COLS // 2, 0, GRID_W - NA_COLS)
    in_win = (kc >= col_start) & (kc < col_start + NA_COLS)
    dc = np.clip(kc - qc, -(NA_COLS - 1), NA_COLS - 1) + (NA_COLS - 1)
    e = np.zeros((2 * NA_COLS - 1, GRID_W, GRID_W), np.float32)
    for d in range(2 * NA_COLS - 1):
        e[d] = ((dc == d) & in_win).astype(np.float32)
    pen = np.where(in_win, 0.0, NEG_INF).astype(np.float32)
    return e, pen


def bias_table(rpb2, comm=None):
    e, pen = _bias_expand()
    n_dr = 2 * NA_ROWS - 1
    ea = np.zeros((31, GRID_W, 128), np.float32)
    ea[:, :, :GRID_W] = e
    eb = np.zeros((31, GRID_W, 128), np.float32)
    eb[:, :, GRID_W:] = e
    pen2 = np.concatenate([pen, pen], 1)
    ea = jnp.asarray(ea.reshape(31, GRID_W * 128))
    eb = jnp.asarray(eb.reshape(31, GRID_W * 128))
    sel_a = np.zeros((BT_LEN, n_dr), np.float32)
    sel_b = np.zeros((BT_LEN, n_dr), np.float32)
    for r in range(BT_LEN):
        dr = r - BT_PAD
        if 0 <= dr < n_dr:
            sel_a[r, dr] = 1.0
        if 0 <= dr + 1 < n_dr:
            sel_b[r, dr + 1] = 1.0
    sel_a, sel_b = jnp.asarray(sel_a), jnp.asarray(sel_b)
    pen2 = jnp.asarray(pen2.reshape(1, GRID_W * 128))
    hi = lax.Precision.HIGHEST

    def body(rpb_ref, sa_ref, sb_ref, ea_ref, eb_ref, pen_ref, o_ref, ra_s, rb_s):
        for h in range(NA_HEADS):
            rp = rpb_ref[h]
            ra_s[h * BT_LEN:(h + 1) * BT_LEN, :] = jnp.dot(sa_ref[...], rp, preferred_element_type=F32, precision=hi)
            rb_s[h * BT_LEN:(h + 1) * BT_LEN, :] = jnp.dot(sb_ref[...], rp, preferred_element_type=F32, precision=hi)
        o_ref[...] = (jnp.dot(ra_s[...], ea_ref[...], preferred_element_type=F32, precision=hi)
                      + jnp.dot(rb_s[...], eb_ref[...], preferred_element_type=F32, precision=hi) + pen_ref[...])

    tcol = 2048
    rows = NA_HEADS * BT_LEN
    res, extra = _call(
        body, name="bias_table", grid=(GRID_W * 128 // tcol,),
        in_specs=[_full((NA_HEADS, n_dr, 31)), _full((BT_LEN, n_dr)), _full((BT_LEN, n_dr)),
                  pl.BlockSpec((31, tcol), lambda j: (0, j)), pl.BlockSpec((31, tcol), lambda j: (0, j)),
                  pl.BlockSpec((1, tcol), lambda j: (0, j))],
        out_specs=[pl.BlockSpec((rows, tcol), lambda j: (0, j))],
        out_shape=[jax.ShapeDtypeStruct((rows, GRID_W * 128), F32)],
        scratch_shapes=[pltpu.VMEM((rows, 31), F32), pltpu.VMEM((rows, 31), F32)], sem=("parallel",),
        args=(rpb2, sel_a, sel_b, ea, eb, pen2), comm=comm)
    return res[0].reshape(NA_HEADS, BT_LEN, GRID_W, 128), extra


def _key_window(j):
    ws = jnp.clip(Q_ROWS * j - 4, 0, GRID_ROWS - KEY_ROWS)
    return ws, pl.multiple_of(CTX_LEN + ws * GRID_W, 256)


def _head_mask(hh):
    lane = lax.broadcasted_iota(jnp.int32, (Q_TILE, 128), 1)
    return (lane < HEAD_DIM) if hh == 0 else (lane >= HEAD_DIM)


def _attn_scores(j, ws, q_rot_h, q_pl_h, kw, kc, hh, bt_ref, s_ref):
    s_ref[:, :KEY_TILE] = _dot_nt(q_rot_h, kw)
    s_ref[:, KEY_TILE:] = _dot_nt(q_pl_h, kc)
    lane = lax.broadcasted_iota(jnp.int32, (GRID_W, 128), 1)
    base = ws - Q_ROWS * j + (NA_ROWS - 1) + BT_PAD
    for qi in range(Q_ROWS):
        rs = jnp.clip(Q_ROWS * j + qi - NA_ROWS // 2, 0, GRID_ROWS - NA_ROWS)
        for m in range(KEY_ROWS // 2):
            k0 = ws + 2 * m
            p0 = jnp.where((k0 >= rs) & (k0 < rs + NA_ROWS), 0.0, NEG_INF)
            p1 = jnp.where((k0 + 1 >= rs) & (k0 + 1 < rs + NA_ROWS), 0.0, NEG_INF)
            pen = jnp.where(lane < GRID_W, p0, p1)
            rows = slice(qi * GRID_W, (qi + 1) * GRID_W)
            cols = slice(128 * m, 128 * (m + 1))
            s_ref[rows, cols] = s_ref[rows, cols] + bt_ref[hh, base + 2 * m - qi] + pen
    return base


def attn_fwd(q_rot, q_pl, kk, vv, bt, comm=None):
    def body(qr_ref, qp_ref, kk_ref, vv_ref, bt_ref, o_ref, lse_ref, s_ref):
        j = pl.program_id(1)
        ws, start = _key_window(j)
        win = pl.ds(start, KEY_TILE)
        kw, kc = kk_ref[win, :], kk_ref[:CTX_LEN, :]
        vw, vc = vv_ref[win, :], vv_ref[:CTX_LEN, :]
        qr, qp = qr_ref[...], qp_ref[...]
        outs = []
        for hh in range(2):
            msk = _head_mask(hh)
            _attn_scores(j, ws, jnp.where(msk, qr, 0), jnp.where(msk, qp, 0), kw, kc, hh, bt_ref, s_ref)
            s = s_ref[...]
            mx = jnp.max(s, axis=-1, keepdims=True)
            pr = jnp.exp(s - mx)
            l = jnp.sum(pr, axis=-1, keepdims=True)
            prb = pr.astype(BF16)
            o = jnp.dot(prb[:, :KEY_TILE], vw, preferred_element_type=F32)
            o = o + jnp.dot(prb[:, KEY_TILE:], vc, preferred_element_type=F32)
            outs.append(o / l)
            lse_ref[hh] = mx + jnp.log(l)
        o_ref[...] = jnp.where(_head_mask(0), outs[0], outs[1])

    qspec = pl.BlockSpec((Q_TILE, 128), lambda hp, j: (j + 1, hp))
    kspec = pl.BlockSpec((ZLEN, 128), lambda hp, j: (0, hp))
    res, extra = _call(
        body, name="attn_fwd", grid=(NA_HEADS // 2, SEQ // Q_TILE),
        in_specs=[qspec, qspec, kspec, kspec, pl.BlockSpec((2, BT_LEN, GRID_W, 128), lambda hp, j: (hp, 0, 0, 0))],
        out_specs=[pl.BlockSpec((Q_TILE, 128), lambda hp, j: (j, hp)),
                   pl.BlockSpec((2, Q_TILE, 1), lambda hp, j: (hp, j, 0))],
        out_shape=[jax.ShapeDtypeStruct((SEQ, D_MODEL), F32), jax.ShapeDtypeStruct((NA_HEADS, SEQ, 1), F32)],
        scratch_shapes=[pltpu.VMEM((Q_TILE, KEY_TILE + CTX_LEN), F32)], sem=("parallel", "arbitrary"),
        args=(q_rot, q_pl, kk, vv, bt), comm=comm)
    return res[0], res[1], extra


def merge_fwd(y_rnn, y_na, p, z, g2, w_rnn, w_na, w_out):
    def body(yr_ref, yn_ref, mr_ref, mn_ref, x_ref, g2_ref, wr_ref, wn_ref, wo_ref, u_ref, v_ref, mg_ref, out_ref, x1_ref):
        u = jnp.dot(yr_ref[...], wr_ref[...], preferred_element_type=F32)
        v = jnp.dot(yn_ref[...].astype(BF16), wn_ref[...], preferred_element_type=F32)
        merged = (_sigmoid(mr_ref[...]) * u + _sigmoid(mn_ref[...]) * v).astype(BF16)
        out = jnp.dot(merged, wo_ref[...], preferred_element_type=F32)
        u_ref[...] = u
        v_ref[...] = v
        mg_ref[...] = merged
        out_ref[...] = out
        x1_ref[...] = x_ref[...] + g2_ref[...] * out

    row = pl.BlockSpec((ROW_TILE, D_MODEL), lambda i: (i, 0))
    lat = lambda cb: pl.BlockSpec((ROW_TILE, D_MODEL), lambda i: (i + 1, cb))
    wspec = _full((D_MODEL, D_MODEL))
    f32o = jax.ShapeDtypeStruct((SEQ, D_MODEL), F32)
    return pl.pallas_call(
        body, name="merge_fwd", grid=(SEQ // ROW_TILE,),
        in_specs=[row, row, lat(5), lat(6), lat(0), _full((1, D_MODEL)), wspec, wspec, wspec],
        out_specs=[row] * 5,
        out_shape=[f32o, f32o, jax.ShapeDtypeStruct((SEQ, D_MODEL), BF16), f32o, f32o],
        compiler_params=_params("parallel"),
    )(y_rnn, y_na, p, p, z, g2, w_rnn, w_na, w_out)


FF_TILE = 256
FF_TILES = D_FF // FF_TILE


FF_ROWS = 64
FF_HALO = 8
FF_SLAB = FF_ROWS + 2 * FF_HALO


def _ffn_row_chunks(chunk, init):
    carry = chunk(0, 0, -1, init)
    carry = lax.fori_loop(1, SEQ // FF_ROWS - 1,
                          lambda ci, cr: chunk(pl.multiple_of(ci * FF_ROWS - FF_HALO, 8), FF_HALO, 0, cr), carry)
    return chunk(SEQ - FF_SLAB, 2 * FF_HALO, 1, carry)


def _ffn_shifts(edge):
    row = _row_ids(FF_SLAB, FF_TILE)

    def prev(x):
        r = pltpu.roll(x, 1, 0)
        return jnp.where(row >= 1, r, 0.0) if edge == -1 else r

    def nxt(x):
        r = pltpu.roll(x, FF_SLAB - 1, 0)
        return jnp.where(row < FF_SLAB - 1, r, 0.0) if edge == 1 else r

    return prev, nxt


def ffn_act(hpre, conv_w, conv_b):
    def body(ha_ref, hg_ref, wa_ref, wg_ref, ba_ref, bg_ref, o_ref):
        wa, wg, ba, bg = wa_ref[...], wg_ref[...], ba_ref[...], bg_ref[...]

        def chunk(lo, mid, edge, carry):
            prev, nxt = _ffn_shifts(edge)
            ha, hg = ha_ref[pl.ds(lo, FF_SLAB), :], hg_ref[pl.ds(lo, FF_SLAB), :]
            a = prev(ha) * wa[0:1] + ha * wa[1:2] + nxt(ha) * wa[2:3] + ba
            g = prev(hg) * wg[0:1] + hg * wg[1:2] + nxt(hg) * wg[2:3] + bg
            o_ref[pl.ds(lo + mid, FF_ROWS), :] = (a * _sigmoid(a) * g)[mid:mid + FF_ROWS].astype(BF16)
            return carry

        _ffn_row_chunks(chunk, 0)

    col = lambda rows, off: pl.BlockSpec((rows, FF_TILE), lambda j: (0, j + off))
    return pl.pallas_call(
        body, name="ffn_act", grid=(FF_TILES,),
        in_specs=[col(SEQ, 0), col(SEQ, FF_TILES), col(3, 0), col(3, FF_TILES), col(1, 0), col(1, FF_TILES)],
        out_specs=col(SEQ, 0),
        out_shape=jax.ShapeDtypeStruct((SEQ, D_FF), BF16),
        compiler_params=_params("parallel"),
    )(hpre, hpre, conv_w, conv_w, conv_b, conv_b)


def ffn_down_loss(act, w_down, x1, g5, target):
    def body(a_ref, w_ref, x1_ref, g5_ref, t_ref, f_ref, dy_ref, df_ref, ls_ref, dg_ref):
        i = pl.program_id(0)
        f = jnp.dot(a_ref[...], w_ref[...], preferred_element_type=F32)
        g5 = g5_ref[...]
        err = x1_ref[...] + g5 * f - t_ref[...]
        dy = err * (1.0 / D_MODEL)
        f_ref[...] = f
        dy_ref[...] = dy
        df_ref[...] = (dy * g5).astype(BF16)

        @pl.when(i == 0)
        def _():
            ls_ref[...] = jnp.zeros_like(ls_ref)
            dg_ref[...] = jnp.zeros_like(dg_ref)

        ls_ref[...] = ls_ref[...] + jnp.sum(err * err)
        dg_ref[...] = dg_ref[...] + jnp.sum(dy * f, axis=0, keepdims=True)

    row = pl.BlockSpec((ROW_TILE, D_MODEL), lambda i: (i, 0))
    f32o = jax.ShapeDtypeStruct((SEQ, D_MODEL), F32)
    return pl.pallas_call(
        body, name="ffn_down_loss", grid=(SEQ // ROW_TILE,),
        in_specs=[pl.BlockSpec((ROW_TILE, D_FF), lambda i: (i, 0)), _full((D_FF, D_MODEL)), row, _full((1, D_MODEL)), row],
        out_specs=[row, row, row, _full((8, 128)), _full((1, D_MODEL))],
        out_shape=[f32o, f32o, jax.ShapeDtypeStruct((SEQ, D_MODEL), BF16), jax.ShapeDtypeStruct((8, 128), F32),
                   jax.ShapeDtypeStruct((1, D_MODEL), F32)],
        compiler_params=_params("arbitrary"),
    )(act, w_down, x1, g5, target)


def ffn_down_bwd(df, w_down):
    def body(df_ref, w_ref, o_ref):
        o_ref[...] = _dot_nt(df_ref[...], w_ref[...])

    return pl.pallas_call(
        body, name="ffn_down_bwd", grid=(SEQ // ROW_TILE,),
        in_specs=[pl.BlockSpec((ROW_TILE, D_MODEL), lambda i: (i, 0)), _full((D_FF, D_MODEL))],
        out_specs=pl.BlockSpec((ROW_TILE, D_FF), lambda i: (i, 0)),
        out_shape=jax.ShapeDtypeStruct((SEQ, D_FF), F32),
        compiler_params=_params("parallel"),
    )(df, w_down)


def ffn_act_bwd(hpre, d_act, conv_w, conv_b):
    def body(ha_ref, hg_ref, da_ref, wa_ref, wg_ref, ba_ref, bg_ref, dha_ref, dhg_ref, dwa_ref, dwg_ref, dba_ref, dbg_ref):
        wa, wg, ba, bg = wa_ref[...], wg_ref[...], ba_ref[...], bg_ref[...]

        def chunk(lo, mid, edge, acc):
            prev, nxt = _ffn_shifts(edge)
            rows = pl.ds(lo, FF_SLAB)
            ha, hg, dact = ha_ref[rows, :], hg_ref[rows, :], da_ref[rows, :]
            hap, han, hgp, hgn = prev(ha), nxt(ha), prev(hg), nxt(hg)
            a = hap * wa[0:1] + ha * wa[1:2] + han * wa[2:3] + ba
            g = hgp * wg[0:1] + hg * wg[1:2] + hgn * wg[2:3] + bg
            sig = _sigmoid(a)
            dca = dact * g * (sig * (1.0 + a * (1.0 - sig)))
            dcg = dact * a * sig
            m = slice(mid, mid + FF_ROWS)
            sums = []
            for dc, h, hp, hn, w, dh_ref in ((dca, ha, hap, han, wa, dha_ref), (dcg, hg, hgp, hgn, wg, dhg_ref)):
                dcm = dc[m]
                sums += [jnp.sum(dcm * hp[m], axis=0, keepdims=True), jnp.sum(dcm * h[m], axis=0, keepdims=True),
                         jnp.sum(dcm * hn[m], axis=0, keepdims=True), jnp.sum(dcm, axis=0, keepdims=True)]
                dh = nxt(dc) * w[0:1] + dc * w[1:2] + prev(dc) * w[2:3]
                dh_ref[pl.ds(lo + mid, FF_ROWS), :] = dh[m].astype(BF16)
            return tuple(x + y for x, y in zip(acc, sums))

        acc = _ffn_row_chunks(chunk, tuple(jnp.zeros((1, FF_TILE), F32) for _ in range(8)))
        dwa_ref[0:1, :], dwa_ref[1:2, :], dwa_ref[2:3, :], dba_ref[...] = acc[0], acc[1], acc[2], acc[3]
        dwg_ref[0:1, :], dwg_ref[1:2, :], dwg_ref[2:3, :], dbg_ref[...] = acc[4], acc[5], acc[6], acc[7]

    col = lambda rows, off: pl.BlockSpec((rows, FF_TILE), lambda j: (0, j + off))
    hshape = jax.ShapeDtypeStruct((SEQ, D_FF), BF16)
    wshape = jax.ShapeDtypeStruct((3, D_FF), F32)
    bshape = jax.ShapeDtypeStruct((1, D_FF), F32)
    return pl.pallas_call(
        body, name="ffn_act_bwd", grid=(FF_TILES,),
        in_specs=[col(SEQ, 0), col(SEQ, FF_TILES), col(SEQ, 0), col(3, 0), col(3, FF_TILES), col(1, 0), col(1, FF_TILES)],
        out_specs=[col(SEQ, 0), col(SEQ, 0), col(3, 0), col(3, 0), col(1, 0), col(1, 0)],
        out_shape=[hshape, hshape, wshape, wshape, bshape, bshape],
        compiler_params=_params("parallel"),
    )(hpre, hpre, d_act, conv_w, conv_w, conv_b, conv_b)


def _norm_mod_bwd(x, dxn, gain, scale):
    rstd = lax.rsqrt(jnp.mean(x * x, axis=-1, keepdims=True) + EPS)
    nrm = x * rstd
    dsh = jnp.sum(dxn, axis=0, keepdims=True)
    dsc = jnp.sum(dxn * nrm, axis=0, keepdims=True) * gain
    dgn = jnp.sum(dxn * nrm, axis=0, keepdims=True) * (1.0 + scale)
    dn = dxn * (gain * (1.0 + scale))
    dx = rstd * (dn - nrm * jnp.mean(dn * nrm, axis=-1, keepdims=True))
    return dx, dsh, dsc, dgn


def ffn_up_bwd(dha, dhg, w_up, x1, dy, gain, scale):
    def body(dha_ref, dhg_ref, w_ref, x_ref, dy_ref, g_ref, sc_ref, dx_ref, dsh_ref, dsc_ref, dgn_ref):
        i = pl.program_id(0)
        dxn = _dot_nt(dha_ref[...], w_ref[:, :D_FF]) + _dot_nt(dhg_ref[...], w_ref[:, D_FF:])
        dx, dsh, dsc, dgn = _norm_mod_bwd(x_ref[...], dxn, g_ref[...], sc_ref[...])
        dx_ref[...] = dy_ref[...] + dx

        @pl.when(i == 0)
        def _():
            dsh_ref[...] = dsh
            dsc_ref[...] = dsc
            dgn_ref[...] = dgn

        @pl.when(i > 0)
        def _():
            dsh_ref[...] = dsh_ref[...] + dsh
            dsc_ref[...] = dsc_ref[...] + dsc
            dgn_ref[...] = dgn_ref[...] + dgn

    row = pl.BlockSpec((ROW_TILE, D_MODEL), lambda i: (i, 0))
    vec = _full((1, D_MODEL))
    vshape = jax.ShapeDtypeStruct((1, D_MODEL), F32)
    return pl.pallas_call(
        body, name="ffn_up_bwd", grid=(SEQ // ROW_TILE,),
        in_specs=[pl.BlockSpec((ROW_TILE, D_FF), lambda i: (i, 0)), pl.BlockSpec((ROW_TILE, D_FF), lambda i: (i, 0)),
                  _full((D_MODEL, 2 * D_FF)), row, row, vec, vec],
        out_specs=[row, vec, vec, vec],
        out_shape=[jax.ShapeDtypeStruct((SEQ, D_MODEL), F32), vshape, vshape, vshape],
        compiler_params=_params("arbitrary"),
    )(dha, dhg, w_up, x1, dy, gain, scale)


def merge_bwd(dx1, out, g2, p, u, v, w_rnn, w_na, w_out, comm=None):
    def body(dx_ref, out_ref, g2_ref, mr_ref, mn_ref, u_ref, v_ref, wr_ref, wn_ref, wo_ref,
             dout_ref, du_ref, dv_ref, dmr_ref, dmn_ref, dyr_ref, dyn_ref, dg2_ref):
        i = pl.program_id(0)

        @pl.when(i == 0)
        def _():
            dmr_ref[...] = jnp.zeros_like(dmr_ref)
            dmn_ref[...] = jnp.zeros_like(dmn_ref)
            dg2_ref[...] = jnp.zeros_like(dg2_ref)

        @pl.when(i > 0)
        def _():
            dx = dx_ref[...]
            dg2_ref[...] = dg2_ref[...] + jnp.sum(dx * out_ref[...], axis=0, keepdims=True)
            dout = (dx * g2_ref[...]).astype(BF16)
            dout_ref[...] = dout
            dm = _dot_nt(dout, wo_ref[...])
            sr = _sigmoid(mr_ref[...])
            sn = _sigmoid(mn_ref[...])
            du = (dm * sr).astype(BF16)
            dv = (dm * sn).astype(BF16)
            du_ref[...] = du
            dv_ref[...] = dv
            dmr_ref[...] = (dm * u_ref[...] * (sr * (1.0 - sr))).astype(BF16)
            dmn_ref[...] = (dm * v_ref[...] * (sn * (1.0 - sn))).astype(BF16)
            dyr_ref[...] = _dot_nt(du, wr_ref[...])
            dyn_ref[...] = _dot_nt(dv, wn_ref[...])

    lat = pl.BlockSpec((ROW_TILE, D_MODEL), lambda i: (jnp.maximum(i - 1, 0), 0))
    zrow = pl.BlockSpec((ROW_TILE, D_MODEL), lambda i: (i, 0))
    pcol = lambda cb: pl.BlockSpec((ROW_TILE, D_MODEL), lambda i: (i, cb))
    wspec = _full((D_MODEL, D_MODEL))
    tb = jax.ShapeDtypeStruct((SEQ, D_MODEL), BF16)
    zb = jax.ShapeDtypeStruct((ZLEN, D_MODEL), BF16)
    tf = jax.ShapeDtypeStruct((SEQ, D_MODEL), F32)
    res, extra = _call(
        body, name="merge_bwd", grid=(ZLEN // ROW_TILE,),
        in_specs=[lat, lat, _full((1, D_MODEL)), pcol(5), pcol(6), lat, lat, wspec, wspec, wspec],
        out_specs=[lat, lat, lat, zrow, zrow, lat, lat, _full((1, D_MODEL))],
        out_shape=[tb, tb, tb, zb, zb, tf, tf, jax.ShapeDtypeStruct((1, D_MODEL), F32)],
        sem=("arbitrary",), args=(dx1, out, g2, p, p, u, v, w_rnn, w_na, w_out), comm=comm)
    return (*res, extra)


def attn_bwd(q_rot, q_pl, kk, vv, bt, y_na, d_yna, lse, comm=None):
    def body(qr_ref, qp_ref, kk_ref, vv_ref, bt_ref, o_ref, do_ref, lse_ref,
             dqr_ref, dqp_ref, dk_ref, dv_ref, dbt_ref, s_ref):
        jj = pl.program_id(1)

        @pl.when(jj == 0)
        def _():
            dqr_ref[...] = jnp.zeros_like(dqr_ref)
            dqp_ref[...] = jnp.zeros_like(dqp_ref)
            dk_ref[...] = jnp.zeros_like(dk_ref)
            dv_ref[...] = jnp.zeros_like(dv_ref)
            dbt_ref[...] = jnp.zeros_like(dbt_ref)

        @pl.when(jj > 0)
        def _():
            j = jj - 1
            ws, start = _key_window(j)
            win = pl.ds(start, KEY_TILE)
            kw, kc = kk_ref[win, :], kk_ref[:CTX_LEN, :]
            vw, vc = vv_ref[win, :], vv_ref[:CTX_LEN, :]
            qr, qp = qr_ref[...], qp_ref[...]
            do = do_ref[...]
            do_o = do * o_ref[...]
            dq_r, dq_p = [], []
            for hh in range(2):
                msk = _head_mask(hh)
                q_r, q_p = jnp.where(msk, qr, 0), jnp.where(msk, qp, 0)
                base = _attn_scores(j, ws, q_r, q_p, kw, kc, hh, bt_ref, s_ref)
                pr = jnp.exp(s_ref[...] - lse_ref[hh])
                delta = jnp.sum(jnp.where(msk, do_o, 0.0), axis=-1, keepdims=True)
                dob = jnp.where(msk, do, 0.0).astype(BF16)
                ds_lat = pr[:, :KEY_TILE] * (_dot_nt(dob, vw) - delta)
                ds_ctx = pr[:, KEY_TILE:] * (_dot_nt(dob, vc) - delta)
                for qi in range(Q_ROWS):
                    for m in range(KEY_ROWS // 2):
                        idx = base + 2 * m - qi
                        dbt_ref[hh, idx] = dbt_ref[hh, idx] + ds_lat[qi * GRID_W:(qi + 1) * GRID_W, 128 * m:128 * (m + 1)]
                dsb_lat = ds_lat.astype(BF16)
                dsb_ctx = ds_ctx.astype(BF16)
                prb = pr.astype(BF16)
                dq_r.append(jnp.dot(dsb_lat, kw, preferred_element_type=F32))
                dq_p.append(jnp.dot(dsb_ctx, kc, preferred_element_type=F32))
                dk_ref[win, :] = dk_ref[win, :] + _dot_tn(dsb_lat, q_r)
                dk_ref[:CTX_LEN, :] = dk_ref[:CTX_LEN, :] + _dot_tn(dsb_ctx, q_p)
                dv_ref[win, :] = dv_ref[win, :] + _dot_tn(prb[:, :KEY_TILE], dob)
                dv_ref[:CTX_LEN, :] = dv_ref[:CTX_LEN, :] + _dot_tn(prb[:, KEY_TILE:], dob)
            dqr_ref[...] = jnp.where(_head_mask(0), dq_r[0], dq_r[1])
            dqp_ref[...] = jnp.where(_head_mask(0), dq_p[0], dq_p[1])

    lat = lambda jj: jnp.maximum(jj - 1, 0)
    qspec = pl.BlockSpec((Q_TILE, 128), lambda hp, jj: (lat(jj) + 1, hp))
    kspec = pl.BlockSpec((ZLEN, 128), lambda hp, jj: (0, hp))
    btspec = pl.BlockSpec((2, BT_LEN, GRID_W, 128), lambda hp, jj: (hp, 0, 0, 0))
    ospec = pl.BlockSpec((Q_TILE, 128), lambda hp, jj: (lat(jj), hp))
    dqspec = pl.BlockSpec((Q_TILE, 128), lambda hp, jj: (jj, hp))
    zshape = jax.ShapeDtypeStruct((ZLEN, D_MODEL), F32)
    res, extra = _call(
        body, name="attn_bwd", grid=(NA_HEADS // 2, ZLEN // Q_TILE),
        in_specs=[qspec, qspec, kspec, kspec, btspec, ospec, ospec,
                  pl.BlockSpec((2, Q_TILE, 1), lambda hp, jj: (hp, lat(jj), 0))],
        out_specs=[dqspec, dqspec, kspec, kspec, btspec],
        out_shape=[zshape, zshape, zshape, zshape, jax.ShapeDtypeStruct((NA_HEADS, BT_LEN, GRID_W, 128), F32)],
        scratch_shapes=[pltpu.VMEM((Q_TILE, KEY_TILE + CTX_LEN), F32)], sem=("parallel", "arbitrary"),
        args=(q_rot, q_pl, kk, vv, bt, y_na, d_yna, lse), comm=comm)
    return (*res, extra)


def qkv_bwd(dq_rot, dq_pl, dk, dv, p, qg2, kg2, cos, sin, ones, comm=None):
    scale = HEAD_DIM ** -0.5
    n_hp, n_i = NA_HEADS // 2, ZLEN // PREP_TILE

    def norm_rope_bwd(d_rot, d_extra, x, gain, cos_t, sin_t, ones_m, dx_ref, acc_ref):
        xh, rstd = _head_rms(x, ones_m, 1.0)
        dn = d_rot * cos_t + _rope_partner(d_rot * sin_t)
        if d_extra is not None:
            dn = (dn + d_extra) * scale
        acc_ref[...] = acc_ref[...] + jnp.sum(dn * xh, axis=0, keepdims=True)
        dxh = dn * gain
        seg = jnp.dot(dxh * xh, ones_m, preferred_element_type=F32, precision=lax.Precision.HIGHEST) * (1.0 / HEAD_DIM)
        dx_ref[...] = (rstd * (dxh - xh * seg)).astype(BF16)

    def body(dqr_ref, dqp_ref, dk_ref, dv_ref, xq_ref, xk_ref, qg_ref, kg_ref, cos_ref, sin_ref, ones_ref,
             dxq_ref, dxk_ref, dxv_ref, dgq_ref, dgk_ref, accq_ref, acck_ref):
        hp, i = pl.program_id(0), pl.program_id(1)

        @pl.when((hp == 0) & (i == 0))
        def _():
            accq_ref[...] = jnp.zeros_like(accq_ref)
            acck_ref[...] = jnp.zeros_like(acck_ref)

        ones_m = ones_ref[...]
        cos_t, sin_t = cos_ref[...], sin_ref[...]
        norm_rope_bwd(dqr_ref[...], dqp_ref[...], xq_ref[...], qg_ref[...], cos_t, sin_t, ones_m, dxq_ref, accq_ref)
        norm_rope_bwd(dk_ref[...], None, xk_ref[...], kg_ref[...], cos_t, sin_t, ones_m, dxk_ref, acck_ref)
        dxv_ref[...] = dv_ref[...].astype(BF16)

        @pl.when((hp == n_hp - 1) & (i == n_i - 1))
        def _():
            dgq_ref[...] = accq_ref[:, :HEAD_DIM] + accq_ref[:, HEAD_DIM:]
            dgk_ref[...] = acck_ref[:, :HEAD_DIM] + acck_ref[:, HEAD_DIM:]

    col = lambda base: pl.BlockSpec((PREP_TILE, 128), lambda hp, i: (i, base + hp))
    small = pl.BlockSpec((1, 128), lambda hp, i: (0, 0))
    tab = pl.BlockSpec((PREP_TILE, 128), lambda hp, i: (i, 0))
    zb = jax.ShapeDtypeStruct((ZLEN, D_MODEL), BF16)
    gshape = jax.ShapeDtypeStruct((1, HEAD_DIM), F32)
    res, extra = _call(
        body, name="qkv_bwd", grid=(n_hp, n_i),
        in_specs=[col(0)] * 4 + [col(32), col(8), small, small, tab, tab, _full((128, 128))],
        out_specs=[col(0)] * 3 + [_full((1, HEAD_DIM))] * 2,
        out_shape=[zb, zb, zb, gshape, gshape],
        scratch_shapes=[pltpu.VMEM((1, 128), F32)] * 2, sem=("arbitrary", "arbitrary"),
        args=(dq_rot, dq_pl, dk, dv, p, p, qg2, kg2, cos, sin, ones), comm=comm)
    return (*res, extra)


def rpb_grad(dbt):
    e, _ = _bias_expand()
    n_dr = 2 * NA_ROWS - 1
    ea = np.zeros((31, GRID_W, 128), np.float32)
    ea[:, :, :GRID_W] = e
    eb = np.zeros((31, GRID_W, 128), np.float32)
    eb[:, :, GRID_W:] = e
    eat = jnp.asarray(ea.reshape(31, GRID_W * 128).T.copy())
    ebt = jnp.asarray(eb.reshape(31, GRID_W * 128).T.copy())
    sel_at = np.zeros((n_dr, BT_LEN), np.float32)
    sel_bt = np.zeros((n_dr, BT_LEN), np.float32)
    for r in range(BT_LEN):
        dr = r - BT_PAD
        if 0 <= dr < n_dr:
            sel_at[dr, r] = 1.0
        if 0 <= dr + 1 < n_dr:
            sel_bt[dr + 1, r] = 1.0
    hi = lax.Precision.HIGHEST

    tk = 2048
    wide = GRID_W * 128
    rows = NA_HEADS * BT_LEN
    n_k = wide // tk

    def body(d_ref, sa_ref, sb_ref, ea_ref, eb_ref, o_ref, a_s, b_s):
        k = pl.program_id(0)
        dm = d_ref[...]
        a = jnp.dot(dm, ea_ref[...], preferred_element_type=F32, precision=hi)
        b = jnp.dot(dm, eb_ref[...], preferred_element_type=F32, precision=hi)

        @pl.when(k == 0)
        def _():
            a_s[...] = a
            b_s[...] = b

        @pl.when(k > 0)
        def _():
            a_s[...] = a_s[...] + a
            b_s[...] = b_s[...] + b

        @pl.when(k == n_k - 1)
        def _():
            for h in range(NA_HEADS):
                sl = slice(h * BT_LEN, (h + 1) * BT_LEN)
                o_ref[h] = (jnp.dot(sa_ref[...], a_s[sl, :], preferred_element_type=F32, precision=hi)
                            + jnp.dot(sb_ref[...], b_s[sl, :], preferred_element_type=F32, precision=hi))

    return pl.pallas_call(
        body, name="rpb_grad", grid=(n_k,),
        in_specs=[pl.BlockSpec((rows, tk), lambda k: (0, k)), _full((n_dr, BT_LEN)), _full((n_dr, BT_LEN)),
                  pl.BlockSpec((tk, 31), lambda k: (k, 0)), pl.BlockSpec((tk, 31), lambda k: (k, 0))],
        out_specs=_full((NA_HEADS, n_dr, 31)),
        out_shape=jax.ShapeDtypeStruct((NA_HEADS, n_dr, 31), F32),
        scratch_shapes=[pltpu.VMEM((rows, 31), F32), pltpu.VMEM((rows, 31), F32)],
        compiler_params=_params("arbitrary"),
    )(dbt.reshape(rows, wide), jnp.asarray(sel_at), jnp.asarray(sel_bt), eat, ebt)


def lru_bwd(p, d_yrnn, conv_w, conv_b, wa, ba, wx, bx, lam, comm=None):
    blk, wspec = _lru_in_specs()

    def body(xr_ref, gx_ref, dy_ref, cw_ref, cb_ref, wa_ref, ba_ref, wx_ref, bx_ref, lam_ref,
             dxr_ref, dgx_ref, dcw_ref, dcb_ref, dwa_ref, dba_ref, dwx_ref, dbx_ref, dlam_ref,
             a_s, b_s, h_s, l_s, hsum_s, dxc_s, dh_s):
        xr = xr_ref[...]
        cw = cw_ref[...]
        xc = _lru_conv(xr, cw, cb_ref[...])
        xcb = xc.astype(BF16)
        g, dg = _gelu_parts(gx_ref[CTX_LEN:, :])
        dy = dy_ref[...]
        dh_s[:CTX_LEN, :] = jnp.zeros((CTX_LEN, LRU_BLOCK_W), F32)
        dh_s[CTX_LEN:, :] = dy * g
        row = _row_ids(ZLEN, LRU_BLOCK_W)
        zero = jnp.zeros((1, LRU_BLOCK_W), F32)
        for d in (0, 1):
            wab = wa_ref[d, 0].astype(BF16)
            wxb = wx_ref[d, 0].astype(BF16)
            lam_d = lam_ref[d:d + 1, :]
            r, gi, sp, a, sq, b = _lru_gates(xc, xcb, wab, ba_ref[d:d + 1, :], wxb, bx_ref[d:d + 1, :], lam_d)
            a_s[...] = a
            b_s[...] = b
            _lru_scan_dir(d, a_s, b_s, h_s)
            h = h_s[...]
            if d == 0:
                hsum_s[...] = h
                h_prev = jnp.where(row >= 1, pltpu.roll(h, 1, 0), 0.0)
                a_s[...] = pltpu.roll(a, ZLEN - 1, 0)
                _scan_down(a_s, dh_s, l_s, 0, N_CHUNK, zero)
            else:
                hsum_s[...] = hsum_s[...] + h
                h_prev = jnp.where(row == CTX_LEN - 1, 0.0, pltpu.roll(h, ZLEN - 1, 0))
                a_s[...] = pltpu.roll(a, 1, 0)
                c = _scan_up(a_s, dh_s, l_s, CTX_CHUNKS, N_CHUNK, zero)
                _scan_up(a_s, dh_s, l_s, 0, CTX_CHUNKS, c)
            db = l_s[...]
            da = db * h_prev
            dsq = db * gi * xc
            dgi = db * sq * xc
            dxc_d = db * sq * gi
            dla = da * a - dsq * (a * a) / sq
            dr = dla * ((-LRU_C) * sp)
            dsp = jnp.sum(dla * ((-LRU_C) * r), axis=0, keepdims=True)
            dlam_ref[d:d + 1, :] = -dsp * _sigmoid(-lam_d)
            dzr = dr * r * (1.0 - r)
            dzi = dgi * gi * (1.0 - gi)
            dba_ref[d:d + 1, :] = jnp.sum(dzr, axis=0, keepdims=True)
            dbx_ref[d:d + 1, :] = jnp.sum(dzi, axis=0, keepdims=True)
            dzrb = dzr.astype(BF16)
            dzib = dzi.astype(BF16)
            dwa_ref[d, 0] = _dot_tn(xcb, dzrb)
            dwx_ref[d, 0] = _dot_tn(xcb, dzib)
            dxc_d = dxc_d + _dot_nt(dzrb, wab) + _dot_nt(dzib, wxb)
            if d == 0:
                dxc_s[...] = dxc_d
            else:
                dxc_s[...] = dxc_s[...] + dxc_d
        dxc = dxc_s[...]
        dxr_ref[...] = _lru_conv_t(dxc, cw).astype(BF16)
        dcb_ref[...] = jnp.sum(dxc, axis=0, keepdims=True)
        segpos = jnp.where(row < CTX_LEN, row, row - CTX_LEN)
        seglen = jnp.where(row < CTX_LEN, CTX_LEN, SEQ)
        for k in range(4):
            off = k - 2
            if off == 0:
                sh = xr
            else:
                ok = (segpos + off >= 0) & (segpos + off < seglen)
                sh = jnp.where(ok, pltpu.roll(xr, (-off) % ZLEN, 0), 0.0)
            dcw_ref[k:k + 1, :] = jnp.sum(dxc * sh, axis=0, keepdims=True)
        dgx_ref[:CTX_LEN, :] = jnp.zeros((CTX_LEN, LRU_BLOCK_W), BF16)
        dgx_ref[CTX_LEN:, :] = (dy * hsum_s[CTX_LEN:, :] * dg).astype(BF16)

    zs = pltpu.VMEM((ZLEN, LRU_BLOCK_W), F32)
    zb = jax.ShapeDtypeStruct((ZLEN, D_MODEL), BF16)
    v2 = jax.ShapeDtypeStruct((2, D_MODEL), F32)
    w4 = jax.ShapeDtypeStruct((2, LRU_BLOCKS, LRU_BLOCK_W, LRU_BLOCK_W), F32)
    res, extra = _call(
        body, name="lru_bwd", grid=(LRU_BLOCKS,),
        in_specs=[blk(ZLEN), pl.BlockSpec((ZLEN, LRU_BLOCK_W), lambda b: (0, 24 + b)), blk(SEQ), blk(4), blk(1),
                  wspec, blk(2), wspec, blk(2), blk(2)],
        out_specs=[blk(ZLEN), blk(ZLEN), blk(4), blk(1), wspec, blk(2), wspec, blk(2), blk(2)],
        out_shape=[zb, zb, jax.ShapeDtypeStruct((4, D_MODEL), F32), jax.ShapeDtypeStruct((1, D_MODEL), F32),
                   w4, v2, w4, v2, v2],
        scratch_shapes=[zs] * 7, sem=("arbitrary",),
        args=(p, p, d_yrnn, conv_w, conv_b, wa, ba, wx, bx, lam), comm=comm)
    return (*res, extra)


def in_proj_bwd(dgs, w_in, z, dx1, gain, scale, comm=None):
    def body(*refs):
        dg_refs = refs[:7]
        w_ref, z_ref, dx1_ref, g_ref, sc_ref, gx_ref, dsh_ref, dsc_ref, dgn_ref = refs[7:]
        i = pl.program_id(0)
        dxn = _dot_nt(dg_refs[0][...], w_ref[:, 0:D_MODEL])
        for g in range(1, 7):
            dxn = dxn + _dot_nt(dg_refs[g][...], w_ref[:, g * D_MODEL:(g + 1) * D_MODEL])
        dx, dsh, dsc, dgn = _norm_mod_bwd(z_ref[...], dxn, g_ref[...], sc_ref[0])

        @pl.when(i <= 1)
        def _():
            dsh_ref[0] = dsh
            dsc_ref[0] = dsc

        @pl.when(i > 1)
        def _():
            dsh_ref[0] = dsh_ref[0] + dsh
            dsc_ref[0] = dsc_ref[0] + dsc

        @pl.when(i == 0)
        def _():
            dgn_ref[...] = dgn

        @pl.when(i > 0)
        def _():
            dgn_ref[...] = dgn_ref[...] + dgn
            gx_ref[...] = dx1_ref[...] + dx

    zrow = pl.BlockSpec((ROW_TILE, D_MODEL), lambda i: (i, 0))
    lat = pl.BlockSpec((ROW_TILE, D_MODEL), lambda i: (jnp.maximum(i - 1, 0), 0))
    mod = pl.BlockSpec((1, 1, D_MODEL), lambda i: (jnp.minimum(i, 1), 0, 0))
    mshape = jax.ShapeDtypeStruct((2, 1, D_MODEL), F32)
    res, extra = _call(
        body, name="in_proj_bwd", grid=(ZLEN // ROW_TILE,),
        in_specs=[zrow] * 7 + [_full((D_MODEL, IN_COLS)), zrow, lat, _full((1, D_MODEL)), mod],
        out_specs=[lat, mod, mod, _full((1, D_MODEL))],
        out_shape=[jax.ShapeDtypeStruct((SEQ, D_MODEL), F32), mshape, mshape, jax.ShapeDtypeStruct((1, D_MODEL), F32)],
        sem=("arbitrary",), args=(*dgs, w_in, z, dx1, gain, scale), comm=comm)
    return (*res, extra)


def matmul_tn(a, b, name, tm, tn, prev=None, col_block=0, total_cols=None):
    k, m = a.shape
    n = b.shape[1]
    total_cols = n if total_cols is None else total_cols
    assert m % tm == 0 and n % tn == 0
    off = col_block * (n // tn)

    def body(a_ref, b_ref, *rest):
        rest[-1][...] = _dot_tn(a_ref[...].astype(BF16), b_ref[...]).astype(BF16)

    in_specs = [pl.BlockSpec((k, tm), lambda i, j: (0, i)), pl.BlockSpec((k, tn), lambda i, j: (0, j))]
    args = [a, b]
    aliases = {}
    if prev is not None:
        in_specs.append(pl.BlockSpec(memory_space=pl.ANY))
        args.append(prev)
        aliases = {2: 0}
    return pl.pallas_call(
        body, name=name, grid=(m // tm, n // tn), in_specs=in_specs,
        out_specs=pl.BlockSpec((tm, tn), lambda i, j: (i, j + off)),
        out_shape=jax.ShapeDtypeStruct((m, total_cols), BF16),
        input_output_aliases=aliases,
        compiler_params=_params("parallel", "parallel"),
    )(*args)


def local_step(z, target, modx, modc, norm_mix_g, norm_ffn_g, w_in, conv_w, conv_b, wa, ba, wx, bx, lam, qg, kg, rpb,
               w_rnn, w_na, w_out, w_up, fconv_w, fconv_b, w_down, idx=None, bt=None):
    dist = idx is not None
    c_idx = idx[1:2] if dist else None
    d = D_MODEL
    mx = [modx[:, k * d:(k + 1) * d] for k in range(N_MOD)]
    shift = jnp.stack([modc[:, 0:d], mx[0]])
    scale = jnp.stack([modc[:, d:2 * d], mx[1]])
    cos, sin = _rope_tables()
    ones = _head_ones()
    qg2 = jnp.tile(qg, (1, 2))
    kg2 = jnp.tile(kg, (1, 2))

    xn = norm_mod(z, norm_mix_g, shift, scale, "norm_mix")
    if bt is None:
        bt, _ = bias_table(rpb)
    p, _ = matmul_wide(xn, w_in, "in_proj", 3 * ROW_TILE, 1792)
    y_rnn, got = lru_fwd(p, conv_w, conv_b, wa, ba, wx, bx, lam,
                         comm=gather_weights_comm([w_down], [5]) if dist else None)
    if dist:
        w_down = got[0]
    q_rot, q_pl, kk, vv, got = qkv_prep(p, qg2, kg2, cos, sin, ones,
                                        comm=gather_weights_comm([w_rnn, w_na, w_out], [1, 2, 3]) if dist else None)
    if dist:
        w_rnn, w_na, w_out = got
    y_na, lse, got = attn_fwd(q_rot, q_pl, kk, vv, bt, comm=gather_weights_comm([w_up], [4]) if dist else None)
    if dist:
        w_up = got[0]
    u, v, merged, out, x1 = merge_fwd(y_rnn, y_na, p, z, mx[2], w_rnn, w_na, w_out)
    xn2 = norm_mod(x1, norm_ffn_g, mx[3][None], mx[4][None], "norm_ffn")
    hpre, _ = matmul_wide(xn2, w_up, "ffn_up", 2 * ROW_TILE, 1408)
    act = ffn_act(hpre, fconv_w, fconv_b)
    f, dy, df, loss_sq, dg5 = ffn_down_loss(act, w_down, x1, mx[5], target)

    partials, pieces = {}, {}

    def views_of(which, grads):
        return [_grad_view(g, BIG[w][1], BIG[w][2]) for w, g in zip(which, grads)]

    def chip_partials(which, views, recv):
        for w, gv, r in zip(which, views, recv):
            partials[w] = add_halves(gv, r, c_idx, "add_halves_" + BIG[w][0])
        return scatter_pieces_comm([partials[w] for w in which], which)

    d_act = ffn_down_bwd(df, w_down)
    dha, dhg, d_fcw_a, d_fcw_g, d_fcb_a, d_fcb_g = ffn_act_bwd(hpre, d_act, fconv_w, fconv_b)
    d_fcw = jnp.concatenate([d_fcw_a, d_fcw_g], axis=1)
    d_fcb = jnp.concatenate([d_fcb_a, d_fcb_g], axis=1)
    dx1, d_s3, d_s4, d_gffn = ffn_up_bwd(dha, dhg, w_up, x1, dy, norm_ffn_g, mx[4])
    g_w_down = matmul_tn(act, df, "gw_down", 256, D_MODEL)
    g_w_up = matmul_tn(xn2, dha, "gw_up_a", 512, 1408, total_cols=2 * D_FF)
    g_w_up = matmul_tn(xn2, dhg, "gw_up_g", 512, 1408, prev=g_w_up, col_block=1, total_cols=2 * D_FF)
    v_ffn = views_of([4, 5], [g_w_up, g_w_down]) if dist else None
    *mb, got = merge_bwd(dx1, out, mx[2], p, u, v, w_rnn, w_na, w_out,
                         comm=exchange_halves_comm(v_ffn) if dist else None)
    dout, du, dv, dmr, dmn, dyr, dyn, dg2 = mb
    recv_ffn = got
    g_w_out = matmul_tn(merged, dout, "gw_out", 1024, 512)
    g_w_rnn = matmul_tn(y_rnn, du, "gw_rnn", 1024, 512)
    g_w_na = matmul_tn(y_na, dv, "gw_na", 1024, 512)
    v_mix = views_of([1, 2, 3], [g_w_rnn, g_w_na, g_w_out]) if dist else None
    *lru_grads, got = lru_bwd(p, dyr, conv_w, conv_b, wa, ba, wx, bx, lam,
                              comm=join_comms(chip_partials([4, 5], v_ffn, recv_ffn),
                                              exchange_halves_comm(v_mix)) if dist else None)
    dxr, dgx, d_cw, d_cb, d_wa, d_ba, d_wx, d_bx, d_lam = lru_grads
    if dist:
        pieces[4], pieces[5] = got[:2]
    lru_w_all = {}
    dqr, dqp, dk, dvh, dbt, got = attn_bwd(
        q_rot, q_pl, kk, vv, bt, y_na, dyn, lse,
        comm=join_comms(chip_partials([1, 2, 3], v_mix, got[2:]),
                        all_gather_comm(d_wa.reshape(-1, LRU_BLOCK_W))) if dist else None)
    if dist:
        pieces[1], pieces[2], pieces[3], lru_w_all["lru_wa"] = got
    dq_cols, dk_cols, dv_cols, d_qg, d_kg, got = qkv_bwd(
        dqr, dqp, dk, dvh, p, qg2, kg2, cos, sin, ones,
        comm=all_gather_comm(d_wx.reshape(-1, LRU_BLOCK_W)) if dist else None)
    if dist:
        lru_w_all["lru_wx"] = got[0]
    d_rpb = rpb_grad(dbt)
    dgs = [dxr, dk_cols, dv_cols, dgx, dq_cols, dmr, dmn]
    g_w_in = None
    for g in range(7):
        g_w_in = matmul_tn(xn, dgs[g], "gw_in_%d" % g, 1024, 512, prev=g_w_in, col_block=g, total_cols=IN_COLS)
    if dist:
        v_in = views_of([0], [g_w_in])
        recv_in = run_comm(exchange_halves_comm(v_in), "grad_exchange_w_in")
    grad_x, dsh, dsc, d_gmix, got = in_proj_bwd(dgs, w_in, z, dx1, norm_mix_g, scale,
                                                comm=chip_partials([0], v_in, recv_in) if dist else None)
    if dist:
        pieces[0] = got[0]

    d_modx = jnp.concatenate([dsh[1], dsc[1], dg2, d_s3, d_s4, dg5], axis=1)
    d_modc = jnp.concatenate([dsh[0], dsc[0]], axis=1)
    return dict(loss_sq=loss_sq, grad_x=grad_x, d_modx=d_modx, d_modc=d_modc, norm_mix_g=d_gmix, norm_ffn_g=d_gffn,
                w_in=g_w_in, lru_conv_w=d_cw, lru_conv_b=d_cb, lru_wa=d_wa, lru_ba=d_ba, lru_wx=d_wx, lru_bx=d_bx,
                lru_lambda=d_lam, q_norm_g=d_qg, k_norm_g=d_kg, na_rpb=d_rpb, w_rnn_out=g_w_rnn, w_na_out=g_w_na,
                w_out=g_w_out, w_up=g_w_up, ffn_conv_w=d_fcw, ffn_conv_b=d_fcb, w_down=g_w_down,
                partials=partials, pieces=pieces, lru_w_all=lru_w_all)


def _mesh_pos():
    return lax.axis_index("x"), lax.axis_index("y"), lax.axis_index("c")


def _other_chips(x, y):
    return [(1 - x, y), (x, 1 - y), (1 - x, 1 - y)]


BIG = (("w_in", (D_MODEL, IN_COLS), 1), ("w_rnn_out", (D_MODEL, D_MODEL), 0), ("w_na_out", (D_MODEL, D_MODEL), 0),
       ("w_out", (D_MODEL, D_MODEL), 0), ("w_up", (D_MODEL, 2 * D_FF), 1), ("w_down", (D_FF, D_MODEL), 0))


def _shard_shape(full, axis):
    r, c = full
    return (r // N_SHARD, c) if axis == 0 else (r, c // N_SHARD)


def _slot(ref, full, axis, s, h):
    r, c = full
    if axis == 0:
        rs = r // N_SHARD
        return ref.at[pl.ds(s * rs + h * (rs // 2), rs // 2), :]
    cs = c // N_SHARD
    return ref.at[pl.ds(h * (r // 2), r // 2), pl.ds(s * cs, cs)]


def cast_into_full(x, full, axis, idx, name):
    r, c = x.shape
    tr = next(t for t in (512, 352, 256, 128) if r % t == 0)
    nb = r // tr

    def body(idx_ref, x_ref, o_ref):
        o_ref[...] = x_ref[...].astype(BF16)

    if axis == 0:
        out_spec = pl.BlockSpec((tr, c), lambda i, idx_ref: (idx_ref[0] * nb + i, 0))
    else:
        out_spec = pl.BlockSpec((tr, c), lambda i, idx_ref: (i, idx_ref[0]))
    return pl.pallas_call(
        body, name=name,
        grid_spec=pltpu.PrefetchScalarGridSpec(
            num_scalar_prefetch=1, grid=(nb,), in_specs=[pl.BlockSpec((tr, c), lambda i, idx_ref: (i, 0))],
            out_specs=out_spec),
        out_shape=jax.ShapeDtypeStruct(full, BF16),
        compiler_params=_params("parallel"),
    )(idx, x)


def run_comm(comm, name):
    k_in, k_out = len(comm.inputs), len(comm.out_shapes)

    def body(*refs):
        start, mid, end = comm.emit(refs[:k_in], refs[k_in:k_in + k_out], refs[k_in + k_out:])
        start()
        mid()
        end()

    hbm = pl.BlockSpec(memory_space=pl.ANY)
    return pl.pallas_call(
        body, name=name, in_specs=[hbm] * k_in, out_specs=[hbm] * k_out, out_shape=list(comm.out_shapes),
        input_output_aliases=dict(comm.aliases), scratch_shapes=list(comm.scratch),
        compiler_params=pltpu.CompilerParams(vmem_limit_bytes=VMEM_LIMIT_V7X),
    )(*comm.inputs)


def gather_weights_comm(fulls, which):
    nw = len(which)
    specs = [BIG[w] for w in which]

    def emit(_, outs, sems):
        send1, recv1, send2, recv2 = sems
        x, y, c = _mesh_pos()
        sibling = (x, y, 1 - c)
        chips = _other_chips(x, y)
        s_me = 2 * x + y
        shards = [2 * chip[0] + chip[1] for chip in chips]

        def ici(w, j, shard):
            _, full, axis = specs[w]
            dst = _slot(outs[w], full, axis, shard, c)
            return pltpu.make_async_remote_copy(
                src_ref=dst, dst_ref=dst, send_sem=send1.at[3 * w + j],
                recv_sem=recv1.at[3 * w + j], device_id=(*chips[j], c), device_id_type=MESH_T)

        def d2d(w, j, shard, half):
            _, full, axis = specs[w]
            dst = _slot(outs[w], full, axis, shard, half)
            return pltpu.make_async_remote_copy(
                src_ref=dst, dst_ref=dst, send_sem=send2.at[3 * w + j], recv_sem=recv2.at[3 * w + j],
                device_id=sibling, device_id_type=MESH_T)

        pairs = [(w, j) for w in range(nw) for j in range(3)]

        def start():
            for w, j in pairs:
                ici(w, j, s_me).start()

        def mid():
            for w, j in pairs:
                ici(w, j, shards[j]).wait_recv()
                d2d(w, j, shards[j], c).start()

        def end():
            for w, j in pairs:
                d2d(w, j, shards[j], 1 - c).wait_recv()
            for w, j in pairs:
                ici(w, j, s_me).wait_send()
                d2d(w, j, shards[j], c).wait_send()

        return start, mid, end

    return Comm(list(fulls), [jax.ShapeDtypeStruct(full, BF16) for _, full, _ in specs], {i: i for i in range(nw)},
                [pltpu.SemaphoreType.DMA((3 * nw,))] * 4, emit)


def join_comms(a, b):
    ai, ao, asc = len(a.inputs), len(a.out_shapes), len(a.scratch)

    def emit(ins, outs, sems):
        fa = a.emit(ins[:ai], outs[:ao], sems[:asc])
        fb = b.emit(ins[ai:], outs[ao:], sems[asc:])

        def both(k):
            def run():
                fa[k]()
                fb[k]()
            return run

        return both(0), both(1), both(2)

    aliases = dict(a.aliases)
    aliases.update({ai + i: ao + o for i, o in b.aliases.items()})
    return Comm(a.inputs + b.inputs, a.out_shapes + b.out_shapes, aliases, a.scratch + b.scratch, emit)


def all_gather_comm(x):
    def emit(srcs, outs, sems):
        send_sems, recv_sems, local_sem = sems
        x_ref, out_ref = srcs[0], outs[0]
        x, y, c = _mesh_pos()
        me, sibling = (x, y, c), (x, y, 1 - c)
        chips = _other_chips(x, y)

        def blk(px, py, pc):
            return out_ref.at[4 * px + 2 * py + pc]

        def copy(k, block, to, src=None):
            return pltpu.make_async_remote_copy(
                src_ref=blk(*block) if src is None else src, dst_ref=blk(*block),
                send_sem=send_sems.at[k], recv_sem=recv_sems.at[k], device_id=to, device_id_type=MESH_T)

        def mine():
            return pltpu.make_async_copy(x_ref, blk(*me), local_sem)

        def start():
            mine().start()
            copy(0, me, sibling, src=x_ref).start()
            for j, chip in enumerate(chips):
                copy(1 + j, me, (*chip, c), src=x_ref).start()

        def mid():
            for j, chip in enumerate(chips):
                copy(1 + j, (*chip, c), me).wait_recv()
                copy(4 + j, (*chip, c), sibling).start()

        def end():
            copy(0, sibling, me).wait_recv()
            for j, chip in enumerate(chips):
                copy(4 + j, (*chip, 1 - c), me).wait_recv()
            copy(0, me, sibling, src=x_ref).wait_send()
            for j, chip in enumerate(chips):
                copy(1 + j, me, (*chip, c), src=x_ref).wait_send()
                copy(4 + j, (*chip, c), sibling).wait_send()
            mine().wait()

        return start, mid, end

    return Comm([x], [jax.ShapeDtypeStruct((N_DEV,) + x.shape, F32)], {},
                [pltpu.SemaphoreType.DMA((7,)), pltpu.SemaphoreType.DMA((7,)), pltpu.SemaphoreType.DMA], emit)


def sum_blocks(g, name):
    _, r, c = g.shape
    tr = 256 if r % 256 == 0 else r

    def body(g_ref, o_ref):
        acc = g_ref[0]
        for k in range(1, N_DEV):
            acc = acc + g_ref[k]
        o_ref[...] = acc

    return pl.pallas_call(
        body, name=name, grid=(r // tr,),
        in_specs=[pl.BlockSpec((N_DEV, tr, c), lambda i: (0, i, 0))],
        out_specs=pl.BlockSpec((tr, c), lambda i: (i, 0)),
        out_shape=jax.ShapeDtypeStruct((r, c), F32),
        compiler_params=_params("parallel"),
    )(g)


def _grad_view(g, full, axis):
    r, c = full
    if axis == 0:
        return g.reshape(N_SHARD, 2, r // N_SHARD // 2, c)
    return g.reshape(1, 2, r // 2, c)


def exchange_halves_comm(gviews):
    nw = len(gviews)

    def emit(srcs, outs, sems):
        send_sems, recv_sems = sems
        x, y, c = _mesh_pos()

        def copies():
            return [pltpu.make_async_remote_copy(
                src_ref=srcs[w].at[:, pl.ds(1 - c, 1)], dst_ref=outs[w], send_sem=send_sems.at[w],
                recv_sem=recv_sems.at[w], device_id=(x, y, 1 - c), device_id_type=MESH_T) for w in range(nw)]

        def start():
            for cp in copies():
                cp.start()

        def end():
            for cp in copies():
                cp.wait()

        return start, lambda: None, end

    return Comm(list(gviews), [jax.ShapeDtypeStruct((g.shape[0], 1) + g.shape[2:], BF16) for g in gviews], {},
                [pltpu.SemaphoreType.DMA((nw,)), pltpu.SemaphoreType.DMA((nw,))], emit)


def _row_tile(rh):
    return 128 if rh % 128 == 0 else rh


def add_halves(gview, recv, c_idx, name):
    a, _, rh, cc = gview.shape
    tr = _row_tile(rh)

    def body(c_ref, g_ref, r_ref, o_ref):
        o_ref[0] = (g_ref[0, 0].astype(F32) + r_ref[0, 0].astype(F32)).astype(BF16)

    return pl.pallas_call(
        body, name=name,
        grid_spec=pltpu.PrefetchScalarGridSpec(
            num_scalar_prefetch=1, grid=(a, rh // tr),
            in_specs=[pl.BlockSpec((1, 1, tr, cc), lambda s, i, c_ref: (s, c_ref[0], i, 0)),
                      pl.BlockSpec((1, 1, tr, cc), lambda s, i, c_ref: (s, 0, i, 0))],
            out_specs=pl.BlockSpec((1, tr, cc), lambda s, i, c_ref: (s, i, 0))),
        out_shape=jax.ShapeDtypeStruct((a, rh, cc), BF16),
        compiler_params=_params("parallel", "parallel"),
    )(c_idx, gview, recv)


def _piece_shape(full, axis):
    rs, cs = _shard_shape(full, axis)
    return (rs // 2, cs)


def scatter_pieces_comm(partials, which):
    nw = len(which)
    specs = [BIG[w] for w in which]

    def emit(srcs, outs, sems):
        send_sems, recv_sems = sems
        x, y, c = _mesh_pos()
        chips = _other_chips(x, y)

        def copies():
            cps = []
            for w, (_, full, axis) in enumerate(specs):
                cs = full[1] // N_SHARD
                for j, chip in enumerate(chips):
                    s_j = 2 * chip[0] + chip[1]
                    src = srcs[w].at[s_j] if axis == 0 else srcs[w].at[0, :, pl.ds(s_j * cs, cs)]
                    cps.append(pltpu.make_async_remote_copy(
                        src_ref=src, dst_ref=outs[w].at[j], send_sem=send_sems.at[3 * w + j],
                        recv_sem=recv_sems.at[3 * w + j], device_id=(*chip, c), device_id_type=MESH_T))
            return cps

        def start():
            for cp in copies():
                cp.start()

        def mid():
            pass

        def end():
            for cp in copies():
                cp.wait()

        return start, mid, end

    return Comm(list(partials), [jax.ShapeDtypeStruct((3,) + _piece_shape(full, axis), BF16) for _, full, axis in specs],
                {}, [pltpu.SemaphoreType.DMA((3 * nw,)), pltpu.SemaphoreType.DMA((3 * nw,))], emit)


def add_pieces(partial, recv, idx, axis, name):
    _, rh, cs = recv.shape
    tr = _row_tile(rh)

    def body(idx_ref, p_ref, r_ref, o_ref):
        o_ref[0] = ((p_ref[0].astype(F32) + r_ref[0].astype(F32)) + r_ref[1].astype(F32)) + r_ref[2].astype(F32)

    if axis == 0:
        pspec = pl.BlockSpec((1, tr, cs), lambda i, idx_ref: (idx_ref[0], i, 0))
    else:
        pspec = pl.BlockSpec((1, tr, cs), lambda i, idx_ref: (0, i, idx_ref[0]))
    return pl.pallas_call(
        body, name=name,
        grid_spec=pltpu.PrefetchScalarGridSpec(
            num_scalar_prefetch=1, grid=(rh // tr,),
            in_specs=[pspec, pl.BlockSpec((3, tr, cs), lambda i, idx_ref: (0, i, 0))],
            out_specs=pl.BlockSpec((1, tr, cs), lambda i, idx_ref: (idx_ref[1], i, 0))),
        out_shape=jax.ShapeDtypeStruct((2, rh, cs), F32),
        compiler_params=_params("parallel"),
    )(idx, partial, recv)


def join_halves_comm(halves):
    nw = len(halves)

    def emit(_, outs, sems):
        send_sems, recv_sems = sems
        x, y, c = _mesh_pos()

        def copy(w, half):
            return pltpu.make_async_remote_copy(
                src_ref=outs[w].at[half], dst_ref=outs[w].at[half], send_sem=send_sems.at[w], recv_sem=recv_sems.at[w],
                device_id=(x, y, 1 - c), device_id_type=MESH_T)

        def start():
            for w in range(nw):
                copy(w, c).start()

        def end():
            for w in range(nw):
                copy(w, c).wait_send()
                copy(w, 1 - c).wait_recv()

        return start, lambda: None, end

    return Comm(list(halves), [jax.ShapeDtypeStruct(h.shape, F32) for h in halves], {i: i for i in range(nw)},
                [pltpu.SemaphoreType.DMA((nw,))] * 2, emit)


MOD_COLS = N_MOD * D_MODEL // N_SHARD
MOD_TILE = 512


def mod_fwd(c16, w_mod):
    def body(c_ref, w_ref, s_ref, o_ref):
        cv = c_ref[...]
        s = cv * _sigmoid(cv)
        s_ref[...] = s
        o_ref[...] = jnp.dot(s.astype(BF16), w_ref[...].astype(BF16), preferred_element_type=F32)

    return pl.pallas_call(
        body, name="mod_fwd", grid=(MOD_COLS // MOD_TILE,),
        in_specs=[_full((16, D_MODEL)), pl.BlockSpec((D_MODEL, MOD_TILE), lambda j: (0, j))],
        out_specs=[_full((16, D_MODEL)), pl.BlockSpec((16, MOD_TILE), lambda j: (0, j))],
        out_shape=[jax.ShapeDtypeStruct((16, D_MODEL), F32), jax.ShapeDtypeStruct((16, MOD_COLS), F32)],
        compiler_params=_params("arbitrary"),
    )(c16, w_mod)


def mod_bwd(s16, dm16, w_mod):
    hi = lax.Precision.HIGHEST

    def body(s_ref, d_ref, w_ref, gw_ref, ds_ref):
        j = pl.program_id(0)
        dm = d_ref[...]
        gw_ref[...] = lax.dot_general(s_ref[...], dm, (((0,), (0,)), ((), ())), preferred_element_type=F32, precision=hi)
        part = lax.dot_general(dm, w_ref[...], (((1,), (1,)), ((), ())), preferred_element_type=F32, precision=hi)

        @pl.when(j == 0)
        def _():
            ds_ref[...] = part

        @pl.when(j > 0)
        def _():
            ds_ref[...] = ds_ref[...] + part

    return pl.pallas_call(
        body, name="mod_bwd", grid=(MOD_COLS // MOD_TILE,),
        in_specs=[_full((16, D_MODEL)), pl.BlockSpec((16, MOD_TILE), lambda j: (0, j)),
                  pl.BlockSpec((D_MODEL, MOD_TILE), lambda j: (0, j))],
        out_specs=[pl.BlockSpec((D_MODEL, MOD_TILE), lambda j: (0, j)), _full((16, D_MODEL))],
        out_shape=[jax.ShapeDtypeStruct((D_MODEL, MOD_COLS), F32), jax.ShapeDtypeStruct((16, D_MODEL), F32)],
        compiler_params=_params("arbitrary"),
    )(s16, dm16, w_mod)


def cctx_grad(parts, c_ctx):
    def body(p_ref, c_ref, o_ref):
        ds = p_ref[0:1, :]
        for s in range(1, N_SHARD):
            ds = ds + p_ref[16 * s:16 * s + 1, :]
        cv = c_ref[...]
        sg = _sigmoid(cv)
        o_ref[...] = ds * (sg * (1.0 + cv * (1.0 - sg)))

    return pl.pallas_call(
        body, name="cctx_grad", in_specs=[_full((N_DEV * 8, D_MODEL)), _full((1, D_MODEL))],
        out_specs=_full((1, D_MODEL)), out_shape=jax.ShapeDtypeStruct((1, D_MODEL), F32),
    )(parts, c_ctx)


def add_rows(a, b, name):
    def body(a_ref, b_ref, o_ref):
        o_ref[...] = a_ref[...] + b_ref[...]

    return pl.pallas_call(body, name=name, in_specs=[_full(a.shape), _full(b.shape)], out_specs=_full(a.shape),
                          out_shape=jax.ShapeDtypeStruct(a.shape, F32))(a, b)


def _adamw_update(w_ref, g_ref, m_ref, v_ref, d_ref, nm_ref, nv_ref):
    g_ = g_ref[...]
    m_ = ADAM_B1 * m_ref[...] + (1.0 - ADAM_B1) * g_
    v_ = ADAM_B2 * v_ref[...] + (1.0 - ADAM_B2) * (g_ * g_)
    m_hat = m_ / (1.0 - ADAM_B1 ** ADAM_STEP)
    v_hat = v_ / (1.0 - ADAM_B2 ** ADAM_STEP)
    d_ref[...] = -ADAM_LR * (m_hat / (jnp.sqrt(v_hat) + ADAM_EPS) + ADAM_WD * w_ref[...])
    nm_ref[...] = m_
    nv_ref[...] = v_


def adamw_many(ws, gs, ms, vs):
    n = len(ws)

    def body(*refs):
        for i in range(n):
            _adamw_update(*[refs[k * n + i] for k in range(7)])

    shapes = [jax.ShapeDtypeStruct(w.shape, F32) for w in ws]
    return pl.pallas_call(body, name="adamw_small", out_shape=shapes * 3,
                          compiler_params=pltpu.CompilerParams(vmem_limit_bytes=VMEM_LIMIT_V7X))(*ws, *gs, *ms, *vs)


def adamw(w, g, m, v, name, comm=None):
    r, c = w.shape
    tr = 128 if (r % 128 == 0 and r > 128) else r

    def body(w_ref, g_ref, m_ref, v_ref, d_ref, nm_ref, nv_ref):
        _adamw_update(w_ref, g_ref, m_ref, v_ref, d_ref, nm_ref, nv_ref)

    spec = pl.BlockSpec((tr, c), lambda i: (i, 0))
    shp = jax.ShapeDtypeStruct((r, c), F32)
    res, extra = _call(body, name=name, grid=(r // tr,), in_specs=[spec] * 4, out_specs=[spec] * 3,
                       out_shape=[shp] * 3, sem=("parallel",), args=(w, g, m, v), comm=comm)
    return (*res, extra)


LANES = 1024


def _pack(arrs):
    rows, spans, at = [], [], 0
    for a in arrs:
        n = int(np.prod(a.shape))
        nr = 8 * -(-n // (8 * LANES))
        flat = a.reshape(-1)
        if nr * LANES != n:
            flat = jnp.concatenate([flat, jnp.zeros((nr * LANES - n,), F32)])
        rows.append(flat.reshape(nr, LANES))
        spans.append((at, nr, n, a.shape))
        at += nr
    return jnp.concatenate(rows, axis=0), spans


def _unpack(buf, spans):
    out = []
    for at, nr, n, shape in spans:
        out.append(buf[at:at + nr].reshape(-1)[:n].reshape(shape))
    return out


SMALL_SHARD = ("lru_conv_w", "lru_ba", "lru_bx", "lru_lambda", "ffn_conv_w")


def kernel(x, c, ctx, c_ctx, w_mod, b_mod, norm_mix_g, norm_ffn_g, w_in, lru_conv_w, lru_conv_b, lru_wa, lru_ba, lru_wx, lru_bx, lru_lambda, q_norm_g, k_norm_g, na_rpb, w_rnn_out, w_na_out, w_out, w_up, ffn_conv_w, ffn_conv_b, w_down, loss_target, m_c_ctx, m_w_mod, m_b_mod, m_norm_mix_g, m_norm_ffn_g, m_w_in, m_lru_conv_w, m_lru_conv_b, m_lru_wa, m_lru_ba, m_lru_wx, m_lru_bx, m_lru_lambda, m_q_norm_g, m_k_norm_g, m_na_rpb, m_w_rnn_out, m_w_na_out, m_w_out, m_w_up, m_ffn_conv_w, m_ffn_conv_b, m_w_down, v_c_ctx, v_w_mod, v_b_mod, v_norm_mix_g, v_norm_ffn_g, v_w_in, v_lru_conv_w, v_lru_conv_b, v_lru_wa, v_lru_ba, v_lru_wx, v_lru_bx, v_lru_lambda, v_q_norm_g, v_k_norm_g, v_na_rpb, v_w_rnn_out, v_w_na_out, v_w_out, v_w_up, v_ffn_conv_w, v_ffn_conv_b, v_w_down):
    weights = dict(c_ctx=c_ctx, w_mod=w_mod, b_mod=b_mod, norm_mix_g=norm_mix_g, norm_ffn_g=norm_ffn_g, w_in=w_in,
                   lru_conv_w=lru_conv_w, lru_conv_b=lru_conv_b, lru_wa=lru_wa, lru_ba=lru_ba, lru_wx=lru_wx,
                   lru_bx=lru_bx, lru_lambda=lru_lambda, q_norm_g=q_norm_g, k_norm_g=k_norm_g, na_rpb=na_rpb,
                   w_rnn_out=w_rnn_out, w_na_out=w_na_out, w_out=w_out, w_up=w_up, ffn_conv_w=ffn_conv_w,
                   ffn_conv_b=ffn_conv_b, w_down=w_down)
    mom1 = dict(c_ctx=m_c_ctx, w_mod=m_w_mod, b_mod=m_b_mod, norm_mix_g=m_norm_mix_g, norm_ffn_g=m_norm_ffn_g,
                w_in=m_w_in, lru_conv_w=m_lru_conv_w, lru_conv_b=m_lru_conv_b, lru_wa=m_lru_wa, lru_ba=m_lru_ba,
                lru_wx=m_lru_wx, lru_bx=m_lru_bx, lru_lambda=m_lru_lambda, q_norm_g=m_q_norm_g, k_norm_g=m_k_norm_g,
                na_rpb=m_na_rpb, w_rnn_out=m_w_rnn_out, w_na_out=m_w_na_out, w_out=m_w_out, w_up=m_w_up,
                ffn_conv_w=m_ffn_conv_w, ffn_conv_b=m_ffn_conv_b, w_down=m_w_down)
    mom2 = dict(c_ctx=v_c_ctx, w_mod=v_w_mod, b_mod=v_b_mod, norm_mix_g=v_norm_mix_g, norm_ffn_g=v_norm_ffn_g,
                w_in=v_w_in, lru_conv_w=v_lru_conv_w, lru_conv_b=v_lru_conv_b, lru_wa=v_lru_wa, lru_ba=v_lru_ba,
                lru_wx=v_lru_wx, lru_bx=v_lru_bx, lru_lambda=v_lru_lambda, q_norm_g=v_q_norm_g, k_norm_g=v_k_norm_g,
                na_rpb=v_na_rpb, w_rnn_out=v_w_rnn_out, w_na_out=v_w_na_out, w_out=v_w_out, w_up=v_w_up,
                ffn_conv_w=v_ffn_conv_w, ffn_conv_b=v_ffn_conv_b, w_down=v_w_down)
    order = list(weights)
    d = D_MODEL
    mx_, my_, mc_ = _mesh_pos()
    shard = 2 * mx_ + my_
    dev = 2 * shard + mc_

    idx = jnp.stack([shard, mc_]).astype(jnp.int32)
    wsh = {name: cast_into_full(weights[name][0], full, axis, idx, "cast_" + name) for name, full, axis in BIG}
    local_small, small_spans = _pack([c] + [weights[k][0] for k in SMALL_SHARD])
    bt, (w_in_full, gath) = bias_table(na_rpb[0], comm=join_comms(gather_weights_comm([wsh["w_in"]], [0]),
                                                                  all_gather_comm(local_small)))
    per_dev = [_unpack(gath[k], small_spans) for k in range(N_DEV)]
    c_all = jnp.concatenate([per_dev[k][0] for k in range(N_DEV)], axis=0)
    full_small = {name: jnp.concatenate([per_dev[2 * s][1 + i] for s in range(N_SHARD)], axis=-1)
                  for i, name in enumerate(SMALL_SHARD)}
    c16 = jnp.concatenate([c_all, c_ctx.reshape(1, d), jnp.zeros((7, d), F32)], axis=0)
    s16, mod_part = mod_fwd(c16, w_mod[0])
    mod_all = run_comm(all_gather_comm(mod_part), "gather_mod")[0]
    mod = jnp.concatenate([mod_all[2 * s] for s in range(N_SHARD)], axis=1) + b_mod
    modx = lax.dynamic_slice(mod, (dev, 0), (1, N_MOD * d))
    modc = mod[8:9]

    z = jnp.concatenate([ctx[0], x[0]], axis=0)
    res = local_step(z, loss_target[0], modx, modc, norm_mix_g, norm_ffn_g, w_in_full, full_small["lru_conv_w"],
                     lru_conv_b, lru_wa[0], full_small["lru_ba"], lru_wx[0], full_small["lru_bx"],
                     full_small["lru_lambda"], q_norm_g, k_norm_g, na_rpb[0], wsh["w_rnn_out"], wsh["w_na_out"],
                     wsh["w_out"], wsh["w_up"], full_small["ffn_conv_w"], ffn_conv_b, wsh["w_down"], idx=idx, bt=bt)

    halves = [add_pieces(res["partials"][i], res["pieces"][i], idx, BIG[i][2], "add_pieces_" + BIG[i][0])
              for i in range(len(BIG))]
    lru_tot = {k: sum_blocks(res["lru_w_all"][k], "sum_" + k).reshape(weights[k].shape[1:])
               for k in ("lru_wa", "lru_wx")}
    small_names = ["norm_mix_g", "norm_ffn_g", "lru_conv_w", "lru_conv_b", "lru_ba", "lru_bx",
                   "lru_lambda", "q_norm_g", "k_norm_g", "na_rpb", "ffn_conv_w", "ffn_conv_b"]
    local_g, g_spans = _pack([res["loss_sq"][0:1, 0:1], res["d_modx"], res["d_modc"]] + [res[k] for k in small_names])
    n_rows = local_g.shape[0]
    *joined, g_all = run_comm(join_comms(join_halves_comm(halves), all_gather_comm(local_g)), "tail_exchange")
    grads = {name: joined[i].reshape(_shard_shape(full, axis)) for i, (name, full, axis) in enumerate(BIG)}
    grads.update(lru_tot)
    g_tot = sum_blocks(g_all, "sum_small")
    tot = _unpack(g_tot, g_spans)
    loss = (0.5 / d) * tot[0][0, 0]
    small_tot = dict(zip(small_names, tot[3:]))
    at_x = g_spans[1][0]
    dmx_rows = g_all.reshape(N_DEV, n_rows, LANES)[:, at_x:at_x + N_MOD, :].reshape(N_DEV, N_MOD * d)
    dmc_row = jnp.concatenate([tot[2], jnp.zeros((1, 4 * d), F32)], axis=1)
    dm16 = jnp.concatenate([dmx_rows, dmc_row, jnp.zeros((7, N_MOD * d), F32)], axis=0)
    grads["b_mod"] = add_rows(tot[1], dmc_row, "b_mod_grad")
    g_w_mod, ds16 = mod_bwd(s16, lax.dynamic_slice(dm16, (0, shard * MOD_COLS), (16, MOD_COLS)), w_mod[0])
    grads["w_mod"] = g_w_mod
    for k in small_names:
        g = small_tot[k]
        if k in SMALL_SHARD:
            w_sh = weights[k].shape[-1]
            g = lax.dynamic_slice_in_dim(g, shard * w_sh, w_sh, axis=g.ndim - 1)
        grads[k] = g

    delta, new_m, new_v = {}, {}, {}
    for name, _, _ in BIG + (("w_mod", None, None),):
        *upd, got = adamw(weights[name][0], grads[name], mom1[name][0], mom2[name][0], "adamw_" + name,
                          comm=all_gather_comm(ds16[8:16]) if name == "w_in" else None)
        delta[name], new_m[name], new_v[name] = upd
        if name == "w_in":
            grads["c_ctx"] = cctx_grad(got[0].reshape(N_DEV * 8, d), c_ctx.reshape(1, d))
    rest = [k for k in order if k not in delta]
    views = {k: (grads[k].shape if grads[k].ndim <= 3 else (-1, grads[k].shape[-1])) for k in rest}
    small = adamw_many(*[[t[k].reshape(views[k]) for k in rest] for t in (weights, grads, mom1, mom2)])
    n_rest = len(rest)
    for i, k in enumerate(rest):
        delta[k], new_m[k], new_v[k] = small[i], small[n_rest + i], small[2 * n_rest + i]

    shaped = lambda t: [t[k].reshape(weights[k].shape) for k in order]
    return (loss, res["grad_x"][None], *shaped(grads), *shaped(delta), *shaped(new_m), *shaped(new_v))
```

```python
import numpy as np
import jax
import jax.numpy as jnp
from jax import lax
from jax.experimental import pallas as pl
from jax.experimental.pallas import tpu as pltpu

F32 = jnp.float32
BF16 = jnp.bfloat16

D_MODEL = 1024
SEQ = 2048
CTX_LEN = 256
ZLEN = SEQ + CTX_LEN
GRID_W = 64
GRID_ROWS = SEQ // GRID_W
LRU_BLOCK_W = 128
LRU_BLOCKS = 8
LRU_C = 8.0
NA_HEADS = 16
HEAD_DIM = 64
NA_ROWS = 8
NA_COLS = 16
ROPE_BASE = 10000.0
D_FF = 2816
N_MOD = 6
IN_COLS = 7 * D_MODEL
EPS = 1e-6
NEG_INF = -1e30
N_DEV = 8
N_SHARD = 4

ADAM_LR = 0.001
ADAM_B1 = 0.9
ADAM_B2 = 0.999
ADAM_EPS = 1e-08
ADAM_WD = 0.01
ADAM_STEP = 10

ROW_TILE = 256
Q_ROWS = 4
Q_TILE = Q_ROWS * GRID_W
KEY_ROWS = 12
KEY_TILE = KEY_ROWS * GRID_W
BT_PAD = 4
BT_LEN = 24
VMEM_LIMIT_V7X = 56 * 1024 * 1024

MESH_T = pl.DeviceIdType.MESH


def _params(*sem):
    return pltpu.CompilerParams(dimension_semantics=sem if sem else None, vmem_limit_bytes=VMEM_LIMIT_V7X)


def _full(shape):
    nd = len(shape)
    return pl.BlockSpec(shape, lambda *_: (0,) * nd)


class Comm:
    def __init__(self, inputs, out_shapes, aliases, scratch, emit):
        self.inputs, self.out_shapes, self.aliases, self.scratch, self.emit = inputs, out_shapes, aliases, scratch, emit


def _call(body, *, name, grid, in_specs, out_specs, out_shape, args, scratch_shapes=(), sem=(), comm=None):
    n_in, n_out, n_sc = len(in_specs), len(out_specs), len(scratch_shapes)
    if comm is None:
        res = pl.pallas_call(body, name=name, grid=grid, in_specs=list(in_specs), out_specs=list(out_specs),
                             out_shape=list(out_shape), scratch_shapes=list(scratch_shapes),
                             compiler_params=_params(*sem))(*args)
        return list(res), []
    k_in, k_out = len(comm.inputs), len(comm.out_shapes)
    steps = int(np.prod(grid))

    def hosted(*refs):
        ins, cins = refs[:n_in], refs[n_in:n_in + k_in]
        at = n_in + k_in
        outs, couts = refs[at:at + n_out], refs[at + n_out:at + n_out + k_out]
        at += n_out + k_out
        scr, cscr = refs[at:at + n_sc], refs[at + n_sc:]
        start, mid, end = comm.emit(cins, couts, cscr)
        lin = pl.program_id(0)
        for ax in range(1, len(grid)):
            lin = lin * grid[ax] + pl.program_id(ax)
        pl.when(lin == 0)(start)
        body(*ins, *outs, *scr)
        pl.when(lin == steps - 1 - steps // 7)(mid)
        pl.when(lin == steps - 1)(end)

    hbm = pl.BlockSpec(memory_space=pl.ANY)
    res = pl.pallas_call(
        hosted, name=name, grid=grid, in_specs=list(in_specs) + [hbm] * k_in, out_specs=list(out_specs) + [hbm] * k_out,
        out_shape=list(out_shape) + list(comm.out_shapes), scratch_shapes=list(scratch_shapes) + list(comm.scratch),
        input_output_aliases={n_in + i: n_out + o for i, o in comm.aliases.items()},
        compiler_params=_params(*(("arbitrary",) * len(grid))))(*args, *comm.inputs)
    return list(res[:n_out]), list(res[n_out:])


def _sigmoid(x):
    return 0.5 * jnp.tanh(0.5 * x) + 0.5


def _gelu_parts(x):
    c0 = 0.7978845608028654
    inner = c0 * (x + 0.044715 * x * x * x)
    t = jnp.tanh(inner)
    g = 0.5 * x * (1.0 + t)
    dg = 0.5 * (1.0 + t) + 0.5 * x * (1.0 - t * t) * c0 * (1.0 + 3.0 * 0.044715 * x * x)
    return g, dg


def _dot_nt(a, b):
    return lax.dot_general(a, b, (((1,), (1,)), ((), ())), preferred_element_type=F32)


def _dot_tn(a, b):
    return lax.dot_general(a, b, (((0,), (0,)), ((), ())), preferred_element_type=F32)


def norm_mod(xin, gain, shift, scale, name):
    r, d = xin.shape
    s_mod = shift.shape[0]
    assert r % ROW_TILE == 0

    def body(x_ref, g_ref, sh_ref, sc_ref, xn_ref):
        x = x_ref[...]
        nrm = x * lax.rsqrt(jnp.mean(x * x, axis=-1, keepdims=True) + EPS)
        xn_ref[...] = ((nrm * g_ref[...]) * (1.0 + sc_ref[0]) + sh_ref[0]).astype(BF16)

    mod_spec = pl.BlockSpec((1, 1, d), lambda i: (jnp.minimum(i, s_mod - 1), 0, 0))
    return pl.pallas_call(
        body, name=name, grid=(r // ROW_TILE,),
        in_specs=[pl.BlockSpec((ROW_TILE, d), lambda i: (i, 0)), _full((1, d)), mod_spec, mod_spec],
        out_specs=pl.BlockSpec((ROW_TILE, d), lambda i: (i, 0)),
        out_shape=jax.ShapeDtypeStruct((r, d), BF16),
        compiler_params=_params("parallel"),
    )(xin, gain, shift, scale)


def matmul_wide(a, b, name, tm, tn, comm=None):
    m, k = a.shape
    n = b.shape[1]
    assert m % tm == 0 and n % tn == 0

    def body(a_ref, b_ref, o_ref):
        o_ref[...] = jnp.dot(a_ref[...], b_ref[...], preferred_element_type=F32)

    res, extra = _call(
        body, name=name, grid=(n // tn, m // tm),
        in_specs=[pl.BlockSpec((tm, k), lambda j, i: (i, 0)), pl.BlockSpec((k, tn), lambda j, i: (0, j))],
        out_specs=[pl.BlockSpec((tm, tn), lambda j, i: (i, j))],
        out_shape=[jax.ShapeDtypeStruct((m, n), F32)],
        sem=("parallel", "parallel"), args=(a, b), comm=comm)
    return res[0], extra


def _row_ids(n, w):
    return lax.broadcasted_iota(jnp.int32, (n, w), 0)


def _lru_conv(xr, cw, cb):
    row = _row_ids(ZLEN, LRU_BLOCK_W)
    segpos = jnp.where(row < CTX_LEN, row, row - CTX_LEN)
    seglen = jnp.where(row < CTX_LEN, CTX_LEN, SEQ)
    acc = xr * cw[2:3, :] + cb
    for k in (0, 1, 3):
        off = k - 2
        sh = pltpu.roll(xr, (-off) % ZLEN, 0)
        ok = (segpos + off >= 0) & (segpos + off < seglen)
        acc = acc + jnp.where(ok, sh, 0.0) * cw[k:k + 1, :]
    return acc


def _lru_conv_t(dxc, cw):
    row = _row_ids(ZLEN, LRU_BLOCK_W)
    segpos = jnp.where(row < CTX_LEN, row, row - CTX_LEN)
    seglen = jnp.where(row < CTX_LEN, CTX_LEN, SEQ)
    acc = dxc * cw[2:3, :]
    for k in (0, 1, 3):
        off = k - 2
        sh = pltpu.roll(dxc, off % ZLEN, 0)
        ok = (segpos - off >= 0) & (segpos - off < seglen)
        acc = acc + jnp.where(ok, sh, 0.0) * cw[k:k + 1, :]
    return acc


def _lru_gates(xc, xcb, wa, ba, wx, bx, lam):
    r = _sigmoid(jnp.dot(xcb, wa, preferred_element_type=F32) + ba)
    i = _sigmoid(jnp.dot(xcb, wx, preferred_element_type=F32) + bx)
    sp = jnp.maximum(-lam, 0.0) + jnp.log1p(jnp.exp(-jnp.abs(lam)))
    la = (-LRU_C) * r * sp
    a = jnp.exp(la)
    sq = jnp.sqrt(-jnp.tanh(la) * (1.0 + a * a))
    b = sq * i * xc
    return r, i, sp, a, sq, b


def _scan8_fwd(a, b, rid):
    for s in (1, 2, 4):
        a_s = pltpu.roll(a, s, 0)
        b_s = pltpu.roll(b, s, 0)
        m = rid >= s
        b = jnp.where(m, a * b_s + b, b)
        a = jnp.where(m, a * a_s, a)
    return a, b


def _scan8_rev(a, b, rid):
    for s in (1, 2, 4):
        a_s = pltpu.roll(a, 8 - s, 0)
        b_s = pltpu.roll(b, 8 - s, 0)
        m = rid < 8 - s
        b = jnp.where(m, a * b_s + b, b)
        a = jnp.where(m, a * a_s, a)
    return a, b


N_CHUNK = ZLEN // 8
CTX_CHUNKS = CTX_LEN // 8
SCAN_UNROLL = 8


def _scan_up(a_ref, b_ref, h_ref, lo, hi, carry):
    rid = _row_ids(8, LRU_BLOCK_W)
    assert (hi - lo) % SCAN_UNROLL == 0

    def step(g, c):
        base = pl.multiple_of((lo + g * SCAN_UNROLL) * 8, 8)
        for u in range(SCAN_UNROLL):
            sl = pl.ds(base + 8 * u, 8)
            a, b = _scan8_fwd(a_ref[sl, :], b_ref[sl, :], rid)
            h_ref[sl, :] = b + a * c
            c = b[7:8, :] + a[7:8, :] * c
        return c

    return lax.fori_loop(0, (hi - lo) // SCAN_UNROLL, step, carry)


def _scan_down(a_ref, b_ref, h_ref, lo, hi, carry):
    rid = _row_ids(8, LRU_BLOCK_W)
    assert (hi - lo) % SCAN_UNROLL == 0

    def step(g, c):
        base = pl.multiple_of((hi - (g + 1) * SCAN_UNROLL) * 8, 8)
        for u in reversed(range(SCAN_UNROLL)):
            sl = pl.ds(base + 8 * u, 8)
            a, b = _scan8_rev(a_ref[sl, :], b_ref[sl, :], rid)
            h_ref[sl, :] = b + a * c
            c = b[0:1, :] + a[0:1, :] * c
        return c

    return lax.fori_loop(0, (hi - lo) // SCAN_UNROLL, step, carry)


def _lru_scan_dir(d, a_ref, b_ref, h_ref):
    zero = jnp.zeros((1, LRU_BLOCK_W), F32)
    if d == 0:
        _scan_up(a_ref, b_ref, h_ref, 0, N_CHUNK, zero)
    else:
        c = _scan_down(a_ref, b_ref, h_ref, 0, CTX_CHUNKS, zero)
        _scan_down(a_ref, b_ref, h_ref, CTX_CHUNKS, N_CHUNK, c)


def _lru_in_specs():
    blk = lambda rows: pl.BlockSpec((rows, LRU_BLOCK_W), lambda b: (0, b))
    wspec = pl.BlockSpec((2, 1, LRU_BLOCK_W, LRU_BLOCK_W), lambda b: (0, b, 0, 0))
    return blk, wspec


def lru_fwd(p, conv_w, conv_b, wa, ba, wx, bx, lam, comm=None):
    blk, wspec = _lru_in_specs()

    def body(xr_ref, gx_ref, cw_ref, cb_ref, wa_ref, ba_ref, wx_ref, bx_ref, lam_ref, y_ref, a_s, b_s, h_s, hsum_s):
        xr = xr_ref[...]
        xc = _lru_conv(xr, cw_ref[...], cb_ref[...])
        xcb = xc.astype(BF16)
        for d in (0, 1):
            _, _, _, a, _, b = _lru_gates(xc, xcb, wa_ref[d, 0].astype(BF16), ba_ref[d:d + 1, :],
                                          wx_ref[d, 0].astype(BF16), bx_ref[d:d + 1, :], lam_ref[d:d + 1, :])
            a_s[...] = a
            b_s[...] = b
            _lru_scan_dir(d, a_s, b_s, h_s)
            if d == 0:
                hsum_s[...] = h_s[...]
            else:
                hsum_s[...] = hsum_s[...] + h_s[...]
        g, _ = _gelu_parts(gx_ref[CTX_LEN:, :])
        y_ref[...] = (hsum_s[CTX_LEN:, :] * g).astype(BF16)

    zs = pltpu.VMEM((ZLEN, LRU_BLOCK_W), F32)
    res, extra = _call(
        body, name="lru_fwd", grid=(LRU_BLOCKS,),
        in_specs=[blk(ZLEN), pl.BlockSpec((ZLEN, LRU_BLOCK_W), lambda b: (0, 24 + b)), blk(4), blk(1),
                  wspec, blk(2), wspec, blk(2), blk(2)],
        out_specs=[pl.BlockSpec((SEQ, LRU_BLOCK_W), lambda b: (0, b))],
        out_shape=[jax.ShapeDtypeStruct((SEQ, D_MODEL), BF16)],
        scratch_shapes=[zs, zs, zs, zs], sem=("arbitrary",),
        args=(p, p, conv_w, conv_b, wa, ba, wx, bx, lam), comm=comm)
    return res[0], extra


def _rope_tables():
    t = np.arange(SEQ)
    lane = np.arange(2 * HEAD_DIM)
    in_head = lane % HEAD_DIM
    j = (in_head % 32) % 16
    freq = ROPE_BASE ** (-j.astype(np.float64) / 16.0)
    pos = np.where(in_head[None, :] < 32, (t // GRID_W)[:, None], (t % GRID_W)[:, None]).astype(np.float64)
    ang = (pos.astype(np.float32) * freq.astype(np.float32)[None, :]).astype(np.float32)
    cos = np.cos(ang).astype(np.float32)
    sin = np.sin(ang).astype(np.float32)
    sgn = np.where((in_head % 32) < 16, -1.0, 1.0).astype(np.float32)
    cos = np.concatenate([np.ones((CTX_LEN, 2 * HEAD_DIM), np.float32), cos], 0)
    sin = np.concatenate([np.zeros((CTX_LEN, 2 * HEAD_DIM), np.float32), sin * sgn[None, :]], 0)
    return jnp.asarray(cos), jnp.asarray(sin)


def _head_ones():
    lane = np.arange(2 * HEAD_DIM)
    return jnp.asarray((lane[:, None] // HEAD_DIM == lane[None, :] // HEAD_DIM).astype(np.float32))


def _rope_partner(x):
    lane = lax.broadcasted_iota(jnp.int32, x.shape, 1)
    return jnp.where((lane % 32) < 16, pltpu.roll(x, 128 - 16, 1), pltpu.roll(x, 16, 1))


def _head_sum(t, ones):
    hi = t.astype(BF16)
    lo = (t - hi.astype(F32)).astype(BF16)
    ones_b = ones.astype(BF16)
    return jnp.dot(hi, ones_b, preferred_element_type=F32) + jnp.dot(lo, ones_b, preferred_element_type=F32)


def _head_rms(x, ones, gain):
    ms = _head_sum(x * x, ones) * (1.0 / HEAD_DIM)
    rstd = lax.rsqrt(ms + EPS)
    return x * rstd * gain, rstd


PREP_TILE = 768


def qkv_prep(p, qg2, kg2, cos, sin, ones, comm=None):
    scale = HEAD_DIM ** -0.5

    def body(q_ref, k_ref, v_ref, qg_ref, kg_ref, cos_ref, sin_ref, ones_ref, qr_ref, qp_ref, kk_ref, vv_ref):
        ones_m = ones_ref[...]
        c, s = cos_ref[...], sin_ref[...]
        qn, _ = _head_rms(q_ref[...], ones_m, qg_ref[...])
        qn = qn * scale
        qr_ref[...] = (qn * c + _rope_partner(qn) * s).astype(BF16)
        qp_ref[...] = qn.astype(BF16)
        kn, _ = _head_rms(k_ref[...], ones_m, kg_ref[...])
        kk_ref[...] = (kn * c + _rope_partner(kn) * s).astype(BF16)
        vv_ref[...] = v_ref[...].astype(BF16)

    col = lambda base: pl.BlockSpec((PREP_TILE, 128), lambda hp, i: (i, base + hp))
    small = pl.BlockSpec((1, 128), lambda hp, i: (0, 0))
    tab = pl.BlockSpec((PREP_TILE, 128), lambda hp, i: (i, 0))
    oshape = jax.ShapeDtypeStruct((ZLEN, D_MODEL), BF16)
    res, extra = _call(
        body, name="qkv_prep", grid=(NA_HEADS // 2, ZLEN // PREP_TILE),
        in_specs=[col(32), col(8), col(16), small, small, tab, tab, _full((128, 128))],
        out_specs=[col(0)] * 4, out_shape=[oshape] * 4, sem=("parallel", "parallel"),
        args=(p, p, p, qg2, kg2, cos, sin, ones), comm=comm)
    return (*res, extra)


def _bias_expand():
    qc = np.arange(GRID_W)[:, None]
    kc = np.arange(GRID_W)[None, :]
    col_start = np.clip(qc - NA_COLS // 2, 0, GRID_W - NA_COLS)
    in_win = (kc >= col_start) & (kc < col_start + NA_COLS)
    dc = np.clip(kc - qc, -(NA_COLS - 1), NA_COLS - 1) + (NA_COLS - 1)
    e = np.zeros((2 * NA_COLS - 1, GRID_W, GRID_W), np.float32)
    for d in range(2 * NA_COLS - 1):
        e[d] = ((dc == d) & in_win).astype(np.float32)
    pen = np.where(in_win, 0.0, NEG_INF).astype(np.float32)
    return e, pen


def bias_table(rpb2, comm=None):
    e, pen = _bias_expand()
    n_dr = 2 * NA_ROWS - 1
    ea = np.zeros((31, GRID_W, 128), np.float32)
    ea[:, :, :GRID_W] = e
    eb = np.zeros((31, GRID_W, 128), np.float32)
    eb[:, :, GRID_W:] = e
    pen2 = np.concatenate([pen, pen], 1)
    ea = jnp.asarray(ea.reshape(31, GRID_W * 128))
    eb = jnp.asarray(eb.reshape(31, GRID_W * 128))
    sel_a = np.zeros((BT_LEN, n_dr), np.float32)
    sel_b = np.zeros((BT_LEN, n_dr), np.float32)
    for r in range(BT_LEN):
        dr = r - BT_PAD
        if 0 <= dr < n_dr:
            sel_a[r, dr] = 1.0
        if 0 <= dr + 1 < n_dr:
            sel_b[r, dr + 1] = 1.0
    sel_a, sel_b = jnp.asarray(sel_a), jnp.asarray(sel_b)
    pen2 = jnp.asarray(pen2.reshape(1, GRID_W * 128))
    hi = lax.Precision.HIGHEST

    def body(rpb_ref, sa_ref, sb_ref, ea_ref, eb_ref, pen_ref, o_ref, ra_s, rb_s):
        for h in range(NA_HEADS):
            rp = rpb_ref[h]
            ra_s[h * BT_LEN:(h + 1) * BT_LEN, :] = jnp.dot(sa_ref[...], rp, preferred_element_type=F32, precision=hi)
            rb_s[h * BT_LEN:(h + 1) * BT_LEN, :] = jnp.dot(sb_ref[...], rp, preferred_element_type=F32, precision=hi)
        o_ref[...] = (jnp.dot(ra_s[...], ea_ref[...], preferred_element_type=F32, precision=hi)
                      + jnp.dot(rb_s[...], eb_ref[...], preferred_element_type=F32, precision=hi) + pen_ref[...])

    tcol = 2048
    rows = NA_HEADS * BT_LEN
    res, extra = _call(
        body, name="bias_table", grid=(GRID_W * 128 // tcol,),
        in_specs=[_full((NA_HEADS, n_dr, 31)), _full((BT_LEN, n_dr)), _full((BT_LEN, n_dr)),
                  pl.BlockSpec((31, tcol), lambda j: (0, j)), pl.BlockSpec((31, tcol), lambda j: (0, j)),
                  pl.BlockSpec((1, tcol), lambda j: (0, j))],
        out_specs=[pl.BlockSpec((rows, tcol), lambda j: (0, j))],
        out_shape=[jax.ShapeDtypeStruct((rows, GRID_W * 128), F32)],
        scratch_shapes=[pltpu.VMEM((rows, 31), F32), pltpu.VMEM((rows, 31), F32)], sem=("parallel",),
        args=(rpb2, sel_a, sel_b, ea, eb, pen2), comm=comm)
    return res[0].reshape(NA_HEADS, BT_LEN, GRID_W, 128), extra


def _key_window(j):
    ws = jnp.clip(Q_ROWS * j - 4, 0, GRID_ROWS - KEY_ROWS)
    return ws, pl.multiple_of(CTX_LEN + ws * GRID_W, 256)


def _head_mask(hh):
    lane = lax.broadcasted_iota(jnp.int32, (Q_TILE, 128), 1)
    return (lane < HEAD_DIM) if hh == 0 else (lane >= HEAD_DIM)


def _attn_scores(j, ws, q_rot_h, q_pl_h, kw, kc, hh, bt_ref, s_ref):
    s_ref[:, :KEY_TILE] = _dot_nt(q_rot_h, kw)
    s_ref[:, KEY_TILE:] = _dot_nt(q_pl_h, kc)
    lane = lax.broadcasted_iota(jnp.int32, (GRID_W, 128), 1)
    base = ws - Q_ROWS * j + (NA_ROWS - 1) + BT_PAD
    for qi in range(Q_ROWS):
        rs = jnp.clip(Q_ROWS * j + qi - NA_ROWS // 2, 0, GRID_ROWS - NA_ROWS)
        for m in range(KEY_ROWS // 2):
            k0 = ws + 2 * m
            p0 = jnp.where((k0 >= rs) & (k0 < rs + NA_ROWS), 0.0, NEG_INF)
            p1 = jnp.where((k0 + 1 >= rs) & (k0 + 1 < rs + NA_ROWS), 0.0, NEG_INF)
            pen = jnp.where(lane < GRID_W, p0, p1)
            rows = slice(qi * GRID_W, (qi + 1) * GRID_W)
            cols = slice(128 * m, 128 * (m + 1))
            s_ref[rows, cols] = s_ref[rows, cols] + bt_ref[hh, base + 2 * m - qi] + pen
    return base


def attn_fwd(q_rot, q_pl, kk, vv, bt, comm=None):
    def body(qr_ref, qp_ref, kk_ref, vv_ref, bt_ref, o_ref, lse_ref, s_ref):
        j = pl.program_id(1)
        ws, start = _key_window(j)
        win = pl.ds(start, KEY_TILE)
        kw, kc = kk_ref[win, :], kk_ref[:CTX_LEN, :]
        vw, vc = vv_ref[win, :], vv_ref[:CTX_LEN, :]
        qr, qp = qr_ref[...], qp_ref[...]
        outs = []
        for hh in range(2):
            msk = _head_mask(hh)
            _attn_scores(j, ws, jnp.where(msk, qr, 0), jnp.where(msk, qp, 0), kw, kc, hh, bt_ref, s_ref)
            s = s_ref[...]
            mx = jnp.max(s, axis=-1, keepdims=True)
            pr = jnp.exp(s - mx)
            l = jnp.sum(pr, axis=-1, keepdims=True)
            prb = pr.astype(BF16)
            o = jnp.dot(prb[:, :KEY_TILE], vw, preferred_element_type=F32)
            o = o + jnp.dot(prb[:, KEY_TILE:], vc, preferred_element_type=F32)
            outs.append(o / l)
            lse_ref[hh] = mx + jnp.log(l)
        o_ref[...] = jnp.where(_head_mask(0), outs[0], outs[1])

    qspec = pl.BlockSpec((Q_TILE, 128), lambda hp, j: (j + 1, hp))
    kspec = pl.BlockSpec((ZLEN, 128), lambda hp, j: (0, hp))
    res, extra = _call(
        body, name="attn_fwd", grid=(NA_HEADS // 2, SEQ // Q_TILE),
        in_specs=[qspec, qspec, kspec, kspec, pl.BlockSpec((2, BT_LEN, GRID_W, 128), lambda hp, j: (hp, 0, 0, 0))],
        out_specs=[pl.BlockSpec((Q_TILE, 128), lambda hp, j: (j, hp)),
                   pl.BlockSpec((2, Q_TILE, 1), lambda hp, j: (hp, j, 0))],
        out_shape=[jax.ShapeDtypeStruct((SEQ, D_MODEL), F32), jax.ShapeDtypeStruct((NA_HEADS, SEQ, 1), F32)],
        scratch_shapes=[pltpu.VMEM((Q_TILE, KEY_TILE + CTX_LEN), F32)], sem=("parallel", "arbitrary"),
        args=(q_rot, q_pl, kk, vv, bt), comm=comm)
    return res[0], res[1], extra


def merge_fwd(y_rnn, y_na, p, z, g2, w_rnn, w_na, w_out):
    def body(yr_ref, yn_ref, mr_ref, mn_ref, x_ref, g2_ref, wr_ref, wn_ref, wo_ref, u_ref, v_ref, mg_ref, out_ref, x1_ref):
        u = jnp.dot(yr_ref[...], wr_ref[...], preferred_element_type=F32)
        v = jnp.dot(yn_ref[...].astype(BF16), wn_ref[...], preferred_element_type=F32)
        merged = (_sigmoid(mr_ref[...]) * u + _sigmoid(mn_ref[...]) * v).astype(BF16)
        out = jnp.dot(merged, wo_ref[...], preferred_element_type=F32)
        u_ref[...] = u
        v_ref[...] = v
        mg_ref[...] = merged
        out_ref[...] = out
        x1_ref[...] = x_ref[...] + g2_ref[...] * out

    row = pl.BlockSpec((ROW_TILE, D_MODEL), lambda i: (i, 0))
    lat = lambda cb: pl.BlockSpec((ROW_TILE, D_MODEL), lambda i: (i + 1, cb))
    wspec = _full((D_MODEL, D_MODEL))
    f32o = jax.ShapeDtypeStruct((SEQ, D_MODEL), F32)
    return pl.pallas_call(
        body, name="merge_fwd", grid=(SEQ // ROW_TILE,),
        in_specs=[row, row, lat(5), lat(6), lat(0), _full((1, D_MODEL)), wspec, wspec, wspec],
        out_specs=[row] * 5,
        out_shape=[f32o, f32o, jax.ShapeDtypeStruct((SEQ, D_MODEL), BF16), f32o, f32o],
        compiler_params=_params("parallel"),
    )(y_rnn, y_na, p, p, z, g2, w_rnn, w_na, w_out)


FF_TILE = 256
FF_TILES = D_FF // FF_TILE


FF_ROWS = 64
FF_HALO = 8
FF_SLAB = FF_ROWS + 2 * FF_HALO


def _ffn_row_chunks(chunk, init):
    carry = chunk(0, 0, -1, init)
    carry = lax.fori_loop(1, SEQ // FF_ROWS - 1,
                          lambda ci, cr: chunk(pl.multiple_of(ci * FF_ROWS - FF_HALO, 8), FF_HALO, 0, cr), carry)
    return chunk(SEQ - FF_SLAB, 2 * FF_HALO, 1, carry)


def _ffn_shifts(edge):
    row = _row_ids(FF_SLAB, FF_TILE)

    def prev(x):
        r = pltpu.roll(x, 1, 0)
        return jnp.where(row >= 1, r, 0.0) if edge == -1 else r

    def nxt(x):
        r = pltpu.roll(x, FF_SLAB - 1, 0)
        return jnp.where(row < FF_SLAB - 1, r, 0.0) if edge == 1 else r

    return prev, nxt


def ffn_act(hpre, conv_w, conv_b):
    def body(ha_ref, hg_ref, wa_ref, wg_ref, ba_ref, bg_ref, o_ref):
        wa, wg, ba, bg = wa_ref[...], wg_ref[...], ba_ref[...], bg_ref[...]

        def chunk(lo, mid, edge, carry):
            prev, nxt = _ffn_shifts(edge)
            ha, hg = ha_ref[pl.ds(lo, FF_SLAB), :], hg_ref[pl.ds(lo, FF_SLAB), :]
            a = prev(ha) * wa[0:1] + ha * wa[1:2] + nxt(ha) * wa[2:3] + ba
            g = prev(hg) * wg[0:1] + hg * wg[1:2] + nxt(hg) * wg[2:3] + bg
            o_ref[pl.ds(lo + mid, FF_ROWS), :] = (a * _sigmoid(a) * g)[mid:mid + FF_ROWS].astype(BF16)
            return carry

        _ffn_row_chunks(chunk, 0)

    col = lambda rows, off: pl.BlockSpec((rows, FF_TILE), lambda j: (0, j + off))
    return pl.pallas_call(
        body, name="ffn_act", grid=(FF_TILES,),
        in_specs=[col(SEQ, 0), col(SEQ, FF_TILES), col(3, 0), col(3, FF_TILES), col(1, 0), col(1, FF_TILES)],
        out_specs=col(SEQ, 0),
        out_shape=jax.ShapeDtypeStruct((SEQ, D_FF), BF16),
        compiler_params=_params("parallel"),
    )(hpre, hpre, conv_w, conv_w, conv_b, conv_b)


def ffn_down_loss(act, w_down, x1, g5, target):
    def body(a_ref, w_ref, x1_ref, g5_ref, t_ref, f_ref, dy_ref, df_ref, ls_ref, dg_ref):
        i = pl.program_id(0)
        f = jnp.dot(a_ref[...], w_ref[...], preferred_element_type=F32)
        g5 = g5_ref[...]
        err = x1_ref[...] + g5 * f - t_ref[...]
        dy = err * (1.0 / D_MODEL)
        f_ref[...] = f
        dy_ref[...] = dy
        df_ref[...] = (dy * g5).astype(BF16)

        @pl.when(i == 0)
        def _():
            ls_ref[...] = jnp.zeros_like(ls_ref)
            dg_ref[...] = jnp.zeros_like(dg_ref)

        ls_ref[...] = ls_ref[...] + jnp.sum(err * err)
        dg_ref[...] = dg_ref[...] + jnp.sum(dy * f, axis=0, keepdims=True)

    row = pl.BlockSpec((ROW_TILE, D_MODEL), lambda i: (i, 0))
    f32o = jax.ShapeDtypeStruct((SEQ, D_MODEL), F32)
    return pl.pallas_call(
        body, name="ffn_down_loss", grid=(SEQ // ROW_TILE,),
        in_specs=[pl.BlockSpec((ROW_TILE, D_FF), lambda i: (i, 0)), _full((D_FF, D_MODEL)), row, _full((1, D_MODEL)), row],
        out_specs=[row, row, row, _full((8, 128)), _full((1, D_MODEL))],
        out_shape=[f32o, f32o, jax.ShapeDtypeStruct((SEQ, D_MODEL), BF16), jax.ShapeDtypeStruct((8, 128), F32),
                   jax.ShapeDtypeStruct((1, D_MODEL), F32)],
        compiler_params=_params("arbitrary"),
    )(act, w_down, x1, g5, target)


def ffn_down_bwd(df, w_down):
    def body(df_ref, w_ref, o_ref):
        o_ref[...] = _dot_nt(df_ref[...], w_ref[...])

    return pl.pallas_call(
        body, name="ffn_down_bwd", grid=(SEQ // ROW_TILE,),
        in_specs=[pl.BlockSpec((ROW_TILE, D_MODEL), lambda i: (i, 0)), _full((D_FF, D_MODEL))],
        out_specs=pl.BlockSpec((ROW_TILE, D_FF), lambda i: (i, 0)),
        out_shape=jax.ShapeDtypeStruct((SEQ, D_FF), F32),
        compiler_params=_params("parallel"),
    )(df, w_down)


def ffn_act_bwd(hpre, d_act, conv_w, conv_b):
    def body(ha_ref, hg_ref, da_ref, wa_ref, wg_ref, ba_ref, bg_ref, dha_ref, dhg_ref, dwa_ref, dwg_ref, dba_ref, dbg_ref):
        wa, wg, ba, bg = wa_ref[...], wg_ref[...], ba_ref[...], bg_ref[...]

        def chunk(lo, mid, edge, acc):
            prev, nxt = _ffn_shifts(edge)
            rows = pl.ds(lo, FF_SLAB)
            ha, hg, dact = ha_ref[rows, :], hg_ref[rows, :], da_ref[rows, :]
            hap, han, hgp, hgn = prev(ha), nxt(ha), prev(hg), nxt(hg)
            a = hap * wa[0:1] + ha * wa[1:2] + han * wa[2:3] + ba
            g = hgp * wg[0:1] + hg * wg[1:2] + hgn * wg[2:3] + bg
            sig = _sigmoid(a)
            dca = dact * g * (sig * (1.0 + a * (1.0 - sig)))
            dcg = dact * a * sig
            m = slice(mid, mid + FF_ROWS)
            sums = []
            for dc, h, hp, hn, w, dh_ref in ((dca, ha, hap, han, wa, dha_ref), (dcg, hg, hgp, hgn, wg, dhg_ref)):
                dcm = dc[m]
                sums += [jnp.sum(dcm * hp[m], axis=0, keepdims=True), jnp.sum(dcm * h[m], axis=0, keepdims=True),
                         jnp.sum(dcm * hn[m], axis=0, keepdims=True), jnp.sum(dcm, axis=0, keepdims=True)]
                dh = nxt(dc) * w[0:1] + dc * w[1:2] + prev(dc) * w[2:3]
                dh_ref[pl.ds(lo + mid, FF_ROWS), :] = dh[m].astype(BF16)
            return tuple(x + y for x, y in zip(acc, sums))

        acc = _ffn_row_chunks(chunk, tuple(jnp.zeros((1, FF_TILE), F32) for _ in range(8)))
        dwa_ref[0:1, :], dwa_ref[1:2, :], dwa_ref[2:3, :], dba_ref[...] = acc[0], acc[1], acc[2], acc[3]
        dwg_ref[0:1, :], dwg_ref[1:2, :], dwg_ref[2:3, :], dbg_ref[...] = acc[4], acc[5], acc[6], acc[7]

    col = lambda rows, off: pl.BlockSpec((rows, FF_TILE), lambda j: (0, j + off))
    hshape = jax.ShapeDtypeStruct((SEQ, D_FF), BF16)
    wshape = jax.ShapeDtypeStruct((3, D_FF), F32)
    bshape = jax.ShapeDtypeStruct((1, D_FF), F32)
    return pl.pallas_call(
        body, name="ffn_act_bwd", grid=(FF_TILES,),
        in_specs=[col(SEQ, 0), col(SEQ, FF_TILES), col(SEQ, 0), col(3, 0), col(3, FF_TILES), col(1, 0), col(1, FF_TILES)],
        out_specs=[col(SEQ, 0), col(SEQ, 0), col(3, 0), col(3, 0), col(1, 0), col(1, 0)],
        out_shape=[hshape, hshape, wshape, wshape, bshape, bshape],
        compiler_params=_params("parallel"),
    )(hpre, hpre, d_act, conv_w, conv_w, conv_b, conv_b)


def _norm_mod_bwd(x, dxn, gain, scale):
    rstd = lax.rsqrt(jnp.mean(x * x, axis=-1, keepdims=True) + EPS)
    nrm = x * rstd
    dsh = jnp.sum(dxn, axis=0, keepdims=True)
    dsc = jnp.sum(dxn * nrm, axis=0, keepdims=True) * gain
    dgn = jnp.sum(dxn * nrm, axis=0, keepdims=True) * (1.0 + scale)
    dn = dxn * (gain * (1.0 + scale))
    dx = rstd * (dn - nrm * jnp.mean(dn * nrm, axis=-1, keepdims=True))
    return dx, dsh, dsc, dgn


def ffn_up_bwd(dha, dhg, w_up, x1, dy, gain, scale):
    def body(dha_ref, dhg_ref, w_ref, x_ref, dy_ref, g_ref, sc_ref, dx_ref, dsh_ref, dsc_ref, dgn_ref):
        i = pl.program_id(0)
        dxn = _dot_nt(dha_ref[...], w_ref[:, :D_FF]) + _dot_nt(dhg_ref[...], w_ref[:, D_FF:])
        dx, dsh, dsc, dgn = _norm_mod_bwd(x_ref[...], dxn, g_ref[...], sc_ref[...])
        dx_ref[...] = dy_ref[...] + dx

        @pl.when(i == 0)
        def _():
            dsh_ref[...] = dsh
            dsc_ref[...] = dsc
            dgn_ref[...] = dgn

        @pl.when(i > 0)
        def _():
            dsh_ref[...] = dsh_ref[...] + dsh
            dsc_ref[...] = dsc_ref[...] + dsc
            dgn_ref[...] = dgn_ref[...] + dgn

    row = pl.BlockSpec((ROW_TILE, D_MODEL), lambda i: (i, 0))
    vec = _full((1, D_MODEL))
    vshape = jax.ShapeDtypeStruct((1, D_MODEL), F32)
    return pl.pallas_call(
        body, name="ffn_up_bwd", grid=(SEQ // ROW_TILE,),
        in_specs=[pl.BlockSpec((ROW_TILE, D_FF), lambda i: (i, 0)), pl.BlockSpec((ROW_TILE, D_FF), lambda i: (i, 0)),
                  _full((D_MODEL, 2 * D_FF)), row, row, vec, vec],
        out_specs=[row, vec, vec, vec],
        out_shape=[jax.ShapeDtypeStruct((SEQ, D_MODEL), F32), vshape, vshape, vshape],
        compiler_params=_params("arbitrary"),
    )(dha, dhg, w_up, x1, dy, gain, scale)


def merge_bwd(dx1, out, g2, p, u, v, w_rnn, w_na, w_out, comm=None):
    def body(dx_ref, out_ref, g2_ref, mr_ref, mn_ref, u_ref, v_ref, wr_ref, wn_ref, wo_ref,
             dout_ref, du_ref, dv_ref, dmr_ref, dmn_ref, dyr_ref, dyn_ref, dg2_ref):
        i = pl.program_id(0)

        @pl.when(i == 0)
        def _():
            dmr_ref[...] = jnp.zeros_like(dmr_ref)
            dmn_ref[...] = jnp.zeros_like(dmn_ref)
            dg2_ref[...] = jnp.zeros_like(dg2_ref)

        @pl.when(i > 0)
        def _():
            dx = dx_ref[...]
            dg2_ref[...] = dg2_ref[...] + jnp.sum(dx * out_ref[...], axis=0, keepdims=True)
            dout = (dx * g2_ref[...]).astype(BF16)
            dout_ref[...] = dout
            dm = _dot_nt(dout, wo_ref[...])
            sr = _sigmoid(mr_ref[...])
            sn = _sigmoid(mn_ref[...])
            du = (dm * sr).astype(BF16)
            dv = (dm * sn).astype(BF16)
            du_ref[...] = du
            dv_ref[...] = dv
            dmr_ref[...] = (dm * u_ref[...] * (sr * (1.0 - sr))).astype(BF16)
            dmn_ref[...] = (dm * v_ref[...] * (sn * (1.0 - sn))).astype(BF16)
            dyr_ref[...] = _dot_nt(du, wr_ref[...])
            dyn_ref[...] = _dot_nt(dv, wn_ref[...])

    lat = pl.BlockSpec((ROW_TILE, D_MODEL), lambda i: (jnp.maximum(i - 1, 0), 0))
    zrow = pl.BlockSpec((ROW_TILE, D_MODEL), lambda i: (i, 0))
    pcol = lambda cb: pl.BlockSpec((ROW_TILE, D_MODEL), lambda i: (i, cb))
    wspec = _full((D_MODEL, D_MODEL))
    tb = jax.ShapeDtypeStruct((SEQ, D_MODEL), BF16)
    zb = jax.ShapeDtypeStruct((ZLEN, D_MODEL), BF16)
    tf = jax.ShapeDtypeStruct((SEQ, D_MODEL), F32)
    res, extra = _call(
        body, name="merge_bwd", grid=(ZLEN // ROW_TILE,),
        in_specs=[lat, lat, _full((1, D_MODEL)), pcol(5), pcol(6), lat, lat, wspec, wspec, wspec],
        out_specs=[lat, lat, lat, zrow, zrow, lat, lat, _full((1, D_MODEL))],
        out_shape=[tb, tb, tb, zb, zb, tf, tf, jax.ShapeDtypeStruct((1, D_MODEL), F32)],
        sem=("arbitrary",), args=(dx1, out, g2, p, p, u, v, w_rnn, w_na, w_out), comm=comm)
    return (*res, extra)


def attn_bwd(q_rot, q_pl, kk, vv, bt, y_na, d_yna, lse, comm=None):
    def body(qr_ref, qp_ref, kk_ref, vv_ref, bt_ref, o_ref, do_ref, lse_ref,
             dqr_ref, dqp_ref, dk_ref, dv_ref, dbt_ref, s_ref):
        jj = pl.program_id(1)

        @pl.when(jj == 0)
        def _():
            dqr_ref[...] = jnp.zeros_like(dqr_ref)
            dqp_ref[...] = jnp.zeros_like(dqp_ref)
            dk_ref[...] = jnp.zeros_like(dk_ref)
            dv_ref[...] = jnp.zeros_like(dv_ref)
            dbt_ref[...] = jnp.zeros_like(dbt_ref)

        @pl.when(jj > 0)
        def _():
            j = jj - 1
            ws, start = _key_window(j)
            win = pl.ds(start, KEY_TILE)
            kw, kc = kk_ref[win, :], kk_ref[:CTX_LEN, :]
            vw, vc = vv_ref[win, :], vv_ref[:CTX_LEN, :]
            qr, qp = qr_ref[...], qp_ref[...]
            do = do_ref[...]
            do_o = do * o_ref[...]
            dq_r, dq_p = [], []
            for hh in range(2):
                msk = _head_mask(hh)
                q_r, q_p = jnp.where(msk, qr, 0), jnp.where(msk, qp, 0)
                base = _attn_scores(j, ws, q_r, q_p, kw, kc, hh, bt_ref, s_ref)
                pr = jnp.exp(s_ref[...] - lse_ref[hh])
                delta = jnp.sum(jnp.where(msk, do_o, 0.0), axis=-1, keepdims=True)
                dob = jnp.where(msk, do, 0.0).astype(BF16)
                ds_lat = pr[:, :KEY_TILE] * (_dot_nt(dob, vw) - delta)
                ds_ctx = pr[:, KEY_TILE:] * (_dot_nt(dob, vc) - delta)
                for qi in range(Q_ROWS):
                    for m in range(KEY_ROWS // 2):
                        idx = base + 2 * m - qi
                        dbt_ref[hh, idx] = dbt_ref[hh, idx] + ds_lat[qi * GRID_W:(qi + 1) * GRID_W, 128 * m:128 * (m + 1)]
                dsb_lat = ds_lat.astype(BF16)
                dsb_ctx = ds_ctx.astype(BF16)
                prb = pr.astype(BF16)
                dq_r.append(jnp.dot(dsb_lat, kw, preferred_element_type=F32))
                dq_p.append(jnp.dot(dsb_ctx, kc, preferred_element_type=F32))
                dk_ref[win, :] = dk_ref[win, :] + _dot_tn(dsb_lat, q_r)
                dk_ref[:CTX_LEN, :] = dk_ref[:CTX_LEN, :] + _dot_tn(dsb_ctx, q_p)
                dv_ref[win, :] = dv_ref[win, :] + _dot_tn(prb[:, :KEY_TILE], dob)
                dv_ref[:CTX_LEN, :] = dv_ref[:CTX_LEN, :] + _dot_tn(prb[:, KEY_TILE:], dob)
            dqr_ref[...] = jnp.where(_head_mask(0), dq_r[0], dq_r[1])
            dqp_ref[...] = jnp.where(_head_mask(0), dq_p[0], dq_p[1])

    lat = lambda jj: jnp.maximum(jj - 1, 0)
    qspec = pl.BlockSpec((Q_TILE, 128), lambda hp, jj: (lat(jj) + 1, hp))
    kspec = pl.BlockSpec((ZLEN, 128), lambda hp, jj: (0, hp))
    btspec = pl.BlockSpec((2, BT_LEN, GRID_W, 128), lambda hp, jj: (hp, 0, 0, 0))
    ospec = pl.BlockSpec((Q_TILE, 128), lambda hp, jj: (lat(jj), hp))
    dqspec = pl.BlockSpec((Q_TILE, 128), lambda hp, jj: (jj, hp))
    zshape = jax.ShapeDtypeStruct((ZLEN, D_MODEL), F32)
    res, extra = _call(
        body, name="attn_bwd", grid=(NA_HEADS // 2, ZLEN // Q_TILE),
        in_specs=[qspec, qspec, kspec, kspec, btspec, ospec, ospec,
                  pl.BlockSpec((2, Q_TILE, 1), lambda hp, jj: (hp, lat(jj), 0))],
        out_specs=[dqspec, dqspec, kspec, kspec, btspec],
        out_shape=[zshape, zshape, zshape, zshape, jax.ShapeDtypeStruct((NA_HEADS, BT_LEN, GRID_W, 128), F32)],
        scratch_shapes=[pltpu.VMEM((Q_TILE, KEY_TILE + CTX_LEN), F32)], sem=("parallel", "arbitrary"),
        args=(q_rot, q_pl, kk, vv, bt, y_na, d_yna, lse), comm=comm)
    return (*res, extra)


def qkv_bwd(dq_rot, dq_pl, dk, dv, p, qg2, kg2, cos, sin, ones, comm=None):
    scale = HEAD_DIM ** -0.5
    n_hp, n_i = NA_HEADS // 2, ZLEN // PREP_TILE

    def norm_rope_bwd(d_rot, d_extra, x, gain, cos_t, sin_t, ones_m, dx_ref, acc_ref):
        xh, rstd = _head_rms(x, ones_m, 1.0)
        dn = d_rot * cos_t + _rope_partner(d_rot * sin_t)
        if d_extra is not None:
            dn = (dn + d_extra) * scale
        acc_ref[...] = acc_ref[...] + jnp.sum(dn * xh, axis=0, keepdims=True)
        dxh = dn * gain
        seg = _head_sum(dxh * xh, ones_m) * (1.0 / HEAD_DIM)
        dx_ref[...] = (rstd * (dxh - xh * seg)).astype(BF16)

    def body(dqr_ref, dqp_ref, dk_ref, dv_ref, xq_ref, xk_ref, qg_ref, kg_ref, cos_ref, sin_ref, ones_ref,
             dxq_ref, dxk_ref, dxv_ref, dgq_ref, dgk_ref, accq_ref, acck_ref):
        hp, i = pl.program_id(0), pl.program_id(1)

        @pl.when((hp == 0) & (i == 0))
        def _():
            accq_ref[...] = jnp.zeros_like(accq_ref)
            acck_ref[...] = jnp.zeros_like(acck_ref)

        ones_m = ones_ref[...]
        cos_t, sin_t = cos_ref[...], sin_ref[...]
        norm_rope_bwd(dqr_ref[...], dqp_ref[...], xq_ref[...], qg_ref[...], cos_t, sin_t, ones_m, dxq_ref, accq_ref)
        norm_rope_bwd(dk_ref[...], None, xk_ref[...], kg_ref[...], cos_t, sin_t, ones_m, dxk_ref, acck_ref)
        dxv_ref[...] = dv_ref[...].astype(BF16)

        @pl.when((hp == n_hp - 1) & (i == n_i - 1))
        def _():
            dgq_ref[...] = accq_ref[:, :HEAD_DIM] + accq_ref[:, HEAD_DIM:]
            dgk_ref[...] = acck_ref[:, :HEAD_DIM] + acck_ref[:, HEAD_DIM:]

    col = lambda base: pl.BlockSpec((PREP_TILE, 128), lambda hp, i: (i, base + hp))
    small = pl.BlockSpec((1, 128), lambda hp, i: (0, 0))
    tab = pl.BlockSpec((PREP_TILE, 128), lambda hp, i: (i, 0))
    zb = jax.ShapeDtypeStruct((ZLEN, D_MODEL), BF16)
    gshape = jax.ShapeDtypeStruct((1, HEAD_DIM), F32)
    res, extra = _call(
        body, name="qkv_bwd", grid=(n_hp, n_i),
        in_specs=[col(0)] * 4 + [col(32), col(8), small, small, tab, tab, _full((128, 128))],
        out_specs=[col(0)] * 3 + [_full((1, HEAD_DIM))] * 2,
        out_shape=[zb, zb, zb, gshape, gshape],
        scratch_shapes=[pltpu.VMEM((1, 128), F32)] * 2, sem=("arbitrary", "arbitrary"),
        args=(dq_rot, dq_pl, dk, dv, p, p, qg2, kg2, cos, sin, ones), comm=comm)
    return (*res, extra)


def rpb_grad(dbt):
    e, _ = _bias_expand()
    n_dr = 2 * NA_ROWS - 1
    ea = np.zeros((31, GRID_W, 128), np.float32)
    ea[:, :, :GRID_W] = e
    eb = np.zeros((31, GRID_W, 128), np.float32)
    eb[:, :, GRID_W:] = e
    eat = jnp.asarray(ea.reshape(31, GRID_W * 128).T.copy())
    ebt = jnp.asarray(eb.reshape(31, GRID_W * 128).T.copy())
    sel_at = np.zeros((n_dr, BT_LEN), np.float32)
    sel_bt = np.zeros((n_dr, BT_LEN), np.float32)
    for r in range(BT_LEN):
        dr = r - BT_PAD
        if 0 <= dr < n_dr:
            sel_at[dr, r] = 1.0
        if 0 <= dr + 1 < n_dr:
            sel_bt[dr + 1, r] = 1.0
    hi = lax.Precision.HIGHEST

    tk = 2048
    wide = GRID_W * 128
    rows = NA_HEADS * BT_LEN
    n_k = wide // tk

    def body(d_ref, sa_ref, sb_ref, ea_ref, eb_ref, o_ref, a_s, b_s):
        k = pl.program_id(0)
        dm = d_ref[...]
        d_hi = dm.astype(BF16)
        rest = dm - d_hi.astype(F32)
        d_mid = rest.astype(BF16)
        d_lo = (rest - d_mid.astype(F32)).astype(BF16)
        ea_b, eb_b = ea_ref[...].astype(BF16), eb_ref[...].astype(BF16)
        a = sum(jnp.dot(t, ea_b, preferred_element_type=F32) for t in (d_hi, d_mid, d_lo))
        b = sum(jnp.dot(t, eb_b, preferred_element_type=F32) for t in (d_hi, d_mid, d_lo))

        @pl.when(k == 0)
        def _():
            a_s[...] = a
            b_s[...] = b

        @pl.when(k > 0)
        def _():
            a_s[...] = a_s[...] + a
            b_s[...] = b_s[...] + b

        @pl.when(k == n_k - 1)
        def _():
            for h in range(NA_HEADS):
                sl = slice(h * BT_LEN, (h + 1) * BT_LEN)
                o_ref[h] = (jnp.dot(sa_ref[...], a_s[sl, :], preferred_element_type=F32, precision=hi)
                            + jnp.dot(sb_ref[...], b_s[sl, :], preferred_element_type=F32, precision=hi))

    return pl.pallas_call(
        body, name="rpb_grad", grid=(n_k,),
        in_specs=[pl.BlockSpec((rows, tk), lambda k: (0, k)), _full((n_dr, BT_LEN)), _full((n_dr, BT_LEN)),
                  pl.BlockSpec((tk, 31), lambda k: (k, 0)), pl.BlockSpec((tk, 31), lambda k: (k, 0))],
        out_specs=_full((NA_HEADS, n_dr, 31)),
        out_shape=jax.ShapeDtypeStruct((NA_HEADS, n_dr, 31), F32),
        scratch_shapes=[pltpu.VMEM((rows, 31), F32), pltpu.VMEM((rows, 31), F32)],
        compiler_params=_params("arbitrary"),
    )(dbt.reshape(rows, wide), jnp.asarray(sel_at), jnp.asarray(sel_bt), eat, ebt)


def lru_bwd(p, d_yrnn, conv_w, conv_b, wa, ba, wx, bx, lam, comm=None):
    blk, wspec = _lru_in_specs()

    def body(xr_ref, gx_ref, dy_ref, cw_ref, cb_ref, wa_ref, ba_ref, wx_ref, bx_ref, lam_ref,
             dxr_ref, dgx_ref, dcw_ref, dcb_ref, dwa_ref, dba_ref, dwx_ref, dbx_ref, dlam_ref,
             a_s, b_s, h_s, l_s, hsum_s, dxc_s, dh_s):
        xr = xr_ref[...]
        cw = cw_ref[...]
        xc = _lru_conv(xr, cw, cb_ref[...])
        xcb = xc.astype(BF16)
        g, dg = _gelu_parts(gx_ref[CTX_LEN:, :])
        dy = dy_ref[...]
        dh_s[:CTX_LEN, :] = jnp.zeros((CTX_LEN, LRU_BLOCK_W), F32)
        dh_s[CTX_LEN:, :] = dy * g
        row = _row_ids(ZLEN, LRU_BLOCK_W)
        zero = jnp.zeros((1, LRU_BLOCK_W), F32)
        for d in (0, 1):
            wab = wa_ref[d, 0].astype(BF16)
            wxb = wx_ref[d, 0].astype(BF16)
            lam_d = lam_ref[d:d + 1, :]
            r, gi, sp, a, sq, b = _lru_gates(xc, xcb, wab, ba_ref[d:d + 1, :], wxb, bx_ref[d:d + 1, :], lam_d)
            a_s[...] = a
            b_s[...] = b
            _lru_scan_dir(d, a_s, b_s, h_s)
            h = h_s[...]
            if d == 0:
                hsum_s[...] = h
                h_prev = jnp.where(row >= 1, pltpu.roll(h, 1, 0), 0.0)
                a_s[...] = pltpu.roll(a, ZLEN - 1, 0)
                _scan_down(a_s, dh_s, l_s, 0, N_CHUNK, zero)
            else:
                hsum_s[...] = hsum_s[...] + h
                h_prev = jnp.where(row == CTX_LEN - 1, 0.0, pltpu.roll(h, ZLEN - 1, 0))
                a_s[...] = pltpu.roll(a, 1, 0)
                c = _scan_up(a_s, dh_s, l_s, CTX_CHUNKS, N_CHUNK, zero)
                _scan_up(a_s, dh_s, l_s, 0, CTX_CHUNKS, c)
            db = l_s[...]
            da = db * h_prev
            dsq = db * gi * xc
            dgi = db * sq * xc
            dxc_d = db * sq * gi
            dla = da * a - dsq * (a * a) / sq
            dr = dla * ((-LRU_C) * sp)
            dsp = jnp.sum(dla * ((-LRU_C) * r), axis=0, keepdims=True)
            dlam_ref[d:d + 1, :] = -dsp * _sigmoid(-lam_d)
            dzr = dr * r * (1.0 - r)
            dzi = dgi * gi * (1.0 - gi)
            dba_ref[d:d + 1, :] = jnp.sum(dzr, axis=0, keepdims=True)
            dbx_ref[d:d + 1, :] = jnp.sum(dzi, axis=0, keepdims=True)
            dzrb = dzr.astype(BF16)
            dzib = dzi.astype(BF16)
            dwa_ref[d, 0] = _dot_tn(xcb, dzrb)
            dwx_ref[d, 0] = _dot_tn(xcb, dzib)
            dxc_d = dxc_d + _dot_nt(dzrb, wab) + _dot_nt(dzib, wxb)
            if d == 0:
                dxc_s[...] = dxc_d
            else:
                dxc_s[...] = dxc_s[...] + dxc_d
        dxc = dxc_s[...]
        dxr_ref[...] = _lru_conv_t(dxc, cw).astype(BF16)
        dcb_ref[...] = jnp.sum(dxc, axis=0, keepdims=True)
        segpos = jnp.where(row < CTX_LEN, row, row - CTX_LEN)
        seglen = jnp.where(row < CTX_LEN, CTX_LEN, SEQ)
        for k in range(4):
            off = k - 2
            if off == 0:
                sh = xr
            else:
                ok = (segpos + off >= 0) & (segpos + off < seglen)
                sh = jnp.where(ok, pltpu.roll(xr, (-off) % ZLEN, 0), 0.0)
            dcw_ref[k:k + 1, :] = jnp.sum(dxc * sh, axis=0, keepdims=True)
        dgx_ref[:CTX_LEN, :] = jnp.zeros((CTX_LEN, LRU_BLOCK_W), BF16)
        dgx_ref[CTX_LEN:, :] = (dy * hsum_s[CTX_LEN:, :] * dg).astype(BF16)

    zs = pltpu.VMEM((ZLEN, LRU_BLOCK_W), F32)
    zb = jax.ShapeDtypeStruct((ZLEN, D_MODEL), BF16)
    v2 = jax.ShapeDtypeStruct((2, D_MODEL), F32)
    w4 = jax.ShapeDtypeStruct((2, LRU_BLOCKS, LRU_BLOCK_W, LRU_BLOCK_W), F32)
    res, extra = _call(
        body, name="lru_bwd", grid=(LRU_BLOCKS,),
        in_specs=[blk(ZLEN), pl.BlockSpec((ZLEN, LRU_BLOCK_W), lambda b: (0, 24 + b)), blk(SEQ), blk(4), blk(1),
                  wspec, blk(2), wspec, blk(2), blk(2)],
        out_specs=[blk(ZLEN), blk(ZLEN), blk(4), blk(1), wspec, blk(2), wspec, blk(2), blk(2)],
        out_shape=[zb, zb, jax.ShapeDtypeStruct((4, D_MODEL), F32), jax.ShapeDtypeStruct((1, D_MODEL), F32),
                   w4, v2, w4, v2, v2],
        scratch_shapes=[zs] * 7, sem=("arbitrary",),
        args=(p, p, d_yrnn, conv_w, conv_b, wa, ba, wx, bx, lam), comm=comm)
    return (*res, extra)


def in_proj_bwd(dgs, w_in, z, dx1, gain, scale, comm=None):
    def body(*refs):
        dg_refs = refs[:7]
        w_ref, z_ref, dx1_ref, g_ref, sc_ref, gx_ref, dsh_ref, dsc_ref, dgn_ref = refs[7:]
        i = pl.program_id(0)
        dxn = _dot_nt(dg_refs[0][...], w_ref[:, 0:D_MODEL])
        for g in range(1, 7):
            dxn = dxn + _dot_nt(dg_refs[g][...], w_ref[:, g * D_MODEL:(g + 1) * D_MODEL])
        dx, dsh, dsc, dgn = _norm_mod_bwd(z_ref[...], dxn, g_ref[...], sc_ref[0])

        @pl.when(i <= 1)
        def _():
            dsh_ref[0] = dsh
            dsc_ref[0] = dsc

        @pl.when(i > 1)
        def _():
            dsh_ref[0] = dsh_ref[0] + dsh
            dsc_ref[0] = dsc_ref[0] + dsc

        @pl.when(i == 0)
        def _():
            dgn_ref[...] = dgn

        @pl.when(i > 0)
        def _():
            dgn_ref[...] = dgn_ref[...] + dgn
            gx_ref[...] = dx1_ref[...] + dx

    zrow = pl.BlockSpec((ROW_TILE, D_MODEL), lambda i: (i, 0))
    lat = pl.BlockSpec((ROW_TILE, D_MODEL), lambda i: (jnp.maximum(i - 1, 0), 0))
    mod = pl.BlockSpec((1, 1, D_MODEL), lambda i: (jnp.minimum(i, 1), 0, 0))
    mshape = jax.ShapeDtypeStruct((2, 1, D_MODEL), F32)
    res, extra = _call(
        body, name="in_proj_bwd", grid=(ZLEN // ROW_TILE,),
        in_specs=[zrow] * 7 + [_full((D_MODEL, IN_COLS)), zrow, lat, _full((1, D_MODEL)), mod],
        out_specs=[lat, mod, mod, _full((1, D_MODEL))],
        out_shape=[jax.ShapeDtypeStruct((SEQ, D_MODEL), F32), mshape, mshape, jax.ShapeDtypeStruct((1, D_MODEL), F32)],
        sem=("arbitrary",), args=(*dgs, w_in, z, dx1, gain, scale), comm=comm)
    return (*res, extra)


def matmul_tn(a, b, name, tm, tn, prev=None, col_block=0, total_cols=None):
    k, m = a.shape
    n = b.shape[1]
    total_cols = n if total_cols is None else total_cols
    assert m % tm == 0 and n % tn == 0
    off = col_block * (n // tn)

    def body(a_ref, b_ref, *rest):
        rest[-1][...] = _dot_tn(a_ref[...].astype(BF16), b_ref[...]).astype(BF16)

    in_specs = [pl.BlockSpec((k, tm), lambda i, j: (0, i)), pl.BlockSpec((k, tn), lambda i, j: (0, j))]
    args = [a, b]
    aliases = {}
    if prev is not None:
        in_specs.append(pl.BlockSpec(memory_space=pl.ANY))
        args.append(prev)
        aliases = {2: 0}
    return pl.pallas_call(
        body, name=name, grid=(m // tm, n // tn), in_specs=in_specs,
        out_specs=pl.BlockSpec((tm, tn), lambda i, j: (i, j + off)),
        out_shape=jax.ShapeDtypeStruct((m, total_cols), BF16),
        input_output_aliases=aliases,
        compiler_params=_params("parallel", "parallel"),
    )(*args)


def local_step(z, target, modx, modc, norm_mix_g, norm_ffn_g, w_in, conv_w, conv_b, wa, ba, wx, bx, lam, qg, kg, rpb,
               w_rnn, w_na, w_out, w_up, fconv_w, fconv_b, w_down, idx=None, bt=None):
    dist = idx is not None
    c_idx = idx[1:2] if dist else None
    d = D_MODEL
    mx = [modx[:, k * d:(k + 1) * d] for k in range(N_MOD)]
    shift = jnp.stack([modc[:, 0:d], mx[0]])
    scale = jnp.stack([modc[:, d:2 * d], mx[1]])
    cos, sin = _rope_tables()
    ones = _head_ones()
    qg2 = jnp.tile(qg, (1, 2))
    kg2 = jnp.tile(kg, (1, 2))

    xn = norm_mod(z, norm_mix_g, shift, scale, "norm_mix")
    if bt is None:
        bt, _ = bias_table(rpb)
    p, _ = matmul_wide(xn, w_in, "in_proj", 3 * ROW_TILE, 1792)
    y_rnn, got = lru_fwd(p, conv_w, conv_b, wa, ba, wx, bx, lam,
                         comm=gather_weights_comm([w_down], [5]) if dist else None)
    if dist:
        w_down = got[0]
    q_rot, q_pl, kk, vv, got = qkv_prep(p, qg2, kg2, cos, sin, ones,
                                        comm=gather_weights_comm([w_rnn, w_na, w_out], [1, 2, 3]) if dist else None)
    if dist:
        w_rnn, w_na, w_out = got
    y_na, lse, got = attn_fwd(q_rot, q_pl, kk, vv, bt, comm=gather_weights_comm([w_up], [4]) if dist else None)
    if dist:
        w_up = got[0]
    u, v, merged, out, x1 = merge_fwd(y_rnn, y_na, p, z, mx[2], w_rnn, w_na, w_out)
    xn2 = norm_mod(x1, norm_ffn_g, mx[3][None], mx[4][None], "norm_ffn")
    hpre, _ = matmul_wide(xn2, w_up, "ffn_up", 2 * ROW_TILE, 1408)
    act = ffn_act(hpre, fconv_w, fconv_b)
    f, dy, df, loss_sq, dg5 = ffn_down_loss(act, w_down, x1, mx[5], target)

    partials, pieces = {}, {}

    def views_of(which, grads):
        return [_grad_view(g, BIG[w][1], BIG[w][2]) for w, g in zip(which, grads)]

    def chip_partials(which, views, recv):
        for w, gv, r in zip(which, views, recv):
            partials[w] = add_halves(gv, r, c_idx, "add_halves_" + BIG[w][0])
        return scatter_pieces_comm([partials[w] for w in which], which)

    d_act = ffn_down_bwd(df, w_down)
    dha, dhg, d_fcw_a, d_fcw_g, d_fcb_a, d_fcb_g = ffn_act_bwd(hpre, d_act, fconv_w, fconv_b)
    d_fcw = jnp.concatenate([d_fcw_a, d_fcw_g], axis=1)
    d_fcb = jnp.concatenate([d_fcb_a, d_fcb_g], axis=1)
    dx1, d_s3, d_s4, d_gffn = ffn_up_bwd(dha, dhg, w_up, x1, dy, norm_ffn_g, mx[4])
    g_w_down = matmul_tn(act, df, "gw_down", 256, D_MODEL)
    g_w_up = matmul_tn(xn2, dha, "gw_up_a", 512, 1408, total_cols=2 * D_FF)
    g_w_up = matmul_tn(xn2, dhg, "gw_up_g", 512, 1408, prev=g_w_up, col_block=1, total_cols=2 * D_FF)
    v_ffn = views_of([4, 5], [g_w_up, g_w_down]) if dist else None
    *mb, got = merge_bwd(dx1, out, mx[2], p, u, v, w_rnn, w_na, w_out,
                         comm=exchange_halves_comm(v_ffn) if dist else None)
    dout, du, dv, dmr, dmn, dyr, dyn, dg2 = mb
    recv_ffn = got
    g_w_out = matmul_tn(merged, dout, "gw_out", 1024, 512)
    g_w_rnn = matmul_tn(y_rnn, du, "gw_rnn", 1024, 512)
    g_w_na = matmul_tn(y_na, dv, "gw_na", 1024, 512)
    v_mix = views_of([1, 2, 3], [g_w_rnn, g_w_na, g_w_out]) if dist else None
    *lru_grads, got = lru_bwd(p, dyr, conv_w, conv_b, wa, ba, wx, bx, lam,
                              comm=join_comms(chip_partials([4, 5], v_ffn, recv_ffn),
                                              exchange_halves_comm(v_mix)) if dist else None)
    dxr, dgx, d_cw, d_cb, d_wa, d_ba, d_wx, d_bx, d_lam = lru_grads
    if dist:
        pieces[4], pieces[5] = got[:2]
    lru_w_all = {}
    dqr, dqp, dk, dvh, dbt, got = attn_bwd(
        q_rot, q_pl, kk, vv, bt, y_na, dyn, lse,
        comm=join_comms(chip_partials([1, 2, 3], v_mix, got[2:]),
                        all_gather_comm(d_wa.reshape(-1, LRU_BLOCK_W))) if dist else None)
    if dist:
        pieces[1], pieces[2], pieces[3], lru_w_all["lru_wa"] = got
    dq_cols, dk_cols, dv_cols, d_qg, d_kg, got = qkv_bwd(
        dqr, dqp, dk, dvh, p, qg2, kg2, cos, sin, ones,
        comm=all_gather_comm(d_wx.reshape(-1, LRU_BLOCK_W)) if dist else None)
    if dist:
        lru_w_all["lru_wx"] = got[0]
    d_rpb = rpb_grad(dbt)
    dgs = [dxr, dk_cols, dv_cols, dgx, dq_cols, dmr, dmn]
    g_w_in = None
    for g in range(7):
        g_w_in = matmul_tn(xn, dgs[g], "gw_in_%d" % g, 1024, 512, prev=g_w_in, col_block=g, total_cols=IN_COLS)
    if dist:
        v_in = views_of([0], [g_w_in])
        recv_in = run_comm(exchange_halves_comm(v_in), "grad_exchange_w_in")
    grad_x, dsh, dsc, d_gmix, got = in_proj_bwd(dgs, w_in, z, dx1, norm_mix_g, scale,
                                                comm=chip_partials([0], v_in, recv_in) if dist else None)
    if dist:
        pieces[0] = got[0]

    d_modx = jnp.concatenate([dsh[1], dsc[1], dg2, d_s3, d_s4, dg5], axis=1)
    d_modc = jnp.concatenate([dsh[0], dsc[0]], axis=1)
    return dict(loss_sq=loss_sq, grad_x=grad_x, d_modx=d_modx, d_modc=d_modc, norm_mix_g=d_gmix, norm_ffn_g=d_gffn,
                w_in=g_w_in, lru_conv_w=d_cw, lru_conv_b=d_cb, lru_wa=d_wa, lru_ba=d_ba, lru_wx=d_wx, lru_bx=d_bx,
                lru_lambda=d_lam, q_norm_g=d_qg, k_norm_g=d_kg, na_rpb=d_rpb, w_rnn_out=g_w_rnn, w_na_out=g_w_na,
                w_out=g_w_out, w_up=g_w_up, ffn_conv_w=d_fcw, ffn_conv_b=d_fcb, w_down=g_w_down,
                partials=partials, pieces=pieces, lru_w_all=lru_w_all)


def _mesh_pos():
    return lax.axis_index("x"), lax.axis_index("y"), lax.axis_index("c")


def _other_chips(x, y):
    return [(1 - x, y), (x, 1 - y), (1 - x, 1 - y)]


BIG = (("w_in", (D_MODEL, IN_COLS), 1), ("w_rnn_out", (D_MODEL, D_MODEL), 0), ("w_na_out", (D_MODEL, D_MODEL), 0),
       ("w_out", (D_MODEL, D_MODEL), 0), ("w_up", (D_MODEL, 2 * D_FF), 1), ("w_down", (D_FF, D_MODEL), 0))


def _shard_shape(full, axis):
    r, c = full
    return (r // N_SHARD, c) if axis == 0 else (r, c // N_SHARD)


def _slot(ref, full, axis, s, h):
    r, c = full
    if axis == 0:
        rs = r // N_SHARD
        return ref.at[pl.ds(s * rs + h * (rs // 2), rs // 2), :]
    cs = c // N_SHARD
    return ref.at[pl.ds(h * (r // 2), r // 2), pl.ds(s * cs, cs)]


def cast_into_full(x, full, axis, idx, name):
    r, c = x.shape
    tr = next(t for t in (512, 352, 256, 128) if r % t == 0)
    nb = r // tr

    def body(idx_ref, x_ref, o_ref):
        o_ref[...] = x_ref[...].astype(BF16)

    if axis == 0:
        out_spec = pl.BlockSpec((tr, c), lambda i, idx_ref: (idx_ref[0] * nb + i, 0))
    else:
        out_spec = pl.BlockSpec((tr, c), lambda i, idx_ref: (i, idx_ref[0]))
    return pl.pallas_call(
        body, name=name,
        grid_spec=pltpu.PrefetchScalarGridSpec(
            num_scalar_prefetch=1, grid=(nb,), in_specs=[pl.BlockSpec((tr, c), lambda i, idx_ref: (i, 0))],
            out_specs=out_spec),
        out_shape=jax.ShapeDtypeStruct(full, BF16),
        compiler_params=_params("parallel"),
    )(idx, x)


def run_comm(comm, name):
    k_in, k_out = len(comm.inputs), len(comm.out_shapes)

    def body(*refs):
        start, mid, end = comm.emit(refs[:k_in], refs[k_in:k_in + k_out], refs[k_in + k_out:])
        start()
        mid()
        end()

    hbm = pl.BlockSpec(memory_space=pl.ANY)
    return pl.pallas_call(
        body, name=name, in_specs=[hbm] * k_in, out_specs=[hbm] * k_out, out_shape=list(comm.out_shapes),
        input_output_aliases=dict(comm.aliases), scratch_shapes=list(comm.scratch),
        compiler_params=pltpu.CompilerParams(vmem_limit_bytes=VMEM_LIMIT_V7X),
    )(*comm.inputs)


def gather_weights_comm(fulls, which):
    nw = len(which)
    specs = [BIG[w] for w in which]

    def emit(_, outs, sems):
        send1, recv1, send2, recv2 = sems
        x, y, c = _mesh_pos()
        sibling = (x, y, 1 - c)
        chips = _other_chips(x, y)
        s_me = 2 * x + y
        shards = [2 * chip[0] + chip[1] for chip in chips]

        def ici(w, j, shard):
            _, full, axis = specs[w]
            dst = _slot(outs[w], full, axis, shard, c)
            return pltpu.make_async_remote_copy(
                src_ref=dst, dst_ref=dst, send_sem=send1.at[3 * w + j],
                recv_sem=recv1.at[3 * w + j], device_id=(*chips[j], c), device_id_type=MESH_T)

        def d2d(w, j, shard, half):
            _, full, axis = specs[w]
            dst = _slot(outs[w], full, axis, shard, half)
            return pltpu.make_async_remote_copy(
                src_ref=dst, dst_ref=dst, send_sem=send2.at[3 * w + j], recv_sem=recv2.at[3 * w + j],
                device_id=sibling, device_id_type=MESH_T)

        pairs = [(w, j) for w in range(nw) for j in range(3)]

        def start():
            for w, j in pairs:
                ici(w, j, s_me).start()

        def mid():
            for w, j in pairs:
                ici(w, j, shards[j]).wait_recv()
                d2d(w, j, shards[j], c).start()

        def end():
            for w, j in pairs:
                d2d(w, j, shards[j], 1 - c).wait_recv()
            for w, j in pairs:
                ici(w, j, s_me).wait_send()
                d2d(w, j, shards[j], c).wait_send()

        return start, mid, end

    return Comm(list(fulls), [jax.ShapeDtypeStruct(full, BF16) for _, full, _ in specs], {i: i for i in range(nw)},
                [pltpu.SemaphoreType.DMA((3 * nw,))] * 4, emit)


def join_comms(a, b):
    ai, ao, asc = len(a.inputs), len(a.out_shapes), len(a.scratch)

    def emit(ins, outs, sems):
        fa = a.emit(ins[:ai], outs[:ao], sems[:asc])
        fb = b.emit(ins[ai:], outs[ao:], sems[asc:])

        def both(k):
            def run():
                fa[k]()
                fb[k]()
            return run

        return both(0), both(1), both(2)

    aliases = dict(a.aliases)
    aliases.update({ai + i: ao + o for i, o in b.aliases.items()})
    return Comm(a.inputs + b.inputs, a.out_shapes + b.out_shapes, aliases, a.scratch + b.scratch, emit)


def all_gather_comm(x):
    def emit(srcs, outs, sems):
        send_sems, recv_sems, local_sem = sems
        x_ref, out_ref = srcs[0], outs[0]
        x, y, c = _mesh_pos()
        me, sibling = (x, y, c), (x, y, 1 - c)
        chips = _other_chips(x, y)

        def blk(px, py, pc):
            return out_ref.at[4 * px + 2 * py + pc]

        def copy(k, block, to, src=None):
            return pltpu.make_async_remote_copy(
                src_ref=blk(*block) if src is None else src, dst_ref=blk(*block),
                send_sem=send_sems.at[k], recv_sem=recv_sems.at[k], device_id=to, device_id_type=MESH_T)

        def mine():
            return pltpu.make_async_copy(x_ref, blk(*me), local_sem)

        def start():
            mine().start()
            copy(0, me, sibling, src=x_ref).start()
            for j, chip in enumerate(chips):
                copy(1 + j, me, (*chip, c), src=x_ref).start()

        def mid():
            for j, chip in enumerate(chips):
                copy(1 + j, (*chip, c), me).wait_recv()
                copy(4 + j, (*chip, c), sibling).start()

        def end():
            copy(0, sibling, me).wait_recv()
            for j, chip in enumerate(chips):
                copy(4 + j, (*chip, 1 - c), me).wait_recv()
            copy(0, me, sibling, src=x_ref).wait_send()
            for j, chip in enumerate(chips):
                copy(1 + j, me, (*chip, c), src=x_ref).wait_send()
                copy(4 + j, (*chip, c), sibling).wait_send()
            mine().wait()

        return start, mid, end

    return Comm([x], [jax.ShapeDtypeStruct((N_DEV,) + x.shape, F32)], {},
                [pltpu.SemaphoreType.DMA((7,)), pltpu.SemaphoreType.DMA((7,)), pltpu.SemaphoreType.DMA], emit)


def sum_blocks(g, name):
    _, r, c = g.shape
    tr = 256 if r % 256 == 0 else r

    def body(g_ref, o_ref):
        acc = g_ref[0]
        for k in range(1, N_DEV):
            acc = acc + g_ref[k]
        o_ref[...] = acc

    return pl.pallas_call(
        body, name=name, grid=(r // tr,),
        in_specs=[pl.BlockSpec((N_DEV, tr, c), lambda i: (0, i, 0))],
        out_specs=pl.BlockSpec((tr, c), lambda i: (i, 0)),
        out_shape=jax.ShapeDtypeStruct((r, c), F32),
        compiler_params=_params("parallel"),
    )(g)


def _grad_view(g, full, axis):
    r, c = full
    if axis == 0:
        return g.reshape(N_SHARD, 2, r // N_SHARD // 2, c)
    return g.reshape(1, 2, r // 2, c)


def exchange_halves_comm(gviews):
    nw = len(gviews)

    def emit(srcs, outs, sems):
        send_sems, recv_sems = sems
        x, y, c = _mesh_pos()

        def copies():
            return [pltpu.make_async_remote_copy(
                src_ref=srcs[w].at[:, pl.ds(1 - c, 1)], dst_ref=outs[w], send_sem=send_sems.at[w],
                recv_sem=recv_sems.at[w], device_id=(x, y, 1 - c), device_id_type=MESH_T) for w in range(nw)]

        def start():
            for cp in copies():
                cp.start()

        def end():
            for cp in copies():
                cp.wait()

        return start, lambda: None, end

    return Comm(list(gviews), [jax.ShapeDtypeStruct((g.shape[0], 1) + g.shape[2:], BF16) for g in gviews], {},
                [pltpu.SemaphoreType.DMA((nw,)), pltpu.SemaphoreType.DMA((nw,))], emit)


def _row_tile(rh):
    return 128 if rh % 128 == 0 else rh


def add_halves(gview, recv, c_idx, name):
    a, _, rh, cc = gview.shape
    tr = _row_tile(rh)

    def body(c_ref, g_ref, r_ref, o_ref):
        o_ref[0] = (g_ref[0, 0].astype(F32) + r_ref[0, 0].astype(F32)).astype(BF16)

    return pl.pallas_call(
        body, name=name,
        grid_spec=pltpu.PrefetchScalarGridSpec(
            num_scalar_prefetch=1, grid=(a, rh // tr),
            in_specs=[pl.BlockSpec((1, 1, tr, cc), lambda s, i, c_ref: (s, c_ref[0], i, 0)),
                      pl.BlockSpec((1, 1, tr, cc), lambda s, i, c_ref: (s, 0, i, 0))],
            out_specs=pl.BlockSpec((1, tr, cc), lambda s, i, c_ref: (s, i, 0))),
        out_shape=jax.ShapeDtypeStruct((a, rh, cc), BF16),
        compiler_params=_params("parallel", "parallel"),
    )(c_idx, gview, recv)


def _piece_shape(full, axis):
    rs, cs = _shard_shape(full, axis)
    return (rs // 2, cs)


def scatter_pieces_comm(partials, which):
    nw = len(which)
    specs = [BIG[w] for w in which]

    def emit(srcs, outs, sems):
        send_sems, recv_sems = sems
        x, y, c = _mesh_pos()
        chips = _other_chips(x, y)

        def copies():
            cps = []
            for w, (_, full, axis) in enumerate(specs):
                cs = full[1] // N_SHARD
                for j, chip in enumerate(chips):
                    s_j = 2 * chip[0] + chip[1]
                    src = srcs[w].at[s_j] if axis == 0 else srcs[w].at[0, :, pl.ds(s_j * cs, cs)]
                    cps.append(pltpu.make_async_remote_copy(
                        src_ref=src, dst_ref=outs[w].at[j], send_sem=send_sems.at[3 * w + j],
                        recv_sem=recv_sems.at[3 * w + j], device_id=(*chip, c), device_id_type=MESH_T))
            return cps

        def start():
            for cp in copies():
                cp.start()

        def mid():
            pass

        def end():
            for cp in copies():
                cp.wait()

        return start, mid, end

    return Comm(list(partials), [jax.ShapeDtypeStruct((3,) + _piece_shape(full, axis), BF16) for _, full, axis in specs],
                {}, [pltpu.SemaphoreType.DMA((3 * nw,)), pltpu.SemaphoreType.DMA((3 * nw,))], emit)


def add_pieces(partial, recv, idx, axis, name):
    _, rh, cs = recv.shape
    tr = _row_tile(rh)

    def body(idx_ref, p_ref, r_ref, o_ref):
        o_ref[0] = ((p_ref[0].astype(F32) + r_ref[0].astype(F32)) + r_ref[1].astype(F32)) + r_ref[2].astype(F32)

    if axis == 0:
        pspec = pl.BlockSpec((1, tr, cs), lambda i, idx_ref: (idx_ref[0], i, 0))
    else:
        pspec = pl.BlockSpec((1, tr, cs), lambda i, idx_ref: (0, i, idx_ref[0]))
    return pl.pallas_call(
        body, name=name,
        grid_spec=pltpu.PrefetchScalarGridSpec(
            num_scalar_prefetch=1, grid=(rh // tr,),
            in_specs=[pspec, pl.BlockSpec((3, tr, cs), lambda i, idx_ref: (0, i, 0))],
            out_specs=pl.BlockSpec((1, tr, cs), lambda i, idx_ref: (idx_ref[1], i, 0))),
        out_shape=jax.ShapeDtypeStruct((2, rh, cs), F32),
        compiler_params=_params("parallel"),
    )(idx, partial, recv)


def join_halves_comm(halves):
    nw = len(halves)

    def emit(_, outs, sems):
        send_sems, recv_sems = sems
        x, y, c = _mesh_pos()

        def copy(w, half):
            return pltpu.make_async_remote_copy(
                src_ref=outs[w].at[half], dst_ref=outs[w].at[half], send_sem=send_sems.at[w], recv_sem=recv_sems.at[w],
                device_id=(x, y, 1 - c), device_id_type=MESH_T)

        def start():
            for w in range(nw):
                copy(w, c).start()

        def end():
            for w in range(nw):
                copy(w, c).wait_send()
                copy(w, 1 - c).wait_recv()

        return start, lambda: None, end

    return Comm(list(halves), [jax.ShapeDtypeStruct(h.shape, F32) for h in halves], {i: i for i in range(nw)},
                [pltpu.SemaphoreType.DMA((nw,))] * 2, emit)


MOD_COLS = N_MOD * D_MODEL // N_SHARD
MOD_TILE = 512


def mod_fwd(c16, w_mod):
    def body(c_ref, w_ref, s_ref, o_ref):
        cv = c_ref[...]
        s = cv * _sigmoid(cv)
        s_ref[...] = s
        o_ref[...] = jnp.dot(s.astype(BF16), w_ref[...].astype(BF16), preferred_element_type=F32)

    return pl.pallas_call(
        body, name="mod_fwd", grid=(MOD_COLS // MOD_TILE,),
        in_specs=[_full((16, D_MODEL)), pl.BlockSpec((D_MODEL, MOD_TILE), lambda j: (0, j))],
        out_specs=[_full((16, D_MODEL)), pl.BlockSpec((16, MOD_TILE), lambda j: (0, j))],
        out_shape=[jax.ShapeDtypeStruct((16, D_MODEL), F32), jax.ShapeDtypeStruct((16, MOD_COLS), F32)],
        compiler_params=_params("arbitrary"),
    )(c16, w_mod)


def mod_bwd(s16, dm16, w_mod):
    hi = lax.Precision.HIGHEST

    def body(s_ref, d_ref, w_ref, gw_ref, ds_ref):
        j = pl.program_id(0)
        dm = d_ref[...]
        gw_ref[...] = lax.dot_general(s_ref[...], dm, (((0,), (0,)), ((), ())), preferred_element_type=F32, precision=hi)
        part = lax.dot_general(dm, w_ref[...], (((1,), (1,)), ((), ())), preferred_element_type=F32, precision=hi)

        @pl.when(j == 0)
        def _():
            ds_ref[...] = part

        @pl.when(j > 0)
        def _():
            ds_ref[...] = ds_ref[...] + part

    return pl.pallas_call(
        body, name="mod_bwd", grid=(MOD_COLS // MOD_TILE,),
        in_specs=[_full((16, D_MODEL)), pl.BlockSpec((16, MOD_TILE), lambda j: (0, j)),
                  pl.BlockSpec((D_MODEL, MOD_TILE), lambda j: (0, j))],
        out_specs=[pl.BlockSpec((D_MODEL, MOD_TILE), lambda j: (0, j)), _full((16, D_MODEL))],
        out_shape=[jax.ShapeDtypeStruct((D_MODEL, MOD_COLS), F32), jax.ShapeDtypeStruct((16, D_MODEL), F32)],
        compiler_params=_params("arbitrary"),
    )(s16, dm16, w_mod)


def cctx_grad(parts, c_ctx):
    def body(p_ref, c_ref, o_ref):
        ds = p_ref[0:1, :]
        for s in range(1, N_SHARD):
            ds = ds + p_ref[16 * s:16 * s + 1, :]
        cv = c_ref[...]
        sg = _sigmoid(cv)
        o_ref[...] = ds * (sg * (1.0 + cv * (1.0 - sg)))

    return pl.pallas_call(
        body, name="cctx_grad", in_specs=[_full((N_DEV * 8, D_MODEL)), _full((1, D_MODEL))],
        out_specs=_full((1, D_MODEL)), out_shape=jax.ShapeDtypeStruct((1, D_MODEL), F32),
    )(parts, c_ctx)


def add_rows(a, b, name):
    def body(a_ref, b_ref, o_ref):
        o_ref[...] = a_ref[...] + b_ref[...]

    return pl.pallas_call(body, name=name, in_specs=[_full(a.shape), _full(b.shape)], out_specs=_full(a.shape),
                          out_shape=jax.ShapeDtypeStruct(a.shape, F32))(a, b)


def _adamw_update(w_ref, g_ref, m_ref, v_ref, d_ref, nm_ref, nv_ref):
    g_ = g_ref[...]
    m_ = ADAM_B1 * m_ref[...] + (1.0 - ADAM_B1) * g_
    v_ = ADAM_B2 * v_ref[...] + (1.0 - ADAM_B2) * (g_ * g_)
    m_hat = m_ / (1.0 - ADAM_B1 ** ADAM_STEP)
    v_hat = v_ / (1.0 - ADAM_B2 ** ADAM_STEP)
    d_ref[...] = -ADAM_LR * (m_hat / (jnp.sqrt(v_hat) + ADAM_EPS) + ADAM_WD * w_ref[...])
    nm_ref[...] = m_
    nv_ref[...] = v_


def adamw_many(ws, gs, ms, vs):
    n = len(ws)

    def body(*refs):
        for i in range(n):
            _adamw_update(*[refs[k * n + i] for k in range(7)])

    shapes = [jax.ShapeDtypeStruct(w.shape, F32) for w in ws]
    return pl.pallas_call(body, name="adamw_small", out_shape=shapes * 3,
                          compiler_params=pltpu.CompilerParams(vmem_limit_bytes=VMEM_LIMIT_V7X))(*ws, *gs, *ms, *vs)


def adamw(w, g, m, v, name, comm=None):
    r, c = w.shape
    tr = 128 if (r % 128 == 0 and r > 128) else r

    def body(w_ref, g_ref, m_ref, v_ref, d_ref, nm_ref, nv_ref):
        _adamw_update(w_ref, g_ref, m_ref, v_ref, d_ref, nm_ref, nv_ref)

    spec = pl.BlockSpec((tr, c), lambda i: (i, 0))
    shp = jax.ShapeDtypeStruct((r, c), F32)
    res, extra = _call(body, name=name, grid=(r // tr,), in_specs=[spec] * 4, out_specs=[spec] * 3,
                       out_shape=[shp] * 3, sem=("parallel",), args=(w, g, m, v), comm=comm)
    return (*res, extra)


LANES = 1024


def _pack(arrs):
    rows, spans, at = [], [], 0
    for a in arrs:
        n = int(np.prod(a.shape))
        nr = 8 * -(-n // (8 * LANES))
        flat = a.reshape(-1)
        if nr * LANES != n:
            flat = jnp.concatenate([flat, jnp.zeros((nr * LANES - n,), F32)])
        rows.append(flat.reshape(nr, LANES))
        spans.append((at, nr, n, a.shape))
        at += nr
    return jnp.concatenate(rows, axis=0), spans


def _unpack(buf, spans):
    out = []
    for at, nr, n, shape in spans:
        out.append(buf[at:at + nr].reshape(-1)[:n].reshape(shape))
    return out


SMALL_SHARD = ("lru_conv_w", "lru_ba", "lru_bx", "lru_lambda", "ffn_conv_w")


def kernel(x, c, ctx, c_ctx, w_mod, b_mod, norm_mix_g, norm_ffn_g, w_in, lru_conv_w, lru_conv_b, lru_wa, lru_ba, lru_wx, lru_bx, lru_lambda, q_norm_g, k_norm_g, na_rpb, w_rnn_out, w_na_out, w_out, w_up, ffn_conv_w, ffn_conv_b, w_down, loss_target, m_c_ctx, m_w_mod, m_b_mod, m_norm_mix_g, m_norm_ffn_g, m_w_in, m_lru_conv_w, m_lru_conv_b, m_lru_wa, m_lru_ba, m_lru_wx, m_lru_bx, m_lru_lambda, m_q_norm_g, m_k_norm_g, m_na_rpb, m_w_rnn_out, m_w_na_out, m_w_out, m_w_up, m_ffn_conv_w, m_ffn_conv_b, m_w_down, v_c_ctx, v_w_mod, v_b_mod, v_norm_mix_g, v_norm_ffn_g, v_w_in, v_lru_conv_w, v_lru_conv_b, v_lru_wa, v_lru_ba, v_lru_wx, v_lru_bx, v_lru_lambda, v_q_norm_g, v_k_norm_g, v_na_rpb, v_w_rnn_out, v_w_na_out, v_w_out, v_w_up, v_ffn_conv_w, v_ffn_conv_b, v_w_down):
    weights = dict(c_ctx=c_ctx, w_mod=w_mod, b_mod=b_mod, norm_mix_g=norm_mix_g, norm_ffn_g=norm_ffn_g, w_in=w_in,
                   lru_conv_w=lru_conv_w, lru_conv_b=lru_conv_b, lru_wa=lru_wa, lru_ba=lru_ba, lru_wx=lru_wx,
                   lru_bx=lru_bx, lru_lambda=lru_lambda, q_norm_g=q_norm_g, k_norm_g=k_norm_g, na_rpb=na_rpb,
                   w_rnn_out=w_rnn_out, w_na_out=w_na_out, w_out=w_out, w_up=w_up, ffn_conv_w=ffn_conv_w,
                   ffn_conv_b=ffn_conv_b, w_down=w_down)
    mom1 = dict(c_ctx=m_c_ctx, w_mod=m_w_mod, b_mod=m_b_mod, norm_mix_g=m_norm_mix_g, norm_ffn_g=m_norm_ffn_g,
                w_in=m_w_in, lru_conv_w=m_lru_conv_w, lru_conv_b=m_lru_conv_b, lru_wa=m_lru_wa, lru_ba=m_lru_ba,
                lru_wx=m_lru_wx, lru_bx=m_lru_bx, lru_lambda=m_lru_lambda, q_norm_g=m_q_norm_g, k_norm_g=m_k_norm_g,
                na_rpb=m_na_rpb, w_rnn_out=m_w_rnn_out, w_na_out=m_w_na_out, w_out=m_w_out, w_up=m_w_up,
                ffn_conv_w=m_ffn_conv_w, ffn_conv_b=m_ffn_conv_b, w_down=m_w_down)
    mom2 = dict(c_ctx=v_c_ctx, w_mod=v_w_mod, b_mod=v_b_mod, norm_mix_g=v_norm_mix_g, norm_ffn_g=v_norm_ffn_g,
                w_in=v_w_in, lru_conv_w=v_lru_conv_w, lru_conv_b=v_lru_conv_b, lru_wa=v_lru_wa, lru_ba=v_lru_ba,
                lru_wx=v_lru_wx, lru_bx=v_lru_bx, lru_lambda=v_lru_lambda, q_norm_g=v_q_norm_g, k_norm_g=v_k_norm_g,
                na_rpb=v_na_rpb, w_rnn_out=v_w_rnn_out, w_na_out=v_w_na_out, w_out=v_w_out, w_up=v_w_up,
                ffn_conv_w=v_ffn_conv_w, ffn_conv_b=v_ffn_conv_b, w_down=v_w_down)
    order = list(weights)
    d = D_MODEL
    mx_, my_, mc_ = _mesh_pos()
    shard = 2 * mx_ + my_
    dev = 2 * shard + mc_

    idx = jnp.stack([shard, mc_]).astype(jnp.int32)
    wsh = {name: cast_into_full(weights[name][0], full, axis, idx, "cast_" + name) for name, full, axis in BIG}
    local_small, small_spans = _pack([c] + [weights[k][0] for k in SMALL_SHARD])
    bt, (w_in_full, gath) = bias_table(na_rpb[0], comm=join_comms(gather_weights_comm([wsh["w_in"]], [0]),
                                                                  all_gather_comm(local_small)))
    per_dev = [_unpack(gath[k], small_spans) for k in range(N_DEV)]
    c_all = jnp.concatenate([per_dev[k][0] for k in range(N_DEV)], axis=0)
    full_small = {name: jnp.concatenate([per_dev[2 * s][1 + i] for s in range(N_SHARD)], axis=-1)
                  for i, name in enumerate(SMALL_SHARD)}
    c16 = jnp.concatenate([c_all, c_ctx.reshape(1, d), jnp.zeros((7, d), F32)], axis=0)
    s16, mod_part = mod_fwd(c16, w_mod[0])
    mod_all = run_comm(all_gather_comm(mod_part), "gather_mod")[0]
    mod = jnp.concatenate([mod_all[2 * s] for s in range(N_SHARD)], axis=1) + b_mod
    modx = lax.dynamic_slice(mod, (dev, 0), (1, N_MOD * d))
    modc = mod[8:9]

    z = jnp.concatenate([ctx[0], x[0]], axis=0)
    res = local_step(z, loss_target[0], modx, modc, norm_mix_g, norm_ffn_g, w_in_full, full_small["lru_conv_w"],
                     lru_conv_b, lru_wa[0], full_small["lru_ba"], lru_wx[0], full_small["lru_bx"],
                     full_small["lru_lambda"], q_norm_g, k_norm_g, na_rpb[0], wsh["w_rnn_out"], wsh["w_na_out"],
                     wsh["w_out"], wsh["w_up"], full_small["ffn_conv_w"], ffn_conv_b, wsh["w_down"], idx=idx, bt=bt)

    halves = [add_pieces(res["partials"][i], res["pieces"][i], idx, BIG[i][2], "add_pieces_" + BIG[i][0])
              for i in range(len(BIG))]
    lru_tot = {k: sum_blocks(res["lru_w_all"][k], "sum_" + k).reshape(weights[k].shape[1:])
               for k in ("lru_wa", "lru_wx")}
    small_names = ["norm_mix_g", "norm_ffn_g", "lru_conv_w", "lru_conv_b", "lru_ba", "lru_bx",
                   "lru_lambda", "q_norm_g", "k_norm_g", "na_rpb", "ffn_conv_w", "ffn_conv_b"]
    local_g, g_spans = _pack([res["loss_sq"][0:1, 0:1], res["d_modx"], res["d_modc"]] + [res[k] for k in small_names])
    n_rows = local_g.shape[0]
    *joined, g_all = run_comm(join_comms(join_halves_comm(halves), all_gather_comm(local_g)), "tail_exchange")
    grads = {name: joined[i].reshape(_shard_shape(full, axis)) for i, (name, full, axis) in enumerate(BIG)}
    grads.update(lru_tot)
    g_tot = sum_blocks(g_all, "sum_small")
    tot = _unpack(g_tot, g_spans)
    loss = (0.5 / d) * tot[0][0, 0]
    small_tot = dict(zip(small_names, tot[3:]))
    at_x = g_spans[1][0]
    dmx_rows = g_all.reshape(N_DEV, n_rows, LANES)[:, at_x:at_x + N_MOD, :].reshape(N_DEV, N_MOD * d)
    dmc_row = jnp.concatenate([tot[2], jnp.zeros((1, 4 * d), F32)], axis=1)
    dm16 = jnp.concatenate([dmx_rows, dmc_row, jnp.zeros((7, N_MOD * d), F32)], axis=0)
    grads["b_mod"] = add_rows(tot[1], dmc_row, "b_mod_grad")
    g_w_mod, ds16 = mod_bwd(s16, lax.dynamic_slice(dm16, (0, shard * MOD_COLS), (16, MOD_COLS)), w_mod[0])
    grads["w_mod"] = g_w_mod
    for k in small_names:
        g = small_tot[k]
        if k in SMALL_SHARD:
            w_sh = weights[k].shape[-1]
            g = lax.dynamic_slice_in_dim(g, shard * w_sh, w_sh, axis=g.ndim - 1)
        grads[k] = g

    delta, new_m, new_v = {}, {}, {}
    for name, _, _ in BIG + (("w_mod", None, None),):
        *upd, got = adamw(weights[name][0], grads[name], mom1[name][0], mom2[name][0], "adamw_" + name,
                          comm=all_gather_comm(ds16[8:16]) if name == "w_in" else None)
        delta[name], new_m[name], new_v[name] = upd
        if name == "w_in":
            grads["c_ctx"] = cctx_grad(got[0].reshape(N_DEV * 8, d), c_ctx.reshape(1, d))
    rest = [k for k in order if k not in delta]
    views = {k: (grads[k].shape if grads[k].ndim <= 3 else (-1, grads[k].shape[-1])) for k in rest}
    small = adamw_many(*[[t[k].reshape(views[k]) for k in rest] for t in (weights, grads, mom1, mom2)])
    n_rest = len(rest)
    for i, k in enumerate(rest):
        delta[k], new_m[k], new_v[k] = small[i], small[n_rest + i], small[2 * n_rest + i]

    shaped = lambda t: [t[k].reshape(weights[k].shape) for k in order]
    return (loss, res["grad_x"][None], *shaped(grads), *shaped(delta), *shaped(new_m), *shaped(new_v))
```

```python
import numpy as np
import jax
import jax.numpy as jnp
from jax import lax
from jax.experimental import pallas as pl
from jax.experimental.pallas import tpu as pltpu

F32 = jnp.float32
BF16 = jnp.bfloat16

D_MODEL = 1024
SEQ = 2048
CTX_LEN = 256
ZLEN = SEQ + CTX_LEN
GRID_W = 64
GRID_ROWS = SEQ // GRID_W
LRU_BLOCK_W = 128
LRU_BLOCKS = 8
LRU_C = 8.0
NA_HEADS = 16
HEAD_DIM = 64
NA_ROWS = 8
NA_COLS = 16
ROPE_BASE = 10000.0
D_FF = 2816
N_MOD = 6
IN_COLS = 7 * D_MODEL
EPS = 1e-6
NEG_INF = -1e30
N_DEV = 8
N_SHARD = 4

ADAM_LR = 0.001
ADAM_B1 = 0.9
ADAM_B2 = 0.999
ADAM_EPS = 1e-08
ADAM_WD = 0.01
ADAM_STEP = 10

ROW_TILE = 256
Q_ROWS = 4
Q_TILE = Q_ROWS * GRID_W
KEY_ROWS = 12
KEY_TILE = KEY_ROWS * GRID_W
BT_PAD = 4
BT_LEN = 24
VMEM_LIMIT_V7X = 56 * 1024 * 1024

MESH_T = pl.DeviceIdType.MESH


def _params(*sem):
    return pltpu.CompilerParams(dimension_semantics=sem if sem else None, vmem_limit_bytes=VMEM_LIMIT_V7X)


def _full(shape):
    nd = len(shape)
    return pl.BlockSpec(shape, lambda *_: (0,) * nd)


class Comm:
    def __init__(self, inputs, out_shapes, aliases, scratch, emit):
        self.inputs, self.out_shapes, self.aliases, self.scratch, self.emit = inputs, out_shapes, aliases, scratch, emit


def _call(body, *, name, grid, in_specs, out_specs, out_shape, args, scratch_shapes=(), sem=(), comm=None):
    n_in, n_out, n_sc = len(in_specs), len(out_specs), len(scratch_shapes)
    if comm is None:
        res = pl.pallas_call(body, name=name, grid=grid, in_specs=list(in_specs), out_specs=list(out_specs),
                             out_shape=list(out_shape), scratch_shapes=list(scratch_shapes),
                             compiler_params=_params(*sem))(*args)
        return list(res), []
    k_in, k_out = len(comm.inputs), len(comm.out_shapes)
    steps = int(np.prod(grid))

    def hosted(*refs):
        ins, cins = refs[:n_in], refs[n_in:n_in + k_in]
        at = n_in + k_in
        outs, couts = refs[at:at + n_out], refs[at + n_out:at + n_out + k_out]
        at += n_out + k_out
        scr, cscr = refs[at:at + n_sc], refs[at + n_sc:]
        start, mid, end = comm.emit(cins, couts, cscr)
        lin = pl.program_id(0)
        for ax in range(1, len(grid)):
            lin = lin * grid[ax] + pl.program_id(ax)
        pl.when(lin == 0)(start)
        body(*ins, *outs, *scr)
        pl.when(lin == steps - 1 - steps // 7)(mid)
        pl.when(lin == steps - 1)(end)

    hbm = pl.BlockSpec(memory_space=pl.ANY)
    res = pl.pallas_call(
        hosted, name=name, grid=grid, in_specs=list(in_specs) + [hbm] * k_in, out_specs=list(out_specs) + [hbm] * k_out,
        out_shape=list(out_shape) + list(comm.out_shapes), scratch_shapes=list(scratch_shapes) + list(comm.scratch),
        input_output_aliases={n_in + i: n_out + o for i, o in comm.aliases.items()},
        compiler_params=_params(*(("arbitrary",) * len(grid))))(*args, *comm.inputs)
    return list(res[:n_out]), list(res[n_out:])


def _sigmoid(x):
    return 0.5 * jnp.tanh(0.5 * x) + 0.5


def _gelu_parts(x):
    c0 = 0.7978845608028654
    inner = c0 * (x + 0.044715 * x * x * x)
    t = jnp.tanh(inner)
    g = 0.5 * x * (1.0 + t)
    dg = 0.5 * (1.0 + t) + 0.5 * x * (1.0 - t * t) * c0 * (1.0 + 3.0 * 0.044715 * x * x)
    return g, dg


def _dot_nt(a, b):
    return lax.dot_general(a, b, (((1,), (1,)), ((), ())), preferred_element_type=F32)


def _dot_tn(a, b):
    return lax.dot_general(a, b, (((0,), (0,)), ((), ())), preferred_element_type=F32)


def norm_mod(xin, gain, shift, scale, name):
    r, d = xin.shape
    s_mod = shift.shape[0]
    assert r % ROW_TILE == 0

    def body(x_ref, g_ref, sh_ref, sc_ref, xn_ref):
        x = x_ref[...]
        nrm = x * lax.rsqrt(jnp.mean(x * x, axis=-1, keepdims=True) + EPS)
        xn_ref[...] = ((nrm * g_ref[...]) * (1.0 + sc_ref[0]) + sh_ref[0]).astype(BF16)

    mod_spec = pl.BlockSpec((1, 1, d), lambda i: (jnp.minimum(i, s_mod - 1), 0, 0))
    return pl.pallas_call(
        body, name=name, grid=(r // ROW_TILE,),
        in_specs=[pl.BlockSpec((ROW_TILE, d), lambda i: (i, 0)), _full((1, d)), mod_spec, mod_spec],
        out_specs=pl.BlockSpec((ROW_TILE, d), lambda i: (i, 0)),
        out_shape=jax.ShapeDtypeStruct((r, d), BF16),
        compiler_params=_params("parallel"),
    )(xin, gain, shift, scale)


def matmul_wide(a, b, name, tm, tn, comm=None):
    m, k = a.shape
    n = b.shape[1]
    assert m % tm == 0 and n % tn == 0

    def body(a_ref, b_ref, o_ref):
        o_ref[...] = jnp.dot(a_ref[...], b_ref[...], preferred_element_type=F32)

    res, extra = _call(
        body, name=name, grid=(n // tn, m // tm),
        in_specs=[pl.BlockSpec((tm, k), lambda j, i: (i, 0)), pl.BlockSpec((k, tn), lambda j, i: (0, j))],
        out_specs=[pl.BlockSpec((tm, tn), lambda j, i: (i, j))],
        out_shape=[jax.ShapeDtypeStruct((m, n), F32)],
        sem=("parallel", "parallel"), args=(a, b), comm=comm)
    return res[0], extra


def _row_ids(n, w):
    return lax.broadcasted_iota(jnp.int32, (n, w), 0)


def _lru_conv(xr, cw, cb):
    row = _row_ids(ZLEN, LRU_BLOCK_W)
    segpos = jnp.where(row < CTX_LEN, row, row - CTX_LEN)
    seglen = jnp.where(row < CTX_LEN, CTX_LEN, SEQ)
    acc = xr * cw[2:3, :] + cb
    for k in (0, 1, 3):
        off = k - 2
        sh = pltpu.roll(xr, (-off) % ZLEN, 0)
        ok = (segpos + off >= 0) & (segpos + off < seglen)
        acc = acc + jnp.where(ok, sh, 0.0) * cw[k:k + 1, :]
    return acc


def _lru_conv_t(dxc, cw):
    row = _row_ids(ZLEN, LRU_BLOCK_W)
    segpos = jnp.where(row < CTX_LEN, row, row - CTX_LEN)
    seglen = jnp.where(row < CTX_LEN, CTX_LEN, SEQ)
    acc = dxc * cw[2:3, :]
    for k in (0, 1, 3):
        off = k - 2
        sh = pltpu.roll(dxc, off % ZLEN, 0)
        ok = (segpos - off >= 0) & (segpos - off < seglen)
        acc = acc + jnp.where(ok, sh, 0.0) * cw[k:k + 1, :]
    return acc


def _lru_gates(xc, xcb, wa, ba, wx, bx, lam):
    r = _sigmoid(jnp.dot(xcb, wa, preferred_element_type=F32) + ba)
    i = _sigmoid(jnp.dot(xcb, wx, preferred_element_type=F32) + bx)
    sp = jnp.maximum(-lam, 0.0) + jnp.log1p(jnp.exp(-jnp.abs(lam)))
    la = (-LRU_C) * r * sp
    a = jnp.exp(la)
    sq = jnp.sqrt(-jnp.tanh(la) * (1.0 + a * a))
    b = sq * i * xc
    return r, i, sp, a, sq, b


def _scan8_fwd(a, b, rid):
    for s in (1, 2, 4):
        a_s = pltpu.roll(a, s, 0)
        b_s = pltpu.roll(b, s, 0)
        m = rid >= s
        b = jnp.where(m, a * b_s + b, b)
        a = jnp.where(m, a * a_s, a)
    return a, b


def _scan8_rev(a, b, rid):
    for s in (1, 2, 4):
        a_s = pltpu.roll(a, 8 - s, 0)
        b_s = pltpu.roll(b, 8 - s, 0)
        m = rid < 8 - s
        b = jnp.where(m, a * b_s + b, b)
        a = jnp.where(m, a * a_s, a)
    return a, b


N_CHUNK = ZLEN // 8
CTX_CHUNKS = CTX_LEN // 8
SCAN_UNROLL = 8


def _scan_up(a_ref, b_ref, h_ref, lo, hi, carry):
    rid = _row_ids(8, LRU_BLOCK_W)
    assert (hi - lo) % SCAN_UNROLL == 0

    def step(g, c):
        base = pl.multiple_of((lo + g * SCAN_UNROLL) * 8, 8)
        for u in range(SCAN_UNROLL):
            sl = pl.ds(base + 8 * u, 8)
            a, b = _scan8_fwd(a_ref[sl, :], b_ref[sl, :], rid)
            h_ref[sl, :] = b + a * c
            c = b[7:8, :] + a[7:8, :] * c
        return c

    return lax.fori_loop(0, (hi - lo) // SCAN_UNROLL, step, carry)


def _scan_down(a_ref, b_ref, h_ref, lo, hi, carry):
    rid = _row_ids(8, LRU_BLOCK_W)
    assert (hi - lo) % SCAN_UNROLL == 0

    def step(g, c):
        base = pl.multiple_of((hi - (g + 1) * SCAN_UNROLL) * 8, 8)
        for u in reversed(range(SCAN_UNROLL)):
            sl = pl.ds(base + 8 * u, 8)
            a, b = _scan8_rev(a_ref[sl, :], b_ref[sl, :], rid)
            h_ref[sl, :] = b + a * c
            c = b[0:1, :] + a[0:1, :] * c
        return c

    return lax.fori_loop(0, (hi - lo) // SCAN_UNROLL, step, carry)


def _lru_scan_dir(d, a_ref, b_ref, h_ref):
    zero = jnp.zeros((1, LRU_BLOCK_W), F32)
    if d == 0:
        _scan_up(a_ref, b_ref, h_ref, 0, N_CHUNK, zero)
    else:
        c = _scan_down(a_ref, b_ref, h_ref, 0, CTX_CHUNKS, zero)
        _scan_down(a_ref, b_ref, h_ref, CTX_CHUNKS, N_CHUNK, c)


def _lru_in_specs():
    blk = lambda rows: pl.BlockSpec((rows, LRU_BLOCK_W), lambda b: (0, b))
    wspec = pl.BlockSpec((2, 1, LRU_BLOCK_W, LRU_BLOCK_W), lambda b: (0, b, 0, 0))
    return blk, wspec


def lru_fwd(p, conv_w, conv_b, wa, ba, wx, bx, lam, comm=None):
    blk, wspec = _lru_in_specs()

    def body(xr_ref, gx_ref, cw_ref, cb_ref, wa_ref, ba_ref, wx_ref, bx_ref, lam_ref, y_ref, a_s, b_s, h_s, hsum_s):
        xr = xr_ref[...]
        xc = _lru_conv(xr, cw_ref[...], cb_ref[...])
        xcb = xc.astype(BF16)
        for d in (0, 1):
            _, _, _, a, _, b = _lru_gates(xc, xcb, wa_ref[d, 0].astype(BF16), ba_ref[d:d + 1, :],
                                          wx_ref[d, 0].astype(BF16), bx_ref[d:d + 1, :], lam_ref[d:d + 1, :])
            a_s[...] = a
            b_s[...] = b
            _lru_scan_dir(d, a_s, b_s, h_s)
            if d == 0:
                hsum_s[...] = h_s[...]
            else:
                hsum_s[...] = hsum_s[...] + h_s[...]
        g, _ = _gelu_parts(gx_ref[CTX_LEN:, :])
        y_ref[...] = (hsum_s[CTX_LEN:, :] * g).astype(BF16)

    zs = pltpu.VMEM((ZLEN, LRU_BLOCK_W), F32)
    res, extra = _call(
        body, name="lru_fwd", grid=(LRU_BLOCKS,),
        in_specs=[blk(ZLEN), pl.BlockSpec((ZLEN, LRU_BLOCK_W), lambda b: (0, 24 + b)), blk(4), blk(1),
                  wspec, blk(2), wspec, blk(2), blk(2)],
        out_specs=[pl.BlockSpec((SEQ, LRU_BLOCK_W), lambda b: (0, b))],
        out_shape=[jax.ShapeDtypeStruct((SEQ, D_MODEL), BF16)],
        scratch_shapes=[zs, zs, zs, zs], sem=("arbitrary",),
        args=(p, p, conv_w, conv_b, wa, ba, wx, bx, lam), comm=comm)
    return res[0], extra


def _rope_tables():
    t = np.arange(SEQ)
    lane = np.arange(2 * HEAD_DIM)
    in_head = lane % HEAD_DIM
    j = (in_head % 32) % 16
    freq = ROPE_BASE ** (-j.astype(np.float64) / 16.0)
    pos = np.where(in_head[None, :] < 32, (t // GRID_W)[:, None], (t % GRID_W)[:, None]).astype(np.float64)
    ang = (pos.astype(np.float32) * freq.astype(np.float32)[None, :]).astype(np.float32)
    cos = np.cos(ang).astype(np.float32)
    sin = np.sin(ang).astype(np.float32)
    sgn = np.where((in_head % 32) < 16, -1.0, 1.0).astype(np.float32)
    cos = np.concatenate([np.ones((CTX_LEN, 2 * HEAD_DIM), np.float32), cos], 0)
    sin = np.concatenate([np.zeros((CTX_LEN, 2 * HEAD_DIM), np.float32), sin * sgn[None, :]], 0)
    return jnp.asarray(cos), jnp.asarray(sin)


def _head_ones():
    lane = np.arange(2 * HEAD_DIM)
    return jnp.asarray((lane[:, None] // HEAD_DIM == lane[None, :] // HEAD_DIM).astype(np.float32))


def _rope_partner(x):
    lane = lax.broadcasted_iota(jnp.int32, x.shape, 1)
    return jnp.where((lane % 32) < 16, pltpu.roll(x, 128 - 16, 1), pltpu.roll(x, 16, 1))


def _head_sum(t, ones):
    hi = t.astype(BF16)
    lo = (t - hi.astype(F32)).astype(BF16)
    ones_b = ones.astype(BF16)
    return jnp.dot(hi, ones_b, preferred_element_type=F32) + jnp.dot(lo, ones_b, preferred_element_type=F32)


def _head_rms(x, ones, gain):
    ms = _head_sum(x * x, ones) * (1.0 / HEAD_DIM)
    rstd = lax.rsqrt(ms + EPS)
    return x * rstd * gain, rstd


PREP_TILE = 768


def qkv_prep(p, qg2, kg2, cos, sin, ones, comm=None):
    scale = HEAD_DIM ** -0.5

    def body(q_ref, k_ref, v_ref, qg_ref, kg_ref, cos_ref, sin_ref, ones_ref, qr_ref, qp_ref, kk_ref, vv_ref):
        ones_m = ones_ref[...]
        c, s = cos_ref[...], sin_ref[...]
        qn, _ = _head_rms(q_ref[...], ones_m, qg_ref[...])
        qn = qn * scale
        qr_ref[...] = (qn * c + _rope_partner(qn) * s).astype(BF16)
        qp_ref[...] = qn.astype(BF16)
        kn, _ = _head_rms(k_ref[...], ones_m, kg_ref[...])
        kk_ref[...] = (kn * c + _rope_partner(kn) * s).astype(BF16)
        vv_ref[...] = v_ref[...].astype(BF16)

    col = lambda base: pl.BlockSpec((PREP_TILE, 128), lambda hp, i: (i, base + hp))
    small = pl.BlockSpec((1, 128), lambda hp, i: (0, 0))
    tab = pl.BlockSpec((PREP_TILE, 128), lambda hp, i: (i, 0))
    oshape = jax.ShapeDtypeStruct((ZLEN, D_MODEL), BF16)
    res, extra = _call(
        body, name="qkv_prep", grid=(NA_HEADS // 2, ZLEN // PREP_TILE),
        in_specs=[col(32), col(8), col(16), small, small, tab, tab, _full((128, 128))],
        out_specs=[col(0)] * 4, out_shape=[oshape] * 4, sem=("parallel", "parallel"),
        args=(p, p, p, qg2, kg2, cos, sin, ones), comm=comm)
    return (*res, extra)


def _bias_expand():
    qc = np.arange(GRID_W)[:, None]
    kc = np.arange(GRID_W)[None, :]
    col_start = np.clip(qc - NA_COLS // 2, 0, GRID_W - NA_COLS)
    in_win = (kc >= col_start) & (kc < col_start + NA_COLS)
    dc = np.clip(kc - qc, -(NA_COLS - 1), NA_COLS - 1) + (NA_COLS - 1)
    e = np.zeros((2 * NA_COLS - 1, GRID_W, GRID_W), np.float32)
    for d in range(2 * NA_COLS - 1):
        e[d] = ((dc == d) & in_win).astype(np.float32)
    pen = np.where(in_win, 0.0, NEG_INF).astype(np.float32)
    return e, pen


def bias_table(rpb2, comm=None):
    e, pen = _bias_expand()
    n_dr = 2 * NA_ROWS - 1
    ea = np.zeros((31, GRID_W, 128), np.float32)
    ea[:, :, :GRID_W] = e
    eb = np.zeros((31, GRID_W, 128), np.float32)
    eb[:, :, GRID_W:] = e
    pen2 = np.concatenate([pen, pen], 1)
    ea = jnp.asarray(ea.reshape(31, GRID_W * 128))
    eb = jnp.asarray(eb.reshape(31, GRID_W * 128))
    sel_a = np.zeros((BT_LEN, n_dr), np.float32)
    sel_b = np.zeros((BT_LEN, n_dr), np.float32)
    for r in range(BT_LEN):
        dr = r - BT_PAD
        if 0 <= dr < n_dr:
            sel_a[r, dr] = 1.0
        if 0 <= dr + 1 < n_dr:
            sel_b[r, dr + 1] = 1.0
    sel_a, sel_b = jnp.asarray(sel_a), jnp.asarray(sel_b)
    pen2 = jnp.asarray(pen2.reshape(1, GRID_W * 128))
    hi = lax.Precision.HIGHEST

    def body(rpb_ref, sa_ref, sb_ref, ea_ref, eb_ref, pen_ref, o_ref, ra_s, rb_s):
        for h in range(NA_HEADS):
            rp = rpb_ref[h]
            ra_s[h * BT_LEN:(h + 1) * BT_LEN, :] = jnp.dot(sa_ref[...], rp, preferred_element_type=F32, precision=hi)
            rb_s[h * BT_LEN:(h + 1) * BT_LEN, :] = jnp.dot(sb_ref[...], rp, preferred_element_type=F32, precision=hi)
        o_ref[...] = (jnp.dot(ra_s[...], ea_ref[...], preferred_element_type=F32, precision=hi)
                      + jnp.dot(rb_s[...], eb_ref[...], preferred_element_type=F32, precision=hi) + pen_ref[...])

    tcol = 2048
    rows = NA_HEADS * BT_LEN
    res, extra = _call(
        body, name="bias_table", grid=(GRID_W * 128 // tcol,),
        in_specs=[_full((NA_HEADS, n_dr, 31)), _full((BT_LEN, n_dr)), _full((BT_LEN, n_dr)),
                  pl.BlockSpec((31, tcol), lambda j: (0, j)), pl.BlockSpec((31, tcol), lambda j: (0, j)),
                  pl.BlockSpec((1, tcol), lambda j: (0, j))],
        out_specs=[pl.BlockSpec((rows, tcol), lambda j: (0, j))],
        out_shape=[jax.ShapeDtypeStruct((rows, GRID_W * 128), F32)],
        scratch_shapes=[pltpu.VMEM((rows, 31), F32), pltpu.VMEM((rows, 31), F32)], sem=("parallel",),
        args=(rpb2, sel_a, sel_b, ea, eb, pen2), comm=comm)
    return res[0].reshape(NA_HEADS, BT_LEN, GRID_W, 128), extra


def _key_window(j):
    ws = jnp.clip(Q_ROWS * j - 4, 0, GRID_ROWS - KEY_ROWS)
    return ws, pl.multiple_of(CTX_LEN + ws * GRID_W, 256)


def _head_mask(hh):
    lane = lax.broadcasted_iota(jnp.int32, (Q_TILE, 128), 1)
    return (lane < HEAD_DIM) if hh == 0 else (lane >= HEAD_DIM)


def _attn_scores(j, ws, q_rot_h, q_pl_h, kw, kc, hh, bt_ref, s_ref):
    s_ref[:, :KEY_TILE] = _dot_nt(q_rot_h, kw)
    s_ref[:, KEY_TILE:] = _dot_nt(q_pl_h, kc)
    lane = lax.broadcasted_iota(jnp.int32, (GRID_W, 128), 1)
    base = ws - Q_ROWS * j + (NA_ROWS - 1) + BT_PAD
    for qi in range(Q_ROWS):
        rs = jnp.clip(Q_ROWS * j + qi - NA_ROWS // 2, 0, GRID_ROWS - NA_ROWS)
        for m in range(KEY_ROWS // 2):
            k0 = ws + 2 * m
            p0 = jnp.where((k0 >= rs) & (k0 < rs + NA_ROWS), 0.0, NEG_INF)
            p1 = jnp.where((k0 + 1 >= rs) & (k0 + 1 < rs + NA_ROWS), 0.0, NEG_INF)
            pen = jnp.where(lane < GRID_W, p0, p1)
            rows = slice(qi * GRID_W, (qi + 1) * GRID_W)
            cols = slice(128 * m, 128 * (m + 1))
            s_ref[rows, cols] = s_ref[rows, cols] + bt_ref[hh, base + 2 * m - qi] + pen
    return base


def attn_fwd(q_rot, q_pl, kk, vv, bt, comm=None):
    def body(qr_ref, qp_ref, kk_ref, vv_ref, bt_ref, o_ref, lse_ref, s_ref):
        j = pl.program_id(1)
        ws, start = _key_window(j)
        win = pl.ds(start, KEY_TILE)
        kw, kc = kk_ref[win, :], kk_ref[:CTX_LEN, :]
        vw, vc = vv_ref[win, :], vv_ref[:CTX_LEN, :]
        qr, qp = qr_ref[...], qp_ref[...]
        outs = []
        for hh in range(2):
            msk = _head_mask(hh)
            _attn_scores(j, ws, jnp.where(msk, qr, 0), jnp.where(msk, qp, 0), kw, kc, hh, bt_ref, s_ref)
            s = s_ref[...]
            mx = jnp.max(s, axis=-1, keepdims=True)
            pr = jnp.exp(s - mx)
            l = jnp.sum(pr, axis=-1, keepdims=True)
            prb = pr.astype(BF16)
            o = jnp.dot(prb[:, :KEY_TILE], vw, preferred_element_type=F32)
            o = o + jnp.dot(prb[:, KEY_TILE:], vc, preferred_element_type=F32)
            outs.append(o / l)
            lse_ref[hh] = mx + jnp.log(l)
        o_ref[...] = jnp.where(_head_mask(0), outs[0], outs[1])

    qspec = pl.BlockSpec((Q_TILE, 128), lambda hp, j: (j + 1, hp))
    kspec = pl.BlockSpec((ZLEN, 128), lambda hp, j: (0, hp))
    res, extra = _call(
        body, name="attn_fwd", grid=(NA_HEADS // 2, SEQ // Q_TILE),
        in_specs=[qspec, qspec, kspec, kspec, pl.BlockSpec((2, BT_LEN, GRID_W, 128), lambda hp, j: (hp, 0, 0, 0))],
        out_specs=[pl.BlockSpec((Q_TILE, 128), lambda hp, j: (j, hp)),
                   pl.BlockSpec((2, Q_TILE, 1), lambda hp, j: (hp, j, 0))],
        out_shape=[jax.ShapeDtypeStruct((SEQ, D_MODEL), F32), jax.ShapeDtypeStruct((NA_HEADS, SEQ, 1), F32)],
        scratch_shapes=[pltpu.VMEM((Q_TILE, KEY_TILE + CTX_LEN), F32)], sem=("parallel", "arbitrary"),
        args=(q_rot, q_pl, kk, vv, bt), comm=comm)
    return res[0], res[1], extra


def merge_fwd(y_rnn, y_na, p, z, g2, w_rnn, w_na, w_out):
    def body(yr_ref, yn_ref, mr_ref, mn_ref, x_ref, g2_ref, wr_ref, wn_ref, wo_ref, u_ref, v_ref, mg_ref, out_ref, x1_ref):
        u = jnp.dot(yr_ref[...], wr_ref[...], preferred_element_type=F32)
        v = jnp.dot(yn_ref[...].astype(BF16), wn_ref[...], preferred_element_type=F32)
        merged = (_sigmoid(mr_ref[...]) * u + _sigmoid(mn_ref[...]) * v).astype(BF16)
        out = jnp.dot(merged, wo_ref[...], preferred_element_type=F32)
        u_ref[...] = u
        v_ref[...] = v
        mg_ref[...] = merged
        out_ref[...] = out
        x1_ref[...] = x_ref[...] + g2_ref[...] * out

    row = pl.BlockSpec((ROW_TILE, D_MODEL), lambda i: (i, 0))
    lat = lambda cb: pl.BlockSpec((ROW_TILE, D_MODEL), lambda i: (i + 1, cb))
    wspec = _full((D_MODEL, D_MODEL))
    f32o = jax.ShapeDtypeStruct((SEQ, D_MODEL), F32)
    return pl.pallas_call(
        body, name="merge_fwd", grid=(SEQ // ROW_TILE,),
        in_specs=[row, row, lat(5), lat(6), lat(0), _full((1, D_MODEL)), wspec, wspec, wspec],
        out_specs=[row] * 5,
        out_shape=[f32o, f32o, jax.ShapeDtypeStruct((SEQ, D_MODEL), BF16), f32o, f32o],
        compiler_params=_params("parallel"),
    )(y_rnn, y_na, p, p, z, g2, w_rnn, w_na, w_out)


FF_TILE = 256
FF_TILES = D_FF // FF_TILE


FF_ROWS = 64
FF_HALO = 8
FF_SLAB = FF_ROWS + 2 * FF_HALO


def _ffn_row_chunks(chunk, init):
    carry = chunk(0, 0, -1, init)
    carry = lax.fori_loop(1, SEQ // FF_ROWS - 1,
                          lambda ci, cr: chunk(pl.multiple_of(ci * FF_ROWS - FF_HALO, 8), FF_HALO, 0, cr), carry)
    return chunk(SEQ - FF_SLAB, 2 * FF_HALO, 1, carry)


def _ffn_shifts(edge):
    row = _row_ids(FF_SLAB, FF_TILE)

    def prev(x):
        r = pltpu.roll(x, 1, 0)
        return jnp.where(row >= 1, r, 0.0) if edge == -1 else r

    def nxt(x):
        r = pltpu.roll(x, FF_SLAB - 1, 0)
        return jnp.where(row < FF_SLAB - 1, r, 0.0) if edge == 1 else r

    return prev, nxt


def ffn_act(hpre, conv_w, conv_b):
    def body(ha_ref, hg_ref, wa_ref, wg_ref, ba_ref, bg_ref, o_ref):
        wa, wg, ba, bg = wa_ref[...], wg_ref[...], ba_ref[...], bg_ref[...]

        def chunk(lo, mid, edge, carry):
            prev, nxt = _ffn_shifts(edge)
            ha, hg = ha_ref[pl.ds(lo, FF_SLAB), :], hg_ref[pl.ds(lo, FF_SLAB), :]
            a = prev(ha) * wa[0:1] + ha * wa[1:2] + nxt(ha) * wa[2:3] + ba
            g = prev(hg) * wg[0:1] + hg * wg[1:2] + nxt(hg) * wg[2:3] + bg
            o_ref[pl.ds(lo + mid, FF_ROWS), :] = (a * _sigmoid(a) * g)[mid:mid + FF_ROWS].astype(BF16)
            return carry

        _ffn_row_chunks(chunk, 0)

    col = lambda rows, off: pl.BlockSpec((rows, FF_TILE), lambda j: (0, j + off))
    return pl.pallas_call(
        body, name="ffn_act", grid=(FF_TILES,),
        in_specs=[col(SEQ, 0), col(SEQ, FF_TILES), col(3, 0), col(3, FF_TILES), col(1, 0), col(1, FF_TILES)],
        out_specs=col(SEQ, 0),
        out_shape=jax.ShapeDtypeStruct((SEQ, D_FF), BF16),
        compiler_params=_params("parallel"),
    )(hpre, hpre, conv_w, conv_w, conv_b, conv_b)


def ffn_down_loss(act, w_down, x1, g5, target):
    def body(a_ref, w_ref, x1_ref, g5_ref, t_ref, f_ref, dy_ref, df_ref, ls_ref, dg_ref):
        i = pl.program_id(0)
        f = jnp.dot(a_ref[...], w_ref[...], preferred_element_type=F32)
        g5 = g5_ref[...]
        err = x1_ref[...] + g5 * f - t_ref[...]
        dy = err * (1.0 / D_MODEL)
        f_ref[...] = f
        dy_ref[...] = dy
        df_ref[...] = (dy * g5).astype(BF16)

        @pl.when(i == 0)
        def _():
            ls_ref[...] = jnp.zeros_like(ls_ref)
            dg_ref[...] = jnp.zeros_like(dg_ref)

        ls_ref[...] = ls_ref[...] + jnp.sum(err * err)
        dg_ref[...] = dg_ref[...] + jnp.sum(dy * f, axis=0, keepdims=True)

    row = pl.BlockSpec((ROW_TILE, D_MODEL), lambda i: (i, 0))
    f32o = jax.ShapeDtypeStruct((SEQ, D_MODEL), F32)
    return pl.pallas_call(
        body, name="ffn_down_loss", grid=(SEQ // ROW_TILE,),
        in_specs=[pl.BlockSpec((ROW_TILE, D_FF), lambda i: (i, 0)), _full((D_FF, D_MODEL)), row, _full((1, D_MODEL)), row],
        out_specs=[row, row, row, _full((8, 128)), _full((1, D_MODEL))],
        out_shape=[f32o, f32o, jax.ShapeDtypeStruct((SEQ, D_MODEL), BF16), jax.ShapeDtypeStruct((8, 128), F32),
                   jax.ShapeDtypeStruct((1, D_MODEL), F32)],
        compiler_params=_params("arbitrary"),
    )(act, w_down, x1, g5, target)


def ffn_down_bwd(df, w_down):
    def body(df_ref, w_ref, o_ref):
        o_ref[...] = _dot_nt(df_ref[...], w_ref[...])

    return pl.pallas_call(
        body, name="ffn_down_bwd", grid=(SEQ // ROW_TILE,),
        in_specs=[pl.BlockSpec((ROW_TILE, D_MODEL), lambda i: (i, 0)), _full((D_FF, D_MODEL))],
        out_specs=pl.BlockSpec((ROW_TILE, D_FF), lambda i: (i, 0)),
        out_shape=jax.ShapeDtypeStruct((SEQ, D_FF), F32),
        compiler_params=_params("parallel"),
    )(df, w_down)


def ffn_act_bwd(hpre, d_act, conv_w, conv_b):
    def body(ha_ref, hg_ref, da_ref, wa_ref, wg_ref, ba_ref, bg_ref, dha_ref, dhg_ref, dwa_ref, dwg_ref, dba_ref, dbg_ref):
        wa, wg, ba, bg = wa_ref[...], wg_ref[...], ba_ref[...], bg_ref[...]

        def chunk(lo, mid, edge, acc):
            prev, nxt = _ffn_shifts(edge)
            rows = pl.ds(lo, FF_SLAB)
            ha, hg, dact = ha_ref[rows, :], hg_ref[rows, :], da_ref[rows, :]
            hap, han, hgp, hgn = prev(ha), nxt(ha), prev(hg), nxt(hg)
            a = hap * wa[0:1] + ha * wa[1:2] + han * wa[2:3] + ba
            g = hgp * wg[0:1] + hg * wg[1:2] + hgn * wg[2:3] + bg
            sig = _sigmoid(a)
            dca = dact * g * (sig * (1.0 + a * (1.0 - sig)))
            dcg = dact * a * sig
            m = slice(mid, mid + FF_ROWS)
            sums = []
            for dc, h, hp, hn, w, dh_ref in ((dca, ha, hap, han, wa, dha_ref), (dcg, hg, hgp, hgn, wg, dhg_ref)):
                dcm = dc[m]
                sums += [jnp.sum(dcm * hp[m], axis=0, keepdims=True), jnp.sum(dcm * h[m], axis=0, keepdims=True),
                         jnp.sum(dcm * hn[m], axis=0, keepdims=True), jnp.sum(dcm, axis=0, keepdims=True)]
                dh = nxt(dc) * w[0:1] + dc * w[1:2] + prev(dc) * w[2:3]
                dh_ref[pl.ds(lo + mid, FF_ROWS), :] = dh[m].astype(BF16)
            return tuple(x + y for x, y in zip(acc, sums))

        acc = _ffn_row_chunks(chunk, tuple(jnp.zeros((1, FF_TILE), F32) for _ in range(8)))
        dwa_ref[0:1, :], dwa_ref[1:2, :], dwa_ref[2:3, :], dba_ref[...] = acc[0], acc[1], acc[2], acc[3]
        dwg_ref[0:1, :], dwg_ref[1:2, :], dwg_ref[2:3, :], dbg_ref[...] = acc[4], acc[5], acc[6], acc[7]

    col = lambda rows, off: pl.BlockSpec((rows, FF_TILE), lambda j: (0, j + off))
    hshape = jax.ShapeDtypeStruct((SEQ, D_FF), BF16)
    wshape = jax.ShapeDtypeStruct((3, D_FF), F32)
    bshape = jax.ShapeDtypeStruct((1, D_FF), F32)
    return pl.pallas_call(
        body, name="ffn_act_bwd", grid=(FF_TILES,),
        in_specs=[col(SEQ, 0), col(SEQ, FF_TILES), col(SEQ, 0), col(3, 0), col(3, FF_TILES), col(1, 0), col(1, FF_TILES)],
        out_specs=[col(SEQ, 0), col(SEQ, 0), col(3, 0), col(3, 0), col(1, 0), col(1, 0)],
        out_shape=[hshape, hshape, wshape, wshape, bshape, bshape],
        compiler_params=_params("parallel"),
    )(hpre, hpre, d_act, conv_w, conv_w, conv_b, conv_b)


def _norm_mod_bwd(x, dxn, gain, scale):
    rstd = lax.rsqrt(jnp.mean(x * x, axis=-1, keepdims=True) + EPS)
    nrm = x * rstd
    dsh = jnp.sum(dxn, axis=0, keepdims=True)
    dsc = jnp.sum(dxn * nrm, axis=0, keepdims=True) * gain
    dgn = jnp.sum(dxn * nrm, axis=0, keepdims=True) * (1.0 + scale)
    dn = dxn * (gain * (1.0 + scale))
    dx = rstd * (dn - nrm * jnp.mean(dn * nrm, axis=-1, keepdims=True))
    return dx, dsh, dsc, dgn


def ffn_up_bwd(dha, dhg, w_up, x1, dy, gain, scale):
    def body(dha_ref, dhg_ref, w_ref, x_ref, dy_ref, g_ref, sc_ref, dx_ref, dsh_ref, dsc_ref, dgn_ref):
        i = pl.program_id(0)
        dxn = _dot_nt(dha_ref[...], w_ref[:, :D_FF]) + _dot_nt(dhg_ref[...], w_ref[:, D_FF:])
        dx, dsh, dsc, dgn = _norm_mod_bwd(x_ref[...], dxn, g_ref[...], sc_ref[...])
        dx_ref[...] = dy_ref[...] + dx

        @pl.when(i == 0)
        def _():
            dsh_ref[...] = dsh
            dsc_ref[...] = dsc
            dgn_ref[...] = dgn

        @pl.when(i > 0)
        def _():
            dsh_ref[...] = dsh_ref[...] + dsh
            dsc_ref[...] = dsc_ref[...] + dsc
            dgn_ref[...] = dgn_ref[...] + dgn

    row = pl.BlockSpec((ROW_TILE, D_MODEL), lambda i: (i, 0))
    vec = _full((1, D_MODEL))
    vshape = jax.ShapeDtypeStruct((1, D_MODEL), F32)
    return pl.pallas_call(
        body, name="ffn_up_bwd", grid=(SEQ // ROW_TILE,),
        in_specs=[pl.BlockSpec((ROW_TILE, D_FF), lambda i: (i, 0)), pl.BlockSpec((ROW_TILE, D_FF), lambda i: (i, 0)),
                  _full((D_MODEL, 2 * D_FF)), row, row, vec, vec],
        out_specs=[row, vec, vec, vec],
        out_shape=[jax.ShapeDtypeStruct((SEQ, D_MODEL), F32), vshape, vshape, vshape],
        compiler_params=_params("arbitrary"),
    )(dha, dhg, w_up, x1, dy, gain, scale)


def merge_bwd(dx1, out, g2, p, u, v, w_rnn, w_na, w_out, comm=None):
    def body(dx_ref, out_ref, g2_ref, mr_ref, mn_ref, u_ref, v_ref, wr_ref, wn_ref, wo_ref,
             dout_ref, du_ref, dv_ref, dmr_ref, dmn_ref, dyr_ref, dyn_ref, dg2_ref):
        i = pl.program_id(0)

        @pl.when(i == 0)
        def _():
            dmr_ref[...] = jnp.zeros_like(dmr_ref)
            dmn_ref[...] = jnp.zeros_like(dmn_ref)
            dg2_ref[...] = jnp.zeros_like(dg2_ref)

        @pl.when(i > 0)
        def _():
            dx = dx_ref[...]
            dg2_ref[...] = dg2_ref[...] + jnp.sum(dx * out_ref[...], axis=0, keepdims=True)
            dout = (dx * g2_ref[...]).astype(BF16)
            dout_ref[...] = dout
            dm = _dot_nt(dout, wo_ref[...])
            sr = _sigmoid(mr_ref[...])
            sn = _sigmoid(mn_ref[...])
            du = (dm * sr).astype(BF16)
            dv = (dm * sn).astype(BF16)
            du_ref[...] = du
            dv_ref[...] = dv
            dmr_ref[...] = (dm * u_ref[...] * (sr * (1.0 - sr))).astype(BF16)
            dmn_ref[...] = (dm * v_ref[...] * (sn * (1.0 - sn))).astype(BF16)
            dyr_ref[...] = _dot_nt(du, wr_ref[...])
            dyn_ref[...] = _dot_nt(dv, wn_ref[...])

    lat = pl.BlockSpec((ROW_TILE, D_MODEL), lambda i: (jnp.maximum(i - 1, 0), 0))
    zrow = pl.BlockSpec((ROW_TILE, D_MODEL), lambda i: (i, 0))
    pcol = lambda cb: pl.BlockSpec((ROW_TILE, D_MODEL), lambda i: (i, cb))
    wspec = _full((D_MODEL, D_MODEL))
    tb = jax.ShapeDtypeStruct((SEQ, D_MODEL), BF16)
    zb = jax.ShapeDtypeStruct((ZLEN, D_MODEL), BF16)
    tf = jax.ShapeDtypeStruct((SEQ, D_MODEL), F32)
    res, extra = _call(
        body, name="merge_bwd", grid=(ZLEN // ROW_TILE,),
        in_specs=[lat, lat, _full((1, D_MODEL)), pcol(5), pcol(6), lat, lat, wspec, wspec, wspec],
        out_specs=[lat, lat, lat, zrow, zrow, lat, lat, _full((1, D_MODEL))],
        out_shape=[tb, tb, tb, zb, zb, tf, tf, jax.ShapeDtypeStruct((1, D_MODEL), F32)],
        sem=("arbitrary",), args=(dx1, out, g2, p, p, u, v, w_rnn, w_na, w_out), comm=comm)
    return (*res, extra)


def attn_bwd(q_rot, q_pl, kk, vv, bt, y_na, d_yna, lse, comm=None):
    def body(qr_ref, qp_ref, kk_ref, vv_ref, bt_ref, o_ref, do_ref, lse_ref,
             dqr_ref, dqp_ref, dk_ref, dv_ref, dbt_ref, s_ref):
        jj = pl.program_id(1)

        @pl.when(jj == 0)
        def _():
            dqr_ref[...] = jnp.zeros_like(dqr_ref)
            dqp_ref[...] = jnp.zeros_like(dqp_ref)
            dk_ref[...] = jnp.zeros_like(dk_ref)
            dv_ref[...] = jnp.zeros_like(dv_ref)
            dbt_ref[...] = jnp.zeros_like(dbt_ref)

        @pl.when(jj > 0)
        def _():
            j = jj - 1
            ws, start = _key_window(j)
            win = pl.ds(start, KEY_TILE)
            kw, kc = kk_ref[win, :], kk_ref[:CTX_LEN, :]
            vw, vc = vv_ref[win, :], vv_ref[:CTX_LEN, :]
            qr, qp = qr_ref[...], qp_ref[...]
            do = do_ref[...]
            do_o = do * o_ref[...]
            dq_r, dq_p = [], []
            for hh in range(2):
                msk = _head_mask(hh)
                q_r, q_p = jnp.where(msk, qr, 0), jnp.where(msk, qp, 0)
                base = _attn_scores(j, ws, q_r, q_p, kw, kc, hh, bt_ref, s_ref)
                pr = jnp.exp(s_ref[...] - lse_ref[hh])
                delta = jnp.sum(jnp.where(msk, do_o, 0.0), axis=-1, keepdims=True)
                dob = jnp.where(msk, do, 0.0).astype(BF16)
                ds_lat = pr[:, :KEY_TILE] * (_dot_nt(dob, vw) - delta)
                ds_ctx = pr[:, KEY_TILE:] * (_dot_nt(dob, vc) - delta)
                for qi in range(Q_ROWS):
                    for m in range(KEY_ROWS // 2):
                        idx = base + 2 * m - qi
                        dbt_ref[hh, idx] = dbt_ref[hh, idx] + ds_lat[qi * GRID_W:(qi + 1) * GRID_W, 128 * m:128 * (m + 1)]
                dsb_lat = ds_lat.astype(BF16)
                dsb_ctx = ds_ctx.astype(BF16)
                prb = pr.astype(BF16)
                dq_r.append(jnp.dot(dsb_lat, kw, preferred_element_type=F32))
                dq_p.append(jnp.dot(dsb_ctx, kc, preferred_element_type=F32))
                dk_ref[win, :] = dk_ref[win, :] + _dot_tn(dsb_lat, q_r)
                dk_ref[:CTX_LEN, :] = dk_ref[:CTX_LEN, :] + _dot_tn(dsb_ctx, q_p)
                dv_ref[win, :] = dv_ref[win, :] + _dot_tn(prb[:, :KEY_TILE], dob)
                dv_ref[:CTX_LEN, :] = dv_ref[:CTX_LEN, :] + _dot_tn(prb[:, KEY_TILE:], dob)
            dqr_ref[...] = jnp.where(_head_mask(0), dq_r[0], dq_r[1])
            dqp_ref[...] = jnp.where(_head_mask(0), dq_p[0], dq_p[1])

    lat = lambda jj: jnp.maximum(jj - 1, 0)
    qspec = pl.BlockSpec((Q_TILE, 128), lambda hp, jj: (lat(jj) + 1, hp))
    kspec = pl.BlockSpec((ZLEN, 128), lambda hp, jj: (0, hp))
    btspec = pl.BlockSpec((2, BT_LEN, GRID_W, 128), lambda hp, jj: (hp, 0, 0, 0))
    ospec = pl.BlockSpec((Q_TILE, 128), lambda hp, jj: (lat(jj), hp))
    dqspec = pl.BlockSpec((Q_TILE, 128), lambda hp, jj: (jj, hp))
    zshape = jax.ShapeDtypeStruct((ZLEN, D_MODEL), F32)
    res, extra = _call(
        body, name="attn_bwd", grid=(NA_HEADS // 2, ZLEN // Q_TILE),
        in_specs=[qspec, qspec, kspec, kspec, btspec, ospec, ospec,
                  pl.BlockSpec((2, Q_TILE, 1), lambda hp, jj: (hp, lat(jj), 0))],
        out_specs=[dqspec, dqspec, kspec, kspec, btspec],
        out_shape=[zshape, zshape, zshape, zshape, jax.ShapeDtypeStruct((NA_HEADS, BT_LEN, GRID_W, 128), F32)],
        scratch_shapes=[pltpu.VMEM((Q_TILE, KEY_TILE + CTX_LEN), F32)], sem=("parallel", "arbitrary"),
        args=(q_rot, q_pl, kk, vv, bt, y_na, d_yna, lse), comm=comm)
    return (*res, extra)


def qkv_bwd(dq_rot, dq_pl, dk, dv, p, qg2, kg2, cos, sin, ones, comm=None):
    scale = HEAD_DIM ** -0.5
    n_hp, n_i = NA_HEADS // 2, ZLEN // PREP_TILE

    def norm_rope_bwd(d_rot, d_extra, x, gain, cos_t, sin_t, ones_m, dx_ref, acc_ref):
        xh, rstd = _head_rms(x, ones_m, 1.0)
        dn = d_rot * cos_t + _rope_partner(d_rot * sin_t)
        if d_extra is not None:
            dn = (dn + d_extra) * scale
        acc_ref[...] = acc_ref[...] + jnp.sum(dn * xh, axis=0, keepdims=True)
        dxh = dn * gain
        seg = _head_sum(dxh * xh, ones_m) * (1.0 / HEAD_DIM)
        dx_ref[...] = (rstd * (dxh - xh * seg)).astype(BF16)

    def body(dqr_ref, dqp_ref, dk_ref, dv_ref, xq_ref, xk_ref, qg_ref, kg_ref, cos_ref, sin_ref, ones_ref,
             dxq_ref, dxk_ref, dxv_ref, dgq_ref, dgk_ref, accq_ref, acck_ref):
        hp, i = pl.program_id(0), pl.program_id(1)

        @pl.when((hp == 0) & (i == 0))
        def _():
            accq_ref[...] = jnp.zeros_like(accq_ref)
            acck_ref[...] = jnp.zeros_like(acck_ref)

        ones_m = ones_ref[...]
        cos_t, sin_t = cos_ref[...], sin_ref[...]
        norm_rope_bwd(dqr_ref[...], dqp_ref[...], xq_ref[...], qg_ref[...], cos_t, sin_t, ones_m, dxq_ref, accq_ref)
        norm_rope_bwd(dk_ref[...], None, xk_ref[...], kg_ref[...], cos_t, sin_t, ones_m, dxk_ref, acck_ref)
        dxv_ref[...] = dv_ref[...].astype(BF16)

        @pl.when((hp == n_hp - 1) & (i == n_i - 1))
        def _():
            dgq_ref[...] = accq_ref[:, :HEAD_DIM] + accq_ref[:, HEAD_DIM:]
            dgk_ref[...] = acck_ref[:, :HEAD_DIM] + acck_ref[:, HEAD_DIM:]

    col = lambda base: pl.BlockSpec((PREP_TILE, 128), lambda hp, i: (i, base + hp))
    small = pl.BlockSpec((1, 128), lambda hp, i: (0, 0))
    tab = pl.BlockSpec((PREP_TILE, 128), lambda hp, i: (i, 0))
    zb = jax.ShapeDtypeStruct((ZLEN, D_MODEL), BF16)
    gshape = jax.ShapeDtypeStruct((1, HEAD_DIM), F32)
    res, extra = _call(
        body, name="qkv_bwd", grid=(n_hp, n_i),
        in_specs=[col(0)] * 4 + [col(32), col(8), small, small, tab, tab, _full((128, 128))],
        out_specs=[col(0)] * 3 + [_full((1, HEAD_DIM))] * 2,
        out_shape=[zb, zb, zb, gshape, gshape],
        scratch_shapes=[pltpu.VMEM((1, 128), F32)] * 2, sem=("arbitrary", "arbitrary"),
        args=(dq_rot, dq_pl, dk, dv, p, p, qg2, kg2, cos, sin, ones), comm=comm)
    return (*res, extra)


def rpb_grad(dbt):
    e, _ = _bias_expand()
    n_dr = 2 * NA_ROWS - 1
    ea = np.zeros((31, GRID_W, 128), np.float32)
    ea[:, :, :GRID_W] = e
    eb = np.zeros((31, GRID_W, 128), np.float32)
    eb[:, :, GRID_W:] = e
    eat = jnp.asarray(ea.reshape(31, GRID_W * 128).T.copy())
    ebt = jnp.asarray(eb.reshape(31, GRID_W * 128).T.copy())
    sel_at = np.zeros((n_dr, BT_LEN), np.float32)
    sel_bt = np.zeros((n_dr, BT_LEN), np.float32)
    for r in range(BT_LEN):
        dr = r - BT_PAD
        if 0 <= dr < n_dr:
            sel_at[dr, r] = 1.0
        if 0 <= dr + 1 < n_dr:
            sel_bt[dr + 1, r] = 1.0
    hi = lax.Precision.HIGHEST

    tk = 2048
    wide = GRID_W * 128
    rows = NA_HEADS * BT_LEN
    n_k = wide // tk

    def body(d_ref, sa_ref, sb_ref, ea_ref, eb_ref, o_ref, a_s, b_s):
        k = pl.program_id(0)
        dm = d_ref[...]
        d_hi = dm.astype(BF16)
        rest = dm - d_hi.astype(F32)
        d_mid = rest.astype(BF16)
        d_lo = (rest - d_mid.astype(F32)).astype(BF16)
        ea_b, eb_b = ea_ref[...].astype(BF16), eb_ref[...].astype(BF16)
        a = sum(jnp.dot(t, ea_b, preferred_element_type=F32) for t in (d_hi, d_mid, d_lo))
        b = sum(jnp.dot(t, eb_b, preferred_element_type=F32) for t in (d_hi, d_mid, d_lo))

        @pl.when(k == 0)
        def _():
            a_s[...] = a
            b_s[...] = b

        @pl.when(k > 0)
        def _():
            a_s[...] = a_s[...] + a
            b_s[...] = b_s[...] + b

        @pl.when(k == n_k - 1)
        def _():
            for h in range(NA_HEADS):
                sl = slice(h * BT_LEN, (h + 1) * BT_LEN)
                o_ref[h] = (jnp.dot(sa_ref[...], a_s[sl, :], preferred_element_type=F32, precision=hi)
                            + jnp.dot(sb_ref[...], b_s[sl, :], preferred_element_type=F32, precision=hi))

    return pl.pallas_call(
        body, name="rpb_grad", grid=(n_k,),
        in_specs=[pl.BlockSpec((rows, tk), lambda k: (0, k)), _full((n_dr, BT_LEN)), _full((n_dr, BT_LEN)),
                  pl.BlockSpec((tk, 31), lambda k: (k, 0)), pl.BlockSpec((tk, 31), lambda k: (k, 0))],
        out_specs=_full((NA_HEADS, n_dr, 31)),
        out_shape=jax.ShapeDtypeStruct((NA_HEADS, n_dr, 31), F32),
        scratch_shapes=[pltpu.VMEM((rows, 31), F32), pltpu.VMEM((rows, 31), F32)],
        compiler_params=_params("arbitrary"),
    )(dbt.reshape(rows, wide), jnp.asarray(sel_at), jnp.asarray(sel_bt), eat, ebt)


def lru_bwd(p, d_yrnn, conv_w, conv_b, wa, ba, wx, bx, lam, comm=None):
    blk, wspec = _lru_in_specs()

    def body(xr_ref, gx_ref, dy_ref, cw_ref, cb_ref, wa_ref, ba_ref, wx_ref, bx_ref, lam_ref,
             dxr_ref, dgx_ref, dcw_ref, dcb_ref, dwa_ref, dba_ref, dwx_ref, dbx_ref, dlam_ref,
             a_s, b_s, h_s, l_s, hsum_s, dxc_s, dh_s):
        xr = xr_ref[...]
        cw = cw_ref[...]
        xc = _lru_conv(xr, cw, cb_ref[...])
        xcb = xc.astype(BF16)
        g, dg = _gelu_parts(gx_ref[CTX_LEN:, :])
        dy = dy_ref[...]
        dh_s[:CTX_LEN, :] = jnp.zeros((CTX_LEN, LRU_BLOCK_W), F32)
        dh_s[CTX_LEN:, :] = dy * g
        row = _row_ids(ZLEN, LRU_BLOCK_W)
        zero = jnp.zeros((1, LRU_BLOCK_W), F32)
        for d in (0, 1):
            wab = wa_ref[d, 0].astype(BF16)
            wxb = wx_ref[d, 0].astype(BF16)
            lam_d = lam_ref[d:d + 1, :]
            r, gi, sp, a, sq, b = _lru_gates(xc, xcb, wab, ba_ref[d:d + 1, :], wxb, bx_ref[d:d + 1, :], lam_d)
            a_s[...] = a
            b_s[...] = b
            _lru_scan_dir(d, a_s, b_s, h_s)
            h = h_s[...]
            if d == 0:
                hsum_s[...] = h
                h_prev = jnp.where(row >= 1, pltpu.roll(h, 1, 0), 0.0)
                a_s[...] = pltpu.roll(a, ZLEN - 1, 0)
                _scan_down(a_s, dh_s, l_s, 0, N_CHUNK, zero)
            else:
                hsum_s[...] = hsum_s[...] + h
                h_prev = jnp.where(row == CTX_LEN - 1, 0.0, pltpu.roll(h, ZLEN - 1, 0))
                a_s[...] = pltpu.roll(a, 1, 0)
                c = _scan_up(a_s, dh_s, l_s, CTX_CHUNKS, N_CHUNK, zero)
                _scan_up(a_s, dh_s, l_s, 0, CTX_CHUNKS, c)
            db = l_s[...]
            da = db * h_prev
            dsq = db * gi * xc
            dgi = db * sq * xc
            dxc_d = db * sq * gi
            dla = da * a - dsq * (a * a) / sq
            dr = dla * ((-LRU_C) * sp)
            dsp = jnp.sum(dla * ((-LRU_C) * r), axis=0, keepdims=True)
            dlam_ref[d:d + 1, :] = -dsp * _sigmoid(-lam_d)
            dzr = dr * r * (1.0 - r)
            dzi = dgi * gi * (1.0 - gi)
            dba_ref[d:d + 1, :] = jnp.sum(dzr, axis=0, keepdims=True)
            dbx_ref[d:d + 1, :] = jnp.sum(dzi, axis=0, keepdims=True)
            dzrb = dzr.astype(BF16)
            dzib = dzi.astype(BF16)
            dwa_ref[d, 0] = _dot_tn(xcb, dzrb)
            dwx_ref[d, 0] = _dot_tn(xcb, dzib)
            dxc_d = dxc_d + _dot_nt(dzrb, wab) + _dot_nt(dzib, wxb)
            if d == 0:
                dxc_s[...] = dxc_d
            else:
                dxc_s[...] = dxc_s[...] + dxc_d
        dxc = dxc_s[...]
        dxr_ref[...] = _lru_conv_t(dxc, cw).astype(BF16)
        dcb_ref[...] = jnp.sum(dxc, axis=0, keepdims=True)
        segpos = jnp.where(row < CTX_LEN, row, row - CTX_LEN)
        seglen = jnp.where(row < CTX_LEN, CTX_LEN, SEQ)
        for k in range(4):
            off = k - 2
            if off == 0:
                sh = xr
            else:
                ok = (segpos + off >= 0) & (segpos + off < seglen)
                sh = jnp.where(ok, pltpu.roll(xr, (-off) % ZLEN, 0), 0.0)
            dcw_ref[k:k + 1, :] = jnp.sum(dxc * sh, axis=0, keepdims=True)
        dgx_ref[:CTX_LEN, :] = jnp.zeros((CTX_LEN, LRU_BLOCK_W), BF16)
        dgx_ref[CTX_LEN:, :] = (dy * hsum_s[CTX_LEN:, :] * dg).astype(BF16)

    zs = pltpu.VMEM((ZLEN, LRU_BLOCK_W), F32)
    zb = jax.ShapeDtypeStruct((ZLEN, D_MODEL), BF16)
    v2 = jax.ShapeDtypeStruct((2, D_MODEL), F32)
    w4 = jax.ShapeDtypeStruct((2, LRU_BLOCKS, LRU_BLOCK_W, LRU_BLOCK_W), F32)
    res, extra = _call(
        body, name="lru_bwd", grid=(LRU_BLOCKS,),
        in_specs=[blk(ZLEN), pl.BlockSpec((ZLEN, LRU_BLOCK_W), lambda b: (0, 24 + b)), blk(SEQ), blk(4), blk(1),
                  wspec, blk(2), wspec, blk(2), blk(2)],
        out_specs=[blk(ZLEN), blk(ZLEN), blk(4), blk(1), wspec, blk(2), wspec, blk(2), blk(2)],
        out_shape=[zb, zb, jax.ShapeDtypeStruct((4, D_MODEL), F32), jax.ShapeDtypeStruct((1, D_MODEL), F32),
                   w4, v2, w4, v2, v2],
        scratch_shapes=[zs] * 7, sem=("arbitrary",),
        args=(p, p, d_yrnn, conv_w, conv_b, wa, ba, wx, bx, lam), comm=comm)
    return (*res, extra)


def in_proj_bwd(dgs, w_in, z, dx1, gain, scale, comm=None):
    def body(*refs):
        dg_refs = refs[:7]
        w_ref, z_ref, dx1_ref, g_ref, sc_ref, gx_ref, dsh_ref, dsc_ref, dgn_ref = refs[7:]
        i = pl.program_id(0)
        dxn = _dot_nt(dg_refs[0][...], w_ref[:, 0:D_MODEL])
        for g in range(1, 7):
            dxn = dxn + _dot_nt(dg_refs[g][...], w_ref[:, g * D_MODEL:(g + 1) * D_MODEL])
        dx, dsh, dsc, dgn = _norm_mod_bwd(z_ref[...], dxn, g_ref[...], sc_ref[0])

        @pl.when(i <= 1)
        def _():
            dsh_ref[0] = dsh
            dsc_ref[0] = dsc

        @pl.when(i > 1)
        def _():
            dsh_ref[0] = dsh_ref[0] + dsh
            dsc_ref[0] = dsc_ref[0] + dsc

        @pl.when(i == 0)
        def _():
            dgn_ref[...] = dgn

        @pl.when(i > 0)
        def _():
            dgn_ref[...] = dgn_ref[...] + dgn
            gx_ref[...] = dx1_ref[...] + dx

    zrow = pl.BlockSpec((ROW_TILE, D_MODEL), lambda i: (i, 0))
    lat = pl.BlockSpec((ROW_TILE, D_MODEL), lambda i: (jnp.maximum(i - 1, 0), 0))
    mod = pl.BlockSpec((1, 1, D_MODEL), lambda i: (jnp.minimum(i, 1), 0, 0))
    mshape = jax.ShapeDtypeStruct((2, 1, D_MODEL), F32)
    res, extra = _call(
        body, name="in_proj_bwd", grid=(ZLEN // ROW_TILE,),
        in_specs=[zrow] * 7 + [_full((D_MODEL, IN_COLS)), zrow, lat, _full((1, D_MODEL)), mod],
        out_specs=[lat, mod, mod, _full((1, D_MODEL))],
        out_shape=[jax.ShapeDtypeStruct((SEQ, D_MODEL), F32), mshape, mshape, jax.ShapeDtypeStruct((1, D_MODEL), F32)],
        sem=("arbitrary",), args=(*dgs, w_in, z, dx1, gain, scale), comm=comm)
    return (*res, extra)


def matmul_tn(a, b, name, tm, tn, prev=None, col_block=0, total_cols=None):
    k, m = a.shape
    n = b.shape[1]
    total_cols = n if total_cols is None else total_cols
    assert m % tm == 0 and n % tn == 0
    off = col_block * (n // tn)

    def body(a_ref, b_ref, *rest):
        rest[-1][...] = _dot_tn(a_ref[...].astype(BF16), b_ref[...]).astype(BF16)

    in_specs = [pl.BlockSpec((k, tm), lambda i, j: (0, i)), pl.BlockSpec((k, tn), lambda i, j: (0, j))]
    args = [a, b]
    aliases = {}
    if prev is not None:
        in_specs.append(pl.BlockSpec(memory_space=pl.ANY))
        args.append(prev)
        aliases = {2: 0}
    return pl.pallas_call(
        body, name=name, grid=(m // tm, n // tn), in_specs=in_specs,
        out_specs=pl.BlockSpec((tm, tn), lambda i, j: (i, j + off)),
        out_shape=jax.ShapeDtypeStruct((m, total_cols), BF16),
        input_output_aliases=aliases,
        compiler_params=_params("parallel", "parallel"),
    )(*args)


def local_step(z, target, modx, modc, norm_mix_g, norm_ffn_g, w_in, conv_w, conv_b, wa, ba, wx, bx, lam, qg, kg, rpb,
               w_rnn, w_na, w_out, w_up, fconv_w, fconv_b, w_down, idx=None, bt=None):
    dist = idx is not None
    c_idx = idx[1:2] if dist else None
    d = D_MODEL
    mx = [modx[:, k * d:(k + 1) * d] for k in range(N_MOD)]
    shift = jnp.stack([modc[:, 0:d], mx[0]])
    scale = jnp.stack([modc[:, d:2 * d], mx[1]])
    cos, sin = _rope_tables()
    ones = _head_ones()
    qg2 = jnp.tile(qg, (1, 2))
    kg2 = jnp.tile(kg, (1, 2))

    xn = norm_mod(z, norm_mix_g, shift, scale, "norm_mix")
    if bt is None:
        bt, _ = bias_table(rpb)
    p, got = matmul_wide(xn, w_in, "in_proj", 3 * ROW_TILE, 1792,
                         comm=gather_weights_comm([w_rnn, w_na, w_out], [1, 2, 3]) if dist else None)
    if dist:
        w_rnn, w_na, w_out = got
    y_rnn, got = lru_fwd(p, conv_w, conv_b, wa, ba, wx, bx, lam,
                         comm=gather_weights_comm([w_down], [5]) if dist else None)
    if dist:
        w_down = got[0]
    q_rot, q_pl, kk, vv, _ = qkv_prep(p, qg2, kg2, cos, sin, ones)
    y_na, lse, got = attn_fwd(q_rot, q_pl, kk, vv, bt, comm=gather_weights_comm([w_up], [4]) if dist else None)
    if dist:
        w_up = got[0]
    u, v, merged, out, x1 = merge_fwd(y_rnn, y_na, p, z, mx[2], w_rnn, w_na, w_out)
    xn2 = norm_mod(x1, norm_ffn_g, mx[3][None], mx[4][None], "norm_ffn")
    hpre, _ = matmul_wide(xn2, w_up, "ffn_up", 2 * ROW_TILE, 1408)
    act = ffn_act(hpre, fconv_w, fconv_b)
    f, dy, df, loss_sq, dg5 = ffn_down_loss(act, w_down, x1, mx[5], target)

    partials, pieces = {}, {}

    def views_of(which, grads):
        return [_grad_view(g, BIG[w][1], BIG[w][2]) for w, g in zip(which, grads)]

    def chip_partials(which, views, recv):
        for w, gv, r in zip(which, views, recv):
            partials[w] = add_halves(gv, r, c_idx, "add_halves_" + BIG[w][0])
        return scatter_pieces_comm([partials[w] for w in which], which)

    d_act = ffn_down_bwd(df, w_down)
    dha, dhg, d_fcw_a, d_fcw_g, d_fcb_a, d_fcb_g = ffn_act_bwd(hpre, d_act, fconv_w, fconv_b)
    d_fcw = jnp.concatenate([d_fcw_a, d_fcw_g], axis=1)
    d_fcb = jnp.concatenate([d_fcb_a, d_fcb_g], axis=1)
    dx1, d_s3, d_s4, d_gffn = ffn_up_bwd(dha, dhg, w_up, x1, dy, norm_ffn_g, mx[4])
    g_w_down = matmul_tn(act, df, "gw_down", 256, D_MODEL)
    g_w_up = matmul_tn(xn2, dha, "gw_up_a", 512, 1408, total_cols=2 * D_FF)
    g_w_up = matmul_tn(xn2, dhg, "gw_up_g", 512, 1408, prev=g_w_up, col_block=1, total_cols=2 * D_FF)
    v_ffn = views_of([4, 5], [g_w_up, g_w_down]) if dist else None
    *mb, got = merge_bwd(dx1, out, mx[2], p, u, v, w_rnn, w_na, w_out,
                         comm=exchange_halves_comm(v_ffn) if dist else None)
    dout, du, dv, dmr, dmn, dyr, dyn, dg2 = mb
    recv_ffn = got
    g_w_out = matmul_tn(merged, dout, "gw_out", 1024, 512)
    g_w_rnn = matmul_tn(y_rnn, du, "gw_rnn", 1024, 512)
    g_w_na = matmul_tn(y_na, dv, "gw_na", 1024, 512)
    v_mix = views_of([1, 2, 3], [g_w_rnn, g_w_na, g_w_out]) if dist else None
    *lru_grads, got = lru_bwd(p, dyr, conv_w, conv_b, wa, ba, wx, bx, lam,
                              comm=join_comms(chip_partials([4, 5], v_ffn, recv_ffn),
                                              exchange_halves_comm(v_mix)) if dist else None)
    dxr, dgx, d_cw, d_cb, d_wa, d_ba, d_wx, d_bx, d_lam = lru_grads
    if dist:
        pieces[4], pieces[5] = got[:2]
    lru_w_all = {}
    dqr, dqp, dk, dvh, dbt, got = attn_bwd(
        q_rot, q_pl, kk, vv, bt, y_na, dyn, lse,
        comm=join_comms(chip_partials([1, 2, 3], v_mix, got[2:]),
                        join_comms(all_gather_comm(d_wa.reshape(-1, LRU_BLOCK_W)),
                                   all_gather_comm(d_wx.reshape(-1, LRU_BLOCK_W)))) if dist else None)
    if dist:
        pieces[1], pieces[2], pieces[3], lru_w_all["lru_wa"], lru_w_all["lru_wx"] = got
    dq_cols, dk_cols, dv_cols, d_qg, d_kg, _ = qkv_bwd(dqr, dqp, dk, dvh, p, qg2, kg2, cos, sin, ones)
    d_rpb = rpb_grad(dbt)
    dgs = [dxr, dk_cols, dv_cols, dgx, dq_cols, dmr, dmn]
    g_w_in = None
    for g in range(7):
        g_w_in = matmul_tn(xn, dgs[g], "gw_in_%d" % g, 1024, 512, prev=g_w_in, col_block=g, total_cols=IN_COLS)
    if dist:
        v_in = views_of([0], [g_w_in])
        recv_in = run_comm(exchange_halves_comm(v_in), "grad_exchange_w_in")
    grad_x, dsh, dsc, d_gmix, got = in_proj_bwd(dgs, w_in, z, dx1, norm_mix_g, scale,
                                                comm=chip_partials([0], v_in, recv_in) if dist else None)
    if dist:
        pieces[0] = got[0]

    d_modx = jnp.concatenate([dsh[1], dsc[1], dg2, d_s3, d_s4, dg5], axis=1)
    d_modc = jnp.concatenate([dsh[0], dsc[0]], axis=1)
    return dict(loss_sq=loss_sq, grad_x=grad_x, d_modx=d_modx, d_modc=d_modc, norm_mix_g=d_gmix, norm_ffn_g=d_gffn,
                w_in=g_w_in, lru_conv_w=d_cw, lru_conv_b=d_cb, lru_wa=d_wa, lru_ba=d_ba, lru_wx=d_wx, lru_bx=d_bx,
                lru_lambda=d_lam, q_norm_g=d_qg, k_norm_g=d_kg, na_rpb=d_rpb, w_rnn_out=g_w_rnn, w_na_out=g_w_na,
                w_out=g_w_out, w_up=g_w_up, ffn_conv_w=d_fcw, ffn_conv_b=d_fcb, w_down=g_w_down,
                partials=partials, pieces=pieces, lru_w_all=lru_w_all)


def _mesh_pos():
    return lax.axis_index("x"), lax.axis_index("y"), lax.axis_index("c")


def _other_chips(x, y):
    return [(1 - x, y), (x, 1 - y), (1 - x, 1 - y)]


BIG = (("w_in", (D_MODEL, IN_COLS), 1), ("w_rnn_out", (D_MODEL, D_MODEL), 0), ("w_na_out", (D_MODEL, D_MODEL), 0),
       ("w_out", (D_MODEL, D_MODEL), 0), ("w_up", (D_MODEL, 2 * D_FF), 1), ("w_down", (D_FF, D_MODEL), 0))


def _shard_shape(full, axis):
    r, c = full
    return (r // N_SHARD, c) if axis == 0 else (r, c // N_SHARD)


def _slot(ref, full, axis, s, h):
    r, c = full
    if axis == 0:
        rs = r // N_SHARD
        return ref.at[pl.ds(s * rs + h * (rs // 2), rs // 2), :]
    cs = c // N_SHARD
    return ref.at[pl.ds(h * (r // 2), r // 2), pl.ds(s * cs, cs)]


def cast_into_full(x, full, axis, idx, name):
    r, c = x.shape
    tr = next(t for t in (512, 352, 256, 128) if r % t == 0)
    nb = r // tr

    def body(idx_ref, x_ref, o_ref):
        o_ref[...] = x_ref[...].astype(BF16)

    if axis == 0:
        out_spec = pl.BlockSpec((tr, c), lambda i, idx_ref: (idx_ref[0] * nb + i, 0))
    else:
        out_spec = pl.BlockSpec((tr, c), lambda i, idx_ref: (i, idx_ref[0]))
    return pl.pallas_call(
        body, name=name,
        grid_spec=pltpu.PrefetchScalarGridSpec(
            num_scalar_prefetch=1, grid=(nb,), in_specs=[pl.BlockSpec((tr, c), lambda i, idx_ref: (i, 0))],
            out_specs=out_spec),
        out_shape=jax.ShapeDtypeStruct(full, BF16),
        compiler_params=_params("parallel"),
    )(idx, x)


def run_comm(comm, name):
    k_in, k_out = len(comm.inputs), len(comm.out_shapes)

    def body(*refs):
        start, mid, end = comm.emit(refs[:k_in], refs[k_in:k_in + k_out], refs[k_in + k_out:])
        start()
        mid()
        end()

    hbm = pl.BlockSpec(memory_space=pl.ANY)
    return pl.pallas_call(
        body, name=name, in_specs=[hbm] * k_in, out_specs=[hbm] * k_out, out_shape=list(comm.out_shapes),
        input_output_aliases=dict(comm.aliases), scratch_shapes=list(comm.scratch),
        compiler_params=pltpu.CompilerParams(vmem_limit_bytes=VMEM_LIMIT_V7X),
    )(*comm.inputs)


def gather_weights_comm(fulls, which):
    nw = len(which)
    specs = [BIG[w] for w in which]

    def emit(_, outs, sems):
        send1, recv1, send2, recv2 = sems
        x, y, c = _mesh_pos()
        sibling = (x, y, 1 - c)
        chips = _other_chips(x, y)
        s_me = 2 * x + y
        shards = [2 * chip[0] + chip[1] for chip in chips]

        def ici(w, j, shard):
            _, full, axis = specs[w]
            dst = _slot(outs[w], full, axis, shard, c)
            return pltpu.make_async_remote_copy(
                src_ref=dst, dst_ref=dst, send_sem=send1.at[3 * w + j],
                recv_sem=recv1.at[3 * w + j], device_id=(*chips[j], c), device_id_type=MESH_T)

        def d2d(w, j, shard, half):
            _, full, axis = specs[w]
            dst = _slot(outs[w], full, axis, shard, half)
            return pltpu.make_async_remote_copy(
                src_ref=dst, dst_ref=dst, send_sem=send2.at[3 * w + j], recv_sem=recv2.at[3 * w + j],
                device_id=sibling, device_id_type=MESH_T)

        pairs = [(w, j) for w in range(nw) for j in range(3)]

        def start():
            for w, j in pairs:
                ici(w, j, s_me).start()

        def mid():
            for w, j in pairs:
                ici(w, j, shards[j]).wait_recv()
                d2d(w, j, shards[j], c).start()

        def end():
            for w, j in pairs:
                d2d(w, j, shards[j], 1 - c).wait_recv()
            for w, j in pairs:
                ici(w, j, s_me).wait_send()
                d2d(w, j, shards[j], c).wait_send()

        return start, mid, end

    return Comm(list(fulls), [jax.ShapeDtypeStruct(full, BF16) for _, full, _ in specs], {i: i for i in range(nw)},
                [pltpu.SemaphoreType.DMA((3 * nw,))] * 4, emit)


def join_comms(a, b):
    ai, ao, asc = len(a.inputs), len(a.out_shapes), len(a.scratch)

    def emit(ins, outs, sems):
        fa = a.emit(ins[:ai], outs[:ao], sems[:asc])
        fb = b.emit(ins[ai:], outs[ao:], sems[asc:])

        def both(k):
            def run():
                fa[k]()
                fb[k]()
            return run

        return both(0), both(1), both(2)

    aliases = dict(a.aliases)
    aliases.update({ai + i: ao + o for i, o in b.aliases.items()})
    return Comm(a.inputs + b.inputs, a.out_shapes + b.out_shapes, aliases, a.scratch + b.scratch, emit)


def all_gather_comm(x):
    def emit(srcs, outs, sems):
        send_sems, recv_sems, local_sem = sems
        x_ref, out_ref = srcs[0], outs[0]
        x, y, c = _mesh_pos()
        me, sibling = (x, y, c), (x, y, 1 - c)
        chips = _other_chips(x, y)

        def blk(px, py, pc):
            return out_ref.at[4 * px + 2 * py + pc]

        def copy(k, block, to, src=None):
            return pltpu.make_async_remote_copy(
                src_ref=blk(*block) if src is None else src, dst_ref=blk(*block),
                send_sem=send_sems.at[k], recv_sem=recv_sems.at[k], device_id=to, device_id_type=MESH_T)

        def mine():
            return pltpu.make_async_copy(x_ref, blk(*me), local_sem)

        def start():
            mine().start()
            copy(0, me, sibling, src=x_ref).start()
            for j, chip in enumerate(chips):
                copy(1 + j, me, (*chip, c), src=x_ref).start()

        def mid():
            for j, chip in enumerate(chips):
                copy(1 + j, (*chip, c), me).wait_recv()
                copy(4 + j, (*chip, c), sibling).start()

        def end():
            copy(0, sibling, me).wait_recv()
            for j, chip in enumerate(chips):
                copy(4 + j, (*chip, 1 - c), me).wait_recv()
            copy(0, me, sibling, src=x_ref).wait_send()
            for j, chip in enumerate(chips):
                copy(1 + j, me, (*chip, c), src=x_ref).wait_send()
                copy(4 + j, (*chip, c), sibling).wait_send()
            mine().wait()

        return start, mid, end

    return Comm([x], [jax.ShapeDtypeStruct((N_DEV,) + x.shape, F32)], {},
                [pltpu.SemaphoreType.DMA((7,)), pltpu.SemaphoreType.DMA((7,)), pltpu.SemaphoreType.DMA], emit)


def sum_blocks(g, name):
    _, r, c = g.shape
    tr = 256 if r % 256 == 0 else r

    def body(g_ref, o_ref):
        acc = g_ref[0]
        for k in range(1, N_DEV):
            acc = acc + g_ref[k]
        o_ref[...] = acc

    return pl.pallas_call(
        body, name=name, grid=(r // tr,),
        in_specs=[pl.BlockSpec((N_DEV, tr, c), lambda i: (0, i, 0))],
        out_specs=pl.BlockSpec((tr, c), lambda i: (i, 0)),
        out_shape=jax.ShapeDtypeStruct((r, c), F32),
        compiler_params=_params("parallel"),
    )(g)


def _grad_view(g, full, axis):
    r, c = full
    if axis == 0:
        return g.reshape(N_SHARD, 2, r // N_SHARD // 2, c)
    return g.reshape(1, 2, r // 2, c)


def exchange_halves_comm(gviews):
    nw = len(gviews)

    def emit(srcs, outs, sems):
        send_sems, recv_sems = sems
        x, y, c = _mesh_pos()

        def copies():
            return [pltpu.make_async_remote_copy(
                src_ref=srcs[w].at[:, pl.ds(1 - c, 1)], dst_ref=outs[w], send_sem=send_sems.at[w],
                recv_sem=recv_sems.at[w], device_id=(x, y, 1 - c), device_id_type=MESH_T) for w in range(nw)]

        def start():
            for cp in copies():
                cp.start()

        def end():
            for cp in copies():
                cp.wait()

        return start, lambda: None, end

    return Comm(list(gviews), [jax.ShapeDtypeStruct((g.shape[0], 1) + g.shape[2:], BF16) for g in gviews], {},
                [pltpu.SemaphoreType.DMA((nw,)), pltpu.SemaphoreType.DMA((nw,))], emit)


def _row_tile(rh):
    return 128 if rh % 128 == 0 else rh


def add_halves(gview, recv, c_idx, name):
    a, _, rh, cc = gview.shape
    tr = _row_tile(rh)

    def body(c_ref, g_ref, r_ref, o_ref):
        o_ref[0] = (g_ref[0, 0].astype(F32) + r_ref[0, 0].astype(F32)).astype(BF16)

    return pl.pallas_call(
        body, name=name,
        grid_spec=pltpu.PrefetchScalarGridSpec(
            num_scalar_prefetch=1, grid=(a, rh // tr),
            in_specs=[pl.BlockSpec((1, 1, tr, cc), lambda s, i, c_ref: (s, c_ref[0], i, 0)),
                      pl.BlockSpec((1, 1, tr, cc), lambda s, i, c_ref: (s, 0, i, 0))],
            out_specs=pl.BlockSpec((1, tr, cc), lambda s, i, c_ref: (s, i, 0))),
        out_shape=jax.ShapeDtypeStruct((a, rh, cc), BF16),
        compiler_params=_params("parallel", "parallel"),
    )(c_idx, gview, recv)


def _piece_shape(full, axis):
    rs, cs = _shard_shape(full, axis)
    return (rs // 2, cs)


def scatter_pieces_comm(partials, which):
    nw = len(which)
    specs = [BIG[w] for w in which]

    def emit(srcs, outs, sems):
        send_sems, recv_sems = sems
        x, y, c = _mesh_pos()
        chips = _other_chips(x, y)

        def copies():
            cps = []
            for w, (_, full, axis) in enumerate(specs):
                cs = full[1] // N_SHARD
                for j, chip in enumerate(chips):
                    s_j = 2 * chip[0] + chip[1]
                    src = srcs[w].at[s_j] if axis == 0 else srcs[w].at[0, :, pl.ds(s_j * cs, cs)]
                    cps.append(pltpu.make_async_remote_copy(
                        src_ref=src, dst_ref=outs[w].at[j], send_sem=send_sems.at[3 * w + j],
                        recv_sem=recv_sems.at[3 * w + j], device_id=(*chip, c), device_id_type=MESH_T))
            return cps

        def start():
            for cp in copies():
                cp.start()

        def mid():
            pass

        def end():
            for cp in copies():
                cp.wait()

        return start, mid, end

    return Comm(list(partials), [jax.ShapeDtypeStruct((3,) + _piece_shape(full, axis), BF16) for _, full, axis in specs],
                {}, [pltpu.SemaphoreType.DMA((3 * nw,)), pltpu.SemaphoreType.DMA((3 * nw,))], emit)


def add_pieces(partial, recv, idx, axis, name):
    _, rh, cs = recv.shape
    tr = _row_tile(rh)

    def body(idx_ref, p_ref, r_ref, o_ref):
        o_ref[0] = ((p_ref[0].astype(F32) + r_ref[0].astype(F32)) + r_ref[1].astype(F32)) + r_ref[2].astype(F32)

    if axis == 0:
        pspec = pl.BlockSpec((1, tr, cs), lambda i, idx_ref: (idx_ref[0], i, 0))
    else:
        pspec = pl.BlockSpec((1, tr, cs), lambda i, idx_ref: (0, i, idx_ref[0]))
    return pl.pallas_call(
        body, name=name,
        grid_spec=pltpu.PrefetchScalarGridSpec(
            num_scalar_prefetch=1, grid=(rh // tr,),
            in_specs=[pspec, pl.BlockSpec((3, tr, cs), lambda i, idx_ref: (0, i, 0))],
            out_specs=pl.BlockSpec((1, tr, cs), lambda i, idx_ref: (idx_ref[1], i, 0))),
        out_shape=jax.ShapeDtypeStruct((2, rh, cs), F32),
        compiler_params=_params("parallel"),
    )(idx, partial, recv)


def join_halves_comm(halves):
    nw = len(halves)

    def emit(_, outs, sems):
        send_sems, recv_sems = sems
        x, y, c = _mesh_pos()

        def copy(w, half):
            return pltpu.make_async_remote_copy(
                src_ref=outs[w].at[half], dst_ref=outs[w].at[half], send_sem=send_sems.at[w], recv_sem=recv_sems.at[w],
                device_id=(x, y, 1 - c), device_id_type=MESH_T)

        def start():
            for w in range(nw):
                copy(w, c).start()

        def end():
            for w in range(nw):
                copy(w, c).wait_send()
                copy(w, 1 - c).wait_recv()

        return start, lambda: None, end

    return Comm(list(halves), [jax.ShapeDtypeStruct(h.shape, F32) for h in halves], {i: i for i in range(nw)},
                [pltpu.SemaphoreType.DMA((nw,))] * 2, emit)


MOD_COLS = N_MOD * D_MODEL // N_SHARD
MOD_TILE = 512


def mod_fwd(c16, w_mod):
    def body(c_ref, w_ref, s_ref, o_ref):
        cv = c_ref[...]
        s = cv * _sigmoid(cv)
        s_ref[...] = s
        o_ref[...] = jnp.dot(s.astype(BF16), w_ref[...].astype(BF16), preferred_element_type=F32)

    return pl.pallas_call(
        body, name="mod_fwd", grid=(MOD_COLS // MOD_TILE,),
        in_specs=[_full((16, D_MODEL)), pl.BlockSpec((D_MODEL, MOD_TILE), lambda j: (0, j))],
        out_specs=[_full((16, D_MODEL)), pl.BlockSpec((16, MOD_TILE), lambda j: (0, j))],
        out_shape=[jax.ShapeDtypeStruct((16, D_MODEL), F32), jax.ShapeDtypeStruct((16, MOD_COLS), F32)],
        compiler_params=_params("arbitrary"),
    )(c16, w_mod)


def mod_bwd(s16, dm16, w_mod):
    hi = lax.Precision.HIGHEST

    def body(s_ref, d_ref, w_ref, gw_ref, ds_ref):
        j = pl.program_id(0)
        dm = d_ref[...]
        gw_ref[...] = lax.dot_general(s_ref[...], dm, (((0,), (0,)), ((), ())), preferred_element_type=F32, precision=hi)
        part = lax.dot_general(dm, w_ref[...], (((1,), (1,)), ((), ())), preferred_element_type=F32, precision=hi)

        @pl.when(j == 0)
        def _():
            ds_ref[...] = part

        @pl.when(j > 0)
        def _():
            ds_ref[...] = ds_ref[...] + part

    return pl.pallas_call(
        body, name="mod_bwd", grid=(MOD_COLS // MOD_TILE,),
        in_specs=[_full((16, D_MODEL)), pl.BlockSpec((16, MOD_TILE), lambda j: (0, j)),
                  pl.BlockSpec((D_MODEL, MOD_TILE), lambda j: (0, j))],
        out_specs=[pl.BlockSpec((D_MODEL, MOD_TILE), lambda j: (0, j)), _full((16, D_MODEL))],
        out_shape=[jax.ShapeDtypeStruct((D_MODEL, MOD_COLS), F32), jax.ShapeDtypeStruct((16, D_MODEL), F32)],
        compiler_params=_params("arbitrary"),
    )(s16, dm16, w_mod)


def cctx_grad(parts, c_ctx):
    def body(p_ref, c_ref, o_ref):
        ds = p_ref[0:1, :]
        for s in range(1, N_SHARD):
            ds = ds + p_ref[16 * s:16 * s + 1, :]
        cv = c_ref[...]
        sg = _sigmoid(cv)
        o_ref[...] = ds * (sg * (1.0 + cv * (1.0 - sg)))

    return pl.pallas_call(
        body, name="cctx_grad", in_specs=[_full((N_DEV * 8, D_MODEL)), _full((1, D_MODEL))],
        out_specs=_full((1, D_MODEL)), out_shape=jax.ShapeDtypeStruct((1, D_MODEL), F32),
    )(parts, c_ctx)


def add_rows(a, b, name):
    def body(a_ref, b_ref, o_ref):
        o_ref[...] = a_ref[...] + b_ref[...]

    return pl.pallas_call(body, name=name, in_specs=[_full(a.shape), _full(b.shape)], out_specs=_full(a.shape),
                          out_shape=jax.ShapeDtypeStruct(a.shape, F32))(a, b)


def _adamw_update(w_ref, g_ref, m_ref, v_ref, d_ref, nm_ref, nv_ref):
    g_ = g_ref[...]
    m_ = ADAM_B1 * m_ref[...] + (1.0 - ADAM_B1) * g_
    v_ = ADAM_B2 * v_ref[...] + (1.0 - ADAM_B2) * (g_ * g_)
    m_hat = m_ / (1.0 - ADAM_B1 ** ADAM_STEP)
    v_hat = v_ / (1.0 - ADAM_B2 ** ADAM_STEP)
    d_ref[...] = -ADAM_LR * (m_hat / (jnp.sqrt(v_hat) + ADAM_EPS) + ADAM_WD * w_ref[...])
    nm_ref[...] = m_
    nv_ref[...] = v_


def adamw_many(ws, gs, ms, vs):
    n = len(ws)

    def body(*refs):
        for i in range(n):
            _adamw_update(*[refs[k * n + i] for k in range(7)])

    shapes = [jax.ShapeDtypeStruct(w.shape, F32) for w in ws]
    return pl.pallas_call(body, name="adamw_small", out_shape=shapes * 3,
                          compiler_params=pltpu.CompilerParams(vmem_limit_bytes=VMEM_LIMIT_V7X))(*ws, *gs, *ms, *vs)


def adamw(w, g, m, v, name, comm=None):
    r, c = w.shape
    tr = 128 if (r % 128 == 0 and r > 128) else r

    def body(w_ref, g_ref, m_ref, v_ref, d_ref, nm_ref, nv_ref):
        _adamw_update(w_ref, g_ref, m_ref, v_ref, d_ref, nm_ref, nv_ref)

    spec = pl.BlockSpec((tr, c), lambda i: (i, 0))
    shp = jax.ShapeDtypeStruct((r, c), F32)
    res, extra = _call(body, name=name, grid=(r // tr,), in_specs=[spec] * 4, out_specs=[spec] * 3,
                       out_shape=[shp] * 3, sem=("parallel",), args=(w, g, m, v), comm=comm)
    return (*res, extra)


LANES = 1024


def _pack(arrs):
    rows, spans, at = [], [], 0
    for a in arrs:
        n = int(np.prod(a.shape))
        nr = 8 * -(-n // (8 * LANES))
        flat = a.reshape(-1)
        if nr * LANES != n:
            flat = jnp.concatenate([flat, jnp.zeros((nr * LANES - n,), F32)])
        rows.append(flat.reshape(nr, LANES))
        spans.append((at, nr, n, a.shape))
        at += nr
    return jnp.concatenate(rows, axis=0), spans


def _unpack(buf, spans):
    out = []
    for at, nr, n, shape in spans:
        out.append(buf[at:at + nr].reshape(-1)[:n].reshape(shape))
    return out


SMALL_SHARD = ("lru_conv_w", "lru_ba", "lru_bx", "lru_lambda", "ffn_conv_w")


def kernel(x, c, ctx, c_ctx, w_mod, b_mod, norm_mix_g, norm_ffn_g, w_in, lru_conv_w, lru_conv_b, lru_wa, lru_ba, lru_wx, lru_bx, lru_lambda, q_norm_g, k_norm_g, na_rpb, w_rnn_out, w_na_out, w_out, w_up, ffn_conv_w, ffn_conv_b, w_down, loss_target, m_c_ctx, m_w_mod, m_b_mod, m_norm_mix_g, m_norm_ffn_g, m_w_in, m_lru_conv_w, m_lru_conv_b, m_lru_wa, m_lru_ba, m_lru_wx, m_lru_bx, m_lru_lambda, m_q_norm_g, m_k_norm_g, m_na_rpb, m_w_rnn_out, m_w_na_out, m_w_out, m_w_up, m_ffn_conv_w, m_ffn_conv_b, m_w_down, v_c_ctx, v_w_mod, v_b_mod, v_norm_mix_g, v_norm_ffn_g, v_w_in, v_lru_conv_w, v_lru_conv_b, v_lru_wa, v_lru_ba, v_lru_wx, v_lru_bx, v_lru_lambda, v_q_norm_g, v_k_norm_g, v_na_rpb, v_w_rnn_out, v_w_na_out, v_w_out, v_w_up, v_ffn_conv_w, v_ffn_conv_b, v_w_down):
    weights = dict(c_ctx=c_ctx, w_mod=w_mod, b_mod=b_mod, norm_mix_g=norm_mix_g, norm_ffn_g=norm_ffn_g, w_in=w_in,
                   lru_conv_w=lru_conv_w, lru_conv_b=lru_conv_b, lru_wa=lru_wa, lru_ba=lru_ba, lru_wx=lru_wx,
                   lru_bx=lru_bx, lru_lambda=lru_lambda, q_norm_g=q_norm_g, k_norm_g=k_norm_g, na_rpb=na_rpb,
                   w_rnn_out=w_rnn_out, w_na_out=w_na_out, w_out=w_out, w_up=w_up, ffn_conv_w=ffn_conv_w,
                   ffn_conv_b=ffn_conv_b, w_down=w_down)
    mom1 = dict(c_ctx=m_c_ctx, w_mod=m_w_mod, b_mod=m_b_mod, norm_mix_g=m_norm_mix_g, norm_ffn_g=m_norm_ffn_g,
                w_in=m_w_in, lru_conv_w=m_lru_conv_w, lru_conv_b=m_lru_conv_b, lru_wa=m_lru_wa, lru_ba=m_lru_ba,
                lru_wx=m_lru_wx, lru_bx=m_lru_bx, lru_lambda=m_lru_lambda, q_norm_g=m_q_norm_g, k_norm_g=m_k_norm_g,
                na_rpb=m_na_rpb, w_rnn_out=m_w_rnn_out, w_na_out=m_w_na_out, w_out=m_w_out, w_up=m_w_up,
                ffn_conv_w=m_ffn_conv_w, ffn_conv_b=m_ffn_conv_b, w_down=m_w_down)
    mom2 = dict(c_ctx=v_c_ctx, w_mod=v_w_mod, b_mod=v_b_mod, norm_mix_g=v_norm_mix_g, norm_ffn_g=v_norm_ffn_g,
                w_in=v_w_in, lru_conv_w=v_lru_conv_w, lru_conv_b=v_lru_conv_b, lru_wa=v_lru_wa, lru_ba=v_lru_ba,
                lru_wx=v_lru_wx, lru_bx=v_lru_bx, lru_lambda=v_lru_lambda, q_norm_g=v_q_norm_g, k_norm_g=v_k_norm_g,
                na_rpb=v_na_rpb, w_rnn_out=v_w_rnn_out, w_na_out=v_w_na_out, w_out=v_w_out, w_up=v_w_up,
                ffn_conv_w=v_ffn_conv_w, ffn_conv_b=v_ffn_conv_b, w_down=v_w_down)
    order = list(weights)
    d = D_MODEL
    mx_, my_, mc_ = _mesh_pos()
    shard = 2 * mx_ + my_
    dev = 2 * shard + mc_

    idx = jnp.stack([shard, mc_]).astype(jnp.int32)
    wsh = {name: cast_into_full(weights[name][0], full, axis, idx, "cast_" + name) for name, full, axis in BIG}
    local_small, small_spans = _pack([c] + [weights[k][0] for k in SMALL_SHARD])
    bt, (w_in_full, gath) = bias_table(na_rpb[0], comm=join_comms(gather_weights_comm([wsh["w_in"]], [0]),
                                                                  all_gather_comm(local_small)))
    per_dev = [_unpack(gath[k], small_spans) for k in range(N_DEV)]
    c_all = jnp.concatenate([per_dev[k][0] for k in range(N_DEV)], axis=0)
    full_small = {name: jnp.concatenate([per_dev[2 * s][1 + i] for s in range(N_SHARD)], axis=-1)
                  for i, name in enumerate(SMALL_SHARD)}
    c16 = jnp.concatenate([c_all, c_ctx.reshape(1, d), jnp.zeros((7, d), F32)], axis=0)
    s16, mod_part = mod_fwd(c16, w_mod[0])
    mod_all = run_comm(all_gather_comm(mod_part), "gather_mod")[0]
    mod = jnp.concatenate([mod_all[2 * s] for s in range(N_SHARD)], axis=1) + b_mod
    modx = lax.dynamic_slice(mod, (dev, 0), (1, N_MOD * d))
    modc = mod[8:9]

    z = jnp.concatenate([ctx[0], x[0]], axis=0)
    res = local_step(z, loss_target[0], modx, modc, norm_mix_g, norm_ffn_g, w_in_full, full_small["lru_conv_w"],
                     lru_conv_b, lru_wa[0], full_small["lru_ba"], lru_wx[0], full_small["lru_bx"],
                     full_small["lru_lambda"], q_norm_g, k_norm_g, na_rpb[0], wsh["w_rnn_out"], wsh["w_na_out"],
                     wsh["w_out"], wsh["w_up"], full_small["ffn_conv_w"], ffn_conv_b, wsh["w_down"], idx=idx, bt=bt)

    halves = [add_pieces(res["partials"][i], res["pieces"][i], idx, BIG[i][2], "add_pieces_" + BIG[i][0])
              for i in range(len(BIG))]
    lru_tot = {k: sum_blocks(res["lru_w_all"][k], "sum_" + k).reshape(weights[k].shape[1:])
               for k in ("lru_wa", "lru_wx")}
    small_names = ["norm_mix_g", "norm_ffn_g", "lru_conv_w", "lru_conv_b", "lru_ba", "lru_bx",
                   "lru_lambda", "q_norm_g", "k_norm_g", "na_rpb", "ffn_conv_w", "ffn_conv_b"]
    local_g, g_spans = _pack([res["loss_sq"][0:1, 0:1], res["d_modx"], res["d_modc"]] + [res[k] for k in small_names])
    n_rows = local_g.shape[0]
    *joined, g_all = run_comm(join_comms(join_halves_comm(halves), all_gather_comm(local_g)), "tail_exchange")
    grads = {name: joined[i].reshape(_shard_shape(full, axis)) for i, (name, full, axis) in enumerate(BIG)}
    grads.update(lru_tot)
    g_tot = sum_blocks(g_all, "sum_small")
    tot = _unpack(g_tot, g_spans)
    loss = (0.5 / d) * tot[0][0, 0]
    small_tot = dict(zip(small_names, tot[3:]))
    at_x = g_spans[1][0]
    dmx_rows = g_all.reshape(N_DEV, n_rows, LANES)[:, at_x:at_x + N_MOD, :].reshape(N_DEV, N_MOD * d)
    dmc_row = jnp.concatenate([tot[2], jnp.zeros((1, 4 * d), F32)], axis=1)
    dm16 = jnp.concatenate([dmx_rows, dmc_row, jnp.zeros((7, N_MOD * d), F32)], axis=0)
    grads["b_mod"] = add_rows(tot[1], dmc_row, "b_mod_grad")
    g_w_mod, ds16 = mod_bwd(s16, lax.dynamic_slice(dm16, (0, shard * MOD_COLS), (16, MOD_COLS)), w_mod[0])
    grads["w_mod"] = g_w_mod
    for k in small_names:
        g = small_tot[k]
        if k in SMALL_SHARD:
            w_sh = weights[k].shape[-1]
            g = lax.dynamic_slice_in_dim(g, shard * w_sh, w_sh, axis=g.ndim - 1)
        grads[k] = g

    delta, new_m, new_v = {}, {}, {}
    for name, _, _ in BIG + (("w_mod", None, None),):
        *upd, got = adamw(weights[name][0], grads[name], mom1[name][0], mom2[name][0], "adamw_" + name,
                          comm=all_gather_comm(ds16[8:16]) if name == "w_in" else None)
        delta[name], new_m[name], new_v[name] = upd
        if name == "w_in":
            grads["c_ctx"] = cctx_grad(got[0].reshape(N_DEV * 8, d), c_ctx.reshape(1, d))
    rest = [k for k in order if k not in delta]
    views = {k: (grads[k].shape if grads[k].ndim <= 3 else (-1, grads[k].shape[-1])) for k in rest}
    small = adamw_many(*[[t[k].reshape(views[k]) for k in rest] for t in (weights, grads, mom1, mom2)])
    n_rest = len(rest)
    for i, k in enumerate(rest):
        delta[k], new_m[k], new_v[k] = small[i], small[n_rest + i], small[2 * n_rest + i]

    shaped = lambda t: [t[k].reshape(weights[k].shape) for k in order]
    return (loss, res["grad_x"][None], *shaped(grads), *shaped(delta), *shaped(new_m), *shaped(new_v))
```

```python
import numpy as np
import jax
import jax.numpy as jnp
from jax import lax
from jax.experimental import pallas as pl
from jax.experimental.pallas import tpu as pltpu

F32 = jnp.float32
BF16 = jnp.bfloat16

D_MODEL = 1024
SEQ = 2048
CTX_LEN = 256
ZLEN = SEQ + CTX_LEN
GRID_W = 64
GRID_ROWS = SEQ // GRID_W
LRU_BLOCK_W = 128
LRU_BLOCKS = 8
LRU_C = 8.0
NA_HEADS = 16
HEAD_DIM = 64
NA_ROWS = 8
NA_COLS = 16
ROPE_BASE = 10000.0
D_FF = 2816
N_MOD = 6
IN_COLS = 7 * D_MODEL
EPS = 1e-6
NEG_INF = -1e30
N_DEV = 8
N_SHARD = 4

ADAM_LR = 0.001
ADAM_B1 = 0.9
ADAM_B2 = 0.999
ADAM_EPS = 1e-08
ADAM_WD = 0.01
ADAM_STEP = 10

ROW_TILE = 256
Q_ROWS = 4
Q_TILE = Q_ROWS * GRID_W
KEY_ROWS = 12
KEY_TILE = KEY_ROWS * GRID_W
BT_PAD = 4
BT_LEN = 24
VMEM_LIMIT_V7X = 56 * 1024 * 1024

MESH_T = pl.DeviceIdType.MESH


def _params(*sem):
    return pltpu.CompilerParams(dimension_semantics=sem if sem else None, vmem_limit_bytes=VMEM_LIMIT_V7X)


def _full(shape):
    nd = len(shape)
    return pl.BlockSpec(shape, lambda *_: (0,) * nd)


class Comm:
    def __init__(self, inputs, out_shapes, aliases, scratch, emit):
        self.inputs, self.out_shapes, self.aliases, self.scratch, self.emit = inputs, out_shapes, aliases, scratch, emit


def _call(body, *, name, grid, in_specs, out_specs, out_shape, args, scratch_shapes=(), sem=(), comm=None):
    n_in, n_out, n_sc = len(in_specs), len(out_specs), len(scratch_shapes)
    if comm is None:
        res = pl.pallas_call(body, name=name, grid=grid, in_specs=list(in_specs), out_specs=list(out_specs),
                             out_shape=list(out_shape), scratch_shapes=list(scratch_shapes),
                             compiler_params=_params(*sem))(*args)
        return list(res), []
    k_in, k_out = len(comm.inputs), len(comm.out_shapes)
    steps = int(np.prod(grid))

    def hosted(*refs):
        ins, cins = refs[:n_in], refs[n_in:n_in + k_in]
        at = n_in + k_in
        outs, couts = refs[at:at + n_out], refs[at + n_out:at + n_out + k_out]
        at += n_out + k_out
        scr, cscr = refs[at:at + n_sc], refs[at + n_sc:]
        start, mid, end = comm.emit(cins, couts, cscr)
        lin = pl.program_id(0)
        for ax in range(1, len(grid)):
            lin = lin * grid[ax] + pl.program_id(ax)
        pl.when(lin == 0)(start)
        body(*ins, *outs, *scr)
        pl.when(lin == steps - 1 - steps // 7)(mid)
        pl.when(lin == steps - 1)(end)

    hbm = pl.BlockSpec(memory_space=pl.ANY)
    res = pl.pallas_call(
        hosted, name=name, grid=grid, in_specs=list(in_specs) + [hbm] * k_in, out_specs=list(out_specs) + [hbm] * k_out,
        out_shape=list(out_shape) + list(comm.out_shapes), scratch_shapes=list(scratch_shapes) + list(comm.scratch),
        input_output_aliases={n_in + i: n_out + o for i, o in comm.aliases.items()},
        compiler_params=_params(*(("arbitrary",) * len(grid))))(*args, *comm.inputs)
    return list(res[:n_out]), list(res[n_out:])


def _sigmoid(x):
    return 0.5 * jnp.tanh(0.5 * x) + 0.5


def _gelu_parts(x):
    c0 = 0.7978845608028654
    inner = c0 * (x + 0.044715 * x * x * x)
    t = jnp.tanh(inner)
    g = 0.5 * x * (1.0 + t)
    dg = 0.5 * (1.0 + t) + 0.5 * x * (1.0 - t * t) * c0 * (1.0 + 3.0 * 0.044715 * x * x)
    return g, dg


def _dot_nt(a, b):
    return lax.dot_general(a, b, (((1,), (1,)), ((), ())), preferred_element_type=F32)


def _dot_tn(a, b):
    return lax.dot_general(a, b, (((0,), (0,)), ((), ())), preferred_element_type=F32)


def norm_mod(xin, gain, shift, scale, name):
    r, d = xin.shape
    s_mod = shift.shape[0]
    assert r % ROW_TILE == 0

    def body(x_ref, g_ref, sh_ref, sc_ref, xn_ref):
        x = x_ref[...]
        nrm = x * lax.rsqrt(jnp.mean(x * x, axis=-1, keepdims=True) + EPS)
        xn_ref[...] = ((nrm * g_ref[...]) * (1.0 + sc_ref[0]) + sh_ref[0]).astype(BF16)

    mod_spec = pl.BlockSpec((1, 1, d), lambda i: (jnp.minimum(i, s_mod - 1), 0, 0))
    return pl.pallas_call(
        body, name=name, grid=(r // ROW_TILE,),
        in_specs=[pl.BlockSpec((ROW_TILE, d), lambda i: (i, 0)), _full((1, d)), mod_spec, mod_spec],
        out_specs=pl.BlockSpec((ROW_TILE, d), lambda i: (i, 0)),
        out_shape=jax.ShapeDtypeStruct((r, d), BF16),
        compiler_params=_params("parallel"),
    )(xin, gain, shift, scale)


def matmul_wide(a, b, name, tm, tn, comm=None):
    m, k = a.shape
    n = b.shape[1]
    assert m % tm == 0 and n % tn == 0

    def body(a_ref, b_ref, o_ref):
        o_ref[...] = jnp.dot(a_ref[...], b_ref[...], preferred_element_type=F32)

    res, extra = _call(
        body, name=name, grid=(n // tn, m // tm),
        in_specs=[pl.BlockSpec((tm, k), lambda j, i: (i, 0)), pl.BlockSpec((k, tn), lambda j, i: (0, j))],
        out_specs=[pl.BlockSpec((tm, tn), lambda j, i: (i, j))],
        out_shape=[jax.ShapeDtypeStruct((m, n), F32)],
        sem=("parallel", "parallel"), args=(a, b), comm=comm)
    return res[0], extra


def _row_ids(n, w):
    return lax.broadcasted_iota(jnp.int32, (n, w), 0)


def _lru_conv(xr, cw, cb):
    row = _row_ids(ZLEN, LRU_BLOCK_W)
    segpos = jnp.where(row < CTX_LEN, row, row - CTX_LEN)
    seglen = jnp.where(row < CTX_LEN, CTX_LEN, SEQ)
    acc = xr * cw[2:3, :] + cb
    for k in (0, 1, 3):
        off = k - 2
        sh = pltpu.roll(xr, (-off) % ZLEN, 0)
        ok = (segpos + off >= 0) & (segpos + off < seglen)
        acc = acc + jnp.where(ok, sh, 0.0) * cw[k:k + 1, :]
    return acc


def _lru_conv_t(dxc, cw):
    row = _row_ids(ZLEN, LRU_BLOCK_W)
    segpos = jnp.where(row < CTX_LEN, row, row - CTX_LEN)
    seglen = jnp.where(row < CTX_LEN, CTX_LEN, SEQ)
    acc = dxc * cw[2:3, :]
    for k in (0, 1, 3):
        off = k - 2
        sh = pltpu.roll(dxc, off % ZLEN, 0)
        ok = (segpos - off >= 0) & (segpos - off < seglen)
        acc = acc + jnp.where(ok, sh, 0.0) * cw[k:k + 1, :]
    return acc


def _lru_gates(xc, xcb, wa, ba, wx, bx, lam):
    r = _sigmoid(jnp.dot(xcb, wa, preferred_element_type=F32) + ba)
    i = _sigmoid(jnp.dot(xcb, wx, preferred_element_type=F32) + bx)
    sp = jnp.maximum(-lam, 0.0) + jnp.log1p(jnp.exp(-jnp.abs(lam)))
    la = (-LRU_C) * r * sp
    a = jnp.exp(la)
    sq = jnp.sqrt(-jnp.tanh(la) * (1.0 + a * a))
    b = sq * i * xc
    return r, i, sp, a, sq, b


def _scan8_fwd(a, b, rid):
    for s in (1, 2, 4):
        a_s = pltpu.roll(a, s, 0)
        b_s = pltpu.roll(b, s, 0)
        m = rid >= s
        b = jnp.where(m, a * b_s + b, b)
        a = jnp.where(m, a * a_s, a)
    return a, b


def _scan8_rev(a, b, rid):
    for s in (1, 2, 4):
        a_s = pltpu.roll(a, 8 - s, 0)
        b_s = pltpu.roll(b, 8 - s, 0)
        m = rid < 8 - s
        b = jnp.where(m, a * b_s + b, b)
        a = jnp.where(m, a * a_s, a)
    return a, b


N_CHUNK = ZLEN // 8
CTX_CHUNKS = CTX_LEN // 8
SCAN_UNROLL = 8


def _scan_up(a_ref, b_ref, h_ref, lo, hi, carry):
    rid = _row_ids(8, LRU_BLOCK_W)
    assert (hi - lo) % SCAN_UNROLL == 0

    def step(g, c):
        base = pl.multiple_of((lo + g * SCAN_UNROLL) * 8, 8)
        for u in range(SCAN_UNROLL):
            sl = pl.ds(base + 8 * u, 8)
            a, b = _scan8_fwd(a_ref[sl, :], b_ref[sl, :], rid)
            h_ref[sl, :] = b + a * c
            c = b[7:8, :] + a[7:8, :] * c
        return c

    return lax.fori_loop(0, (hi - lo) // SCAN_UNROLL, step, carry)


def _scan_down(a_ref, b_ref, h_ref, lo, hi, carry):
    rid = _row_ids(8, LRU_BLOCK_W)
    assert (hi - lo) % SCAN_UNROLL == 0

    def step(g, c):
        base = pl.multiple_of((hi - (g + 1) * SCAN_UNROLL) * 8, 8)
        for u in reversed(range(SCAN_UNROLL)):
            sl = pl.ds(base + 8 * u, 8)
            a, b = _scan8_rev(a_ref[sl, :], b_ref[sl, :], rid)
            h_ref[sl, :] = b + a * c
            c = b[0:1, :] + a[0:1, :] * c
        return c

    return lax.fori_loop(0, (hi - lo) // SCAN_UNROLL, step, carry)


def _lru_scan_dir(d, a_ref, b_ref, h_ref):
    zero = jnp.zeros((1, LRU_BLOCK_W), F32)
    if d == 0:
        _scan_up(a_ref, b_ref, h_ref, 0, N_CHUNK, zero)
    else:
        c = _scan_down(a_ref, b_ref, h_ref, 0, CTX_CHUNKS, zero)
        _scan_down(a_ref, b_ref, h_ref, CTX_CHUNKS, N_CHUNK, c)


def _lru_in_specs():
    blk = lambda rows: pl.BlockSpec((rows, LRU_BLOCK_W), lambda b: (0, b))
    wspec = pl.BlockSpec((2, 1, LRU_BLOCK_W, LRU_BLOCK_W), lambda b: (0, b, 0, 0))
    return blk, wspec


def lru_fwd(p, conv_w, conv_b, wa, ba, wx, bx, lam, comm=None):
    blk, wspec = _lru_in_specs()

    def body(xr_ref, gx_ref, cw_ref, cb_ref, wa_ref, ba_ref, wx_ref, bx_ref, lam_ref, y_ref, a_s, b_s, h_s, hsum_s):
        xr = xr_ref[...]
        xc = _lru_conv(xr, cw_ref[...], cb_ref[...])
        xcb = xc.astype(BF16)
        for d in (0, 1):
            _, _, _, a, _, b = _lru_gates(xc, xcb, wa_ref[d, 0].astype(BF16), ba_ref[d:d + 1, :],
                                          wx_ref[d, 0].astype(BF16), bx_ref[d:d + 1, :], lam_ref[d:d + 1, :])
            a_s[...] = a
            b_s[...] = b
            _lru_scan_dir(d, a_s, b_s, h_s)
            if d == 0:
                hsum_s[...] = h_s[...]
            else:
                hsum_s[...] = hsum_s[...] + h_s[...]
        g, _ = _gelu_parts(gx_ref[CTX_LEN:, :])
        y_ref[...] = (hsum_s[CTX_LEN:, :] * g).astype(BF16)

    zs = pltpu.VMEM((ZLEN, LRU_BLOCK_W), F32)
    res, extra = _call(
        body, name="lru_fwd", grid=(LRU_BLOCKS,),
        in_specs=[blk(ZLEN), pl.BlockSpec((ZLEN, LRU_BLOCK_W), lambda b: (0, 24 + b)), blk(4), blk(1),
                  wspec, blk(2), wspec, blk(2), blk(2)],
        out_specs=[pl.BlockSpec((SEQ, LRU_BLOCK_W), lambda b: (0, b))],
        out_shape=[jax.ShapeDtypeStruct((SEQ, D_MODEL), BF16)],
        scratch_shapes=[zs, zs, zs, zs], sem=("arbitrary",),
        args=(p, p, conv_w, conv_b, wa, ba, wx, bx, lam), comm=comm)
    return res[0], extra


def _rope_tables():
    t = np.arange(SEQ)
    lane = np.arange(2 * HEAD_DIM)
    in_head = lane % HEAD_DIM
    j = (in_head % 32) % 16
    freq = ROPE_BASE ** (-j.astype(np.float64) / 16.0)
    pos = np.where(in_head[None, :] < 32, (t // GRID_W)[:, None], (t % GRID_W)[:, None]).astype(np.float64)
    ang = (pos.astype(np.float32) * freq.astype(np.float32)[None, :]).astype(np.float32)
    cos = np.cos(ang).astype(np.float32)
    sin = np.sin(ang).astype(np.float32)
    sgn = np.where((in_head % 32) < 16, -1.0, 1.0).astype(np.float32)
    cos = np.concatenate([np.ones((CTX_LEN, 2 * HEAD_DIM), np.float32), cos], 0)
    sin = np.concatenate([np.zeros((CTX_LEN, 2 * HEAD_DIM), np.float32), sin * sgn[None, :]], 0)
    return jnp.asarray(cos), jnp.asarray(sin)


def _head_ones():
    lane = np.arange(2 * HEAD_DIM)
    return jnp.asarray((lane[:, None] // HEAD_DIM == lane[None, :] // HEAD_DIM).astype(np.float32))


def _rope_partner(x):
    lane = lax.broadcasted_iota(jnp.int32, x.shape, 1)
    return jnp.where((lane % 32) < 16, pltpu.roll(x, 128 - 16, 1), pltpu.roll(x, 16, 1))


def _head_sum(t, ones):
    hi = t.astype(BF16)
    lo = (t - hi.astype(F32)).astype(BF16)
    ones_b = ones.astype(BF16)
    return jnp.dot(hi, ones_b, preferred_element_type=F32) + jnp.dot(lo, ones_b, preferred_element_type=F32)


def _head_rms(x, ones, gain):
    ms = _head_sum(x * x, ones) * (1.0 / HEAD_DIM)
    rstd = lax.rsqrt(ms + EPS)
    return x * rstd * gain, rstd


PREP_TILE = 768


def qkv_prep(p, qg2, kg2, cos, sin, ones, comm=None):
    scale = HEAD_DIM ** -0.5

    def body(q_ref, k_ref, v_ref, qg_ref, kg_ref, cos_ref, sin_ref, ones_ref, qr_ref, qp_ref, kk_ref, vv_ref):
        ones_m = ones_ref[...]
        c, s = cos_ref[...], sin_ref[...]
        qn, _ = _head_rms(q_ref[...], ones_m, qg_ref[...])
        qn = qn * scale
        qr_ref[...] = (qn * c + _rope_partner(qn) * s).astype(BF16)
        qp_ref[...] = qn.astype(BF16)
        kn, _ = _head_rms(k_ref[...], ones_m, kg_ref[...])
        kk_ref[...] = (kn * c + _rope_partner(kn) * s).astype(BF16)
        vv_ref[...] = v_ref[...].astype(BF16)

    col = lambda base: pl.BlockSpec((PREP_TILE, 128), lambda hp, i: (i, base + hp))
    small = pl.BlockSpec((1, 128), lambda hp, i: (0, 0))
    tab = pl.BlockSpec((PREP_TILE, 128), lambda hp, i: (i, 0))
    oshape = jax.ShapeDtypeStruct((ZLEN, D_MODEL), BF16)
    res, extra = _call(
        body, name="qkv_prep", grid=(NA_HEADS // 2, ZLEN // PREP_TILE),
        in_specs=[col(32), col(8), col(16), small, small, tab, tab, _full((128, 128))],
        out_specs=[col(0)] * 4, out_shape=[oshape] * 4, sem=("parallel", "parallel"),
        args=(p, p, p, qg2, kg2, cos, sin, ones), comm=comm)
    return (*res, extra)


def _bias_expand():
    qc = np.arange(GRID_W)[:, None]
    kc = np.arange(GRID_W)[None, :]
    col_start = np.clip(qc - NA_COLS // 2, 0, GRID_W - NA_COLS)
    in_win = (kc >= col_start) & (kc < col_start + NA_COLS)
    dc = np.clip(kc - qc, -(NA_COLS - 1), NA_COLS - 1) + (NA_COLS - 1)
    e = np.zeros((2 * NA_COLS - 1, GRID_W, GRID_W), np.float32)
    for d in range(2 * NA_COLS - 1):
        e[d] = ((dc == d) & in_win).astype(np.float32)
    pen = np.where(in_win, 0.0, NEG_INF).astype(np.float32)
    return e, pen


def bias_table(rpb2, comm=None):
    e, pen = _bias_expand()
    n_dr = 2 * NA_ROWS - 1
    ea = np.zeros((31, GRID_W, 128), np.float32)
    ea[:, :, :GRID_W] = e
    eb = np.zeros((31, GRID_W, 128), np.float32)
    eb[:, :, GRID_W:] = e
    pen2 = np.concatenate([pen, pen], 1)
    ea = jnp.asarray(ea.reshape(31, GRID_W * 128))
    eb = jnp.asarray(eb.reshape(31, GRID_W * 128))
    sel_a = np.zeros((BT_LEN, n_dr), np.float32)
    sel_b = np.zeros((BT_LEN, n_dr), np.float32)
    for r in range(BT_LEN):
        dr = r - BT_PAD
        if 0 <= dr < n_dr:
            sel_a[r, dr] = 1.0
        if 0 <= dr + 1 < n_dr:
            sel_b[r, dr + 1] = 1.0
    sel_a, sel_b = jnp.asarray(sel_a), jnp.asarray(sel_b)
    pen2 = jnp.asarray(pen2.reshape(1, GRID_W * 128))
    hi = lax.Precision.HIGHEST

    def body(rpb_ref, sa_ref, sb_ref, ea_ref, eb_ref, pen_ref, o_ref, ra_s, rb_s):
        for h in range(NA_HEADS):
            rp = rpb_ref[h]
            ra_s[h * BT_LEN:(h + 1) * BT_LEN, :] = jnp.dot(sa_ref[...], rp, preferred_element_type=F32, precision=hi)
            rb_s[h * BT_LEN:(h + 1) * BT_LEN, :] = jnp.dot(sb_ref[...], rp, preferred_element_type=F32, precision=hi)
        o_ref[...] = (jnp.dot(ra_s[...], ea_ref[...], preferred_element_type=F32, precision=hi)
                      + jnp.dot(rb_s[...], eb_ref[...], preferred_element_type=F32, precision=hi) + pen_ref[...])

    tcol = 2048
    rows = NA_HEADS * BT_LEN
    res, extra = _call(
        body, name="bias_table", grid=(GRID_W * 128 // tcol,),
        in_specs=[_full((NA_HEADS, n_dr, 31)), _full((BT_LEN, n_dr)), _full((BT_LEN, n_dr)),
                  pl.BlockSpec((31, tcol), lambda j: (0, j)), pl.BlockSpec((31, tcol), lambda j: (0, j)),
                  pl.BlockSpec((1, tcol), lambda j: (0, j))],
        out_specs=[pl.BlockSpec((rows, tcol), lambda j: (0, j))],
        out_shape=[jax.ShapeDtypeStruct((rows, GRID_W * 128), F32)],
        scratch_shapes=[pltpu.VMEM((rows, 31), F32), pltpu.VMEM((rows, 31), F32)], sem=("parallel",),
        args=(rpb2, sel_a, sel_b, ea, eb, pen2), comm=comm)
    return res[0].reshape(NA_HEADS, BT_LEN, GRID_W, 128), extra


def _key_window(j):
    ws = jnp.clip(Q_ROWS * j - 4, 0, GRID_ROWS - KEY_ROWS)
    return ws, pl.multiple_of(CTX_LEN + ws * GRID_W, 256)


def _head_mask(hh):
    lane = lax.broadcasted_iota(jnp.int32, (Q_TILE, 128), 1)
    return (lane < HEAD_DIM) if hh == 0 else (lane >= HEAD_DIM)


def _attn_scores(j, ws, q_rot_h, q_pl_h, kw, kc, hh, bt_ref, s_ref):
    s_ref[:, :KEY_TILE] = _dot_nt(q_rot_h, kw)
    s_ref[:, KEY_TILE:] = _dot_nt(q_pl_h, kc)
    lane = lax.broadcasted_iota(jnp.int32, (GRID_W, 128), 1)
    base = ws - Q_ROWS * j + (NA_ROWS - 1) + BT_PAD
    for qi in range(Q_ROWS):
        rs = jnp.clip(Q_ROWS * j + qi - NA_ROWS // 2, 0, GRID_ROWS - NA_ROWS)
        for m in range(KEY_ROWS // 2):
            k0 = ws + 2 * m
            p0 = jnp.where((k0 >= rs) & (k0 < rs + NA_ROWS), 0.0, NEG_INF)
            p1 = jnp.where((k0 + 1 >= rs) & (k0 + 1 < rs + NA_ROWS), 0.0, NEG_INF)
            pen = jnp.where(lane < GRID_W, p0, p1)
            rows = slice(qi * GRID_W, (qi + 1) * GRID_W)
            cols = slice(128 * m, 128 * (m + 1))
            s_ref[rows, cols] = s_ref[rows, cols] + bt_ref[hh, base + 2 * m - qi] + pen
    return base


def attn_fwd(q_rot, q_pl, kk, vv, bt, comm=None):
    def body(qr_ref, qp_ref, kk_ref, vv_ref, bt_ref, o_ref, lse_ref, s_ref):
        j = pl.program_id(1)
        ws, start = _key_window(j)
        win = pl.ds(start, KEY_TILE)
        kw, kc = kk_ref[win, :], kk_ref[:CTX_LEN, :]
        vw, vc = vv_ref[win, :], vv_ref[:CTX_LEN, :]
        qr, qp = qr_ref[...], qp_ref[...]
        outs = []
        for hh in range(2):
            msk = _head_mask(hh)
            _attn_scores(j, ws, jnp.where(msk, qr, 0), jnp.where(msk, qp, 0), kw, kc, hh, bt_ref, s_ref)
            s = s_ref[...]
            mx = jnp.max(s, axis=-1, keepdims=True)
            pr = jnp.exp(s - mx)
            l = jnp.sum(pr, axis=-1, keepdims=True)
            prb = pr.astype(BF16)
            o = jnp.dot(prb[:, :KEY_TILE], vw, preferred_element_type=F32)
            o = o + jnp.dot(prb[:, KEY_TILE:], vc, preferred_element_type=F32)
            outs.append(o / l)
            lse_ref[hh] = mx + jnp.log(l)
        o_ref[...] = jnp.where(_head_mask(0), outs[0], outs[1])

    qspec = pl.BlockSpec((Q_TILE, 128), lambda hp, j: (j + 1, hp))
    kspec = pl.BlockSpec((ZLEN, 128), lambda hp, j: (0, hp))
    res, extra = _call(
        body, name="attn_fwd", grid=(NA_HEADS // 2, SEQ // Q_TILE),
        in_specs=[qspec, qspec, kspec, kspec, pl.BlockSpec((2, BT_LEN, GRID_W, 128), lambda hp, j: (hp, 0, 0, 0))],
        out_specs=[pl.BlockSpec((Q_TILE, 128), lambda hp, j: (j, hp)),
                   pl.BlockSpec((2, Q_TILE, 1), lambda hp, j: (hp, j, 0))],
        out_shape=[jax.ShapeDtypeStruct((SEQ, D_MODEL), F32), jax.ShapeDtypeStruct((NA_HEADS, SEQ, 1), F32)],
        scratch_shapes=[pltpu.VMEM((Q_TILE, KEY_TILE + CTX_LEN), F32)], sem=("parallel", "arbitrary"),
        args=(q_rot, q_pl, kk, vv, bt), comm=comm)
    return res[0], res[1], extra


def merge_fwd(y_rnn, y_na, p, z, g2, w_rnn, w_na, w_out):
    def body(yr_ref, yn_ref, mr_ref, mn_ref, x_ref, g2_ref, wr_ref, wn_ref, wo_ref, u_ref, v_ref, mg_ref, out_ref, x1_ref):
        u = jnp.dot(yr_ref[...], wr_ref[...], preferred_element_type=F32)
        v = jnp.dot(yn_ref[...].astype(BF16), wn_ref[...], preferred_element_type=F32)
        merged = (_sigmoid(mr_ref[...]) * u + _sigmoid(mn_ref[...]) * v).astype(BF16)
        out = jnp.dot(merged, wo_ref[...], preferred_element_type=F32)
        u_ref[...] = u
        v_ref[...] = v
        mg_ref[...] = merged
        out_ref[...] = out
        x1_ref[...] = x_ref[...] + g2_ref[...] * out

    row = pl.BlockSpec((ROW_TILE, D_MODEL), lambda i: (i, 0))
    lat = lambda cb: pl.BlockSpec((ROW_TILE, D_MODEL), lambda i: (i + 1, cb))
    wspec = _full((D_MODEL, D_MODEL))
    f32o = jax.ShapeDtypeStruct((SEQ, D_MODEL), F32)
    return pl.pallas_call(
        body, name="merge_fwd", grid=(SEQ // ROW_TILE,),
        in_specs=[row, row, lat(5), lat(6), lat(0), _full((1, D_MODEL)), wspec, wspec, wspec],
        out_specs=[row] * 5,
        out_shape=[f32o, f32o, jax.ShapeDtypeStruct((SEQ, D_MODEL), BF16), f32o, f32o],
        compiler_params=_params("parallel"),
    )(y_rnn, y_na, p, p, z, g2, w_rnn, w_na, w_out)


FF_TILE = 256
FF_TILES = D_FF // FF_TILE


FF_ROWS = 64
FF_HALO = 8
FF_SLAB = FF_ROWS + 2 * FF_HALO


def _ffn_row_chunks(chunk, init):
    carry = chunk(0, 0, -1, init)
    carry = lax.fori_loop(1, SEQ // FF_ROWS - 1,
                          lambda ci, cr: chunk(pl.multiple_of(ci * FF_ROWS - FF_HALO, 8), FF_HALO, 0, cr), carry)
    return chunk(SEQ - FF_SLAB, 2 * FF_HALO, 1, carry)


def _ffn_shifts(edge):
    row = _row_ids(FF_SLAB, FF_TILE)

    def prev(x):
        r = pltpu.roll(x, 1, 0)
        return jnp.where(row >= 1, r, 0.0) if edge == -1 else r

    def nxt(x):
        r = pltpu.roll(x, FF_SLAB - 1, 0)
        return jnp.where(row < FF_SLAB - 1, r, 0.0) if edge == 1 else r

    return prev, nxt


def ffn_act(hpre, conv_w, conv_b):
    def body(ha_ref, hg_ref, wa_ref, wg_ref, ba_ref, bg_ref, o_ref):
        wa, wg, ba, bg = wa_ref[...], wg_ref[...], ba_ref[...], bg_ref[...]

        def chunk(lo, mid, edge, carry):
            prev, nxt = _ffn_shifts(edge)
            ha, hg = ha_ref[pl.ds(lo, FF_SLAB), :], hg_ref[pl.ds(lo, FF_SLAB), :]
            a = prev(ha) * wa[0:1] + ha * wa[1:2] + nxt(ha) * wa[2:3] + ba
            g = prev(hg) * wg[0:1] + hg * wg[1:2] + nxt(hg) * wg[2:3] + bg
            o_ref[pl.ds(lo + mid, FF_ROWS), :] = (a * _sigmoid(a) * g)[mid:mid + FF_ROWS].astype(BF16)
            return carry

        _ffn_row_chunks(chunk, 0)

    col = lambda rows, off: pl.BlockSpec((rows, FF_TILE), lambda j: (0, j + off))
    return pl.pallas_call(
        body, name="ffn_act", grid=(FF_TILES,),
        in_specs=[col(SEQ, 0), col(SEQ, FF_TILES), col(3, 0), col(3, FF_TILES), col(1, 0), col(1, FF_TILES)],
        out_specs=col(SEQ, 0),
        out_shape=jax.ShapeDtypeStruct((SEQ, D_FF), BF16),
        compiler_params=_params("parallel"),
    )(hpre, hpre, conv_w, conv_w, conv_b, conv_b)


def ffn_down_loss(act, w_down, x1, g5, target):
    def body(a_ref, w_ref, x1_ref, g5_ref, t_ref, f_ref, dy_ref, df_ref, ls_ref, dg_ref):
        i = pl.program_id(0)
        f = jnp.dot(a_ref[...], w_ref[...], preferred_element_type=F32)
        g5 = g5_ref[...]
        err = x1_ref[...] + g5 * f - t_ref[...]
        dy = err * (1.0 / D_MODEL)
        f_ref[...] = f
        dy_ref[...] = dy
        df_ref[...] = (dy * g5).astype(BF16)

        @pl.when(i == 0)
        def _():
            ls_ref[...] = jnp.zeros_like(ls_ref)
            dg_ref[...] = jnp.zeros_like(dg_ref)

        ls_ref[...] = ls_ref[...] + jnp.sum(err * err)
        dg_ref[...] = dg_ref[...] + jnp.sum(dy * f, axis=0, keepdims=True)

    row = pl.BlockSpec((ROW_TILE, D_MODEL), lambda i: (i, 0))
    f32o = jax.ShapeDtypeStruct((SEQ, D_MODEL), F32)
    return pl.pallas_call(
        body, name="ffn_down_loss", grid=(SEQ // ROW_TILE,),
        in_specs=[pl.BlockSpec((ROW_TILE, D_FF), lambda i: (i, 0)), _full((D_FF, D_MODEL)), row, _full((1, D_MODEL)), row],
        out_specs=[row, row, row, _full((8, 128)), _full((1, D_MODEL))],
        out_shape=[f32o, f32o, jax.ShapeDtypeStruct((SEQ, D_MODEL), BF16), jax.ShapeDtypeStruct((8, 128), F32),
                   jax.ShapeDtypeStruct((1, D_MODEL), F32)],
        compiler_params=_params("arbitrary"),
    )(act, w_down, x1, g5, target)


def ffn_down_bwd(df, w_down):
    def body(df_ref, w_ref, o_ref):
        o_ref[...] = _dot_nt(df_ref[...], w_ref[...])

    return pl.pallas_call(
        body, name="ffn_down_bwd", grid=(SEQ // ROW_TILE,),
        in_specs=[pl.BlockSpec((ROW_TILE, D_MODEL), lambda i: (i, 0)), _full((D_FF, D_MODEL))],
        out_specs=pl.BlockSpec((ROW_TILE, D_FF), lambda i: (i, 0)),
        out_shape=jax.ShapeDtypeStruct((SEQ, D_FF), F32),
        compiler_params=_params("parallel"),
    )(df, w_down)


def ffn_act_bwd(hpre, d_act, conv_w, conv_b):
    def body(ha_ref, hg_ref, da_ref, wa_ref, wg_ref, ba_ref, bg_ref, dha_ref, dhg_ref, dwa_ref, dwg_ref, dba_ref, dbg_ref):
        wa, wg, ba, bg = wa_ref[...], wg_ref[...], ba_ref[...], bg_ref[...]

        def chunk(lo, mid, edge, acc):
            prev, nxt = _ffn_shifts(edge)
            rows = pl.ds(lo, FF_SLAB)
            ha, hg, dact = ha_ref[rows, :], hg_ref[rows, :], da_ref[rows, :]
            hap, han, hgp, hgn = prev(ha), nxt(ha), prev(hg), nxt(hg)
            a = hap * wa[0:1] + ha * wa[1:2] + han * wa[2:3] + ba
            g = hgp * wg[0:1] + hg * wg[1:2] + hgn * wg[2:3] + bg
            sig = _sigmoid(a)
            dca = dact * g * (sig * (1.0 + a * (1.0 - sig)))
            dcg = dact * a * sig
            m = slice(mid, mid + FF_ROWS)
            sums = []
            for dc, h, hp, hn, w, dh_ref in ((dca, ha, hap, han, wa, dha_ref), (dcg, hg, hgp, hgn, wg, dhg_ref)):
                dcm = dc[m]
                sums += [jnp.sum(dcm * hp[m], axis=0, keepdims=True), jnp.sum(dcm * h[m], axis=0, keepdims=True),
                         jnp.sum(dcm * hn[m], axis=0, keepdims=True), jnp.sum(dcm, axis=0, keepdims=True)]
                dh = nxt(dc) * w[0:1] + dc * w[1:2] + prev(dc) * w[2:3]
                dh_ref[pl.ds(lo + mid, FF_ROWS), :] = dh[m].astype(BF16)
            return tuple(x + y for x, y in zip(acc, sums))

        acc = _ffn_row_chunks(chunk, tuple(jnp.zeros((1, FF_TILE), F32) for _ in range(8)))
        dwa_ref[0:1, :], dwa_ref[1:2, :], dwa_ref[2:3, :], dba_ref[...] = acc[0], acc[1], acc[2], acc[3]
        dwg_ref[0:1, :], dwg_ref[1:2, :], dwg_ref[2:3, :], dbg_ref[...] = acc[4], acc[5], acc[6], acc[7]

    col = lambda rows, off: pl.BlockSpec((rows, FF_TILE), lambda j: (0, j + off))
    hshape = jax.ShapeDtypeStruct((SEQ, D_FF), BF16)
    wshape = jax.ShapeDtypeStruct((3, D_FF), F32)
    bshape = jax.ShapeDtypeStruct((1, D_FF), F32)
    return pl.pallas_call(
        body, name="ffn_act_bwd", grid=(FF_TILES,),
        in_specs=[col(SEQ, 0), col(SEQ, FF_TILES), col(SEQ, 0), col(3, 0), col(3, FF_TILES), col(1, 0), col(1, FF_TILES)],
        out_specs=[col(SEQ, 0), col(SEQ, 0), col(3, 0), col(3, 0), col(1, 0), col(1, 0)],
        out_shape=[hshape, hshape, wshape, wshape, bshape, bshape],
        compiler_params=_params("parallel"),
    )(hpre, hpre, d_act, conv_w, conv_w, conv_b, conv_b)


def _norm_mod_bwd(x, dxn, gain, scale):
    rstd = lax.rsqrt(jnp.mean(x * x, axis=-1, keepdims=True) + EPS)
    nrm = x * rstd
    dsh = jnp.sum(dxn, axis=0, keepdims=True)
    dsc = jnp.sum(dxn * nrm, axis=0, keepdims=True) * gain
    dgn = jnp.sum(dxn * nrm, axis=0, keepdims=True) * (1.0 + scale)
    dn = dxn * (gain * (1.0 + scale))
    dx = rstd * (dn - nrm * jnp.mean(dn * nrm, axis=-1, keepdims=True))
    return dx, dsh, dsc, dgn


def ffn_up_bwd(dha, dhg, w_up, x1, dy, gain, scale):
    def body(dha_ref, dhg_ref, w_ref, x_ref, dy_ref, g_ref, sc_ref, dx_ref, dsh_ref, dsc_ref, dgn_ref):
        i = pl.program_id(0)
        dxn = _dot_nt(dha_ref[...], w_ref[:, :D_FF]) + _dot_nt(dhg_ref[...], w_ref[:, D_FF:])
        dx, dsh, dsc, dgn = _norm_mod_bwd(x_ref[...], dxn, g_ref[...], sc_ref[...])
        dx_ref[...] = dy_ref[...] + dx

        @pl.when(i == 0)
        def _():
            dsh_ref[...] = dsh
            dsc_ref[...] = dsc
            dgn_ref[...] = dgn

        @pl.when(i > 0)
        def _():
            dsh_ref[...] = dsh_ref[...] + dsh
            dsc_ref[...] = dsc_ref[...] + dsc
            dgn_ref[...] = dgn_ref[...] + dgn

    row = pl.BlockSpec((ROW_TILE, D_MODEL), lambda i: (i, 0))
    vec = _full((1, D_MODEL))
    vshape = jax.ShapeDtypeStruct((1, D_MODEL), F32)
    return pl.pallas_call(
        body, name="ffn_up_bwd", grid=(SEQ // ROW_TILE,),
        in_specs=[pl.BlockSpec((ROW_TILE, D_FF), lambda i: (i, 0)), pl.BlockSpec((ROW_TILE, D_FF), lambda i: (i, 0)),
                  _full((D_MODEL, 2 * D_FF)), row, row, vec, vec],
        out_specs=[row, vec, vec, vec],
        out_shape=[jax.ShapeDtypeStruct((SEQ, D_MODEL), F32), vshape, vshape, vshape],
        compiler_params=_params("arbitrary"),
    )(dha, dhg, w_up, x1, dy, gain, scale)


def merge_bwd(dx1, out, g2, p, u, v, w_rnn, w_na, w_out, comm=None):
    def body(dx_ref, out_ref, g2_ref, mr_ref, mn_ref, u_ref, v_ref, wr_ref, wn_ref, wo_ref,
             dout_ref, du_ref, dv_ref, dmr_ref, dmn_ref, dyr_ref, dyn_ref, dg2_ref):
        i = pl.program_id(0)

        @pl.when(i == 0)
        def _():
            dmr_ref[...] = jnp.zeros_like(dmr_ref)
            dmn_ref[...] = jnp.zeros_like(dmn_ref)
            dg2_ref[...] = jnp.zeros_like(dg2_ref)

        @pl.when(i > 0)
        def _():
            dx = dx_ref[...]
            dg2_ref[...] = dg2_ref[...] + jnp.sum(dx * out_ref[...], axis=0, keepdims=True)
            dout = (dx * g2_ref[...]).astype(BF16)
            dout_ref[...] = dout
            dm = _dot_nt(dout, wo_ref[...])
            sr = _sigmoid(mr_ref[...])
            sn = _sigmoid(mn_ref[...])
            du = (dm * sr).astype(BF16)
            dv = (dm * sn).astype(BF16)
            du_ref[...] = du
            dv_ref[...] = dv
            dmr_ref[...] = (dm * u_ref[...] * (sr * (1.0 - sr))).astype(BF16)
            dmn_ref[...] = (dm * v_ref[...] * (sn * (1.0 - sn))).astype(BF16)
            dyr_ref[...] = _dot_nt(du, wr_ref[...])
            dyn_ref[...] = _dot_nt(dv, wn_ref[...])

    lat = pl.BlockSpec((ROW_TILE, D_MODEL), lambda i: (jnp.maximum(i - 1, 0), 0))
    zrow = pl.BlockSpec((ROW_TILE, D_MODEL), lambda i: (i, 0))
    pcol = lambda cb: pl.BlockSpec((ROW_TILE, D_MODEL), lambda i: (i, cb))
    wspec = _full((D_MODEL, D_MODEL))
    tb = jax.ShapeDtypeStruct((SEQ, D_MODEL), BF16)
    zb = jax.ShapeDtypeStruct((ZLEN, D_MODEL), BF16)
    tf = jax.ShapeDtypeStruct((SEQ, D_MODEL), F32)
    res, extra = _call(
        body, name="merge_bwd", grid=(ZLEN // ROW_TILE,),
        in_specs=[lat, lat, _full((1, D_MODEL)), pcol(5), pcol(6), lat, lat, wspec, wspec, wspec],
        out_specs=[lat, lat, lat, zrow, zrow, lat, lat, _full((1, D_MODEL))],
        out_shape=[tb, tb, tb, zb, zb, tf, tf, jax.ShapeDtypeStruct((1, D_MODEL), F32)],
        sem=("arbitrary",), args=(dx1, out, g2, p, p, u, v, w_rnn, w_na, w_out), comm=comm)
    return (*res, extra)


def attn_bwd(q_rot, q_pl, kk, vv, bt, y_na, d_yna, lse, comm=None):
    def body(qr_ref, qp_ref, kk_ref, vv_ref, bt_ref, o_ref, do_ref, lse_ref,
             dqr_ref, dqp_ref, dk_ref, dv_ref, dbt_ref, s_ref):
        jj = pl.program_id(1)

        @pl.when(jj == 0)
        def _():
            dqr_ref[...] = jnp.zeros_like(dqr_ref)
            dqp_ref[...] = jnp.zeros_like(dqp_ref)
            dk_ref[...] = jnp.zeros_like(dk_ref)
            dv_ref[...] = jnp.zeros_like(dv_ref)
            dbt_ref[...] = jnp.zeros_like(dbt_ref)

        @pl.when(jj > 0)
        def _():
            j = jj - 1
            ws, start = _key_window(j)
            win = pl.ds(start, KEY_TILE)
            kw, kc = kk_ref[win, :], kk_ref[:CTX_LEN, :]
            vw, vc = vv_ref[win, :], vv_ref[:CTX_LEN, :]
            qr, qp = qr_ref[...], qp_ref[...]
            do = do_ref[...]
            do_o = do * o_ref[...]
            dq_r, dq_p = [], []
            for hh in range(2):
                msk = _head_mask(hh)
                q_r, q_p = jnp.where(msk, qr, 0), jnp.where(msk, qp, 0)
                base = _attn_scores(j, ws, q_r, q_p, kw, kc, hh, bt_ref, s_ref)
                pr = jnp.exp(s_ref[...] - lse_ref[hh])
                delta = jnp.sum(jnp.where(msk, do_o, 0.0), axis=-1, keepdims=True)
                dob = jnp.where(msk, do, 0.0).astype(BF16)
                ds_lat = pr[:, :KEY_TILE] * (_dot_nt(dob, vw) - delta)
                ds_ctx = pr[:, KEY_TILE:] * (_dot_nt(dob, vc) - delta)
                for qi in range(Q_ROWS):
                    for m in range(KEY_ROWS // 2):
                        idx = base + 2 * m - qi
                        dbt_ref[hh, idx] = dbt_ref[hh, idx] + ds_lat[qi * GRID_W:(qi + 1) * GRID_W, 128 * m:128 * (m + 1)]
                dsb_lat = ds_lat.astype(BF16)
                dsb_ctx = ds_ctx.astype(BF16)
                prb = pr.astype(BF16)
                dq_r.append(jnp.dot(dsb_lat, kw, preferred_element_type=F32))
                dq_p.append(jnp.dot(dsb_ctx, kc, preferred_element_type=F32))
                dk_ref[win, :] = dk_ref[win, :] + _dot_tn(dsb_lat, q_r)
                dk_ref[:CTX_LEN, :] = dk_ref[:CTX_LEN, :] + _dot_tn(dsb_ctx, q_p)
                dv_ref[win, :] = dv_ref[win, :] + _dot_tn(prb[:, :KEY_TILE], dob)
                dv_ref[:CTX_LEN, :] = dv_ref[:CTX_LEN, :] + _dot_tn(prb[:, KEY_TILE:], dob)
            dqr_ref[...] = jnp.where(_head_mask(0), dq_r[0], dq_r[1])
            dqp_ref[...] = jnp.where(_head_mask(0), dq_p[0], dq_p[1])

    lat = lambda jj: jnp.maximum(jj - 1, 0)
    qspec = pl.BlockSpec((Q_TILE, 128), lambda hp, jj: (lat(jj) + 1, hp))
    kspec = pl.BlockSpec((ZLEN, 128), lambda hp, jj: (0, hp))
    btspec = pl.BlockSpec((2, BT_LEN, GRID_W, 128), lambda hp, jj: (hp, 0, 0, 0))
    ospec = pl.BlockSpec((Q_TILE, 128), lambda hp, jj: (lat(jj), hp))
    dqspec = pl.BlockSpec((Q_TILE, 128), lambda hp, jj: (jj, hp))
    zshape = jax.ShapeDtypeStruct((ZLEN, D_MODEL), F32)
    res, extra = _call(
        body, name="attn_bwd", grid=(NA_HEADS // 2, ZLEN // Q_TILE),
        in_specs=[qspec, qspec, kspec, kspec, btspec, ospec, ospec,
                  pl.BlockSpec((2, Q_TILE, 1), lambda hp, jj: (hp, lat(jj), 0))],
        out_specs=[dqspec, dqspec, kspec, kspec, btspec],
        out_shape=[zshape, zshape, zshape, zshape, jax.ShapeDtypeStruct((NA_HEADS, BT_LEN, GRID_W, 128), F32)],
        scratch_shapes=[pltpu.VMEM((Q_TILE, KEY_TILE + CTX_LEN), F32)], sem=("parallel", "arbitrary"),
        args=(q_rot, q_pl, kk, vv, bt, y_na, d_yna, lse), comm=comm)
    return (*res, extra)


def qkv_bwd(dq_rot, dq_pl, dk, dv, p, qg2, kg2, cos, sin, ones, comm=None):
    scale = HEAD_DIM ** -0.5
    n_hp, n_i = NA_HEADS // 2, ZLEN // PREP_TILE

    def norm_rope_bwd(d_rot, d_extra, x, gain, cos_t, sin_t, ones_m, dx_ref, acc_ref):
        xh, rstd = _head_rms(x, ones_m, 1.0)
        dn = d_rot * cos_t + _rope_partner(d_rot * sin_t)
        if d_extra is not None:
            dn = (dn + d_extra) * scale
        acc_ref[...] = acc_ref[...] + jnp.sum(dn * xh, axis=0, keepdims=True)
        dxh = dn * gain
        seg = _head_sum(dxh * xh, ones_m) * (1.0 / HEAD_DIM)
        dx_ref[...] = (rstd * (dxh - xh * seg)).astype(BF16)

    def body(dqr_ref, dqp_ref, dk_ref, dv_ref, xq_ref, xk_ref, qg_ref, kg_ref, cos_ref, sin_ref, ones_ref,
             dxq_ref, dxk_ref, dxv_ref, dgq_ref, dgk_ref, accq_ref, acck_ref):
        hp, i = pl.program_id(0), pl.program_id(1)

        @pl.when((hp == 0) & (i == 0))
        def _():
            accq_ref[...] = jnp.zeros_like(accq_ref)
            acck_ref[...] = jnp.zeros_like(acck_ref)

        ones_m = ones_ref[...]
        cos_t, sin_t = cos_ref[...], sin_ref[...]
        norm_rope_bwd(dqr_ref[...], dqp_ref[...], xq_ref[...], qg_ref[...], cos_t, sin_t, ones_m, dxq_ref, accq_ref)
        norm_rope_bwd(dk_ref[...], None, xk_ref[...], kg_ref[...], cos_t, sin_t, ones_m, dxk_ref, acck_ref)
        dxv_ref[...] = dv_ref[...].astype(BF16)

        @pl.when((hp == n_hp - 1) & (i == n_i - 1))
        def _():
            dgq_ref[...] = accq_ref[:, :HEAD_DIM] + accq_ref[:, HEAD_DIM:]
            dgk_ref[...] = acck_ref[:, :HEAD_DIM] + acck_ref[:, HEAD_DIM:]

    col = lambda base: pl.BlockSpec((PREP_TILE, 128), lambda hp, i: (i, base + hp))
    small = pl.BlockSpec((1, 128), lambda hp, i: (0, 0))
    tab = pl.BlockSpec((PREP_TILE, 128), lambda hp, i: (i, 0))
    zb = jax.ShapeDtypeStruct((ZLEN, D_MODEL), BF16)
    gshape = jax.ShapeDtypeStruct((1, HEAD_DIM), F32)
    res, extra = _call(
        body, name="qkv_bwd", grid=(n_hp, n_i),
        in_specs=[col(0)] * 4 + [col(32), col(8), small, small, tab, tab, _full((128, 128))],
        out_specs=[col(0)] * 3 + [_full((1, HEAD_DIM))] * 2,
        out_shape=[zb, zb, zb, gshape, gshape],
        scratch_shapes=[pltpu.VMEM((1, 128), F32)] * 2, sem=("arbitrary", "arbitrary"),
        args=(dq_rot, dq_pl, dk, dv, p, p, qg2, kg2, cos, sin, ones), comm=comm)
    return (*res, extra)


def rpb_grad(dbt):
    e, _ = _bias_expand()
    n_dr = 2 * NA_ROWS - 1
    ea = np.zeros((31, GRID_W, 128), np.float32)
    ea[:, :, :GRID_W] = e
    eb = np.zeros((31, GRID_W, 128), np.float32)
    eb[:, :, GRID_W:] = e
    eat = jnp.asarray(ea.reshape(31, GRID_W * 128).T.copy())
    ebt = jnp.asarray(eb.reshape(31, GRID_W * 128).T.copy())
    sel_at = np.zeros((n_dr, BT_LEN), np.float32)
    sel_bt = np.zeros((n_dr, BT_LEN), np.float32)
    for r in range(BT_LEN):
        dr = r - BT_PAD
        if 0 <= dr < n_dr:
            sel_at[dr, r] = 1.0
        if 0 <= dr + 1 < n_dr:
            sel_bt[dr + 1, r] = 1.0
    hi = lax.Precision.HIGHEST

    tk = 2048
    wide = GRID_W * 128
    rows = NA_HEADS * BT_LEN
    n_k = wide // tk

    def body(d_ref, sa_ref, sb_ref, ea_ref, eb_ref, o_ref, a_s, b_s):
        k = pl.program_id(0)
        dm = d_ref[...]
        d_hi = dm.astype(BF16)
        rest = dm - d_hi.astype(F32)
        d_mid = rest.astype(BF16)
        d_lo = (rest - d_mid.astype(F32)).astype(BF16)
        ea_b, eb_b = ea_ref[...].astype(BF16), eb_ref[...].astype(BF16)
        a = sum(jnp.dot(t, ea_b, preferred_element_type=F32) for t in (d_hi, d_mid, d_lo))
        b = sum(jnp.dot(t, eb_b, preferred_element_type=F32) for t in (d_hi, d_mid, d_lo))

        @pl.when(k == 0)
        def _():
            a_s[...] = a
            b_s[...] = b

        @pl.when(k > 0)
        def _():
            a_s[...] = a_s[...] + a
            b_s[...] = b_s[...] + b

        @pl.when(k == n_k - 1)
        def _():
            for h in range(NA_HEADS):
                sl = slice(h * BT_LEN, (h + 1) * BT_LEN)
                o_ref[h] = (jnp.dot(sa_ref[...], a_s[sl, :], preferred_element_type=F32, precision=hi)
                            + jnp.dot(sb_ref[...], b_s[sl, :], preferred_element_type=F32, precision=hi))

    return pl.pallas_call(
        body, name="rpb_grad", grid=(n_k,),
        in_specs=[pl.BlockSpec((rows, tk), lambda k: (0, k)), _full((n_dr, BT_LEN)), _full((n_dr, BT_LEN)),
                  pl.BlockSpec((tk, 31), lambda k: (k, 0)), pl.BlockSpec((tk, 31), lambda k: (k, 0))],
        out_specs=_full((NA_HEADS, n_dr, 31)),
        out_shape=jax.ShapeDtypeStruct((NA_HEADS, n_dr, 31), F32),
        scratch_shapes=[pltpu.VMEM((rows, 31), F32), pltpu.VMEM((rows, 31), F32)],
        compiler_params=_params("arbitrary"),
    )(dbt.reshape(rows, wide), jnp.asarray(sel_at), jnp.asarray(sel_bt), eat, ebt)


def lru_bwd(p, d_yrnn, conv_w, conv_b, wa, ba, wx, bx, lam, comm=None):
    blk, wspec = _lru_in_specs()

    def body(xr_ref, gx_ref, dy_ref, cw_ref, cb_ref, wa_ref, ba_ref, wx_ref, bx_ref, lam_ref,
             dxr_ref, dgx_ref, dcw_ref, dcb_ref, dwa_ref, dba_ref, dwx_ref, dbx_ref, dlam_ref,
             a_s, b_s, h_s, l_s, hsum_s, dxc_s, dh_s):
        xr = xr_ref[...]
        cw = cw_ref[...]
        xc = _lru_conv(xr, cw, cb_ref[...])
        xcb = xc.astype(BF16)
        g, dg = _gelu_parts(gx_ref[CTX_LEN:, :])
        dy = dy_ref[...]
        dh_s[:CTX_LEN, :] = jnp.zeros((CTX_LEN, LRU_BLOCK_W), F32)
        dh_s[CTX_LEN:, :] = dy * g
        row = _row_ids(ZLEN, LRU_BLOCK_W)
        zero = jnp.zeros((1, LRU_BLOCK_W), F32)
        for d in (0, 1):
            wab = wa_ref[d, 0].astype(BF16)
            wxb = wx_ref[d, 0].astype(BF16)
            lam_d = lam_ref[d:d + 1, :]
            r, gi, sp, a, sq, b = _lru_gates(xc, xcb, wab, ba_ref[d:d + 1, :], wxb, bx_ref[d:d + 1, :], lam_d)
            a_s[...] = a
            b_s[...] = b
            _lru_scan_dir(d, a_s, b_s, h_s)
            h = h_s[...]
            if d == 0:
                hsum_s[...] = h
                h_prev = jnp.where(row >= 1, pltpu.roll(h, 1, 0), 0.0)
                a_s[...] = pltpu.roll(a, ZLEN - 1, 0)
                _scan_down(a_s, dh_s, l_s, 0, N_CHUNK, zero)
            else:
                hsum_s[...] = hsum_s[...] + h
                h_prev = jnp.where(row == CTX_LEN - 1, 0.0, pltpu.roll(h, ZLEN - 1, 0))
                a_s[...] = pltpu.roll(a, 1, 0)
                c = _scan_up(a_s, dh_s, l_s, CTX_CHUNKS, N_CHUNK, zero)
                _scan_up(a_s, dh_s, l_s, 0, CTX_CHUNKS, c)
            db = l_s[...]
            da = db * h_prev
            dsq = db * gi * xc
            dgi = db * sq * xc
            dxc_d = db * sq * gi
            dla = da * a - dsq * (a * a) / sq
            dr = dla * ((-LRU_C) * sp)
            dsp = jnp.sum(dla * ((-LRU_C) * r), axis=0, keepdims=True)
            dlam_ref[d:d + 1, :] = -dsp * _sigmoid(-lam_d)
            dzr = dr * r * (1.0 - r)
            dzi = dgi * gi * (1.0 - gi)
            dba_ref[d:d + 1, :] = jnp.sum(dzr, axis=0, keepdims=True)
            dbx_ref[d:d + 1, :] = jnp.sum(dzi, axis=0, keepdims=True)
            dzrb = dzr.astype(BF16)
            dzib = dzi.astype(BF16)
            dwa_ref[d, 0] = _dot_tn(xcb, dzrb)
            dwx_ref[d, 0] = _dot_tn(xcb, dzib)
            dxc_d = dxc_d + _dot_nt(dzrb, wab) + _dot_nt(dzib, wxb)
            if d == 0:
                dxc_s[...] = dxc_d
            else:
                dxc_s[...] = dxc_s[...] + dxc_d
        dxc = dxc_s[...]
        dxr_ref[...] = _lru_conv_t(dxc, cw).astype(BF16)
        dcb_ref[...] = jnp.sum(dxc, axis=0, keepdims=True)
        segpos = jnp.where(row < CTX_LEN, row, row - CTX_LEN)
        seglen = jnp.where(row < CTX_LEN, CTX_LEN, SEQ)
        for k in range(4):
            off = k - 2
            if off == 0:
                sh = xr
            else:
                ok = (segpos + off >= 0) & (segpos + off < seglen)
                sh = jnp.where(ok, pltpu.roll(xr, (-off) % ZLEN, 0), 0.0)
            dcw_ref[k:k + 1, :] = jnp.sum(dxc * sh, axis=0, keepdims=True)
        dgx_ref[:CTX_LEN, :] = jnp.zeros((CTX_LEN, LRU_BLOCK_W), BF16)
        dgx_ref[CTX_LEN:, :] = (dy * hsum_s[CTX_LEN:, :] * dg).astype(BF16)

    zs = pltpu.VMEM((ZLEN, LRU_BLOCK_W), F32)
    zb = jax.ShapeDtypeStruct((ZLEN, D_MODEL), BF16)
    v2 = jax.ShapeDtypeStruct((2, D_MODEL), F32)
    w4 = jax.ShapeDtypeStruct((2, LRU_BLOCKS, LRU_BLOCK_W, LRU_BLOCK_W), F32)
    res, extra = _call(
        body, name="lru_bwd", grid=(LRU_BLOCKS,),
        in_specs=[blk(ZLEN), pl.BlockSpec((ZLEN, LRU_BLOCK_W), lambda b: (0, 24 + b)), blk(SEQ), blk(4), blk(1),
                  wspec, blk(2), wspec, blk(2), blk(2)],
        out_specs=[blk(ZLEN), blk(ZLEN), blk(4), blk(1), wspec, blk(2), wspec, blk(2), blk(2)],
        out_shape=[zb, zb, jax.ShapeDtypeStruct((4, D_MODEL), F32), jax.ShapeDtypeStruct((1, D_MODEL), F32),
                   w4, v2, w4, v2, v2],
        scratch_shapes=[zs] * 7, sem=("arbitrary",),
        args=(p, p, d_yrnn, conv_w, conv_b, wa, ba, wx, bx, lam), comm=comm)
    return (*res, extra)


def in_proj_bwd(dgs, w_in, z, dx1, gain, scale, comm=None):
    def body(*refs):
        dg_refs = refs[:7]
        w_ref, z_ref, dx1_ref, g_ref, sc_ref, gx_ref, dsh_ref, dsc_ref, dgn_ref = refs[7:]
        i = pl.program_id(0)
        dxn = _dot_nt(dg_refs[0][...], w_ref[:, 0:D_MODEL])
        for g in range(1, 7):
            dxn = dxn + _dot_nt(dg_refs[g][...], w_ref[:, g * D_MODEL:(g + 1) * D_MODEL])
        dx, dsh, dsc, dgn = _norm_mod_bwd(z_ref[...], dxn, g_ref[...], sc_ref[0])

        @pl.when(i <= 1)
        def _():
            dsh_ref[0] = dsh
            dsc_ref[0] = dsc

        @pl.when(i > 1)
        def _():
            dsh_ref[0] = dsh_ref[0] + dsh
            dsc_ref[0] = dsc_ref[0] + dsc

        @pl.when(i == 0)
        def _():
            dgn_ref[...] = dgn

        @pl.when(i > 0)
        def _():
            dgn_ref[...] = dgn_ref[...] + dgn
            gx_ref[...] = dx1_ref[...] + dx

    zrow = pl.BlockSpec((ROW_TILE, D_MODEL), lambda i: (i, 0))
    lat = pl.BlockSpec((ROW_TILE, D_MODEL), lambda i: (jnp.maximum(i - 1, 0), 0))
    mod = pl.BlockSpec((1, 1, D_MODEL), lambda i: (jnp.minimum(i, 1), 0, 0))
    mshape = jax.ShapeDtypeStruct((2, 1, D_MODEL), F32)
    res, extra = _call(
        body, name="in_proj_bwd", grid=(ZLEN // ROW_TILE,),
        in_specs=[zrow] * 7 + [_full((D_MODEL, IN_COLS)), zrow, lat, _full((1, D_MODEL)), mod],
        out_specs=[lat, mod, mod, _full((1, D_MODEL))],
        out_shape=[jax.ShapeDtypeStruct((SEQ, D_MODEL), F32), mshape, mshape, jax.ShapeDtypeStruct((1, D_MODEL), F32)],
        sem=("arbitrary",), args=(*dgs, w_in, z, dx1, gain, scale), comm=comm)
    return (*res, extra)


def matmul_tn(a, b, name, tm, tn, prev=None, col_block=0, total_cols=None):
    k, m = a.shape
    n = b.shape[1]
    total_cols = n if total_cols is None else total_cols
    assert m % tm == 0 and n % tn == 0
    off = col_block * (n // tn)

    def body(a_ref, b_ref, *rest):
        rest[-1][...] = _dot_tn(a_ref[...].astype(BF16), b_ref[...]).astype(BF16)

    in_specs = [pl.BlockSpec((k, tm), lambda i, j: (0, i)), pl.BlockSpec((k, tn), lambda i, j: (0, j))]
    args = [a, b]
    aliases = {}
    if prev is not None:
        in_specs.append(pl.BlockSpec(memory_space=pl.ANY))
        args.append(prev)
        aliases = {2: 0}
    return pl.pallas_call(
        body, name=name, grid=(m // tm, n // tn), in_specs=in_specs,
        out_specs=pl.BlockSpec((tm, tn), lambda i, j: (i, j + off)),
        out_shape=jax.ShapeDtypeStruct((m, total_cols), BF16),
        input_output_aliases=aliases,
        compiler_params=_params("parallel", "parallel"),
    )(*args)


def local_step(z, target, modx, modc, norm_mix_g, norm_ffn_g, w_in, conv_w, conv_b, wa, ba, wx, bx, lam, qg, kg, rpb,
               w_rnn, w_na, w_out, w_up, fconv_w, fconv_b, w_down, idx=None, bt=None):
    dist = idx is not None
    c_idx = idx[1:2] if dist else None
    d = D_MODEL
    mx = [modx[:, k * d:(k + 1) * d] for k in range(N_MOD)]
    shift = jnp.stack([modc[:, 0:d], mx[0]])
    scale = jnp.stack([modc[:, d:2 * d], mx[1]])
    cos, sin = _rope_tables()
    ones = _head_ones()
    qg2 = jnp.tile(qg, (1, 2))
    kg2 = jnp.tile(kg, (1, 2))

    xn = norm_mod(z, norm_mix_g, shift, scale, "norm_mix")
    if bt is None:
        bt, _ = bias_table(rpb)
    p, got = matmul_wide(xn, w_in, "in_proj", 3 * ROW_TILE, 1792,
                         comm=gather_weights_comm([w_rnn, w_na, w_out], [1, 2, 3]) if dist else None)
    if dist:
        w_rnn, w_na, w_out = got
    y_rnn, _ = lru_fwd(p, conv_w, conv_b, wa, ba, wx, bx, lam)
    q_rot, q_pl, kk, vv, got = qkv_prep(p, qg2, kg2, cos, sin, ones,
                                        comm=gather_weights_comm([w_down], [5]) if dist else None)
    if dist:
        w_down = got[0]
    y_na, lse, got = attn_fwd(q_rot, q_pl, kk, vv, bt, comm=gather_weights_comm([w_up], [4]) if dist else None)
    if dist:
        w_up = got[0]
    u, v, merged, out, x1 = merge_fwd(y_rnn, y_na, p, z, mx[2], w_rnn, w_na, w_out)
    xn2 = norm_mod(x1, norm_ffn_g, mx[3][None], mx[4][None], "norm_ffn")
    hpre, _ = matmul_wide(xn2, w_up, "ffn_up", 2 * ROW_TILE, 1408)
    act = ffn_act(hpre, fconv_w, fconv_b)
    f, dy, df, loss_sq, dg5 = ffn_down_loss(act, w_down, x1, mx[5], target)

    partials, pieces = {}, {}

    def views_of(which, grads):
        return [_grad_view(g, BIG[w][1], BIG[w][2]) for w, g in zip(which, grads)]

    def chip_partials(which, views, recv):
        for w, gv, r in zip(which, views, recv):
            partials[w] = add_halves(gv, r, c_idx, "add_halves_" + BIG[w][0])
        return scatter_pieces_comm([partials[w] for w in which], which)

    d_act = ffn_down_bwd(df, w_down)
    dha, dhg, d_fcw_a, d_fcw_g, d_fcb_a, d_fcb_g = ffn_act_bwd(hpre, d_act, fconv_w, fconv_b)
    d_fcw = jnp.concatenate([d_fcw_a, d_fcw_g], axis=1)
    d_fcb = jnp.concatenate([d_fcb_a, d_fcb_g], axis=1)
    dx1, d_s3, d_s4, d_gffn = ffn_up_bwd(dha, dhg, w_up, x1, dy, norm_ffn_g, mx[4])
    g_w_down = matmul_tn(act, df, "gw_down", 256, D_MODEL)
    g_w_up = matmul_tn(xn2, dha, "gw_up_a", 512, 1408, total_cols=2 * D_FF)
    g_w_up = matmul_tn(xn2, dhg, "gw_up_g", 512, 1408, prev=g_w_up, col_block=1, total_cols=2 * D_FF)
    v_ffn = views_of([4, 5], [g_w_up, g_w_down]) if dist else None
    *mb, got = merge_bwd(dx1, out, mx[2], p, u, v, w_rnn, w_na, w_out,
                         comm=exchange_halves_comm(v_ffn) if dist else None)
    dout, du, dv, dmr, dmn, dyr, dyn, dg2 = mb
    recv_ffn = got
    g_w_out = matmul_tn(merged, dout, "gw_out", 1024, 512)
    g_w_rnn = matmul_tn(y_rnn, du, "gw_rnn", 1024, 512)
    g_w_na = matmul_tn(y_na, dv, "gw_na", 1024, 512)
    v_mix = views_of([1, 2, 3], [g_w_rnn, g_w_na, g_w_out]) if dist else None
    *lru_grads, got = lru_bwd(p, dyr, conv_w, conv_b, wa, ba, wx, bx, lam,
                              comm=join_comms(chip_partials([4, 5], v_ffn, recv_ffn),
                                              exchange_halves_comm(v_mix)) if dist else None)
    dxr, dgx, d_cw, d_cb, d_wa, d_ba, d_wx, d_bx, d_lam = lru_grads
    recv_mix = got[2:]
    if dist:
        pieces[4], pieces[5] = got[:2]
    lru_w_all = {}
    dqr, dqp, dk, dvh, dbt, got = attn_bwd(
        q_rot, q_pl, kk, vv, bt, y_na, dyn, lse,
        comm=join_comms(all_gather_comm(d_wa.reshape(-1, LRU_BLOCK_W)),
                        all_gather_comm(d_wx.reshape(-1, LRU_BLOCK_W))) if dist else None)
    if dist:
        lru_w_all["lru_wa"], lru_w_all["lru_wx"] = got
    dq_cols, dk_cols, dv_cols, d_qg, d_kg, got = qkv_bwd(
        dqr, dqp, dk, dvh, p, qg2, kg2, cos, sin, ones,
        comm=chip_partials([1, 2, 3], v_mix, recv_mix) if dist else None)
    if dist:
        pieces[1], pieces[2], pieces[3] = got
    d_rpb = rpb_grad(dbt)
    dgs = [dxr, dk_cols, dv_cols, dgx, dq_cols, dmr, dmn]
    g_w_in = None
    for g in range(7):
        g_w_in = matmul_tn(xn, dgs[g], "gw_in_%d" % g, 1024, 512, prev=g_w_in, col_block=g, total_cols=IN_COLS)
    if dist:
        v_in = views_of([0], [g_w_in])
        recv_in = run_comm(exchange_halves_comm(v_in), "grad_exchange_w_in")
    grad_x, dsh, dsc, d_gmix, got = in_proj_bwd(dgs, w_in, z, dx1, norm_mix_g, scale,
                                                comm=chip_partials([0], v_in, recv_in) if dist else None)
    if dist:
        pieces[0] = got[0]

    d_modx = jnp.concatenate([dsh[1], dsc[1], dg2, d_s3, d_s4, dg5], axis=1)
    d_modc = jnp.concatenate([dsh[0], dsc[0]], axis=1)
    return dict(loss_sq=loss_sq, grad_x=grad_x, d_modx=d_modx, d_modc=d_modc, norm_mix_g=d_gmix, norm_ffn_g=d_gffn,
                w_in=g_w_in, lru_conv_w=d_cw, lru_conv_b=d_cb, lru_wa=d_wa, lru_ba=d_ba, lru_wx=d_wx, lru_bx=d_bx,
                lru_lambda=d_lam, q_norm_g=d_qg, k_norm_g=d_kg, na_rpb=d_rpb, w_rnn_out=g_w_rnn, w_na_out=g_w_na,
                w_out=g_w_out, w_up=g_w_up, ffn_conv_w=d_fcw, ffn_conv_b=d_fcb, w_down=g_w_down,
                partials=partials, pieces=pieces, lru_w_all=lru_w_all)


def _mesh_pos():
    return lax.axis_index("x"), lax.axis_index("y"), lax.axis_index("c")


def _other_chips(x, y):
    return [(1 - x, y), (x, 1 - y), (1 - x, 1 - y)]


BIG = (("w_in", (D_MODEL, IN_COLS), 1), ("w_rnn_out", (D_MODEL, D_MODEL), 0), ("w_na_out", (D_MODEL, D_MODEL), 0),
       ("w_out", (D_MODEL, D_MODEL), 0), ("w_up", (D_MODEL, 2 * D_FF), 1), ("w_down", (D_FF, D_MODEL), 0))


def _shard_shape(full, axis):
    r, c = full
    return (r // N_SHARD, c) if axis == 0 else (r, c // N_SHARD)


def _slot(ref, full, axis, s, h):
    r, c = full
    if axis == 0:
        rs = r // N_SHARD
        return ref.at[pl.ds(s * rs + h * (rs // 2), rs // 2), :]
    cs = c // N_SHARD
    return ref.at[pl.ds(h * (r // 2), r // 2), pl.ds(s * cs, cs)]


def cast_into_full(x, full, axis, idx, name):
    r, c = x.shape
    tr = next(t for t in (512, 352, 256, 128) if r % t == 0)
    nb = r // tr

    def body(idx_ref, x_ref, o_ref):
        o_ref[...] = x_ref[...].astype(BF16)

    if axis == 0:
        out_spec = pl.BlockSpec((tr, c), lambda i, idx_ref: (idx_ref[0] * nb + i, 0))
    else:
        out_spec = pl.BlockSpec((tr, c), lambda i, idx_ref: (i, idx_ref[0]))
    return pl.pallas_call(
        body, name=name,
        grid_spec=pltpu.PrefetchScalarGridSpec(
            num_scalar_prefetch=1, grid=(nb,), in_specs=[pl.BlockSpec((tr, c), lambda i, idx_ref: (i, 0))],
            out_specs=out_spec),
        out_shape=jax.ShapeDtypeStruct(full, BF16),
        compiler_params=_params("parallel"),
    )(idx, x)


def run_comm(comm, name):
    k_in, k_out = len(comm.inputs), len(comm.out_shapes)

    def body(*refs):
        start, mid, end = comm.emit(refs[:k_in], refs[k_in:k_in + k_out], refs[k_in + k_out:])
        start()
        mid()
        end()

    hbm = pl.BlockSpec(memory_space=pl.ANY)
    return pl.pallas_call(
        body, name=name, in_specs=[hbm] * k_in, out_specs=[hbm] * k_out, out_shape=list(comm.out_shapes),
        input_output_aliases=dict(comm.aliases), scratch_shapes=list(comm.scratch),
        compiler_params=pltpu.CompilerParams(vmem_limit_bytes=VMEM_LIMIT_V7X),
    )(*comm.inputs)


def gather_weights_comm(fulls, which):
    nw = len(which)
    specs = [BIG[w] for w in which]

    def emit(_, outs, sems):
        send1, recv1, send2, recv2 = sems
        x, y, c = _mesh_pos()
        sibling = (x, y, 1 - c)
        chips = _other_chips(x, y)
        s_me = 2 * x + y
        shards = [2 * chip[0] + chip[1] for chip in chips]

        def ici(w, j, shard):
            _, full, axis = specs[w]
            dst = _slot(outs[w], full, axis, shard, c)
            return pltpu.make_async_remote_copy(
                src_ref=dst, dst_ref=dst, send_sem=send1.at[3 * w + j],
                recv_sem=recv1.at[3 * w + j], device_id=(*chips[j], c), device_id_type=MESH_T)

        def d2d(w, j, shard, half):
            _, full, axis = specs[w]
            dst = _slot(outs[w], full, axis, shard, half)
            return pltpu.make_async_remote_copy(
                src_ref=dst, dst_ref=dst, send_sem=send2.at[3 * w + j], recv_sem=recv2.at[3 * w + j],
                device_id=sibling, device_id_type=MESH_T)

        pairs = [(w, j) for w in range(nw) for j in range(3)]

        def start():
            for w, j in pairs:
                ici(w, j, s_me).start()

        def mid():
            for w, j in pairs:
                ici(w, j, shards[j]).wait_recv()
                d2d(w, j, shards[j], c).start()

        def end():
            for w, j in pairs:
                d2d(w, j, shards[j], 1 - c).wait_recv()
            for w, j in pairs:
                ici(w, j, s_me).wait_send()
                d2d(w, j, shards[j], c).wait_send()

        return start, mid, end

    return Comm(list(fulls), [jax.ShapeDtypeStruct(full, BF16) for _, full, _ in specs], {i: i for i in range(nw)},
                [pltpu.SemaphoreType.DMA((3 * nw,))] * 4, emit)


def join_comms(a, b):
    ai, ao, asc = len(a.inputs), len(a.out_shapes), len(a.scratch)

    def emit(ins, outs, sems):
        fa = a.emit(ins[:ai], outs[:ao], sems[:asc])
        fb = b.emit(ins[ai:], outs[ao:], sems[asc:])

        def both(k):
            def run():
                fa[k]()
                fb[k]()
            return run

        return both(0), both(1), both(2)

    aliases = dict(a.aliases)
    aliases.update({ai + i: ao + o for i, o in b.aliases.items()})
    return Comm(a.inputs + b.inputs, a.out_shapes + b.out_shapes, aliases, a.scratch + b.scratch, emit)


def all_gather_comm(x):
    def emit(srcs, outs, sems):
        send_sems, recv_sems, local_sem = sems
        x_ref, out_ref = srcs[0], outs[0]
        x, y, c = _mesh_pos()
        me, sibling = (x, y, c), (x, y, 1 - c)
        chips = _other_chips(x, y)

        def blk(px, py, pc):
            return out_ref.at[4 * px + 2 * py + pc]

        def copy(k, block, to, src=None):
            return pltpu.make_async_remote_copy(
                src_ref=blk(*block) if src is None else src, dst_ref=blk(*block),
                send_sem=send_sems.at[k], recv_sem=recv_sems.at[k], device_id=to, device_id_type=MESH_T)

        def mine():
            return pltpu.make_async_copy(x_ref, blk(*me), local_sem)

        def start():
            mine().start()
            copy(0, me, sibling, src=x_ref).start()
            for j, chip in enumerate(chips):
                copy(1 + j, me, (*chip, c), src=x_ref).start()

        def mid():
            for j, chip in enumerate(chips):
                copy(1 + j, (*chip, c), me).wait_recv()
                copy(4 + j, (*chip, c), sibling).start()

        def end():
            copy(0, sibling, me).wait_recv()
            for j, chip in enumerate(chips):
                copy(4 + j, (*chip, 1 - c), me).wait_recv()
            copy(0, me, sibling, src=x_ref).wait_send()
            for j, chip in enumerate(chips):
                copy(1 + j, me, (*chip, c), src=x_ref).wait_send()
                copy(4 + j, (*chip, c), sibling).wait_send()
            mine().wait()

        return start, mid, end

    return Comm([x], [jax.ShapeDtypeStruct((N_DEV,) + x.shape, F32)], {},
                [pltpu.SemaphoreType.DMA((7,)), pltpu.SemaphoreType.DMA((7,)), pltpu.SemaphoreType.DMA], emit)


def sum_blocks(g, name):
    _, r, c = g.shape
    tr = 256 if r % 256 == 0 else r

    def body(g_ref, o_ref):
        acc = g_ref[0]
        for k in range(1, N_DEV):
            acc = acc + g_ref[k]
        o_ref[...] = acc

    return pl.pallas_call(
        body, name=name, grid=(r // tr,),
        in_specs=[pl.BlockSpec((N_DEV, tr, c), lambda i: (0, i, 0))],
        out_specs=pl.BlockSpec((tr, c), lambda i: (i, 0)),
        out_shape=jax.ShapeDtypeStruct((r, c), F32),
        compiler_params=_params("parallel"),
    )(g)


def _grad_view(g, full, axis):
    r, c = full
    if axis == 0:
        return g.reshape(N_SHARD, 2, r // N_SHARD // 2, c)
    return g.reshape(1, 2, r // 2, c)


def exchange_halves_comm(gviews):
    nw = len(gviews)

    def emit(srcs, outs, sems):
        send_sems, recv_sems = sems
        x, y, c = _mesh_pos()

        def copies():
            return [pltpu.make_async_remote_copy(
                src_ref=srcs[w].at[:, pl.ds(1 - c, 1)], dst_ref=outs[w], send_sem=send_sems.at[w],
                recv_sem=recv_sems.at[w], device_id=(x, y, 1 - c), device_id_type=MESH_T) for w in range(nw)]

        def start():
            for cp in copies():
                cp.start()

        def end():
            for cp in copies():
                cp.wait()

        return start, lambda: None, end

    return Comm(list(gviews), [jax.ShapeDtypeStruct((g.shape[0], 1) + g.shape[2:], BF16) for g in gviews], {},
                [pltpu.SemaphoreType.DMA((nw,)), pltpu.SemaphoreType.DMA((nw,))], emit)


def _row_tile(rh):
    return 128 if rh % 128 == 0 else rh


def add_halves(gview, recv, c_idx, name):
    a, _, rh, cc = gview.shape
    tr = _row_tile(rh)

    def body(c_ref, g_ref, r_ref, o_ref):
        o_ref[0] = (g_ref[0, 0].astype(F32) + r_ref[0, 0].astype(F32)).astype(BF16)

    return pl.pallas_call(
        body, name=name,
        grid_spec=pltpu.PrefetchScalarGridSpec(
            num_scalar_prefetch=1, grid=(a, rh // tr),
            in_specs=[pl.BlockSpec((1, 1, tr, cc), lambda s, i, c_ref: (s, c_ref[0], i, 0)),
                      pl.BlockSpec((1, 1, tr, cc), lambda s, i, c_ref: (s, 0, i, 0))],
            out_specs=pl.BlockSpec((1, tr, cc), lambda s, i, c_ref: (s, i, 0))),
        out_shape=jax.ShapeDtypeStruct((a, rh, cc), BF16),
        compiler_params=_params("parallel", "parallel"),
    )(c_idx, gview, recv)


def _piece_shape(full, axis):
    rs, cs = _shard_shape(full, axis)
    return (rs // 2, cs)


def scatter_pieces_comm(partials, which):
    nw = len(which)
    specs = [BIG[w] for w in which]

    def emit(srcs, outs, sems):
        send_sems, recv_sems = sems
        x, y, c = _mesh_pos()
        chips = _other_chips(x, y)

        def copies():
            cps = []
            for w, (_, full, axis) in enumerate(specs):
                cs = full[1] // N_SHARD
                for j, chip in enumerate(chips):
                    s_j = 2 * chip[0] + chip[1]
                    src = srcs[w].at[s_j] if axis == 0 else srcs[w].at[0, :, pl.ds(s_j * cs, cs)]
                    cps.append(pltpu.make_async_remote_copy(
                        src_ref=src, dst_ref=outs[w].at[j], send_sem=send_sems.at[3 * w + j],
                        recv_sem=recv_sems.at[3 * w + j], device_id=(*chip, c), device_id_type=MESH_T))
            return cps

        def start():
            for cp in copies():
                cp.start()

        def mid():
            pass

        def end():
            for cp in copies():
                cp.wait()

        return start, mid, end

    return Comm(list(partials), [jax.ShapeDtypeStruct((3,) + _piece_shape(full, axis), BF16) for _, full, axis in specs],
                {}, [pltpu.SemaphoreType.DMA((3 * nw,)), pltpu.SemaphoreType.DMA((3 * nw,))], emit)


def add_pieces(partial, recv, idx, axis, name):
    _, rh, cs = recv.shape
    tr = _row_tile(rh)

    def body(idx_ref, p_ref, r_ref, o_ref):
        o_ref[0] = ((p_ref[0].astype(F32) + r_ref[0].astype(F32)) + r_ref[1].astype(F32)) + r_ref[2].astype(F32)

    if axis == 0:
        pspec = pl.BlockSpec((1, tr, cs), lambda i, idx_ref: (idx_ref[0], i, 0))
    else:
        pspec = pl.BlockSpec((1, tr, cs), lambda i, idx_ref: (0, i, idx_ref[0]))
    return pl.pallas_call(
        body, name=name,
        grid_spec=pltpu.PrefetchScalarGridSpec(
            num_scalar_prefetch=1, grid=(rh // tr,),
            in_specs=[pspec, pl.BlockSpec((3, tr, cs), lambda i, idx_ref: (0, i, 0))],
            out_specs=pl.BlockSpec((1, tr, cs), lambda i, idx_ref: (idx_ref[1], i, 0))),
        out_shape=jax.ShapeDtypeStruct((2, rh, cs), F32),
        compiler_params=_params("parallel"),
    )(idx, partial, recv)


def join_halves_comm(halves):
    nw = len(halves)

    def emit(_, outs, sems):
        send_sems, recv_sems = sems
        x, y, c = _mesh_pos()

        def copy(w, half):
            return pltpu.make_async_remote_copy(
                src_ref=outs[w].at[half], dst_ref=outs[w].at[half], send_sem=send_sems.at[w], recv_sem=recv_sems.at[w],
                device_id=(x, y, 1 - c), device_id_type=MESH_T)

        def start():
            for w in range(nw):
                copy(w, c).start()

        def end():
            for w in range(nw):
                copy(w, c).wait_send()
                copy(w, 1 - c).wait_recv()

        return start, lambda: None, end

    return Comm(list(halves), [jax.ShapeDtypeStruct(h.shape, F32) for h in halves], {i: i for i in range(nw)},
                [pltpu.SemaphoreType.DMA((nw,))] * 2, emit)


MOD_COLS = N_MOD * D_MODEL // N_SHARD
MOD_TILE = 512


def mod_fwd(c16, w_mod):
    def body(c_ref, w_ref, s_ref, o_ref):
        cv = c_ref[...]
        s = cv * _sigmoid(cv)
        s_ref[...] = s
        o_ref[...] = jnp.dot(s.astype(BF16), w_ref[...].astype(BF16), preferred_element_type=F32)

    return pl.pallas_call(
        body, name="mod_fwd", grid=(MOD_COLS // MOD_TILE,),
        in_specs=[_full((16, D_MODEL)), pl.BlockSpec((D_MODEL, MOD_TILE), lambda j: (0, j))],
        out_specs=[_full((16, D_MODEL)), pl.BlockSpec((16, MOD_TILE), lambda j: (0, j))],
        out_shape=[jax.ShapeDtypeStruct((16, D_MODEL), F32), jax.ShapeDtypeStruct((16, MOD_COLS), F32)],
        compiler_params=_params("arbitrary"),
    )(c16, w_mod)


def mod_bwd(s16, dm16, w_mod):
    hi = lax.Precision.HIGHEST

    def body(s_ref, d_ref, w_ref, gw_ref, ds_ref):
        j = pl.program_id(0)
        dm = d_ref[...]
        gw_ref[...] = lax.dot_general(s_ref[...], dm, (((0,), (0,)), ((), ())), preferred_element_type=F32, precision=hi)
        part = lax.dot_general(dm, w_ref[...], (((1,), (1,)), ((), ())), preferred_element_type=F32, precision=hi)

        @pl.when(j == 0)
        def _():
            ds_ref[...] = part

        @pl.when(j > 0)
        def _():
            ds_ref[...] = ds_ref[...] + part

    return pl.pallas_call(
        body, name="mod_bwd", grid=(MOD_COLS // MOD_TILE,),
        in_specs=[_full((16, D_MODEL)), pl.BlockSpec((16, MOD_TILE), lambda j: (0, j)),
                  pl.BlockSpec((D_MODEL, MOD_TILE), lambda j: (0, j))],
        out_specs=[pl.BlockSpec((D_MODEL, MOD_TILE), lambda j: (0, j)), _full((16, D_MODEL))],
        out_shape=[jax.ShapeDtypeStruct((D_MODEL, MOD_COLS), F32), jax.ShapeDtypeStruct((16, D_MODEL), F32)],
        compiler_params=_params("arbitrary"),
    )(s16, dm16, w_mod)


def cctx_grad(parts, c_ctx):
    def body(p_ref, c_ref, o_ref):
        ds = p_ref[0:1, :]
        for s in range(1, N_SHARD):
            ds = ds + p_ref[16 * s:16 * s + 1, :]
        cv = c_ref[...]
        sg = _sigmoid(cv)
        o_ref[...] = ds * (sg * (1.0 + cv * (1.0 - sg)))

    return pl.pallas_call(
        body, name="cctx_grad", in_specs=[_full((N_DEV * 8, D_MODEL)), _full((1, D_MODEL))],
        out_specs=_full((1, D_MODEL)), out_shape=jax.ShapeDtypeStruct((1, D_MODEL), F32),
    )(parts, c_ctx)


def add_rows(a, b, name):
    def body(a_ref, b_ref, o_ref):
        o_ref[...] = a_ref[...] + b_ref[...]

    return pl.pallas_call(body, name=name, in_specs=[_full(a.shape), _full(b.shape)], out_specs=_full(a.shape),
                          out_shape=jax.ShapeDtypeStruct(a.shape, F32))(a, b)


def _adamw_update(w_ref, g_ref, m_ref, v_ref, d_ref, nm_ref, nv_ref):
    g_ = g_ref[...]
    m_ = ADAM_B1 * m_ref[...] + (1.0 - ADAM_B1) * g_
    v_ = ADAM_B2 * v_ref[...] + (1.0 - ADAM_B2) * (g_ * g_)
    m_hat = m_ / (1.0 - ADAM_B1 ** ADAM_STEP)
    v_hat = v_ / (1.0 - ADAM_B2 ** ADAM_STEP)
    d_ref[...] = -ADAM_LR * (m_hat / (jnp.sqrt(v_hat) + ADAM_EPS) + ADAM_WD * w_ref[...])
    nm_ref[...] = m_
    nv_ref[...] = v_


def adamw_many(ws, gs, ms, vs):
    n = len(ws)

    def body(*refs):
        for i in range(n):
            _adamw_update(*[refs[k * n + i] for k in range(7)])

    shapes = [jax.ShapeDtypeStruct(w.shape, F32) for w in ws]
    return pl.pallas_call(body, name="adamw_small", out_shape=shapes * 3,
                          compiler_params=pltpu.CompilerParams(vmem_limit_bytes=VMEM_LIMIT_V7X))(*ws, *gs, *ms, *vs)


def adamw(w, g, m, v, name, comm=None):
    r, c = w.shape
    tr = 128 if (r % 128 == 0 and r > 128) else r

    def body(w_ref, g_ref, m_ref, v_ref, d_ref, nm_ref, nv_ref):
        _adamw_update(w_ref, g_ref, m_ref, v_ref, d_ref, nm_ref, nv_ref)

    spec = pl.BlockSpec((tr, c), lambda i: (i, 0))
    shp = jax.ShapeDtypeStruct((r, c), F32)
    res, extra = _call(body, name=name, grid=(r // tr,), in_specs=[spec] * 4, out_specs=[spec] * 3,
                       out_shape=[shp] * 3, sem=("parallel",), args=(w, g, m, v), comm=comm)
    return (*res, extra)


LANES = 1024


def _pack(arrs):
    rows, spans, at = [], [], 0
    for a in arrs:
        n = int(np.prod(a.shape))
        nr = 8 * -(-n // (8 * LANES))
        flat = a.reshape(-1)
        if nr * LANES != n:
            flat = jnp.concatenate([flat, jnp.zeros((nr * LANES - n,), F32)])
        rows.append(flat.reshape(nr, LANES))
        spans.append((at, nr, n, a.shape))
        at += nr
    return jnp.concatenate(rows, axis=0), spans


def _unpack(buf, spans):
    out = []
    for at, nr, n, shape in spans:
        out.append(buf[at:at + nr].reshape(-1)[:n].reshape(shape))
    return out


SMALL_SHARD = ("lru_conv_w", "lru_ba", "lru_bx", "lru_lambda", "ffn_conv_w")


def kernel(x, c, ctx, c_ctx, w_mod, b_mod, norm_mix_g, norm_ffn_g, w_in, lru_conv_w, lru_conv_b, lru_wa, lru_ba, lru_wx, lru_bx, lru_lambda, q_norm_g, k_norm_g, na_rpb, w_rnn_out, w_na_out, w_out, w_up, ffn_conv_w, ffn_conv_b, w_down, loss_target, m_c_ctx, m_w_mod, m_b_mod, m_norm_mix_g, m_norm_ffn_g, m_w_in, m_lru_conv_w, m_lru_conv_b, m_lru_wa, m_lru_ba, m_lru_wx, m_lru_bx, m_lru_lambda, m_q_norm_g, m_k_norm_g, m_na_rpb, m_w_rnn_out, m_w_na_out, m_w_out, m_w_up, m_ffn_conv_w, m_ffn_conv_b, m_w_down, v_c_ctx, v_w_mod, v_b_mod, v_norm_mix_g, v_norm_ffn_g, v_w_in, v_lru_conv_w, v_lru_conv_b, v_lru_wa, v_lru_ba, v_lru_wx, v_lru_bx, v_lru_lambda, v_q_norm_g, v_k_norm_g, v_na_rpb, v_w_rnn_out, v_w_na_out, v_w_out, v_w_up, v_ffn_conv_w, v_ffn_conv_b, v_w_down):
    weights = dict(c_ctx=c_ctx, w_mod=w_mod, b_mod=b_mod, norm_mix_g=norm_mix_g, norm_ffn_g=norm_ffn_g, w_in=w_in,
                   lru_conv_w=lru_conv_w, lru_conv_b=lru_conv_b, lru_wa=lru_wa, lru_ba=lru_ba, lru_wx=lru_wx,
                   lru_bx=lru_bx, lru_lambda=lru_lambda, q_norm_g=q_norm_g, k_norm_g=k_norm_g, na_rpb=na_rpb,
                   w_rnn_out=w_rnn_out, w_na_out=w_na_out, w_out=w_out, w_up=w_up, ffn_conv_w=ffn_conv_w,
                   ffn_conv_b=ffn_conv_b, w_down=w_down)
    mom1 = dict(c_ctx=m_c_ctx, w_mod=m_w_mod, b_mod=m_b_mod, norm_mix_g=m_norm_mix_g, norm_ffn_g=m_norm_ffn_g,
                w_in=m_w_in, lru_conv_w=m_lru_conv_w, lru_conv_b=m_lru_conv_b, lru_wa=m_lru_wa, lru_ba=m_lru_ba,
                lru_wx=m_lru_wx, lru_bx=m_lru_bx, lru_lambda=m_lru_lambda, q_norm_g=m_q_norm_g, k_norm_g=m_k_norm_g,
                na_rpb=m_na_rpb, w_rnn_out=m_w_rnn_out, w_na_out=m_w_na_out, w_out=m_w_out, w_up=m_w_up,
                ffn_conv_w=m_ffn_conv_w, ffn_conv_b=m_ffn_conv_b, w_down=m_w_down)
    mom2 = dict(c_ctx=v_c_ctx, w_mod=v_w_mod, b_mod=v_b_mod, norm_mix_g=v_norm_mix_g, norm_ffn_g=v_norm_ffn_g,
                w_in=v_w_in, lru_conv_w=v_lru_conv_w, lru_conv_b=v_lru_conv_b, lru_wa=v_lru_wa, lru_ba=v_lru_ba,
                lru_wx=v_lru_wx, lru_bx=v_lru_bx, lru_lambda=v_lru_lambda, q_norm_g=v_q_norm_g, k_norm_g=v_k_norm_g,
                na_rpb=v_na_rpb, w_rnn_out=v_w_rnn_out, w_na_out=v_w_na_out, w_out=v_w_out, w_up=v_w_up,
                ffn_conv_w=v_ffn_conv_w, ffn_conv_b=v_ffn_conv_b, w_down=v_w_down)
    order = list(weights)
    d = D_MODEL
    mx_, my_, mc_ = _mesh_pos()
    shard = 2 * mx_ + my_
    dev = 2 * shard + mc_

    idx = jnp.stack([shard, mc_]).astype(jnp.int32)
    wsh = {name: cast_into_full(weights[name][0], full, axis, idx, "cast_" + name) for name, full, axis in BIG}
    local_small, small_spans = _pack([c] + [weights[k][0] for k in SMALL_SHARD])
    bt, (w_in_full, gath) = bias_table(na_rpb[0], comm=join_comms(gather_weights_comm([wsh["w_in"]], [0]),
                                                                  all_gather_comm(local_small)))
    per_dev = [_unpack(gath[k], small_spans) for k in range(N_DEV)]
    c_all = jnp.concatenate([per_dev[k][0] for k in range(N_DEV)], axis=0)
    full_small = {name: jnp.concatenate([per_dev[2 * s][1 + i] for s in range(N_SHARD)], axis=-1)
                  for i, name in enumerate(SMALL_SHARD)}
    c16 = jnp.concatenate([c_all, c_ctx.reshape(1, d), jnp.zeros((7, d), F32)], axis=0)
    s16, mod_part = mod_fwd(c16, w_mod[0])
    mod_all = run_comm(all_gather_comm(mod_part), "gather_mod")[0]
    mod = jnp.concatenate([mod_all[2 * s] for s in range(N_SHARD)], axis=1) + b_mod
    modx = lax.dynamic_slice(mod, (dev, 0), (1, N_MOD * d))
    modc = mod[8:9]

    z = jnp.concatenate([ctx[0], x[0]], axis=0)
    res = local_step(z, loss_target[0], modx, modc, norm_mix_g, norm_ffn_g, w_in_full, full_small["lru_conv_w"],
                     lru_conv_b, lru_wa[0], full_small["lru_ba"], lru_wx[0], full_small["lru_bx"],
                     full_small["lru_lambda"], q_norm_g, k_norm_g, na_rpb[0], wsh["w_rnn_out"], wsh["w_na_out"],
                     wsh["w_out"], wsh["w_up"], full_small["ffn_conv_w"], ffn_conv_b, wsh["w_down"], idx=idx, bt=bt)

    halves = [add_pieces(res["partials"][i], res["pieces"][i], idx, BIG[i][2], "add_pieces_" + BIG[i][0])
              for i in range(len(BIG))]
    lru_tot = {k: sum_blocks(res["lru_w_all"][k], "sum_" + k).reshape(weights[k].shape[1:])
               for k in ("lru_wa", "lru_wx")}
    small_names = ["norm_mix_g", "norm_ffn_g", "lru_conv_w", "lru_conv_b", "lru_ba", "lru_bx",
                   "lru_lambda", "q_norm_g", "k_norm_g", "na_rpb", "ffn_conv_w", "ffn_conv_b"]
    local_g, g_spans = _pack([res["loss_sq"][0:1, 0:1], res["d_modx"], res["d_modc"]] + [res[k] for k in small_names])
    n_rows = local_g.shape[0]
    *joined, g_all = run_comm(join_comms(join_halves_comm(halves), all_gather_comm(local_g)), "tail_exchange")
    grads = {name: joined[i].reshape(_shard_shape(full, axis)) for i, (name, full, axis) in enumerate(BIG)}
    grads.update(lru_tot)
    g_tot = sum_blocks(g_all, "sum_small")
    tot = _unpack(g_tot, g_spans)
    loss = (0.5 / d) * tot[0][0, 0]
    small_tot = dict(zip(small_names, tot[3:]))
    at_x = g_spans[1][0]
    dmx_rows = g_all.reshape(N_DEV, n_rows, LANES)[:, at_x:at_x + N_MOD, :].reshape(N_DEV, N_MOD * d)
    dmc_row = jnp.concatenate([tot[2], jnp.zeros((1, 4 * d), F32)], axis=1)
    dm16 = jnp.concatenate([dmx_rows, dmc_row, jnp.zeros((7, N_MOD * d), F32)], axis=0)
    grads["b_mod"] = add_rows(tot[1], dmc_row, "b_mod_grad")
    g_w_mod, ds16 = mod_bwd(s16, lax.dynamic_slice(dm16, (0, shard * MOD_COLS), (16, MOD_COLS)), w_mod[0])
    grads["w_mod"] = g_w_mod
    for k in small_names:
        g = small_tot[k]
        if k in SMALL_SHARD:
            w_sh = weights[k].shape[-1]
            g = lax.dynamic_slice_in_dim(g, shard * w_sh, w_sh, axis=g.ndim - 1)
        grads[k] = g

    delta, new_m, new_v = {}, {}, {}
    for name, _, _ in BIG + (("w_mod", None, None),):
        *upd, got = adamw(weights[name][0], grads[name], mom1[name][0], mom2[name][0], "adamw_" + name,
                          comm=all_gather_comm(ds16[8:16]) if name == "w_in" else None)
        delta[name], new_m[name], new_v[name] = upd
        if name == "w_in":
            grads["c_ctx"] = cctx_grad(got[0].reshape(N_DEV * 8, d), c_ctx.reshape(1, d))
    rest = [k for k in order if k not in delta]
    views = {k: (grads[k].shape if grads[k].ndim <= 3 else (-1, grads[k].shape[-1])) for k in rest}
    small = adamw_many(*[[t[k].reshape(views[k]) for k in rest] for t in (weights, grads, mom1, mom2)])
    n_rest = len(rest)
    for i, k in enumerate(rest):
        delta[k], new_m[k], new_v[k] = small[i], small[n_rest + i], small[2 * n_rest + i]

    shaped = lambda t: [t[k].reshape(weights[k].shape) for k in order]
    return (loss, res["grad_x"][None], *shaped(grads), *shaped(delta), *shaped(new_m), *shaped(new_v))
```

```python
import numpy as np
import jax
import jax.numpy as jnp
from jax import lax
from jax.experimental import pallas as pl
from jax.experimental.pallas import tpu as pltpu

F32 = jnp.float32
BF16 = jnp.bfloat16

D_MODEL = 1024
SEQ = 2048
CTX_LEN = 256
ZLEN = SEQ + CTX_LEN
GRID_W = 64
GRID_ROWS = SEQ // GRID_W
LRU_BLOCK_W = 128
LRU_BLOCKS = 8
LRU_C = 8.0
NA_HEADS = 16
HEAD_DIM = 64
NA_ROWS = 8
NA_COLS = 16
ROPE_BASE = 10000.0
D_FF = 2816
N_MOD = 6
IN_COLS = 7 * D_MODEL
EPS = 1e-6
NEG_INF = -1e30
N_DEV = 8
N_SHARD = 4

ADAM_LR = 0.001
ADAM_B1 = 0.9
ADAM_B2 = 0.999
ADAM_EPS = 1e-08
ADAM_WD = 0.01
ADAM_STEP = 10

ROW_TILE = 256
Q_ROWS = 4
Q_TILE = Q_ROWS * GRID_W
KEY_ROWS = 12
KEY_TILE = KEY_ROWS * GRID_W
BT_PAD = 4
BT_LEN = 24
VMEM_LIMIT_V7X = 56 * 1024 * 1024

MESH_T = pl.DeviceIdType.MESH


def _params(*sem):
    return pltpu.CompilerParams(dimension_semantics=sem if sem else None, vmem_limit_bytes=VMEM_LIMIT_V7X)


def _full(shape):
    nd = len(shape)
    return pl.BlockSpec(shape, lambda *_: (0,) * nd)


class Comm:
    def __init__(self, inputs, out_shapes, aliases, scratch, emit):
        self.inputs, self.out_shapes, self.aliases, self.scratch, self.emit = inputs, out_shapes, aliases, scratch, emit


def _call(body, *, name, grid, in_specs, out_specs, out_shape, args, scratch_shapes=(), sem=(), comm=None):
    n_in, n_out, n_sc = len(in_specs), len(out_specs), len(scratch_shapes)
    if comm is None:
        res = pl.pallas_call(body, name=name, grid=grid, in_specs=list(in_specs), out_specs=list(out_specs),
                             out_shape=list(out_shape), scratch_shapes=list(scratch_shapes),
                             compiler_params=_params(*sem))(*args)
        return list(res), []
    k_in, k_out = len(comm.inputs), len(comm.out_shapes)
    steps = int(np.prod(grid))

    def hosted(*refs):
        ins, cins = refs[:n_in], refs[n_in:n_in + k_in]
        at = n_in + k_in
        outs, couts = refs[at:at + n_out], refs[at + n_out:at + n_out + k_out]
        at += n_out + k_out
        scr, cscr = refs[at:at + n_sc], refs[at + n_sc:]
        start, mid, end = comm.emit(cins, couts, cscr)
        lin = pl.program_id(0)
        for ax in range(1, len(grid)):
            lin = lin * grid[ax] + pl.program_id(ax)
        pl.when(lin == 0)(start)
        body(*ins, *outs, *scr)
        pl.when(lin == steps - 1 - steps // 7)(mid)
        pl.when(lin == steps - 1)(end)

    hbm = pl.BlockSpec(memory_space=pl.ANY)
    res = pl.pallas_call(
        hosted, name=name, grid=grid, in_specs=list(in_specs) + [hbm] * k_in, out_specs=list(out_specs) + [hbm] * k_out,
        out_shape=list(out_shape) + list(comm.out_shapes), scratch_shapes=list(scratch_shapes) + list(comm.scratch),
        input_output_aliases={n_in + i: n_out + o for i, o in comm.aliases.items()},
        compiler_params=_params(*(("arbitrary",) * len(grid))))(*args, *comm.inputs)
    return list(res[:n_out]), list(res[n_out:])


def _sigmoid(x):
    return 0.5 * jnp.tanh(0.5 * x) + 0.5


def _gelu_parts(x):
    c0 = 0.7978845608028654
    inner = c0 * (x + 0.044715 * x * x * x)
    t = jnp.tanh(inner)
    g = 0.5 * x * (1.0 + t)
    dg = 0.5 * (1.0 + t) + 0.5 * x * (1.0 - t * t) * c0 * (1.0 + 3.0 * 0.044715 * x * x)
    return g, dg


def _dot_nt(a, b):
    return lax.dot_general(a, b, (((1,), (1,)), ((), ())), preferred_element_type=F32)


def _dot_tn(a, b):
    return lax.dot_general(a, b, (((0,), (0,)), ((), ())), preferred_element_type=F32)


def norm_mod(xin, gain, shift, scale, name):
    r, d = xin.shape
    s_mod = shift.shape[0]
    assert r % ROW_TILE == 0

    def body(x_ref, g_ref, sh_ref, sc_ref, xn_ref):
        x = x_ref[...]
        nrm = x * lax.rsqrt(jnp.mean(x * x, axis=-1, keepdims=True) + EPS)
        xn_ref[...] = ((nrm * g_ref[...]) * (1.0 + sc_ref[0]) + sh_ref[0]).astype(BF16)

    mod_spec = pl.BlockSpec((1, 1, d), lambda i: (jnp.minimum(i, s_mod - 1), 0, 0))
    return pl.pallas_call(
        body, name=name, grid=(r // ROW_TILE,),
        in_specs=[pl.BlockSpec((ROW_TILE, d), lambda i: (i, 0)), _full((1, d)), mod_spec, mod_spec],
        out_specs=pl.BlockSpec((ROW_TILE, d), lambda i: (i, 0)),
        out_shape=jax.ShapeDtypeStruct((r, d), BF16),
        compiler_params=_params("parallel"),
    )(xin, gain, shift, scale)


def matmul_wide(a, b, name, tm, tn, comm=None):
    m, k = a.shape
    n = b.shape[1]
    assert m % tm == 0 and n % tn == 0

    def body(a_ref, b_ref, o_ref):
        o_ref[...] = jnp.dot(a_ref[...], b_ref[...], preferred_element_type=F32)

    res, extra = _call(
        body, name=name, grid=(n // tn, m // tm),
        in_specs=[pl.BlockSpec((tm, k), lambda j, i: (i, 0)), pl.BlockSpec((k, tn), lambda j, i: (0, j))],
        out_specs=[pl.BlockSpec((tm, tn), lambda j, i: (i, j))],
        out_shape=[jax.ShapeDtypeStruct((m, n), F32)],
        sem=("parallel", "parallel"), args=(a, b), comm=comm)
    return res[0], extra


def _row_ids(n, w):
    return lax.broadcasted_iota(jnp.int32, (n, w), 0)


def _lru_conv(xr, cw, cb):
    row = _row_ids(ZLEN, LRU_BLOCK_W)
    segpos = jnp.where(row < CTX_LEN, row, row - CTX_LEN)
    seglen = jnp.where(row < CTX_LEN, CTX_LEN, SEQ)
    acc = xr * cw[2:3, :] + cb
    for k in (0, 1, 3):
        off = k - 2
        sh = pltpu.roll(xr, (-off) % ZLEN, 0)
        ok = (segpos + off >= 0) & (segpos + off < seglen)
        acc = acc + jnp.where(ok, sh, 0.0) * cw[k:k + 1, :]
    return acc


def _lru_conv_t(dxc, cw):
    row = _row_ids(ZLEN, LRU_BLOCK_W)
    segpos = jnp.where(row < CTX_LEN, row, row - CTX_LEN)
    seglen = jnp.where(row < CTX_LEN, CTX_LEN, SEQ)
    acc = dxc * cw[2:3, :]
    for k in (0, 1, 3):
        off = k - 2
        sh = pltpu.roll(dxc, off % ZLEN, 0)
        ok = (segpos - off >= 0) & (segpos - off < seglen)
        acc = acc + jnp.where(ok, sh, 0.0) * cw[k:k + 1, :]
    return acc


def _lru_gates(xc, xcb, wa, ba, wx, bx, lam):
    r = _sigmoid(jnp.dot(xcb, wa, preferred_element_type=F32) + ba)
    i = _sigmoid(jnp.dot(xcb, wx, preferred_element_type=F32) + bx)
    sp = jnp.maximum(-lam, 0.0) + jnp.log1p(jnp.exp(-jnp.abs(lam)))
    la = (-LRU_C) * r * sp
    a = jnp.exp(la)
    sq = jnp.sqrt(-jnp.tanh(la) * (1.0 + a * a))
    b = sq * i * xc
    return r, i, sp, a, sq, b


def _scan8_fwd(a, b, rid):
    for s in (1, 2, 4):
        a_s = pltpu.roll(a, s, 0)
        b_s = pltpu.roll(b, s, 0)
        m = rid >= s
        b = jnp.where(m, a * b_s + b, b)
        a = jnp.where(m, a * a_s, a)
    return a, b


def _scan8_rev(a, b, rid):
    for s in (1, 2, 4):
        a_s = pltpu.roll(a, 8 - s, 0)
        b_s = pltpu.roll(b, 8 - s, 0)
        m = rid < 8 - s
        b = jnp.where(m, a * b_s + b, b)
        a = jnp.where(m, a * a_s, a)
    return a, b


N_CHUNK = ZLEN // 8
CTX_CHUNKS = CTX_LEN // 8
SCAN_UNROLL = 8


def _scan_up(a_ref, b_ref, h_ref, lo, hi, carry):
    rid = _row_ids(8, LRU_BLOCK_W)
    assert (hi - lo) % SCAN_UNROLL == 0

    def step(g, c):
        base = pl.multiple_of((lo + g * SCAN_UNROLL) * 8, 8)
        for u in range(SCAN_UNROLL):
            sl = pl.ds(base + 8 * u, 8)
            a, b = _scan8_fwd(a_ref[sl, :], b_ref[sl, :], rid)
            h_ref[sl, :] = b + a * c
            c = b[7:8, :] + a[7:8, :] * c
        return c

    return lax.fori_loop(0, (hi - lo) // SCAN_UNROLL, step, carry)


def _scan_down(a_ref, b_ref, h_ref, lo, hi, carry):
    rid = _row_ids(8, LRU_BLOCK_W)
    assert (hi - lo) % SCAN_UNROLL == 0

    def step(g, c):
        base = pl.multiple_of((hi - (g + 1) * SCAN_UNROLL) * 8, 8)
        for u in reversed(range(SCAN_UNROLL)):
            sl = pl.ds(base + 8 * u, 8)
            a, b = _scan8_rev(a_ref[sl, :], b_ref[sl, :], rid)
            h_ref[sl, :] = b + a * c
            c = b[0:1, :] + a[0:1, :] * c
        return c

    return lax.fori_loop(0, (hi - lo) // SCAN_UNROLL, step, carry)


def _lru_scan_dir(d, a_ref, b_ref, h_ref):
    zero = jnp.zeros((1, LRU_BLOCK_W), F32)
    if d == 0:
        _scan_up(a_ref, b_ref, h_ref, 0, N_CHUNK, zero)
    else:
        c = _scan_down(a_ref, b_ref, h_ref, 0, CTX_CHUNKS, zero)
        _scan_down(a_ref, b_ref, h_ref, CTX_CHUNKS, N_CHUNK, c)


def _lru_in_specs():
    blk = lambda rows: pl.BlockSpec((rows, LRU_BLOCK_W), lambda b: (0, b))
    wspec = pl.BlockSpec((2, 1, LRU_BLOCK_W, LRU_BLOCK_W), lambda b: (0, b, 0, 0))
    return blk, wspec


def lru_fwd(p, conv_w, conv_b, wa, ba, wx, bx, lam, comm=None):
    blk, wspec = _lru_in_specs()

    def body(xr_ref, gx_ref, cw_ref, cb_ref, wa_ref, ba_ref, wx_ref, bx_ref, lam_ref, y_ref, a_s, b_s, h_s, hsum_s):
        xr = xr_ref[...]
        xc = _lru_conv(xr, cw_ref[...], cb_ref[...])
        xcb = xc.astype(BF16)
        for d in (0, 1):
            _, _, _, a, _, b = _lru_gates(xc, xcb, wa_ref[d, 0].astype(BF16), ba_ref[d:d + 1, :],
                                          wx_ref[d, 0].astype(BF16), bx_ref[d:d + 1, :], lam_ref[d:d + 1, :])
            a_s[...] = a
            b_s[...] = b
            _lru_scan_dir(d, a_s, b_s, h_s)
            if d == 0:
                hsum_s[...] = h_s[...]
            else:
                hsum_s[...] = hsum_s[...] + h_s[...]
        g, _ = _gelu_parts(gx_ref[CTX_LEN:, :])
        y_ref[...] = (hsum_s[CTX_LEN:, :] * g).astype(BF16)

    zs = pltpu.VMEM((ZLEN, LRU_BLOCK_W), F32)
    res, extra = _call(
        body, name="lru_fwd", grid=(LRU_BLOCKS,),
        in_specs=[blk(ZLEN), pl.BlockSpec((ZLEN, LRU_BLOCK_W), lambda b: (0, 24 + b)), blk(4), blk(1),
                  wspec, blk(2), wspec, blk(2), blk(2)],
        out_specs=[pl.BlockSpec((SEQ, LRU_BLOCK_W), lambda b: (0, b))],
        out_shape=[jax.ShapeDtypeStruct((SEQ, D_MODEL), BF16)],
        scratch_shapes=[zs, zs, zs, zs], sem=("arbitrary",),
        args=(p, p, conv_w, conv_b, wa, ba, wx, bx, lam), comm=comm)
    return res[0], extra


def _rope_tables():
    t = np.arange(SEQ)
    lane = np.arange(2 * HEAD_DIM)
    in_head = lane % HEAD_DIM
    j = (in_head % 32) % 16
    freq = ROPE_BASE ** (-j.astype(np.float64) / 16.0)
    pos = np.where(in_head[None, :] < 32, (t // GRID_W)[:, None], (t % GRID_W)[:, None]).astype(np.float64)
    ang = (pos.astype(np.float32) * freq.astype(np.float32)[None, :]).astype(np.float32)
    cos = np.cos(ang).astype(np.float32)
    sin = np.sin(ang).astype(np.float32)
    sgn = np.where((in_head % 32) < 16, -1.0, 1.0).astype(np.float32)
    cos = np.concatenate([np.ones((CTX_LEN, 2 * HEAD_DIM), np.float32), cos], 0)
    sin = np.concatenate([np.zeros((CTX_LEN, 2 * HEAD_DIM), np.float32), sin * sgn[None, :]], 0)
    return jnp.asarray(cos), jnp.asarray(sin)


def _head_ones():
    lane = np.arange(2 * HEAD_DIM)
    return jnp.asarray((lane[:, None] // HEAD_DIM == lane[None, :] // HEAD_DIM).astype(np.float32))


def _rope_partner(x):
    lane = lax.broadcasted_iota(jnp.int32, x.shape, 1)
    return jnp.where((lane % 32) < 16, pltpu.roll(x, 128 - 16, 1), pltpu.roll(x, 16, 1))


def _head_sum(t, ones):
    hi = t.astype(BF16)
    lo = (t - hi.astype(F32)).astype(BF16)
    ones_b = ones.astype(BF16)
    return jnp.dot(hi, ones_b, preferred_element_type=F32) + jnp.dot(lo, ones_b, preferred_element_type=F32)


def _head_rms(x, ones, gain):
    ms = _head_sum(x * x, ones) * (1.0 / HEAD_DIM)
    rstd = lax.rsqrt(ms + EPS)
    return x * rstd * gain, rstd


PREP_TILE = 768


def qkv_prep(p, qg2, kg2, cos, sin, ones, comm=None):
    scale = HEAD_DIM ** -0.5

    def body(q_ref, k_ref, v_ref, qg_ref, kg_ref, cos_ref, sin_ref, ones_ref, qr_ref, qp_ref, kk_ref, vv_ref):
        ones_m = ones_ref[...]
        c, s = cos_ref[...], sin_ref[...]
        qn, _ = _head_rms(q_ref[...], ones_m, qg_ref[...])
        qn = qn * scale
        qr_ref[...] = (qn * c + _rope_partner(qn) * s).astype(BF16)
        qp_ref[...] = qn.astype(BF16)
        kn, _ = _head_rms(k_ref[...], ones_m, kg_ref[...])
        kk_ref[...] = (kn * c + _rope_partner(kn) * s).astype(BF16)
        vv_ref[...] = v_ref[...].astype(BF16)

    col = lambda base: pl.BlockSpec((PREP_TILE, 128), lambda hp, i: (i, base + hp))
    small = pl.BlockSpec((1, 128), lambda hp, i: (0, 0))
    tab = pl.BlockSpec((PREP_TILE, 128), lambda hp, i: (i, 0))
    oshape = jax.ShapeDtypeStruct((ZLEN, D_MODEL), BF16)
    res, extra = _call(
        body, name="qkv_prep", grid=(NA_HEADS // 2, ZLEN // PREP_TILE),
        in_specs=[col(32), col(8), col(16), small, small, tab, tab, _full((128, 128))],
        out_specs=[col(0)] * 4, out_shape=[oshape] * 4, sem=("parallel", "parallel"),
        args=(p, p, p, qg2, kg2, cos, sin, ones), comm=comm)
    return (*res, extra)


def _bias_expand():
    qc = np.arange(GRID_W)[:, None]
    kc = np.arange(GRID_W)[None, :]
    col_start = np.clip(qc - NA_COLS // 2, 0, GRID_W - NA_COLS)
    in_win = (kc >= col_start) & (kc < col_start + NA_COLS)
    dc = np.clip(kc - qc, -(NA_COLS - 1), NA_COLS - 1) + (NA_COLS - 1)
    e = np.zeros((2 * NA_COLS - 1, GRID_W, GRID_W), np.float32)
    for d in range(2 * NA_COLS - 1):
        e[d] = ((dc == d) & in_win).astype(np.float32)
    pen = np.where(in_win, 0.0, NEG_INF).astype(np.float32)
    return e, pen


def bias_table(rpb2, comm=None):
    e, pen = _bias_expand()
    n_dr = 2 * NA_ROWS - 1
    ea = np.zeros((31, GRID_W, 128), np.float32)
    ea[:, :, :GRID_W] = e
    eb = np.zeros((31, GRID_W, 128), np.float32)
    eb[:, :, GRID_W:] = e
    pen2 = np.concatenate([pen, pen], 1)
    ea = jnp.asarray(ea.reshape(31, GRID_W * 128))
    eb = jnp.asarray(eb.reshape(31, GRID_W * 128))
    sel_a = np.zeros((BT_LEN, n_dr), np.float32)
    sel_b = np.zeros((BT_LEN, n_dr), np.float32)
    for r in range(BT_LEN):
        dr = r - BT_PAD
        if 0 <= dr < n_dr:
            sel_a[r, dr] = 1.0
        if 0 <= dr + 1 < n_dr:
            sel_b[r, dr + 1] = 1.0
    sel_a, sel_b = jnp.asarray(sel_a), jnp.asarray(sel_b)
    pen2 = jnp.asarray(pen2.reshape(1, GRID_W * 128))
    hi = lax.Precision.HIGHEST

    def body(rpb_ref, sa_ref, sb_ref, ea_ref, eb_ref, pen_ref, o_ref, ra_s, rb_s):
        for h in range(NA_HEADS):
            rp = rpb_ref[h]
            ra_s[h * BT_LEN:(h + 1) * BT_LEN, :] = jnp.dot(sa_ref[...], rp, preferred_element_type=F32, precision=hi)
            rb_s[h * BT_LEN:(h + 1) * BT_LEN, :] = jnp.dot(sb_ref[...], rp, preferred_element_type=F32, precision=hi)
        o_ref[...] = (jnp.dot(ra_s[...], ea_ref[...], preferred_element_type=F32, precision=hi)
                      + jnp.dot(rb_s[...], eb_ref[...], preferred_element_type=F32, precision=hi) + pen_ref[...])

    tcol = 2048
    rows = NA_HEADS * BT_LEN
    res, extra = _call(
        body, name="bias_table", grid=(GRID_W * 128 // tcol,),
        in_specs=[_full((NA_HEADS, n_dr, 31)), _full((BT_LEN, n_dr)), _full((BT_LEN, n_dr)),
                  pl.BlockSpec((31, tcol), lambda j: (0, j)), pl.BlockSpec((31, tcol), lambda j: (0, j)),
                  pl.BlockSpec((1, tcol), lambda j: (0, j))],
        out_specs=[pl.BlockSpec((rows, tcol), lambda j: (0, j))],
        out_shape=[jax.ShapeDtypeStruct((rows, GRID_W * 128), F32)],
        scratch_shapes=[pltpu.VMEM((rows, 31), F32), pltpu.VMEM((rows, 31), F32)], sem=("parallel",),
        args=(rpb2, sel_a, sel_b, ea, eb, pen2), comm=comm)
    return res[0].reshape(NA_HEADS, BT_LEN, GRID_W, 128), extra


def _key_window(j):
    ws = jnp.clip(Q_ROWS * j - 4, 0, GRID_ROWS - KEY_ROWS)
    return ws, pl.multiple_of(CTX_LEN + ws * GRID_W, 256)


def _head_mask(hh):
    lane = lax.broadcasted_iota(jnp.int32, (Q_TILE, 128), 1)
    return (lane < HEAD_DIM) if hh == 0 else (lane >= HEAD_DIM)


def _attn_scores(j, ws, q_rot_h, q_pl_h, kw, kc, hh, bt_ref, s_ref):
    s_ref[:, :KEY_TILE] = _dot_nt(q_rot_h, kw)
    s_ref[:, KEY_TILE:] = _dot_nt(q_pl_h, kc)
    lane = lax.broadcasted_iota(jnp.int32, (GRID_W, 128), 1)
    base = ws - Q_ROWS * j + (NA_ROWS - 1) + BT_PAD
    for qi in range(Q_ROWS):
        rs = jnp.clip(Q_ROWS * j + qi - NA_ROWS // 2, 0, GRID_ROWS - NA_ROWS)
        for m in range(KEY_ROWS // 2):
            k0 = ws + 2 * m
            p0 = jnp.where((k0 >= rs) & (k0 < rs + NA_ROWS), 0.0, NEG_INF)
            p1 = jnp.where((k0 + 1 >= rs) & (k0 + 1 < rs + NA_ROWS), 0.0, NEG_INF)
            pen = jnp.where(lane < GRID_W, p0, p1)
            rows = slice(qi * GRID_W, (qi + 1) * GRID_W)
            cols = slice(128 * m, 128 * (m + 1))
            s_ref[rows, cols] = s_ref[rows, cols] + bt_ref[hh, base + 2 * m - qi] + pen
    return base


def attn_fwd(q_rot, q_pl, kk, vv, bt, comm=None):
    def body(qr_ref, qp_ref, kk_ref, vv_ref, bt_ref, o_ref, lse_ref, s_ref):
        j = pl.program_id(1)
        ws, start = _key_window(j)
        win = pl.ds(start, KEY_TILE)
        kw, kc = kk_ref[win, :], kk_ref[:CTX_LEN, :]
        vw, vc = vv_ref[win, :], vv_ref[:CTX_LEN, :]
        qr, qp = qr_ref[...], qp_ref[...]
        outs = []
        for hh in range(2):
            msk = _head_mask(hh)
            _attn_scores(j, ws, jnp.where(msk, qr, 0), jnp.where(msk, qp, 0), kw, kc, hh, bt_ref, s_ref)
            s = s_ref[...]
            mx = jnp.max(s, axis=-1, keepdims=True)
            pr = jnp.exp(s - mx)
            l = jnp.sum(pr, axis=-1, keepdims=True)
            prb = pr.astype(BF16)
            o = jnp.dot(prb[:, :KEY_TILE], vw, preferred_element_type=F32)
            o = o + jnp.dot(prb[:, KEY_TILE:], vc, preferred_element_type=F32)
            outs.append(o / l)
            lse_ref[hh] = mx + jnp.log(l)
        o_ref[...] = jnp.where(_head_mask(0), outs[0], outs[1])

    qspec = pl.BlockSpec((Q_TILE, 128), lambda hp, j: (j + 1, hp))
    kspec = pl.BlockSpec((ZLEN, 128), lambda hp, j: (0, hp))
    res, extra = _call(
        body, name="attn_fwd", grid=(NA_HEADS // 2, SEQ // Q_TILE),
        in_specs=[qspec, qspec, kspec, kspec, pl.BlockSpec((2, BT_LEN, GRID_W, 128), lambda hp, j: (hp, 0, 0, 0))],
        out_specs=[pl.BlockSpec((Q_TILE, 128), lambda hp, j: (j, hp)),
                   pl.BlockSpec((2, Q_TILE, 1), lambda hp, j: (hp, j, 0))],
        out_shape=[jax.ShapeDtypeStruct((SEQ, D_MODEL), F32), jax.ShapeDtypeStruct((NA_HEADS, SEQ, 1), F32)],
        scratch_shapes=[pltpu.VMEM((Q_TILE, KEY_TILE + CTX_LEN), F32)], sem=("parallel", "arbitrary"),
        args=(q_rot, q_pl, kk, vv, bt), comm=comm)
    return res[0], res[1], extra


def merge_fwd(y_rnn, y_na, p, z, g2, w_rnn, w_na, w_out):
    def body(yr_ref, yn_ref, mr_ref, mn_ref, x_ref, g2_ref, wr_ref, wn_ref, wo_ref, u_ref, v_ref, mg_ref, out_ref, x1_ref):
        u = jnp.dot(yr_ref[...], wr_ref[...], preferred_element_type=F32)
        v = jnp.dot(yn_ref[...].astype(BF16), wn_ref[...], preferred_element_type=F32)
        merged = (_sigmoid(mr_ref[...]) * u + _sigmoid(mn_ref[...]) * v).astype(BF16)
        out = jnp.dot(merged, wo_ref[...], preferred_element_type=F32)
        u_ref[...] = u
        v_ref[...] = v
        mg_ref[...] = merged
        out_ref[...] = out
        x1_ref[...] = x_ref[...] + g2_ref[...] * out

    row = pl.BlockSpec((ROW_TILE, D_MODEL), lambda i: (i, 0))
    lat = lambda cb: pl.BlockSpec((ROW_TILE, D_MODEL), lambda i: (i + 1, cb))
    wspec = _full((D_MODEL, D_MODEL))
    f32o = jax.ShapeDtypeStruct((SEQ, D_MODEL), F32)
    return pl.pallas_call(
        body, name="merge_fwd", grid=(SEQ // ROW_TILE,),
        in_specs=[row, row, lat(5), lat(6), lat(0), _full((1, D_MODEL)), wspec, wspec, wspec],
        out_specs=[row] * 5,
        out_shape=[f32o, f32o, jax.ShapeDtypeStruct((SEQ, D_MODEL), BF16), f32o, f32o],
        compiler_params=_params("parallel"),
    )(y_rnn, y_na, p, p, z, g2, w_rnn, w_na, w_out)


FF_TILE = 256
FF_TILES = D_FF // FF_TILE


FF_ROWS = 64
FF_HALO = 8
FF_SLAB = FF_ROWS + 2 * FF_HALO


def _ffn_row_chunks(chunk, init):
    carry = chunk(0, 0, -1, init)
    carry = lax.fori_loop(1, SEQ // FF_ROWS - 1,
                          lambda ci, cr: chunk(pl.multiple_of(ci * FF_ROWS - FF_HALO, 8), FF_HALO, 0, cr), carry)
    return chunk(SEQ - FF_SLAB, 2 * FF_HALO, 1, carry)


def _ffn_shifts(edge):
    row = _row_ids(FF_SLAB, FF_TILE)

    def prev(x):
        r = pltpu.roll(x, 1, 0)
        return jnp.where(row >= 1, r, 0.0) if edge == -1 else r

    def nxt(x):
        r = pltpu.roll(x, FF_SLAB - 1, 0)
        return jnp.where(row < FF_SLAB - 1, r, 0.0) if edge == 1 else r

    return prev, nxt


def ffn_act(hpre, conv_w, conv_b):
    def body(ha_ref, hg_ref, wa_ref, wg_ref, ba_ref, bg_ref, o_ref):
        wa, wg, ba, bg = wa_ref[...], wg_ref[...], ba_ref[...], bg_ref[...]

        def chunk(lo, mid, edge, carry):
            prev, nxt = _ffn_shifts(edge)
            ha, hg = ha_ref[pl.ds(lo, FF_SLAB), :], hg_ref[pl.ds(lo, FF_SLAB), :]
            a = prev(ha) * wa[0:1] + ha * wa[1:2] + nxt(ha) * wa[2:3] + ba
            g = prev(hg) * wg[0:1] + hg * wg[1:2] + nxt(hg) * wg[2:3] + bg
            o_ref[pl.ds(lo + mid, FF_ROWS), :] = (a * _sigmoid(a) * g)[mid:mid + FF_ROWS].astype(BF16)
            return carry

        _ffn_row_chunks(chunk, 0)

    col = lambda rows, off: pl.BlockSpec((rows, FF_TILE), lambda j: (0, j + off))
    return pl.pallas_call(
        body, name="ffn_act", grid=(FF_TILES,),
        in_specs=[col(SEQ, 0), col(SEQ, FF_TILES), col(3, 0), col(3, FF_TILES), col(1, 0), col(1, FF_TILES)],
        out_specs=col(SEQ, 0),
        out_shape=jax.ShapeDtypeStruct((SEQ, D_FF), BF16),
        compiler_params=_params("parallel"),
    )(hpre, hpre, conv_w, conv_w, conv_b, conv_b)


def ffn_down_loss(act, w_down, x1, g5, target):
    def body(a_ref, w_ref, x1_ref, g5_ref, t_ref, f_ref, dy_ref, df_ref, ls_ref, dg_ref):
        i = pl.program_id(0)
        f = jnp.dot(a_ref[...], w_ref[...], preferred_element_type=F32)
        g5 = g5_ref[...]
        err = x1_ref[...] + g5 * f - t_ref[...]
        dy = err * (1.0 / D_MODEL)
        f_ref[...] = f
        dy_ref[...] = dy
        df_ref[...] = (dy * g5).astype(BF16)

        @pl.when(i == 0)
        def _():
            ls_ref[...] = jnp.zeros_like(ls_ref)
            dg_ref[...] = jnp.zeros_like(dg_ref)

        ls_ref[...] = ls_ref[...] + jnp.sum(err * err)
        dg_ref[...] = dg_ref[...] + jnp.sum(dy * f, axis=0, keepdims=True)

    row = pl.BlockSpec((ROW_TILE, D_MODEL), lambda i: (i, 0))
    f32o = jax.ShapeDtypeStruct((SEQ, D_MODEL), F32)
    return pl.pallas_call(
        body, name="ffn_down_loss", grid=(SEQ // ROW_TILE,),
        in_specs=[pl.BlockSpec((ROW_TILE, D_FF), lambda i: (i, 0)), _full((D_FF, D_MODEL)), row, _full((1, D_MODEL)), row],
        out_specs=[row, row, row, _full((8, 128)), _full((1, D_MODEL))],
        out_shape=[f32o, f32o, jax.ShapeDtypeStruct((SEQ, D_MODEL), BF16), jax.ShapeDtypeStruct((8, 128), F32),
                   jax.ShapeDtypeStruct((1, D_MODEL), F32)],
        compiler_params=_params("arbitrary"),
    )(act, w_down, x1, g5, target)


def ffn_down_bwd(df, w_down):
    def body(df_ref, w_ref, o_ref):
        o_ref[...] = _dot_nt(df_ref[...], w_ref[...])

    return pl.pallas_call(
        body, name="ffn_down_bwd", grid=(SEQ // ROW_TILE,),
        in_specs=[pl.BlockSpec((ROW_TILE, D_MODEL), lambda i: (i, 0)), _full((D_FF, D_MODEL))],
        out_specs=pl.BlockSpec((ROW_TILE, D_FF), lambda i: (i, 0)),
        out_shape=jax.ShapeDtypeStruct((SEQ, D_FF), F32),
        compiler_params=_params("parallel"),
    )(df, w_down)


def ffn_act_bwd(hpre, d_act, conv_w, conv_b):
    def body(ha_ref, hg_ref, da_ref, wa_ref, wg_ref, ba_ref, bg_ref, dha_ref, dhg_ref, dwa_ref, dwg_ref, dba_ref, dbg_ref):
        wa, wg, ba, bg = wa_ref[...], wg_ref[...], ba_ref[...], bg_ref[...]

        def chunk(lo, mid, edge, acc):
            prev, nxt = _ffn_shifts(edge)
            rows = pl.ds(lo, FF_SLAB)
            ha, hg, dact = ha_ref[rows, :], hg_ref[rows, :], da_ref[rows, :]
            hap, han, hgp, hgn = prev(ha), nxt(ha), prev(hg), nxt(hg)
            a = hap * wa[0:1] + ha * wa[1:2] + han * wa[2:3] + ba
            g = hgp * wg[0:1] + hg * wg[1:2] + hgn * wg[2:3] + bg
            sig = _sigmoid(a)
            dca = dact * g * (sig * (1.0 + a * (1.0 - sig)))
            dcg = dact * a * sig
            m = slice(mid, mid + FF_ROWS)
            sums = []
            for dc, h, hp, hn, w, dh_ref in ((dca, ha, hap, han, wa, dha_ref), (dcg, hg, hgp, hgn, wg, dhg_ref)):
                dcm = dc[m]
                sums += [jnp.sum(dcm * hp[m], axis=0, keepdims=True), jnp.sum(dcm * h[m], axis=0, keepdims=True),
                         jnp.sum(dcm * hn[m], axis=0, keepdims=True), jnp.sum(dcm, axis=0, keepdims=True)]
                dh = nxt(dc) * w[0:1] + dc * w[1:2] + prev(dc) * w[2:3]
                dh_ref[pl.ds(lo + mid, FF_ROWS), :] = dh[m].astype(BF16)
            return tuple(x + y for x, y in zip(acc, sums))

        acc = _ffn_row_chunks(chunk, tuple(jnp.zeros((1, FF_TILE), F32) for _ in range(8)))
        dwa_ref[0:1, :], dwa_ref[1:2, :], dwa_ref[2:3, :], dba_ref[...] = acc[0], acc[1], acc[2], acc[3]
        dwg_ref[0:1, :], dwg_ref[1:2, :], dwg_ref[2:3, :], dbg_ref[...] = acc[4], acc[5], acc[6], acc[7]

    col = lambda rows, off: pl.BlockSpec((rows, FF_TILE), lambda j: (0, j + off))
    hshape = jax.ShapeDtypeStruct((SEQ, D_FF), BF16)
    wshape = jax.ShapeDtypeStruct((3, D_FF), F32)
    bshape = jax.ShapeDtypeStruct((1, D_FF), F32)
    return pl.pallas_call(
        body, name="ffn_act_bwd", grid=(FF_TILES,),
        in_specs=[col(SEQ, 0), col(SEQ, FF_TILES), col(SEQ, 0), col(3, 0), col(3, FF_TILES), col(1, 0), col(1, FF_TILES)],
        out_specs=[col(SEQ, 0), col(SEQ, 0), col(3, 0), col(3, 0), col(1, 0), col(1, 0)],
        out_shape=[hshape, hshape, wshape, wshape, bshape, bshape],
        compiler_params=_params("parallel"),
    )(hpre, hpre, d_act, conv_w, conv_w, conv_b, conv_b)


def _norm_mod_bwd(x, dxn, gain, scale):
    rstd = lax.rsqrt(jnp.mean(x * x, axis=-1, keepdims=True) + EPS)
    nrm = x * rstd
    dsh = jnp.sum(dxn, axis=0, keepdims=True)
    dsc = jnp.sum(dxn * nrm, axis=0, keepdims=True) * gain
    dgn = jnp.sum(dxn * nrm, axis=0, keepdims=True) * (1.0 + scale)
    dn = dxn * (gain * (1.0 + scale))
    dx = rstd * (dn - nrm * jnp.mean(dn * nrm, axis=-1, keepdims=True))
    return dx, dsh, dsc, dgn


def ffn_up_bwd(dha, dhg, w_up, x1, dy, gain, scale):
    def body(dha_ref, dhg_ref, w_ref, x_ref, dy_ref, g_ref, sc_ref, dx_ref, dsh_ref, dsc_ref, dgn_ref):
        i = pl.program_id(0)
        dxn = _dot_nt(dha_ref[...], w_ref[:, :D_FF]) + _dot_nt(dhg_ref[...], w_ref[:, D_FF:])
        dx, dsh, dsc, dgn = _norm_mod_bwd(x_ref[...], dxn, g_ref[...], sc_ref[...])
        dx_ref[...] = dy_ref[...] + dx

        @pl.when(i == 0)
        def _():
            dsh_ref[...] = dsh
            dsc_ref[...] = dsc
            dgn_ref[...] = dgn

        @pl.when(i > 0)
        def _():
            dsh_ref[...] = dsh_ref[...] + dsh
            dsc_ref[...] = dsc_ref[...] + dsc
            dgn_ref[...] = dgn_ref[...] + dgn

    row = pl.BlockSpec((ROW_TILE, D_MODEL), lambda i: (i, 0))
    vec = _full((1, D_MODEL))
    vshape = jax.ShapeDtypeStruct((1, D_MODEL), F32)
    return pl.pallas_call(
        body, name="ffn_up_bwd", grid=(SEQ // ROW_TILE,),
        in_specs=[pl.BlockSpec((ROW_TILE, D_FF), lambda i: (i, 0)), pl.BlockSpec((ROW_TILE, D_FF), lambda i: (i, 0)),
                  _full((D_MODEL, 2 * D_FF)), row, row, vec, vec],
        out_specs=[row, vec, vec, vec],
        out_shape=[jax.ShapeDtypeStruct((SEQ, D_MODEL), F32), vshape, vshape, vshape],
        compiler_params=_params("arbitrary"),
    )(dha, dhg, w_up, x1, dy, gain, scale)


def merge_bwd(dx1, out, g2, p, u, v, w_rnn, w_na, w_out, comm=None):
    def body(dx_ref, out_ref, g2_ref, mr_ref, mn_ref, u_ref, v_ref, wr_ref, wn_ref, wo_ref,
             dout_ref, du_ref, dv_ref, dmr_ref, dmn_ref, dyr_ref, dyn_ref, dg2_ref):
        i = pl.program_id(0)

        @pl.when(i == 0)
        def _():
            dmr_ref[...] = jnp.zeros_like(dmr_ref)
            dmn_ref[...] = jnp.zeros_like(dmn_ref)
            dg2_ref[...] = jnp.zeros_like(dg2_ref)

        @pl.when(i > 0)
        def _():
            dx = dx_ref[...]
            dg2_ref[...] = dg2_ref[...] + jnp.sum(dx * out_ref[...], axis=0, keepdims=True)
            dout = (dx * g2_ref[...]).astype(BF16)
            dout_ref[...] = dout
            dm = _dot_nt(dout, wo_ref[...])
            sr = _sigmoid(mr_ref[...])
            sn = _sigmoid(mn_ref[...])
            du = (dm * sr).astype(BF16)
            dv = (dm * sn).astype(BF16)
            du_ref[...] = du
            dv_ref[...] = dv
            dmr_ref[...] = (dm * u_ref[...] * (sr * (1.0 - sr))).astype(BF16)
            dmn_ref[...] = (dm * v_ref[...] * (sn * (1.0 - sn))).astype(BF16)
            dyr_ref[...] = _dot_nt(du, wr_ref[...])
            dyn_ref[...] = _dot_nt(dv, wn_ref[...])

    lat = pl.BlockSpec((ROW_TILE, D_MODEL), lambda i: (jnp.maximum(i - 1, 0), 0))
    zrow = pl.BlockSpec((ROW_TILE, D_MODEL), lambda i: (i, 0))
    pcol = lambda cb: pl.BlockSpec((ROW_TILE, D_MODEL), lambda i: (i, cb))
    wspec = _full((D_MODEL, D_MODEL))
    tb = jax.ShapeDtypeStruct((SEQ, D_MODEL), BF16)
    zb = jax.ShapeDtypeStruct((ZLEN, D_MODEL), BF16)
    tf = jax.ShapeDtypeStruct((SEQ, D_MODEL), F32)
    res, extra = _call(
        body, name="merge_bwd", grid=(ZLEN // ROW_TILE,),
        in_specs=[lat, lat, _full((1, D_MODEL)), pcol(5), pcol(6), lat, lat, wspec, wspec, wspec],
        out_specs=[lat, lat, lat, zrow, zrow, lat, lat, _full((1, D_MODEL))],
        out_shape=[tb, tb, tb, zb, zb, tf, tf, jax.ShapeDtypeStruct((1, D_MODEL), F32)],
        sem=("arbitrary",), args=(dx1, out, g2, p, p, u, v, w_rnn, w_na, w_out), comm=comm)
    return (*res, extra)


def attn_bwd(q_rot, q_pl, kk, vv, bt, y_na, d_yna, lse, comm=None):
    def body(qr_ref, qp_ref, kk_ref, vv_ref, bt_ref, o_ref, do_ref, lse_ref,
             dqr_ref, dqp_ref, dk_ref, dv_ref, dbt_ref, s_ref):
        jj = pl.program_id(1)

        @pl.when(jj == 0)
        def _():
            dqr_ref[...] = jnp.zeros_like(dqr_ref)
            dqp_ref[...] = jnp.zeros_like(dqp_ref)
            dk_ref[...] = jnp.zeros_like(dk_ref)
            dv_ref[...] = jnp.zeros_like(dv_ref)
            dbt_ref[...] = jnp.zeros_like(dbt_ref)

        @pl.when(jj > 0)
        def _():
            j = jj - 1
            ws, start = _key_window(j)
            win = pl.ds(start, KEY_TILE)
            kw, kc = kk_ref[win, :], kk_ref[:CTX_LEN, :]
            vw, vc = vv_ref[win, :], vv_ref[:CTX_LEN, :]
            qr, qp = qr_ref[...], qp_ref[...]
            do = do_ref[...]
            do_o = do * o_ref[...]
            dq_r, dq_p = [], []
            for hh in range(2):
                msk = _head_mask(hh)
                q_r, q_p = jnp.where(msk, qr, 0), jnp.where(msk, qp, 0)
                base = _attn_scores(j, ws, q_r, q_p, kw, kc, hh, bt_ref, s_ref)
                pr = jnp.exp(s_ref[...] - lse_ref[hh])
                delta = jnp.sum(jnp.where(msk, do_o, 0.0), axis=-1, keepdims=True)
                dob = jnp.where(msk, do, 0.0).astype(BF16)
                ds_lat = pr[:, :KEY_TILE] * (_dot_nt(dob, vw) - delta)
                ds_ctx = pr[:, KEY_TILE:] * (_dot_nt(dob, vc) - delta)
                for qi in range(Q_ROWS):
                    for m in range(KEY_ROWS // 2):
                        idx = base + 2 * m - qi
                        dbt_ref[hh, idx] = dbt_ref[hh, idx] + ds_lat[qi * GRID_W:(qi + 1) * GRID_W, 128 * m:128 * (m + 1)]
                dsb_lat = ds_lat.astype(BF16)
                dsb_ctx = ds_ctx.astype(BF16)
                prb = pr.astype(BF16)
                dq_r.append(jnp.dot(dsb_lat, kw, preferred_element_type=F32))
                dq_p.append(jnp.dot(dsb_ctx, kc, preferred_element_type=F32))
                dk_ref[win, :] = dk_ref[win, :] + _dot_tn(dsb_lat, q_r)
                dk_ref[:CTX_LEN, :] = dk_ref[:CTX_LEN, :] + _dot_tn(dsb_ctx, q_p)
                dv_ref[win, :] = dv_ref[win, :] + _dot_tn(prb[:, :KEY_TILE], dob)
                dv_ref[:CTX_LEN, :] = dv_ref[:CTX_LEN, :] + _dot_tn(prb[:, KEY_TILE:], dob)
            dqr_ref[...] = jnp.where(_head_mask(0), dq_r[0], dq_r[1])
            dqp_ref[...] = jnp.where(_head_mask(0), dq_p[0], dq_p[1])

    lat = lambda jj: jnp.maximum(jj - 1, 0)
    qspec = pl.BlockSpec((Q_TILE, 128), lambda hp, jj: (lat(jj) + 1, hp))
    kspec = pl.BlockSpec((ZLEN, 128), lambda hp, jj: (0, hp))
    btspec = pl.BlockSpec((2, BT_LEN, GRID_W, 128), lambda hp, jj: (hp, 0, 0, 0))
    ospec = pl.BlockSpec((Q_TILE, 128), lambda hp, jj: (lat(jj), hp))
    dqspec = pl.BlockSpec((Q_TILE, 128), lambda hp, jj: (jj, hp))
    zshape = jax.ShapeDtypeStruct((ZLEN, D_MODEL), F32)
    res, extra = _call(
        body, name="attn_bwd", grid=(NA_HEADS // 2, ZLEN // Q_TILE),
        in_specs=[qspec, qspec, kspec, kspec, btspec, ospec, ospec,
                  pl.BlockSpec((2, Q_TILE, 1), lambda hp, jj: (hp, lat(jj), 0))],
        out_specs=[dqspec, dqspec, kspec, kspec, btspec],
        out_shape=[zshape, zshape, zshape, zshape, jax.ShapeDtypeStruct((NA_HEADS, BT_LEN, GRID_W, 128), F32)],
        scratch_shapes=[pltpu.VMEM((Q_TILE, KEY_TILE + CTX_LEN), F32)], sem=("parallel", "arbitrary"),
        args=(q_rot, q_pl, kk, vv, bt, y_na, d_yna, lse), comm=comm)
    return (*res, extra)


def qkv_bwd(dq_rot, dq_pl, dk, dv, p, qg2, kg2, cos, sin, ones, comm=None):
    scale = HEAD_DIM ** -0.5
    n_hp, n_i = NA_HEADS // 2, ZLEN // PREP_TILE

    def norm_rope_bwd(d_rot, d_extra, x, gain, cos_t, sin_t, ones_m, dx_ref, acc_ref):
        xh, rstd = _head_rms(x, ones_m, 1.0)
        dn = d_rot * cos_t + _rope_partner(d_rot * sin_t)
        if d_extra is not None:
            dn = (dn + d_extra) * scale
        acc_ref[...] = acc_ref[...] + jnp.sum(dn * xh, axis=0, keepdims=True)
        dxh = dn * gain
        seg = _head_sum(dxh * xh, ones_m) * (1.0 / HEAD_DIM)
        dx_ref[...] = (rstd * (dxh - xh * seg)).astype(BF16)

    def body(dqr_ref, dqp_ref, dk_ref, dv_ref, xq_ref, xk_ref, qg_ref, kg_ref, cos_ref, sin_ref, ones_ref,
             dxq_ref, dxk_ref, dxv_ref, dgq_ref, dgk_ref, accq_ref, acck_ref):
        hp, i = pl.program_id(0), pl.program_id(1)

        @pl.when((hp == 0) & (i == 0))
        def _():
            accq_ref[...] = jnp.zeros_like(accq_ref)
            acck_ref[...] = jnp.zeros_like(acck_ref)

        ones_m = ones_ref[...]
        cos_t, sin_t = cos_ref[...], sin_ref[...]
        norm_rope_bwd(dqr_ref[...], dqp_ref[...], xq_ref[...], qg_ref[...], cos_t, sin_t, ones_m, dxq_ref, accq_ref)
        norm_rope_bwd(dk_ref[...], None, xk_ref[...], kg_ref[...], cos_t, sin_t, ones_m, dxk_ref, acck_ref)
        dxv_ref[...] = dv_ref[...].astype(BF16)

        @pl.when((hp == n_hp - 1) & (i == n_i - 1))
        def _():
            dgq_ref[...] = accq_ref[:, :HEAD_DIM] + accq_ref[:, HEAD_DIM:]
            dgk_ref[...] = acck_ref[:, :HEAD_DIM] + acck_ref[:, HEAD_DIM:]

    col = lambda base: pl.BlockSpec((PREP_TILE, 128), lambda hp, i: (i, base + hp))
    small = pl.BlockSpec((1, 128), lambda hp, i: (0, 0))
    tab = pl.BlockSpec((PREP_TILE, 128), lambda hp, i: (i, 0))
    zb = jax.ShapeDtypeStruct((ZLEN, D_MODEL), BF16)
    gshape = jax.ShapeDtypeStruct((1, HEAD_DIM), F32)
    res, extra = _call(
        body, name="qkv_bwd", grid=(n_hp, n_i),
        in_specs=[col(0)] * 4 + [col(32), col(8), small, small, tab, tab, _full((128, 128))],
        out_specs=[col(0)] * 3 + [_full((1, HEAD_DIM))] * 2,
        out_shape=[zb, zb, zb, gshape, gshape],
        scratch_shapes=[pltpu.VMEM((1, 128), F32)] * 2, sem=("arbitrary", "arbitrary"),
        args=(dq_rot, dq_pl, dk, dv, p, p, qg2, kg2, cos, sin, ones), comm=comm)
    return (*res, extra)


def rpb_grad(dbt):
    e, _ = _bias_expand()
    n_dr = 2 * NA_ROWS - 1
    ea = np.zeros((31, GRID_W, 128), np.float32)
    ea[:, :, :GRID_W] = e
    eb = np.zeros((31, GRID_W, 128), np.float32)
    eb[:, :, GRID_W:] = e
    eat = jnp.asarray(ea.reshape(31, GRID_W * 128).T.copy())
    ebt = jnp.asarray(eb.reshape(31, GRID_W * 128).T.copy())
    sel_at = np.zeros((n_dr, BT_LEN), np.float32)
    sel_bt = np.zeros((n_dr, BT_LEN), np.float32)
    for r in range(BT_LEN):
        dr = r - BT_PAD
        if 0 <= dr < n_dr:
            sel_at[dr, r] = 1.0
        if 0 <= dr + 1 < n_dr:
            sel_bt[dr + 1, r] = 1.0
    hi = lax.Precision.HIGHEST

    tk = 2048
    wide = GRID_W * 128
    rows = NA_HEADS * BT_LEN
    n_k = wide // tk

    def body(d_ref, sa_ref, sb_ref, ea_ref, eb_ref, o_ref, a_s, b_s):
        k = pl.program_id(0)
        dm = d_ref[...]
        d_hi = dm.astype(BF16)
        rest = dm - d_hi.astype(F32)
        d_mid = rest.astype(BF16)
        d_lo = (rest - d_mid.astype(F32)).astype(BF16)
        ea_b, eb_b = ea_ref[...].astype(BF16), eb_ref[...].astype(BF16)
        a = sum(jnp.dot(t, ea_b, preferred_element_type=F32) for t in (d_hi, d_mid, d_lo))
        b = sum(jnp.dot(t, eb_b, preferred_element_type=F32) for t in (d_hi, d_mid, d_lo))

        @pl.when(k == 0)
        def _():
            a_s[...] = a
            b_s[...] = b

        @pl.when(k > 0)
        def _():
            a_s[...] = a_s[...] + a
            b_s[...] = b_s[...] + b

        @pl.when(k == n_k - 1)
        def _():
            for h in range(NA_HEADS):
                sl = slice(h * BT_LEN, (h + 1) * BT_LEN)
                o_ref[h] = (jnp.dot(sa_ref[...], a_s[sl, :], preferred_element_type=F32, precision=hi)
                            + jnp.dot(sb_ref[...], b_s[sl, :], preferred_element_type=F32, precision=hi))

    return pl.pallas_call(
        body, name="rpb_grad", grid=(n_k,),
        in_specs=[pl.BlockSpec((rows, tk), lambda k: (0, k)), _full((n_dr, BT_LEN)), _full((n_dr, BT_LEN)),
                  pl.BlockSpec((tk, 31), lambda k: (k, 0)), pl.BlockSpec((tk, 31), lambda k: (k, 0))],
        out_specs=_full((NA_HEADS, n_dr, 31)),
        out_shape=jax.ShapeDtypeStruct((NA_HEADS, n_dr, 31), F32),
        scratch_shapes=[pltpu.VMEM((rows, 31), F32), pltpu.VMEM((rows, 31), F32)],
        compiler_params=_params("arbitrary"),
    )(dbt.reshape(rows, wide), jnp.asarray(sel_at), jnp.asarray(sel_bt), eat, ebt)


def lru_bwd(p, d_yrnn, conv_w, conv_b, wa, ba, wx, bx, lam, comm=None):
    blk, wspec = _lru_in_specs()

    def body(xr_ref, gx_ref, dy_ref, cw_ref, cb_ref, wa_ref, ba_ref, wx_ref, bx_ref, lam_ref,
             dxr_ref, dgx_ref, dcw_ref, dcb_ref, dwa_ref, dba_ref, dwx_ref, dbx_ref, dlam_ref,
             a_s, b_s, h_s, l_s, hsum_s, dxc_s, dh_s):
        xr = xr_ref[...]
        cw = cw_ref[...]
        xc = _lru_conv(xr, cw, cb_ref[...])
        xcb = xc.astype(BF16)
        g, dg = _gelu_parts(gx_ref[CTX_LEN:, :])
        dy = dy_ref[...]
        dh_s[:CTX_LEN, :] = jnp.zeros((CTX_LEN, LRU_BLOCK_W), F32)
        dh_s[CTX_LEN:, :] = dy * g
        row = _row_ids(ZLEN, LRU_BLOCK_W)
        zero = jnp.zeros((1, LRU_BLOCK_W), F32)
        for d in (0, 1):
            wab = wa_ref[d, 0].astype(BF16)
            wxb = wx_ref[d, 0].astype(BF16)
            lam_d = lam_ref[d:d + 1, :]
            r, gi, sp, a, sq, b = _lru_gates(xc, xcb, wab, ba_ref[d:d + 1, :], wxb, bx_ref[d:d + 1, :], lam_d)
            a_s[...] = a
            b_s[...] = b
            _lru_scan_dir(d, a_s, b_s, h_s)
            h = h_s[...]
            if d == 0:
                hsum_s[...] = h
                h_prev = jnp.where(row >= 1, pltpu.roll(h, 1, 0), 0.0)
                a_s[...] = pltpu.roll(a, ZLEN - 1, 0)
                _scan_down(a_s, dh_s, l_s, 0, N_CHUNK, zero)
            else:
                hsum_s[...] = hsum_s[...] + h
                h_prev = jnp.where(row == CTX_LEN - 1, 0.0, pltpu.roll(h, ZLEN - 1, 0))
                a_s[...] = pltpu.roll(a, 1, 0)
                c = _scan_up(a_s, dh_s, l_s, CTX_CHUNKS, N_CHUNK, zero)
                _scan_up(a_s, dh_s, l_s, 0, CTX_CHUNKS, c)
            db = l_s[...]
            da = db * h_prev
            dsq = db * gi * xc
            dgi = db * sq * xc
            dxc_d = db * sq * gi
            dla = da * a - dsq * (a * a) / sq
            dr = dla * ((-LRU_C) * sp)
            dsp = jnp.sum(dla * ((-LRU_C) * r), axis=0, keepdims=True)
            dlam_ref[d:d + 1, :] = -dsp * _sigmoid(-lam_d)
            dzr = dr * r * (1.0 - r)
            dzi = dgi * gi * (1.0 - gi)
            dba_ref[d:d + 1, :] = jnp.sum(dzr, axis=0, keepdims=True)
            dbx_ref[d:d + 1, :] = jnp.sum(dzi, axis=0, keepdims=True)
            dzrb = dzr.astype(BF16)
            dzib = dzi.astype(BF16)
            dwa_ref[d, 0] = _dot_tn(xcb, dzrb)
            dwx_ref[d, 0] = _dot_tn(xcb, dzib)
            dxc_d = dxc_d + _dot_nt(dzrb, wab) + _dot_nt(dzib, wxb)
            if d == 0:
                dxc_s[...] = dxc_d
            else:
                dxc_s[...] = dxc_s[...] + dxc_d
        dxc = dxc_s[...]
        dxr_ref[...] = _lru_conv_t(dxc, cw).astype(BF16)
        dcb_ref[...] = jnp.sum(dxc, axis=0, keepdims=True)
        segpos = jnp.where(row < CTX_LEN, row, row - CTX_LEN)
        seglen = jnp.where(row < CTX_LEN, CTX_LEN, SEQ)
        for k in range(4):
            off = k - 2
            if off == 0:
                sh = xr
            else:
                ok = (segpos + off >= 0) & (segpos + off < seglen)
                sh = jnp.where(ok, pltpu.roll(xr, (-off) % ZLEN, 0), 0.0)
            dcw_ref[k:k + 1, :] = jnp.sum(dxc * sh, axis=0, keepdims=True)
        dgx_ref[:CTX_LEN, :] = jnp.zeros((CTX_LEN, LRU_BLOCK_W), BF16)
        dgx_ref[CTX_LEN:, :] = (dy * hsum_s[CTX_LEN:, :] * dg).astype(BF16)

    zs = pltpu.VMEM((ZLEN, LRU_BLOCK_W), F32)
    zb = jax.ShapeDtypeStruct((ZLEN, D_MODEL), BF16)
    v2 = jax.ShapeDtypeStruct((2, D_MODEL), F32)
    w4 = jax.ShapeDtypeStruct((2, LRU_BLOCKS, LRU_BLOCK_W, LRU_BLOCK_W), F32)
    res, extra = _call(
        body, name="lru_bwd", grid=(LRU_BLOCKS,),
        in_specs=[blk(ZLEN), pl.BlockSpec((ZLEN, LRU_BLOCK_W), lambda b: (0, 24 + b)), blk(SEQ), blk(4), blk(1),
                  wspec, blk(2), wspec, blk(2), blk(2)],
        out_specs=[blk(ZLEN), blk(ZLEN), blk(4), blk(1), wspec, blk(2), wspec, blk(2), blk(2)],
        out_shape=[zb, zb, jax.ShapeDtypeStruct((4, D_MODEL), F32), jax.ShapeDtypeStruct((1, D_MODEL), F32),
                   w4, v2, w4, v2, v2],
        scratch_shapes=[zs] * 7, sem=("arbitrary",),
        args=(p, p, d_yrnn, conv_w, conv_b, wa, ba, wx, bx, lam), comm=comm)
    return (*res, extra)


def in_proj_bwd(dgs, w_in, z, dx1, gain, scale, comm=None):
    def body(*refs):
        dg_refs = refs[:7]
        w_ref, z_ref, dx1_ref, g_ref, sc_ref, gx_ref, dsh_ref, dsc_ref, dgn_ref = refs[7:]
        i = pl.program_id(0)
        dxn = _dot_nt(dg_refs[0][...], w_ref[:, 0:D_MODEL])
        for g in range(1, 7):
            dxn = dxn + _dot_nt(dg_refs[g][...], w_ref[:, g * D_MODEL:(g + 1) * D_MODEL])
        dx, dsh, dsc, dgn = _norm_mod_bwd(z_ref[...], dxn, g_ref[...], sc_ref[0])

        @pl.when(i <= 1)
        def _():
            dsh_ref[0] = dsh
            dsc_ref[0] = dsc

        @pl.when(i > 1)
        def _():
            dsh_ref[0] = dsh_ref[0] + dsh
            dsc_ref[0] = dsc_ref[0] + dsc

        @pl.when(i == 0)
        def _():
            dgn_ref[...] = dgn

        @pl.when(i > 0)
        def _():
            dgn_ref[...] = dgn_ref[...] + dgn
            gx_ref[...] = dx1_ref[...] + dx

    zrow = pl.BlockSpec((ROW_TILE, D_MODEL), lambda i: (i, 0))
    lat = pl.BlockSpec((ROW_TILE, D_MODEL), lambda i: (jnp.maximum(i - 1, 0), 0))
    mod = pl.BlockSpec((1, 1, D_MODEL), lambda i: (jnp.minimum(i, 1), 0, 0))
    mshape = jax.ShapeDtypeStruct((2, 1, D_MODEL), F32)
    res, extra = _call(
        body, name="in_proj_bwd", grid=(ZLEN // ROW_TILE,),
        in_specs=[zrow] * 7 + [_full((D_MODEL, IN_COLS)), zrow, lat, _full((1, D_MODEL)), mod],
        out_specs=[lat, mod, mod, _full((1, D_MODEL))],
        out_shape=[jax.ShapeDtypeStruct((SEQ, D_MODEL), F32), mshape, mshape, jax.ShapeDtypeStruct((1, D_MODEL), F32)],
        sem=("arbitrary",), args=(*dgs, w_in, z, dx1, gain, scale), comm=comm)
    return (*res, extra)


def matmul_tn(a, b, name, tm, tn, prev=None, col_block=0, total_cols=None):
    k, m = a.shape
    n = b.shape[1]
    total_cols = n if total_cols is None else total_cols
    assert m % tm == 0 and n % tn == 0
    off = col_block * (n // tn)

    def body(a_ref, b_ref, *rest):
        rest[-1][...] = _dot_tn(a_ref[...].astype(BF16), b_ref[...]).astype(BF16)

    in_specs = [pl.BlockSpec((k, tm), lambda i, j: (0, i)), pl.BlockSpec((k, tn), lambda i, j: (0, j))]
    args = [a, b]
    aliases = {}
    if prev is not None:
        in_specs.append(pl.BlockSpec(memory_space=pl.ANY))
        args.append(prev)
        aliases = {2: 0}
    return pl.pallas_call(
        body, name=name, grid=(m // tm, n // tn), in_specs=in_specs,
        out_specs=pl.BlockSpec((tm, tn), lambda i, j: (i, j + off)),
        out_shape=jax.ShapeDtypeStruct((m, total_cols), BF16),
        input_output_aliases=aliases,
        compiler_params=_params("parallel", "parallel"),
    )(*args)


def local_step(z, target, modx, modc, norm_mix_g, norm_ffn_g, w_in, conv_w, conv_b, wa, ba, wx, bx, lam, qg, kg, rpb,
               w_rnn, w_na, w_out, w_up, fconv_w, fconv_b, w_down, idx=None, bt=None):
    dist = idx is not None
    c_idx = idx[1:2] if dist else None
    d = D_MODEL
    mx = [modx[:, k * d:(k + 1) * d] for k in range(N_MOD)]
    shift = jnp.stack([modc[:, 0:d], mx[0]])
    scale = jnp.stack([modc[:, d:2 * d], mx[1]])
    cos, sin = _rope_tables()
    ones = _head_ones()
    qg2 = jnp.tile(qg, (1, 2))
    kg2 = jnp.tile(kg, (1, 2))

    xn = norm_mod(z, norm_mix_g, shift, scale, "norm_mix")
    if bt is None:
        bt, _ = bias_table(rpb)
    p, got = matmul_wide(xn, w_in, "in_proj", 3 * ROW_TILE, 1792,
                         comm=gather_weights_comm([w_rnn, w_na, w_out], [1, 2, 3]) if dist else None)
    if dist:
        w_rnn, w_na, w_out = got
    y_rnn, got = lru_fwd(p, conv_w, conv_b, wa, ba, wx, bx, lam,
                         comm=gather_weights_comm([w_down], [5]) if dist else None)
    if dist:
        w_down = got[0]
    q_rot, q_pl, kk, vv, _ = qkv_prep(p, qg2, kg2, cos, sin, ones)
    y_na, lse, got = attn_fwd(q_rot, q_pl, kk, vv, bt, comm=gather_weights_comm([w_up], [4]) if dist else None)
    if dist:
        w_up = got[0]
    u, v, merged, out, x1 = merge_fwd(y_rnn, y_na, p, z, mx[2], w_rnn, w_na, w_out)
    xn2 = norm_mod(x1, norm_ffn_g, mx[3][None], mx[4][None], "norm_ffn")
    hpre, _ = matmul_wide(xn2, w_up, "ffn_up", 2 * ROW_TILE, 1408)
    act = ffn_act(hpre, fconv_w, fconv_b)
    f, dy, df, loss_sq, dg5 = ffn_down_loss(act, w_down, x1, mx[5], target)

    partials, pieces = {}, {}

    def views_of(which, grads):
        return [_grad_view(g, BIG[w][1], BIG[w][2]) for w, g in zip(which, grads)]

    def chip_partials(which, views, recv):
        for w, gv, r in zip(which, views, recv):
            partials[w] = add_halves(gv, r, c_idx, "add_halves_" + BIG[w][0])
        return scatter_pieces_comm([partials[w] for w in which], which)

    d_act = ffn_down_bwd(df, w_down)
    dha, dhg, d_fcw_a, d_fcw_g, d_fcb_a, d_fcb_g = ffn_act_bwd(hpre, d_act, fconv_w, fconv_b)
    d_fcw = jnp.concatenate([d_fcw_a, d_fcw_g], axis=1)
    d_fcb = jnp.concatenate([d_fcb_a, d_fcb_g], axis=1)
    dx1, d_s3, d_s4, d_gffn = ffn_up_bwd(dha, dhg, w_up, x1, dy, norm_ffn_g, mx[4])
    g_w_down = matmul_tn(act, df, "gw_down", 256, D_MODEL)
    g_w_up = matmul_tn(xn2, dha, "gw_up_a", 512, 1408, total_cols=2 * D_FF)
    g_w_up = matmul_tn(xn2, dhg, "gw_up_g", 512, 1408, prev=g_w_up, col_block=1, total_cols=2 * D_FF)
    v_ffn = views_of([4, 5], [g_w_up, g_w_down]) if dist else None
    *mb, got = merge_bwd(dx1, out, mx[2], p, u, v, w_rnn, w_na, w_out,
                         comm=exchange_halves_comm(v_ffn) if dist else None)
    dout, du, dv, dmr, dmn, dyr, dyn, dg2 = mb
    recv_ffn = got
    g_w_out = matmul_tn(merged, dout, "gw_out", 1024, 512)
    g_w_rnn = matmul_tn(y_rnn, du, "gw_rnn", 1024, 512)
    g_w_na = matmul_tn(y_na, dv, "gw_na", 1024, 512)
    v_mix = views_of([1, 2, 3], [g_w_rnn, g_w_na, g_w_out]) if dist else None
    if dist:
        chip_partials([4, 5], v_ffn, recv_ffn)
    *lru_grads, got = lru_bwd(p, dyr, conv_w, conv_b, wa, ba, wx, bx, lam,
                              comm=join_comms(scatter_pieces_comm([partials[4]], [4]),
                                              exchange_halves_comm(v_mix)) if dist else None)
    dxr, dgx, d_cw, d_cb, d_wa, d_ba, d_wx, d_bx, d_lam = lru_grads
    recv_mix = got[1:]
    if dist:
        pieces[4] = got[0]
    lru_w_all = {}
    dqr, dqp, dk, dvh, dbt, got = attn_bwd(
        q_rot, q_pl, kk, vv, bt, y_na, dyn, lse,
        comm=join_comms(scatter_pieces_comm([partials[5]], [5]),
                        join_comms(all_gather_comm(d_wa.reshape(-1, LRU_BLOCK_W)),
                                   all_gather_comm(d_wx.reshape(-1, LRU_BLOCK_W)))) if dist else None)
    if dist:
        pieces[5], lru_w_all["lru_wa"], lru_w_all["lru_wx"] = got
    dq_cols, dk_cols, dv_cols, d_qg, d_kg, got = qkv_bwd(
        dqr, dqp, dk, dvh, p, qg2, kg2, cos, sin, ones,
        comm=chip_partials([1, 2, 3], v_mix, recv_mix) if dist else None)
    if dist:
        pieces[1], pieces[2], pieces[3] = got
    d_rpb = rpb_grad(dbt)
    dgs = [dxr, dk_cols, dv_cols, dgx, dq_cols, dmr, dmn]
    g_w_in = None
    for g in range(7):
        g_w_in = matmul_tn(xn, dgs[g], "gw_in_%d" % g, 1024, 512, prev=g_w_in, col_block=g, total_cols=IN_COLS)
    if dist:
        v_in = views_of([0], [g_w_in])
        recv_in = run_comm(exchange_halves_comm(v_in), "grad_exchange_w_in")
    grad_x, dsh, dsc, d_gmix, got = in_proj_bwd(dgs, w_in, z, dx1, norm_mix_g, scale,
                                                comm=chip_partials([0], v_in, recv_in) if dist else None)
    if dist:
        pieces[0] = got[0]

    d_modx = jnp.concatenate([dsh[1], dsc[1], dg2, d_s3, d_s4, dg5], axis=1)
    d_modc = jnp.concatenate([dsh[0], dsc[0]], axis=1)
    return dict(loss_sq=loss_sq, grad_x=grad_x, d_modx=d_modx, d_modc=d_modc, norm_mix_g=d_gmix, norm_ffn_g=d_gffn,
                w_in=g_w_in, lru_conv_w=d_cw, lru_conv_b=d_cb, lru_wa=d_wa, lru_ba=d_ba, lru_wx=d_wx, lru_bx=d_bx,
                lru_lambda=d_lam, q_norm_g=d_qg, k_norm_g=d_kg, na_rpb=d_rpb, w_rnn_out=g_w_rnn, w_na_out=g_w_na,
                w_out=g_w_out, w_up=g_w_up, ffn_conv_w=d_fcw, ffn_conv_b=d_fcb, w_down=g_w_down,
                partials=partials, pieces=pieces, lru_w_all=lru_w_all)


def _mesh_pos():
    return lax.axis_index("x"), lax.axis_index("y"), lax.axis_index("c")


def _other_chips(x, y):
    return [(1 - x, y), (x, 1 - y), (1 - x, 1 - y)]


BIG = (("w_in", (D_MODEL, IN_COLS), 1), ("w_rnn_out", (D_MODEL, D_MODEL), 0), ("w_na_out", (D_MODEL, D_MODEL), 0),
       ("w_out", (D_MODEL, D_MODEL), 0), ("w_up", (D_MODEL, 2 * D_FF), 1), ("w_down", (D_FF, D_MODEL), 0))


def _shard_shape(full, axis):
    r, c = full
    return (r // N_SHARD, c) if axis == 0 else (r, c // N_SHARD)


def _slot(ref, full, axis, s, h):
    r, c = full
    if axis == 0:
        rs = r // N_SHARD
        return ref.at[pl.ds(s * rs + h * (rs // 2), rs // 2), :]
    cs = c // N_SHARD
    return ref.at[pl.ds(h * (r // 2), r // 2), pl.ds(s * cs, cs)]


def cast_into_full(x, full, axis, idx, name):
    r, c = x.shape
    tr = next(t for t in (512, 352, 256, 128) if r % t == 0)
    nb = r // tr

    def body(idx_ref, x_ref, o_ref):
        o_ref[...] = x_ref[...].astype(BF16)

    if axis == 0:
        out_spec = pl.BlockSpec((tr, c), lambda i, idx_ref: (idx_ref[0] * nb + i, 0))
    else:
        out_spec = pl.BlockSpec((tr, c), lambda i, idx_ref: (i, idx_ref[0]))
    return pl.pallas_call(
        body, name=name,
        grid_spec=pltpu.PrefetchScalarGridSpec(
            num_scalar_prefetch=1, grid=(nb,), in_specs=[pl.BlockSpec((tr, c), lambda i, idx_ref: (i, 0))],
            out_specs=out_spec),
        out_shape=jax.ShapeDtypeStruct(full, BF16),
        compiler_params=_params("parallel"),
    )(idx, x)


def run_comm(comm, name):
    k_in, k_out = len(comm.inputs), len(comm.out_shapes)

    def body(*refs):
        start, mid, end = comm.emit(refs[:k_in], refs[k_in:k_in + k_out], refs[k_in + k_out:])
        start()
        mid()
        end()

    hbm = pl.BlockSpec(memory_space=pl.ANY)
    return pl.pallas_call(
        body, name=name, in_specs=[hbm] * k_in, out_specs=[hbm] * k_out, out_shape=list(comm.out_shapes),
        input_output_aliases=dict(comm.aliases), scratch_shapes=list(comm.scratch),
        compiler_params=pltpu.CompilerParams(vmem_limit_bytes=VMEM_LIMIT_V7X),
    )(*comm.inputs)


def gather_weights_comm(fulls, which):
    nw = len(which)
    specs = [BIG[w] for w in which]

    def emit(_, outs, sems):
        send1, recv1, send2, recv2 = sems
        x, y, c = _mesh_pos()
        sibling = (x, y, 1 - c)
        chips = _other_chips(x, y)
        s_me = 2 * x + y
        shards = [2 * chip[0] + chip[1] for chip in chips]

        def ici(w, j, shard):
            _, full, axis = specs[w]
            dst = _slot(outs[w], full, axis, shard, c)
            return pltpu.make_async_remote_copy(
                src_ref=dst, dst_ref=dst, send_sem=send1.at[3 * w + j],
                recv_sem=recv1.at[3 * w + j], device_id=(*chips[j], c), device_id_type=MESH_T)

        def d2d(w, j, shard, half):
            _, full, axis = specs[w]
            dst = _slot(outs[w], full, axis, shard, half)
            return pltpu.make_async_remote_copy(
                src_ref=dst, dst_ref=dst, send_sem=send2.at[3 * w + j], recv_sem=recv2.at[3 * w + j],
                device_id=sibling, device_id_type=MESH_T)

        pairs = [(w, j) for w in range(nw) for j in range(3)]

        def start():
            for w, j in pairs:
                ici(w, j, s_me).start()

        def mid():
            for w, j in pairs:
                ici(w, j, shards[j]).wait_recv()
                d2d(w, j, shards[j], c).start()

        def end():
            for w, j in pairs:
                d2d(w, j, shards[j], 1 - c).wait_recv()
            for w, j in pairs:
                ici(w, j, s_me).wait_send()
                d2d(w, j, shards[j], c).wait_send()

        return start, mid, end

    return Comm(list(fulls), [jax.ShapeDtypeStruct(full, BF16) for _, full, _ in specs], {i: i for i in range(nw)},
                [pltpu.SemaphoreType.DMA((3 * nw,))] * 4, emit)


def join_comms(a, b):
    ai, ao, asc = len(a.inputs), len(a.out_shapes), len(a.scratch)

    def emit(ins, outs, sems):
        fa = a.emit(ins[:ai], outs[:ao], sems[:asc])
        fb = b.emit(ins[ai:], outs[ao:], sems[asc:])

        def both(k):
            def run():
                fa[k]()
                fb[k]()
            return run

        return both(0), both(1), both(2)

    aliases = dict(a.aliases)
    aliases.update({ai + i: ao + o for i, o in b.aliases.items()})
    return Comm(a.inputs + b.inputs, a.out_shapes + b.out_shapes, aliases, a.scratch + b.scratch, emit)


def all_gather_comm(x):
    def emit(srcs, outs, sems):
        send_sems, recv_sems, local_sem = sems
        x_ref, out_ref = srcs[0], outs[0]
        x, y, c = _mesh_pos()
        me, sibling = (x, y, c), (x, y, 1 - c)
        chips = _other_chips(x, y)

        def blk(px, py, pc):
            return out_ref.at[4 * px + 2 * py + pc]

        def copy(k, block, to, src=None):
            return pltpu.make_async_remote_copy(
                src_ref=blk(*block) if src is None else src, dst_ref=blk(*block),
                send_sem=send_sems.at[k], recv_sem=recv_sems.at[k], device_id=to, device_id_type=MESH_T)

        def mine():
            return pltpu.make_async_copy(x_ref, blk(*me), local_sem)

        def start():
            mine().start()
            copy(0, me, sibling, src=x_ref).start()
            for j, chip in enumerate(chips):
                copy(1 + j, me, (*chip, c), src=x_ref).start()

        def mid():
            for j, chip in enumerate(chips):
                copy(1 + j, (*chip, c), me).wait_recv()
                copy(4 + j, (*chip, c), sibling).start()

        def end():
            copy(0, sibling, me).wait_recv()
            for j, chip in enumerate(chips):
                copy(4 + j, (*chip, 1 - c), me).wait_recv()
            copy(0, me, sibling, src=x_ref).wait_send()
            for j, chip in enumerate(chips):
                copy(1 + j, me, (*chip, c), src=x_ref).wait_send()
                copy(4 + j, (*chip, c), sibling).wait_send()
            mine().wait()

        return start, mid, end

    return Comm([x], [jax.ShapeDtypeStruct((N_DEV,) + x.shape, F32)], {},
                [pltpu.SemaphoreType.DMA((7,)), pltpu.SemaphoreType.DMA((7,)), pltpu.SemaphoreType.DMA], emit)


def sum_blocks(g, name):
    _, r, c = g.shape
    tr = 256 if r % 256 == 0 else r

    def body(g_ref, o_ref):
        acc = g_ref[0]
        for k in range(1, N_DEV):
            acc = acc + g_ref[k]
        o_ref[...] = acc

    return pl.pallas_call(
        body, name=name, grid=(r // tr,),
        in_specs=[pl.BlockSpec((N_DEV, tr, c), lambda i: (0, i, 0))],
        out_specs=pl.BlockSpec((tr, c), lambda i: (i, 0)),
        out_shape=jax.ShapeDtypeStruct((r, c), F32),
        compiler_params=_params("parallel"),
    )(g)


def _grad_view(g, full, axis):
    r, c = full
    if axis == 0:
        return g.reshape(N_SHARD, 2, r // N_SHARD // 2, c)
    return g.reshape(1, 2, r // 2, c)


def exchange_halves_comm(gviews):
    nw = len(gviews)

    def emit(srcs, outs, sems):
        send_sems, recv_sems = sems
        x, y, c = _mesh_pos()

        def copies():
            return [pltpu.make_async_remote_copy(
                src_ref=srcs[w].at[:, pl.ds(1 - c, 1)], dst_ref=outs[w], send_sem=send_sems.at[w],
                recv_sem=recv_sems.at[w], device_id=(x, y, 1 - c), device_id_type=MESH_T) for w in range(nw)]

        def start():
            for cp in copies():
                cp.start()

        def end():
            for cp in copies():
                cp.wait()

        return start, lambda: None, end

    return Comm(list(gviews), [jax.ShapeDtypeStruct((g.shape[0], 1) + g.shape[2:], BF16) for g in gviews], {},
                [pltpu.SemaphoreType.DMA((nw,)), pltpu.SemaphoreType.DMA((nw,))], emit)


def _row_tile(rh):
    return 128 if rh % 128 == 0 else rh


def add_halves(gview, recv, c_idx, name):
    a, _, rh, cc = gview.shape
    tr = _row_tile(rh)

    def body(c_ref, g_ref, r_ref, o_ref):
        o_ref[0] = (g_ref[0, 0].astype(F32) + r_ref[0, 0].astype(F32)).astype(BF16)

    return pl.pallas_call(
        body, name=name,
        grid_spec=pltpu.PrefetchScalarGridSpec(
            num_scalar_prefetch=1, grid=(a, rh // tr),
            in_specs=[pl.BlockSpec((1, 1, tr, cc), lambda s, i, c_ref: (s, c_ref[0], i, 0)),
                      pl.BlockSpec((1, 1, tr, cc), lambda s, i, c_ref: (s, 0, i, 0))],
            out_specs=pl.BlockSpec((1, tr, cc), lambda s, i, c_ref: (s, i, 0))),
        out_shape=jax.ShapeDtypeStruct((a, rh, cc), BF16),
        compiler_params=_params("parallel", "parallel"),
    )(c_idx, gview, recv)


def _piece_shape(full, axis):
    rs, cs = _shard_shape(full, axis)
    return (rs // 2, cs)


def scatter_pieces_comm(partials, which):
    nw = len(which)
    specs = [BIG[w] for w in which]

    def emit(srcs, outs, sems):
        send_sems, recv_sems = sems
        x, y, c = _mesh_pos()
        chips = _other_chips(x, y)

        def copies():
            cps = []
            for w, (_, full, axis) in enumerate(specs):
                cs = full[1] // N_SHARD
                for j, chip in enumerate(chips):
                    s_j = 2 * chip[0] + chip[1]
                    src = srcs[w].at[s_j] if axis == 0 else srcs[w].at[0, :, pl.ds(s_j * cs, cs)]
                    cps.append(pltpu.make_async_remote_copy(
                        src_ref=src, dst_ref=outs[w].at[j], send_sem=send_sems.at[3 * w + j],
                        recv_sem=recv_sems.at[3 * w + j], device_id=(*chip, c), device_id_type=MESH_T))
            return cps

        def start():
            for cp in copies():
                cp.start()

        def mid():
            pass

        def end():
            for cp in copies():
                cp.wait()

        return start, mid, end

    return Comm(list(partials), [jax.ShapeDtypeStruct((3,) + _piece_shape(full, axis), BF16) for _, full, axis in specs],
                {}, [pltpu.SemaphoreType.DMA((3 * nw,)), pltpu.SemaphoreType.DMA((3 * nw,))], emit)


def add_pieces(partial, recv, idx, axis, name):
    _, rh, cs = recv.shape
    tr = _row_tile(rh)

    def body(idx_ref, p_ref, r_ref, o_ref):
        o_ref[0] = ((p_ref[0].astype(F32) + r_ref[0].astype(F32)) + r_ref[1].astype(F32)) + r_ref[2].astype(F32)

    if axis == 0:
        pspec = pl.BlockSpec((1, tr, cs), lambda i, idx_ref: (idx_ref[0], i, 0))
    else:
        pspec = pl.BlockSpec((1, tr, cs), lambda i, idx_ref: (0, i, idx_ref[0]))
    return pl.pallas_call(
        body, name=name,
        grid_spec=pltpu.PrefetchScalarGridSpec(
            num_scalar_prefetch=1, grid=(rh // tr,),
            in_specs=[pspec, pl.BlockSpec((3, tr, cs), lambda i, idx_ref: (0, i, 0))],
            out_specs=pl.BlockSpec((1, tr, cs), lambda i, idx_ref: (idx_ref[1], i, 0))),
        out_shape=jax.ShapeDtypeStruct((2, rh, cs), F32),
        compiler_params=_params("parallel"),
    )(idx, partial, recv)


def join_halves_comm(halves):
    nw = len(halves)

    def emit(_, outs, sems):
        send_sems, recv_sems = sems
        x, y, c = _mesh_pos()

        def copy(w, half):
            return pltpu.make_async_remote_copy(
                src_ref=outs[w].at[half], dst_ref=outs[w].at[half], send_sem=send_sems.at[w], recv_sem=recv_sems.at[w],
                device_id=(x, y, 1 - c), device_id_type=MESH_T)

        def start():
            for w in range(nw):
                copy(w, c).start()

        def end():
            for w in range(nw):
                copy(w, c).wait_send()
                copy(w, 1 - c).wait_recv()

        return start, lambda: None, end

    return Comm(list(halves), [jax.ShapeDtypeStruct(h.shape, F32) for h in halves], {i: i for i in range(nw)},
                [pltpu.SemaphoreType.DMA((nw,))] * 2, emit)


MOD_COLS = N_MOD * D_MODEL // N_SHARD
MOD_TILE = 512


def mod_fwd(c16, w_mod):
    def body(c_ref, w_ref, s_ref, o_ref):
        cv = c_ref[...]
        s = cv * _sigmoid(cv)
        s_ref[...] = s
        o_ref[...] = jnp.dot(s.astype(BF16), w_ref[...].astype(BF16), preferred_element_type=F32)

    return pl.pallas_call(
        body, name="mod_fwd", grid=(MOD_COLS // MOD_TILE,),
        in_specs=[_full((16, D_MODEL)), pl.BlockSpec((D_MODEL, MOD_TILE), lambda j: (0, j))],
        out_specs=[_full((16, D_MODEL)), pl.BlockSpec((16, MOD_TILE), lambda j: (0, j))],
        out_shape=[jax.ShapeDtypeStruct((16, D_MODEL), F32), jax.ShapeDtypeStruct((16, MOD_COLS), F32)],
        compiler_params=_params("arbitrary"),
    )(c16, w_mod)


def mod_bwd(s16, dm16, w_mod):
    hi = lax.Precision.HIGHEST

    def body(s_ref, d_ref, w_ref, gw_ref, ds_ref):
        j = pl.program_id(0)
        dm = d_ref[...]
        gw_ref[...] = lax.dot_general(s_ref[...], dm, (((0,), (0,)), ((), ())), preferred_element_type=F32, precision=hi)
        part = lax.dot_general(dm, w_ref[...], (((1,), (1,)), ((), ())), preferred_element_type=F32, precision=hi)

        @pl.when(j == 0)
        def _():
            ds_ref[...] = part

        @pl.when(j > 0)
        def _():
            ds_ref[...] = ds_ref[...] + part

    return pl.pallas_call(
        body, name="mod_bwd", grid=(MOD_COLS // MOD_TILE,),
        in_specs=[_full((16, D_MODEL)), pl.BlockSpec((16, MOD_TILE), lambda j: (0, j)),
                  pl.BlockSpec((D_MODEL, MOD_TILE), lambda j: (0, j))],
        out_specs=[pl.BlockSpec((D_MODEL, MOD_TILE), lambda j: (0, j)), _full((16, D_MODEL))],
        out_shape=[jax.ShapeDtypeStruct((D_MODEL, MOD_COLS), F32), jax.ShapeDtypeStruct((16, D_MODEL), F32)],
        compiler_params=_params("arbitrary"),
    )(s16, dm16, w_mod)


def cctx_grad(parts, c_ctx):
    def body(p_ref, c_ref, o_ref):
        ds = p_ref[0:1, :]
        for s in range(1, N_SHARD):
            ds = ds + p_ref[16 * s:16 * s + 1, :]
        cv = c_ref[...]
        sg = _sigmoid(cv)
        o_ref[...] = ds * (sg * (1.0 + cv * (1.0 - sg)))

    return pl.pallas_call(
        body, name="cctx_grad", in_specs=[_full((N_DEV * 8, D_MODEL)), _full((1, D_MODEL))],
        out_specs=_full((1, D_MODEL)), out_shape=jax.ShapeDtypeStruct((1, D_MODEL), F32),
    )(parts, c_ctx)


def add_rows(a, b, name):
    def body(a_ref, b_ref, o_ref):
        o_ref[...] = a_ref[...] + b_ref[...]

    return pl.pallas_call(body, name=name, in_specs=[_full(a.shape), _full(b.shape)], out_specs=_full(a.shape),
                          out_shape=jax.ShapeDtypeStruct(a.shape, F32))(a, b)


def _adamw_update(w_ref, g_ref, m_ref, v_ref, d_ref, nm_ref, nv_ref):
    g_ = g_ref[...]
    m_ = ADAM_B1 * m_ref[...] + (1.0 - ADAM_B1) * g_
    v_ = ADAM_B2 * v_ref[...] + (1.0 - ADAM_B2) * (g_ * g_)
    m_hat = m_ / (1.0 - ADAM_B1 ** ADAM_STEP)
    v_hat = v_ / (1.0 - ADAM_B2 ** ADAM_STEP)
    d_ref[...] = -ADAM_LR * (m_hat / (jnp.sqrt(v_hat) + ADAM_EPS) + ADAM_WD * w_ref[...])
    nm_ref[...] = m_
    nv_ref[...] = v_


def adamw_many(ws, gs, ms, vs):
    n = len(ws)

    def body(*refs):
        for i in range(n):
            _adamw_update(*[refs[k * n + i] for k in range(7)])

    shapes = [jax.ShapeDtypeStruct(w.shape, F32) for w in ws]
    return pl.pallas_call(body, name="adamw_small", out_shape=shapes * 3,
                          compiler_params=pltpu.CompilerParams(vmem_limit_bytes=VMEM_LIMIT_V7X))(*ws, *gs, *ms, *vs)


def adamw(w, g, m, v, name, comm=None):
    r, c = w.shape
    tr = 128 if (r % 128 == 0 and r > 128) else r

    def body(w_ref, g_ref, m_ref, v_ref, d_ref, nm_ref, nv_ref):
        _adamw_update(w_ref, g_ref, m_ref, v_ref, d_ref, nm_ref, nv_ref)

    spec = pl.BlockSpec((tr, c), lambda i: (i, 0))
    shp = jax.ShapeDtypeStruct((r, c), F32)
    res, extra = _call(body, name=name, grid=(r // tr,), in_specs=[spec] * 4, out_specs=[spec] * 3,
                       out_shape=[shp] * 3, sem=("parallel",), args=(w, g, m, v), comm=comm)
    return (*res, extra)


LANES = 1024


def _pack(arrs):
    rows, spans, at = [], [], 0
    for a in arrs:
        n = int(np.prod(a.shape))
        nr = 8 * -(-n // (8 * LANES))
        flat = a.reshape(-1)
        if nr * LANES != n:
            flat = jnp.concatenate([flat, jnp.zeros((nr * LANES - n,), F32)])
        rows.append(flat.reshape(nr, LANES))
        spans.append((at, nr, n, a.shape))
        at += nr
    return jnp.concatenate(rows, axis=0), spans


def _unpack(buf, spans):
    out = []
    for at, nr, n, shape in spans:
        out.append(buf[at:at + nr].reshape(-1)[:n].reshape(shape))
    return out


SMALL_SHARD = ("lru_conv_w", "lru_ba", "lru_bx", "lru_lambda", "ffn_conv_w")


def kernel(x, c, ctx, c_ctx, w_mod, b_mod, norm_mix_g, norm_ffn_g, w_in, lru_conv_w, lru_conv_b, lru_wa, lru_ba, lru_wx, lru_bx, lru_lambda, q_norm_g, k_norm_g, na_rpb, w_rnn_out, w_na_out, w_out, w_up, ffn_conv_w, ffn_conv_b, w_down, loss_target, m_c_ctx, m_w_mod, m_b_mod, m_norm_mix_g, m_norm_ffn_g, m_w_in, m_lru_conv_w, m_lru_conv_b, m_lru_wa, m_lru_ba, m_lru_wx, m_lru_bx, m_lru_lambda, m_q_norm_g, m_k_norm_g, m_na_rpb, m_w_rnn_out, m_w_na_out, m_w_out, m_w_up, m_ffn_conv_w, m_ffn_conv_b, m_w_down, v_c_ctx, v_w_mod, v_b_mod, v_norm_mix_g, v_norm_ffn_g, v_w_in, v_lru_conv_w, v_lru_conv_b, v_lru_wa, v_lru_ba, v_lru_wx, v_lru_bx, v_lru_lambda, v_q_norm_g, v_k_norm_g, v_na_rpb, v_w_rnn_out, v_w_na_out, v_w_out, v_w_up, v_ffn_conv_w, v_ffn_conv_b, v_w_down):
    weights = dict(c_ctx=c_ctx, w_mod=w_mod, b_mod=b_mod, norm_mix_g=norm_mix_g, norm_ffn_g=norm_ffn_g, w_in=w_in,
                   lru_conv_w=lru_conv_w, lru_conv_b=lru_conv_b, lru_wa=lru_wa, lru_ba=lru_ba, lru_wx=lru_wx,
                   lru_bx=lru_bx, lru_lambda=lru_lambda, q_norm_g=q_norm_g, k_norm_g=k_norm_g, na_rpb=na_rpb,
                   w_rnn_out=w_rnn_out, w_na_out=w_na_out, w_out=w_out, w_up=w_up, ffn_conv_w=ffn_conv_w,
                   ffn_conv_b=ffn_conv_b, w_down=w_down)
    mom1 = dict(c_ctx=m_c_ctx, w_mod=m_w_mod, b_mod=m_b_mod, norm_mix_g=m_norm_mix_g, norm_ffn_g=m_norm_ffn_g,
                w_in=m_w_in, lru_conv_w=m_lru_conv_w, lru_conv_b=m_lru_conv_b, lru_wa=m_lru_wa, lru_ba=m_lru_ba,
                lru_wx=m_lru_wx, lru_bx=m_lru_bx, lru_lambda=m_lru_lambda, q_norm_g=m_q_norm_g, k_norm_g=m_k_norm_g,
                na_rpb=m_na_rpb, w_rnn_out=m_w_rnn_out, w_na_out=m_w_na_out, w_out=m_w_out, w_up=m_w_up,
                ffn_conv_w=m_ffn_conv_w, ffn_conv_b=m_ffn_conv_b, w_down=m_w_down)
    mom2 = dict(c_ctx=v_c_ctx, w_mod=v_w_mod, b_mod=v_b_mod, norm_mix_g=v_norm_mix_g, norm_ffn_g=v_norm_ffn_g,
                w_in=v_w_in, lru_conv_w=v_lru_conv_w, lru_conv_b=v_lru_conv_b, lru_wa=v_lru_wa, lru_ba=v_lru_ba,
                lru_wx=v_lru_wx, lru_bx=v_lru_bx, lru_lambda=v_lru_lambda, q_norm_g=v_q_norm_g, k_norm_g=v_k_norm_g,
                na_rpb=v_na_rpb, w_rnn_out=v_w_rnn_out, w_na_out=v_w_na_out, w_out=v_w_out, w_up=v_w_up,
                ffn_conv_w=v_ffn_conv_w, ffn_conv_b=v_ffn_conv_b, w_down=v_w_down)
    order = list(weights)
    d = D_MODEL
    mx_, my_, mc_ = _mesh_pos()
    shard = 2 * mx_ + my_
    dev = 2 * shard + mc_

    idx = jnp.stack([shard, mc_]).astype(jnp.int32)
    wsh = {name: cast_into_full(weights[name][0], full, axis, idx, "cast_" + name) for name, full, axis in BIG}
    local_small, small_spans = _pack([c] + [weights[k][0] for k in SMALL_SHARD])
    bt, (w_in_full, gath) = bias_table(na_rpb[0], comm=join_comms(gather_weights_comm([wsh["w_in"]], [0]),
                                                                  all_gather_comm(local_small)))
    per_dev = [_unpack(gath[k], small_spans) for k in range(N_DEV)]
    c_all = jnp.concatenate([per_dev[k][0] for k in range(N_DEV)], axis=0)
    full_small = {name: jnp.concatenate([per_dev[2 * s][1 + i] for s in range(N_SHARD)], axis=-1)
                  for i, name in enumerate(SMALL_SHARD)}
    c16 = jnp.concatenate([c_all, c_ctx.reshape(1, d), jnp.zeros((7, d), F32)], axis=0)
    s16, mod_part = mod_fwd(c16, w_mod[0])
    mod_all = run_comm(all_gather_comm(mod_part), "gather_mod")[0]
    mod = jnp.concatenate([mod_all[2 * s] for s in range(N_SHARD)], axis=1) + b_mod
    modx = lax.dynamic_slice(mod, (dev, 0), (1, N_MOD * d))
    modc = mod[8:9]

    z = jnp.concatenate([ctx[0], x[0]], axis=0)
    res = local_step(z, loss_target[0], modx, modc, norm_mix_g, norm_ffn_g, w_in_full, full_small["lru_conv_w"],
                     lru_conv_b, lru_wa[0], full_small["lru_ba"], lru_wx[0], full_small["lru_bx"],
                     full_small["lru_lambda"], q_norm_g, k_norm_g, na_rpb[0], wsh["w_rnn_out"], wsh["w_na_out"],
                     wsh["w_out"], wsh["w_up"], full_small["ffn_conv_w"], ffn_conv_b, wsh["w_down"], idx=idx, bt=bt)

    halves = [add_pieces(res["partials"][i], res["pieces"][i], idx, BIG[i][2], "add_pieces_" + BIG[i][0])
              for i in range(len(BIG))]
    lru_tot = {k: sum_blocks(res["lru_w_all"][k], "sum_" + k).reshape(weights[k].shape[1:])
               for k in ("lru_wa", "lru_wx")}
    small_names = ["norm_mix_g", "norm_ffn_g", "lru_conv_w", "lru_conv_b", "lru_ba", "lru_bx",
                   "lru_lambda", "q_norm_g", "k_norm_g", "na_rpb", "ffn_conv_w", "ffn_conv_b"]
    local_g, g_spans = _pack([res["loss_sq"][0:1, 0:1], res["d_modx"], res["d_modc"]] + [res[k] for k in small_names])
    n_rows = local_g.shape[0]
    *joined, g_all = run_comm(join_comms(join_halves_comm(halves), all_gather_comm(local_g)), "tail_exchange")
    grads = {name: joined[i].reshape(_shard_shape(full, axis)) for i, (name, full, axis) in enumerate(BIG)}
    grads.update(lru_tot)
    g_tot = sum_blocks(g_all, "sum_small")
    tot = _unpack(g_tot, g_spans)
    loss = (0.5 / d) * tot[0][0, 0]
    small_tot = dict(zip(small_names, tot[3:]))
    at_x = g_spans[1][0]
    dmx_rows = g_all.reshape(N_DEV, n_rows, LANES)[:, at_x:at_x + N_MOD, :].reshape(N_DEV, N_MOD * d)
    dmc_row = jnp.concatenate([tot[2], jnp.zeros((1, 4 * d), F32)], axis=1)
    dm16 = jnp.concatenate([dmx_rows, dmc_row, jnp.zeros((7, N_MOD * d), F32)], axis=0)
    grads["b_mod"] = add_rows(tot[1], dmc_row, "b_mod_grad")
    g_w_mod, ds16 = mod_bwd(s16, lax.dynamic_slice(dm16, (0, shard * MOD_COLS), (16, MOD_COLS)), w_mod[0])
    grads["w_mod"] = g_w_mod
    for k in small_names:
        g = small_tot[k]
        if k in SMALL_SHARD:
            w_sh = weights[k].shape[-1]
            g = lax.dynamic_slice_in_dim(g, shard * w_sh, w_sh, axis=g.ndim - 1)
        grads[k] = g

    delta, new_m, new_v = {}, {}, {}
    for name, _, _ in BIG + (("w_mod", None, None),):
        *upd, got = adamw(weights[name][0], grads[name], mom1[name][0], mom2[name][0], "adamw_" + name,
                          comm=all_gather_comm(ds16[8:16]) if name == "w_in" else None)
        delta[name], new_m[name], new_v[name] = upd
        if name == "w_in":
            grads["c_ctx"] = cctx_grad(got[0].reshape(N_DEV * 8, d), c_ctx.reshape(1, d))
    rest = [k for k in order if k not in delta]
    views = {k: (grads[k].shape if grads[k].ndim <= 3 else (-1, grads[k].shape[-1])) for k in rest}
    small = adamw_many(*[[t[k].reshape(views[k]) for k in rest] for t in (weights, grads, mom1, mom2)])
    n_rest = len(rest)
    for i, k in enumerate(rest):
        delta[k], new_m[k], new_v[k] = small[i], small[n_rest + i], small[2 * n_rest + i]

    shaped = lambda t: [t[k].reshape(weights[k].shape) for k in order]
    return (loss, res["grad_x"][None], *shaped(grads), *shaped(delta), *shaped(new_m), *shaped(new_v))
```

```python
import numpy as np
import jax
import jax.numpy as jnp
from jax import lax
from jax.experimental import pallas as pl
from jax.experimental.pallas import tpu as pltpu

F32 = jnp.float32
BF16 = jnp.bfloat16

D_MODEL = 1024
SEQ = 2048
CTX_LEN = 256
ZLEN = SEQ + CTX_LEN
GRID_W = 64
GRID_ROWS = SEQ // GRID_W
LRU_BLOCK_W = 128
LRU_BLOCKS = 8
LRU_C = 8.0
NA_HEADS = 16
HEAD_DIM = 64
NA_ROWS = 8
NA_COLS = 16
ROPE_BASE = 10000.0
D_FF = 2816
N_MOD = 6
IN_COLS = 7 * D_MODEL
EPS = 1e-6
NEG_INF = -1e30
N_DEV = 8
N_SHARD = 4

ADAM_LR = 0.001
ADAM_B1 = 0.9
ADAM_B2 = 0.999
ADAM_EPS = 1e-08
ADAM_WD = 0.01
ADAM_STEP = 10

ROW_TILE = 256
Q_ROWS = 4
Q_TILE = Q_ROWS * GRID_W
KEY_ROWS = 12
KEY_TILE = KEY_ROWS * GRID_W
BT_PAD = 4
BT_LEN = 24
VMEM_LIMIT_V7X = 56 * 1024 * 1024

MESH_T = pl.DeviceIdType.MESH


def _params(*sem):
    return pltpu.CompilerParams(dimension_semantics=sem if sem else None, vmem_limit_bytes=VMEM_LIMIT_V7X)


def _full(shape):
    nd = len(shape)
    return pl.BlockSpec(shape, lambda *_: (0,) * nd)


class Comm:
    def __init__(self, inputs, out_shapes, aliases, scratch, emit):
        self.inputs, self.out_shapes, self.aliases, self.scratch, self.emit = inputs, out_shapes, aliases, scratch, emit


def _call(body, *, name, grid, in_specs, out_specs, out_shape, args, scratch_shapes=(), sem=(), comm=None):
    n_in, n_out, n_sc = len(in_specs), len(out_specs), len(scratch_shapes)
    if comm is None:
        res = pl.pallas_call(body, name=name, grid=grid, in_specs=list(in_specs), out_specs=list(out_specs),
                             out_shape=list(out_shape), scratch_shapes=list(scratch_shapes),
                             compiler_params=_params(*sem))(*args)
        return list(res), []
    k_in, k_out = len(comm.inputs), len(comm.out_shapes)
    steps = int(np.prod(grid))

    def hosted(*refs):
        ins, cins = refs[:n_in], refs[n_in:n_in + k_in]
        at = n_in + k_in
        outs, couts = refs[at:at + n_out], refs[at + n_out:at + n_out + k_out]
        at += n_out + k_out
        scr, cscr = refs[at:at + n_sc], refs[at + n_sc:]
        start, mid, end = comm.emit(cins, couts, cscr)
        lin = pl.program_id(0)
        for ax in range(1, len(grid)):
            lin = lin * grid[ax] + pl.program_id(ax)
        pl.when(lin == 0)(start)
        body(*ins, *outs, *scr)
        pl.when(lin == steps - 1 - steps // 7)(mid)
        pl.when(lin == steps - 1)(end)

    hbm = pl.BlockSpec(memory_space=pl.ANY)
    res = pl.pallas_call(
        hosted, name=name, grid=grid, in_specs=list(in_specs) + [hbm] * k_in, out_specs=list(out_specs) + [hbm] * k_out,
        out_shape=list(out_shape) + list(comm.out_shapes), scratch_shapes=list(scratch_shapes) + list(comm.scratch),
        input_output_aliases={n_in + i: n_out + o for i, o in comm.aliases.items()},
        compiler_params=_params(*(("arbitrary",) * len(grid))))(*args, *comm.inputs)
    return list(res[:n_out]), list(res[n_out:])


def _sigmoid(x):
    return 0.5 * jnp.tanh(0.5 * x) + 0.5


def _gelu_parts(x):
    c0 = 0.7978845608028654
    inner = c0 * (x + 0.044715 * x * x * x)
    t = jnp.tanh(inner)
    g = 0.5 * x * (1.0 + t)
    dg = 0.5 * (1.0 + t) + 0.5 * x * (1.0 - t * t) * c0 * (1.0 + 3.0 * 0.044715 * x * x)
    return g, dg


def _dot_nt(a, b):
    return lax.dot_general(a, b, (((1,), (1,)), ((), ())), preferred_element_type=F32)


def _dot_tn(a, b):
    return lax.dot_general(a, b, (((0,), (0,)), ((), ())), preferred_element_type=F32)


def norm_mod(xin, gain, shift, scale, name):
    r, d = xin.shape
    s_mod = shift.shape[0]
    assert r % ROW_TILE == 0

    def body(x_ref, g_ref, sh_ref, sc_ref, xn_ref):
        x = x_ref[...]
        nrm = x * lax.rsqrt(jnp.mean(x * x, axis=-1, keepdims=True) + EPS)
        xn_ref[...] = ((nrm * g_ref[...]) * (1.0 + sc_ref[0]) + sh_ref[0]).astype(BF16)

    mod_spec = pl.BlockSpec((1, 1, d), lambda i: (jnp.minimum(i, s_mod - 1), 0, 0))
    return pl.pallas_call(
        body, name=name, grid=(r // ROW_TILE,),
        in_specs=[pl.BlockSpec((ROW_TILE, d), lambda i: (i, 0)), _full((1, d)), mod_spec, mod_spec],
        out_specs=pl.BlockSpec((ROW_TILE, d), lambda i: (i, 0)),
        out_shape=jax.ShapeDtypeStruct((r, d), BF16),
        compiler_params=_params("parallel"),
    )(xin, gain, shift, scale)


def matmul_wide(a, b, name, tm, tn, comm=None):
    m, k = a.shape
    n = b.shape[1]
    assert m % tm == 0 and n % tn == 0

    def body(a_ref, b_ref, o_ref):
        o_ref[...] = jnp.dot(a_ref[...], b_ref[...], preferred_element_type=F32)

    res, extra = _call(
        body, name=name, grid=(n // tn, m // tm),
        in_specs=[pl.BlockSpec((tm, k), lambda j, i: (i, 0)), pl.BlockSpec((k, tn), lambda j, i: (0, j))],
        out_specs=[pl.BlockSpec((tm, tn), lambda j, i: (i, j))],
        out_shape=[jax.ShapeDtypeStruct((m, n), F32)],
        sem=("parallel", "parallel"), args=(a, b), comm=comm)
    return res[0], extra


def _row_ids(n, w):
    return lax.broadcasted_iota(jnp.int32, (n, w), 0)


def _lru_conv(xr, cw, cb):
    row = _row_ids(ZLEN, LRU_BLOCK_W)
    segpos = jnp.where(row < CTX_LEN, row, row - CTX_LEN)
    seglen = jnp.where(row < CTX_LEN, CTX_LEN, SEQ)
    acc = xr * cw[2:3, :] + cb
    for k in (0, 1, 3):
        off = k - 2
        sh = pltpu.roll(xr, (-off) % ZLEN, 0)
        ok = (segpos + off >= 0) & (segpos + off < seglen)
        acc = acc + jnp.where(ok, sh, 0.0) * cw[k:k + 1, :]
    return acc


def _lru_conv_t(dxc, cw):
    row = _row_ids(ZLEN, LRU_BLOCK_W)
    segpos = jnp.where(row < CTX_LEN, row, row - CTX_LEN)
    seglen = jnp.where(row < CTX_LEN, CTX_LEN, SEQ)
    acc = dxc * cw[2:3, :]
    for k in (0, 1, 3):
        off = k - 2
        sh = pltpu.roll(dxc, off % ZLEN, 0)
        ok = (segpos - off >= 0) & (segpos - off < seglen)
        acc = acc + jnp.where(ok, sh, 0.0) * cw[k:k + 1, :]
    return acc


def _lru_gates(xc, xcb, wa, ba, wx, bx, lam):
    r = _sigmoid(jnp.dot(xcb, wa, preferred_element_type=F32) + ba)
    i = _sigmoid(jnp.dot(xcb, wx, preferred_element_type=F32) + bx)
    sp = jnp.maximum(-lam, 0.0) + jnp.log1p(jnp.exp(-jnp.abs(lam)))
    la = (-LRU_C) * r * sp
    a = jnp.exp(la)
    sq = jnp.sqrt(-jnp.tanh(la) * (1.0 + a * a))
    b = sq * i * xc
    return r, i, sp, a, sq, b


def _scan8_fwd(a, b, rid):
    for s in (1, 2, 4):
        a_s = pltpu.roll(a, s, 0)
        b_s = pltpu.roll(b, s, 0)
        m = rid >= s
        b = jnp.where(m, a * b_s + b, b)
        a = jnp.where(m, a * a_s, a)
    return a, b


def _scan8_rev(a, b, rid):
    for s in (1, 2, 4):
        a_s = pltpu.roll(a, 8 - s, 0)
        b_s = pltpu.roll(b, 8 - s, 0)
        m = rid < 8 - s
        b = jnp.where(m, a * b_s + b, b)
        a = jnp.where(m, a * a_s, a)
    return a, b


N_CHUNK = ZLEN // 8
CTX_CHUNKS = CTX_LEN // 8
SCAN_UNROLL = 8


def _scan_up(a_ref, b_ref, h_ref, lo, hi, carry):
    rid = _row_ids(8, LRU_BLOCK_W)
    assert (hi - lo) % SCAN_UNROLL == 0

    def step(g, c):
        base = pl.multiple_of((lo + g * SCAN_UNROLL) * 8, 8)
        for u in range(SCAN_UNROLL):
            sl = pl.ds(base + 8 * u, 8)
            a, b = _scan8_fwd(a_ref[sl, :], b_ref[sl, :], rid)
            h_ref[sl, :] = b + a * c
            c = b[7:8, :] + a[7:8, :] * c
        return c

    return lax.fori_loop(0, (hi - lo) // SCAN_UNROLL, step, carry)


def _scan_down(a_ref, b_ref, h_ref, lo, hi, carry):
    rid = _row_ids(8, LRU_BLOCK_W)
    assert (hi - lo) % SCAN_UNROLL == 0

    def step(g, c):
        base = pl.multiple_of((hi - (g + 1) * SCAN_UNROLL) * 8, 8)
        for u in reversed(range(SCAN_UNROLL)):
            sl = pl.ds(base + 8 * u, 8)
            a, b = _scan8_rev(a_ref[sl, :], b_ref[sl, :], rid)
            h_ref[sl, :] = b + a * c
            c = b[0:1, :] + a[0:1, :] * c
        return c

    return lax.fori_loop(0, (hi - lo) // SCAN_UNROLL, step, carry)


def _lru_scan_dir(d, a_ref, b_ref, h_ref):
    zero = jnp.zeros((1, LRU_BLOCK_W), F32)
    if d == 0:
        _scan_up(a_ref, b_ref, h_ref, 0, N_CHUNK, zero)
    else:
        c = _scan_down(a_ref, b_ref, h_ref, 0, CTX_CHUNKS, zero)
        _scan_down(a_ref, b_ref, h_ref, CTX_CHUNKS, N_CHUNK, c)


def _lru_in_specs():
    blk = lambda rows: pl.BlockSpec((rows, LRU_BLOCK_W), lambda b: (0, b))
    wspec = pl.BlockSpec((2, 1, LRU_BLOCK_W, LRU_BLOCK_W), lambda b: (0, b, 0, 0))
    return blk, wspec


def lru_fwd(p, conv_w, conv_b, wa, ba, wx, bx, lam, comm=None):
    blk, wspec = _lru_in_specs()

    def body(xr_ref, gx_ref, cw_ref, cb_ref, wa_ref, ba_ref, wx_ref, bx_ref, lam_ref, y_ref, a_s, b_s, h_s, hsum_s):
        xr = xr_ref[...]
        xc = _lru_conv(xr, cw_ref[...], cb_ref[...])
        xcb = xc.astype(BF16)
        for d in (0, 1):
            _, _, _, a, _, b = _lru_gates(xc, xcb, wa_ref[d, 0].astype(BF16), ba_ref[d:d + 1, :],
                                          wx_ref[d, 0].astype(BF16), bx_ref[d:d + 1, :], lam_ref[d:d + 1, :])
            a_s[...] = a
            b_s[...] = b
            _lru_scan_dir(d, a_s, b_s, h_s)
            if d == 0:
                hsum_s[...] = h_s[...]
            else:
                hsum_s[...] = hsum_s[...] + h_s[...]
        g, _ = _gelu_parts(gx_ref[CTX_LEN:, :])
        y_ref[...] = (hsum_s[CTX_LEN:, :] * g).astype(BF16)

    zs = pltpu.VMEM((ZLEN, LRU_BLOCK_W), F32)
    res, extra = _call(
        body, name="lru_fwd", grid=(LRU_BLOCKS,),
        in_specs=[blk(ZLEN), pl.BlockSpec((ZLEN, LRU_BLOCK_W), lambda b: (0, 24 + b)), blk(4), blk(1),
                  wspec, blk(2), wspec, blk(2), blk(2)],
        out_specs=[pl.BlockSpec((SEQ, LRU_BLOCK_W), lambda b: (0, b))],
        out_shape=[jax.ShapeDtypeStruct((SEQ, D_MODEL), BF16)],
        scratch_shapes=[zs, zs, zs, zs], sem=("arbitrary",),
        args=(p, p, conv_w, conv_b, wa, ba, wx, bx, lam), comm=comm)
    return res[0], extra


def _rope_tables():
    t = np.arange(SEQ)
    lane = np.arange(2 * HEAD_DIM)
    in_head = lane % HEAD_DIM
    j = (in_head % 32) % 16
    freq = ROPE_BASE ** (-j.astype(np.float64) / 16.0)
    pos = np.where(in_head[None, :] < 32, (t // GRID_W)[:, None], (t % GRID_W)[:, None]).astype(np.float64)
    ang = (pos.astype(np.float32) * freq.astype(np.float32)[None, :]).astype(np.float32)
    cos = np.cos(ang).astype(np.float32)
    sin = np.sin(ang).astype(np.float32)
    sgn = np.where((in_head % 32) < 16, -1.0, 1.0).astype(np.float32)
    cos = np.concatenate([np.ones((CTX_LEN, 2 * HEAD_DIM), np.float32), cos], 0)
    sin = np.concatenate([np.zeros((CTX_LEN, 2 * HEAD_DIM), np.float32), sin * sgn[None, :]], 0)
    return jnp.asarray(cos), jnp.asarray(sin)


def _head_ones():
    lane = np.arange(2 * HEAD_DIM)
    return jnp.asarray((lane[:, None] // HEAD_DIM == lane[None, :] // HEAD_DIM).astype(np.float32))


def _rope_partner(x):
    lane = lax.broadcasted_iota(jnp.int32, x.shape, 1)
    return jnp.where((lane % 32) < 16, pltpu.roll(x, 128 - 16, 1), pltpu.roll(x, 16, 1))


def _head_sum(t, ones):
    hi = t.astype(BF16)
    lo = (t - hi.astype(F32)).astype(BF16)
    ones_b = ones.astype(BF16)
    return jnp.dot(hi, ones_b, preferred_element_type=F32) + jnp.dot(lo, ones_b, preferred_element_type=F32)


def _head_rms(x, ones, gain):
    ms = _head_sum(x * x, ones) * (1.0 / HEAD_DIM)
    rstd = lax.rsqrt(ms + EPS)
    return x * rstd * gain, rstd


PREP_TILE = 768


def qkv_prep(p, qg2, kg2, cos, sin, ones, comm=None):
    scale = HEAD_DIM ** -0.5

    def body(q_ref, k_ref, v_ref, qg_ref, kg_ref, cos_ref, sin_ref, ones_ref, qr_ref, qp_ref, kk_ref, vv_ref):
        ones_m = ones_ref[...]
        c, s = cos_ref[...], sin_ref[...]
        qn, _ = _head_rms(q_ref[...], ones_m, qg_ref[...])
        qn = qn * scale
        qr_ref[...] = (qn * c + _rope_partner(qn) * s).astype(BF16)
        qp_ref[...] = qn.astype(BF16)
        kn, _ = _head_rms(k_ref[...], ones_m, kg_ref[...])
        kk_ref[...] = (kn * c + _rope_partner(kn) * s).astype(BF16)
        vv_ref[...] = v_ref[...].astype(BF16)

    col = lambda base: pl.BlockSpec((PREP_TILE, 128), lambda hp, i: (i, base + hp))
    small = pl.BlockSpec((1, 128), lambda hp, i: (0, 0))
    tab = pl.BlockSpec((PREP_TILE, 128), lambda hp, i: (i, 0))
    oshape = jax.ShapeDtypeStruct((ZLEN, D_MODEL), BF16)
    res, extra = _call(
        body, name="qkv_prep", grid=(NA_HEADS // 2, ZLEN // PREP_TILE),
        in_specs=[col(32), col(8), col(16), small, small, tab, tab, _full((128, 128))],
        out_specs=[col(0)] * 4, out_shape=[oshape] * 4, sem=("parallel", "parallel"),
        args=(p, p, p, qg2, kg2, cos, sin, ones), comm=comm)
    return (*res, extra)


def _bias_expand():
    qc = np.arange(GRID_W)[:, None]
    kc = np.arange(GRID_W)[None, :]
    col_start = np.clip(qc - NA_COLS // 2, 0, GRID_W - NA_COLS)
    in_win = (kc >= col_start) & (kc < col_start + NA_COLS)
    dc = np.clip(kc - qc, -(NA_COLS - 1), NA_COLS - 1) + (NA_COLS - 1)
    e = np.zeros((2 * NA_COLS - 1, GRID_W, GRID_W), np.float32)
    for d in range(2 * NA_COLS - 1):
        e[d] = ((dc == d) & in_win).astype(np.float32)
    pen = np.where(in_win, 0.0, NEG_INF).astype(np.float32)
    return e, pen


def bias_table(rpb2, comm=None):
    e, pen = _bias_expand()
    n_dr = 2 * NA_ROWS - 1
    ea = np.zeros((31, GRID_W, 128), np.float32)
    ea[:, :, :GRID_W] = e
    eb = np.zeros((31, GRID_W, 128), np.float32)
    eb[:, :, GRID_W:] = e
    pen2 = np.concatenate([pen, pen], 1)
    ea = jnp.asarray(ea.reshape(31, GRID_W * 128))
    eb = jnp.asarray(eb.reshape(31, GRID_W * 128))
    sel_a = np.zeros((BT_LEN, n_dr), np.float32)
    sel_b = np.zeros((BT_LEN, n_dr), np.float32)
    for r in range(BT_LEN):
        dr = r - BT_PAD
        if 0 <= dr < n_dr:
            sel_a[r, dr] = 1.0
        if 0 <= dr + 1 < n_dr:
            sel_b[r, dr + 1] = 1.0
    sel_a, sel_b = jnp.asarray(sel_a), jnp.asarray(sel_b)
    pen2 = jnp.asarray(pen2.reshape(1, GRID_W * 128))
    hi = lax.Precision.HIGHEST

    def body(rpb_ref, sa_ref, sb_ref, ea_ref, eb_ref, pen_ref, o_ref, ra_s, rb_s):
        for h in range(NA_HEADS):
            rp = rpb_ref[h]
            ra_s[h * BT_LEN:(h + 1) * BT_LEN, :] = jnp.dot(sa_ref[...], rp, preferred_element_type=F32, precision=hi)
            rb_s[h * BT_LEN:(h + 1) * BT_LEN, :] = jnp.dot(sb_ref[...], rp, preferred_element_type=F32, precision=hi)
        o_ref[...] = (jnp.dot(ra_s[...], ea_ref[...], preferred_element_type=F32, precision=hi)
                      + jnp.dot(rb_s[...], eb_ref[...], preferred_element_type=F32, precision=hi) + pen_ref[...])

    tcol = 2048
    rows = NA_HEADS * BT_LEN
    res, extra = _call(
        body, name="bias_table", grid=(GRID_W * 128 // tcol,),
        in_specs=[_full((NA_HEADS, n_dr, 31)), _full((BT_LEN, n_dr)), _full((BT_LEN, n_dr)),
                  pl.BlockSpec((31, tcol), lambda j: (0, j)), pl.BlockSpec((31, tcol), lambda j: (0, j)),
                  pl.BlockSpec((1, tcol), lambda j: (0, j))],
        out_specs=[pl.BlockSpec((rows, tcol), lambda j: (0, j))],
        out_shape=[jax.ShapeDtypeStruct((rows, GRID_W * 128), F32)],
        scratch_shapes=[pltpu.VMEM((rows, 31), F32), pltpu.VMEM((rows, 31), F32)], sem=("parallel",),
        args=(rpb2, sel_a, sel_b, ea, eb, pen2), comm=comm)
    return res[0].reshape(NA_HEADS, BT_LEN, GRID_W, 128), extra


def _key_window(j):
    ws = jnp.clip(Q_ROWS * j - 4, 0, GRID_ROWS - KEY_ROWS)
    return ws, pl.multiple_of(CTX_LEN + ws * GRID_W, 256)


def _head_mask(hh):
    lane = lax.broadcasted_iota(jnp.int32, (Q_TILE, 128), 1)
    return (lane < HEAD_DIM) if hh == 0 else (lane >= HEAD_DIM)


def _attn_scores(j, ws, q_rot_h, q_pl_h, kw, kc, hh, bt_ref, s_ref):
    s_ref[:, :KEY_TILE] = _dot_nt(q_rot_h, kw)
    s_ref[:, KEY_TILE:] = _dot_nt(q_pl_h, kc)
    lane = lax.broadcasted_iota(jnp.int32, (GRID_W, 128), 1)
    base = ws - Q_ROWS * j + (NA_ROWS - 1) + BT_PAD
    for qi in range(Q_ROWS):
        rs = jnp.clip(Q_ROWS * j + qi - NA_ROWS // 2, 0, GRID_ROWS - NA_ROWS)
        for m in range(KEY_ROWS // 2):
            k0 = ws + 2 * m
            p0 = jnp.where((k0 >= rs) & (k0 < rs + NA_ROWS), 0.0, NEG_INF)
            p1 = jnp.where((k0 + 1 >= rs) & (k0 + 1 < rs + NA_ROWS), 0.0, NEG_INF)
            pen = jnp.where(lane < GRID_W, p0, p1)
            rows = slice(qi * GRID_W, (qi + 1) * GRID_W)
            cols = slice(128 * m, 128 * (m + 1))
            s_ref[rows, cols] = s_ref[rows, cols] + bt_ref[hh, base + 2 * m - qi] + pen
    return base


def attn_fwd(q_rot, q_pl, kk, vv, bt, comm=None):
    def body(qr_ref, qp_ref, kk_ref, vv_ref, bt_ref, o_ref, lse_ref, s_ref):
        j = pl.program_id(1)
        ws, start = _key_window(j)
        win = pl.ds(start, KEY_TILE)
        kw, kc = kk_ref[win, :], kk_ref[:CTX_LEN, :]
        vw, vc = vv_ref[win, :], vv_ref[:CTX_LEN, :]
        qr, qp = qr_ref[...], qp_ref[...]
        outs = []
        for hh in range(2):
            msk = _head_mask(hh)
            _attn_scores(j, ws, jnp.where(msk, qr, 0), jnp.where(msk, qp, 0), kw, kc, hh, bt_ref, s_ref)
            s = s_ref[...]
            mx = jnp.max(s, axis=-1, keepdims=True)
            pr = jnp.exp(s - mx)
            l = jnp.sum(pr, axis=-1, keepdims=True)
            prb = pr.astype(BF16)
            o = jnp.dot(prb[:, :KEY_TILE], vw, preferred_element_type=F32)
            o = o + jnp.dot(prb[:, KEY_TILE:], vc, preferred_element_type=F32)
            outs.append(o / l)
            lse_ref[hh] = mx + jnp.log(l)
        o_ref[...] = jnp.where(_head_mask(0), outs[0], outs[1])

    qspec = pl.BlockSpec((Q_TILE, 128), lambda hp, j: (j + 1, hp))
    kspec = pl.BlockSpec((ZLEN, 128), lambda hp, j: (0, hp))
    res, extra = _call(
        body, name="attn_fwd", grid=(NA_HEADS // 2, SEQ // Q_TILE),
        in_specs=[qspec, qspec, kspec, kspec, pl.BlockSpec((2, BT_LEN, GRID_W, 128), lambda hp, j: (hp, 0, 0, 0))],
        out_specs=[pl.BlockSpec((Q_TILE, 128), lambda hp, j: (j, hp)),
                   pl.BlockSpec((2, Q_TILE, 1), lambda hp, j: (hp, j, 0))],
        out_shape=[jax.ShapeDtypeStruct((SEQ, D_MODEL), F32), jax.ShapeDtypeStruct((NA_HEADS, SEQ, 1), F32)],
        scratch_shapes=[pltpu.VMEM((Q_TILE, KEY_TILE + CTX_LEN), F32)], sem=("parallel", "arbitrary"),
        args=(q_rot, q_pl, kk, vv, bt), comm=comm)
    return res[0], res[1], extra


def merge_fwd(y_rnn, y_na, p, z, g2, w_rnn, w_na, w_out):
    def body(yr_ref, yn_ref, mr_ref, mn_ref, x_ref, g2_ref, wr_ref, wn_ref, wo_ref, u_ref, v_ref, mg_ref, out_ref, x1_ref):
        u = jnp.dot(yr_ref[...], wr_ref[...], preferred_element_type=F32)
        v = jnp.dot(yn_ref[...].astype(BF16), wn_ref[...], preferred_element_type=F32)
        merged = (_sigmoid(mr_ref[...]) * u + _sigmoid(mn_ref[...]) * v).astype(BF16)
        out = jnp.dot(merged, wo_ref[...], preferred_element_type=F32)
        u_ref[...] = u
        v_ref[...] = v
        mg_ref[...] = merged
        out_ref[...] = out
        x1_ref[...] = x_ref[...] + g2_ref[...] * out

    row = pl.BlockSpec((ROW_TILE, D_MODEL), lambda i: (i, 0))
    lat = lambda cb: pl.BlockSpec((ROW_TILE, D_MODEL), lambda i: (i + 1, cb))
    wspec = _full((D_MODEL, D_MODEL))
    f32o = jax.ShapeDtypeStruct((SEQ, D_MODEL), F32)
    return pl.pallas_call(
        body, name="merge_fwd", grid=(SEQ // ROW_TILE,),
        in_specs=[row, row, lat(5), lat(6), lat(0), _full((1, D_MODEL)), wspec, wspec, wspec],
        out_specs=[row] * 5,
        out_shape=[f32o, f32o, jax.ShapeDtypeStruct((SEQ, D_MODEL), BF16), f32o, f32o],
        compiler_params=_params("parallel"),
    )(y_rnn, y_na, p, p, z, g2, w_rnn, w_na, w_out)


FF_TILE = 256
FF_TILES = D_FF // FF_TILE


FF_ROWS = 64
FF_HALO = 8
FF_SLAB = FF_ROWS + 2 * FF_HALO


def _ffn_row_chunks(chunk, init):
    carry = chunk(0, 0, -1, init)
    carry = lax.fori_loop(1, SEQ // FF_ROWS - 1,
                          lambda ci, cr: chunk(pl.multiple_of(ci * FF_ROWS - FF_HALO, 8), FF_HALO, 0, cr), carry)
    return chunk(SEQ - FF_SLAB, 2 * FF_HALO, 1, carry)


def _ffn_shifts(edge):
    row = _row_ids(FF_SLAB, FF_TILE)

    def prev(x):
        r = pltpu.roll(x, 1, 0)
        return jnp.where(row >= 1, r, 0.0) if edge == -1 else r

    def nxt(x):
        r = pltpu.roll(x, FF_SLAB - 1, 0)
        return jnp.where(row < FF_SLAB - 1, r, 0.0) if edge == 1 else r

    return prev, nxt


def ffn_act(hpre, conv_w, conv_b):
    def body(ha_ref, hg_ref, wa_ref, wg_ref, ba_ref, bg_ref, o_ref):
        wa, wg, ba, bg = wa_ref[...], wg_ref[...], ba_ref[...], bg_ref[...]

        def chunk(lo, mid, edge, carry):
            prev, nxt = _ffn_shifts(edge)
            ha, hg = ha_ref[pl.ds(lo, FF_SLAB), :], hg_ref[pl.ds(lo, FF_SLAB), :]
            a = prev(ha) * wa[0:1] + ha * wa[1:2] + nxt(ha) * wa[2:3] + ba
            g = prev(hg) * wg[0:1] + hg * wg[1:2] + nxt(hg) * wg[2:3] + bg
            o_ref[pl.ds(lo + mid, FF_ROWS), :] = (a * _sigmoid(a) * g)[mid:mid + FF_ROWS].astype(BF16)
            return carry

        _ffn_row_chunks(chunk, 0)

    col = lambda rows, off: pl.BlockSpec((rows, FF_TILE), lambda j: (0, j + off))
    return pl.pallas_call(
        body, name="ffn_act", grid=(FF_TILES,),
        in_specs=[col(SEQ, 0), col(SEQ, FF_TILES), col(3, 0), col(3, FF_TILES), col(1, 0), col(1, FF_TILES)],
        out_specs=col(SEQ, 0),
        out_shape=jax.ShapeDtypeStruct((SEQ, D_FF), BF16),
        compiler_params=_params("parallel"),
    )(hpre, hpre, conv_w, conv_w, conv_b, conv_b)


def ffn_down_loss(act, w_down, x1, g5, target):
    def body(a_ref, w_ref, x1_ref, g5_ref, t_ref, f_ref, dy_ref, df_ref, ls_ref, dg_ref):
        i = pl.program_id(0)
        f = jnp.dot(a_ref[...], w_ref[...], preferred_element_type=F32)
        g5 = g5_ref[...]
        err = x1_ref[...] + g5 * f - t_ref[...]
        dy = err * (1.0 / D_MODEL)
        f_ref[...] = f
        dy_ref[...] = dy
        df_ref[...] = (dy * g5).astype(BF16)

        @pl.when(i == 0)
        def _():
            ls_ref[...] = jnp.zeros_like(ls_ref)
            dg_ref[...] = jnp.zeros_like(dg_ref)

        ls_ref[...] = ls_ref[...] + jnp.sum(err * err)
        dg_ref[...] = dg_ref[...] + jnp.sum(dy * f, axis=0, keepdims=True)

    row = pl.BlockSpec((ROW_TILE, D_MODEL), lambda i: (i, 0))
    f32o = jax.ShapeDtypeStruct((SEQ, D_MODEL), F32)
    return pl.pallas_call(
        body, name="ffn_down_loss", grid=(SEQ // ROW_TILE,),
        in_specs=[pl.BlockSpec((ROW_TILE, D_FF), lambda i: (i, 0)), _full((D_FF, D_MODEL)), row, _full((1, D_MODEL)), row],
        out_specs=[row, row, row, _full((8, 128)), _full((1, D_MODEL))],
        out_shape=[f32o, f32o, jax.ShapeDtypeStruct((SEQ, D_MODEL), BF16), jax.ShapeDtypeStruct((8, 128), F32),
                   jax.ShapeDtypeStruct((1, D_MODEL), F32)],
        compiler_params=_params("arbitrary"),
    )(act, w_down, x1, g5, target)


def ffn_down_bwd(df, w_down):
    def body(df_ref, w_ref, o_ref):
        o_ref[...] = _dot_nt(df_ref[...], w_ref[...])

    return pl.pallas_call(
        body, name="ffn_down_bwd", grid=(SEQ // ROW_TILE,),
        in_specs=[pl.BlockSpec((ROW_TILE, D_MODEL), lambda i: (i, 0)), _full((D_FF, D_MODEL))],
        out_specs=pl.BlockSpec((ROW_TILE, D_FF), lambda i: (i, 0)),
        out_shape=jax.ShapeDtypeStruct((SEQ, D_FF), F32),
        compiler_params=_params("parallel"),
    )(df, w_down)


def ffn_act_bwd(hpre, d_act, conv_w, conv_b):
    def body(ha_ref, hg_ref, da_ref, wa_ref, wg_ref, ba_ref, bg_ref, dha_ref, dhg_ref, dwa_ref, dwg_ref, dba_ref, dbg_ref):
        wa, wg, ba, bg = wa_ref[...], wg_ref[...], ba_ref[...], bg_ref[...]

        def chunk(lo, mid, edge, acc):
            prev, nxt = _ffn_shifts(edge)
            rows = pl.ds(lo, FF_SLAB)
            ha, hg, dact = ha_ref[rows, :], hg_ref[rows, :], da_ref[rows, :]
            hap, han, hgp, hgn = prev(ha), nxt(ha), prev(hg), nxt(hg)
            a = hap * wa[0:1] + ha * wa[1:2] + han * wa[2:3] + ba
            g = hgp * wg[0:1] + hg * wg[1:2] + hgn * wg[2:3] + bg
            sig = _sigmoid(a)
            dca = dact * g * (sig * (1.0 + a * (1.0 - sig)))
            dcg = dact * a * sig
            m = slice(mid, mid + FF_ROWS)
            sums = []
            for dc, h, hp, hn, w, dh_ref in ((dca, ha, hap, han, wa, dha_ref), (dcg, hg, hgp, hgn, wg, dhg_ref)):
                dcm = dc[m]
                sums += [jnp.sum(dcm * hp[m], axis=0, keepdims=True), jnp.sum(dcm * h[m], axis=0, keepdims=True),
                         jnp.sum(dcm * hn[m], axis=0, keepdims=True), jnp.sum(dcm, axis=0, keepdims=True)]
                dh = nxt(dc) * w[0:1] + dc * w[1:2] + prev(dc) * w[2:3]
                dh_ref[pl.ds(lo + mid, FF_ROWS), :] = dh[m].astype(BF16)
            return tuple(x + y for x, y in zip(acc, sums))

        acc = _ffn_row_chunks(chunk, tuple(jnp.zeros((1, FF_TILE), F32) for _ in range(8)))
        dwa_ref[0:1, :], dwa_ref[1:2, :], dwa_ref[2:3, :], dba_ref[...] = acc[0], acc[1], acc[2], acc[3]
        dwg_ref[0:1, :], dwg_ref[1:2, :], dwg_ref[2:3, :], dbg_ref[...] = acc[4], acc[5], acc[6], acc[7]

    col = lambda rows, off: pl.BlockSpec((rows, FF_TILE), lambda j: (0, j + off))
    hshape = jax.ShapeDtypeStruct((SEQ, D_FF), BF16)
    wshape = jax.ShapeDtypeStruct((3, D_FF), F32)
    bshape = jax.ShapeDtypeStruct((1, D_FF), F32)
    return pl.pallas_call(
        body, name="ffn_act_bwd", grid=(FF_TILES,),
        in_specs=[col(SEQ, 0), col(SEQ, FF_TILES), col(SEQ, 0), col(3, 0), col(3, FF_TILES), col(1, 0), col(1, FF_TILES)],
        out_specs=[col(SEQ, 0), col(SEQ, 0), col(3, 0), col(3, 0), col(1, 0), col(1, 0)],
        out_shape=[hshape, hshape, wshape, wshape, bshape, bshape],
        compiler_params=_params("parallel"),
    )(hpre, hpre, d_act, conv_w, conv_w, conv_b, conv_b)


def _norm_mod_bwd(x, dxn, gain, scale):
    rstd = lax.rsqrt(jnp.mean(x * x, axis=-1, keepdims=True) + EPS)
    nrm = x * rstd
    dsh = jnp.sum(dxn, axis=0, keepdims=True)
    dsc = jnp.sum(dxn * nrm, axis=0, keepdims=True) * gain
    dgn = jnp.sum(dxn * nrm, axis=0, keepdims=True) * (1.0 + scale)
    dn = dxn * (gain * (1.0 + scale))
    dx = rstd * (dn - nrm * jnp.mean(dn * nrm, axis=-1, keepdims=True))
    return dx, dsh, dsc, dgn


def ffn_up_bwd(dha, dhg, w_up, x1, dy, gain, scale):
    def body(dha_ref, dhg_ref, w_ref, x_ref, dy_ref, g_ref, sc_ref, dx_ref, dsh_ref, dsc_ref, dgn_ref):
        i = pl.program_id(0)
        dxn = _dot_nt(dha_ref[...], w_ref[:, :D_FF]) + _dot_nt(dhg_ref[...], w_ref[:, D_FF:])
        dx, dsh, dsc, dgn = _norm_mod_bwd(x_ref[...], dxn, g_ref[...], sc_ref[...])
        dx_ref[...] = dy_ref[...] + dx

        @pl.when(i == 0)
        def _():
            dsh_ref[...] = dsh
            dsc_ref[...] = dsc
            dgn_ref[...] = dgn

        @pl.when(i > 0)
        def _():
            dsh_ref[...] = dsh_ref[...] + dsh
            dsc_ref[...] = dsc_ref[...] + dsc
            dgn_ref[...] = dgn_ref[...] + dgn

    row = pl.BlockSpec((ROW_TILE, D_MODEL), lambda i: (i, 0))
    vec = _full((1, D_MODEL))
    vshape = jax.ShapeDtypeStruct((1, D_MODEL), F32)
    return pl.pallas_call(
        body, name="ffn_up_bwd", grid=(SEQ // ROW_TILE,),
        in_specs=[pl.BlockSpec((ROW_TILE, D_FF), lambda i: (i, 0)), pl.BlockSpec((ROW_TILE, D_FF), lambda i: (i, 0)),
                  _full((D_MODEL, 2 * D_FF)), row, row, vec, vec],
        out_specs=[row, vec, vec, vec],
        out_shape=[jax.ShapeDtypeStruct((SEQ, D_MODEL), F32), vshape, vshape, vshape],
        compiler_params=_params("arbitrary"),
    )(dha, dhg, w_up, x1, dy, gain, scale)


def merge_bwd(dx1, out, g2, p, u, v, w_rnn, w_na, w_out, comm=None):
    def body(dx_ref, out_ref, g2_ref, mr_ref, mn_ref, u_ref, v_ref, wr_ref, wn_ref, wo_ref,
             dout_ref, du_ref, dv_ref, dmr_ref, dmn_ref, dyr_ref, dyn_ref, dg2_ref):
        i = pl.program_id(0)

        @pl.when(i == 0)
        def _():
            dmr_ref[...] = jnp.zeros_like(dmr_ref)
            dmn_ref[...] = jnp.zeros_like(dmn_ref)
            dg2_ref[...] = jnp.zeros_like(dg2_ref)

        @pl.when(i > 0)
        def _():
            dx = dx_ref[...]
            dg2_ref[...] = dg2_ref[...] + jnp.sum(dx * out_ref[...], axis=0, keepdims=True)
            dout = (dx * g2_ref[...]).astype(BF16)
            dout_ref[...] = dout
            dm = _dot_nt(dout, wo_ref[...])
            sr = _sigmoid(mr_ref[...])
            sn = _sigmoid(mn_ref[...])
            du = (dm * sr).astype(BF16)
            dv = (dm * sn).astype(BF16)
            du_ref[...] = du
            dv_ref[...] = dv
            dmr_ref[...] = (dm * u_ref[...] * (sr * (1.0 - sr))).astype(BF16)
            dmn_ref[...] = (dm * v_ref[...] * (sn * (1.0 - sn))).astype(BF16)
            dyr_ref[...] = _dot_nt(du, wr_ref[...])
            dyn_ref[...] = _dot_nt(dv, wn_ref[...])

    lat = pl.BlockSpec((ROW_TILE, D_MODEL), lambda i: (jnp.maximum(i - 1, 0), 0))
    zrow = pl.BlockSpec((ROW_TILE, D_MODEL), lambda i: (i, 0))
    pcol = lambda cb: pl.BlockSpec((ROW_TILE, D_MODEL), lambda i: (i, cb))
    wspec = _full((D_MODEL, D_MODEL))
    tb = jax.ShapeDtypeStruct((SEQ, D_MODEL), BF16)
    zb = jax.ShapeDtypeStruct((ZLEN, D_MODEL), BF16)
    tf = jax.ShapeDtypeStruct((SEQ, D_MODEL), F32)
    res, extra = _call(
        body, name="merge_bwd", grid=(ZLEN // ROW_TILE,),
        in_specs=[lat, lat, _full((1, D_MODEL)), pcol(5), pcol(6), lat, lat, wspec, wspec, wspec],
        out_specs=[lat, lat, lat, zrow, zrow, lat, lat, _full((1, D_MODEL))],
        out_shape=[tb, tb, tb, zb, zb, tf, tf, jax.ShapeDtypeStruct((1, D_MODEL), F32)],
        sem=("arbitrary",), args=(dx1, out, g2, p, p, u, v, w_rnn, w_na, w_out), comm=comm)
    return (*res, extra)


def attn_bwd(q_rot, q_pl, kk, vv, bt, y_na, d_yna, lse, comm=None):
    def body(qr_ref, qp_ref, kk_ref, vv_ref, bt_ref, o_ref, do_ref, lse_ref,
             dqr_ref, dqp_ref, dk_ref, dv_ref, dbt_ref, s_ref):
        jj = pl.program_id(1)

        @pl.when(jj == 0)
        def _():
            dqr_ref[...] = jnp.zeros_like(dqr_ref)
            dqp_ref[...] = jnp.zeros_like(dqp_ref)
            dk_ref[...] = jnp.zeros_like(dk_ref)
            dv_ref[...] = jnp.zeros_like(dv_ref)
            dbt_ref[...] = jnp.zeros_like(dbt_ref)

        @pl.when(jj > 0)
        def _():
            j = jj - 1
            ws, start = _key_window(j)
            win = pl.ds(start, KEY_TILE)
            kw, kc = kk_ref[win, :], kk_ref[:CTX_LEN, :]
            vw, vc = vv_ref[win, :], vv_ref[:CTX_LEN, :]
            qr, qp = qr_ref[...], qp_ref[...]
            do = do_ref[...]
            do_o = do * o_ref[...]
            dq_r, dq_p = [], []
            for hh in range(2):
                msk = _head_mask(hh)
                q_r, q_p = jnp.where(msk, qr, 0), jnp.where(msk, qp, 0)
                base = _attn_scores(j, ws, q_r, q_p, kw, kc, hh, bt_ref, s_ref)
                pr = jnp.exp(s_ref[...] - lse_ref[hh])
                delta = jnp.sum(jnp.where(msk, do_o, 0.0), axis=-1, keepdims=True)
                dob = jnp.where(msk, do, 0.0).astype(BF16)
                ds_lat = pr[:, :KEY_TILE] * (_dot_nt(dob, vw) - delta)
                ds_ctx = pr[:, KEY_TILE:] * (_dot_nt(dob, vc) - delta)
                for qi in range(Q_ROWS):
                    for m in range(KEY_ROWS // 2):
                        idx = base + 2 * m - qi
                        dbt_ref[hh, idx] = dbt_ref[hh, idx] + ds_lat[qi * GRID_W:(qi + 1) * GRID_W, 128 * m:128 * (m + 1)]
                dsb_lat = ds_lat.astype(BF16)
                dsb_ctx = ds_ctx.astype(BF16)
                prb = pr.astype(BF16)
                dq_r.append(jnp.dot(dsb_lat, kw, preferred_element_type=F32))
                dq_p.append(jnp.dot(dsb_ctx, kc, preferred_element_type=F32))
                dk_ref[win, :] = dk_ref[win, :] + _dot_tn(dsb_lat, q_r)
                dk_ref[:CTX_LEN, :] = dk_ref[:CTX_LEN, :] + _dot_tn(dsb_ctx, q_p)
                dv_ref[win, :] = dv_ref[win, :] + _dot_tn(prb[:, :KEY_TILE], dob)
                dv_ref[:CTX_LEN, :] = dv_ref[:CTX_LEN, :] + _dot_tn(prb[:, KEY_TILE:], dob)
            dqr_ref[...] = jnp.where(_head_mask(0), dq_r[0], dq_r[1])
            dqp_ref[...] = jnp.where(_head_mask(0), dq_p[0], dq_p[1])

    lat = lambda jj: jnp.maximum(jj - 1, 0)
    qspec = pl.BlockSpec((Q_TILE, 128), lambda hp, jj: (lat(jj) + 1, hp))
    kspec = pl.BlockSpec((ZLEN, 128), lambda hp, jj: (0, hp))
    btspec = pl.BlockSpec((2, BT_LEN, GRID_W, 128), lambda hp, jj: (hp, 0, 0, 0))
    ospec = pl.BlockSpec((Q_TILE, 128), lambda hp, jj: (lat(jj), hp))
    dqspec = pl.BlockSpec((Q_TILE, 128), lambda hp, jj: (jj, hp))
    zshape = jax.ShapeDtypeStruct((ZLEN, D_MODEL), F32)
    res, extra = _call(
        body, name="attn_bwd", grid=(NA_HEADS // 2, ZLEN // Q_TILE),
        in_specs=[qspec, qspec, kspec, kspec, btspec, ospec, ospec,
                  pl.BlockSpec((2, Q_TILE, 1), lambda hp, jj: (hp, lat(jj), 0))],
        out_specs=[dqspec, dqspec, kspec, kspec, btspec],
        out_shape=[zshape, zshape, zshape, zshape, jax.ShapeDtypeStruct((NA_HEADS, BT_LEN, GRID_W, 128), F32)],
        scratch_shapes=[pltpu.VMEM((Q_TILE, KEY_TILE + CTX_LEN), F32)], sem=("parallel", "arbitrary"),
        args=(q_rot, q_pl, kk, vv, bt, y_na, d_yna, lse), comm=comm)
    return (*res, extra)


def qkv_bwd(dq_rot, dq_pl, dk, dv, p, qg2, kg2, cos, sin, ones, comm=None):
    scale = HEAD_DIM ** -0.5
    n_hp, n_i = NA_HEADS // 2, ZLEN // PREP_TILE

    def norm_rope_bwd(d_rot, d_extra, x, gain, cos_t, sin_t, ones_m, dx_ref, acc_ref):
        xh, rstd = _head_rms(x, ones_m, 1.0)
        dn = d_rot * cos_t + _rope_partner(d_rot * sin_t)
        if d_extra is not None:
            dn = (dn + d_extra) * scale
        acc_ref[...] = acc_ref[...] + jnp.sum(dn * xh, axis=0, keepdims=True)
        dxh = dn * gain
        seg = _head_sum(dxh * xh, ones_m) * (1.0 / HEAD_DIM)
        dx_ref[...] = (rstd * (dxh - xh * seg)).astype(BF16)

    def body(dqr_ref, dqp_ref, dk_ref, dv_ref, xq_ref, xk_ref, qg_ref, kg_ref, cos_ref, sin_ref, ones_ref,
             dxq_ref, dxk_ref, dxv_ref, dgq_ref, dgk_ref, accq_ref, acck_ref):
        hp, i = pl.program_id(0), pl.program_id(1)

        @pl.when((hp == 0) & (i == 0))
        def _():
            accq_ref[...] = jnp.zeros_like(accq_ref)
            acck_ref[...] = jnp.zeros_like(acck_ref)

        ones_m = ones_ref[...]
        cos_t, sin_t = cos_ref[...], sin_ref[...]
        norm_rope_bwd(dqr_ref[...], dqp_ref[...], xq_ref[...], qg_ref[...], cos_t, sin_t, ones_m, dxq_ref, accq_ref)
        norm_rope_bwd(dk_ref[...], None, xk_ref[...], kg_ref[...], cos_t, sin_t, ones_m, dxk_ref, acck_ref)
        dxv_ref[...] = dv_ref[...].astype(BF16)

        @pl.when((hp == n_hp - 1) & (i == n_i - 1))
        def _():
            dgq_ref[...] = accq_ref[:, :HEAD_DIM] + accq_ref[:, HEAD_DIM:]
            dgk_ref[...] = acck_ref[:, :HEAD_DIM] + acck_ref[:, HEAD_DIM:]

    col = lambda base: pl.BlockSpec((PREP_TILE, 128), lambda hp, i: (i, base + hp))
    small = pl.BlockSpec((1, 128), lambda hp, i: (0, 0))
    tab = pl.BlockSpec((PREP_TILE, 128), lambda hp, i: (i, 0))
    zb = jax.ShapeDtypeStruct((ZLEN, D_MODEL), BF16)
    gshape = jax.ShapeDtypeStruct((1, HEAD_DIM), F32)
    res, extra = _call(
        body, name="qkv_bwd", grid=(n_hp, n_i),
        in_specs=[col(0)] * 4 + [col(32), col(8), small, small, tab, tab, _full((128, 128))],
        out_specs=[col(0)] * 3 + [_full((1, HEAD_DIM))] * 2,
        out_shape=[zb, zb, zb, gshape, gshape],
        scratch_shapes=[pltpu.VMEM((1, 128), F32)] * 2, sem=("arbitrary", "arbitrary"),
        args=(dq_rot, dq_pl, dk, dv, p, p, qg2, kg2, cos, sin, ones), comm=comm)
    return (*res, extra)


def rpb_grad(dbt, comm=None):
    e, _ = _bias_expand()
    n_dr = 2 * NA_ROWS - 1
    ea = np.zeros((31, GRID_W, 128), np.float32)
    ea[:, :, :GRID_W] = e
    eb = np.zeros((31, GRID_W, 128), np.float32)
    eb[:, :, GRID_W:] = e
    eat = jnp.asarray(ea.reshape(31, GRID_W * 128).T.copy())
    ebt = jnp.asarray(eb.reshape(31, GRID_W * 128).T.copy())
    sel_at = np.zeros((n_dr, BT_LEN), np.float32)
    sel_bt = np.zeros((n_dr, BT_LEN), np.float32)
    for r in range(BT_LEN):
        dr = r - BT_PAD
        if 0 <= dr < n_dr:
            sel_at[dr, r] = 1.0
        if 0 <= dr + 1 < n_dr:
            sel_bt[dr + 1, r] = 1.0
    hi = lax.Precision.HIGHEST

    tk = 2048
    wide = GRID_W * 128
    rows = NA_HEADS * BT_LEN
    n_k = wide // tk

    def body(d_ref, sa_ref, sb_ref, ea_ref, eb_ref, o_ref, a_s, b_s):
        k = pl.program_id(0)
        dm = d_ref[...]
        d_hi = dm.astype(BF16)
        rest = dm - d_hi.astype(F32)
        d_mid = rest.astype(BF16)
        d_lo = (rest - d_mid.astype(F32)).astype(BF16)
        ea_b, eb_b = ea_ref[...].astype(BF16), eb_ref[...].astype(BF16)
        a = sum(jnp.dot(t, ea_b, preferred_element_type=F32) for t in (d_hi, d_mid, d_lo))
        b = sum(jnp.dot(t, eb_b, preferred_element_type=F32) for t in (d_hi, d_mid, d_lo))

        @pl.when(k == 0)
        def _():
            a_s[...] = a
            b_s[...] = b

        @pl.when(k > 0)
        def _():
            a_s[...] = a_s[...] + a
            b_s[...] = b_s[...] + b

        @pl.when(k == n_k - 1)
        def _():
            for h in range(NA_HEADS):
                sl = slice(h * BT_LEN, (h + 1) * BT_LEN)
                o_ref[h] = (jnp.dot(sa_ref[...], a_s[sl, :], preferred_element_type=F32, precision=hi)
                            + jnp.dot(sb_ref[...], b_s[sl, :], preferred_element_type=F32, precision=hi))

    res, extra = _call(
        body, name="rpb_grad", grid=(n_k,),
        in_specs=[pl.BlockSpec((rows, tk), lambda k: (0, k)), _full((n_dr, BT_LEN)), _full((n_dr, BT_LEN)),
                  pl.BlockSpec((tk, 31), lambda k: (k, 0)), pl.BlockSpec((tk, 31), lambda k: (k, 0))],
        out_specs=[_full((NA_HEADS, n_dr, 31))],
        out_shape=[jax.ShapeDtypeStruct((NA_HEADS, n_dr, 31), F32)],
        scratch_shapes=[pltpu.VMEM((rows, 31), F32), pltpu.VMEM((rows, 31), F32)], sem=("arbitrary",),
        args=(dbt.reshape(rows, wide), jnp.asarray(sel_at), jnp.asarray(sel_bt), eat, ebt), comm=comm)
    return res[0], extra


def lru_bwd(p, d_yrnn, conv_w, conv_b, wa, ba, wx, bx, lam, comm=None):
    blk, wspec = _lru_in_specs()

    def body(xr_ref, gx_ref, dy_ref, cw_ref, cb_ref, wa_ref, ba_ref, wx_ref, bx_ref, lam_ref,
             dxr_ref, dgx_ref, dcw_ref, dcb_ref, dwa_ref, dba_ref, dwx_ref, dbx_ref, dlam_ref,
             a_s, b_s, h_s, l_s, hsum_s, dxc_s, dh_s):
        xr = xr_ref[...]
        cw = cw_ref[...]
        xc = _lru_conv(xr, cw, cb_ref[...])
        xcb = xc.astype(BF16)
        g, dg = _gelu_parts(gx_ref[CTX_LEN:, :])
        dy = dy_ref[...]
        dh_s[:CTX_LEN, :] = jnp.zeros((CTX_LEN, LRU_BLOCK_W), F32)
        dh_s[CTX_LEN:, :] = dy * g
        row = _row_ids(ZLEN, LRU_BLOCK_W)
        zero = jnp.zeros((1, LRU_BLOCK_W), F32)
        for d in (0, 1):
            wab = wa_ref[d, 0].astype(BF16)
            wxb = wx_ref[d, 0].astype(BF16)
            lam_d = lam_ref[d:d + 1, :]
            r, gi, sp, a, sq, b = _lru_gates(xc, xcb, wab, ba_ref[d:d + 1, :], wxb, bx_ref[d:d + 1, :], lam_d)
            a_s[...] = a
            b_s[...] = b
            _lru_scan_dir(d, a_s, b_s, h_s)
            h = h_s[...]
            if d == 0:
                hsum_s[...] = h
                h_prev = jnp.where(row >= 1, pltpu.roll(h, 1, 0), 0.0)
                a_s[...] = pltpu.roll(a, ZLEN - 1, 0)
                _scan_down(a_s, dh_s, l_s, 0, N_CHUNK, zero)
            else:
                hsum_s[...] = hsum_s[...] + h
                h_prev = jnp.where(row == CTX_LEN - 1, 0.0, pltpu.roll(h, ZLEN - 1, 0))
                a_s[...] = pltpu.roll(a, 1, 0)
                c = _scan_up(a_s, dh_s, l_s, CTX_CHUNKS, N_CHUNK, zero)
                _scan_up(a_s, dh_s, l_s, 0, CTX_CHUNKS, c)
            db = l_s[...]
            da = db * h_prev
            dsq = db * gi * xc
            dgi = db * sq * xc
            dxc_d = db * sq * gi
            dla = da * a - dsq * (a * a) / sq
            dr = dla * ((-LRU_C) * sp)
            dsp = jnp.sum(dla * ((-LRU_C) * r), axis=0, keepdims=True)
            dlam_ref[d:d + 1, :] = -dsp * _sigmoid(-lam_d)
            dzr = dr * r * (1.0 - r)
            dzi = dgi * gi * (1.0 - gi)
            dba_ref[d:d + 1, :] = jnp.sum(dzr, axis=0, keepdims=True)
            dbx_ref[d:d + 1, :] = jnp.sum(dzi, axis=0, keepdims=True)
            dzrb = dzr.astype(BF16)
            dzib = dzi.astype(BF16)
            dwa_ref[d, 0] = _dot_tn(xcb, dzrb)
            dwx_ref[d, 0] = _dot_tn(xcb, dzib)
            dxc_d = dxc_d + _dot_nt(dzrb, wab) + _dot_nt(dzib, wxb)
            if d == 0:
                dxc_s[...] = dxc_d
            else:
                dxc_s[...] = dxc_s[...] + dxc_d
        dxc = dxc_s[...]
        dxr_ref[...] = _lru_conv_t(dxc, cw).astype(BF16)
        dcb_ref[...] = jnp.sum(dxc, axis=0, keepdims=True)
        segpos = jnp.where(row < CTX_LEN, row, row - CTX_LEN)
        seglen = jnp.where(row < CTX_LEN, CTX_LEN, SEQ)
        for k in range(4):
            off = k - 2
            if off == 0:
                sh = xr
            else:
                ok = (segpos + off >= 0) & (segpos + off < seglen)
                sh = jnp.where(ok, pltpu.roll(xr, (-off) % ZLEN, 0), 0.0)
            dcw_ref[k:k + 1, :] = jnp.sum(dxc * sh, axis=0, keepdims=True)
        dgx_ref[:CTX_LEN, :] = jnp.zeros((CTX_LEN, LRU_BLOCK_W), BF16)
        dgx_ref[CTX_LEN:, :] = (dy * hsum_s[CTX_LEN:, :] * dg).astype(BF16)

    zs = pltpu.VMEM((ZLEN, LRU_BLOCK_W), F32)
    zb = jax.ShapeDtypeStruct((ZLEN, D_MODEL), BF16)
    v2 = jax.ShapeDtypeStruct((2, D_MODEL), F32)
    w4 = jax.ShapeDtypeStruct((2, LRU_BLOCKS, LRU_BLOCK_W, LRU_BLOCK_W), F32)
    res, extra = _call(
        body, name="lru_bwd", grid=(LRU_BLOCKS,),
        in_specs=[blk(ZLEN), pl.BlockSpec((ZLEN, LRU_BLOCK_W), lambda b: (0, 24 + b)), blk(SEQ), blk(4), blk(1),
                  wspec, blk(2), wspec, blk(2), blk(2)],
        out_specs=[blk(ZLEN), blk(ZLEN), blk(4), blk(1), wspec, blk(2), wspec, blk(2), blk(2)],
        out_shape=[zb, zb, jax.ShapeDtypeStruct((4, D_MODEL), F32), jax.ShapeDtypeStruct((1, D_MODEL), F32),
                   w4, v2, w4, v2, v2],
        scratch_shapes=[zs] * 7, sem=("arbitrary",),
        args=(p, p, d_yrnn, conv_w, conv_b, wa, ba, wx, bx, lam), comm=comm)
    return (*res, extra)


def in_proj_bwd(dgs, w_in, z, dx1, gain, scale, comm=None):
    def body(*refs):
        dg_refs = refs[:7]
        w_ref, z_ref, dx1_ref, g_ref, sc_ref, gx_ref, dsh_ref, dsc_ref, dgn_ref = refs[7:]
        i = pl.program_id(0)
        dxn = _dot_nt(dg_refs[0][...], w_ref[:, 0:D_MODEL])
        for g in range(1, 7):
            dxn = dxn + _dot_nt(dg_refs[g][...], w_ref[:, g * D_MODEL:(g + 1) * D_MODEL])
        dx, dsh, dsc, dgn = _norm_mod_bwd(z_ref[...], dxn, g_ref[...], sc_ref[0])

        @pl.when(i <= 1)
        def _():
            dsh_ref[0] = dsh
            dsc_ref[0] = dsc

        @pl.when(i > 1)
        def _():
            dsh_ref[0] = dsh_ref[0] + dsh
            dsc_ref[0] = dsc_ref[0] + dsc

        @pl.when(i == 0)
        def _():
            dgn_ref[...] = dgn

        @pl.when(i > 0)
        def _():
            dgn_ref[...] = dgn_ref[...] + dgn
            gx_ref[...] = dx1_ref[...] + dx

    zrow = pl.BlockSpec((ROW_TILE, D_MODEL), lambda i: (i, 0))
    lat = pl.BlockSpec((ROW_TILE, D_MODEL), lambda i: (jnp.maximum(i - 1, 0), 0))
    mod = pl.BlockSpec((1, 1, D_MODEL), lambda i: (jnp.minimum(i, 1), 0, 0))
    mshape = jax.ShapeDtypeStruct((2, 1, D_MODEL), F32)
    res, extra = _call(
        body, name="in_proj_bwd", grid=(ZLEN // ROW_TILE,),
        in_specs=[zrow] * 7 + [_full((D_MODEL, IN_COLS)), zrow, lat, _full((1, D_MODEL)), mod],
        out_specs=[lat, mod, mod, _full((1, D_MODEL))],
        out_shape=[jax.ShapeDtypeStruct((SEQ, D_MODEL), F32), mshape, mshape, jax.ShapeDtypeStruct((1, D_MODEL), F32)],
        sem=("arbitrary",), args=(*dgs, w_in, z, dx1, gain, scale), comm=comm)
    return (*res, extra)


def matmul_tn(a, b, name, tm, tn, prev=None, col_block=0, total_cols=None):
    k, m = a.shape
    n = b.shape[1]
    total_cols = n if total_cols is None else total_cols
    assert m % tm == 0 and n % tn == 0
    off = col_block * (n // tn)

    def body(a_ref, b_ref, *rest):
        rest[-1][...] = _dot_tn(a_ref[...].astype(BF16), b_ref[...]).astype(BF16)

    in_specs = [pl.BlockSpec((k, tm), lambda i, j: (0, i)), pl.BlockSpec((k, tn), lambda i, j: (0, j))]
    args = [a, b]
    aliases = {}
    if prev is not None:
        in_specs.append(pl.BlockSpec(memory_space=pl.ANY))
        args.append(prev)
        aliases = {2: 0}
    return pl.pallas_call(
        body, name=name, grid=(m // tm, n // tn), in_specs=in_specs,
        out_specs=pl.BlockSpec((tm, tn), lambda i, j: (i, j + off)),
        out_shape=jax.ShapeDtypeStruct((m, total_cols), BF16),
        input_output_aliases=aliases,
        compiler_params=_params("parallel", "parallel"),
    )(*args)


def local_step(z, target, modx, modc, norm_mix_g, norm_ffn_g, w_in, conv_w, conv_b, wa, ba, wx, bx, lam, qg, kg, rpb,
               w_rnn, w_na, w_out, w_up, fconv_w, fconv_b, w_down, idx=None, bt=None):
    dist = idx is not None
    c_idx = idx[1:2] if dist else None
    d = D_MODEL
    mx = [modx[:, k * d:(k + 1) * d] for k in range(N_MOD)]
    shift = jnp.stack([modc[:, 0:d], mx[0]])
    scale = jnp.stack([modc[:, d:2 * d], mx[1]])
    cos, sin = _rope_tables()
    ones = _head_ones()
    qg2 = jnp.tile(qg, (1, 2))
    kg2 = jnp.tile(kg, (1, 2))

    xn = norm_mod(z, norm_mix_g, shift, scale, "norm_mix")
    if bt is None:
        bt, _ = bias_table(rpb)
    p, got = matmul_wide(xn, w_in, "in_proj", 3 * ROW_TILE, 1792,
                         comm=gather_weights_comm([w_rnn, w_na, w_out], [1, 2, 3]) if dist else None)
    if dist:
        w_rnn, w_na, w_out = got
    y_rnn, got = lru_fwd(p, conv_w, conv_b, wa, ba, wx, bx, lam,
                         comm=gather_weights_comm([w_down], [5]) if dist else None)
    if dist:
        w_down = got[0]
    q_rot, q_pl, kk, vv, _ = qkv_prep(p, qg2, kg2, cos, sin, ones)
    y_na, lse, got = attn_fwd(q_rot, q_pl, kk, vv, bt, comm=gather_weights_comm([w_up], [4]) if dist else None)
    if dist:
        w_up = got[0]
    u, v, merged, out, x1 = merge_fwd(y_rnn, y_na, p, z, mx[2], w_rnn, w_na, w_out)
    xn2 = norm_mod(x1, norm_ffn_g, mx[3][None], mx[4][None], "norm_ffn")
    hpre, _ = matmul_wide(xn2, w_up, "ffn_up", 2 * ROW_TILE, 1408)
    act = ffn_act(hpre, fconv_w, fconv_b)
    f, dy, df, loss_sq, dg5 = ffn_down_loss(act, w_down, x1, mx[5], target)

    partials, pieces = {}, {}

    def views_of(which, grads):
        return [_grad_view(g, BIG[w][1], BIG[w][2]) for w, g in zip(which, grads)]

    def chip_partials(which, views, recv):
        for w, gv, r in zip(which, views, recv):
            partials[w] = add_halves(gv, r, c_idx, "add_halves_" + BIG[w][0])
        return scatter_pieces_comm([partials[w] for w in which], which)

    d_act = ffn_down_bwd(df, w_down)
    dha, dhg, d_fcw_a, d_fcw_g, d_fcb_a, d_fcb_g = ffn_act_bwd(hpre, d_act, fconv_w, fconv_b)
    d_fcw = jnp.concatenate([d_fcw_a, d_fcw_g], axis=1)
    d_fcb = jnp.concatenate([d_fcb_a, d_fcb_g], axis=1)
    dx1, d_s3, d_s4, d_gffn = ffn_up_bwd(dha, dhg, w_up, x1, dy, norm_ffn_g, mx[4])
    g_w_down = matmul_tn(act, df, "gw_down", 256, D_MODEL)
    g_w_up = matmul_tn(xn2, dha, "gw_up_a", 512, 1408, total_cols=2 * D_FF)
    g_w_up = matmul_tn(xn2, dhg, "gw_up_g", 512, 1408, prev=g_w_up, col_block=1, total_cols=2 * D_FF)
    v_ffn = views_of([4, 5], [g_w_up, g_w_down]) if dist else None
    *mb, got = merge_bwd(dx1, out, mx[2], p, u, v, w_rnn, w_na, w_out,
                         comm=exchange_halves_comm(v_ffn) if dist else None)
    dout, du, dv, dmr, dmn, dyr, dyn, dg2 = mb
    recv_ffn = got
    g_w_out = matmul_tn(merged, dout, "gw_out", 1024, 512)
    g_w_rnn = matmul_tn(y_rnn, du, "gw_rnn", 1024, 512)
    g_w_na = matmul_tn(y_na, dv, "gw_na", 1024, 512)
    v_mix = views_of([1, 2, 3], [g_w_rnn, g_w_na, g_w_out]) if dist else None
    *lru_grads, got = lru_bwd(p, dyr, conv_w, conv_b, wa, ba, wx, bx, lam,
                              comm=join_comms(chip_partials([4, 5], v_ffn, recv_ffn),
                                              exchange_halves_comm(v_mix)) if dist else None)
    dxr, dgx, d_cw, d_cb, d_wa, d_ba, d_wx, d_bx, d_lam = lru_grads
    if dist:
        pieces[4], pieces[5] = got[:2]
    lru_w_all = {}
    dqr, dqp, dk, dvh, dbt, got = attn_bwd(
        q_rot, q_pl, kk, vv, bt, y_na, dyn, lse,
        comm=join_comms(chip_partials([1, 2, 3], v_mix, got[2:]),
                        join_comms(all_gather_comm(d_wa.reshape(-1, LRU_BLOCK_W)),
                                   all_gather_comm(d_wx.reshape(-1, LRU_BLOCK_W)))) if dist else None)
    if dist:
        pieces[1], pieces[2], pieces[3], lru_w_all["lru_wa"], lru_w_all["lru_wx"] = got
    dq_cols, dk_cols, dv_cols, d_qg, d_kg, _ = qkv_bwd(dqr, dqp, dk, dvh, p, qg2, kg2, cos, sin, ones)
    dgs = [dxr, dk_cols, dv_cols, dgx, dq_cols, dmr, dmn]
    g_w_in = None
    for g in range(7):
        g_w_in = matmul_tn(xn, dgs[g], "gw_in_%d" % g, 1024, 512, prev=g_w_in, col_block=g, total_cols=IN_COLS)
    v_in = views_of([0], [g_w_in]) if dist else None
    d_rpb, recv_in = rpb_grad(dbt, comm=exchange_halves_comm(v_in) if dist else None)
    grad_x, dsh, dsc, d_gmix, got = in_proj_bwd(dgs, w_in, z, dx1, norm_mix_g, scale,
                                                comm=chip_partials([0], v_in, recv_in) if dist else None)
    if dist:
        pieces[0] = got[0]

    d_modx = jnp.concatenate([dsh[1], dsc[1], dg2, d_s3, d_s4, dg5], axis=1)
    d_modc = jnp.concatenate([dsh[0], dsc[0]], axis=1)
    return dict(loss_sq=loss_sq, grad_x=grad_x, d_modx=d_modx, d_modc=d_modc, norm_mix_g=d_gmix, norm_ffn_g=d_gffn,
                w_in=g_w_in, lru_conv_w=d_cw, lru_conv_b=d_cb, lru_wa=d_wa, lru_ba=d_ba, lru_wx=d_wx, lru_bx=d_bx,
                lru_lambda=d_lam, q_norm_g=d_qg, k_norm_g=d_kg, na_rpb=d_rpb, w_rnn_out=g_w_rnn, w_na_out=g_w_na,
                w_out=g_w_out, w_up=g_w_up, ffn_conv_w=d_fcw, ffn_conv_b=d_fcb, w_down=g_w_down,
                partials=partials, pieces=pieces, lru_w_all=lru_w_all)


def _mesh_pos():
    return lax.axis_index("x"), lax.axis_index("y"), lax.axis_index("c")


def _other_chips(x, y):
    return [(1 - x, y), (x, 1 - y), (1 - x, 1 - y)]


BIG = (("w_in", (D_MODEL, IN_COLS), 1), ("w_rnn_out", (D_MODEL, D_MODEL), 0), ("w_na_out", (D_MODEL, D_MODEL), 0),
       ("w_out", (D_MODEL, D_MODEL), 0), ("w_up", (D_MODEL, 2 * D_FF), 1), ("w_down", (D_FF, D_MODEL), 0))


def _shard_shape(full, axis):
    r, c = full
    return (r // N_SHARD, c) if axis == 0 else (r, c // N_SHARD)


def _slot(ref, full, axis, s, h):
    r, c = full
    if axis == 0:
        rs = r // N_SHARD
        return ref.at[pl.ds(s * rs + h * (rs // 2), rs // 2), :]
    cs = c // N_SHARD
    return ref.at[pl.ds(h * (r // 2), r // 2), pl.ds(s * cs, cs)]


def cast_into_full(x, full, axis, idx, name):
    r, c = x.shape
    tr = next(t for t in (512, 352, 256, 128) if r % t == 0)
    nb = r // tr

    def body(idx_ref, x_ref, o_ref):
        o_ref[...] = x_ref[...].astype(BF16)

    if axis == 0:
        out_spec = pl.BlockSpec((tr, c), lambda i, idx_ref: (idx_ref[0] * nb + i, 0))
    else:
        out_spec = pl.BlockSpec((tr, c), lambda i, idx_ref: (i, idx_ref[0]))
    return pl.pallas_call(
        body, name=name,
        grid_spec=pltpu.PrefetchScalarGridSpec(
            num_scalar_prefetch=1, grid=(nb,), in_specs=[pl.BlockSpec((tr, c), lambda i, idx_ref: (i, 0))],
            out_specs=out_spec),
        out_shape=jax.ShapeDtypeStruct(full, BF16),
        compiler_params=_params("parallel"),
    )(idx, x)


def run_comm(comm, name):
    k_in, k_out = len(comm.inputs), len(comm.out_shapes)

    def body(*refs):
        start, mid, end = comm.emit(refs[:k_in], refs[k_in:k_in + k_out], refs[k_in + k_out:])
        start()
        mid()
        end()

    hbm = pl.BlockSpec(memory_space=pl.ANY)
    return pl.pallas_call(
        body, name=name, in_specs=[hbm] * k_in, out_specs=[hbm] * k_out, out_shape=list(comm.out_shapes),
        input_output_aliases=dict(comm.aliases), scratch_shapes=list(comm.scratch),
        compiler_params=pltpu.CompilerParams(vmem_limit_bytes=VMEM_LIMIT_V7X),
    )(*comm.inputs)


def gather_weights_comm(fulls, which):
    nw = len(which)
    specs = [BIG[w] for w in which]

    def emit(_, outs, sems):
        send1, recv1, send2, recv2 = sems
        x, y, c = _mesh_pos()
        sibling = (x, y, 1 - c)
        chips = _other_chips(x, y)
        s_me = 2 * x + y
        shards = [2 * chip[0] + chip[1] for chip in chips]

        def ici(w, j, shard):
            _, full, axis = specs[w]
            dst = _slot(outs[w], full, axis, shard, c)
            return pltpu.make_async_remote_copy(
                src_ref=dst, dst_ref=dst, send_sem=send1.at[3 * w + j],
                recv_sem=recv1.at[3 * w + j], device_id=(*chips[j], c), device_id_type=MESH_T)

        def d2d(w, j, shard, half):
            _, full, axis = specs[w]
            dst = _slot(outs[w], full, axis, shard, half)
            return pltpu.make_async_remote_copy(
                src_ref=dst, dst_ref=dst, send_sem=send2.at[3 * w + j], recv_sem=recv2.at[3 * w + j],
                device_id=sibling, device_id_type=MESH_T)

        pairs = [(w, j) for w in range(nw) for j in range(3)]

        def start():
            for w, j in pairs:
                ici(w, j, s_me).start()

        def mid():
            for w, j in pairs:
                ici(w, j, shards[j]).wait_recv()
                d2d(w, j, shards[j], c).start()

        def end():
            for w, j in pairs:
                d2d(w, j, shards[j], 1 - c).wait_recv()
            for w, j in pairs:
                ici(w, j, s_me).wait_send()
                d2d(w, j, shards[j], c).wait_send()

        return start, mid, end

    return Comm(list(fulls), [jax.ShapeDtypeStruct(full, BF16) for _, full, _ in specs], {i: i for i in range(nw)},
                [pltpu.SemaphoreType.DMA((3 * nw,))] * 4, emit)


def join_comms(a, b):
    ai, ao, asc = len(a.inputs), len(a.out_shapes), len(a.scratch)

    def emit(ins, outs, sems):
        fa = a.emit(ins[:ai], outs[:ao], sems[:asc])
        fb = b.emit(ins[ai:], outs[ao:], sems[asc:])

        def both(k):
            def run():
                fa[k]()
                fb[k]()
            return run

        return both(0), both(1), both(2)

    aliases = dict(a.aliases)
    aliases.update({ai + i: ao + o for i, o in b.aliases.items()})
    return Comm(a.inputs + b.inputs, a.out_shapes + b.out_shapes, aliases, a.scratch + b.scratch, emit)


def all_gather_comm(x):
    def emit(srcs, outs, sems):
        send_sems, recv_sems, local_sem = sems
        x_ref, out_ref = srcs[0], outs[0]
        x, y, c = _mesh_pos()
        me, sibling = (x, y, c), (x, y, 1 - c)
        chips = _other_chips(x, y)

        def blk(px, py, pc):
            return out_ref.at[4 * px + 2 * py + pc]

        def copy(k, block, to, src=None):
            return pltpu.make_async_remote_copy(
                src_ref=blk(*block) if src is None else src, dst_ref=blk(*block),
                send_sem=send_sems.at[k], recv_sem=recv_sems.at[k], device_id=to, device_id_type=MESH_T)

        def mine():
            return pltpu.make_async_copy(x_ref, blk(*me), local_sem)

        def start():
            mine().start()
            copy(0, me, sibling, src=x_ref).start()
            for j, chip in enumerate(chips):
                copy(1 + j, me, (*chip, c), src=x_ref).start()

        def mid():
            for j, chip in enumerate(chips):
                copy(1 + j, (*chip, c), me).wait_recv()
                copy(4 + j, (*chip, c), sibling).start()

        def end():
            copy(0, sibling, me).wait_recv()
            for j, chip in enumerate(chips):
                copy(4 + j, (*chip, 1 - c), me).wait_recv()
            copy(0, me, sibling, src=x_ref).wait_send()
            for j, chip in enumerate(chips):
                copy(1 + j, me, (*chip, c), src=x_ref).wait_send()
                copy(4 + j, (*chip, c), sibling).wait_send()
            mine().wait()

        return start, mid, end

    return Comm([x], [jax.ShapeDtypeStruct((N_DEV,) + x.shape, F32)], {},
                [pltpu.SemaphoreType.DMA((7,)), pltpu.SemaphoreType.DMA((7,)), pltpu.SemaphoreType.DMA], emit)


def sum_blocks(g, name):
    _, r, c = g.shape
    tr = 256 if r % 256 == 0 else r

    def body(g_ref, o_ref):
        acc = g_ref[0]
        for k in range(1, N_DEV):
            acc = acc + g_ref[k]
        o_ref[...] = acc

    return pl.pallas_call(
        body, name=name, grid=(r // tr,),
        in_specs=[pl.BlockSpec((N_DEV, tr, c), lambda i: (0, i, 0))],
        out_specs=pl.BlockSpec((tr, c), lambda i: (i, 0)),
        out_shape=jax.ShapeDtypeStruct((r, c), F32),
        compiler_params=_params("parallel"),
    )(g)


def _grad_view(g, full, axis):
    r, c = full
    if axis == 0:
        return g.reshape(N_SHARD, 2, r // N_SHARD // 2, c)
    return g.reshape(1, 2, r // 2, c)


def exchange_halves_comm(gviews):
    nw = len(gviews)

    def emit(srcs, outs, sems):
        send_sems, recv_sems = sems
        x, y, c = _mesh_pos()

        def copies():
            return [pltpu.make_async_remote_copy(
                src_ref=srcs[w].at[:, pl.ds(1 - c, 1)], dst_ref=outs[w], send_sem=send_sems.at[w],
                recv_sem=recv_sems.at[w], device_id=(x, y, 1 - c), device_id_type=MESH_T) for w in range(nw)]

        def start():
            for cp in copies():
                cp.start()

        def end():
            for cp in copies():
                cp.wait()

        return start, lambda: None, end

    return Comm(list(gviews), [jax.ShapeDtypeStruct((g.shape[0], 1) + g.shape[2:], BF16) for g in gviews], {},
                [pltpu.SemaphoreType.DMA((nw,)), pltpu.SemaphoreType.DMA((nw,))], emit)


def _row_tile(rh):
    return 128 if rh % 128 == 0 else rh


def add_halves(gview, recv, c_idx, name):
    a, _, rh, cc = gview.shape
    tr = _row_tile(rh)

    def body(c_ref, g_ref, r_ref, o_ref):
        o_ref[0] = (g_ref[0, 0].astype(F32) + r_ref[0, 0].astype(F32)).astype(BF16)

    return pl.pallas_call(
        body, name=name,
        grid_spec=pltpu.PrefetchScalarGridSpec(
            num_scalar_prefetch=1, grid=(a, rh // tr),
            in_specs=[pl.BlockSpec((1, 1, tr, cc), lambda s, i, c_ref: (s, c_ref[0], i, 0)),
                      pl.BlockSpec((1, 1, tr, cc), lambda s, i, c_ref: (s, 0, i, 0))],
            out_specs=pl.BlockSpec((1, tr, cc), lambda s, i, c_ref: (s, i, 0))),
        out_shape=jax.ShapeDtypeStruct((a, rh, cc), BF16),
        compiler_params=_params("parallel", "parallel"),
    )(c_idx, gview, recv)


def _piece_shape(full, axis):
    rs, cs = _shard_shape(full, axis)
    return (rs // 2, cs)


def scatter_pieces_comm(partials, which):
    nw = len(which)
    specs = [BIG[w] for w in which]

    def emit(srcs, outs, sems):
        send_sems, recv_sems = sems
        x, y, c = _mesh_pos()
        chips = _other_chips(x, y)

        def copies():
            cps = []
            for w, (_, full, axis) in enumerate(specs):
                cs = full[1] // N_SHARD
                for j, chip in enumerate(chips):
                    s_j = 2 * chip[0] + chip[1]
                    src = srcs[w].at[s_j] if axis == 0 else srcs[w].at[0, :, pl.ds(s_j * cs, cs)]
                    cps.append(pltpu.make_async_remote_copy(
                        src_ref=src, dst_ref=outs[w].at[j], send_sem=send_sems.at[3 * w + j],
                        recv_sem=recv_sems.at[3 * w + j], device_id=(*chip, c), device_id_type=MESH_T))
            return cps

        def start():
            for cp in copies():
                cp.start()

        def mid():
            pass

        def end():
            for cp in copies():
                cp.wait()

        return start, mid, end

    return Comm(list(partials), [jax.ShapeDtypeStruct((3,) + _piece_shape(full, axis), BF16) for _, full, axis in specs],
                {}, [pltpu.SemaphoreType.DMA((3 * nw,)), pltpu.SemaphoreType.DMA((3 * nw,))], emit)


def add_pieces(partial, recv, idx, axis, name):
    _, rh, cs = recv.shape
    tr = _row_tile(rh)

    def body(idx_ref, p_ref, r_ref, o_ref):
        o_ref[0] = ((p_ref[0].astype(F32) + r_ref[0].astype(F32)) + r_ref[1].astype(F32)) + r_ref[2].astype(F32)

    if axis == 0:
        pspec = pl.BlockSpec((1, tr, cs), lambda i, idx_ref: (idx_ref[0], i, 0))
    else:
        pspec = pl.BlockSpec((1, tr, cs), lambda i, idx_ref: (0, i, idx_ref[0]))
    return pl.pallas_call(
        body, name=name,
        grid_spec=pltpu.PrefetchScalarGridSpec(
            num_scalar_prefetch=1, grid=(rh // tr,),
            in_specs=[pspec, pl.BlockSpec((3, tr, cs), lambda i, idx_ref: (0, i, 0))],
            out_specs=pl.BlockSpec((1, tr, cs), lambda i, idx_ref: (idx_ref[1], i, 0))),
        out_shape=jax.ShapeDtypeStruct((2, rh, cs), F32),
        compiler_params=_params("parallel"),
    )(idx, partial, recv)


def join_halves_comm(halves):
    nw = len(halves)

    def emit(_, outs, sems):
        send_sems, recv_sems = sems
        x, y, c = _mesh_pos()

        def copy(w, half):
            return pltpu.make_async_remote_copy(
                src_ref=outs[w].at[half], dst_ref=outs[w].at[half], send_sem=send_sems.at[w], recv_sem=recv_sems.at[w],
                device_id=(x, y, 1 - c), device_id_type=MESH_T)

        def start():
            for w in range(nw):
                copy(w, c).start()

        def end():
            for w in range(nw):
                copy(w, c).wait_send()
                copy(w, 1 - c).wait_recv()

        return start, lambda: None, end

    return Comm(list(halves), [jax.ShapeDtypeStruct(h.shape, F32) for h in halves], {i: i for i in range(nw)},
                [pltpu.SemaphoreType.DMA((nw,))] * 2, emit)


MOD_COLS = N_MOD * D_MODEL // N_SHARD
MOD_TILE = 512


def mod_fwd(c16, w_mod):
    def body(c_ref, w_ref, s_ref, o_ref):
        cv = c_ref[...]
        s = cv * _sigmoid(cv)
        s_ref[...] = s
        o_ref[...] = jnp.dot(s.astype(BF16), w_ref[...].astype(BF16), preferred_element_type=F32)

    return pl.pallas_call(
        body, name="mod_fwd", grid=(MOD_COLS // MOD_TILE,),
        in_specs=[_full((16, D_MODEL)), pl.BlockSpec((D_MODEL, MOD_TILE), lambda j: (0, j))],
        out_specs=[_full((16, D_MODEL)), pl.BlockSpec((16, MOD_TILE), lambda j: (0, j))],
        out_shape=[jax.ShapeDtypeStruct((16, D_MODEL), F32), jax.ShapeDtypeStruct((16, MOD_COLS), F32)],
        compiler_params=_params("arbitrary"),
    )(c16, w_mod)


def mod_bwd(s16, dm16, w_mod):
    hi = lax.Precision.HIGHEST

    def body(s_ref, d_ref, w_ref, gw_ref, ds_ref):
        j = pl.program_id(0)
        dm = d_ref[...]
        gw_ref[...] = lax.dot_general(s_ref[...], dm, (((0,), (0,)), ((), ())), preferred_element_type=F32, precision=hi)
        part = lax.dot_general(dm, w_ref[...], (((1,), (1,)), ((), ())), preferred_element_type=F32, precision=hi)

        @pl.when(j == 0)
        def _():
            ds_ref[...] = part

        @pl.when(j > 0)
        def _():
            ds_ref[...] = ds_ref[...] + part

    return pl.pallas_call(
        body, name="mod_bwd", grid=(MOD_COLS // MOD_TILE,),
        in_specs=[_full((16, D_MODEL)), pl.BlockSpec((16, MOD_TILE), lambda j: (0, j)),
                  pl.BlockSpec((D_MODEL, MOD_TILE), lambda j: (0, j))],
        out_specs=[pl.BlockSpec((D_MODEL, MOD_TILE), lambda j: (0, j)), _full((16, D_MODEL))],
        out_shape=[jax.ShapeDtypeStruct((D_MODEL, MOD_COLS), F32), jax.ShapeDtypeStruct((16, D_MODEL), F32)],
        compiler_params=_params("arbitrary"),
    )(s16, dm16, w_mod)


def cctx_grad(parts, c_ctx):
    def body(p_ref, c_ref, o_ref):
        ds = p_ref[0:1, :]
        for s in range(1, N_SHARD):
            ds = ds + p_ref[16 * s:16 * s + 1, :]
        cv = c_ref[...]
        sg = _sigmoid(cv)
        o_ref[...] = ds * (sg * (1.0 + cv * (1.0 - sg)))

    return pl.pallas_call(
        body, name="cctx_grad", in_specs=[_full((N_DEV * 8, D_MODEL)), _full((1, D_MODEL))],
        out_specs=_full((1, D_MODEL)), out_shape=jax.ShapeDtypeStruct((1, D_MODEL), F32),
    )(parts, c_ctx)


def add_rows(a, b, name):
    def body(a_ref, b_ref, o_ref):
        o_ref[...] = a_ref[...] + b_ref[...]

    return pl.pallas_call(body, name=name, in_specs=[_full(a.shape), _full(b.shape)], out_specs=_full(a.shape),
                          out_shape=jax.ShapeDtypeStruct(a.shape, F32))(a, b)


def _adamw_update(w_ref, g_ref, m_ref, v_ref, d_ref, nm_ref, nv_ref):
    g_ = g_ref[...]
    m_ = ADAM_B1 * m_ref[...] + (1.0 - ADAM_B1) * g_
    v_ = ADAM_B2 * v_ref[...] + (1.0 - ADAM_B2) * (g_ * g_)
    m_hat = m_ / (1.0 - ADAM_B1 ** ADAM_STEP)
    v_hat = v_ / (1.0 - ADAM_B2 ** ADAM_STEP)
    d_ref[...] = -ADAM_LR * (m_hat / (jnp.sqrt(v_hat) + ADAM_EPS) + ADAM_WD * w_ref[...])
    nm_ref[...] = m_
    nv_ref[...] = v_


def adamw_many(ws, gs, ms, vs):
    n = len(ws)

    def body(*refs):
        for i in range(n):
            _adamw_update(*[refs[k * n + i] for k in range(7)])

    shapes = [jax.ShapeDtypeStruct(w.shape, F32) for w in ws]
    return pl.pallas_call(body, name="adamw_small", out_shape=shapes * 3,
                          compiler_params=pltpu.CompilerParams(vmem_limit_bytes=VMEM_LIMIT_V7X))(*ws, *gs, *ms, *vs)


def adamw(w, g, m, v, name, comm=None):
    r, c = w.shape
    tr = 128 if (r % 128 == 0 and r > 128) else r

    def body(w_ref, g_ref, m_ref, v_ref, d_ref, nm_ref, nv_ref):
        _adamw_update(w_ref, g_ref, m_ref, v_ref, d_ref, nm_ref, nv_ref)

    spec = pl.BlockSpec((tr, c), lambda i: (i, 0))
    shp = jax.ShapeDtypeStruct((r, c), F32)
    res, extra = _call(body, name=name, grid=(r // tr,), in_specs=[spec] * 4, out_specs=[spec] * 3,
                       out_shape=[shp] * 3, sem=("parallel",), args=(w, g, m, v), comm=comm)
    return (*res, extra)


LANES = 1024


def _pack(arrs):
    rows, spans, at = [], [], 0
    for a in arrs:
        n = int(np.prod(a.shape))
        nr = 8 * -(-n // (8 * LANES))
        flat = a.reshape(-1)
        if nr * LANES != n:
            flat = jnp.concatenate([flat, jnp.zeros((nr * LANES - n,), F32)])
        rows.append(flat.reshape(nr, LANES))
        spans.append((at, nr, n, a.shape))
        at += nr
    return jnp.concatenate(rows, axis=0), spans


def _unpack(buf, spans):
    out = []
    for at, nr, n, shape in spans:
        out.append(buf[at:at + nr].reshape(-1)[:n].reshape(shape))
    return out


SMALL_SHARD = ("lru_conv_w", "lru_ba", "lru_bx", "lru_lambda", "ffn_conv_w")


def kernel(x, c, ctx, c_ctx, w_mod, b_mod, norm_mix_g, norm_ffn_g, w_in, lru_conv_w, lru_conv_b, lru_wa, lru_ba, lru_wx, lru_bx, lru_lambda, q_norm_g, k_norm_g, na_rpb, w_rnn_out, w_na_out, w_out, w_up, ffn_conv_w, ffn_conv_b, w_down, loss_target, m_c_ctx, m_w_mod, m_b_mod, m_norm_mix_g, m_norm_ffn_g, m_w_in, m_lru_conv_w, m_lru_conv_b, m_lru_wa, m_lru_ba, m_lru_wx, m_lru_bx, m_lru_lambda, m_q_norm_g, m_k_norm_g, m_na_rpb, m_w_rnn_out, m_w_na_out, m_w_out, m_w_up, m_ffn_conv_w, m_ffn_conv_b, m_w_down, v_c_ctx, v_w_mod, v_b_mod, v_norm_mix_g, v_norm_ffn_g, v_w_in, v_lru_conv_w, v_lru_conv_b, v_lru_wa, v_lru_ba, v_lru_wx, v_lru_bx, v_lru_lambda, v_q_norm_g, v_k_norm_g, v_na_rpb, v_w_rnn_out, v_w_na_out, v_w_out, v_w_up, v_ffn_conv_w, v_ffn_conv_b, v_w_down):
    weights = dict(c_ctx=c_ctx, w_mod=w_mod, b_mod=b_mod, norm_mix_g=norm_mix_g, norm_ffn_g=norm_ffn_g, w_in=w_in,
                   lru_conv_w=lru_conv_w, lru_conv_b=lru_conv_b, lru_wa=lru_wa, lru_ba=lru_ba, lru_wx=lru_wx,
                   lru_bx=lru_bx, lru_lambda=lru_lambda, q_norm_g=q_norm_g, k_norm_g=k_norm_g, na_rpb=na_rpb,
                   w_rnn_out=w_rnn_out, w_na_out=w_na_out, w_out=w_out, w_up=w_up, ffn_conv_w=ffn_conv_w,
                   ffn_conv_b=ffn_conv_b, w_down=w_down)
    mom1 = dict(c_ctx=m_c_ctx, w_mod=m_w_mod, b_mod=m_b_mod, norm_mix_g=m_norm_mix_g, norm_ffn_g=m_norm_ffn_g,
                w_in=m_w_in, lru_conv_w=m_lru_conv_w, lru_conv_b=m_lru_conv_b, lru_wa=m_lru_wa, lru_ba=m_lru_ba,
                lru_wx=m_lru_wx, lru_bx=m_lru_bx, lru_lambda=m_lru_lambda, q_norm_g=m_q_norm_g, k_norm_g=m_k_norm_g,
                na_rpb=m_na_rpb, w_rnn_out=m_w_rnn_out, w_na_out=m_w_na_out, w_out=m_w_out, w_up=m_w_up,
                ffn_conv_w=m_ffn_conv_w, ffn_conv_b=m_ffn_conv_b, w_down=m_w_down)
    mom2 = dict(c_ctx=v_c_ctx, w_mod=v_w_mod, b_mod=v_b_mod, norm_mix_g=v_norm_mix_g, norm_ffn_g=v_norm_ffn_g,
                w_in=v_w_in, lru_conv_w=v_lru_conv_w, lru_conv_b=v_lru_conv_b, lru_wa=v_lru_wa, lru_ba=v_lru_ba,
                lru_wx=v_lru_wx, lru_bx=v_lru_bx, lru_lambda=v_lru_lambda, q_norm_g=v_q_norm_g, k_norm_g=v_k_norm_g,
                na_rpb=v_na_rpb, w_rnn_out=v_w_rnn_out, w_na_out=v_w_na_out, w_out=v_w_out, w_up=v_w_up,
                ffn_conv_w=v_ffn_conv_w, ffn_conv_b=v_ffn_conv_b, w_down=v_w_down)
    order = list(weights)
    d = D_MODEL
    mx_, my_, mc_ = _mesh_pos()
    shard = 2 * mx_ + my_
    dev = 2 * shard + mc_

    idx = jnp.stack([shard, mc_]).astype(jnp.int32)
    wsh = {name: cast_into_full(weights[name][0], full, axis, idx, "cast_" + name) for name, full, axis in BIG}
    local_small, small_spans = _pack([c] + [weights[k][0] for k in SMALL_SHARD])
    bt, (w_in_full, gath) = bias_table(na_rpb[0], comm=join_comms(gather_weights_comm([wsh["w_in"]], [0]),
                                                                  all_gather_comm(local_small)))
    per_dev = [_unpack(gath[k], small_spans) for k in range(N_DEV)]
    c_all = jnp.concatenate([per_dev[k][0] for k in range(N_DEV)], axis=0)
    full_small = {name: jnp.concatenate([per_dev[2 * s][1 + i] for s in range(N_SHARD)], axis=-1)
                  for i, name in enumerate(SMALL_SHARD)}
    c16 = jnp.concatenate([c_all, c_ctx.reshape(1, d), jnp.zeros((7, d), F32)], axis=0)
    s16, mod_part = mod_fwd(c16, w_mod[0])
    mod_all = run_comm(all_gather_comm(mod_part), "gather_mod")[0]
    mod = jnp.concatenate([mod_all[2 * s] for s in range(N_SHARD)], axis=1) + b_mod
    modx = lax.dynamic_slice(mod, (dev, 0), (1, N_MOD * d))
    modc = mod[8:9]

    z = jnp.concatenate([ctx[0], x[0]], axis=0)
    res = local_step(z, loss_target[0], modx, modc, norm_mix_g, norm_ffn_g, w_in_full, full_small["lru_conv_w"],
                     lru_conv_b, lru_wa[0], full_small["lru_ba"], lru_wx[0], full_small["lru_bx"],
                     full_small["lru_lambda"], q_norm_g, k_norm_g, na_rpb[0], wsh["w_rnn_out"], wsh["w_na_out"],
                     wsh["w_out"], wsh["w_up"], full_small["ffn_conv_w"], ffn_conv_b, wsh["w_down"], idx=idx, bt=bt)

    halves = [add_pieces(res["partials"][i], res["pieces"][i], idx, BIG[i][2], "add_pieces_" + BIG[i][0])
              for i in range(len(BIG))]
    lru_tot = {k: sum_blocks(res["lru_w_all"][k], "sum_" + k).reshape(weights[k].shape[1:])
               for k in ("lru_wa", "lru_wx")}
    small_names = ["norm_mix_g", "norm_ffn_g", "lru_conv_w", "lru_conv_b", "lru_ba", "lru_bx",
                   "lru_lambda", "q_norm_g", "k_norm_g", "na_rpb", "ffn_conv_w", "ffn_conv_b"]
    local_g, g_spans = _pack([res["loss_sq"][0:1, 0:1], res["d_modx"], res["d_modc"]] + [res[k] for k in small_names])
    n_rows = local_g.shape[0]
    *joined, g_all = run_comm(join_comms(join_halves_comm(halves), all_gather_comm(local_g)), "tail_exchange")
    grads = {name: joined[i].reshape(_shard_shape(full, axis)) for i, (name, full, axis) in enumerate(BIG)}
    grads.update(lru_tot)
    g_tot = sum_blocks(g_all, "sum_small")
    tot = _unpack(g_tot, g_spans)
    loss = (0.5 / d) * tot[0][0, 0]
    small_tot = dict(zip(small_names, tot[3:]))
    at_x = g_spans[1][0]
    dmx_rows = g_all.reshape(N_DEV, n_rows, LANES)[:, at_x:at_x + N_MOD, :].reshape(N_DEV, N_MOD * d)
    dmc_row = jnp.concatenate([tot[2], jnp.zeros((1, 4 * d), F32)], axis=1)
    dm16 = jnp.concatenate([dmx_rows, dmc_row, jnp.zeros((7, N_MOD * d), F32)], axis=0)
    grads["b_mod"] = add_rows(tot[1], dmc_row, "b_mod_grad")
    g_w_mod, ds16 = mod_bwd(s16, lax.dynamic_slice(dm16, (0, shard * MOD_COLS), (16, MOD_COLS)), w_mod[0])
    grads["w_mod"] = g_w_mod
    for k in small_names:
        g = small_tot[k]
        if k in SMALL_SHARD:
            w_sh = weights[k].shape[-1]
            g = lax.dynamic_slice_in_dim(g, shard * w_sh, w_sh, axis=g.ndim - 1)
        grads[k] = g

    delta, new_m, new_v = {}, {}, {}
    for name, _, _ in BIG + (("w_mod", None, None),):
        *upd, got = adamw(weights[name][0], grads[name], mom1[name][0], mom2[name][0], "adamw_" + name,
                          comm=all_gather_comm(ds16[8:16]) if name == "w_in" else None)
        delta[name], new_m[name], new_v[name] = upd
        if name == "w_in":
            grads["c_ctx"] = cctx_grad(got[0].reshape(N_DEV * 8, d), c_ctx.reshape(1, d))
    rest = [k for k in order if k not in delta]
    views = {k: (grads[k].shape if grads[k].ndim <= 3 else (-1, grads[k].shape[-1])) for k in rest}
    small = adamw_many(*[[t[k].reshape(views[k]) for k in rest] for t in (weights, grads, mom1, mom2)])
    n_rest = len(rest)
    for i, k in enumerate(rest):
        delta[k], new_m[k], new_v[k] = small[i], small[n_rest + i], small[2 * n_rest + i]

    shaped = lambda t: [t[k].reshape(weights[k].shape) for k in order]
    return (loss, res["grad_x"][None], *shaped(grads), *shaped(delta), *shaped(new_m), *shaped(new_v))
```

```python
import numpy as np
import jax
import jax.numpy as jnp
from jax import lax
from jax.experimental import pallas as pl
from jax.experimental.pallas import tpu as pltpu

F32 = jnp.float32
BF16 = jnp.bfloat16

D_MODEL = 1024
SEQ = 2048
CTX_LEN = 256
ZLEN = SEQ + CTX_LEN
GRID_W = 64
GRID_ROWS = SEQ // GRID_W
LRU_BLOCK_W = 128
LRU_BLOCKS = 8
LRU_C = 8.0
NA_HEADS = 16
HEAD_DIM = 64
NA_ROWS = 8
NA_COLS = 16
ROPE_BASE = 10000.0
D_FF = 2816
N_MOD = 6
IN_COLS = 7 * D_MODEL
EPS = 1e-6
NEG_INF = -1e30
N_DEV = 8
N_SHARD = 4

ADAM_LR = 0.001
ADAM_B1 = 0.9
ADAM_B2 = 0.999
ADAM_EPS = 1e-08
ADAM_WD = 0.01
ADAM_STEP = 10

ROW_TILE = 256
Q_ROWS = 4
Q_TILE = Q_ROWS * GRID_W
KEY_ROWS = 12
KEY_TILE = KEY_ROWS * GRID_W
BT_PAD = 4
BT_LEN = 24
VMEM_LIMIT_V7X = 56 * 1024 * 1024

MESH_T = pl.DeviceIdType.MESH


def _params(*sem):
    return pltpu.CompilerParams(dimension_semantics=sem if sem else None, vmem_limit_bytes=VMEM_LIMIT_V7X)


def _full(shape):
    nd = len(shape)
    return pl.BlockSpec(shape, lambda *_: (0,) * nd)


class Comm:
    def __init__(self, inputs, out_shapes, aliases, scratch, emit):
        self.inputs, self.out_shapes, self.aliases, self.scratch, self.emit = inputs, out_shapes, aliases, scratch, emit


def _call(body, *, name, grid, in_specs, out_specs, out_shape, args, scratch_shapes=(), sem=(), comm=None):
    n_in, n_out, n_sc = len(in_specs), len(out_specs), len(scratch_shapes)
    if comm is None:
        res = pl.pallas_call(body, name=name, grid=grid, in_specs=list(in_specs), out_specs=list(out_specs),
                             out_shape=list(out_shape), scratch_shapes=list(scratch_shapes),
                             compiler_params=_params(*sem))(*args)
        return list(res), []
    k_in, k_out = len(comm.inputs), len(comm.out_shapes)
    steps = int(np.prod(grid))

    def hosted(*refs):
        ins, cins = refs[:n_in], refs[n_in:n_in + k_in]
        at = n_in + k_in
        outs, couts = refs[at:at + n_out], refs[at + n_out:at + n_out + k_out]
        at += n_out + k_out
        scr, cscr = refs[at:at + n_sc], refs[at + n_sc:]
        start, mid, end = comm.emit(cins, couts, cscr)
        lin = pl.program_id(0)
        for ax in range(1, len(grid)):
            lin = lin * grid[ax] + pl.program_id(ax)
        pl.when(lin == 0)(start)
        body(*ins, *outs, *scr)
        pl.when(lin == steps - 1 - steps // 7)(mid)
        pl.when(lin == steps - 1)(end)

    hbm = pl.BlockSpec(memory_space=pl.ANY)
    res = pl.pallas_call(
        hosted, name=name, grid=grid, in_specs=list(in_specs) + [hbm] * k_in, out_specs=list(out_specs) + [hbm] * k_out,
        out_shape=list(out_shape) + list(comm.out_shapes), scratch_shapes=list(scratch_shapes) + list(comm.scratch),
        input_output_aliases={n_in + i: n_out + o for i, o in comm.aliases.items()},
        compiler_params=_params(*(("arbitrary",) * len(grid))))(*args, *comm.inputs)
    return list(res[:n_out]), list(res[n_out:])


def _sigmoid(x):
    return 0.5 * jnp.tanh(0.5 * x) + 0.5


def _gelu_parts(x):
    c0 = 0.7978845608028654
    inner = c0 * (x + 0.044715 * x * x * x)
    t = jnp.tanh(inner)
    g = 0.5 * x * (1.0 + t)
    dg = 0.5 * (1.0 + t) + 0.5 * x * (1.0 - t * t) * c0 * (1.0 + 3.0 * 0.044715 * x * x)
    return g, dg


def _dot_nt(a, b):
    return lax.dot_general(a, b, (((1,), (1,)), ((), ())), preferred_element_type=F32)


def _dot_tn(a, b):
    return lax.dot_general(a, b, (((0,), (0,)), ((), ())), preferred_element_type=F32)


def norm_matmul(xin, gain, shift, scale, w, name, tm, tn, ctx_rows=0, comm=None):
    r, d = xin.shape
    n = w.shape[1]
    assert r % tm == 0 and n % tn == 0

    def body(x_ref, g_ref, sh_ref, sc_ref, w_ref, y_ref, xn_hbm, xn_s, sem):
        j, i = pl.program_id(0), pl.program_id(1)
        rows = pl.ds(pl.multiple_of(i * tm, tm), tm)

        @pl.when(j == 0)
        def _():
            x = x_ref[...]
            nrm = x * lax.rsqrt(jnp.mean(x * x, axis=-1, keepdims=True) + EPS)
            sh, sc = sh_ref[shift.shape[0] - 1], sc_ref[shift.shape[0] - 1]
            if ctx_rows:
                is_ctx = i * tm + lax.broadcasted_iota(jnp.int32, (tm, 1), 0) < ctx_rows
                sh, sc = jnp.where(is_ctx, sh_ref[0], sh), jnp.where(is_ctx, sc_ref[0], sc)
            xn_s[rows, :] = ((nrm * g_ref[...]) * (1.0 + sc) + sh).astype(BF16)
            cp = pltpu.make_async_copy(xn_s.at[rows, :], xn_hbm.at[rows, :], sem)
            cp.start()
            cp.wait()

        y_ref[...] = jnp.dot(xn_s[rows, :], w_ref[...], preferred_element_type=F32)

    res, extra = _call(
        body, name=name, grid=(n // tn, r // tm),
        in_specs=[pl.BlockSpec((tm, d), lambda j, i: (i, 0)), _full((1, d)), _full(shift.shape), _full(scale.shape),
                  pl.BlockSpec((d, tn), lambda j, i: (0, j))],
        out_specs=[pl.BlockSpec((tm, tn), lambda j, i: (i, j)), pl.BlockSpec(memory_space=pl.ANY)],
        out_shape=[jax.ShapeDtypeStruct((r, n), F32), jax.ShapeDtypeStruct((r, d), BF16)],
        scratch_shapes=[pltpu.VMEM((r, d), BF16), pltpu.SemaphoreType.DMA],
        sem=("arbitrary", "arbitrary"), args=(xin, gain, shift, scale, w), comm=comm)
    return res[1], res[0], extra


def _row_ids(n, w):
    return lax.broadcasted_iota(jnp.int32, (n, w), 0)


def _lru_conv(xr, cw, cb):
    row = _row_ids(ZLEN, LRU_BLOCK_W)
    segpos = jnp.where(row < CTX_LEN, row, row - CTX_LEN)
    seglen = jnp.where(row < CTX_LEN, CTX_LEN, SEQ)
    acc = xr * cw[2:3, :] + cb
    for k in (0, 1, 3):
        off = k - 2
        sh = pltpu.roll(xr, (-off) % ZLEN, 0)
        ok = (segpos + off >= 0) & (segpos + off < seglen)
        acc = acc + jnp.where(ok, sh, 0.0) * cw[k:k + 1, :]
    return acc


def _lru_conv_t(dxc, cw):
    row = _row_ids(ZLEN, LRU_BLOCK_W)
    segpos = jnp.where(row < CTX_LEN, row, row - CTX_LEN)
    seglen = jnp.where(row < CTX_LEN, CTX_LEN, SEQ)
    acc = dxc * cw[2:3, :]
    for k in (0, 1, 3):
        off = k - 2
        sh = pltpu.roll(dxc, off % ZLEN, 0)
        ok = (segpos - off >= 0) & (segpos - off < seglen)
        acc = acc + jnp.where(ok, sh, 0.0) * cw[k:k + 1, :]
    return acc


def _lru_gates(xc, xcb, wa, ba, wx, bx, lam):
    r = _sigmoid(jnp.dot(xcb, wa, preferred_element_type=F32) + ba)
    i = _sigmoid(jnp.dot(xcb, wx, preferred_element_type=F32) + bx)
    sp = jnp.maximum(-lam, 0.0) + jnp.log1p(jnp.exp(-jnp.abs(lam)))
    la = (-LRU_C) * r * sp
    a = jnp.exp(la)
    sq = jnp.sqrt(-jnp.tanh(la) * (1.0 + a * a))
    b = sq * i * xc
    return r, i, sp, a, sq, b


def _scan8_fwd(a, b, rid):
    for s in (1, 2, 4):
        a_s = pltpu.roll(a, s, 0)
        b_s = pltpu.roll(b, s, 0)
        m = rid >= s
        b = jnp.where(m, a * b_s + b, b)
        a = jnp.where(m, a * a_s, a)
    return a, b


def _scan8_rev(a, b, rid):
    for s in (1, 2, 4):
        a_s = pltpu.roll(a, 8 - s, 0)
        b_s = pltpu.roll(b, 8 - s, 0)
        m = rid < 8 - s
        b = jnp.where(m, a * b_s + b, b)
        a = jnp.where(m, a * a_s, a)
    return a, b


N_CHUNK = ZLEN // 8
CTX_CHUNKS = CTX_LEN // 8
SCAN_UNROLL = 8


def _scan_up(a_ref, b_ref, h_ref, lo, hi, carry):
    rid = _row_ids(8, LRU_BLOCK_W)
    assert (hi - lo) % SCAN_UNROLL == 0

    def step(g, c):
        base = pl.multiple_of((lo + g * SCAN_UNROLL) * 8, 8)
        for u in range(SCAN_UNROLL):
            sl = pl.ds(base + 8 * u, 8)
            a, b = _scan8_fwd(a_ref[sl, :], b_ref[sl, :], rid)
            h_ref[sl, :] = b + a * c
            c = b[7:8, :] + a[7:8, :] * c
        return c

    return lax.fori_loop(0, (hi - lo) // SCAN_UNROLL, step, carry)


def _scan_down(a_ref, b_ref, h_ref, lo, hi, carry):
    rid = _row_ids(8, LRU_BLOCK_W)
    assert (hi - lo) % SCAN_UNROLL == 0

    def step(g, c):
        base = pl.multiple_of((hi - (g + 1) * SCAN_UNROLL) * 8, 8)
        for u in reversed(range(SCAN_UNROLL)):
            sl = pl.ds(base + 8 * u, 8)
            a, b = _scan8_rev(a_ref[sl, :], b_ref[sl, :], rid)
            h_ref[sl, :] = b + a * c
            c = b[0:1, :] + a[0:1, :] * c
        return c

    return lax.fori_loop(0, (hi - lo) // SCAN_UNROLL, step, carry)


def _lru_scan_dir(d, a_ref, b_ref, h_ref):
    zero = jnp.zeros((1, LRU_BLOCK_W), F32)
    if d == 0:
        _scan_up(a_ref, b_ref, h_ref, 0, N_CHUNK, zero)
    else:
        c = _scan_down(a_ref, b_ref, h_ref, 0, CTX_CHUNKS, zero)
        _scan_down(a_ref, b_ref, h_ref, CTX_CHUNKS, N_CHUNK, c)


def _lru_in_specs():
    blk = lambda rows: pl.BlockSpec((rows, LRU_BLOCK_W), lambda b: (0, b))
    wspec = pl.BlockSpec((2, 1, LRU_BLOCK_W, LRU_BLOCK_W), lambda b: (0, b, 0, 0))
    return blk, wspec


def lru_fwd(p, conv_w, conv_b, wa, ba, wx, bx, lam, comm=None):
    blk, wspec = _lru_in_specs()

    def body(xr_ref, gx_ref, cw_ref, cb_ref, wa_ref, ba_ref, wx_ref, bx_ref, lam_ref, y_ref, a_s, b_s, h_s, hsum_s):
        xr = xr_ref[...]
        xc = _lru_conv(xr, cw_ref[...], cb_ref[...])
        xcb = xc.astype(BF16)
        for d in (0, 1):
            _, _, _, a, _, b = _lru_gates(xc, xcb, wa_ref[d, 0].astype(BF16), ba_ref[d:d + 1, :],
                                          wx_ref[d, 0].astype(BF16), bx_ref[d:d + 1, :], lam_ref[d:d + 1, :])
            a_s[...] = a
            b_s[...] = b
            _lru_scan_dir(d, a_s, b_s, h_s)
            if d == 0:
                hsum_s[...] = h_s[...]
            else:
                hsum_s[...] = hsum_s[...] + h_s[...]
        g, _ = _gelu_parts(gx_ref[CTX_LEN:, :])
        y_ref[...] = (hsum_s[CTX_LEN:, :] * g).astype(BF16)

    zs = pltpu.VMEM((ZLEN, LRU_BLOCK_W), F32)
    res, extra = _call(
        body, name="lru_fwd", grid=(LRU_BLOCKS,),
        in_specs=[blk(ZLEN), pl.BlockSpec((ZLEN, LRU_BLOCK_W), lambda b: (0, 24 + b)), blk(4), blk(1),
                  wspec, blk(2), wspec, blk(2), blk(2)],
        out_specs=[pl.BlockSpec((SEQ, LRU_BLOCK_W), lambda b: (0, b))],
        out_shape=[jax.ShapeDtypeStruct((SEQ, D_MODEL), BF16)],
        scratch_shapes=[zs, zs, zs, zs], sem=("arbitrary",),
        args=(p, p, conv_w, conv_b, wa, ba, wx, bx, lam), comm=comm)
    return res[0], extra


def _rope_tables():
    t = np.arange(SEQ)
    lane = np.arange(2 * HEAD_DIM)
    in_head = lane % HEAD_DIM
    j = (in_head % 32) % 16
    freq = ROPE_BASE ** (-j.astype(np.float64) / 16.0)
    pos = np.where(in_head[None, :] < 32, (t // GRID_W)[:, None], (t % GRID_W)[:, None]).astype(np.float64)
    ang = (pos.astype(np.float32) * freq.astype(np.float32)[None, :]).astype(np.float32)
    cos = np.cos(ang).astype(np.float32)
    sin = np.sin(ang).astype(np.float32)
    sgn = np.where((in_head % 32) < 16, -1.0, 1.0).astype(np.float32)
    cos = np.concatenate([np.ones((CTX_LEN, 2 * HEAD_DIM), np.float32), cos], 0)
    sin = np.concatenate([np.zeros((CTX_LEN, 2 * HEAD_DIM), np.float32), sin * sgn[None, :]], 0)
    return jnp.asarray(cos), jnp.asarray(sin)


def _head_ones():
    lane = np.arange(2 * HEAD_DIM)
    return jnp.asarray((lane[:, None] // HEAD_DIM == lane[None, :] // HEAD_DIM).astype(np.float32))


def _rope_partner(x):
    lane = lax.broadcasted_iota(jnp.int32, x.shape, 1)
    return jnp.where((lane % 32) < 16, pltpu.roll(x, 128 - 16, 1), pltpu.roll(x, 16, 1))


def _head_sum(t, ones):
    hi = t.astype(BF16)
    lo = (t - hi.astype(F32)).astype(BF16)
    ones_b = ones.astype(BF16)
    return jnp.dot(hi, ones_b, preferred_element_type=F32) + jnp.dot(lo, ones_b, preferred_element_type=F32)


def _head_rms(x, ones, gain):
    ms = _head_sum(x * x, ones) * (1.0 / HEAD_DIM)
    rstd = lax.rsqrt(ms + EPS)
    return x * rstd * gain, rstd


PREP_TILE = 768


def qkv_prep(p, qg2, kg2, cos, sin, ones, comm=None):
    scale = HEAD_DIM ** -0.5

    def body(q_ref, k_ref, v_ref, qg_ref, kg_ref, cos_ref, sin_ref, ones_ref, qr_ref, qp_ref, kk_ref, vv_ref):
        ones_m = ones_ref[...]
        c, s = cos_ref[...], sin_ref[...]
        qn, _ = _head_rms(q_ref[...], ones_m, qg_ref[...])
        qn = qn * scale
        qr_ref[...] = (qn * c + _rope_partner(qn) * s).astype(BF16)
        qp_ref[...] = qn.astype(BF16)
        kn, _ = _head_rms(k_ref[...], ones_m, kg_ref[...])
        kk_ref[...] = (kn * c + _rope_partner(kn) * s).astype(BF16)
        vv_ref[...] = v_ref[...].astype(BF16)

    col = lambda base: pl.BlockSpec((PREP_TILE, 128), lambda hp, i: (i, base + hp))
    small = pl.BlockSpec((1, 128), lambda hp, i: (0, 0))
    tab = pl.BlockSpec((PREP_TILE, 128), lambda hp, i: (i, 0))
    oshape = jax.ShapeDtypeStruct((ZLEN, D_MODEL), BF16)
    res, extra = _call(
        body, name="qkv_prep", grid=(NA_HEADS // 2, ZLEN // PREP_TILE),
        in_specs=[col(32), col(8), col(16), small, small, tab, tab, _full((128, 128))],
        out_specs=[col(0)] * 4, out_shape=[oshape] * 4, sem=("parallel", "parallel"),
        args=(p, p, p, qg2, kg2, cos, sin, ones), comm=comm)
    return (*res, extra)


def _bias_expand():
    qc = np.arange(GRID_W)[:, None]
    kc = np.arange(GRID_W)[None, :]
    col_start = np.clip(qc - NA_COLS // 2, 0, GRID_W - NA_COLS)
    in_win = (kc >= col_start) & (kc < col_start + NA_COLS)
    dc = np.clip(kc - qc, -(NA_COLS - 1), NA_COLS - 1) + (NA_COLS - 1)
    e = np.zeros((2 * NA_COLS - 1, GRID_W, GRID_W), np.float32)
    for d in range(2 * NA_COLS - 1):
        e[d] = ((dc == d) & in_win).astype(np.float32)
    pen = np.where(in_win, 0.0, NEG_INF).astype(np.float32)
    return e, pen


def bias_table(rpb2, comm=None):
    e, pen = _bias_expand()
    n_dr = 2 * NA_ROWS - 1
    ea = np.zeros((31, GRID_W, 128), np.float32)
    ea[:, :, :GRID_W] = e
    eb = np.zeros((31, GRID_W, 128), np.float32)
    eb[:, :, GRID_W:] = e
    pen2 = np.concatenate([pen, pen], 1)
    ea = jnp.asarray(ea.reshape(31, GRID_W * 128))
    eb = jnp.asarray(eb.reshape(31, GRID_W * 128))
    sel_a = np.zeros((BT_LEN, n_dr), np.float32)
    sel_b = np.zeros((BT_LEN, n_dr), np.float32)
    for r in range(BT_LEN):
        dr = r - BT_PAD
        if 0 <= dr < n_dr:
            sel_a[r, dr] = 1.0
        if 0 <= dr + 1 < n_dr:
            sel_b[r, dr + 1] = 1.0
    sel_a, sel_b = jnp.asarray(sel_a), jnp.asarray(sel_b)
    pen2 = jnp.asarray(pen2.reshape(1, GRID_W * 128))
    hi = lax.Precision.HIGHEST

    def body(rpb_ref, sa_ref, sb_ref, ea_ref, eb_ref, pen_ref, o_ref, ra_s, rb_s):
        for h in range(NA_HEADS):
            rp = rpb_ref[h]
            ra_s[h * BT_LEN:(h + 1) * BT_LEN, :] = jnp.dot(sa_ref[...], rp, preferred_element_type=F32, precision=hi)
            rb_s[h * BT_LEN:(h + 1) * BT_LEN, :] = jnp.dot(sb_ref[...], rp, preferred_element_type=F32, precision=hi)
        o_ref[...] = (jnp.dot(ra_s[...], ea_ref[...], preferred_element_type=F32, precision=hi)
                      + jnp.dot(rb_s[...], eb_ref[...], preferred_element_type=F32, precision=hi) + pen_ref[...])

    tcol = 2048
    rows = NA_HEADS * BT_LEN
    res, extra = _call(
        body, name="bias_table", grid=(GRID_W * 128 // tcol,),
        in_specs=[_full((NA_HEADS, n_dr, 31)), _full((BT_LEN, n_dr)), _full((BT_LEN, n_dr)),
                  pl.BlockSpec((31, tcol), lambda j: (0, j)), pl.BlockSpec((31, tcol), lambda j: (0, j)),
                  pl.BlockSpec((1, tcol), lambda j: (0, j))],
        out_specs=[pl.BlockSpec((rows, tcol), lambda j: (0, j))],
        out_shape=[jax.ShapeDtypeStruct((rows, GRID_W * 128), F32)],
        scratch_shapes=[pltpu.VMEM((rows, 31), F32), pltpu.VMEM((rows, 31), F32)], sem=("parallel",),
        args=(rpb2, sel_a, sel_b, ea, eb, pen2), comm=comm)
    return res[0].reshape(NA_HEADS, BT_LEN, GRID_W, 128), extra


def _key_window(j):
    ws = jnp.clip(Q_ROWS * j - 4, 0, GRID_ROWS - KEY_ROWS)
    return ws, pl.multiple_of(CTX_LEN + ws * GRID_W, 256)


def _head_mask(hh):
    lane = lax.broadcasted_iota(jnp.int32, (Q_TILE, 128), 1)
    return (lane < HEAD_DIM) if hh == 0 else (lane >= HEAD_DIM)


def _attn_scores(j, ws, q_rot_h, q_pl_h, kw, kc, hh, bt_ref, s_ref):
    s_ref[:, :KEY_TILE] = _dot_nt(q_rot_h, kw)
    s_ref[:, KEY_TILE:] = _dot_nt(q_pl_h, kc)
    lane = lax.broadcasted_iota(jnp.int32, (GRID_W, 128), 1)
    base = ws - Q_ROWS * j + (NA_ROWS - 1) + BT_PAD
    for qi in range(Q_ROWS):
        rs = jnp.clip(Q_ROWS * j + qi - NA_ROWS // 2, 0, GRID_ROWS - NA_ROWS)
        for m in range(KEY_ROWS // 2):
            k0 = ws + 2 * m
            p0 = jnp.where((k0 >= rs) & (k0 < rs + NA_ROWS), 0.0, NEG_INF)
            p1 = jnp.where((k0 + 1 >= rs) & (k0 + 1 < rs + NA_ROWS), 0.0, NEG_INF)
            pen = jnp.where(lane < GRID_W, p0, p1)
            rows = slice(qi * GRID_W, (qi + 1) * GRID_W)
            cols = slice(128 * m, 128 * (m + 1))
            s_ref[rows, cols] = s_ref[rows, cols] + bt_ref[hh, base + 2 * m - qi] + pen
    return base


def attn_fwd(q_rot, q_pl, kk, vv, bt, comm=None):
    def body(qr_ref, qp_ref, kk_ref, vv_ref, bt_ref, o_ref, lse_ref, s_ref):
        j = pl.program_id(1)
        ws, start = _key_window(j)
        win = pl.ds(start, KEY_TILE)
        kw, kc = kk_ref[win, :], kk_ref[:CTX_LEN, :]
        vw, vc = vv_ref[win, :], vv_ref[:CTX_LEN, :]
        qr, qp = qr_ref[...], qp_ref[...]
        outs = []
        for hh in range(2):
            msk = _head_mask(hh)
            _attn_scores(j, ws, jnp.where(msk, qr, 0), jnp.where(msk, qp, 0), kw, kc, hh, bt_ref, s_ref)
            s = s_ref[...]
            mx = jnp.max(s, axis=-1, keepdims=True)
            pr = jnp.exp(s - mx)
            l = jnp.sum(pr, axis=-1, keepdims=True)
            prb = pr.astype(BF16)
            o = jnp.dot(prb[:, :KEY_TILE], vw, preferred_element_type=F32)
            o = o + jnp.dot(prb[:, KEY_TILE:], vc, preferred_element_type=F32)
            outs.append(o / l)
            lse_ref[hh] = mx + jnp.log(l)
        o_ref[...] = jnp.where(_head_mask(0), outs[0], outs[1])

    qspec = pl.BlockSpec((Q_TILE, 128), lambda hp, j: (j + 1, hp))
    kspec = pl.BlockSpec((ZLEN, 128), lambda hp, j: (0, hp))
    res, extra = _call(
        body, name="attn_fwd", grid=(NA_HEADS // 2, SEQ // Q_TILE),
        in_specs=[qspec, qspec, kspec, kspec, pl.BlockSpec((2, BT_LEN, GRID_W, 128), lambda hp, j: (hp, 0, 0, 0))],
        out_specs=[pl.BlockSpec((Q_TILE, 128), lambda hp, j: (j, hp)),
                   pl.BlockSpec((2, Q_TILE, 1), lambda hp, j: (hp, j, 0))],
        out_shape=[jax.ShapeDtypeStruct((SEQ, D_MODEL), F32), jax.ShapeDtypeStruct((NA_HEADS, SEQ, 1), F32)],
        scratch_shapes=[pltpu.VMEM((Q_TILE, KEY_TILE + CTX_LEN), F32)], sem=("parallel", "arbitrary"),
        args=(q_rot, q_pl, kk, vv, bt), comm=comm)
    return res[0], res[1], extra


def merge_fwd(y_rnn, y_na, p, z, g2, w_rnn, w_na, w_out):
    def body(yr_ref, yn_ref, mr_ref, mn_ref, x_ref, g2_ref, wr_ref, wn_ref, wo_ref, u_ref, v_ref, mg_ref, out_ref, x1_ref):
        u = jnp.dot(yr_ref[...], wr_ref[...], preferred_element_type=F32)
        v = jnp.dot(yn_ref[...].astype(BF16), wn_ref[...], preferred_element_type=F32)
        merged = (_sigmoid(mr_ref[...]) * u + _sigmoid(mn_ref[...]) * v).astype(BF16)
        out = jnp.dot(merged, wo_ref[...], preferred_element_type=F32)
        u_ref[...] = u
        v_ref[...] = v
        mg_ref[...] = merged
        out_ref[...] = out
        x1_ref[...] = x_ref[...] + g2_ref[...] * out

    row = pl.BlockSpec((ROW_TILE, D_MODEL), lambda i: (i, 0))
    lat = lambda cb: pl.BlockSpec((ROW_TILE, D_MODEL), lambda i: (i + 1, cb))
    wspec = _full((D_MODEL, D_MODEL))
    f32o = jax.ShapeDtypeStruct((SEQ, D_MODEL), F32)
    return pl.pallas_call(
        body, name="merge_fwd", grid=(SEQ // ROW_TILE,),
        in_specs=[row, row, lat(5), lat(6), lat(0), _full((1, D_MODEL)), wspec, wspec, wspec],
        out_specs=[row] * 5,
        out_shape=[f32o, f32o, jax.ShapeDtypeStruct((SEQ, D_MODEL), BF16), f32o, f32o],
        compiler_params=_params("parallel"),
    )(y_rnn, y_na, p, p, z, g2, w_rnn, w_na, w_out)


FF_TILE = 256
FF_TILES = D_FF // FF_TILE


FF_ROWS = 64
FF_HALO = 8
FF_SLAB = FF_ROWS + 2 * FF_HALO


def _ffn_row_chunks(chunk, init):
    carry = chunk(0, 0, -1, init)
    carry = lax.fori_loop(1, SEQ // FF_ROWS - 1,
                          lambda ci, cr: chunk(pl.multiple_of(ci * FF_ROWS - FF_HALO, 8), FF_HALO, 0, cr), carry)
    return chunk(SEQ - FF_SLAB, 2 * FF_HALO, 1, carry)


def _ffn_shifts(edge):
    row = _row_ids(FF_SLAB, FF_TILE)

    def prev(x):
        r = pltpu.roll(x, 1, 0)
        return jnp.where(row >= 1, r, 0.0) if edge == -1 else r

    def nxt(x):
        r = pltpu.roll(x, FF_SLAB - 1, 0)
        return jnp.where(row < FF_SLAB - 1, r, 0.0) if edge == 1 else r

    return prev, nxt


def ffn_act(hpre, conv_w, conv_b):
    def body(ha_ref, hg_ref, wa_ref, wg_ref, ba_ref, bg_ref, o_ref):
        wa, wg, ba, bg = wa_ref[...], wg_ref[...], ba_ref[...], bg_ref[...]

        def chunk(lo, mid, edge, carry):
            prev, nxt = _ffn_shifts(edge)
            ha, hg = ha_ref[pl.ds(lo, FF_SLAB), :], hg_ref[pl.ds(lo, FF_SLAB), :]
            a = prev(ha) * wa[0:1] + ha * wa[1:2] + nxt(ha) * wa[2:3] + ba
            g = prev(hg) * wg[0:1] + hg * wg[1:2] + nxt(hg) * wg[2:3] + bg
            o_ref[pl.ds(lo + mid, FF_ROWS), :] = (a * _sigmoid(a) * g)[mid:mid + FF_ROWS].astype(BF16)
            return carry

        _ffn_row_chunks(chunk, 0)

    col = lambda rows, off: pl.BlockSpec((rows, FF_TILE), lambda j: (0, j + off))
    return pl.pallas_call(
        body, name="ffn_act", grid=(FF_TILES,),
        in_specs=[col(SEQ, 0), col(SEQ, FF_TILES), col(3, 0), col(3, FF_TILES), col(1, 0), col(1, FF_TILES)],
        out_specs=col(SEQ, 0),
        out_shape=jax.ShapeDtypeStruct((SEQ, D_FF), BF16),
        compiler_params=_params("parallel"),
    )(hpre, hpre, conv_w, conv_w, conv_b, conv_b)


def ffn_down_loss(act, w_down, x1, g5, target):
    def body(a_ref, w_ref, x1_ref, g5_ref, t_ref, f_ref, dy_ref, df_ref, ls_ref, dg_ref):
        i = pl.program_id(0)
        f = jnp.dot(a_ref[...], w_ref[...], preferred_element_type=F32)
        g5 = g5_ref[...]
        err = x1_ref[...] + g5 * f - t_ref[...]
        dy = err * (1.0 / D_MODEL)
        f_ref[...] = f
        dy_ref[...] = dy
        df_ref[...] = (dy * g5).astype(BF16)

        @pl.when(i == 0)
        def _():
            ls_ref[...] = jnp.zeros_like(ls_ref)
            dg_ref[...] = jnp.zeros_like(dg_ref)

        ls_ref[...] = ls_ref[...] + jnp.sum(err * err)
        dg_ref[...] = dg_ref[...] + jnp.sum(dy * f, axis=0, keepdims=True)

    row = pl.BlockSpec((ROW_TILE, D_MODEL), lambda i: (i, 0))
    f32o = jax.ShapeDtypeStruct((SEQ, D_MODEL), F32)
    return pl.pallas_call(
        body, name="ffn_down_loss", grid=(SEQ // ROW_TILE,),
        in_specs=[pl.BlockSpec((ROW_TILE, D_FF), lambda i: (i, 0)), _full((D_FF, D_MODEL)), row, _full((1, D_MODEL)), row],
        out_specs=[row, row, row, _full((8, 128)), _full((1, D_MODEL))],
        out_shape=[f32o, f32o, jax.ShapeDtypeStruct((SEQ, D_MODEL), BF16), jax.ShapeDtypeStruct((8, 128), F32),
                   jax.ShapeDtypeStruct((1, D_MODEL), F32)],
        compiler_params=_params("arbitrary"),
    )(act, w_down, x1, g5, target)


def ffn_down_bwd(df, w_down):
    def body(df_ref, w_ref, o_ref):
        o_ref[...] = _dot_nt(df_ref[...], w_ref[...])

    return pl.pallas_call(
        body, name="ffn_down_bwd", grid=(SEQ // ROW_TILE,),
        in_specs=[pl.BlockSpec((ROW_TILE, D_MODEL), lambda i: (i, 0)), _full((D_FF, D_MODEL))],
        out_specs=pl.BlockSpec((ROW_TILE, D_FF), lambda i: (i, 0)),
        out_shape=jax.ShapeDtypeStruct((SEQ, D_FF), F32),
        compiler_params=_params("parallel"),
    )(df, w_down)


def ffn_act_bwd(hpre, d_act, conv_w, conv_b):
    def body(ha_ref, hg_ref, da_ref, wa_ref, wg_ref, ba_ref, bg_ref, dha_ref, dhg_ref, dwa_ref, dwg_ref, dba_ref, dbg_ref):
        wa, wg, ba, bg = wa_ref[...], wg_ref[...], ba_ref[...], bg_ref[...]

        def chunk(lo, mid, edge, acc):
            prev, nxt = _ffn_shifts(edge)
            rows = pl.ds(lo, FF_SLAB)
            ha, hg, dact = ha_ref[rows, :], hg_ref[rows, :], da_ref[rows, :]
            hap, han, hgp, hgn = prev(ha), nxt(ha), prev(hg), nxt(hg)
            a = hap * wa[0:1] + ha * wa[1:2] + han * wa[2:3] + ba
            g = hgp * wg[0:1] + hg * wg[1:2] + hgn * wg[2:3] + bg
            sig = _sigmoid(a)
            dca = dact * g * (sig * (1.0 + a * (1.0 - sig)))
            dcg = dact * a * sig
            m = slice(mid, mid + FF_ROWS)
            sums = []
            for dc, h, hp, hn, w, dh_ref in ((dca, ha, hap, han, wa, dha_ref), (dcg, hg, hgp, hgn, wg, dhg_ref)):
                dcm = dc[m]
                sums += [jnp.sum(dcm * hp[m], axis=0, keepdims=True), jnp.sum(dcm * h[m], axis=0, keepdims=True),
                         jnp.sum(dcm * hn[m], axis=0, keepdims=True), jnp.sum(dcm, axis=0, keepdims=True)]
                dh = nxt(dc) * w[0:1] + dc * w[1:2] + prev(dc) * w[2:3]
                dh_ref[pl.ds(lo + mid, FF_ROWS), :] = dh[m].astype(BF16)
            return tuple(x + y for x, y in zip(acc, sums))

        acc = _ffn_row_chunks(chunk, tuple(jnp.zeros((1, FF_TILE), F32) for _ in range(8)))
        dwa_ref[0:1, :], dwa_ref[1:2, :], dwa_ref[2:3, :], dba_ref[...] = acc[0], acc[1], acc[2], acc[3]
        dwg_ref[0:1, :], dwg_ref[1:2, :], dwg_ref[2:3, :], dbg_ref[...] = acc[4], acc[5], acc[6], acc[7]

    col = lambda rows, off: pl.BlockSpec((rows, FF_TILE), lambda j: (0, j + off))
    hshape = jax.ShapeDtypeStruct((SEQ, D_FF), BF16)
    wshape = jax.ShapeDtypeStruct((3, D_FF), F32)
    bshape = jax.ShapeDtypeStruct((1, D_FF), F32)
    return pl.pallas_call(
        body, name="ffn_act_bwd", grid=(FF_TILES,),
        in_specs=[col(SEQ, 0), col(SEQ, FF_TILES), col(SEQ, 0), col(3, 0), col(3, FF_TILES), col(1, 0), col(1, FF_TILES)],
        out_specs=[col(SEQ, 0), col(SEQ, 0), col(3, 0), col(3, 0), col(1, 0), col(1, 0)],
        out_shape=[hshape, hshape, wshape, wshape, bshape, bshape],
        compiler_params=_params("parallel"),
    )(hpre, hpre, d_act, conv_w, conv_w, conv_b, conv_b)


def _norm_mod_bwd(x, dxn, gain, scale):
    rstd = lax.rsqrt(jnp.mean(x * x, axis=-1, keepdims=True) + EPS)
    nrm = x * rstd
    dsh = jnp.sum(dxn, axis=0, keepdims=True)
    dsc = jnp.sum(dxn * nrm, axis=0, keepdims=True) * gain
    dgn = jnp.sum(dxn * nrm, axis=0, keepdims=True) * (1.0 + scale)
    dn = dxn * (gain * (1.0 + scale))
    dx = rstd * (dn - nrm * jnp.mean(dn * nrm, axis=-1, keepdims=True))
    return dx, dsh, dsc, dgn


def ffn_up_bwd(dha, dhg, w_up, x1, dy, gain, scale):
    def body(dha_ref, dhg_ref, w_ref, x_ref, dy_ref, g_ref, sc_ref, dx_ref, dsh_ref, dsc_ref, dgn_ref):
        i = pl.program_id(0)
        dxn = _dot_nt(dha_ref[...], w_ref[:, :D_FF]) + _dot_nt(dhg_ref[...], w_ref[:, D_FF:])
        dx, dsh, dsc, dgn = _norm_mod_bwd(x_ref[...], dxn, g_ref[...], sc_ref[...])
        dx_ref[...] = dy_ref[...] + dx

        @pl.when(i == 0)
        def _():
            dsh_ref[...] = dsh
            dsc_ref[...] = dsc
            dgn_ref[...] = dgn

        @pl.when(i > 0)
        def _():
            dsh_ref[...] = dsh_ref[...] + dsh
            dsc_ref[...] = dsc_ref[...] + dsc
            dgn_ref[...] = dgn_ref[...] + dgn

    row = pl.BlockSpec((ROW_TILE, D_MODEL), lambda i: (i, 0))
    vec = _full((1, D_MODEL))
    vshape = jax.ShapeDtypeStruct((1, D_MODEL), F32)
    return pl.pallas_call(
        body, name="ffn_up_bwd", grid=(SEQ // ROW_TILE,),
        in_specs=[pl.BlockSpec((ROW_TILE, D_FF), lambda i: (i, 0)), pl.BlockSpec((ROW_TILE, D_FF), lambda i: (i, 0)),
                  _full((D_MODEL, 2 * D_FF)), row, row, vec, vec],
        out_specs=[row, vec, vec, vec],
        out_shape=[jax.ShapeDtypeStruct((SEQ, D_MODEL), F32), vshape, vshape, vshape],
        compiler_params=_params("arbitrary"),
    )(dha, dhg, w_up, x1, dy, gain, scale)


def merge_bwd(dx1, out, g2, p, u, v, w_rnn, w_na, w_out, comm=None):
    def body(dx_ref, out_ref, g2_ref, mr_ref, mn_ref, u_ref, v_ref, wr_ref, wn_ref, wo_ref,
             dout_ref, du_ref, dv_ref, dmr_ref, dmn_ref, dyr_ref, dyn_ref, dg2_ref):
        i = pl.program_id(0)

        @pl.when(i == 0)
        def _():
            dmr_ref[...] = jnp.zeros_like(dmr_ref)
            dmn_ref[...] = jnp.zeros_like(dmn_ref)
            dg2_ref[...] = jnp.zeros_like(dg2_ref)

        @pl.when(i > 0)
        def _():
            dx = dx_ref[...]
            dg2_ref[...] = dg2_ref[...] + jnp.sum(dx * out_ref[...], axis=0, keepdims=True)
            dout = (dx * g2_ref[...]).astype(BF16)
            dout_ref[...] = dout
            dm = _dot_nt(dout, wo_ref[...])
            sr = _sigmoid(mr_ref[...])
            sn = _sigmoid(mn_ref[...])
            du = (dm * sr).astype(BF16)
            dv = (dm * sn).astype(BF16)
            du_ref[...] = du
            dv_ref[...] = dv
            dmr_ref[...] = (dm * u_ref[...] * (sr * (1.0 - sr))).astype(BF16)
            dmn_ref[...] = (dm * v_ref[...] * (sn * (1.0 - sn))).astype(BF16)
            dyr_ref[...] = _dot_nt(du, wr_ref[...])
            dyn_ref[...] = _dot_nt(dv, wn_ref[...])

    lat = pl.BlockSpec((ROW_TILE, D_MODEL), lambda i: (jnp.maximum(i - 1, 0), 0))
    zrow = pl.BlockSpec((ROW_TILE, D_MODEL), lambda i: (i, 0))
    pcol = lambda cb: pl.BlockSpec((ROW_TILE, D_MODEL), lambda i: (i, cb))
    wspec = _full((D_MODEL, D_MODEL))
    tb = jax.ShapeDtypeStruct((SEQ, D_MODEL), BF16)
    zb = jax.ShapeDtypeStruct((ZLEN, D_MODEL), BF16)
    tf = jax.ShapeDtypeStruct((SEQ, D_MODEL), F32)
    res, extra = _call(
        body, name="merge_bwd", grid=(ZLEN // ROW_TILE,),
        in_specs=[lat, lat, _full((1, D_MODEL)), pcol(5), pcol(6), lat, lat, wspec, wspec, wspec],
        out_specs=[lat, lat, lat, zrow, zrow, lat, lat, _full((1, D_MODEL))],
        out_shape=[tb, tb, tb, zb, zb, tf, tf, jax.ShapeDtypeStruct((1, D_MODEL), F32)],
        sem=("arbitrary",), args=(dx1, out, g2, p, p, u, v, w_rnn, w_na, w_out), comm=comm)
    return (*res, extra)


def attn_bwd(q_rot, q_pl, kk, vv, bt, y_na, d_yna, lse, comm=None):
    def body(qr_ref, qp_ref, kk_ref, vv_ref, bt_ref, o_ref, do_ref, lse_ref,
             dqr_ref, dqp_ref, dk_ref, dv_ref, dbt_ref, s_ref):
        jj = pl.program_id(1)

        @pl.when(jj == 0)
        def _():
            dqr_ref[...] = jnp.zeros_like(dqr_ref)
            dqp_ref[...] = jnp.zeros_like(dqp_ref)
            dk_ref[...] = jnp.zeros_like(dk_ref)
            dv_ref[...] = jnp.zeros_like(dv_ref)
            dbt_ref[...] = jnp.zeros_like(dbt_ref)

        @pl.when(jj > 0)
        def _():
            j = jj - 1
            ws, start = _key_window(j)
            win = pl.ds(start, KEY_TILE)
            kw, kc = kk_ref[win, :], kk_ref[:CTX_LEN, :]
            vw, vc = vv_ref[win, :], vv_ref[:CTX_LEN, :]
            qr, qp = qr_ref[...], qp_ref[...]
            do = do_ref[...]
            do_o = do * o_ref[...]
            dq_r, dq_p = [], []
            for hh in range(2):
                msk = _head_mask(hh)
                q_r, q_p = jnp.where(msk, qr, 0), jnp.where(msk, qp, 0)
                base = _attn_scores(j, ws, q_r, q_p, kw, kc, hh, bt_ref, s_ref)
                pr = jnp.exp(s_ref[...] - lse_ref[hh])
                delta = jnp.sum(jnp.where(msk, do_o, 0.0), axis=-1, keepdims=True)
                dob = jnp.where(msk, do, 0.0).astype(BF16)
                ds_lat = pr[:, :KEY_TILE] * (_dot_nt(dob, vw) - delta)
                ds_ctx = pr[:, KEY_TILE:] * (_dot_nt(dob, vc) - delta)
                for qi in range(Q_ROWS):
                    for m in range(KEY_ROWS // 2):
                        idx = base + 2 * m - qi
                        dbt_ref[hh, idx] = dbt_ref[hh, idx] + ds_lat[qi * GRID_W:(qi + 1) * GRID_W, 128 * m:128 * (m + 1)]
                dsb_lat = ds_lat.astype(BF16)
                dsb_ctx = ds_ctx.astype(BF16)
                prb = pr.astype(BF16)
                dq_r.append(jnp.dot(dsb_lat, kw, preferred_element_type=F32))
                dq_p.append(jnp.dot(dsb_ctx, kc, preferred_element_type=F32))
                dk_ref[win, :] = dk_ref[win, :] + _dot_tn(dsb_lat, q_r)
                dk_ref[:CTX_LEN, :] = dk_ref[:CTX_LEN, :] + _dot_tn(dsb_ctx, q_p)
                dv_ref[win, :] = dv_ref[win, :] + _dot_tn(prb[:, :KEY_TILE], dob)
                dv_ref[:CTX_LEN, :] = dv_ref[:CTX_LEN, :] + _dot_tn(prb[:, KEY_TILE:], dob)
            dqr_ref[...] = jnp.where(_head_mask(0), dq_r[0], dq_r[1])
            dqp_ref[...] = jnp.where(_head_mask(0), dq_p[0], dq_p[1])

    lat = lambda jj: jnp.maximum(jj - 1, 0)
    qspec = pl.BlockSpec((Q_TILE, 128), lambda hp, jj: (lat(jj) + 1, hp))
    kspec = pl.BlockSpec((ZLEN, 128), lambda hp, jj: (0, hp))
    btspec = pl.BlockSpec((2, BT_LEN, GRID_W, 128), lambda hp, jj: (hp, 0, 0, 0))
    ospec = pl.BlockSpec((Q_TILE, 128), lambda hp, jj: (lat(jj), hp))
    dqspec = pl.BlockSpec((Q_TILE, 128), lambda hp, jj: (jj, hp))
    zshape = jax.ShapeDtypeStruct((ZLEN, D_MODEL), F32)
    res, extra = _call(
        body, name="attn_bwd", grid=(NA_HEADS // 2, ZLEN // Q_TILE),
        in_specs=[qspec, qspec, kspec, kspec, btspec, ospec, ospec,
                  pl.BlockSpec((2, Q_TILE, 1), lambda hp, jj: (hp, lat(jj), 0))],
        out_specs=[dqspec, dqspec, kspec, kspec, btspec],
        out_shape=[zshape, zshape, zshape, zshape, jax.ShapeDtypeStruct((NA_HEADS, BT_LEN, GRID_W, 128), F32)],
        scratch_shapes=[pltpu.VMEM((Q_TILE, KEY_TILE + CTX_LEN), F32)], sem=("parallel", "arbitrary"),
        args=(q_rot, q_pl, kk, vv, bt, y_na, d_yna, lse), comm=comm)
    return (*res, extra)


def qkv_bwd(dq_rot, dq_pl, dk, dv, p, qg2, kg2, cos, sin, ones, comm=None):
    scale = HEAD_DIM ** -0.5
    n_hp, n_i = NA_HEADS // 2, ZLEN // PREP_TILE

    def norm_rope_bwd(d_rot, d_extra, x, gain, cos_t, sin_t, ones_m, dx_ref, acc_ref):
        xh, rstd = _head_rms(x, ones_m, 1.0)
        dn = d_rot * cos_t + _rope_partner(d_rot * sin_t)
        if d_extra is not None:
            dn = (dn + d_extra) * scale
        acc_ref[...] = acc_ref[...] + jnp.sum(dn * xh, axis=0, keepdims=True)
        dxh = dn * gain
        seg = _head_sum(dxh * xh, ones_m) * (1.0 / HEAD_DIM)
        dx_ref[...] = (rstd * (dxh - xh * seg)).astype(BF16)

    def body(dqr_ref, dqp_ref, dk_ref, dv_ref, xq_ref, xk_ref, qg_ref, kg_ref, cos_ref, sin_ref, ones_ref,
             dxq_ref, dxk_ref, dxv_ref, dgq_ref, dgk_ref, accq_ref, acck_ref):
        hp, i = pl.program_id(0), pl.program_id(1)

        @pl.when((hp == 0) & (i == 0))
        def _():
            accq_ref[...] = jnp.zeros_like(accq_ref)
            acck_ref[...] = jnp.zeros_like(acck_ref)

        ones_m = ones_ref[...]
        cos_t, sin_t = cos_ref[...], sin_ref[...]
        norm_rope_bwd(dqr_ref[...], dqp_ref[...], xq_ref[...], qg_ref[...], cos_t, sin_t, ones_m, dxq_ref, accq_ref)
        norm_rope_bwd(dk_ref[...], None, xk_ref[...], kg_ref[...], cos_t, sin_t, ones_m, dxk_ref, acck_ref)
        dxv_ref[...] = dv_ref[...].astype(BF16)

        @pl.when((hp == n_hp - 1) & (i == n_i - 1))
        def _():
            dgq_ref[...] = accq_ref[:, :HEAD_DIM] + accq_ref[:, HEAD_DIM:]
            dgk_ref[...] = acck_ref[:, :HEAD_DIM] + acck_ref[:, HEAD_DIM:]

    col = lambda base: pl.BlockSpec((PREP_TILE, 128), lambda hp, i: (i, base + hp))
    small = pl.BlockSpec((1, 128), lambda hp, i: (0, 0))
    tab = pl.BlockSpec((PREP_TILE, 128), lambda hp, i: (i, 0))
    zb = jax.ShapeDtypeStruct((ZLEN, D_MODEL), BF16)
    gshape = jax.ShapeDtypeStruct((1, HEAD_DIM), F32)
    res, extra = _call(
        body, name="qkv_bwd", grid=(n_hp, n_i),
        in_specs=[col(0)] * 4 + [col(32), col(8), small, small, tab, tab, _full((128, 128))],
        out_specs=[col(0)] * 3 + [_full((1, HEAD_DIM))] * 2,
        out_shape=[zb, zb, zb, gshape, gshape],
        scratch_shapes=[pltpu.VMEM((1, 128), F32)] * 2, sem=("arbitrary", "arbitrary"),
        args=(dq_rot, dq_pl, dk, dv, p, p, qg2, kg2, cos, sin, ones), comm=comm)
    return (*res, extra)


def rpb_grad(dbt):
    e, _ = _bias_expand()
    n_dr = 2 * NA_ROWS - 1
    ea = np.zeros((31, GRID_W, 128), np.float32)
    ea[:, :, :GRID_W] = e
    eb = np.zeros((31, GRID_W, 128), np.float32)
    eb[:, :, GRID_W:] = e
    eat = jnp.asarray(ea.reshape(31, GRID_W * 128).T.copy())
    ebt = jnp.asarray(eb.reshape(31, GRID_W * 128).T.copy())
    sel_at = np.zeros((n_dr, BT_LEN), np.float32)
    sel_bt = np.zeros((n_dr, BT_LEN), np.float32)
    for r in range(BT_LEN):
        dr = r - BT_PAD
        if 0 <= dr < n_dr:
            sel_at[dr, r] = 1.0
        if 0 <= dr + 1 < n_dr:
            sel_bt[dr + 1, r] = 1.0
    hi = lax.Precision.HIGHEST

    tk = 2048
    wide = GRID_W * 128
    rows = NA_HEADS * BT_LEN
    n_k = wide // tk

    def body(d_ref, sa_ref, sb_ref, ea_ref, eb_ref, o_ref, a_s, b_s):
        k = pl.program_id(0)
        dm = d_ref[...]
        d_hi = dm.astype(BF16)
        rest = dm - d_hi.astype(F32)
        d_mid = rest.astype(BF16)
        d_lo = (rest - d_mid.astype(F32)).astype(BF16)
        ea_b, eb_b = ea_ref[...].astype(BF16), eb_ref[...].astype(BF16)
        a = sum(jnp.dot(t, ea_b, preferred_element_type=F32) for t in (d_hi, d_mid, d_lo))
        b = sum(jnp.dot(t, eb_b, preferred_element_type=F32) for t in (d_hi, d_mid, d_lo))

        @pl.when(k == 0)
        def _():
            a_s[...] = a
            b_s[...] = b

        @pl.when(k > 0)
        def _():
            a_s[...] = a_s[...] + a
            b_s[...] = b_s[...] + b

        @pl.when(k == n_k - 1)
        def _():
            for h in range(NA_HEADS):
                sl = slice(h * BT_LEN, (h + 1) * BT_LEN)
                o_ref[h] = (jnp.dot(sa_ref[...], a_s[sl, :], preferred_element_type=F32, precision=hi)
                            + jnp.dot(sb_ref[...], b_s[sl, :], preferred_element_type=F32, precision=hi))

    return pl.pallas_call(
        body, name="rpb_grad", grid=(n_k,),
        in_specs=[pl.BlockSpec((rows, tk), lambda k: (0, k)), _full((n_dr, BT_LEN)), _full((n_dr, BT_LEN)),
                  pl.BlockSpec((tk, 31), lambda k: (k, 0)), pl.BlockSpec((tk, 31), lambda k: (k, 0))],
        out_specs=_full((NA_HEADS, n_dr, 31)),
        out_shape=jax.ShapeDtypeStruct((NA_HEADS, n_dr, 31), F32),
        scratch_shapes=[pltpu.VMEM((rows, 31), F32), pltpu.VMEM((rows, 31), F32)],
        compiler_params=_params("arbitrary"),
    )(dbt.reshape(rows, wide), jnp.asarray(sel_at), jnp.asarray(sel_bt), eat, ebt)


def lru_bwd(p, d_yrnn, conv_w, conv_b, wa, ba, wx, bx, lam, comm=None):
    blk, wspec = _lru_in_specs()

    def body(xr_ref, gx_ref, dy_ref, cw_ref, cb_ref, wa_ref, ba_ref, wx_ref, bx_ref, lam_ref,
             dxr_ref, dgx_ref, dcw_ref, dcb_ref, dwa_ref, dba_ref, dwx_ref, dbx_ref, dlam_ref,
             a_s, b_s, h_s, l_s, hsum_s, dxc_s, dh_s):
        xr = xr_ref[...]
        cw = cw_ref[...]
        xc = _lru_conv(xr, cw, cb_ref[...])
        xcb = xc.astype(BF16)
        g, dg = _gelu_parts(gx_ref[CTX_LEN:, :])
        dy = dy_ref[...]
        dh_s[:CTX_LEN, :] = jnp.zeros((CTX_LEN, LRU_BLOCK_W), F32)
        dh_s[CTX_LEN:, :] = dy * g
        row = _row_ids(ZLEN, LRU_BLOCK_W)
        zero = jnp.zeros((1, LRU_BLOCK_W), F32)
        for d in (0, 1):
            wab = wa_ref[d, 0].astype(BF16)
            wxb = wx_ref[d, 0].astype(BF16)
            lam_d = lam_ref[d:d + 1, :]
            r, gi, sp, a, sq, b = _lru_gates(xc, xcb, wab, ba_ref[d:d + 1, :], wxb, bx_ref[d:d + 1, :], lam_d)
            a_s[...] = a
            b_s[...] = b
            _lru_scan_dir(d, a_s, b_s, h_s)
            h = h_s[...]
            if d == 0:
                hsum_s[...] = h
                h_prev = jnp.where(row >= 1, pltpu.roll(h, 1, 0), 0.0)
                a_s[...] = pltpu.roll(a, ZLEN - 1, 0)
                _scan_down(a_s, dh_s, l_s, 0, N_CHUNK, zero)
            else:
                hsum_s[...] = hsum_s[...] + h
                h_prev = jnp.where(row == CTX_LEN - 1, 0.0, pltpu.roll(h, ZLEN - 1, 0))
                a_s[...] = pltpu.roll(a, 1, 0)
                c = _scan_up(a_s, dh_s, l_s, CTX_CHUNKS, N_CHUNK, zero)
                _scan_up(a_s, dh_s, l_s, 0, CTX_CHUNKS, c)
            db = l_s[...]
            da = db * h_prev
            dsq = db * gi * xc
            dgi = db * sq * xc
            dxc_d = db * sq * gi
            dla = da * a - dsq * (a * a) / sq
            dr = dla * ((-LRU_C) * sp)
            dsp = jnp.sum(dla * ((-LRU_C) * r), axis=0, keepdims=True)
            dlam_ref[d:d + 1, :] = -dsp * _sigmoid(-lam_d)
            dzr = dr * r * (1.0 - r)
            dzi = dgi * gi * (1.0 - gi)
            dba_ref[d:d + 1, :] = jnp.sum(dzr, axis=0, keepdims=True)
            dbx_ref[d:d + 1, :] = jnp.sum(dzi, axis=0, keepdims=True)
            dzrb = dzr.astype(BF16)
            dzib = dzi.astype(BF16)
            dwa_ref[d, 0] = _dot_tn(xcb, dzrb)
            dwx_ref[d, 0] = _dot_tn(xcb, dzib)
            dxc_d = dxc_d + _dot_nt(dzrb, wab) + _dot_nt(dzib, wxb)
            if d == 0:
                dxc_s[...] = dxc_d
            else:
                dxc_s[...] = dxc_s[...] + dxc_d
        dxc = dxc_s[...]
        dxr_ref[...] = _lru_conv_t(dxc, cw).astype(BF16)
        dcb_ref[...] = jnp.sum(dxc, axis=0, keepdims=True)
        segpos = jnp.where(row < CTX_LEN, row, row - CTX_LEN)
        seglen = jnp.where(row < CTX_LEN, CTX_LEN, SEQ)
        for k in range(4):
            off = k - 2
            if off == 0:
                sh = xr
            else:
                ok = (segpos + off >= 0) & (segpos + off < seglen)
                sh = jnp.where(ok, pltpu.roll(xr, (-off) % ZLEN, 0), 0.0)
            dcw_ref[k:k + 1, :] = jnp.sum(dxc * sh, axis=0, keepdims=True)
        dgx_ref[:CTX_LEN, :] = jnp.zeros((CTX_LEN, LRU_BLOCK_W), BF16)
        dgx_ref[CTX_LEN:, :] = (dy * hsum_s[CTX_LEN:, :] * dg).astype(BF16)

    zs = pltpu.VMEM((ZLEN, LRU_BLOCK_W), F32)
    zb = jax.ShapeDtypeStruct((ZLEN, D_MODEL), BF16)
    v2 = jax.ShapeDtypeStruct((2, D_MODEL), F32)
    w4 = jax.ShapeDtypeStruct((2, LRU_BLOCKS, LRU_BLOCK_W, LRU_BLOCK_W), F32)
    res, extra = _call(
        body, name="lru_bwd", grid=(LRU_BLOCKS,),
        in_specs=[blk(ZLEN), pl.BlockSpec((ZLEN, LRU_BLOCK_W), lambda b: (0, 24 + b)), blk(SEQ), blk(4), blk(1),
                  wspec, blk(2), wspec, blk(2), blk(2)],
        out_specs=[blk(ZLEN), blk(ZLEN), blk(4), blk(1), wspec, blk(2), wspec, blk(2), blk(2)],
        out_shape=[zb, zb, jax.ShapeDtypeStruct((4, D_MODEL), F32), jax.ShapeDtypeStruct((1, D_MODEL), F32),
                   w4, v2, w4, v2, v2],
        scratch_shapes=[zs] * 7, sem=("arbitrary",),
        args=(p, p, d_yrnn, conv_w, conv_b, wa, ba, wx, bx, lam), comm=comm)
    return (*res, extra)


def in_proj_bwd(dgs, w_in, z, dx1, gain, scale, comm=None):
    def body(*refs):
        dg_refs = refs[:7]
        w_ref, z_ref, dx1_ref, g_ref, sc_ref, gx_ref, dsh_ref, dsc_ref, dgn_ref = refs[7:]
        i = pl.program_id(0)
        dxn = _dot_nt(dg_refs[0][...], w_ref[:, 0:D_MODEL])
        for g in range(1, 7):
            dxn = dxn + _dot_nt(dg_refs[g][...], w_ref[:, g * D_MODEL:(g + 1) * D_MODEL])
        dx, dsh, dsc, dgn = _norm_mod_bwd(z_ref[...], dxn, g_ref[...], sc_ref[0])

        @pl.when(i <= 1)
        def _():
            dsh_ref[0] = dsh
            dsc_ref[0] = dsc

        @pl.when(i > 1)
        def _():
            dsh_ref[0] = dsh_ref[0] + dsh
            dsc_ref[0] = dsc_ref[0] + dsc

        @pl.when(i == 0)
        def _():
            dgn_ref[...] = dgn

        @pl.when(i > 0)
        def _():
            dgn_ref[...] = dgn_ref[...] + dgn
            gx_ref[...] = dx1_ref[...] + dx

    zrow = pl.BlockSpec((ROW_TILE, D_MODEL), lambda i: (i, 0))
    lat = pl.BlockSpec((ROW_TILE, D_MODEL), lambda i: (jnp.maximum(i - 1, 0), 0))
    mod = pl.BlockSpec((1, 1, D_MODEL), lambda i: (jnp.minimum(i, 1), 0, 0))
    mshape = jax.ShapeDtypeStruct((2, 1, D_MODEL), F32)
    res, extra = _call(
        body, name="in_proj_bwd", grid=(ZLEN // ROW_TILE,),
        in_specs=[zrow] * 7 + [_full((D_MODEL, IN_COLS)), zrow, lat, _full((1, D_MODEL)), mod],
        out_specs=[lat, mod, mod, _full((1, D_MODEL))],
        out_shape=[jax.ShapeDtypeStruct((SEQ, D_MODEL), F32), mshape, mshape, jax.ShapeDtypeStruct((1, D_MODEL), F32)],
        sem=("arbitrary",), args=(*dgs, w_in, z, dx1, gain, scale), comm=comm)
    return (*res, extra)


def matmul_tn(a, b, name, tm, tn, prev=None, col_block=0, total_cols=None):
    k, m = a.shape
    n = b.shape[1]
    total_cols = n if total_cols is None else total_cols
    assert m % tm == 0 and n % tn == 0
    off = col_block * (n // tn)

    def body(a_ref, b_ref, *rest):
        rest[-1][...] = _dot_tn(a_ref[...].astype(BF16), b_ref[...]).astype(BF16)

    in_specs = [pl.BlockSpec((k, tm), lambda i, j: (0, i)), pl.BlockSpec((k, tn), lambda i, j: (0, j))]
    args = [a, b]
    aliases = {}
    if prev is not None:
        in_specs.append(pl.BlockSpec(memory_space=pl.ANY))
        args.append(prev)
        aliases = {2: 0}
    return pl.pallas_call(
        body, name=name, grid=(m // tm, n // tn), in_specs=in_specs,
        out_specs=pl.BlockSpec((tm, tn), lambda i, j: (i, j + off)),
        out_shape=jax.ShapeDtypeStruct((m, total_cols), BF16),
        input_output_aliases=aliases,
        compiler_params=_params("parallel", "parallel"),
    )(*args)


def local_step(z, target, modx, modc, norm_mix_g, norm_ffn_g, w_in, conv_w, conv_b, wa, ba, wx, bx, lam, qg, kg, rpb,
               w_rnn, w_na, w_out, w_up, fconv_w, fconv_b, w_down, idx=None, bt=None):
    dist = idx is not None
    c_idx = idx[1:2] if dist else None
    d = D_MODEL
    mx = [modx[:, k * d:(k + 1) * d] for k in range(N_MOD)]
    shift = jnp.stack([modc[:, 0:d], mx[0]])
    scale = jnp.stack([modc[:, d:2 * d], mx[1]])
    cos, sin = _rope_tables()
    ones = _head_ones()
    qg2 = jnp.tile(qg, (1, 2))
    kg2 = jnp.tile(kg, (1, 2))

    if bt is None:
        bt, _ = bias_table(rpb)
    xn, p, got = norm_matmul(z, norm_mix_g, shift, scale, w_in, "in_proj", 3 * ROW_TILE, 1792, ctx_rows=CTX_LEN,
                             comm=gather_weights_comm([w_rnn, w_na, w_out], [1, 2, 3]) if dist else None)
    if dist:
        w_rnn, w_na, w_out = got
    y_rnn, got = lru_fwd(p, conv_w, conv_b, wa, ba, wx, bx, lam,
                         comm=gather_weights_comm([w_down], [5]) if dist else None)
    if dist:
        w_down = got[0]
    q_rot, q_pl, kk, vv, _ = qkv_prep(p, qg2, kg2, cos, sin, ones)
    y_na, lse, got = attn_fwd(q_rot, q_pl, kk, vv, bt, comm=gather_weights_comm([w_up], [4]) if dist else None)
    if dist:
        w_up = got[0]
    u, v, merged, out, x1 = merge_fwd(y_rnn, y_na, p, z, mx[2], w_rnn, w_na, w_out)
    xn2, hpre, _ = norm_matmul(x1, norm_ffn_g, mx[3][None], mx[4][None], w_up, "ffn_up", 2 * ROW_TILE, 1408)
    act = ffn_act(hpre, fconv_w, fconv_b)
    f, dy, df, loss_sq, dg5 = ffn_down_loss(act, w_down, x1, mx[5], target)

    partials, pieces = {}, {}

    def views_of(which, grads):
        return [_grad_view(g, BIG[w][1], BIG[w][2]) for w, g in zip(which, grads)]

    def chip_partials(which, views, recv):
        for w, gv, r in zip(which, views, recv):
            partials[w] = add_halves(gv, r, c_idx, "add_halves_" + BIG[w][0])
        return scatter_pieces_comm([partials[w] for w in which], which)

    d_act = ffn_down_bwd(df, w_down)
    dha, dhg, d_fcw_a, d_fcw_g, d_fcb_a, d_fcb_g = ffn_act_bwd(hpre, d_act, fconv_w, fconv_b)
    d_fcw = jnp.concatenate([d_fcw_a, d_fcw_g], axis=1)
    d_fcb = jnp.concatenate([d_fcb_a, d_fcb_g], axis=1)
    dx1, d_s3, d_s4, d_gffn = ffn_up_bwd(dha, dhg, w_up, x1, dy, norm_ffn_g, mx[4])
    g_w_down = matmul_tn(act, df, "gw_down", 256, D_MODEL)
    g_w_up = matmul_tn(xn2, dha, "gw_up_a", 512, 1408, total_cols=2 * D_FF)
    g_w_up = matmul_tn(xn2, dhg, "gw_up_g", 512, 1408, prev=g_w_up, col_block=1, total_cols=2 * D_FF)
    v_ffn = views_of([4, 5], [g_w_up, g_w_down]) if dist else None
    *mb, got = merge_bwd(dx1, out, mx[2], p, u, v, w_rnn, w_na, w_out,
                         comm=exchange_halves_comm(v_ffn) if dist else None)
    dout, du, dv, dmr, dmn, dyr, dyn, dg2 = mb
    recv_ffn = got
    g_w_out = matmul_tn(merged, dout, "gw_out", 1024, 512)
    g_w_rnn = matmul_tn(y_rnn, du, "gw_rnn", 1024, 512)
    g_w_na = matmul_tn(y_na, dv, "gw_na", 1024, 512)
    v_mix = views_of([1, 2, 3], [g_w_rnn, g_w_na, g_w_out]) if dist else None
    *lru_grads, got = lru_bwd(p, dyr, conv_w, conv_b, wa, ba, wx, bx, lam,
                              comm=join_comms(chip_partials([4, 5], v_ffn, recv_ffn),
                                              exchange_halves_comm(v_mix)) if dist else None)
    dxr, dgx, d_cw, d_cb, d_wa, d_ba, d_wx, d_bx, d_lam = lru_grads
    if dist:
        pieces[4], pieces[5] = got[:2]
    lru_w_all = {}
    dqr, dqp, dk, dvh, dbt, got = attn_bwd(
        q_rot, q_pl, kk, vv, bt, y_na, dyn, lse,
        comm=join_comms(chip_partials([1, 2, 3], v_mix, got[2:]),
                        join_comms(all_gather_comm(d_wa.reshape(-1, LRU_BLOCK_W)),
                                   all_gather_comm(d_wx.reshape(-1, LRU_BLOCK_W)))) if dist else None)
    if dist:
        pieces[1], pieces[2], pieces[3], lru_w_all["lru_wa"], lru_w_all["lru_wx"] = got
    dq_cols, dk_cols, dv_cols, d_qg, d_kg, _ = qkv_bwd(dqr, dqp, dk, dvh, p, qg2, kg2, cos, sin, ones)
    d_rpb = rpb_grad(dbt)
    dgs = [dxr, dk_cols, dv_cols, dgx, dq_cols, dmr, dmn]
    g_w_in = None
    for g in range(7):
        g_w_in = matmul_tn(xn, dgs[g], "gw_in_%d" % g, 1024, 512, prev=g_w_in, col_block=g, total_cols=IN_COLS)
    if dist:
        v_in = views_of([0], [g_w_in])
        recv_in = run_comm(exchange_halves_comm(v_in), "grad_exchange_w_in")
    grad_x, dsh, dsc, d_gmix, got = in_proj_bwd(dgs, w_in, z, dx1, norm_mix_g, scale,
                                                comm=chip_partials([0], v_in, recv_in) if dist else None)
    if dist:
        pieces[0] = got[0]

    d_modx = jnp.concatenate([dsh[1], dsc[1], dg2, d_s3, d_s4, dg5], axis=1)
    d_modc = jnp.concatenate([dsh[0], dsc[0]], axis=1)
    return dict(loss_sq=loss_sq, grad_x=grad_x, d_modx=d_modx, d_modc=d_modc, norm_mix_g=d_gmix, norm_ffn_g=d_gffn,
                w_in=g_w_in, lru_conv_w=d_cw, lru_conv_b=d_cb, lru_wa=d_wa, lru_ba=d_ba, lru_wx=d_wx, lru_bx=d_bx,
                lru_lambda=d_lam, q_norm_g=d_qg, k_norm_g=d_kg, na_rpb=d_rpb, w_rnn_out=g_w_rnn, w_na_out=g_w_na,
                w_out=g_w_out, w_up=g_w_up, ffn_conv_w=d_fcw, ffn_conv_b=d_fcb, w_down=g_w_down,
                partials=partials, pieces=pieces, lru_w_all=lru_w_all)


def _mesh_pos():
    return lax.axis_index("x"), lax.axis_index("y"), lax.axis_index("c")


def _other_chips(x, y):
    return [(1 - x, y), (x, 1 - y), (1 - x, 1 - y)]


BIG = (("w_in", (D_MODEL, IN_COLS), 1), ("w_rnn_out", (D_MODEL, D_MODEL), 0), ("w_na_out", (D_MODEL, D_MODEL), 0),
       ("w_out", (D_MODEL, D_MODEL), 0), ("w_up", (D_MODEL, 2 * D_FF), 1), ("w_down", (D_FF, D_MODEL), 0))


def _shard_shape(full, axis):
    r, c = full
    return (r // N_SHARD, c) if axis == 0 else (r, c // N_SHARD)


def _slot(ref, full, axis, s, h):
    r, c = full
    if axis == 0:
        rs = r // N_SHARD
        return ref.at[pl.ds(s * rs + h * (rs // 2), rs // 2), :]
    cs = c // N_SHARD
    return ref.at[pl.ds(h * (r // 2), r // 2), pl.ds(s * cs, cs)]


def cast_into_full(x, full, axis, idx, name):
    r, c = x.shape
    tr = next(t for t in (512, 352, 256, 128) if r % t == 0)
    nb = r // tr

    def body(idx_ref, x_ref, o_ref):
        o_ref[...] = x_ref[...].astype(BF16)

    if axis == 0:
        out_spec = pl.BlockSpec((tr, c), lambda i, idx_ref: (idx_ref[0] * nb + i, 0))
    else:
        out_spec = pl.BlockSpec((tr, c), lambda i, idx_ref: (i, idx_ref[0]))
    return pl.pallas_call(
        body, name=name,
        grid_spec=pltpu.PrefetchScalarGridSpec(
            num_scalar_prefetch=1, grid=(nb,), in_specs=[pl.BlockSpec((tr, c), lambda i, idx_ref: (i, 0))],
            out_specs=out_spec),
        out_shape=jax.ShapeDtypeStruct(full, BF16),
        compiler_params=_params("parallel"),
    )(idx, x)


def run_comm(comm, name):
    k_in, k_out = len(comm.inputs), len(comm.out_shapes)

    def body(*refs):
        start, mid, end = comm.emit(refs[:k_in], refs[k_in:k_in + k_out], refs[k_in + k_out:])
        start()
        mid()
        end()

    hbm = pl.BlockSpec(memory_space=pl.ANY)
    return pl.pallas_call(
        body, name=name, in_specs=[hbm] * k_in, out_specs=[hbm] * k_out, out_shape=list(comm.out_shapes),
        input_output_aliases=dict(comm.aliases), scratch_shapes=list(comm.scratch),
        compiler_params=pltpu.CompilerParams(vmem_limit_bytes=VMEM_LIMIT_V7X),
    )(*comm.inputs)


def gather_weights_comm(fulls, which):
    nw = len(which)
    specs = [BIG[w] for w in which]

    def emit(_, outs, sems):
        send1, recv1, send2, recv2 = sems
        x, y, c = _mesh_pos()
        sibling = (x, y, 1 - c)
        chips = _other_chips(x, y)
        s_me = 2 * x + y
        shards = [2 * chip[0] + chip[1] for chip in chips]

        def ici(w, j, shard):
            _, full, axis = specs[w]
            dst = _slot(outs[w], full, axis, shard, c)
            return pltpu.make_async_remote_copy(
                src_ref=dst, dst_ref=dst, send_sem=send1.at[3 * w + j],
                recv_sem=recv1.at[3 * w + j], device_id=(*chips[j], c), device_id_type=MESH_T)

        def d2d(w, j, shard, half):
            _, full, axis = specs[w]
            dst = _slot(outs[w], full, axis, shard, half)
            return pltpu.make_async_remote_copy(
                src_ref=dst, dst_ref=dst, send_sem=send2.at[3 * w + j], recv_sem=recv2.at[3 * w + j],
                device_id=sibling, device_id_type=MESH_T)

        pairs = [(w, j) for w in range(nw) for j in range(3)]

        def start():
            for w, j in pairs:
                ici(w, j, s_me).start()

        def mid():
            for w, j in pairs:
                ici(w, j, shards[j]).wait_recv()
                d2d(w, j, shards[j], c).start()

        def end():
            for w, j in pairs:
                d2d(w, j, shards[j], 1 - c).wait_recv()
            for w, j in pairs:
                ici(w, j, s_me).wait_send()
                d2d(w, j, shards[j], c).wait_send()

        return start, mid, end

    return Comm(list(fulls), [jax.ShapeDtypeStruct(full, BF16) for _, full, _ in specs], {i: i for i in range(nw)},
                [pltpu.SemaphoreType.DMA((3 * nw,))] * 4, emit)


def join_comms(a, b):
    ai, ao, asc = len(a.inputs), len(a.out_shapes), len(a.scratch)

    def emit(ins, outs, sems):
        fa = a.emit(ins[:ai], outs[:ao], sems[:asc])
        fb = b.emit(ins[ai:], outs[ao:], sems[asc:])

        def both(k):
            def run():
                fa[k]()
                fb[k]()
            return run

        return both(0), both(1), both(2)

    aliases = dict(a.aliases)
    aliases.update({ai + i: ao + o for i, o in b.aliases.items()})
    return Comm(a.inputs + b.inputs, a.out_shapes + b.out_shapes, aliases, a.scratch + b.scratch, emit)


def all_gather_comm(x):
    def emit(srcs, outs, sems):
        send_sems, recv_sems, local_sem = sems
        x_ref, out_ref = srcs[0], outs[0]
        x, y, c = _mesh_pos()
        me, sibling = (x, y, c), (x, y, 1 - c)
        chips = _other_chips(x, y)

        def blk(px, py, pc):
            return out_ref.at[4 * px + 2 * py + pc]

        def copy(k, block, to, src=None):
            return pltpu.make_async_remote_copy(
                src_ref=blk(*block) if src is None else src, dst_ref=blk(*block),
                send_sem=send_sems.at[k], recv_sem=recv_sems.at[k], device_id=to, device_id_type=MESH_T)

        def mine():
            return pltpu.make_async_copy(x_ref, blk(*me), local_sem)

        def start():
            mine().start()
            copy(0, me, sibling, src=x_ref).start()
            for j, chip in enumerate(chips):
                copy(1 + j, me, (*chip, c), src=x_ref).start()

        def mid():
            for j, chip in enumerate(chips):
                copy(1 + j, (*chip, c), me).wait_recv()
                copy(4 + j, (*chip, c), sibling).start()

        def end():
            copy(0, sibling, me).wait_recv()
            for j, chip in enumerate(chips):
                copy(4 + j, (*chip, 1 - c), me).wait_recv()
            copy(0, me, sibling, src=x_ref).wait_send()
            for j, chip in enumerate(chips):
                copy(1 + j, me, (*chip, c), src=x_ref).wait_send()
                copy(4 + j, (*chip, c), sibling).wait_send()
            mine().wait()

        return start, mid, end

    return Comm([x], [jax.ShapeDtypeStruct((N_DEV,) + x.shape, F32)], {},
                [pltpu.SemaphoreType.DMA((7,)), pltpu.SemaphoreType.DMA((7,)), pltpu.SemaphoreType.DMA], emit)


def sum_blocks(g, name):
    _, r, c = g.shape
    tr = 256 if r % 256 == 0 else r

    def body(g_ref, o_ref):
        acc = g_ref[0]
        for k in range(1, N_DEV):
            acc = acc + g_ref[k]
        o_ref[...] = acc

    return pl.pallas_call(
        body, name=name, grid=(r // tr,),
        in_specs=[pl.BlockSpec((N_DEV, tr, c), lambda i: (0, i, 0))],
        out_specs=pl.BlockSpec((tr, c), lambda i: (i, 0)),
        out_shape=jax.ShapeDtypeStruct((r, c), F32),
        compiler_params=_params("parallel"),
    )(g)


def _grad_view(g, full, axis):
    r, c = full
    if axis == 0:
        return g.reshape(N_SHARD, 2, r // N_SHARD // 2, c)
    return g.reshape(1, 2, r // 2, c)


def exchange_halves_comm(gviews):
    nw = len(gviews)

    def emit(srcs, outs, sems):
        send_sems, recv_sems = sems
        x, y, c = _mesh_pos()

        def copies():
            return [pltpu.make_async_remote_copy(
                src_ref=srcs[w].at[:, pl.ds(1 - c, 1)], dst_ref=outs[w], send_sem=send_sems.at[w],
                recv_sem=recv_sems.at[w], device_id=(x, y, 1 - c), device_id_type=MESH_T) for w in range(nw)]

        def start():
            for cp in copies():
                cp.start()

        def end():
            for cp in copies():
                cp.wait()

        return start, lambda: None, end

    return Comm(list(gviews), [jax.ShapeDtypeStruct((g.shape[0], 1) + g.shape[2:], BF16) for g in gviews], {},
                [pltpu.SemaphoreType.DMA((nw,)), pltpu.SemaphoreType.DMA((nw,))], emit)


def _row_tile(rh):
    return 128 if rh % 128 == 0 else rh


def add_halves(gview, recv, c_idx, name):
    a, _, rh, cc = gview.shape
    tr = _row_tile(rh)

    def body(c_ref, g_ref, r_ref, o_ref):
        o_ref[0] = (g_ref[0, 0].astype(F32) + r_ref[0, 0].astype(F32)).astype(BF16)

    return pl.pallas_call(
        body, name=name,
        grid_spec=pltpu.PrefetchScalarGridSpec(
            num_scalar_prefetch=1, grid=(a, rh // tr),
            in_specs=[pl.BlockSpec((1, 1, tr, cc), lambda s, i, c_ref: (s, c_ref[0], i, 0)),
                      pl.BlockSpec((1, 1, tr, cc), lambda s, i, c_ref: (s, 0, i, 0))],
            out_specs=pl.BlockSpec((1, tr, cc), lambda s, i, c_ref: (s, i, 0))),
        out_shape=jax.ShapeDtypeStruct((a, rh, cc), BF16),
        compiler_params=_params("parallel", "parallel"),
    )(c_idx, gview, recv)


def _piece_shape(full, axis):
    rs, cs = _shard_shape(full, axis)
    return (rs // 2, cs)


def scatter_pieces_comm(partials, which):
    nw = len(which)
    specs = [BIG[w] for w in which]

    def emit(srcs, outs, sems):
        send_sems, recv_sems = sems
        x, y, c = _mesh_pos()
        chips = _other_chips(x, y)

        def copies():
            cps = []
            for w, (_, full, axis) in enumerate(specs):
                cs = full[1] // N_SHARD
                for j, chip in enumerate(chips):
                    s_j = 2 * chip[0] + chip[1]
                    src = srcs[w].at[s_j] if axis == 0 else srcs[w].at[0, :, pl.ds(s_j * cs, cs)]
                    cps.append(pltpu.make_async_remote_copy(
                        src_ref=src, dst_ref=outs[w].at[j], send_sem=send_sems.at[3 * w + j],
                        recv_sem=recv_sems.at[3 * w + j], device_id=(*chip, c), device_id_type=MESH_T))
            return cps

        def start():
            for cp in copies():
                cp.start()

        def mid():
            pass

        def end():
            for cp in copies():
                cp.wait()

        return start, mid, end

    return Comm(list(partials), [jax.ShapeDtypeStruct((3,) + _piece_shape(full, axis), BF16) for _, full, axis in specs],
                {}, [pltpu.SemaphoreType.DMA((3 * nw,)), pltpu.SemaphoreType.DMA((3 * nw,))], emit)


def add_pieces(partial, recv, idx, axis, name):
    _, rh, cs = recv.shape
    tr = _row_tile(rh)

    def body(idx_ref, p_ref, r_ref, o_ref):
        o_ref[0] = ((p_ref[0].astype(F32) + r_ref[0].astype(F32)) + r_ref[1].astype(F32)) + r_ref[2].astype(F32)

    if axis == 0:
        pspec = pl.BlockSpec((1, tr, cs), lambda i, idx_ref: (idx_ref[0], i, 0))
    else:
        pspec = pl.BlockSpec((1, tr, cs), lambda i, idx_ref: (0, i, idx_ref[0]))
    return pl.pallas_call(
        body, name=name,
        grid_spec=pltpu.PrefetchScalarGridSpec(
            num_scalar_prefetch=1, grid=(rh // tr,),
            in_specs=[pspec, pl.BlockSpec((3, tr, cs), lambda i, idx_ref: (0, i, 0))],
            out_specs=pl.BlockSpec((1, tr, cs), lambda i, idx_ref: (idx_ref[1], i, 0))),
        out_shape=jax.ShapeDtypeStruct((2, rh, cs), F32),
        compiler_params=_params("parallel"),
    )(idx, partial, recv)


def join_halves_comm(halves):
    nw = len(halves)

    def emit(_, outs, sems):
        send_sems, recv_sems = sems
        x, y, c = _mesh_pos()

        def copy(w, half):
            return pltpu.make_async_remote_copy(
                src_ref=outs[w].at[half], dst_ref=outs[w].at[half], send_sem=send_sems.at[w], recv_sem=recv_sems.at[w],
                device_id=(x, y, 1 - c), device_id_type=MESH_T)

        def start():
            for w in range(nw):
                copy(w, c).start()

        def end():
            for w in range(nw):
                copy(w, c).wait_send()
                copy(w, 1 - c).wait_recv()

        return start, lambda: None, end

    return Comm(list(halves), [jax.ShapeDtypeStruct(h.shape, F32) for h in halves], {i: i for i in range(nw)},
                [pltpu.SemaphoreType.DMA((nw,))] * 2, emit)


MOD_COLS = N_MOD * D_MODEL // N_SHARD
MOD_TILE = 512


def mod_fwd(c16, w_mod):
    def body(c_ref, w_ref, s_ref, o_ref):
        cv = c_ref[...]
        s = cv * _sigmoid(cv)
        s_ref[...] = s
        o_ref[...] = jnp.dot(s.astype(BF16), w_ref[...].astype(BF16), preferred_element_type=F32)

    return pl.pallas_call(
        body, name="mod_fwd", grid=(MOD_COLS // MOD_TILE,),
        in_specs=[_full((16, D_MODEL)), pl.BlockSpec((D_MODEL, MOD_TILE), lambda j: (0, j))],
        out_specs=[_full((16, D_MODEL)), pl.BlockSpec((16, MOD_TILE), lambda j: (0, j))],
        out_shape=[jax.ShapeDtypeStruct((16, D_MODEL), F32), jax.ShapeDtypeStruct((16, MOD_COLS), F32)],
        compiler_params=_params("arbitrary"),
    )(c16, w_mod)


def mod_bwd(s16, dm16, w_mod):
    hi = lax.Precision.HIGHEST

    def body(s_ref, d_ref, w_ref, gw_ref, ds_ref):
        j = pl.program_id(0)
        dm = d_ref[...]
        gw_ref[...] = lax.dot_general(s_ref[...], dm, (((0,), (0,)), ((), ())), preferred_element_type=F32, precision=hi)
        part = lax.dot_general(dm, w_ref[...], (((1,), (1,)), ((), ())), preferred_element_type=F32, precision=hi)

        @pl.when(j == 0)
        def _():
            ds_ref[...] = part

        @pl.when(j > 0)
        def _():
            ds_ref[...] = ds_ref[...] + part

    return pl.pallas_call(
        body, name="mod_bwd", grid=(MOD_COLS // MOD_TILE,),
        in_specs=[_full((16, D_MODEL)), pl.BlockSpec((16, MOD_TILE), lambda j: (0, j)),
                  pl.BlockSpec((D_MODEL, MOD_TILE), lambda j: (0, j))],
        out_specs=[pl.BlockSpec((D_MODEL, MOD_TILE), lambda j: (0, j)), _full((16, D_MODEL))],
        out_shape=[jax.ShapeDtypeStruct((D_MODEL, MOD_COLS), F32), jax.ShapeDtypeStruct((16, D_MODEL), F32)],
        compiler_params=_params("arbitrary"),
    )(s16, dm16, w_mod)


def cctx_grad(parts, c_ctx):
    def body(p_ref, c_ref, o_ref):
        ds = p_ref[0:1, :]
        for s in range(1, N_SHARD):
            ds = ds + p_ref[16 * s:16 * s + 1, :]
        cv = c_ref[...]
        sg = _sigmoid(cv)
        o_ref[...] = ds * (sg * (1.0 + cv * (1.0 - sg)))

    return pl.pallas_call(
        body, name="cctx_grad", in_specs=[_full((N_DEV * 8, D_MODEL)), _full((1, D_MODEL))],
        out_specs=_full((1, D_MODEL)), out_shape=jax.ShapeDtypeStruct((1, D_MODEL), F32),
    )(parts, c_ctx)


def add_rows(a, b, name):
    def body(a_ref, b_ref, o_ref):
        o_ref[...] = a_ref[...] + b_ref[...]

    return pl.pallas_call(body, name=name, in_specs=[_full(a.shape), _full(b.shape)], out_specs=_full(a.shape),
                          out_shape=jax.ShapeDtypeStruct(a.shape, F32))(a, b)


def _adamw_update(w_ref, g_ref, m_ref, v_ref, d_ref, nm_ref, nv_ref):
    g_ = g_ref[...]
    m_ = ADAM_B1 * m_ref[...] + (1.0 - ADAM_B1) * g_
    v_ = ADAM_B2 * v_ref[...] + (1.0 - ADAM_B2) * (g_ * g_)
    m_hat = m_ / (1.0 - ADAM_B1 ** ADAM_STEP)
    v_hat = v_ / (1.0 - ADAM_B2 ** ADAM_STEP)
    d_ref[...] = -ADAM_LR * (m_hat / (jnp.sqrt(v_hat) + ADAM_EPS) + ADAM_WD * w_ref[...])
    nm_ref[...] = m_
    nv_ref[...] = v_


def adamw_many(ws, gs, ms, vs):
    n = len(ws)

    def body(*refs):
        for i in range(n):
            _adamw_update(*[refs[k * n + i] for k in range(7)])

    shapes = [jax.ShapeDtypeStruct(w.shape, F32) for w in ws]
    return pl.pallas_call(body, name="adamw_small", out_shape=shapes * 3,
                          compiler_params=pltpu.CompilerParams(vmem_limit_bytes=VMEM_LIMIT_V7X))(*ws, *gs, *ms, *vs)


def adamw(w, g, m, v, name, comm=None):
    r, c = w.shape
    tr = 128 if (r % 128 == 0 and r > 128) else r

    def body(w_ref, g_ref, m_ref, v_ref, d_ref, nm_ref, nv_ref):
        _adamw_update(w_ref, g_ref, m_ref, v_ref, d_ref, nm_ref, nv_ref)

    spec = pl.BlockSpec((tr, c), lambda i: (i, 0))
    shp = jax.ShapeDtypeStruct((r, c), F32)
    res, extra = _call(body, name=name, grid=(r // tr,), in_specs=[spec] * 4, out_specs=[spec] * 3,
                       out_shape=[shp] * 3, sem=("parallel",), args=(w, g, m, v), comm=comm)
    return (*res, extra)


LANES = 1024


def _pack(arrs):
    rows, spans, at = [], [], 0
    for a in arrs:
        n = int(np.prod(a.shape))
        nr = 8 * -(-n // (8 * LANES))
        flat = a.reshape(-1)
        if nr * LANES != n:
            flat = jnp.concatenate([flat, jnp.zeros((nr * LANES - n,), F32)])
        rows.append(flat.reshape(nr, LANES))
        spans.append((at, nr, n, a.shape))
        at += nr
    return jnp.concatenate(rows, axis=0), spans


def _unpack(buf, spans):
    out = []
    for at, nr, n, shape in spans:
        out.append(buf[at:at + nr].reshape(-1)[:n].reshape(shape))
    return out


SMALL_SHARD = ("lru_conv_w", "lru_ba", "lru_bx", "lru_lambda", "ffn_conv_w")


def kernel(x, c, ctx, c_ctx, w_mod, b_mod, norm_mix_g, norm_ffn_g, w_in, lru_conv_w, lru_conv_b, lru_wa, lru_ba, lru_wx, lru_bx, lru_lambda, q_norm_g, k_norm_g, na_rpb, w_rnn_out, w_na_out, w_out, w_up, ffn_conv_w, ffn_conv_b, w_down, loss_target, m_c_ctx, m_w_mod, m_b_mod, m_norm_mix_g, m_norm_ffn_g, m_w_in, m_lru_conv_w, m_lru_conv_b, m_lru_wa, m_lru_ba, m_lru_wx, m_lru_bx, m_lru_lambda, m_q_norm_g, m_k_norm_g, m_na_rpb, m_w_rnn_out, m_w_na_out, m_w_out, m_w_up, m_ffn_conv_w, m_ffn_conv_b, m_w_down, v_c_ctx, v_w_mod, v_b_mod, v_norm_mix_g, v_norm_ffn_g, v_w_in, v_lru_conv_w, v_lru_conv_b, v_lru_wa, v_lru_ba, v_lru_wx, v_lru_bx, v_lru_lambda, v_q_norm_g, v_k_norm_g, v_na_rpb, v_w_rnn_out, v_w_na_out, v_w_out, v_w_up, v_ffn_conv_w, v_ffn_conv_b, v_w_down):
    weights = dict(c_ctx=c_ctx, w_mod=w_mod, b_mod=b_mod, norm_mix_g=norm_mix_g, norm_ffn_g=norm_ffn_g, w_in=w_in,
                   lru_conv_w=lru_conv_w, lru_conv_b=lru_conv_b, lru_wa=lru_wa, lru_ba=lru_ba, lru_wx=lru_wx,
                   lru_bx=lru_bx, lru_lambda=lru_lambda, q_norm_g=q_norm_g, k_norm_g=k_norm_g, na_rpb=na_rpb,
                   w_rnn_out=w_rnn_out, w_na_out=w_na_out, w_out=w_out, w_up=w_up, ffn_conv_w=ffn_conv_w,
                   ffn_conv_b=ffn_conv_b, w_down=w_down)
    mom1 = dict(c_ctx=m_c_ctx, w_mod=m_w_mod, b_mod=m_b_mod, norm_mix_g=m_norm_mix_g, norm_ffn_g=m_norm_ffn_g,
                w_in=m_w_in, lru_conv_w=m_lru_conv_w, lru_conv_b=m_lru_conv_b, lru_wa=m_lru_wa, lru_ba=m_lru_ba,
                lru_wx=m_lru_wx, lru_bx=m_lru_bx, lru_lambda=m_lru_lambda, q_norm_g=m_q_norm_g, k_norm_g=m_k_norm_g,
                na_rpb=m_na_rpb, w_rnn_out=m_w_rnn_out, w_na_out=m_w_na_out, w_out=m_w_out, w_up=m_w_up,
                ffn_conv_w=m_ffn_conv_w, ffn_conv_b=m_ffn_conv_b, w_down=m_w_down)
    mom2 = dict(c_ctx=v_c_ctx, w_mod=v_w_mod, b_mod=v_b_mod, norm_mix_g=v_norm_mix_g, norm_ffn_g=v_norm_ffn_g,
                w_in=v_w_in, lru_conv_w=v_lru_conv_w, lru_conv_b=v_lru_conv_b, lru_wa=v_lru_wa, lru_ba=v_lru_ba,
                lru_wx=v_lru_wx, lru_bx=v_lru_bx, lru_lambda=v_lru_lambda, q_norm_g=v_q_norm_g, k_norm_g=v_k_norm_g,
                na_rpb=v_na_rpb, w_rnn_out=v_w_rnn_out, w_na_out=v_w_na_out, w_out=v_w_out, w_up=v_w_up,
                ffn_conv_w=v_ffn_conv_w, ffn_conv_b=v_ffn_conv_b, w_down=v_w_down)
    order = list(weights)
    d = D_MODEL
    mx_, my_, mc_ = _mesh_pos()
    shard = 2 * mx_ + my_
    dev = 2 * shard + mc_

    idx = jnp.stack([shard, mc_]).astype(jnp.int32)
    wsh = {name: cast_into_full(weights[name][0], full, axis, idx, "cast_" + name) for name, full, axis in BIG}
    local_small, small_spans = _pack([c] + [weights[k][0] for k in SMALL_SHARD])
    bt, (w_in_full, gath) = bias_table(na_rpb[0], comm=join_comms(gather_weights_comm([wsh["w_in"]], [0]),
                                                                  all_gather_comm(local_small)))
    per_dev = [_unpack(gath[k], small_spans) for k in range(N_DEV)]
    c_all = jnp.concatenate([per_dev[k][0] for k in range(N_DEV)], axis=0)
    full_small = {name: jnp.concatenate([per_dev[2 * s][1 + i] for s in range(N_SHARD)], axis=-1)
                  for i, name in enumerate(SMALL_SHARD)}
    c16 = jnp.concatenate([c_all, c_ctx.reshape(1, d), jnp.zeros((7, d), F32)], axis=0)
    s16, mod_part = mod_fwd(c16, w_mod[0])
    mod_all = run_comm(all_gather_comm(mod_part), "gather_mod")[0]
    mod = jnp.concatenate([mod_all[2 * s] for s in range(N_SHARD)], axis=1) + b_mod
    modx = lax.dynamic_slice(mod, (dev, 0), (1, N_MOD * d))
    modc = mod[8:9]

    z = jnp.concatenate([ctx[0], x[0]], axis=0)
    res = local_step(z, loss_target[0], modx, modc, norm_mix_g, norm_ffn_g, w_in_full, full_small["lru_conv_w"],
                     lru_conv_b, lru_wa[0], full_small["lru_ba"], lru_wx[0], full_small["lru_bx"],
                     full_small["lru_lambda"], q_norm_g, k_norm_g, na_rpb[0], wsh["w_rnn_out"], wsh["w_na_out"],
                     wsh["w_out"], wsh["w_up"], full_small["ffn_conv_w"], ffn_conv_b, wsh["w_down"], idx=idx, bt=bt)

    halves = [add_pieces(res["partials"][i], res["pieces"][i], idx, BIG[i][2], "add_pieces_" + BIG[i][0])
              for i in range(len(BIG))]
    lru_tot = {k: sum_blocks(res["lru_w_all"][k], "sum_" + k).reshape(weights[k].shape[1:])
               for k in ("lru_wa", "lru_wx")}
    small_names = ["norm_mix_g", "norm_ffn_g", "lru_conv_w", "lru_conv_b", "lru_ba", "lru_bx",
                   "lru_lambda", "q_norm_g", "k_norm_g", "na_rpb", "ffn_conv_w", "ffn_conv_b"]
    local_g, g_spans = _pack([res["loss_sq"][0:1, 0:1], res["d_modx"], res["d_modc"]] + [res[k] for k in small_names])
    n_rows = local_g.shape[0]
    *joined, g_all = run_comm(join_comms(join_halves_comm(halves), all_gather_comm(local_g)), "tail_exchange")
    grads = {name: joined[i].reshape(_shard_shape(full, axis)) for i, (name, full, axis) in enumerate(BIG)}
    grads.update(lru_tot)
    g_tot = sum_blocks(g_all, "sum_small")
    tot = _unpack(g_tot, g_spans)
    loss = (0.5 / d) * tot[0][0, 0]
    small_tot = dict(zip(small_names, tot[3:]))
    at_x = g_spans[1][0]
    dmx_rows = g_all.reshape(N_DEV, n_rows, LANES)[:, at_x:at_x + N_MOD, :].reshape(N_DEV, N_MOD * d)
    dmc_row = jnp.concatenate([tot[2], jnp.zeros((1, 4 * d), F32)], axis=1)
    dm16 = jnp.concatenate([dmx_rows, dmc_row, jnp.zeros((7, N_MOD * d), F32)], axis=0)
    grads["b_mod"] = add_rows(tot[1], dmc_row, "b_mod_grad")
    g_w_mod, ds16 = mod_bwd(s16, lax.dynamic_slice(dm16, (0, shard * MOD_COLS), (16, MOD_COLS)), w_mod[0])
    grads["w_mod"] = g_w_mod
    for k in small_names:
        g = small_tot[k]
        if k in SMALL_SHARD:
            w_sh = weights[k].shape[-1]
            g = lax.dynamic_slice_in_dim(g, shard * w_sh, w_sh, axis=g.ndim - 1)
        grads[k] = g

    delta, new_m, new_v = {}, {}, {}
    for name, _, _ in BIG + (("w_mod", None, None),):
        *upd, got = adamw(weights[name][0], grads[name], mom1[name][0], mom2[name][0], "adamw_" + name,
                          comm=all_gather_comm(ds16[8:16]) if name == "w_in" else None)
        delta[name], new_m[name], new_v[name] = upd
        if name == "w_in":
            grads["c_ctx"] = cctx_grad(got[0].reshape(N_DEV * 8, d), c_ctx.reshape(1, d))
    rest = [k for k in order if k not in delta]
    views = {k: (grads[k].shape if grads[k].ndim <= 3 else (-1, grads[k].shape[-1])) for k in rest}
    small = adamw_many(*[[t[k].reshape(views[k]) for k in rest] for t in (weights, grads, mom1, mom2)])
    n_rest = len(rest)
    for i, k in enumerate(rest):
        delta[k], new_m[k], new_v[k] = small[i], small[n_rest + i], small[2 * n_rest + i]

    shaped = lambda t: [t[k].reshape(weights[k].shape) for k in order]
    return (loss, res["grad_x"][None], *shaped(grads), *shaped(delta), *shaped(new_m), *shaped(new_v))
```

```python
import numpy as np
import jax
import jax.numpy as jnp
from jax import lax
from jax.experimental import pallas as pl
from jax.experimental.pallas import tpu as pltpu

F32 = jnp.float32
BF16 = jnp.bfloat16

D_MODEL = 1024
SEQ = 2048
CTX_LEN = 256
ZLEN = SEQ + CTX_LEN
GRID_W = 64
GRID_ROWS = SEQ // GRID_W
LRU_BLOCK_W = 128
LRU_BLOCKS = 8
LRU_C = 8.0
NA_HEADS = 16
HEAD_DIM = 64
NA_ROWS = 8
NA_COLS = 16
ROPE_BASE = 10000.0
D_FF = 2816
N_MOD = 6
IN_COLS = 7 * D_MODEL
EPS = 1e-6
NEG_INF = -1e30
N_DEV = 8
N_SHARD = 4

ADAM_LR = 0.001
ADAM_B1 = 0.9
ADAM_B2 = 0.999
ADAM_EPS = 1e-08
ADAM_WD = 0.01
ADAM_STEP = 10

ROW_TILE = 256
Q_ROWS = 4
Q_TILE = Q_ROWS * GRID_W
KEY_ROWS = 12
KEY_TILE = KEY_ROWS * GRID_W
BT_PAD = 4
BT_LEN = 24
VMEM_LIMIT_V7X = 56 * 1024 * 1024

MESH_T = pl.DeviceIdType.MESH


def _params(*sem):
    return pltpu.CompilerParams(dimension_semantics=sem if sem else None, vmem_limit_bytes=VMEM_LIMIT_V7X)


def _full(shape):
    nd = len(shape)
    return pl.BlockSpec(shape, lambda *_: (0,) * nd)


class Comm:
    def __init__(self, inputs, out_shapes, aliases, scratch, emit):
        self.inputs, self.out_shapes, self.aliases, self.scratch, self.emit = inputs, out_shapes, aliases, scratch, emit


def _call(body, *, name, grid, in_specs, out_specs, out_shape, args, scratch_shapes=(), sem=(), comm=None):
    n_in, n_out, n_sc = len(in_specs), len(out_specs), len(scratch_shapes)
    if comm is None:
        res = pl.pallas_call(body, name=name, grid=grid, in_specs=list(in_specs), out_specs=list(out_specs),
                             out_shape=list(out_shape), scratch_shapes=list(scratch_shapes),
                             compiler_params=_params(*sem))(*args)
        return list(res), []
    k_in, k_out = len(comm.inputs), len(comm.out_shapes)
    steps = int(np.prod(grid))

    def hosted(*refs):
        ins, cins = refs[:n_in], refs[n_in:n_in + k_in]
        at = n_in + k_in
        outs, couts = refs[at:at + n_out], refs[at + n_out:at + n_out + k_out]
        at += n_out + k_out
        scr, cscr = refs[at:at + n_sc], refs[at + n_sc:]
        start, mid, end = comm.emit(cins, couts, cscr)
        lin = pl.program_id(0)
        for ax in range(1, len(grid)):
            lin = lin * grid[ax] + pl.program_id(ax)
        pl.when(lin == 0)(start)
        body(*ins, *outs, *scr)
        pl.when(lin == steps - 1 - steps // 7)(mid)
        pl.when(lin == steps - 1)(end)

    hbm = pl.BlockSpec(memory_space=pl.ANY)
    res = pl.pallas_call(
        hosted, name=name, grid=grid, in_specs=list(in_specs) + [hbm] * k_in, out_specs=list(out_specs) + [hbm] * k_out,
        out_shape=list(out_shape) + list(comm.out_shapes), scratch_shapes=list(scratch_shapes) + list(comm.scratch),
        input_output_aliases={n_in + i: n_out + o for i, o in comm.aliases.items()},
        compiler_params=_params(*(("arbitrary",) * len(grid))))(*args, *comm.inputs)
    return list(res[:n_out]), list(res[n_out:])


def _sigmoid(x):
    return 0.5 * jnp.tanh(0.5 * x) + 0.5


def _gelu_parts(x):
    c0 = 0.7978845608028654
    inner = c0 * (x + 0.044715 * x * x * x)
    t = jnp.tanh(inner)
    g = 0.5 * x * (1.0 + t)
    dg = 0.5 * (1.0 + t) + 0.5 * x * (1.0 - t * t) * c0 * (1.0 + 3.0 * 0.044715 * x * x)
    return g, dg


def _dot_nt(a, b):
    return lax.dot_general(a, b, (((1,), (1,)), ((), ())), preferred_element_type=F32)


def _dot_tn(a, b):
    return lax.dot_general(a, b, (((0,), (0,)), ((), ())), preferred_element_type=F32)


def norm_matmul(xin, gain, shift, scale, w, name, tm, tn, ctx_rows=0, comm=None):
    r, d = xin.shape
    n = w.shape[1]
    assert r % tm == 0 and n % tn == 0

    def body(x_ref, g_ref, sh_ref, sc_ref, w_ref, y_ref, xn_hbm, xn_s, sem):
        j, i = pl.program_id(0), pl.program_id(1)
        rows = pl.ds(pl.multiple_of(i * tm, tm), tm)

        @pl.when(j == 0)
        def _():
            x = x_ref[...]
            nrm = x * lax.rsqrt(jnp.mean(x * x, axis=-1, keepdims=True) + EPS)
            sh, sc = sh_ref[shift.shape[0] - 1], sc_ref[shift.shape[0] - 1]
            if ctx_rows:
                is_ctx = i * tm + lax.broadcasted_iota(jnp.int32, (tm, 1), 0) < ctx_rows
                sh, sc = jnp.where(is_ctx, sh_ref[0], sh), jnp.where(is_ctx, sc_ref[0], sc)
            xn_s[rows, :] = ((nrm * g_ref[...]) * (1.0 + sc) + sh).astype(BF16)
            pltpu.make_async_copy(xn_s.at[rows, :], xn_hbm.at[rows, :], sem.at[i]).start()

        y_ref[...] = jnp.dot(xn_s[rows, :], w_ref[...], preferred_element_type=F32)

        @pl.when((j == n // tn - 1) & (i == r // tm - 1))
        def _():
            for t in range(r // tm):
                tile = pl.ds(t * tm, tm)
                pltpu.make_async_copy(xn_s.at[tile, :], xn_hbm.at[tile, :], sem.at[t]).wait()

    res, extra = _call(
        body, name=name, grid=(n // tn, r // tm),
        in_specs=[pl.BlockSpec((tm, d), lambda j, i: (i, 0)), _full((1, d)), _full(shift.shape), _full(scale.shape),
                  pl.BlockSpec((d, tn), lambda j, i: (0, j))],
        out_specs=[pl.BlockSpec((tm, tn), lambda j, i: (i, j)), pl.BlockSpec(memory_space=pl.ANY)],
        out_shape=[jax.ShapeDtypeStruct((r, n), F32), jax.ShapeDtypeStruct((r, d), BF16)],
        scratch_shapes=[pltpu.VMEM((r, d), BF16), pltpu.SemaphoreType.DMA((r // tm,))],
        sem=("arbitrary", "arbitrary"), args=(xin, gain, shift, scale, w), comm=comm)
    return res[1], res[0], extra


def _row_ids(n, w):
    return lax.broadcasted_iota(jnp.int32, (n, w), 0)


def _lru_conv(xr, cw, cb):
    row = _row_ids(ZLEN, LRU_BLOCK_W)
    segpos = jnp.where(row < CTX_LEN, row, row - CTX_LEN)
    seglen = jnp.where(row < CTX_LEN, CTX_LEN, SEQ)
    acc = xr * cw[2:3, :] + cb
    for k in (0, 1, 3):
        off = k - 2
        sh = pltpu.roll(xr, (-off) % ZLEN, 0)
        ok = (segpos + off >= 0) & (segpos + off < seglen)
        acc = acc + jnp.where(ok, sh, 0.0) * cw[k:k + 1, :]
    return acc


def _lru_conv_t(dxc, cw):
    row = _row_ids(ZLEN, LRU_BLOCK_W)
    segpos = jnp.where(row < CTX_LEN, row, row - CTX_LEN)
    seglen = jnp.where(row < CTX_LEN, CTX_LEN, SEQ)
    acc = dxc * cw[2:3, :]
    for k in (0, 1, 3):
        off = k - 2
        sh = pltpu.roll(dxc, off % ZLEN, 0)
        ok = (segpos - off >= 0) & (segpos - off < seglen)
        acc = acc + jnp.where(ok, sh, 0.0) * cw[k:k + 1, :]
    return acc


def _lru_gates(xc, xcb, wa, ba, wx, bx, lam):
    r = _sigmoid(jnp.dot(xcb, wa, preferred_element_type=F32) + ba)
    i = _sigmoid(jnp.dot(xcb, wx, preferred_element_type=F32) + bx)
    sp = jnp.maximum(-lam, 0.0) + jnp.log1p(jnp.exp(-jnp.abs(lam)))
    la = (-LRU_C) * r * sp
    a = jnp.exp(la)
    sq = jnp.sqrt(-jnp.tanh(la) * (1.0 + a * a))
    b = sq * i * xc
    return r, i, sp, a, sq, b


def _scan8_fwd(a, b, rid):
    for s in (1, 2, 4):
        a_s = pltpu.roll(a, s, 0)
        b_s = pltpu.roll(b, s, 0)
        m = rid >= s
        b = jnp.where(m, a * b_s + b, b)
        a = jnp.where(m, a * a_s, a)
    return a, b


def _scan8_rev(a, b, rid):
    for s in (1, 2, 4):
        a_s = pltpu.roll(a, 8 - s, 0)
        b_s = pltpu.roll(b, 8 - s, 0)
        m = rid < 8 - s
        b = jnp.where(m, a * b_s + b, b)
        a = jnp.where(m, a * a_s, a)
    return a, b


N_CHUNK = ZLEN // 8
CTX_CHUNKS = CTX_LEN // 8
SCAN_UNROLL = 8


def _scan_up(a_ref, b_ref, h_ref, lo, hi, carry):
    rid = _row_ids(8, LRU_BLOCK_W)
    assert (hi - lo) % SCAN_UNROLL == 0

    def step(g, c):
        base = pl.multiple_of((lo + g * SCAN_UNROLL) * 8, 8)
        for u in range(SCAN_UNROLL):
            sl = pl.ds(base + 8 * u, 8)
            a, b = _scan8_fwd(a_ref[sl, :], b_ref[sl, :], rid)
            h_ref[sl, :] = b + a * c
            c = b[7:8, :] + a[7:8, :] * c
        return c

    return lax.fori_loop(0, (hi - lo) // SCAN_UNROLL, step, carry)


def _scan_down(a_ref, b_ref, h_ref, lo, hi, carry):
    rid = _row_ids(8, LRU_BLOCK_W)
    assert (hi - lo) % SCAN_UNROLL == 0

    def step(g, c):
        base = pl.multiple_of((hi - (g + 1) * SCAN_UNROLL) * 8, 8)
        for u in reversed(range(SCAN_UNROLL)):
            sl = pl.ds(base + 8 * u, 8)
            a, b = _scan8_rev(a_ref[sl, :], b_ref[sl, :], rid)
            h_ref[sl, :] = b + a * c
            c = b[0:1, :] + a[0:1, :] * c
        return c

    return lax.fori_loop(0, (hi - lo) // SCAN_UNROLL, step, carry)


def _lru_scan_dir(d, a_ref, b_ref, h_ref):
    zero = jnp.zeros((1, LRU_BLOCK_W), F32)
    if d == 0:
        _scan_up(a_ref, b_ref, h_ref, 0, N_CHUNK, zero)
    else:
        c = _scan_down(a_ref, b_ref, h_ref, 0, CTX_CHUNKS, zero)
        _scan_down(a_ref, b_ref, h_ref, CTX_CHUNKS, N_CHUNK, c)


def _lru_in_specs():
    blk = lambda rows: pl.BlockSpec((rows, LRU_BLOCK_W), lambda b: (0, b))
    wspec = pl.BlockSpec((2, 1, LRU_BLOCK_W, LRU_BLOCK_W), lambda b: (0, b, 0, 0))
    return blk, wspec


def lru_fwd(p, conv_w, conv_b, wa, ba, wx, bx, lam, comm=None):
    blk, wspec = _lru_in_specs()

    def body(xr_ref, gx_ref, cw_ref, cb_ref, wa_ref, ba_ref, wx_ref, bx_ref, lam_ref, y_ref, a_s, b_s, h_s, hsum_s):
        xr = xr_ref[...]
        xc = _lru_conv(xr, cw_ref[...], cb_ref[...])
        xcb = xc.astype(BF16)
        for d in (0, 1):
            _, _, _, a, _, b = _lru_gates(xc, xcb, wa_ref[d, 0].astype(BF16), ba_ref[d:d + 1, :],
                                          wx_ref[d, 0].astype(BF16), bx_ref[d:d + 1, :], lam_ref[d:d + 1, :])
            a_s[...] = a
            b_s[...] = b
            _lru_scan_dir(d, a_s, b_s, h_s)
            if d == 0:
                hsum_s[...] = h_s[...]
            else:
                hsum_s[...] = hsum_s[...] + h_s[...]
        g, _ = _gelu_parts(gx_ref[CTX_LEN:, :])
        y_ref[...] = (hsum_s[CTX_LEN:, :] * g).astype(BF16)

    zs = pltpu.VMEM((ZLEN, LRU_BLOCK_W), F32)
    res, extra = _call(
        body, name="lru_fwd", grid=(LRU_BLOCKS,),
        in_specs=[blk(ZLEN), pl.BlockSpec((ZLEN, LRU_BLOCK_W), lambda b: (0, 24 + b)), blk(4), blk(1),
                  wspec, blk(2), wspec, blk(2), blk(2)],
        out_specs=[pl.BlockSpec((SEQ, LRU_BLOCK_W), lambda b: (0, b))],
        out_shape=[jax.ShapeDtypeStruct((SEQ, D_MODEL), BF16)],
        scratch_shapes=[zs, zs, zs, zs], sem=("arbitrary",),
        args=(p, p, conv_w, conv_b, wa, ba, wx, bx, lam), comm=comm)
    return res[0], extra


def _rope_tables():
    t = np.arange(SEQ)
    lane = np.arange(2 * HEAD_DIM)
    in_head = lane % HEAD_DIM
    j = (in_head % 32) % 16
    freq = ROPE_BASE ** (-j.astype(np.float64) / 16.0)
    pos = np.where(in_head[None, :] < 32, (t // GRID_W)[:, None], (t % GRID_W)[:, None]).astype(np.float64)
    ang = (pos.astype(np.float32) * freq.astype(np.float32)[None, :]).astype(np.float32)
    cos = np.cos(ang).astype(np.float32)
    sin = np.sin(ang).astype(np.float32)
    sgn = np.where((in_head % 32) < 16, -1.0, 1.0).astype(np.float32)
    cos = np.concatenate([np.ones((CTX_LEN, 2 * HEAD_DIM), np.float32), cos], 0)
    sin = np.concatenate([np.zeros((CTX_LEN, 2 * HEAD_DIM), np.float32), sin * sgn[None, :]], 0)
    return jnp.asarray(cos), jnp.asarray(sin)


def _head_ones():
    lane = np.arange(2 * HEAD_DIM)
    return jnp.asarray((lane[:, None] // HEAD_DIM == lane[None, :] // HEAD_DIM).astype(np.float32))


def _rope_partner(x):
    lane = lax.broadcasted_iota(jnp.int32, x.shape, 1)
    return jnp.where((lane % 32) < 16, pltpu.roll(x, 128 - 16, 1), pltpu.roll(x, 16, 1))


def _head_sum(t, ones):
    hi = t.astype(BF16)
    lo = (t - hi.astype(F32)).astype(BF16)
    ones_b = ones.astype(BF16)
    return jnp.dot(hi, ones_b, preferred_element_type=F32) + jnp.dot(lo, ones_b, preferred_element_type=F32)


def _head_rms(x, ones, gain):
    ms = _head_sum(x * x, ones) * (1.0 / HEAD_DIM)
    rstd = lax.rsqrt(ms + EPS)
    return x * rstd * gain, rstd


PREP_TILE = 768


def qkv_prep(p, qg2, kg2, cos, sin, ones, comm=None):
    scale = HEAD_DIM ** -0.5

    def body(q_ref, k_ref, v_ref, qg_ref, kg_ref, cos_ref, sin_ref, ones_ref, qr_ref, qp_ref, kk_ref, vv_ref):
        ones_m = ones_ref[...]
        c, s = cos_ref[...], sin_ref[...]
        qn, _ = _head_rms(q_ref[...], ones_m, qg_ref[...])
        qn = qn * scale
        qr_ref[...] = (qn * c + _rope_partner(qn) * s).astype(BF16)
        qp_ref[...] = qn.astype(BF16)
        kn, _ = _head_rms(k_ref[...], ones_m, kg_ref[...])
        kk_ref[...] = (kn * c + _rope_partner(kn) * s).astype(BF16)
        vv_ref[...] = v_ref[...].astype(BF16)

    col = lambda base: pl.BlockSpec((PREP_TILE, 128), lambda hp, i: (i, base + hp))
    small = pl.BlockSpec((1, 128), lambda hp, i: (0, 0))
    tab = pl.BlockSpec((PREP_TILE, 128), lambda hp, i: (i, 0))
    oshape = jax.ShapeDtypeStruct((ZLEN, D_MODEL), BF16)
    res, extra = _call(
        body, name="qkv_prep", grid=(NA_HEADS // 2, ZLEN // PREP_TILE),
        in_specs=[col(32), col(8), col(16), small, small, tab, tab, _full((128, 128))],
        out_specs=[col(0)] * 4, out_shape=[oshape] * 4, sem=("parallel", "parallel"),
        args=(p, p, p, qg2, kg2, cos, sin, ones), comm=comm)
    return (*res, extra)


def _bias_expand():
    qc = np.arange(GRID_W)[:, None]
    kc = np.arange(GRID_W)[None, :]
    col_start = np.clip(qc - NA_COLS // 2, 0, GRID_W - NA_COLS)
    in_win = (kc >= col_start) & (kc < col_start + NA_COLS)
    dc = np.clip(kc - qc, -(NA_COLS - 1), NA_COLS - 1) + (NA_COLS - 1)
    e = np.zeros((2 * NA_COLS - 1, GRID_W, GRID_W), np.float32)
    for d in range(2 * NA_COLS - 1):
        e[d] = ((dc == d) & in_win).astype(np.float32)
    pen = np.where(in_win, 0.0, NEG_INF).astype(np.float32)
    return e, pen


def bias_table(rpb2, comm=None):
    e, pen = _bias_expand()
    n_dr = 2 * NA_ROWS - 1
    ea = np.zeros((31, GRID_W, 128), np.float32)
    ea[:, :, :GRID_W] = e
    eb = np.zeros((31, GRID_W, 128), np.float32)
    eb[:, :, GRID_W:] = e
    pen2 = np.concatenate([pen, pen], 1)
    ea = jnp.asarray(ea.reshape(31, GRID_W * 128))
    eb = jnp.asarray(eb.reshape(31, GRID_W * 128))
    sel_a = np.zeros((BT_LEN, n_dr), np.float32)
    sel_b = np.zeros((BT_LEN, n_dr), np.float32)
    for r in range(BT_LEN):
        dr = r - BT_PAD
        if 0 <= dr < n_dr:
            sel_a[r, dr] = 1.0
        if 0 <= dr + 1 < n_dr:
            sel_b[r, dr + 1] = 1.0
    sel_a, sel_b = jnp.asarray(sel_a), jnp.asarray(sel_b)
    pen2 = jnp.asarray(pen2.reshape(1, GRID_W * 128))
    hi = lax.Precision.HIGHEST

    def body(rpb_ref, sa_ref, sb_ref, ea_ref, eb_ref, pen_ref, o_ref, ra_s, rb_s):
        for h in range(NA_HEADS):
            rp = rpb_ref[h]
            ra_s[h * BT_LEN:(h + 1) * BT_LEN, :] = jnp.dot(sa_ref[...], rp, preferred_element_type=F32, precision=hi)
            rb_s[h * BT_LEN:(h + 1) * BT_LEN, :] = jnp.dot(sb_ref[...], rp, preferred_element_type=F32, precision=hi)
        o_ref[...] = (jnp.dot(ra_s[...], ea_ref[...], preferred_element_type=F32, precision=hi)
                      + jnp.dot(rb_s[...], eb_ref[...], preferred_element_type=F32, precision=hi) + pen_ref[...])

    tcol = 2048
    rows = NA_HEADS * BT_LEN
    res, extra = _call(
        body, name="bias_table", grid=(GRID_W * 128 // tcol,),
        in_specs=[_full((NA_HEADS, n_dr, 31)), _full((BT_LEN, n_dr)), _full((BT_LEN, n_dr)),
                  pl.BlockSpec((31, tcol), lambda j: (0, j)), pl.BlockSpec((31, tcol), lambda j: (0, j)),
                  pl.BlockSpec((1, tcol), lambda j: (0, j))],
        out_specs=[pl.BlockSpec((rows, tcol), lambda j: (0, j))],
        out_shape=[jax.ShapeDtypeStruct((rows, GRID_W * 128), F32)],
        scratch_shapes=[pltpu.VMEM((rows, 31), F32), pltpu.VMEM((rows, 31), F32)], sem=("parallel",),
        args=(rpb2, sel_a, sel_b, ea, eb, pen2), comm=comm)
    return res[0].reshape(NA_HEADS, BT_LEN, GRID_W, 128), extra


def _key_window(j):
    ws = jnp.clip(Q_ROWS * j - 4, 0, GRID_ROWS - KEY_ROWS)
    return ws, pl.multiple_of(CTX_LEN + ws * GRID_W, 256)


def _head_mask(hh):
    lane = lax.broadcasted_iota(jnp.int32, (Q_TILE, 128), 1)
    return (lane < HEAD_DIM) if hh == 0 else (lane >= HEAD_DIM)


def _attn_scores(j, ws, q_rot_h, q_pl_h, kw, kc, hh, bt_ref, s_ref):
    s_ref[:, :KEY_TILE] = _dot_nt(q_rot_h, kw)
    s_ref[:, KEY_TILE:] = _dot_nt(q_pl_h, kc)
    lane = lax.broadcasted_iota(jnp.int32, (GRID_W, 128), 1)
    base = ws - Q_ROWS * j + (NA_ROWS - 1) + BT_PAD
    for qi in range(Q_ROWS):
        rs = jnp.clip(Q_ROWS * j + qi - NA_ROWS // 2, 0, GRID_ROWS - NA_ROWS)
        for m in range(KEY_ROWS // 2):
            k0 = ws + 2 * m
            p0 = jnp.where((k0 >= rs) & (k0 < rs + NA_ROWS), 0.0, NEG_INF)
            p1 = jnp.where((k0 + 1 >= rs) & (k0 + 1 < rs + NA_ROWS), 0.0, NEG_INF)
            pen = jnp.where(lane < GRID_W, p0, p1)
            rows = slice(qi * GRID_W, (qi + 1) * GRID_W)
            cols = slice(128 * m, 128 * (m + 1))
            s_ref[rows, cols] = s_ref[rows, cols] + bt_ref[hh, base + 2 * m - qi] + pen
    return base


def attn_fwd(q_rot, q_pl, kk, vv, bt, comm=None):
    def body(qr_ref, qp_ref, kk_ref, vv_ref, bt_ref, o_ref, lse_ref, s_ref):
        j = pl.program_id(1)
        ws, start = _key_window(j)
        win = pl.ds(start, KEY_TILE)
        kw, kc = kk_ref[win, :], kk_ref[:CTX_LEN, :]
        vw, vc = vv_ref[win, :], vv_ref[:CTX_LEN, :]
        qr, qp = qr_ref[...], qp_ref[...]
        outs = []
        for hh in range(2):
            msk = _head_mask(hh)
            _attn_scores(j, ws, jnp.where(msk, qr, 0), jnp.where(msk, qp, 0), kw, kc, hh, bt_ref, s_ref)
            s = s_ref[...]
            mx = jnp.max(s, axis=-1, keepdims=True)
            pr = jnp.exp(s - mx)
            l = jnp.sum(pr, axis=-1, keepdims=True)
            prb = pr.astype(BF16)
            o = jnp.dot(prb[:, :KEY_TILE], vw, preferred_element_type=F32)
            o = o + jnp.dot(prb[:, KEY_TILE:], vc, preferred_element_type=F32)
            outs.append(o / l)
            lse_ref[hh] = mx + jnp.log(l)
        o_ref[...] = jnp.where(_head_mask(0), outs[0], outs[1])

    qspec = pl.BlockSpec((Q_TILE, 128), lambda hp, j: (j + 1, hp))
    kspec = pl.BlockSpec((ZLEN, 128), lambda hp, j: (0, hp))
    res, extra = _call(
        body, name="attn_fwd", grid=(NA_HEADS // 2, SEQ // Q_TILE),
        in_specs=[qspec, qspec, kspec, kspec, pl.BlockSpec((2, BT_LEN, GRID_W, 128), lambda hp, j: (hp, 0, 0, 0))],
        out_specs=[pl.BlockSpec((Q_TILE, 128), lambda hp, j: (j, hp)),
                   pl.BlockSpec((2, Q_TILE, 1), lambda hp, j: (hp, j, 0))],
        out_shape=[jax.ShapeDtypeStruct((SEQ, D_MODEL), F32), jax.ShapeDtypeStruct((NA_HEADS, SEQ, 1), F32)],
        scratch_shapes=[pltpu.VMEM((Q_TILE, KEY_TILE + CTX_LEN), F32)], sem=("parallel", "arbitrary"),
        args=(q_rot, q_pl, kk, vv, bt), comm=comm)
    return res[0], res[1], extra


def merge_fwd(y_rnn, y_na, p, z, g2, w_rnn, w_na, w_out):
    def body(yr_ref, yn_ref, mr_ref, mn_ref, x_ref, g2_ref, wr_ref, wn_ref, wo_ref, u_ref, v_ref, mg_ref, out_ref, x1_ref):
        u = jnp.dot(yr_ref[...], wr_ref[...], preferred_element_type=F32)
        v = jnp.dot(yn_ref[...].astype(BF16), wn_ref[...], preferred_element_type=F32)
        merged = (_sigmoid(mr_ref[...]) * u + _sigmoid(mn_ref[...]) * v).astype(BF16)
        out = jnp.dot(merged, wo_ref[...], preferred_element_type=F32)
        u_ref[...] = u
        v_ref[...] = v
        mg_ref[...] = merged
        out_ref[...] = out
        x1_ref[...] = x_ref[...] + g2_ref[...] * out

    row = pl.BlockSpec((ROW_TILE, D_MODEL), lambda i: (i, 0))
    lat = lambda cb: pl.BlockSpec((ROW_TILE, D_MODEL), lambda i: (i + 1, cb))
    wspec = _full((D_MODEL, D_MODEL))
    f32o = jax.ShapeDtypeStruct((SEQ, D_MODEL), F32)
    return pl.pallas_call(
        body, name="merge_fwd", grid=(SEQ // ROW_TILE,),
        in_specs=[row, row, lat(5), lat(6), lat(0), _full((1, D_MODEL)), wspec, wspec, wspec],
        out_specs=[row] * 5,
        out_shape=[f32o, f32o, jax.ShapeDtypeStruct((SEQ, D_MODEL), BF16), f32o, f32o],
        compiler_params=_params("parallel"),
    )(y_rnn, y_na, p, p, z, g2, w_rnn, w_na, w_out)


FF_TILE = 256
FF_TILES = D_FF // FF_TILE


FF_ROWS = 64
FF_HALO = 8
FF_SLAB = FF_ROWS + 2 * FF_HALO


def _ffn_row_chunks(chunk, init):
    carry = chunk(0, 0, -1, init)
    carry = lax.fori_loop(1, SEQ // FF_ROWS - 1,
                          lambda ci, cr: chunk(pl.multiple_of(ci * FF_ROWS - FF_HALO, 8), FF_HALO, 0, cr), carry)
    return chunk(SEQ - FF_SLAB, 2 * FF_HALO, 1, carry)


def _ffn_shifts(edge):
    row = _row_ids(FF_SLAB, FF_TILE)

    def prev(x):
        r = pltpu.roll(x, 1, 0)
        return jnp.where(row >= 1, r, 0.0) if edge == -1 else r

    def nxt(x):
        r = pltpu.roll(x, FF_SLAB - 1, 0)
        return jnp.where(row < FF_SLAB - 1, r, 0.0) if edge == 1 else r

    return prev, nxt


def ffn_act(hpre, conv_w, conv_b):
    def body(ha_ref, hg_ref, wa_ref, wg_ref, ba_ref, bg_ref, o_ref):
        wa, wg, ba, bg = wa_ref[...], wg_ref[...], ba_ref[...], bg_ref[...]

        def chunk(lo, mid, edge, carry):
            prev, nxt = _ffn_shifts(edge)
            ha, hg = ha_ref[pl.ds(lo, FF_SLAB), :], hg_ref[pl.ds(lo, FF_SLAB), :]
            a = prev(ha) * wa[0:1] + ha * wa[1:2] + nxt(ha) * wa[2:3] + ba
            g = prev(hg) * wg[0:1] + hg * wg[1:2] + nxt(hg) * wg[2:3] + bg
            o_ref[pl.ds(lo + mid, FF_ROWS), :] = (a * _sigmoid(a) * g)[mid:mid + FF_ROWS].astype(BF16)
            return carry

        _ffn_row_chunks(chunk, 0)

    col = lambda rows, off: pl.BlockSpec((rows, FF_TILE), lambda j: (0, j + off))
    return pl.pallas_call(
        body, name="ffn_act", grid=(FF_TILES,),
        in_specs=[col(SEQ, 0), col(SEQ, FF_TILES), col(3, 0), col(3, FF_TILES), col(1, 0), col(1, FF_TILES)],
        out_specs=col(SEQ, 0),
        out_shape=jax.ShapeDtypeStruct((SEQ, D_FF), BF16),
        compiler_params=_params("parallel"),
    )(hpre, hpre, conv_w, conv_w, conv_b, conv_b)


def ffn_down_loss(act, w_down, x1, g5, target):
    def body(a_ref, w_ref, x1_ref, g5_ref, t_ref, f_ref, dy_ref, df_ref, ls_ref, dg_ref):
        i = pl.program_id(0)
        f = jnp.dot(a_ref[...], w_ref[...], preferred_element_type=F32)
        g5 = g5_ref[...]
        err = x1_ref[...] + g5 * f - t_ref[...]
        dy = err * (1.0 / D_MODEL)
        f_ref[...] = f
        dy_ref[...] = dy
        df_ref[...] = (dy * g5).astype(BF16)

        @pl.when(i == 0)
        def _():
            ls_ref[...] = jnp.zeros_like(ls_ref)
            dg_ref[...] = jnp.zeros_like(dg_ref)

        ls_ref[...] = ls_ref[...] + jnp.sum(err * err)
        dg_ref[...] = dg_ref[...] + jnp.sum(dy * f, axis=0, keepdims=True)

    row = pl.BlockSpec((ROW_TILE, D_MODEL), lambda i: (i, 0))
    f32o = jax.ShapeDtypeStruct((SEQ, D_MODEL), F32)
    return pl.pallas_call(
        body, name="ffn_down_loss", grid=(SEQ // ROW_TILE,),
        in_specs=[pl.BlockSpec((ROW_TILE, D_FF), lambda i: (i, 0)), _full((D_FF, D_MODEL)), row, _full((1, D_MODEL)), row],
        out_specs=[row, row, row, _full((8, 128)), _full((1, D_MODEL))],
        out_shape=[f32o, f32o, jax.ShapeDtypeStruct((SEQ, D_MODEL), BF16), jax.ShapeDtypeStruct((8, 128), F32),
                   jax.ShapeDtypeStruct((1, D_MODEL), F32)],
        compiler_params=_params("arbitrary"),
    )(act, w_down, x1, g5, target)


def ffn_down_bwd(df, w_down):
    def body(df_ref, w_ref, o_ref):
        o_ref[...] = _dot_nt(df_ref[...], w_ref[...])

    return pl.pallas_call(
        body, name="ffn_down_bwd", grid=(SEQ // ROW_TILE,),
        in_specs=[pl.BlockSpec((ROW_TILE, D_MODEL), lambda i: (i, 0)), _full((D_FF, D_MODEL))],
        out_specs=pl.BlockSpec((ROW_TILE, D_FF), lambda i: (i, 0)),
        out_shape=jax.ShapeDtypeStruct((SEQ, D_FF), F32),
        compiler_params=_params("parallel"),
    )(df, w_down)


def ffn_act_bwd(hpre, d_act, conv_w, conv_b):
    def body(ha_ref, hg_ref, da_ref, wa_ref, wg_ref, ba_ref, bg_ref, dha_ref, dhg_ref, dwa_ref, dwg_ref, dba_ref, dbg_ref):
        wa, wg, ba, bg = wa_ref[...], wg_ref[...], ba_ref[...], bg_ref[...]

        def chunk(lo, mid, edge, acc):
            prev, nxt = _ffn_shifts(edge)
            rows = pl.ds(lo, FF_SLAB)
            ha, hg, dact = ha_ref[rows, :], hg_ref[rows, :], da_ref[rows, :]
            hap, han, hgp, hgn = prev(ha), nxt(ha), prev(hg), nxt(hg)
            a = hap * wa[0:1] + ha * wa[1:2] + han * wa[2:3] + ba
            g = hgp * wg[0:1] + hg * wg[1:2] + hgn * wg[2:3] + bg
            sig = _sigmoid(a)
            dca = dact * g * (sig * (1.0 + a * (1.0 - sig)))
            dcg = dact * a * sig
            m = slice(mid, mid + FF_ROWS)
            sums = []
            for dc, h, hp, hn, w, dh_ref in ((dca, ha, hap, han, wa, dha_ref), (dcg, hg, hgp, hgn, wg, dhg_ref)):
                dcm = dc[m]
                sums += [jnp.sum(dcm * hp[m], axis=0, keepdims=True), jnp.sum(dcm * h[m], axis=0, keepdims=True),
                         jnp.sum(dcm * hn[m], axis=0, keepdims=True), jnp.sum(dcm, axis=0, keepdims=True)]
                dh = nxt(dc) * w[0:1] + dc * w[1:2] + prev(dc) * w[2:3]
                dh_ref[pl.ds(lo + mid, FF_ROWS), :] = dh[m].astype(BF16)
            return tuple(x + y for x, y in zip(acc, sums))

        acc = _ffn_row_chunks(chunk, tuple(jnp.zeros((1, FF_TILE), F32) for _ in range(8)))
        dwa_ref[0:1, :], dwa_ref[1:2, :], dwa_ref[2:3, :], dba_ref[...] = acc[0], acc[1], acc[2], acc[3]
        dwg_ref[0:1, :], dwg_ref[1:2, :], dwg_ref[2:3, :], dbg_ref[...] = acc[4], acc[5], acc[6], acc[7]

    col = lambda rows, off: pl.BlockSpec((rows, FF_TILE), lambda j: (0, j + off))
    hshape = jax.ShapeDtypeStruct((SEQ, D_FF), BF16)
    wshape = jax.ShapeDtypeStruct((3, D_FF), F32)
    bshape = jax.ShapeDtypeStruct((1, D_FF), F32)
    return pl.pallas_call(
        body, name="ffn_act_bwd", grid=(FF_TILES,),
        in_specs=[col(SEQ, 0), col(SEQ, FF_TILES), col(SEQ, 0), col(3, 0), col(3, FF_TILES), col(1, 0), col(1, FF_TILES)],
        out_specs=[col(SEQ, 0), col(SEQ, 0), col(3, 0), col(3, 0), col(1, 0), col(1, 0)],
        out_shape=[hshape, hshape, wshape, wshape, bshape, bshape],
        compiler_params=_params("parallel"),
    )(hpre, hpre, d_act, conv_w, conv_w, conv_b, conv_b)


def _norm_mod_bwd(x, dxn, gain, scale):
    rstd = lax.rsqrt(jnp.mean(x * x, axis=-1, keepdims=True) + EPS)
    nrm = x * rstd
    dsh = jnp.sum(dxn, axis=0, keepdims=True)
    dsc = jnp.sum(dxn * nrm, axis=0, keepdims=True) * gain
    dgn = jnp.sum(dxn * nrm, axis=0, keepdims=True) * (1.0 + scale)
    dn = dxn * (gain * (1.0 + scale))
    dx = rstd * (dn - nrm * jnp.mean(dn * nrm, axis=-1, keepdims=True))
    return dx, dsh, dsc, dgn


def ffn_up_bwd(dha, dhg, w_up, x1, dy, gain, scale):
    def body(dha_ref, dhg_ref, w_ref, x_ref, dy_ref, g_ref, sc_ref, dx_ref, dsh_ref, dsc_ref, dgn_ref):
        i = pl.program_id(0)
        dxn = _dot_nt(dha_ref[...], w_ref[:, :D_FF]) + _dot_nt(dhg_ref[...], w_ref[:, D_FF:])
        dx, dsh, dsc, dgn = _norm_mod_bwd(x_ref[...], dxn, g_ref[...], sc_ref[...])
        dx_ref[...] = dy_ref[...] + dx

        @pl.when(i == 0)
        def _():
            dsh_ref[...] = dsh
            dsc_ref[...] = dsc
            dgn_ref[...] = dgn

        @pl.when(i > 0)
        def _():
            dsh_ref[...] = dsh_ref[...] + dsh
            dsc_ref[...] = dsc_ref[...] + dsc
            dgn_ref[...] = dgn_ref[...] + dgn

    row = pl.BlockSpec((ROW_TILE, D_MODEL), lambda i: (i, 0))
    vec = _full((1, D_MODEL))
    vshape = jax.ShapeDtypeStruct((1, D_MODEL), F32)
    return pl.pallas_call(
        body, name="ffn_up_bwd", grid=(SEQ // ROW_TILE,),
        in_specs=[pl.BlockSpec((ROW_TILE, D_FF), lambda i: (i, 0)), pl.BlockSpec((ROW_TILE, D_FF), lambda i: (i, 0)),
                  _full((D_MODEL, 2 * D_FF)), row, row, vec, vec],
        out_specs=[row, vec, vec, vec],
        out_shape=[jax.ShapeDtypeStruct((SEQ, D_MODEL), F32), vshape, vshape, vshape],
        compiler_params=_params("arbitrary"),
    )(dha, dhg, w_up, x1, dy, gain, scale)


def merge_bwd(dx1, out, g2, p, u, v, w_rnn, w_na, w_out, comm=None):
    def body(dx_ref, out_ref, g2_ref, mr_ref, mn_ref, u_ref, v_ref, wr_ref, wn_ref, wo_ref,
             dout_ref, du_ref, dv_ref, dmr_ref, dmn_ref, dyr_ref, dyn_ref, dg2_ref):
        i = pl.program_id(0)

        @pl.when(i == 0)
        def _():
            dmr_ref[...] = jnp.zeros_like(dmr_ref)
            dmn_ref[...] = jnp.zeros_like(dmn_ref)
            dg2_ref[...] = jnp.zeros_like(dg2_ref)

        @pl.when(i > 0)
        def _():
            dx = dx_ref[...]
            dg2_ref[...] = dg2_ref[...] + jnp.sum(dx * out_ref[...], axis=0, keepdims=True)
            dout = (dx * g2_ref[...]).astype(BF16)
            dout_ref[...] = dout
            dm = _dot_nt(dout, wo_ref[...])
            sr = _sigmoid(mr_ref[...])
            sn = _sigmoid(mn_ref[...])
            du = (dm * sr).astype(BF16)
            dv = (dm * sn).astype(BF16)
            du_ref[...] = du
            dv_ref[...] = dv
            dmr_ref[...] = (dm * u_ref[...] * (sr * (1.0 - sr))).astype(BF16)
            dmn_ref[...] = (dm * v_ref[...] * (sn * (1.0 - sn))).astype(BF16)
            dyr_ref[...] = _dot_nt(du, wr_ref[...])
            dyn_ref[...] = _dot_nt(dv, wn_ref[...])

    lat = pl.BlockSpec((ROW_TILE, D_MODEL), lambda i: (jnp.maximum(i - 1, 0), 0))
    zrow = pl.BlockSpec((ROW_TILE, D_MODEL), lambda i: (i, 0))
    pcol = lambda cb: pl.BlockSpec((ROW_TILE, D_MODEL), lambda i: (i, cb))
    wspec = _full((D_MODEL, D_MODEL))
    tb = jax.ShapeDtypeStruct((SEQ, D_MODEL), BF16)
    zb = jax.ShapeDtypeStruct((ZLEN, D_MODEL), BF16)
    tf = jax.ShapeDtypeStruct((SEQ, D_MODEL), F32)
    res, extra = _call(
        body, name="merge_bwd", grid=(ZLEN // ROW_TILE,),
        in_specs=[lat, lat, _full((1, D_MODEL)), pcol(5), pcol(6), lat, lat, wspec, wspec, wspec],
        out_specs=[lat, lat, lat, zrow, zrow, lat, lat, _full((1, D_MODEL))],
        out_shape=[tb, tb, tb, zb, zb, tf, tf, jax.ShapeDtypeStruct((1, D_MODEL), F32)],
        sem=("arbitrary",), args=(dx1, out, g2, p, p, u, v, w_rnn, w_na, w_out), comm=comm)
    return (*res, extra)


def attn_bwd(q_rot, q_pl, kk, vv, bt, y_na, d_yna, lse, comm=None):
    def body(qr_ref, qp_ref, kk_ref, vv_ref, bt_ref, o_ref, do_ref, lse_ref,
             dqr_ref, dqp_ref, dk_ref, dv_ref, dbt_ref, s_ref):
        jj = pl.program_id(1)

        @pl.when(jj == 0)
        def _():
            dqr_ref[...] = jnp.zeros_like(dqr_ref)
            dqp_ref[...] = jnp.zeros_like(dqp_ref)
            dk_ref[...] = jnp.zeros_like(dk_ref)
            dv_ref[...] = jnp.zeros_like(dv_ref)
            dbt_ref[...] = jnp.zeros_like(dbt_ref)

        @pl.when(jj > 0)
        def _():
            j = jj - 1
            ws, start = _key_window(j)
            win = pl.ds(start, KEY_TILE)
            kw, kc = kk_ref[win, :], kk_ref[:CTX_LEN, :]
            vw, vc = vv_ref[win, :], vv_ref[:CTX_LEN, :]
            qr, qp = qr_ref[...], qp_ref[...]
            do = do_ref[...]
            do_o = do * o_ref[...]
            dq_r, dq_p = [], []
            for hh in range(2):
                msk = _head_mask(hh)
                q_r, q_p = jnp.where(msk, qr, 0), jnp.where(msk, qp, 0)
                base = _attn_scores(j, ws, q_r, q_p, kw, kc, hh, bt_ref, s_ref)
                pr = jnp.exp(s_ref[...] - lse_ref[hh])
                delta = jnp.sum(jnp.where(msk, do_o, 0.0), axis=-1, keepdims=True)
                dob = jnp.where(msk, do, 0.0).astype(BF16)
                ds_lat = pr[:, :KEY_TILE] * (_dot_nt(dob, vw) - delta)
                ds_ctx = pr[:, KEY_TILE:] * (_dot_nt(dob, vc) - delta)
                for qi in range(Q_ROWS):
                    for m in range(KEY_ROWS // 2):
                        idx = base + 2 * m - qi
                        dbt_ref[hh, idx] = dbt_ref[hh, idx] + ds_lat[qi * GRID_W:(qi + 1) * GRID_W, 128 * m:128 * (m + 1)]
                dsb_lat = ds_lat.astype(BF16)
                dsb_ctx = ds_ctx.astype(BF16)
                prb = pr.astype(BF16)
                dq_r.append(jnp.dot(dsb_lat, kw, preferred_element_type=F32))
                dq_p.append(jnp.dot(dsb_ctx, kc, preferred_element_type=F32))
                dk_ref[win, :] = dk_ref[win, :] + _dot_tn(dsb_lat, q_r)
                dk_ref[:CTX_LEN, :] = dk_ref[:CTX_LEN, :] + _dot_tn(dsb_ctx, q_p)
                dv_ref[win, :] = dv_ref[win, :] + _dot_tn(prb[:, :KEY_TILE], dob)
                dv_ref[:CTX_LEN, :] = dv_ref[:CTX_LEN, :] + _dot_tn(prb[:, KEY_TILE:], dob)
            dqr_ref[...] = jnp.where(_head_mask(0), dq_r[0], dq_r[1])
            dqp_ref[...] = jnp.where(_head_mask(0), dq_p[0], dq_p[1])

    lat = lambda jj: jnp.maximum(jj - 1, 0)
    qspec = pl.BlockSpec((Q_TILE, 128), lambda hp, jj: (lat(jj) + 1, hp))
    kspec = pl.BlockSpec((ZLEN, 128), lambda hp, jj: (0, hp))
    btspec = pl.BlockSpec((2, BT_LEN, GRID_W, 128), lambda hp, jj: (hp, 0, 0, 0))
    ospec = pl.BlockSpec((Q_TILE, 128), lambda hp, jj: (lat(jj), hp))
    dqspec = pl.BlockSpec((Q_TILE, 128), lambda hp, jj: (jj, hp))
    zshape = jax.ShapeDtypeStruct((ZLEN, D_MODEL), F32)
    res, extra = _call(
        body, name="attn_bwd", grid=(NA_HEADS // 2, ZLEN // Q_TILE),
        in_specs=[qspec, qspec, kspec, kspec, btspec, ospec, ospec,
                  pl.BlockSpec((2, Q_TILE, 1), lambda hp, jj: (hp, lat(jj), 0))],
        out_specs=[dqspec, dqspec, kspec, kspec, btspec],
        out_shape=[zshape, zshape, zshape, zshape, jax.ShapeDtypeStruct((NA_HEADS, BT_LEN, GRID_W, 128), F32)],
        scratch_shapes=[pltpu.VMEM((Q_TILE, KEY_TILE + CTX_LEN), F32)], sem=("parallel", "arbitrary"),
        args=(q_rot, q_pl, kk, vv, bt, y_na, d_yna, lse), comm=comm)
    return (*res, extra)


def qkv_bwd(dq_rot, dq_pl, dk, dv, p, qg2, kg2, cos, sin, ones, comm=None):
    scale = HEAD_DIM ** -0.5
    n_hp, n_i = NA_HEADS // 2, ZLEN // PREP_TILE

    def norm_rope_bwd(d_rot, d_extra, x, gain, cos_t, sin_t, ones_m, dx_ref, acc_ref):
        xh, rstd = _head_rms(x, ones_m, 1.0)
        dn = d_rot * cos_t + _rope_partner(d_rot * sin_t)
        if d_extra is not None:
            dn = (dn + d_extra) * scale
        acc_ref[...] = acc_ref[...] + jnp.sum(dn * xh, axis=0, keepdims=True)
        dxh = dn * gain
        seg = _head_sum(dxh * xh, ones_m) * (1.0 / HEAD_DIM)
        dx_ref[...] = (rstd * (dxh - xh * seg)).astype(BF16)

    def body(dqr_ref, dqp_ref, dk_ref, dv_ref, xq_ref, xk_ref, qg_ref, kg_ref, cos_ref, sin_ref, ones_ref,
             dxq_ref, dxk_ref, dxv_ref, dgq_ref, dgk_ref, accq_ref, acck_ref):
        hp, i = pl.program_id(0), pl.program_id(1)

        @pl.when((hp == 0) & (i == 0))
        def _():
            accq_ref[...] = jnp.zeros_like(accq_ref)
            acck_ref[...] = jnp.zeros_like(acck_ref)

        ones_m = ones_ref[...]
        cos_t, sin_t = cos_ref[...], sin_ref[...]
        norm_rope_bwd(dqr_ref[...], dqp_ref[...], xq_ref[...], qg_ref[...], cos_t, sin_t, ones_m, dxq_ref, accq_ref)
        norm_rope_bwd(dk_ref[...], None, xk_ref[...], kg_ref[...], cos_t, sin_t, ones_m, dxk_ref, acck_ref)
        dxv_ref[...] = dv_ref[...].astype(BF16)

        @pl.when((hp == n_hp - 1) & (i == n_i - 1))
        def _():
            dgq_ref[...] = accq_ref[:, :HEAD_DIM] + accq_ref[:, HEAD_DIM:]
            dgk_ref[...] = acck_ref[:, :HEAD_DIM] + acck_ref[:, HEAD_DIM:]

    col = lambda base: pl.BlockSpec((PREP_TILE, 128), lambda hp, i: (i, base + hp))
    small = pl.BlockSpec((1, 128), lambda hp, i: (0, 0))
    tab = pl.BlockSpec((PREP_TILE, 128), lambda hp, i: (i, 0))
    zb = jax.ShapeDtypeStruct((ZLEN, D_MODEL), BF16)
    gshape = jax.ShapeDtypeStruct((1, HEAD_DIM), F32)
    res, extra = _call(
        body, name="qkv_bwd", grid=(n_hp, n_i),
        in_specs=[col(0)] * 4 + [col(32), col(8), small, small, tab, tab, _full((128, 128))],
        out_specs=[col(0)] * 3 + [_full((1, HEAD_DIM))] * 2,
        out_shape=[zb, zb, zb, gshape, gshape],
        scratch_shapes=[pltpu.VMEM((1, 128), F32)] * 2, sem=("arbitrary", "arbitrary"),
        args=(dq_rot, dq_pl, dk, dv, p, p, qg2, kg2, cos, sin, ones), comm=comm)
    return (*res, extra)


def rpb_grad(dbt):
    e, _ = _bias_expand()
    n_dr = 2 * NA_ROWS - 1
    ea = np.zeros((31, GRID_W, 128), np.float32)
    ea[:, :, :GRID_W] = e
    eb = np.zeros((31, GRID_W, 128), np.float32)
    eb[:, :, GRID_W:] = e
    eat = jnp.asarray(ea.reshape(31, GRID_W * 128).T.copy())
    ebt = jnp.asarray(eb.reshape(31, GRID_W * 128).T.copy())
    sel_at = np.zeros((n_dr, BT_LEN), np.float32)
    sel_bt = np.zeros((n_dr, BT_LEN), np.float32)
    for r in range(BT_LEN):
        dr = r - BT_PAD
        if 0 <= dr < n_dr:
            sel_at[dr, r] = 1.0
        if 0 <= dr + 1 < n_dr:
            sel_bt[dr + 1, r] = 1.0
    hi = lax.Precision.HIGHEST

    tk = 2048
    wide = GRID_W * 128
    rows = NA_HEADS * BT_LEN
    n_k = wide // tk

    def body(d_ref, sa_ref, sb_ref, ea_ref, eb_ref, o_ref, a_s, b_s):
        k = pl.program_id(0)
        dm = d_ref[...]
        d_hi = dm.astype(BF16)
        rest = dm - d_hi.astype(F32)
        d_mid = rest.astype(BF16)
        d_lo = (rest - d_mid.astype(F32)).astype(BF16)
        ea_b, eb_b = ea_ref[...].astype(BF16), eb_ref[...].astype(BF16)
        a = sum(jnp.dot(t, ea_b, preferred_element_type=F32) for t in (d_hi, d_mid, d_lo))
        b = sum(jnp.dot(t, eb_b, preferred_element_type=F32) for t in (d_hi, d_mid, d_lo))

        @pl.when(k == 0)
        def _():
            a_s[...] = a
            b_s[...] = b

        @pl.when(k > 0)
        def _():
            a_s[...] = a_s[...] + a
            b_s[...] = b_s[...] + b

        @pl.when(k == n_k - 1)
        def _():
            for h in range(NA_HEADS):
                sl = slice(h * BT_LEN, (h + 1) * BT_LEN)
                o_ref[h] = (jnp.dot(sa_ref[...], a_s[sl, :], preferred_element_type=F32, precision=hi)
                            + jnp.dot(sb_ref[...], b_s[sl, :], preferred_element_type=F32, precision=hi))

    return pl.pallas_call(
        body, name="rpb_grad", grid=(n_k,),
        in_specs=[pl.BlockSpec((rows, tk), lambda k: (0, k)), _full((n_dr, BT_LEN)), _full((n_dr, BT_LEN)),
                  pl.BlockSpec((tk, 31), lambda k: (k, 0)), pl.BlockSpec((tk, 31), lambda k: (k, 0))],
        out_specs=_full((NA_HEADS, n_dr, 31)),
        out_shape=jax.ShapeDtypeStruct((NA_HEADS, n_dr, 31), F32),
        scratch_shapes=[pltpu.VMEM((rows, 31), F32), pltpu.VMEM((rows, 31), F32)],
        compiler_params=_params("arbitrary"),
    )(dbt.reshape(rows, wide), jnp.asarray(sel_at), jnp.asarray(sel_bt), eat, ebt)


def lru_bwd(p, d_yrnn, conv_w, conv_b, wa, ba, wx, bx, lam, comm=None):
    blk, wspec = _lru_in_specs()

    def body(xr_ref, gx_ref, dy_ref, cw_ref, cb_ref, wa_ref, ba_ref, wx_ref, bx_ref, lam_ref,
             dxr_ref, dgx_ref, dcw_ref, dcb_ref, dwa_ref, dba_ref, dwx_ref, dbx_ref, dlam_ref,
             a_s, b_s, h_s, l_s, hsum_s, dxc_s, dh_s):
        xr = xr_ref[...]
        cw = cw_ref[...]
        xc = _lru_conv(xr, cw, cb_ref[...])
        xcb = xc.astype(BF16)
        g, dg = _gelu_parts(gx_ref[CTX_LEN:, :])
        dy = dy_ref[...]
        dh_s[:CTX_LEN, :] = jnp.zeros((CTX_LEN, LRU_BLOCK_W), F32)
        dh_s[CTX_LEN:, :] = dy * g
        row = _row_ids(ZLEN, LRU_BLOCK_W)
        zero = jnp.zeros((1, LRU_BLOCK_W), F32)
        for d in (0, 1):
            wab = wa_ref[d, 0].astype(BF16)
            wxb = wx_ref[d, 0].astype(BF16)
            lam_d = lam_ref[d:d + 1, :]
            r, gi, sp, a, sq, b = _lru_gates(xc, xcb, wab, ba_ref[d:d + 1, :], wxb, bx_ref[d:d + 1, :], lam_d)
            a_s[...] = a
            b_s[...] = b
            _lru_scan_dir(d, a_s, b_s, h_s)
            h = h_s[...]
            if d == 0:
                hsum_s[...] = h
                h_prev = jnp.where(row >= 1, pltpu.roll(h, 1, 0), 0.0)
                a_s[...] = pltpu.roll(a, ZLEN - 1, 0)
                _scan_down(a_s, dh_s, l_s, 0, N_CHUNK, zero)
            else:
                hsum_s[...] = hsum_s[...] + h
                h_prev = jnp.where(row == CTX_LEN - 1, 0.0, pltpu.roll(h, ZLEN - 1, 0))
                a_s[...] = pltpu.roll(a, 1, 0)
                c = _scan_up(a_s, dh_s, l_s, CTX_CHUNKS, N_CHUNK, zero)
                _scan_up(a_s, dh_s, l_s, 0, CTX_CHUNKS, c)
            db = l_s[...]
            da = db * h_prev
            dsq = db * gi * xc
            dgi = db * sq * xc
            dxc_d = db * sq * gi
            dla = da * a - dsq * (a * a) / sq
            dr = dla * ((-LRU_C) * sp)
            dsp = jnp.sum(dla * ((-LRU_C) * r), axis=0, keepdims=True)
            dlam_ref[d:d + 1, :] = -dsp * _sigmoid(-lam_d)
            dzr = dr * r * (1.0 - r)
            dzi = dgi * gi * (1.0 - gi)
            dba_ref[d:d + 1, :] = jnp.sum(dzr, axis=0, keepdims=True)
            dbx_ref[d:d + 1, :] = jnp.sum(dzi, axis=0, keepdims=True)
            dzrb = dzr.astype(BF16)
            dzib = dzi.astype(BF16)
            dwa_ref[d, 0] = _dot_tn(xcb, dzrb)
            dwx_ref[d, 0] = _dot_tn(xcb, dzib)
            dxc_d = dxc_d + _dot_nt(dzrb, wab) + _dot_nt(dzib, wxb)
            if d == 0:
                dxc_s[...] = dxc_d
            else:
                dxc_s[...] = dxc_s[...] + dxc_d
        dxc = dxc_s[...]
        dxr_ref[...] = _lru_conv_t(dxc, cw).astype(BF16)
        dcb_ref[...] = jnp.sum(dxc, axis=0, keepdims=True)
        segpos = jnp.where(row < CTX_LEN, row, row - CTX_LEN)
        seglen = jnp.where(row < CTX_LEN, CTX_LEN, SEQ)
        for k in range(4):
            off = k - 2
            if off == 0:
                sh = xr
            else:
                ok = (segpos + off >= 0) & (segpos + off < seglen)
                sh = jnp.where(ok, pltpu.roll(xr, (-off) % ZLEN, 0), 0.0)
            dcw_ref[k:k + 1, :] = jnp.sum(dxc * sh, axis=0, keepdims=True)
        dgx_ref[:CTX_LEN, :] = jnp.zeros((CTX_LEN, LRU_BLOCK_W), BF16)
        dgx_ref[CTX_LEN:, :] = (dy * hsum_s[CTX_LEN:, :] * dg).astype(BF16)

    zs = pltpu.VMEM((ZLEN, LRU_BLOCK_W), F32)
    zb = jax.ShapeDtypeStruct((ZLEN, D_MODEL), BF16)
    v2 = jax.ShapeDtypeStruct((2, D_MODEL), F32)
    w4 = jax.ShapeDtypeStruct((2, LRU_BLOCKS, LRU_BLOCK_W, LRU_BLOCK_W), F32)
    res, extra = _call(
        body, name="lru_bwd", grid=(LRU_BLOCKS,),
        in_specs=[blk(ZLEN), pl.BlockSpec((ZLEN, LRU_BLOCK_W), lambda b: (0, 24 + b)), blk(SEQ), blk(4), blk(1),
                  wspec, blk(2), wspec, blk(2), blk(2)],
        out_specs=[blk(ZLEN), blk(ZLEN), blk(4), blk(1), wspec, blk(2), wspec, blk(2), blk(2)],
        out_shape=[zb, zb, jax.ShapeDtypeStruct((4, D_MODEL), F32), jax.ShapeDtypeStruct((1, D_MODEL), F32),
                   w4, v2, w4, v2, v2],
        scratch_shapes=[zs] * 7, sem=("arbitrary",),
        args=(p, p, d_yrnn, conv_w, conv_b, wa, ba, wx, bx, lam), comm=comm)
    return (*res, extra)


def in_proj_bwd(dgs, w_in, z, dx1, gain, scale, comm=None):
    def body(*refs):
        dg_refs = refs[:7]
        w_ref, z_ref, dx1_ref, g_ref, sc_ref, gx_ref, dsh_ref, dsc_ref, dgn_ref = refs[7:]
        i = pl.program_id(0)
        dxn = _dot_nt(dg_refs[0][...], w_ref[:, 0:D_MODEL])
        for g in range(1, 7):
            dxn = dxn + _dot_nt(dg_refs[g][...], w_ref[:, g * D_MODEL:(g + 1) * D_MODEL])
        dx, dsh, dsc, dgn = _norm_mod_bwd(z_ref[...], dxn, g_ref[...], sc_ref[0])

        @pl.when(i <= 1)
        def _():
            dsh_ref[0] = dsh
            dsc_ref[0] = dsc

        @pl.when(i > 1)
        def _():
            dsh_ref[0] = dsh_ref[0] + dsh
            dsc_ref[0] = dsc_ref[0] + dsc

        @pl.when(i == 0)
        def _():
            dgn_ref[...] = dgn

        @pl.when(i > 0)
        def _():
            dgn_ref[...] = dgn_ref[...] + dgn
            gx_ref[...] = dx1_ref[...] + dx

    zrow = pl.BlockSpec((ROW_TILE, D_MODEL), lambda i: (i, 0))
    lat = pl.BlockSpec((ROW_TILE, D_MODEL), lambda i: (jnp.maximum(i - 1, 0), 0))
    mod = pl.BlockSpec((1, 1, D_MODEL), lambda i: (jnp.minimum(i, 1), 0, 0))
    mshape = jax.ShapeDtypeStruct((2, 1, D_MODEL), F32)
    res, extra = _call(
        body, name="in_proj_bwd", grid=(ZLEN // ROW_TILE,),
        in_specs=[zrow] * 7 + [_full((D_MODEL, IN_COLS)), zrow, lat, _full((1, D_MODEL)), mod],
        out_specs=[lat, mod, mod, _full((1, D_MODEL))],
        out_shape=[jax.ShapeDtypeStruct((SEQ, D_MODEL), F32), mshape, mshape, jax.ShapeDtypeStruct((1, D_MODEL), F32)],
        sem=("arbitrary",), args=(*dgs, w_in, z, dx1, gain, scale), comm=comm)
    return (*res, extra)


def matmul_tn(a, b, name, tm, tn, prev=None, col_block=0, total_cols=None):
    k, m = a.shape
    n = b.shape[1]
    total_cols = n if total_cols is None else total_cols
    assert m % tm == 0 and n % tn == 0
    off = col_block * (n // tn)

    def body(a_ref, b_ref, *rest):
        rest[-1][...] = _dot_tn(a_ref[...].astype(BF16), b_ref[...]).astype(BF16)

    in_specs = [pl.BlockSpec((k, tm), lambda i, j: (0, i)), pl.BlockSpec((k, tn), lambda i, j: (0, j))]
    args = [a, b]
    aliases = {}
    if prev is not None:
        in_specs.append(pl.BlockSpec(memory_space=pl.ANY))
        args.append(prev)
        aliases = {2: 0}
    return pl.pallas_call(
        body, name=name, grid=(m // tm, n // tn), in_specs=in_specs,
        out_specs=pl.BlockSpec((tm, tn), lambda i, j: (i, j + off)),
        out_shape=jax.ShapeDtypeStruct((m, total_cols), BF16),
        input_output_aliases=aliases,
        compiler_params=_params("parallel", "parallel"),
    )(*args)


def local_step(z, target, modx, modc, norm_mix_g, norm_ffn_g, w_in, conv_w, conv_b, wa, ba, wx, bx, lam, qg, kg, rpb,
               w_rnn, w_na, w_out, w_up, fconv_w, fconv_b, w_down, idx=None, bt=None):
    dist = idx is not None
    c_idx = idx[1:2] if dist else None
    d = D_MODEL
    mx = [modx[:, k * d:(k + 1) * d] for k in range(N_MOD)]
    shift = jnp.stack([modc[:, 0:d], mx[0]])
    scale = jnp.stack([modc[:, d:2 * d], mx[1]])
    cos, sin = _rope_tables()
    ones = _head_ones()
    qg2 = jnp.tile(qg, (1, 2))
    kg2 = jnp.tile(kg, (1, 2))

    if bt is None:
        bt, _ = bias_table(rpb)
    xn, p, got = norm_matmul(z, norm_mix_g, shift, scale, w_in, "in_proj", 3 * ROW_TILE, 1792, ctx_rows=CTX_LEN,
                             comm=gather_weights_comm([w_rnn, w_na, w_out], [1, 2, 3]) if dist else None)
    if dist:
        w_rnn, w_na, w_out = got
    y_rnn, got = lru_fwd(p, conv_w, conv_b, wa, ba, wx, bx, lam,
                         comm=gather_weights_comm([w_down], [5]) if dist else None)
    if dist:
        w_down = got[0]
    q_rot, q_pl, kk, vv, _ = qkv_prep(p, qg2, kg2, cos, sin, ones)
    y_na, lse, got = attn_fwd(q_rot, q_pl, kk, vv, bt, comm=gather_weights_comm([w_up], [4]) if dist else None)
    if dist:
        w_up = got[0]
    u, v, merged, out, x1 = merge_fwd(y_rnn, y_na, p, z, mx[2], w_rnn, w_na, w_out)
    xn2, hpre, _ = norm_matmul(x1, norm_ffn_g, mx[3][None], mx[4][None], w_up, "ffn_up", 2 * ROW_TILE, 1408)
    act = ffn_act(hpre, fconv_w, fconv_b)
    f, dy, df, loss_sq, dg5 = ffn_down_loss(act, w_down, x1, mx[5], target)

    partials, pieces = {}, {}

    def views_of(which, grads):
        return [_grad_view(g, BIG[w][1], BIG[w][2]) for w, g in zip(which, grads)]

    def chip_partials(which, views, recv):
        for w, gv, r in zip(which, views, recv):
            partials[w] = add_halves(gv, r, c_idx, "add_halves_" + BIG[w][0])
        return scatter_pieces_comm([partials[w] for w in which], which)

    d_act = ffn_down_bwd(df, w_down)
    dha, dhg, d_fcw_a, d_fcw_g, d_fcb_a, d_fcb_g = ffn_act_bwd(hpre, d_act, fconv_w, fconv_b)
    d_fcw = jnp.concatenate([d_fcw_a, d_fcw_g], axis=1)
    d_fcb = jnp.concatenate([d_fcb_a, d_fcb_g], axis=1)
    dx1, d_s3, d_s4, d_gffn = ffn_up_bwd(dha, dhg, w_up, x1, dy, norm_ffn_g, mx[4])
    g_w_down = matmul_tn(act, df, "gw_down", 256, D_MODEL)
    g_w_up = matmul_tn(xn2, dha, "gw_up_a", 512, 1408, total_cols=2 * D_FF)
    g_w_up = matmul_tn(xn2, dhg, "gw_up_g", 512, 1408, prev=g_w_up, col_block=1, total_cols=2 * D_FF)
    v_ffn = views_of([4, 5], [g_w_up, g_w_down]) if dist else None
    *mb, got = merge_bwd(dx1, out, mx[2], p, u, v, w_rnn, w_na, w_out,
                         comm=exchange_halves_comm(v_ffn) if dist else None)
    dout, du, dv, dmr, dmn, dyr, dyn, dg2 = mb
    recv_ffn = got
    g_w_out = matmul_tn(merged, dout, "gw_out", 1024, 512)
    g_w_rnn = matmul_tn(y_rnn, du, "gw_rnn", 1024, 512)
    g_w_na = matmul_tn(y_na, dv, "gw_na", 1024, 512)
    v_mix = views_of([1, 2, 3], [g_w_rnn, g_w_na, g_w_out]) if dist else None
    *lru_grads, got = lru_bwd(p, dyr, conv_w, conv_b, wa, ba, wx, bx, lam,
                              comm=join_comms(chip_partials([4, 5], v_ffn, recv_ffn),
                                              exchange_halves_comm(v_mix)) if dist else None)
    dxr, dgx, d_cw, d_cb, d_wa, d_ba, d_wx, d_bx, d_lam = lru_grads
    if dist:
        pieces[4], pieces[5] = got[:2]
    lru_w_all = {}
    dqr, dqp, dk, dvh, dbt, got = attn_bwd(
        q_rot, q_pl, kk, vv, bt, y_na, dyn, lse,
        comm=join_comms(chip_partials([1, 2, 3], v_mix, got[2:]),
                        join_comms(all_gather_comm(d_wa.reshape(-1, LRU_BLOCK_W)),
                                   all_gather_comm(d_wx.reshape(-1, LRU_BLOCK_W)))) if dist else None)
    if dist:
        pieces[1], pieces[2], pieces[3], lru_w_all["lru_wa"], lru_w_all["lru_wx"] = got
    dq_cols, dk_cols, dv_cols, d_qg, d_kg, _ = qkv_bwd(dqr, dqp, dk, dvh, p, qg2, kg2, cos, sin, ones)
    d_rpb = rpb_grad(dbt)
    dgs = [dxr, dk_cols, dv_cols, dgx, dq_cols, dmr, dmn]
    g_w_in = None
    for g in range(7):
        g_w_in = matmul_tn(xn, dgs[g], "gw_in_%d" % g, 1024, 512, prev=g_w_in, col_block=g, total_cols=IN_COLS)
    if dist:
        v_in = views_of([0], [g_w_in])
        recv_in = run_comm(exchange_halves_comm(v_in), "grad_exchange_w_in")
    grad_x, dsh, dsc, d_gmix, got = in_proj_bwd(dgs, w_in, z, dx1, norm_mix_g, scale,
                                                comm=chip_partials([0], v_in, recv_in) if dist else None)
    if dist:
        pieces[0] = got[0]

    d_modx = jnp.concatenate([dsh[1], dsc[1], dg2, d_s3, d_s4, dg5], axis=1)
    d_modc = jnp.concatenate([dsh[0], dsc[0]], axis=1)
    return dict(loss_sq=loss_sq, grad_x=grad_x, d_modx=d_modx, d_modc=d_modc, norm_mix_g=d_gmix, norm_ffn_g=d_gffn,
                w_in=g_w_in, lru_conv_w=d_cw, lru_conv_b=d_cb, lru_wa=d_wa, lru_ba=d_ba, lru_wx=d_wx, lru_bx=d_bx,
                lru_lambda=d_lam, q_norm_g=d_qg, k_norm_g=d_kg, na_rpb=d_rpb, w_rnn_out=g_w_rnn, w_na_out=g_w_na,
                w_out=g_w_out, w_up=g_w_up, ffn_conv_w=d_fcw, ffn_conv_b=d_fcb, w_down=g_w_down,
                partials=partials, pieces=pieces, lru_w_all=lru_w_all)


def _mesh_pos():
    return lax.axis_index("x"), lax.axis_index("y"), lax.axis_index("c")


def _other_chips(x, y):
    return [(1 - x, y), (x, 1 - y), (1 - x, 1 - y)]


BIG = (("w_in", (D_MODEL, IN_COLS), 1), ("w_rnn_out", (D_MODEL, D_MODEL), 0), ("w_na_out", (D_MODEL, D_MODEL), 0),
       ("w_out", (D_MODEL, D_MODEL), 0), ("w_up", (D_MODEL, 2 * D_FF), 1), ("w_down", (D_FF, D_MODEL), 0))


def _shard_shape(full, axis):
    r, c = full
    return (r // N_SHARD, c) if axis == 0 else (r, c // N_SHARD)


def _slot(ref, full, axis, s, h):
    r, c = full
    if axis == 0:
        rs = r // N_SHARD
        return ref.at[pl.ds(s * rs + h * (rs // 2), rs // 2), :]
    cs = c // N_SHARD
    return ref.at[pl.ds(h * (r // 2), r // 2), pl.ds(s * cs, cs)]


def cast_into_full(x, full, axis, idx, name):
    r, c = x.shape
    tr = next(t for t in (512, 352, 256, 128) if r % t == 0)
    nb = r // tr

    def body(idx_ref, x_ref, o_ref):
        o_ref[...] = x_ref[...].astype(BF16)

    if axis == 0:
        out_spec = pl.BlockSpec((tr, c), lambda i, idx_ref: (idx_ref[0] * nb + i, 0))
    else:
        out_spec = pl.BlockSpec((tr, c), lambda i, idx_ref: (i, idx_ref[0]))
    return pl.pallas_call(
        body, name=name,
        grid_spec=pltpu.PrefetchScalarGridSpec(
            num_scalar_prefetch=1, grid=(nb,), in_specs=[pl.BlockSpec((tr, c), lambda i, idx_ref: (i, 0))],
            out_specs=out_spec),
        out_shape=jax.ShapeDtypeStruct(full, BF16),
        compiler_params=_params("parallel"),
    )(idx, x)


def run_comm(comm, name):
    k_in, k_out = len(comm.inputs), len(comm.out_shapes)

    def body(*refs):
        start, mid, end = comm.emit(refs[:k_in], refs[k_in:k_in + k_out], refs[k_in + k_out:])
        start()
        mid()
        end()

    hbm = pl.BlockSpec(memory_space=pl.ANY)
    return pl.pallas_call(
        body, name=name, in_specs=[hbm] * k_in, out_specs=[hbm] * k_out, out_shape=list(comm.out_shapes),
        input_output_aliases=dict(comm.aliases), scratch_shapes=list(comm.scratch),
        compiler_params=pltpu.CompilerParams(vmem_limit_bytes=VMEM_LIMIT_V7X),
    )(*comm.inputs)


def gather_weights_comm(fulls, which):
    nw = len(which)
    specs = [BIG[w] for w in which]

    def emit(_, outs, sems):
        send1, recv1, send2, recv2 = sems
        x, y, c = _mesh_pos()
        sibling = (x, y, 1 - c)
        chips = _other_chips(x, y)
        s_me = 2 * x + y
        shards = [2 * chip[0] + chip[1] for chip in chips]

        def ici(w, j, shard):
            _, full, axis = specs[w]
            dst = _slot(outs[w], full, axis, shard, c)
            return pltpu.make_async_remote_copy(
                src_ref=dst, dst_ref=dst, send_sem=send1.at[3 * w + j],
                recv_sem=recv1.at[3 * w + j], device_id=(*chips[j], c), device_id_type=MESH_T)

        def d2d(w, j, shard, half):
            _, full, axis = specs[w]
            dst = _slot(outs[w], full, axis, shard, half)
            return pltpu.make_async_remote_copy(
                src_ref=dst, dst_ref=dst, send_sem=send2.at[3 * w + j], recv_sem=recv2.at[3 * w + j],
                device_id=sibling, device_id_type=MESH_T)

        pairs = [(w, j) for w in range(nw) for j in range(3)]

        def start():
            for w, j in pairs:
                ici(w, j, s_me).start()

        def mid():
            for w, j in pairs:
                ici(w, j, shards[j]).wait_recv()
                d2d(w, j, shards[j], c).start()

        def end():
            for w, j in pairs:
                d2d(w, j, shards[j], 1 - c).wait_recv()
            for w, j in pairs:
                ici(w, j, s_me).wait_send()
                d2d(w, j, shards[j], c).wait_send()

        return start, mid, end

    return Comm(list(fulls), [jax.ShapeDtypeStruct(full, BF16) for _, full, _ in specs], {i: i for i in range(nw)},
                [pltpu.SemaphoreType.DMA((3 * nw,))] * 4, emit)


def join_comms(a, b):
    ai, ao, asc = len(a.inputs), len(a.out_shapes), len(a.scratch)

    def emit(ins, outs, sems):
        fa = a.emit(ins[:ai], outs[:ao], sems[:asc])
        fb = b.emit(ins[ai:], outs[ao:], sems[asc:])

        def both(k):
            def run():
                fa[k]()
                fb[k]()
            return run

        return both(0), both(1), both(2)

    aliases = dict(a.aliases)
    aliases.update({ai + i: ao + o for i, o in b.aliases.items()})
    return Comm(a.inputs + b.inputs, a.out_shapes + b.out_shapes, aliases, a.scratch + b.scratch, emit)


def all_gather_comm(x):
    def emit(srcs, outs, sems):
        send_sems, recv_sems, local_sem = sems
        x_ref, out_ref = srcs[0], outs[0]
        x, y, c = _mesh_pos()
        me, sibling = (x, y, c), (x, y, 1 - c)
        chips = _other_chips(x, y)

        def blk(px, py, pc):
            return out_ref.at[4 * px + 2 * py + pc]

        def copy(k, block, to, src=None):
            return pltpu.make_async_remote_copy(
                src_ref=blk(*block) if src is None else src, dst_ref=blk(*block),
                send_sem=send_sems.at[k], recv_sem=recv_sems.at[k], device_id=to, device_id_type=MESH_T)

        def mine():
            return pltpu.make_async_copy(x_ref, blk(*me), local_sem)

        def start():
            mine().start()
            copy(0, me, sibling, src=x_ref).start()
            for j, chip in enumerate(chips):
                copy(1 + j, me, (*chip, c), src=x_ref).start()

        def mid():
            for j, chip in enumerate(chips):
                copy(1 + j, (*chip, c), me).wait_recv()
                copy(4 + j, (*chip, c), sibling).start()

        def end():
            copy(0, sibling, me).wait_recv()
            for j, chip in enumerate(chips):
                copy(4 + j, (*chip, 1 - c), me).wait_recv()
            copy(0, me, sibling, src=x_ref).wait_send()
            for j, chip in enumerate(chips):
                copy(1 + j, me, (*chip, c), src=x_ref).wait_send()
                copy(4 + j, (*chip, c), sibling).wait_send()
            mine().wait()

        return start, mid, end

    return Comm([x], [jax.ShapeDtypeStruct((N_DEV,) + x.shape, F32)], {},
                [pltpu.SemaphoreType.DMA((7,)), pltpu.SemaphoreType.DMA((7,)), pltpu.SemaphoreType.DMA], emit)


def sum_blocks(g, name):
    _, r, c = g.shape
    tr = 256 if r % 256 == 0 else r

    def body(g_ref, o_ref):
        acc = g_ref[0]
        for k in range(1, N_DEV):
            acc = acc + g_ref[k]
        o_ref[...] = acc

    return pl.pallas_call(
        body, name=name, grid=(r // tr,),
        in_specs=[pl.BlockSpec((N_DEV, tr, c), lambda i: (0, i, 0))],
        out_specs=pl.BlockSpec((tr, c), lambda i: (i, 0)),
        out_shape=jax.ShapeDtypeStruct((r, c), F32),
        compiler_params=_params("parallel"),
    )(g)


def _grad_view(g, full, axis):
    r, c = full
    if axis == 0:
        return g.reshape(N_SHARD, 2, r // N_SHARD // 2, c)
    return g.reshape(1, 2, r // 2, c)


def exchange_halves_comm(gviews):
    nw = len(gviews)

    def emit(srcs, outs, sems):
        send_sems, recv_sems = sems
        x, y, c = _mesh_pos()

        def copies():
            return [pltpu.make_async_remote_copy(
                src_ref=srcs[w].at[:, pl.ds(1 - c, 1)], dst_ref=outs[w], send_sem=send_sems.at[w],
                recv_sem=recv_sems.at[w], device_id=(x, y, 1 - c), device_id_type=MESH_T) for w in range(nw)]

        def start():
            for cp in copies():
                cp.start()

        def end():
            for cp in copies():
                cp.wait()

        return start, lambda: None, end

    return Comm(list(gviews), [jax.ShapeDtypeStruct((g.shape[0], 1) + g.shape[2:], BF16) for g in gviews], {},
                [pltpu.SemaphoreType.DMA((nw,)), pltpu.SemaphoreType.DMA((nw,))], emit)


def _row_tile(rh):
    return 128 if rh % 128 == 0 else rh


def add_halves(gview, recv, c_idx, name):
    a, _, rh, cc = gview.shape
    tr = _row_tile(rh)

    def body(c_ref, g_ref, r_ref, o_ref):
        o_ref[0] = (g_ref[0, 0].astype(F32) + r_ref[0, 0].astype(F32)).astype(BF16)

    return pl.pallas_call(
        body, name=name,
        grid_spec=pltpu.PrefetchScalarGridSpec(
            num_scalar_prefetch=1, grid=(a, rh // tr),
            in_specs=[pl.BlockSpec((1, 1, tr, cc), lambda s, i, c_ref: (s, c_ref[0], i, 0)),
                      pl.BlockSpec((1, 1, tr, cc), lambda s, i, c_ref: (s, 0, i, 0))],
            out_specs=pl.BlockSpec((1, tr, cc), lambda s, i, c_ref: (s, i, 0))),
        out_shape=jax.ShapeDtypeStruct((a, rh, cc), BF16),
        compiler_params=_params("parallel", "parallel"),
    )(c_idx, gview, recv)


def _piece_shape(full, axis):
    rs, cs = _shard_shape(full, axis)
    return (rs // 2, cs)


def scatter_pieces_comm(partials, which):
    nw = len(which)
    specs = [BIG[w] for w in which]

    def emit(srcs, outs, sems):
        send_sems, recv_sems = sems
        x, y, c = _mesh_pos()
        chips = _other_chips(x, y)

        def copies():
            cps = []
            for w, (_, full, axis) in enumerate(specs):
                cs = full[1] // N_SHARD
                for j, chip in enumerate(chips):
                    s_j = 2 * chip[0] + chip[1]
                    src = srcs[w].at[s_j] if axis == 0 else srcs[w].at[0, :, pl.ds(s_j * cs, cs)]
                    cps.append(pltpu.make_async_remote_copy(
                        src_ref=src, dst_ref=outs[w].at[j], send_sem=send_sems.at[3 * w + j],
                        recv_sem=recv_sems.at[3 * w + j], device_id=(*chip, c), device_id_type=MESH_T))
            return cps

        def start():
            for cp in copies():
                cp.start()

        def mid():
            pass

        def end():
            for cp in copies():
                cp.wait()

        return start, mid, end

    return Comm(list(partials), [jax.ShapeDtypeStruct((3,) + _piece_shape(full, axis), BF16) for _, full, axis in specs],
                {}, [pltpu.SemaphoreType.DMA((3 * nw,)), pltpu.SemaphoreType.DMA((3 * nw,))], emit)


def add_pieces(partial, recv, idx, axis, name):
    _, rh, cs = recv.shape
    tr = _row_tile(rh)

    def body(idx_ref, p_ref, r_ref, o_ref):
        o_ref[0] = ((p_ref[0].astype(F32) + r_ref[0].astype(F32)) + r_ref[1].astype(F32)) + r_ref[2].astype(F32)

    if axis == 0:
        pspec = pl.BlockSpec((1, tr, cs), lambda i, idx_ref: (idx_ref[0], i, 0))
    else:
        pspec = pl.BlockSpec((1, tr, cs), lambda i, idx_ref: (0, i, idx_ref[0]))
    return pl.pallas_call(
        body, name=name,
        grid_spec=pltpu.PrefetchScalarGridSpec(
            num_scalar_prefetch=1, grid=(rh // tr,),
            in_specs=[pspec, pl.BlockSpec((3, tr, cs), lambda i, idx_ref: (0, i, 0))],
            out_specs=pl.BlockSpec((1, tr, cs), lambda i, idx_ref: (idx_ref[1], i, 0))),
        out_shape=jax.ShapeDtypeStruct((2, rh, cs), F32),
        compiler_params=_params("parallel"),
    )(idx, partial, recv)


def join_halves_comm(halves):
    nw = len(halves)

    def emit(_, outs, sems):
        send_sems, recv_sems = sems
        x, y, c = _mesh_pos()

        def copy(w, half):
            return pltpu.make_async_remote_copy(
                src_ref=outs[w].at[half], dst_ref=outs[w].at[half], send_sem=send_sems.at[w], recv_sem=recv_sems.at[w],
                device_id=(x, y, 1 - c), device_id_type=MESH_T)

        def start():
            for w in range(nw):
                copy(w, c).start()

        def end():
            for w in range(nw):
                copy(w, c).wait_send()
                copy(w, 1 - c).wait_recv()

        return start, lambda: None, end

    return Comm(list(halves), [jax.ShapeDtypeStruct(h.shape, F32) for h in halves], {i: i for i in range(nw)},
                [pltpu.SemaphoreType.DMA((nw,))] * 2, emit)


MOD_COLS = N_MOD * D_MODEL // N_SHARD
MOD_TILE = 512


def mod_fwd(c16, w_mod):
    def body(c_ref, w_ref, s_ref, o_ref):
        cv = c_ref[...]
        s = cv * _sigmoid(cv)
        s_ref[...] = s
        o_ref[...] = jnp.dot(s.astype(BF16), w_ref[...].astype(BF16), preferred_element_type=F32)

    return pl.pallas_call(
        body, name="mod_fwd", grid=(MOD_COLS // MOD_TILE,),
        in_specs=[_full((16, D_MODEL)), pl.BlockSpec((D_MODEL, MOD_TILE), lambda j: (0, j))],
        out_specs=[_full((16, D_MODEL)), pl.BlockSpec((16, MOD_TILE), lambda j: (0, j))],
        out_shape=[jax.ShapeDtypeStruct((16, D_MODEL), F32), jax.ShapeDtypeStruct((16, MOD_COLS), F32)],
        compiler_params=_params("arbitrary"),
    )(c16, w_mod)


def mod_bwd(s16, dm16, w_mod):
    hi = lax.Precision.HIGHEST

    def body(s_ref, d_ref, w_ref, gw_ref, ds_ref):
        j = pl.program_id(0)
        dm = d_ref[...]
        gw_ref[...] = lax.dot_general(s_ref[...], dm, (((0,), (0,)), ((), ())), preferred_element_type=F32, precision=hi)
        part = lax.dot_general(dm, w_ref[...], (((1,), (1,)), ((), ())), preferred_element_type=F32, precision=hi)

        @pl.when(j == 0)
        def _():
            ds_ref[...] = part

        @pl.when(j > 0)
        def _():
            ds_ref[...] = ds_ref[...] + part

    return pl.pallas_call(
        body, name="mod_bwd", grid=(MOD_COLS // MOD_TILE,),
        in_specs=[_full((16, D_MODEL)), pl.BlockSpec((16, MOD_TILE), lambda j: (0, j)),
                  pl.BlockSpec((D_MODEL, MOD_TILE), lambda j: (0, j))],
        out_specs=[pl.BlockSpec((D_MODEL, MOD_TILE), lambda j: (0, j)), _full((16, D_MODEL))],
        out_shape=[jax.ShapeDtypeStruct((D_MODEL, MOD_COLS), F32), jax.ShapeDtypeStruct((16, D_MODEL), F32)],
        compiler_params=_params("arbitrary"),
    )(s16, dm16, w_mod)


def cctx_grad(parts, c_ctx):
    def body(p_ref, c_ref, o_ref):
        ds = p_ref[0:1, :]
        for s in range(1, N_SHARD):
            ds = ds + p_ref[16 * s:16 * s + 1, :]
        cv = c_ref[...]
        sg = _sigmoid(cv)
        o_ref[...] = ds * (sg * (1.0 + cv * (1.0 - sg)))

    return pl.pallas_call(
        body, name="cctx_grad", in_specs=[_full((N_DEV * 8, D_MODEL)), _full((1, D_MODEL))],
        out_specs=_full((1, D_MODEL)), out_shape=jax.ShapeDtypeStruct((1, D_MODEL), F32),
    )(parts, c_ctx)


def add_rows(a, b, name):
    def body(a_ref, b_ref, o_ref):
        o_ref[...] = a_ref[...] + b_ref[...]

    return pl.pallas_call(body, name=name, in_specs=[_full(a.shape), _full(b.shape)], out_specs=_full(a.shape),
                          out_shape=jax.ShapeDtypeStruct(a.shape, F32))(a, b)


def _adamw_update(w_ref, g_ref, m_ref, v_ref, d_ref, nm_ref, nv_ref):
    g_ = g_ref[...]
    m_ = ADAM_B1 * m_ref[...] + (1.0 - ADAM_B1) * g_
    v_ = ADAM_B2 * v_ref[...] + (1.0 - ADAM_B2) * (g_ * g_)
    m_hat = m_ / (1.0 - ADAM_B1 ** ADAM_STEP)
    v_hat = v_ / (1.0 - ADAM_B2 ** ADAM_STEP)
    d_ref[...] = -ADAM_LR * (m_hat / (jnp.sqrt(v_hat) + ADAM_EPS) + ADAM_WD * w_ref[...])
    nm_ref[...] = m_
    nv_ref[...] = v_


def adamw_many(ws, gs, ms, vs):
    n = len(ws)

    def body(*refs):
        for i in range(n):
            _adamw_update(*[refs[k * n + i] for k in range(7)])

    shapes = [jax.ShapeDtypeStruct(w.shape, F32) for w in ws]
    return pl.pallas_call(body, name="adamw_small", out_shape=shapes * 3,
                          compiler_params=pltpu.CompilerParams(vmem_limit_bytes=VMEM_LIMIT_V7X))(*ws, *gs, *ms, *vs)


def adamw(w, g, m, v, name, comm=None):
    r, c = w.shape
    tr = 128 if (r % 128 == 0 and r > 128) else r

    def body(w_ref, g_ref, m_ref, v_ref, d_ref, nm_ref, nv_ref):
        _adamw_update(w_ref, g_ref, m_ref, v_ref, d_ref, nm_ref, nv_ref)

    spec = pl.BlockSpec((tr, c), lambda i: (i, 0))
    shp = jax.ShapeDtypeStruct((r, c), F32)
    res, extra = _call(body, name=name, grid=(r // tr,), in_specs=[spec] * 4, out_specs=[spec] * 3,
                       out_shape=[shp] * 3, sem=("parallel",), args=(w, g, m, v), comm=comm)
    return (*res, extra)


LANES = 1024


def _pack(arrs):
    rows, spans, at = [], [], 0
    for a in arrs:
        n = int(np.prod(a.shape))
        nr = 8 * -(-n // (8 * LANES))
        flat = a.reshape(-1)
        if nr * LANES != n:
            flat = jnp.concatenate([flat, jnp.zeros((nr * LANES - n,), F32)])
        rows.append(flat.reshape(nr, LANES))
        spans.append((at, nr, n, a.shape))
        at += nr
    return jnp.concatenate(rows, axis=0), spans


def _unpack(buf, spans):
    out = []
    for at, nr, n, shape in spans:
        out.append(buf[at:at + nr].reshape(-1)[:n].reshape(shape))
    return out


SMALL_SHARD = ("lru_conv_w", "lru_ba", "lru_bx", "lru_lambda", "ffn_conv_w")


def kernel(x, c, ctx, c_ctx, w_mod, b_mod, norm_mix_g, norm_ffn_g, w_in, lru_conv_w, lru_conv_b, lru_wa, lru_ba, lru_wx, lru_bx, lru_lambda, q_norm_g, k_norm_g, na_rpb, w_rnn_out, w_na_out, w_out, w_up, ffn_conv_w, ffn_conv_b, w_down, loss_target, m_c_ctx, m_w_mod, m_b_mod, m_norm_mix_g, m_norm_ffn_g, m_w_in, m_lru_conv_w, m_lru_conv_b, m_lru_wa, m_lru_ba, m_lru_wx, m_lru_bx, m_lru_lambda, m_q_norm_g, m_k_norm_g, m_na_rpb, m_w_rnn_out, m_w_na_out, m_w_out, m_w_up, m_ffn_conv_w, m_ffn_conv_b, m_w_down, v_c_ctx, v_w_mod, v_b_mod, v_norm_mix_g, v_norm_ffn_g, v_w_in, v_lru_conv_w, v_lru_conv_b, v_lru_wa, v_lru_ba, v_lru_wx, v_lru_bx, v_lru_lambda, v_q_norm_g, v_k_norm_g, v_na_rpb, v_w_rnn_out, v_w_na_out, v_w_out, v_w_up, v_ffn_conv_w, v_ffn_conv_b, v_w_down):
    weights = dict(c_ctx=c_ctx, w_mod=w_mod, b_mod=b_mod, norm_mix_g=norm_mix_g, norm_ffn_g=norm_ffn_g, w_in=w_in,
                   lru_conv_w=lru_conv_w, lru_conv_b=lru_conv_b, lru_wa=lru_wa, lru_ba=lru_ba, lru_wx=lru_wx,
                   lru_bx=lru_bx, lru_lambda=lru_lambda, q_norm_g=q_norm_g, k_norm_g=k_norm_g, na_rpb=na_rpb,
                   w_rnn_out=w_rnn_out, w_na_out=w_na_out, w_out=w_out, w_up=w_up, ffn_conv_w=ffn_conv_w,
                   ffn_conv_b=ffn_conv_b, w_down=w_down)
    mom1 = dict(c_ctx=m_c_ctx, w_mod=m_w_mod, b_mod=m_b_mod, norm_mix_g=m_norm_mix_g, norm_ffn_g=m_norm_ffn_g,
                w_in=m_w_in, lru_conv_w=m_lru_conv_w, lru_conv_b=m_lru_conv_b, lru_wa=m_lru_wa, lru_ba=m_lru_ba,
                lru_wx=m_lru_wx, lru_bx=m_lru_bx, lru_lambda=m_lru_lambda, q_norm_g=m_q_norm_g, k_norm_g=m_k_norm_g,
                na_rpb=m_na_rpb, w_rnn_out=m_w_rnn_out, w_na_out=m_w_na_out, w_out=m_w_out, w_up=m_w_up,
                ffn_conv_w=m_ffn_conv_w, ffn_conv_b=m_ffn_conv_b, w_down=m_w_down)
    mom2 = dict(c_ctx=v_c_ctx, w_mod=v_w_mod, b_mod=v_b_mod, norm_mix_g=v_norm_mix_g, norm_ffn_g=v_norm_ffn_g,
                w_in=v_w_in, lru_conv_w=v_lru_conv_w, lru_conv_b=v_lru_conv_b, lru_wa=v_lru_wa, lru_ba=v_lru_ba,
                lru_wx=v_lru_wx, lru_bx=v_lru_bx, lru_lambda=v_lru_lambda, q_norm_g=v_q_norm_g, k_norm_g=v_k_norm_g,
                na_rpb=v_na_rpb, w_rnn_out=v_w_rnn_out, w_na_out=v_w_na_out, w_out=v_w_out, w_up=v_w_up,
                ffn_conv_w=v_ffn_conv_w, ffn_conv_b=v_ffn_conv_b, w_down=v_w_down)
    order = list(weights)
    d = D_MODEL
    mx_, my_, mc_ = _mesh_pos()
    shard = 2 * mx_ + my_
    dev = 2 * shard + mc_

    idx = jnp.stack([shard, mc_]).astype(jnp.int32)
    wsh = {name: cast_into_full(weights[name][0], full, axis, idx, "cast_" + name) for name, full, axis in BIG}
    local_small, small_spans = _pack([c] + [weights[k][0] for k in SMALL_SHARD])
    bt, (w_in_full, gath) = bias_table(na_rpb[0], comm=join_comms(gather_weights_comm([wsh["w_in"]], [0]),
                                                                  all_gather_comm(local_small)))
    per_dev = [_unpack(gath[k], small_spans) for k in range(N_DEV)]
    c_all = jnp.concatenate([per_dev[k][0] for k in range(N_DEV)], axis=0)
    full_small = {name: jnp.concatenate([per_dev[2 * s][1 + i] for s in range(N_SHARD)], axis=-1)
                  for i, name in enumerate(SMALL_SHARD)}
    c16 = jnp.concatenate([c_all, c_ctx.reshape(1, d), jnp.zeros((7, d), F32)], axis=0)
    s16, mod_part = mod_fwd(c16, w_mod[0])
    mod_all = run_comm(all_gather_comm(mod_part), "gather_mod")[0]
    mod = jnp.concatenate([mod_all[2 * s] for s in range(N_SHARD)], axis=1) + b_mod
    modx = lax.dynamic_slice(mod, (dev, 0), (1, N_MOD * d))
    modc = mod[8:9]

    z = jnp.concatenate([ctx[0], x[0]], axis=0)
    res = local_step(z, loss_target[0], modx, modc, norm_mix_g, norm_ffn_g, w_in_full, full_small["lru_conv_w"],
                     lru_conv_b, lru_wa[0], full_small["lru_ba"], lru_wx[0], full_small["lru_bx"],
                     full_small["lru_lambda"], q_norm_g, k_norm_g, na_rpb[0], wsh["w_rnn_out"], wsh["w_na_out"],
                     wsh["w_out"], wsh["w_up"], full_small["ffn_conv_w"], ffn_conv_b, wsh["w_down"], idx=idx, bt=bt)

    halves = [add_pieces(res["partials"][i], res["pieces"][i], idx, BIG[i][2], "add_pieces_" + BIG[i][0])
              for i in range(len(BIG))]
    lru_tot = {k: sum_blocks(res["lru_w_all"][k], "sum_" + k).reshape(weights[k].shape[1:])
               for k in ("lru_wa", "lru_wx")}
    small_names = ["norm_mix_g", "norm_ffn_g", "lru_conv_w", "lru_conv_b", "lru_ba", "lru_bx",
                   "lru_lambda", "q_norm_g", "k_norm_g", "na_rpb", "ffn_conv_w", "ffn_conv_b"]
    local_g, g_spans = _pack([res["loss_sq"][0:1, 0:1], res["d_modx"], res["d_modc"]] + [res[k] for k in small_names])
    n_rows = local_g.shape[0]
    *joined, g_all = run_comm(join_comms(join_halves_comm(halves), all_gather_comm(local_g)), "tail_exchange")
    grads = {name: joined[i].reshape(_shard_shape(full, axis)) for i, (name, full, axis) in enumerate(BIG)}
    grads.update(lru_tot)
    g_tot = sum_blocks(g_all, "sum_small")
    tot = _unpack(g_tot, g_spans)
    loss = (0.5 / d) * tot[0][0, 0]
    small_tot = dict(zip(small_names, tot[3:]))
    at_x = g_spans[1][0]
    dmx_rows = g_all.reshape(N_DEV, n_rows, LANES)[:, at_x:at_x + N_MOD, :].reshape(N_DEV, N_MOD * d)
    dmc_row = jnp.concatenate([tot[2], jnp.zeros((1, 4 * d), F32)], axis=1)
    dm16 = jnp.concatenate([dmx_rows, dmc_row, jnp.zeros((7, N_MOD * d), F32)], axis=0)
    grads["b_mod"] = add_rows(tot[1], dmc_row, "b_mod_grad")
    g_w_mod, ds16 = mod_bwd(s16, lax.dynamic_slice(dm16, (0, shard * MOD_COLS), (16, MOD_COLS)), w_mod[0])
    grads["w_mod"] = g_w_mod
    for k in small_names:
        g = small_tot[k]
        if k in SMALL_SHARD:
            w_sh = weights[k].shape[-1]
            g = lax.dynamic_slice_in_dim(g, shard * w_sh, w_sh, axis=g.ndim - 1)
        grads[k] = g

    delta, new_m, new_v = {}, {}, {}
    for name, _, _ in BIG + (("w_mod", None, None),):
        *upd, got = adamw(weights[name][0], grads[name], mom1[name][0], mom2[name][0], "adamw_" + name,
                          comm=all_gather_comm(ds16[8:16]) if name == "w_in" else None)
        delta[name], new_m[name], new_v[name] = upd
        if name == "w_in":
            grads["c_ctx"] = cctx_grad(got[0].reshape(N_DEV * 8, d), c_ctx.reshape(1, d))
    rest = [k for k in order if k not in delta]
    views = {k: (grads[k].shape if grads[k].ndim <= 3 else (-1, grads[k].shape[-1])) for k in rest}
    small = adamw_many(*[[t[k].reshape(views[k]) for k in rest] for t in (weights, grads, mom1, mom2)])
    n_rest = len(rest)
    for i, k in enumerate(rest):
        delta[k], new_m[k], new_v[k] = small[i], small[n_rest + i], small[2 * n_rest + i]

    shaped = lambda t: [t[k].reshape(weights[k].shape) for k in order]
    return (loss, res["grad_x"][None], *shaped(grads), *shaped(delta), *shaped(new_m), *shaped(new_v))
```

```python
import numpy as np
import jax
import jax.numpy as jnp
from jax import lax
from jax.experimental import pallas as pl
from jax.experimental.pallas import tpu as pltpu

F32 = jnp.float32
BF16 = jnp.bfloat16

D_MODEL = 1024
SEQ = 2048
CTX_LEN = 256
ZLEN = SEQ + CTX_LEN
GRID_W = 64
GRID_ROWS = SEQ // GRID_W
LRU_BLOCK_W = 128
LRU_BLOCKS = 8
LRU_C = 8.0
NA_HEADS = 16
HEAD_DIM = 64
NA_ROWS = 8
NA_COLS = 16
ROPE_BASE = 10000.0
D_FF = 2816
N_MOD = 6
IN_COLS = 7 * D_MODEL
EPS = 1e-6
NEG_INF = -1e30
N_DEV = 8
N_SHARD = 4

ADAM_LR = 0.001
ADAM_B1 = 0.9
ADAM_B2 = 0.999
ADAM_EPS = 1e-08
ADAM_WD = 0.01
ADAM_STEP = 10

ROW_TILE = 256
Q_ROWS = 4
Q_TILE = Q_ROWS * GRID_W
KEY_ROWS = 12
KEY_TILE = KEY_ROWS * GRID_W
BT_PAD = 4
BT_LEN = 24
VMEM_LIMIT_V7X = 56 * 1024 * 1024

MESH_T = pl.DeviceIdType.MESH


def _params(*sem):
    return pltpu.CompilerParams(dimension_semantics=sem if sem else None, vmem_limit_bytes=VMEM_LIMIT_V7X)


def _full(shape):
    nd = len(shape)
    return pl.BlockSpec(shape, lambda *_: (0,) * nd)


class Comm:
    def __init__(self, inputs, out_shapes, aliases, scratch, emit):
        self.inputs, self.out_shapes, self.aliases, self.scratch, self.emit = inputs, out_shapes, aliases, scratch, emit


def _call(body, *, name, grid, in_specs, out_specs, out_shape, args, scratch_shapes=(), sem=(), comm=None):
    n_in, n_out, n_sc = len(in_specs), len(out_specs), len(scratch_shapes)
    if comm is None:
        res = pl.pallas_call(body, name=name, grid=grid, in_specs=list(in_specs), out_specs=list(out_specs),
                             out_shape=list(out_shape), scratch_shapes=list(scratch_shapes),
                             compiler_params=_params(*sem))(*args)
        return list(res), []
    k_in, k_out = len(comm.inputs), len(comm.out_shapes)
    steps = int(np.prod(grid))

    def hosted(*refs):
        ins, cins = refs[:n_in], refs[n_in:n_in + k_in]
        at = n_in + k_in
        outs, couts = refs[at:at + n_out], refs[at + n_out:at + n_out + k_out]
        at += n_out + k_out
        scr, cscr = refs[at:at + n_sc], refs[at + n_sc:]
        start, mid, end = comm.emit(cins, couts, cscr)
        lin = pl.program_id(0)
        for ax in range(1, len(grid)):
            lin = lin * grid[ax] + pl.program_id(ax)
        pl.when(lin == 0)(start)
        body(*ins, *outs, *scr)
        pl.when(lin == steps - 1 - steps // 7)(mid)
        pl.when(lin == steps - 1)(end)

    hbm = pl.BlockSpec(memory_space=pl.ANY)
    res = pl.pallas_call(
        hosted, name=name, grid=grid, in_specs=list(in_specs) + [hbm] * k_in, out_specs=list(out_specs) + [hbm] * k_out,
        out_shape=list(out_shape) + list(comm.out_shapes), scratch_shapes=list(scratch_shapes) + list(comm.scratch),
        input_output_aliases={n_in + i: n_out + o for i, o in comm.aliases.items()},
        compiler_params=_params(*(("arbitrary",) * len(grid))))(*args, *comm.inputs)
    return list(res[:n_out]), list(res[n_out:])


def _sigmoid(x):
    return 0.5 * jnp.tanh(0.5 * x) + 0.5


def _gelu_parts(x):
    c0 = 0.7978845608028654
    inner = c0 * (x + 0.044715 * x * x * x)
    t = jnp.tanh(inner)
    g = 0.5 * x * (1.0 + t)
    dg = 0.5 * (1.0 + t) + 0.5 * x * (1.0 - t * t) * c0 * (1.0 + 3.0 * 0.044715 * x * x)
    return g, dg


def _dot_nt(a, b):
    return lax.dot_general(a, b, (((1,), (1,)), ((), ())), preferred_element_type=F32)


def _dot_tn(a, b):
    return lax.dot_general(a, b, (((0,), (0,)), ((), ())), preferred_element_type=F32)


def norm_matmul(xin, gain, shift, scale, w, name, tm, tn, ctx_rows=0, comm=None):
    r, d = xin.shape
    n = w.shape[1]
    assert r % tm == 0 and n % tn == 0

    def body(x_ref, g_ref, sh_ref, sc_ref, w_ref, y_ref, xn_hbm, xn_s, sem):
        j, i = pl.program_id(0), pl.program_id(1)
        rows = pl.ds(pl.multiple_of(i * tm, tm), tm)

        @pl.when(j == 0)
        def _():
            x = x_ref[...]
            nrm = x * lax.rsqrt(jnp.mean(x * x, axis=-1, keepdims=True) + EPS)
            sh, sc = sh_ref[shift.shape[0] - 1], sc_ref[shift.shape[0] - 1]
            if ctx_rows:
                is_ctx = i * tm + lax.broadcasted_iota(jnp.int32, (tm, 1), 0) < ctx_rows
                sh, sc = jnp.where(is_ctx, sh_ref[0], sh), jnp.where(is_ctx, sc_ref[0], sc)
            xn_s[rows, :] = ((nrm * g_ref[...]) * (1.0 + sc) + sh).astype(BF16)
            pltpu.make_async_copy(xn_s.at[rows, :], xn_hbm.at[rows, :], sem.at[i]).start()

        y_ref[...] = jnp.dot(xn_s[rows, :], w_ref[...], preferred_element_type=F32)

        @pl.when((j == n // tn - 1) & (i == r // tm - 1))
        def _():
            for t in range(r // tm):
                tile = pl.ds(t * tm, tm)
                pltpu.make_async_copy(xn_s.at[tile, :], xn_hbm.at[tile, :], sem.at[t]).wait()

    res, extra = _call(
        body, name=name, grid=(n // tn, r // tm),
        in_specs=[pl.BlockSpec((tm, d), lambda j, i: (i, 0)), _full((1, d)), _full(shift.shape), _full(scale.shape),
                  pl.BlockSpec((d, tn), lambda j, i: (0, j))],
        out_specs=[pl.BlockSpec((tm, tn), lambda j, i: (i, j)), pl.BlockSpec(memory_space=pl.ANY)],
        out_shape=[jax.ShapeDtypeStruct((r, n), F32), jax.ShapeDtypeStruct((r, d), BF16)],
        scratch_shapes=[pltpu.VMEM((r, d), BF16), pltpu.SemaphoreType.DMA((r // tm,))],
        sem=("arbitrary", "arbitrary"), args=(xin, gain, shift, scale, w), comm=comm)
    return res[1], res[0], extra


def _row_ids(n, w):
    return lax.broadcasted_iota(jnp.int32, (n, w), 0)


def _lru_conv(xr, cw, cb):
    row = _row_ids(ZLEN, LRU_BLOCK_W)
    segpos = jnp.where(row < CTX_LEN, row, row - CTX_LEN)
    seglen = jnp.where(row < CTX_LEN, CTX_LEN, SEQ)
    acc = xr * cw[2:3, :] + cb
    for k in (0, 1, 3):
        off = k - 2
        sh = pltpu.roll(xr, (-off) % ZLEN, 0)
        ok = (segpos + off >= 0) & (segpos + off < seglen)
        acc = acc + jnp.where(ok, sh, 0.0) * cw[k:k + 1, :]
    return acc


def _lru_conv_t(dxc, cw):
    row = _row_ids(ZLEN, LRU_BLOCK_W)
    segpos = jnp.where(row < CTX_LEN, row, row - CTX_LEN)
    seglen = jnp.where(row < CTX_LEN, CTX_LEN, SEQ)
    acc = dxc * cw[2:3, :]
    for k in (0, 1, 3):
        off = k - 2
        sh = pltpu.roll(dxc, off % ZLEN, 0)
        ok = (segpos - off >= 0) & (segpos - off < seglen)
        acc = acc + jnp.where(ok, sh, 0.0) * cw[k:k + 1, :]
    return acc


def _lru_gates(xc, xcb, wa, ba, wx, bx, lam):
    r = _sigmoid(jnp.dot(xcb, wa, preferred_element_type=F32) + ba)
    i = _sigmoid(jnp.dot(xcb, wx, preferred_element_type=F32) + bx)
    sp = jnp.maximum(-lam, 0.0) + jnp.log1p(jnp.exp(-jnp.abs(lam)))
    la = (-LRU_C) * r * sp
    a = jnp.exp(la)
    sq = jnp.sqrt(-jnp.tanh(la) * (1.0 + a * a))
    b = sq * i * xc
    return r, i, sp, a, sq, b


def _scan8_fwd(a, b, rid):
    for s in (1, 2, 4):
        a_s = pltpu.roll(a, s, 0)
        b_s = pltpu.roll(b, s, 0)
        m = rid >= s
        b = jnp.where(m, a * b_s + b, b)
        a = jnp.where(m, a * a_s, a)
    return a, b


def _scan8_rev(a, b, rid):
    for s in (1, 2, 4):
        a_s = pltpu.roll(a, 8 - s, 0)
        b_s = pltpu.roll(b, 8 - s, 0)
        m = rid < 8 - s
        b = jnp.where(m, a * b_s + b, b)
        a = jnp.where(m, a * a_s, a)
    return a, b


N_CHUNK = ZLEN // 8
CTX_CHUNKS = CTX_LEN // 8
SCAN_UNROLL = 8


def _scan_up(a_ref, b_ref, h_ref, lo, hi, carry):
    rid = _row_ids(8, LRU_BLOCK_W)
    assert (hi - lo) % SCAN_UNROLL == 0

    def step(g, c):
        base = pl.multiple_of((lo + g * SCAN_UNROLL) * 8, 8)
        for u in range(SCAN_UNROLL):
            sl = pl.ds(base + 8 * u, 8)
            a, b = _scan8_fwd(a_ref[sl, :], b_ref[sl, :], rid)
            h_ref[sl, :] = b + a * c
            c = b[7:8, :] + a[7:8, :] * c
        return c

    return lax.fori_loop(0, (hi - lo) // SCAN_UNROLL, step, carry)


def _scan_down(a_ref, b_ref, h_ref, lo, hi, carry):
    rid = _row_ids(8, LRU_BLOCK_W)
    assert (hi - lo) % SCAN_UNROLL == 0

    def step(g, c):
        base = pl.multiple_of((hi - (g + 1) * SCAN_UNROLL) * 8, 8)
        for u in reversed(range(SCAN_UNROLL)):
            sl = pl.ds(base + 8 * u, 8)
            a, b = _scan8_rev(a_ref[sl, :], b_ref[sl, :], rid)
            h_ref[sl, :] = b + a * c
            c = b[0:1, :] + a[0:1, :] * c
        return c

    return lax.fori_loop(0, (hi - lo) // SCAN_UNROLL, step, carry)


def _lru_scan_dir(d, a_ref, b_ref, h_ref):
    zero = jnp.zeros((1, LRU_BLOCK_W), F32)
    if d == 0:
        _scan_up(a_ref, b_ref, h_ref, 0, N_CHUNK, zero)
    else:
        c = _scan_down(a_ref, b_ref, h_ref, 0, CTX_CHUNKS, zero)
        _scan_down(a_ref, b_ref, h_ref, CTX_CHUNKS, N_CHUNK, c)


def _lru_in_specs():
    blk = lambda rows: pl.BlockSpec((rows, LRU_BLOCK_W), lambda b: (0, b))
    wspec = pl.BlockSpec((2, 1, LRU_BLOCK_W, LRU_BLOCK_W), lambda b: (0, b, 0, 0))
    return blk, wspec


def lru_fwd(p, conv_w, conv_b, wa, ba, wx, bx, lam, comm=None):
    blk, wspec = _lru_in_specs()

    def body(xr_ref, gx_ref, cw_ref, cb_ref, wa_ref, ba_ref, wx_ref, bx_ref, lam_ref, y_ref, a_s, b_s, h_s, hsum_s):
        xr = xr_ref[...]
        xc = _lru_conv(xr, cw_ref[...], cb_ref[...])
        xcb = xc.astype(BF16)
        for d in (0, 1):
            _, _, _, a, _, b = _lru_gates(xc, xcb, wa_ref[d, 0].astype(BF16), ba_ref[d:d + 1, :],
                                          wx_ref[d, 0].astype(BF16), bx_ref[d:d + 1, :], lam_ref[d:d + 1, :])
            a_s[...] = a
            b_s[...] = b
            _lru_scan_dir(d, a_s, b_s, h_s)
            if d == 0:
                hsum_s[...] = h_s[...]
            else:
                hsum_s[...] = hsum_s[...] + h_s[...]
        g, _ = _gelu_parts(gx_ref[CTX_LEN:, :])
        y_ref[...] = (hsum_s[CTX_LEN:, :] * g).astype(BF16)

    zs = pltpu.VMEM((ZLEN, LRU_BLOCK_W), F32)
    res, extra = _call(
        body, name="lru_fwd", grid=(LRU_BLOCKS,),
        in_specs=[blk(ZLEN), pl.BlockSpec((ZLEN, LRU_BLOCK_W), lambda b: (0, 24 + b)), blk(4), blk(1),
                  wspec, blk(2), wspec, blk(2), blk(2)],
        out_specs=[pl.BlockSpec((SEQ, LRU_BLOCK_W), lambda b: (0, b))],
        out_shape=[jax.ShapeDtypeStruct((SEQ, D_MODEL), BF16)],
        scratch_shapes=[zs, zs, zs, zs], sem=("arbitrary",),
        args=(p, p, conv_w, conv_b, wa, ba, wx, bx, lam), comm=comm)
    return res[0], extra


def _rope_tables():
    t = np.arange(SEQ)
    lane = np.arange(2 * HEAD_DIM)
    in_head = lane % HEAD_DIM
    j = (in_head % 32) % 16
    freq = ROPE_BASE ** (-j.astype(np.float64) / 16.0)
    pos = np.where(in_head[None, :] < 32, (t // GRID_W)[:, None], (t % GRID_W)[:, None]).astype(np.float64)
    ang = (pos.astype(np.float32) * freq.astype(np.float32)[None, :]).astype(np.float32)
    cos = np.cos(ang).astype(np.float32)
    sin = np.sin(ang).astype(np.float32)
    sgn = np.where((in_head % 32) < 16, -1.0, 1.0).astype(np.float32)
    cos = np.concatenate([np.ones((CTX_LEN, 2 * HEAD_DIM), np.float32), cos], 0)
    sin = np.concatenate([np.zeros((CTX_LEN, 2 * HEAD_DIM), np.float32), sin * sgn[None, :]], 0)
    return jnp.asarray(cos), jnp.asarray(sin)


def _head_ones():
    lane = np.arange(2 * HEAD_DIM)
    return jnp.asarray((lane[:, None] // HEAD_DIM == lane[None, :] // HEAD_DIM).astype(np.float32))


def _rope_partner(x):
    lane = lax.broadcasted_iota(jnp.int32, x.shape, 1)
    return jnp.where((lane % 32) < 16, pltpu.roll(x, 128 - 16, 1), pltpu.roll(x, 16, 1))


def _head_sum(t, ones):
    hi = t.astype(BF16)
    lo = (t - hi.astype(F32)).astype(BF16)
    ones_b = ones.astype(BF16)
    return jnp.dot(hi, ones_b, preferred_element_type=F32) + jnp.dot(lo, ones_b, preferred_element_type=F32)


def _head_rms(x, ones, gain):
    ms = _head_sum(x * x, ones) * (1.0 / HEAD_DIM)
    rstd = lax.rsqrt(ms + EPS)
    return x * rstd * gain, rstd


PREP_TILE = 768


def qkv_prep(p, qg2, kg2, cos, sin, ones, comm=None):
    scale = HEAD_DIM ** -0.5

    def body(q_ref, k_ref, v_ref, qg_ref, kg_ref, cos_ref, sin_ref, ones_ref, qr_ref, qp_ref, kk_ref, vv_ref):
        ones_m = ones_ref[...]
        c, s = cos_ref[...], sin_ref[...]
        qn, _ = _head_rms(q_ref[...], ones_m, qg_ref[...])
        qn = qn * scale
        qr_ref[...] = (qn * c + _rope_partner(qn) * s).astype(BF16)
        qp_ref[...] = qn.astype(BF16)
        kn, _ = _head_rms(k_ref[...], ones_m, kg_ref[...])
        kk_ref[...] = (kn * c + _rope_partner(kn) * s).astype(BF16)
        vv_ref[...] = v_ref[...].astype(BF16)

    col = lambda base: pl.BlockSpec((PREP_TILE, 128), lambda hp, i: (i, base + hp))
    small = pl.BlockSpec((1, 128), lambda hp, i: (0, 0))
    tab = pl.BlockSpec((PREP_TILE, 128), lambda hp, i: (i, 0))
    oshape = jax.ShapeDtypeStruct((ZLEN, D_MODEL), BF16)
    res, extra = _call(
        body, name="qkv_prep", grid=(NA_HEADS // 2, ZLEN // PREP_TILE),
        in_specs=[col(32), col(8), col(16), small, small, tab, tab, _full((128, 128))],
        out_specs=[col(0)] * 4, out_shape=[oshape] * 4, sem=("parallel", "parallel"),
        args=(p, p, p, qg2, kg2, cos, sin, ones), comm=comm)
    return (*res, extra)


def _bias_expand():
    qc = np.arange(GRID_W)[:, None]
    kc = np.arange(GRID_W)[None, :]
    col_start = np.clip(qc - NA_COLS // 2, 0, GRID_W - NA_COLS)
    in_win = (kc >= col_start) & (kc < col_start + NA_COLS)
    dc = np.clip(kc - qc, -(NA_COLS - 1), NA_COLS - 1) + (NA_COLS - 1)
    e = np.zeros((2 * NA_COLS - 1, GRID_W, GRID_W), np.float32)
    for d in range(2 * NA_COLS - 1):
        e[d] = ((dc == d) & in_win).astype(np.float32)
    pen = np.where(in_win, 0.0, NEG_INF).astype(np.float32)
    return e, pen


def bias_table(rpb2, comm=None):
    e, pen = _bias_expand()
    n_dr = 2 * NA_ROWS - 1
    ea = np.zeros((31, GRID_W, 128), np.float32)
    ea[:, :, :GRID_W] = e
    eb = np.zeros((31, GRID_W, 128), np.float32)
    eb[:, :, GRID_W:] = e
    pen2 = np.concatenate([pen, pen], 1)
    ea = jnp.asarray(ea.reshape(31, GRID_W * 128))
    eb = jnp.asarray(eb.reshape(31, GRID_W * 128))
    sel_a = np.zeros((BT_LEN, n_dr), np.float32)
    sel_b = np.zeros((BT_LEN, n_dr), np.float32)
    for r in range(BT_LEN):
        dr = r - BT_PAD
        if 0 <= dr < n_dr:
            sel_a[r, dr] = 1.0
        if 0 <= dr + 1 < n_dr:
            sel_b[r, dr + 1] = 1.0
    sel_a, sel_b = jnp.asarray(sel_a), jnp.asarray(sel_b)
    pen2 = jnp.asarray(pen2.reshape(1, GRID_W * 128))
    hi = lax.Precision.HIGHEST

    def body(rpb_ref, sa_ref, sb_ref, ea_ref, eb_ref, pen_ref, o_ref, ra_s, rb_s):
        for h in range(NA_HEADS):
            rp = rpb_ref[h]
            ra_s[h * BT_LEN:(h + 1) * BT_LEN, :] = jnp.dot(sa_ref[...], rp, preferred_element_type=F32, precision=hi)
            rb_s[h * BT_LEN:(h + 1) * BT_LEN, :] = jnp.dot(sb_ref[...], rp, preferred_element_type=F32, precision=hi)
        o_ref[...] = (jnp.dot(ra_s[...], ea_ref[...], preferred_element_type=F32, precision=hi)
                      + jnp.dot(rb_s[...], eb_ref[...], preferred_element_type=F32, precision=hi) + pen_ref[...])

    tcol = 2048
    rows = NA_HEADS * BT_LEN
    res, extra = _call(
        body, name="bias_table", grid=(GRID_W * 128 // tcol,),
        in_specs=[_full((NA_HEADS, n_dr, 31)), _full((BT_LEN, n_dr)), _full((BT_LEN, n_dr)),
                  pl.BlockSpec((31, tcol), lambda j: (0, j)), pl.BlockSpec((31, tcol), lambda j: (0, j)),
                  pl.BlockSpec((1, tcol), lambda j: (0, j))],
        out_specs=[pl.BlockSpec((rows, tcol), lambda j: (0, j))],
        out_shape=[jax.ShapeDtypeStruct((rows, GRID_W * 128), F32)],
        scratch_shapes=[pltpu.VMEM((rows, 31), F32), pltpu.VMEM((rows, 31), F32)], sem=("parallel",),
        args=(rpb2, sel_a, sel_b, ea, eb, pen2), comm=comm)
    return res[0].reshape(NA_HEADS, BT_LEN, GRID_W, 128), extra


def _key_window(j):
    ws = jnp.clip(Q_ROWS * j - 4, 0, GRID_ROWS - KEY_ROWS)
    return ws, pl.multiple_of(CTX_LEN + ws * GRID_W, 256)


def _head_mask(hh):
    lane = lax.broadcasted_iota(jnp.int32, (Q_TILE, 128), 1)
    return (lane < HEAD_DIM) if hh == 0 else (lane >= HEAD_DIM)


def _attn_scores(j, ws, q_rot_h, q_pl_h, kw, kc, hh, bt_ref, s_ref):
    s_ref[:, :KEY_TILE] = _dot_nt(q_rot_h, kw)
    s_ref[:, KEY_TILE:] = _dot_nt(q_pl_h, kc)
    lane = lax.broadcasted_iota(jnp.int32, (GRID_W, 128), 1)
    base = ws - Q_ROWS * j + (NA_ROWS - 1) + BT_PAD
    for qi in range(Q_ROWS):
        rs = jnp.clip(Q_ROWS * j + qi - NA_ROWS // 2, 0, GRID_ROWS - NA_ROWS)
        for m in range(KEY_ROWS // 2):
            k0 = ws + 2 * m
            p0 = jnp.where((k0 >= rs) & (k0 < rs + NA_ROWS), 0.0, NEG_INF)
            p1 = jnp.where((k0 + 1 >= rs) & (k0 + 1 < rs + NA_ROWS), 0.0, NEG_INF)
            pen = jnp.where(lane < GRID_W, p0, p1)
            rows = slice(qi * GRID_W, (qi + 1) * GRID_W)
            cols = slice(128 * m, 128 * (m + 1))
            s_ref[rows, cols] = s_ref[rows, cols] + bt_ref[hh, base + 2 * m - qi] + pen
    return base


def attn_fwd(q_rot, q_pl, kk, vv, bt, comm=None):
    def body(qr_ref, qp_ref, kk_ref, vv_ref, bt_ref, o_ref, lse_ref, s_ref):
        j = pl.program_id(1)
        ws, start = _key_window(j)
        win = pl.ds(start, KEY_TILE)
        kw, kc = kk_ref[win, :], kk_ref[:CTX_LEN, :]
        vw, vc = vv_ref[win, :], vv_ref[:CTX_LEN, :]
        qr, qp = qr_ref[...], qp_ref[...]
        outs = []
        for hh in range(2):
            msk = _head_mask(hh)
            _attn_scores(j, ws, jnp.where(msk, qr, 0), jnp.where(msk, qp, 0), kw, kc, hh, bt_ref, s_ref)
            s = s_ref[...]
            mx = jnp.max(s, axis=-1, keepdims=True)
            pr = jnp.exp(s - mx)
            l = jnp.sum(pr, axis=-1, keepdims=True)
            prb = pr.astype(BF16)
            o = jnp.dot(prb[:, :KEY_TILE], vw, preferred_element_type=F32)
            o = o + jnp.dot(prb[:, KEY_TILE:], vc, preferred_element_type=F32)
            outs.append(o / l)
            lse_ref[hh] = mx + jnp.log(l)
        o_ref[...] = jnp.where(_head_mask(0), outs[0], outs[1])

    qspec = pl.BlockSpec((Q_TILE, 128), lambda hp, j: (j + 1, hp))
    kspec = pl.BlockSpec((ZLEN, 128), lambda hp, j: (0, hp))
    res, extra = _call(
        body, name="attn_fwd", grid=(NA_HEADS // 2, SEQ // Q_TILE),
        in_specs=[qspec, qspec, kspec, kspec, pl.BlockSpec((2, BT_LEN, GRID_W, 128), lambda hp, j: (hp, 0, 0, 0))],
        out_specs=[pl.BlockSpec((Q_TILE, 128), lambda hp, j: (j, hp)),
                   pl.BlockSpec((2, Q_TILE, 1), lambda hp, j: (hp, j, 0))],
        out_shape=[jax.ShapeDtypeStruct((SEQ, D_MODEL), F32), jax.ShapeDtypeStruct((NA_HEADS, SEQ, 1), F32)],
        scratch_shapes=[pltpu.VMEM((Q_TILE, KEY_TILE + CTX_LEN), F32)], sem=("parallel", "arbitrary"),
        args=(q_rot, q_pl, kk, vv, bt), comm=comm)
    return res[0], res[1], extra


def merge_fwd(y_rnn, y_na, p, z, g2, w_rnn, w_na, w_out):
    def body(yr_ref, yn_ref, mr_ref, mn_ref, x_ref, g2_ref, wr_ref, wn_ref, wo_ref, u_ref, v_ref, mg_ref, out_ref, x1_ref):
        u = jnp.dot(yr_ref[...], wr_ref[...], preferred_element_type=F32)
        v = jnp.dot(yn_ref[...].astype(BF16), wn_ref[...], preferred_element_type=F32)
        merged = (_sigmoid(mr_ref[...]) * u + _sigmoid(mn_ref[...]) * v).astype(BF16)
        out = jnp.dot(merged, wo_ref[...], preferred_element_type=F32)
        u_ref[...] = u
        v_ref[...] = v
        mg_ref[...] = merged
        out_ref[...] = out
        x1_ref[...] = x_ref[...] + g2_ref[...] * out

    row = pl.BlockSpec((ROW_TILE, D_MODEL), lambda i: (i, 0))
    lat = lambda cb: pl.BlockSpec((ROW_TILE, D_MODEL), lambda i: (i + 1, cb))
    wspec = _full((D_MODEL, D_MODEL))
    f32o = jax.ShapeDtypeStruct((SEQ, D_MODEL), F32)
    return pl.pallas_call(
        body, name="merge_fwd", grid=(SEQ // ROW_TILE,),
        in_specs=[row, row, lat(5), lat(6), lat(0), _full((1, D_MODEL)), wspec, wspec, wspec],
        out_specs=[row] * 5,
        out_shape=[f32o, f32o, jax.ShapeDtypeStruct((SEQ, D_MODEL), BF16), f32o, f32o],
        compiler_params=_params("parallel"),
    )(y_rnn, y_na, p, p, z, g2, w_rnn, w_na, w_out)


FF_TILE = 256
FF_TILES = D_FF // FF_TILE


FF_ROWS = 64
FF_HALO = 8
FF_SLAB = FF_ROWS + 2 * FF_HALO


def _ffn_row_chunks(chunk, init):
    carry = chunk(0, 0, -1, init)
    carry = lax.fori_loop(1, SEQ // FF_ROWS - 1,
                          lambda ci, cr: chunk(pl.multiple_of(ci * FF_ROWS - FF_HALO, 8), FF_HALO, 0, cr), carry)
    return chunk(SEQ - FF_SLAB, 2 * FF_HALO, 1, carry)


def _ffn_shifts(edge):
    row = _row_ids(FF_SLAB, FF_TILE)

    def prev(x):
        r = pltpu.roll(x, 1, 0)
        return jnp.where(row >= 1, r, 0.0) if edge == -1 else r

    def nxt(x):
        r = pltpu.roll(x, FF_SLAB - 1, 0)
        return jnp.where(row < FF_SLAB - 1, r, 0.0) if edge == 1 else r

    return prev, nxt


def ffn_act(hpre, conv_w, conv_b):
    def body(ha_ref, hg_ref, wa_ref, wg_ref, ba_ref, bg_ref, o_ref):
        wa, wg, ba, bg = wa_ref[...], wg_ref[...], ba_ref[...], bg_ref[...]

        def chunk(lo, mid, edge, carry):
            prev, nxt = _ffn_shifts(edge)
            ha, hg = ha_ref[pl.ds(lo, FF_SLAB), :], hg_ref[pl.ds(lo, FF_SLAB), :]
            a = prev(ha) * wa[0:1] + ha * wa[1:2] + nxt(ha) * wa[2:3] + ba
            g = prev(hg) * wg[0:1] + hg * wg[1:2] + nxt(hg) * wg[2:3] + bg
            o_ref[pl.ds(lo + mid, FF_ROWS), :] = (a * _sigmoid(a) * g)[mid:mid + FF_ROWS].astype(BF16)
            return carry

        _ffn_row_chunks(chunk, 0)

    col = lambda rows, off: pl.BlockSpec((rows, FF_TILE), lambda j: (0, j + off))
    return pl.pallas_call(
        body, name="ffn_act", grid=(FF_TILES,),
        in_specs=[col(SEQ, 0), col(SEQ, FF_TILES), col(3, 0), col(3, FF_TILES), col(1, 0), col(1, FF_TILES)],
        out_specs=col(SEQ, 0),
        out_shape=jax.ShapeDtypeStruct((SEQ, D_FF), BF16),
        compiler_params=_params("parallel"),
    )(hpre, hpre, conv_w, conv_w, conv_b, conv_b)


def ffn_down_loss(act, w_down, x1, g5, target):
    def body(a_ref, w_ref, x1_ref, g5_ref, t_ref, f_ref, dy_ref, df_ref, ls_ref, dg_ref):
        i = pl.program_id(0)
        f = jnp.dot(a_ref[...], w_ref[...], preferred_element_type=F32)
        g5 = g5_ref[...]
        err = x1_ref[...] + g5 * f - t_ref[...]
        dy = err * (1.0 / D_MODEL)
        f_ref[...] = f
        dy_ref[...] = dy
        df_ref[...] = (dy * g5).astype(BF16)

        @pl.when(i == 0)
        def _():
            ls_ref[...] = jnp.zeros_like(ls_ref)
            dg_ref[...] = jnp.zeros_like(dg_ref)

        ls_ref[...] = ls_ref[...] + jnp.sum(err * err)
        dg_ref[...] = dg_ref[...] + jnp.sum(dy * f, axis=0, keepdims=True)

    row = pl.BlockSpec((ROW_TILE, D_MODEL), lambda i: (i, 0))
    f32o = jax.ShapeDtypeStruct((SEQ, D_MODEL), F32)
    return pl.pallas_call(
        body, name="ffn_down_loss", grid=(SEQ // ROW_TILE,),
        in_specs=[pl.BlockSpec((ROW_TILE, D_FF), lambda i: (i, 0)), _full((D_FF, D_MODEL)), row, _full((1, D_MODEL)), row],
        out_specs=[row, row, row, _full((8, 128)), _full((1, D_MODEL))],
        out_shape=[f32o, f32o, jax.ShapeDtypeStruct((SEQ, D_MODEL), BF16), jax.ShapeDtypeStruct((8, 128), F32),
                   jax.ShapeDtypeStruct((1, D_MODEL), F32)],
        compiler_params=_params("arbitrary"),
    )(act, w_down, x1, g5, target)


def ffn_act_bwd(hpre, df, w_down, conv_w, conv_b):
    def body(ha_ref, hg_ref, df_ref, wd_ref, wa_ref, wg_ref, ba_ref, bg_ref,
             dha_ref, dhg_ref, dwa_ref, dwg_ref, dba_ref, dbg_ref, da_ref):
        wa, wg, ba, bg = wa_ref[...], wg_ref[...], ba_ref[...], bg_ref[...]
        da_ref[...] = _dot_nt(df_ref[...], wd_ref[...])

        def chunk(lo, mid, edge, acc):
            prev, nxt = _ffn_shifts(edge)
            rows = pl.ds(lo, FF_SLAB)
            ha, hg, dact = ha_ref[rows, :], hg_ref[rows, :], da_ref[rows, :]
            hap, han, hgp, hgn = prev(ha), nxt(ha), prev(hg), nxt(hg)
            a = hap * wa[0:1] + ha * wa[1:2] + han * wa[2:3] + ba
            g = hgp * wg[0:1] + hg * wg[1:2] + hgn * wg[2:3] + bg
            sig = _sigmoid(a)
            dca = dact * g * (sig * (1.0 + a * (1.0 - sig)))
            dcg = dact * a * sig
            m = slice(mid, mid + FF_ROWS)
            sums = []
            for dc, h, hp, hn, w, dh_ref in ((dca, ha, hap, han, wa, dha_ref), (dcg, hg, hgp, hgn, wg, dhg_ref)):
                dcm = dc[m]
                sums += [jnp.sum(dcm * hp[m], axis=0, keepdims=True), jnp.sum(dcm * h[m], axis=0, keepdims=True),
                         jnp.sum(dcm * hn[m], axis=0, keepdims=True), jnp.sum(dcm, axis=0, keepdims=True)]
                dh = nxt(dc) * w[0:1] + dc * w[1:2] + prev(dc) * w[2:3]
                dh_ref[pl.ds(lo + mid, FF_ROWS), :] = dh[m].astype(BF16)
            return tuple(x + y for x, y in zip(acc, sums))

        acc = _ffn_row_chunks(chunk, tuple(jnp.zeros((1, FF_TILE), F32) for _ in range(8)))
        dwa_ref[0:1, :], dwa_ref[1:2, :], dwa_ref[2:3, :], dba_ref[...] = acc[0], acc[1], acc[2], acc[3]
        dwg_ref[0:1, :], dwg_ref[1:2, :], dwg_ref[2:3, :], dbg_ref[...] = acc[4], acc[5], acc[6], acc[7]

    col = lambda rows, off: pl.BlockSpec((rows, FF_TILE), lambda j: (0, j + off))
    hshape = jax.ShapeDtypeStruct((SEQ, D_FF), BF16)
    wshape = jax.ShapeDtypeStruct((3, D_FF), F32)
    bshape = jax.ShapeDtypeStruct((1, D_FF), F32)
    return pl.pallas_call(
        body, name="ffn_act_bwd", grid=(FF_TILES,),
        in_specs=[col(SEQ, 0), col(SEQ, FF_TILES), _full((SEQ, D_MODEL)),
                  pl.BlockSpec((FF_TILE, D_MODEL), lambda j: (j, 0)),
                  col(3, 0), col(3, FF_TILES), col(1, 0), col(1, FF_TILES)],
        out_specs=[col(SEQ, 0), col(SEQ, 0), col(3, 0), col(3, 0), col(1, 0), col(1, 0)],
        out_shape=[hshape, hshape, wshape, wshape, bshape, bshape],
        scratch_shapes=[pltpu.VMEM((SEQ, FF_TILE), F32)],
        compiler_params=_params("parallel"),
    )(hpre, hpre, df, w_down, conv_w, conv_w, conv_b, conv_b)


def _norm_mod_bwd(x, dxn, gain, scale):
    rstd = lax.rsqrt(jnp.mean(x * x, axis=-1, keepdims=True) + EPS)
    nrm = x * rstd
    dsh = jnp.sum(dxn, axis=0, keepdims=True)
    dsc = jnp.sum(dxn * nrm, axis=0, keepdims=True) * gain
    dgn = jnp.sum(dxn * nrm, axis=0, keepdims=True) * (1.0 + scale)
    dn = dxn * (gain * (1.0 + scale))
    dx = rstd * (dn - nrm * jnp.mean(dn * nrm, axis=-1, keepdims=True))
    return dx, dsh, dsc, dgn


def ffn_up_bwd(dha, dhg, w_up, x1, dy, gain, scale):
    def body(dha_ref, dhg_ref, w_ref, x_ref, dy_ref, g_ref, sc_ref, dx_ref, dsh_ref, dsc_ref, dgn_ref):
        i = pl.program_id(0)
        dxn = _dot_nt(dha_ref[...], w_ref[:, :D_FF]) + _dot_nt(dhg_ref[...], w_ref[:, D_FF:])
        dx, dsh, dsc, dgn = _norm_mod_bwd(x_ref[...], dxn, g_ref[...], sc_ref[...])
        dx_ref[...] = dy_ref[...] + dx

        @pl.when(i == 0)
        def _():
            dsh_ref[...] = dsh
            dsc_ref[...] = dsc
            dgn_ref[...] = dgn

        @pl.when(i > 0)
        def _():
            dsh_ref[...] = dsh_ref[...] + dsh
            dsc_ref[...] = dsc_ref[...] + dsc
            dgn_ref[...] = dgn_ref[...] + dgn

    row = pl.BlockSpec((ROW_TILE, D_MODEL), lambda i: (i, 0))
    vec = _full((1, D_MODEL))
    vshape = jax.ShapeDtypeStruct((1, D_MODEL), F32)
    return pl.pallas_call(
        body, name="ffn_up_bwd", grid=(SEQ // ROW_TILE,),
        in_specs=[pl.BlockSpec((ROW_TILE, D_FF), lambda i: (i, 0)), pl.BlockSpec((ROW_TILE, D_FF), lambda i: (i, 0)),
                  _full((D_MODEL, 2 * D_FF)), row, row, vec, vec],
        out_specs=[row, vec, vec, vec],
        out_shape=[jax.ShapeDtypeStruct((SEQ, D_MODEL), F32), vshape, vshape, vshape],
        compiler_params=_params("arbitrary"),
    )(dha, dhg, w_up, x1, dy, gain, scale)


def merge_bwd(dx1, out, g2, p, u, v, w_rnn, w_na, w_out, comm=None):
    def body(dx_ref, out_ref, g2_ref, mr_ref, mn_ref, u_ref, v_ref, wr_ref, wn_ref, wo_ref,
             dout_ref, du_ref, dv_ref, dmr_ref, dmn_ref, dyr_ref, dyn_ref, dg2_ref):
        i = pl.program_id(0)

        @pl.when(i == 0)
        def _():
            dmr_ref[...] = jnp.zeros_like(dmr_ref)
            dmn_ref[...] = jnp.zeros_like(dmn_ref)
            dg2_ref[...] = jnp.zeros_like(dg2_ref)

        @pl.when(i > 0)
        def _():
            dx = dx_ref[...]
            dg2_ref[...] = dg2_ref[...] + jnp.sum(dx * out_ref[...], axis=0, keepdims=True)
            dout = (dx * g2_ref[...]).astype(BF16)
            dout_ref[...] = dout
            dm = _dot_nt(dout, wo_ref[...])
            sr = _sigmoid(mr_ref[...])
            sn = _sigmoid(mn_ref[...])
            du = (dm * sr).astype(BF16)
            dv = (dm * sn).astype(BF16)
            du_ref[...] = du
            dv_ref[...] = dv
            dmr_ref[...] = (dm * u_ref[...] * (sr * (1.0 - sr))).astype(BF16)
            dmn_ref[...] = (dm * v_ref[...] * (sn * (1.0 - sn))).astype(BF16)
            dyr_ref[...] = _dot_nt(du, wr_ref[...])
            dyn_ref[...] = _dot_nt(dv, wn_ref[...])

    lat = pl.BlockSpec((ROW_TILE, D_MODEL), lambda i: (jnp.maximum(i - 1, 0), 0))
    zrow = pl.BlockSpec((ROW_TILE, D_MODEL), lambda i: (i, 0))
    pcol = lambda cb: pl.BlockSpec((ROW_TILE, D_MODEL), lambda i: (i, cb))
    wspec = _full((D_MODEL, D_MODEL))
    tb = jax.ShapeDtypeStruct((SEQ, D_MODEL), BF16)
    zb = jax.ShapeDtypeStruct((ZLEN, D_MODEL), BF16)
    tf = jax.ShapeDtypeStruct((SEQ, D_MODEL), F32)
    res, extra = _call(
        body, name="merge_bwd", grid=(ZLEN // ROW_TILE,),
        in_specs=[lat, lat, _full((1, D_MODEL)), pcol(5), pcol(6), lat, lat, wspec, wspec, wspec],
        out_specs=[lat, lat, lat, zrow, zrow, lat, lat, _full((1, D_MODEL))],
        out_shape=[tb, tb, tb, zb, zb, tf, tf, jax.ShapeDtypeStruct((1, D_MODEL), F32)],
        sem=("arbitrary",), args=(dx1, out, g2, p, p, u, v, w_rnn, w_na, w_out), comm=comm)
    return (*res, extra)


def attn_bwd(q_rot, q_pl, kk, vv, bt, y_na, d_yna, lse, comm=None):
    def body(qr_ref, qp_ref, kk_ref, vv_ref, bt_ref, o_ref, do_ref, lse_ref,
             dqr_ref, dqp_ref, dk_ref, dv_ref, dbt_ref, s_ref):
        jj = pl.program_id(1)

        @pl.when(jj == 0)
        def _():
            dqr_ref[...] = jnp.zeros_like(dqr_ref)
            dqp_ref[...] = jnp.zeros_like(dqp_ref)
            dk_ref[...] = jnp.zeros_like(dk_ref)
            dv_ref[...] = jnp.zeros_like(dv_ref)
            dbt_ref[...] = jnp.zeros_like(dbt_ref)

        @pl.when(jj > 0)
        def _():
            j = jj - 1
            ws, start = _key_window(j)
            win = pl.ds(start, KEY_TILE)
            kw, kc = kk_ref[win, :], kk_ref[:CTX_LEN, :]
            vw, vc = vv_ref[win, :], vv_ref[:CTX_LEN, :]
            qr, qp = qr_ref[...], qp_ref[...]
            do = do_ref[...]
            do_o = do * o_ref[...]
            dq_r, dq_p = [], []
            for hh in range(2):
                msk = _head_mask(hh)
                q_r, q_p = jnp.where(msk, qr, 0), jnp.where(msk, qp, 0)
                base = _attn_scores(j, ws, q_r, q_p, kw, kc, hh, bt_ref, s_ref)
                pr = jnp.exp(s_ref[...] - lse_ref[hh])
                delta = jnp.sum(jnp.where(msk, do_o, 0.0), axis=-1, keepdims=True)
                dob = jnp.where(msk, do, 0.0).astype(BF16)
                ds_lat = pr[:, :KEY_TILE] * (_dot_nt(dob, vw) - delta)
                ds_ctx = pr[:, KEY_TILE:] * (_dot_nt(dob, vc) - delta)
                for qi in range(Q_ROWS):
                    for m in range(KEY_ROWS // 2):
                        idx = base + 2 * m - qi
                        dbt_ref[hh, idx] = dbt_ref[hh, idx] + ds_lat[qi * GRID_W:(qi + 1) * GRID_W, 128 * m:128 * (m + 1)]
                dsb_lat = ds_lat.astype(BF16)
                dsb_ctx = ds_ctx.astype(BF16)
                prb = pr.astype(BF16)
                dq_r.append(jnp.dot(dsb_lat, kw, preferred_element_type=F32))
                dq_p.append(jnp.dot(dsb_ctx, kc, preferred_element_type=F32))
                dk_ref[win, :] = dk_ref[win, :] + _dot_tn(dsb_lat, q_r)
                dk_ref[:CTX_LEN, :] = dk_ref[:CTX_LEN, :] + _dot_tn(dsb_ctx, q_p)
                dv_ref[win, :] = dv_ref[win, :] + _dot_tn(prb[:, :KEY_TILE], dob)
                dv_ref[:CTX_LEN, :] = dv_ref[:CTX_LEN, :] + _dot_tn(prb[:, KEY_TILE:], dob)
            dqr_ref[...] = jnp.where(_head_mask(0), dq_r[0], dq_r[1])
            dqp_ref[...] = jnp.where(_head_mask(0), dq_p[0], dq_p[1])

    lat = lambda jj: jnp.maximum(jj - 1, 0)
    qspec = pl.BlockSpec((Q_TILE, 128), lambda hp, jj: (lat(jj) + 1, hp))
    kspec = pl.BlockSpec((ZLEN, 128), lambda hp, jj: (0, hp))
    btspec = pl.BlockSpec((2, BT_LEN, GRID_W, 128), lambda hp, jj: (hp, 0, 0, 0))
    ospec = pl.BlockSpec((Q_TILE, 128), lambda hp, jj: (lat(jj), hp))
    dqspec = pl.BlockSpec((Q_TILE, 128), lambda hp, jj: (jj, hp))
    zshape = jax.ShapeDtypeStruct((ZLEN, D_MODEL), F32)
    res, extra = _call(
        body, name="attn_bwd", grid=(NA_HEADS // 2, ZLEN // Q_TILE),
        in_specs=[qspec, qspec, kspec, kspec, btspec, ospec, ospec,
                  pl.BlockSpec((2, Q_TILE, 1), lambda hp, jj: (hp, lat(jj), 0))],
        out_specs=[dqspec, dqspec, kspec, kspec, btspec],
        out_shape=[zshape, zshape, zshape, zshape, jax.ShapeDtypeStruct((NA_HEADS, BT_LEN, GRID_W, 128), F32)],
        scratch_shapes=[pltpu.VMEM((Q_TILE, KEY_TILE + CTX_LEN), F32)], sem=("parallel", "arbitrary"),
        args=(q_rot, q_pl, kk, vv, bt, y_na, d_yna, lse), comm=comm)
    return (*res, extra)


def qkv_bwd(dq_rot, dq_pl, dk, dv, p, qg2, kg2, cos, sin, ones, comm=None):
    scale = HEAD_DIM ** -0.5
    n_hp, n_i = NA_HEADS // 2, ZLEN // PREP_TILE

    def norm_rope_bwd(d_rot, d_extra, x, gain, cos_t, sin_t, ones_m, dx_ref, acc_ref):
        xh, rstd = _head_rms(x, ones_m, 1.0)
        dn = d_rot * cos_t + _rope_partner(d_rot * sin_t)
        if d_extra is not None:
            dn = (dn + d_extra) * scale
        acc_ref[...] = acc_ref[...] + jnp.sum(dn * xh, axis=0, keepdims=True)
        dxh = dn * gain
        seg = _head_sum(dxh * xh, ones_m) * (1.0 / HEAD_DIM)
        dx_ref[...] = (rstd * (dxh - xh * seg)).astype(BF16)

    def body(dqr_ref, dqp_ref, dk_ref, dv_ref, xq_ref, xk_ref, qg_ref, kg_ref, cos_ref, sin_ref, ones_ref,
             dxq_ref, dxk_ref, dxv_ref, dgq_ref, dgk_ref, accq_ref, acck_ref):
        hp, i = pl.program_id(0), pl.program_id(1)

        @pl.when((hp == 0) & (i == 0))
        def _():
            accq_ref[...] = jnp.zeros_like(accq_ref)
            acck_ref[...] = jnp.zeros_like(acck_ref)

        ones_m = ones_ref[...]
        cos_t, sin_t = cos_ref[...], sin_ref[...]
        norm_rope_bwd(dqr_ref[...], dqp_ref[...], xq_ref[...], qg_ref[...], cos_t, sin_t, ones_m, dxq_ref, accq_ref)
        norm_rope_bwd(dk_ref[...], None, xk_ref[...], kg_ref[...], cos_t, sin_t, ones_m, dxk_ref, acck_ref)
        dxv_ref[...] = dv_ref[...].astype(BF16)

        @pl.when((hp == n_hp - 1) & (i == n_i - 1))
        def _():
            dgq_ref[...] = accq_ref[:, :HEAD_DIM] + accq_ref[:, HEAD_DIM:]
            dgk_ref[...] = acck_ref[:, :HEAD_DIM] + acck_ref[:, HEAD_DIM:]

    col = lambda base: pl.BlockSpec((PREP_TILE, 128), lambda hp, i: (i, base + hp))
    small = pl.BlockSpec((1, 128), lambda hp, i: (0, 0))
    tab = pl.BlockSpec((PREP_TILE, 128), lambda hp, i: (i, 0))
    zb = jax.ShapeDtypeStruct((ZLEN, D_MODEL), BF16)
    gshape = jax.ShapeDtypeStruct((1, HEAD_DIM), F32)
    res, extra = _call(
        body, name="qkv_bwd", grid=(n_hp, n_i),
        in_specs=[col(0)] * 4 + [col(32), col(8), small, small, tab, tab, _full((128, 128))],
        out_specs=[col(0)] * 3 + [_full((1, HEAD_DIM))] * 2,
        out_shape=[zb, zb, zb, gshape, gshape],
        scratch_shapes=[pltpu.VMEM((1, 128), F32)] * 2, sem=("arbitrary", "arbitrary"),
        args=(dq_rot, dq_pl, dk, dv, p, p, qg2, kg2, cos, sin, ones), comm=comm)
    return (*res, extra)


def rpb_grad(dbt):
    e, _ = _bias_expand()
    n_dr = 2 * NA_ROWS - 1
    ea = np.zeros((31, GRID_W, 128), np.float32)
    ea[:, :, :GRID_W] = e
    eb = np.zeros((31, GRID_W, 128), np.float32)
    eb[:, :, GRID_W:] = e
    eat = jnp.asarray(ea.reshape(31, GRID_W * 128).T.copy())
    ebt = jnp.asarray(eb.reshape(31, GRID_W * 128).T.copy())
    sel_at = np.zeros((n_dr, BT_LEN), np.float32)
    sel_bt = np.zeros((n_dr, BT_LEN), np.float32)
    for r in range(BT_LEN):
        dr = r - BT_PAD
        if 0 <= dr < n_dr:
            sel_at[dr, r] = 1.0
        if 0 <= dr + 1 < n_dr:
            sel_bt[dr + 1, r] = 1.0
    hi = lax.Precision.HIGHEST

    tk = 2048
    wide = GRID_W * 128
    rows = NA_HEADS * BT_LEN
    n_k = wide // tk

    def body(d_ref, sa_ref, sb_ref, ea_ref, eb_ref, o_ref, a_s, b_s):
        k = pl.program_id(0)
        dm = d_ref[...]
        d_hi = dm.astype(BF16)
        rest = dm - d_hi.astype(F32)
        d_mid = rest.astype(BF16)
        d_lo = (rest - d_mid.astype(F32)).astype(BF16)
        ea_b, eb_b = ea_ref[...].astype(BF16), eb_ref[...].astype(BF16)
        a = sum(jnp.dot(t, ea_b, preferred_element_type=F32) for t in (d_hi, d_mid, d_lo))
        b = sum(jnp.dot(t, eb_b, preferred_element_type=F32) for t in (d_hi, d_mid, d_lo))

        @pl.when(k == 0)
        def _():
            a_s[...] = a
            b_s[...] = b

        @pl.when(k > 0)
        def _():
            a_s[...] = a_s[...] + a
            b_s[...] = b_s[...] + b

        @pl.when(k == n_k - 1)
        def _():
            for h in range(NA_HEADS):
                sl = slice(h * BT_LEN, (h + 1) * BT_LEN)
                o_ref[h] = (jnp.dot(sa_ref[...], a_s[sl, :], preferred_element_type=F32, precision=hi)
                            + jnp.dot(sb_ref[...], b_s[sl, :], preferred_element_type=F32, precision=hi))

    return pl.pallas_call(
        body, name="rpb_grad", grid=(n_k,),
        in_specs=[pl.BlockSpec((rows, tk), lambda k: (0, k)), _full((n_dr, BT_LEN)), _full((n_dr, BT_LEN)),
                  pl.BlockSpec((tk, 31), lambda k: (k, 0)), pl.BlockSpec((tk, 31), lambda k: (k, 0))],
        out_specs=_full((NA_HEADS, n_dr, 31)),
        out_shape=jax.ShapeDtypeStruct((NA_HEADS, n_dr, 31), F32),
        scratch_shapes=[pltpu.VMEM((rows, 31), F32), pltpu.VMEM((rows, 31), F32)],
        compiler_params=_params("arbitrary"),
    )(dbt.reshape(rows, wide), jnp.asarray(sel_at), jnp.asarray(sel_bt), eat, ebt)


def lru_bwd(p, d_yrnn, conv_w, conv_b, wa, ba, wx, bx, lam, comm=None):
    blk, wspec = _lru_in_specs()

    def body(xr_ref, gx_ref, dy_ref, cw_ref, cb_ref, wa_ref, ba_ref, wx_ref, bx_ref, lam_ref,
             dxr_ref, dgx_ref, dcw_ref, dcb_ref, dwa_ref, dba_ref, dwx_ref, dbx_ref, dlam_ref,
             a_s, b_s, h_s, l_s, hsum_s, dxc_s, dh_s):
        xr = xr_ref[...]
        cw = cw_ref[...]
        xc = _lru_conv(xr, cw, cb_ref[...])
        xcb = xc.astype(BF16)
        g, dg = _gelu_parts(gx_ref[CTX_LEN:, :])
        dy = dy_ref[...]
        dh_s[:CTX_LEN, :] = jnp.zeros((CTX_LEN, LRU_BLOCK_W), F32)
        dh_s[CTX_LEN:, :] = dy * g
        row = _row_ids(ZLEN, LRU_BLOCK_W)
        zero = jnp.zeros((1, LRU_BLOCK_W), F32)
        for d in (0, 1):
            wab = wa_ref[d, 0].astype(BF16)
            wxb = wx_ref[d, 0].astype(BF16)
            lam_d = lam_ref[d:d + 1, :]
            r, gi, sp, a, sq, b = _lru_gates(xc, xcb, wab, ba_ref[d:d + 1, :], wxb, bx_ref[d:d + 1, :], lam_d)
            a_s[...] = a
            b_s[...] = b
            _lru_scan_dir(d, a_s, b_s, h_s)
            h = h_s[...]
            if d == 0:
                hsum_s[...] = h
                h_prev = jnp.where(row >= 1, pltpu.roll(h, 1, 0), 0.0)
                a_s[...] = pltpu.roll(a, ZLEN - 1, 0)
                _scan_down(a_s, dh_s, l_s, 0, N_CHUNK, zero)
            else:
                hsum_s[...] = hsum_s[...] + h
                h_prev = jnp.where(row == CTX_LEN - 1, 0.0, pltpu.roll(h, ZLEN - 1, 0))
                a_s[...] = pltpu.roll(a, 1, 0)
                c = _scan_up(a_s, dh_s, l_s, CTX_CHUNKS, N_CHUNK, zero)
                _scan_up(a_s, dh_s, l_s, 0, CTX_CHUNKS, c)
            db = l_s[...]
            da = db * h_prev
            dsq = db * gi * xc
            dgi = db * sq * xc
            dxc_d = db * sq * gi
            dla = da * a - dsq * (a * a) / sq
            dr = dla * ((-LRU_C) * sp)
            dsp = jnp.sum(dla * ((-LRU_C) * r), axis=0, keepdims=True)
            dlam_ref[d:d + 1, :] = -dsp * _sigmoid(-lam_d)
            dzr = dr * r * (1.0 - r)
            dzi = dgi * gi * (1.0 - gi)
            dba_ref[d:d + 1, :] = jnp.sum(dzr, axis=0, keepdims=True)
            dbx_ref[d:d + 1, :] = jnp.sum(dzi, axis=0, keepdims=True)
            dzrb = dzr.astype(BF16)
            dzib = dzi.astype(BF16)
            dwa_ref[d, 0] = _dot_tn(xcb, dzrb)
            dwx_ref[d, 0] = _dot_tn(xcb, dzib)
            dxc_d = dxc_d + _dot_nt(dzrb, wab) + _dot_nt(dzib, wxb)
            if d == 0:
                dxc_s[...] = dxc_d
            else:
                dxc_s[...] = dxc_s[...] + dxc_d
        dxc = dxc_s[...]
        dxr_ref[...] = _lru_conv_t(dxc, cw).astype(BF16)
        dcb_ref[...] = jnp.sum(dxc, axis=0, keepdims=True)
        segpos = jnp.where(row < CTX_LEN, row, row - CTX_LEN)
        seglen = jnp.where(row < CTX_LEN, CTX_LEN, SEQ)
        for k in range(4):
            off = k - 2
            if off == 0:
                sh = xr
            else:
                ok = (segpos + off >= 0) & (segpos + off < seglen)
                sh = jnp.where(ok, pltpu.roll(xr, (-off) % ZLEN, 0), 0.0)
            dcw_ref[k:k + 1, :] = jnp.sum(dxc * sh, axis=0, keepdims=True)
        dgx_ref[:CTX_LEN, :] = jnp.zeros((CTX_LEN, LRU_BLOCK_W), BF16)
        dgx_ref[CTX_LEN:, :] = (dy * hsum_s[CTX_LEN:, :] * dg).astype(BF16)

    zs = pltpu.VMEM((ZLEN, LRU_BLOCK_W), F32)
    zb = jax.ShapeDtypeStruct((ZLEN, D_MODEL), BF16)
    v2 = jax.ShapeDtypeStruct((2, D_MODEL), F32)
    w4 = jax.ShapeDtypeStruct((2, LRU_BLOCKS, LRU_BLOCK_W, LRU_BLOCK_W), F32)
    res, extra = _call(
        body, name="lru_bwd", grid=(LRU_BLOCKS,),
        in_specs=[blk(ZLEN), pl.BlockSpec((ZLEN, LRU_BLOCK_W), lambda b: (0, 24 + b)), blk(SEQ), blk(4), blk(1),
                  wspec, blk(2), wspec, blk(2), blk(2)],
        out_specs=[blk(ZLEN), blk(ZLEN), blk(4), blk(1), wspec, blk(2), wspec, blk(2), blk(2)],
        out_shape=[zb, zb, jax.ShapeDtypeStruct((4, D_MODEL), F32), jax.ShapeDtypeStruct((1, D_MODEL), F32),
                   w4, v2, w4, v2, v2],
        scratch_shapes=[zs] * 7, sem=("arbitrary",),
        args=(p, p, d_yrnn, conv_w, conv_b, wa, ba, wx, bx, lam), comm=comm)
    return (*res, extra)


def in_proj_bwd(dgs, w_in, z, dx1, gain, scale, comm=None):
    def body(*refs):
        dg_refs = refs[:7]
        w_ref, z_ref, dx1_ref, g_ref, sc_ref, gx_ref, dsh_ref, dsc_ref, dgn_ref = refs[7:]
        i = pl.program_id(0)
        dxn = _dot_nt(dg_refs[0][...], w_ref[:, 0:D_MODEL])
        for g in range(1, 7):
            dxn = dxn + _dot_nt(dg_refs[g][...], w_ref[:, g * D_MODEL:(g + 1) * D_MODEL])
        dx, dsh, dsc, dgn = _norm_mod_bwd(z_ref[...], dxn, g_ref[...], sc_ref[0])

        @pl.when(i <= 1)
        def _():
            dsh_ref[0] = dsh
            dsc_ref[0] = dsc

        @pl.when(i > 1)
        def _():
            dsh_ref[0] = dsh_ref[0] + dsh
            dsc_ref[0] = dsc_ref[0] + dsc

        @pl.when(i == 0)
        def _():
            dgn_ref[...] = dgn

        @pl.when(i > 0)
        def _():
            dgn_ref[...] = dgn_ref[...] + dgn
            gx_ref[...] = dx1_ref[...] + dx

    zrow = pl.BlockSpec((ROW_TILE, D_MODEL), lambda i: (i, 0))
    lat = pl.BlockSpec((ROW_TILE, D_MODEL), lambda i: (jnp.maximum(i - 1, 0), 0))
    mod = pl.BlockSpec((1, 1, D_MODEL), lambda i: (jnp.minimum(i, 1), 0, 0))
    mshape = jax.ShapeDtypeStruct((2, 1, D_MODEL), F32)
    res, extra = _call(
        body, name="in_proj_bwd", grid=(ZLEN // ROW_TILE,),
        in_specs=[zrow] * 7 + [_full((D_MODEL, IN_COLS)), zrow, lat, _full((1, D_MODEL)), mod],
        out_specs=[lat, mod, mod, _full((1, D_MODEL))],
        out_shape=[jax.ShapeDtypeStruct((SEQ, D_MODEL), F32), mshape, mshape, jax.ShapeDtypeStruct((1, D_MODEL), F32)],
        sem=("arbitrary",), args=(*dgs, w_in, z, dx1, gain, scale), comm=comm)
    return (*res, extra)


def matmul_tn(a, b, name, tm, tn, prev=None, col_block=0, total_cols=None):
    k, m = a.shape
    n = b.shape[1]
    total_cols = n if total_cols is None else total_cols
    assert m % tm == 0 and n % tn == 0
    off = col_block * (n // tn)

    def body(a_ref, b_ref, *rest):
        rest[-1][...] = _dot_tn(a_ref[...].astype(BF16), b_ref[...]).astype(BF16)

    in_specs = [pl.BlockSpec((k, tm), lambda i, j: (0, i)), pl.BlockSpec((k, tn), lambda i, j: (0, j))]
    args = [a, b]
    aliases = {}
    if prev is not None:
        in_specs.append(pl.BlockSpec(memory_space=pl.ANY))
        args.append(prev)
        aliases = {2: 0}
    return pl.pallas_call(
        body, name=name, grid=(m // tm, n // tn), in_specs=in_specs,
        out_specs=pl.BlockSpec((tm, tn), lambda i, j: (i, j + off)),
        out_shape=jax.ShapeDtypeStruct((m, total_cols), BF16),
        input_output_aliases=aliases,
        compiler_params=_params("parallel", "parallel"),
    )(*args)


def local_step(z, target, modx, modc, norm_mix_g, norm_ffn_g, w_in, conv_w, conv_b, wa, ba, wx, bx, lam, qg, kg, rpb,
               w_rnn, w_na, w_out, w_up, fconv_w, fconv_b, w_down, idx=None, bt=None):
    dist = idx is not None
    c_idx = idx[1:2] if dist else None
    d = D_MODEL
    mx = [modx[:, k * d:(k + 1) * d] for k in range(N_MOD)]
    shift = jnp.stack([modc[:, 0:d], mx[0]])
    scale = jnp.stack([modc[:, d:2 * d], mx[1]])
    cos, sin = _rope_tables()
    ones = _head_ones()
    qg2 = jnp.tile(qg, (1, 2))
    kg2 = jnp.tile(kg, (1, 2))

    if bt is None:
        bt, _ = bias_table(rpb)
    xn, p, got = norm_matmul(z, norm_mix_g, shift, scale, w_in, "in_proj", 3 * ROW_TILE, 1792, ctx_rows=CTX_LEN,
                             comm=gather_weights_comm([w_rnn, w_na, w_out], [1, 2, 3]) if dist else None)
    if dist:
        w_rnn, w_na, w_out = got
    y_rnn, got = lru_fwd(p, conv_w, conv_b, wa, ba, wx, bx, lam,
                         comm=gather_weights_comm([w_down], [5]) if dist else None)
    if dist:
        w_down = got[0]
    q_rot, q_pl, kk, vv, _ = qkv_prep(p, qg2, kg2, cos, sin, ones)
    y_na, lse, got = attn_fwd(q_rot, q_pl, kk, vv, bt, comm=gather_weights_comm([w_up], [4]) if dist else None)
    if dist:
        w_up = got[0]
    u, v, merged, out, x1 = merge_fwd(y_rnn, y_na, p, z, mx[2], w_rnn, w_na, w_out)
    xn2, hpre, _ = norm_matmul(x1, norm_ffn_g, mx[3][None], mx[4][None], w_up, "ffn_up", 2 * ROW_TILE, 1408)
    act = ffn_act(hpre, fconv_w, fconv_b)
    f, dy, df, loss_sq, dg5 = ffn_down_loss(act, w_down, x1, mx[5], target)

    partials, pieces = {}, {}

    def views_of(which, grads):
        return [_grad_view(g, BIG[w][1], BIG[w][2]) for w, g in zip(which, grads)]

    def chip_partials(which, views, recv):
        for w, gv, r in zip(which, views, recv):
            partials[w] = add_halves(gv, r, c_idx, "add_halves_" + BIG[w][0])
        return scatter_pieces_comm([partials[w] for w in which], which)

    dha, dhg, d_fcw_a, d_fcw_g, d_fcb_a, d_fcb_g = ffn_act_bwd(hpre, df, w_down, fconv_w, fconv_b)
    d_fcw = jnp.concatenate([d_fcw_a, d_fcw_g], axis=1)
    d_fcb = jnp.concatenate([d_fcb_a, d_fcb_g], axis=1)
    dx1, d_s3, d_s4, d_gffn = ffn_up_bwd(dha, dhg, w_up, x1, dy, norm_ffn_g, mx[4])
    g_w_down = matmul_tn(act, df, "gw_down", 256, D_MODEL)
    g_w_up = matmul_tn(xn2, dha, "gw_up_a", 512, 1408, total_cols=2 * D_FF)
    g_w_up = matmul_tn(xn2, dhg, "gw_up_g", 512, 1408, prev=g_w_up, col_block=1, total_cols=2 * D_FF)
    v_ffn = views_of([4, 5], [g_w_up, g_w_down]) if dist else None
    *mb, got = merge_bwd(dx1, out, mx[2], p, u, v, w_rnn, w_na, w_out,
                         comm=exchange_halves_comm(v_ffn) if dist else None)
    dout, du, dv, dmr, dmn, dyr, dyn, dg2 = mb
    recv_ffn = got
    g_w_out = matmul_tn(merged, dout, "gw_out", 1024, 512)
    g_w_rnn = matmul_tn(y_rnn, du, "gw_rnn", 1024, 512)
    g_w_na = matmul_tn(y_na, dv, "gw_na", 1024, 512)
    v_mix = views_of([1, 2, 3], [g_w_rnn, g_w_na, g_w_out]) if dist else None
    *lru_grads, got = lru_bwd(p, dyr, conv_w, conv_b, wa, ba, wx, bx, lam,
                              comm=join_comms(chip_partials([4, 5], v_ffn, recv_ffn),
                                              exchange_halves_comm(v_mix)) if dist else None)
    dxr, dgx, d_cw, d_cb, d_wa, d_ba, d_wx, d_bx, d_lam = lru_grads
    if dist:
        pieces[4], pieces[5] = got[:2]
    lru_w_all = {}
    dqr, dqp, dk, dvh, dbt, got = attn_bwd(
        q_rot, q_pl, kk, vv, bt, y_na, dyn, lse,
        comm=join_comms(chip_partials([1, 2, 3], v_mix, got[2:]),
                        join_comms(all_gather_comm(d_wa.reshape(-1, LRU_BLOCK_W)),
                                   all_gather_comm(d_wx.reshape(-1, LRU_BLOCK_W)))) if dist else None)
    if dist:
        pieces[1], pieces[2], pieces[3], lru_w_all["lru_wa"], lru_w_all["lru_wx"] = got
    dq_cols, dk_cols, dv_cols, d_qg, d_kg, _ = qkv_bwd(dqr, dqp, dk, dvh, p, qg2, kg2, cos, sin, ones)
    d_rpb = rpb_grad(dbt)
    dgs = [dxr, dk_cols, dv_cols, dgx, dq_cols, dmr, dmn]
    g_w_in = None
    for g in range(7):
        g_w_in = matmul_tn(xn, dgs[g], "gw_in_%d" % g, 1024, 512, prev=g_w_in, col_block=g, total_cols=IN_COLS)
    if dist:
        v_in = views_of([0], [g_w_in])
        recv_in = run_comm(exchange_halves_comm(v_in), "grad_exchange_w_in")
    grad_x, dsh, dsc, d_gmix, got = in_proj_bwd(dgs, w_in, z, dx1, norm_mix_g, scale,
                                                comm=chip_partials([0], v_in, recv_in) if dist else None)
    if dist:
        pieces[0] = got[0]

    d_modx = jnp.concatenate([dsh[1], dsc[1], dg2, d_s3, d_s4, dg5], axis=1)
    d_modc = jnp.concatenate([dsh[0], dsc[0]], axis=1)
    return dict(loss_sq=loss_sq, grad_x=grad_x, d_modx=d_modx, d_modc=d_modc, norm_mix_g=d_gmix, norm_ffn_g=d_gffn,
                w_in=g_w_in, lru_conv_w=d_cw, lru_conv_b=d_cb, lru_wa=d_wa, lru_ba=d_ba, lru_wx=d_wx, lru_bx=d_bx,
                lru_lambda=d_lam, q_norm_g=d_qg, k_norm_g=d_kg, na_rpb=d_rpb, w_rnn_out=g_w_rnn, w_na_out=g_w_na,
                w_out=g_w_out, w_up=g_w_up, ffn_conv_w=d_fcw, ffn_conv_b=d_fcb, w_down=g_w_down,
                partials=partials, pieces=pieces, lru_w_all=lru_w_all)


def _mesh_pos():
    return lax.axis_index("x"), lax.axis_index("y"), lax.axis_index("c")


def _other_chips(x, y):
    return [(1 - x, y), (x, 1 - y), (1 - x, 1 - y)]


BIG = (("w_in", (D_MODEL, IN_COLS), 1), ("w_rnn_out", (D_MODEL, D_MODEL), 0), ("w_na_out", (D_MODEL, D_MODEL), 0),
       ("w_out", (D_MODEL, D_MODEL), 0), ("w_up", (D_MODEL, 2 * D_FF), 1), ("w_down", (D_FF, D_MODEL), 0))


def _shard_shape(full, axis):
    r, c = full
    return (r // N_SHARD, c) if axis == 0 else (r, c // N_SHARD)


def _slot(ref, full, axis, s, h):
    r, c = full
    if axis == 0:
        rs = r // N_SHARD
        return ref.at[pl.ds(s * rs + h * (rs // 2), rs // 2), :]
    cs = c // N_SHARD
    return ref.at[pl.ds(h * (r // 2), r // 2), pl.ds(s * cs, cs)]


def cast_into_full(x, full, axis, idx, name):
    r, c = x.shape
    tr = next(t for t in (512, 352, 256, 128) if r % t == 0)
    nb = r // tr

    def body(idx_ref, x_ref, o_ref):
        o_ref[...] = x_ref[...].astype(BF16)

    if axis == 0:
        out_spec = pl.BlockSpec((tr, c), lambda i, idx_ref: (idx_ref[0] * nb + i, 0))
    else:
        out_spec = pl.BlockSpec((tr, c), lambda i, idx_ref: (i, idx_ref[0]))
    return pl.pallas_call(
        body, name=name,
        grid_spec=pltpu.PrefetchScalarGridSpec(
            num_scalar_prefetch=1, grid=(nb,), in_specs=[pl.BlockSpec((tr, c), lambda i, idx_ref: (i, 0))],
            out_specs=out_spec),
        out_shape=jax.ShapeDtypeStruct(full, BF16),
        compiler_params=_params("parallel"),
    )(idx, x)


def run_comm(comm, name):
    k_in, k_out = len(comm.inputs), len(comm.out_shapes)

    def body(*refs):
        start, mid, end = comm.emit(refs[:k_in], refs[k_in:k_in + k_out], refs[k_in + k_out:])
        start()
        mid()
        end()

    hbm = pl.BlockSpec(memory_space=pl.ANY)
    return pl.pallas_call(
        body, name=name, in_specs=[hbm] * k_in, out_specs=[hbm] * k_out, out_shape=list(comm.out_shapes),
        input_output_aliases=dict(comm.aliases), scratch_shapes=list(comm.scratch),
        compiler_params=pltpu.CompilerParams(vmem_limit_bytes=VMEM_LIMIT_V7X),
    )(*comm.inputs)


def gather_weights_comm(fulls, which):
    nw = len(which)
    specs = [BIG[w] for w in which]

    def emit(_, outs, sems):
        send1, recv1, send2, recv2 = sems
        x, y, c = _mesh_pos()
        sibling = (x, y, 1 - c)
        chips = _other_chips(x, y)
        s_me = 2 * x + y
        shards = [2 * chip[0] + chip[1] for chip in chips]

        def ici(w, j, shard):
            _, full, axis = specs[w]
            dst = _slot(outs[w], full, axis, shard, c)
            return pltpu.make_async_remote_copy(
                src_ref=dst, dst_ref=dst, send_sem=send1.at[3 * w + j],
                recv_sem=recv1.at[3 * w + j], device_id=(*chips[j], c), device_id_type=MESH_T)

        def d2d(w, j, shard, half):
            _, full, axis = specs[w]
            dst = _slot(outs[w], full, axis, shard, half)
            return pltpu.make_async_remote_copy(
                src_ref=dst, dst_ref=dst, send_sem=send2.at[3 * w + j], recv_sem=recv2.at[3 * w + j],
                device_id=sibling, device_id_type=MESH_T)

        pairs = [(w, j) for w in range(nw) for j in range(3)]

        def start():
            for w, j in pairs:
                ici(w, j, s_me).start()

        def mid():
            for w, j in pairs:
                ici(w, j, shards[j]).wait_recv()
                d2d(w, j, shards[j], c).start()

        def end():
            for w, j in pairs:
                d2d(w, j, shards[j], 1 - c).wait_recv()
            for w, j in pairs:
                ici(w, j, s_me).wait_send()
                d2d(w, j, shards[j], c).wait_send()

        return start, mid, end

    return Comm(list(fulls), [jax.ShapeDtypeStruct(full, BF16) for _, full, _ in specs], {i: i for i in range(nw)},
                [pltpu.SemaphoreType.DMA((3 * nw,))] * 4, emit)


def join_comms(a, b):
    ai, ao, asc = len(a.inputs), len(a.out_shapes), len(a.scratch)

    def emit(ins, outs, sems):
        fa = a.emit(ins[:ai], outs[:ao], sems[:asc])
        fb = b.emit(ins[ai:], outs[ao:], sems[asc:])

        def both(k):
            def run():
                fa[k]()
                fb[k]()
            return run

        return both(0), both(1), both(2)

    aliases = dict(a.aliases)
    aliases.update({ai + i: ao + o for i, o in b.aliases.items()})
    return Comm(a.inputs + b.inputs, a.out_shapes + b.out_shapes, aliases, a.scratch + b.scratch, emit)


def all_gather_comm(x):
    def emit(srcs, outs, sems):
        send_sems, recv_sems, local_sem = sems
        x_ref, out_ref = srcs[0], outs[0]
        x, y, c = _mesh_pos()
        me, sibling = (x, y, c), (x, y, 1 - c)
        chips = _other_chips(x, y)

        def blk(px, py, pc):
            return out_ref.at[4 * px + 2 * py + pc]

        def copy(k, block, to, src=None):
            return pltpu.make_async_remote_copy(
                src_ref=blk(*block) if src is None else src, dst_ref=blk(*block),
                send_sem=send_sems.at[k], recv_sem=recv_sems.at[k], device_id=to, device_id_type=MESH_T)

        def mine():
            return pltpu.make_async_copy(x_ref, blk(*me), local_sem)

        def start():
            mine().start()
            copy(0, me, sibling, src=x_ref).start()
            for j, chip in enumerate(chips):
                copy(1 + j, me, (*chip, c), src=x_ref).start()

        def mid():
            for j, chip in enumerate(chips):
                copy(1 + j, (*chip, c), me).wait_recv()
                copy(4 + j, (*chip, c), sibling).start()

        def end():
            copy(0, sibling, me).wait_recv()
            for j, chip in enumerate(chips):
                copy(4 + j, (*chip, 1 - c), me).wait_recv()
            copy(0, me, sibling, src=x_ref).wait_send()
            for j, chip in enumerate(chips):
                copy(1 + j, me, (*chip, c), src=x_ref).wait_send()
                copy(4 + j, (*chip, c), sibling).wait_send()
            mine().wait()

        return start, mid, end

    return Comm([x], [jax.ShapeDtypeStruct((N_DEV,) + x.shape, F32)], {},
                [pltpu.SemaphoreType.DMA((7,)), pltpu.SemaphoreType.DMA((7,)), pltpu.SemaphoreType.DMA], emit)


def sum_blocks(g, name):
    _, r, c = g.shape
    tr = 256 if r % 256 == 0 else r

    def body(g_ref, o_ref):
        acc = g_ref[0]
        for k in range(1, N_DEV):
            acc = acc + g_ref[k]
        o_ref[...] = acc

    return pl.pallas_call(
        body, name=name, grid=(r // tr,),
        in_specs=[pl.BlockSpec((N_DEV, tr, c), lambda i: (0, i, 0))],
        out_specs=pl.BlockSpec((tr, c), lambda i: (i, 0)),
        out_shape=jax.ShapeDtypeStruct((r, c), F32),
        compiler_params=_params("parallel"),
    )(g)


def _grad_view(g, full, axis):
    r, c = full
    if axis == 0:
        return g.reshape(N_SHARD, 2, r // N_SHARD // 2, c)
    return g.reshape(1, 2, r // 2, c)


def exchange_halves_comm(gviews):
    nw = len(gviews)

    def emit(srcs, outs, sems):
        send_sems, recv_sems = sems
        x, y, c = _mesh_pos()

        def copies():
            return [pltpu.make_async_remote_copy(
                src_ref=srcs[w].at[:, pl.ds(1 - c, 1)], dst_ref=outs[w], send_sem=send_sems.at[w],
                recv_sem=recv_sems.at[w], device_id=(x, y, 1 - c), device_id_type=MESH_T) for w in range(nw)]

        def start():
            for cp in copies():
                cp.start()

        def end():
            for cp in copies():
                cp.wait()

        return start, lambda: None, end

    return Comm(list(gviews), [jax.ShapeDtypeStruct((g.shape[0], 1) + g.shape[2:], BF16) for g in gviews], {},
                [pltpu.SemaphoreType.DMA((nw,)), pltpu.SemaphoreType.DMA((nw,))], emit)


def _row_tile(rh):
    return 128 if rh % 128 == 0 else rh


def add_halves(gview, recv, c_idx, name):
    a, _, rh, cc = gview.shape
    tr = _row_tile(rh)

    def body(c_ref, g_ref, r_ref, o_ref):
        o_ref[0] = (g_ref[0, 0].astype(F32) + r_ref[0, 0].astype(F32)).astype(BF16)

    return pl.pallas_call(
        body, name=name,
        grid_spec=pltpu.PrefetchScalarGridSpec(
            num_scalar_prefetch=1, grid=(a, rh // tr),
            in_specs=[pl.BlockSpec((1, 1, tr, cc), lambda s, i, c_ref: (s, c_ref[0], i, 0)),
                      pl.BlockSpec((1, 1, tr, cc), lambda s, i, c_ref: (s, 0, i, 0))],
            out_specs=pl.BlockSpec((1, tr, cc), lambda s, i, c_ref: (s, i, 0))),
        out_shape=jax.ShapeDtypeStruct((a, rh, cc), BF16),
        compiler_params=_params("parallel", "parallel"),
    )(c_idx, gview, recv)


def _piece_shape(full, axis):
    rs, cs = _shard_shape(full, axis)
    return (rs // 2, cs)


def scatter_pieces_comm(partials, which):
    nw = len(which)
    specs = [BIG[w] for w in which]

    def emit(srcs, outs, sems):
        send_sems, recv_sems = sems
        x, y, c = _mesh_pos()
        chips = _other_chips(x, y)

        def copies():
            cps = []
            for w, (_, full, axis) in enumerate(specs):
                cs = full[1] // N_SHARD
                for j, chip in enumerate(chips):
                    s_j = 2 * chip[0] + chip[1]
                    src = srcs[w].at[s_j] if axis == 0 else srcs[w].at[0, :, pl.ds(s_j * cs, cs)]
                    cps.append(pltpu.make_async_remote_copy(
                        src_ref=src, dst_ref=outs[w].at[j], send_sem=send_sems.at[3 * w + j],
                        recv_sem=recv_sems.at[3 * w + j], device_id=(*chip, c), device_id_type=MESH_T))
            return cps

        def start():
            for cp in copies():
                cp.start()

        def mid():
            pass

        def end():
            for cp in copies():
                cp.wait()

        return start, mid, end

    return Comm(list(partials), [jax.ShapeDtypeStruct((3,) + _piece_shape(full, axis), BF16) for _, full, axis in specs],
                {}, [pltpu.SemaphoreType.DMA((3 * nw,)), pltpu.SemaphoreType.DMA((3 * nw,))], emit)


def add_pieces(partial, recv, idx, axis, name):
    _, rh, cs = recv.shape
    tr = _row_tile(rh)

    def body(idx_ref, p_ref, r_ref, o_ref):
        o_ref[0] = ((p_ref[0].astype(F32) + r_ref[0].astype(F32)) + r_ref[1].astype(F32)) + r_ref[2].astype(F32)

    if axis == 0:
        pspec = pl.BlockSpec((1, tr, cs), lambda i, idx_ref: (idx_ref[0], i, 0))
    else:
        pspec = pl.BlockSpec((1, tr, cs), lambda i, idx_ref: (0, i, idx_ref[0]))
    return pl.pallas_call(
        body, name=name,
        grid_spec=pltpu.PrefetchScalarGridSpec(
            num_scalar_prefetch=1, grid=(rh // tr,),
            in_specs=[pspec, pl.BlockSpec((3, tr, cs), lambda i, idx_ref: (0, i, 0))],
            out_specs=pl.BlockSpec((1, tr, cs), lambda i, idx_ref: (idx_ref[1], i, 0))),
        out_shape=jax.ShapeDtypeStruct((2, rh, cs), F32),
        compiler_params=_params("parallel"),
    )(idx, partial, recv)


def join_halves_comm(halves):
    nw = len(halves)

    def emit(_, outs, sems):
        send_sems, recv_sems = sems
        x, y, c = _mesh_pos()

        def copy(w, half):
            return pltpu.make_async_remote_copy(
                src_ref=outs[w].at[half], dst_ref=outs[w].at[half], send_sem=send_sems.at[w], recv_sem=recv_sems.at[w],
                device_id=(x, y, 1 - c), device_id_type=MESH_T)

        def start():
            for w in range(nw):
                copy(w, c).start()

        def end():
            for w in range(nw):
                copy(w, c).wait_send()
                copy(w, 1 - c).wait_recv()

        return start, lambda: None, end

    return Comm(list(halves), [jax.ShapeDtypeStruct(h.shape, F32) for h in halves], {i: i for i in range(nw)},
                [pltpu.SemaphoreType.DMA((nw,))] * 2, emit)


MOD_COLS = N_MOD * D_MODEL // N_SHARD
MOD_TILE = 512


def mod_fwd(c16, w_mod):
    def body(c_ref, w_ref, s_ref, o_ref):
        cv = c_ref[...]
        s = cv * _sigmoid(cv)
        s_ref[...] = s
        o_ref[...] = jnp.dot(s.astype(BF16), w_ref[...].astype(BF16), preferred_element_type=F32)

    return pl.pallas_call(
        body, name="mod_fwd", grid=(MOD_COLS // MOD_TILE,),
        in_specs=[_full((16, D_MODEL)), pl.BlockSpec((D_MODEL, MOD_TILE), lambda j: (0, j))],
        out_specs=[_full((16, D_MODEL)), pl.BlockSpec((16, MOD_TILE), lambda j: (0, j))],
        out_shape=[jax.ShapeDtypeStruct((16, D_MODEL), F32), jax.ShapeDtypeStruct((16, MOD_COLS), F32)],
        compiler_params=_params("arbitrary"),
    )(c16, w_mod)


def mod_bwd(s16, dm16, w_mod):
    hi = lax.Precision.HIGHEST

    def body(s_ref, d_ref, w_ref, gw_ref, ds_ref):
        j = pl.program_id(0)
        dm = d_ref[...]
        gw_ref[...] = lax.dot_general(s_ref[...], dm, (((0,), (0,)), ((), ())), preferred_element_type=F32, precision=hi)
        part = lax.dot_general(dm, w_ref[...], (((1,), (1,)), ((), ())), preferred_element_type=F32, precision=hi)

        @pl.when(j == 0)
        def _():
            ds_ref[...] = part

        @pl.when(j > 0)
        def _():
            ds_ref[...] = ds_ref[...] + part

    return pl.pallas_call(
        body, name="mod_bwd", grid=(MOD_COLS // MOD_TILE,),
        in_specs=[_full((16, D_MODEL)), pl.BlockSpec((16, MOD_TILE), lambda j: (0, j)),
                  pl.BlockSpec((D_MODEL, MOD_TILE), lambda j: (0, j))],
        out_specs=[pl.BlockSpec((D_MODEL, MOD_TILE), lambda j: (0, j)), _full((16, D_MODEL))],
        out_shape=[jax.ShapeDtypeStruct((D_MODEL, MOD_COLS), F32), jax.ShapeDtypeStruct((16, D_MODEL), F32)],
        compiler_params=_params("arbitrary"),
    )(s16, dm16, w_mod)


def cctx_grad(parts, c_ctx):
    def body(p_ref, c_ref, o_ref):
        ds = p_ref[0:1, :]
        for s in range(1, N_SHARD):
            ds = ds + p_ref[16 * s:16 * s + 1, :]
        cv = c_ref[...]
        sg = _sigmoid(cv)
        o_ref[...] = ds * (sg * (1.0 + cv * (1.0 - sg)))

    return pl.pallas_call(
        body, name="cctx_grad", in_specs=[_full((N_DEV * 8, D_MODEL)), _full((1, D_MODEL))],
        out_specs=_full((1, D_MODEL)), out_shape=jax.ShapeDtypeStruct((1, D_MODEL), F32),
    )(parts, c_ctx)


def add_rows(a, b, name):
    def body(a_ref, b_ref, o_ref):
        o_ref[...] = a_ref[...] + b_ref[...]

    return pl.pallas_call(body, name=name, in_specs=[_full(a.shape), _full(b.shape)], out_specs=_full(a.shape),
                          out_shape=jax.ShapeDtypeStruct(a.shape, F32))(a, b)


def _adamw_update(w_ref, g_ref, m_ref, v_ref, d_ref, nm_ref, nv_ref):
    g_ = g_ref[...]
    m_ = ADAM_B1 * m_ref[...] + (1.0 - ADAM_B1) * g_
    v_ = ADAM_B2 * v_ref[...] + (1.0 - ADAM_B2) * (g_ * g_)
    m_hat = m_ / (1.0 - ADAM_B1 ** ADAM_STEP)
    v_hat = v_ / (1.0 - ADAM_B2 ** ADAM_STEP)
    d_ref[...] = -ADAM_LR * (m_hat / (jnp.sqrt(v_hat) + ADAM_EPS) + ADAM_WD * w_ref[...])
    nm_ref[...] = m_
    nv_ref[...] = v_


def adamw_many(ws, gs, ms, vs):
    n = len(ws)

    def body(*refs):
        for i in range(n):
            _adamw_update(*[refs[k * n + i] for k in range(7)])

    shapes = [jax.ShapeDtypeStruct(w.shape, F32) for w in ws]
    return pl.pallas_call(body, name="adamw_small", out_shape=shapes * 3,
                          compiler_params=pltpu.CompilerParams(vmem_limit_bytes=VMEM_LIMIT_V7X))(*ws, *gs, *ms, *vs)


def adamw(w, g, m, v, name, comm=None):
    r, c = w.shape
    tr = 128 if (r % 128 == 0 and r > 128) else r

    def body(w_ref, g_ref, m_ref, v_ref, d_ref, nm_ref, nv_ref):
        _adamw_update(w_ref, g_ref, m_ref, v_ref, d_ref, nm_ref, nv_ref)

    spec = pl.BlockSpec((tr, c), lambda i: (i, 0))
    shp = jax.ShapeDtypeStruct((r, c), F32)
    res, extra = _call(body, name=name, grid=(r // tr,), in_specs=[spec] * 4, out_specs=[spec] * 3,
                       out_shape=[shp] * 3, sem=("parallel",), args=(w, g, m, v), comm=comm)
    return (*res, extra)


LANES = 1024


def _pack(arrs):
    rows, spans, at = [], [], 0
    for a in arrs:
        n = int(np.prod(a.shape))
        nr = 8 * -(-n // (8 * LANES))
        flat = a.reshape(-1)
        if nr * LANES != n:
            flat = jnp.concatenate([flat, jnp.zeros((nr * LANES - n,), F32)])
        rows.append(flat.reshape(nr, LANES))
        spans.append((at, nr, n, a.shape))
        at += nr
    return jnp.concatenate(rows, axis=0), spans


def _unpack(buf, spans):
    out = []
    for at, nr, n, shape in spans:
        out.append(buf[at:at + nr].reshape(-1)[:n].reshape(shape))
    return out


SMALL_SHARD = ("lru_conv_w", "lru_ba", "lru_bx", "lru_lambda", "ffn_conv_w")


def kernel(x, c, ctx, c_ctx, w_mod, b_mod, norm_mix_g, norm_ffn_g, w_in, lru_conv_w, lru_conv_b, lru_wa, lru_ba, lru_wx, lru_bx, lru_lambda, q_norm_g, k_norm_g, na_rpb, w_rnn_out, w_na_out, w_out, w_up, ffn_conv_w, ffn_conv_b, w_down, loss_target, m_c_ctx, m_w_mod, m_b_mod, m_norm_mix_g, m_norm_ffn_g, m_w_in, m_lru_conv_w, m_lru_conv_b, m_lru_wa, m_lru_ba, m_lru_wx, m_lru_bx, m_lru_lambda, m_q_norm_g, m_k_norm_g, m_na_rpb, m_w_rnn_out, m_w_na_out, m_w_out, m_w_up, m_ffn_conv_w, m_ffn_conv_b, m_w_down, v_c_ctx, v_w_mod, v_b_mod, v_norm_mix_g, v_norm_ffn_g, v_w_in, v_lru_conv_w, v_lru_conv_b, v_lru_wa, v_lru_ba, v_lru_wx, v_lru_bx, v_lru_lambda, v_q_norm_g, v_k_norm_g, v_na_rpb, v_w_rnn_out, v_w_na_out, v_w_out, v_w_up, v_ffn_conv_w, v_ffn_conv_b, v_w_down):
    weights = dict(c_ctx=c_ctx, w_mod=w_mod, b_mod=b_mod, norm_mix_g=norm_mix_g, norm_ffn_g=norm_ffn_g, w_in=w_in,
                   lru_conv_w=lru_conv_w, lru_conv_b=lru_conv_b, lru_wa=lru_wa, lru_ba=lru_ba, lru_wx=lru_wx,
                   lru_bx=lru_bx, lru_lambda=lru_lambda, q_norm_g=q_norm_g, k_norm_g=k_norm_g, na_rpb=na_rpb,
                   w_rnn_out=w_rnn_out, w_na_out=w_na_out, w_out=w_out, w_up=w_up, ffn_conv_w=ffn_conv_w,
                   ffn_conv_b=ffn_conv_b, w_down=w_down)
    mom1 = dict(c_ctx=m_c_ctx, w_mod=m_w_mod, b_mod=m_b_mod, norm_mix_g=m_norm_mix_g, norm_ffn_g=m_norm_ffn_g,
                w_in=m_w_in, lru_conv_w=m_lru_conv_w, lru_conv_b=m_lru_conv_b, lru_wa=m_lru_wa, lru_ba=m_lru_ba,
                lru_wx=m_lru_wx, lru_bx=m_lru_bx, lru_lambda=m_lru_lambda, q_norm_g=m_q_norm_g, k_norm_g=m_k_norm_g,
                na_rpb=m_na_rpb, w_rnn_out=m_w_rnn_out, w_na_out=m_w_na_out, w_out=m_w_out, w_up=m_w_up,
                ffn_conv_w=m_ffn_conv_w, ffn_conv_b=m_ffn_conv_b, w_down=m_w_down)
    mom2 = dict(c_ctx=v_c_ctx, w_mod=v_w_mod, b_mod=v_b_mod, norm_mix_g=v_norm_mix_g, norm_ffn_g=v_norm_ffn_g,
                w_in=v_w_in, lru_conv_w=v_lru_conv_w, lru_conv_b=v_lru_conv_b, lru_wa=v_lru_wa, lru_ba=v_lru_ba,
                lru_wx=v_lru_wx, lru_bx=v_lru_bx, lru_lambda=v_lru_lambda, q_norm_g=v_q_norm_g, k_norm_g=v_k_norm_g,
                na_rpb=v_na_rpb, w_rnn_out=v_w_rnn_out, w_na_out=v_w_na_out, w_out=v_w_out, w_up=v_w_up,
                ffn_conv_w=v_ffn_conv_w, ffn_conv_b=v_ffn_conv_b, w_down=v_w_down)
    order = list(weights)
    d = D_MODEL
    mx_, my_, mc_ = _mesh_pos()
    shard = 2 * mx_ + my_
    dev = 2 * shard + mc_

    idx = jnp.stack([shard, mc_]).astype(jnp.int32)
    wsh = {name: cast_into_full(weights[name][0], full, axis, idx, "cast_" + name) for name, full, axis in BIG}
    local_small, small_spans = _pack([c] + [weights[k][0] for k in SMALL_SHARD])
    bt, (w_in_full, gath) = bias_table(na_rpb[0], comm=join_comms(gather_weights_comm([wsh["w_in"]], [0]),
                                                                  all_gather_comm(local_small)))
    per_dev = [_unpack(gath[k], small_spans) for k in range(N_DEV)]
    c_all = jnp.concatenate([per_dev[k][0] for k in range(N_DEV)], axis=0)
    full_small = {name: jnp.concatenate([per_dev[2 * s][1 + i] for s in range(N_SHARD)], axis=-1)
                  for i, name in enumerate(SMALL_SHARD)}
    c16 = jnp.concatenate([c_all, c_ctx.reshape(1, d), jnp.zeros((7, d), F32)], axis=0)
    s16, mod_part = mod_fwd(c16, w_mod[0])
    mod_all = run_comm(all_gather_comm(mod_part), "gather_mod")[0]
    mod = jnp.concatenate([mod_all[2 * s] for s in range(N_SHARD)], axis=1) + b_mod
    modx = lax.dynamic_slice(mod, (dev, 0), (1, N_MOD * d))
    modc = mod[8:9]

    z = jnp.concatenate([ctx[0], x[0]], axis=0)
    res = local_step(z, loss_target[0], modx, modc, norm_mix_g, norm_ffn_g, w_in_full, full_small["lru_conv_w"],
                     lru_conv_b, lru_wa[0], full_small["lru_ba"], lru_wx[0], full_small["lru_bx"],
                     full_small["lru_lambda"], q_norm_g, k_norm_g, na_rpb[0], wsh["w_rnn_out"], wsh["w_na_out"],
                     wsh["w_out"], wsh["w_up"], full_small["ffn_conv_w"], ffn_conv_b, wsh["w_down"], idx=idx, bt=bt)

    halves = [add_pieces(res["partials"][i], res["pieces"][i], idx, BIG[i][2], "add_pieces_" + BIG[i][0])
              for i in range(len(BIG))]
    lru_tot = {k: sum_blocks(res["lru_w_all"][k], "sum_" + k).reshape(weights[k].shape[1:])
               for k in ("lru_wa", "lru_wx")}
    small_names = ["norm_mix_g", "norm_ffn_g", "lru_conv_w", "lru_conv_b", "lru_ba", "lru_bx",
                   "lru_lambda", "q_norm_g", "k_norm_g", "na_rpb", "ffn_conv_w", "ffn_conv_b"]
    local_g, g_spans = _pack([res["loss_sq"][0:1, 0:1], res["d_modx"], res["d_modc"]] + [res[k] for k in small_names])
    n_rows = local_g.shape[0]
    *joined, g_all = run_comm(join_comms(join_halves_comm(halves), all_gather_comm(local_g)), "tail_exchange")
    grads = {name: joined[i].reshape(_shard_shape(full, axis)) for i, (name, full, axis) in enumerate(BIG)}
    grads.update(lru_tot)
    g_tot = sum_blocks(g_all, "sum_small")
    tot = _unpack(g_tot, g_spans)
    loss = (0.5 / d) * tot[0][0, 0]
    small_tot = dict(zip(small_names, tot[3:]))
    at_x = g_spans[1][0]
    dmx_rows = g_all.reshape(N_DEV, n_rows, LANES)[:, at_x:at_x + N_MOD, :].reshape(N_DEV, N_MOD * d)
    dmc_row = jnp.concatenate([tot[2], jnp.zeros((1, 4 * d), F32)], axis=1)
    dm16 = jnp.concatenate([dmx_rows, dmc_row, jnp.zeros((7, N_MOD * d), F32)], axis=0)
    grads["b_mod"] = add_rows(tot[1], dmc_row, "b_mod_grad")
    g_w_mod, ds16 = mod_bwd(s16, lax.dynamic_slice(dm16, (0, shard * MOD_COLS), (16, MOD_COLS)), w_mod[0])
    grads["w_mod"] = g_w_mod
    for k in small_names:
        g = small_tot[k]
        if k in SMALL_SHARD:
            w_sh = weights[k].shape[-1]
            g = lax.dynamic_slice_in_dim(g, shard * w_sh, w_sh, axis=g.ndim - 1)
        grads[k] = g

    delta, new_m, new_v = {}, {}, {}
    for name, _, _ in BIG + (("w_mod", None, None),):
        *upd, got = adamw(weights[name][0], grads[name], mom1[name][0], mom2[name][0], "adamw_" + name,
                          comm=all_gather_comm(ds16[8:16]) if name == "w_in" else None)
        delta[name], new_m[name], new_v[name] = upd
        if name == "w_in":
            grads["c_ctx"] = cctx_grad(got[0].reshape(N_DEV * 8, d), c_ctx.reshape(1, d))
    rest = [k for k in order if k not in delta]
    views = {k: (grads[k].shape if grads[k].ndim <= 3 else (-1, grads[k].shape[-1])) for k in rest}
    small = adamw_many(*[[t[k].reshape(views[k]) for k in rest] for t in (weights, grads, mom1, mom2)])
    n_rest = len(rest)
    for i, k in enumerate(rest):
        delta[k], new_m[k], new_v[k] = small[i], small[n_rest + i], small[2 * n_rest + i]

    shaped = lambda t: [t[k].reshape(weights[k].shape) for k in order]
    return (loss, res["grad_x"][None], *shaped(grads), *shaped(delta), *shaped(new_m), *shaped(new_v))
```
